```python
import math
import jax, jax.numpy as jnp
from jax import lax
import numpy as np

D_MODEL = 2048
BATCH = 8
SEQ = 8192
DEPTH = 1

GMLP_CHUNK = 128
GMLP_GROUPS = 4
GMLP_GROUP_DIM = 128
GMLP_WIDTH = GMLP_GROUPS * GMLP_GROUP_DIM

DN_HEADS = 8
DN_HEAD_DIM = 128
DN_WIDTH = DN_HEADS * DN_HEAD_DIM
DN_CONV = 4
DN_CHUNK = 64

XA_HEADS = 4
XA_HEAD_DIM = 128
XA_WIDTH = XA_HEADS * XA_HEAD_DIM
N_MEM = 256

MIX_WIDTH = GMLP_WIDTH + DN_WIDTH + XA_WIDTH
EPS = 1e-6

SEG_WIDTHS = (GMLP_WIDTH, GMLP_WIDTH, GMLP_WIDTH,
              DN_WIDTH, DN_WIDTH, DN_WIDTH, DN_WIDTH, DN_HEADS, DN_HEADS,
              XA_WIDTH, XA_WIDTH)
IN_WIDTH = sum(SEG_WIDTHS)
SPLIT_POINTS = tuple(sum(SEG_WIDTHS[:i + 1]) for i in range(len(SEG_WIDTHS) - 1))

kernel_name = "hybrid_gmlp_gated_deltanet_memxattn_block"


def rmsnorm(x, g):
    xf = x.astype(jnp.float32)
    y = xf * lax.rsqrt(jnp.mean(xf * xf, axis=-1, keepdims=True) + EPS)
    return (y * g.astype(jnp.float32)).astype(x.dtype)


def layernorm(x, g, b):
    xf = x.astype(jnp.float32)
    mu = jnp.mean(xf, axis=-1, keepdims=True)
    xc = xf - mu
    y = xc * lax.rsqrt(jnp.mean(xc * xc, axis=-1, keepdims=True) + EPS)
    return (y * g.astype(jnp.float32) + b.astype(jnp.float32)).astype(x.dtype)


def l2norm(x):
    xf = x.astype(jnp.float32)
    return xf * lax.rsqrt(jnp.sum(xf * xf, axis=-1, keepdims=True) + EPS)


def gmlp_spatial_gating(u, v, ws, bs, ln_g, ln_b):
    B, S, _ = u.shape
    u = jax.nn.gelu(u)
    v = layernorm(jax.nn.gelu(v), ln_g, ln_b)
    nc = S // GMLP_CHUNK
    v = v.reshape(B, nc, GMLP_CHUNK, GMLP_GROUPS, GMLP_GROUP_DIM)
    causal = jnp.tril(jnp.ones((GMLP_CHUNK, GMLP_CHUNK), dtype=bool))
    w = jnp.where(causal[None], ws, 0)
    s = jnp.einsum('gts,bcsgd->bctgd', w, v) + bs.T[None, None, :, :, None]
    return u * s.reshape(B, S, GMLP_WIDTH)


def causal_conv_silu(x, w):
    K = w.shape[0]
    S = x.shape[1]
    xp = jnp.pad(x, ((0, 0), (K - 1, 0), (0, 0)))
    y = sum(w[k] * xp[:, k:k + S] for k in range(K))
    return jax.nn.silu(y)


def gated_delta_rule(q, k, v, g, beta):
    f32 = jnp.float32
    q, k, v, g, beta = (t.astype(f32) for t in (q, k, v, g, beta))
    B, H, S, Dk = q.shape
    Dv = v.shape[-1]
    C = DN_CHUNK
    N = S // C
    q = q.reshape(B, H, N, C, Dk)
    k = k.reshape(B, H, N, C, Dk)
    v = v.reshape(B, H, N, C, Dv)
    g = g.reshape(B, H, N, C)
    beta = beta.reshape(B, H, N, C)

    g_cum = jnp.cumsum(g, axis=-1)
    tri = jnp.tril(jnp.ones((C, C), dtype=bool))
    strict = jnp.tril(jnp.ones((C, C), dtype=bool), k=-1)
    diff = g_cum[..., :, None] - g_cum[..., None, :]
    decay = jnp.exp(jnp.where(tri, diff, -jnp.inf))

    kb = k * beta[..., None]
    vb = v * beta[..., None]
    a = jnp.where(strict, jnp.einsum('bhnid,bhnjd->bhnij', kb, k) * decay, 0.0)
    eye = jnp.eye(C, dtype=f32)
    T = lax.linalg.triangular_solve(a + eye, jnp.broadcast_to(eye, a.shape),
                                    left_side=True, lower=True, unit_diagonal=True)
    value = jnp.einsum('bhnij,bhnjd->bhnid', T, vb)
    k_cumdecay = jnp.einsum('bhnij,bhnjd->bhnid', T, kb * jnp.exp(g_cum)[..., None])
    attn_intra = jnp.where(tri, jnp.einsum('bhnid,bhnjd->bhnij', q, k) * decay, 0.0)
    q_g = q * jnp.exp(g_cum)[..., None]
    g_last = g_cum[..., -1]
    k_dec = k * jnp.exp(g_last[..., None] - g_cum)[..., None]

    def step(state, inp):
        qg_c, kcd_c, val_c, ai_c, kd_c, gl_c = inp
        v_new = val_c - jnp.einsum('bhid,bhde->bhie', kcd_c, state)
        o_c = jnp.einsum('bhid,bhde->bhie', qg_c, state) + jnp.einsum('bhij,bhje->bhie', ai_c, v_new)
        state = state * jnp.exp(gl_c)[..., None, None] + jnp.einsum('bhid,bhie->bhde', kd_c, v_new)
        return state, o_c

    xs = tuple(jnp.moveaxis(t, 2, 0) for t in (q_g, k_cumdecay, value, attn_intra, k_dec, g_last))
    s0 = jnp.zeros((B, H, Dk, Dv), f32)
    _, o = lax.scan(step, s0, xs)
    return jnp.moveaxis(o, 0, 2).reshape(B, H, S, Dv)


def _fwd_setup_inputs(seed: int = 0) -> dict:
    key = jax.random.key(seed)
    ks = jax.random.split(key, 16)
    f32 = jnp.float32
    nrm = lambda k_, shp: jax.random.normal(k_, shp, f32)
    x = nrm(ks[0], (BATCH, SEQ, D_MODEL))
    mem = nrm(ks[1], (BATCH, N_MEM, D_MODEL))
    ln_g = 1.0 + 0.1 * nrm(ks[2], (DEPTH, D_MODEL))
    w_in = nrm(ks[3], (DEPTH, D_MODEL, IN_WIDTH)) * D_MODEL ** -0.5
    gmlp_ln_g = 1.0 + 0.1 * nrm(ks[4], (DEPTH, GMLP_WIDTH))
    gmlp_ln_b = 0.1 * nrm(ks[5], (DEPTH, GMLP_WIDTH))
    gmlp_ws = nrm(ks[6], (DEPTH, GMLP_GROUPS, GMLP_CHUNK, GMLP_CHUNK)) * GMLP_CHUNK ** -0.5
    gmlp_bs = 1.0 + 0.1 * nrm(ks[7], (DEPTH, GMLP_GROUPS, GMLP_CHUNK))
    conv_w = nrm(ks[8], (DEPTH, DN_CONV, 3 * DN_WIDTH)) * DN_CONV ** -0.5
    dn_a_log = jnp.log(jax.random.uniform(ks[9], (DEPTH, DN_HEADS), f32, minval=1.0, maxval=16.0))
    dt = jnp.exp(jax.random.uniform(ks[10], (DEPTH, DN_HEADS), f32,
                                    minval=math.log(1e-3), maxval=math.log(1e-1)))
    dn_dt_bias = dt + jnp.log(-jnp.expm1(-dt))
    dn_norm_g = 1.0 + 0.1 * nrm(ks[11], (DEPTH, DN_HEAD_DIM))
    mem_norm_g = 1.0 + 0.1 * nrm(ks[12], (DEPTH, D_MODEL))
    w_mem_kv = nrm(ks[13], (DEPTH, D_MODEL, 2 * XA_WIDTH)) * D_MODEL ** -0.5
    w_out = nrm(ks[14], (DEPTH, MIX_WIDTH, D_MODEL)) * MIX_WIDTH ** -0.5
    final_g = 1.0 + 0.1 * nrm(ks[15], (D_MODEL,))
    return {"x": x, "mem": mem, "ln_g": ln_g, "w_in": w_in,
            "gmlp_ln_g": gmlp_ln_g, "gmlp_ln_b": gmlp_ln_b, "gmlp_ws": gmlp_ws, "gmlp_bs": gmlp_bs,
            "conv_w": conv_w, "dn_a_log": dn_a_log, "dn_dt_bias": dn_dt_bias, "dn_norm_g": dn_norm_g,
            "mem_norm_g": mem_norm_g, "w_mem_kv": w_mem_kv, "w_out": w_out, "final_g": final_g}


def _fwd_reference(x, mem, ln_g, w_in, gmlp_ln_g, gmlp_ln_b, gmlp_ws, gmlp_bs, conv_w, dn_a_log,
              dn_dt_bias, dn_norm_g, mem_norm_g, w_mem_kv, w_out, final_g):
    B, S, _ = x.shape
    M = mem.shape[1]
    for l in range(DEPTH):
        h = rmsnorm(x, ln_g[l])
        proj = h @ w_in[l]
        (g_u, g_v, g_z, d_q, d_k, d_v, d_z, d_a, d_b, c_q, c_z) = jnp.split(proj, SPLIT_POINTS, axis=-1)

        out_a = gmlp_spatial_gating(g_u, g_v, gmlp_ws[l], gmlp_bs[l], gmlp_ln_g[l], gmlp_ln_b[l]) * jax.nn.silu(g_z)

        qkv = causal_conv_silu(jnp.concatenate([d_q, d_k, d_v], axis=-1), conv_w[l])
        q, k, v = jnp.split(qkv, 3, axis=-1)
        to_heads = lambda t: t.reshape(B, S, DN_HEADS, DN_HEAD_DIM).transpose(0, 2, 1, 3)
        q = l2norm(to_heads(q)) * DN_HEAD_DIM ** -0.5
        k = l2norm(to_heads(k))
        v = to_heads(v)
        g = -jnp.exp(dn_a_log[l].astype(jnp.float32)) * jax.nn.softplus(
            d_a.astype(jnp.float32) + dn_dt_bias[l].astype(jnp.float32))
        beta = jax.nn.sigmoid(d_b.astype(jnp.float32))
        o = gated_delta_rule(q, k, v, g.transpose(0, 2, 1), beta.transpose(0, 2, 1))
        o = o.transpose(0, 2, 1, 3).astype(x.dtype)
        o = rmsnorm(o, dn_norm_g[l]) * jax.nn.silu(d_z.reshape(B, S, DN_HEADS, DN_HEAD_DIM))
        out_b = o.reshape(B, S, DN_WIDTH)

        m = rmsnorm(mem, mem_norm_g[l])
        mk, mv = jnp.split(m @ w_mem_kv[l], 2, axis=-1)
        mk = mk.reshape(B, M, XA_HEADS, XA_HEAD_DIM)
        mv = mv.reshape(B, M, XA_HEADS, XA_HEAD_DIM)
        cq = c_q.reshape(B, S, XA_HEADS, XA_HEAD_DIM)
        scores = jnp.einsum('bshd,bmhd->bhsm', cq, mk).astype(jnp.float32) * XA_HEAD_DIM ** -0.5
        p = jax.nn.softmax(scores, axis=-1).astype(mv.dtype)
        out_c = jnp.einsum('bhsm,bmhd->bshd', p, mv).reshape(B, S, XA_WIDTH) * jax.nn.silu(c_z)

        mixed = jnp.concatenate([out_a, out_b, out_c], axis=-1)
        x = x + mixed @ w_out[l]
    return rmsnorm(x, final_g)


import jax as _jax
import jax.numpy as _jnp

TWIN_FORMAT = 'train_step'
FWD_PARAMS = ['x', 'mem', 'ln_g', 'w_in', 'gmlp_ln_g', 'gmlp_ln_b', 'gmlp_ws', 'gmlp_bs', 'conv_w', 'dn_a_log', 'dn_dt_bias', 'dn_norm_g', 'mem_norm_g', 'w_mem_kv', 'w_out', 'final_g']
TWIN_WEIGHTS = ['ln_g', 'w_in', 'gmlp_ln_g', 'gmlp_ln_b', 'gmlp_ws', 'gmlp_bs', 'conv_w', 'dn_a_log', 'dn_dt_bias', 'dn_norm_g', 'mem_norm_g', 'w_mem_kv', 'w_out', 'final_g']
TWIN_DIFF_INPUT = 'x'
TWIN_INPUTS = ['x', 'mem', 'ln_g', 'w_in', 'gmlp_ln_g', 'gmlp_ln_b', 'gmlp_ws', 'gmlp_bs', 'conv_w', 'dn_a_log', 'dn_dt_bias', 'dn_norm_g', 'mem_norm_g', 'w_mem_kv', 'w_out', 'final_g', 'loss_target', 'm_ln_g', 'm_w_in', 'm_gmlp_ln_g', 'm_gmlp_ln_b', 'm_gmlp_ws', 'm_gmlp_bs', 'm_conv_w', 'm_dn_a_log', 'm_dn_dt_bias', 'm_dn_norm_g', 'm_mem_norm_g', 'm_w_mem_kv', 'm_w_out', 'm_final_g', 'v_ln_g', 'v_w_in', 'v_gmlp_ln_g', 'v_gmlp_ln_b', 'v_gmlp_ws', 'v_gmlp_bs', 'v_conv_w', 'v_dn_a_log', 'v_dn_dt_bias', 'v_dn_norm_g', 'v_mem_norm_g', 'v_w_mem_kv', 'v_w_out', 'v_final_g']
TWIN_OUTPUTS = ['loss', 'grad_x', 'grad_ln_g', 'grad_w_in', 'grad_gmlp_ln_g', 'grad_gmlp_ln_b', 'grad_gmlp_ws', 'grad_gmlp_bs', 'grad_conv_w', 'grad_dn_a_log', 'grad_dn_dt_bias', 'grad_dn_norm_g', 'grad_mem_norm_g', 'grad_w_mem_kv', 'grad_w_out', 'grad_final_g', 'delta_ln_g', 'delta_w_in', 'delta_gmlp_ln_g', 'delta_gmlp_ln_b', 'delta_gmlp_ws', 'delta_gmlp_bs', 'delta_conv_w', 'delta_dn_a_log', 'delta_dn_dt_bias', 'delta_dn_norm_g', 'delta_mem_norm_g', 'delta_w_mem_kv', 'delta_w_out', 'delta_final_g', 'new_m_ln_g', 'new_m_w_in', 'new_m_gmlp_ln_g', 'new_m_gmlp_ln_b', 'new_m_gmlp_ws', 'new_m_gmlp_bs', 'new_m_conv_w', 'new_m_dn_a_log', 'new_m_dn_dt_bias', 'new_m_dn_norm_g', 'new_m_mem_norm_g', 'new_m_w_mem_kv', 'new_m_w_out', 'new_m_final_g', 'new_v_ln_g', 'new_v_w_in', 'new_v_gmlp_ln_g', 'new_v_gmlp_ln_b', 'new_v_gmlp_ws', 'new_v_gmlp_bs', 'new_v_conv_w', 'new_v_dn_a_log', 'new_v_dn_dt_bias', 'new_v_dn_norm_g', 'new_v_mem_norm_g', 'new_v_w_mem_kv', 'new_v_w_out', 'new_v_final_g']
TWIN_LEAF_KINDS = {'loss': 'loss', 'grad_x': 'grad_x', 'grad_ln_g': 'grad_w', 'grad_w_in': 'grad_w', 'grad_gmlp_ln_g': 'grad_w', 'grad_gmlp_ln_b': 'grad_w', 'grad_gmlp_ws': 'grad_w', 'grad_gmlp_bs': 'grad_w', 'grad_conv_w': 'grad_w', 'grad_dn_a_log': 'grad_w', 'grad_dn_dt_bias': 'grad_w', 'grad_dn_norm_g': 'grad_w', 'grad_mem_norm_g': 'grad_w', 'grad_w_mem_kv': 'grad_w', 'grad_w_out': 'grad_w', 'grad_final_g': 'grad_w', 'delta_ln_g': 'delta_w', 'delta_w_in': 'delta_w', 'delta_gmlp_ln_g': 'delta_w', 'delta_gmlp_ln_b': 'delta_w', 'delta_gmlp_ws': 'delta_w', 'delta_gmlp_bs': 'delta_w', 'delta_conv_w': 'delta_w', 'delta_dn_a_log': 'delta_w', 'delta_dn_dt_bias': 'delta_w', 'delta_dn_norm_g': 'delta_w', 'delta_mem_norm_g': 'delta_w', 'delta_w_mem_kv': 'delta_w', 'delta_w_out': 'delta_w', 'delta_final_g': 'delta_w', 'new_m_ln_g': 'new_m', 'new_m_w_in': 'new_m', 'new_m_gmlp_ln_g': 'new_m', 'new_m_gmlp_ln_b': 'new_m', 'new_m_gmlp_ws': 'new_m', 'new_m_gmlp_bs': 'new_m', 'new_m_conv_w': 'new_m', 'new_m_dn_a_log': 'new_m', 'new_m_dn_dt_bias': 'new_m', 'new_m_dn_norm_g': 'new_m', 'new_m_mem_norm_g': 'new_m', 'new_m_w_mem_kv': 'new_m', 'new_m_w_out': 'new_m', 'new_m_final_g': 'new_m', 'new_v_ln_g': 'new_v', 'new_v_w_in': 'new_v', 'new_v_gmlp_ln_g': 'new_v', 'new_v_gmlp_ln_b': 'new_v', 'new_v_gmlp_ws': 'new_v', 'new_v_gmlp_bs': 'new_v', 'new_v_conv_w': 'new_v', 'new_v_dn_a_log': 'new_v', 'new_v_dn_dt_bias': 'new_v', 'new_v_dn_norm_g': 'new_v', 'new_v_mem_norm_g': 'new_v', 'new_v_w_mem_kv': 'new_v', 'new_v_w_out': 'new_v', 'new_v_final_g': 'new_v'}


def _forward(args):
    return _fwd_reference(*[args[k] for k in FWD_PARAMS])


def _output_shape():
    def fwd():
        inp = _fwd_setup_inputs(0)
        return _fwd_reference(*[inp[k] for k in FWD_PARAMS])
    out = _jax.eval_shape(fwd)
    return out.shape, out.dtype

N_MICROBATCH = 1
ADAM_LR = 0.001
ADAM_B1 = 0.9
ADAM_B2 = 0.999
ADAM_EPS = 1e-08
ADAM_WD = 0.01
ADAM_STEP = 10
PER_EXAMPLE_BATCH_AXIS = {'x': 0, 'mem': 0, 'loss_target': 0}
SHARED_INPUTS = []
_WEIGHT_DTYPES = {'ln_g': _jnp.float32, 'w_in': _jnp.float32, 'gmlp_ln_g': _jnp.float32, 'gmlp_ln_b': _jnp.float32, 'gmlp_ws': _jnp.float32, 'gmlp_bs': _jnp.float32, 'conv_w': _jnp.float32, 'dn_a_log': _jnp.float32, 'dn_dt_bias': _jnp.float32, 'dn_norm_g': _jnp.float32, 'mem_norm_g': _jnp.float32, 'w_mem_kv': _jnp.float32, 'w_out': _jnp.float32, 'final_g': _jnp.float32}
MOMENT_SCALE = {'ln_g': 9.368054e-02, 'w_in': 5.183921e-02, 'gmlp_ln_g': 3.327396e-02, 'gmlp_ln_b': 4.074723e-02, 'gmlp_ws': 3.241616e-02, 'gmlp_bs': 4.610509e-02, 'conv_w': 5.598989e-02, 'dn_a_log': 2.830521e-01, 'dn_dt_bias': 2.795498e-01, 'dn_norm_g': 2.191462e-01, 'mem_norm_g': 5.389659e-03, 'w_mem_kv': 7.546989e-03, 'w_out': 6.639970e-02, 'final_g': 3.201514e+01}


def _to_microbatches(a, axis):
    t = _jnp.moveaxis(a, axis, 0)
    t = t.reshape((N_MICROBATCH, t.shape[0] // N_MICROBATCH) + t.shape[1:])
    return _jnp.moveaxis(t, 1, axis + 1)


def setup_inputs(seed: int = 0) -> dict:
    inp = _fwd_setup_inputs(seed)
    key = _jax.random.fold_in(_jax.random.key(seed), 7919)
    shape, _ = _output_shape()
    out = dict(inp)
    out["loss_target"] = _jax.random.normal(_jax.random.fold_in(key, 0), shape, _jnp.float32)
    for i, name in enumerate(TWIN_WEIGHTS):
        w = inp[name].astype(_jnp.float32)
        if MOMENT_SCALE is None:
            s = _jnp.sqrt(_jnp.mean(_jnp.square(w)) + 1e-30)
        else:
            s = MOMENT_SCALE[name]
        km, kv = _jax.random.split(_jax.random.fold_in(key, i + 1))
        out[name] = w
        out["m_" + name] = s * _jax.random.normal(km, w.shape, _jnp.float32)
        out["v_" + name] = (s * s) * _jax.random.uniform(kv, w.shape, _jnp.float32, 0.5, 1.5)
    if N_MICROBATCH > 1:
        for name, axis in PER_EXAMPLE_BATCH_AXIS.items():
            out[name] = _to_microbatches(out[name], axis)
    return {'x': out['x'], 'mem': out['mem'], 'ln_g': out['ln_g'], 'w_in': out['w_in'], 'gmlp_ln_g': out['gmlp_ln_g'], 'gmlp_ln_b': out['gmlp_ln_b'], 'gmlp_ws': out['gmlp_ws'], 'gmlp_bs': out['gmlp_bs'], 'conv_w': out['conv_w'], 'dn_a_log': out['dn_a_log'], 'dn_dt_bias': out['dn_dt_bias'], 'dn_norm_g': out['dn_norm_g'], 'mem_norm_g': out['mem_norm_g'], 'w_mem_kv': out['w_mem_kv'], 'w_out': out['w_out'], 'final_g': out['final_g'], 'loss_target': out['loss_target'], 'm_ln_g': out['m_ln_g'], 'm_w_in': out['m_w_in'], 'm_gmlp_ln_g': out['m_gmlp_ln_g'], 'm_gmlp_ln_b': out['m_gmlp_ln_b'], 'm_gmlp_ws': out['m_gmlp_ws'], 'm_gmlp_bs': out['m_gmlp_bs'], 'm_conv_w': out['m_conv_w'], 'm_dn_a_log': out['m_dn_a_log'], 'm_dn_dt_bias': out['m_dn_dt_bias'], 'm_dn_norm_g': out['m_dn_norm_g'], 'm_mem_norm_g': out['m_mem_norm_g'], 'm_w_mem_kv': out['m_w_mem_kv'], 'm_w_out': out['m_w_out'], 'm_final_g': out['m_final_g'], 'v_ln_g': out['v_ln_g'], 'v_w_in': out['v_w_in'], 'v_gmlp_ln_g': out['v_gmlp_ln_g'], 'v_gmlp_ln_b': out['v_gmlp_ln_b'], 'v_gmlp_ws': out['v_gmlp_ws'], 'v_gmlp_bs': out['v_gmlp_bs'], 'v_conv_w': out['v_conv_w'], 'v_dn_a_log': out['v_dn_a_log'], 'v_dn_dt_bias': out['v_dn_dt_bias'], 'v_dn_norm_g': out['v_dn_norm_g'], 'v_mem_norm_g': out['v_mem_norm_g'], 'v_w_mem_kv': out['v_w_mem_kv'], 'v_w_out': out['v_w_out'], 'v_final_g': out['v_final_g']}


def _loss(weights, diff, rest, loss_target):
    with _jax.named_scope("forward"):
        args = {**rest, TWIN_DIFF_INPUT: diff, **{k: w.astype(_WEIGHT_DTYPES[k]) for k, w in weights.items()}}
        y = _forward(args)
    with _jax.named_scope("loss_head"):
        err = _jnp.square(y.astype(_jnp.float32) - loss_target)
        return 0.5 * _jnp.sum(_jnp.mean(err, axis=-1)) if err.ndim else 0.5 * err


def _adamw(w, g, m, v):
    m = ADAM_B1 * m + (1.0 - ADAM_B1) * g
    v = ADAM_B2 * v + (1.0 - ADAM_B2) * _jnp.square(g)
    m_hat = m / (1.0 - ADAM_B1 ** ADAM_STEP)
    v_hat = v / (1.0 - ADAM_B2 ** ADAM_STEP)
    delta = -ADAM_LR * (m_hat / (_jnp.sqrt(v_hat) + ADAM_EPS) + ADAM_WD * w)
    return delta, m, v


def reference(x, mem, ln_g, w_in, gmlp_ln_g, gmlp_ln_b, gmlp_ws, gmlp_bs, conv_w, dn_a_log, dn_dt_bias, dn_norm_g, mem_norm_g, w_mem_kv, w_out, final_g, loss_target, m_ln_g, m_w_in, m_gmlp_ln_g, m_gmlp_ln_b, m_gmlp_ws, m_gmlp_bs, m_conv_w, m_dn_a_log, m_dn_dt_bias, m_dn_norm_g, m_mem_norm_g, m_w_mem_kv, m_w_out, m_final_g, v_ln_g, v_w_in, v_gmlp_ln_g, v_gmlp_ln_b, v_gmlp_ws, v_gmlp_bs, v_conv_w, v_dn_a_log, v_dn_dt_bias, v_dn_norm_g, v_mem_norm_g, v_w_mem_kv, v_w_out, v_final_g):
    given = dict(x=x, mem=mem, ln_g=ln_g, w_in=w_in, gmlp_ln_g=gmlp_ln_g, gmlp_ln_b=gmlp_ln_b, gmlp_ws=gmlp_ws, gmlp_bs=gmlp_bs, conv_w=conv_w, dn_a_log=dn_a_log, dn_dt_bias=dn_dt_bias, dn_norm_g=dn_norm_g, mem_norm_g=mem_norm_g, w_mem_kv=w_mem_kv, w_out=w_out, final_g=final_g, loss_target=loss_target, m_ln_g=m_ln_g, m_w_in=m_w_in, m_gmlp_ln_g=m_gmlp_ln_g, m_gmlp_ln_b=m_gmlp_ln_b, m_gmlp_ws=m_gmlp_ws, m_gmlp_bs=m_gmlp_bs, m_conv_w=m_conv_w, m_dn_a_log=m_dn_a_log, m_dn_dt_bias=m_dn_dt_bias, m_dn_norm_g=m_dn_norm_g, m_mem_norm_g=m_mem_norm_g, m_w_mem_kv=m_w_mem_kv, m_w_out=m_w_out, m_final_g=m_final_g, v_ln_g=v_ln_g, v_w_in=v_w_in, v_gmlp_ln_g=v_gmlp_ln_g, v_gmlp_ln_b=v_gmlp_ln_b, v_gmlp_ws=v_gmlp_ws, v_gmlp_bs=v_gmlp_bs, v_conv_w=v_conv_w, v_dn_a_log=v_dn_a_log, v_dn_dt_bias=v_dn_dt_bias, v_dn_norm_g=v_dn_norm_g, v_mem_norm_g=v_mem_norm_g, v_w_mem_kv=v_w_mem_kv, v_w_out=v_w_out, v_final_g=v_final_g)
    weights = {n: given[n] for n in TWIN_WEIGHTS}
    shared = {n: given[n] for n in SHARED_INPUTS}
    per_example = {n: given[n] for n in ['x', 'mem']}
    grad_fn = _jax.value_and_grad(_loss, argnums=(0, 1))

    def one_microbatch(ex, loss_target):
        ex = dict(ex)
        diff = ex.pop(TWIN_DIFF_INPUT)
        return grad_fn(weights, diff, {**shared, **ex}, loss_target)

    if N_MICROBATCH == 1:
        loss, (grad_w, grad_x) = one_microbatch(per_example, given["loss_target"])
    else:
        def body(carry, xs):
            loss_sum, grad_sum = carry
            l_k, (gw_k, gx_k) = one_microbatch(xs[0], xs[1])
            with _jax.named_scope("update"):
                return (loss_sum + l_k, _jax.tree.map(_jnp.add, grad_sum, gw_k)), gx_k

        init = (_jnp.zeros((), _jnp.float32), _jax.tree.map(_jnp.zeros_like, weights))
        (loss, grad_w), grad_x = _jax.lax.scan(body, init, (per_example, given["loss_target"]))
    with _jax.named_scope("update"):
        delta_w, new_m, new_v = {}, {}, {}
        for n in TWIN_WEIGHTS:
            delta_w[n], new_m[n], new_v[n] = _adamw(weights[n], grad_w[n], given["m_" + n], given["v_" + n])
    return (loss, grad_x, *[grad_w[n] for n in TWIN_WEIGHTS], *[delta_w[n] for n in TWIN_WEIGHTS],
            *[new_m[n] for n in TWIN_WEIGHTS], *[new_v[n] for n in TWIN_WEIGHTS])
```

```python
import functools

import jax
import jax.numpy as jnp
from jax import lax
from jax.experimental import pallas as pl
from jax.experimental.pallas import tpu as pltpu

F32 = jnp.float32
BF16 = jnp.bfloat16
HIGHEST = lax.Precision.HIGHEST
MESH_ID = pl.DeviceIdType.MESH

N_DEV = 8
EPS = 1e-6
GMLP_W = 512
GMLP_G = 4
GMLP_T = 128
DN_W = 1024
DN_H = 8
HEAD = 128
DN_K = 4
CH = 64
XA_W = 512
XA_H = 4
LANE = 128
HALO = 8
MAIN_W = 4 * DN_W + 3 * GMLP_W + 2 * XA_W
MIX_W = DN_W + GMLP_W + XA_W
VMEM_LIMIT = 56 * 1024 * 1024

ADAM_LR = 0.001
ADAM_B1 = 0.9
ADAM_B2 = 0.999
ADAM_EPS = 1e-08
ADAM_WD = 0.01
ADAM_STEP = 10


def _sds(shape, dtype=F32):
    return jax.ShapeDtypeStruct(tuple(shape), dtype)


def _params(sem=None):
    if sem is None:
        return pltpu.CompilerParams(vmem_limit_bytes=VMEM_LIMIT)
    return pltpu.CompilerParams(dimension_semantics=tuple(sem), vmem_limit_bytes=VMEM_LIMIT)


def _tile(n, prefs):
    for p in prefs:
        if n % p == 0:
            return p
    return n


def _mm(a, b):
    return jnp.dot(a.astype(BF16), b.astype(BF16), preferred_element_type=F32)


def _mm_nt(a, b):
    return lax.dot_general(a.astype(BF16), b.astype(BF16), (((1,), (1,)), ((), ())), preferred_element_type=F32)


def _mm_tn(a, b):
    return lax.dot_general(a.astype(BF16), b.astype(BF16), (((0,), (0,)), ((), ())), preferred_element_type=F32)


def _mm_hi(a, b):
    return jnp.dot(a, b, precision=HIGHEST, preferred_element_type=F32)


_GELU_C = 0.7978845608028654
_GELU_A = 0.044715


def _gelu(x):
    return 0.5 * x * (1.0 + jnp.tanh(_GELU_C * (x + _GELU_A * x * x * x)))


def _gelu_grad(x):
    t = jnp.tanh(_GELU_C * (x + _GELU_A * x * x * x))
    return 0.5 * (1.0 + t) + 0.5 * x * (1.0 - t * t) * _GELU_C * (1.0 + 3.0 * _GELU_A * x * x)


def _silu(x):
    return x * jax.nn.sigmoid(x)


def _silu_grad(x):
    s = jax.nn.sigmoid(x)
    return s * (1.0 + x * (1.0 - s))


def _rowsum(x):
    return jnp.sum(x, axis=-1, keepdims=True)


def _colsum(x):
    return jnp.sum(x, axis=0, keepdims=True)


def _iota2(shape, dim):
    return lax.broadcasted_iota(jnp.int32, shape, dim)


def _chunk_tri(tm, upper):
    r = _iota2((tm, tm), 0)
    c = _iota2((tm, tm), 1)
    same = lax.shift_right_logical(r, 6) == lax.shift_right_logical(c, 6)
    tri = (r <= c) if upper else (r >= c)
    return jnp.where(same & tri, 1.0, 0.0).astype(F32)


def _exchange(gather, scatter, name):
    arrs = list(gather) + list(scatter)
    n = len(arrs)
    n_g = len(gather)

    def body(*refs):
        ins = refs[:n]
        outs = refs[n:2 * n]
        send_sems, recv_sems, loc_sems = refs[2 * n:]
        x, y, c = lax.axis_index("x"), lax.axis_index("y"), lax.axis_index("c")
        me = 4 * x + 2 * y + c
        started = []
        for a in range(n):
            for j in range(1, N_DEV):
                px = 1 - x if (j >> 2) & 1 else x
                py = 1 - y if (j >> 1) & 1 else y
                pc = 1 - c if j & 1 else c
                peer = 4 * px + 2 * py + pc
                src = ins[a] if a < n_g else ins[a].at[peer]
                send = pltpu.make_async_remote_copy(
                    src_ref=src, dst_ref=outs[a].at[me], send_sem=send_sems.at[a, j - 1], recv_sem=recv_sems.at[a, j - 1],
                    device_id=(px, py, pc), device_id_type=MESH_ID)
                send.start()
                recv = pltpu.make_async_remote_copy(
                    src_ref=src, dst_ref=outs[a].at[peer], send_sem=send_sems.at[a, j - 1], recv_sem=recv_sems.at[a, j - 1],
                    device_id=(px, py, pc), device_id_type=MESH_ID)
                started.append((send, recv))
            src = ins[a] if a < n_g else ins[a].at[me]
            own = pltpu.make_async_copy(src, outs[a].at[me], loc_sems.at[a])
            own.start()
            started.append((own, None))
        for send, recv in started:
            if recv is None:
                send.wait()
            else:
                send.wait_send()
                recv.wait_recv()

    out_shape = [_sds((N_DEV,) + a.shape, a.dtype) for a in gather] + [_sds(a.shape, a.dtype) for a in scatter]
    any_spec = pl.BlockSpec(memory_space=pl.ANY)
    return pl.pallas_call(
        body, name=name, out_shape=out_shape,
        in_specs=[any_spec] * n, out_specs=[any_spec] * n,
        scratch_shapes=[pltpu.SemaphoreType.DMA((n, N_DEV - 1)), pltpu.SemaphoreType.DMA((n, N_DEV - 1)),
                        pltpu.SemaphoreType.DMA((n,))],
        compiler_params=pltpu.CompilerParams(has_side_effects=True),
    )(*arrs)


def _inproj(x, ln_g, w_main, w_ab):
    s, d = x.shape
    n = w_main.shape[1]
    tm = _tile(s, (512, 256, 128))
    tn = _tile(n, (1664, 512, 128))

    def body(x_ref, g_ref, w_ref, wab_ref, proj_ref, ab_ref, h_ref, hs):
        @pl.when(pl.program_id(1) == 0)
        def _():
            xv = x_ref[...]
            r = lax.rsqrt(jnp.mean(xv * xv, axis=-1, keepdims=True) + EPS)
            h = (xv * r * g_ref[...]).astype(BF16)
            hs[...] = h
            h_ref[...] = h
            ab_ref[...] = jnp.dot(h, wab_ref[...], preferred_element_type=F32)

        proj_ref[...] = jnp.dot(hs[...], w_ref[...], preferred_element_type=F32)

    return pl.pallas_call(
        body, name="inproj", grid=(s // tm, n // tn),
        in_specs=[pl.BlockSpec((tm, d), lambda i, j: (i, 0)), pl.BlockSpec((1, d), lambda i, j: (0, 0)),
                  pl.BlockSpec((d, tn), lambda i, j: (0, j)), pl.BlockSpec((d, LANE), lambda i, j: (0, 0))],
        out_specs=[pl.BlockSpec((tm, tn), lambda i, j: (i, j)), pl.BlockSpec((tm, LANE), lambda i, j: (i, 0)),
                   pl.BlockSpec((tm, d), lambda i, j: (i, 0))],
        out_shape=[_sds((s, n)), _sds((s, LANE)), _sds((s, d), BF16)],
        scratch_shapes=[pltpu.VMEM((tm, d), BF16)],
        compiler_params=_params(("parallel", "arbitrary")),
    )(x, ln_g, w_main, w_ab)


def _matmul_tn(a, b, name):
    k, m = a.shape
    n = b.shape[1]
    tm = _tile(m, (1024, 512, 256, 128))
    tn = _tile(n, (512, 256, 128))
    tk = _tile(k, (1024, 512, 256, 128))

    def body(a_ref, b_ref, o_ref):
        @pl.when(pl.program_id(2) == 0)
        def _():
            o_ref[...] = jnp.zeros_like(o_ref)

        o_ref[...] += _mm_tn(a_ref[...], b_ref[...])

    return pl.pallas_call(
        body, name=name, grid=(m // tm, n // tn, k // tk),
        in_specs=[pl.BlockSpec((tk, tm), lambda i, j, l: (l, i)), pl.BlockSpec((tk, tn), lambda i, j, l: (l, j))],
        out_specs=pl.BlockSpec((tm, tn), lambda i, j, l: (i, j)),
        out_shape=_sds((m, n)),
        compiler_params=_params(("parallel", "parallel", "arbitrary")),
    )(a, b)


def _dh(pieces, weights):
    s = pieces[0].shape[0]
    d = weights[0].shape[0]
    npc = len(pieces)
    tm = _tile(s, (256, 128))
    tn = _tile(d, (1024, 512, 256, 128))

    def body(*refs):
        o_ref = refs[2 * npc]
        acc = _mm_nt(refs[0][...], refs[npc][...])
        for p in range(1, npc):
            acc += _mm_nt(refs[p][...], refs[npc + p][...])
        o_ref[...] = acc

    in_specs = [pl.BlockSpec((tm, p.shape[1]), lambda j, i: (i, 0)) for p in pieces]
    in_specs += [pl.BlockSpec((tn, w.shape[1]), lambda j, i: (j, 0)) for w in weights]
    return pl.pallas_call(
        body, name="dh", grid=(d // tn, s // tm), in_specs=in_specs,
        out_specs=pl.BlockSpec((tm, tn), lambda j, i: (i, j)), out_shape=_sds((s, d)),
        compiler_params=_params(("parallel", "parallel")),
    )(*pieces, *weights)


def _rms_bwd(x, dh, dx2, ln_g):
    s, d = x.shape
    tm = _tile(s, (256, 128))

    def body(x_ref, dh_ref, dx2_ref, g_ref, gx_ref, dg_ref):
        @pl.when(pl.program_id(0) == 0)
        def _():
            dg_ref[...] = jnp.zeros_like(dg_ref)

        xv = x_ref[...]
        r = lax.rsqrt(jnp.mean(xv * xv, axis=-1, keepdims=True) + EPS)
        xhat = xv * r
        dhv = dh_ref[...]
        dg_ref[...] += _colsum(dhv * xhat)
        dxh = dhv * g_ref[...]
        gx_ref[...] = dx2_ref[...] + r * (dxh - xhat * jnp.mean(dxh * xhat, axis=-1, keepdims=True))

    row = pl.BlockSpec((tm, d), lambda i: (i, 0))
    vec = pl.BlockSpec((1, d), lambda i: (0, 0))
    return pl.pallas_call(
        body, name="rms_bwd", grid=(s // tm,), in_specs=[row, row, row, vec], out_specs=[row, vec],
        out_shape=[_sds((s, d)), _sds((1, d))], compiler_params=_params(("arbitrary",)),
    )(x, dh, dx2, ln_g)


def _final(x, tgt, out_b, out_a, out_c, w_out, final_g):
    s, d = x.shape
    tm = _tile(s, (256, 128))

    def body(x_ref, t_ref, b_ref, a_ref, c_ref, w_ref, g_ref, dx2_ref, dx2b_ref, dm_ref, loss_ref, dg_ref):
        @pl.when(pl.program_id(0) == 0)
        def _():
            loss_ref[...] = jnp.zeros_like(loss_ref)
            dg_ref[...] = jnp.zeros_like(dg_ref)

        x2 = x_ref[...]
        x2 += jnp.dot(b_ref[...], w_ref[0:DN_W, :], preferred_element_type=F32)
        x2 += jnp.dot(a_ref[...], w_ref[DN_W:DN_W + GMLP_W, :], preferred_element_type=F32)
        x2 += jnp.dot(c_ref[...], w_ref[DN_W + GMLP_W:MIX_W, :], preferred_element_type=F32)
        r = lax.rsqrt(jnp.mean(x2 * x2, axis=-1, keepdims=True) + EPS)
        xhat = x2 * r
        g = g_ref[...]
        err = xhat * g - t_ref[...]
        tok = 0.5 * jnp.mean(err * err, axis=-1, keepdims=True)
        loss_ref[...] += jnp.broadcast_to(_colsum(tok), loss_ref.shape)
        dy = err * (1.0 / d)
        dg_ref[...] += _colsum(dy * xhat)
        dxh = dy * g
        dx2 = r * (dxh - xhat * jnp.mean(dxh * xhat, axis=-1, keepdims=True))
        dx2_ref[...] = dx2
        dx2b = dx2.astype(BF16)
        dx2b_ref[...] = dx2b
        dm_ref[...] = _mm_nt(dx2b, w_ref[...])

    row = pl.BlockSpec((tm, d), lambda i: (i, 0))
    vec = pl.BlockSpec((1, d), lambda i: (0, 0))
    return pl.pallas_call(
        body, name="final", grid=(s // tm,),
        in_specs=[row, row, pl.BlockSpec((tm, DN_W), lambda i: (i, 0)), pl.BlockSpec((tm, GMLP_W), lambda i: (i, 0)),
                  pl.BlockSpec((tm, XA_W), lambda i: (i, 0)), pl.BlockSpec((MIX_W, d), lambda i: (0, 0)), vec],
        out_specs=[row, row, pl.BlockSpec((tm, MIX_W), lambda i: (i, 0)), pl.BlockSpec((1, LANE), lambda i: (0, 0)), vec],
        out_shape=[_sds((s, d)), _sds((s, d), BF16), _sds((s, MIX_W)), _sds((1, LANE)), _sds((1, d))],
        compiler_params=_params(("arbitrary",)),
    )(x, tgt, out_b, out_a, out_c, w_out, final_g)


GU_BLK = (4 * DN_W) // GMLP_W


def _gmlp_norm(gv, lng, lnb):
    va = _gelu(gv)
    mu = jnp.mean(va, axis=-1, keepdims=True)
    xc = va - mu
    rstd = lax.rsqrt(jnp.mean(xc * xc, axis=-1, keepdims=True) + EPS)
    vhat = xc * rstd
    return vhat, rstd, vhat * lng + lnb


def _gmlp_fwd(proj, lng, lnb, ws, bs_t):
    s = proj.shape[0]
    tm = _tile(s, (512, 256, 128))

    def body(u_ref, v_ref, z_ref, lng_ref, lnb_ref, ws_ref, bst_ref, o_ref):
        _, _, vn = _gmlp_norm(v_ref[...], lng_ref[...], lnb_ref[...])
        tri = _iota2((GMLP_T, GMLP_T), 0) >= _iota2((GMLP_T, GMLP_T), 1)
        for g in range(GMLP_G):
            cs = slice(g * HEAD, (g + 1) * HEAD)
            w = jnp.where(tri, ws_ref[g], 0.0).astype(BF16)
            b = bst_ref[:, g:g + 1]
            for c in range(tm // GMLP_T):
                rs = slice(c * GMLP_T, (c + 1) * GMLP_T)
                sg = _mm(w, vn[rs, cs]) + b
                o_ref[rs, cs] = (_gelu(u_ref[rs, cs]) * sg * _silu(z_ref[rs, cs])).astype(BF16)

    col = lambda k: pl.BlockSpec((tm, GMLP_W), lambda i: (i, GU_BLK + k))
    vec = pl.BlockSpec((1, GMLP_W), lambda i: (0, 0))
    return pl.pallas_call(
        body, name="gmlp_fwd", grid=(s // tm,),
        in_specs=[col(0), col(1), col(2), vec, vec, pl.BlockSpec((GMLP_G, GMLP_T, GMLP_T), lambda i: (0, 0, 0)),
                  pl.BlockSpec((GMLP_T, GMLP_G), lambda i: (0, 0))],
        out_specs=pl.BlockSpec((tm, GMLP_W), lambda i: (i, 0)), out_shape=_sds((s, GMLP_W), BF16),
        compiler_params=_params(("parallel",)),
    )(proj, proj, proj, lng, lnb, ws, bs_t)


def _gmlp_bwd(proj, dmixed, lng, lnb, ws, bs_t):
    s = proj.shape[0]
    tm = _tile(s, (512, 256, 128))

    def body(u_ref, v_ref, z_ref, d_ref, lng_ref, lnb_ref, ws_ref, bst_ref,
             dp_ref, dws_ref, dbst_ref, dlng_ref, dlnb_ref, dvn):
        @pl.when(pl.program_id(0) == 0)
        def _():
            dws_ref[...] = jnp.zeros_like(dws_ref)
            dbst_ref[...] = jnp.zeros_like(dbst_ref)
            dlng_ref[...] = jnp.zeros_like(dlng_ref)
            dlnb_ref[...] = jnp.zeros_like(dlnb_ref)

        gv = v_ref[...]
        lng_v = lng_ref[...]
        vhat, rstd, vn = _gmlp_norm(gv, lng_v, lnb_ref[...])
        tri = _iota2((GMLP_T, GMLP_T), 0) >= _iota2((GMLP_T, GMLP_T), 1)
        for g in range(GMLP_G):
            cs = slice(g * HEAD, (g + 1) * HEAD)
            w = jnp.where(tri, ws_ref[g], 0.0).astype(BF16)
            b = bst_ref[:, g:g + 1]
            dw_acc = jnp.zeros((GMLP_T, GMLP_T), F32)
            db_acc = jnp.zeros((GMLP_T, 1), F32)
            for c in range(tm // GMLP_T):
                rs = slice(c * GMLP_T, (c + 1) * GMLP_T)
                vn_b = vn[rs, cs]
                sg = _mm(w, vn_b) + b
                gu = u_ref[rs, cs]
                gz = z_ref[rs, cs]
                da = d_ref[rs, cs]
                uact = _gelu(gu)
                sz = _silu(gz)
                ds = da * uact * sz
                dp_ref[rs, cs] = (da * sg * sz * _gelu_grad(gu)).astype(BF16)
                dp_ref[rs, 2 * GMLP_W + g * HEAD:2 * GMLP_W + (g + 1) * HEAD] = (da * uact * sg * _silu_grad(gz)).astype(BF16)
                dw_acc += _mm_nt(ds, vn_b)
                db_acc += _rowsum(ds)
                dvn[rs, cs] = _mm_tn(w, ds)
            dws_ref[g] += jnp.where(tri, dw_acc, 0.0)
            dbst_ref[:, g:g + 1] += db_acc
        dvn_v = dvn[...]
        dlng_ref[...] += _colsum(dvn_v * vhat)
        dlnb_ref[...] += _colsum(dvn_v)
        dvh = dvn_v * lng_v
        dva = rstd * (dvh - jnp.mean(dvh, axis=-1, keepdims=True) - vhat * jnp.mean(dvh * vhat, axis=-1, keepdims=True))
        dp_ref[:, GMLP_W:2 * GMLP_W] = (dva * _gelu_grad(gv)).astype(BF16)

    col = lambda k: pl.BlockSpec((tm, GMLP_W), lambda i: (i, GU_BLK + k))
    vec = pl.BlockSpec((1, GMLP_W), lambda i: (0, 0))
    wsp = pl.BlockSpec((GMLP_G, GMLP_T, GMLP_T), lambda i: (0, 0, 0))
    bsp = pl.BlockSpec((GMLP_T, GMLP_G), lambda i: (0, 0))
    return pl.pallas_call(
        body, name="gmlp_bwd", grid=(s // tm,),
        in_specs=[col(0), col(1), col(2), pl.BlockSpec((tm, GMLP_W), lambda i: (i, DN_W // GMLP_W)), vec, vec, wsp, bsp],
        out_specs=[pl.BlockSpec((tm, 3 * GMLP_W), lambda i: (i, 0)), wsp, bsp, vec, vec],
        out_shape=[_sds((s, 3 * GMLP_W), BF16), _sds((GMLP_G, GMLP_T, GMLP_T)), _sds((GMLP_T, GMLP_G)),
                   _sds((1, GMLP_W)), _sds((1, GMLP_W))],
        scratch_shapes=[pltpu.VMEM((tm, GMLP_W), F32)],
        compiler_params=_params(("arbitrary",)),
    )(proj, proj, proj, dmixed, lng, lnb, ws, bs_t)


CQ_BLK = (4 * DN_W + 3 * GMLP_W) // XA_W


def _memkv_fwd(mem, g, w_kv):
    nm, d = mem.shape

    def body(m_ref, g_ref, w_ref, kv_ref):
        mv = m_ref[...]
        r = lax.rsqrt(jnp.mean(mv * mv, axis=-1, keepdims=True) + EPS)
        kv_ref[...] = _mm(mv * r * g_ref[...], w_ref[...])

    return pl.pallas_call(body, name="memkv_fwd", out_shape=_sds((nm, 2 * XA_W)), compiler_params=_params())(mem, g, w_kv)


def _memkv_bwd(mem, g, w_kv, dkv):
    nm, d = mem.shape

    def body(m_ref, g_ref, w_ref, dkv_ref, dw_ref, dg_ref):
        mv = m_ref[...]
        r = lax.rsqrt(jnp.mean(mv * mv, axis=-1, keepdims=True) + EPS)
        xhat = mv * r
        dkv_v = dkv_ref[...]
        dw_ref[...] = _mm_tn(xhat * g_ref[...], dkv_v)
        dg_ref[...] = _colsum(_mm_nt(dkv_v, w_ref[...]) * xhat)

    return pl.pallas_call(body, name="memkv_bwd", out_shape=[_sds((d, 2 * XA_W)), _sds((1, d))],
                          compiler_params=_params())(mem, g, w_kv, dkv)


def _xattn_probs(q, mk):
    sc = _mm_nt(q, mk) * (HEAD ** -0.5)
    e = jnp.exp(sc - jnp.max(sc, axis=-1, keepdims=True))
    return e / _rowsum(e)


def _xattn_fwd(proj, mkv):
    s = proj.shape[0]
    nm = mkv.shape[0]
    tm = _tile(s, (512, 256, 128))

    def body(q_ref, z_ref, kv_ref, o_ref):
        for h in range(XA_H):
            cs = slice(h * HEAD, (h + 1) * HEAD)
            p = _xattn_probs(q_ref[:, cs], kv_ref[:, cs])
            ctx = _mm(p, kv_ref[:, XA_W + h * HEAD:XA_W + (h + 1) * HEAD])
            o_ref[:, cs] = (ctx * _silu(z_ref[:, cs])).astype(BF16)

    col = lambda k: pl.BlockSpec((tm, XA_W), lambda i: (i, CQ_BLK + k))
    return pl.pallas_call(
        body, name="xattn_fwd", grid=(s // tm,),
        in_specs=[col(0), col(1), pl.BlockSpec((nm, 2 * XA_W), lambda i: (0, 0))],
        out_specs=pl.BlockSpec((tm, XA_W), lambda i: (i, 0)), out_shape=_sds((s, XA_W), BF16),
        compiler_params=_params(("parallel",)),
    )(proj, proj, mkv)


def _xattn_bwd(proj, dmixed, mkv):
    s = proj.shape[0]
    nm = mkv.shape[0]
    tm = _tile(s, (512, 256, 128))

    def body(q_ref, z_ref, d_ref, kv_ref, dp_ref, dkv_ref):
        @pl.when(pl.program_id(0) == 0)
        def _():
            dkv_ref[...] = jnp.zeros_like(dkv_ref)

        for h in range(XA_H):
            cs = slice(h * HEAD, (h + 1) * HEAD)
            vs = slice(XA_W + h * HEAD, XA_W + (h + 1) * HEAD)
            q = q_ref[:, cs]
            z = z_ref[:, cs]
            mk = kv_ref[:, cs]
            mv = kv_ref[:, vs]
            p = _xattn_probs(q, mk)
            ctx = _mm(p, mv)
            dc = d_ref[:, cs]
            dctx = dc * _silu(z)
            dp_ref[:, vs] = (dc * ctx * _silu_grad(z)).astype(BF16)
            dp = _mm_nt(dctx, mv)
            dkv_ref[:, vs] += _mm_tn(p, dctx)
            ds = p * (dp - _rowsum(dp * p)) * (HEAD ** -0.5)
            dp_ref[:, cs] = _mm(ds, mk).astype(BF16)
            dkv_ref[:, cs] += _mm_tn(ds, q)

    col = lambda k: pl.BlockSpec((tm, XA_W), lambda i: (i, CQ_BLK + k))
    kvs = pl.BlockSpec((nm, 2 * XA_W), lambda i: (0, 0))
    return pl.pallas_call(
        body, name="xattn_bwd", grid=(s // tm,),
        in_specs=[col(0), col(1), pl.BlockSpec((tm, XA_W), lambda i: (i, (DN_W + GMLP_W) // XA_W)), kvs],
        out_specs=[pl.BlockSpec((tm, 2 * XA_W), lambda i: (i, 0)), kvs],
        out_shape=[_sds((s, 2 * XA_W), BF16), _sds((nm, 2 * XA_W))],
        compiler_params=_params(("arbitrary",)),
    )(proj, proj, dmixed, mkv)


def _softplus(x):
    return jnp.maximum(x, 0.0) + jnp.log1p(jnp.exp(-jnp.abs(x)))


def _dn_pre(proj, ab, conv_w, alog_row, dt_row):
    s = proj.shape[0]
    tm = _tile(s, (256, 128))
    w3 = 3 * DN_W

    def body(x_ref, halo_ref, ab_ref, cw_ref, al_ref, dt_ref, q_ref, k_ref, v_ref, gb_ref, gbt_ref, ext):
        i = pl.program_id(0)
        ext[0:HALO, :] = jnp.where(i > 0, halo_ref[...], 0.0)
        ext[HALO:HALO + tm, :] = x_ref[...]
        yc = cw_ref[0:1, :] * ext[pl.ds(HALO - DN_K + 1, tm), :]
        for t in range(1, DN_K):
            yc += cw_ref[t:t + 1, :] * ext[pl.ds(HALO - DN_K + 1 + t, tm), :]
        act = _silu(yc)
        for h in range(DN_H):
            cs = slice(h * HEAD, (h + 1) * HEAD)
            qa = act[:, cs]
            q_ref[:, cs] = qa * (lax.rsqrt(_rowsum(qa * qa) + EPS) * (HEAD ** -0.5))
            ka = act[:, DN_W + h * HEAD:DN_W + (h + 1) * HEAD]
            k_ref[:, cs] = ka * lax.rsqrt(_rowsum(ka * ka) + EPS)
        v_ref[...] = act[:, 2 * DN_W:w3]
        abv = ab_ref[...]
        lane = _iota2((tm, LANE), 1)
        g = jnp.where(lane < DN_H, -jnp.exp(al_ref[...]) * _softplus(abv + dt_ref[...]), 0.0)
        gc = _mm_hi(_chunk_tri(tm, False), g)
        gbv = jnp.where(lane < DN_H, gc, jnp.where(lane < 2 * DN_H, jax.nn.sigmoid(abv), 0.0))
        gb_ref[...] = gbv
        for c in range(tm // CH):
            gbt_ref[c] = gbv[c * CH:(c + 1) * CH, :].T[0:2 * DN_H, :]

    hb = tm // HALO
    row = lambda w: pl.BlockSpec((tm, w), lambda i: (i, 0))
    vec = pl.BlockSpec((1, LANE), lambda i: (0, 0))
    return pl.pallas_call(
        body, name="dn_pre", grid=(s // tm,),
        in_specs=[row(w3), pl.BlockSpec((HALO, w3), lambda i: (jnp.maximum(i * hb - 1, 0), 0)), row(LANE),
                  pl.BlockSpec((DN_K, w3), lambda i: (0, 0)), vec, vec],
        out_specs=[row(DN_W), row(DN_W), row(DN_W), row(LANE), pl.BlockSpec((tm // CH, 2 * DN_H, CH), lambda i: (i, 0, 0))],
        out_shape=[_sds((s, DN_W)), _sds((s, DN_W)), _sds((s, DN_W)), _sds((s, LANE)), _sds((s // CH, 2 * DN_H, CH))],
        scratch_shapes=[pltpu.VMEM((tm + HALO, w3), F32)],
        compiler_params=_params(("parallel",)),
    )(proj, proj, ab, conv_w, alog_row, dt_row)


def _chunk_common(q, k, v, gbv, gbt, h):
    gc = gbv[:, h:h + 1]
    beta = gbv[:, DN_H + h:DN_H + h + 1]
    gr = gbt[h:h + 1, :]
    ii = _iota2((CH, CH), 0)
    jj = _iota2((CH, CH), 1)
    dec = jnp.exp(jnp.where(ii >= jj, gc - gr, -1e30))
    eg = jnp.exp(gc)
    gl = gr[:, CH - 1:CH]
    ekd = jnp.exp(gl - gc)
    kb = k * beta
    a = jnp.where(ii > jj, _mm_nt(kb, k) * dec, 0.0)
    ai = _mm_nt(q, k) * dec
    return dict(gc=gc, beta=beta, dec=dec, eg=eg, gl=gl, ekd=ekd, kb=kb, vb=v * beta, kbe=kb * eg, a=a, ai=ai,
                strict=ii > jj, eye=jnp.where(ii == jj, 1.0, 0.0).astype(F32))


def _dn_local(q, k, v, gb, gbt):
    s = q.shape[0]
    cpb = 4 if (s // CH) % 4 == 0 else 1
    tb = cpb * CH
    nblk = s // tb

    def body(q_ref, k_ref, v_ref, gb_ref, gbt_ref, u_ref, w_ref, qg_ref, kd_ref, t_ref, ai_ref, egl_ref):
        def chunk(c, carry):
            r0 = pl.multiple_of(c * CH, CH)
            rows = pl.ds(r0, CH)
            gbv = gb_ref[rows, :]
            gbt_v = gbt_ref[c]
            for h in range(DN_H):
                cs = slice(h * HEAD, (h + 1) * HEAD)
                qv = q_ref[rows, cs]
                kv = k_ref[rows, cs]
                m = _chunk_common(qv, kv, v_ref[rows, cs], gbv, gbt_v, h)
                a = m["a"]
                t = m["eye"] - a
                p = _mm_hi(a, a)
                t = t + _mm_hi(t, p)
                for _ in range(4):
                    p = _mm_hi(p, p)
                    t = t + _mm_hi(t, p)
                t_ref[h, rows, :] = t
                ai_ref[h, rows, :] = m["ai"]
                u_ref[rows, cs] = _mm(t, m["vb"])
                w_ref[rows, cs] = _mm(t, m["kbe"]).astype(BF16)
                qg_ref[rows, cs] = (qv * m["eg"]).astype(BF16)
                kd_ref[rows, cs] = (kv * m["ekd"]).astype(BF16)
                egl_ref[c, h:h + 1, :] = jnp.broadcast_to(jnp.exp(m["gl"]), (1, LANE))
            return carry

        lax.fori_loop(0, cpb, chunk, 0)

    row = pl.BlockSpec((tb, DN_W), lambda i: (i, 0))
    sq = pl.BlockSpec((DN_H, tb, CH), lambda i: (0, i, 0))
    return pl.pallas_call(
        body, name="dn_local", grid=(nblk,),
        in_specs=[row, row, row, pl.BlockSpec((tb, LANE), lambda i: (i, 0)),
                  pl.BlockSpec((cpb, 2 * DN_H, CH), lambda i: (i, 0, 0))],
        out_specs=[row, row, row, row, sq, sq, pl.BlockSpec((cpb, DN_H, LANE), lambda i: (i, 0, 0))],
        out_shape=[_sds((s, DN_W)), _sds((s, DN_W), BF16), _sds((s, DN_W), BF16), _sds((s, DN_W), BF16),
                   _sds((DN_H, s, CH)), _sds((DN_H, s, CH)), _sds((s // CH, DN_H, LANE))],
        compiler_params=_params(("parallel",)),
    )(q, k, v, gb, gbt)


def _scan_cpb(s):
    return 8 if (s // CH) % 8 == 0 else 1


def _dn_scan(u, w, qg, kd, ai, egl, proj, norm_g):
    s = u.shape[0]
    cpb = _scan_cpb(s)
    tb = cpb * CH
    nblk = s // tb

    def body(u_ref, w_ref, qg_ref, kd_ref, ai_ref, egl_ref, z_ref, ng_ref, o_ref, vn_ref, st_ref, ob_ref, state):
        @pl.when(pl.program_id(0) == 0)
        def _():
            state[...] = jnp.zeros_like(state)

        ng = ng_ref[...]

        def chunk(c, carry):
            r0 = pl.multiple_of(c * CH, CH)
            rows = pl.ds(r0, CH)
            for h in range(DN_H):
                cs = slice(h * HEAD, (h + 1) * HEAD)
                st = state[h]
                st_ref[c, h] = st
                stb = st.astype(BF16)
                vn = u_ref[rows, cs] - jnp.dot(w_ref[rows, cs], stb, preferred_element_type=F32)
                vn_ref[rows, cs] = vn
                vnb = vn.astype(BF16)
                o = jnp.dot(qg_ref[rows, cs], stb, preferred_element_type=F32) + _mm(ai_ref[h, rows, :], vnb)
                state[h] = st * egl_ref[c, h:h + 1, :] + _mm_tn(kd_ref[rows, cs], vnb)
                o_ref[rows, cs] = o
                r = lax.rsqrt(jnp.mean(o * o, axis=-1, keepdims=True) + EPS)
                ob_ref[rows, cs] = (o * r * ng * _silu(z_ref[rows, cs])).astype(BF16)
            return carry

        lax.fori_loop(0, cpb, chunk, 0)

    row = pl.BlockSpec((tb, DN_W), lambda i: (i, 0))
    return pl.pallas_call(
        body, name="dn_scan", grid=(nblk,),
        in_specs=[row, row, row, row, pl.BlockSpec((DN_H, tb, CH), lambda i: (0, i, 0)),
                  pl.BlockSpec((cpb, DN_H, LANE), lambda i: (i, 0, 0)), pl.BlockSpec((tb, DN_W), lambda i: (i, 3)),
                  pl.BlockSpec((1, HEAD), lambda i: (0, 0))],
        out_specs=[row, row, pl.BlockSpec((cpb, DN_H, HEAD, HEAD), lambda i: (i, 0, 0, 0)), row],
        out_shape=[_sds((s, DN_W)), _sds((s, DN_W)), _sds((s // CH, DN_H, HEAD, HEAD)), _sds((s, DN_W), BF16)],
        scratch_shapes=[pltpu.VMEM((DN_H, HEAD, HEAD), F32)],
        compiler_params=_params(("arbitrary",)),
    )(u, w, qg, kd, ai, egl, proj, norm_g)


def _dn_scan_bwd(dmixed, o, proj, norm_g, w, qg, kd, ai, egl):
    s = o.shape[0]
    cpb = _scan_cpb(s)
    tb = cpb * CH
    nblk = s // tb

    def body(dm_ref, o_ref, z_ref, ng_ref, w_ref, qg_ref, kd_ref, ai_ref, egl_ref,
             do_ref, dvn_ref, dst_ref, dz_ref, dng_ref, dstate):
        @pl.when(pl.program_id(0) == 0)
        def _():
            dstate[...] = jnp.zeros_like(dstate)
            dng_ref[...] = jnp.zeros_like(dng_ref)

        ng = ng_ref[...]

        def chunk(cc, carry):
            c = cpb - 1 - cc
            r0 = pl.multiple_of(c * CH, CH)
            rows = pl.ds(r0, CH)
            dng = jnp.zeros((1, HEAD), F32)
            for h in range(DN_H):
                cs = slice(h * HEAD, (h + 1) * HEAD)
                o = o_ref[rows, cs]
                z = z_ref[rows, cs]
                db = dm_ref[rows, cs]
                r = lax.rsqrt(jnp.mean(o * o, axis=-1, keepdims=True) + EPS)
                ohat = o * r
                dz_ref[rows, cs] = (db * ohat * ng * _silu_grad(z)).astype(BF16)
                dyn = db * _silu(z)
                dng += _colsum(dyn * ohat)
                doh = dyn * ng
                do = r * (doh - ohat * jnp.mean(doh * ohat, axis=-1, keepdims=True))
                do_ref[rows, cs] = do
                dsn = dstate[h]
                dst_ref[c, h] = dsn
                dob = do.astype(BF16)
                dvn = _mm_tn(ai_ref[h, rows, :], dob) + jnp.dot(kd_ref[rows, cs], dsn.astype(BF16),
                                                                 preferred_element_type=F32)
                dvn_ref[rows, cs] = dvn
                dstate[h] = (_mm_tn(qg_ref[rows, cs], dob) + egl_ref[c, h:h + 1, :] * dsn
                             - _mm_tn(w_ref[rows, cs], dvn))
            dng_ref[...] += dng
            return carry

        lax.fori_loop(0, cpb, chunk, 0)

    rev = lambda i: (nblk - 1 - i, 0)
    row = pl.BlockSpec((tb, DN_W), rev)
    vec = pl.BlockSpec((1, HEAD), lambda i: (0, 0))
    return pl.pallas_call(
        body, name="dn_scan_bwd", grid=(nblk,),
        in_specs=[row, row, pl.BlockSpec((tb, DN_W), lambda i: (nblk - 1 - i, 3)), vec, row, row, row,
                  pl.BlockSpec((DN_H, tb, CH), lambda i: (0, nblk - 1 - i, 0)),
                  pl.BlockSpec((cpb, DN_H, LANE), lambda i: (nblk - 1 - i, 0, 0))],
        out_specs=[row, row, pl.BlockSpec((cpb, DN_H, HEAD, HEAD), lambda i: (nblk - 1 - i, 0, 0, 0)), row, vec],
        out_shape=[_sds((s, DN_W)), _sds((s, DN_W)), _sds((s // CH, DN_H, HEAD, HEAD)), _sds((s, DN_W), BF16),
                   _sds((1, HEAD))],
        scratch_shapes=[pltpu.VMEM((DN_H, HEAD, HEAD), F32)],
        compiler_params=_params(("arbitrary",)),
    )(dmixed, o, proj, norm_g, w, qg, kd, ai, egl)


def _dn_local_bwd(q, k, v, gb, gbt, t, vn, st, dst, do, dvn):
    s = q.shape[0]
    cpb = 4 if (s // CH) % 4 == 0 else 1
    tb = cpb * CH
    nblk = s // tb

    def body(q_ref, k_ref, v_ref, gb_ref, gbt_ref, t_ref, vn_ref, st_ref, dst_ref, do_ref, dvn_ref,
             dq_ref, dk_ref, dv_ref, dgb_ref):
        lane = _iota2((CH, LANE), 1)
        last = _iota2((CH, 1), 0) == CH - 1

        def chunk(c, carry):
            r0 = pl.multiple_of(c * CH, CH)
            rows = pl.ds(r0, CH)
            gbv = gb_ref[rows, :]
            gbt_v = gbt_ref[c]
            dgb = jnp.zeros((CH, LANE), F32)
            for h in range(DN_H):
                cs = slice(h * HEAD, (h + 1) * HEAD)
                qv = q_ref[rows, cs]
                kv = k_ref[rows, cs]
                vv = v_ref[rows, cs]
                m = _chunk_common(qv, kv, vv, gbv, gbt_v, h)
                dec, eg, ekd, kb, kbe, beta = m["dec"], m["eg"], m["ekd"], m["kb"], m["kbe"], m["beta"]
                qg = qv * eg
                kd = kv * ekd
                stv = st_ref[c, h]
                dsn = dst_ref[c, h]
                stb = stv.astype(BF16)
                dob = do_ref[rows, cs].astype(BF16)
                dvnb = dvn_ref[rows, cs].astype(BF16)
                vnb = vn_ref[rows, cs].astype(BF16)
                tb_ = t_ref[h, rows, :].astype(BF16)
                dqg = _mm_nt(dob, stb)
                dai = _mm_nt(dob, vnb)
                dkd = _mm_nt(vnb, dsn)
                dgl = jnp.exp(m["gl"]) * _colsum(_rowsum(stv * dsn)) + _colsum(_rowsum(dkd * kd))
                dw = -_mm_nt(dvnb, stb)
                dt = _mm_nt(dvnb, m["vb"]) + _mm_nt(dw, kbe)
                dvb = _mm_tn(tb_, dvnb)
                dkbe = _mm_tn(tb_, dw)
                da = jnp.where(m["strict"], -_mm_nt(_mm_tn(tb_, dt), tb_), 0.0)
                dkk = da * dec
                dqk = dai * dec
                dkb = _mm(dkk, kv) + dkbe * eg
                mm_ = da * m["a"] + dai * m["ai"]
                dq_ref[rows, cs] = _mm(dqk, kv) + dqg * eg
                dk_ref[rows, cs] = _mm_tn(dkk, kb) + _mm_tn(dqk, qv) + dkd * ekd + dkb * beta
                dv_ref[rows, cs] = dvb * beta
                dgc = (_rowsum(mm_) - _rowsum(mm_.T) + _rowsum(dqg * qg) - _rowsum(dkd * kd) + _rowsum(dkbe * kbe)
                       + jnp.where(last, dgl, 0.0))
                dbeta = _rowsum(dkb * kv) + _rowsum(dvb * vv)
                dgb = jnp.where(lane == h, dgc, jnp.where(lane == DN_H + h, dbeta, dgb))
            dgb_ref[rows, :] = dgb
            return carry

        lax.fori_loop(0, cpb, chunk, 0)

    row = pl.BlockSpec((tb, DN_W), lambda i: (i, 0))
    gbs = pl.BlockSpec((tb, LANE), lambda i: (i, 0))
    sts = pl.BlockSpec((cpb, DN_H, HEAD, HEAD), lambda i: (i, 0, 0, 0))
    return pl.pallas_call(
        body, name="dn_local_bwd", grid=(nblk,),
        in_specs=[row, row, row, gbs, pl.BlockSpec((cpb, 2 * DN_H, CH), lambda i: (i, 0, 0)),
                  pl.BlockSpec((DN_H, tb, CH), lambda i: (0, i, 0)), row, sts, sts, row, row],
        out_specs=[row, row, row, gbs],
        out_shape=[_sds((s, DN_W)), _sds((s, DN_W)), _sds((s, DN_W)), _sds((s, LANE))],
        compiler_params=_params(("parallel",)),
    )(q, k, v, gb, gbt, t, vn, st, dst, do, dvn)


def _dn_pre_bwd(proj, ab, conv_w, alog_row, dt_row, dq, dk, dv, dgb):
    s = proj.shape[0]
    tm = _tile(s, (256, 128))
    w3 = 3 * DN_W
    nblk = s // tm

    def body(x_ref, halo_ref, ab_ref, cw_ref, al_ref, dt_ref, dq_ref, dk_ref, dv_ref, dgb_ref,
             dx_ref, dab_ref, dcw_ref, dal_ref, ddt_ref, ext, exd, carry):
        i = pl.program_id(0)

        @pl.when(i == 0)
        def _():
            carry[...] = jnp.zeros_like(carry)
            dcw_ref[...] = jnp.zeros_like(dcw_ref)
            dal_ref[...] = jnp.zeros_like(dal_ref)
            ddt_ref[...] = jnp.zeros_like(ddt_ref)

        ext[0:HALO, :] = jnp.where(i < nblk - 1, halo_ref[...], 0.0)
        ext[HALO:HALO + tm, :] = x_ref[...]
        yc = cw_ref[0:1, :] * ext[pl.ds(HALO - DN_K + 1, tm), :]
        for t in range(1, DN_K):
            yc += cw_ref[t:t + 1, :] * ext[pl.ds(HALO - DN_K + 1 + t, tm), :]
        sg = jax.nn.sigmoid(yc)
        act = yc * sg
        dact = sg * (1.0 + yc * (1.0 - sg))
        for h in range(DN_H):
            cs = slice(h * HEAD, (h + 1) * HEAD)
            ks = slice(DN_W + h * HEAD, DN_W + (h + 1) * HEAD)
            qa = act[:, cs]
            rq = lax.rsqrt(_rowsum(qa * qa) + EPS)
            qh = qa * rq
            dqv = dq_ref[:, cs]
            exd[0:tm, cs] = (HEAD ** -0.5) * rq * (dqv - qh * _rowsum(dqv * qh)) * dact[:, cs]
            ka = act[:, ks]
            rk = lax.rsqrt(_rowsum(ka * ka) + EPS)
            kh = ka * rk
            dkv = dk_ref[:, cs]
            exd[0:tm, ks] = rk * (dkv - kh * _rowsum(dkv * kh)) * dact[:, ks]
        exd[0:tm, 2 * DN_W:w3] = dv_ref[...] * dact[:, 2 * DN_W:w3]
        exd[tm:tm + HALO, :] = carry[...]
        dyc = exd[0:tm, :]
        dx = cw_ref[0:1, :] * exd[pl.ds(DN_K - 1, tm), :]
        dcw_ref[0:1, :] += _colsum(dyc * ext[pl.ds(HALO - DN_K + 1, tm), :])
        for t in range(1, DN_K):
            dx += cw_ref[t:t + 1, :] * exd[pl.ds(DN_K - 1 - t, tm), :]
            dcw_ref[t:t + 1, :] += _colsum(dyc * ext[pl.ds(HALO - DN_K + 1 + t, tm), :])
        dx_ref[...] = dx.astype(BF16)
        carry[...] = exd[0:HALO, :]

        lane = _iota2((tm, LANE), 1)
        dgbv = dgb_ref[...]
        dg = _mm_hi(_chunk_tri(tm, True), jnp.where(lane < DN_H, dgbv, 0.0))
        abv = ab_ref[...]
        xa = abv + dt_ref[...]
        nea = -jnp.exp(al_ref[...])
        d_da = jnp.where(lane < DN_H, dg * nea * jax.nn.sigmoid(xa), 0.0)
        dal_ref[...] += _colsum(jnp.where(lane < DN_H, dg * nea * _softplus(xa), 0.0))
        ddt_ref[...] += _colsum(d_da)
        beta = jax.nn.sigmoid(abv)
        d_db = jnp.where((lane >= DN_H) & (lane < 2 * DN_H), dgbv * beta * (1.0 - beta), 0.0)
        dab_ref[...] = (d_da + d_db).astype(BF16)

    hb = tm // HALO
    rev = lambda i: (nblk - 1 - i, 0)
    row = lambda w: pl.BlockSpec((tm, w), rev)
    vec = pl.BlockSpec((1, LANE), lambda i: (0, 0))
    cws = pl.BlockSpec((DN_K, w3), lambda i: (0, 0))
    return pl.pallas_call(
        body, name="dn_pre_bwd", grid=(nblk,),
        in_specs=[row(w3), pl.BlockSpec((HALO, w3), lambda i: (jnp.maximum((nblk - 1 - i) * hb - 1, 0), 0)), row(LANE),
                  cws, vec, vec, row(DN_W), row(DN_W), row(DN_W), row(LANE)],
        out_specs=[row(w3), row(LANE), cws, vec, vec],
        out_shape=[_sds((s, w3), BF16), _sds((s, LANE), BF16), _sds((DN_K, w3)), _sds((1, LANE)), _sds((1, LANE))],
        scratch_shapes=[pltpu.VMEM((tm + HALO, w3), F32), pltpu.VMEM((tm + HALO, w3), F32), pltpu.VMEM((HALO, w3), F32)],
        compiler_params=_params(("arbitrary",)),
    )(proj, proj, ab, conv_w, alog_row, dt_row, dq, dk, dv, dgb)


def _adam(parts, w, m, v, name):
    r, c = w.shape
    tr = _tile(r, (128, 64, 32, 16, 8))

    def body(p_ref, w_ref, m_ref, v_ref, g_ref, d_ref, nm_ref, nv_ref):
        g = p_ref[0]
        for k in range(1, N_DEV):
            g = g + p_ref[k]
        g_ref[...] = g
        mn = ADAM_B1 * m_ref[...] + (1.0 - ADAM_B1) * g
        vn = ADAM_B2 * v_ref[...] + (1.0 - ADAM_B2) * (g * g)
        m_hat = mn / (1.0 - ADAM_B1 ** ADAM_STEP)
        v_hat = vn / (1.0 - ADAM_B2 ** ADAM_STEP)
        d_ref[...] = -ADAM_LR * (m_hat / (jnp.sqrt(v_hat) + ADAM_EPS) + ADAM_WD * w_ref[...])
        nm_ref[...] = mn
        nv_ref[...] = vn

    blk = pl.BlockSpec((tr, c), lambda i: (i, 0))
    return pl.pallas_call(
        body, name=name, grid=(r // tr,),
        in_specs=[pl.BlockSpec((N_DEV, tr, c), lambda i: (0, i, 0)), blk, blk, blk],
        out_specs=[blk, blk, blk, blk], out_shape=[_sds((r, c))] * 4,
        compiler_params=_params(("parallel",)),
    )(parts, w, m, v)


_PACK_ROWS = 8


def _pack(vals):
    tiles = []
    for a in vals:
        flat = a.reshape(-1).astype(F32)
        unit = _PACK_ROWS * LANE
        n = -(-flat.shape[0] // unit) * unit
        tiles.append(jnp.pad(flat, (0, n - flat.shape[0])).reshape(n // LANE, LANE))
    return jnp.concatenate(tiles, axis=0)


def _unpack(packed, shapes):
    out = []
    r0 = 0
    for shp in shapes:
        size = 1
        for dim in shp:
            size *= dim
        unit = _PACK_ROWS * LANE
        rows = -(-size // unit) * _PACK_ROWS
        out.append(packed[r0:r0 + rows].reshape(-1)[:size].reshape(shp))
        r0 += rows
    return out


def _lane_row(vec8):
    return jnp.pad(vec8.reshape(1, -1).astype(F32), ((0, 0), (0, LANE - vec8.size)))


def kernel(x, mem, ln_g, w_in, gmlp_ln_g, gmlp_ln_b, gmlp_ws, gmlp_bs, conv_w, dn_a_log, dn_dt_bias, dn_norm_g, mem_norm_g, w_mem_kv, w_out, final_g, loss_target, m_ln_g, m_w_in, m_gmlp_ln_g, m_gmlp_ln_b, m_gmlp_ws, m_gmlp_bs, m_conv_w, m_dn_a_log, m_dn_dt_bias, m_dn_norm_g, m_mem_norm_g, m_w_mem_kv, m_w_out, m_final_g, v_ln_g, v_w_in, v_gmlp_ln_g, v_gmlp_ln_b, v_gmlp_ws, v_gmlp_bs, v_conv_w, v_dn_a_log, v_dn_dt_bias, v_dn_norm_g, v_mem_norm_g, v_w_mem_kv, v_w_out, v_final_g):
    xs = x[0]
    mems = mem[0]
    tgt = loss_target[0]
    s, d = xs.shape
    shard_w = w_in.shape[2]
    in_w = N_DEV * shard_w
    me = 4 * lax.axis_index("x") + 2 * lax.axis_index("y") + lax.axis_index("c")

    g_in, g_out, g_kv, g_conv = _exchange(
        [w_in[0].astype(BF16), w_out[0].astype(BF16), w_mem_kv[0].astype(BF16), conv_w[0]], [], "gather_weights")
    w_full = g_in.transpose(1, 0, 2).reshape(d, in_w)
    o_g, o_dn, o_ab = 0, 3 * GMLP_W, 3 * GMLP_W + 4 * DN_W
    o_xa = o_ab + 2 * DN_H
    w_qkv = w_full[:, o_dn:o_dn + 3 * DN_W]
    w_dz = w_full[:, o_dn + 3 * DN_W:o_ab]
    w_gm = w_full[:, o_g:o_dn]
    w_xa = w_full[:, o_xa:in_w]
    w_ab = jnp.pad(w_full[:, o_ab:o_xa], ((0, 0), (0, LANE - 2 * DN_H)))
    w_main = jnp.concatenate([w_qkv, w_dz, w_gm, w_xa], axis=1)
    wo = g_out.reshape(MIX_W, d)
    wo_perm = jnp.concatenate([wo[GMLP_W:GMLP_W + DN_W], wo[0:GMLP_W], wo[GMLP_W + DN_W:MIX_W]], axis=0)
    w_kv = g_kv.reshape(d, 2 * XA_W)
    conv_full = g_conv.transpose(1, 0, 2).reshape(DN_K, 3 * DN_W)

    ln_g2 = ln_g.reshape(1, d)
    lng2 = gmlp_ln_g.reshape(1, GMLP_W)
    lnb2 = gmlp_ln_b.reshape(1, GMLP_W)
    ws3 = gmlp_ws[0]
    bs_t = gmlp_bs[0].T
    alog_row = _lane_row(dn_a_log)
    dt_row = _lane_row(dn_dt_bias)
    dn_g2 = dn_norm_g.reshape(1, HEAD)
    mem_g2 = mem_norm_g.reshape(1, d)
    fin_g2 = final_g.reshape(1, d)

    proj, ab, h = _inproj(xs, ln_g2, w_main, w_ab)
    out_a = _gmlp_fwd(proj, lng2, lnb2, ws3, bs_t)
    mkv = _memkv_fwd(mems, mem_g2, w_kv)
    out_c = _xattn_fwd(proj, mkv)
    q, k, v, gb, gbt = _dn_pre(proj, ab, conv_full, alog_row, dt_row)
    u, wk, qg, kd, tmat, ai, egl = _dn_local(q, k, v, gb, gbt)
    o, vn, st, out_b = _dn_scan(u, wk, qg, kd, ai, egl, proj, dn_g2)

    dx2, dx2b, dmixed, loss_acc, d_fin_g = _final(xs, tgt, out_b, out_a, out_c, wo_perm, fin_g2)
    loss = lax.psum(loss_acc[0, 0], ("x", "y", "c"))

    dwo_b = _matmul_tn(out_b, dx2b, "dw_out_b")
    dwo_a = _matmul_tn(out_a, dx2b, "dw_out_a")
    dwo_c = _matmul_tn(out_c, dx2b, "dw_out_c")
    d_w_out = jnp.concatenate([dwo_a, dwo_b, dwo_c], axis=0)

    dp_g, d_ws, d_bst, d_lng, d_lnb = _gmlp_bwd(proj, dmixed, lng2, lnb2, ws3, bs_t)
    dp_x, dmkv = _xattn_bwd(proj, dmixed, mkv)
    d_w_kv, d_mem_g = _memkv_bwd(mems, mem_g2, w_kv, dmkv)
    do, dvn, dst, dp_dz, d_dn_g = _dn_scan_bwd(dmixed, o, proj, dn_g2, wk, qg, kd, ai, egl)
    dq, dk, dv, dgb = _dn_local_bwd(q, k, v, gb, gbt, tmat, vn, st, dst, do, dvn)
    dp_qkv, dp_ab, d_conv, d_alog, d_dt = _dn_pre_bwd(proj, ab, conv_full, alog_row, dt_row, dq, dk, dv, dgb)

    pieces = [dp_qkv, dp_dz, dp_g, dp_x, dp_ab]
    dh = _dh(pieces, [w_qkv, w_dz, w_gm, w_xa, w_ab])
    grad_x, d_ln_g = _rms_bwd(xs, dh, dx2, ln_g2)
    dw_qkv = _matmul_tn(h, dp_qkv, "dw_in_qkv")
    dw_dz = _matmul_tn(h, dp_dz, "dw_in_dz")
    dw_gm = _matmul_tn(h, dp_g, "dw_in_gmlp")
    dw_xa = _matmul_tn(h, dp_x, "dw_in_xa")
    dw_ab = _matmul_tn(h, dp_ab, "dw_in_ab")
    d_w_in = jnp.concatenate([dw_gm, dw_qkv, dw_dz, dw_ab[:, :2 * DN_H], dw_xa], axis=1)

    small_shapes = [ln_g.shape, gmlp_ln_g.shape, gmlp_ln_b.shape, gmlp_ws.shape, gmlp_bs.shape, dn_a_log.shape,
                    dn_dt_bias.shape, dn_norm_g.shape, mem_norm_g.shape, final_g.shape, (DN_K, 3 * DN_W)]
    small_g = _pack([d_ln_g, d_lng, d_lnb, d_ws, d_bst.T, d_alog[:, :DN_H], d_dt[:, :DN_H], d_dn_g, d_mem_g, d_fin_g,
                     d_conv])
    zc = jnp.zeros((DN_K, 3 * DN_W), F32)
    small_w = _pack([ln_g, gmlp_ln_g, gmlp_ln_b, gmlp_ws, gmlp_bs, dn_a_log, dn_dt_bias, dn_norm_g, mem_norm_g, final_g, zc])
    small_m = _pack([m_ln_g, m_gmlp_ln_g, m_gmlp_ln_b, m_gmlp_ws, m_gmlp_bs, m_dn_a_log, m_dn_dt_bias, m_dn_norm_g,
                     m_mem_norm_g, m_final_g, zc])
    small_v = _pack([v_ln_g, v_gmlp_ln_g, v_gmlp_ln_b, v_gmlp_ws, v_gmlp_bs, v_dn_a_log, v_dn_dt_bias, v_dn_norm_g,
                     v_mem_norm_g, v_final_g, zc + 1.0])

    send_in = d_w_in.reshape(d, N_DEV, shard_w).transpose(1, 0, 2)
    send_out = d_w_out.reshape(N_DEV, MIX_W // N_DEV, d)
    send_kv = d_w_kv.reshape(N_DEV, d // N_DEV, 2 * XA_W)
    all_small, r_in, r_out, r_kv = _exchange([small_g], [send_in, send_out, send_kv], "exchange_grads")

    g_w_in, dl_w_in, nm_w_in, nv_w_in = _adam(r_in, w_in[0], m_w_in[0], v_w_in[0], "adam_w_in")
    g_w_out, dl_w_out, nm_w_out, nv_w_out = _adam(r_out, w_out[0], m_w_out[0], v_w_out[0], "adam_w_out")
    g_w_kv, dl_w_kv, nm_w_kv, nv_w_kv = _adam(r_kv, w_mem_kv[0], m_w_mem_kv[0], v_w_mem_kv[0], "adam_w_kv")
    sm = [_unpack(t, small_shapes) for t in _adam(all_small, small_w, small_m, small_v, "adam_small")]

    conv_parts = lax.dynamic_slice(all_small, (0, all_small.shape[1] - (DN_K * 3 * DN_W) // LANE, 0),
                                   (N_DEV, (DN_K * 3 * DN_W) // LANE, LANE)).reshape(N_DEV, DN_K, 3 * DN_W)
    cshard = conv_w.shape[2]
    conv_parts = lax.dynamic_slice(conv_parts, (0, 0, me * cshard), (N_DEV, DN_K, cshard))
    cpad = ((0, 0), (0, HALO - DN_K), (0, 0))
    conv_res = _adam(jnp.pad(conv_parts, cpad), jnp.pad(conv_w[0], cpad[1:]), jnp.pad(m_conv_w[0], cpad[1:]),
                     jnp.pad(v_conv_w[0], cpad[1:], constant_values=1.0), "adam_conv")
    g_conv_s, dl_conv, nm_conv, nv_conv = [t[:DN_K][None] for t in conv_res]

    def group(idx, big_in, big_conv, big_kv, big_out):
        names = sm[idx]
        return [names[0], big_in[None], names[1], names[2], names[3], names[4], big_conv, names[5], names[6], names[7],
                names[8], big_kv[None], big_out[None], names[9]]

    grads = group(0, g_w_in, g_conv_s, g_w_kv, g_w_out)
    deltas = group(1, dl_w_in, dl_conv, dl_w_kv, dl_w_out)
    new_m = group(2, nm_w_in, nm_conv, nm_w_kv, nm_w_out)
    new_v = group(3, nv_w_in, nv_conv, nv_w_kv, nv_w_out)
    return (loss, grad_x[None], *grads, *deltas, *new_m, *new_v)
```

```python
import functools

import jax
import jax.numpy as jnp
from jax import lax
from jax.experimental import pallas as pl
from jax.experimental.pallas import tpu as pltpu

F32 = jnp.float32
BF16 = jnp.bfloat16
HIGHEST = lax.Precision.HIGHEST
MESH_ID = pl.DeviceIdType.MESH

N_DEV = 8
EPS = 1e-6
GMLP_W = 512
GMLP_G = 4
GMLP_T = 128
DN_W = 1024
DN_H = 8
HEAD = 128
DN_K = 4
CH = 64
XA_W = 512
XA_H = 4
LANE = 128
HALO = 8
MAIN_W = 4 * DN_W + 3 * GMLP_W + 2 * XA_W
MIX_W = DN_W + GMLP_W + XA_W
VMEM_LIMIT = 56 * 1024 * 1024

ADAM_LR = 0.001
ADAM_B1 = 0.9
ADAM_B2 = 0.999
ADAM_EPS = 1e-08
ADAM_WD = 0.01
ADAM_STEP = 10


def _sds(shape, dtype=F32):
    return jax.ShapeDtypeStruct(tuple(shape), dtype)


def _params(sem=None):
    if sem is None:
        return pltpu.CompilerParams(vmem_limit_bytes=VMEM_LIMIT)
    return pltpu.CompilerParams(dimension_semantics=tuple(sem), vmem_limit_bytes=VMEM_LIMIT)


def _tile(n, prefs):
    for p in prefs:
        if n % p == 0:
            return p
    return n


def _mm(a, b):
    return jnp.dot(a.astype(BF16), b.astype(BF16), preferred_element_type=F32)


def _mm_nt(a, b):
    return lax.dot_general(a.astype(BF16), b.astype(BF16), (((1,), (1,)), ((), ())), preferred_element_type=F32)


def _mm_tn(a, b):
    return lax.dot_general(a.astype(BF16), b.astype(BF16), (((0,), (0,)), ((), ())), preferred_element_type=F32)


def _mm_hi(a, b):
    return jnp.dot(a, b, precision=HIGHEST, preferred_element_type=F32)


def _mm_3x(a, b):
    return jnp.dot(a, b, precision=lax.Precision.HIGH, preferred_element_type=F32)


_GELU_C = 0.7978845608028654
_GELU_A = 0.044715


def _gelu(x):
    return 0.5 * x * (1.0 + jnp.tanh(_GELU_C * (x + _GELU_A * x * x * x)))


def _gelu_grad(x):
    t = jnp.tanh(_GELU_C * (x + _GELU_A * x * x * x))
    return 0.5 * (1.0 + t) + 0.5 * x * (1.0 - t * t) * _GELU_C * (1.0 + 3.0 * _GELU_A * x * x)


def _silu(x):
    return x * jax.nn.sigmoid(x)


def _silu_grad(x):
    s = jax.nn.sigmoid(x)
    return s * (1.0 + x * (1.0 - s))


def _rowsum(x):
    return jnp.sum(x, axis=-1, keepdims=True)


def _colsum(x):
    return jnp.sum(x, axis=0, keepdims=True)


def _iota2(shape, dim):
    return lax.broadcasted_iota(jnp.int32, shape, dim)


def _chunk_tri(tm, upper):
    r = _iota2((tm, tm), 0)
    c = _iota2((tm, tm), 1)
    same = lax.shift_right_logical(r, 6) == lax.shift_right_logical(c, 6)
    tri = (r <= c) if upper else (r >= c)
    return jnp.where(same & tri, 1.0, 0.0).astype(F32)


N_CHIP = 4


def _mesh_place():
    x, y, c = lax.axis_index("x"), lax.axis_index("y"), lax.axis_index("c")
    chips = [(1 - x, y), (x, 1 - y), (1 - x, 1 - y)]
    return x, y, c, (x, y, 1 - c), chips


def _gather_two_level(arrs, name):
    n = len(arrs)

    def body(*refs):
        ins = refs[:n]
        outs = refs[n:2 * n]
        send_sems, recv_sems, loc_sems = refs[2 * n:]
        x, y, c, sib, chips = _mesh_place()

        def copy(a, k, block, to, src=None):
            slot = outs[a].at[4 * block[0] + 2 * block[1] + block[2]]
            return pltpu.make_async_remote_copy(
                src_ref=slot if src is None else src, dst_ref=slot, send_sem=send_sems.at[a, k],
                recv_sem=recv_sems.at[a, k], device_id=to, device_id_type=MESH_ID)

        me = (x, y, c)
        sends, locs = [], []
        for a in range(n):
            own = pltpu.make_async_copy(ins[a], outs[a].at[4 * x + 2 * y + c], loc_sems.at[a])
            own.start()
            locs.append(own)
            first = [copy(a, 0, me, sib, src=ins[a])]
            first += [copy(a, 1 + j, me, (*chip, c), src=ins[a]) for j, chip in enumerate(chips)]
            for cp in first:
                cp.start()
            sends += first
        for a in range(n):
            for j, chip in enumerate(chips):
                copy(a, 1 + j, (*chip, c), me).wait_recv()
                passed = copy(a, 4 + j, (*chip, c), sib)
                passed.start()
                sends.append(passed)
        for a in range(n):
            copy(a, 0, sib, me).wait_recv()
            for j, chip in enumerate(chips):
                copy(a, 4 + j, (*chip, 1 - c), me).wait_recv()
        for cp in sends:
            cp.wait_send()
        for cp in locs:
            cp.wait()

    any_spec = pl.BlockSpec(memory_space=pl.ANY)
    return pl.pallas_call(
        body, name=name, out_shape=[_sds((N_DEV,) + a.shape, a.dtype) for a in arrs],
        in_specs=[any_spec] * n, out_specs=[any_spec] * n,
        scratch_shapes=[pltpu.SemaphoreType.DMA((n, N_DEV - 1)), pltpu.SemaphoreType.DMA((n, N_DEV - 1)),
                        pltpu.SemaphoreType.DMA((n,))],
        compiler_params=pltpu.CompilerParams(has_side_effects=True),
    )(*arrs)


def _swap_halves(small, grads, name):
    n = len(grads)

    def body(*refs):
        small_ref = refs[0]
        ins = refs[1:1 + n]
        small_out = refs[1 + n]
        kept = refs[2 + n:2 + 2 * n]
        got = refs[2 + 2 * n:2 + 3 * n]
        s_send, s_recv, g_send, g_recv, loc_sems = refs[2 + 3 * n:]
        x, y, c, sib, _ = _mesh_place()
        me = 4 * x + 2 * y + c
        sends, recvs, locs = [], [], []
        for j in range(1, N_DEV):
            px = 1 - x if (j >> 2) & 1 else x
            py = 1 - y if (j >> 1) & 1 else y
            pc = 1 - c if j & 1 else c
            cp = pltpu.make_async_remote_copy(
                src_ref=small_ref, dst_ref=small_out.at[me], send_sem=s_send.at[j - 1], recv_sem=s_recv.at[j - 1],
                device_id=(px, py, pc), device_id_type=MESH_ID)
            cp.start()
            sends.append(cp)
            recvs.append(pltpu.make_async_remote_copy(
                src_ref=small_ref, dst_ref=small_out.at[4 * px + 2 * py + pc], send_sem=s_send.at[j - 1],
                recv_sem=s_recv.at[j - 1], device_id=(px, py, pc), device_id_type=MESH_ID))
        own = pltpu.make_async_copy(small_ref, small_out.at[me], loc_sems.at[n, 0])
        own.start()
        locs.append(own)
        for a in range(n):
            for chip in range(N_CHIP):
                cp = pltpu.make_async_remote_copy(
                    src_ref=ins[a].at[2 * chip + 1 - c], dst_ref=got[a].at[chip], send_sem=g_send.at[a, chip],
                    recv_sem=g_recv.at[a, chip], device_id=sib, device_id_type=MESH_ID)
                cp.start()
                sends.append(cp)
                recvs.append(cp)
                keep = pltpu.make_async_copy(ins[a].at[2 * chip + c], kept[a].at[chip], loc_sems.at[a, chip])
                keep.start()
                locs.append(keep)
        for cp in sends:
            cp.wait_send()
        for cp in recvs:
            cp.wait_recv()
        for cp in locs:
            cp.wait()

    half = [_sds((N_CHIP,) + g.shape[1:], g.dtype) for g in grads]
    any_spec = pl.BlockSpec(memory_space=pl.ANY)
    res = pl.pallas_call(
        body, name=name, out_shape=[_sds((N_DEV,) + small.shape, small.dtype)] + half + half,
        in_specs=[any_spec] * (1 + n), out_specs=[any_spec] * (1 + 2 * n),
        scratch_shapes=[pltpu.SemaphoreType.DMA((N_DEV - 1,)), pltpu.SemaphoreType.DMA((N_DEV - 1,)),
                        pltpu.SemaphoreType.DMA((n, N_CHIP)), pltpu.SemaphoreType.DMA((n, N_CHIP)),
                        pltpu.SemaphoreType.DMA((n + 1, N_CHIP))],
        compiler_params=pltpu.CompilerParams(has_side_effects=True),
    )(small, *grads)
    return res[0], res[1:1 + n], res[1 + n:]


def _pair_sum(kept, got, name):
    nc, r, c = kept.shape
    tr = _tile(r, (256, 128, 64, 32, 16))

    def body(a_ref, b_ref, o_ref):
        o_ref[...] = (a_ref[...] + b_ref[...]).astype(BF16)

    blk = pl.BlockSpec((1, tr, c), lambda i, j: (i, j, 0))
    return pl.pallas_call(body, name=name, grid=(nc, r // tr), in_specs=[blk, blk], out_specs=blk,
                          out_shape=_sds(kept.shape, BF16), compiler_params=_params(("parallel", "parallel")))(kept, got)


def _chip_exchange(parts, name):
    n = len(parts)

    def body(*refs):
        ins = refs[:n]
        outs = refs[n:2 * n]
        send_sems, recv_sems, loc_sems = refs[2 * n:]
        x, y, c, _, chips = _mesh_place()
        mine = 2 * x + y
        sends, locs = [], []
        for a in range(n):
            own = pltpu.make_async_copy(ins[a].at[mine], outs[a].at[mine], loc_sems.at[a])
            own.start()
            locs.append(own)
            for j, chip in enumerate(chips):
                q = 2 * chip[0] + chip[1]
                cp = pltpu.make_async_remote_copy(
                    src_ref=ins[a].at[q], dst_ref=outs[a].at[mine], send_sem=send_sems.at[a, j], recv_sem=recv_sems.at[a, j],
                    device_id=(*chip, c), device_id_type=MESH_ID)
                cp.start()
                landed = pltpu.make_async_remote_copy(
                    src_ref=ins[a].at[q], dst_ref=outs[a].at[q], send_sem=send_sems.at[a, j], recv_sem=recv_sems.at[a, j],
                    device_id=(*chip, c), device_id_type=MESH_ID)
                sends.append((cp, landed))
        for cp, landed in sends:
            cp.wait_send()
            landed.wait_recv()
        for cp in locs:
            cp.wait()

    any_spec = pl.BlockSpec(memory_space=pl.ANY)
    return pl.pallas_call(
        body, name=name, out_shape=[_sds(p.shape, p.dtype) for p in parts],
        in_specs=[any_spec] * n, out_specs=[any_spec] * n,
        scratch_shapes=[pltpu.SemaphoreType.DMA((n, N_CHIP - 1)), pltpu.SemaphoreType.DMA((n, N_CHIP - 1)),
                        pltpu.SemaphoreType.DMA((n,))],
        compiler_params=pltpu.CompilerParams(has_side_effects=True),
    )(*parts)


def _inproj(x, ln_g, w_main, w_ab):
    s, d = x.shape
    n = w_main.shape[1]
    tm = _tile(s, (512, 256, 128))
    tn = _tile(n, (1664, 512, 128))

    def body(x_ref, g_ref, w_ref, wab_ref, proj_ref, ab_ref, h_ref, hs):
        @pl.when(pl.program_id(1) == 0)
        def _():
            xv = x_ref[...]
            r = lax.rsqrt(jnp.mean(xv * xv, axis=-1, keepdims=True) + EPS)
            h = (xv * r * g_ref[...]).astype(BF16)
            hs[...] = h
            h_ref[...] = h
            ab_ref[...] = jnp.dot(h, wab_ref[...], preferred_element_type=F32)

        proj_ref[...] = jnp.dot(hs[...], w_ref[...], preferred_element_type=F32)

    return pl.pallas_call(
        body, name="inproj", grid=(s // tm, n // tn),
        in_specs=[pl.BlockSpec((tm, d), lambda i, j: (i, 0)), pl.BlockSpec((1, d), lambda i, j: (0, 0)),
                  pl.BlockSpec((d, tn), lambda i, j: (0, j)), pl.BlockSpec((d, LANE), lambda i, j: (0, 0))],
        out_specs=[pl.BlockSpec((tm, tn), lambda i, j: (i, j)), pl.BlockSpec((tm, LANE), lambda i, j: (i, 0)),
                   pl.BlockSpec((tm, d), lambda i, j: (i, 0))],
        out_shape=[_sds((s, n)), _sds((s, LANE)), _sds((s, d), BF16)],
        scratch_shapes=[pltpu.VMEM((tm, d), BF16)],
        compiler_params=_params(("parallel", "arbitrary")),
    )(x, ln_g, w_main, w_ab)


def _matmul_tn(a, b, name):
    k, m = a.shape
    n = b.shape[1]
    tm = _tile(m, (1024, 512, 256, 128))
    tn = _tile(n, (512, 256, 128))
    tk = _tile(k, (1024, 512, 256, 128))

    def body(a_ref, b_ref, o_ref):
        @pl.when(pl.program_id(2) == 0)
        def _():
            o_ref[...] = jnp.zeros_like(o_ref)

        o_ref[...] += _mm_tn(a_ref[...], b_ref[...])

    return pl.pallas_call(
        body, name=name, grid=(m // tm, n // tn, k // tk),
        in_specs=[pl.BlockSpec((tk, tm), lambda i, j, l: (l, i)), pl.BlockSpec((tk, tn), lambda i, j, l: (l, j))],
        out_specs=pl.BlockSpec((tm, tn), lambda i, j, l: (i, j)),
        out_shape=_sds((m, n)),
        compiler_params=_params(("parallel", "parallel", "arbitrary")),
    )(a, b)


def _dh(pieces, weights):
    s = pieces[0].shape[0]
    d = weights[0].shape[0]
    npc = len(pieces)
    tm = _tile(s, (256, 128))
    tn = _tile(d, (1024, 512, 256, 128))

    def body(*refs):
        o_ref = refs[2 * npc]
        acc = _mm_nt(refs[0][...], refs[npc][...])
        for p in range(1, npc):
            acc += _mm_nt(refs[p][...], refs[npc + p][...])
        o_ref[...] = acc

    in_specs = [pl.BlockSpec((tm, p.shape[1]), lambda j, i: (i, 0)) for p in pieces]
    in_specs += [pl.BlockSpec((tn, w.shape[1]), lambda j, i: (j, 0)) for w in weights]
    return pl.pallas_call(
        body, name="dh", grid=(d // tn, s // tm), in_specs=in_specs,
        out_specs=pl.BlockSpec((tm, tn), lambda j, i: (i, j)), out_shape=_sds((s, d)),
        compiler_params=_params(("parallel", "parallel")),
    )(*pieces, *weights)


def _rms_bwd(x, dh, dx2, ln_g):
    s, d = x.shape
    tm = _tile(s, (256, 128))

    def body(x_ref, dh_ref, dx2_ref, g_ref, gx_ref, dg_ref):
        @pl.when(pl.program_id(0) == 0)
        def _():
            dg_ref[...] = jnp.zeros_like(dg_ref)

        xv = x_ref[...]
        r = lax.rsqrt(jnp.mean(xv * xv, axis=-1, keepdims=True) + EPS)
        xhat = xv * r
        dhv = dh_ref[...]
        dg_ref[...] += _colsum(dhv * xhat)
        dxh = dhv * g_ref[...]
        gx_ref[...] = dx2_ref[...] + r * (dxh - xhat * jnp.mean(dxh * xhat, axis=-1, keepdims=True))

    row = pl.BlockSpec((tm, d), lambda i: (i, 0))
    vec = pl.BlockSpec((1, d), lambda i: (0, 0))
    return pl.pallas_call(
        body, name="rms_bwd", grid=(s // tm,), in_specs=[row, row, row, vec], out_specs=[row, vec],
        out_shape=[_sds((s, d)), _sds((1, d))], compiler_params=_params(("arbitrary",)),
    )(x, dh, dx2, ln_g)


def _final(x, tgt, out_b, out_a, out_c, w_out, final_g):
    s, d = x.shape
    tm = _tile(s, (256, 128))

    def body(x_ref, t_ref, b_ref, a_ref, c_ref, w_ref, g_ref, dx2_ref, dx2b_ref, dm_ref, loss_ref, dg_ref):
        @pl.when(pl.program_id(0) == 0)
        def _():
            loss_ref[...] = jnp.zeros_like(loss_ref)
            dg_ref[...] = jnp.zeros_like(dg_ref)

        x2 = x_ref[...]
        x2 += jnp.dot(b_ref[...], w_ref[0:DN_W, :], preferred_element_type=F32)
        x2 += jnp.dot(a_ref[...], w_ref[DN_W:DN_W + GMLP_W, :], preferred_element_type=F32)
        x2 += jnp.dot(c_ref[...], w_ref[DN_W + GMLP_W:MIX_W, :], preferred_element_type=F32)
        r = lax.rsqrt(jnp.mean(x2 * x2, axis=-1, keepdims=True) + EPS)
        xhat = x2 * r
        g = g_ref[...]
        err = xhat * g - t_ref[...]
        tok = 0.5 * jnp.mean(err * err, axis=-1, keepdims=True)
        loss_ref[...] += jnp.broadcast_to(_colsum(tok), loss_ref.shape)
        dy = err * (1.0 / d)
        dg_ref[...] += _colsum(dy * xhat)
        dxh = dy * g
        dx2 = r * (dxh - xhat * jnp.mean(dxh * xhat, axis=-1, keepdims=True))
        dx2_ref[...] = dx2
        dx2b = dx2.astype(BF16)
        dx2b_ref[...] = dx2b
        dm_ref[...] = _mm_nt(dx2b, w_ref[...])

    row = pl.BlockSpec((tm, d), lambda i: (i, 0))
    vec = pl.BlockSpec((1, d), lambda i: (0, 0))
    return pl.pallas_call(
        body, name="final", grid=(s // tm,),
        in_specs=[row, row, pl.BlockSpec((tm, DN_W), lambda i: (i, 0)), pl.BlockSpec((tm, GMLP_W), lambda i: (i, 0)),
                  pl.BlockSpec((tm, XA_W), lambda i: (i, 0)), pl.BlockSpec((MIX_W, d), lambda i: (0, 0)), vec],
        out_specs=[row, row, pl.BlockSpec((tm, MIX_W), lambda i: (i, 0)), pl.BlockSpec((1, LANE), lambda i: (0, 0)), vec],
        out_shape=[_sds((s, d)), _sds((s, d), BF16), _sds((s, MIX_W)), _sds((1, LANE)), _sds((1, d))],
        compiler_params=_params(("arbitrary",)),
    )(x, tgt, out_b, out_a, out_c, w_out, final_g)


GU_BLK = (4 * DN_W) // GMLP_W


def _gmlp_norm(gv, lng, lnb):
    va = _gelu(gv)
    mu = jnp.mean(va, axis=-1, keepdims=True)
    xc = va - mu
    rstd = lax.rsqrt(jnp.mean(xc * xc, axis=-1, keepdims=True) + EPS)
    vhat = xc * rstd
    return vhat, rstd, vhat * lng + lnb


def _gmlp_fwd(proj, lng, lnb, ws, bs_t):
    s = proj.shape[0]
    tm = _tile(s, (512, 256, 128))

    def body(u_ref, v_ref, z_ref, lng_ref, lnb_ref, ws_ref, bst_ref, o_ref):
        _, _, vn = _gmlp_norm(v_ref[...], lng_ref[...], lnb_ref[...])
        tri = _iota2((GMLP_T, GMLP_T), 0) >= _iota2((GMLP_T, GMLP_T), 1)
        for g in range(GMLP_G):
            cs = slice(g * HEAD, (g + 1) * HEAD)
            w = jnp.where(tri, ws_ref[g], 0.0).astype(BF16)
            b = bst_ref[:, g:g + 1]
            for c in range(tm // GMLP_T):
                rs = slice(c * GMLP_T, (c + 1) * GMLP_T)
                sg = _mm(w, vn[rs, cs]) + b
                o_ref[rs, cs] = (_gelu(u_ref[rs, cs]) * sg * _silu(z_ref[rs, cs])).astype(BF16)

    col = lambda k: pl.BlockSpec((tm, GMLP_W), lambda i: (i, GU_BLK + k))
    vec = pl.BlockSpec((1, GMLP_W), lambda i: (0, 0))
    return pl.pallas_call(
        body, name="gmlp_fwd", grid=(s // tm,),
        in_specs=[col(0), col(1), col(2), vec, vec, pl.BlockSpec((GMLP_G, GMLP_T, GMLP_T), lambda i: (0, 0, 0)),
                  pl.BlockSpec((GMLP_T, GMLP_G), lambda i: (0, 0))],
        out_specs=pl.BlockSpec((tm, GMLP_W), lambda i: (i, 0)), out_shape=_sds((s, GMLP_W), BF16),
        compiler_params=_params(("parallel",)),
    )(proj, proj, proj, lng, lnb, ws, bs_t)


def _gmlp_bwd(proj, dmixed, lng, lnb, ws, bs_t):
    s = proj.shape[0]
    tm = _tile(s, (512, 256, 128))

    def body(u_ref, v_ref, z_ref, d_ref, lng_ref, lnb_ref, ws_ref, bst_ref,
             dp_ref, dws_ref, dbst_ref, dlng_ref, dlnb_ref, dvn):
        @pl.when(pl.program_id(0) == 0)
        def _():
            dws_ref[...] = jnp.zeros_like(dws_ref)
            dbst_ref[...] = jnp.zeros_like(dbst_ref)
            dlng_ref[...] = jnp.zeros_like(dlng_ref)
            dlnb_ref[...] = jnp.zeros_like(dlnb_ref)

        gv = v_ref[...]
        lng_v = lng_ref[...]
        vhat, rstd, vn = _gmlp_norm(gv, lng_v, lnb_ref[...])
        tri = _iota2((GMLP_T, GMLP_T), 0) >= _iota2((GMLP_T, GMLP_T), 1)
        for g in range(GMLP_G):
            cs = slice(g * HEAD, (g + 1) * HEAD)
            w = jnp.where(tri, ws_ref[g], 0.0).astype(BF16)
            b = bst_ref[:, g:g + 1]
            dw_acc = jnp.zeros((GMLP_T, GMLP_T), F32)
            db_acc = jnp.zeros((GMLP_T, 1), F32)
            for c in range(tm // GMLP_T):
                rs = slice(c * GMLP_T, (c + 1) * GMLP_T)
                vn_b = vn[rs, cs]
                sg = _mm(w, vn_b) + b
                gu = u_ref[rs, cs]
                gz = z_ref[rs, cs]
                da = d_ref[rs, cs]
                uact = _gelu(gu)
                sz = _silu(gz)
                ds = da * uact * sz
                dp_ref[rs, cs] = (da * sg * sz * _gelu_grad(gu)).astype(BF16)
                dp_ref[rs, 2 * GMLP_W + g * HEAD:2 * GMLP_W + (g + 1) * HEAD] = (da * uact * sg * _silu_grad(gz)).astype(BF16)
                dw_acc += _mm_nt(ds, vn_b)
                db_acc += _rowsum(ds)
                dvn[rs, cs] = _mm_tn(w, ds)
            dws_ref[g] += jnp.where(tri, dw_acc, 0.0)
            dbst_ref[:, g:g + 1] += db_acc
        dvn_v = dvn[...]
        dlng_ref[...] += _colsum(dvn_v * vhat)
        dlnb_ref[...] += _colsum(dvn_v)
        dvh = dvn_v * lng_v
        dva = rstd * (dvh - jnp.mean(dvh, axis=-1, keepdims=True) - vhat * jnp.mean(dvh * vhat, axis=-1, keepdims=True))
        dp_ref[:, GMLP_W:2 * GMLP_W] = (dva * _gelu_grad(gv)).astype(BF16)

    col = lambda k: pl.BlockSpec((tm, GMLP_W), lambda i: (i, GU_BLK + k))
    vec = pl.BlockSpec((1, GMLP_W), lambda i: (0, 0))
    wsp = pl.BlockSpec((GMLP_G, GMLP_T, GMLP_T), lambda i: (0, 0, 0))
    bsp = pl.BlockSpec((GMLP_T, GMLP_G), lambda i: (0, 0))
    return pl.pallas_call(
        body, name="gmlp_bwd", grid=(s // tm,),
        in_specs=[col(0), col(1), col(2), pl.BlockSpec((tm, GMLP_W), lambda i: (i, DN_W // GMLP_W)), vec, vec, wsp, bsp],
        out_specs=[pl.BlockSpec((tm, 3 * GMLP_W), lambda i: (i, 0)), wsp, bsp, vec, vec],
        out_shape=[_sds((s, 3 * GMLP_W), BF16), _sds((GMLP_G, GMLP_T, GMLP_T)), _sds((GMLP_T, GMLP_G)),
                   _sds((1, GMLP_W)), _sds((1, GMLP_W))],
        scratch_shapes=[pltpu.VMEM((tm, GMLP_W), F32)],
        compiler_params=_params(("arbitrary",)),
    )(proj, proj, proj, dmixed, lng, lnb, ws, bs_t)


CQ_BLK = (4 * DN_W + 3 * GMLP_W) // XA_W


def _memkv_fwd(mem, g, w_kv):
    nm, d = mem.shape

    def body(m_ref, g_ref, w_ref, kv_ref):
        mv = m_ref[...]
        r = lax.rsqrt(jnp.mean(mv * mv, axis=-1, keepdims=True) + EPS)
        kv_ref[...] = _mm(mv * r * g_ref[...], w_ref[...])

    return pl.pallas_call(body, name="memkv_fwd", out_shape=_sds((nm, 2 * XA_W)), compiler_params=_params())(mem, g, w_kv)


def _memkv_bwd(mem, g, w_kv, dkv):
    nm, d = mem.shape

    def body(m_ref, g_ref, w_ref, dkv_ref, dw_ref, dg_ref):
        mv = m_ref[...]
        r = lax.rsqrt(jnp.mean(mv * mv, axis=-1, keepdims=True) + EPS)
        xhat = mv * r
        dkv_v = dkv_ref[...]
        dw_ref[...] = _mm_tn(xhat * g_ref[...], dkv_v)
        dg_ref[...] = _colsum(_mm_nt(dkv_v, w_ref[...]) * xhat)

    return pl.pallas_call(body, name="memkv_bwd", out_shape=[_sds((d, 2 * XA_W)), _sds((1, d))],
                          compiler_params=_params())(mem, g, w_kv, dkv)


def _xattn_probs(q, mk):
    sc = _mm_nt(q, mk) * (HEAD ** -0.5)
    e = jnp.exp(sc - jnp.max(sc, axis=-1, keepdims=True))
    return e / _rowsum(e)


def _xattn_fwd(proj, mkv):
    s = proj.shape[0]
    nm = mkv.shape[0]
    tm = _tile(s, (512, 256, 128))

    def body(q_ref, z_ref, kv_ref, o_ref):
        for h in range(XA_H):
            cs = slice(h * HEAD, (h + 1) * HEAD)
            p = _xattn_probs(q_ref[:, cs], kv_ref[:, cs])
            ctx = _mm(p, kv_ref[:, XA_W + h * HEAD:XA_W + (h + 1) * HEAD])
            o_ref[:, cs] = (ctx * _silu(z_ref[:, cs])).astype(BF16)

    col = lambda k: pl.BlockSpec((tm, XA_W), lambda i: (i, CQ_BLK + k))
    return pl.pallas_call(
        body, name="xattn_fwd", grid=(s // tm,),
        in_specs=[col(0), col(1), pl.BlockSpec((nm, 2 * XA_W), lambda i: (0, 0))],
        out_specs=pl.BlockSpec((tm, XA_W), lambda i: (i, 0)), out_shape=_sds((s, XA_W), BF16),
        compiler_params=_params(("parallel",)),
    )(proj, proj, mkv)


def _xattn_bwd(proj, dmixed, mkv):
    s = proj.shape[0]
    nm = mkv.shape[0]
    tm = _tile(s, (512, 256, 128))

    def body(q_ref, z_ref, d_ref, kv_ref, dp_ref, dkv_ref):
        @pl.when(pl.program_id(0) == 0)
        def _():
            dkv_ref[...] = jnp.zeros_like(dkv_ref)

        for h in range(XA_H):
            cs = slice(h * HEAD, (h + 1) * HEAD)
            vs = slice(XA_W + h * HEAD, XA_W + (h + 1) * HEAD)
            q = q_ref[:, cs]
            z = z_ref[:, cs]
            mk = kv_ref[:, cs]
            mv = kv_ref[:, vs]
            p = _xattn_probs(q, mk)
            ctx = _mm(p, mv)
            dc = d_ref[:, cs]
            dctx = dc * _silu(z)
            dp_ref[:, vs] = (dc * ctx * _silu_grad(z)).astype(BF16)
            dp = _mm_nt(dctx, mv)
            dkv_ref[:, vs] += _mm_tn(p, dctx)
            ds = p * (dp - _rowsum(dp * p)) * (HEAD ** -0.5)
            dp_ref[:, cs] = _mm(ds, mk).astype(BF16)
            dkv_ref[:, cs] += _mm_tn(ds, q)

    col = lambda k: pl.BlockSpec((tm, XA_W), lambda i: (i, CQ_BLK + k))
    kvs = pl.BlockSpec((nm, 2 * XA_W), lambda i: (0, 0))
    return pl.pallas_call(
        body, name="xattn_bwd", grid=(s // tm,),
        in_specs=[col(0), col(1), pl.BlockSpec((tm, XA_W), lambda i: (i, (DN_W + GMLP_W) // XA_W)), kvs],
        out_specs=[pl.BlockSpec((tm, 2 * XA_W), lambda i: (i, 0)), kvs],
        out_shape=[_sds((s, 2 * XA_W), BF16), _sds((nm, 2 * XA_W))],
        compiler_params=_params(("arbitrary",)),
    )(proj, proj, dmixed, mkv)


def _softplus(x):
    return jnp.maximum(x, 0.0) + jnp.log1p(jnp.exp(-jnp.abs(x)))


def _dn_pre(proj, ab, conv_w, alog_row, dt_row):
    s = proj.shape[0]
    tm = _tile(s, (256, 128))
    w3 = 3 * DN_W

    def body(x_ref, halo_ref, ab_ref, cw_ref, al_ref, dt_ref, q_ref, k_ref, v_ref, gb_ref, gbt_ref, ext):
        i = pl.program_id(0)
        ext[0:HALO, :] = jnp.where(i > 0, halo_ref[...], 0.0)
        ext[HALO:HALO + tm, :] = x_ref[...]
        yc = cw_ref[0:1, :] * ext[pl.ds(HALO - DN_K + 1, tm), :]
        for t in range(1, DN_K):
            yc += cw_ref[t:t + 1, :] * ext[pl.ds(HALO - DN_K + 1 + t, tm), :]
        act = _silu(yc)
        for h in range(DN_H):
            cs = slice(h * HEAD, (h + 1) * HEAD)
            qa = act[:, cs]
            q_ref[:, cs] = qa * (lax.rsqrt(_rowsum(qa * qa) + EPS) * (HEAD ** -0.5))
            ka = act[:, DN_W + h * HEAD:DN_W + (h + 1) * HEAD]
            k_ref[:, cs] = ka * lax.rsqrt(_rowsum(ka * ka) + EPS)
        v_ref[...] = act[:, 2 * DN_W:w3]
        abv = ab_ref[...]
        lane = _iota2((tm, LANE), 1)
        g = jnp.where(lane < DN_H, -jnp.exp(al_ref[...]) * _softplus(abv + dt_ref[...]), 0.0)
        gc = _mm_hi(_chunk_tri(tm, False), g)
        gbv = jnp.where(lane < DN_H, gc, jnp.where(lane < 2 * DN_H, jax.nn.sigmoid(abv), 0.0))
        gb_ref[...] = gbv
        for c in range(tm // CH):
            gbt_ref[c] = gbv[c * CH:(c + 1) * CH, :].T[0:2 * DN_H, :]

    hb = tm // HALO
    row = lambda w: pl.BlockSpec((tm, w), lambda i: (i, 0))
    vec = pl.BlockSpec((1, LANE), lambda i: (0, 0))
    return pl.pallas_call(
        body, name="dn_pre", grid=(s // tm,),
        in_specs=[row(w3), pl.BlockSpec((HALO, w3), lambda i: (jnp.maximum(i * hb - 1, 0), 0)), row(LANE),
                  pl.BlockSpec((DN_K, w3), lambda i: (0, 0)), vec, vec],
        out_specs=[row(DN_W), row(DN_W), row(DN_W), row(LANE), pl.BlockSpec((tm // CH, 2 * DN_H, CH), lambda i: (i, 0, 0))],
        out_shape=[_sds((s, DN_W)), _sds((s, DN_W)), _sds((s, DN_W)), _sds((s, LANE)), _sds((s // CH, 2 * DN_H, CH))],
        scratch_shapes=[pltpu.VMEM((tm + HALO, w3), F32)],
        compiler_params=_params(("parallel",)),
    )(proj, proj, ab, conv_w, alog_row, dt_row)


HEADS = tuple(range(DN_H))


def _hcols(h):
    return slice(h * HEAD, (h + 1) * HEAD)


def _chunk_scalings(k, v, gbv, gbt, h):
    gc = gbv[:, h:h + 1]
    beta = gbv[:, DN_H + h:DN_H + h + 1]
    gr = gbt[h:h + 1, :]
    ii = _iota2((CH, CH), 0)
    jj = _iota2((CH, CH), 1)
    dec = jnp.exp(jnp.where(ii >= jj, gc - gr, -1e30))
    eg = jnp.exp(gc)
    gl = gr[:, CH - 1:CH]
    kb = k * beta
    return dict(beta=beta, dec=dec, eg=eg, gl=gl, ekd=jnp.exp(gl - gc), kb=kb, vb=v * beta, kbe=kb * eg)


def _chunk_scores(m, q, k):
    kq = _mm_nt(jnp.concatenate([m["kb"], q], axis=0), k)
    strict = _iota2((CH, CH), 0) > _iota2((CH, CH), 1)
    return jnp.where(strict, kq[0:CH] * m["dec"], 0.0), kq[CH:2 * CH] * m["dec"]


def _dn_local(q, k, v, gb, gbt):
    s = q.shape[0]
    cpb = 4 if (s // CH) % 4 == 0 else 1
    tb = cpb * CH
    nblk = s // tb

    def body(q_ref, k_ref, v_ref, gb_ref, gbt_ref, u_ref, w_ref, qg_ref, kd_ref, t_ref, ai_ref, egl_ref):
        def chunk(c, carry):
            r0 = pl.multiple_of(c * CH, CH)
            rows = pl.ds(r0, CH)
            gbv = gb_ref[rows, :]
            gbt_v = gbt_ref[c]
            qs = [q_ref[rows, _hcols(h)] for h in HEADS]
            ks = [k_ref[rows, _hcols(h)] for h in HEADS]
            ms = [_chunk_scalings(ks[h], v_ref[rows, _hcols(h)], gbv, gbt_v, h) for h in HEADS]
            for h in HEADS:
                qg_ref[rows, _hcols(h)] = (qs[h] * ms[h]["eg"]).astype(BF16)
                kd_ref[rows, _hcols(h)] = (ks[h] * ms[h]["ekd"]).astype(BF16)
                egl_ref[c, h:h + 1, :] = jnp.broadcast_to(jnp.exp(ms[h]["gl"]), (1, LANE))
            sc = [_chunk_scores(ms[h], qs[h], ks[h]) for h in HEADS]
            for h in HEADS:
                ai_ref[h, rows, :] = sc[h][1]
            eye = jnp.where(_iota2((CH, CH), 0) == _iota2((CH, CH), 1), 1.0, 0.0).astype(F32)
            ts = [eye - sc[h][0] for h in HEADS]
            ps = [_mm_3x(sc[h][0], sc[h][0]) for h in HEADS]
            for step in range(5):
                ts = [ts[h] + _mm_3x(ts[h], ps[h]) for h in HEADS]
                if step < 4:
                    ps = [_mm_3x(ps[h], ps[h]) for h in HEADS]
            for h in HEADS:
                t_ref[h, rows, :] = ts[h]
                uw = _mm(ts[h], jnp.concatenate([ms[h]["vb"], ms[h]["kbe"]], axis=1))
                u_ref[rows, _hcols(h)] = uw[:, 0:HEAD]
                w_ref[rows, _hcols(h)] = uw[:, HEAD:2 * HEAD].astype(BF16)
            return carry

        lax.fori_loop(0, cpb, chunk, 0)

    row = pl.BlockSpec((tb, DN_W), lambda i: (i, 0))
    sq = pl.BlockSpec((DN_H, tb, CH), lambda i: (0, i, 0))
    return pl.pallas_call(
        body, name="dn_local", grid=(nblk,),
        in_specs=[row, row, row, pl.BlockSpec((tb, LANE), lambda i: (i, 0)),
                  pl.BlockSpec((cpb, 2 * DN_H, CH), lambda i: (i, 0, 0))],
        out_specs=[row, row, row, row, sq, sq, pl.BlockSpec((cpb, DN_H, LANE), lambda i: (i, 0, 0))],
        out_shape=[_sds((s, DN_W)), _sds((s, DN_W), BF16), _sds((s, DN_W), BF16), _sds((s, DN_W), BF16),
                   _sds((DN_H, s, CH)), _sds((DN_H, s, CH)), _sds((s // CH, DN_H, LANE))],
        compiler_params=_params(("parallel",)),
    )(q, k, v, gb, gbt)


def _scan_cpb(s):
    return 8 if (s // CH) % 8 == 0 else 1


def _dn_scan(u, w, qg, kd, ai, egl, proj, norm_g):
    s = u.shape[0]
    cpb = _scan_cpb(s)
    tb = cpb * CH
    nblk = s // tb

    def body(u_ref, w_ref, qg_ref, kd_ref, ai_ref, egl_ref, z_ref, ng_ref, o_ref, vn_ref, st_ref, ob_ref, state):
        @pl.when(pl.program_id(0) == 0)
        def _():
            state[...] = jnp.zeros_like(state)

        ng = ng_ref[...]

        def chunk(c, carry):
            r0 = pl.multiple_of(c * CH, CH)
            rows = pl.ds(r0, CH)
            sts = [state[h] for h in HEADS]
            stb = [sts[h].astype(BF16) for h in HEADS]
            for h in HEADS:
                st_ref[c, h] = sts[h]
            vns = [u_ref[rows, _hcols(h)] - jnp.dot(w_ref[rows, _hcols(h)], stb[h], preferred_element_type=F32)
                   for h in HEADS]
            vnb = [vns[h].astype(BF16) for h in HEADS]
            for h in HEADS:
                state[h] = sts[h] * egl_ref[c, h:h + 1, :] + _mm_tn(kd_ref[rows, _hcols(h)], vnb[h])
            os_ = [jnp.dot(qg_ref[rows, _hcols(h)], stb[h], preferred_element_type=F32) + _mm(ai_ref[h, rows, :], vnb[h])
                   for h in HEADS]
            for h in HEADS:
                o = os_[h]
                vn_ref[rows, _hcols(h)] = vns[h]
                o_ref[rows, _hcols(h)] = o
                r = lax.rsqrt(jnp.mean(o * o, axis=-1, keepdims=True) + EPS)
                ob_ref[rows, _hcols(h)] = (o * r * ng * _silu(z_ref[rows, _hcols(h)])).astype(BF16)
            return carry

        lax.fori_loop(0, cpb, chunk, 0)

    row = pl.BlockSpec((tb, DN_W), lambda i: (i, 0))
    return pl.pallas_call(
        body, name="dn_scan", grid=(nblk,),
        in_specs=[row, row, row, row, pl.BlockSpec((DN_H, tb, CH), lambda i: (0, i, 0)),
                  pl.BlockSpec((cpb, DN_H, LANE), lambda i: (i, 0, 0)), pl.BlockSpec((tb, DN_W), lambda i: (i, 3)),
                  pl.BlockSpec((1, HEAD), lambda i: (0, 0))],
        out_specs=[row, row, pl.BlockSpec((cpb, DN_H, HEAD, HEAD), lambda i: (i, 0, 0, 0)), row],
        out_shape=[_sds((s, DN_W)), _sds((s, DN_W)), _sds((s // CH, DN_H, HEAD, HEAD)), _sds((s, DN_W), BF16)],
        scratch_shapes=[pltpu.VMEM((DN_H, HEAD, HEAD), F32)],
        compiler_params=_params(("arbitrary",)),
    )(u, w, qg, kd, ai, egl, proj, norm_g)


def _dn_scan_bwd(dmixed, o, proj, norm_g, w, qg, kd, ai, egl):
    s = o.shape[0]
    cpb = _scan_cpb(s)
    tb = cpb * CH
    nblk = s // tb

    def body(dm_ref, o_ref, z_ref, ng_ref, w_ref, qg_ref, kd_ref, ai_ref, egl_ref,
             do_ref, dvn_ref, dst_ref, dz_ref, dng_ref, dstate):
        @pl.when(pl.program_id(0) == 0)
        def _():
            dstate[...] = jnp.zeros_like(dstate)
            dng_ref[...] = jnp.zeros_like(dng_ref)

        ng = ng_ref[...]

        def chunk(cc, carry):
            c = cpb - 1 - cc
            r0 = pl.multiple_of(c * CH, CH)
            rows = pl.ds(r0, CH)
            dng = jnp.zeros((1, HEAD), F32)
            dob = []
            for h in HEADS:
                cs = _hcols(h)
                o = o_ref[rows, cs]
                z = z_ref[rows, cs]
                db = dm_ref[rows, cs]
                r = lax.rsqrt(jnp.mean(o * o, axis=-1, keepdims=True) + EPS)
                ohat = o * r
                dz_ref[rows, cs] = (db * ohat * ng * _silu_grad(z)).astype(BF16)
                dyn = db * _silu(z)
                dng += _colsum(dyn * ohat)
                doh = dyn * ng
                do = r * (doh - ohat * jnp.mean(doh * ohat, axis=-1, keepdims=True))
                do_ref[rows, cs] = do
                dob.append(do.astype(BF16))
            dng_ref[...] += dng
            dsn = [dstate[h] for h in HEADS]
            for h in HEADS:
                dst_ref[c, h] = dsn[h]
            dvn = [_mm_tn(ai_ref[h, rows, :], dob[h])
                   + jnp.dot(kd_ref[rows, _hcols(h)], dsn[h].astype(BF16), preferred_element_type=F32) for h in HEADS]
            part = [_mm_tn(qg_ref[rows, _hcols(h)], dob[h]) + egl_ref[c, h:h + 1, :] * dsn[h] for h in HEADS]
            for h in HEADS:
                dvn_ref[rows, _hcols(h)] = dvn[h]
                dstate[h] = part[h] - _mm_tn(w_ref[rows, _hcols(h)], dvn[h])
            return carry

        lax.fori_loop(0, cpb, chunk, 0)

    rev = lambda i: (nblk - 1 - i, 0)
    row = pl.BlockSpec((tb, DN_W), rev)
    vec = pl.BlockSpec((1, HEAD), lambda i: (0, 0))
    return pl.pallas_call(
        body, name="dn_scan_bwd", grid=(nblk,),
        in_specs=[row, row, pl.BlockSpec((tb, DN_W), lambda i: (nblk - 1 - i, 3)), vec, row, row, row,
                  pl.BlockSpec((DN_H, tb, CH), lambda i: (0, nblk - 1 - i, 0)),
                  pl.BlockSpec((cpb, DN_H, LANE), lambda i: (nblk - 1 - i, 0, 0))],
        out_specs=[row, row, pl.BlockSpec((cpb, DN_H, HEAD, HEAD), lambda i: (nblk - 1 - i, 0, 0, 0)), row, vec],
        out_shape=[_sds((s, DN_W)), _sds((s, DN_W)), _sds((s // CH, DN_H, HEAD, HEAD)), _sds((s, DN_W), BF16),
                   _sds((1, HEAD))],
        scratch_shapes=[pltpu.VMEM((DN_H, HEAD, HEAD), F32)],
        compiler_params=_params(("arbitrary",)),
    )(dmixed, o, proj, norm_g, w, qg, kd, ai, egl)


def _dn_local_bwd(q, k, v, gb, gbt, t, vn, st, dst, do, dvn):
    s = q.shape[0]
    cpb = 4 if (s // CH) % 4 == 0 else 1
    tb = cpb * CH
    nblk = s // tb

    def body(q_ref, k_ref, v_ref, gb_ref, gbt_ref, t_ref, vn_ref, st_ref, dst_ref, do_ref, dvn_ref,
             dq_ref, dk_ref, dv_ref, dgb_ref):
        lane = _iota2((CH, LANE), 1)
        last = _iota2((CH, 1), 0) == CH - 1

        def chunk(c, carry):
            r0 = pl.multiple_of(c * CH, CH)
            rows = pl.ds(r0, CH)
            gbv = gb_ref[rows, :]
            gbt_v = gbt_ref[c]
            strict = _iota2((CH, CH), 0) > _iota2((CH, CH), 1)
            qs = [q_ref[rows, _hcols(h)] for h in HEADS]
            ks = [k_ref[rows, _hcols(h)] for h in HEADS]
            vs = [v_ref[rows, _hcols(h)] for h in HEADS]
            ms = [_chunk_scalings(ks[h], vs[h], gbv, gbt_v, h) for h in HEADS]
            sts = [st_ref[c, h] for h in HEADS]
            dsn = [dst_ref[c, h] for h in HEADS]
            dob = [do_ref[rows, _hcols(h)].astype(BF16) for h in HEADS]
            dvnb = [dvn_ref[rows, _hcols(h)].astype(BF16) for h in HEADS]
            vnb = [vn_ref[rows, _hcols(h)].astype(BF16) for h in HEADS]
            tbf = [t_ref[h, rows, :].astype(BF16) for h in HEADS]
            sc = [_chunk_scores(ms[h], qs[h], ks[h]) for h in HEADS]
            xs_ = [_mm_nt(jnp.concatenate([dob[h], dvnb[h]], axis=0), sts[h]) for h in HEADS]
            dai = [_mm_nt(dob[h], vnb[h]) for h in HEADS]
            dkd = [_mm_nt(vnb[h], dsn[h]) for h in HEADS]
            dqg = [xs_[h][0:CH] for h in HEADS]
            duw = [jnp.concatenate([dvnb[h], (-xs_[h][CH:2 * CH]).astype(BF16)], axis=1) for h in HEADS]
            dt = [_mm_nt(duw[h], jnp.concatenate([ms[h]["vb"], ms[h]["kbe"]], axis=1)) for h in HEADS]
            dvk = [_mm_tn(tbf[h], duw[h]) for h in HEADS]
            tdt = [_mm_tn(tbf[h], dt[h]) for h in HEADS]
            da = [jnp.where(strict, -_mm_nt(tdt[h], tbf[h]), 0.0) for h in HEADS]
            dsc = [jnp.concatenate([da[h] * ms[h]["dec"], dai[h] * ms[h]["dec"]], axis=0) for h in HEADS]
            dkq = [_mm(dsc[h], ks[h]) for h in HEADS]
            dk1 = [_mm_tn(dsc[h], jnp.concatenate([ms[h]["kb"], qs[h]], axis=0)) for h in HEADS]
            dgb = jnp.zeros((CH, LANE), F32)
            for h in HEADS:
                m = ms[h]
                eg, ekd, beta = m["eg"], m["ekd"], m["beta"]
                dvb = dvk[h][:, 0:HEAD]
                dkbe = dvk[h][:, HEAD:2 * HEAD]
                kd = ks[h] * ekd
                dkb = dkq[h][0:CH] + dkbe * eg
                dq_ref[rows, _hcols(h)] = dkq[h][CH:2 * CH] + dqg[h] * eg
                dk_ref[rows, _hcols(h)] = dk1[h] + dkd[h] * ekd + dkb * beta
                dv_ref[rows, _hcols(h)] = dvb * beta
                dgl = jnp.exp(m["gl"]) * _colsum(_rowsum(sts[h] * dsn[h])) + _colsum(_rowsum(dkd[h] * kd))
                mm_ = da[h] * sc[h][0] + dai[h] * sc[h][1]
                dgc = (_rowsum(mm_) - _rowsum(mm_.T) + _rowsum(dqg[h] * qs[h] * eg) - _rowsum(dkd[h] * kd)
                       + _rowsum(dkbe * m["kbe"]) + jnp.where(last, dgl, 0.0))
                dbeta = _rowsum(dkb * ks[h]) + _rowsum(dvb * vs[h])
                dgb = jnp.where(lane == h, dgc, jnp.where(lane == DN_H + h, dbeta, dgb))
            dgb_ref[rows, :] = dgb
            return carry

        lax.fori_loop(0, cpb, chunk, 0)

    row = pl.BlockSpec((tb, DN_W), lambda i: (i, 0))
    gbs = pl.BlockSpec((tb, LANE), lambda i: (i, 0))
    sts = pl.BlockSpec((cpb, DN_H, HEAD, HEAD), lambda i: (i, 0, 0, 0))
    return pl.pallas_call(
        body, name="dn_local_bwd", grid=(nblk,),
        in_specs=[row, row, row, gbs, pl.BlockSpec((cpb, 2 * DN_H, CH), lambda i: (i, 0, 0)),
                  pl.BlockSpec((DN_H, tb, CH), lambda i: (0, i, 0)), row, sts, sts, row, row],
        out_specs=[row, row, row, gbs],
        out_shape=[_sds((s, DN_W)), _sds((s, DN_W)), _sds((s, DN_W)), _sds((s, LANE))],
        compiler_params=_params(("parallel",)),
    )(q, k, v, gb, gbt, t, vn, st, dst, do, dvn)


def _dn_pre_bwd(proj, ab, conv_w, alog_row, dt_row, dq, dk, dv, dgb):
    s = proj.shape[0]
    tm = _tile(s, (256, 128))
    w3 = 3 * DN_W
    nblk = s // tm

    def body(x_ref, halo_ref, ab_ref, cw_ref, al_ref, dt_ref, dq_ref, dk_ref, dv_ref, dgb_ref,
             dx_ref, dab_ref, dcw_ref, dal_ref, ddt_ref, ext, exd, carry):
        i = pl.program_id(0)

        @pl.when(i == 0)
        def _():
            carry[...] = jnp.zeros_like(carry)
            dcw_ref[...] = jnp.zeros_like(dcw_ref)
            dal_ref[...] = jnp.zeros_like(dal_ref)
            ddt_ref[...] = jnp.zeros_like(ddt_ref)

        ext[0:HALO, :] = jnp.where(i < nblk - 1, halo_ref[...], 0.0)
        ext[HALO:HALO + tm, :] = x_ref[...]
        yc = cw_ref[0:1, :] * ext[pl.ds(HALO - DN_K + 1, tm), :]
        for t in range(1, DN_K):
            yc += cw_ref[t:t + 1, :] * ext[pl.ds(HALO - DN_K + 1 + t, tm), :]
        sg = jax.nn.sigmoid(yc)
        act = yc * sg
        dact = sg * (1.0 + yc * (1.0 - sg))
        for h in range(DN_H):
            cs = slice(h * HEAD, (h + 1) * HEAD)
            ks = slice(DN_W + h * HEAD, DN_W + (h + 1) * HEAD)
            qa = act[:, cs]
            rq = lax.rsqrt(_rowsum(qa * qa) + EPS)
            qh = qa * rq
            dqv = dq_ref[:, cs]
            exd[0:tm, cs] = (HEAD ** -0.5) * rq * (dqv - qh * _rowsum(dqv * qh)) * dact[:, cs]
            ka = act[:, ks]
            rk = lax.rsqrt(_rowsum(ka * ka) + EPS)
            kh = ka * rk
            dkv = dk_ref[:, cs]
            exd[0:tm, ks] = rk * (dkv - kh * _rowsum(dkv * kh)) * dact[:, ks]
        exd[0:tm, 2 * DN_W:w3] = dv_ref[...] * dact[:, 2 * DN_W:w3]
        exd[tm:tm + HALO, :] = carry[...]
        dyc = exd[0:tm, :]
        dx = cw_ref[0:1, :] * exd[pl.ds(DN_K - 1, tm), :]
        dcw_ref[0:1, :] += _colsum(dyc * ext[pl.ds(HALO - DN_K + 1, tm), :])
        for t in range(1, DN_K):
            dx += cw_ref[t:t + 1, :] * exd[pl.ds(DN_K - 1 - t, tm), :]
            dcw_ref[t:t + 1, :] += _colsum(dyc * ext[pl.ds(HALO - DN_K + 1 + t, tm), :])
        dx_ref[...] = dx.astype(BF16)
        carry[...] = exd[0:HALO, :]

        lane = _iota2((tm, LANE), 1)
        dgbv = dgb_ref[...]
        dg = _mm_hi(_chunk_tri(tm, True), jnp.where(lane < DN_H, dgbv, 0.0))
        abv = ab_ref[...]
        xa = abv + dt_ref[...]
        nea = -jnp.exp(al_ref[...])
        d_da = jnp.where(lane < DN_H, dg * nea * jax.nn.sigmoid(xa), 0.0)
        dal_ref[...] += _colsum(jnp.where(lane < DN_H, dg * nea * _softplus(xa), 0.0))
        ddt_ref[...] += _colsum(d_da)
        beta = jax.nn.sigmoid(abv)
        d_db = jnp.where((lane >= DN_H) & (lane < 2 * DN_H), dgbv * beta * (1.0 - beta), 0.0)
        dab_ref[...] = (d_da + d_db).astype(BF16)

    hb = tm // HALO
    rev = lambda i: (nblk - 1 - i, 0)
    row = lambda w: pl.BlockSpec((tm, w), rev)
    vec = pl.BlockSpec((1, LANE), lambda i: (0, 0))
    cws = pl.BlockSpec((DN_K, w3), lambda i: (0, 0))
    return pl.pallas_call(
        body, name="dn_pre_bwd", grid=(nblk,),
        in_specs=[row(w3), pl.BlockSpec((HALO, w3), lambda i: (jnp.maximum((nblk - 1 - i) * hb - 1, 0), 0)), row(LANE),
                  cws, vec, vec, row(DN_W), row(DN_W), row(DN_W), row(LANE)],
        out_specs=[row(w3), row(LANE), cws, vec, vec],
        out_shape=[_sds((s, w3), BF16), _sds((s, LANE), BF16), _sds((DN_K, w3)), _sds((1, LANE)), _sds((1, LANE))],
        scratch_shapes=[pltpu.VMEM((tm + HALO, w3), F32), pltpu.VMEM((tm + HALO, w3), F32), pltpu.VMEM((HALO, w3), F32)],
        compiler_params=_params(("arbitrary",)),
    )(proj, proj, ab, conv_w, alog_row, dt_row, dq, dk, dv, dgb)


def _adam(parts, w, m, v, name):
    r, c = w.shape
    n_parts = parts.shape[0]
    tr = _tile(r, (128, 64, 32, 16, 8))

    def body(p_ref, w_ref, m_ref, v_ref, g_ref, d_ref, nm_ref, nv_ref):
        g = p_ref[0].astype(F32)
        for k in range(1, n_parts):
            g = g + p_ref[k].astype(F32)
        g_ref[...] = g
        mn = ADAM_B1 * m_ref[...] + (1.0 - ADAM_B1) * g
        vn = ADAM_B2 * v_ref[...] + (1.0 - ADAM_B2) * (g * g)
        m_hat = mn / (1.0 - ADAM_B1 ** ADAM_STEP)
        v_hat = vn / (1.0 - ADAM_B2 ** ADAM_STEP)
        d_ref[...] = -ADAM_LR * (m_hat / (jnp.sqrt(v_hat) + ADAM_EPS) + ADAM_WD * w_ref[...])
        nm_ref[...] = mn
        nv_ref[...] = vn

    blk = pl.BlockSpec((tr, c), lambda i: (i, 0))
    return pl.pallas_call(
        body, name=name, grid=(r // tr,),
        in_specs=[pl.BlockSpec((n_parts, tr, c), lambda i: (0, i, 0)), blk, blk, blk],
        out_specs=[blk, blk, blk, blk], out_shape=[_sds((r, c))] * 4,
        compiler_params=_params(("parallel",)),
    )(parts, w, m, v)


_PACK_ROWS = 8


def _pack(vals):
    tiles = []
    for a in vals:
        flat = a.reshape(-1).astype(F32)
        unit = _PACK_ROWS * LANE
        n = -(-flat.shape[0] // unit) * unit
        tiles.append(jnp.pad(flat, (0, n - flat.shape[0])).reshape(n // LANE, LANE))
    return jnp.concatenate(tiles, axis=0)


def _unpack(packed, shapes):
    out = []
    r0 = 0
    for shp in shapes:
        size = 1
        for dim in shp:
            size *= dim
        unit = _PACK_ROWS * LANE
        rows = -(-size // unit) * _PACK_ROWS
        out.append(packed[r0:r0 + rows].reshape(-1)[:size].reshape(shp))
        r0 += rows
    return out


def _lane_row(vec8):
    return jnp.pad(vec8.reshape(1, -1).astype(F32), ((0, 0), (0, LANE - vec8.size)))


def kernel(x, mem, ln_g, w_in, gmlp_ln_g, gmlp_ln_b, gmlp_ws, gmlp_bs, conv_w, dn_a_log, dn_dt_bias, dn_norm_g, mem_norm_g, w_mem_kv, w_out, final_g, loss_target, m_ln_g, m_w_in, m_gmlp_ln_g, m_gmlp_ln_b, m_gmlp_ws, m_gmlp_bs, m_conv_w, m_dn_a_log, m_dn_dt_bias, m_dn_norm_g, m_mem_norm_g, m_w_mem_kv, m_w_out, m_final_g, v_ln_g, v_w_in, v_gmlp_ln_g, v_gmlp_ln_b, v_gmlp_ws, v_gmlp_bs, v_conv_w, v_dn_a_log, v_dn_dt_bias, v_dn_norm_g, v_mem_norm_g, v_w_mem_kv, v_w_out, v_final_g):
    xs = x[0]
    mems = mem[0]
    tgt = loss_target[0]
    s, d = xs.shape
    shard_w = w_in.shape[2]
    in_w = N_DEV * shard_w
    me = 4 * lax.axis_index("x") + 2 * lax.axis_index("y") + lax.axis_index("c")

    g_in, g_out, g_kv, g_conv = _gather_two_level(
        [w_in[0].astype(BF16), w_out[0].astype(BF16), w_mem_kv[0].astype(BF16), conv_w[0]], "gather_weights")
    w_full = g_in.transpose(1, 0, 2).reshape(d, in_w)
    o_g, o_dn, o_ab = 0, 3 * GMLP_W, 3 * GMLP_W + 4 * DN_W
    o_xa = o_ab + 2 * DN_H
    w_qkv = w_full[:, o_dn:o_dn + 3 * DN_W]
    w_dz = w_full[:, o_dn + 3 * DN_W:o_ab]
    w_gm = w_full[:, o_g:o_dn]
    w_xa = w_full[:, o_xa:in_w]
    w_ab = jnp.pad(w_full[:, o_ab:o_xa], ((0, 0), (0, LANE - 2 * DN_H)))
    w_main = jnp.concatenate([w_qkv, w_dz, w_gm, w_xa], axis=1)
    wo = g_out.reshape(MIX_W, d)
    wo_perm = jnp.concatenate([wo[GMLP_W:GMLP_W + DN_W], wo[0:GMLP_W], wo[GMLP_W + DN_W:MIX_W]], axis=0)
    w_kv = g_kv.reshape(d, 2 * XA_W)
    conv_full = g_conv.transpose(1, 0, 2).reshape(DN_K, 3 * DN_W)

    ln_g2 = ln_g.reshape(1, d)
    lng2 = gmlp_ln_g.reshape(1, GMLP_W)
    lnb2 = gmlp_ln_b.reshape(1, GMLP_W)
    ws3 = gmlp_ws[0]
    bs_t = gmlp_bs[0].T
    alog_row = _lane_row(dn_a_log)
    dt_row = _lane_row(dn_dt_bias)
    dn_g2 = dn_norm_g.reshape(1, HEAD)
    mem_g2 = mem_norm_g.reshape(1, d)
    fin_g2 = final_g.reshape(1, d)

    proj, ab, h = _inproj(xs, ln_g2, w_main, w_ab)
    out_a = _gmlp_fwd(proj, lng2, lnb2, ws3, bs_t)
    mkv = _memkv_fwd(mems, mem_g2, w_kv)
    out_c = _xattn_fwd(proj, mkv)
    q, k, v, gb, gbt = _dn_pre(proj, ab, conv_full, alog_row, dt_row)
    u, wk, qg, kd, tmat, ai, egl = _dn_local(q, k, v, gb, gbt)
    o, vn, st, out_b = _dn_scan(u, wk, qg, kd, ai, egl, proj, dn_g2)

    dx2, dx2b, dmixed, loss_acc, d_fin_g = _final(xs, tgt, out_b, out_a, out_c, wo_perm, fin_g2)
    loss = lax.psum(loss_acc[0, 0], ("x", "y", "c"))

    dwo_b = _matmul_tn(out_b, dx2b, "dw_out_b")
    dwo_a = _matmul_tn(out_a, dx2b, "dw_out_a")
    dwo_c = _matmul_tn(out_c, dx2b, "dw_out_c")
    d_w_out = jnp.concatenate([dwo_a, dwo_b, dwo_c], axis=0)

    dp_g, d_ws, d_bst, d_lng, d_lnb = _gmlp_bwd(proj, dmixed, lng2, lnb2, ws3, bs_t)
    dp_x, dmkv = _xattn_bwd(proj, dmixed, mkv)
    d_w_kv, d_mem_g = _memkv_bwd(mems, mem_g2, w_kv, dmkv)
    do, dvn, dst, dp_dz, d_dn_g = _dn_scan_bwd(dmixed, o, proj, dn_g2, wk, qg, kd, ai, egl)
    dq, dk, dv, dgb = _dn_local_bwd(q, k, v, gb, gbt, tmat, vn, st, dst, do, dvn)
    dp_qkv, dp_ab, d_conv, d_alog, d_dt = _dn_pre_bwd(proj, ab, conv_full, alog_row, dt_row, dq, dk, dv, dgb)

    pieces = [dp_qkv, dp_dz, dp_g, dp_x, dp_ab]
    dh = _dh(pieces, [w_qkv, w_dz, w_gm, w_xa, w_ab])
    grad_x, d_ln_g = _rms_bwd(xs, dh, dx2, ln_g2)
    dw_qkv = _matmul_tn(h, dp_qkv, "dw_in_qkv")
    dw_dz = _matmul_tn(h, dp_dz, "dw_in_dz")
    dw_gm = _matmul_tn(h, dp_g, "dw_in_gmlp")
    dw_xa = _matmul_tn(h, dp_x, "dw_in_xa")
    dw_ab = _matmul_tn(h, dp_ab, "dw_in_ab")
    d_w_in = jnp.concatenate([dw_gm, dw_qkv, dw_dz, dw_ab[:, :2 * DN_H], dw_xa], axis=1)

    small_shapes = [ln_g.shape, gmlp_ln_g.shape, gmlp_ln_b.shape, gmlp_ws.shape, gmlp_bs.shape, dn_a_log.shape,
                    dn_dt_bias.shape, dn_norm_g.shape, mem_norm_g.shape, final_g.shape, (DN_K, 3 * DN_W)]
    small_g = _pack([d_ln_g, d_lng, d_lnb, d_ws, d_bst.T, d_alog[:, :DN_H], d_dt[:, :DN_H], d_dn_g, d_mem_g, d_fin_g,
                     d_conv])
    zc = jnp.zeros((DN_K, 3 * DN_W), F32)
    small_w = _pack([ln_g, gmlp_ln_g, gmlp_ln_b, gmlp_ws, gmlp_bs, dn_a_log, dn_dt_bias, dn_norm_g, mem_norm_g, final_g, zc])
    small_m = _pack([m_ln_g, m_gmlp_ln_g, m_gmlp_ln_b, m_gmlp_ws, m_gmlp_bs, m_dn_a_log, m_dn_dt_bias, m_dn_norm_g,
                     m_mem_norm_g, m_final_g, zc])
    small_v = _pack([v_ln_g, v_gmlp_ln_g, v_gmlp_ln_b, v_gmlp_ws, v_gmlp_bs, v_dn_a_log, v_dn_dt_bias, v_dn_norm_g,
                     v_mem_norm_g, v_final_g, zc + 1.0])

    send_in = d_w_in.reshape(d, N_DEV, shard_w).transpose(1, 0, 2)
    send_out = d_w_out.reshape(N_DEV, MIX_W // N_DEV, d)
    send_kv = d_w_kv.reshape(N_DEV, d // N_DEV, 2 * XA_W)
    all_small, kept, got = _swap_halves(small_g, [send_in, send_out, send_kv], "swap_halves")
    chip_sums = [_pair_sum(kept[i], got[i], "pair_sum_%d" % i) for i in range(3)]
    r_in, r_out, r_kv = _chip_exchange(chip_sums, "chip_exchange")

    g_w_in, dl_w_in, nm_w_in, nv_w_in = _adam(r_in, w_in[0], m_w_in[0], v_w_in[0], "adam_w_in")
    g_w_out, dl_w_out, nm_w_out, nv_w_out = _adam(r_out, w_out[0], m_w_out[0], v_w_out[0], "adam_w_out")
    g_w_kv, dl_w_kv, nm_w_kv, nv_w_kv = _adam(r_kv, w_mem_kv[0], m_w_mem_kv[0], v_w_mem_kv[0], "adam_w_kv")
    sm = [_unpack(t, small_shapes) for t in _adam(all_small, small_w, small_m, small_v, "adam_small")]

    conv_parts = lax.dynamic_slice(all_small, (0, all_small.shape[1] - (DN_K * 3 * DN_W) // LANE, 0),
                                   (N_DEV, (DN_K * 3 * DN_W) // LANE, LANE)).reshape(N_DEV, DN_K, 3 * DN_W)
    cshard = conv_w.shape[2]
    conv_parts = lax.dynamic_slice(conv_parts, (0, 0, me * cshard), (N_DEV, DN_K, cshard))
    cpad = ((0, 0), (0, HALO - DN_K), (0, 0))
    conv_res = _adam(jnp.pad(conv_parts, cpad), jnp.pad(conv_w[0], cpad[1:]), jnp.pad(m_conv_w[0], cpad[1:]),
                     jnp.pad(v_conv_w[0], cpad[1:], constant_values=1.0), "adam_conv")
    g_conv_s, dl_conv, nm_conv, nv_conv = [t[:DN_K][None] for t in conv_res]

    def group(idx, big_in, big_conv, big_kv, big_out):
        names = sm[idx]
        return [names[0], big_in[None], names[1], names[2], names[3], names[4], big_conv, names[5], names[6], names[7],
                names[8], big_kv[None], big_out[None], names[9]]

    grads = group(0, g_w_in, g_conv_s, g_w_kv, g_w_out)
    deltas = group(1, dl_w_in, dl_conv, dl_w_kv, dl_w_out)
    new_m = group(2, nm_w_in, nm_conv, nm_w_kv, nm_w_out)
    new_v = group(3, nv_w_in, nv_conv, nv_w_kv, nv_w_out)
    return (loss, grad_x[None], *grads, *deltas, *new_m, *new_v)
```

```python
import functools

import jax
import jax.numpy as jnp
from jax import lax
from jax.experimental import pallas as pl
from jax.experimental.pallas import tpu as pltpu

F32 = jnp.float32
BF16 = jnp.bfloat16
HIGHEST = lax.Precision.HIGHEST
MESH_ID = pl.DeviceIdType.MESH

N_DEV = 8
EPS = 1e-6
GMLP_W = 512
GMLP_G = 4
GMLP_T = 128
DN_W = 1024
DN_H = 8
HEAD = 128
DN_K = 4
CH = 64
XA_W = 512
XA_H = 4
LANE = 128
HALO = 8
MAIN_W = 4 * DN_W + 3 * GMLP_W + 2 * XA_W
MIX_W = DN_W + GMLP_W + XA_W
VMEM_LIMIT = 56 * 1024 * 1024

ADAM_LR = 0.001
ADAM_B1 = 0.9
ADAM_B2 = 0.999
ADAM_EPS = 1e-08
ADAM_WD = 0.01
ADAM_STEP = 10


def _sds(shape, dtype=F32):
    return jax.ShapeDtypeStruct(tuple(shape), dtype)


def _params(sem=None):
    if sem is None:
        return pltpu.CompilerParams(vmem_limit_bytes=VMEM_LIMIT)
    return pltpu.CompilerParams(dimension_semantics=tuple(sem), vmem_limit_bytes=VMEM_LIMIT)


def _tile(n, prefs):
    for p in prefs:
        if n % p == 0:
            return p
    return n


def _mm(a, b):
    return jnp.dot(a.astype(BF16), b.astype(BF16), preferred_element_type=F32)


def _mm_nt(a, b):
    return lax.dot_general(a.astype(BF16), b.astype(BF16), (((1,), (1,)), ((), ())), preferred_element_type=F32)


def _mm_tn(a, b):
    return lax.dot_general(a.astype(BF16), b.astype(BF16), (((0,), (0,)), ((), ())), preferred_element_type=F32)


def _mm_hi(a, b):
    return jnp.dot(a, b, precision=HIGHEST, preferred_element_type=F32)


def _mm_3x(a, b):
    return jnp.dot(a, b, precision=lax.Precision.HIGH, preferred_element_type=F32)


_GELU_C = 0.7978845608028654
_GELU_A = 0.044715


def _gelu(x):
    return 0.5 * x * (1.0 + jnp.tanh(_GELU_C * (x + _GELU_A * x * x * x)))


def _gelu_grad(x):
    t = jnp.tanh(_GELU_C * (x + _GELU_A * x * x * x))
    return 0.5 * (1.0 + t) + 0.5 * x * (1.0 - t * t) * _GELU_C * (1.0 + 3.0 * _GELU_A * x * x)


def _silu(x):
    return x * jax.nn.sigmoid(x)


def _silu_grad(x):
    s = jax.nn.sigmoid(x)
    return s * (1.0 + x * (1.0 - s))


def _rowsum(x):
    return jnp.sum(x, axis=-1, keepdims=True)


def _colsum(x):
    return jnp.sum(x, axis=0, keepdims=True)


def _iota2(shape, dim):
    return lax.broadcasted_iota(jnp.int32, shape, dim)


def _chunk_tri(tm, upper):
    r = _iota2((tm, tm), 0)
    c = _iota2((tm, tm), 1)
    same = lax.shift_right_logical(r, 6) == lax.shift_right_logical(c, 6)
    tri = (r <= c) if upper else (r >= c)
    return jnp.where(same & tri, 1.0, 0.0).astype(F32)


N_CHIP = 4


def _mesh_place():
    x, y, c = lax.axis_index("x"), lax.axis_index("y"), lax.axis_index("c")
    chips = [(1 - x, y), (x, 1 - y), (1 - x, 1 - y)]
    return x, y, c, (x, y, 1 - c), chips


def _gather_two_level(arrs, name):
    n = len(arrs)

    def body(*refs):
        ins = refs[:n]
        outs = refs[n:2 * n]
        send_sems, recv_sems, loc_sems = refs[2 * n:]
        x, y, c, sib, chips = _mesh_place()

        def copy(a, k, block, to, src=None):
            slot = outs[a].at[4 * block[0] + 2 * block[1] + block[2]]
            return pltpu.make_async_remote_copy(
                src_ref=slot if src is None else src, dst_ref=slot, send_sem=send_sems.at[a, k],
                recv_sem=recv_sems.at[a, k], device_id=to, device_id_type=MESH_ID)

        me = (x, y, c)
        sends, locs = [], []
        for a in range(n):
            own = pltpu.make_async_copy(ins[a], outs[a].at[4 * x + 2 * y + c], loc_sems.at[a])
            own.start()
            locs.append(own)
            first = [copy(a, 0, me, sib, src=ins[a])]
            first += [copy(a, 1 + j, me, (*chip, c), src=ins[a]) for j, chip in enumerate(chips)]
            for cp in first:
                cp.start()
            sends += first
        for a in range(n):
            for j, chip in enumerate(chips):
                copy(a, 1 + j, (*chip, c), me).wait_recv()
                passed = copy(a, 4 + j, (*chip, c), sib)
                passed.start()
                sends.append(passed)
        for a in range(n):
            copy(a, 0, sib, me).wait_recv()
            for j, chip in enumerate(chips):
                copy(a, 4 + j, (*chip, 1 - c), me).wait_recv()
        for cp in sends:
            cp.wait_send()
        for cp in locs:
            cp.wait()

    any_spec = pl.BlockSpec(memory_space=pl.ANY)
    return pl.pallas_call(
        body, name=name, out_shape=[_sds((N_DEV,) + a.shape, a.dtype) for a in arrs],
        in_specs=[any_spec] * n, out_specs=[any_spec] * n,
        scratch_shapes=[pltpu.SemaphoreType.DMA((n, N_DEV - 1)), pltpu.SemaphoreType.DMA((n, N_DEV - 1)),
                        pltpu.SemaphoreType.DMA((n,))],
        compiler_params=pltpu.CompilerParams(has_side_effects=True),
    )(*arrs)


def _swap_halves(small, grads, name):
    n = len(grads)

    def body(*refs):
        small_ref = refs[0]
        ins = refs[1:1 + n]
        small_out = refs[1 + n]
        got = refs[2 + n:2 + 2 * n]
        s_send, s_recv, g_send, g_recv, loc_sem = refs[2 + 2 * n:]
        x, y, c, sib, _ = _mesh_place()
        me = 4 * x + 2 * y + c
        sends, recvs = [], []
        for j in range(1, N_DEV):
            px = 1 - x if (j >> 2) & 1 else x
            py = 1 - y if (j >> 1) & 1 else y
            pc = 1 - c if j & 1 else c
            cp = pltpu.make_async_remote_copy(
                src_ref=small_ref, dst_ref=small_out.at[me], send_sem=s_send.at[j - 1], recv_sem=s_recv.at[j - 1],
                device_id=(px, py, pc), device_id_type=MESH_ID)
            cp.start()
            sends.append(cp)
            recvs.append(pltpu.make_async_remote_copy(
                src_ref=small_ref, dst_ref=small_out.at[4 * px + 2 * py + pc], send_sem=s_send.at[j - 1],
                recv_sem=s_recv.at[j - 1], device_id=(px, py, pc), device_id_type=MESH_ID))
        own = pltpu.make_async_copy(small_ref, small_out.at[me], loc_sem)
        own.start()
        for a in range(n):
            for chip in range(N_CHIP):
                cp = pltpu.make_async_remote_copy(
                    src_ref=ins[a].at[2 * chip + 1 - c], dst_ref=got[a].at[chip], send_sem=g_send.at[a, chip],
                    recv_sem=g_recv.at[a, chip], device_id=sib, device_id_type=MESH_ID)
                cp.start()
                sends.append(cp)
                recvs.append(cp)
        for cp in sends:
            cp.wait_send()
        for cp in recvs:
            cp.wait_recv()
        own.wait()

    half = [_sds((N_CHIP,) + g.shape[1:], g.dtype) for g in grads]
    any_spec = pl.BlockSpec(memory_space=pl.ANY)
    res = pl.pallas_call(
        body, name=name, out_shape=[_sds((N_DEV,) + small.shape, small.dtype)] + half,
        in_specs=[any_spec] * (1 + n), out_specs=[any_spec] * (1 + n),
        scratch_shapes=[pltpu.SemaphoreType.DMA((N_DEV - 1,)), pltpu.SemaphoreType.DMA((N_DEV - 1,)),
                        pltpu.SemaphoreType.DMA((n, N_CHIP)), pltpu.SemaphoreType.DMA((n, N_CHIP)),
                        pltpu.SemaphoreType.DMA],
        compiler_params=pltpu.CompilerParams(has_side_effects=True),
    )(small, *grads)
    return res[0], res[1:]


def _pair_sum(core, mine, got, name):
    nc, r, c = got.shape
    tr = _tile(r, (256, 128, 64, 32, 16))

    def body(core_ref, a_ref, b_ref, o_ref):
        o_ref[...] = (a_ref[...].astype(F32) + b_ref[...].astype(F32)).astype(BF16)

    return pl.pallas_call(
        body, name=name, out_shape=_sds(got.shape, BF16),
        grid_spec=pltpu.PrefetchScalarGridSpec(
            num_scalar_prefetch=1, grid=(nc, r // tr),
            in_specs=[pl.BlockSpec((1, tr, c), lambda i, j, core_ref: (2 * i + core_ref[0], j, 0)),
                      pl.BlockSpec((1, tr, c), lambda i, j, core_ref: (i, j, 0))],
            out_specs=pl.BlockSpec((1, tr, c), lambda i, j, core_ref: (i, j, 0))),
        compiler_params=_params(("parallel", "parallel")),
    )(core, mine, got)


def _chip_exchange(parts, name):
    n = len(parts)

    def body(*refs):
        ins = refs[:n]
        outs = refs[n:2 * n]
        send_sems, recv_sems, loc_sems = refs[2 * n:]
        x, y, c, _, chips = _mesh_place()
        mine = 2 * x + y
        sends, locs = [], []
        for a in range(n):
            own = pltpu.make_async_copy(ins[a].at[mine], outs[a].at[mine], loc_sems.at[a])
            own.start()
            locs.append(own)
            for j, chip in enumerate(chips):
                q = 2 * chip[0] + chip[1]
                cp = pltpu.make_async_remote_copy(
                    src_ref=ins[a].at[q], dst_ref=outs[a].at[mine], send_sem=send_sems.at[a, j], recv_sem=recv_sems.at[a, j],
                    device_id=(*chip, c), device_id_type=MESH_ID)
                cp.start()
                landed = pltpu.make_async_remote_copy(
                    src_ref=ins[a].at[q], dst_ref=outs[a].at[q], send_sem=send_sems.at[a, j], recv_sem=recv_sems.at[a, j],
                    device_id=(*chip, c), device_id_type=MESH_ID)
                sends.append((cp, landed))
        for cp, landed in sends:
            cp.wait_send()
            landed.wait_recv()
        for cp in locs:
            cp.wait()

    any_spec = pl.BlockSpec(memory_space=pl.ANY)
    return pl.pallas_call(
        body, name=name, out_shape=[_sds(p.shape, p.dtype) for p in parts],
        in_specs=[any_spec] * n, out_specs=[any_spec] * n,
        scratch_shapes=[pltpu.SemaphoreType.DMA((n, N_CHIP - 1)), pltpu.SemaphoreType.DMA((n, N_CHIP - 1)),
                        pltpu.SemaphoreType.DMA((n,))],
        compiler_params=pltpu.CompilerParams(has_side_effects=True),
    )(*parts)


def _inproj(x, ln_g, w_main, w_ab):
    s, d = x.shape
    n = w_main.shape[1]
    tm = _tile(s, (512, 256, 128))
    tn = _tile(n, (1664, 512, 128))

    def body(x_ref, g_ref, w_ref, wab_ref, proj_ref, ab_ref, h_ref, hs):
        @pl.when(pl.program_id(1) == 0)
        def _():
            xv = x_ref[...]
            r = lax.rsqrt(jnp.mean(xv * xv, axis=-1, keepdims=True) + EPS)
            h = (xv * r * g_ref[...]).astype(BF16)
            hs[...] = h
            h_ref[...] = h
            ab_ref[...] = jnp.dot(h, wab_ref[...], preferred_element_type=F32)

        proj_ref[...] = jnp.dot(hs[...], w_ref[...], preferred_element_type=F32)

    return pl.pallas_call(
        body, name="inproj", grid=(s // tm, n // tn),
        in_specs=[pl.BlockSpec((tm, d), lambda i, j: (i, 0)), pl.BlockSpec((1, d), lambda i, j: (0, 0)),
                  pl.BlockSpec((d, tn), lambda i, j: (0, j)), pl.BlockSpec((d, LANE), lambda i, j: (0, 0))],
        out_specs=[pl.BlockSpec((tm, tn), lambda i, j: (i, j)), pl.BlockSpec((tm, LANE), lambda i, j: (i, 0)),
                   pl.BlockSpec((tm, d), lambda i, j: (i, 0))],
        out_shape=[_sds((s, n)), _sds((s, LANE)), _sds((s, d), BF16)],
        scratch_shapes=[pltpu.VMEM((tm, d), BF16)],
        compiler_params=_params(("parallel", "arbitrary")),
    )(x, ln_g, w_main, w_ab)


def _matmul_tn(a, b, name):
    k, m = a.shape
    n = b.shape[1]
    tm = _tile(m, (1024, 512, 256, 128))
    tn = _tile(n, (512, 256, 128))
    tk = _tile(k, (1024, 512, 256, 128))

    def body(a_ref, b_ref, o_ref):
        @pl.when(pl.program_id(2) == 0)
        def _():
            o_ref[...] = jnp.zeros_like(o_ref)

        o_ref[...] += _mm_tn(a_ref[...], b_ref[...])

    return pl.pallas_call(
        body, name=name, grid=(m // tm, n // tn, k // tk),
        in_specs=[pl.BlockSpec((tk, tm), lambda i, j, l: (l, i)), pl.BlockSpec((tk, tn), lambda i, j, l: (l, j))],
        out_specs=pl.BlockSpec((tm, tn), lambda i, j, l: (i, j)),
        out_shape=_sds((m, n)),
        compiler_params=_params(("parallel", "parallel", "arbitrary")),
    )(a, b)


def _dh(pieces, weights):
    s = pieces[0].shape[0]
    d = weights[0].shape[0]
    npc = len(pieces)
    tm = _tile(s, (256, 128))
    tn = _tile(d, (1024, 512, 256, 128))

    def body(*refs):
        o_ref = refs[2 * npc]
        acc = _mm_nt(refs[0][...], refs[npc][...])
        for p in range(1, npc):
            acc += _mm_nt(refs[p][...], refs[npc + p][...])
        o_ref[...] = acc

    in_specs = [pl.BlockSpec((tm, p.shape[1]), lambda j, i: (i, 0)) for p in pieces]
    in_specs += [pl.BlockSpec((tn, w.shape[1]), lambda j, i: (j, 0)) for w in weights]
    return pl.pallas_call(
        body, name="dh", grid=(d // tn, s // tm), in_specs=in_specs,
        out_specs=pl.BlockSpec((tm, tn), lambda j, i: (i, j)), out_shape=_sds((s, d)),
        compiler_params=_params(("parallel", "parallel")),
    )(*pieces, *weights)


def _rms_bwd(x, dh, dx2, ln_g):
    s, d = x.shape
    tm = _tile(s, (256, 128))

    def body(x_ref, dh_ref, dx2_ref, g_ref, gx_ref, dg_ref):
        @pl.when(pl.program_id(0) == 0)
        def _():
            dg_ref[...] = jnp.zeros_like(dg_ref)

        xv = x_ref[...]
        r = lax.rsqrt(jnp.mean(xv * xv, axis=-1, keepdims=True) + EPS)
        xhat = xv * r
        dhv = dh_ref[...]
        dg_ref[...] += _colsum(dhv * xhat)
        dxh = dhv * g_ref[...]
        gx_ref[...] = dx2_ref[...] + r * (dxh - xhat * jnp.mean(dxh * xhat, axis=-1, keepdims=True))

    row = pl.BlockSpec((tm, d), lambda i: (i, 0))
    vec = pl.BlockSpec((1, d), lambda i: (0, 0))
    return pl.pallas_call(
        body, name="rms_bwd", grid=(s // tm,), in_specs=[row, row, row, vec], out_specs=[row, vec],
        out_shape=[_sds((s, d)), _sds((1, d))], compiler_params=_params(("arbitrary",)),
    )(x, dh, dx2, ln_g)


def _final(x, tgt, out_b, out_a, out_c, w_out, final_g):
    s, d = x.shape
    tm = _tile(s, (256, 128))

    def body(x_ref, t_ref, b_ref, a_ref, c_ref, w_ref, g_ref, dx2_ref, dx2b_ref, dm_ref, loss_ref, dg_ref):
        @pl.when(pl.program_id(0) == 0)
        def _():
            loss_ref[...] = jnp.zeros_like(loss_ref)
            dg_ref[...] = jnp.zeros_like(dg_ref)

        x2 = x_ref[...]
        x2 += jnp.dot(b_ref[...], w_ref[0:DN_W, :], preferred_element_type=F32)
        x2 += jnp.dot(a_ref[...], w_ref[DN_W:DN_W + GMLP_W, :], preferred_element_type=F32)
        x2 += jnp.dot(c_ref[...], w_ref[DN_W + GMLP_W:MIX_W, :], preferred_element_type=F32)
        r = lax.rsqrt(jnp.mean(x2 * x2, axis=-1, keepdims=True) + EPS)
        xhat = x2 * r
        g = g_ref[...]
        err = xhat * g - t_ref[...]
        tok = 0.5 * jnp.mean(err * err, axis=-1, keepdims=True)
        loss_ref[...] += jnp.broadcast_to(_colsum(tok), loss_ref.shape)
        dy = err * (1.0 / d)
        dg_ref[...] += _colsum(dy * xhat)
        dxh = dy * g
        dx2 = r * (dxh - xhat * jnp.mean(dxh * xhat, axis=-1, keepdims=True))
        dx2_ref[...] = dx2
        dx2b = dx2.astype(BF16)
        dx2b_ref[...] = dx2b
        dm_ref[...] = _mm_nt(dx2b, w_ref[...])

    row = pl.BlockSpec((tm, d), lambda i: (i, 0))
    vec = pl.BlockSpec((1, d), lambda i: (0, 0))
    return pl.pallas_call(
        body, name="final", grid=(s // tm,),
        in_specs=[row, row, pl.BlockSpec((tm, DN_W), lambda i: (i, 0)), pl.BlockSpec((tm, GMLP_W), lambda i: (i, 0)),
                  pl.BlockSpec((tm, XA_W), lambda i: (i, 0)), pl.BlockSpec((MIX_W, d), lambda i: (0, 0)), vec],
        out_specs=[row, row, pl.BlockSpec((tm, MIX_W), lambda i: (i, 0)), pl.BlockSpec((1, LANE), lambda i: (0, 0)), vec],
        out_shape=[_sds((s, d)), _sds((s, d), BF16), _sds((s, MIX_W)), _sds((1, LANE)), _sds((1, d))],
        compiler_params=_params(("arbitrary",)),
    )(x, tgt, out_b, out_a, out_c, w_out, final_g)


GU_BLK = (4 * DN_W) // GMLP_W


def _gmlp_norm(gv, lng, lnb):
    va = _gelu(gv)
    mu = jnp.mean(va, axis=-1, keepdims=True)
    xc = va - mu
    rstd = lax.rsqrt(jnp.mean(xc * xc, axis=-1, keepdims=True) + EPS)
    vhat = xc * rstd
    return vhat, rstd, vhat * lng + lnb


def _gmlp_fwd(proj, lng, lnb, ws, bs_t):
    s = proj.shape[0]
    tm = _tile(s, (512, 256, 128))

    def body(u_ref, v_ref, z_ref, lng_ref, lnb_ref, ws_ref, bst_ref, o_ref):
        _, _, vn = _gmlp_norm(v_ref[...], lng_ref[...], lnb_ref[...])
        tri = _iota2((GMLP_T, GMLP_T), 0) >= _iota2((GMLP_T, GMLP_T), 1)
        for g in range(GMLP_G):
            cs = slice(g * HEAD, (g + 1) * HEAD)
            w = jnp.where(tri, ws_ref[g], 0.0).astype(BF16)
            b = bst_ref[:, g:g + 1]
            for c in range(tm // GMLP_T):
                rs = slice(c * GMLP_T, (c + 1) * GMLP_T)
                sg = _mm(w, vn[rs, cs]) + b
                o_ref[rs, cs] = (_gelu(u_ref[rs, cs]) * sg * _silu(z_ref[rs, cs])).astype(BF16)

    col = lambda k: pl.BlockSpec((tm, GMLP_W), lambda i: (i, GU_BLK + k))
    vec = pl.BlockSpec((1, GMLP_W), lambda i: (0, 0))
    return pl.pallas_call(
        body, name="gmlp_fwd", grid=(s // tm,),
        in_specs=[col(0), col(1), col(2), vec, vec, pl.BlockSpec((GMLP_G, GMLP_T, GMLP_T), lambda i: (0, 0, 0)),
                  pl.BlockSpec((GMLP_T, GMLP_G), lambda i: (0, 0))],
        out_specs=pl.BlockSpec((tm, GMLP_W), lambda i: (i, 0)), out_shape=_sds((s, GMLP_W), BF16),
        compiler_params=_params(("parallel",)),
    )(proj, proj, proj, lng, lnb, ws, bs_t)


def _gmlp_bwd(proj, dmixed, lng, lnb, ws, bs_t):
    s = proj.shape[0]
    tm = _tile(s, (512, 256, 128))

    def body(u_ref, v_ref, z_ref, d_ref, lng_ref, lnb_ref, ws_ref, bst_ref,
             dp_ref, dws_ref, dbst_ref, dlng_ref, dlnb_ref, dvn):
        @pl.when(pl.program_id(0) == 0)
        def _():
            dws_ref[...] = jnp.zeros_like(dws_ref)
            dbst_ref[...] = jnp.zeros_like(dbst_ref)
            dlng_ref[...] = jnp.zeros_like(dlng_ref)
            dlnb_ref[...] = jnp.zeros_like(dlnb_ref)

        gv = v_ref[...]
        lng_v = lng_ref[...]
        vhat, rstd, vn = _gmlp_norm(gv, lng_v, lnb_ref[...])
        tri = _iota2((GMLP_T, GMLP_T), 0) >= _iota2((GMLP_T, GMLP_T), 1)
        for g in range(GMLP_G):
            cs = slice(g * HEAD, (g + 1) * HEAD)
            w = jnp.where(tri, ws_ref[g], 0.0).astype(BF16)
            b = bst_ref[:, g:g + 1]
            dw_acc = jnp.zeros((GMLP_T, GMLP_T), F32)
            db_acc = jnp.zeros((GMLP_T, 1), F32)
            for c in range(tm // GMLP_T):
                rs = slice(c * GMLP_T, (c + 1) * GMLP_T)
                vn_b = vn[rs, cs]
                sg = _mm(w, vn_b) + b
                gu = u_ref[rs, cs]
                gz = z_ref[rs, cs]
                da = d_ref[rs, cs]
                uact = _gelu(gu)
                sz = _silu(gz)
                ds = da * uact * sz
                dp_ref[rs, cs] = (da * sg * sz * _gelu_grad(gu)).astype(BF16)
                dp_ref[rs, 2 * GMLP_W + g * HEAD:2 * GMLP_W + (g + 1) * HEAD] = (da * uact * sg * _silu_grad(gz)).astype(BF16)
                dw_acc += _mm_nt(ds, vn_b)
                db_acc += _rowsum(ds)
                dvn[rs, cs] = _mm_tn(w, ds)
            dws_ref[g] += jnp.where(tri, dw_acc, 0.0)
            dbst_ref[:, g:g + 1] += db_acc
        dvn_v = dvn[...]
        dlng_ref[...] += _colsum(dvn_v * vhat)
        dlnb_ref[...] += _colsum(dvn_v)
        dvh = dvn_v * lng_v
        dva = rstd * (dvh - jnp.mean(dvh, axis=-1, keepdims=True) - vhat * jnp.mean(dvh * vhat, axis=-1, keepdims=True))
        dp_ref[:, GMLP_W:2 * GMLP_W] = (dva * _gelu_grad(gv)).astype(BF16)

    col = lambda k: pl.BlockSpec((tm, GMLP_W), lambda i: (i, GU_BLK + k))
    vec = pl.BlockSpec((1, GMLP_W), lambda i: (0, 0))
    wsp = pl.BlockSpec((GMLP_G, GMLP_T, GMLP_T), lambda i: (0, 0, 0))
    bsp = pl.BlockSpec((GMLP_T, GMLP_G), lambda i: (0, 0))
    return pl.pallas_call(
        body, name="gmlp_bwd", grid=(s // tm,),
        in_specs=[col(0), col(1), col(2), pl.BlockSpec((tm, GMLP_W), lambda i: (i, DN_W // GMLP_W)), vec, vec, wsp, bsp],
        out_specs=[pl.BlockSpec((tm, 3 * GMLP_W), lambda i: (i, 0)), wsp, bsp, vec, vec],
        out_shape=[_sds((s, 3 * GMLP_W), BF16), _sds((GMLP_G, GMLP_T, GMLP_T)), _sds((GMLP_T, GMLP_G)),
                   _sds((1, GMLP_W)), _sds((1, GMLP_W))],
        scratch_shapes=[pltpu.VMEM((tm, GMLP_W), F32)],
        compiler_params=_params(("arbitrary",)),
    )(proj, proj, proj, dmixed, lng, lnb, ws, bs_t)


CQ_BLK = (4 * DN_W + 3 * GMLP_W) // XA_W


def _memkv_fwd(mem, g, w_kv):
    nm, d = mem.shape

    def body(m_ref, g_ref, w_ref, kv_ref):
        mv = m_ref[...]
        r = lax.rsqrt(jnp.mean(mv * mv, axis=-1, keepdims=True) + EPS)
        kv_ref[...] = _mm(mv * r * g_ref[...], w_ref[...])

    return pl.pallas_call(body, name="memkv_fwd", out_shape=_sds((nm, 2 * XA_W)), compiler_params=_params())(mem, g, w_kv)


def _memkv_bwd(mem, g, w_kv, dkv):
    nm, d = mem.shape

    def body(m_ref, g_ref, w_ref, dkv_ref, dw_ref, dg_ref):
        mv = m_ref[...]
        r = lax.rsqrt(jnp.mean(mv * mv, axis=-1, keepdims=True) + EPS)
        xhat = mv * r
        dkv_v = dkv_ref[...]
        dw_ref[...] = _mm_tn(xhat * g_ref[...], dkv_v)
        dg_ref[...] = _colsum(_mm_nt(dkv_v, w_ref[...]) * xhat)

    return pl.pallas_call(body, name="memkv_bwd", out_shape=[_sds((d, 2 * XA_W)), _sds((1, d))],
                          compiler_params=_params())(mem, g, w_kv, dkv)


def _xattn_probs(q, mk):
    sc = _mm_nt(q, mk) * (HEAD ** -0.5)
    e = jnp.exp(sc - jnp.max(sc, axis=-1, keepdims=True))
    return e / _rowsum(e)


def _xattn_fwd(proj, mkv):
    s = proj.shape[0]
    nm = mkv.shape[0]
    tm = _tile(s, (512, 256, 128))

    def body(q_ref, z_ref, kv_ref, o_ref):
        for h in range(XA_H):
            cs = slice(h * HEAD, (h + 1) * HEAD)
            p = _xattn_probs(q_ref[:, cs], kv_ref[:, cs])
            ctx = _mm(p, kv_ref[:, XA_W + h * HEAD:XA_W + (h + 1) * HEAD])
            o_ref[:, cs] = (ctx * _silu(z_ref[:, cs])).astype(BF16)

    col = lambda k: pl.BlockSpec((tm, XA_W), lambda i: (i, CQ_BLK + k))
    return pl.pallas_call(
        body, name="xattn_fwd", grid=(s // tm,),
        in_specs=[col(0), col(1), pl.BlockSpec((nm, 2 * XA_W), lambda i: (0, 0))],
        out_specs=pl.BlockSpec((tm, XA_W), lambda i: (i, 0)), out_shape=_sds((s, XA_W), BF16),
        compiler_params=_params(("parallel",)),
    )(proj, proj, mkv)


def _xattn_bwd(proj, dmixed, mkv):
    s = proj.shape[0]
    nm = mkv.shape[0]
    tm = _tile(s, (512, 256, 128))

    def body(q_ref, z_ref, d_ref, kv_ref, dp_ref, dkv_ref):
        @pl.when(pl.program_id(0) == 0)
        def _():
            dkv_ref[...] = jnp.zeros_like(dkv_ref)

        for h in range(XA_H):
            cs = slice(h * HEAD, (h + 1) * HEAD)
            vs = slice(XA_W + h * HEAD, XA_W + (h + 1) * HEAD)
            q = q_ref[:, cs]
            z = z_ref[:, cs]
            mk = kv_ref[:, cs]
            mv = kv_ref[:, vs]
            p = _xattn_probs(q, mk)
            ctx = _mm(p, mv)
            dc = d_ref[:, cs]
            dctx = dc * _silu(z)
            dp_ref[:, vs] = (dc * ctx * _silu_grad(z)).astype(BF16)
            dp = _mm_nt(dctx, mv)
            dkv_ref[:, vs] += _mm_tn(p, dctx)
            ds = p * (dp - _rowsum(dp * p)) * (HEAD ** -0.5)
            dp_ref[:, cs] = _mm(ds, mk).astype(BF16)
            dkv_ref[:, cs] += _mm_tn(ds, q)

    col = lambda k: pl.BlockSpec((tm, XA_W), lambda i: (i, CQ_BLK + k))
    kvs = pl.BlockSpec((nm, 2 * XA_W), lambda i: (0, 0))
    return pl.pallas_call(
        body, name="xattn_bwd", grid=(s // tm,),
        in_specs=[col(0), col(1), pl.BlockSpec((tm, XA_W), lambda i: (i, (DN_W + GMLP_W) // XA_W)), kvs],
        out_specs=[pl.BlockSpec((tm, 2 * XA_W), lambda i: (i, 0)), kvs],
        out_shape=[_sds((s, 2 * XA_W), BF16), _sds((nm, 2 * XA_W))],
        compiler_params=_params(("arbitrary",)),
    )(proj, proj, dmixed, mkv)


def _softplus(x):
    return jnp.maximum(x, 0.0) + jnp.log1p(jnp.exp(-jnp.abs(x)))


def _dn_pre(proj, ab, conv_w, alog_row, dt_row):
    s = proj.shape[0]
    tm = _tile(s, (256, 128))
    w3 = 3 * DN_W

    def body(x_ref, halo_ref, ab_ref, cw_ref, al_ref, dt_ref, q_ref, k_ref, v_ref, gb_ref, gbt_ref, ext):
        i = pl.program_id(0)
        ext[0:HALO, :] = jnp.where(i > 0, halo_ref[...], 0.0)
        ext[HALO:HALO + tm, :] = x_ref[...]
        yc = cw_ref[0:1, :] * ext[pl.ds(HALO - DN_K + 1, tm), :]
        for t in range(1, DN_K):
            yc += cw_ref[t:t + 1, :] * ext[pl.ds(HALO - DN_K + 1 + t, tm), :]
        act = _silu(yc)
        for h in range(DN_H):
            cs = slice(h * HEAD, (h + 1) * HEAD)
            qa = act[:, cs]
            q_ref[:, cs] = qa * (lax.rsqrt(_rowsum(qa * qa) + EPS) * (HEAD ** -0.5))
            ka = act[:, DN_W + h * HEAD:DN_W + (h + 1) * HEAD]
            k_ref[:, cs] = ka * lax.rsqrt(_rowsum(ka * ka) + EPS)
        v_ref[...] = act[:, 2 * DN_W:w3]
        abv = ab_ref[...]
        lane = _iota2((tm, LANE), 1)
        g = jnp.where(lane < DN_H, -jnp.exp(al_ref[...]) * _softplus(abv + dt_ref[...]), 0.0)
        gc = _mm_hi(_chunk_tri(tm, False), g)
        gbv = jnp.where(lane < DN_H, gc, jnp.where(lane < 2 * DN_H, jax.nn.sigmoid(abv), 0.0))
        gb_ref[...] = gbv
        for c in range(tm // CH):
            gbt_ref[c] = gbv[c * CH:(c + 1) * CH, :].T[0:2 * DN_H, :]

    hb = tm // HALO
    row = lambda w: pl.BlockSpec((tm, w), lambda i: (i, 0))
    vec = pl.BlockSpec((1, LANE), lambda i: (0, 0))
    return pl.pallas_call(
        body, name="dn_pre", grid=(s // tm,),
        in_specs=[row(w3), pl.BlockSpec((HALO, w3), lambda i: (jnp.maximum(i * hb - 1, 0), 0)), row(LANE),
                  pl.BlockSpec((DN_K, w3), lambda i: (0, 0)), vec, vec],
        out_specs=[row(DN_W), row(DN_W), row(DN_W), row(LANE), pl.BlockSpec((tm // CH, 2 * DN_H, CH), lambda i: (i, 0, 0))],
        out_shape=[_sds((s, DN_W)), _sds((s, DN_W)), _sds((s, DN_W)), _sds((s, LANE)), _sds((s // CH, 2 * DN_H, CH))],
        scratch_shapes=[pltpu.VMEM((tm + HALO, w3), F32)],
        compiler_params=_params(("parallel",)),
    )(proj, proj, ab, conv_w, alog_row, dt_row)


HEADS = tuple(range(DN_H))


def _hcols(h):
    return slice(h * HEAD, (h + 1) * HEAD)


def _chunk_scalings(k, v, gbv, gbt, h):
    gc = gbv[:, h:h + 1]
    beta = gbv[:, DN_H + h:DN_H + h + 1]
    gr = gbt[h:h + 1, :]
    ii = _iota2((CH, CH), 0)
    jj = _iota2((CH, CH), 1)
    dec = jnp.exp(jnp.where(ii >= jj, gc - gr, -1e30))
    eg = jnp.exp(gc)
    gl = gr[:, CH - 1:CH]
    kb = k * beta
    return dict(beta=beta, dec=dec, eg=eg, gl=gl, ekd=jnp.exp(gl - gc), kb=kb, vb=v * beta, kbe=kb * eg)


def _chunk_scores(m, q, k):
    kq = _mm_nt(jnp.concatenate([m["kb"], q], axis=0), k)
    strict = _iota2((CH, CH), 0) > _iota2((CH, CH), 1)
    return jnp.where(strict, kq[0:CH] * m["dec"], 0.0), kq[CH:2 * CH] * m["dec"]


def _dn_local(q, k, v, gb, gbt):
    s = q.shape[0]
    cpb = 4 if (s // CH) % 4 == 0 else 1
    tb = cpb * CH
    nblk = s // tb

    def body(q_ref, k_ref, v_ref, gb_ref, gbt_ref, u_ref, w_ref, qg_ref, kd_ref, t_ref, ai_ref, egl_ref):
        def chunk(c, carry):
            r0 = pl.multiple_of(c * CH, CH)
            rows = pl.ds(r0, CH)
            gbv = gb_ref[rows, :]
            gbt_v = gbt_ref[c]
            qs = [q_ref[rows, _hcols(h)] for h in HEADS]
            ks = [k_ref[rows, _hcols(h)] for h in HEADS]
            ms = [_chunk_scalings(ks[h], v_ref[rows, _hcols(h)], gbv, gbt_v, h) for h in HEADS]
            for h in HEADS:
                qg_ref[rows, _hcols(h)] = (qs[h] * ms[h]["eg"]).astype(BF16)
                kd_ref[rows, _hcols(h)] = (ks[h] * ms[h]["ekd"]).astype(BF16)
                egl_ref[c, h:h + 1, :] = jnp.broadcast_to(jnp.exp(ms[h]["gl"]), (1, LANE))
            sc = [_chunk_scores(ms[h], qs[h], ks[h]) for h in HEADS]
            for h in HEADS:
                ai_ref[h, rows, :] = sc[h][1]
            eye = jnp.where(_iota2((CH, CH), 0) == _iota2((CH, CH), 1), 1.0, 0.0).astype(F32)
            ts = [eye - sc[h][0] for h in HEADS]
            ps = [_mm_3x(sc[h][0], sc[h][0]) for h in HEADS]
            for step in range(5):
                ts = [ts[h] + _mm_3x(ts[h], ps[h]) for h in HEADS]
                if step < 4:
                    ps = [_mm_3x(ps[h], ps[h]) for h in HEADS]
            for h in HEADS:
                t_ref[h, rows, :] = ts[h]
                uw = _mm(ts[h], jnp.concatenate([ms[h]["vb"], ms[h]["kbe"]], axis=1))
                u_ref[rows, _hcols(h)] = uw[:, 0:HEAD]
                w_ref[rows, _hcols(h)] = uw[:, HEAD:2 * HEAD].astype(BF16)
            return carry

        lax.fori_loop(0, cpb, chunk, 0)

    row = pl.BlockSpec((tb, DN_W), lambda i: (i, 0))
    sq = pl.BlockSpec((DN_H, tb, CH), lambda i: (0, i, 0))
    return pl.pallas_call(
        body, name="dn_local", grid=(nblk,),
        in_specs=[row, row, row, pl.BlockSpec((tb, LANE), lambda i: (i, 0)),
                  pl.BlockSpec((cpb, 2 * DN_H, CH), lambda i: (i, 0, 0))],
        out_specs=[row, row, row, row, sq, sq, pl.BlockSpec((cpb, DN_H, LANE), lambda i: (i, 0, 0))],
        out_shape=[_sds((s, DN_W)), _sds((s, DN_W), BF16), _sds((s, DN_W), BF16), _sds((s, DN_W), BF16),
                   _sds((DN_H, s, CH)), _sds((DN_H, s, CH)), _sds((s // CH, DN_H, LANE))],
        compiler_params=_params(("parallel",)),
    )(q, k, v, gb, gbt)


def _scan_cpb(s):
    return 8 if (s // CH) % 8 == 0 else 1


def _dn_scan(u, w, qg, kd, ai, egl, proj, norm_g):
    s = u.shape[0]
    cpb = _scan_cpb(s)
    tb = cpb * CH
    nblk = s // tb

    def body(u_ref, w_ref, qg_ref, kd_ref, ai_ref, egl_ref, z_ref, ng_ref, o_ref, vn_ref, st_ref, ob_ref, state):
        @pl.when(pl.program_id(0) == 0)
        def _():
            state[...] = jnp.zeros_like(state)

        ng = ng_ref[...]

        def chunk(c, carry):
            r0 = pl.multiple_of(c * CH, CH)
            rows = pl.ds(r0, CH)
            sts = [state[h] for h in HEADS]
            stb = [sts[h].astype(BF16) for h in HEADS]
            for h in HEADS:
                st_ref[c, h] = sts[h]
            vns = [u_ref[rows, _hcols(h)] - jnp.dot(w_ref[rows, _hcols(h)], stb[h], preferred_element_type=F32)
                   for h in HEADS]
            vnb = [vns[h].astype(BF16) for h in HEADS]
            for h in HEADS:
                state[h] = sts[h] * egl_ref[c, h:h + 1, :] + _mm_tn(kd_ref[rows, _hcols(h)], vnb[h])
            os_ = [jnp.dot(qg_ref[rows, _hcols(h)], stb[h], preferred_element_type=F32) + _mm(ai_ref[h, rows, :], vnb[h])
                   for h in HEADS]
            for h in HEADS:
                o = os_[h]
                vn_ref[rows, _hcols(h)] = vns[h]
                o_ref[rows, _hcols(h)] = o
                r = lax.rsqrt(jnp.mean(o * o, axis=-1, keepdims=True) + EPS)
                ob_ref[rows, _hcols(h)] = (o * r * ng * _silu(z_ref[rows, _hcols(h)])).astype(BF16)
            return carry

        lax.fori_loop(0, cpb, chunk, 0)

    row = pl.BlockSpec((tb, DN_W), lambda i: (i, 0))
    return pl.pallas_call(
        body, name="dn_scan", grid=(nblk,),
        in_specs=[row, row, row, row, pl.BlockSpec((DN_H, tb, CH), lambda i: (0, i, 0)),
                  pl.BlockSpec((cpb, DN_H, LANE), lambda i: (i, 0, 0)), pl.BlockSpec((tb, DN_W), lambda i: (i, 3)),
                  pl.BlockSpec((1, HEAD), lambda i: (0, 0))],
        out_specs=[row, row, pl.BlockSpec((cpb, DN_H, HEAD, HEAD), lambda i: (i, 0, 0, 0)), row],
        out_shape=[_sds((s, DN_W)), _sds((s, DN_W)), _sds((s // CH, DN_H, HEAD, HEAD)), _sds((s, DN_W), BF16)],
        scratch_shapes=[pltpu.VMEM((DN_H, HEAD, HEAD), F32)],
        compiler_params=_params(("arbitrary",)),
    )(u, w, qg, kd, ai, egl, proj, norm_g)


def _dn_scan_bwd(dmixed, o, proj, norm_g, w, qg, kd, ai, egl):
    s = o.shape[0]
    cpb = _scan_cpb(s)
    tb = cpb * CH
    nblk = s // tb

    def body(dm_ref, o_ref, z_ref, ng_ref, w_ref, qg_ref, kd_ref, ai_ref, egl_ref,
             do_ref, dvn_ref, dst_ref, dz_ref, dng_ref, dstate):
        @pl.when(pl.program_id(0) == 0)
        def _():
            dstate[...] = jnp.zeros_like(dstate)
            dng_ref[...] = jnp.zeros_like(dng_ref)

        ng = ng_ref[...]

        def chunk(cc, carry):
            c = cpb - 1 - cc
            r0 = pl.multiple_of(c * CH, CH)
            rows = pl.ds(r0, CH)
            dng = jnp.zeros((1, HEAD), F32)
            dob = []
            for h in HEADS:
                cs = _hcols(h)
                o = o_ref[rows, cs]
                z = z_ref[rows, cs]
                db = dm_ref[rows, cs]
                r = lax.rsqrt(jnp.mean(o * o, axis=-1, keepdims=True) + EPS)
                ohat = o * r
                dz_ref[rows, cs] = (db * ohat * ng * _silu_grad(z)).astype(BF16)
                dyn = db * _silu(z)
                dng += _colsum(dyn * ohat)
                doh = dyn * ng
                do = r * (doh - ohat * jnp.mean(doh * ohat, axis=-1, keepdims=True))
                do_ref[rows, cs] = do
                dob.append(do.astype(BF16))
            dng_ref[...] += dng
            dsn = [dstate[h] for h in HEADS]
            for h in HEADS:
                dst_ref[c, h] = dsn[h]
            dvn = [_mm_tn(ai_ref[h, rows, :], dob[h])
                   + jnp.dot(kd_ref[rows, _hcols(h)], dsn[h].astype(BF16), preferred_element_type=F32) for h in HEADS]
            part = [_mm_tn(qg_ref[rows, _hcols(h)], dob[h]) + egl_ref[c, h:h + 1, :] * dsn[h] for h in HEADS]
            for h in HEADS:
                dvn_ref[rows, _hcols(h)] = dvn[h]
                dstate[h] = part[h] - _mm_tn(w_ref[rows, _hcols(h)], dvn[h])
            return carry

        lax.fori_loop(0, cpb, chunk, 0)

    rev = lambda i: (nblk - 1 - i, 0)
    row = pl.BlockSpec((tb, DN_W), rev)
    vec = pl.BlockSpec((1, HEAD), lambda i: (0, 0))
    return pl.pallas_call(
        body, name="dn_scan_bwd", grid=(nblk,),
        in_specs=[row, row, pl.BlockSpec((tb, DN_W), lambda i: (nblk - 1 - i, 3)), vec, row, row, row,
                  pl.BlockSpec((DN_H, tb, CH), lambda i: (0, nblk - 1 - i, 0)),
                  pl.BlockSpec((cpb, DN_H, LANE), lambda i: (nblk - 1 - i, 0, 0))],
        out_specs=[row, row, pl.BlockSpec((cpb, DN_H, HEAD, HEAD), lambda i: (nblk - 1 - i, 0, 0, 0)), row, vec],
        out_shape=[_sds((s, DN_W)), _sds((s, DN_W)), _sds((s // CH, DN_H, HEAD, HEAD)), _sds((s, DN_W), BF16),
                   _sds((1, HEAD))],
        scratch_shapes=[pltpu.VMEM((DN_H, HEAD, HEAD), F32)],
        compiler_params=_params(("arbitrary",)),
    )(dmixed, o, proj, norm_g, w, qg, kd, ai, egl)


def _dn_local_bwd(q, k, v, gb, gbt, t, vn, st, dst, do, dvn):
    s = q.shape[0]
    cpb = 4 if (s // CH) % 4 == 0 else 1
    tb = cpb * CH
    nblk = s // tb

    def body(q_ref, k_ref, v_ref, gb_ref, gbt_ref, t_ref, vn_ref, st_ref, dst_ref, do_ref, dvn_ref,
             dq_ref, dk_ref, dv_ref, dgb_ref):
        lane = _iota2((CH, LANE), 1)
        last = _iota2((CH, 1), 0) == CH - 1

        def chunk(c, carry):
            r0 = pl.multiple_of(c * CH, CH)
            rows = pl.ds(r0, CH)
            gbv = gb_ref[rows, :]
            gbt_v = gbt_ref[c]
            strict = _iota2((CH, CH), 0) > _iota2((CH, CH), 1)
            qs = [q_ref[rows, _hcols(h)] for h in HEADS]
            ks = [k_ref[rows, _hcols(h)] for h in HEADS]
            vs = [v_ref[rows, _hcols(h)] for h in HEADS]
            ms = [_chunk_scalings(ks[h], vs[h], gbv, gbt_v, h) for h in HEADS]
            sts = [st_ref[c, h] for h in HEADS]
            dsn = [dst_ref[c, h] for h in HEADS]
            dob = [do_ref[rows, _hcols(h)].astype(BF16) for h in HEADS]
            dvnb = [dvn_ref[rows, _hcols(h)].astype(BF16) for h in HEADS]
            vnb = [vn_ref[rows, _hcols(h)].astype(BF16) for h in HEADS]
            tbf = [t_ref[h, rows, :].astype(BF16) for h in HEADS]
            sc = [_chunk_scores(ms[h], qs[h], ks[h]) for h in HEADS]
            xs_ = [_mm_nt(jnp.concatenate([dob[h], dvnb[h]], axis=0), sts[h]) for h in HEADS]
            dai = [_mm_nt(dob[h], vnb[h]) for h in HEADS]
            dkd = [_mm_nt(vnb[h], dsn[h]) for h in HEADS]
            dqg = [xs_[h][0:CH] for h in HEADS]
            duw = [jnp.concatenate([dvnb[h], (-xs_[h][CH:2 * CH]).astype(BF16)], axis=1) for h in HEADS]
            dt = [_mm_nt(duw[h], jnp.concatenate([ms[h]["vb"], ms[h]["kbe"]], axis=1)) for h in HEADS]
            dvk = [_mm_tn(tbf[h], duw[h]) for h in HEADS]
            tdt = [_mm_tn(tbf[h], dt[h]) for h in HEADS]
            da = [jnp.where(strict, -_mm_nt(tdt[h], tbf[h]), 0.0) for h in HEADS]
            dsc = [jnp.concatenate([da[h] * ms[h]["dec"], dai[h] * ms[h]["dec"]], axis=0) for h in HEADS]
            dkq = [_mm(dsc[h], ks[h]) for h in HEADS]
            dk1 = [_mm_tn(dsc[h], jnp.concatenate([ms[h]["kb"], qs[h]], axis=0)) for h in HEADS]
            dgb = jnp.zeros((CH, LANE), F32)
            for h in HEADS:
                m = ms[h]
                eg, ekd, beta = m["eg"], m["ekd"], m["beta"]
                dvb = dvk[h][:, 0:HEAD]
                dkbe = dvk[h][:, HEAD:2 * HEAD]
                kd = ks[h] * ekd
                dkb = dkq[h][0:CH] + dkbe * eg
                dq_ref[rows, _hcols(h)] = dkq[h][CH:2 * CH] + dqg[h] * eg
                dk_ref[rows, _hcols(h)] = dk1[h] + dkd[h] * ekd + dkb * beta
                dv_ref[rows, _hcols(h)] = dvb * beta
                dgl = jnp.exp(m["gl"]) * _colsum(_rowsum(sts[h] * dsn[h])) + _colsum(_rowsum(dkd[h] * kd))
                mm_ = da[h] * sc[h][0] + dai[h] * sc[h][1]
                dgc = (_rowsum(mm_) - _rowsum(mm_.T) + _rowsum(dqg[h] * qs[h] * eg) - _rowsum(dkd[h] * kd)
                       + _rowsum(dkbe * m["kbe"]) + jnp.where(last, dgl, 0.0))
                dbeta = _rowsum(dkb * ks[h]) + _rowsum(dvb * vs[h])
                dgb = jnp.where(lane == h, dgc, jnp.where(lane == DN_H + h, dbeta, dgb))
            dgb_ref[rows, :] = dgb
            return carry

        lax.fori_loop(0, cpb, chunk, 0)

    row = pl.BlockSpec((tb, DN_W), lambda i: (i, 0))
    gbs = pl.BlockSpec((tb, LANE), lambda i: (i, 0))
    sts = pl.BlockSpec((cpb, DN_H, HEAD, HEAD), lambda i: (i, 0, 0, 0))
    return pl.pallas_call(
        body, name="dn_local_bwd", grid=(nblk,),
        in_specs=[row, row, row, gbs, pl.BlockSpec((cpb, 2 * DN_H, CH), lambda i: (i, 0, 0)),
                  pl.BlockSpec((DN_H, tb, CH), lambda i: (0, i, 0)), row, sts, sts, row, row],
        out_specs=[row, row, row, gbs],
        out_shape=[_sds((s, DN_W)), _sds((s, DN_W)), _sds((s, DN_W)), _sds((s, LANE))],
        compiler_params=_params(("parallel",)),
    )(q, k, v, gb, gbt, t, vn, st, dst, do, dvn)


def _dn_pre_bwd(proj, ab, conv_w, alog_row, dt_row, dq, dk, dv, dgb):
    s = proj.shape[0]
    tm = _tile(s, (256, 128))
    w3 = 3 * DN_W
    nblk = s // tm

    def body(x_ref, halo_ref, ab_ref, cw_ref, al_ref, dt_ref, dq_ref, dk_ref, dv_ref, dgb_ref,
             dx_ref, dab_ref, dcw_ref, dal_ref, ddt_ref, ext, exd, carry):
        i = pl.program_id(0)

        @pl.when(i == 0)
        def _():
            carry[...] = jnp.zeros_like(carry)
            dcw_ref[...] = jnp.zeros_like(dcw_ref)
            dal_ref[...] = jnp.zeros_like(dal_ref)
            ddt_ref[...] = jnp.zeros_like(ddt_ref)

        ext[0:HALO, :] = jnp.where(i < nblk - 1, halo_ref[...], 0.0)
        ext[HALO:HALO + tm, :] = x_ref[...]
        yc = cw_ref[0:1, :] * ext[pl.ds(HALO - DN_K + 1, tm), :]
        for t in range(1, DN_K):
            yc += cw_ref[t:t + 1, :] * ext[pl.ds(HALO - DN_K + 1 + t, tm), :]
        sg = jax.nn.sigmoid(yc)
        act = yc * sg
        dact = sg * (1.0 + yc * (1.0 - sg))
        for h in range(DN_H):
            cs = slice(h * HEAD, (h + 1) * HEAD)
            ks = slice(DN_W + h * HEAD, DN_W + (h + 1) * HEAD)
            qa = act[:, cs]
            rq = lax.rsqrt(_rowsum(qa * qa) + EPS)
            qh = qa * rq
            dqv = dq_ref[:, cs]
            exd[0:tm, cs] = (HEAD ** -0.5) * rq * (dqv - qh * _rowsum(dqv * qh)) * dact[:, cs]
            ka = act[:, ks]
            rk = lax.rsqrt(_rowsum(ka * ka) + EPS)
            kh = ka * rk
            dkv = dk_ref[:, cs]
            exd[0:tm, ks] = rk * (dkv - kh * _rowsum(dkv * kh)) * dact[:, ks]
        exd[0:tm, 2 * DN_W:w3] = dv_ref[...] * dact[:, 2 * DN_W:w3]
        exd[tm:tm + HALO, :] = carry[...]
        dyc = exd[0:tm, :]
        dx = cw_ref[0:1, :] * exd[pl.ds(DN_K - 1, tm), :]
        dcw_ref[0:1, :] += _colsum(dyc * ext[pl.ds(HALO - DN_K + 1, tm), :])
        for t in range(1, DN_K):
            dx += cw_ref[t:t + 1, :] * exd[pl.ds(DN_K - 1 - t, tm), :]
            dcw_ref[t:t + 1, :] += _colsum(dyc * ext[pl.ds(HALO - DN_K + 1 + t, tm), :])
        dx_ref[...] = dx.astype(BF16)
        carry[...] = exd[0:HALO, :]

        lane = _iota2((tm, LANE), 1)
        dgbv = dgb_ref[...]
        dg = _mm_hi(_chunk_tri(tm, True), jnp.where(lane < DN_H, dgbv, 0.0))
        abv = ab_ref[...]
        xa = abv + dt_ref[...]
        nea = -jnp.exp(al_ref[...])
        d_da = jnp.where(lane < DN_H, dg * nea * jax.nn.sigmoid(xa), 0.0)
        dal_ref[...] += _colsum(jnp.where(lane < DN_H, dg * nea * _softplus(xa), 0.0))
        ddt_ref[...] += _colsum(d_da)
        beta = jax.nn.sigmoid(abv)
        d_db = jnp.where((lane >= DN_H) & (lane < 2 * DN_H), dgbv * beta * (1.0 - beta), 0.0)
        dab_ref[...] = (d_da + d_db).astype(BF16)

    hb = tm // HALO
    rev = lambda i: (nblk - 1 - i, 0)
    row = lambda w: pl.BlockSpec((tm, w), rev)
    vec = pl.BlockSpec((1, LANE), lambda i: (0, 0))
    cws = pl.BlockSpec((DN_K, w3), lambda i: (0, 0))
    return pl.pallas_call(
        body, name="dn_pre_bwd", grid=(nblk,),
        in_specs=[row(w3), pl.BlockSpec((HALO, w3), lambda i: (jnp.maximum((nblk - 1 - i) * hb - 1, 0), 0)), row(LANE),
                  cws, vec, vec, row(DN_W), row(DN_W), row(DN_W), row(LANE)],
        out_specs=[row(w3), row(LANE), cws, vec, vec],
        out_shape=[_sds((s, w3), BF16), _sds((s, LANE), BF16), _sds((DN_K, w3)), _sds((1, LANE)), _sds((1, LANE))],
        scratch_shapes=[pltpu.VMEM((tm + HALO, w3), F32), pltpu.VMEM((tm + HALO, w3), F32), pltpu.VMEM((HALO, w3), F32)],
        compiler_params=_params(("arbitrary",)),
    )(proj, proj, ab, conv_w, alog_row, dt_row, dq, dk, dv, dgb)


def _adam(parts, w, m, v, name):
    r, c = w.shape
    n_parts = parts.shape[0]
    tr = _tile(r, (128, 64, 32, 16, 8))

    def body(p_ref, w_ref, m_ref, v_ref, g_ref, d_ref, nm_ref, nv_ref):
        g = p_ref[0].astype(F32)
        for k in range(1, n_parts):
            g = g + p_ref[k].astype(F32)
        g_ref[...] = g
        mn = ADAM_B1 * m_ref[...] + (1.0 - ADAM_B1) * g
        vn = ADAM_B2 * v_ref[...] + (1.0 - ADAM_B2) * (g * g)
        m_hat = mn / (1.0 - ADAM_B1 ** ADAM_STEP)
        v_hat = vn / (1.0 - ADAM_B2 ** ADAM_STEP)
        d_ref[...] = -ADAM_LR * (m_hat / (jnp.sqrt(v_hat) + ADAM_EPS) + ADAM_WD * w_ref[...])
        nm_ref[...] = mn
        nv_ref[...] = vn

    blk = pl.BlockSpec((tr, c), lambda i: (i, 0))
    return pl.pallas_call(
        body, name=name, grid=(r // tr,),
        in_specs=[pl.BlockSpec((n_parts, tr, c), lambda i: (0, i, 0)), blk, blk, blk],
        out_specs=[blk, blk, blk, blk], out_shape=[_sds((r, c))] * 4,
        compiler_params=_params(("parallel",)),
    )(parts, w, m, v)


_PACK_ROWS = 8


def _pack(vals):
    tiles = []
    for a in vals:
        flat = a.reshape(-1).astype(F32)
        unit = _PACK_ROWS * LANE
        n = -(-flat.shape[0] // unit) * unit
        tiles.append(jnp.pad(flat, (0, n - flat.shape[0])).reshape(n // LANE, LANE))
    return jnp.concatenate(tiles, axis=0)


def _unpack(packed, shapes):
    out = []
    r0 = 0
    for shp in shapes:
        size = 1
        for dim in shp:
            size *= dim
        unit = _PACK_ROWS * LANE
        rows = -(-size // unit) * _PACK_ROWS
        out.append(packed[r0:r0 + rows].reshape(-1)[:size].reshape(shp))
        r0 += rows
    return out


def _lane_row(vec8):
    return jnp.pad(vec8.reshape(1, -1).astype(F32), ((0, 0), (0, LANE - vec8.size)))


def kernel(x, mem, ln_g, w_in, gmlp_ln_g, gmlp_ln_b, gmlp_ws, gmlp_bs, conv_w, dn_a_log, dn_dt_bias, dn_norm_g, mem_norm_g, w_mem_kv, w_out, final_g, loss_target, m_ln_g, m_w_in, m_gmlp_ln_g, m_gmlp_ln_b, m_gmlp_ws, m_gmlp_bs, m_conv_w, m_dn_a_log, m_dn_dt_bias, m_dn_norm_g, m_mem_norm_g, m_w_mem_kv, m_w_out, m_final_g, v_ln_g, v_w_in, v_gmlp_ln_g, v_gmlp_ln_b, v_gmlp_ws, v_gmlp_bs, v_conv_w, v_dn_a_log, v_dn_dt_bias, v_dn_norm_g, v_mem_norm_g, v_w_mem_kv, v_w_out, v_final_g):
    xs = x[0]
    mems = mem[0]
    tgt = loss_target[0]
    s, d = xs.shape
    shard_w = w_in.shape[2]
    in_w = N_DEV * shard_w
    me = 4 * lax.axis_index("x") + 2 * lax.axis_index("y") + lax.axis_index("c")

    g_in, g_out, g_kv, g_conv = _gather_two_level(
        [w_in[0].astype(BF16), w_out[0].astype(BF16), w_mem_kv[0].astype(BF16), conv_w[0]], "gather_weights")
    w_full = g_in.transpose(1, 0, 2).reshape(d, in_w)
    o_g, o_dn, o_ab = 0, 3 * GMLP_W, 3 * GMLP_W + 4 * DN_W
    o_xa = o_ab + 2 * DN_H
    w_qkv = w_full[:, o_dn:o_dn + 3 * DN_W]
    w_dz = w_full[:, o_dn + 3 * DN_W:o_ab]
    w_gm = w_full[:, o_g:o_dn]
    w_xa = w_full[:, o_xa:in_w]
    w_ab = jnp.pad(w_full[:, o_ab:o_xa], ((0, 0), (0, LANE - 2 * DN_H)))
    w_main = jnp.concatenate([w_qkv, w_dz, w_gm, w_xa], axis=1)
    wo = g_out.reshape(MIX_W, d)
    wo_perm = jnp.concatenate([wo[GMLP_W:GMLP_W + DN_W], wo[0:GMLP_W], wo[GMLP_W + DN_W:MIX_W]], axis=0)
    w_kv = g_kv.reshape(d, 2 * XA_W)
    conv_full = g_conv.transpose(1, 0, 2).reshape(DN_K, 3 * DN_W)

    ln_g2 = ln_g.reshape(1, d)
    lng2 = gmlp_ln_g.reshape(1, GMLP_W)
    lnb2 = gmlp_ln_b.reshape(1, GMLP_W)
    ws3 = gmlp_ws[0]
    bs_t = gmlp_bs[0].T
    alog_row = _lane_row(dn_a_log)
    dt_row = _lane_row(dn_dt_bias)
    dn_g2 = dn_norm_g.reshape(1, HEAD)
    mem_g2 = mem_norm_g.reshape(1, d)
    fin_g2 = final_g.reshape(1, d)

    proj, ab, h = _inproj(xs, ln_g2, w_main, w_ab)
    out_a = _gmlp_fwd(proj, lng2, lnb2, ws3, bs_t)
    mkv = _memkv_fwd(mems, mem_g2, w_kv)
    out_c = _xattn_fwd(proj, mkv)
    q, k, v, gb, gbt = _dn_pre(proj, ab, conv_full, alog_row, dt_row)
    u, wk, qg, kd, tmat, ai, egl = _dn_local(q, k, v, gb, gbt)
    o, vn, st, out_b = _dn_scan(u, wk, qg, kd, ai, egl, proj, dn_g2)

    dx2, dx2b, dmixed, loss_acc, d_fin_g = _final(xs, tgt, out_b, out_a, out_c, wo_perm, fin_g2)
    loss = lax.psum(loss_acc[0, 0], ("x", "y", "c"))

    dwo_b = _matmul_tn(out_b, dx2b, "dw_out_b")
    dwo_a = _matmul_tn(out_a, dx2b, "dw_out_a")
    dwo_c = _matmul_tn(out_c, dx2b, "dw_out_c")
    d_w_out = jnp.concatenate([dwo_a, dwo_b, dwo_c], axis=0)

    dp_g, d_ws, d_bst, d_lng, d_lnb = _gmlp_bwd(proj, dmixed, lng2, lnb2, ws3, bs_t)
    dp_x, dmkv = _xattn_bwd(proj, dmixed, mkv)
    d_w_kv, d_mem_g = _memkv_bwd(mems, mem_g2, w_kv, dmkv)
    do, dvn, dst, dp_dz, d_dn_g = _dn_scan_bwd(dmixed, o, proj, dn_g2, wk, qg, kd, ai, egl)
    dq, dk, dv, dgb = _dn_local_bwd(q, k, v, gb, gbt, tmat, vn, st, dst, do, dvn)
    dp_qkv, dp_ab, d_conv, d_alog, d_dt = _dn_pre_bwd(proj, ab, conv_full, alog_row, dt_row, dq, dk, dv, dgb)

    pieces = [dp_qkv, dp_dz, dp_g, dp_x, dp_ab]
    dh = _dh(pieces, [w_qkv, w_dz, w_gm, w_xa, w_ab])
    grad_x, d_ln_g = _rms_bwd(xs, dh, dx2, ln_g2)
    dw_qkv = _matmul_tn(h, dp_qkv, "dw_in_qkv")
    dw_dz = _matmul_tn(h, dp_dz, "dw_in_dz")
    dw_gm = _matmul_tn(h, dp_g, "dw_in_gmlp")
    dw_xa = _matmul_tn(h, dp_x, "dw_in_xa")
    dw_ab = _matmul_tn(h, dp_ab, "dw_in_ab")
    d_w_in = jnp.concatenate([dw_gm, dw_qkv, dw_dz, dw_ab[:, :2 * DN_H], dw_xa], axis=1)

    small_shapes = [ln_g.shape, gmlp_ln_g.shape, gmlp_ln_b.shape, gmlp_ws.shape, gmlp_bs.shape, dn_a_log.shape,
                    dn_dt_bias.shape, dn_norm_g.shape, mem_norm_g.shape, final_g.shape, (DN_K, 3 * DN_W)]
    small_g = _pack([d_ln_g, d_lng, d_lnb, d_ws, d_bst.T, d_alog[:, :DN_H], d_dt[:, :DN_H], d_dn_g, d_mem_g, d_fin_g,
                     d_conv])
    zc = jnp.zeros((DN_K, 3 * DN_W), F32)
    small_w = _pack([ln_g, gmlp_ln_g, gmlp_ln_b, gmlp_ws, gmlp_bs, dn_a_log, dn_dt_bias, dn_norm_g, mem_norm_g, final_g, zc])
    small_m = _pack([m_ln_g, m_gmlp_ln_g, m_gmlp_ln_b, m_gmlp_ws, m_gmlp_bs, m_dn_a_log, m_dn_dt_bias, m_dn_norm_g,
                     m_mem_norm_g, m_final_g, zc])
    small_v = _pack([v_ln_g, v_gmlp_ln_g, v_gmlp_ln_b, v_gmlp_ws, v_gmlp_bs, v_dn_a_log, v_dn_dt_bias, v_dn_norm_g,
                     v_mem_norm_g, v_final_g, zc + 1.0])

    send_in = d_w_in.reshape(d, N_DEV, shard_w).transpose(1, 0, 2).astype(BF16)
    send_out = d_w_out.reshape(N_DEV, MIX_W // N_DEV, d).astype(BF16)
    send_kv = d_w_kv.reshape(N_DEV, d // N_DEV, 2 * XA_W).astype(BF16)
    sends = [send_in, send_out, send_kv]
    all_small, got = _swap_halves(small_g, sends, "swap_halves")
    core = lax.axis_index("c").astype(jnp.int32).reshape(1)
    chip_sums = [_pair_sum(core, sends[i], got[i], "pair_sum_%d" % i) for i in range(3)]
    r_in, r_out, r_kv = _chip_exchange(chip_sums, "chip_exchange")

    g_w_in, dl_w_in, nm_w_in, nv_w_in = _adam(r_in, w_in[0], m_w_in[0], v_w_in[0], "adam_w_in")
    g_w_out, dl_w_out, nm_w_out, nv_w_out = _adam(r_out, w_out[0], m_w_out[0], v_w_out[0], "adam_w_out")
    g_w_kv, dl_w_kv, nm_w_kv, nv_w_kv = _adam(r_kv, w_mem_kv[0], m_w_mem_kv[0], v_w_mem_kv[0], "adam_w_kv")
    sm = [_unpack(t, small_shapes) for t in _adam(all_small, small_w, small_m, small_v, "adam_small")]

    conv_parts = lax.dynamic_slice(all_small, (0, all_small.shape[1] - (DN_K * 3 * DN_W) // LANE, 0),
                                   (N_DEV, (DN_K * 3 * DN_W) // LANE, LANE)).reshape(N_DEV, DN_K, 3 * DN_W)
    cshard = conv_w.shape[2]
    conv_parts = lax.dynamic_slice(conv_parts, (0, 0, me * cshard), (N_DEV, DN_K, cshard))
    cpad = ((0, 0), (0, HALO - DN_K), (0, 0))
    conv_res = _adam(jnp.pad(conv_parts, cpad), jnp.pad(conv_w[0], cpad[1:]), jnp.pad(m_conv_w[0], cpad[1:]),
                     jnp.pad(v_conv_w[0], cpad[1:], constant_values=1.0), "adam_conv")
    g_conv_s, dl_conv, nm_conv, nv_conv = [t[:DN_K][None] for t in conv_res]

    def group(idx, big_in, big_conv, big_kv, big_out):
        names = sm[idx]
        return [names[0], big_in[None], names[1], names[2], names[3], names[4], big_conv, names[5], names[6], names[7],
                names[8], big_kv[None], big_out[None], names[9]]

    grads = group(0, g_w_in, g_conv_s, g_w_kv, g_w_out)
    deltas = group(1, dl_w_in, dl_conv, dl_w_kv, dl_w_out)
    new_m = group(2, nm_w_in, nm_conv, nm_w_kv, nm_w_out)
    new_v = group(3, nv_w_in, nv_conv, nv_w_kv, nv_w_out)
    return (loss, grad_x[None], *grads, *deltas, *new_m, *new_v)
```

```python
import functools

import jax
import jax.numpy as jnp
from jax import lax
from jax.experimental import pallas as pl
from jax.experimental.pallas import tpu as pltpu

F32 = jnp.float32
BF16 = jnp.bfloat16
HIGHEST = lax.Precision.HIGHEST
MESH_ID = pl.DeviceIdType.MESH

N_DEV = 8
EPS = 1e-6
GMLP_W = 512
GMLP_G = 4
GMLP_T = 128
DN_W = 1024
DN_H = 8
HEAD = 128
DN_K = 4
CH = 64
XA_W = 512
XA_H = 4
LANE = 128
HALO = 8
MAIN_W = 4 * DN_W + 3 * GMLP_W + 2 * XA_W
MIX_W = DN_W + GMLP_W + XA_W
VMEM_LIMIT = 56 * 1024 * 1024

ADAM_LR = 0.001
ADAM_B1 = 0.9
ADAM_B2 = 0.999
ADAM_EPS = 1e-08
ADAM_WD = 0.01
ADAM_STEP = 10


def _sds(shape, dtype=F32):
    return jax.ShapeDtypeStruct(tuple(shape), dtype)


def _params(sem=None):
    if sem is None:
        return pltpu.CompilerParams(vmem_limit_bytes=VMEM_LIMIT)
    return pltpu.CompilerParams(dimension_semantics=tuple(sem), vmem_limit_bytes=VMEM_LIMIT)


def _tile(n, prefs):
    for p in prefs:
        if n % p == 0:
            return p
    return n


def _mm(a, b):
    return jnp.dot(a.astype(BF16), b.astype(BF16), preferred_element_type=F32)


def _mm_nt(a, b):
    return lax.dot_general(a.astype(BF16), b.astype(BF16), (((1,), (1,)), ((), ())), preferred_element_type=F32)


def _mm_tn(a, b):
    return lax.dot_general(a.astype(BF16), b.astype(BF16), (((0,), (0,)), ((), ())), preferred_element_type=F32)


def _mm_hi(a, b):
    return jnp.dot(a, b, precision=HIGHEST, preferred_element_type=F32)


def _mm_3x(a, b):
    return jnp.dot(a, b, precision=lax.Precision.HIGH, preferred_element_type=F32)


_GELU_C = 0.7978845608028654
_GELU_A = 0.044715


def _gelu(x):
    return 0.5 * x * (1.0 + jnp.tanh(_GELU_C * (x + _GELU_A * x * x * x)))


def _gelu_grad(x):
    t = jnp.tanh(_GELU_C * (x + _GELU_A * x * x * x))
    return 0.5 * (1.0 + t) + 0.5 * x * (1.0 - t * t) * _GELU_C * (1.0 + 3.0 * _GELU_A * x * x)


def _silu(x):
    return x * jax.nn.sigmoid(x)


def _silu_grad(x):
    s = jax.nn.sigmoid(x)
    return s * (1.0 + x * (1.0 - s))


def _rowsum(x):
    return jnp.sum(x, axis=-1, keepdims=True)


def _colsum(x):
    return jnp.sum(x, axis=0, keepdims=True)


def _iota2(shape, dim):
    return lax.broadcasted_iota(jnp.int32, shape, dim)


def _chunk_tri(tm, upper):
    r = _iota2((tm, tm), 0)
    c = _iota2((tm, tm), 1)
    same = lax.shift_right_logical(r, 6) == lax.shift_right_logical(c, 6)
    tri = (r <= c) if upper else (r >= c)
    return jnp.where(same & tri, 1.0, 0.0).astype(F32)


N_CHIP = 4


def _mesh_place():
    x, y, c = lax.axis_index("x"), lax.axis_index("y"), lax.axis_index("c")
    chips = [(1 - x, y), (x, 1 - y), (1 - x, 1 - y)]
    return x, y, c, (x, y, 1 - c), chips


def _gather_two_level(arrs, name):
    n = len(arrs)

    def body(*refs):
        ins = refs[:n]
        outs = refs[n:2 * n]
        send_sems, recv_sems, loc_sems = refs[2 * n:]
        x, y, c, sib, chips = _mesh_place()

        def copy(a, k, block, to, src=None):
            slot = outs[a].at[4 * block[0] + 2 * block[1] + block[2]]
            return pltpu.make_async_remote_copy(
                src_ref=slot if src is None else src, dst_ref=slot, send_sem=send_sems.at[a, k],
                recv_sem=recv_sems.at[a, k], device_id=to, device_id_type=MESH_ID)

        me = (x, y, c)
        sends, locs = [], []
        for a in range(n):
            own = pltpu.make_async_copy(ins[a], outs[a].at[4 * x + 2 * y + c], loc_sems.at[a])
            own.start()
            locs.append(own)
            first = [copy(a, 0, me, sib, src=ins[a])]
            first += [copy(a, 1 + j, me, (*chip, c), src=ins[a]) for j, chip in enumerate(chips)]
            for cp in first:
                cp.start()
            sends += first
        for a in range(n):
            for j, chip in enumerate(chips):
                copy(a, 1 + j, (*chip, c), me).wait_recv()
                passed = copy(a, 4 + j, (*chip, c), sib)
                passed.start()
                sends.append(passed)
        for a in range(n):
            copy(a, 0, sib, me).wait_recv()
            for j, chip in enumerate(chips):
                copy(a, 4 + j, (*chip, 1 - c), me).wait_recv()
        for cp in sends:
            cp.wait_send()
        for cp in locs:
            cp.wait()

    any_spec = pl.BlockSpec(memory_space=pl.ANY)
    return pl.pallas_call(
        body, name=name, out_shape=[_sds((N_DEV,) + a.shape, a.dtype) for a in arrs],
        in_specs=[any_spec] * n, out_specs=[any_spec] * n,
        scratch_shapes=[pltpu.SemaphoreType.DMA((n, N_DEV - 1)), pltpu.SemaphoreType.DMA((n, N_DEV - 1)),
                        pltpu.SemaphoreType.DMA((n,))],
        compiler_params=pltpu.CompilerParams(has_side_effects=True),
    )(*arrs)


def _swap_halves(small, grads, name):
    n = len(grads)

    def body(*refs):
        small_ref = refs[0]
        ins = refs[1:1 + n]
        small_out = refs[1 + n]
        got = refs[2 + n:2 + 2 * n]
        s_send, s_recv, g_send, g_recv, loc_sem = refs[2 + 2 * n:]
        x, y, c, sib, _ = _mesh_place()
        me = 4 * x + 2 * y + c
        sends, recvs = [], []
        for j in range(1, N_DEV):
            px = 1 - x if (j >> 2) & 1 else x
            py = 1 - y if (j >> 1) & 1 else y
            pc = 1 - c if j & 1 else c
            cp = pltpu.make_async_remote_copy(
                src_ref=small_ref, dst_ref=small_out.at[me], send_sem=s_send.at[j - 1], recv_sem=s_recv.at[j - 1],
                device_id=(px, py, pc), device_id_type=MESH_ID)
            cp.start()
            sends.append(cp)
            recvs.append(pltpu.make_async_remote_copy(
                src_ref=small_ref, dst_ref=small_out.at[4 * px + 2 * py + pc], send_sem=s_send.at[j - 1],
                recv_sem=s_recv.at[j - 1], device_id=(px, py, pc), device_id_type=MESH_ID))
        own = pltpu.make_async_copy(small_ref, small_out.at[me], loc_sem)
        own.start()
        for a in range(n):
            for chip in range(N_CHIP):
                cp = pltpu.make_async_remote_copy(
                    src_ref=ins[a].at[2 * chip + 1 - c], dst_ref=got[a].at[chip], send_sem=g_send.at[a, chip],
                    recv_sem=g_recv.at[a, chip], device_id=sib, device_id_type=MESH_ID)
                cp.start()
                sends.append(cp)
                recvs.append(cp)
        for cp in sends:
            cp.wait_send()
        for cp in recvs:
            cp.wait_recv()
        own.wait()

    half = [_sds((N_CHIP,) + g.shape[1:], g.dtype) for g in grads]
    any_spec = pl.BlockSpec(memory_space=pl.ANY)
    res = pl.pallas_call(
        body, name=name, out_shape=[_sds((N_DEV,) + small.shape, small.dtype)] + half,
        in_specs=[any_spec] * (1 + n), out_specs=[any_spec] * (1 + n),
        scratch_shapes=[pltpu.SemaphoreType.DMA((N_DEV - 1,)), pltpu.SemaphoreType.DMA((N_DEV - 1,)),
                        pltpu.SemaphoreType.DMA((n, N_CHIP)), pltpu.SemaphoreType.DMA((n, N_CHIP)),
                        pltpu.SemaphoreType.DMA],
        compiler_params=pltpu.CompilerParams(has_side_effects=True),
    )(small, *grads)
    return res[0], res[1:]


def _pair_sum(core, mine, got, name):
    nc, r, c = got.shape
    tr = _tile(r, (256, 128, 64, 32, 16))

    def body(core_ref, a_ref, b_ref, o_ref):
        o_ref[...] = (a_ref[...].astype(F32) + b_ref[...].astype(F32)).astype(BF16)

    return pl.pallas_call(
        body, name=name, out_shape=_sds(got.shape, BF16),
        grid_spec=pltpu.PrefetchScalarGridSpec(
            num_scalar_prefetch=1, grid=(nc, r // tr),
            in_specs=[pl.BlockSpec((1, tr, c), lambda i, j, core_ref: (2 * i + core_ref[0], j, 0)),
                      pl.BlockSpec((1, tr, c), lambda i, j, core_ref: (i, j, 0))],
            out_specs=pl.BlockSpec((1, tr, c), lambda i, j, core_ref: (i, j, 0))),
        compiler_params=_params(("parallel", "parallel")),
    )(core, mine, got)


def _chip_exchange(parts, name):
    n = len(parts)

    def body(*refs):
        ins = refs[:n]
        outs = refs[n:2 * n]
        send_sems, recv_sems, loc_sems = refs[2 * n:]
        x, y, c, _, chips = _mesh_place()
        mine = 2 * x + y
        sends, locs = [], []
        for a in range(n):
            own = pltpu.make_async_copy(ins[a].at[mine], outs[a].at[mine], loc_sems.at[a])
            own.start()
            locs.append(own)
            for j, chip in enumerate(chips):
                q = 2 * chip[0] + chip[1]
                cp = pltpu.make_async_remote_copy(
                    src_ref=ins[a].at[q], dst_ref=outs[a].at[mine], send_sem=send_sems.at[a, j], recv_sem=recv_sems.at[a, j],
                    device_id=(*chip, c), device_id_type=MESH_ID)
                cp.start()
                landed = pltpu.make_async_remote_copy(
                    src_ref=ins[a].at[q], dst_ref=outs[a].at[q], send_sem=send_sems.at[a, j], recv_sem=recv_sems.at[a, j],
                    device_id=(*chip, c), device_id_type=MESH_ID)
                sends.append((cp, landed))
        for cp, landed in sends:
            cp.wait_send()
            landed.wait_recv()
        for cp in locs:
            cp.wait()

    any_spec = pl.BlockSpec(memory_space=pl.ANY)
    return pl.pallas_call(
        body, name=name, out_shape=[_sds(p.shape, p.dtype) for p in parts],
        in_specs=[any_spec] * n, out_specs=[any_spec] * n,
        scratch_shapes=[pltpu.SemaphoreType.DMA((n, N_CHIP - 1)), pltpu.SemaphoreType.DMA((n, N_CHIP - 1)),
                        pltpu.SemaphoreType.DMA((n,))],
        compiler_params=pltpu.CompilerParams(has_side_effects=True),
    )(*parts)


def _inproj(x, ln_g, w_main, w_ab):
    s, d = x.shape
    n = w_main.shape[1]
    tm = _tile(s, (512, 256, 128))
    tn = _tile(n, (1664, 512, 128))

    def body(x_ref, g_ref, w_ref, wab_ref, proj_ref, ab_ref, ht_ref, hs):
        @pl.when(pl.program_id(1) == 0)
        def _():
            xv = x_ref[...]
            r = lax.rsqrt(jnp.mean(xv * xv, axis=-1, keepdims=True) + EPS)
            hf = xv * r * g_ref[...]
            h = hf.astype(BF16)
            hs[...] = h
            ht_ref[...] = hf.T.astype(BF16)
            ab_ref[...] = jnp.dot(h, wab_ref[...], preferred_element_type=F32)

        proj_ref[...] = jnp.dot(hs[...], w_ref[...], preferred_element_type=F32)

    return pl.pallas_call(
        body, name="inproj", grid=(s // tm, n // tn),
        in_specs=[pl.BlockSpec((tm, d), lambda i, j: (i, 0)), pl.BlockSpec((1, d), lambda i, j: (0, 0)),
                  pl.BlockSpec((d, tn), lambda i, j: (0, j)), pl.BlockSpec((d, LANE), lambda i, j: (0, 0))],
        out_specs=[pl.BlockSpec((tm, tn), lambda i, j: (i, j)), pl.BlockSpec((tm, LANE), lambda i, j: (i, 0)),
                   pl.BlockSpec((d, tm), lambda i, j: (0, i))],
        out_shape=[_sds((s, n)), _sds((s, LANE)), _sds((d, s), BF16)],
        scratch_shapes=[pltpu.VMEM((tm, d), BF16)],
        compiler_params=_params(("parallel", "arbitrary")),
    )(x, ln_g, w_main, w_ab)


def _matmul_acc(a, b, name):
    m, k = a.shape
    n = b.shape[1]
    tm = _tile(m, (2048, 1024, 512, 256, 128))
    tn = _tile(n, (1024, 512, 256, 128))
    tk = _tile(k, (1024, 512, 256, 128))

    def body(a_ref, b_ref, o_ref):
        @pl.when(pl.program_id(2) == 0)
        def _():
            o_ref[...] = jnp.zeros_like(o_ref)

        o_ref[...] += jnp.dot(a_ref[...], b_ref[...], preferred_element_type=F32)

    return pl.pallas_call(
        body, name=name, grid=(m // tm, n // tn, k // tk),
        in_specs=[pl.BlockSpec((tm, tk), lambda i, j, l: (i, l)), pl.BlockSpec((tk, tn), lambda i, j, l: (l, j))],
        out_specs=pl.BlockSpec((tm, tn), lambda i, j, l: (i, j)),
        out_shape=_sds((m, n)),
        compiler_params=_params(("parallel", "parallel", "arbitrary")),
    )(a, b)


def _matmul_tn(a, b, name):
    k, m = a.shape
    n = b.shape[1]
    tm = _tile(m, (1024, 512, 256, 128))
    tn = _tile(n, (1024, 512, 256, 128))
    tk = _tile(k, (1024, 512, 256, 128))

    def body(a_ref, b_ref, o_ref):
        @pl.when(pl.program_id(2) == 0)
        def _():
            o_ref[...] = jnp.zeros_like(o_ref)

        o_ref[...] += _mm_tn(a_ref[...], b_ref[...])

    return pl.pallas_call(
        body, name=name, grid=(m // tm, n // tn, k // tk),
        in_specs=[pl.BlockSpec((tk, tm), lambda i, j, l: (l, i)), pl.BlockSpec((tk, tn), lambda i, j, l: (l, j))],
        out_specs=pl.BlockSpec((tm, tn), lambda i, j, l: (i, j)),
        out_shape=_sds((m, n)),
        compiler_params=_params(("parallel", "parallel", "arbitrary")),
    )(a, b)


def _dh(terms):
    s = terms[0][0].shape[0]
    d = terms[0][1].shape[0]
    npc = len(terms)
    tm = _tile(s, (256, 128))
    tn = _tile(d, (1024, 512, 256, 128))

    def body(*refs):
        o_ref = refs[2 * npc]
        acc = _mm_nt(refs[0][...], refs[npc][...])
        for p in range(1, npc):
            acc += _mm_nt(refs[p][...], refs[npc + p][...])
        o_ref[...] = acc

    in_specs = [pl.BlockSpec((tm, w), functools.partial(lambda j, i, pb: (i, pb), pb=pb)) for _, _, w, pb, _ in terms]
    in_specs += [pl.BlockSpec((tn, w), functools.partial(lambda j, i, wb: (j, wb), wb=wb)) for _, _, w, _, wb in terms]
    return pl.pallas_call(
        body, name="dh", grid=(d // tn, s // tm), in_specs=in_specs,
        out_specs=pl.BlockSpec((tm, tn), lambda j, i: (i, j)), out_shape=_sds((s, d)),
        compiler_params=_params(("parallel", "parallel")),
    )(*[t[0] for t in terms], *[t[1] for t in terms])


def _rms_bwd(x, dh, dx2, ln_g):
    s, d = x.shape
    tm = _tile(s, (256, 128))

    def body(x_ref, dh_ref, dx2_ref, g_ref, gx_ref, dg_ref):
        @pl.when(pl.program_id(0) == 0)
        def _():
            dg_ref[...] = jnp.zeros_like(dg_ref)

        xv = x_ref[...]
        r = lax.rsqrt(jnp.mean(xv * xv, axis=-1, keepdims=True) + EPS)
        xhat = xv * r
        dhv = dh_ref[...]
        dg_ref[...] += _colsum(dhv * xhat)
        dxh = dhv * g_ref[...]
        gx_ref[...] = dx2_ref[...] + r * (dxh - xhat * jnp.mean(dxh * xhat, axis=-1, keepdims=True))

    row = pl.BlockSpec((tm, d), lambda i: (i, 0))
    vec = pl.BlockSpec((1, d), lambda i: (0, 0))
    return pl.pallas_call(
        body, name="rms_bwd", grid=(s // tm,), in_specs=[row, row, row, vec], out_specs=[row, vec],
        out_shape=[_sds((s, d)), _sds((1, d))], compiler_params=_params(("arbitrary",)),
    )(x, dh, dx2, ln_g)


def _final(x, tgt, out_b, out_a, out_c, w_out, final_g):
    s, d = x.shape
    tm = _tile(s, (256, 128))

    def body(x_ref, t_ref, b_ref, a_ref, c_ref, w_ref, g_ref, dx2_ref, dx2b_ref, dm_ref, loss_ref, dg_ref):
        @pl.when(pl.program_id(0) == 0)
        def _():
            loss_ref[...] = jnp.zeros_like(loss_ref)
            dg_ref[...] = jnp.zeros_like(dg_ref)

        x2 = x_ref[...]
        x2 += jnp.dot(b_ref[...], w_ref[0:DN_W, :], preferred_element_type=F32)
        x2 += jnp.dot(a_ref[...], w_ref[DN_W:DN_W + GMLP_W, :], preferred_element_type=F32)
        x2 += jnp.dot(c_ref[...], w_ref[DN_W + GMLP_W:MIX_W, :], preferred_element_type=F32)
        r = lax.rsqrt(jnp.mean(x2 * x2, axis=-1, keepdims=True) + EPS)
        xhat = x2 * r
        g = g_ref[...]
        err = xhat * g - t_ref[...]
        tok = 0.5 * jnp.mean(err * err, axis=-1, keepdims=True)
        loss_ref[...] += jnp.broadcast_to(_colsum(tok), loss_ref.shape)
        dy = err * (1.0 / d)
        dg_ref[...] += _colsum(dy * xhat)
        dxh = dy * g
        dx2 = r * (dxh - xhat * jnp.mean(dxh * xhat, axis=-1, keepdims=True))
        dx2_ref[...] = dx2
        dx2b = dx2.astype(BF16)
        dx2b_ref[...] = dx2b
        dm_ref[...] = _mm_nt(dx2b, w_ref[...])

    row = pl.BlockSpec((tm, d), lambda i: (i, 0))
    vec = pl.BlockSpec((1, d), lambda i: (0, 0))
    return pl.pallas_call(
        body, name="final", grid=(s // tm,),
        in_specs=[row, row, pl.BlockSpec((tm, DN_W), lambda i: (i, 0)), pl.BlockSpec((tm, GMLP_W), lambda i: (i, 0)),
                  pl.BlockSpec((tm, XA_W), lambda i: (i, 0)), pl.BlockSpec((MIX_W, d), lambda i: (0, 0)), vec],
        out_specs=[row, row, pl.BlockSpec((tm, MIX_W), lambda i: (i, 0)), pl.BlockSpec((1, LANE), lambda i: (0, 0)), vec],
        out_shape=[_sds((s, d)), _sds((s, d), BF16), _sds((s, MIX_W)), _sds((1, LANE)), _sds((1, d))],
        compiler_params=_params(("arbitrary",)),
    )(x, tgt, out_b, out_a, out_c, w_out, final_g)


GU_BLK = (4 * DN_W) // GMLP_W


def _gmlp_norm(gv, lng, lnb):
    va = _gelu(gv)
    mu = jnp.mean(va, axis=-1, keepdims=True)
    xc = va - mu
    rstd = lax.rsqrt(jnp.mean(xc * xc, axis=-1, keepdims=True) + EPS)
    vhat = xc * rstd
    return vhat, rstd, vhat * lng + lnb


def _gmlp_fwd(proj, lng, lnb, ws, bs_t):
    s = proj.shape[0]
    tm = _tile(s, (512, 256, 128))

    def body(u_ref, v_ref, z_ref, lng_ref, lnb_ref, ws_ref, bst_ref, o_ref):
        _, _, vn = _gmlp_norm(v_ref[...], lng_ref[...], lnb_ref[...])
        tri = _iota2((GMLP_T, GMLP_T), 0) >= _iota2((GMLP_T, GMLP_T), 1)
        for g in range(GMLP_G):
            cs = slice(g * HEAD, (g + 1) * HEAD)
            w = jnp.where(tri, ws_ref[g], 0.0).astype(BF16)
            b = bst_ref[:, g:g + 1]
            for c in range(tm // GMLP_T):
                rs = slice(c * GMLP_T, (c + 1) * GMLP_T)
                sg = _mm(w, vn[rs, cs]) + b
                o_ref[rs, cs] = (_gelu(u_ref[rs, cs]) * sg * _silu(z_ref[rs, cs])).astype(BF16)

    col = lambda k: pl.BlockSpec((tm, GMLP_W), lambda i: (i, GU_BLK + k))
    vec = pl.BlockSpec((1, GMLP_W), lambda i: (0, 0))
    return pl.pallas_call(
        body, name="gmlp_fwd", grid=(s // tm,),
        in_specs=[col(0), col(1), col(2), vec, vec, pl.BlockSpec((GMLP_G, GMLP_T, GMLP_T), lambda i: (0, 0, 0)),
                  pl.BlockSpec((GMLP_T, GMLP_G), lambda i: (0, 0))],
        out_specs=pl.BlockSpec((tm, GMLP_W), lambda i: (i, 0)), out_shape=_sds((s, GMLP_W), BF16),
        compiler_params=_params(("parallel",)),
    )(proj, proj, proj, lng, lnb, ws, bs_t)


def _gmlp_bwd(proj, dmixed, lng, lnb, ws, bs_t):
    s = proj.shape[0]
    tm = _tile(s, (512, 256, 128))

    def body(u_ref, v_ref, z_ref, d_ref, lng_ref, lnb_ref, ws_ref, bst_ref,
             dp_ref, dws_ref, dbst_ref, dlng_ref, dlnb_ref, dvn):
        @pl.when(pl.program_id(0) == 0)
        def _():
            dws_ref[...] = jnp.zeros_like(dws_ref)
            dbst_ref[...] = jnp.zeros_like(dbst_ref)
            dlng_ref[...] = jnp.zeros_like(dlng_ref)
            dlnb_ref[...] = jnp.zeros_like(dlnb_ref)

        gv = v_ref[...]
        lng_v = lng_ref[...]
        vhat, rstd, vn = _gmlp_norm(gv, lng_v, lnb_ref[...])
        tri = _iota2((GMLP_T, GMLP_T), 0) >= _iota2((GMLP_T, GMLP_T), 1)
        for g in range(GMLP_G):
            cs = slice(g * HEAD, (g + 1) * HEAD)
            w = jnp.where(tri, ws_ref[g], 0.0).astype(BF16)
            b = bst_ref[:, g:g + 1]
            dw_acc = jnp.zeros((GMLP_T, GMLP_T), F32)
            db_acc = jnp.zeros((GMLP_T, 1), F32)
            for c in range(tm // GMLP_T):
                rs = slice(c * GMLP_T, (c + 1) * GMLP_T)
                vn_b = vn[rs, cs]
                sg = _mm(w, vn_b) + b
                gu = u_ref[rs, cs]
                gz = z_ref[rs, cs]
                da = d_ref[rs, cs]
                uact = _gelu(gu)
                sz = _silu(gz)
                ds = da * uact * sz
                dp_ref[rs, cs] = (da * sg * sz * _gelu_grad(gu)).astype(BF16)
                dp_ref[rs, 2 * GMLP_W + g * HEAD:2 * GMLP_W + (g + 1) * HEAD] = (da * uact * sg * _silu_grad(gz)).astype(BF16)
                dw_acc += _mm_nt(ds, vn_b)
                db_acc += _rowsum(ds)
                dvn[rs, cs] = _mm_tn(w, ds)
            dws_ref[g] += jnp.where(tri, dw_acc, 0.0)
            dbst_ref[:, g:g + 1] += db_acc
        dvn_v = dvn[...]
        dlng_ref[...] += _colsum(dvn_v * vhat)
        dlnb_ref[...] += _colsum(dvn_v)
        dvh = dvn_v * lng_v
        dva = rstd * (dvh - jnp.mean(dvh, axis=-1, keepdims=True) - vhat * jnp.mean(dvh * vhat, axis=-1, keepdims=True))
        dp_ref[:, GMLP_W:2 * GMLP_W] = (dva * _gelu_grad(gv)).astype(BF16)

    col = lambda k: pl.BlockSpec((tm, GMLP_W), lambda i: (i, GU_BLK + k))
    vec = pl.BlockSpec((1, GMLP_W), lambda i: (0, 0))
    wsp = pl.BlockSpec((GMLP_G, GMLP_T, GMLP_T), lambda i: (0, 0, 0))
    bsp = pl.BlockSpec((GMLP_T, GMLP_G), lambda i: (0, 0))
    return pl.pallas_call(
        body, name="gmlp_bwd", grid=(s // tm,),
        in_specs=[col(0), col(1), col(2), pl.BlockSpec((tm, GMLP_W), lambda i: (i, DN_W // GMLP_W)), vec, vec, wsp, bsp],
        out_specs=[pl.BlockSpec((tm, 3 * GMLP_W), lambda i: (i, 0)), wsp, bsp, vec, vec],
        out_shape=[_sds((s, 3 * GMLP_W), BF16), _sds((GMLP_G, GMLP_T, GMLP_T)), _sds((GMLP_T, GMLP_G)),
                   _sds((1, GMLP_W)), _sds((1, GMLP_W))],
        scratch_shapes=[pltpu.VMEM((tm, GMLP_W), F32)],
        compiler_params=_params(("arbitrary",)),
    )(proj, proj, proj, dmixed, lng, lnb, ws, bs_t)


CQ_BLK = (4 * DN_W + 3 * GMLP_W) // XA_W


def _memkv_fwd(mem, g, w_kv):
    nm, d = mem.shape

    def body(m_ref, g_ref, w_ref, kv_ref):
        mv = m_ref[...]
        r = lax.rsqrt(jnp.mean(mv * mv, axis=-1, keepdims=True) + EPS)
        kv_ref[...] = _mm(mv * r * g_ref[...], w_ref[...])

    return pl.pallas_call(body, name="memkv_fwd", out_shape=_sds((nm, 2 * XA_W)), compiler_params=_params())(mem, g, w_kv)


def _memkv_bwd(mem, g, w_kv, dkv):
    nm, d = mem.shape

    def body(m_ref, g_ref, w_ref, dkv_ref, dw_ref, dg_ref):
        mv = m_ref[...]
        r = lax.rsqrt(jnp.mean(mv * mv, axis=-1, keepdims=True) + EPS)
        xhat = mv * r
        dkv_v = dkv_ref[...]
        dw_ref[...] = _mm_tn(xhat * g_ref[...], dkv_v)
        dg_ref[...] = _colsum(_mm_nt(dkv_v, w_ref[...]) * xhat)

    return pl.pallas_call(body, name="memkv_bwd", out_shape=[_sds((d, 2 * XA_W)), _sds((1, d))],
                          compiler_params=_params())(mem, g, w_kv, dkv)


def _xattn_probs(q, mk):
    sc = _mm_nt(q, mk) * (HEAD ** -0.5)
    e = jnp.exp(sc - jnp.max(sc, axis=-1, keepdims=True))
    return e / _rowsum(e)


def _xattn_fwd(proj, mkv):
    s = proj.shape[0]
    nm = mkv.shape[0]
    tm = _tile(s, (512, 256, 128))

    def body(q_ref, z_ref, kv_ref, o_ref):
        for h in range(XA_H):
            cs = slice(h * HEAD, (h + 1) * HEAD)
            p = _xattn_probs(q_ref[:, cs], kv_ref[:, cs])
            ctx = _mm(p, kv_ref[:, XA_W + h * HEAD:XA_W + (h + 1) * HEAD])
            o_ref[:, cs] = (ctx * _silu(z_ref[:, cs])).astype(BF16)

    col = lambda k: pl.BlockSpec((tm, XA_W), lambda i: (i, CQ_BLK + k))
    return pl.pallas_call(
        body, name="xattn_fwd", grid=(s // tm,),
        in_specs=[col(0), col(1), pl.BlockSpec((nm, 2 * XA_W), lambda i: (0, 0))],
        out_specs=pl.BlockSpec((tm, XA_W), lambda i: (i, 0)), out_shape=_sds((s, XA_W), BF16),
        compiler_params=_params(("parallel",)),
    )(proj, proj, mkv)


def _xattn_bwd(proj, dmixed, mkv):
    s = proj.shape[0]
    nm = mkv.shape[0]
    tm = _tile(s, (512, 256, 128))

    def body(q_ref, z_ref, d_ref, kv_ref, dp_ref, dkv_ref):
        @pl.when(pl.program_id(0) == 0)
        def _():
            dkv_ref[...] = jnp.zeros_like(dkv_ref)

        for h in range(XA_H):
            cs = slice(h * HEAD, (h + 1) * HEAD)
            vs = slice(XA_W + h * HEAD, XA_W + (h + 1) * HEAD)
            q = q_ref[:, cs]
            z = z_ref[:, cs]
            mk = kv_ref[:, cs]
            mv = kv_ref[:, vs]
            p = _xattn_probs(q, mk)
            ctx = _mm(p, mv)
            dc = d_ref[:, cs]
            dctx = dc * _silu(z)
            dp_ref[:, vs] = (dc * ctx * _silu_grad(z)).astype(BF16)
            dp = _mm_nt(dctx, mv)
            dkv_ref[:, vs] += _mm_tn(p, dctx)
            ds = p * (dp - _rowsum(dp * p)) * (HEAD ** -0.5)
            dp_ref[:, cs] = _mm(ds, mk).astype(BF16)
            dkv_ref[:, cs] += _mm_tn(ds, q)

    col = lambda k: pl.BlockSpec((tm, XA_W), lambda i: (i, CQ_BLK + k))
    kvs = pl.BlockSpec((nm, 2 * XA_W), lambda i: (0, 0))
    return pl.pallas_call(
        body, name="xattn_bwd", grid=(s // tm,),
        in_specs=[col(0), col(1), pl.BlockSpec((tm, XA_W), lambda i: (i, (DN_W + GMLP_W) // XA_W)), kvs],
        out_specs=[pl.BlockSpec((tm, 2 * XA_W), lambda i: (i, 0)), kvs],
        out_shape=[_sds((s, 2 * XA_W), BF16), _sds((nm, 2 * XA_W))],
        compiler_params=_params(("arbitrary",)),
    )(proj, proj, dmixed, mkv)


def _softplus(x):
    return jnp.maximum(x, 0.0) + jnp.log1p(jnp.exp(-jnp.abs(x)))


def _dn_pre(proj, ab, conv_w, alog_row, dt_row):
    s = proj.shape[0]
    tm = _tile(s, (256, 128))
    w3 = 3 * DN_W

    def body(x_ref, halo_ref, ab_ref, cw_ref, al_ref, dt_ref, q_ref, k_ref, v_ref, gb_ref, gbt_ref, yc_ref, ext):
        i = pl.program_id(0)
        ext[0:HALO, :] = jnp.where(i > 0, halo_ref[...], 0.0)
        ext[HALO:HALO + tm, :] = x_ref[...]
        yc = cw_ref[0:1, :] * ext[pl.ds(HALO - DN_K + 1, tm), :]
        for t in range(1, DN_K):
            yc += cw_ref[t:t + 1, :] * ext[pl.ds(HALO - DN_K + 1 + t, tm), :]
        yc_ref[...] = yc
        act = _silu(yc)
        for h in range(DN_H):
            cs = slice(h * HEAD, (h + 1) * HEAD)
            qa = act[:, cs]
            q_ref[:, cs] = qa * (lax.rsqrt(_rowsum(qa * qa) + EPS) * (HEAD ** -0.5))
            ka = act[:, DN_W + h * HEAD:DN_W + (h + 1) * HEAD]
            k_ref[:, cs] = ka * lax.rsqrt(_rowsum(ka * ka) + EPS)
        v_ref[...] = act[:, 2 * DN_W:w3]
        abv = ab_ref[...]
        lane = _iota2((tm, LANE), 1)
        g = jnp.where(lane < DN_H, -jnp.exp(al_ref[...]) * _softplus(abv + dt_ref[...]), 0.0)
        gc = _mm_hi(_chunk_tri(tm, False), g)
        gbv = jnp.where(lane < DN_H, gc, jnp.where(lane < 2 * DN_H, jax.nn.sigmoid(abv), 0.0))
        gb_ref[...] = gbv
        for c in range(tm // CH):
            gbt_ref[c] = gbv[c * CH:(c + 1) * CH, :].T[0:2 * DN_H, :]

    hb = tm // HALO
    row = lambda w: pl.BlockSpec((tm, w), lambda i: (i, 0))
    vec = pl.BlockSpec((1, LANE), lambda i: (0, 0))
    return pl.pallas_call(
        body, name="dn_pre", grid=(s // tm,),
        in_specs=[row(w3), pl.BlockSpec((HALO, w3), lambda i: (jnp.maximum(i * hb - 1, 0), 0)), row(LANE),
                  pl.BlockSpec((DN_K, w3), lambda i: (0, 0)), vec, vec],
        out_specs=[row(DN_W), row(DN_W), row(DN_W), row(LANE), pl.BlockSpec((tm // CH, 2 * DN_H, CH), lambda i: (i, 0, 0)),
                   row(w3)],
        out_shape=[_sds((s, DN_W)), _sds((s, DN_W)), _sds((s, DN_W)), _sds((s, LANE)), _sds((s // CH, 2 * DN_H, CH)),
                   _sds((s, w3))],
        scratch_shapes=[pltpu.VMEM((tm + HALO, w3), F32)],
        compiler_params=_params(("parallel",)),
    )(proj, proj, ab, conv_w, alog_row, dt_row)


HEADS = tuple(range(DN_H))


def _hcols(h):
    return slice(h * HEAD, (h + 1) * HEAD)


def _chunk_scalings(k, v, gbv, gbt, h):
    gc = gbv[:, h:h + 1]
    beta = gbv[:, DN_H + h:DN_H + h + 1]
    gr = gbt[h:h + 1, :]
    ii = _iota2((CH, CH), 0)
    jj = _iota2((CH, CH), 1)
    dec = jnp.exp(jnp.where(ii >= jj, gc - gr, -1e30))
    eg = jnp.exp(gc)
    gl = gr[:, CH - 1:CH]
    kb = k * beta
    return dict(beta=beta, dec=dec, eg=eg, gl=gl, ekd=jnp.exp(gl - gc), kb=kb, vb=v * beta, kbe=kb * eg)


def _chunk_scores(m, q, k):
    kq = _mm_nt(jnp.concatenate([m["kb"], q], axis=0), k)
    strict = _iota2((CH, CH), 0) > _iota2((CH, CH), 1)
    return jnp.where(strict, kq[0:CH] * m["dec"], 0.0), kq[CH:2 * CH] * m["dec"]


def _dn_local(q, k, v, gb, gbt):
    s = q.shape[0]
    cpb = 4 if (s // CH) % 4 == 0 else 1
    tb = cpb * CH
    nblk = s // tb

    def body(q_ref, k_ref, v_ref, gb_ref, gbt_ref, u_ref, w_ref, qg_ref, kd_ref, t_ref, ai_ref, egl_ref):
        def chunk(c, carry):
            r0 = pl.multiple_of(c * CH, CH)
            rows = pl.ds(r0, CH)
            gbv = gb_ref[rows, :]
            gbt_v = gbt_ref[c]
            qs = [q_ref[rows, _hcols(h)] for h in HEADS]
            ks = [k_ref[rows, _hcols(h)] for h in HEADS]
            ms = [_chunk_scalings(ks[h], v_ref[rows, _hcols(h)], gbv, gbt_v, h) for h in HEADS]
            for h in HEADS:
                qg_ref[rows, _hcols(h)] = (qs[h] * ms[h]["eg"]).astype(BF16)
                kd_ref[rows, _hcols(h)] = (ks[h] * ms[h]["ekd"]).astype(BF16)
                egl_ref[c, h:h + 1, :] = jnp.broadcast_to(jnp.exp(ms[h]["gl"]), (1, LANE))
            sc = [_chunk_scores(ms[h], qs[h], ks[h]) for h in HEADS]
            for h in HEADS:
                ai_ref[h, rows, :] = sc[h][1]
            eye = jnp.where(_iota2((CH, CH), 0) == _iota2((CH, CH), 1), 1.0, 0.0).astype(F32)
            ts = [eye - sc[h][0] for h in HEADS]
            ps = [_mm_3x(sc[h][0], sc[h][0]) for h in HEADS]
            ts = [ts[h] + _mm_3x(ts[h], ps[h]) for h in HEADS]
            for _ in range(4):
                ps = [_mm(ps[h], ps[h]) for h in HEADS]
                ts = [ts[h] + _mm(ts[h], ps[h]) for h in HEADS]
            for h in HEADS:
                t_ref[h, rows, :] = ts[h]
                uw = _mm(ts[h], jnp.concatenate([ms[h]["vb"], ms[h]["kbe"]], axis=1))
                u_ref[rows, _hcols(h)] = uw[:, 0:HEAD]
                w_ref[rows, _hcols(h)] = uw[:, HEAD:2 * HEAD].astype(BF16)
            return carry

        lax.fori_loop(0, cpb, chunk, 0)

    row = pl.BlockSpec((tb, DN_W), lambda i: (i, 0))
    sq = pl.BlockSpec((DN_H, tb, CH), lambda i: (0, i, 0))
    return pl.pallas_call(
        body, name="dn_local", grid=(nblk,),
        in_specs=[row, row, row, pl.BlockSpec((tb, LANE), lambda i: (i, 0)),
                  pl.BlockSpec((cpb, 2 * DN_H, CH), lambda i: (i, 0, 0))],
        out_specs=[row, row, row, row, sq, sq, pl.BlockSpec((cpb, DN_H, LANE), lambda i: (i, 0, 0))],
        out_shape=[_sds((s, DN_W)), _sds((s, DN_W), BF16), _sds((s, DN_W), BF16), _sds((s, DN_W), BF16),
                   _sds((DN_H, s, CH)), _sds((DN_H, s, CH)), _sds((s // CH, DN_H, LANE))],
        compiler_params=_params(("parallel",)),
    )(q, k, v, gb, gbt)


def _scan_cpb(s):
    return 8 if (s // CH) % 8 == 0 else 1


def _dn_scan(u, w, qg, kd, ai, egl, proj, norm_g):
    s = u.shape[0]
    cpb = _scan_cpb(s)
    tb = cpb * CH
    nblk = s // tb

    def body(u_ref, w_ref, qg_ref, kd_ref, ai_ref, egl_ref, z_ref, ng_ref, o_ref, vn_ref, st_ref, ob_ref, state):
        @pl.when(pl.program_id(0) == 0)
        def _():
            state[...] = jnp.zeros_like(state)

        ng = ng_ref[...]

        def chunk(c, carry):
            r0 = pl.multiple_of(c * CH, CH)
            rows = pl.ds(r0, CH)
            sts = [state[h] for h in HEADS]
            stb = [sts[h].astype(BF16) for h in HEADS]
            for h in HEADS:
                st_ref[c, h] = sts[h]
            vns = [u_ref[rows, _hcols(h)] - jnp.dot(w_ref[rows, _hcols(h)], stb[h], preferred_element_type=F32)
                   for h in HEADS]
            vnb = [vns[h].astype(BF16) for h in HEADS]
            for h in HEADS:
                state[h] = sts[h] * egl_ref[c, h:h + 1, :] + _mm_tn(kd_ref[rows, _hcols(h)], vnb[h])
            os_ = [jnp.dot(qg_ref[rows, _hcols(h)], stb[h], preferred_element_type=F32) + _mm(ai_ref[h, rows, :], vnb[h])
                   for h in HEADS]
            for h in HEADS:
                o = os_[h]
                vn_ref[rows, _hcols(h)] = vns[h]
                o_ref[rows, _hcols(h)] = o
                r = lax.rsqrt(jnp.mean(o * o, axis=-1, keepdims=True) + EPS)
                ob_ref[rows, _hcols(h)] = (o * r * ng * _silu(z_ref[rows, _hcols(h)])).astype(BF16)
            return carry

        lax.fori_loop(0, cpb, chunk, 0)

    row = pl.BlockSpec((tb, DN_W), lambda i: (i, 0))
    return pl.pallas_call(
        body, name="dn_scan", grid=(nblk,),
        in_specs=[row, row, row, row, pl.BlockSpec((DN_H, tb, CH), lambda i: (0, i, 0)),
                  pl.BlockSpec((cpb, DN_H, LANE), lambda i: (i, 0, 0)), pl.BlockSpec((tb, DN_W), lambda i: (i, 3)),
                  pl.BlockSpec((1, HEAD), lambda i: (0, 0))],
        out_specs=[row, row, pl.BlockSpec((cpb, DN_H, HEAD, HEAD), lambda i: (i, 0, 0, 0)), row],
        out_shape=[_sds((s, DN_W)), _sds((s, DN_W)), _sds((s // CH, DN_H, HEAD, HEAD)), _sds((s, DN_W), BF16)],
        scratch_shapes=[pltpu.VMEM((DN_H, HEAD, HEAD), F32)],
        compiler_params=_params(("arbitrary",)),
    )(u, w, qg, kd, ai, egl, proj, norm_g)


def _dn_scan_bwd(dmixed, o, proj, norm_g, w, qg, kd, ai, egl):
    s = o.shape[0]
    cpb = _scan_cpb(s)
    tb = cpb * CH
    nblk = s // tb

    def body(dm_ref, o_ref, z_ref, ng_ref, w_ref, qg_ref, kd_ref, ai_ref, egl_ref,
             do_ref, dvn_ref, dst_ref, dz_ref, dng_ref, dstate):
        @pl.when(pl.program_id(0) == 0)
        def _():
            dstate[...] = jnp.zeros_like(dstate)
            dng_ref[...] = jnp.zeros_like(dng_ref)

        ng = ng_ref[...]

        def chunk(cc, carry):
            c = cpb - 1 - cc
            r0 = pl.multiple_of(c * CH, CH)
            rows = pl.ds(r0, CH)
            dng = jnp.zeros((1, HEAD), F32)
            dob = []
            for h in HEADS:
                cs = _hcols(h)
                o = o_ref[rows, cs]
                z = z_ref[rows, cs]
                db = dm_ref[rows, cs]
                r = lax.rsqrt(jnp.mean(o * o, axis=-1, keepdims=True) + EPS)
                ohat = o * r
                dz_ref[rows, cs] = (db * ohat * ng * _silu_grad(z)).astype(BF16)
                dyn = db * _silu(z)
                dng += _colsum(dyn * ohat)
                doh = dyn * ng
                do = r * (doh - ohat * jnp.mean(doh * ohat, axis=-1, keepdims=True))
                do_ref[rows, cs] = do
                dob.append(do.astype(BF16))
            dng_ref[...] += dng
            dsn = [dstate[h] for h in HEADS]
            for h in HEADS:
                dst_ref[c, h] = dsn[h]
            dvn = [_mm_tn(ai_ref[h, rows, :], dob[h])
                   + jnp.dot(kd_ref[rows, _hcols(h)], dsn[h].astype(BF16), preferred_element_type=F32) for h in HEADS]
            part = [_mm_tn(qg_ref[rows, _hcols(h)], dob[h]) + egl_ref[c, h:h + 1, :] * dsn[h] for h in HEADS]
            for h in HEADS:
                dvn_ref[rows, _hcols(h)] = dvn[h]
                dstate[h] = part[h] - _mm_tn(w_ref[rows, _hcols(h)], dvn[h])
            return carry

        lax.fori_loop(0, cpb, chunk, 0)

    rev = lambda i: (nblk - 1 - i, 0)
    row = pl.BlockSpec((tb, DN_W), rev)
    vec = pl.BlockSpec((1, HEAD), lambda i: (0, 0))
    return pl.pallas_call(
        body, name="dn_scan_bwd", grid=(nblk,),
        in_specs=[row, row, pl.BlockSpec((tb, DN_W), lambda i: (nblk - 1 - i, 3)), vec, row, row, row,
                  pl.BlockSpec((DN_H, tb, CH), lambda i: (0, nblk - 1 - i, 0)),
                  pl.BlockSpec((cpb, DN_H, LANE), lambda i: (nblk - 1 - i, 0, 0))],
        out_specs=[row, row, pl.BlockSpec((cpb, DN_H, HEAD, HEAD), lambda i: (nblk - 1 - i, 0, 0, 0)), row, vec],
        out_shape=[_sds((s, DN_W)), _sds((s, DN_W)), _sds((s // CH, DN_H, HEAD, HEAD)), _sds((s, DN_W), BF16),
                   _sds((1, HEAD))],
        scratch_shapes=[pltpu.VMEM((DN_H, HEAD, HEAD), F32)],
        compiler_params=_params(("arbitrary",)),
    )(dmixed, o, proj, norm_g, w, qg, kd, ai, egl)


def _dn_local_bwd(q, k, v, gb, gbt, t, vn, st, dst, do, dvn):
    s = q.shape[0]
    cpb = 4 if (s // CH) % 4 == 0 else 1
    tb = cpb * CH
    nblk = s // tb

    def body(q_ref, k_ref, v_ref, gb_ref, gbt_ref, t_ref, vn_ref, st_ref, dst_ref, do_ref, dvn_ref,
             dq_ref, dk_ref, dv_ref, dgb_ref):
        lane = _iota2((CH, LANE), 1)
        last = _iota2((CH, 1), 0) == CH - 1

        def chunk(c, carry):
            r0 = pl.multiple_of(c * CH, CH)
            rows = pl.ds(r0, CH)
            gbv = gb_ref[rows, :]
            gbt_v = gbt_ref[c]
            strict = _iota2((CH, CH), 0) > _iota2((CH, CH), 1)
            qs = [q_ref[rows, _hcols(h)] for h in HEADS]
            ks = [k_ref[rows, _hcols(h)] for h in HEADS]
            vs = [v_ref[rows, _hcols(h)] for h in HEADS]
            ms = [_chunk_scalings(ks[h], vs[h], gbv, gbt_v, h) for h in HEADS]
            sts = [st_ref[c, h] for h in HEADS]
            dsn = [dst_ref[c, h] for h in HEADS]
            dob = [do_ref[rows, _hcols(h)].astype(BF16) for h in HEADS]
            dvnb = [dvn_ref[rows, _hcols(h)].astype(BF16) for h in HEADS]
            vnb = [vn_ref[rows, _hcols(h)].astype(BF16) for h in HEADS]
            tbf = [t_ref[h, rows, :].astype(BF16) for h in HEADS]
            sc = [_chunk_scores(ms[h], qs[h], ks[h]) for h in HEADS]
            xs_ = [_mm_nt(jnp.concatenate([dob[h], dvnb[h]], axis=0), sts[h]) for h in HEADS]
            dai = [_mm_nt(dob[h], vnb[h]) for h in HEADS]
            dkd = [_mm_nt(vnb[h], dsn[h]) for h in HEADS]
            dqg = [xs_[h][0:CH] for h in HEADS]
            duw = [jnp.concatenate([dvnb[h], (-xs_[h][CH:2 * CH]).astype(BF16)], axis=1) for h in HEADS]
            dt = [_mm_nt(duw[h], jnp.concatenate([ms[h]["vb"], ms[h]["kbe"]], axis=1)) for h in HEADS]
            dvk = [_mm_tn(tbf[h], duw[h]) for h in HEADS]
            tdt = [_mm_tn(tbf[h], dt[h]) for h in HEADS]
            da = [jnp.where(strict, -_mm_nt(tdt[h], tbf[h]), 0.0) for h in HEADS]
            dsc = [jnp.concatenate([da[h] * ms[h]["dec"], dai[h] * ms[h]["dec"]], axis=0) for h in HEADS]
            dkq = [_mm(dsc[h], ks[h]) for h in HEADS]
            dk1 = [_mm_tn(dsc[h], jnp.concatenate([ms[h]["kb"], qs[h]], axis=0)) for h in HEADS]
            dgb = jnp.zeros((CH, LANE), F32)
            for h in HEADS:
                m = ms[h]
                eg, ekd, beta = m["eg"], m["ekd"], m["beta"]
                dvb = dvk[h][:, 0:HEAD]
                dkbe = dvk[h][:, HEAD:2 * HEAD]
                kd = ks[h] * ekd
                dkb = dkq[h][0:CH] + dkbe * eg
                dq_ref[rows, _hcols(h)] = dkq[h][CH:2 * CH] + dqg[h] * eg
                dk_ref[rows, _hcols(h)] = dk1[h] + dkd[h] * ekd + dkb * beta
                dv_ref[rows, _hcols(h)] = dvb * beta
                dgl = jnp.exp(m["gl"]) * _colsum(_rowsum(sts[h] * dsn[h])) + _colsum(_rowsum(dkd[h] * kd))
                mm_ = da[h] * sc[h][0] + dai[h] * sc[h][1]
                dgc = (_rowsum(mm_) - _rowsum(mm_.T) + _rowsum(dqg[h] * qs[h] * eg) - _rowsum(dkd[h] * kd)
                       + _rowsum(dkbe * m["kbe"]) + jnp.where(last, dgl, 0.0))
                dbeta = _rowsum(dkb * ks[h]) + _rowsum(dvb * vs[h])
                dgb = jnp.where(lane == h, dgc, jnp.where(lane == DN_H + h, dbeta, dgb))
            dgb_ref[rows, :] = dgb
            return carry

        lax.fori_loop(0, cpb, chunk, 0)

    row = pl.BlockSpec((tb, DN_W), lambda i: (i, 0))
    gbs = pl.BlockSpec((tb, LANE), lambda i: (i, 0))
    sts = pl.BlockSpec((cpb, DN_H, HEAD, HEAD), lambda i: (i, 0, 0, 0))
    return pl.pallas_call(
        body, name="dn_local_bwd", grid=(nblk,),
        in_specs=[row, row, row, gbs, pl.BlockSpec((cpb, 2 * DN_H, CH), lambda i: (i, 0, 0)),
                  pl.BlockSpec((DN_H, tb, CH), lambda i: (0, i, 0)), row, sts, sts, row, row],
        out_specs=[row, row, row, gbs],
        out_shape=[_sds((s, DN_W)), _sds((s, DN_W)), _sds((s, DN_W)), _sds((s, LANE))],
        compiler_params=_params(("parallel",)),
    )(q, k, v, gb, gbt, t, vn, st, dst, do, dvn)


def _dn_pre_bwd(proj, yc_all, ab, conv_w, alog_row, dt_row, dq, dk, dv, dgb):
    s = proj.shape[0]
    tm = _tile(s, (256, 128))
    w3 = 3 * DN_W
    nblk = s // tm

    def body(x_ref, yc_ref, ab_ref, cw_ref, al_ref, dt_ref, dq_ref, dk_ref, dv_ref, dgb_ref,
             dx_ref, dab_ref, dcw_ref, dal_ref, ddt_ref, exd, carry):
        i = pl.program_id(0)

        @pl.when(i == 0)
        def _():
            carry[...] = jnp.zeros_like(carry)
            dcw_ref[...] = jnp.zeros_like(dcw_ref)
            dal_ref[...] = jnp.zeros_like(dal_ref)
            ddt_ref[...] = jnp.zeros_like(ddt_ref)

        yc = yc_ref[...]
        sg = jax.nn.sigmoid(yc)
        act = yc * sg
        dact = sg * (1.0 + yc * (1.0 - sg))
        for h in range(DN_H):
            cs = slice(h * HEAD, (h + 1) * HEAD)
            ks = slice(DN_W + h * HEAD, DN_W + (h + 1) * HEAD)
            qa = act[:, cs]
            rq = lax.rsqrt(_rowsum(qa * qa) + EPS)
            qh = qa * rq
            dqv = dq_ref[:, cs]
            exd[0:tm, cs] = (HEAD ** -0.5) * rq * (dqv - qh * _rowsum(dqv * qh)) * dact[:, cs]
            ka = act[:, ks]
            rk = lax.rsqrt(_rowsum(ka * ka) + EPS)
            kh = ka * rk
            dkv = dk_ref[:, cs]
            exd[0:tm, ks] = rk * (dkv - kh * _rowsum(dkv * kh)) * dact[:, ks]
        exd[0:tm, 2 * DN_W:w3] = dv_ref[...] * dact[:, 2 * DN_W:w3]
        exd[tm:tm + HALO, :] = carry[...]
        xv = x_ref[...]
        dx = None
        for t in range(DN_K):
            view = exd[pl.ds(DN_K - 1 - t, tm), :]
            dcw_ref[t:t + 1, :] += _colsum(view * xv)
            term = cw_ref[t:t + 1, :] * view
            dx = term if dx is None else dx + term
        dx_ref[...] = dx.astype(BF16)
        carry[...] = exd[0:HALO, :]

        lane = _iota2((tm, LANE), 1)
        dgbv = dgb_ref[...]
        dg = _mm_hi(_chunk_tri(tm, True), jnp.where(lane < DN_H, dgbv, 0.0))
        abv = ab_ref[...]
        xa = abv + dt_ref[...]
        nea = -jnp.exp(al_ref[...])
        d_da = jnp.where(lane < DN_H, dg * nea * jax.nn.sigmoid(xa), 0.0)
        dal_ref[...] += _colsum(jnp.where(lane < DN_H, dg * nea * _softplus(xa), 0.0))
        ddt_ref[...] += _colsum(d_da)
        beta = jax.nn.sigmoid(abv)
        d_db = jnp.where((lane >= DN_H) & (lane < 2 * DN_H), dgbv * beta * (1.0 - beta), 0.0)
        dab_ref[...] = (d_da + d_db).astype(BF16)

    rev = lambda i: (nblk - 1 - i, 0)
    row = lambda w: pl.BlockSpec((tm, w), rev)
    vec = pl.BlockSpec((1, LANE), lambda i: (0, 0))
    cws = pl.BlockSpec((DN_K, w3), lambda i: (0, 0))
    return pl.pallas_call(
        body, name="dn_pre_bwd", grid=(nblk,),
        in_specs=[row(w3), row(w3), row(LANE), cws, vec, vec, row(DN_W), row(DN_W), row(DN_W), row(LANE)],
        out_specs=[row(w3), row(LANE), cws, vec, vec],
        out_shape=[_sds((s, w3), BF16), _sds((s, LANE), BF16), _sds((DN_K, w3)), _sds((1, LANE)), _sds((1, LANE))],
        scratch_shapes=[pltpu.VMEM((tm + HALO, w3), F32), pltpu.VMEM((HALO, w3), F32)],
        compiler_params=_params(("arbitrary",)),
    )(proj, yc_all, ab, conv_w, alog_row, dt_row, dq, dk, dv, dgb)


def _adam(parts, w, m, v, name):
    r, c = w.shape
    n_parts = parts.shape[0]
    tr = _tile(r, (128, 64, 32, 16, 8))

    def body(p_ref, w_ref, m_ref, v_ref, g_ref, d_ref, nm_ref, nv_ref):
        g = p_ref[0].astype(F32)
        for k in range(1, n_parts):
            g = g + p_ref[k].astype(F32)
        g_ref[...] = g
        mn = ADAM_B1 * m_ref[...] + (1.0 - ADAM_B1) * g
        vn = ADAM_B2 * v_ref[...] + (1.0 - ADAM_B2) * (g * g)
        m_hat = mn / (1.0 - ADAM_B1 ** ADAM_STEP)
        v_hat = vn / (1.0 - ADAM_B2 ** ADAM_STEP)
        d_ref[...] = -ADAM_LR * (m_hat / (jnp.sqrt(v_hat) + ADAM_EPS) + ADAM_WD * w_ref[...])
        nm_ref[...] = mn
        nv_ref[...] = vn

    blk = pl.BlockSpec((tr, c), lambda i: (i, 0))
    return pl.pallas_call(
        body, name=name, grid=(r // tr,),
        in_specs=[pl.BlockSpec((n_parts, tr, c), lambda i: (0, i, 0)), blk, blk, blk],
        out_specs=[blk, blk, blk, blk], out_shape=[_sds((r, c))] * 4,
        compiler_params=_params(("parallel",)),
    )(parts, w, m, v)


_PACK_ROWS = 8


def _pack(vals):
    tiles = []
    for a in vals:
        flat = a.reshape(-1).astype(F32)
        unit = _PACK_ROWS * LANE
        n = -(-flat.shape[0] // unit) * unit
        tiles.append(jnp.pad(flat, (0, n - flat.shape[0])).reshape(n // LANE, LANE))
    return jnp.concatenate(tiles, axis=0)


def _unpack(packed, shapes):
    out = []
    r0 = 0
    for shp in shapes:
        size = 1
        for dim in shp:
            size *= dim
        unit = _PACK_ROWS * LANE
        rows = -(-size // unit) * _PACK_ROWS
        out.append(packed[r0:r0 + rows].reshape(-1)[:size].reshape(shp))
        r0 += rows
    return out


def _lane_row(vec8):
    return jnp.pad(vec8.reshape(1, -1).astype(F32), ((0, 0), (0, LANE - vec8.size)))


def kernel(x, mem, ln_g, w_in, gmlp_ln_g, gmlp_ln_b, gmlp_ws, gmlp_bs, conv_w, dn_a_log, dn_dt_bias, dn_norm_g, mem_norm_g, w_mem_kv, w_out, final_g, loss_target, m_ln_g, m_w_in, m_gmlp_ln_g, m_gmlp_ln_b, m_gmlp_ws, m_gmlp_bs, m_conv_w, m_dn_a_log, m_dn_dt_bias, m_dn_norm_g, m_mem_norm_g, m_w_mem_kv, m_w_out, m_final_g, v_ln_g, v_w_in, v_gmlp_ln_g, v_gmlp_ln_b, v_gmlp_ws, v_gmlp_bs, v_conv_w, v_dn_a_log, v_dn_dt_bias, v_dn_norm_g, v_mem_norm_g, v_w_mem_kv, v_w_out, v_final_g):
    xs = x[0]
    mems = mem[0]
    tgt = loss_target[0]
    s, d = xs.shape
    shard_w = w_in.shape[2]
    in_w = N_DEV * shard_w
    me = 4 * lax.axis_index("x") + 2 * lax.axis_index("y") + lax.axis_index("c")

    g_in, g_out, g_kv, g_conv = _gather_two_level(
        [w_in[0].astype(BF16), w_out[0].astype(BF16), w_mem_kv[0].astype(BF16), conv_w[0]], "gather_weights")
    o_g, o_dn, o_ab = 0, 3 * GMLP_W, 3 * GMLP_W + 4 * DN_W
    o_xa = o_ab + 2 * DN_H

    def shard_cols(lo, hi):
        out = []
        while lo < hi:
            sh = lo // shard_w
            end = min(hi, (sh + 1) * shard_w)
            out.append(g_in[sh][:, lo - sh * shard_w:end - sh * shard_w])
            lo = end
        return out

    w_main = jnp.concatenate(shard_cols(o_dn, o_ab) + shard_cols(o_g, o_dn) + shard_cols(o_xa, in_w), axis=1)
    w_ab = jnp.pad(jnp.concatenate(shard_cols(o_ab, o_xa), axis=1), ((0, 0), (0, LANE - 2 * DN_H)))
    wo = g_out.reshape(MIX_W, d)
    wo_perm = jnp.concatenate([wo[GMLP_W:GMLP_W + DN_W], wo[0:GMLP_W], wo[GMLP_W + DN_W:MIX_W]], axis=0)
    w_kv = g_kv.reshape(d, 2 * XA_W)
    conv_full = g_conv.transpose(1, 0, 2).reshape(DN_K, 3 * DN_W)

    ln_g2 = ln_g.reshape(1, d)
    lng2 = gmlp_ln_g.reshape(1, GMLP_W)
    lnb2 = gmlp_ln_b.reshape(1, GMLP_W)
    ws3 = gmlp_ws[0]
    bs_t = gmlp_bs[0].T
    alog_row = _lane_row(dn_a_log)
    dt_row = _lane_row(dn_dt_bias)
    dn_g2 = dn_norm_g.reshape(1, HEAD)
    mem_g2 = mem_norm_g.reshape(1, d)
    fin_g2 = final_g.reshape(1, d)

    proj, ab, h_t = _inproj(xs, ln_g2, w_main, w_ab)
    out_a = _gmlp_fwd(proj, lng2, lnb2, ws3, bs_t)
    mkv = _memkv_fwd(mems, mem_g2, w_kv)
    out_c = _xattn_fwd(proj, mkv)
    q, k, v, gb, gbt, yc = _dn_pre(proj, ab, conv_full, alog_row, dt_row)
    u, wk, qg, kd, tmat, ai, egl = _dn_local(q, k, v, gb, gbt)
    o, vn, st, out_b = _dn_scan(u, wk, qg, kd, ai, egl, proj, dn_g2)

    dx2, dx2b, dmixed, loss_acc, d_fin_g = _final(xs, tgt, out_b, out_a, out_c, wo_perm, fin_g2)
    loss = lax.psum(loss_acc[0, 0], ("x", "y", "c"))

    dwo_b = _matmul_tn(out_b, dx2b, "dw_out_b")
    dwo_a = _matmul_tn(out_a, dx2b, "dw_out_a")
    dwo_c = _matmul_tn(out_c, dx2b, "dw_out_c")
    d_w_out = jnp.concatenate([dwo_a, dwo_b, dwo_c], axis=0)

    dp_g, d_ws, d_bst, d_lng, d_lnb = _gmlp_bwd(proj, dmixed, lng2, lnb2, ws3, bs_t)
    dp_x, dmkv = _xattn_bwd(proj, dmixed, mkv)
    d_w_kv, d_mem_g = _memkv_bwd(mems, mem_g2, w_kv, dmkv)
    do, dvn, dst, dp_dz, d_dn_g = _dn_scan_bwd(dmixed, o, proj, dn_g2, wk, qg, kd, ai, egl)
    dq, dk, dv, dgb = _dn_local_bwd(q, k, v, gb, gbt, tmat, vn, st, dst, do, dvn)
    dp_qkv, dp_ab, d_conv, d_alog, d_dt = _dn_pre_bwd(proj, yc, ab, conv_full, alog_row, dt_row, dq, dk, dv, dgb)

    terms = [(dp_qkv, w_main, 3 * DN_W, 0, 0), (dp_dz, w_main, DN_W, 0, 3)]
    terms += [(dp_g, w_main, GMLP_W, b, GU_BLK + b) for b in range(3)]
    terms += [(dp_x, w_main, XA_W, b, CQ_BLK + b) for b in range(2)]
    terms += [(dp_ab, w_ab, LANE, 0, 0)]
    dh = _dh(terms)
    grad_x, d_ln_g = _rms_bwd(xs, dh, dx2, ln_g2)
    dw_qkv = _matmul_acc(h_t, dp_qkv, "dw_in_qkv")
    dw_dz = _matmul_acc(h_t, dp_dz, "dw_in_dz")
    dw_gm = _matmul_acc(h_t, dp_g, "dw_in_gmlp")
    dw_xa = _matmul_acc(h_t, dp_x, "dw_in_xa")
    dw_ab = _matmul_acc(h_t, dp_ab, "dw_in_ab")
    segs = [(o_g, dw_gm), (o_dn, dw_qkv), (o_dn + 3 * DN_W, dw_dz), (o_ab, dw_ab[:, :2 * DN_H]), (o_xa, dw_xa)]
    shards = []
    for sh in range(N_DEV):
        lo, hi = sh * shard_w, (sh + 1) * shard_w
        parts = [arr[:, max(lo, off) - off:min(hi, off + arr.shape[1]) - off] for off, arr in segs
                 if off < hi and off + arr.shape[1] > lo]
        shards.append(jnp.concatenate(parts, axis=1).astype(BF16))
    send_in = jnp.stack(shards)

    small_shapes = [ln_g.shape, gmlp_ln_g.shape, gmlp_ln_b.shape, gmlp_ws.shape, gmlp_bs.shape, dn_a_log.shape,
                    dn_dt_bias.shape, dn_norm_g.shape, mem_norm_g.shape, final_g.shape, (DN_K, 3 * DN_W)]
    small_g = _pack([d_ln_g, d_lng, d_lnb, d_ws, d_bst.T, d_alog[:, :DN_H], d_dt[:, :DN_H], d_dn_g, d_mem_g, d_fin_g,
                     d_conv])
    zc = jnp.zeros((DN_K, 3 * DN_W), F32)
    small_w = _pack([ln_g, gmlp_ln_g, gmlp_ln_b, gmlp_ws, gmlp_bs, dn_a_log, dn_dt_bias, dn_norm_g, mem_norm_g, final_g, zc])
    small_m = _pack([m_ln_g, m_gmlp_ln_g, m_gmlp_ln_b, m_gmlp_ws, m_gmlp_bs, m_dn_a_log, m_dn_dt_bias, m_dn_norm_g,
                     m_mem_norm_g, m_final_g, zc])
    small_v = _pack([v_ln_g, v_gmlp_ln_g, v_gmlp_ln_b, v_gmlp_ws, v_gmlp_bs, v_dn_a_log, v_dn_dt_bias, v_dn_norm_g,
                     v_mem_norm_g, v_final_g, zc + 1.0])

    send_out =d_w_out.reshape(N_DEV, MIX_W // N_DEV, d).astype(BF16)
    send_kv = d_w_kv.reshape(N_DEV, d // N_DEV, 2 * XA_W).astype(BF16)
    sends = [send_in, send_out, send_kv]
    all_small, got = _swap_halves(small_g, sends, "swap_halves")
    core = lax.axis_index("c").astype(jnp.int32).reshape(1)
    chip_sums = [_pair_sum(core, sends[i], got[i], "pair_sum_%d" % i) for i in range(3)]
    r_in, r_out, r_kv = _chip_exchange(chip_sums, "chip_exchange")

    g_w_in, dl_w_in, nm_w_in, nv_w_in = _adam(r_in, w_in[0], m_w_in[0], v_w_in[0], "adam_w_in")
    g_w_out, dl_w_out, nm_w_out, nv_w_out = _adam(r_out, w_out[0], m_w_out[0], v_w_out[0], "adam_w_out")
    g_w_kv, dl_w_kv, nm_w_kv, nv_w_kv = _adam(r_kv, w_mem_kv[0], m_w_mem_kv[0], v_w_mem_kv[0], "adam_w_kv")
    sm = [_unpack(t, small_shapes) for t in _adam(all_small, small_w, small_m, small_v, "adam_small")]

    conv_parts = lax.dynamic_slice(all_small, (0, all_small.shape[1] - (DN_K * 3 * DN_W) // LANE, 0),
                                   (N_DEV, (DN_K * 3 * DN_W) // LANE, LANE)).reshape(N_DEV, DN_K, 3 * DN_W)
    cshard = conv_w.shape[2]
    conv_parts = lax.dynamic_slice(conv_parts, (0, 0, me * cshard), (N_DEV, DN_K, cshard))
    cpad = ((0, 0), (0, HALO - DN_K), (0, 0))
    conv_res = _adam(jnp.pad(conv_parts, cpad), jnp.pad(conv_w[0], cpad[1:]), jnp.pad(m_conv_w[0], cpad[1:]),
                     jnp.pad(v_conv_w[0], cpad[1:], constant_values=1.0), "adam_conv")
    g_conv_s, dl_conv, nm_conv, nv_conv = [t[:DN_K][None] for t in conv_res]

    def group(idx, big_in, big_conv, big_kv, big_out):
        names = sm[idx]
        return [names[0], big_in[None], names[1], names[2], names[3], names[4], big_conv, names[5], names[6], names[7],
                names[8], big_kv[None], big_out[None], names[9]]

    grads = group(0, g_w_in, g_conv_s, g_w_kv, g_w_out)
    deltas = group(1, dl_w_in, dl_conv, dl_w_kv, dl_w_out)
    new_m = group(2, nm_w_in, nm_conv, nm_w_kv, nm_w_out)
    new_v = group(3, nv_w_in, nv_conv, nv_w_kv, nv_w_out)
    return (loss, grad_x[None], *grads, *deltas, *new_m, *new_v)
```

```python
import functools

import jax
import jax.numpy as jnp
from jax import lax
from jax.experimental import pallas as pl
from jax.experimental.pallas import tpu as pltpu

F32 = jnp.float32
BF16 = jnp.bfloat16
HIGHEST = lax.Precision.HIGHEST
MESH_ID = pl.DeviceIdType.MESH

N_DEV = 8
EPS = 1e-6
GMLP_W = 512
GMLP_G = 4
GMLP_T = 128
DN_W = 1024
DN_H = 8
HEAD = 128
DN_K = 4
CH = 64
XA_W = 512
XA_H = 4
LANE = 128
HALO = 8
MAIN_W = 4 * DN_W + 3 * GMLP_W + 2 * XA_W
MIX_W = DN_W + GMLP_W + XA_W
VMEM_LIMIT = 56 * 1024 * 1024

ADAM_LR = 0.001
ADAM_B1 = 0.9
ADAM_B2 = 0.999
ADAM_EPS = 1e-08
ADAM_WD = 0.01
ADAM_STEP = 10


def _sds(shape, dtype=F32):
    return jax.ShapeDtypeStruct(tuple(shape), dtype)


def _params(sem=None):
    if sem is None:
        return pltpu.CompilerParams(vmem_limit_bytes=VMEM_LIMIT)
    return pltpu.CompilerParams(dimension_semantics=tuple(sem), vmem_limit_bytes=VMEM_LIMIT)


def _tile(n, prefs):
    for p in prefs:
        if n % p == 0:
            return p
    return n


def _mm(a, b):
    return jnp.dot(a.astype(BF16), b.astype(BF16), preferred_element_type=F32)


def _mm_nt(a, b):
    return lax.dot_general(a.astype(BF16), b.astype(BF16), (((1,), (1,)), ((), ())), preferred_element_type=F32)


def _mm_tn(a, b):
    return lax.dot_general(a.astype(BF16), b.astype(BF16), (((0,), (0,)), ((), ())), preferred_element_type=F32)


def _mm_hi(a, b):
    return jnp.dot(a, b, precision=HIGHEST, preferred_element_type=F32)


def _mm_3x(a, b):
    return jnp.dot(a, b, precision=lax.Precision.HIGH, preferred_element_type=F32)


_GELU_C = 0.7978845608028654
_GELU_A = 0.044715


def _gelu(x):
    return 0.5 * x * (1.0 + jnp.tanh(_GELU_C * (x + _GELU_A * x * x * x)))


def _gelu_grad(x):
    t = jnp.tanh(_GELU_C * (x + _GELU_A * x * x * x))
    return 0.5 * (1.0 + t) + 0.5 * x * (1.0 - t * t) * _GELU_C * (1.0 + 3.0 * _GELU_A * x * x)


def _silu(x):
    return x * jax.nn.sigmoid(x)


def _silu_grad(x):
    s = jax.nn.sigmoid(x)
    return s * (1.0 + x * (1.0 - s))


def _rowsum(x):
    return jnp.sum(x, axis=-1, keepdims=True)


def _colsum(x):
    return jnp.sum(x, axis=0, keepdims=True)


def _iota2(shape, dim):
    return lax.broadcasted_iota(jnp.int32, shape, dim)


def _chunk_tri(tm, upper):
    r = _iota2((tm, tm), 0)
    c = _iota2((tm, tm), 1)
    same = lax.shift_right_logical(r, 6) == lax.shift_right_logical(c, 6)
    tri = (r <= c) if upper else (r >= c)
    return jnp.where(same & tri, 1.0, 0.0).astype(F32)


N_CHIP = 4


def _mesh_place():
    x, y, c = lax.axis_index("x"), lax.axis_index("y"), lax.axis_index("c")
    chips = [(1 - x, y), (x, 1 - y), (1 - x, 1 - y)]
    return x, y, c, (x, y, 1 - c), chips


class _Gather:
    def __init__(self, ins, outs, send_sems, recv_sems, loc_sems):
        self.ins, self.outs, self.send_sems, self.recv_sems, self.loc_sems = ins, outs, send_sems, recv_sems, loc_sems
        self.x, self.y, self.c, self.sib, self.chips = _mesh_place()
        self.me = (self.x, self.y, self.c)

    def copy(self, a, k, block, to, src=None):
        slot = self.outs[a].at[4 * block[0] + 2 * block[1] + block[2]]
        return pltpu.make_async_remote_copy(
            src_ref=slot if src is None else src, dst_ref=slot, send_sem=self.send_sems.at[a, k],
            recv_sem=self.recv_sems.at[a, k], device_id=to, device_id_type=MESH_ID)

    def own(self, a):
        return pltpu.make_async_copy(self.ins[a], self.outs[a].at[4 * self.x + 2 * self.y + self.c], self.loc_sems.at[a])

    def first(self, a):
        return [self.copy(a, 0, self.me, self.sib, src=self.ins[a])] + [
            self.copy(a, 1 + j, self.me, (*chip, self.c), src=self.ins[a]) for j, chip in enumerate(self.chips)]

    def passed(self, a, j):
        return self.copy(a, 4 + j, (*self.chips[j], self.c), self.sib)

    def start(self):
        for a in range(len(self.ins)):
            self.own(a).start()
            for cp in self.first(a):
                cp.start()

    def finish(self):
        n = len(self.ins)
        for a in range(n):
            for j, chip in enumerate(self.chips):
                self.copy(a, 1 + j, (*chip, self.c), self.me).wait_recv()
                self.passed(a, j).start()
        for a in range(n):
            self.copy(a, 0, self.sib, self.me).wait_recv()
            for j, chip in enumerate(self.chips):
                self.copy(a, 4 + j, (*chip, 1 - self.c), self.me).wait_recv()
        for a in range(n):
            for cp in self.first(a) + [self.passed(a, j) for j in range(N_CHIP - 1)]:
                cp.wait_send()
            self.own(a).wait()

    @staticmethod
    def sems(n):
        return [pltpu.SemaphoreType.DMA((n, N_DEV - 1)), pltpu.SemaphoreType.DMA((n, N_DEV - 1)),
                pltpu.SemaphoreType.DMA((n,))]


def _gather_two_level(arrs, name):
    n = len(arrs)

    def body(*refs):
        g = _Gather(refs[:n], refs[n:2 * n], *refs[2 * n:])
        g.start()
        g.finish()

    any_spec = pl.BlockSpec(memory_space=pl.ANY)
    return pl.pallas_call(
        body, name=name, out_shape=[_sds((N_DEV,) + a.shape, a.dtype) for a in arrs],
        in_specs=[any_spec] * n, out_specs=[any_spec] * n, scratch_shapes=_Gather.sems(n),
        compiler_params=pltpu.CompilerParams(has_side_effects=True),
    )(*arrs)


def _swap_halves(small, grads, name):
    n = len(grads)

    def body(*refs):
        small_ref = refs[0]
        ins = refs[1:1 + n]
        small_out = refs[1 + n]
        got = refs[2 + n:2 + 2 * n]
        s_send, s_recv, g_send, g_recv, loc_sem = refs[2 + 2 * n:]
        x, y, c, sib, _ = _mesh_place()
        me = 4 * x + 2 * y + c
        sends, recvs = [], []
        for j in range(1, N_DEV):
            px = 1 - x if (j >> 2) & 1 else x
            py = 1 - y if (j >> 1) & 1 else y
            pc = 1 - c if j & 1 else c
            cp = pltpu.make_async_remote_copy(
                src_ref=small_ref, dst_ref=small_out.at[me], send_sem=s_send.at[j - 1], recv_sem=s_recv.at[j - 1],
                device_id=(px, py, pc), device_id_type=MESH_ID)
            cp.start()
            sends.append(cp)
            recvs.append(pltpu.make_async_remote_copy(
                src_ref=small_ref, dst_ref=small_out.at[4 * px + 2 * py + pc], send_sem=s_send.at[j - 1],
                recv_sem=s_recv.at[j - 1], device_id=(px, py, pc), device_id_type=MESH_ID))
        own = pltpu.make_async_copy(small_ref, small_out.at[me], loc_sem)
        own.start()
        for a in range(n):
            for chip in range(N_CHIP):
                cp = pltpu.make_async_remote_copy(
                    src_ref=ins[a].at[2 * chip + 1 - c], dst_ref=got[a].at[chip], send_sem=g_send.at[a, chip],
                    recv_sem=g_recv.at[a, chip], device_id=sib, device_id_type=MESH_ID)
                cp.start()
                sends.append(cp)
                recvs.append(cp)
        for cp in sends:
            cp.wait_send()
        for cp in recvs:
            cp.wait_recv()
        own.wait()

    half = [_sds((N_CHIP,) + g.shape[1:], g.dtype) for g in grads]
    any_spec = pl.BlockSpec(memory_space=pl.ANY)
    res = pl.pallas_call(
        body, name=name, out_shape=[_sds((N_DEV,) + small.shape, small.dtype)] + half,
        in_specs=[any_spec] * (1 + n), out_specs=[any_spec] * (1 + n),
        scratch_shapes=[pltpu.SemaphoreType.DMA((N_DEV - 1,)), pltpu.SemaphoreType.DMA((N_DEV - 1,)),
                        pltpu.SemaphoreType.DMA((n, N_CHIP)), pltpu.SemaphoreType.DMA((n, N_CHIP)),
                        pltpu.SemaphoreType.DMA],
        compiler_params=pltpu.CompilerParams(has_side_effects=True),
    )(small, *grads)
    return res[0], res[1:]


def _pair_sum(core, mine, got, name):
    nc, r, c = got.shape
    tr = _tile(r, (256, 128, 64, 32, 16))

    def body(core_ref, a_ref, b_ref, o_ref):
        o_ref[...] = (a_ref[...].astype(F32) + b_ref[...].astype(F32)).astype(BF16)

    return pl.pallas_call(
        body, name=name, out_shape=_sds(got.shape, BF16),
        grid_spec=pltpu.PrefetchScalarGridSpec(
            num_scalar_prefetch=1, grid=(nc, r // tr),
            in_specs=[pl.BlockSpec((1, tr, c), lambda i, j, core_ref: (2 * i + core_ref[0], j, 0)),
                      pl.BlockSpec((1, tr, c), lambda i, j, core_ref: (i, j, 0))],
            out_specs=pl.BlockSpec((1, tr, c), lambda i, j, core_ref: (i, j, 0))),
        compiler_params=_params(("parallel", "parallel")),
    )(core, mine, got)


class _ChipExchange:
    def __init__(self, ins, outs, send_sems, recv_sems, loc_sems):
        self.ins, self.outs, self.send_sems, self.recv_sems, self.loc_sems = ins, outs, send_sems, recv_sems, loc_sems
        self.x, self.y, self.c, _, self.chips = _mesh_place()
        self.mine = 2 * self.x + self.y

    def own(self, a):
        return pltpu.make_async_copy(self.ins[a].at[self.mine], self.outs[a].at[self.mine], self.loc_sems.at[a])

    def copy(self, a, j, lands_in):
        chip = self.chips[j]
        return pltpu.make_async_remote_copy(
            src_ref=self.ins[a].at[2 * chip[0] + chip[1]], dst_ref=self.outs[a].at[lands_in],
            send_sem=self.send_sems.at[a, j], recv_sem=self.recv_sems.at[a, j], device_id=(*chip, self.c),
            device_id_type=MESH_ID)

    def start(self):
        for a in range(len(self.ins)):
            self.own(a).start()
            for j in range(N_CHIP - 1):
                self.copy(a, j, self.mine).start()

    def finish(self):
        for a in range(len(self.ins)):
            for j, chip in enumerate(self.chips):
                self.copy(a, j, self.mine).wait_send()
                self.copy(a, j, 2 * chip[0] + chip[1]).wait_recv()
            self.own(a).wait()

    @staticmethod
    def sems(n):
        return [pltpu.SemaphoreType.DMA((n, N_CHIP - 1)), pltpu.SemaphoreType.DMA((n, N_CHIP - 1)),
                pltpu.SemaphoreType.DMA((n,))]


def _inproj(x, ln_g, w_main, w_ab, late):
    s, d = x.shape
    n = w_main.shape[1]
    tm = _tile(s, (512, 256, 128))
    tn = _tile(n, (1664, 512, 128))
    nl = len(late)
    ni, nj = s // tm, n // tn

    def body(*refs):
        x_ref, g_ref, w_ref, wab_ref = refs[:4]
        proj_ref, ab_ref, ht_ref = refs[4 + nl:7 + nl]
        hs = refs[7 + 2 * nl]
        gather = _Gather(refs[4:4 + nl], refs[7 + nl:7 + 2 * nl], *refs[8 + 2 * nl:])
        step = pl.program_id(0) * nj + pl.program_id(1)

        @pl.when(step == 0)
        def _():
            gather.start()

        @pl.when(pl.program_id(1) == 0)
        def _():
            xv = x_ref[...]
            r = lax.rsqrt(jnp.mean(xv * xv, axis=-1, keepdims=True) + EPS)
            hf = xv * r * g_ref[...]
            h = hf.astype(BF16)
            hs[...] = h
            ht_ref[...] = hf.T.astype(BF16)
            ab_ref[...] = jnp.dot(h, wab_ref[...], preferred_element_type=F32)

        proj_ref[...] = jnp.dot(hs[...], w_ref[...], preferred_element_type=F32)

        @pl.when(step == ni * nj - 1)
        def _():
            gather.finish()

    any_spec = pl.BlockSpec(memory_space=pl.ANY)
    res = pl.pallas_call(
        body, name="inproj", grid=(ni, nj),
        in_specs=[pl.BlockSpec((tm, d), lambda i, j: (i, 0)), pl.BlockSpec((1, d), lambda i, j: (0, 0)),
                  pl.BlockSpec((d, tn), lambda i, j: (0, j)), pl.BlockSpec((d, LANE), lambda i, j: (0, 0))]
        + [any_spec] * nl,
        out_specs=[pl.BlockSpec((tm, tn), lambda i, j: (i, j)), pl.BlockSpec((tm, LANE), lambda i, j: (i, 0)),
                   pl.BlockSpec((d, tm), lambda i, j: (0, i))] + [any_spec] * nl,
        out_shape=[_sds((s, n)), _sds((s, LANE)), _sds((d, s), BF16)]
        + [_sds((N_DEV,) + a.shape, a.dtype) for a in late],
        scratch_shapes=[pltpu.VMEM((tm, d), BF16)] + _Gather.sems(nl),
        compiler_params=_params(("arbitrary", "arbitrary")),
    )(x, ln_g, w_main, w_ab, *late)
    return res[0], res[1], res[2], res[3:]


def _matmul_acc(a, b, name):
    m, k = a.shape
    n = b.shape[1]
    tm = _tile(m, (2048, 1024, 512, 256, 128))
    tn = _tile(n, (1024, 512, 256, 128))
    tk = _tile(k, (1024, 512, 256, 128))

    def body(a_ref, b_ref, o_ref):
        @pl.when(pl.program_id(2) == 0)
        def _():
            o_ref[...] = jnp.zeros_like(o_ref)

        o_ref[...] += jnp.dot(a_ref[...], b_ref[...], preferred_element_type=F32)

    return pl.pallas_call(
        body, name=name, grid=(m // tm, n // tn, k // tk),
        in_specs=[pl.BlockSpec((tm, tk), lambda i, j, l: (i, l)), pl.BlockSpec((tk, tn), lambda i, j, l: (l, j))],
        out_specs=pl.BlockSpec((tm, tn), lambda i, j, l: (i, j)),
        out_shape=_sds((m, n)),
        compiler_params=_params(("parallel", "parallel", "arbitrary")),
    )(a, b)


def _matmul_tn(a, b, name):
    k, m = a.shape
    n = b.shape[1]
    tm = _tile(m, (1024, 512, 256, 128))
    tn = _tile(n, (1024, 512, 256, 128))
    tk = _tile(k, (1024, 512, 256, 128))

    def body(a_ref, b_ref, o_ref):
        @pl.when(pl.program_id(2) == 0)
        def _():
            o_ref[...] = jnp.zeros_like(o_ref)

        o_ref[...] += _mm_tn(a_ref[...], b_ref[...])

    return pl.pallas_call(
        body, name=name, grid=(m // tm, n // tn, k // tk),
        in_specs=[pl.BlockSpec((tk, tm), lambda i, j, l: (l, i)), pl.BlockSpec((tk, tn), lambda i, j, l: (l, j))],
        out_specs=pl.BlockSpec((tm, tn), lambda i, j, l: (i, j)),
        out_shape=_sds((m, n)),
        compiler_params=_params(("parallel", "parallel", "arbitrary")),
    )(a, b)


def _dh(terms, chip_sums):
    s = terms[0][0].shape[0]
    d = terms[0][1].shape[0]
    npc = len(terms)
    nx = len(chip_sums)
    tm = _tile(s, (256, 128))
    tn = _tile(d, (1024, 512, 256, 128))
    nj, ni = d // tn, s // tm

    def body(*refs):
        o_ref = refs[2 * npc + nx]
        exch = _ChipExchange(refs[2 * npc:2 * npc + nx], refs[2 * npc + nx + 1:2 * npc + 2 * nx + 1],
                             *refs[2 * npc + 2 * nx + 1:])
        step = pl.program_id(0) * ni + pl.program_id(1)

        @pl.when(step == 0)
        def _():
            exch.start()

        acc = _mm_nt(refs[0][...], refs[npc][...])
        for p in range(1, npc):
            acc += _mm_nt(refs[p][...], refs[npc + p][...])
        o_ref[...] = acc

        @pl.when(step == ni * nj - 1)
        def _():
            exch.finish()

    any_spec = pl.BlockSpec(memory_space=pl.ANY)
    in_specs = [pl.BlockSpec((tm, w), functools.partial(lambda j, i, pb: (i, pb), pb=pb)) for _, _, w, pb, _ in terms]
    in_specs += [pl.BlockSpec((tn, w), functools.partial(lambda j, i, wb: (j, wb), wb=wb)) for _, _, w, _, wb in terms]
    res = pl.pallas_call(
        body, name="dh", grid=(nj, ni), in_specs=in_specs + [any_spec] * nx,
        out_specs=[pl.BlockSpec((tm, tn), lambda j, i: (i, j))] + [any_spec] * nx,
        out_shape=[_sds((s, d))] + [_sds(p.shape, p.dtype) for p in chip_sums],
        scratch_shapes=_ChipExchange.sems(nx),
        compiler_params=_params(("arbitrary", "arbitrary")),
    )(*[t[0] for t in terms], *[t[1] for t in terms], *chip_sums)
    return res[0], res[1:]


def _rms_bwd(x, dh, dx2, ln_g):
    s, d = x.shape
    tm = _tile(s, (256, 128))

    def body(x_ref, dh_ref, dx2_ref, g_ref, gx_ref, dg_ref):
        @pl.when(pl.program_id(0) == 0)
        def _():
            dg_ref[...] = jnp.zeros_like(dg_ref)

        xv = x_ref[...]
        r = lax.rsqrt(jnp.mean(xv * xv, axis=-1, keepdims=True) + EPS)
        xhat = xv * r
        dhv = dh_ref[...]
        dg_ref[...] += _colsum(dhv * xhat)
        dxh = dhv * g_ref[...]
        gx_ref[...] = dx2_ref[...] + r * (dxh - xhat * jnp.mean(dxh * xhat, axis=-1, keepdims=True))

    row = pl.BlockSpec((tm, d), lambda i: (i, 0))
    vec = pl.BlockSpec((1, d), lambda i: (0, 0))
    return pl.pallas_call(
        body, name="rms_bwd", grid=(s // tm,), in_specs=[row, row, row, vec], out_specs=[row, vec],
        out_shape=[_sds((s, d)), _sds((1, d))], compiler_params=_params(("arbitrary",)),
    )(x, dh, dx2, ln_g)


def _final(x, tgt, out_b, out_a, out_c, w_out, final_g):
    s, d = x.shape
    tm = _tile(s, (256, 128))

    def body(x_ref, t_ref, b_ref, a_ref, c_ref, w_ref, g_ref, dx2_ref, dx2b_ref, dm_ref, loss_ref, dg_ref):
        @pl.when(pl.program_id(0) == 0)
        def _():
            loss_ref[...] = jnp.zeros_like(loss_ref)
            dg_ref[...] = jnp.zeros_like(dg_ref)

        x2 = x_ref[...]
        x2 += jnp.dot(b_ref[...], w_ref[0:DN_W, :], preferred_element_type=F32)
        x2 += jnp.dot(a_ref[...], w_ref[DN_W:DN_W + GMLP_W, :], preferred_element_type=F32)
        x2 += jnp.dot(c_ref[...], w_ref[DN_W + GMLP_W:MIX_W, :], preferred_element_type=F32)
        r = lax.rsqrt(jnp.mean(x2 * x2, axis=-1, keepdims=True) + EPS)
        xhat = x2 * r
        g = g_ref[...]
        err = xhat * g - t_ref[...]
        tok = 0.5 * jnp.mean(err * err, axis=-1, keepdims=True)
        loss_ref[...] += jnp.broadcast_to(_colsum(tok), loss_ref.shape)
        dy = err * (1.0 / d)
        dg_ref[...] += _colsum(dy * xhat)
        dxh = dy * g
        dx2 = r * (dxh - xhat * jnp.mean(dxh * xhat, axis=-1, keepdims=True))
        dx2_ref[...] = dx2
        dx2b = dx2.astype(BF16)
        dx2b_ref[...] = dx2b
        dm_ref[...] = _mm_nt(dx2b, w_ref[...])

    row = pl.BlockSpec((tm, d), lambda i: (i, 0))
    vec = pl.BlockSpec((1, d), lambda i: (0, 0))
    return pl.pallas_call(
        body, name="final", grid=(s // tm,),
        in_specs=[row, row, pl.BlockSpec((tm, DN_W), lambda i: (i, 0)), pl.BlockSpec((tm, GMLP_W), lambda i: (i, 0)),
                  pl.BlockSpec((tm, XA_W), lambda i: (i, 0)), pl.BlockSpec((MIX_W, d), lambda i: (0, 0)), vec],
        out_specs=[row, row, pl.BlockSpec((tm, MIX_W), lambda i: (i, 0)), pl.BlockSpec((1, LANE), lambda i: (0, 0)), vec],
        out_shape=[_sds((s, d)), _sds((s, d), BF16), _sds((s, MIX_W)), _sds((1, LANE)), _sds((1, d))],
        compiler_params=_params(("arbitrary",)),
    )(x, tgt, out_b, out_a, out_c, w_out, final_g)


GU_BLK = (4 * DN_W) // GMLP_W


def _gmlp_norm(gv, lng, lnb):
    va = _gelu(gv)
    mu = jnp.mean(va, axis=-1, keepdims=True)
    xc = va - mu
    rstd = lax.rsqrt(jnp.mean(xc * xc, axis=-1, keepdims=True) + EPS)
    vhat = xc * rstd
    return vhat, rstd, vhat * lng + lnb


def _gmlp_fwd(proj, lng, lnb, ws, bs_t):
    s = proj.shape[0]
    tm = _tile(s, (512, 256, 128))

    def body(u_ref, v_ref, z_ref, lng_ref, lnb_ref, ws_ref, bst_ref, o_ref):
        _, _, vn = _gmlp_norm(v_ref[...], lng_ref[...], lnb_ref[...])
        tri = _iota2((GMLP_T, GMLP_T), 0) >= _iota2((GMLP_T, GMLP_T), 1)
        for g in range(GMLP_G):
            cs = slice(g * HEAD, (g + 1) * HEAD)
            w = jnp.where(tri, ws_ref[g], 0.0).astype(BF16)
            b = bst_ref[:, g:g + 1]
            for c in range(tm // GMLP_T):
                rs = slice(c * GMLP_T, (c + 1) * GMLP_T)
                sg = _mm(w, vn[rs, cs]) + b
                o_ref[rs, cs] = (_gelu(u_ref[rs, cs]) * sg * _silu(z_ref[rs, cs])).astype(BF16)

    col = lambda k: pl.BlockSpec((tm, GMLP_W), lambda i: (i, GU_BLK + k))
    vec = pl.BlockSpec((1, GMLP_W), lambda i: (0, 0))
    return pl.pallas_call(
        body, name="gmlp_fwd", grid=(s // tm,),
        in_specs=[col(0), col(1), col(2), vec, vec, pl.BlockSpec((GMLP_G, GMLP_T, GMLP_T), lambda i: (0, 0, 0)),
                  pl.BlockSpec((GMLP_T, GMLP_G), lambda i: (0, 0))],
        out_specs=pl.BlockSpec((tm, GMLP_W), lambda i: (i, 0)), out_shape=_sds((s, GMLP_W), BF16),
        compiler_params=_params(("parallel",)),
    )(proj, proj, proj, lng, lnb, ws, bs_t)


def _gmlp_bwd(proj, dmixed, lng, lnb, ws, bs_t):
    s = proj.shape[0]
    tm = _tile(s, (512, 256, 128))

    def body(u_ref, v_ref, z_ref, d_ref, lng_ref, lnb_ref, ws_ref, bst_ref,
             dp_ref, dws_ref, dbst_ref, dlng_ref, dlnb_ref, dvn):
        @pl.when(pl.program_id(0) == 0)
        def _():
            dws_ref[...] = jnp.zeros_like(dws_ref)
            dbst_ref[...] = jnp.zeros_like(dbst_ref)
            dlng_ref[...] = jnp.zeros_like(dlng_ref)
            dlnb_ref[...] = jnp.zeros_like(dlnb_ref)

        gv = v_ref[...]
        lng_v = lng_ref[...]
        vhat, rstd, vn = _gmlp_norm(gv, lng_v, lnb_ref[...])
        tri = _iota2((GMLP_T, GMLP_T), 0) >= _iota2((GMLP_T, GMLP_T), 1)
        for g in range(GMLP_G):
            cs = slice(g * HEAD, (g + 1) * HEAD)
            w = jnp.where(tri, ws_ref[g], 0.0).astype(BF16)
            b = bst_ref[:, g:g + 1]
            dw_acc = jnp.zeros((GMLP_T, GMLP_T), F32)
            db_acc = jnp.zeros((GMLP_T, 1), F32)
            for c in range(tm // GMLP_T):
                rs = slice(c * GMLP_T, (c + 1) * GMLP_T)
                vn_b = vn[rs, cs]
                sg = _mm(w, vn_b) + b
                gu = u_ref[rs, cs]
                gz = z_ref[rs, cs]
                da = d_ref[rs, cs]
                uact = _gelu(gu)
                sz = _silu(gz)
                ds = da * uact * sz
                dp_ref[rs, cs] = (da * sg * sz * _gelu_grad(gu)).astype(BF16)
                dp_ref[rs, 2 * GMLP_W + g * HEAD:2 * GMLP_W + (g + 1) * HEAD] = (da * uact * sg * _silu_grad(gz)).astype(BF16)
                dw_acc += _mm_nt(ds, vn_b)
                db_acc += _rowsum(ds)
                dvn[rs, cs] = _mm_tn(w, ds)
            dws_ref[g] += jnp.where(tri, dw_acc, 0.0)
            dbst_ref[:, g:g + 1] += db_acc
        dvn_v = dvn[...]
        dlng_ref[...] += _colsum(dvn_v * vhat)
        dlnb_ref[...] += _colsum(dvn_v)
        dvh = dvn_v * lng_v
        dva = rstd * (dvh - jnp.mean(dvh, axis=-1, keepdims=True) - vhat * jnp.mean(dvh * vhat, axis=-1, keepdims=True))
        dp_ref[:, GMLP_W:2 * GMLP_W] = (dva * _gelu_grad(gv)).astype(BF16)

    col = lambda k: pl.BlockSpec((tm, GMLP_W), lambda i: (i, GU_BLK + k))
    vec = pl.BlockSpec((1, GMLP_W), lambda i: (0, 0))
    wsp = pl.BlockSpec((GMLP_G, GMLP_T, GMLP_T), lambda i: (0, 0, 0))
    bsp = pl.BlockSpec((GMLP_T, GMLP_G), lambda i: (0, 0))
    return pl.pallas_call(
        body, name="gmlp_bwd", grid=(s // tm,),
        in_specs=[col(0), col(1), col(2), pl.BlockSpec((tm, GMLP_W), lambda i: (i, DN_W // GMLP_W)), vec, vec, wsp, bsp],
        out_specs=[pl.BlockSpec((tm, 3 * GMLP_W), lambda i: (i, 0)), wsp, bsp, vec, vec],
        out_shape=[_sds((s, 3 * GMLP_W), BF16), _sds((GMLP_G, GMLP_T, GMLP_T)), _sds((GMLP_T, GMLP_G)),
                   _sds((1, GMLP_W)), _sds((1, GMLP_W))],
        scratch_shapes=[pltpu.VMEM((tm, GMLP_W), F32)],
        compiler_params=_params(("arbitrary",)),
    )(proj, proj, proj, dmixed, lng, lnb, ws, bs_t)


CQ_BLK = (4 * DN_W + 3 * GMLP_W) // XA_W


def _memkv_fwd(mem, g, w_kv):
    nm, d = mem.shape

    def body(m_ref, g_ref, w_ref, kv_ref):
        mv = m_ref[...]
        r = lax.rsqrt(jnp.mean(mv * mv, axis=-1, keepdims=True) + EPS)
        kv_ref[...] = _mm(mv * r * g_ref[...], w_ref[...])

    return pl.pallas_call(body, name="memkv_fwd", out_shape=_sds((nm, 2 * XA_W)), compiler_params=_params())(mem, g, w_kv)


def _memkv_bwd(mem, g, w_kv, dkv):
    nm, d = mem.shape

    def body(m_ref, g_ref, w_ref, dkv_ref, dw_ref, dg_ref):
        mv = m_ref[...]
        r = lax.rsqrt(jnp.mean(mv * mv, axis=-1, keepdims=True) + EPS)
        xhat = mv * r
        dkv_v = dkv_ref[...]
        dw_ref[...] = _mm_tn(xhat * g_ref[...], dkv_v)
        dg_ref[...] = _colsum(_mm_nt(dkv_v, w_ref[...]) * xhat)

    return pl.pallas_call(body, name="memkv_bwd", out_shape=[_sds((d, 2 * XA_W)), _sds((1, d))],
                          compiler_params=_params())(mem, g, w_kv, dkv)


def _xattn_probs(q, mk):
    sc = _mm_nt(q, mk) * (HEAD ** -0.5)
    e = jnp.exp(sc - jnp.max(sc, axis=-1, keepdims=True))
    return e / _rowsum(e)


def _xattn_fwd(proj, mkv):
    s = proj.shape[0]
    nm = mkv.shape[0]
    tm = _tile(s, (512, 256, 128))

    def body(q_ref, z_ref, kv_ref, o_ref):
        for h in range(XA_H):
            cs = slice(h * HEAD, (h + 1) * HEAD)
            p = _xattn_probs(q_ref[:, cs], kv_ref[:, cs])
            ctx = _mm(p, kv_ref[:, XA_W + h * HEAD:XA_W + (h + 1) * HEAD])
            o_ref[:, cs] = (ctx * _silu(z_ref[:, cs])).astype(BF16)

    col = lambda k: pl.BlockSpec((tm, XA_W), lambda i: (i, CQ_BLK + k))
    return pl.pallas_call(
        body, name="xattn_fwd", grid=(s // tm,),
        in_specs=[col(0), col(1), pl.BlockSpec((nm, 2 * XA_W), lambda i: (0, 0))],
        out_specs=pl.BlockSpec((tm, XA_W), lambda i: (i, 0)), out_shape=_sds((s, XA_W), BF16),
        compiler_params=_params(("parallel",)),
    )(proj, proj, mkv)


def _xattn_bwd(proj, dmixed, mkv):
    s = proj.shape[0]
    nm = mkv.shape[0]
    tm = _tile(s, (512, 256, 128))

    def body(q_ref, z_ref, d_ref, kv_ref, dp_ref, dkv_ref):
        @pl.when(pl.program_id(0) == 0)
        def _():
            dkv_ref[...] = jnp.zeros_like(dkv_ref)

        for h in range(XA_H):
            cs = slice(h * HEAD, (h + 1) * HEAD)
            vs = slice(XA_W + h * HEAD, XA_W + (h + 1) * HEAD)
            q = q_ref[:, cs]
            z = z_ref[:, cs]
            mk = kv_ref[:, cs]
            mv = kv_ref[:, vs]
            p = _xattn_probs(q, mk)
            ctx = _mm(p, mv)
            dc = d_ref[:, cs]
            dctx = dc * _silu(z)
            dp_ref[:, vs] = (dc * ctx * _silu_grad(z)).astype(BF16)
            dp = _mm_nt(dctx, mv)
            dkv_ref[:, vs] += _mm_tn(p, dctx)
            ds = p * (dp - _rowsum(dp * p)) * (HEAD ** -0.5)
            dp_ref[:, cs] = _mm(ds, mk).astype(BF16)
            dkv_ref[:, cs] += _mm_tn(ds, q)

    col = lambda k: pl.BlockSpec((tm, XA_W), lambda i: (i, CQ_BLK + k))
    kvs = pl.BlockSpec((nm, 2 * XA_W), lambda i: (0, 0))
    return pl.pallas_call(
        body, name="xattn_bwd", grid=(s // tm,),
        in_specs=[col(0), col(1), pl.BlockSpec((tm, XA_W), lambda i: (i, (DN_W + GMLP_W) // XA_W)), kvs],
        out_specs=[pl.BlockSpec((tm, 2 * XA_W), lambda i: (i, 0)), kvs],
        out_shape=[_sds((s, 2 * XA_W), BF16), _sds((nm, 2 * XA_W))],
        compiler_params=_params(("arbitrary",)),
    )(proj, proj, dmixed, mkv)


def _softplus(x):
    return jnp.maximum(x, 0.0) + jnp.log1p(jnp.exp(-jnp.abs(x)))


def _dn_pre(proj, ab, conv_w, alog_row, dt_row):
    s = proj.shape[0]
    tm = _tile(s, (256, 128))
    w3 = 3 * DN_W

    def body(x_ref, halo_ref, ab_ref, cw_ref, al_ref, dt_ref, q_ref, k_ref, v_ref, gb_ref, gbt_ref, yc_ref, ext):
        i = pl.program_id(0)
        ext[0:HALO, :] = jnp.where(i > 0, halo_ref[...], 0.0)
        ext[HALO:HALO + tm, :] = x_ref[...]
        yc = cw_ref[0:1, :] * ext[pl.ds(HALO - DN_K + 1, tm), :]
        for t in range(1, DN_K):
            yc += cw_ref[t:t + 1, :] * ext[pl.ds(HALO - DN_K + 1 + t, tm), :]
        yc_ref[...] = yc
        act = _silu(yc)
        for h in range(DN_H):
            cs = slice(h * HEAD, (h + 1) * HEAD)
            qa = act[:, cs]
            q_ref[:, cs] = qa * (lax.rsqrt(_rowsum(qa * qa) + EPS) * (HEAD ** -0.5))
            ka = act[:, DN_W + h * HEAD:DN_W + (h + 1) * HEAD]
            k_ref[:, cs] = ka * lax.rsqrt(_rowsum(ka * ka) + EPS)
        v_ref[...] = act[:, 2 * DN_W:w3]
        abv = ab_ref[...]
        lane = _iota2((tm, LANE), 1)
        g = jnp.where(lane < DN_H, -jnp.exp(al_ref[...]) * _softplus(abv + dt_ref[...]), 0.0)
        gc = _mm_hi(_chunk_tri(tm, False), g)
        gbv = jnp.where(lane < DN_H, gc, jnp.where(lane < 2 * DN_H, jax.nn.sigmoid(abv), 0.0))
        gb_ref[...] = gbv
        for c in range(tm // CH):
            gbt_ref[c] = gbv[c * CH:(c + 1) * CH, :].T[0:2 * DN_H, :]

    hb = tm // HALO
    row = lambda w: pl.BlockSpec((tm, w), lambda i: (i, 0))
    vec = pl.BlockSpec((1, LANE), lambda i: (0, 0))
    return pl.pallas_call(
        body, name="dn_pre", grid=(s // tm,),
        in_specs=[row(w3), pl.BlockSpec((HALO, w3), lambda i: (jnp.maximum(i * hb - 1, 0), 0)), row(LANE),
                  pl.BlockSpec((DN_K, w3), lambda i: (0, 0)), vec, vec],
        out_specs=[row(DN_W), row(DN_W), row(DN_W), row(LANE), pl.BlockSpec((tm // CH, 2 * DN_H, CH), lambda i: (i, 0, 0)),
                   row(w3)],
        out_shape=[_sds((s, DN_W)), _sds((s, DN_W)), _sds((s, DN_W)), _sds((s, LANE)), _sds((s // CH, 2 * DN_H, CH)),
                   _sds((s, w3))],
        scratch_shapes=[pltpu.VMEM((tm + HALO, w3), F32)],
        compiler_params=_params(("parallel",)),
    )(proj, proj, ab, conv_w, alog_row, dt_row)


HEADS = tuple(range(DN_H))


def _hcols(h):
    return slice(h * HEAD, (h + 1) * HEAD)


def _chunk_scalings(k, v, gbv, gbt, h):
    gc = gbv[:, h:h + 1]
    beta = gbv[:, DN_H + h:DN_H + h + 1]
    gr = gbt[h:h + 1, :]
    ii = _iota2((CH, CH), 0)
    jj = _iota2((CH, CH), 1)
    dec = jnp.exp(jnp.where(ii >= jj, gc - gr, -1e30))
    eg = jnp.exp(gc)
    gl = gr[:, CH - 1:CH]
    kb = k * beta
    return dict(beta=beta, dec=dec, eg=eg, gl=gl, ekd=jnp.exp(gl - gc), kb=kb, vb=v * beta, kbe=kb * eg)


def _chunk_scores(m, q, k):
    kq = _mm_nt(jnp.concatenate([m["kb"], q], axis=0), k)
    strict = _iota2((CH, CH), 0) > _iota2((CH, CH), 1)
    return jnp.where(strict, kq[0:CH] * m["dec"], 0.0), kq[CH:2 * CH] * m["dec"]


def _dn_local(q, k, v, gb, gbt):
    s = q.shape[0]
    cpb = 4 if (s // CH) % 4 == 0 else 1
    tb = cpb * CH
    nblk = s // tb

    def body(q_ref, k_ref, v_ref, gb_ref, gbt_ref, u_ref, w_ref, qg_ref, kd_ref, t_ref, ai_ref, egl_ref):
        def chunk(c, carry):
            r0 = pl.multiple_of(c * CH, CH)
            rows = pl.ds(r0, CH)
            gbv = gb_ref[rows, :]
            gbt_v = gbt_ref[c]
            qs = [q_ref[rows, _hcols(h)] for h in HEADS]
            ks = [k_ref[rows, _hcols(h)] for h in HEADS]
            ms = [_chunk_scalings(ks[h], v_ref[rows, _hcols(h)], gbv, gbt_v, h) for h in HEADS]
            for h in HEADS:
                qg_ref[rows, _hcols(h)] = (qs[h] * ms[h]["eg"]).astype(BF16)
                kd_ref[rows, _hcols(h)] = (ks[h] * ms[h]["ekd"]).astype(BF16)
                egl_ref[c, h:h + 1, :] = jnp.broadcast_to(jnp.exp(ms[h]["gl"]), (1, LANE))
            sc = [_chunk_scores(ms[h], qs[h], ks[h]) for h in HEADS]
            for h in HEADS:
                ai_ref[h, rows, :] = sc[h][1]
            eye = jnp.where(_iota2((CH, CH), 0) == _iota2((CH, CH), 1), 1.0, 0.0).astype(F32)
            ts = [eye - sc[h][0] for h in HEADS]
            ps = [_mm_3x(sc[h][0], sc[h][0]) for h in HEADS]
            ts = [ts[h] + _mm_3x(ts[h], ps[h]) for h in HEADS]
            for _ in range(4):
                ps = [_mm(ps[h], ps[h]) for h in HEADS]
                ts = [ts[h] + _mm(ts[h], ps[h]) for h in HEADS]
            for h in HEADS:
                t_ref[h, rows, :] = ts[h]
                uw = _mm(ts[h], jnp.concatenate([ms[h]["vb"], ms[h]["kbe"]], axis=1))
                u_ref[rows, _hcols(h)] = uw[:, 0:HEAD]
                w_ref[rows, _hcols(h)] = uw[:, HEAD:2 * HEAD].astype(BF16)
            return carry

        lax.fori_loop(0, cpb, chunk, 0)

    row = pl.BlockSpec((tb, DN_W), lambda i: (i, 0))
    sq = pl.BlockSpec((DN_H, tb, CH), lambda i: (0, i, 0))
    return pl.pallas_call(
        body, name="dn_local", grid=(nblk,),
        in_specs=[row, row, row, pl.BlockSpec((tb, LANE), lambda i: (i, 0)),
                  pl.BlockSpec((cpb, 2 * DN_H, CH), lambda i: (i, 0, 0))],
        out_specs=[row, row, row, row, sq, sq, pl.BlockSpec((cpb, DN_H, LANE), lambda i: (i, 0, 0))],
        out_shape=[_sds((s, DN_W)), _sds((s, DN_W), BF16), _sds((s, DN_W), BF16), _sds((s, DN_W), BF16),
                   _sds((DN_H, s, CH)), _sds((DN_H, s, CH)), _sds((s // CH, DN_H, LANE))],
        compiler_params=_params(("parallel",)),
    )(q, k, v, gb, gbt)


def _scan_cpb(s):
    return 8 if (s // CH) % 8 == 0 else 1


def _dn_scan(u, w, qg, kd, ai, egl, proj, norm_g):
    s = u.shape[0]
    cpb = _scan_cpb(s)
    tb = cpb * CH
    nblk = s // tb

    def body(u_ref, w_ref, qg_ref, kd_ref, ai_ref, egl_ref, z_ref, ng_ref, o_ref, vn_ref, st_ref, ob_ref, state):
        @pl.when(pl.program_id(0) == 0)
        def _():
            state[...] = jnp.zeros_like(state)

        ng = ng_ref[...]

        def chunk(c, carry):
            r0 = pl.multiple_of(c * CH, CH)
            rows = pl.ds(r0, CH)
            sts = [state[h] for h in HEADS]
            stb = [sts[h].astype(BF16) for h in HEADS]
            for h in HEADS:
                st_ref[c, h] = sts[h]
            vns = [u_ref[rows, _hcols(h)] - jnp.dot(w_ref[rows, _hcols(h)], stb[h], preferred_element_type=F32)
                   for h in HEADS]
            vnb = [vns[h].astype(BF16) for h in HEADS]
            for h in HEADS:
                state[h] = sts[h] * egl_ref[c, h:h + 1, :] + _mm_tn(kd_ref[rows, _hcols(h)], vnb[h])
            os_ = [jnp.dot(qg_ref[rows, _hcols(h)], stb[h], preferred_element_type=F32) + _mm(ai_ref[h, rows, :], vnb[h])
                   for h in HEADS]
            for h in HEADS:
                o = os_[h]
                vn_ref[rows, _hcols(h)] = vns[h]
                o_ref[rows, _hcols(h)] = o
                r = lax.rsqrt(jnp.mean(o * o, axis=-1, keepdims=True) + EPS)
                ob_ref[rows, _hcols(h)] = (o * r * ng * _silu(z_ref[rows, _hcols(h)])).astype(BF16)
            return carry

        lax.fori_loop(0, cpb, chunk, 0)

    row = pl.BlockSpec((tb, DN_W), lambda i: (i, 0))
    return pl.pallas_call(
        body, name="dn_scan", grid=(nblk,),
        in_specs=[row, row, row, row, pl.BlockSpec((DN_H, tb, CH), lambda i: (0, i, 0)),
                  pl.BlockSpec((cpb, DN_H, LANE), lambda i: (i, 0, 0)), pl.BlockSpec((tb, DN_W), lambda i: (i, 3)),
                  pl.BlockSpec((1, HEAD), lambda i: (0, 0))],
        out_specs=[row, row, pl.BlockSpec((cpb, DN_H, HEAD, HEAD), lambda i: (i, 0, 0, 0)), row],
        out_shape=[_sds((s, DN_W)), _sds((s, DN_W)), _sds((s // CH, DN_H, HEAD, HEAD)), _sds((s, DN_W), BF16)],
        scratch_shapes=[pltpu.VMEM((DN_H, HEAD, HEAD), F32)],
        compiler_params=_params(("arbitrary",)),
    )(u, w, qg, kd, ai, egl, proj, norm_g)


def _dn_scan_bwd(dmixed, o, proj, norm_g, w, qg, kd, ai, egl):
    s = o.shape[0]
    cpb = _scan_cpb(s)
    tb = cpb * CH
    nblk = s // tb

    def body(dm_ref, o_ref, z_ref, ng_ref, w_ref, qg_ref, kd_ref, ai_ref, egl_ref,
             do_ref, dvn_ref, dst_ref, dz_ref, dng_ref, dstate):
        @pl.when(pl.program_id(0) == 0)
        def _():
            dstate[...] = jnp.zeros_like(dstate)
            dng_ref[...] = jnp.zeros_like(dng_ref)

        ng = ng_ref[...]

        def chunk(cc, carry):
            c = cpb - 1 - cc
            r0 = pl.multiple_of(c * CH, CH)
            rows = pl.ds(r0, CH)
            dng = jnp.zeros((1, HEAD), F32)
            dob = []
            for h in HEADS:
                cs = _hcols(h)
                o = o_ref[rows, cs]
                z = z_ref[rows, cs]
                db = dm_ref[rows, cs]
                r = lax.rsqrt(jnp.mean(o * o, axis=-1, keepdims=True) + EPS)
                ohat = o * r
                dz_ref[rows, cs] = (db * ohat * ng * _silu_grad(z)).astype(BF16)
                dyn = db * _silu(z)
                dng += _colsum(dyn * ohat)
                doh = dyn * ng
                do = r * (doh - ohat * jnp.mean(doh * ohat, axis=-1, keepdims=True))
                do_ref[rows, cs] = do
                dob.append(do.astype(BF16))
            dng_ref[...] += dng
            dsn = [dstate[h] for h in HEADS]
            for h in HEADS:
                dst_ref[c, h] = dsn[h]
            dvn = [_mm_tn(ai_ref[h, rows, :], dob[h])
                   + jnp.dot(kd_ref[rows, _hcols(h)], dsn[h].astype(BF16), preferred_element_type=F32) for h in HEADS]
            part = [_mm_tn(qg_ref[rows, _hcols(h)], dob[h]) + egl_ref[c, h:h + 1, :] * dsn[h] for h in HEADS]
            for h in HEADS:
                dvn_ref[rows, _hcols(h)] = dvn[h]
                dstate[h] = part[h] - _mm_tn(w_ref[rows, _hcols(h)], dvn[h])
            return carry

        lax.fori_loop(0, cpb, chunk, 0)

    rev = lambda i: (nblk - 1 - i, 0)
    row = pl.BlockSpec((tb, DN_W), rev)
    vec = pl.BlockSpec((1, HEAD), lambda i: (0, 0))
    return pl.pallas_call(
        body, name="dn_scan_bwd", grid=(nblk,),
        in_specs=[row, row, pl.BlockSpec((tb, DN_W), lambda i: (nblk - 1 - i, 3)), vec, row, row, row,
                  pl.BlockSpec((DN_H, tb, CH), lambda i: (0, nblk - 1 - i, 0)),
                  pl.BlockSpec((cpb, DN_H, LANE), lambda i: (nblk - 1 - i, 0, 0))],
        out_specs=[row, row, pl.BlockSpec((cpb, DN_H, HEAD, HEAD), lambda i: (nblk - 1 - i, 0, 0, 0)), row, vec],
        out_shape=[_sds((s, DN_W)), _sds((s, DN_W)), _sds((s // CH, DN_H, HEAD, HEAD)), _sds((s, DN_W), BF16),
                   _sds((1, HEAD))],
        scratch_shapes=[pltpu.VMEM((DN_H, HEAD, HEAD), F32)],
        compiler_params=_params(("arbitrary",)),
    )(dmixed, o, proj, norm_g, w, qg, kd, ai, egl)


def _dn_local_bwd(q, k, v, gb, gbt, t, vn, st, dst, do, dvn):
    s = q.shape[0]
    cpb = 4 if (s // CH) % 4 == 0 else 1
    tb = cpb * CH
    nblk = s // tb

    def body(q_ref, k_ref, v_ref, gb_ref, gbt_ref, t_ref, vn_ref, st_ref, dst_ref, do_ref, dvn_ref,
             dq_ref, dk_ref, dv_ref, dgb_ref):
        lane = _iota2((CH, LANE), 1)
        last = _iota2((CH, 1), 0) == CH - 1

        def chunk(c, carry):
            r0 = pl.multiple_of(c * CH, CH)
            rows = pl.ds(r0, CH)
            gbv = gb_ref[rows, :]
            gbt_v = gbt_ref[c]
            strict = _iota2((CH, CH), 0) > _iota2((CH, CH), 1)
            qs = [q_ref[rows, _hcols(h)] for h in HEADS]
            ks = [k_ref[rows, _hcols(h)] for h in HEADS]
            vs = [v_ref[rows, _hcols(h)] for h in HEADS]
            ms = [_chunk_scalings(ks[h], vs[h], gbv, gbt_v, h) for h in HEADS]
            sts = [st_ref[c, h] for h in HEADS]
            dsn = [dst_ref[c, h] for h in HEADS]
            dob = [do_ref[rows, _hcols(h)].astype(BF16) for h in HEADS]
            dvnb = [dvn_ref[rows, _hcols(h)].astype(BF16) for h in HEADS]
            vnb = [vn_ref[rows, _hcols(h)].astype(BF16) for h in HEADS]
            tbf = [t_ref[h, rows, :].astype(BF16) for h in HEADS]
            sc = [_chunk_scores(ms[h], qs[h], ks[h]) for h in HEADS]
            xs_ = [_mm_nt(jnp.concatenate([dob[h], dvnb[h]], axis=0), sts[h]) for h in HEADS]
            dai = [_mm_nt(dob[h], vnb[h]) for h in HEADS]
            dkd = [_mm_nt(vnb[h], dsn[h]) for h in HEADS]
            dqg = [xs_[h][0:CH] for h in HEADS]
            duw = [jnp.concatenate([dvnb[h], (-xs_[h][CH:2 * CH]).astype(BF16)], axis=1) for h in HEADS]
            dt = [_mm_nt(duw[h], jnp.concatenate([ms[h]["vb"], ms[h]["kbe"]], axis=1)) for h in HEADS]
            dvk = [_mm_tn(tbf[h], duw[h]) for h in HEADS]
            tdt = [_mm_tn(tbf[h], dt[h]) for h in HEADS]
            da = [jnp.where(strict, -_mm_nt(tdt[h], tbf[h]), 0.0) for h in HEADS]
            dsc = [jnp.concatenate([da[h] * ms[h]["dec"], dai[h] * ms[h]["dec"]], axis=0) for h in HEADS]
            dkq = [_mm(dsc[h], ks[h]) for h in HEADS]
            dk1 = [_mm_tn(dsc[h], jnp.concatenate([ms[h]["kb"], qs[h]], axis=0)) for h in HEADS]
            dgb = jnp.zeros((CH, LANE), F32)
            for h in HEADS:
                m = ms[h]
                eg, ekd, beta = m["eg"], m["ekd"], m["beta"]
                dvb = dvk[h][:, 0:HEAD]
                dkbe = dvk[h][:, HEAD:2 * HEAD]
                kd = ks[h] * ekd
                dkb = dkq[h][0:CH] + dkbe * eg
                dq_ref[rows, _hcols(h)] = dkq[h][CH:2 * CH] + dqg[h] * eg
                dk_ref[rows, _hcols(h)] = dk1[h] + dkd[h] * ekd + dkb * beta
                dv_ref[rows, _hcols(h)] = dvb * beta
                dgl = jnp.exp(m["gl"]) * _colsum(_rowsum(sts[h] * dsn[h])) + _colsum(_rowsum(dkd[h] * kd))
                mm_ = da[h] * sc[h][0] + dai[h] * sc[h][1]
                dgc = (_rowsum(mm_) - _rowsum(mm_.T) + _rowsum(dqg[h] * qs[h] * eg) - _rowsum(dkd[h] * kd)
                       + _rowsum(dkbe * m["kbe"]) + jnp.where(last, dgl, 0.0))
                dbeta = _rowsum(dkb * ks[h]) + _rowsum(dvb * vs[h])
                dgb = jnp.where(lane == h, dgc, jnp.where(lane == DN_H + h, dbeta, dgb))
            dgb_ref[rows, :] = dgb
            return carry

        lax.fori_loop(0, cpb, chunk, 0)

    row = pl.BlockSpec((tb, DN_W), lambda i: (i, 0))
    gbs = pl.BlockSpec((tb, LANE), lambda i: (i, 0))
    sts = pl.BlockSpec((cpb, DN_H, HEAD, HEAD), lambda i: (i, 0, 0, 0))
    return pl.pallas_call(
        body, name="dn_local_bwd", grid=(nblk,),
        in_specs=[row, row, row, gbs, pl.BlockSpec((cpb, 2 * DN_H, CH), lambda i: (i, 0, 0)),
                  pl.BlockSpec((DN_H, tb, CH), lambda i: (0, i, 0)), row, sts, sts, row, row],
        out_specs=[row, row, row, gbs],
        out_shape=[_sds((s, DN_W)), _sds((s, DN_W)), _sds((s, DN_W)), _sds((s, LANE))],
        compiler_params=_params(("parallel",)),
    )(q, k, v, gb, gbt, t, vn, st, dst, do, dvn)


def _dn_pre_bwd(proj, yc_all, ab, conv_w, alog_row, dt_row, dq, dk, dv, dgb):
    s = proj.shape[0]
    tm = _tile(s, (256, 128))
    w3 = 3 * DN_W
    nblk = s // tm

    def body(x_ref, yc_ref, ab_ref, cw_ref, al_ref, dt_ref, dq_ref, dk_ref, dv_ref, dgb_ref,
             dx_ref, dab_ref, dcw_ref, dal_ref, ddt_ref, exd, carry):
        i = pl.program_id(0)

        @pl.when(i == 0)
        def _():
            carry[...] = jnp.zeros_like(carry)
            dcw_ref[...] = jnp.zeros_like(dcw_ref)
            dal_ref[...] = jnp.zeros_like(dal_ref)
            ddt_ref[...] = jnp.zeros_like(ddt_ref)

        yc = yc_ref[...]
        sg = jax.nn.sigmoid(yc)
        act = yc * sg
        dact = sg * (1.0 + yc * (1.0 - sg))
        for h in range(DN_H):
            cs = slice(h * HEAD, (h + 1) * HEAD)
            ks = slice(DN_W + h * HEAD, DN_W + (h + 1) * HEAD)
            qa = act[:, cs]
            rq = lax.rsqrt(_rowsum(qa * qa) + EPS)
            qh = qa * rq
            dqv = dq_ref[:, cs]
            exd[0:tm, cs] = (HEAD ** -0.5) * rq * (dqv - qh * _rowsum(dqv * qh)) * dact[:, cs]
            ka = act[:, ks]
            rk = lax.rsqrt(_rowsum(ka * ka) + EPS)
            kh = ka * rk
            dkv = dk_ref[:, cs]
            exd[0:tm, ks] = rk * (dkv - kh * _rowsum(dkv * kh)) * dact[:, ks]
        exd[0:tm, 2 * DN_W:w3] = dv_ref[...] * dact[:, 2 * DN_W:w3]
        exd[tm:tm + HALO, :] = carry[...]
        xv = x_ref[...]
        dx = None
        for t in range(DN_K):
            view = exd[pl.ds(DN_K - 1 - t, tm), :]
            dcw_ref[t:t + 1, :] += _colsum(view * xv)
            term = cw_ref[t:t + 1, :] * view
            dx = term if dx is None else dx + term
        dx_ref[...] = dx.astype(BF16)
        carry[...] = exd[0:HALO, :]

        lane = _iota2((tm, LANE), 1)
        dgbv = dgb_ref[...]
        dg = _mm_hi(_chunk_tri(tm, True), jnp.where(lane < DN_H, dgbv, 0.0))
        abv = ab_ref[...]
        xa = abv + dt_ref[...]
        nea = -jnp.exp(al_ref[...])
        d_da = jnp.where(lane < DN_H, dg * nea * jax.nn.sigmoid(xa), 0.0)
        dal_ref[...] += _colsum(jnp.where(lane < DN_H, dg * nea * _softplus(xa), 0.0))
        ddt_ref[...] += _colsum(d_da)
        beta = jax.nn.sigmoid(abv)
        d_db = jnp.where((lane >= DN_H) & (lane < 2 * DN_H), dgbv * beta * (1.0 - beta), 0.0)
        dab_ref[...] = (d_da + d_db).astype(BF16)

    rev = lambda i: (nblk - 1 - i, 0)
    row = lambda w: pl.BlockSpec((tm, w), rev)
    vec = pl.BlockSpec((1, LANE), lambda i: (0, 0))
    cws = pl.BlockSpec((DN_K, w3), lambda i: (0, 0))
    return pl.pallas_call(
        body, name="dn_pre_bwd", grid=(nblk,),
        in_specs=[row(w3), row(w3), row(LANE), cws, vec, vec, row(DN_W), row(DN_W), row(DN_W), row(LANE)],
        out_specs=[row(w3), row(LANE), cws, vec, vec],
        out_shape=[_sds((s, w3), BF16), _sds((s, LANE), BF16), _sds((DN_K, w3)), _sds((1, LANE)), _sds((1, LANE))],
        scratch_shapes=[pltpu.VMEM((tm + HALO, w3), F32), pltpu.VMEM((HALO, w3), F32)],
        compiler_params=_params(("arbitrary",)),
    )(proj, yc_all, ab, conv_w, alog_row, dt_row, dq, dk, dv, dgb)


def _adam(parts, w, m, v, name):
    r, c = w.shape
    n_parts = parts.shape[0]
    tr = _tile(r, (128, 64, 32, 16, 8))

    def body(p_ref, w_ref, m_ref, v_ref, g_ref, d_ref, nm_ref, nv_ref):
        g = p_ref[0].astype(F32)
        for k in range(1, n_parts):
            g = g + p_ref[k].astype(F32)
        g_ref[...] = g
        mn = ADAM_B1 * m_ref[...] + (1.0 - ADAM_B1) * g
        vn = ADAM_B2 * v_ref[...] + (1.0 - ADAM_B2) * (g * g)
        m_hat = mn / (1.0 - ADAM_B1 ** ADAM_STEP)
        v_hat = vn / (1.0 - ADAM_B2 ** ADAM_STEP)
        d_ref[...] = -ADAM_LR * (m_hat / (jnp.sqrt(v_hat) + ADAM_EPS) + ADAM_WD * w_ref[...])
        nm_ref[...] = mn
        nv_ref[...] = vn

    blk = pl.BlockSpec((tr, c), lambda i: (i, 0))
    return pl.pallas_call(
        body, name=name, grid=(r // tr,),
        in_specs=[pl.BlockSpec((n_parts, tr, c), lambda i: (0, i, 0)), blk, blk, blk],
        out_specs=[blk, blk, blk, blk], out_shape=[_sds((r, c))] * 4,
        compiler_params=_params(("parallel",)),
    )(parts, w, m, v)


_PACK_ROWS = 8


def _pack(vals):
    tiles = []
    for a in vals:
        flat = a.reshape(-1).astype(F32)
        unit = _PACK_ROWS * LANE
        n = -(-flat.shape[0] // unit) * unit
        tiles.append(jnp.pad(flat, (0, n - flat.shape[0])).reshape(n // LANE, LANE))
    return jnp.concatenate(tiles, axis=0)


def _unpack(packed, shapes):
    out = []
    r0 = 0
    for shp in shapes:
        size = 1
        for dim in shp:
            size *= dim
        unit = _PACK_ROWS * LANE
        rows = -(-size // unit) * _PACK_ROWS
        out.append(packed[r0:r0 + rows].reshape(-1)[:size].reshape(shp))
        r0 += rows
    return out


def _lane_row(vec8):
    return jnp.pad(vec8.reshape(1, -1).astype(F32), ((0, 0), (0, LANE - vec8.size)))


def kernel(x, mem, ln_g, w_in, gmlp_ln_g, gmlp_ln_b, gmlp_ws, gmlp_bs, conv_w, dn_a_log, dn_dt_bias, dn_norm_g, mem_norm_g, w_mem_kv, w_out, final_g, loss_target, m_ln_g, m_w_in, m_gmlp_ln_g, m_gmlp_ln_b, m_gmlp_ws, m_gmlp_bs, m_conv_w, m_dn_a_log, m_dn_dt_bias, m_dn_norm_g, m_mem_norm_g, m_w_mem_kv, m_w_out, m_final_g, v_ln_g, v_w_in, v_gmlp_ln_g, v_gmlp_ln_b, v_gmlp_ws, v_gmlp_bs, v_conv_w, v_dn_a_log, v_dn_dt_bias, v_dn_norm_g, v_mem_norm_g, v_w_mem_kv, v_w_out, v_final_g):
    xs = x[0]
    mems = mem[0]
    tgt = loss_target[0]
    s, d = xs.shape
    shard_w = w_in.shape[2]
    in_w = N_DEV * shard_w
    me = 4 * lax.axis_index("x") + 2 * lax.axis_index("y") + lax.axis_index("c")

    (g_in,) = _gather_two_level([w_in[0].astype(BF16)], "gather_w_in")
    o_g, o_dn, o_ab = 0, 3 * GMLP_W, 3 * GMLP_W + 4 * DN_W
    o_xa = o_ab + 2 * DN_H

    def shard_cols(lo, hi):
        out = []
        while lo < hi:
            sh = lo // shard_w
            end = min(hi, (sh + 1) * shard_w)
            out.append(g_in[sh][:, lo - sh * shard_w:end - sh * shard_w])
            lo = end
        return out

    w_main = jnp.concatenate(shard_cols(o_dn, o_ab) + shard_cols(o_g, o_dn) + shard_cols(o_xa, in_w), axis=1)
    w_ab = jnp.pad(jnp.concatenate(shard_cols(o_ab, o_xa), axis=1), ((0, 0), (0, LANE - 2 * DN_H)))

    ln_g2 = ln_g.reshape(1, d)
    lng2 = gmlp_ln_g.reshape(1, GMLP_W)
    lnb2 = gmlp_ln_b.reshape(1, GMLP_W)
    ws3 = gmlp_ws[0]
    bs_t = gmlp_bs[0].T
    alog_row = _lane_row(dn_a_log)
    dt_row = _lane_row(dn_dt_bias)
    dn_g2 = dn_norm_g.reshape(1, HEAD)
    mem_g2 = mem_norm_g.reshape(1, d)
    fin_g2 = final_g.reshape(1, d)

    proj, ab, h_t, (g_out, g_kv, g_conv) = _inproj(
        xs, ln_g2, w_main, w_ab, [w_out[0].astype(BF16), w_mem_kv[0].astype(BF16), conv_w[0]])
    wo = g_out.reshape(MIX_W, d)
    wo_perm = jnp.concatenate([wo[GMLP_W:GMLP_W + DN_W], wo[0:GMLP_W], wo[GMLP_W + DN_W:MIX_W]], axis=0)
    w_kv = g_kv.reshape(d, 2 * XA_W)
    conv_full = g_conv.transpose(1, 0, 2).reshape(DN_K, 3 * DN_W)
    out_a = _gmlp_fwd(proj, lng2, lnb2, ws3, bs_t)
    mkv = _memkv_fwd(mems, mem_g2, w_kv)
    out_c = _xattn_fwd(proj, mkv)
    q, k, v, gb, gbt, yc = _dn_pre(proj, ab, conv_full, alog_row, dt_row)
    u, wk, qg, kd, tmat, ai, egl = _dn_local(q, k, v, gb, gbt)
    o, vn, st, out_b = _dn_scan(u, wk, qg, kd, ai, egl, proj, dn_g2)

    dx2, dx2b, dmixed, loss_acc, d_fin_g = _final(xs, tgt, out_b, out_a, out_c, wo_perm, fin_g2)
    loss = lax.psum(loss_acc[0, 0], ("x", "y", "c"))

    dwo_b = _matmul_tn(out_b, dx2b, "dw_out_b")
    dwo_a = _matmul_tn(out_a, dx2b, "dw_out_a")
    dwo_c = _matmul_tn(out_c, dx2b, "dw_out_c")
    d_w_out = jnp.concatenate([dwo_a, dwo_b, dwo_c], axis=0)

    dp_g, d_ws, d_bst, d_lng, d_lnb = _gmlp_bwd(proj, dmixed, lng2, lnb2, ws3, bs_t)
    dp_x, dmkv = _xattn_bwd(proj, dmixed, mkv)
    d_w_kv, d_mem_g = _memkv_bwd(mems, mem_g2, w_kv, dmkv)
    do, dvn, dst, dp_dz, d_dn_g = _dn_scan_bwd(dmixed, o, proj, dn_g2, wk, qg, kd, ai, egl)
    dq, dk, dv, dgb = _dn_local_bwd(q, k, v, gb, gbt, tmat, vn, st, dst, do, dvn)
    dp_qkv, dp_ab, d_conv, d_alog, d_dt = _dn_pre_bwd(proj, yc, ab, conv_full, alog_row, dt_row, dq, dk, dv, dgb)

    terms = [(dp_qkv, w_main, 3 * DN_W, 0, 0), (dp_dz, w_main, DN_W, 0, 3)]
    terms += [(dp_g, w_main, GMLP_W, b, GU_BLK + b) for b in range(3)]
    terms += [(dp_x, w_main, XA_W, b, CQ_BLK + b) for b in range(2)]
    terms += [(dp_ab, w_ab, LANE, 0, 0)]
    dw_qkv = _matmul_acc(h_t, dp_qkv, "dw_in_qkv")
    dw_dz = _matmul_acc(h_t, dp_dz, "dw_in_dz")
    dw_gm = _matmul_acc(h_t, dp_g, "dw_in_gmlp")
    dw_xa = _matmul_acc(h_t, dp_x, "dw_in_xa")
    dw_ab = _matmul_acc(h_t, dp_ab, "dw_in_ab")
    segs = [(o_g, dw_gm), (o_dn, dw_qkv), (o_dn + 3 * DN_W, dw_dz), (o_ab, dw_ab[:, :2 * DN_H]), (o_xa, dw_xa)]
    shards = []
    for sh in range(N_DEV):
        lo, hi = sh * shard_w, (sh + 1) * shard_w
        parts = [arr[:, max(lo, off) - off:min(hi, off + arr.shape[1]) - off] for off, arr in segs
                 if off < hi and off + arr.shape[1] > lo]
        shards.append(jnp.concatenate(parts, axis=1).astype(BF16))
    send_in = jnp.stack(shards)

    small_shapes = [gmlp_ln_g.shape, gmlp_ln_b.shape, gmlp_ws.shape, gmlp_bs.shape, dn_a_log.shape,
                    dn_dt_bias.shape, dn_norm_g.shape, mem_norm_g.shape, final_g.shape, (DN_K, 3 * DN_W)]
    small_g = _pack([d_lng, d_lnb, d_ws, d_bst.T, d_alog[:, :DN_H], d_dt[:, :DN_H], d_dn_g, d_mem_g, d_fin_g, d_conv])
    zc = jnp.zeros((DN_K, 3 * DN_W), F32)
    small_w = _pack([gmlp_ln_g, gmlp_ln_b, gmlp_ws, gmlp_bs, dn_a_log, dn_dt_bias, dn_norm_g, mem_norm_g, final_g, zc])
    small_m = _pack([m_gmlp_ln_g, m_gmlp_ln_b, m_gmlp_ws, m_gmlp_bs, m_dn_a_log, m_dn_dt_bias, m_dn_norm_g,
                     m_mem_norm_g, m_final_g, zc])
    small_v = _pack([v_gmlp_ln_g, v_gmlp_ln_b, v_gmlp_ws, v_gmlp_bs, v_dn_a_log, v_dn_dt_bias, v_dn_norm_g,
                     v_mem_norm_g, v_final_g, zc + 1.0])

    send_out = d_w_out.reshape(N_DEV, MIX_W // N_DEV, d).astype(BF16)
    send_kv = d_w_kv.reshape(N_DEV, d // N_DEV, 2 * XA_W).astype(BF16)
    sends = [send_in, send_out, send_kv]
    all_small, got = _swap_halves(small_g, sends, "swap_halves")
    core = lax.axis_index("c").astype(jnp.int32).reshape(1)
    chip_sums = [_pair_sum(core, sends[i], got[i], "pair_sum_%d" % i) for i in range(3)]
    dh, (r_in, r_out, r_kv) = _dh(terms, chip_sums)
    grad_x, d_ln_g = _rms_bwd(xs, dh, dx2, ln_g2)
    (all_ln_g,) = _gather_two_level([_pack([d_ln_g])], "gather_ln_g")

    g_w_in, dl_w_in, nm_w_in, nv_w_in = _adam(r_in, w_in[0], m_w_in[0], v_w_in[0], "adam_w_in")
    g_w_out, dl_w_out, nm_w_out, nv_w_out = _adam(r_out, w_out[0], m_w_out[0], v_w_out[0], "adam_w_out")
    g_w_kv, dl_w_kv, nm_w_kv, nv_w_kv = _adam(r_kv, w_mem_kv[0], m_w_mem_kv[0], v_w_mem_kv[0], "adam_w_kv")
    sm = [_unpack(t, small_shapes) for t in _adam(all_small, small_w, small_m, small_v, "adam_small")]
    ln_res = [_unpack(t, [ln_g.shape])[0]
              for t in _adam(all_ln_g, _pack([ln_g]), _pack([m_ln_g]), _pack([v_ln_g]), "adam_ln_g")]

    conv_parts = lax.dynamic_slice(all_small, (0, all_small.shape[1] - (DN_K * 3 * DN_W) // LANE, 0),
                                   (N_DEV, (DN_K * 3 * DN_W) // LANE, LANE)).reshape(N_DEV, DN_K, 3 * DN_W)
    cshard = conv_w.shape[2]
    conv_parts = lax.dynamic_slice(conv_parts, (0, 0, me * cshard), (N_DEV, DN_K, cshard))
    cpad = ((0, 0), (0, HALO - DN_K), (0, 0))
    conv_res = _adam(jnp.pad(conv_parts, cpad), jnp.pad(conv_w[0], cpad[1:]), jnp.pad(m_conv_w[0], cpad[1:]),
                     jnp.pad(v_conv_w[0], cpad[1:], constant_values=1.0), "adam_conv")
    g_conv_s, dl_conv, nm_conv, nv_conv = [t[:DN_K][None] for t in conv_res]

    def group(idx, big_in, big_conv, big_kv, big_out):
        names = sm[idx]
        return [ln_res[idx], big_in[None], names[0], names[1], names[2], names[3], big_conv, names[4], names[5], names[6],
                names[7], big_kv[None], big_out[None], names[8]]

    grads = group(0, g_w_in, g_conv_s, g_w_kv, g_w_out)
    deltas = group(1, dl_w_in, dl_conv, dl_w_kv, dl_w_out)
    new_m = group(2, nm_w_in, nm_conv, nm_w_kv, nm_w_out)
    new_v = group(3, nv_w_in, nv_conv, nv_w_kv, nv_w_out)
    return (loss, grad_x[None], *grads, *deltas, *new_m, *new_v)
```

```python
import functools

import jax
import jax.numpy as jnp
from jax import lax
from jax.experimental import pallas as pl
from jax.experimental.pallas import tpu as pltpu

F32 = jnp.float32
BF16 = jnp.bfloat16
HIGHEST = lax.Precision.HIGHEST
MESH_ID = pl.DeviceIdType.MESH

N_DEV = 8
EPS = 1e-6
GMLP_W = 512
GMLP_G = 4
GMLP_T = 128
DN_W = 1024
DN_H = 8
HEAD = 128
DN_K = 4
CH = 64
XA_W = 512
XA_H = 4
LANE = 128
HALO = 8
MAIN_W = 4 * DN_W + 3 * GMLP_W + 2 * XA_W
MIX_W = DN_W + GMLP_W + XA_W
VMEM_LIMIT = 56 * 1024 * 1024

ADAM_LR = 0.001
ADAM_B1 = 0.9
ADAM_B2 = 0.999
ADAM_EPS = 1e-08
ADAM_WD = 0.01
ADAM_STEP = 10


def _sds(shape, dtype=F32):
    return jax.ShapeDtypeStruct(tuple(shape), dtype)


def _params(sem=None):
    if sem is None:
        return pltpu.CompilerParams(vmem_limit_bytes=VMEM_LIMIT)
    return pltpu.CompilerParams(dimension_semantics=tuple(sem), vmem_limit_bytes=VMEM_LIMIT)


def _tile(n, prefs):
    for p in prefs:
        if n % p == 0:
            return p
    return n


def _mm(a, b):
    return jnp.dot(a.astype(BF16), b.astype(BF16), preferred_element_type=F32)


def _mm_nt(a, b):
    return lax.dot_general(a.astype(BF16), b.astype(BF16), (((1,), (1,)), ((), ())), preferred_element_type=F32)


def _mm_tn(a, b):
    return lax.dot_general(a.astype(BF16), b.astype(BF16), (((0,), (0,)), ((), ())), preferred_element_type=F32)


def _mm_hi(a, b):
    return jnp.dot(a, b, precision=HIGHEST, preferred_element_type=F32)


def _mm_3x(a, b):
    return jnp.dot(a, b, precision=lax.Precision.HIGH, preferred_element_type=F32)


_GELU_C = 0.7978845608028654
_GELU_A = 0.044715


def _gelu(x):
    return 0.5 * x * (1.0 + jnp.tanh(_GELU_C * (x + _GELU_A * x * x * x)))


def _gelu_grad(x):
    t = jnp.tanh(_GELU_C * (x + _GELU_A * x * x * x))
    return 0.5 * (1.0 + t) + 0.5 * x * (1.0 - t * t) * _GELU_C * (1.0 + 3.0 * _GELU_A * x * x)


def _silu(x):
    return x * jax.nn.sigmoid(x)


def _silu_grad(x):
    s = jax.nn.sigmoid(x)
    return s * (1.0 + x * (1.0 - s))


def _rowsum(x):
    return jnp.sum(x, axis=-1, keepdims=True)


def _colsum(x):
    return jnp.sum(x, axis=0, keepdims=True)


def _iota2(shape, dim):
    return lax.broadcasted_iota(jnp.int32, shape, dim)


def _chunk_tri(tm, upper):
    r = _iota2((tm, tm), 0)
    c = _iota2((tm, tm), 1)
    same = lax.shift_right_logical(r, 6) == lax.shift_right_logical(c, 6)
    tri = (r <= c) if upper else (r >= c)
    return jnp.where(same & tri, 1.0, 0.0).astype(F32)


N_CHIP = 4


def _mesh_place():
    x, y, c = lax.axis_index("x"), lax.axis_index("y"), lax.axis_index("c")
    chips = [(1 - x, y), (x, 1 - y), (1 - x, 1 - y)]
    return x, y, c, (x, y, 1 - c), chips


class _Gather:
    def __init__(self, ins, outs, send_sems, recv_sems, loc_sems):
        self.ins, self.outs, self.send_sems, self.recv_sems, self.loc_sems = ins, outs, send_sems, recv_sems, loc_sems
        self.x, self.y, self.c, self.sib, self.chips = _mesh_place()
        self.me = (self.x, self.y, self.c)

    def copy(self, a, k, block, to, src=None):
        slot = self.outs[a].at[4 * block[0] + 2 * block[1] + block[2]]
        return pltpu.make_async_remote_copy(
            src_ref=slot if src is None else src, dst_ref=slot, send_sem=self.send_sems.at[a, k],
            recv_sem=self.recv_sems.at[a, k], device_id=to, device_id_type=MESH_ID)

    def own(self, a):
        return pltpu.make_async_copy(self.ins[a], self.outs[a].at[4 * self.x + 2 * self.y + self.c], self.loc_sems.at[a])

    def first(self, a):
        return [self.copy(a, 0, self.me, self.sib, src=self.ins[a])] + [
            self.copy(a, 1 + j, self.me, (*chip, self.c), src=self.ins[a]) for j, chip in enumerate(self.chips)]

    def passed(self, a, j):
        return self.copy(a, 4 + j, (*self.chips[j], self.c), self.sib)

    def start(self):
        for a in range(len(self.ins)):
            self.own(a).start()
            for cp in self.first(a):
                cp.start()

    def finish(self):
        n = len(self.ins)
        for a in range(n):
            for j, chip in enumerate(self.chips):
                self.copy(a, 1 + j, (*chip, self.c), self.me).wait_recv()
                self.passed(a, j).start()
        for a in range(n):
            self.copy(a, 0, self.sib, self.me).wait_recv()
            for j, chip in enumerate(self.chips):
                self.copy(a, 4 + j, (*chip, 1 - self.c), self.me).wait_recv()
        for a in range(n):
            for cp in self.first(a) + [self.passed(a, j) for j in range(N_CHIP - 1)]:
                cp.wait_send()
            self.own(a).wait()

    @staticmethod
    def sems(n):
        return [pltpu.SemaphoreType.DMA((n, N_DEV - 1)), pltpu.SemaphoreType.DMA((n, N_DEV - 1)),
                pltpu.SemaphoreType.DMA((n,))]


def _gather_two_level(arrs, name):
    n = len(arrs)

    def body(*refs):
        g = _Gather(refs[:n], refs[n:2 * n], *refs[2 * n:])
        g.start()
        g.finish()

    any_spec = pl.BlockSpec(memory_space=pl.ANY)
    return pl.pallas_call(
        body, name=name, out_shape=[_sds((N_DEV,) + a.shape, a.dtype) for a in arrs],
        in_specs=[any_spec] * n, out_specs=[any_spec] * n, scratch_shapes=_Gather.sems(n),
        compiler_params=pltpu.CompilerParams(has_side_effects=True),
    )(*arrs)


def _swap_halves(small, grads, name):
    n = len(grads)

    def body(*refs):
        small_ref = refs[0]
        ins = refs[1:1 + n]
        small_out = refs[1 + n]
        got = refs[2 + n:2 + 2 * n]
        s_send, s_recv, g_send, g_recv, loc_sem = refs[2 + 2 * n:]
        x, y, c, sib, _ = _mesh_place()
        me = 4 * x + 2 * y + c
        sends, recvs = [], []
        for j in range(1, N_DEV):
            px = 1 - x if (j >> 2) & 1 else x
            py = 1 - y if (j >> 1) & 1 else y
            pc = 1 - c if j & 1 else c
            cp = pltpu.make_async_remote_copy(
                src_ref=small_ref, dst_ref=small_out.at[me], send_sem=s_send.at[j - 1], recv_sem=s_recv.at[j - 1],
                device_id=(px, py, pc), device_id_type=MESH_ID)
            cp.start()
            sends.append(cp)
            recvs.append(pltpu.make_async_remote_copy(
                src_ref=small_ref, dst_ref=small_out.at[4 * px + 2 * py + pc], send_sem=s_send.at[j - 1],
                recv_sem=s_recv.at[j - 1], device_id=(px, py, pc), device_id_type=MESH_ID))
        own = pltpu.make_async_copy(small_ref, small_out.at[me], loc_sem)
        own.start()
        for a in range(n):
            for chip in range(N_CHIP):
                cp = pltpu.make_async_remote_copy(
                    src_ref=ins[a].at[2 * chip + 1 - c], dst_ref=got[a].at[chip], send_sem=g_send.at[a, chip],
                    recv_sem=g_recv.at[a, chip], device_id=sib, device_id_type=MESH_ID)
                cp.start()
                sends.append(cp)
                recvs.append(cp)
        for cp in sends:
            cp.wait_send()
        for cp in recvs:
            cp.wait_recv()
        own.wait()

    half = [_sds((N_CHIP,) + g.shape[1:], g.dtype) for g in grads]
    any_spec = pl.BlockSpec(memory_space=pl.ANY)
    res = pl.pallas_call(
        body, name=name, out_shape=[_sds((N_DEV,) + small.shape, small.dtype)] + half,
        in_specs=[any_spec] * (1 + n), out_specs=[any_spec] * (1 + n),
        scratch_shapes=[pltpu.SemaphoreType.DMA((N_DEV - 1,)), pltpu.SemaphoreType.DMA((N_DEV - 1,)),
                        pltpu.SemaphoreType.DMA((n, N_CHIP)), pltpu.SemaphoreType.DMA((n, N_CHIP)),
                        pltpu.SemaphoreType.DMA],
        compiler_params=pltpu.CompilerParams(has_side_effects=True),
    )(small, *grads)
    return res[0], res[1:]


def _pair_sum(core, mine, got, name):
    nc, r, c = got.shape
    tr = _tile(r, (256, 128, 64, 32, 16))

    def body(core_ref, a_ref, b_ref, o_ref):
        o_ref[...] = (a_ref[...].astype(F32) + b_ref[...].astype(F32)).astype(BF16)

    return pl.pallas_call(
        body, name=name, out_shape=_sds(got.shape, BF16),
        grid_spec=pltpu.PrefetchScalarGridSpec(
            num_scalar_prefetch=1, grid=(nc, r // tr),
            in_specs=[pl.BlockSpec((1, tr, c), lambda i, j, core_ref: (2 * i + core_ref[0], j, 0)),
                      pl.BlockSpec((1, tr, c), lambda i, j, core_ref: (i, j, 0))],
            out_specs=pl.BlockSpec((1, tr, c), lambda i, j, core_ref: (i, j, 0))),
        compiler_params=_params(("parallel", "parallel")),
    )(core, mine, got)


class _ChipExchange:
    def __init__(self, ins, outs, send_sems, recv_sems, loc_sems):
        self.ins, self.outs, self.send_sems, self.recv_sems, self.loc_sems = ins, outs, send_sems, recv_sems, loc_sems
        self.x, self.y, self.c, _, self.chips = _mesh_place()
        self.mine = 2 * self.x + self.y

    def own(self, a):
        return pltpu.make_async_copy(self.ins[a].at[self.mine], self.outs[a].at[self.mine], self.loc_sems.at[a])

    def copy(self, a, j, lands_in):
        chip = self.chips[j]
        return pltpu.make_async_remote_copy(
            src_ref=self.ins[a].at[2 * chip[0] + chip[1]], dst_ref=self.outs[a].at[lands_in],
            send_sem=self.send_sems.at[a, j], recv_sem=self.recv_sems.at[a, j], device_id=(*chip, self.c),
            device_id_type=MESH_ID)

    def start(self):
        for a in range(len(self.ins)):
            self.own(a).start()
            for j in range(N_CHIP - 1):
                self.copy(a, j, self.mine).start()

    def finish(self):
        for a in range(len(self.ins)):
            for j, chip in enumerate(self.chips):
                self.copy(a, j, self.mine).wait_send()
                self.copy(a, j, 2 * chip[0] + chip[1]).wait_recv()
            self.own(a).wait()

    @staticmethod
    def sems(n):
        return [pltpu.SemaphoreType.DMA((n, N_CHIP - 1)), pltpu.SemaphoreType.DMA((n, N_CHIP - 1)),
                pltpu.SemaphoreType.DMA((n,))]


def _inproj(x, ln_g, w_main, w_ab, late):
    s, d = x.shape
    n = w_main.shape[1]
    tm = _tile(s, (512, 256, 128))
    tn = _tile(n, (1664, 512, 128))
    nl = len(late)
    ni, nj = s // tm, n // tn

    def body(*refs):
        x_ref, g_ref, w_ref, wab_ref = refs[:4]
        proj_ref, ab_ref, ht_ref = refs[4 + nl:7 + nl]
        hs = refs[7 + 2 * nl]
        gather = _Gather(refs[4:4 + nl], refs[7 + nl:7 + 2 * nl], *refs[8 + 2 * nl:])
        step = pl.program_id(0) * nj + pl.program_id(1)

        @pl.when(step == 0)
        def _():
            gather.start()

        @pl.when(pl.program_id(1) == 0)
        def _():
            xv = x_ref[...]
            r = lax.rsqrt(jnp.mean(xv * xv, axis=-1, keepdims=True) + EPS)
            hf = xv * r * g_ref[...]
            h = hf.astype(BF16)
            hs[...] = h
            ht_ref[...] = hf.T.astype(BF16)
            ab_ref[...] = jnp.dot(h, wab_ref[...], preferred_element_type=F32)

        proj_ref[...] = jnp.dot(hs[...], w_ref[...], preferred_element_type=F32)

        @pl.when(step == ni * nj - 1)
        def _():
            gather.finish()

    any_spec = pl.BlockSpec(memory_space=pl.ANY)
    res = pl.pallas_call(
        body, name="inproj", grid=(ni, nj),
        in_specs=[pl.BlockSpec((tm, d), lambda i, j: (i, 0)), pl.BlockSpec((1, d), lambda i, j: (0, 0)),
                  pl.BlockSpec((d, tn), lambda i, j: (0, j)), pl.BlockSpec((d, LANE), lambda i, j: (0, 0))]
        + [any_spec] * nl,
        out_specs=[pl.BlockSpec((tm, tn), lambda i, j: (i, j)), pl.BlockSpec((tm, LANE), lambda i, j: (i, 0)),
                   pl.BlockSpec((d, tm), lambda i, j: (0, i))] + [any_spec] * nl,
        out_shape=[_sds((s, n)), _sds((s, LANE)), _sds((d, s), BF16)]
        + [_sds((N_DEV,) + a.shape, a.dtype) for a in late],
        scratch_shapes=[pltpu.VMEM((tm, d), BF16)] + _Gather.sems(nl),
        compiler_params=_params(("arbitrary", "arbitrary")),
    )(x, ln_g, w_main, w_ab, *late)
    return res[0], res[1], res[2], res[3:]


def _matmul_acc(a, b, name):
    m, k = a.shape
    n = b.shape[1]
    tm = _tile(m, (2048, 1024, 512, 256, 128))
    tn = _tile(n, (1024, 512, 256, 128))
    tk = _tile(k, (1024, 512, 256, 128))

    def body(a_ref, b_ref, o_ref):
        @pl.when(pl.program_id(2) == 0)
        def _():
            o_ref[...] = jnp.zeros_like(o_ref)

        o_ref[...] += jnp.dot(a_ref[...], b_ref[...], preferred_element_type=F32)

    return pl.pallas_call(
        body, name=name, grid=(m // tm, n // tn, k // tk),
        in_specs=[pl.BlockSpec((tm, tk), lambda i, j, l: (i, l)), pl.BlockSpec((tk, tn), lambda i, j, l: (l, j))],
        out_specs=pl.BlockSpec((tm, tn), lambda i, j, l: (i, j)),
        out_shape=_sds((m, n)),
        compiler_params=_params(("parallel", "parallel", "arbitrary")),
    )(a, b)


def _matmul_tn(a, b, name):
    k, m = a.shape
    n = b.shape[1]
    tm = _tile(m, (1024, 512, 256, 128))
    tn = _tile(n, (1024, 512, 256, 128))
    tk = _tile(k, (1024, 512, 256, 128))

    def body(a_ref, b_ref, o_ref):
        @pl.when(pl.program_id(2) == 0)
        def _():
            o_ref[...] = jnp.zeros_like(o_ref)

        o_ref[...] += _mm_tn(a_ref[...], b_ref[...])

    return pl.pallas_call(
        body, name=name, grid=(m // tm, n // tn, k // tk),
        in_specs=[pl.BlockSpec((tk, tm), lambda i, j, l: (l, i)), pl.BlockSpec((tk, tn), lambda i, j, l: (l, j))],
        out_specs=pl.BlockSpec((tm, tn), lambda i, j, l: (i, j)),
        out_shape=_sds((m, n)),
        compiler_params=_params(("parallel", "parallel", "arbitrary")),
    )(a, b)


def _dh(terms, chip_sums):
    s = terms[0][0].shape[0]
    d = terms[0][1].shape[0]
    npc = len(terms)
    nx = len(chip_sums)
    tm = _tile(s, (256, 128))
    tn = _tile(d, (1024, 512, 256, 128))
    nj, ni = d // tn, s // tm

    def body(*refs):
        o_ref = refs[2 * npc + nx]
        exch = _ChipExchange(refs[2 * npc:2 * npc + nx], refs[2 * npc + nx + 1:2 * npc + 2 * nx + 1],
                             *refs[2 * npc + 2 * nx + 1:])
        step = pl.program_id(0) * ni + pl.program_id(1)

        @pl.when(step == 0)
        def _():
            exch.start()

        acc = _mm_nt(refs[0][...], refs[npc][...])
        for p in range(1, npc):
            acc += _mm_nt(refs[p][...], refs[npc + p][...])
        o_ref[...] = acc

        @pl.when(step == ni * nj - 1)
        def _():
            exch.finish()

    any_spec = pl.BlockSpec(memory_space=pl.ANY)
    in_specs = [pl.BlockSpec((tm, w), functools.partial(lambda j, i, pb: (i, pb), pb=pb)) for _, _, w, pb, _ in terms]
    in_specs += [pl.BlockSpec((tn, w), functools.partial(lambda j, i, wb: (j, wb), wb=wb)) for _, _, w, _, wb in terms]
    res = pl.pallas_call(
        body, name="dh", grid=(nj, ni), in_specs=in_specs + [any_spec] * nx,
        out_specs=[pl.BlockSpec((tm, tn), lambda j, i: (i, j))] + [any_spec] * nx,
        out_shape=[_sds((s, d))] + [_sds(p.shape, p.dtype) for p in chip_sums],
        scratch_shapes=_ChipExchange.sems(nx),
        compiler_params=_params(("arbitrary", "arbitrary")),
    )(*[t[0] for t in terms], *[t[1] for t in terms], *chip_sums)
    return res[0], res[1:]


def _rms_bwd(x, dh, dx2, ln_g):
    s, d = x.shape
    tm = _tile(s, (256, 128))

    def body(x_ref, dh_ref, dx2_ref, g_ref, gx_ref, dg_ref):
        @pl.when(pl.program_id(0) == 0)
        def _():
            dg_ref[...] = jnp.zeros_like(dg_ref)

        xv = x_ref[...]
        r = lax.rsqrt(jnp.mean(xv * xv, axis=-1, keepdims=True) + EPS)
        xhat = xv * r
        dhv = dh_ref[...]
        dg_ref[...] += _colsum(dhv * xhat)
        dxh = dhv * g_ref[...]
        gx_ref[...] = dx2_ref[...] + r * (dxh - xhat * jnp.mean(dxh * xhat, axis=-1, keepdims=True))

    row = pl.BlockSpec((tm, d), lambda i: (i, 0))
    vec = pl.BlockSpec((1, d), lambda i: (0, 0))
    return pl.pallas_call(
        body, name="rms_bwd", grid=(s // tm,), in_specs=[row, row, row, vec], out_specs=[row, vec],
        out_shape=[_sds((s, d)), _sds((1, d))], compiler_params=_params(("arbitrary",)),
    )(x, dh, dx2, ln_g)


def _final(x, tgt, out_b, out_a, out_c, w_out, final_g):
    s, d = x.shape
    tm = _tile(s, (256, 128))

    def body(x_ref, t_ref, b_ref, a_ref, c_ref, w_ref, g_ref, dx2_ref, dx2b_ref, dm_ref, loss_ref, dg_ref):
        @pl.when(pl.program_id(0) == 0)
        def _():
            loss_ref[...] = jnp.zeros_like(loss_ref)
            dg_ref[...] = jnp.zeros_like(dg_ref)

        x2 = x_ref[...]
        x2 += jnp.dot(b_ref[...], w_ref[0:DN_W, :], preferred_element_type=F32)
        x2 += jnp.dot(a_ref[...], w_ref[DN_W:DN_W + GMLP_W, :], preferred_element_type=F32)
        x2 += jnp.dot(c_ref[...], w_ref[DN_W + GMLP_W:MIX_W, :], preferred_element_type=F32)
        r = lax.rsqrt(jnp.mean(x2 * x2, axis=-1, keepdims=True) + EPS)
        xhat = x2 * r
        g = g_ref[...]
        err = xhat * g - t_ref[...]
        tok = 0.5 * jnp.mean(err * err, axis=-1, keepdims=True)
        loss_ref[...] += jnp.broadcast_to(_colsum(tok), loss_ref.shape)
        dy = err * (1.0 / d)
        dg_ref[...] += _colsum(dy * xhat)
        dxh = dy * g
        dx2 = r * (dxh - xhat * jnp.mean(dxh * xhat, axis=-1, keepdims=True))
        dx2_ref[...] = dx2
        dx2b = dx2.astype(BF16)
        dx2b_ref[...] = dx2b
        dm_ref[...] = _mm_nt(dx2b, w_ref[...])

    row = pl.BlockSpec((tm, d), lambda i: (i, 0))
    vec = pl.BlockSpec((1, d), lambda i: (0, 0))
    return pl.pallas_call(
        body, name="final", grid=(s // tm,),
        in_specs=[row, row, pl.BlockSpec((tm, DN_W), lambda i: (i, 0)), pl.BlockSpec((tm, GMLP_W), lambda i: (i, 0)),
                  pl.BlockSpec((tm, XA_W), lambda i: (i, 0)), pl.BlockSpec((MIX_W, d), lambda i: (0, 0)), vec],
        out_specs=[row, row, pl.BlockSpec((tm, MIX_W), lambda i: (i, 0)), pl.BlockSpec((1, LANE), lambda i: (0, 0)), vec],
        out_shape=[_sds((s, d)), _sds((s, d), BF16), _sds((s, MIX_W)), _sds((1, LANE)), _sds((1, d))],
        compiler_params=_params(("arbitrary",)),
    )(x, tgt, out_b, out_a, out_c, w_out, final_g)


GU_BLK = (4 * DN_W) // GMLP_W


def _gmlp_norm(gv, lng, lnb):
    va = _gelu(gv)
    mu = jnp.mean(va, axis=-1, keepdims=True)
    xc = va - mu
    rstd = lax.rsqrt(jnp.mean(xc * xc, axis=-1, keepdims=True) + EPS)
    vhat = xc * rstd
    return vhat, rstd, vhat * lng + lnb


def _gmlp_fwd(proj, lng, lnb, ws, bs_t):
    s = proj.shape[0]
    tm = _tile(s, (512, 256, 128))

    def body(u_ref, v_ref, z_ref, lng_ref, lnb_ref, ws_ref, bst_ref, o_ref):
        _, _, vn = _gmlp_norm(v_ref[...], lng_ref[...], lnb_ref[...])
        tri = _iota2((GMLP_T, GMLP_T), 0) >= _iota2((GMLP_T, GMLP_T), 1)
        for g in range(GMLP_G):
            cs = slice(g * HEAD, (g + 1) * HEAD)
            w = jnp.where(tri, ws_ref[g], 0.0).astype(BF16)
            b = bst_ref[:, g:g + 1]
            for c in range(tm // GMLP_T):
                rs = slice(c * GMLP_T, (c + 1) * GMLP_T)
                sg = _mm(w, vn[rs, cs]) + b
                o_ref[rs, cs] = (_gelu(u_ref[rs, cs]) * sg * _silu(z_ref[rs, cs])).astype(BF16)

    col = lambda k: pl.BlockSpec((tm, GMLP_W), lambda i: (i, GU_BLK + k))
    vec = pl.BlockSpec((1, GMLP_W), lambda i: (0, 0))
    return pl.pallas_call(
        body, name="gmlp_fwd", grid=(s // tm,),
        in_specs=[col(0), col(1), col(2), vec, vec, pl.BlockSpec((GMLP_G, GMLP_T, GMLP_T), lambda i: (0, 0, 0)),
                  pl.BlockSpec((GMLP_T, GMLP_G), lambda i: (0, 0))],
        out_specs=pl.BlockSpec((tm, GMLP_W), lambda i: (i, 0)), out_shape=_sds((s, GMLP_W), BF16),
        compiler_params=_params(("parallel",)),
    )(proj, proj, proj, lng, lnb, ws, bs_t)


def _gmlp_bwd(proj, dmixed, lng, lnb, ws, bs_t):
    s = proj.shape[0]
    tm = _tile(s, (512, 256, 128))

    def body(u_ref, v_ref, z_ref, d_ref, lng_ref, lnb_ref, ws_ref, bst_ref,
             dp_ref, dws_ref, dbst_ref, dlng_ref, dlnb_ref, dvn):
        @pl.when(pl.program_id(0) == 0)
        def _():
            dws_ref[...] = jnp.zeros_like(dws_ref)
            dbst_ref[...] = jnp.zeros_like(dbst_ref)
            dlng_ref[...] = jnp.zeros_like(dlng_ref)
            dlnb_ref[...] = jnp.zeros_like(dlnb_ref)

        gv = v_ref[...]
        lng_v = lng_ref[...]
        vhat, rstd, vn = _gmlp_norm(gv, lng_v, lnb_ref[...])
        tri = _iota2((GMLP_T, GMLP_T), 0) >= _iota2((GMLP_T, GMLP_T), 1)
        for g in range(GMLP_G):
            cs = slice(g * HEAD, (g + 1) * HEAD)
            w = jnp.where(tri, ws_ref[g], 0.0).astype(BF16)
            b = bst_ref[:, g:g + 1]
            dw_acc = jnp.zeros((GMLP_T, GMLP_T), F32)
            db_acc = jnp.zeros((GMLP_T, 1), F32)
            for c in range(tm // GMLP_T):
                rs = slice(c * GMLP_T, (c + 1) * GMLP_T)
                vn_b = vn[rs, cs]
                sg = _mm(w, vn_b) + b
                gu = u_ref[rs, cs]
                gz = z_ref[rs, cs]
                da = d_ref[rs, cs]
                uact = _gelu(gu)
                sz = _silu(gz)
                ds = da * uact * sz
                dp_ref[rs, cs] = (da * sg * sz * _gelu_grad(gu)).astype(BF16)
                dp_ref[rs, 2 * GMLP_W + g * HEAD:2 * GMLP_W + (g + 1) * HEAD] = (da * uact * sg * _silu_grad(gz)).astype(BF16)
                dw_acc += _mm_nt(ds, vn_b)
                db_acc += _rowsum(ds)
                dvn[rs, cs] = _mm_tn(w, ds)
            dws_ref[g] += jnp.where(tri, dw_acc, 0.0)
            dbst_ref[:, g:g + 1] += db_acc
        dvn_v = dvn[...]
        dlng_ref[...] += _colsum(dvn_v * vhat)
        dlnb_ref[...] += _colsum(dvn_v)
        dvh = dvn_v * lng_v
        dva = rstd * (dvh - jnp.mean(dvh, axis=-1, keepdims=True) - vhat * jnp.mean(dvh * vhat, axis=-1, keepdims=True))
        dp_ref[:, GMLP_W:2 * GMLP_W] = (dva * _gelu_grad(gv)).astype(BF16)

    col = lambda k: pl.BlockSpec((tm, GMLP_W), lambda i: (i, GU_BLK + k))
    vec = pl.BlockSpec((1, GMLP_W), lambda i: (0, 0))
    wsp = pl.BlockSpec((GMLP_G, GMLP_T, GMLP_T), lambda i: (0, 0, 0))
    bsp = pl.BlockSpec((GMLP_T, GMLP_G), lambda i: (0, 0))
    return pl.pallas_call(
        body, name="gmlp_bwd", grid=(s // tm,),
        in_specs=[col(0), col(1), col(2), pl.BlockSpec((tm, GMLP_W), lambda i: (i, DN_W // GMLP_W)), vec, vec, wsp, bsp],
        out_specs=[pl.BlockSpec((tm, 3 * GMLP_W), lambda i: (i, 0)), wsp, bsp, vec, vec],
        out_shape=[_sds((s, 3 * GMLP_W), BF16), _sds((GMLP_G, GMLP_T, GMLP_T)), _sds((GMLP_T, GMLP_G)),
                   _sds((1, GMLP_W)), _sds((1, GMLP_W))],
        scratch_shapes=[pltpu.VMEM((tm, GMLP_W), F32)],
        compiler_params=_params(("arbitrary",)),
    )(proj, proj, proj, dmixed, lng, lnb, ws, bs_t)


CQ_BLK = (4 * DN_W + 3 * GMLP_W) // XA_W


def _memkv_fwd(mem, g, w_kv):
    nm, d = mem.shape

    def body(m_ref, g_ref, w_ref, kv_ref):
        mv = m_ref[...]
        r = lax.rsqrt(jnp.mean(mv * mv, axis=-1, keepdims=True) + EPS)
        kv_ref[...] = _mm(mv * r * g_ref[...], w_ref[...])

    return pl.pallas_call(body, name="memkv_fwd", out_shape=_sds((nm, 2 * XA_W)), compiler_params=_params())(mem, g, w_kv)


def _memkv_bwd(mem, g, w_kv, dkv):
    nm, d = mem.shape

    def body(m_ref, g_ref, w_ref, dkv_ref, dw_ref, dg_ref):
        mv = m_ref[...]
        r = lax.rsqrt(jnp.mean(mv * mv, axis=-1, keepdims=True) + EPS)
        xhat = mv * r
        dkv_v = dkv_ref[...]
        dw_ref[...] = _mm_tn(xhat * g_ref[...], dkv_v)
        dg_ref[...] = _colsum(_mm_nt(dkv_v, w_ref[...]) * xhat)

    return pl.pallas_call(body, name="memkv_bwd", out_shape=[_sds((d, 2 * XA_W)), _sds((1, d))],
                          compiler_params=_params())(mem, g, w_kv, dkv)


def _xattn_probs(q, mk):
    sc = _mm_nt(q, mk) * (HEAD ** -0.5)
    e = jnp.exp(sc - jnp.max(sc, axis=-1, keepdims=True))
    return e / _rowsum(e)


def _xattn_fwd(proj, mkv):
    s = proj.shape[0]
    nm = mkv.shape[0]
    tm = _tile(s, (512, 256, 128))

    def body(q_ref, z_ref, kv_ref, o_ref):
        for h in range(XA_H):
            cs = slice(h * HEAD, (h + 1) * HEAD)
            p = _xattn_probs(q_ref[:, cs], kv_ref[:, cs])
            ctx = _mm(p, kv_ref[:, XA_W + h * HEAD:XA_W + (h + 1) * HEAD])
            o_ref[:, cs] = (ctx * _silu(z_ref[:, cs])).astype(BF16)

    col = lambda k: pl.BlockSpec((tm, XA_W), lambda i: (i, CQ_BLK + k))
    return pl.pallas_call(
        body, name="xattn_fwd", grid=(s // tm,),
        in_specs=[col(0), col(1), pl.BlockSpec((nm, 2 * XA_W), lambda i: (0, 0))],
        out_specs=pl.BlockSpec((tm, XA_W), lambda i: (i, 0)), out_shape=_sds((s, XA_W), BF16),
        compiler_params=_params(("parallel",)),
    )(proj, proj, mkv)


def _xattn_bwd(proj, dmixed, mkv):
    s = proj.shape[0]
    nm = mkv.shape[0]
    tm = _tile(s, (512, 256, 128))

    def body(q_ref, z_ref, d_ref, kv_ref, dp_ref, dkv_ref):
        @pl.when(pl.program_id(0) == 0)
        def _():
            dkv_ref[...] = jnp.zeros_like(dkv_ref)

        for h in range(XA_H):
            cs = slice(h * HEAD, (h + 1) * HEAD)
            vs = slice(XA_W + h * HEAD, XA_W + (h + 1) * HEAD)
            q = q_ref[:, cs]
            z = z_ref[:, cs]
            mk = kv_ref[:, cs]
            mv = kv_ref[:, vs]
            p = _xattn_probs(q, mk)
            ctx = _mm(p, mv)
            dc = d_ref[:, cs]
            dctx = dc * _silu(z)
            dp_ref[:, vs] = (dc * ctx * _silu_grad(z)).astype(BF16)
            dp = _mm_nt(dctx, mv)
            dkv_ref[:, vs] += _mm_tn(p, dctx)
            ds = p * (dp - _rowsum(dp * p)) * (HEAD ** -0.5)
            dp_ref[:, cs] = _mm(ds, mk).astype(BF16)
            dkv_ref[:, cs] += _mm_tn(ds, q)

    col = lambda k: pl.BlockSpec((tm, XA_W), lambda i: (i, CQ_BLK + k))
    kvs = pl.BlockSpec((nm, 2 * XA_W), lambda i: (0, 0))
    return pl.pallas_call(
        body, name="xattn_bwd", grid=(s // tm,),
        in_specs=[col(0), col(1), pl.BlockSpec((tm, XA_W), lambda i: (i, (DN_W + GMLP_W) // XA_W)), kvs],
        out_specs=[pl.BlockSpec((tm, 2 * XA_W), lambda i: (i, 0)), kvs],
        out_shape=[_sds((s, 2 * XA_W), BF16), _sds((nm, 2 * XA_W))],
        compiler_params=_params(("arbitrary",)),
    )(proj, proj, dmixed, mkv)


def _softplus(x):
    return jnp.maximum(x, 0.0) + jnp.log1p(jnp.exp(-jnp.abs(x)))


def _dn_pre(proj, ab, conv_w, alog_row, dt_row):
    s = proj.shape[0]
    tm = _tile(s, (256, 128))
    w3 = 3 * DN_W

    def body(x_ref, halo_ref, ab_ref, cw_ref, al_ref, dt_ref, q_ref, k_ref, v_ref, gb_ref, gbt_ref, yc_ref):
        i = pl.program_id(0)
        xv = x_ref[...]
        cat = jnp.concatenate([jnp.where(i > 0, halo_ref[...], 0.0), xv[0:HALO]], axis=0)
        yc = cw_ref[DN_K - 1:DN_K, :] * xv
        top = cw_ref[DN_K - 1:DN_K, :] * xv[0:HALO]
        for t in range(DN_K - 1):
            back = DN_K - 1 - t
            yc += cw_ref[t:t + 1, :] * pltpu.roll(xv, back, 0)
            top += cw_ref[t:t + 1, :] * pltpu.roll(cat, back, 0)[HALO:2 * HALO]
        yc = jnp.concatenate([top, yc[HALO:tm]], axis=0)
        yc_ref[...] = yc
        act = _silu(yc)
        for h in range(DN_H):
            cs = slice(h * HEAD, (h + 1) * HEAD)
            qa = act[:, cs]
            q_ref[:, cs] = qa * (lax.rsqrt(_rowsum(qa * qa) + EPS) * (HEAD ** -0.5))
            ka = act[:, DN_W + h * HEAD:DN_W + (h + 1) * HEAD]
            k_ref[:, cs] = ka * lax.rsqrt(_rowsum(ka * ka) + EPS)
        v_ref[...] = act[:, 2 * DN_W:w3]
        abv = ab_ref[...]
        lane = _iota2((tm, LANE), 1)
        g = jnp.where(lane < DN_H, -jnp.exp(al_ref[...]) * _softplus(abv + dt_ref[...]), 0.0)
        gc = _mm_hi(_chunk_tri(tm, False), g)
        gbv = jnp.where(lane < DN_H, gc, jnp.where(lane < 2 * DN_H, jax.nn.sigmoid(abv), 0.0))
        gb_ref[...] = gbv
        for c in range(tm // CH):
            gbt_ref[c] = gbv[c * CH:(c + 1) * CH, :].T[0:2 * DN_H, :]

    hb = tm // HALO
    row = lambda w: pl.BlockSpec((tm, w), lambda i: (i, 0))
    vec = pl.BlockSpec((1, LANE), lambda i: (0, 0))
    return pl.pallas_call(
        body, name="dn_pre", grid=(s // tm,),
        in_specs=[row(w3), pl.BlockSpec((HALO, w3), lambda i: (jnp.maximum(i * hb - 1, 0), 0)), row(LANE),
                  pl.BlockSpec((DN_K, w3), lambda i: (0, 0)), vec, vec],
        out_specs=[row(DN_W), row(DN_W), row(DN_W), row(LANE), pl.BlockSpec((tm // CH, 2 * DN_H, CH), lambda i: (i, 0, 0)),
                   row(w3)],
        out_shape=[_sds((s, DN_W)), _sds((s, DN_W)), _sds((s, DN_W)), _sds((s, LANE)), _sds((s // CH, 2 * DN_H, CH)),
                   _sds((s, w3))],
        compiler_params=_params(("parallel",)),
    )(proj, proj, ab, conv_w, alog_row, dt_row)


HEADS = tuple(range(DN_H))


def _hcols(h):
    return slice(h * HEAD, (h + 1) * HEAD)


def _chunk_scalings(k, v, gbv, gbt, h):
    gc = gbv[:, h:h + 1]
    beta = gbv[:, DN_H + h:DN_H + h + 1]
    gr = gbt[h:h + 1, :]
    ii = _iota2((CH, CH), 0)
    jj = _iota2((CH, CH), 1)
    dec = jnp.exp(jnp.where(ii >= jj, gc - gr, -1e30))
    eg = jnp.exp(gc)
    gl = gr[:, CH - 1:CH]
    kb = k * beta
    return dict(beta=beta, dec=dec, eg=eg, gl=gl, ekd=jnp.exp(gl - gc), kb=kb, vb=v * beta, kbe=kb * eg)


def _chunk_scores(m, q, k):
    kq = _mm_nt(jnp.concatenate([m["kb"], q], axis=0), k)
    strict = _iota2((CH, CH), 0) > _iota2((CH, CH), 1)
    return jnp.where(strict, kq[0:CH] * m["dec"], 0.0), kq[CH:2 * CH] * m["dec"]


def _dn_local(q, k, v, gb, gbt):
    s = q.shape[0]
    cpb = 4 if (s // CH) % 4 == 0 else 1
    tb = cpb * CH
    nblk = s // tb

    def body(q_ref, k_ref, v_ref, gb_ref, gbt_ref, u_ref, w_ref, qg_ref, kd_ref, t_ref, ai_ref, egl_ref):
        def chunk(c, carry):
            r0 = pl.multiple_of(c * CH, CH)
            rows = pl.ds(r0, CH)
            gbv = gb_ref[rows, :]
            gbt_v = gbt_ref[c]
            qs = [q_ref[rows, _hcols(h)] for h in HEADS]
            ks = [k_ref[rows, _hcols(h)] for h in HEADS]
            ms = [_chunk_scalings(ks[h], v_ref[rows, _hcols(h)], gbv, gbt_v, h) for h in HEADS]
            for h in HEADS:
                qg_ref[rows, _hcols(h)] = (qs[h] * ms[h]["eg"]).astype(BF16)
                kd_ref[rows, _hcols(h)] = (ks[h] * ms[h]["ekd"]).astype(BF16)
                egl_ref[c, h:h + 1, :] = jnp.broadcast_to(jnp.exp(ms[h]["gl"]), (1, LANE))
            sc = [_chunk_scores(ms[h], qs[h], ks[h]) for h in HEADS]
            for h in HEADS:
                ai_ref[h, rows, :] = sc[h][1]
            eye = jnp.where(_iota2((CH, CH), 0) == _iota2((CH, CH), 1), 1.0, 0.0).astype(F32)
            ts = [eye - sc[h][0] for h in HEADS]
            ps = [_mm_3x(sc[h][0], sc[h][0]) for h in HEADS]
            ts = [ts[h] + _mm_3x(ts[h], ps[h]) for h in HEADS]
            for _ in range(4):
                ps = [_mm(ps[h], ps[h]) for h in HEADS]
                ts = [ts[h] + _mm(ts[h], ps[h]) for h in HEADS]
            for h in HEADS:
                t_ref[h, rows, :] = ts[h]
                uw = _mm(ts[h], jnp.concatenate([ms[h]["vb"], ms[h]["kbe"]], axis=1))
                u_ref[rows, _hcols(h)] = uw[:, 0:HEAD]
                w_ref[rows, _hcols(h)] = uw[:, HEAD:2 * HEAD].astype(BF16)
            return carry

        lax.fori_loop(0, cpb, chunk, 0)

    row = pl.BlockSpec((tb, DN_W), lambda i: (i, 0))
    sq = pl.BlockSpec((DN_H, tb, CH), lambda i: (0, i, 0))
    return pl.pallas_call(
        body, name="dn_local", grid=(nblk,),
        in_specs=[row, row, row, pl.BlockSpec((tb, LANE), lambda i: (i, 0)),
                  pl.BlockSpec((cpb, 2 * DN_H, CH), lambda i: (i, 0, 0))],
        out_specs=[row, row, row, row, sq, sq, pl.BlockSpec((cpb, DN_H, LANE), lambda i: (i, 0, 0))],
        out_shape=[_sds((s, DN_W)), _sds((s, DN_W), BF16), _sds((s, DN_W), BF16), _sds((s, DN_W), BF16),
                   _sds((DN_H, s, CH)), _sds((DN_H, s, CH)), _sds((s // CH, DN_H, LANE))],
        compiler_params=_params(("parallel",)),
    )(q, k, v, gb, gbt)


def _scan_cpb(s):
    return 8 if (s // CH) % 8 == 0 else 1


def _dn_scan(u, w, qg, kd, ai, egl, proj, norm_g):
    s = u.shape[0]
    cpb = _scan_cpb(s)
    tb = cpb * CH
    nblk = s // tb

    def body(u_ref, w_ref, qg_ref, kd_ref, ai_ref, egl_ref, z_ref, ng_ref, o_ref, vn_ref, st_ref, ob_ref, state):
        @pl.when(pl.program_id(0) == 0)
        def _():
            state[...] = jnp.zeros_like(state)

        ng = ng_ref[...]

        def chunk(c, carry):
            r0 = pl.multiple_of(c * CH, CH)
            rows = pl.ds(r0, CH)
            sts = [state[h] for h in HEADS]
            stb = [sts[h].astype(BF16) for h in HEADS]
            for h in HEADS:
                st_ref[c, h] = sts[h]
            vns = [u_ref[rows, _hcols(h)] - jnp.dot(w_ref[rows, _hcols(h)], stb[h], preferred_element_type=F32)
                   for h in HEADS]
            vnb = [vns[h].astype(BF16) for h in HEADS]
            for h in HEADS:
                state[h] = sts[h] * egl_ref[c, h:h + 1, :] + _mm_tn(kd_ref[rows, _hcols(h)], vnb[h])
            os_ = [jnp.dot(qg_ref[rows, _hcols(h)], stb[h], preferred_element_type=F32) + _mm(ai_ref[h, rows, :], vnb[h])
                   for h in HEADS]
            for h in HEADS:
                o = os_[h]
                vn_ref[rows, _hcols(h)] = vns[h]
                o_ref[rows, _hcols(h)] = o
                r = lax.rsqrt(jnp.mean(o * o, axis=-1, keepdims=True) + EPS)
                ob_ref[rows, _hcols(h)] = (o * r * ng * _silu(z_ref[rows, _hcols(h)])).astype(BF16)
            return carry

        lax.fori_loop(0, cpb, chunk, 0)

    row = pl.BlockSpec((tb, DN_W), lambda i: (i, 0))
    return pl.pallas_call(
        body, name="dn_scan", grid=(nblk,),
        in_specs=[row, row, row, row, pl.BlockSpec((DN_H, tb, CH), lambda i: (0, i, 0)),
                  pl.BlockSpec((cpb, DN_H, LANE), lambda i: (i, 0, 0)), pl.BlockSpec((tb, DN_W), lambda i: (i, 3)),
                  pl.BlockSpec((1, HEAD), lambda i: (0, 0))],
        out_specs=[row, row, pl.BlockSpec((cpb, DN_H, HEAD, HEAD), lambda i: (i, 0, 0, 0)), row],
        out_shape=[_sds((s, DN_W)), _sds((s, DN_W)), _sds((s // CH, DN_H, HEAD, HEAD)), _sds((s, DN_W), BF16)],
        scratch_shapes=[pltpu.VMEM((DN_H, HEAD, HEAD), F32)],
        compiler_params=_params(("arbitrary",)),
    )(u, w, qg, kd, ai, egl, proj, norm_g)


def _dn_scan_bwd(dmixed, o, proj, norm_g, w, qg, kd, ai, egl):
    s = o.shape[0]
    cpb = _scan_cpb(s)
    tb = cpb * CH
    nblk = s // tb

    def body(dm_ref, o_ref, z_ref, ng_ref, w_ref, qg_ref, kd_ref, ai_ref, egl_ref,
             do_ref, dvn_ref, dst_ref, dz_ref, dng_ref, dstate):
        @pl.when(pl.program_id(0) == 0)
        def _():
            dstate[...] = jnp.zeros_like(dstate)
            dng_ref[...] = jnp.zeros_like(dng_ref)

        ng = ng_ref[...]

        def chunk(cc, carry):
            c = cpb - 1 - cc
            r0 = pl.multiple_of(c * CH, CH)
            rows = pl.ds(r0, CH)
            dng = jnp.zeros((1, HEAD), F32)
            dob = []
            for h in HEADS:
                cs = _hcols(h)
                o = o_ref[rows, cs]
                z = z_ref[rows, cs]
                db = dm_ref[rows, cs]
                r = lax.rsqrt(jnp.mean(o * o, axis=-1, keepdims=True) + EPS)
                ohat = o * r
                dz_ref[rows, cs] = (db * ohat * ng * _silu_grad(z)).astype(BF16)
                dyn = db * _silu(z)
                dng += _colsum(dyn * ohat)
                doh = dyn * ng
                do = r * (doh - ohat * jnp.mean(doh * ohat, axis=-1, keepdims=True))
                do_ref[rows, cs] = do
                dob.append(do.astype(BF16))
            dng_ref[...] += dng
            dsn = [dstate[h] for h in HEADS]
            for h in HEADS:
                dst_ref[c, h] = dsn[h]
            dvn = [_mm_tn(ai_ref[h, rows, :], dob[h])
                   + jnp.dot(kd_ref[rows, _hcols(h)], dsn[h].astype(BF16), preferred_element_type=F32) for h in HEADS]
            part = [_mm_tn(qg_ref[rows, _hcols(h)], dob[h]) + egl_ref[c, h:h + 1, :] * dsn[h] for h in HEADS]
            for h in HEADS:
                dvn_ref[rows, _hcols(h)] = dvn[h]
                dstate[h] = part[h] - _mm_tn(w_ref[rows, _hcols(h)], dvn[h])
            return carry

        lax.fori_loop(0, cpb, chunk, 0)

    rev = lambda i: (nblk - 1 - i, 0)
    row = pl.BlockSpec((tb, DN_W), rev)
    vec = pl.BlockSpec((1, HEAD), lambda i: (0, 0))
    return pl.pallas_call(
        body, name="dn_scan_bwd", grid=(nblk,),
        in_specs=[row, row, pl.BlockSpec((tb, DN_W), lambda i: (nblk - 1 - i, 3)), vec, row, row, row,
                  pl.BlockSpec((DN_H, tb, CH), lambda i: (0, nblk - 1 - i, 0)),
                  pl.BlockSpec((cpb, DN_H, LANE), lambda i: (nblk - 1 - i, 0, 0))],
        out_specs=[row, row, pl.BlockSpec((cpb, DN_H, HEAD, HEAD), lambda i: (nblk - 1 - i, 0, 0, 0)), row, vec],
        out_shape=[_sds((s, DN_W)), _sds((s, DN_W)), _sds((s // CH, DN_H, HEAD, HEAD)), _sds((s, DN_W), BF16),
                   _sds((1, HEAD))],
        scratch_shapes=[pltpu.VMEM((DN_H, HEAD, HEAD), F32)],
        compiler_params=_params(("arbitrary",)),
    )(dmixed, o, proj, norm_g, w, qg, kd, ai, egl)


def _dn_local_bwd(q, k, v, gb, gbt, t, vn, st, dst, do, dvn):
    s = q.shape[0]
    cpb = 4 if (s // CH) % 4 == 0 else 1
    tb = cpb * CH
    nblk = s // tb

    def body(q_ref, k_ref, v_ref, gb_ref, gbt_ref, t_ref, vn_ref, st_ref, dst_ref, do_ref, dvn_ref,
             dq_ref, dk_ref, dv_ref, dgb_ref):
        lane = _iota2((CH, LANE), 1)
        last = _iota2((CH, 1), 0) == CH - 1

        def chunk(c, carry):
            r0 = pl.multiple_of(c * CH, CH)
            rows = pl.ds(r0, CH)
            gbv = gb_ref[rows, :]
            gbt_v = gbt_ref[c]
            strict = _iota2((CH, CH), 0) > _iota2((CH, CH), 1)
            qs = [q_ref[rows, _hcols(h)] for h in HEADS]
            ks = [k_ref[rows, _hcols(h)] for h in HEADS]
            vs = [v_ref[rows, _hcols(h)] for h in HEADS]
            ms = [_chunk_scalings(ks[h], vs[h], gbv, gbt_v, h) for h in HEADS]
            sts = [st_ref[c, h] for h in HEADS]
            dsn = [dst_ref[c, h] for h in HEADS]
            dob = [do_ref[rows, _hcols(h)].astype(BF16) for h in HEADS]
            dvnb = [dvn_ref[rows, _hcols(h)].astype(BF16) for h in HEADS]
            vnb = [vn_ref[rows, _hcols(h)].astype(BF16) for h in HEADS]
            tbf = [t_ref[h, rows, :].astype(BF16) for h in HEADS]
            sc = [_chunk_scores(ms[h], qs[h], ks[h]) for h in HEADS]
            xs_ = [_mm_nt(jnp.concatenate([dob[h], dvnb[h]], axis=0), sts[h]) for h in HEADS]
            dai = [_mm_nt(dob[h], vnb[h]) for h in HEADS]
            dkd = [_mm_nt(vnb[h], dsn[h]) for h in HEADS]
            dqg = [xs_[h][0:CH] for h in HEADS]
            duw = [jnp.concatenate([dvnb[h], (-xs_[h][CH:2 * CH]).astype(BF16)], axis=1) for h in HEADS]
            dt = [_mm_nt(duw[h], jnp.concatenate([ms[h]["vb"], ms[h]["kbe"]], axis=1)) for h in HEADS]
            dvk = [_mm_tn(tbf[h], duw[h]) for h in HEADS]
            tdt = [_mm_tn(tbf[h], dt[h]) for h in HEADS]
            da = [jnp.where(strict, -_mm_nt(tdt[h], tbf[h]), 0.0) for h in HEADS]
            dsc = [jnp.concatenate([da[h] * ms[h]["dec"], dai[h] * ms[h]["dec"]], axis=0) for h in HEADS]
            dkq = [_mm(dsc[h], ks[h]) for h in HEADS]
            dk1 = [_mm_tn(dsc[h], jnp.concatenate([ms[h]["kb"], qs[h]], axis=0)) for h in HEADS]
            dgb = jnp.zeros((CH, LANE), F32)
            for h in HEADS:
                m = ms[h]
                eg, ekd, beta = m["eg"], m["ekd"], m["beta"]
                dvb = dvk[h][:, 0:HEAD]
                dkbe = dvk[h][:, HEAD:2 * HEAD]
                kd = ks[h] * ekd
                dkb = dkq[h][0:CH] + dkbe * eg
                dq_ref[rows, _hcols(h)] = dkq[h][CH:2 * CH] + dqg[h] * eg
                dk_ref[rows, _hcols(h)] = dk1[h] + dkd[h] * ekd + dkb * beta
                dv_ref[rows, _hcols(h)] = dvb * beta
                dgl = jnp.exp(m["gl"]) * _colsum(_rowsum(sts[h] * dsn[h])) + _colsum(_rowsum(dkd[h] * kd))
                mm_ = da[h] * sc[h][0] + dai[h] * sc[h][1]
                dgc = (_rowsum(mm_) - _rowsum(mm_.T) + _rowsum(dqg[h] * qs[h] * eg) - _rowsum(dkd[h] * kd)
                       + _rowsum(dkbe * m["kbe"]) + jnp.where(last, dgl, 0.0))
                dbeta = _rowsum(dkb * ks[h]) + _rowsum(dvb * vs[h])
                dgb = jnp.where(lane == h, dgc, jnp.where(lane == DN_H + h, dbeta, dgb))
            dgb_ref[rows, :] = dgb
            return carry

        lax.fori_loop(0, cpb, chunk, 0)

    row = pl.BlockSpec((tb, DN_W), lambda i: (i, 0))
    gbs = pl.BlockSpec((tb, LANE), lambda i: (i, 0))
    sts = pl.BlockSpec((cpb, DN_H, HEAD, HEAD), lambda i: (i, 0, 0, 0))
    return pl.pallas_call(
        body, name="dn_local_bwd", grid=(nblk,),
        in_specs=[row, row, row, gbs, pl.BlockSpec((cpb, 2 * DN_H, CH), lambda i: (i, 0, 0)),
                  pl.BlockSpec((DN_H, tb, CH), lambda i: (0, i, 0)), row, sts, sts, row, row],
        out_specs=[row, row, row, gbs],
        out_shape=[_sds((s, DN_W)), _sds((s, DN_W)), _sds((s, DN_W)), _sds((s, LANE))],
        compiler_params=_params(("parallel",)),
    )(q, k, v, gb, gbt, t, vn, st, dst, do, dvn)


def _dn_pre_bwd(proj, yc_all, ab, conv_w, alog_row, dt_row, dq, dk, dv, dgb):
    s = proj.shape[0]
    tm = _tile(s, (256, 128))
    w3 = 3 * DN_W
    nblk = s // tm

    def body(x_ref, yc_ref, ab_ref, cw_ref, al_ref, dt_ref, dq_ref, dk_ref, dv_ref, dgb_ref,
             dx_ref, dab_ref, dcw_ref, dal_ref, ddt_ref, exd, carry):
        i = pl.program_id(0)

        @pl.when(i == 0)
        def _():
            carry[...] = jnp.zeros_like(carry)
            dcw_ref[...] = jnp.zeros_like(dcw_ref)
            dal_ref[...] = jnp.zeros_like(dal_ref)
            ddt_ref[...] = jnp.zeros_like(ddt_ref)

        yc = yc_ref[...]
        sg = jax.nn.sigmoid(yc)
        act = yc * sg
        dact = sg * (1.0 + yc * (1.0 - sg))
        for h in range(DN_H):
            cs = slice(h * HEAD, (h + 1) * HEAD)
            ks = slice(DN_W + h * HEAD, DN_W + (h + 1) * HEAD)
            qa = act[:, cs]
            rq = lax.rsqrt(_rowsum(qa * qa) + EPS)
            qh = qa * rq
            dqv = dq_ref[:, cs]
            exd[0:tm, cs] = (HEAD ** -0.5) * rq * (dqv - qh * _rowsum(dqv * qh)) * dact[:, cs]
            ka = act[:, ks]
            rk = lax.rsqrt(_rowsum(ka * ka) + EPS)
            kh = ka * rk
            dkv = dk_ref[:, cs]
            exd[0:tm, ks] = rk * (dkv - kh * _rowsum(dkv * kh)) * dact[:, ks]
        exd[0:tm, 2 * DN_W:w3] = dv_ref[...] * dact[:, 2 * DN_W:w3]
        xv = x_ref[...]
        dyc = exd[...]
        cat = jnp.concatenate([dyc[tm - HALO:tm], carry[...]], axis=0)
        dcw_ref[DN_K - 1:DN_K, :] += _colsum(dyc * xv)
        dx = cw_ref[DN_K - 1:DN_K, :] * dyc
        for t in range(DN_K - 1):
            ahead = DN_K - 1 - t
            view = jnp.concatenate([pltpu.roll(dyc, tm - ahead, 0)[0:tm - HALO],
                                    pltpu.roll(cat, 2 * HALO - ahead, 0)[0:HALO]], axis=0)
            dcw_ref[t:t + 1, :] += _colsum(view * xv)
            dx += cw_ref[t:t + 1, :] * view
        dx_ref[...] = dx.astype(BF16)
        carry[...] = dyc[0:HALO]

        lane = _iota2((tm, LANE), 1)
        dgbv = dgb_ref[...]
        dg = _mm_hi(_chunk_tri(tm, True), jnp.where(lane < DN_H, dgbv, 0.0))
        abv = ab_ref[...]
        xa = abv + dt_ref[...]
        nea = -jnp.exp(al_ref[...])
        d_da = jnp.where(lane < DN_H, dg * nea * jax.nn.sigmoid(xa), 0.0)
        dal_ref[...] += _colsum(jnp.where(lane < DN_H, dg * nea * _softplus(xa), 0.0))
        ddt_ref[...] += _colsum(d_da)
        beta = jax.nn.sigmoid(abv)
        d_db = jnp.where((lane >= DN_H) & (lane < 2 * DN_H), dgbv * beta * (1.0 - beta), 0.0)
        dab_ref[...] = (d_da + d_db).astype(BF16)

    rev = lambda i: (nblk - 1 - i, 0)
    row = lambda w: pl.BlockSpec((tm, w), rev)
    vec = pl.BlockSpec((1, LANE), lambda i: (0, 0))
    cws = pl.BlockSpec((DN_K, w3), lambda i: (0, 0))
    return pl.pallas_call(
        body, name="dn_pre_bwd", grid=(nblk,),
        in_specs=[row(w3), row(w3), row(LANE), cws, vec, vec, row(DN_W), row(DN_W), row(DN_W), row(LANE)],
        out_specs=[row(w3), row(LANE), cws, vec, vec],
        out_shape=[_sds((s, w3), BF16), _sds((s, LANE), BF16), _sds((DN_K, w3)), _sds((1, LANE)), _sds((1, LANE))],
        scratch_shapes=[pltpu.VMEM((tm, w3), F32), pltpu.VMEM((HALO, w3), F32)],
        compiler_params=_params(("arbitrary",)),
    )(proj, yc_all, ab, conv_w, alog_row, dt_row, dq, dk, dv, dgb)


def _adam(parts, w, m, v, name):
    r, c = w.shape
    n_parts = parts.shape[0]
    tr = _tile(r, (128, 64, 32, 16, 8))

    def body(p_ref, w_ref, m_ref, v_ref, g_ref, d_ref, nm_ref, nv_ref):
        g = p_ref[0].astype(F32)
        for k in range(1, n_parts):
            g = g + p_ref[k].astype(F32)
        g_ref[...] = g
        mn = ADAM_B1 * m_ref[...] + (1.0 - ADAM_B1) * g
        vn = ADAM_B2 * v_ref[...] + (1.0 - ADAM_B2) * (g * g)
        m_hat = mn / (1.0 - ADAM_B1 ** ADAM_STEP)
        v_hat = vn / (1.0 - ADAM_B2 ** ADAM_STEP)
        d_ref[...] = -ADAM_LR * (m_hat / (jnp.sqrt(v_hat) + ADAM_EPS) + ADAM_WD * w_ref[...])
        nm_ref[...] = mn
        nv_ref[...] = vn

    blk = pl.BlockSpec((tr, c), lambda i: (i, 0))
    return pl.pallas_call(
        body, name=name, grid=(r // tr,),
        in_specs=[pl.BlockSpec((n_parts, tr, c), lambda i: (0, i, 0)), blk, blk, blk],
        out_specs=[blk, blk, blk, blk], out_shape=[_sds((r, c))] * 4,
        compiler_params=_params(("parallel",)),
    )(parts, w, m, v)


_PACK_ROWS = 8


def _pack(vals):
    tiles = []
    for a in vals:
        flat = a.reshape(-1).astype(F32)
        unit = _PACK_ROWS * LANE
        n = -(-flat.shape[0] // unit) * unit
        tiles.append(jnp.pad(flat, (0, n - flat.shape[0])).reshape(n // LANE, LANE))
    return jnp.concatenate(tiles, axis=0)


def _unpack(packed, shapes):
    out = []
    r0 = 0
    for shp in shapes:
        size = 1
        for dim in shp:
            size *= dim
        unit = _PACK_ROWS * LANE
        rows = -(-size // unit) * _PACK_ROWS
        out.append(packed[r0:r0 + rows].reshape(-1)[:size].reshape(shp))
        r0 += rows
    return out


def _lane_row(vec8):
    return jnp.pad(vec8.reshape(1, -1).astype(F32), ((0, 0), (0, LANE - vec8.size)))


def kernel(x, mem, ln_g, w_in, gmlp_ln_g, gmlp_ln_b, gmlp_ws, gmlp_bs, conv_w, dn_a_log, dn_dt_bias, dn_norm_g, mem_norm_g, w_mem_kv, w_out, final_g, loss_target, m_ln_g, m_w_in, m_gmlp_ln_g, m_gmlp_ln_b, m_gmlp_ws, m_gmlp_bs, m_conv_w, m_dn_a_log, m_dn_dt_bias, m_dn_norm_g, m_mem_norm_g, m_w_mem_kv, m_w_out, m_final_g, v_ln_g, v_w_in, v_gmlp_ln_g, v_gmlp_ln_b, v_gmlp_ws, v_gmlp_bs, v_conv_w, v_dn_a_log, v_dn_dt_bias, v_dn_norm_g, v_mem_norm_g, v_w_mem_kv, v_w_out, v_final_g):
    xs = x[0]
    mems = mem[0]
    tgt = loss_target[0]
    s, d = xs.shape
    shard_w = w_in.shape[2]
    in_w = N_DEV * shard_w
    me = 4 * lax.axis_index("x") + 2 * lax.axis_index("y") + lax.axis_index("c")

    (g_in,) = _gather_two_level([w_in[0].astype(BF16)], "gather_w_in")
    o_g, o_dn, o_ab = 0, 3 * GMLP_W, 3 * GMLP_W + 4 * DN_W
    o_xa = o_ab + 2 * DN_H

    def shard_cols(lo, hi):
        out = []
        while lo < hi:
            sh = lo // shard_w
            end = min(hi, (sh + 1) * shard_w)
            out.append(g_in[sh][:, lo - sh * shard_w:end - sh * shard_w])
            lo = end
        return out

    w_main = jnp.concatenate(shard_cols(o_dn, o_ab) + shard_cols(o_g, o_dn) + shard_cols(o_xa, in_w), axis=1)
    w_ab = jnp.pad(jnp.concatenate(shard_cols(o_ab, o_xa), axis=1), ((0, 0), (0, LANE - 2 * DN_H)))

    ln_g2 = ln_g.reshape(1, d)
    lng2 = gmlp_ln_g.reshape(1, GMLP_W)
    lnb2 = gmlp_ln_b.reshape(1, GMLP_W)
    ws3 = gmlp_ws[0]
    bs_t = gmlp_bs[0].T
    alog_row = _lane_row(dn_a_log)
    dt_row = _lane_row(dn_dt_bias)
    dn_g2 = dn_norm_g.reshape(1, HEAD)
    mem_g2 = mem_norm_g.reshape(1, d)
    fin_g2 = final_g.reshape(1, d)

    proj, ab, h_t, (g_out, g_kv, g_conv) = _inproj(
        xs, ln_g2, w_main, w_ab, [w_out[0].astype(BF16), w_mem_kv[0].astype(BF16), conv_w[0]])
    wo = g_out.reshape(MIX_W, d)
    wo_perm = jnp.concatenate([wo[GMLP_W:GMLP_W + DN_W], wo[0:GMLP_W], wo[GMLP_W + DN_W:MIX_W]], axis=0)
    w_kv = g_kv.reshape(d, 2 * XA_W)
    conv_full = g_conv.transpose(1, 0, 2).reshape(DN_K, 3 * DN_W)
    out_a = _gmlp_fwd(proj, lng2, lnb2, ws3, bs_t)
    mkv = _memkv_fwd(mems, mem_g2, w_kv)
    out_c = _xattn_fwd(proj, mkv)
    q, k, v, gb, gbt, yc = _dn_pre(proj, ab, conv_full, alog_row, dt_row)
    u, wk, qg, kd, tmat, ai, egl = _dn_local(q, k, v, gb, gbt)
    o, vn, st, out_b = _dn_scan(u, wk, qg, kd, ai, egl, proj, dn_g2)

    dx2, dx2b, dmixed, loss_acc, d_fin_g = _final(xs, tgt, out_b, out_a, out_c, wo_perm, fin_g2)
    loss = lax.psum(loss_acc[0, 0], ("x", "y", "c"))

    dwo_b = _matmul_tn(out_b, dx2b, "dw_out_b")
    dwo_a = _matmul_tn(out_a, dx2b, "dw_out_a")
    dwo_c = _matmul_tn(out_c, dx2b, "dw_out_c")
    d_w_out = jnp.concatenate([dwo_a, dwo_b, dwo_c], axis=0)

    dp_g, d_ws, d_bst, d_lng, d_lnb = _gmlp_bwd(proj, dmixed, lng2, lnb2, ws3, bs_t)
    dp_x, dmkv = _xattn_bwd(proj, dmixed, mkv)
    d_w_kv, d_mem_g = _memkv_bwd(mems, mem_g2, w_kv, dmkv)
    do, dvn, dst, dp_dz, d_dn_g = _dn_scan_bwd(dmixed, o, proj, dn_g2, wk, qg, kd, ai, egl)
    dq, dk, dv, dgb = _dn_local_bwd(q, k, v, gb, gbt, tmat, vn, st, dst, do, dvn)
    dp_qkv, dp_ab, d_conv, d_alog, d_dt = _dn_pre_bwd(proj, yc, ab, conv_full, alog_row, dt_row, dq, dk, dv, dgb)

    terms = [(dp_qkv, w_main, 3 * DN_W, 0, 0), (dp_dz, w_main, DN_W, 0, 3)]
    terms += [(dp_g, w_main, GMLP_W, b, GU_BLK + b) for b in range(3)]
    terms += [(dp_x, w_main, XA_W, b, CQ_BLK + b) for b in range(2)]
    terms += [(dp_ab, w_ab, LANE, 0, 0)]
    dw_qkv = _matmul_acc(h_t, dp_qkv, "dw_in_qkv")
    dw_dz = _matmul_acc(h_t, dp_dz, "dw_in_dz")
    dw_gm = _matmul_acc(h_t, dp_g, "dw_in_gmlp")
    dw_xa = _matmul_acc(h_t, dp_x, "dw_in_xa")
    dw_ab = _matmul_acc(h_t, dp_ab, "dw_in_ab")
    segs = [(o_g, dw_gm), (o_dn, dw_qkv), (o_dn + 3 * DN_W, dw_dz), (o_ab, dw_ab[:, :2 * DN_H]), (o_xa, dw_xa)]
    shards = []
    for sh in range(N_DEV):
        lo, hi = sh * shard_w, (sh + 1) * shard_w
        parts = [arr[:, max(lo, off) - off:min(hi, off + arr.shape[1]) - off] for off, arr in segs
                 if off < hi and off + arr.shape[1] > lo]
        shards.append(jnp.concatenate(parts, axis=1).astype(BF16))
    send_in = jnp.stack(shards)

    small_shapes = [gmlp_ln_g.shape, gmlp_ln_b.shape, gmlp_ws.shape, gmlp_bs.shape, dn_a_log.shape,
                    dn_dt_bias.shape, dn_norm_g.shape, mem_norm_g.shape, final_g.shape, (DN_K, 3 * DN_W)]
    small_g = _pack([d_lng, d_lnb, d_ws, d_bst.T, d_alog[:, :DN_H], d_dt[:, :DN_H], d_dn_g, d_mem_g, d_fin_g, d_conv])
    zc = jnp.zeros((DN_K, 3 * DN_W), F32)
    small_w = _pack([gmlp_ln_g, gmlp_ln_b, gmlp_ws, gmlp_bs, dn_a_log, dn_dt_bias, dn_norm_g, mem_norm_g, final_g, zc])
    small_m = _pack([m_gmlp_ln_g, m_gmlp_ln_b, m_gmlp_ws, m_gmlp_bs, m_dn_a_log, m_dn_dt_bias, m_dn_norm_g,
                     m_mem_norm_g, m_final_g, zc])
    small_v = _pack([v_gmlp_ln_g, v_gmlp_ln_b, v_gmlp_ws, v_gmlp_bs, v_dn_a_log, v_dn_dt_bias, v_dn_norm_g,
                     v_mem_norm_g, v_final_g, zc + 1.0])

    send_out = d_w_out.reshape(N_DEV, MIX_W // N_DEV, d).astype(BF16)
    send_kv = d_w_kv.reshape(N_DEV, d // N_DEV, 2 * XA_W).astype(BF16)
    sends = [send_in, send_out, send_kv]
    all_small, got = _swap_halves(small_g, sends, "swap_halves")
    core = lax.axis_index("c").astype(jnp.int32).reshape(1)
    chip_sums = [_pair_sum(core, sends[i], got[i], "pair_sum_%d" % i) for i in range(3)]
    dh, (r_in, r_out, r_kv) = _dh(terms, chip_sums)
    grad_x, d_ln_g = _rms_bwd(xs, dh, dx2, ln_g2)
    (all_ln_g,) = _gather_two_level([_pack([d_ln_g])], "gather_ln_g")

    g_w_in, dl_w_in, nm_w_in, nv_w_in = _adam(r_in, w_in[0], m_w_in[0], v_w_in[0], "adam_w_in")
    g_w_out, dl_w_out, nm_w_out, nv_w_out = _adam(r_out, w_out[0], m_w_out[0], v_w_out[0], "adam_w_out")
    g_w_kv, dl_w_kv, nm_w_kv, nv_w_kv = _adam(r_kv, w_mem_kv[0], m_w_mem_kv[0], v_w_mem_kv[0], "adam_w_kv")
    sm = [_unpack(t, small_shapes) for t in _adam(all_small, small_w, small_m, small_v, "adam_small")]
    ln_res = [_unpack(t, [ln_g.shape])[0]
              for t in _adam(all_ln_g, _pack([ln_g]), _pack([m_ln_g]), _pack([v_ln_g]), "adam_ln_g")]

    conv_parts = lax.dynamic_slice(all_small, (0, all_small.shape[1] - (DN_K * 3 * DN_W) // LANE, 0),
                                   (N_DEV, (DN_K * 3 * DN_W) // LANE, LANE)).reshape(N_DEV, DN_K, 3 * DN_W)
    cshard = conv_w.shape[2]
    conv_parts = lax.dynamic_slice(conv_parts, (0, 0, me * cshard), (N_DEV, DN_K, cshard))
    cpad = ((0, 0), (0, HALO - DN_K), (0, 0))
    conv_res = _adam(jnp.pad(conv_parts, cpad), jnp.pad(conv_w[0], cpad[1:]), jnp.pad(m_conv_w[0], cpad[1:]),
                     jnp.pad(v_conv_w[0], cpad[1:], constant_values=1.0), "adam_conv")
    g_conv_s, dl_conv, nm_conv, nv_conv = [t[:DN_K][None] for t in conv_res]

    def group(idx, big_in, big_conv, big_kv, big_out):
        names = sm[idx]
        return [ln_res[idx], big_in[None], names[0], names[1], names[2], names[3], big_conv, names[4], names[5], names[6],
                names[7], big_kv[None], big_out[None], names[8]]

    grads = group(0, g_w_in, g_conv_s, g_w_kv, g_w_out)
    deltas = group(1, dl_w_in, dl_conv, dl_w_kv, dl_w_out)
    new_m = group(2, nm_w_in, nm_conv, nm_w_kv, nm_w_out)
    new_v = group(3, nv_w_in, nv_conv, nv_w_kv, nv_w_out)
    return (loss, grad_x[None], *grads, *deltas, *new_m, *new_v)
```

```python
import functools

import jax
import jax.numpy as jnp
from jax import lax
from jax.experimental import pallas as pl
from jax.experimental.pallas import tpu as pltpu

F32 = jnp.float32
BF16 = jnp.bfloat16
HIGHEST = lax.Precision.HIGHEST
MESH_ID = pl.DeviceIdType.MESH

N_DEV = 8
EPS = 1e-6
GMLP_W = 512
GMLP_G = 4
GMLP_T = 128
DN_W = 1024
DN_H = 8
HEAD = 128
DN_K = 4
CH = 64
XA_W = 512
XA_H = 4
LANE = 128
HALO = 8
MAIN_W = 4 * DN_W + 3 * GMLP_W + 2 * XA_W
MIX_W = DN_W + GMLP_W + XA_W
VMEM_LIMIT = 56 * 1024 * 1024

ADAM_LR = 0.001
ADAM_B1 = 0.9
ADAM_B2 = 0.999
ADAM_EPS = 1e-08
ADAM_WD = 0.01
ADAM_STEP = 10


def _sds(shape, dtype=F32):
    return jax.ShapeDtypeStruct(tuple(shape), dtype)


def _params(sem=None):
    if sem is None:
        return pltpu.CompilerParams(vmem_limit_bytes=VMEM_LIMIT)
    return pltpu.CompilerParams(dimension_semantics=tuple(sem), vmem_limit_bytes=VMEM_LIMIT)


def _tile(n, prefs):
    for p in prefs:
        if n % p == 0:
            return p
    return n


def _mm(a, b):
    return jnp.dot(a.astype(BF16), b.astype(BF16), preferred_element_type=F32)


def _mm_nt(a, b):
    return lax.dot_general(a.astype(BF16), b.astype(BF16), (((1,), (1,)), ((), ())), preferred_element_type=F32)


def _mm_tn(a, b):
    return lax.dot_general(a.astype(BF16), b.astype(BF16), (((0,), (0,)), ((), ())), preferred_element_type=F32)


def _mm_hi(a, b):
    return jnp.dot(a, b, precision=HIGHEST, preferred_element_type=F32)


def _mm_3x(a, b):
    return jnp.dot(a, b, precision=lax.Precision.HIGH, preferred_element_type=F32)


_GELU_C = 0.7978845608028654
_GELU_A = 0.044715


def _gelu(x):
    return 0.5 * x * (1.0 + jnp.tanh(_GELU_C * (x + _GELU_A * x * x * x)))


def _gelu_grad(x):
    t = jnp.tanh(_GELU_C * (x + _GELU_A * x * x * x))
    return 0.5 * (1.0 + t) + 0.5 * x * (1.0 - t * t) * _GELU_C * (1.0 + 3.0 * _GELU_A * x * x)


def _silu(x):
    return x * jax.nn.sigmoid(x)


def _silu_grad(x):
    s = jax.nn.sigmoid(x)
    return s * (1.0 + x * (1.0 - s))


def _rowsum(x):
    return jnp.sum(x, axis=-1, keepdims=True)


def _colsum(x):
    return jnp.sum(x, axis=0, keepdims=True)


def _iota2(shape, dim):
    return lax.broadcasted_iota(jnp.int32, shape, dim)


def _chunk_tri(tm, upper):
    r = _iota2((tm, tm), 0)
    c = _iota2((tm, tm), 1)
    same = lax.shift_right_logical(r, 6) == lax.shift_right_logical(c, 6)
    tri = (r <= c) if upper else (r >= c)
    return jnp.where(same & tri, 1.0, 0.0).astype(F32)


N_CHIP = 4


def _mesh_place():
    x, y, c = lax.axis_index("x"), lax.axis_index("y"), lax.axis_index("c")
    chips = [(1 - x, y), (x, 1 - y), (1 - x, 1 - y)]
    return x, y, c, (x, y, 1 - c), chips


class _Gather:
    def __init__(self, ins, outs, send_sems, recv_sems, loc_sems):
        self.ins, self.outs, self.send_sems, self.recv_sems, self.loc_sems = ins, outs, send_sems, recv_sems, loc_sems
        self.x, self.y, self.c, self.sib, self.chips = _mesh_place()
        self.me = (self.x, self.y, self.c)

    def copy(self, a, k, block, to, src=None):
        slot = self.outs[a].at[4 * block[0] + 2 * block[1] + block[2]]
        return pltpu.make_async_remote_copy(
            src_ref=slot if src is None else src, dst_ref=slot, send_sem=self.send_sems.at[a, k],
            recv_sem=self.recv_sems.at[a, k], device_id=to, device_id_type=MESH_ID)

    def own(self, a):
        return pltpu.make_async_copy(self.ins[a], self.outs[a].at[4 * self.x + 2 * self.y + self.c], self.loc_sems.at[a])

    def first(self, a):
        return [self.copy(a, 0, self.me, self.sib, src=self.ins[a])] + [
            self.copy(a, 1 + j, self.me, (*chip, self.c), src=self.ins[a]) for j, chip in enumerate(self.chips)]

    def passed(self, a, j):
        return self.copy(a, 4 + j, (*self.chips[j], self.c), self.sib)

    def start(self):
        for a in range(len(self.ins)):
            self.own(a).start()
            for cp in self.first(a):
                cp.start()

    def finish(self):
        n = len(self.ins)
        for a in range(n):
            for j, chip in enumerate(self.chips):
                self.copy(a, 1 + j, (*chip, self.c), self.me).wait_recv()
                self.passed(a, j).start()
        for a in range(n):
            self.copy(a, 0, self.sib, self.me).wait_recv()
            for j, chip in enumerate(self.chips):
                self.copy(a, 4 + j, (*chip, 1 - self.c), self.me).wait_recv()
        for a in range(n):
            for cp in self.first(a) + [self.passed(a, j) for j in range(N_CHIP - 1)]:
                cp.wait_send()
            self.own(a).wait()

    @staticmethod
    def sems(n):
        return [pltpu.SemaphoreType.DMA((n, N_DEV - 1)), pltpu.SemaphoreType.DMA((n, N_DEV - 1)),
                pltpu.SemaphoreType.DMA((n,))]


def _gather_two_level(arrs, name):
    n = len(arrs)

    def body(*refs):
        g = _Gather(refs[:n], refs[n:2 * n], *refs[2 * n:])
        g.start()
        g.finish()

    any_spec = pl.BlockSpec(memory_space=pl.ANY)
    return pl.pallas_call(
        body, name=name, out_shape=[_sds((N_DEV,) + a.shape, a.dtype) for a in arrs],
        in_specs=[any_spec] * n, out_specs=[any_spec] * n, scratch_shapes=_Gather.sems(n),
        compiler_params=pltpu.CompilerParams(has_side_effects=True),
    )(*arrs)


def _swap_halves(small, grads, name):
    n = len(grads)

    def body(*refs):
        small_ref = refs[0]
        ins = refs[1:1 + n]
        small_out = refs[1 + n]
        got = refs[2 + n:2 + 2 * n]
        s_send, s_recv, g_send, g_recv, loc_sem = refs[2 + 2 * n:]
        x, y, c, sib, _ = _mesh_place()
        me = 4 * x + 2 * y + c
        sends, recvs = [], []
        for j in range(1, N_DEV):
            px = 1 - x if (j >> 2) & 1 else x
            py = 1 - y if (j >> 1) & 1 else y
            pc = 1 - c if j & 1 else c
            cp = pltpu.make_async_remote_copy(
                src_ref=small_ref, dst_ref=small_out.at[me], send_sem=s_send.at[j - 1], recv_sem=s_recv.at[j - 1],
                device_id=(px, py, pc), device_id_type=MESH_ID)
            cp.start()
            sends.append(cp)
            recvs.append(pltpu.make_async_remote_copy(
                src_ref=small_ref, dst_ref=small_out.at[4 * px + 2 * py + pc], send_sem=s_send.at[j - 1],
                recv_sem=s_recv.at[j - 1], device_id=(px, py, pc), device_id_type=MESH_ID))
        own = pltpu.make_async_copy(small_ref, small_out.at[me], loc_sem)
        own.start()
        for a in range(n):
            for chip in range(N_CHIP):
                cp = pltpu.make_async_remote_copy(
                    src_ref=ins[a].at[2 * chip + 1 - c], dst_ref=got[a].at[chip], send_sem=g_send.at[a, chip],
                    recv_sem=g_recv.at[a, chip], device_id=sib, device_id_type=MESH_ID)
                cp.start()
                sends.append(cp)
                recvs.append(cp)
        for cp in sends:
            cp.wait_send()
        for cp in recvs:
            cp.wait_recv()
        own.wait()

    half = [_sds((N_CHIP,) + g.shape[1:], g.dtype) for g in grads]
    any_spec = pl.BlockSpec(memory_space=pl.ANY)
    res = pl.pallas_call(
        body, name=name, out_shape=[_sds((N_DEV,) + small.shape, small.dtype)] + half,
        in_specs=[any_spec] * (1 + n), out_specs=[any_spec] * (1 + n),
        scratch_shapes=[pltpu.SemaphoreType.DMA((N_DEV - 1,)), pltpu.SemaphoreType.DMA((N_DEV - 1,)),
                        pltpu.SemaphoreType.DMA((n, N_CHIP)), pltpu.SemaphoreType.DMA((n, N_CHIP)),
                        pltpu.SemaphoreType.DMA],
        compiler_params=pltpu.CompilerParams(has_side_effects=True),
    )(small, *grads)
    return res[0], res[1:]


def _pair_sum(core, mine, got, name):
    nc, r, c = got.shape
    tr = _tile(r, (256, 128, 64, 32, 16))

    def body(core_ref, a_ref, b_ref, o_ref):
        o_ref[...] = (a_ref[...].astype(F32) + b_ref[...].astype(F32)).astype(BF16)

    return pl.pallas_call(
        body, name=name, out_shape=_sds(got.shape, BF16),
        grid_spec=pltpu.PrefetchScalarGridSpec(
            num_scalar_prefetch=1, grid=(nc, r // tr),
            in_specs=[pl.BlockSpec((1, tr, c), lambda i, j, core_ref: (2 * i + core_ref[0], j, 0)),
                      pl.BlockSpec((1, tr, c), lambda i, j, core_ref: (i, j, 0))],
            out_specs=pl.BlockSpec((1, tr, c), lambda i, j, core_ref: (i, j, 0))),
        compiler_params=_params(("parallel", "parallel")),
    )(core, mine, got)


class _ChipExchange:
    def __init__(self, ins, outs, send_sems, recv_sems, loc_sems):
        self.ins, self.outs, self.send_sems, self.recv_sems, self.loc_sems = ins, outs, send_sems, recv_sems, loc_sems
        self.x, self.y, self.c, _, self.chips = _mesh_place()
        self.mine = 2 * self.x + self.y

    def own(self, a):
        return pltpu.make_async_copy(self.ins[a].at[self.mine], self.outs[a].at[self.mine], self.loc_sems.at[a])

    def copy(self, a, j, lands_in):
        chip = self.chips[j]
        return pltpu.make_async_remote_copy(
            src_ref=self.ins[a].at[2 * chip[0] + chip[1]], dst_ref=self.outs[a].at[lands_in],
            send_sem=self.send_sems.at[a, j], recv_sem=self.recv_sems.at[a, j], device_id=(*chip, self.c),
            device_id_type=MESH_ID)

    def start(self):
        for a in range(len(self.ins)):
            self.own(a).start()
            for j in range(N_CHIP - 1):
                self.copy(a, j, self.mine).start()

    def finish(self):
        for a in range(len(self.ins)):
            for j, chip in enumerate(self.chips):
                self.copy(a, j, self.mine).wait_send()
                self.copy(a, j, 2 * chip[0] + chip[1]).wait_recv()
            self.own(a).wait()

    @staticmethod
    def sems(n):
        return [pltpu.SemaphoreType.DMA((n, N_CHIP - 1)), pltpu.SemaphoreType.DMA((n, N_CHIP - 1)),
                pltpu.SemaphoreType.DMA((n,))]


def _inproj(x, ln_g, w_main, w_ab, late):
    s, d = x.shape
    n = w_main.shape[1]
    tm = _tile(s, (512, 256, 128))
    tn = _tile(n, (1664, 512, 128))
    nl = len(late)
    ni, nj = s // tm, n // tn

    def body(*refs):
        x_ref, g_ref, w_ref, wab_ref = refs[:4]
        proj_ref, ab_ref, ht_ref = refs[4 + nl:7 + nl]
        hs = refs[7 + 2 * nl]
        gather = _Gather(refs[4:4 + nl], refs[7 + nl:7 + 2 * nl], *refs[8 + 2 * nl:])
        step = pl.program_id(0) * nj + pl.program_id(1)

        @pl.when(step == 0)
        def _():
            gather.start()

        @pl.when(pl.program_id(1) == 0)
        def _():
            xv = x_ref[...]
            r = lax.rsqrt(jnp.mean(xv * xv, axis=-1, keepdims=True) + EPS)
            hf = xv * r * g_ref[...]
            h = hf.astype(BF16)
            hs[...] = h
            ht_ref[...] = hf.T.astype(BF16)
            ab_ref[...] = jnp.dot(h, wab_ref[...], preferred_element_type=F32)

        proj_ref[...] = jnp.dot(hs[...], w_ref[...], preferred_element_type=F32)

        @pl.when(step == ni * nj - 1)
        def _():
            gather.finish()

    any_spec = pl.BlockSpec(memory_space=pl.ANY)
    res = pl.pallas_call(
        body, name="inproj", grid=(ni, nj),
        in_specs=[pl.BlockSpec((tm, d), lambda i, j: (i, 0)), pl.BlockSpec((1, d), lambda i, j: (0, 0)),
                  pl.BlockSpec((d, tn), lambda i, j: (0, j)), pl.BlockSpec((d, LANE), lambda i, j: (0, 0))]
        + [any_spec] * nl,
        out_specs=[pl.BlockSpec((tm, tn), lambda i, j: (i, j)), pl.BlockSpec((tm, LANE), lambda i, j: (i, 0)),
                   pl.BlockSpec((d, tm), lambda i, j: (0, i))] + [any_spec] * nl,
        out_shape=[_sds((s, n)), _sds((s, LANE)), _sds((d, s), BF16)]
        + [_sds((N_DEV,) + a.shape, a.dtype) for a in late],
        scratch_shapes=[pltpu.VMEM((tm, d), BF16)] + _Gather.sems(nl),
        compiler_params=_params(("arbitrary", "arbitrary")),
    )(x, ln_g, w_main, w_ab, *late)
    return res[0], res[1], res[2], res[3:]


def _matmul_acc(a, b, name):
    m, k = a.shape
    n = b.shape[1]
    tm = _tile(m, (2048, 1024, 512, 256, 128))
    tn = _tile(n, (1024, 512, 256, 128))
    tk = _tile(k, (1024, 512, 256, 128))

    def body(a_ref, b_ref, o_ref):
        @pl.when(pl.program_id(2) == 0)
        def _():
            o_ref[...] = jnp.zeros_like(o_ref)

        o_ref[...] += jnp.dot(a_ref[...], b_ref[...], preferred_element_type=F32)

    return pl.pallas_call(
        body, name=name, grid=(m // tm, n // tn, k // tk),
        in_specs=[pl.BlockSpec((tm, tk), lambda i, j, l: (i, l)), pl.BlockSpec((tk, tn), lambda i, j, l: (l, j))],
        out_specs=pl.BlockSpec((tm, tn), lambda i, j, l: (i, j)),
        out_shape=_sds((m, n)),
        compiler_params=_params(("parallel", "parallel", "arbitrary")),
    )(a, b)


def _matmul_tn(a, b, name):
    k, m = a.shape
    n = b.shape[1]
    tm = _tile(m, (1024, 512, 256, 128))
    tn = _tile(n, (1024, 512, 256, 128))
    tk = _tile(k, (1024, 512, 256, 128))

    def body(a_ref, b_ref, o_ref):
        @pl.when(pl.program_id(2) == 0)
        def _():
            o_ref[...] = jnp.zeros_like(o_ref)

        o_ref[...] += _mm_tn(a_ref[...], b_ref[...])

    return pl.pallas_call(
        body, name=name, grid=(m // tm, n // tn, k // tk),
        in_specs=[pl.BlockSpec((tk, tm), lambda i, j, l: (l, i)), pl.BlockSpec((tk, tn), lambda i, j, l: (l, j))],
        out_specs=pl.BlockSpec((tm, tn), lambda i, j, l: (i, j)),
        out_shape=_sds((m, n)),
        compiler_params=_params(("parallel", "parallel", "arbitrary")),
    )(a, b)


def _dh(terms, chip_sums):
    s = terms[0][0].shape[0]
    d = terms[0][1].shape[0]
    npc = len(terms)
    nx = len(chip_sums)
    tm = _tile(s, (256, 128))
    tn = _tile(d, (1024, 512, 256, 128))
    nj, ni = d // tn, s // tm

    def body(*refs):
        o_ref = refs[2 * npc + nx]
        exch = _ChipExchange(refs[2 * npc:2 * npc + nx], refs[2 * npc + nx + 1:2 * npc + 2 * nx + 1],
                             *refs[2 * npc + 2 * nx + 1:])
        step = pl.program_id(0) * ni + pl.program_id(1)

        @pl.when(step == 0)
        def _():
            exch.start()

        acc = _mm_nt(refs[0][...], refs[npc][...])
        for p in range(1, npc):
            acc += _mm_nt(refs[p][...], refs[npc + p][...])
        o_ref[...] = acc

        @pl.when(step == ni * nj - 1)
        def _():
            exch.finish()

    any_spec = pl.BlockSpec(memory_space=pl.ANY)
    in_specs = [pl.BlockSpec((tm, w), functools.partial(lambda j, i, pb: (i, pb), pb=pb)) for _, _, w, pb, _ in terms]
    in_specs += [pl.BlockSpec((tn, w), functools.partial(lambda j, i, wb: (j, wb), wb=wb)) for _, _, w, _, wb in terms]
    res = pl.pallas_call(
        body, name="dh", grid=(nj, ni), in_specs=in_specs + [any_spec] * nx,
        out_specs=[pl.BlockSpec((tm, tn), lambda j, i: (i, j))] + [any_spec] * nx,
        out_shape=[_sds((s, d))] + [_sds(p.shape, p.dtype) for p in chip_sums],
        scratch_shapes=_ChipExchange.sems(nx),
        compiler_params=_params(("arbitrary", "arbitrary")),
    )(*[t[0] for t in terms], *[t[1] for t in terms], *chip_sums)
    return res[0], res[1:]


def _rms_bwd(x, dh, dx2, ln_g):
    s, d = x.shape
    tm = _tile(s, (256, 128))

    def body(x_ref, dh_ref, dx2_ref, g_ref, gx_ref, dg_ref):
        @pl.when(pl.program_id(0) == 0)
        def _():
            dg_ref[...] = jnp.zeros_like(dg_ref)

        xv = x_ref[...]
        r = lax.rsqrt(jnp.mean(xv * xv, axis=-1, keepdims=True) + EPS)
        xhat = xv * r
        dhv = dh_ref[...]
        dg_ref[...] += _colsum(dhv * xhat)
        dxh = dhv * g_ref[...]
        gx_ref[...] = dx2_ref[...] + r * (dxh - xhat * jnp.mean(dxh * xhat, axis=-1, keepdims=True))

    row = pl.BlockSpec((tm, d), lambda i: (i, 0))
    vec = pl.BlockSpec((1, d), lambda i: (0, 0))
    return pl.pallas_call(
        body, name="rms_bwd", grid=(s // tm,), in_specs=[row, row, row, vec], out_specs=[row, vec],
        out_shape=[_sds((s, d)), _sds((1, d))], compiler_params=_params(("arbitrary",)),
    )(x, dh, dx2, ln_g)


def _final(x, tgt, out_b, out_a, out_c, w_out, final_g):
    s, d = x.shape
    tm = _tile(s, (256, 128))

    def body(x_ref, t_ref, b_ref, a_ref, c_ref, w_ref, g_ref, dx2_ref, dx2b_ref, dm_ref, loss_ref, dg_ref):
        @pl.when(pl.program_id(0) == 0)
        def _():
            loss_ref[...] = jnp.zeros_like(loss_ref)
            dg_ref[...] = jnp.zeros_like(dg_ref)

        x2 = x_ref[...]
        x2 += jnp.dot(b_ref[...], w_ref[0:DN_W, :], preferred_element_type=F32)
        x2 += jnp.dot(a_ref[...], w_ref[DN_W:DN_W + GMLP_W, :], preferred_element_type=F32)
        x2 += jnp.dot(c_ref[...], w_ref[DN_W + GMLP_W:MIX_W, :], preferred_element_type=F32)
        r = lax.rsqrt(jnp.mean(x2 * x2, axis=-1, keepdims=True) + EPS)
        xhat = x2 * r
        g = g_ref[...]
        err = xhat * g - t_ref[...]
        tok = 0.5 * jnp.mean(err * err, axis=-1, keepdims=True)
        loss_ref[...] += jnp.broadcast_to(_colsum(tok), loss_ref.shape)
        dy = err * (1.0 / d)
        dg_ref[...] += _colsum(dy * xhat)
        dxh = dy * g
        dx2 = r * (dxh - xhat * jnp.mean(dxh * xhat, axis=-1, keepdims=True))
        dx2_ref[...] = dx2
        dx2b = dx2.astype(BF16)
        dx2b_ref[...] = dx2b
        dm_ref[...] = _mm_nt(dx2b, w_ref[...])

    row = pl.BlockSpec((tm, d), lambda i: (i, 0))
    vec = pl.BlockSpec((1, d), lambda i: (0, 0))
    return pl.pallas_call(
        body, name="final", grid=(s // tm,),
        in_specs=[row, row, pl.BlockSpec((tm, DN_W), lambda i: (i, 0)), pl.BlockSpec((tm, GMLP_W), lambda i: (i, 0)),
                  pl.BlockSpec((tm, XA_W), lambda i: (i, 0)), pl.BlockSpec((MIX_W, d), lambda i: (0, 0)), vec],
        out_specs=[row, row, pl.BlockSpec((tm, MIX_W), lambda i: (i, 0)), pl.BlockSpec((1, LANE), lambda i: (0, 0)), vec],
        out_shape=[_sds((s, d)), _sds((s, d), BF16), _sds((s, MIX_W)), _sds((1, LANE)), _sds((1, d))],
        compiler_params=_params(("arbitrary",)),
    )(x, tgt, out_b, out_a, out_c, w_out, final_g)


GU_BLK = (4 * DN_W) // GMLP_W


def _gmlp_norm(gv, lng, lnb):
    va = _gelu(gv)
    mu = jnp.mean(va, axis=-1, keepdims=True)
    xc = va - mu
    rstd = lax.rsqrt(jnp.mean(xc * xc, axis=-1, keepdims=True) + EPS)
    vhat = xc * rstd
    return vhat, rstd, vhat * lng + lnb


def _gmlp_fwd(proj, lng, lnb, ws, bs_t):
    s = proj.shape[0]
    tm = _tile(s, (512, 256, 128))

    def body(u_ref, v_ref, z_ref, lng_ref, lnb_ref, ws_ref, bst_ref, o_ref):
        _, _, vn = _gmlp_norm(v_ref[...], lng_ref[...], lnb_ref[...])
        tri = _iota2((GMLP_T, GMLP_T), 0) >= _iota2((GMLP_T, GMLP_T), 1)
        for g in range(GMLP_G):
            cs = slice(g * HEAD, (g + 1) * HEAD)
            w = jnp.where(tri, ws_ref[g], 0.0).astype(BF16)
            b = bst_ref[:, g:g + 1]
            for c in range(tm // GMLP_T):
                rs = slice(c * GMLP_T, (c + 1) * GMLP_T)
                sg = _mm(w, vn[rs, cs]) + b
                o_ref[rs, cs] = (_gelu(u_ref[rs, cs]) * sg * _silu(z_ref[rs, cs])).astype(BF16)

    col = lambda k: pl.BlockSpec((tm, GMLP_W), lambda i: (i, GU_BLK + k))
    vec = pl.BlockSpec((1, GMLP_W), lambda i: (0, 0))
    return pl.pallas_call(
        body, name="gmlp_fwd", grid=(s // tm,),
        in_specs=[col(0), col(1), col(2), vec, vec, pl.BlockSpec((GMLP_G, GMLP_T, GMLP_T), lambda i: (0, 0, 0)),
                  pl.BlockSpec((GMLP_T, GMLP_G), lambda i: (0, 0))],
        out_specs=pl.BlockSpec((tm, GMLP_W), lambda i: (i, 0)), out_shape=_sds((s, GMLP_W), BF16),
        compiler_params=_params(("parallel",)),
    )(proj, proj, proj, lng, lnb, ws, bs_t)


def _gmlp_bwd(proj, dmixed, lng, lnb, ws, bs_t):
    s = proj.shape[0]
    tm = _tile(s, (512, 256, 128))

    def body(u_ref, v_ref, z_ref, d_ref, lng_ref, lnb_ref, ws_ref, bst_ref,
             dp_ref, dws_ref, dbst_ref, dlng_ref, dlnb_ref, dvn):
        @pl.when(pl.program_id(0) == 0)
        def _():
            dws_ref[...] = jnp.zeros_like(dws_ref)
            dbst_ref[...] = jnp.zeros_like(dbst_ref)
            dlng_ref[...] = jnp.zeros_like(dlng_ref)
            dlnb_ref[...] = jnp.zeros_like(dlnb_ref)

        gv = v_ref[...]
        lng_v = lng_ref[...]
        vhat, rstd, vn = _gmlp_norm(gv, lng_v, lnb_ref[...])
        tri = _iota2((GMLP_T, GMLP_T), 0) >= _iota2((GMLP_T, GMLP_T), 1)
        for g in range(GMLP_G):
            cs = slice(g * HEAD, (g + 1) * HEAD)
            w = jnp.where(tri, ws_ref[g], 0.0).astype(BF16)
            b = bst_ref[:, g:g + 1]
            dw_acc = jnp.zeros((GMLP_T, GMLP_T), F32)
            db_acc = jnp.zeros((GMLP_T, 1), F32)
            for c in range(tm // GMLP_T):
                rs = slice(c * GMLP_T, (c + 1) * GMLP_T)
                vn_b = vn[rs, cs]
                sg = _mm(w, vn_b) + b
                gu = u_ref[rs, cs]
                gz = z_ref[rs, cs]
                da = d_ref[rs, cs]
                uact = _gelu(gu)
                sz = _silu(gz)
                ds = da * uact * sz
                dp_ref[rs, cs] = (da * sg * sz * _gelu_grad(gu)).astype(BF16)
                dp_ref[rs, 2 * GMLP_W + g * HEAD:2 * GMLP_W + (g + 1) * HEAD] = (da * uact * sg * _silu_grad(gz)).astype(BF16)
                dw_acc += _mm_nt(ds, vn_b)
                db_acc += _rowsum(ds)
                dvn[rs, cs] = _mm_tn(w, ds)
            dws_ref[g] += jnp.where(tri, dw_acc, 0.0)
            dbst_ref[:, g:g + 1] += db_acc
        dvn_v = dvn[...]
        dlng_ref[...] += _colsum(dvn_v * vhat)
        dlnb_ref[...] += _colsum(dvn_v)
        dvh = dvn_v * lng_v
        dva = rstd * (dvh - jnp.mean(dvh, axis=-1, keepdims=True) - vhat * jnp.mean(dvh * vhat, axis=-1, keepdims=True))
        dp_ref[:, GMLP_W:2 * GMLP_W] = (dva * _gelu_grad(gv)).astype(BF16)

    col = lambda k: pl.BlockSpec((tm, GMLP_W), lambda i: (i, GU_BLK + k))
    vec = pl.BlockSpec((1, GMLP_W), lambda i: (0, 0))
    wsp = pl.BlockSpec((GMLP_G, GMLP_T, GMLP_T), lambda i: (0, 0, 0))
    bsp = pl.BlockSpec((GMLP_T, GMLP_G), lambda i: (0, 0))
    return pl.pallas_call(
        body, name="gmlp_bwd", grid=(s // tm,),
        in_specs=[col(0), col(1), col(2), pl.BlockSpec((tm, GMLP_W), lambda i: (i, DN_W // GMLP_W)), vec, vec, wsp, bsp],
        out_specs=[pl.BlockSpec((tm, 3 * GMLP_W), lambda i: (i, 0)), wsp, bsp, vec, vec],
        out_shape=[_sds((s, 3 * GMLP_W), BF16), _sds((GMLP_G, GMLP_T, GMLP_T)), _sds((GMLP_T, GMLP_G)),
                   _sds((1, GMLP_W)), _sds((1, GMLP_W))],
        scratch_shapes=[pltpu.VMEM((tm, GMLP_W), F32)],
        compiler_params=_params(("arbitrary",)),
    )(proj, proj, proj, dmixed, lng, lnb, ws, bs_t)


CQ_BLK = (4 * DN_W + 3 * GMLP_W) // XA_W


def _memkv_fwd(mem, g, w_kv):
    nm, d = mem.shape

    def body(m_ref, g_ref, w_ref, kv_ref):
        mv = m_ref[...]
        r = lax.rsqrt(jnp.mean(mv * mv, axis=-1, keepdims=True) + EPS)
        kv_ref[...] = _mm(mv * r * g_ref[...], w_ref[...])

    return pl.pallas_call(body, name="memkv_fwd", out_shape=_sds((nm, 2 * XA_W)), compiler_params=_params())(mem, g, w_kv)


def _memkv_bwd(mem, g, w_kv, dkv):
    nm, d = mem.shape

    def body(m_ref, g_ref, w_ref, dkv_ref, dw_ref, dg_ref):
        mv = m_ref[...]
        r = lax.rsqrt(jnp.mean(mv * mv, axis=-1, keepdims=True) + EPS)
        xhat = mv * r
        dkv_v = dkv_ref[...]
        dw_ref[...] = _mm_tn(xhat * g_ref[...], dkv_v)
        dg_ref[...] = _colsum(_mm_nt(dkv_v, w_ref[...]) * xhat)

    return pl.pallas_call(body, name="memkv_bwd", out_shape=[_sds((d, 2 * XA_W)), _sds((1, d))],
                          compiler_params=_params())(mem, g, w_kv, dkv)


def _xattn_probs(q, mk):
    sc = _mm_nt(q, mk) * (HEAD ** -0.5)
    e = jnp.exp(sc - jnp.max(sc, axis=-1, keepdims=True))
    return e / _rowsum(e)


def _xattn_fwd(proj, mkv):
    s = proj.shape[0]
    nm = mkv.shape[0]
    tm = _tile(s, (512, 256, 128))

    def body(q_ref, z_ref, kv_ref, o_ref):
        for h in range(XA_H):
            cs = slice(h * HEAD, (h + 1) * HEAD)
            p = _xattn_probs(q_ref[:, cs], kv_ref[:, cs])
            ctx = _mm(p, kv_ref[:, XA_W + h * HEAD:XA_W + (h + 1) * HEAD])
            o_ref[:, cs] = (ctx * _silu(z_ref[:, cs])).astype(BF16)

    col = lambda k: pl.BlockSpec((tm, XA_W), lambda i: (i, CQ_BLK + k))
    return pl.pallas_call(
        body, name="xattn_fwd", grid=(s // tm,),
        in_specs=[col(0), col(1), pl.BlockSpec((nm, 2 * XA_W), lambda i: (0, 0))],
        out_specs=pl.BlockSpec((tm, XA_W), lambda i: (i, 0)), out_shape=_sds((s, XA_W), BF16),
        compiler_params=_params(("parallel",)),
    )(proj, proj, mkv)


def _xattn_bwd(proj, dmixed, mkv):
    s = proj.shape[0]
    nm = mkv.shape[0]
    tm = _tile(s, (512, 256, 128))

    def body(q_ref, z_ref, d_ref, kv_ref, dp_ref, dkv_ref):
        @pl.when(pl.program_id(0) == 0)
        def _():
            dkv_ref[...] = jnp.zeros_like(dkv_ref)

        for h in range(XA_H):
            cs = slice(h * HEAD, (h + 1) * HEAD)
            vs = slice(XA_W + h * HEAD, XA_W + (h + 1) * HEAD)
            q = q_ref[:, cs]
            z = z_ref[:, cs]
            mk = kv_ref[:, cs]
            mv = kv_ref[:, vs]
            p = _xattn_probs(q, mk)
            ctx = _mm(p, mv)
            dc = d_ref[:, cs]
            dctx = dc * _silu(z)
            dp_ref[:, vs] = (dc * ctx * _silu_grad(z)).astype(BF16)
            dp = _mm_nt(dctx, mv)
            dkv_ref[:, vs] += _mm_tn(p, dctx)
            ds = p * (dp - _rowsum(dp * p)) * (HEAD ** -0.5)
            dp_ref[:, cs] = _mm(ds, mk).astype(BF16)
            dkv_ref[:, cs] += _mm_tn(ds, q)

    col = lambda k: pl.BlockSpec((tm, XA_W), lambda i: (i, CQ_BLK + k))
    kvs = pl.BlockSpec((nm, 2 * XA_W), lambda i: (0, 0))
    return pl.pallas_call(
        body, name="xattn_bwd", grid=(s // tm,),
        in_specs=[col(0), col(1), pl.BlockSpec((tm, XA_W), lambda i: (i, (DN_W + GMLP_W) // XA_W)), kvs],
        out_specs=[pl.BlockSpec((tm, 2 * XA_W), lambda i: (i, 0)), kvs],
        out_shape=[_sds((s, 2 * XA_W), BF16), _sds((nm, 2 * XA_W))],
        compiler_params=_params(("arbitrary",)),
    )(proj, proj, dmixed, mkv)


def _softplus(x):
    return jnp.maximum(x, 0.0) + jnp.log1p(jnp.exp(-jnp.abs(x)))


def _dn_pre(proj, ab, conv_w, alog_row, dt_row):
    s = proj.shape[0]
    tm = _tile(s, (256, 128))
    w3 = 3 * DN_W

    def body(x_ref, halo_ref, ab_ref, cw_ref, al_ref, dt_ref, q_ref, k_ref, v_ref, gb_ref, gbt_ref, yc_ref):
        i = pl.program_id(0)
        xv = x_ref[...]
        cat = jnp.concatenate([jnp.where(i > 0, halo_ref[...], 0.0), xv[0:HALO]], axis=0)
        yc = cw_ref[DN_K - 1:DN_K, :] * xv
        top = cw_ref[DN_K - 1:DN_K, :] * xv[0:HALO]
        for t in range(DN_K - 1):
            back = DN_K - 1 - t
            yc += cw_ref[t:t + 1, :] * pltpu.roll(xv, back, 0)
            top += cw_ref[t:t + 1, :] * pltpu.roll(cat, back, 0)[HALO:2 * HALO]
        yc = jnp.concatenate([top, yc[HALO:tm]], axis=0)
        yc_ref[...] = yc
        act = _silu(yc)
        for h in range(DN_H):
            cs = slice(h * HEAD, (h + 1) * HEAD)
            qa = act[:, cs]
            q_ref[:, cs] = qa * (lax.rsqrt(_rowsum(qa * qa) + EPS) * (HEAD ** -0.5))
            ka = act[:, DN_W + h * HEAD:DN_W + (h + 1) * HEAD]
            k_ref[:, cs] = ka * lax.rsqrt(_rowsum(ka * ka) + EPS)
        v_ref[...] = act[:, 2 * DN_W:w3]
        abv = ab_ref[...]
        lane = _iota2((tm, LANE), 1)
        g = jnp.where(lane < DN_H, -jnp.exp(al_ref[...]) * _softplus(abv + dt_ref[...]), 0.0)
        gc = _mm_hi(_chunk_tri(tm, False), g)
        gbv = jnp.where(lane < DN_H, gc, jnp.where(lane < 2 * DN_H, jax.nn.sigmoid(abv), 0.0))
        gb_ref[...] = gbv
        for c in range(tm // CH):
            gbt_ref[c] = gbv[c * CH:(c + 1) * CH, :].T[0:2 * DN_H, :]

    hb = tm // HALO
    row = lambda w: pl.BlockSpec((tm, w), lambda i: (i, 0))
    vec = pl.BlockSpec((1, LANE), lambda i: (0, 0))
    return pl.pallas_call(
        body, name="dn_pre", grid=(s // tm,),
        in_specs=[row(w3), pl.BlockSpec((HALO, w3), lambda i: (jnp.maximum(i * hb - 1, 0), 0)), row(LANE),
                  pl.BlockSpec((DN_K, w3), lambda i: (0, 0)), vec, vec],
        out_specs=[row(DN_W), row(DN_W), row(DN_W), row(LANE), pl.BlockSpec((tm // CH, 2 * DN_H, CH), lambda i: (i, 0, 0)),
                   row(w3)],
        out_shape=[_sds((s, DN_W)), _sds((s, DN_W)), _sds((s, DN_W)), _sds((s, LANE)), _sds((s // CH, 2 * DN_H, CH)),
                   _sds((s, w3))],
        compiler_params=_params(("parallel",)),
    )(proj, proj, ab, conv_w, alog_row, dt_row)


HEADS = tuple(range(DN_H))


def _hcols(h):
    return slice(h * HEAD, (h + 1) * HEAD)


def _chunk_scalings(k, v, gbv, gbt, h):
    gc = jnp.broadcast_to(gbv[:, h:h + 1], (CH, HEAD))
    beta = jnp.broadcast_to(gbv[:, DN_H + h:DN_H + h + 1], (CH, HEAD))
    gr = gbt[h:h + 1, :]
    ii = _iota2((CH, CH), 0)
    jj = _iota2((CH, CH), 1)
    dec = jnp.exp(jnp.where(ii >= jj, gc[:, 0:CH] - gr, -1e30))
    eg = jnp.exp(gc)
    gl = gr[:, CH - 1:CH]
    kb = k * beta
    return dict(beta=beta, dec=dec, eg=eg, gl=gl, ekd=jnp.exp(gl - gc), kb=kb, vb=v * beta, kbe=kb * eg)


def _chunk_scores(m, q, k):
    kq = _mm_nt(jnp.concatenate([m["kb"], q], axis=0), k)
    strict = _iota2((CH, CH), 0) > _iota2((CH, CH), 1)
    return jnp.where(strict, kq[0:CH] * m["dec"], 0.0), kq[CH:2 * CH] * m["dec"]


def _dn_local(q, k, v, gb, gbt):
    s = q.shape[0]
    cpb = 4 if (s // CH) % 4 == 0 else 1
    tb = cpb * CH
    nblk = s // tb

    def body(q_ref, k_ref, v_ref, gb_ref, gbt_ref, u_ref, w_ref, qg_ref, kd_ref, t_ref, ai_ref, egl_ref):
        def chunk(c, carry):
            r0 = pl.multiple_of(c * CH, CH)
            rows = pl.ds(r0, CH)
            gbv = gb_ref[rows, :]
            gbt_v = gbt_ref[c]
            qs = [q_ref[rows, _hcols(h)] for h in HEADS]
            ks = [k_ref[rows, _hcols(h)] for h in HEADS]
            ms = [_chunk_scalings(ks[h], v_ref[rows, _hcols(h)], gbv, gbt_v, h) for h in HEADS]
            for h in HEADS:
                qg_ref[rows, _hcols(h)] = (qs[h] * ms[h]["eg"]).astype(BF16)
                kd_ref[rows, _hcols(h)] = (ks[h] * ms[h]["ekd"]).astype(BF16)
                egl_ref[c, h:h + 1, :] = jnp.broadcast_to(jnp.exp(ms[h]["gl"]), (1, LANE))
            sc = [_chunk_scores(ms[h], qs[h], ks[h]) for h in HEADS]
            for h in HEADS:
                ai_ref[h, rows, :] = sc[h][1]
            eye = jnp.where(_iota2((CH, CH), 0) == _iota2((CH, CH), 1), 1.0, 0.0).astype(F32)
            ts = [eye - sc[h][0] for h in HEADS]
            ps = [_mm_3x(sc[h][0], sc[h][0]) for h in HEADS]
            ts = [ts[h] + _mm_3x(ts[h], ps[h]) for h in HEADS]
            for _ in range(4):
                ps = [_mm(ps[h], ps[h]) for h in HEADS]
                ts = [ts[h] + _mm(ts[h], ps[h]) for h in HEADS]
            for h in HEADS:
                t_ref[h, rows, :] = ts[h]
                uw = _mm(ts[h], jnp.concatenate([ms[h]["vb"], ms[h]["kbe"]], axis=1))
                u_ref[rows, _hcols(h)] = uw[:, 0:HEAD]
                w_ref[rows, _hcols(h)] = uw[:, HEAD:2 * HEAD].astype(BF16)
            return carry

        lax.fori_loop(0, cpb, chunk, 0)

    row = pl.BlockSpec((tb, DN_W), lambda i: (i, 0))
    sq = pl.BlockSpec((DN_H, tb, CH), lambda i: (0, i, 0))
    return pl.pallas_call(
        body, name="dn_local", grid=(nblk,),
        in_specs=[row, row, row, pl.BlockSpec((tb, LANE), lambda i: (i, 0)),
                  pl.BlockSpec((cpb, 2 * DN_H, CH), lambda i: (i, 0, 0))],
        out_specs=[row, row, row, row, sq, sq, pl.BlockSpec((cpb, DN_H, LANE), lambda i: (i, 0, 0))],
        out_shape=[_sds((s, DN_W)), _sds((s, DN_W), BF16), _sds((s, DN_W), BF16), _sds((s, DN_W), BF16),
                   _sds((DN_H, s, CH)), _sds((DN_H, s, CH)), _sds((s // CH, DN_H, LANE))],
        compiler_params=_params(("parallel",)),
    )(q, k, v, gb, gbt)


def _scan_cpb(s):
    return 8 if (s // CH) % 8 == 0 else 1


def _dn_scan(u, w, qg, kd, ai, egl, proj, norm_g):
    s = u.shape[0]
    cpb = _scan_cpb(s)
    tb = cpb * CH
    nblk = s // tb

    def body(u_ref, w_ref, qg_ref, kd_ref, ai_ref, egl_ref, z_ref, ng_ref, o_ref, vn_ref, st_ref, ob_ref, state):
        @pl.when(pl.program_id(0) == 0)
        def _():
            state[...] = jnp.zeros_like(state)

        ng = ng_ref[...]

        def chunk(c, carry):
            r0 = pl.multiple_of(c * CH, CH)
            rows = pl.ds(r0, CH)
            sts = [state[h] for h in HEADS]
            stb = [sts[h].astype(BF16) for h in HEADS]
            for h in HEADS:
                st_ref[c, h] = sts[h]
            vns = [u_ref[rows, _hcols(h)] - jnp.dot(w_ref[rows, _hcols(h)], stb[h], preferred_element_type=F32)
                   for h in HEADS]
            vnb = [vns[h].astype(BF16) for h in HEADS]
            for h in HEADS:
                state[h] = sts[h] * egl_ref[c, h:h + 1, :] + _mm_tn(kd_ref[rows, _hcols(h)], vnb[h])
            os_ = [jnp.dot(qg_ref[rows, _hcols(h)], stb[h], preferred_element_type=F32) + _mm(ai_ref[h, rows, :], vnb[h])
                   for h in HEADS]
            for h in HEADS:
                o = os_[h]
                vn_ref[rows, _hcols(h)] = vns[h]
                o_ref[rows, _hcols(h)] = o
                r = lax.rsqrt(jnp.mean(o * o, axis=-1, keepdims=True) + EPS)
                ob_ref[rows, _hcols(h)] = (o * r * ng * _silu(z_ref[rows, _hcols(h)])).astype(BF16)
            return carry

        lax.fori_loop(0, cpb, chunk, 0)

    row = pl.BlockSpec((tb, DN_W), lambda i: (i, 0))
    return pl.pallas_call(
        body, name="dn_scan", grid=(nblk,),
        in_specs=[row, row, row, row, pl.BlockSpec((DN_H, tb, CH), lambda i: (0, i, 0)),
                  pl.BlockSpec((cpb, DN_H, LANE), lambda i: (i, 0, 0)), pl.BlockSpec((tb, DN_W), lambda i: (i, 3)),
                  pl.BlockSpec((1, HEAD), lambda i: (0, 0))],
        out_specs=[row, row, pl.BlockSpec((cpb, DN_H, HEAD, HEAD), lambda i: (i, 0, 0, 0)), row],
        out_shape=[_sds((s, DN_W)), _sds((s, DN_W)), _sds((s // CH, DN_H, HEAD, HEAD)), _sds((s, DN_W), BF16)],
        scratch_shapes=[pltpu.VMEM((DN_H, HEAD, HEAD), F32)],
        compiler_params=_params(("arbitrary",)),
    )(u, w, qg, kd, ai, egl, proj, norm_g)


def _dn_scan_bwd(dmixed, o, proj, norm_g, w, qg, kd, ai, egl):
    s = o.shape[0]
    cpb = _scan_cpb(s)
    tb = cpb * CH
    nblk = s // tb

    def body(dm_ref, o_ref, z_ref, ng_ref, w_ref, qg_ref, kd_ref, ai_ref, egl_ref,
             do_ref, dvn_ref, dst_ref, dz_ref, dng_ref, dstate):
        @pl.when(pl.program_id(0) == 0)
        def _():
            dstate[...] = jnp.zeros_like(dstate)
            dng_ref[...] = jnp.zeros_like(dng_ref)

        ng = ng_ref[...]

        def chunk(cc, carry):
            c = cpb - 1 - cc
            r0 = pl.multiple_of(c * CH, CH)
            rows = pl.ds(r0, CH)
            dng = jnp.zeros((1, HEAD), F32)
            dob = []
            for h in HEADS:
                cs = _hcols(h)
                o = o_ref[rows, cs]
                z = z_ref[rows, cs]
                db = dm_ref[rows, cs]
                r = lax.rsqrt(jnp.mean(o * o, axis=-1, keepdims=True) + EPS)
                ohat = o * r
                dz_ref[rows, cs] = (db * ohat * ng * _silu_grad(z)).astype(BF16)
                dyn = db * _silu(z)
                dng += _colsum(dyn * ohat)
                doh = dyn * ng
                do = r * (doh - ohat * jnp.mean(doh * ohat, axis=-1, keepdims=True))
                do_ref[rows, cs] = do
                dob.append(do.astype(BF16))
            dng_ref[...] += dng
            dsn = [dstate[h] for h in HEADS]
            for h in HEADS:
                dst_ref[c, h] = dsn[h]
            dvn = [_mm_tn(ai_ref[h, rows, :], dob[h])
                   + jnp.dot(kd_ref[rows, _hcols(h)], dsn[h].astype(BF16), preferred_element_type=F32) for h in HEADS]
            part = [_mm_tn(qg_ref[rows, _hcols(h)], dob[h]) + egl_ref[c, h:h + 1, :] * dsn[h] for h in HEADS]
            for h in HEADS:
                dvn_ref[rows, _hcols(h)] = dvn[h]
                dstate[h] = part[h] - _mm_tn(w_ref[rows, _hcols(h)], dvn[h])
            return carry

        lax.fori_loop(0, cpb, chunk, 0)

    rev = lambda i: (nblk - 1 - i, 0)
    row = pl.BlockSpec((tb, DN_W), rev)
    vec = pl.BlockSpec((1, HEAD), lambda i: (0, 0))
    return pl.pallas_call(
        body, name="dn_scan_bwd", grid=(nblk,),
        in_specs=[row, row, pl.BlockSpec((tb, DN_W), lambda i: (nblk - 1 - i, 3)), vec, row, row, row,
                  pl.BlockSpec((DN_H, tb, CH), lambda i: (0, nblk - 1 - i, 0)),
                  pl.BlockSpec((cpb, DN_H, LANE), lambda i: (nblk - 1 - i, 0, 0))],
        out_specs=[row, row, pl.BlockSpec((cpb, DN_H, HEAD, HEAD), lambda i: (nblk - 1 - i, 0, 0, 0)), row, vec],
        out_shape=[_sds((s, DN_W)), _sds((s, DN_W)), _sds((s // CH, DN_H, HEAD, HEAD)), _sds((s, DN_W), BF16),
                   _sds((1, HEAD))],
        scratch_shapes=[pltpu.VMEM((DN_H, HEAD, HEAD), F32)],
        compiler_params=_params(("arbitrary",)),
    )(dmixed, o, proj, norm_g, w, qg, kd, ai, egl)


def _dn_local_bwd(q, k, v, gb, gbt, t, vn, st, dst, do, dvn):
    s = q.shape[0]
    cpb = 4 if (s // CH) % 4 == 0 else 1
    tb = cpb * CH
    nblk = s // tb

    def body(q_ref, k_ref, v_ref, gb_ref, gbt_ref, t_ref, vn_ref, st_ref, dst_ref, do_ref, dvn_ref,
             dq_ref, dk_ref, dv_ref, dgb_ref):
        lane = _iota2((CH, LANE), 1)
        last = _iota2((CH, 1), 0) == CH - 1

        def chunk(c, carry):
            r0 = pl.multiple_of(c * CH, CH)
            rows = pl.ds(r0, CH)
            gbv = gb_ref[rows, :]
            gbt_v = gbt_ref[c]
            strict = _iota2((CH, CH), 0) > _iota2((CH, CH), 1)
            qs = [q_ref[rows, _hcols(h)] for h in HEADS]
            ks = [k_ref[rows, _hcols(h)] for h in HEADS]
            vs = [v_ref[rows, _hcols(h)] for h in HEADS]
            ms = [_chunk_scalings(ks[h], vs[h], gbv, gbt_v, h) for h in HEADS]
            sts = [st_ref[c, h] for h in HEADS]
            dsn = [dst_ref[c, h] for h in HEADS]
            dob = [do_ref[rows, _hcols(h)].astype(BF16) for h in HEADS]
            dvnb = [dvn_ref[rows, _hcols(h)].astype(BF16) for h in HEADS]
            vnb = [vn_ref[rows, _hcols(h)].astype(BF16) for h in HEADS]
            tbf = [t_ref[h, rows, :].astype(BF16) for h in HEADS]
            sc = [_chunk_scores(ms[h], qs[h], ks[h]) for h in HEADS]
            xs_ = [_mm_nt(jnp.concatenate([dob[h], dvnb[h]], axis=0), sts[h]) for h in HEADS]
            dai = [_mm_nt(dob[h], vnb[h]) for h in HEADS]
            dkd = [_mm_nt(vnb[h], dsn[h]) for h in HEADS]
            dqg = [xs_[h][0:CH] for h in HEADS]
            duw = [jnp.concatenate([dvnb[h], (-xs_[h][CH:2 * CH]).astype(BF16)], axis=1) for h in HEADS]
            dt = [_mm_nt(duw[h], jnp.concatenate([ms[h]["vb"], ms[h]["kbe"]], axis=1)) for h in HEADS]
            dvk = [_mm_tn(tbf[h], duw[h]) for h in HEADS]
            tdt = [_mm_tn(tbf[h], dt[h]) for h in HEADS]
            da = [jnp.where(strict, -_mm_nt(tdt[h], tbf[h]), 0.0) for h in HEADS]
            dsc = [jnp.concatenate([da[h] * ms[h]["dec"], dai[h] * ms[h]["dec"]], axis=0) for h in HEADS]
            dkq = [_mm(dsc[h], ks[h]) for h in HEADS]
            dk1 = [_mm_tn(dsc[h], jnp.concatenate([ms[h]["kb"], qs[h]], axis=0)) for h in HEADS]
            dgb = jnp.zeros((CH, LANE), F32)
            for h in HEADS:
                m = ms[h]
                eg, ekd, beta = m["eg"], m["ekd"], m["beta"]
                dvb = dvk[h][:, 0:HEAD]
                dkbe = dvk[h][:, HEAD:2 * HEAD]
                kd = ks[h] * ekd
                dkb = dkq[h][0:CH] + dkbe * eg
                dq_ref[rows, _hcols(h)] = dkq[h][CH:2 * CH] + dqg[h] * eg
                dk_ref[rows, _hcols(h)] = dk1[h] + dkd[h] * ekd + dkb * beta
                dv_ref[rows, _hcols(h)] = dvb * beta
                dkd_kd = dkd[h] * kd
                dgl = jnp.exp(m["gl"]) * _rowsum(_colsum(sts[h] * dsn[h])) + _rowsum(_colsum(dkd_kd))
                mm_ = da[h] * sc[h][0] + dai[h] * sc[h][1]
                dgc = (_rowsum(mm_ - mm_.T) + _rowsum(dqg[h] * qs[h] * eg - dkd_kd + dkbe * m["kbe"])
                       + jnp.where(last, dgl, 0.0))
                dbeta = _rowsum(dkb * ks[h] + dvb * vs[h])
                dgb = jnp.where(lane == h, dgc, jnp.where(lane == DN_H + h, dbeta, dgb))
            dgb_ref[rows, :] = dgb
            return carry

        lax.fori_loop(0, cpb, chunk, 0)

    row = pl.BlockSpec((tb, DN_W), lambda i: (i, 0))
    gbs = pl.BlockSpec((tb, LANE), lambda i: (i, 0))
    sts = pl.BlockSpec((cpb, DN_H, HEAD, HEAD), lambda i: (i, 0, 0, 0))
    return pl.pallas_call(
        body, name="dn_local_bwd", grid=(nblk,),
        in_specs=[row, row, row, gbs, pl.BlockSpec((cpb, 2 * DN_H, CH), lambda i: (i, 0, 0)),
                  pl.BlockSpec((DN_H, tb, CH), lambda i: (0, i, 0)), row, sts, sts, row, row],
        out_specs=[row, row, row, gbs],
        out_shape=[_sds((s, DN_W)), _sds((s, DN_W)), _sds((s, DN_W)), _sds((s, LANE))],
        compiler_params=_params(("parallel",)),
    )(q, k, v, gb, gbt, t, vn, st, dst, do, dvn)


def _dn_pre_bwd(proj, yc_all, ab, conv_w, alog_row, dt_row, dq, dk, dv, dgb):
    s = proj.shape[0]
    tm = _tile(s, (256, 128))
    w3 = 3 * DN_W
    nblk = s // tm

    def body(x_ref, yc_ref, ab_ref, cw_ref, al_ref, dt_ref, dq_ref, dk_ref, dv_ref, dgb_ref,
             dx_ref, dab_ref, dcw_ref, dal_ref, ddt_ref, exd, carry):
        i = pl.program_id(0)

        @pl.when(i == 0)
        def _():
            carry[...] = jnp.zeros_like(carry)
            dcw_ref[...] = jnp.zeros_like(dcw_ref)
            dal_ref[...] = jnp.zeros_like(dal_ref)
            ddt_ref[...] = jnp.zeros_like(ddt_ref)

        yc = yc_ref[...]
        sg = jax.nn.sigmoid(yc)
        act = yc * sg
        dact = sg * (1.0 + yc * (1.0 - sg))
        for h in range(DN_H):
            cs = slice(h * HEAD, (h + 1) * HEAD)
            ks = slice(DN_W + h * HEAD, DN_W + (h + 1) * HEAD)
            qa = act[:, cs]
            rq = lax.rsqrt(_rowsum(qa * qa) + EPS)
            qh = qa * rq
            dqv = dq_ref[:, cs]
            exd[0:tm, cs] = (HEAD ** -0.5) * rq * (dqv - qh * _rowsum(dqv * qh)) * dact[:, cs]
            ka = act[:, ks]
            rk = lax.rsqrt(_rowsum(ka * ka) + EPS)
            kh = ka * rk
            dkv = dk_ref[:, cs]
            exd[0:tm, ks] = rk * (dkv - kh * _rowsum(dkv * kh)) * dact[:, ks]
        exd[0:tm, 2 * DN_W:w3] = dv_ref[...] * dact[:, 2 * DN_W:w3]
        xv = x_ref[...]
        dyc = exd[...]
        cat = jnp.concatenate([dyc[tm - HALO:tm], carry[...]], axis=0)
        dcw_ref[DN_K - 1:DN_K, :] += _colsum(dyc * xv)
        dx = cw_ref[DN_K - 1:DN_K, :] * dyc
        for t in range(DN_K - 1):
            ahead = DN_K - 1 - t
            view = jnp.concatenate([pltpu.roll(dyc, tm - ahead, 0)[0:tm - HALO],
                                    pltpu.roll(cat, 2 * HALO - ahead, 0)[0:HALO]], axis=0)
            dcw_ref[t:t + 1, :] += _colsum(view * xv)
            dx += cw_ref[t:t + 1, :] * view
        dx_ref[...] = dx.astype(BF16)
        carry[...] = dyc[0:HALO]

        lane = _iota2((tm, LANE), 1)
        dgbv = dgb_ref[...]
        dg = _mm_hi(_chunk_tri(tm, True), jnp.where(lane < DN_H, dgbv, 0.0))
        abv = ab_ref[...]
        xa = abv + dt_ref[...]
        nea = -jnp.exp(al_ref[...])
        d_da = jnp.where(lane < DN_H, dg * nea * jax.nn.sigmoid(xa), 0.0)
        dal_ref[...] += _colsum(jnp.where(lane < DN_H, dg * nea * _softplus(xa), 0.0))
        ddt_ref[...] += _colsum(d_da)
        beta = jax.nn.sigmoid(abv)
        d_db = jnp.where((lane >= DN_H) & (lane < 2 * DN_H), dgbv * beta * (1.0 - beta), 0.0)
        dab_ref[...] = (d_da + d_db).astype(BF16)

    rev = lambda i: (nblk - 1 - i, 0)
    row = lambda w: pl.BlockSpec((tm, w), rev)
    vec = pl.BlockSpec((1, LANE), lambda i: (0, 0))
    cws = pl.BlockSpec((DN_K, w3), lambda i: (0, 0))
    return pl.pallas_call(
        body, name="dn_pre_bwd", grid=(nblk,),
        in_specs=[row(w3), row(w3), row(LANE), cws, vec, vec, row(DN_W), row(DN_W), row(DN_W), row(LANE)],
        out_specs=[row(w3), row(LANE), cws, vec, vec],
        out_shape=[_sds((s, w3), BF16), _sds((s, LANE), BF16), _sds((DN_K, w3)), _sds((1, LANE)), _sds((1, LANE))],
        scratch_shapes=[pltpu.VMEM((tm, w3), F32), pltpu.VMEM((HALO, w3), F32)],
        compiler_params=_params(("arbitrary",)),
    )(proj, yc_all, ab, conv_w, alog_row, dt_row, dq, dk, dv, dgb)


def _adam(parts, w, m, v, name):
    r, c = w.shape
    n_parts = parts.shape[0]
    tr = _tile(r, (128, 64, 32, 16, 8))

    def body(p_ref, w_ref, m_ref, v_ref, g_ref, d_ref, nm_ref, nv_ref):
        g = p_ref[0].astype(F32)
        for k in range(1, n_parts):
            g = g + p_ref[k].astype(F32)
        g_ref[...] = g
        mn = ADAM_B1 * m_ref[...] + (1.0 - ADAM_B1) * g
        vn = ADAM_B2 * v_ref[...] + (1.0 - ADAM_B2) * (g * g)
        m_hat = mn / (1.0 - ADAM_B1 ** ADAM_STEP)
        v_hat = vn / (1.0 - ADAM_B2 ** ADAM_STEP)
        d_ref[...] = -ADAM_LR * (m_hat / (jnp.sqrt(v_hat) + ADAM_EPS) + ADAM_WD * w_ref[...])
        nm_ref[...] = mn
        nv_ref[...] = vn

    blk = pl.BlockSpec((tr, c), lambda i: (i, 0))
    return pl.pallas_call(
        body, name=name, grid=(r // tr,),
        in_specs=[pl.BlockSpec((n_parts, tr, c), lambda i: (0, i, 0)), blk, blk, blk],
        out_specs=[blk, blk, blk, blk], out_shape=[_sds((r, c))] * 4,
        compiler_params=_params(("parallel",)),
    )(parts, w, m, v)


_PACK_ROWS = 8


def _pack(vals):
    tiles = []
    for a in vals:
        flat = a.reshape(-1).astype(F32)
        unit = _PACK_ROWS * LANE
        n = -(-flat.shape[0] // unit) * unit
        tiles.append(jnp.pad(flat, (0, n - flat.shape[0])).reshape(n // LANE, LANE))
    return jnp.concatenate(tiles, axis=0)


def _unpack(packed, shapes):
    out = []
    r0 = 0
    for shp in shapes:
        size = 1
        for dim in shp:
            size *= dim
        unit = _PACK_ROWS * LANE
        rows = -(-size // unit) * _PACK_ROWS
        out.append(packed[r0:r0 + rows].reshape(-1)[:size].reshape(shp))
        r0 += rows
    return out


def _lane_row(vec8):
    return jnp.pad(vec8.reshape(1, -1).astype(F32), ((0, 0), (0, LANE - vec8.size)))


def kernel(x, mem, ln_g, w_in, gmlp_ln_g, gmlp_ln_b, gmlp_ws, gmlp_bs, conv_w, dn_a_log, dn_dt_bias, dn_norm_g, mem_norm_g, w_mem_kv, w_out, final_g, loss_target, m_ln_g, m_w_in, m_gmlp_ln_g, m_gmlp_ln_b, m_gmlp_ws, m_gmlp_bs, m_conv_w, m_dn_a_log, m_dn_dt_bias, m_dn_norm_g, m_mem_norm_g, m_w_mem_kv, m_w_out, m_final_g, v_ln_g, v_w_in, v_gmlp_ln_g, v_gmlp_ln_b, v_gmlp_ws, v_gmlp_bs, v_conv_w, v_dn_a_log, v_dn_dt_bias, v_dn_norm_g, v_mem_norm_g, v_w_mem_kv, v_w_out, v_final_g):
    xs = x[0]
    mems = mem[0]
    tgt = loss_target[0]
    s, d = xs.shape
    shard_w = w_in.shape[2]
    in_w = N_DEV * shard_w
    me = 4 * lax.axis_index("x") + 2 * lax.axis_index("y") + lax.axis_index("c")

    (g_in,) = _gather_two_level([w_in[0].astype(BF16)], "gather_w_in")
    o_g, o_dn, o_ab = 0, 3 * GMLP_W, 3 * GMLP_W + 4 * DN_W
    o_xa = o_ab + 2 * DN_H

    def shard_cols(lo, hi):
        out = []
        while lo < hi:
            sh = lo // shard_w
            end = min(hi, (sh + 1) * shard_w)
            out.append(g_in[sh][:, lo - sh * shard_w:end - sh * shard_w])
            lo = end
        return out

    w_main = jnp.concatenate(shard_cols(o_dn, o_ab) + shard_cols(o_g, o_dn) + shard_cols(o_xa, in_w), axis=1)
    w_ab = jnp.pad(jnp.concatenate(shard_cols(o_ab, o_xa), axis=1), ((0, 0), (0, LANE - 2 * DN_H)))

    ln_g2 = ln_g.reshape(1, d)
    lng2 = gmlp_ln_g.reshape(1, GMLP_W)
    lnb2 = gmlp_ln_b.reshape(1, GMLP_W)
    ws3 = gmlp_ws[0]
    bs_t = gmlp_bs[0].T
    alog_row = _lane_row(dn_a_log)
    dt_row = _lane_row(dn_dt_bias)
    dn_g2 = dn_norm_g.reshape(1, HEAD)
    mem_g2 = mem_norm_g.reshape(1, d)
    fin_g2 = final_g.reshape(1, d)

    proj, ab, h_t, (g_out, g_kv, g_conv) = _inproj(
        xs, ln_g2, w_main, w_ab, [w_out[0].astype(BF16), w_mem_kv[0].astype(BF16), conv_w[0]])
    wo = g_out.reshape(MIX_W, d)
    wo_perm = jnp.concatenate([wo[GMLP_W:GMLP_W + DN_W], wo[0:GMLP_W], wo[GMLP_W + DN_W:MIX_W]], axis=0)
    w_kv = g_kv.reshape(d, 2 * XA_W)
    conv_full = g_conv.transpose(1, 0, 2).reshape(DN_K, 3 * DN_W)
    out_a = _gmlp_fwd(proj, lng2, lnb2, ws3, bs_t)
    mkv = _memkv_fwd(mems, mem_g2, w_kv)
    out_c = _xattn_fwd(proj, mkv)
    q, k, v, gb, gbt, yc = _dn_pre(proj, ab, conv_full, alog_row, dt_row)
    u, wk, qg, kd, tmat, ai, egl = _dn_local(q, k, v, gb, gbt)
    o, vn, st, out_b = _dn_scan(u, wk, qg, kd, ai, egl, proj, dn_g2)

    dx2, dx2b, dmixed, loss_acc, d_fin_g = _final(xs, tgt, out_b, out_a, out_c, wo_perm, fin_g2)
    loss = lax.psum(loss_acc[0, 0], ("x", "y", "c"))

    dwo_b = _matmul_tn(out_b, dx2b, "dw_out_b")
    dwo_a = _matmul_tn(out_a, dx2b, "dw_out_a")
    dwo_c = _matmul_tn(out_c, dx2b, "dw_out_c")
    d_w_out = jnp.concatenate([dwo_a, dwo_b, dwo_c], axis=0)

    dp_g, d_ws, d_bst, d_lng, d_lnb = _gmlp_bwd(proj, dmixed, lng2, lnb2, ws3, bs_t)
    dp_x, dmkv = _xattn_bwd(proj, dmixed, mkv)
    d_w_kv, d_mem_g = _memkv_bwd(mems, mem_g2, w_kv, dmkv)
    do, dvn, dst, dp_dz, d_dn_g = _dn_scan_bwd(dmixed, o, proj, dn_g2, wk, qg, kd, ai, egl)
    dq, dk, dv, dgb = _dn_local_bwd(q, k, v, gb, gbt, tmat, vn, st, dst, do, dvn)
    dp_qkv, dp_ab, d_conv, d_alog, d_dt = _dn_pre_bwd(proj, yc, ab, conv_full, alog_row, dt_row, dq, dk, dv, dgb)

    terms = [(dp_qkv, w_main, 3 * DN_W, 0, 0), (dp_dz, w_main, DN_W, 0, 3)]
    terms += [(dp_g, w_main, GMLP_W, b, GU_BLK + b) for b in range(3)]
    terms += [(dp_x, w_main, XA_W, b, CQ_BLK + b) for b in range(2)]
    terms += [(dp_ab, w_ab, LANE, 0, 0)]
    dw_qkv = _matmul_acc(h_t, dp_qkv, "dw_in_qkv")
    dw_dz = _matmul_acc(h_t, dp_dz, "dw_in_dz")
    dw_gm = _matmul_acc(h_t, dp_g, "dw_in_gmlp")
    dw_xa = _matmul_acc(h_t, dp_x, "dw_in_xa")
    dw_ab = _matmul_acc(h_t, dp_ab, "dw_in_ab")
    segs = [(o_g, dw_gm), (o_dn, dw_qkv), (o_dn + 3 * DN_W, dw_dz), (o_ab, dw_ab[:, :2 * DN_H]), (o_xa, dw_xa)]
    shards = []
    for sh in range(N_DEV):
        lo, hi = sh * shard_w, (sh + 1) * shard_w
        parts = [arr[:, max(lo, off) - off:min(hi, off + arr.shape[1]) - off] for off, arr in segs
                 if off < hi and off + arr.shape[1] > lo]
        shards.append(jnp.concatenate(parts, axis=1).astype(BF16))
    send_in = jnp.stack(shards)

    small_shapes = [gmlp_ln_g.shape, gmlp_ln_b.shape, gmlp_ws.shape, gmlp_bs.shape, dn_a_log.shape,
                    dn_dt_bias.shape, dn_norm_g.shape, mem_norm_g.shape, final_g.shape, (DN_K, 3 * DN_W)]
    small_g = _pack([d_lng, d_lnb, d_ws, d_bst.T, d_alog[:, :DN_H], d_dt[:, :DN_H], d_dn_g, d_mem_g, d_fin_g, d_conv])
    zc = jnp.zeros((DN_K, 3 * DN_W), F32)
    small_w = _pack([gmlp_ln_g, gmlp_ln_b, gmlp_ws, gmlp_bs, dn_a_log, dn_dt_bias, dn_norm_g, mem_norm_g, final_g, zc])
    small_m = _pack([m_gmlp_ln_g, m_gmlp_ln_b, m_gmlp_ws, m_gmlp_bs, m_dn_a_log, m_dn_dt_bias, m_dn_norm_g,
                     m_mem_norm_g, m_final_g, zc])
    small_v = _pack([v_gmlp_ln_g, v_gmlp_ln_b, v_gmlp_ws, v_gmlp_bs, v_dn_a_log, v_dn_dt_bias, v_dn_norm_g,
                     v_mem_norm_g, v_final_g, zc + 1.0])

    send_out = d_w_out.reshape(N_DEV, MIX_W // N_DEV, d).astype(BF16)
    send_kv = d_w_kv.reshape(N_DEV, d // N_DEV, 2 * XA_W).astype(BF16)
    sends = [send_in, send_out, send_kv]
    all_small, got = _swap_halves(small_g, sends, "swap_halves")
    core = lax.axis_index("c").astype(jnp.int32).reshape(1)
    chip_sums = [_pair_sum(core, sends[i], got[i], "pair_sum_%d" % i) for i in range(3)]
    dh, (r_in, r_out, r_kv) = _dh(terms, chip_sums)
    grad_x, d_ln_g = _rms_bwd(xs, dh, dx2, ln_g2)
    (all_ln_g,) = _gather_two_level([_pack([d_ln_g])], "gather_ln_g")

    g_w_in, dl_w_in, nm_w_in, nv_w_in = _adam(r_in, w_in[0], m_w_in[0], v_w_in[0], "adam_w_in")
    g_w_out, dl_w_out, nm_w_out, nv_w_out = _adam(r_out, w_out[0], m_w_out[0], v_w_out[0], "adam_w_out")
    g_w_kv, dl_w_kv, nm_w_kv, nv_w_kv = _adam(r_kv, w_mem_kv[0], m_w_mem_kv[0], v_w_mem_kv[0], "adam_w_kv")
    sm = [_unpack(t, small_shapes) for t in _adam(all_small, small_w, small_m, small_v, "adam_small")]
    ln_res = [_unpack(t, [ln_g.shape])[0]
              for t in _adam(all_ln_g, _pack([ln_g]), _pack([m_ln_g]), _pack([v_ln_g]), "adam_ln_g")]

    conv_parts = lax.dynamic_slice(all_small, (0, all_small.shape[1] - (DN_K * 3 * DN_W) // LANE, 0),
                                   (N_DEV, (DN_K * 3 * DN_W) // LANE, LANE)).reshape(N_DEV, DN_K, 3 * DN_W)
    cshard = conv_w.shape[2]
    conv_parts = lax.dynamic_slice(conv_parts, (0, 0, me * cshard), (N_DEV, DN_K, cshard))
    cpad = ((0, 0), (0, HALO - DN_K), (0, 0))
    conv_res = _adam(jnp.pad(conv_parts, cpad), jnp.pad(conv_w[0], cpad[1:]), jnp.pad(m_conv_w[0], cpad[1:]),
                     jnp.pad(v_conv_w[0], cpad[1:], constant_values=1.0), "adam_conv")
    g_conv_s, dl_conv, nm_conv, nv_conv = [t[:DN_K][None] for t in conv_res]

    def group(idx, big_in, big_conv, big_kv, big_out):
        names = sm[idx]
        return [ln_res[idx], big_in[None], names[0], names[1], names[2], names[3], big_conv, names[4], names[5], names[6],
                names[7], big_kv[None], big_out[None], names[8]]

    grads = group(0, g_w_in, g_conv_s, g_w_kv, g_w_out)
    deltas = group(1, dl_w_in, dl_conv, dl_w_kv, dl_w_out)
    new_m = group(2, nm_w_in, nm_conv, nm_w_kv, nm_w_out)
    new_v = group(3, nv_w_in, nv_conv, nv_w_kv, nv_w_out)
    return (loss, grad_x[None], *grads, *deltas, *new_m, *new_v)
```

```python
import functools

import jax
import jax.numpy as jnp
from jax import lax
from jax.experimental import pallas as pl
from jax.experimental.pallas import tpu as pltpu

F32 = jnp.float32
BF16 = jnp.bfloat16
HIGHEST = lax.Precision.HIGHEST
MESH_ID = pl.DeviceIdType.MESH

N_DEV = 8
EPS = 1e-6
GMLP_W = 512
GMLP_G = 4
GMLP_T = 128
DN_W = 1024
DN_H = 8
HEAD = 128
DN_K = 4
CH = 64
XA_W = 512
XA_H = 4
LANE = 128
HALO = 8
MAIN_W = 4 * DN_W + 3 * GMLP_W + 2 * XA_W
MIX_W = DN_W + GMLP_W + XA_W
VMEM_LIMIT = 56 * 1024 * 1024

ADAM_LR = 0.001
ADAM_B1 = 0.9
ADAM_B2 = 0.999
ADAM_EPS = 1e-08
ADAM_WD = 0.01
ADAM_STEP = 10


def _sds(shape, dtype=F32):
    return jax.ShapeDtypeStruct(tuple(shape), dtype)


def _params(sem=None):
    if sem is None:
        return pltpu.CompilerParams(vmem_limit_bytes=VMEM_LIMIT)
    return pltpu.CompilerParams(dimension_semantics=tuple(sem), vmem_limit_bytes=VMEM_LIMIT)


def _tile(n, prefs):
    for p in prefs:
        if n % p == 0:
            return p
    return n


def _mm(a, b):
    return jnp.dot(a.astype(BF16), b.astype(BF16), preferred_element_type=F32)


def _mm_nt(a, b):
    return lax.dot_general(a.astype(BF16), b.astype(BF16), (((1,), (1,)), ((), ())), preferred_element_type=F32)


def _mm_tn(a, b):
    return lax.dot_general(a.astype(BF16), b.astype(BF16), (((0,), (0,)), ((), ())), preferred_element_type=F32)


def _mm_hi(a, b):
    return jnp.dot(a, b, precision=HIGHEST, preferred_element_type=F32)


def _mm_3x(a, b):
    return jnp.dot(a, b, precision=lax.Precision.HIGH, preferred_element_type=F32)


_GELU_C = 0.7978845608028654
_GELU_A = 0.044715


def _gelu(x):
    return 0.5 * x * (1.0 + jnp.tanh(_GELU_C * (x + _GELU_A * x * x * x)))


def _gelu_grad(x):
    t = jnp.tanh(_GELU_C * (x + _GELU_A * x * x * x))
    return 0.5 * (1.0 + t) + 0.5 * x * (1.0 - t * t) * _GELU_C * (1.0 + 3.0 * _GELU_A * x * x)


def _silu(x):
    return x * jax.nn.sigmoid(x)


def _silu_grad(x):
    s = jax.nn.sigmoid(x)
    return s * (1.0 + x * (1.0 - s))


def _rowsum(x):
    return jnp.sum(x, axis=-1, keepdims=True)


def _colsum(x):
    return jnp.sum(x, axis=0, keepdims=True)


def _iota2(shape, dim):
    return lax.broadcasted_iota(jnp.int32, shape, dim)


def _chunk_tri(tm, upper):
    r = _iota2((tm, tm), 0)
    c = _iota2((tm, tm), 1)
    same = lax.shift_right_logical(r, 6) == lax.shift_right_logical(c, 6)
    tri = (r <= c) if upper else (r >= c)
    return jnp.where(same & tri, 1.0, 0.0).astype(F32)


N_CHIP = 4


def _mesh_place():
    x, y, c = lax.axis_index("x"), lax.axis_index("y"), lax.axis_index("c")
    chips = [(1 - x, y), (x, 1 - y), (1 - x, 1 - y)]
    return x, y, c, (x, y, 1 - c), chips


class _Gather:
    def __init__(self, ins, outs, send_sems, recv_sems, loc_sems):
        self.ins, self.outs, self.send_sems, self.recv_sems, self.loc_sems = ins, outs, send_sems, recv_sems, loc_sems
        self.x, self.y, self.c, self.sib, self.chips = _mesh_place()
        self.me = (self.x, self.y, self.c)

    def copy(self, a, k, block, to, src=None):
        slot = self.outs[a].at[4 * block[0] + 2 * block[1] + block[2]]
        return pltpu.make_async_remote_copy(
            src_ref=slot if src is None else src, dst_ref=slot, send_sem=self.send_sems.at[a, k],
            recv_sem=self.recv_sems.at[a, k], device_id=to, device_id_type=MESH_ID)

    def own(self, a):
        return pltpu.make_async_copy(self.ins[a], self.outs[a].at[4 * self.x + 2 * self.y + self.c], self.loc_sems.at[a])

    def first(self, a):
        return [self.copy(a, 0, self.me, self.sib, src=self.ins[a])] + [
            self.copy(a, 1 + j, self.me, (*chip, self.c), src=self.ins[a]) for j, chip in enumerate(self.chips)]

    def passed(self, a, j):
        return self.copy(a, 4 + j, (*self.chips[j], self.c), self.sib)

    def start(self):
        for a in range(len(self.ins)):
            self.own(a).start()
            for cp in self.first(a):
                cp.start()

    def finish(self):
        n = len(self.ins)
        for a in range(n):
            for j, chip in enumerate(self.chips):
                self.copy(a, 1 + j, (*chip, self.c), self.me).wait_recv()
                self.passed(a, j).start()
        for a in range(n):
            self.copy(a, 0, self.sib, self.me).wait_recv()
            for j, chip in enumerate(self.chips):
                self.copy(a, 4 + j, (*chip, 1 - self.c), self.me).wait_recv()
        for a in range(n):
            for cp in self.first(a) + [self.passed(a, j) for j in range(N_CHIP - 1)]:
                cp.wait_send()
            self.own(a).wait()

    @staticmethod
    def sems(n):
        return [pltpu.SemaphoreType.DMA((n, N_DEV - 1)), pltpu.SemaphoreType.DMA((n, N_DEV - 1)),
                pltpu.SemaphoreType.DMA((n,))]


def _gather_two_level(arrs, name):
    n = len(arrs)

    def body(*refs):
        g = _Gather(refs[:n], refs[n:2 * n], *refs[2 * n:])
        g.start()
        g.finish()

    any_spec = pl.BlockSpec(memory_space=pl.ANY)
    return pl.pallas_call(
        body, name=name, out_shape=[_sds((N_DEV,) + a.shape, a.dtype) for a in arrs],
        in_specs=[any_spec] * n, out_specs=[any_spec] * n, scratch_shapes=_Gather.sems(n),
        compiler_params=pltpu.CompilerParams(has_side_effects=True),
    )(*arrs)


def _swap_halves(small, grads, name):
    n = len(grads)

    def body(*refs):
        small_ref = refs[0]
        ins = refs[1:1 + n]
        small_out = refs[1 + n]
        got = refs[2 + n:2 + 2 * n]
        s_send, s_recv, g_send, g_recv, loc_sem = refs[2 + 2 * n:]
        x, y, c, sib, _ = _mesh_place()
        me = 4 * x + 2 * y + c
        sends, recvs = [], []
        for j in range(1, N_DEV):
            px = 1 - x if (j >> 2) & 1 else x
            py = 1 - y if (j >> 1) & 1 else y
            pc = 1 - c if j & 1 else c
            cp = pltpu.make_async_remote_copy(
                src_ref=small_ref, dst_ref=small_out.at[me], send_sem=s_send.at[j - 1], recv_sem=s_recv.at[j - 1],
                device_id=(px, py, pc), device_id_type=MESH_ID)
            cp.start()
            sends.append(cp)
            recvs.append(pltpu.make_async_remote_copy(
                src_ref=small_ref, dst_ref=small_out.at[4 * px + 2 * py + pc], send_sem=s_send.at[j - 1],
                recv_sem=s_recv.at[j - 1], device_id=(px, py, pc), device_id_type=MESH_ID))
        own = pltpu.make_async_copy(small_ref, small_out.at[me], loc_sem)
        own.start()
        for a in range(n):
            for chip in range(N_CHIP):
                cp = pltpu.make_async_remote_copy(
                    src_ref=ins[a].at[2 * chip + 1 - c], dst_ref=got[a].at[chip], send_sem=g_send.at[a, chip],
                    recv_sem=g_recv.at[a, chip], device_id=sib, device_id_type=MESH_ID)
                cp.start()
                sends.append(cp)
                recvs.append(cp)
        for cp in sends:
            cp.wait_send()
        for cp in recvs:
            cp.wait_recv()
        own.wait()

    half = [_sds((N_CHIP,) + g.shape[1:], g.dtype) for g in grads]
    any_spec = pl.BlockSpec(memory_space=pl.ANY)
    res = pl.pallas_call(
        body, name=name, out_shape=[_sds((N_DEV,) + small.shape, small.dtype)] + half,
        in_specs=[any_spec] * (1 + n), out_specs=[any_spec] * (1 + n),
        scratch_shapes=[pltpu.SemaphoreType.DMA((N_DEV - 1,)), pltpu.SemaphoreType.DMA((N_DEV - 1,)),
                        pltpu.SemaphoreType.DMA((n, N_CHIP)), pltpu.SemaphoreType.DMA((n, N_CHIP)),
                        pltpu.SemaphoreType.DMA],
        compiler_params=pltpu.CompilerParams(has_side_effects=True),
    )(small, *grads)
    return res[0], res[1:]


def _pair_sum(core, mine, got, name):
    nc, r, c = got.shape
    tr = _tile(r, (256, 128, 64, 32, 16))

    def body(core_ref, a_ref, b_ref, o_ref):
        o_ref[...] = (a_ref[...].astype(F32) + b_ref[...].astype(F32)).astype(BF16)

    return pl.pallas_call(
        body, name=name, out_shape=_sds(got.shape, BF16),
        grid_spec=pltpu.PrefetchScalarGridSpec(
            num_scalar_prefetch=1, grid=(nc, r // tr),
            in_specs=[pl.BlockSpec((1, tr, c), lambda i, j, core_ref: (2 * i + core_ref[0], j, 0)),
                      pl.BlockSpec((1, tr, c), lambda i, j, core_ref: (i, j, 0))],
            out_specs=pl.BlockSpec((1, tr, c), lambda i, j, core_ref: (i, j, 0))),
        compiler_params=_params(("parallel", "parallel")),
    )(core, mine, got)


class _ChipExchange:
    def __init__(self, ins, outs, send_sems, recv_sems, loc_sems):
        self.ins, self.outs, self.send_sems, self.recv_sems, self.loc_sems = ins, outs, send_sems, recv_sems, loc_sems
        self.x, self.y, self.c, _, self.chips = _mesh_place()
        self.mine = 2 * self.x + self.y

    def own(self, a):
        return pltpu.make_async_copy(self.ins[a].at[self.mine], self.outs[a].at[self.mine], self.loc_sems.at[a])

    def copy(self, a, j, lands_in):
        chip = self.chips[j]
        return pltpu.make_async_remote_copy(
            src_ref=self.ins[a].at[2 * chip[0] + chip[1]], dst_ref=self.outs[a].at[lands_in],
            send_sem=self.send_sems.at[a, j], recv_sem=self.recv_sems.at[a, j], device_id=(*chip, self.c),
            device_id_type=MESH_ID)

    def start(self):
        for a in range(len(self.ins)):
            self.own(a).start()
            for j in range(N_CHIP - 1):
                self.copy(a, j, self.mine).start()

    def finish(self):
        for a in range(len(self.ins)):
            for j, chip in enumerate(self.chips):
                self.copy(a, j, self.mine).wait_send()
                self.copy(a, j, 2 * chip[0] + chip[1]).wait_recv()
            self.own(a).wait()

    @staticmethod
    def sems(n):
        return [pltpu.SemaphoreType.DMA((n, N_CHIP - 1)), pltpu.SemaphoreType.DMA((n, N_CHIP - 1)),
                pltpu.SemaphoreType.DMA((n,))]


def _inproj(x, ln_g, w_main, w_ab, late):
    s, d = x.shape
    n = w_main.shape[1]
    tm = _tile(s, (512, 256, 128))
    tn = _tile(n, (1664, 512, 128))
    nl = len(late)
    ni, nj = s // tm, n // tn

    def body(*refs):
        x_ref, g_ref, w_ref, wab_ref = refs[:4]
        proj_ref, ab_ref, ht_ref = refs[4 + nl:7 + nl]
        hs = refs[7 + 2 * nl]
        gather = _Gather(refs[4:4 + nl], refs[7 + nl:7 + 2 * nl], *refs[8 + 2 * nl:])
        step = pl.program_id(0) * nj + pl.program_id(1)

        @pl.when(step == 0)
        def _():
            gather.start()

        @pl.when(pl.program_id(1) == 0)
        def _():
            xv = x_ref[...]
            r = lax.rsqrt(jnp.mean(xv * xv, axis=-1, keepdims=True) + EPS)
            hf = xv * r * g_ref[...]
            h = hf.astype(BF16)
            hs[...] = h
            ht_ref[...] = hf.T.astype(BF16)
            ab_ref[...] = jnp.dot(h, wab_ref[...], preferred_element_type=F32)

        proj_ref[...] = jnp.dot(hs[...], w_ref[...], preferred_element_type=F32)

        @pl.when(step == ni * nj - 1)
        def _():
            gather.finish()

    any_spec = pl.BlockSpec(memory_space=pl.ANY)
    res = pl.pallas_call(
        body, name="inproj", grid=(ni, nj),
        in_specs=[pl.BlockSpec((tm, d), lambda i, j: (i, 0)), pl.BlockSpec((1, d), lambda i, j: (0, 0)),
                  pl.BlockSpec((d, tn), lambda i, j: (0, j)), pl.BlockSpec((d, LANE), lambda i, j: (0, 0))]
        + [any_spec] * nl,
        out_specs=[pl.BlockSpec((tm, tn), lambda i, j: (i, j)), pl.BlockSpec((tm, LANE), lambda i, j: (i, 0)),
                   pl.BlockSpec((d, tm), lambda i, j: (0, i))] + [any_spec] * nl,
        out_shape=[_sds((s, n)), _sds((s, LANE)), _sds((d, s), BF16)]
        + [_sds((N_DEV,) + a.shape, a.dtype) for a in late],
        scratch_shapes=[pltpu.VMEM((tm, d), BF16)] + _Gather.sems(nl),
        compiler_params=_params(("arbitrary", "arbitrary")),
    )(x, ln_g, w_main, w_ab, *late)
    return res[0], res[1], res[2], res[3:]


def _matmul_acc(a, b, name):
    m, k = a.shape
    n = b.shape[1]
    tm = _tile(m, (2048, 1024, 512, 256, 128))
    tn = _tile(n, (1024, 512, 256, 128))
    tk = _tile(k, (1024, 512, 256, 128))

    def body(a_ref, b_ref, o_ref):
        @pl.when(pl.program_id(2) == 0)
        def _():
            o_ref[...] = jnp.zeros_like(o_ref)

        o_ref[...] += jnp.dot(a_ref[...], b_ref[...], preferred_element_type=F32)

    return pl.pallas_call(
        body, name=name, grid=(m // tm, n // tn, k // tk),
        in_specs=[pl.BlockSpec((tm, tk), lambda i, j, l: (i, l)), pl.BlockSpec((tk, tn), lambda i, j, l: (l, j))],
        out_specs=pl.BlockSpec((tm, tn), lambda i, j, l: (i, j)),
        out_shape=_sds((m, n)),
        compiler_params=_params(("parallel", "parallel", "arbitrary")),
    )(a, b)


def _matmul_tn(a, b, name):
    k, m = a.shape
    n = b.shape[1]
    tm = _tile(m, (1024, 512, 256, 128))
    tn = _tile(n, (1024, 512, 256, 128))
    tk = _tile(k, (1024, 512, 256, 128))

    def body(a_ref, b_ref, o_ref):
        @pl.when(pl.program_id(2) == 0)
        def _():
            o_ref[...] = jnp.zeros_like(o_ref)

        o_ref[...] += _mm_tn(a_ref[...], b_ref[...])

    return pl.pallas_call(
        body, name=name, grid=(m // tm, n // tn, k // tk),
        in_specs=[pl.BlockSpec((tk, tm), lambda i, j, l: (l, i)), pl.BlockSpec((tk, tn), lambda i, j, l: (l, j))],
        out_specs=pl.BlockSpec((tm, tn), lambda i, j, l: (i, j)),
        out_shape=_sds((m, n)),
        compiler_params=_params(("parallel", "parallel", "arbitrary")),
    )(a, b)


def _dh_rms(pieces, w_main, w_ab, x, dx2, ln_g, chip_sums):
    s, d = x.shape
    npc = len(pieces)
    nx = len(chip_sums)
    tm = _tile(s, (256, 128))
    ni = s // tm
    widths = [p.shape[1] for p in pieces[:-1]]
    offs = [sum(widths[:p]) for p in range(npc - 1)]

    def body(*refs):
        p_refs = refs[:npc]
        w_ref, wab_ref, x_ref, dx2_ref, g_ref = refs[npc:npc + 5]
        gx_ref, dg_ref = refs[npc + 5 + nx:npc + 7 + nx]
        exch = _ChipExchange(refs[npc + 5:npc + 5 + nx], refs[npc + 7 + nx:npc + 7 + 2 * nx], *refs[npc + 7 + 2 * nx:])
        step = pl.program_id(0)

        @pl.when(step == 0)
        def _():
            dg_ref[...] = jnp.zeros_like(dg_ref)
            exch.start()

        dhv = _mm_nt(p_refs[npc - 1][...], wab_ref[...])
        for p in range(npc - 1):
            dhv += _mm_nt(p_refs[p][...], w_ref[:, offs[p]:offs[p] + widths[p]])
        xv = x_ref[...]
        r = lax.rsqrt(jnp.mean(xv * xv, axis=-1, keepdims=True) + EPS)
        xhat = xv * r
        dg_ref[...] += _colsum(dhv * xhat)
        dxh = dhv * g_ref[...]
        gx_ref[...] = dx2_ref[...] + r * (dxh - xhat * jnp.mean(dxh * xhat, axis=-1, keepdims=True))

        @pl.when(step == ni - 1)
        def _():
            exch.finish()

    any_spec = pl.BlockSpec(memory_space=pl.ANY)
    row = pl.BlockSpec((tm, d), lambda i: (i, 0))
    vec = pl.BlockSpec((1, d), lambda i: (0, 0))
    once = lambda a: pl.BlockSpec(a.shape, lambda i: (0, 0), pipeline_mode=pl.Buffered(1))
    in_specs = [pl.BlockSpec((tm, p.shape[1]), lambda i: (i, 0)) for p in pieces]
    in_specs += [once(w_main), once(w_ab), row, row, vec] + [any_spec] * nx
    res = pl.pallas_call(
        body, name="dh_rms", grid=(ni,), in_specs=in_specs,
        out_specs=[row, vec] + [any_spec] * nx,
        out_shape=[_sds((s, d)), _sds((1, d))] + [_sds(p.shape, p.dtype) for p in chip_sums],
        scratch_shapes=_ChipExchange.sems(nx),
        compiler_params=_params(("arbitrary",)),
    )(*pieces, w_main, w_ab, x, dx2, ln_g, *chip_sums)
    return res[0], res[1], res[2:]


def _final(x, tgt, out_b, out_a, out_c, w_out, final_g):
    s, d = x.shape
    tm = _tile(s, (256, 128))

    def body(x_ref, t_ref, b_ref, a_ref, c_ref, w_ref, g_ref, dx2_ref, dx2b_ref, dm_ref, loss_ref, dg_ref):
        @pl.when(pl.program_id(0) == 0)
        def _():
            loss_ref[...] = jnp.zeros_like(loss_ref)
            dg_ref[...] = jnp.zeros_like(dg_ref)

        x2 = x_ref[...]
        x2 += jnp.dot(b_ref[...], w_ref[0:DN_W, :], preferred_element_type=F32)
        x2 += jnp.dot(a_ref[...], w_ref[DN_W:DN_W + GMLP_W, :], preferred_element_type=F32)
        x2 += jnp.dot(c_ref[...], w_ref[DN_W + GMLP_W:MIX_W, :], preferred_element_type=F32)
        r = lax.rsqrt(jnp.mean(x2 * x2, axis=-1, keepdims=True) + EPS)
        xhat = x2 * r
        g = g_ref[...]
        err = xhat * g - t_ref[...]
        tok = 0.5 * jnp.mean(err * err, axis=-1, keepdims=True)
        loss_ref[...] += jnp.broadcast_to(_colsum(tok), loss_ref.shape)
        dy = err * (1.0 / d)
        dg_ref[...] += _colsum(dy * xhat)
        dxh = dy * g
        dx2 = r * (dxh - xhat * jnp.mean(dxh * xhat, axis=-1, keepdims=True))
        dx2_ref[...] = dx2
        dx2b = dx2.astype(BF16)
        dx2b_ref[...] = dx2b
        dm_ref[...] = _mm_nt(dx2b, w_ref[...])

    row = pl.BlockSpec((tm, d), lambda i: (i, 0))
    vec = pl.BlockSpec((1, d), lambda i: (0, 0))
    return pl.pallas_call(
        body, name="final", grid=(s // tm,),
        in_specs=[row, row, pl.BlockSpec((tm, DN_W), lambda i: (i, 0)), pl.BlockSpec((tm, GMLP_W), lambda i: (i, 0)),
                  pl.BlockSpec((tm, XA_W), lambda i: (i, 0)), pl.BlockSpec((MIX_W, d), lambda i: (0, 0)), vec],
        out_specs=[row, row, pl.BlockSpec((tm, MIX_W), lambda i: (i, 0)), pl.BlockSpec((1, LANE), lambda i: (0, 0)), vec],
        out_shape=[_sds((s, d)), _sds((s, d), BF16), _sds((s, MIX_W)), _sds((1, LANE)), _sds((1, d))],
        compiler_params=_params(("arbitrary",)),
    )(x, tgt, out_b, out_a, out_c, w_out, final_g)


GU_BLK = (4 * DN_W) // GMLP_W


def _gmlp_norm(gv, lng, lnb):
    va = _gelu(gv)
    mu = jnp.mean(va, axis=-1, keepdims=True)
    xc = va - mu
    rstd = lax.rsqrt(jnp.mean(xc * xc, axis=-1, keepdims=True) + EPS)
    vhat = xc * rstd
    return vhat, rstd, vhat * lng + lnb


def _gmlp_fwd(proj, lng, lnb, ws, bs_t):
    s = proj.shape[0]
    tm = _tile(s, (512, 256, 128))

    def body(u_ref, v_ref, z_ref, lng_ref, lnb_ref, ws_ref, bst_ref, o_ref):
        _, _, vn = _gmlp_norm(v_ref[...], lng_ref[...], lnb_ref[...])
        tri = _iota2((GMLP_T, GMLP_T), 0) >= _iota2((GMLP_T, GMLP_T), 1)
        for g in range(GMLP_G):
            cs = slice(g * HEAD, (g + 1) * HEAD)
            w = jnp.where(tri, ws_ref[g], 0.0).astype(BF16)
            b = bst_ref[:, g:g + 1]
            for c in range(tm // GMLP_T):
                rs = slice(c * GMLP_T, (c + 1) * GMLP_T)
                sg = _mm(w, vn[rs, cs]) + b
                o_ref[rs, cs] = (_gelu(u_ref[rs, cs]) * sg * _silu(z_ref[rs, cs])).astype(BF16)

    col = lambda k: pl.BlockSpec((tm, GMLP_W), lambda i: (i, GU_BLK + k))
    vec = pl.BlockSpec((1, GMLP_W), lambda i: (0, 0))
    return pl.pallas_call(
        body, name="gmlp_fwd", grid=(s // tm,),
        in_specs=[col(0), col(1), col(2), vec, vec, pl.BlockSpec((GMLP_G, GMLP_T, GMLP_T), lambda i: (0, 0, 0)),
                  pl.BlockSpec((GMLP_T, GMLP_G), lambda i: (0, 0))],
        out_specs=pl.BlockSpec((tm, GMLP_W), lambda i: (i, 0)), out_shape=_sds((s, GMLP_W), BF16),
        compiler_params=_params(("parallel",)),
    )(proj, proj, proj, lng, lnb, ws, bs_t)


def _gmlp_bwd(proj, dmixed, lng, lnb, ws, bs_t):
    s = proj.shape[0]
    tm = _tile(s, (512, 256, 128))

    def body(u_ref, v_ref, z_ref, d_ref, lng_ref, lnb_ref, ws_ref, bst_ref,
             dp_ref, dws_ref, dbst_ref, dlng_ref, dlnb_ref, dvn):
        @pl.when(pl.program_id(0) == 0)
        def _():
            dws_ref[...] = jnp.zeros_like(dws_ref)
            dbst_ref[...] = jnp.zeros_like(dbst_ref)
            dlng_ref[...] = jnp.zeros_like(dlng_ref)
            dlnb_ref[...] = jnp.zeros_like(dlnb_ref)

        gv = v_ref[...]
        lng_v = lng_ref[...]
        vhat, rstd, vn = _gmlp_norm(gv, lng_v, lnb_ref[...])
        tri = _iota2((GMLP_T, GMLP_T), 0) >= _iota2((GMLP_T, GMLP_T), 1)
        for g in range(GMLP_G):
            cs = slice(g * HEAD, (g + 1) * HEAD)
            w = jnp.where(tri, ws_ref[g], 0.0).astype(BF16)
            b = bst_ref[:, g:g + 1]
            dw_acc = jnp.zeros((GMLP_T, GMLP_T), F32)
            db_acc = jnp.zeros((GMLP_T, 1), F32)
            for c in range(tm // GMLP_T):
                rs = slice(c * GMLP_T, (c + 1) * GMLP_T)
                vn_b = vn[rs, cs]
                sg = _mm(w, vn_b) + b
                gu = u_ref[rs, cs]
                gz = z_ref[rs, cs]
                da = d_ref[rs, cs]
                uact = _gelu(gu)
                sz = _silu(gz)
                ds = da * uact * sz
                dp_ref[rs, cs] = (da * sg * sz * _gelu_grad(gu)).astype(BF16)
                dp_ref[rs, 2 * GMLP_W + g * HEAD:2 * GMLP_W + (g + 1) * HEAD] = (da * uact * sg * _silu_grad(gz)).astype(BF16)
                dw_acc += _mm_nt(ds, vn_b)
                db_acc += _rowsum(ds)
                dvn[rs, cs] = _mm_tn(w, ds)
            dws_ref[g] += jnp.where(tri, dw_acc, 0.0)
            dbst_ref[:, g:g + 1] += db_acc
        dvn_v = dvn[...]
        dlng_ref[...] += _colsum(dvn_v * vhat)
        dlnb_ref[...] += _colsum(dvn_v)
        dvh = dvn_v * lng_v
        dva = rstd * (dvh - jnp.mean(dvh, axis=-1, keepdims=True) - vhat * jnp.mean(dvh * vhat, axis=-1, keepdims=True))
        dp_ref[:, GMLP_W:2 * GMLP_W] = (dva * _gelu_grad(gv)).astype(BF16)

    col = lambda k: pl.BlockSpec((tm, GMLP_W), lambda i: (i, GU_BLK + k))
    vec = pl.BlockSpec((1, GMLP_W), lambda i: (0, 0))
    wsp = pl.BlockSpec((GMLP_G, GMLP_T, GMLP_T), lambda i: (0, 0, 0))
    bsp = pl.BlockSpec((GMLP_T, GMLP_G), lambda i: (0, 0))
    return pl.pallas_call(
        body, name="gmlp_bwd", grid=(s // tm,),
        in_specs=[col(0), col(1), col(2), pl.BlockSpec((tm, GMLP_W), lambda i: (i, DN_W // GMLP_W)), vec, vec, wsp, bsp],
        out_specs=[pl.BlockSpec((tm, 3 * GMLP_W), lambda i: (i, 0)), wsp, bsp, vec, vec],
        out_shape=[_sds((s, 3 * GMLP_W), BF16), _sds((GMLP_G, GMLP_T, GMLP_T)), _sds((GMLP_T, GMLP_G)),
                   _sds((1, GMLP_W)), _sds((1, GMLP_W))],
        scratch_shapes=[pltpu.VMEM((tm, GMLP_W), F32)],
        compiler_params=_params(("arbitrary",)),
    )(proj, proj, proj, dmixed, lng, lnb, ws, bs_t)


CQ_BLK = (4 * DN_W + 3 * GMLP_W) // XA_W


def _memkv_fwd(mem, g, w_kv):
    nm, d = mem.shape

    def body(m_ref, g_ref, w_ref, kv_ref):
        mv = m_ref[...]
        r = lax.rsqrt(jnp.mean(mv * mv, axis=-1, keepdims=True) + EPS)
        kv_ref[...] = _mm(mv * r * g_ref[...], w_ref[...])

    return pl.pallas_call(body, name="memkv_fwd", out_shape=_sds((nm, 2 * XA_W)), compiler_params=_params())(mem, g, w_kv)


def _memkv_bwd(mem, g, w_kv, dkv):
    nm, d = mem.shape

    def body(m_ref, g_ref, w_ref, dkv_ref, dw_ref, dg_ref):
        mv = m_ref[...]
        r = lax.rsqrt(jnp.mean(mv * mv, axis=-1, keepdims=True) + EPS)
        xhat = mv * r
        dkv_v = dkv_ref[...]
        dw_ref[...] = _mm_tn(xhat * g_ref[...], dkv_v)
        dg_ref[...] = _colsum(_mm_nt(dkv_v, w_ref[...]) * xhat)

    return pl.pallas_call(body, name="memkv_bwd", out_shape=[_sds((d, 2 * XA_W)), _sds((1, d))],
                          compiler_params=_params())(mem, g, w_kv, dkv)


def _xattn_probs(q, mk):
    sc = _mm_nt(q, mk) * (HEAD ** -0.5)
    e = jnp.exp(sc - jnp.max(sc, axis=-1, keepdims=True))
    return e / _rowsum(e)


def _xattn_fwd(proj, mkv):
    s = proj.shape[0]
    nm = mkv.shape[0]
    tm = _tile(s, (512, 256, 128))

    def body(q_ref, z_ref, kv_ref, o_ref):
        for h in range(XA_H):
            cs = slice(h * HEAD, (h + 1) * HEAD)
            p = _xattn_probs(q_ref[:, cs], kv_ref[:, cs])
            ctx = _mm(p, kv_ref[:, XA_W + h * HEAD:XA_W + (h + 1) * HEAD])
            o_ref[:, cs] = (ctx * _silu(z_ref[:, cs])).astype(BF16)

    col = lambda k: pl.BlockSpec((tm, XA_W), lambda i: (i, CQ_BLK + k))
    return pl.pallas_call(
        body, name="xattn_fwd", grid=(s // tm,),
        in_specs=[col(0), col(1), pl.BlockSpec((nm, 2 * XA_W), lambda i: (0, 0))],
        out_specs=pl.BlockSpec((tm, XA_W), lambda i: (i, 0)), out_shape=_sds((s, XA_W), BF16),
        compiler_params=_params(("parallel",)),
    )(proj, proj, mkv)


def _xattn_bwd(proj, dmixed, mkv):
    s = proj.shape[0]
    nm = mkv.shape[0]
    tm = _tile(s, (512, 256, 128))

    def body(q_ref, z_ref, d_ref, kv_ref, dp_ref, dkv_ref):
        @pl.when(pl.program_id(0) == 0)
        def _():
            dkv_ref[...] = jnp.zeros_like(dkv_ref)

        for h in range(XA_H):
            cs = slice(h * HEAD, (h + 1) * HEAD)
            vs = slice(XA_W + h * HEAD, XA_W + (h + 1) * HEAD)
            q = q_ref[:, cs]
            z = z_ref[:, cs]
            mk = kv_ref[:, cs]
            mv = kv_ref[:, vs]
            p = _xattn_probs(q, mk)
            ctx = _mm(p, mv)
            dc = d_ref[:, cs]
            dctx = dc * _silu(z)
            dp_ref[:, vs] = (dc * ctx * _silu_grad(z)).astype(BF16)
            dp = _mm_nt(dctx, mv)
            dkv_ref[:, vs] += _mm_tn(p, dctx)
            ds = p * (dp - _rowsum(dp * p)) * (HEAD ** -0.5)
            dp_ref[:, cs] = _mm(ds, mk).astype(BF16)
            dkv_ref[:, cs] += _mm_tn(ds, q)

    col = lambda k: pl.BlockSpec((tm, XA_W), lambda i: (i, CQ_BLK + k))
    kvs = pl.BlockSpec((nm, 2 * XA_W), lambda i: (0, 0))
    return pl.pallas_call(
        body, name="xattn_bwd", grid=(s // tm,),
        in_specs=[col(0), col(1), pl.BlockSpec((tm, XA_W), lambda i: (i, (DN_W + GMLP_W) // XA_W)), kvs],
        out_specs=[pl.BlockSpec((tm, 2 * XA_W), lambda i: (i, 0)), kvs],
        out_shape=[_sds((s, 2 * XA_W), BF16), _sds((nm, 2 * XA_W))],
        compiler_params=_params(("arbitrary",)),
    )(proj, proj, dmixed, mkv)


def _softplus(x):
    return jnp.maximum(x, 0.0) + jnp.log1p(jnp.exp(-jnp.abs(x)))


def _dn_pre(proj, ab, conv_w, alog_row, dt_row):
    s = proj.shape[0]
    tm = _tile(s, (256, 128))
    w3 = 3 * DN_W

    def body(x_ref, halo_ref, ab_ref, cw_ref, al_ref, dt_ref, q_ref, k_ref, v_ref, gb_ref, gbt_ref, yc_ref):
        i = pl.program_id(0)
        xv = x_ref[...]
        cat = jnp.concatenate([jnp.where(i > 0, halo_ref[...], 0.0), xv[0:HALO]], axis=0)
        yc = cw_ref[DN_K - 1:DN_K, :] * xv
        top = cw_ref[DN_K - 1:DN_K, :] * xv[0:HALO]
        for t in range(DN_K - 1):
            back = DN_K - 1 - t
            yc += cw_ref[t:t + 1, :] * pltpu.roll(xv, back, 0)
            top += cw_ref[t:t + 1, :] * pltpu.roll(cat, back, 0)[HALO:2 * HALO]
        yc = jnp.concatenate([top, yc[HALO:tm]], axis=0)
        yc_ref[...] = yc
        act = _silu(yc)
        for h in range(DN_H):
            cs = slice(h * HEAD, (h + 1) * HEAD)
            qa = act[:, cs]
            q_ref[:, cs] = qa * (lax.rsqrt(_rowsum(qa * qa) + EPS) * (HEAD ** -0.5))
            ka = act[:, DN_W + h * HEAD:DN_W + (h + 1) * HEAD]
            k_ref[:, cs] = ka * lax.rsqrt(_rowsum(ka * ka) + EPS)
        v_ref[...] = act[:, 2 * DN_W:w3]
        abv = ab_ref[...]
        lane = _iota2((tm, LANE), 1)
        g = jnp.where(lane < DN_H, -jnp.exp(al_ref[...]) * _softplus(abv + dt_ref[...]), 0.0)
        gc = _mm_hi(_chunk_tri(tm, False), g)
        gbv = jnp.where(lane < DN_H, gc, jnp.where(lane < 2 * DN_H, jax.nn.sigmoid(abv), 0.0))
        gb_ref[...] = gbv
        for c in range(tm // CH):
            gbt_ref[c] = gbv[c * CH:(c + 1) * CH, :].T[0:2 * DN_H, :]

    hb = tm // HALO
    row = lambda w: pl.BlockSpec((tm, w), lambda i: (i, 0))
    vec = pl.BlockSpec((1, LANE), lambda i: (0, 0))
    return pl.pallas_call(
        body, name="dn_pre", grid=(s // tm,),
        in_specs=[row(w3), pl.BlockSpec((HALO, w3), lambda i: (jnp.maximum(i * hb - 1, 0), 0)), row(LANE),
                  pl.BlockSpec((DN_K, w3), lambda i: (0, 0)), vec, vec],
        out_specs=[row(DN_W), row(DN_W), row(DN_W), row(LANE), pl.BlockSpec((tm // CH, 2 * DN_H, CH), lambda i: (i, 0, 0)),
                   row(w3)],
        out_shape=[_sds((s, DN_W)), _sds((s, DN_W)), _sds((s, DN_W)), _sds((s, LANE)), _sds((s // CH, 2 * DN_H, CH)),
                   _sds((s, w3))],
        compiler_params=_params(("parallel",)),
    )(proj, proj, ab, conv_w, alog_row, dt_row)


HEADS = tuple(range(DN_H))


def _hcols(h):
    return slice(h * HEAD, (h + 1) * HEAD)


def _chunk_scalings(k, v, gbv, gbt, h):
    gc = jnp.broadcast_to(gbv[:, h:h + 1], (CH, HEAD))
    beta = jnp.broadcast_to(gbv[:, DN_H + h:DN_H + h + 1], (CH, HEAD))
    gr = gbt[h:h + 1, :]
    ii = _iota2((CH, CH), 0)
    jj = _iota2((CH, CH), 1)
    dec = jnp.exp(jnp.where(ii >= jj, gc[:, 0:CH] - gr, -1e30))
    eg = jnp.exp(gc)
    gl = gr[:, CH - 1:CH]
    kb = k * beta
    return dict(beta=beta, dec=dec, eg=eg, gl=gl, ekd=jnp.exp(gl - gc), kb=kb, vb=v * beta, kbe=kb * eg)


def _chunk_scores(m, q, k):
    kq = _mm_nt(jnp.concatenate([m["kb"], q], axis=0), k)
    strict = _iota2((CH, CH), 0) > _iota2((CH, CH), 1)
    return jnp.where(strict, kq[0:CH] * m["dec"], 0.0), kq[CH:2 * CH] * m["dec"]


def _dn_local(q, k, v, gb, gbt):
    s = q.shape[0]
    cpb = 4 if (s // CH) % 4 == 0 else 1
    tb = cpb * CH
    nblk = s // tb

    def body(q_ref, k_ref, v_ref, gb_ref, gbt_ref, u_ref, w_ref, qg_ref, kd_ref, t_ref, ai_ref, egl_ref):
        def chunk(c, carry):
            r0 = pl.multiple_of(c * CH, CH)
            rows = pl.ds(r0, CH)
            gbv = gb_ref[rows, :]
            gbt_v = gbt_ref[c]
            qs = [q_ref[rows, _hcols(h)] for h in HEADS]
            ks = [k_ref[rows, _hcols(h)] for h in HEADS]
            ms = [_chunk_scalings(ks[h], v_ref[rows, _hcols(h)], gbv, gbt_v, h) for h in HEADS]
            for h in HEADS:
                qg_ref[rows, _hcols(h)] = (qs[h] * ms[h]["eg"]).astype(BF16)
                kd_ref[rows, _hcols(h)] = (ks[h] * ms[h]["ekd"]).astype(BF16)
                egl_ref[c, h:h + 1, :] = jnp.broadcast_to(jnp.exp(ms[h]["gl"]), (1, LANE))
            sc = [_chunk_scores(ms[h], qs[h], ks[h]) for h in HEADS]
            for h in HEADS:
                ai_ref[h, rows, :] = sc[h][1]
            eye = jnp.where(_iota2((CH, CH), 0) == _iota2((CH, CH), 1), 1.0, 0.0).astype(F32)
            ts = [eye - sc[h][0] for h in HEADS]
            ps = [_mm_3x(sc[h][0], sc[h][0]) for h in HEADS]
            ts = [ts[h] + _mm_3x(ts[h], ps[h]) for h in HEADS]
            for _ in range(4):
                ps = [_mm(ps[h], ps[h]) for h in HEADS]
                ts = [ts[h] + _mm(ts[h], ps[h]) for h in HEADS]
            for h in HEADS:
                t_ref[h, rows, :] = ts[h]
                uw = _mm(ts[h], jnp.concatenate([ms[h]["vb"], ms[h]["kbe"]], axis=1))
                u_ref[rows, _hcols(h)] = uw[:, 0:HEAD]
                w_ref[rows, _hcols(h)] = uw[:, HEAD:2 * HEAD].astype(BF16)
            return carry

        lax.fori_loop(0, cpb, chunk, 0)

    row = pl.BlockSpec((tb, DN_W), lambda i: (i, 0))
    sq = pl.BlockSpec((DN_H, tb, CH), lambda i: (0, i, 0))
    return pl.pallas_call(
        body, name="dn_local", grid=(nblk,),
        in_specs=[row, row, row, pl.BlockSpec((tb, LANE), lambda i: (i, 0)),
                  pl.BlockSpec((cpb, 2 * DN_H, CH), lambda i: (i, 0, 0))],
        out_specs=[row, row, row, row, sq, sq, pl.BlockSpec((cpb, DN_H, LANE), lambda i: (i, 0, 0))],
        out_shape=[_sds((s, DN_W)), _sds((s, DN_W), BF16), _sds((s, DN_W), BF16), _sds((s, DN_W), BF16),
                   _sds((DN_H, s, CH)), _sds((DN_H, s, CH)), _sds((s // CH, DN_H, LANE))],
        compiler_params=_params(("parallel",)),
    )(q, k, v, gb, gbt)


def _scan_cpb(s):
    return 8 if (s // CH) % 8 == 0 else 1


def _dn_scan(u, w, qg, kd, ai, egl, proj, norm_g):
    s = u.shape[0]
    cpb = _scan_cpb(s)
    tb = cpb * CH
    nblk = s // tb

    def body(u_ref, w_ref, qg_ref, kd_ref, ai_ref, egl_ref, z_ref, ng_ref, o_ref, vn_ref, st_ref, ob_ref, state):
        @pl.when(pl.program_id(0) == 0)
        def _():
            state[...] = jnp.zeros_like(state)

        ng = ng_ref[...]

        def chunk(c, carry):
            r0 = pl.multiple_of(c * CH, CH)
            rows = pl.ds(r0, CH)
            sts = [state[h] for h in HEADS]
            stb = [sts[h].astype(BF16) for h in HEADS]
            for h in HEADS:
                st_ref[c, h] = sts[h]
            vns = [u_ref[rows, _hcols(h)] - jnp.dot(w_ref[rows, _hcols(h)], stb[h], preferred_element_type=F32)
                   for h in HEADS]
            vnb = [vns[h].astype(BF16) for h in HEADS]
            for h in HEADS:
                state[h] = sts[h] * egl_ref[c, h:h + 1, :] + _mm_tn(kd_ref[rows, _hcols(h)], vnb[h])
            os_ = [jnp.dot(qg_ref[rows, _hcols(h)], stb[h], preferred_element_type=F32) + _mm(ai_ref[h, rows, :], vnb[h])
                   for h in HEADS]
            for h in HEADS:
                o = os_[h]
                vn_ref[rows, _hcols(h)] = vns[h]
                o_ref[rows, _hcols(h)] = o
                r = lax.rsqrt(jnp.mean(o * o, axis=-1, keepdims=True) + EPS)
                ob_ref[rows, _hcols(h)] = (o * r * ng * _silu(z_ref[rows, _hcols(h)])).astype(BF16)
            return carry

        lax.fori_loop(0, cpb, chunk, 0)

    row = pl.BlockSpec((tb, DN_W), lambda i: (i, 0))
    return pl.pallas_call(
        body, name="dn_scan", grid=(nblk,),
        in_specs=[row, row, row, row, pl.BlockSpec((DN_H, tb, CH), lambda i: (0, i, 0)),
                  pl.BlockSpec((cpb, DN_H, LANE), lambda i: (i, 0, 0)), pl.BlockSpec((tb, DN_W), lambda i: (i, 3)),
                  pl.BlockSpec((1, HEAD), lambda i: (0, 0))],
        out_specs=[row, row, pl.BlockSpec((cpb, DN_H, HEAD, HEAD), lambda i: (i, 0, 0, 0)), row],
        out_shape=[_sds((s, DN_W)), _sds((s, DN_W)), _sds((s // CH, DN_H, HEAD, HEAD)), _sds((s, DN_W), BF16)],
        scratch_shapes=[pltpu.VMEM((DN_H, HEAD, HEAD), F32)],
        compiler_params=_params(("arbitrary",)),
    )(u, w, qg, kd, ai, egl, proj, norm_g)


def _dn_scan_bwd(dmixed, o, proj, norm_g, w, qg, kd, ai, egl):
    s = o.shape[0]
    cpb = _scan_cpb(s)
    tb = cpb * CH
    nblk = s // tb

    def body(dm_ref, o_ref, z_ref, ng_ref, w_ref, qg_ref, kd_ref, ai_ref, egl_ref,
             do_ref, dvn_ref, dst_ref, dz_ref, dng_ref, dstate):
        @pl.when(pl.program_id(0) == 0)
        def _():
            dstate[...] = jnp.zeros_like(dstate)
            dng_ref[...] = jnp.zeros_like(dng_ref)

        ng = ng_ref[...]

        def chunk(cc, carry):
            c = cpb - 1 - cc
            r0 = pl.multiple_of(c * CH, CH)
            rows = pl.ds(r0, CH)
            dng = jnp.zeros((1, HEAD), F32)
            dob = []
            for h in HEADS:
                cs = _hcols(h)
                o = o_ref[rows, cs]
                z = z_ref[rows, cs]
                db = dm_ref[rows, cs]
                r = lax.rsqrt(jnp.mean(o * o, axis=-1, keepdims=True) + EPS)
                ohat = o * r
                dz_ref[rows, cs] = (db * ohat * ng * _silu_grad(z)).astype(BF16)
                dyn = db * _silu(z)
                dng += _colsum(dyn * ohat)
                doh = dyn * ng
                do = r * (doh - ohat * jnp.mean(doh * ohat, axis=-1, keepdims=True))
                do_ref[rows, cs] = do
                dob.append(do.astype(BF16))
            dng_ref[...] += dng
            dsn = [dstate[h] for h in HEADS]
            for h in HEADS:
                dst_ref[c, h] = dsn[h]
            dvn = [_mm_tn(ai_ref[h, rows, :], dob[h])
                   + jnp.dot(kd_ref[rows, _hcols(h)], dsn[h].astype(BF16), preferred_element_type=F32) for h in HEADS]
            part = [_mm_tn(qg_ref[rows, _hcols(h)], dob[h]) + egl_ref[c, h:h + 1, :] * dsn[h] for h in HEADS]
            for h in HEADS:
                dvn_ref[rows, _hcols(h)] = dvn[h]
                dstate[h] = part[h] - _mm_tn(w_ref[rows, _hcols(h)], dvn[h])
            return carry

        lax.fori_loop(0, cpb, chunk, 0)

    rev = lambda i: (nblk - 1 - i, 0)
    row = pl.BlockSpec((tb, DN_W), rev)
    vec = pl.BlockSpec((1, HEAD), lambda i: (0, 0))
    return pl.pallas_call(
        body, name="dn_scan_bwd", grid=(nblk,),
        in_specs=[row, row, pl.BlockSpec((tb, DN_W), lambda i: (nblk - 1 - i, 3)), vec, row, row, row,
                  pl.BlockSpec((DN_H, tb, CH), lambda i: (0, nblk - 1 - i, 0)),
                  pl.BlockSpec((cpb, DN_H, LANE), lambda i: (nblk - 1 - i, 0, 0))],
        out_specs=[row, row, pl.BlockSpec((cpb, DN_H, HEAD, HEAD), lambda i: (nblk - 1 - i, 0, 0, 0)), row, vec],
        out_shape=[_sds((s, DN_W)), _sds((s, DN_W)), _sds((s // CH, DN_H, HEAD, HEAD)), _sds((s, DN_W), BF16),
                   _sds((1, HEAD))],
        scratch_shapes=[pltpu.VMEM((DN_H, HEAD, HEAD), F32)],
        compiler_params=_params(("arbitrary",)),
    )(dmixed, o, proj, norm_g, w, qg, kd, ai, egl)


def _dn_local_bwd(q, k, v, gb, gbt, t, vn, st, dst, do, dvn):
    s = q.shape[0]
    cpb = 4 if (s // CH) % 4 == 0 else 1
    tb = cpb * CH
    nblk = s // tb

    def body(q_ref, k_ref, v_ref, gb_ref, gbt_ref, t_ref, vn_ref, st_ref, dst_ref, do_ref, dvn_ref,
             dq_ref, dk_ref, dv_ref, dgb_ref):
        lane = _iota2((CH, LANE), 1)
        last = _iota2((CH, 1), 0) == CH - 1

        def chunk(c, carry):
            r0 = pl.multiple_of(c * CH, CH)
            rows = pl.ds(r0, CH)
            gbv = gb_ref[rows, :]
            gbt_v = gbt_ref[c]
            strict = _iota2((CH, CH), 0) > _iota2((CH, CH), 1)
            qs = [q_ref[rows, _hcols(h)] for h in HEADS]
            ks = [k_ref[rows, _hcols(h)] for h in HEADS]
            vs = [v_ref[rows, _hcols(h)] for h in HEADS]
            ms = [_chunk_scalings(ks[h], vs[h], gbv, gbt_v, h) for h in HEADS]
            sts = [st_ref[c, h] for h in HEADS]
            dsn = [dst_ref[c, h] for h in HEADS]
            dob = [do_ref[rows, _hcols(h)].astype(BF16) for h in HEADS]
            dvnb = [dvn_ref[rows, _hcols(h)].astype(BF16) for h in HEADS]
            vnb = [vn_ref[rows, _hcols(h)].astype(BF16) for h in HEADS]
            tbf = [t_ref[h, rows, :].astype(BF16) for h in HEADS]
            sc = [_chunk_scores(ms[h], qs[h], ks[h]) for h in HEADS]
            xs_ = [_mm_nt(jnp.concatenate([dob[h], dvnb[h]], axis=0), sts[h]) for h in HEADS]
            dai = [_mm_nt(dob[h], vnb[h]) for h in HEADS]
            dkd = [_mm_nt(vnb[h], dsn[h]) for h in HEADS]
            dqg = [xs_[h][0:CH] for h in HEADS]
            duw = [jnp.concatenate([dvnb[h], (-xs_[h][CH:2 * CH]).astype(BF16)], axis=1) for h in HEADS]
            dt = [_mm_nt(duw[h], jnp.concatenate([ms[h]["vb"], ms[h]["kbe"]], axis=1)) for h in HEADS]
            dvk = [_mm_tn(tbf[h], duw[h]) for h in HEADS]
            tdt = [_mm_tn(tbf[h], dt[h]) for h in HEADS]
            da = [jnp.where(strict, -_mm_nt(tdt[h], tbf[h]), 0.0) for h in HEADS]
            dsc = [jnp.concatenate([da[h] * ms[h]["dec"], dai[h] * ms[h]["dec"]], axis=0) for h in HEADS]
            dkq = [_mm(dsc[h], ks[h]) for h in HEADS]
            dk1 = [_mm_tn(dsc[h], jnp.concatenate([ms[h]["kb"], qs[h]], axis=0)) for h in HEADS]
            dgb = jnp.zeros((CH, LANE), F32)
            for h in HEADS:
                m = ms[h]
                eg, ekd, beta = m["eg"], m["ekd"], m["beta"]
                dvb = dvk[h][:, 0:HEAD]
                dkbe = dvk[h][:, HEAD:2 * HEAD]
                kd = ks[h] * ekd
                dkb = dkq[h][0:CH] + dkbe * eg
                dq_ref[rows, _hcols(h)] = dkq[h][CH:2 * CH] + dqg[h] * eg
                dk_ref[rows, _hcols(h)] = dk1[h] + dkd[h] * ekd + dkb * beta
                dv_ref[rows, _hcols(h)] = dvb * beta
                dkd_kd = dkd[h] * kd
                dgl = jnp.exp(m["gl"]) * _rowsum(_colsum(sts[h] * dsn[h])) + _rowsum(_colsum(dkd_kd))
                mm_ = da[h] * sc[h][0] + dai[h] * sc[h][1]
                dgc = (_rowsum(mm_ - mm_.T) + _rowsum(dqg[h] * qs[h] * eg - dkd_kd + dkbe * m["kbe"])
                       + jnp.where(last, dgl, 0.0))
                dbeta = _rowsum(dkb * ks[h] + dvb * vs[h])
                dgb = jnp.where(lane == h, dgc, jnp.where(lane == DN_H + h, dbeta, dgb))
            dgb_ref[rows, :] = dgb
            return carry

        lax.fori_loop(0, cpb, chunk, 0)

    row = pl.BlockSpec((tb, DN_W), lambda i: (i, 0))
    gbs = pl.BlockSpec((tb, LANE), lambda i: (i, 0))
    sts = pl.BlockSpec((cpb, DN_H, HEAD, HEAD), lambda i: (i, 0, 0, 0))
    return pl.pallas_call(
        body, name="dn_local_bwd", grid=(nblk,),
        in_specs=[row, row, row, gbs, pl.BlockSpec((cpb, 2 * DN_H, CH), lambda i: (i, 0, 0)),
                  pl.BlockSpec((DN_H, tb, CH), lambda i: (0, i, 0)), row, sts, sts, row, row],
        out_specs=[row, row, row, gbs],
        out_shape=[_sds((s, DN_W)), _sds((s, DN_W)), _sds((s, DN_W)), _sds((s, LANE))],
        compiler_params=_params(("parallel",)),
    )(q, k, v, gb, gbt, t, vn, st, dst, do, dvn)


def _dn_pre_bwd(proj, yc_all, ab, conv_w, alog_row, dt_row, dq, dk, dv, dgb):
    s = proj.shape[0]
    tm = _tile(s, (256, 128))
    w3 = 3 * DN_W
    nblk = s // tm

    def body(x_ref, yc_ref, ab_ref, cw_ref, al_ref, dt_ref, dq_ref, dk_ref, dv_ref, dgb_ref,
             dx_ref, dab_ref, dcw_ref, dal_ref, ddt_ref, exd, carry):
        i = pl.program_id(0)

        @pl.when(i == 0)
        def _():
            carry[...] = jnp.zeros_like(carry)
            dcw_ref[...] = jnp.zeros_like(dcw_ref)
            dal_ref[...] = jnp.zeros_like(dal_ref)
            ddt_ref[...] = jnp.zeros_like(ddt_ref)

        yc = yc_ref[...]
        sg = jax.nn.sigmoid(yc)
        act = yc * sg
        dact = sg * (1.0 + yc * (1.0 - sg))
        for h in range(DN_H):
            cs = slice(h * HEAD, (h + 1) * HEAD)
            ks = slice(DN_W + h * HEAD, DN_W + (h + 1) * HEAD)
            qa = act[:, cs]
            rq = lax.rsqrt(_rowsum(qa * qa) + EPS)
            qh = qa * rq
            dqv = dq_ref[:, cs]
            exd[0:tm, cs] = (HEAD ** -0.5) * rq * (dqv - qh * _rowsum(dqv * qh)) * dact[:, cs]
            ka = act[:, ks]
            rk = lax.rsqrt(_rowsum(ka * ka) + EPS)
            kh = ka * rk
            dkv = dk_ref[:, cs]
            exd[0:tm, ks] = rk * (dkv - kh * _rowsum(dkv * kh)) * dact[:, ks]
        exd[0:tm, 2 * DN_W:w3] = dv_ref[...] * dact[:, 2 * DN_W:w3]
        xv = x_ref[...]
        dyc = exd[...]
        cat = jnp.concatenate([dyc[tm - HALO:tm], carry[...]], axis=0)
        dcw_ref[DN_K - 1:DN_K, :] += _colsum(dyc * xv)
        dx = cw_ref[DN_K - 1:DN_K, :] * dyc
        for t in range(DN_K - 1):
            ahead = DN_K - 1 - t
            view = jnp.concatenate([pltpu.roll(dyc, tm - ahead, 0)[0:tm - HALO],
                                    pltpu.roll(cat, 2 * HALO - ahead, 0)[0:HALO]], axis=0)
            dcw_ref[t:t + 1, :] += _colsum(view * xv)
            dx += cw_ref[t:t + 1, :] * view
        dx_ref[...] = dx.astype(BF16)
        carry[...] = dyc[0:HALO]

        lane = _iota2((tm, LANE), 1)
        dgbv = dgb_ref[...]
        dg = _mm_hi(_chunk_tri(tm, True), jnp.where(lane < DN_H, dgbv, 0.0))
        abv = ab_ref[...]
        xa = abv + dt_ref[...]
        nea = -jnp.exp(al_ref[...])
        d_da = jnp.where(lane < DN_H, dg * nea * jax.nn.sigmoid(xa), 0.0)
        dal_ref[...] += _colsum(jnp.where(lane < DN_H, dg * nea * _softplus(xa), 0.0))
        ddt_ref[...] += _colsum(d_da)
        beta = jax.nn.sigmoid(abv)
        d_db = jnp.where((lane >= DN_H) & (lane < 2 * DN_H), dgbv * beta * (1.0 - beta), 0.0)
        dab_ref[...] = (d_da + d_db).astype(BF16)

    rev = lambda i: (nblk - 1 - i, 0)
    row = lambda w: pl.BlockSpec((tm, w), rev)
    vec = pl.BlockSpec((1, LANE), lambda i: (0, 0))
    cws = pl.BlockSpec((DN_K, w3), lambda i: (0, 0))
    return pl.pallas_call(
        body, name="dn_pre_bwd", grid=(nblk,),
        in_specs=[row(w3), row(w3), row(LANE), cws, vec, vec, row(DN_W), row(DN_W), row(DN_W), row(LANE)],
        out_specs=[row(w3), row(LANE), cws, vec, vec],
        out_shape=[_sds((s, w3), BF16), _sds((s, LANE), BF16), _sds((DN_K, w3)), _sds((1, LANE)), _sds((1, LANE))],
        scratch_shapes=[pltpu.VMEM((tm, w3), F32), pltpu.VMEM((HALO, w3), F32)],
        compiler_params=_params(("arbitrary",)),
    )(proj, yc_all, ab, conv_w, alog_row, dt_row, dq, dk, dv, dgb)


def _adam(parts, w, m, v, name):
    r, c = w.shape
    n_parts = parts.shape[0]
    small = n_parts * r * c * 4 <= 4 * 1024 * 1024
    tr = r if small else _tile(r, (128, 64, 32, 16, 8))

    def body(p_ref, w_ref, m_ref, v_ref, g_ref, d_ref, nm_ref, nv_ref):
        g = p_ref[0].astype(F32)
        for k in range(1, n_parts):
            g = g + p_ref[k].astype(F32)
        g_ref[...] = g
        mn = ADAM_B1 * m_ref[...] + (1.0 - ADAM_B1) * g
        vn = ADAM_B2 * v_ref[...] + (1.0 - ADAM_B2) * (g * g)
        m_hat = mn / (1.0 - ADAM_B1 ** ADAM_STEP)
        v_hat = vn / (1.0 - ADAM_B2 ** ADAM_STEP)
        d_ref[...] = -ADAM_LR * (m_hat / (jnp.sqrt(v_hat) + ADAM_EPS) + ADAM_WD * w_ref[...])
        nm_ref[...] = mn
        nv_ref[...] = vn

    blk = pl.BlockSpec((tr, c), lambda i: (i, 0))
    return pl.pallas_call(
        body, name=name, grid=(r // tr,),
        in_specs=[pl.BlockSpec((n_parts, tr, c), lambda i: (0, i, 0)), blk, blk, blk],
        out_specs=[blk, blk, blk, blk], out_shape=[_sds((r, c))] * 4,
        compiler_params=_params(("parallel",)),
    )(parts, w, m, v)


_PACK_ROWS = 8


def _pack(vals):
    tiles = []
    for a in vals:
        flat = a.reshape(-1).astype(F32)
        unit = _PACK_ROWS * LANE
        n = -(-flat.shape[0] // unit) * unit
        tiles.append(jnp.pad(flat, (0, n - flat.shape[0])).reshape(n // LANE, LANE))
    return jnp.concatenate(tiles, axis=0)


def _unpack(packed, shapes):
    out = []
    r0 = 0
    for shp in shapes:
        size = 1
        for dim in shp:
            size *= dim
        unit = _PACK_ROWS * LANE
        rows = -(-size // unit) * _PACK_ROWS
        out.append(packed[r0:r0 + rows].reshape(-1)[:size].reshape(shp))
        r0 += rows
    return out


def _lane_row(vec8):
    return jnp.pad(vec8.reshape(1, -1).astype(F32), ((0, 0), (0, LANE - vec8.size)))


def kernel(x, mem, ln_g, w_in, gmlp_ln_g, gmlp_ln_b, gmlp_ws, gmlp_bs, conv_w, dn_a_log, dn_dt_bias, dn_norm_g, mem_norm_g, w_mem_kv, w_out, final_g, loss_target, m_ln_g, m_w_in, m_gmlp_ln_g, m_gmlp_ln_b, m_gmlp_ws, m_gmlp_bs, m_conv_w, m_dn_a_log, m_dn_dt_bias, m_dn_norm_g, m_mem_norm_g, m_w_mem_kv, m_w_out, m_final_g, v_ln_g, v_w_in, v_gmlp_ln_g, v_gmlp_ln_b, v_gmlp_ws, v_gmlp_bs, v_conv_w, v_dn_a_log, v_dn_dt_bias, v_dn_norm_g, v_mem_norm_g, v_w_mem_kv, v_w_out, v_final_g):
    xs = x[0]
    mems = mem[0]
    tgt = loss_target[0]
    s, d = xs.shape
    shard_w = w_in.shape[2]
    in_w = N_DEV * shard_w
    me = 4 * lax.axis_index("x") + 2 * lax.axis_index("y") + lax.axis_index("c")

    (g_in,) = _gather_two_level([w_in[0].astype(BF16)], "gather_w_in")
    o_g, o_dn, o_ab = 0, 3 * GMLP_W, 3 * GMLP_W + 4 * DN_W
    o_xa = o_ab + 2 * DN_H

    def shard_cols(lo, hi):
        out = []
        while lo < hi:
            sh = lo // shard_w
            end = min(hi, (sh + 1) * shard_w)
            out.append(g_in[sh][:, lo - sh * shard_w:end - sh * shard_w])
            lo = end
        return out

    w_main = jnp.concatenate(shard_cols(o_dn, o_ab) + shard_cols(o_g, o_dn) + shard_cols(o_xa, in_w), axis=1)
    w_ab = jnp.pad(jnp.concatenate(shard_cols(o_ab, o_xa), axis=1), ((0, 0), (0, LANE - 2 * DN_H)))

    ln_g2 = ln_g.reshape(1, d)
    lng2 = gmlp_ln_g.reshape(1, GMLP_W)
    lnb2 = gmlp_ln_b.reshape(1, GMLP_W)
    ws3 = gmlp_ws[0]
    bs_t = gmlp_bs[0].T
    alog_row = _lane_row(dn_a_log)
    dt_row = _lane_row(dn_dt_bias)
    dn_g2 = dn_norm_g.reshape(1, HEAD)
    mem_g2 = mem_norm_g.reshape(1, d)
    fin_g2 = final_g.reshape(1, d)

    proj, ab, h_t, (g_out, g_kv, g_conv) = _inproj(
        xs, ln_g2, w_main, w_ab, [w_out[0].astype(BF16), w_mem_kv[0].astype(BF16), conv_w[0]])
    wo = g_out.reshape(MIX_W, d)
    wo_perm = jnp.concatenate([wo[GMLP_W:GMLP_W + DN_W], wo[0:GMLP_W], wo[GMLP_W + DN_W:MIX_W]], axis=0)
    w_kv = g_kv.reshape(d, 2 * XA_W)
    conv_full = g_conv.transpose(1, 0, 2).reshape(DN_K, 3 * DN_W)
    out_a = _gmlp_fwd(proj, lng2, lnb2, ws3, bs_t)
    mkv = _memkv_fwd(mems, mem_g2, w_kv)
    out_c = _xattn_fwd(proj, mkv)
    q, k, v, gb, gbt, yc = _dn_pre(proj, ab, conv_full, alog_row, dt_row)
    u, wk, qg, kd, tmat, ai, egl = _dn_local(q, k, v, gb, gbt)
    o, vn, st, out_b = _dn_scan(u, wk, qg, kd, ai, egl, proj, dn_g2)

    dx2, dx2b, dmixed, loss_acc, d_fin_g = _final(xs, tgt, out_b, out_a, out_c, wo_perm, fin_g2)
    loss = lax.psum(loss_acc[0, 0], ("x", "y", "c"))

    dwo_b = _matmul_tn(out_b, dx2b, "dw_out_b")
    dwo_a = _matmul_tn(out_a, dx2b, "dw_out_a")
    dwo_c = _matmul_tn(out_c, dx2b, "dw_out_c")
    d_w_out = jnp.concatenate([dwo_a, dwo_b, dwo_c], axis=0)

    dp_g, d_ws, d_bst, d_lng, d_lnb = _gmlp_bwd(proj, dmixed, lng2, lnb2, ws3, bs_t)
    dp_x, dmkv = _xattn_bwd(proj, dmixed, mkv)
    d_w_kv, d_mem_g = _memkv_bwd(mems, mem_g2, w_kv, dmkv)
    do, dvn, dst, dp_dz, d_dn_g = _dn_scan_bwd(dmixed, o, proj, dn_g2, wk, qg, kd, ai, egl)
    dq, dk, dv, dgb = _dn_local_bwd(q, k, v, gb, gbt, tmat, vn, st, dst, do, dvn)
    dp_qkv, dp_ab, d_conv, d_alog, d_dt = _dn_pre_bwd(proj, yc, ab, conv_full, alog_row, dt_row, dq, dk, dv, dgb)

    dw_qkv = _matmul_acc(h_t, dp_qkv, "dw_in_qkv")
    dw_dz = _matmul_acc(h_t, dp_dz, "dw_in_dz")
    dw_gm = _matmul_acc(h_t, dp_g, "dw_in_gmlp")
    dw_xa = _matmul_acc(h_t, dp_x, "dw_in_xa")
    dw_ab = _matmul_acc(h_t, dp_ab, "dw_in_ab")
    segs = [(o_g, dw_gm), (o_dn, dw_qkv), (o_dn + 3 * DN_W, dw_dz), (o_ab, dw_ab[:, :2 * DN_H]), (o_xa, dw_xa)]
    shards = []
    for sh in range(N_DEV):
        lo, hi = sh * shard_w, (sh + 1) * shard_w
        parts = [arr[:, max(lo, off) - off:min(hi, off + arr.shape[1]) - off] for off, arr in segs
                 if off < hi and off + arr.shape[1] > lo]
        shards.append(jnp.concatenate(parts, axis=1).astype(BF16))
    send_in = jnp.stack(shards)

    small_shapes = [gmlp_ln_g.shape, gmlp_ln_b.shape, gmlp_ws.shape, gmlp_bs.shape, dn_a_log.shape,
                    dn_dt_bias.shape, dn_norm_g.shape, mem_norm_g.shape, final_g.shape, (DN_K, 3 * DN_W)]
    small_g = _pack([d_lng, d_lnb, d_ws, d_bst.T, d_alog[:, :DN_H], d_dt[:, :DN_H], d_dn_g, d_mem_g, d_fin_g, d_conv])
    zc = jnp.zeros((DN_K, 3 * DN_W), F32)
    small_w = _pack([gmlp_ln_g, gmlp_ln_b, gmlp_ws, gmlp_bs, dn_a_log, dn_dt_bias, dn_norm_g, mem_norm_g, final_g, zc])
    small_m = _pack([m_gmlp_ln_g, m_gmlp_ln_b, m_gmlp_ws, m_gmlp_bs, m_dn_a_log, m_dn_dt_bias, m_dn_norm_g,
                     m_mem_norm_g, m_final_g, zc])
    small_v = _pack([v_gmlp_ln_g, v_gmlp_ln_b, v_gmlp_ws, v_gmlp_bs, v_dn_a_log, v_dn_dt_bias, v_dn_norm_g,
                     v_mem_norm_g, v_final_g, zc + 1.0])

    send_out = d_w_out.reshape(N_DEV, MIX_W // N_DEV, d).astype(BF16)
    send_kv = d_w_kv.reshape(N_DEV, d // N_DEV, 2 * XA_W).astype(BF16)
    sends = [send_in, send_out, send_kv]
    all_small, got = _swap_halves(small_g, sends, "swap_halves")
    core = lax.axis_index("c").astype(jnp.int32).reshape(1)
    chip_sums = [_pair_sum(core, sends[i], got[i], "pair_sum_%d" % i) for i in range(3)]
    grad_x, d_ln_g, (r_in, r_out, r_kv) = _dh_rms(
        [dp_qkv, dp_dz, dp_g, dp_x, dp_ab], w_main, w_ab, xs, dx2, ln_g2, chip_sums)
    (all_ln_g,) = _gather_two_level([_pack([d_ln_g])], "gather_ln_g")

    g_w_in, dl_w_in, nm_w_in, nv_w_in = _adam(r_in, w_in[0], m_w_in[0], v_w_in[0], "adam_w_in")
    g_w_out, dl_w_out, nm_w_out, nv_w_out = _adam(r_out, w_out[0], m_w_out[0], v_w_out[0], "adam_w_out")
    g_w_kv, dl_w_kv, nm_w_kv, nv_w_kv = _adam(r_kv, w_mem_kv[0], m_w_mem_kv[0], v_w_mem_kv[0], "adam_w_kv")
    sm = [_unpack(t, small_shapes) for t in _adam(all_small, small_w, small_m, small_v, "adam_small")]
    ln_res = [_unpack(t, [ln_g.shape])[0]
              for t in _adam(all_ln_g, _pack([ln_g]), _pack([m_ln_g]), _pack([v_ln_g]), "adam_ln_g")]

    conv_parts = lax.dynamic_slice(all_small, (0, all_small.shape[1] - (DN_K * 3 * DN_W) // LANE, 0),
                                   (N_DEV, (DN_K * 3 * DN_W) // LANE, LANE)).reshape(N_DEV, DN_K, 3 * DN_W)
    cshard = conv_w.shape[2]
    conv_parts = lax.dynamic_slice(conv_parts, (0, 0, me * cshard), (N_DEV, DN_K, cshard))
    cpad = ((0, 0), (0, HALO - DN_K), (0, 0))
    conv_res = _adam(jnp.pad(conv_parts, cpad), jnp.pad(conv_w[0], cpad[1:]), jnp.pad(m_conv_w[0], cpad[1:]),
                     jnp.pad(v_conv_w[0], cpad[1:], constant_values=1.0), "adam_conv")
    g_conv_s, dl_conv, nm_conv, nv_conv = [t[:DN_K][None] for t in conv_res]

    def group(idx, big_in, big_conv, big_kv, big_out):
        names = sm[idx]
        return [ln_res[idx], big_in[None], names[0], names[1], names[2], names[3], big_conv, names[4], names[5], names[6],
                names[7], big_kv[None], big_out[None], names[8]]

    grads = group(0, g_w_in, g_conv_s, g_w_kv, g_w_out)
    deltas = group(1, dl_w_in, dl_conv, dl_w_kv, dl_w_out)
    new_m = group(2, nm_w_in, nm_conv, nm_w_kv, nm_w_out)
    new_v = group(3, nv_w_in, nv_conv, nv_w_kv, nv_w_out)
    return (loss, grad_x[None], *grads, *deltas, *new_m, *new_v)
```

```python
import functools

import jax
import jax.numpy as jnp
from jax import lax
from jax.experimental import pallas as pl
from jax.experimental.pallas import tpu as pltpu

F32 = jnp.float32
BF16 = jnp.bfloat16
HIGHEST = lax.Precision.HIGHEST
MESH_ID = pl.DeviceIdType.MESH

N_DEV = 8
EPS = 1e-6
GMLP_W = 512
GMLP_G = 4
GMLP_T = 128
DN_W = 1024
DN_H = 8
HEAD = 128
DN_K = 4
CH = 64
XA_W = 512
XA_H = 4
LANE = 128
HALO = 8
MAIN_W = 4 * DN_W + 3 * GMLP_W + 2 * XA_W
MIX_W = DN_W + GMLP_W + XA_W
VMEM_LIMIT = 56 * 1024 * 1024

ADAM_LR = 0.001
ADAM_B1 = 0.9
ADAM_B2 = 0.999
ADAM_EPS = 1e-08
ADAM_WD = 0.01
ADAM_STEP = 10


def _sds(shape, dtype=F32):
    return jax.ShapeDtypeStruct(tuple(shape), dtype)


def _params(sem=None):
    if sem is None:
        return pltpu.CompilerParams(vmem_limit_bytes=VMEM_LIMIT)
    return pltpu.CompilerParams(dimension_semantics=tuple(sem), vmem_limit_bytes=VMEM_LIMIT)


def _tile(n, prefs):
    for p in prefs:
        if n % p == 0:
            return p
    return n


def _mm(a, b):
    return jnp.dot(a.astype(BF16), b.astype(BF16), preferred_element_type=F32)


def _mm_nt(a, b):
    return lax.dot_general(a.astype(BF16), b.astype(BF16), (((1,), (1,)), ((), ())), preferred_element_type=F32)


def _mm_tn(a, b):
    return lax.dot_general(a.astype(BF16), b.astype(BF16), (((0,), (0,)), ((), ())), preferred_element_type=F32)


def _mm_hi(a, b):
    return jnp.dot(a, b, precision=HIGHEST, preferred_element_type=F32)


def _mm_3x(a, b):
    return jnp.dot(a, b, precision=lax.Precision.HIGH, preferred_element_type=F32)


_GELU_C = 0.7978845608028654
_GELU_A = 0.044715


def _gelu(x):
    return 0.5 * x * (1.0 + jnp.tanh(_GELU_C * (x + _GELU_A * x * x * x)))


def _gelu_grad(x):
    t = jnp.tanh(_GELU_C * (x + _GELU_A * x * x * x))
    return 0.5 * (1.0 + t) + 0.5 * x * (1.0 - t * t) * _GELU_C * (1.0 + 3.0 * _GELU_A * x * x)


def _silu(x):
    return x * jax.nn.sigmoid(x)


def _silu_grad(x):
    s = jax.nn.sigmoid(x)
    return s * (1.0 + x * (1.0 - s))


def _rowsum(x):
    return jnp.sum(x, axis=-1, keepdims=True)


def _colsum(x):
    return jnp.sum(x, axis=0, keepdims=True)


def _iota2(shape, dim):
    return lax.broadcasted_iota(jnp.int32, shape, dim)


def _chunk_tri(tm, upper):
    r = _iota2((tm, tm), 0)
    c = _iota2((tm, tm), 1)
    same = lax.shift_right_logical(r, 6) == lax.shift_right_logical(c, 6)
    tri = (r <= c) if upper else (r >= c)
    return jnp.where(same & tri, 1.0, 0.0).astype(F32)


N_CHIP = 4


def _mesh_place():
    x, y, c = lax.axis_index("x"), lax.axis_index("y"), lax.axis_index("c")
    chips = [(1 - x, y), (x, 1 - y), (1 - x, 1 - y)]
    return x, y, c, (x, y, 1 - c), chips


class _Gather:
    def __init__(self, ins, outs, send_sems, recv_sems, loc_sems):
        self.ins, self.outs, self.send_sems, self.recv_sems, self.loc_sems = ins, outs, send_sems, recv_sems, loc_sems
        self.x, self.y, self.c, self.sib, self.chips = _mesh_place()
        self.me = (self.x, self.y, self.c)

    def copy(self, a, k, block, to, src=None):
        slot = self.outs[a].at[4 * block[0] + 2 * block[1] + block[2]]
        return pltpu.make_async_remote_copy(
            src_ref=slot if src is None else src, dst_ref=slot, send_sem=self.send_sems.at[a, k],
            recv_sem=self.recv_sems.at[a, k], device_id=to, device_id_type=MESH_ID)

    def own(self, a):
        return pltpu.make_async_copy(self.ins[a], self.outs[a].at[4 * self.x + 2 * self.y + self.c], self.loc_sems.at[a])

    def first(self, a):
        return [self.copy(a, 0, self.me, self.sib, src=self.ins[a])] + [
            self.copy(a, 1 + j, self.me, (*chip, self.c), src=self.ins[a]) for j, chip in enumerate(self.chips)]

    def passed(self, a, j):
        return self.copy(a, 4 + j, (*self.chips[j], self.c), self.sib)

    def start(self):
        for a in range(len(self.ins)):
            self.own(a).start()
            for cp in self.first(a):
                cp.start()

    def finish(self):
        n = len(self.ins)
        for a in range(n):
            for j, chip in enumerate(self.chips):
                self.copy(a, 1 + j, (*chip, self.c), self.me).wait_recv()
                self.passed(a, j).start()
        for a in range(n):
            self.copy(a, 0, self.sib, self.me).wait_recv()
            for j, chip in enumerate(self.chips):
                self.copy(a, 4 + j, (*chip, 1 - self.c), self.me).wait_recv()
        for a in range(n):
            for cp in self.first(a) + [self.passed(a, j) for j in range(N_CHIP - 1)]:
                cp.wait_send()
            self.own(a).wait()

    @staticmethod
    def sems(n):
        return [pltpu.SemaphoreType.DMA((n, N_DEV - 1)), pltpu.SemaphoreType.DMA((n, N_DEV - 1)),
                pltpu.SemaphoreType.DMA((n,))]


def _gather_two_level(arrs, name):
    n = len(arrs)

    def body(*refs):
        g = _Gather(refs[:n], refs[n:2 * n], *refs[2 * n:])
        g.start()
        g.finish()

    any_spec = pl.BlockSpec(memory_space=pl.ANY)
    return pl.pallas_call(
        body, name=name, out_shape=[_sds((N_DEV,) + a.shape, a.dtype) for a in arrs],
        in_specs=[any_spec] * n, out_specs=[any_spec] * n, scratch_shapes=_Gather.sems(n),
        compiler_params=pltpu.CompilerParams(has_side_effects=True),
    )(*arrs)


def _swap_halves(small, grads, name):
    n = len(grads)

    def body(*refs):
        small_ref = refs[0]
        ins = refs[1:1 + n]
        small_out = refs[1 + n]
        got = refs[2 + n:2 + 2 * n]
        s_send, s_recv, g_send, g_recv, loc_sem = refs[2 + 2 * n:]
        x, y, c, sib, _ = _mesh_place()
        me = 4 * x + 2 * y + c
        sends, recvs = [], []
        for j in range(1, N_DEV):
            px = 1 - x if (j >> 2) & 1 else x
            py = 1 - y if (j >> 1) & 1 else y
            pc = 1 - c if j & 1 else c
            cp = pltpu.make_async_remote_copy(
                src_ref=small_ref, dst_ref=small_out.at[me], send_sem=s_send.at[j - 1], recv_sem=s_recv.at[j - 1],
                device_id=(px, py, pc), device_id_type=MESH_ID)
            cp.start()
            sends.append(cp)
            recvs.append(pltpu.make_async_remote_copy(
                src_ref=small_ref, dst_ref=small_out.at[4 * px + 2 * py + pc], send_sem=s_send.at[j - 1],
                recv_sem=s_recv.at[j - 1], device_id=(px, py, pc), device_id_type=MESH_ID))
        own = pltpu.make_async_copy(small_ref, small_out.at[me], loc_sem)
        own.start()
        for a in range(n):
            for chip in range(N_CHIP):
                cp = pltpu.make_async_remote_copy(
                    src_ref=ins[a].at[2 * chip + 1 - c], dst_ref=got[a].at[chip], send_sem=g_send.at[a, chip],
                    recv_sem=g_recv.at[a, chip], device_id=sib, device_id_type=MESH_ID)
                cp.start()
                sends.append(cp)
                recvs.append(cp)
        for cp in sends:
            cp.wait_send()
        for cp in recvs:
            cp.wait_recv()
        own.wait()

    half = [_sds((N_CHIP,) + g.shape[1:], g.dtype) for g in grads]
    any_spec = pl.BlockSpec(memory_space=pl.ANY)
    res = pl.pallas_call(
        body, name=name, out_shape=[_sds((N_DEV,) + small.shape, small.dtype)] + half,
        in_specs=[any_spec] * (1 + n), out_specs=[any_spec] * (1 + n),
        scratch_shapes=[pltpu.SemaphoreType.DMA((N_DEV - 1,)), pltpu.SemaphoreType.DMA((N_DEV - 1,)),
                        pltpu.SemaphoreType.DMA((n, N_CHIP)), pltpu.SemaphoreType.DMA((n, N_CHIP)),
                        pltpu.SemaphoreType.DMA],
        compiler_params=pltpu.CompilerParams(has_side_effects=True),
    )(small, *grads)
    return res[0], res[1:]


def _pair_sum(core, mine, got, name):
    nc, r, c = got.shape
    tr = _tile(r, (256, 128, 64, 32, 16))

    def body(core_ref, a_ref, b_ref, o_ref):
        o_ref[...] = (a_ref[...].astype(F32) + b_ref[...].astype(F32)).astype(BF16)

    return pl.pallas_call(
        body, name=name, out_shape=_sds(got.shape, BF16),
        grid_spec=pltpu.PrefetchScalarGridSpec(
            num_scalar_prefetch=1, grid=(nc, r // tr),
            in_specs=[pl.BlockSpec((1, tr, c), lambda i, j, core_ref: (2 * i + core_ref[0], j, 0)),
                      pl.BlockSpec((1, tr, c), lambda i, j, core_ref: (i, j, 0))],
            out_specs=pl.BlockSpec((1, tr, c), lambda i, j, core_ref: (i, j, 0))),
        compiler_params=_params(("parallel", "parallel")),
    )(core, mine, got)


class _ChipExchange:
    def __init__(self, ins, outs, send_sems, recv_sems, loc_sems):
        self.ins, self.outs, self.send_sems, self.recv_sems, self.loc_sems = ins, outs, send_sems, recv_sems, loc_sems
        self.x, self.y, self.c, _, self.chips = _mesh_place()
        self.mine = 2 * self.x + self.y

    def own(self, a):
        return pltpu.make_async_copy(self.ins[a].at[self.mine], self.outs[a].at[self.mine], self.loc_sems.at[a])

    def copy(self, a, j, lands_in):
        chip = self.chips[j]
        return pltpu.make_async_remote_copy(
            src_ref=self.ins[a].at[2 * chip[0] + chip[1]], dst_ref=self.outs[a].at[lands_in],
            send_sem=self.send_sems.at[a, j], recv_sem=self.recv_sems.at[a, j], device_id=(*chip, self.c),
            device_id_type=MESH_ID)

    def start(self):
        for a in range(len(self.ins)):
            self.own(a).start()
            for j in range(N_CHIP - 1):
                self.copy(a, j, self.mine).start()

    def finish(self):
        for a in range(len(self.ins)):
            for j, chip in enumerate(self.chips):
                self.copy(a, j, self.mine).wait_send()
                self.copy(a, j, 2 * chip[0] + chip[1]).wait_recv()
            self.own(a).wait()

    @staticmethod
    def sems(n):
        return [pltpu.SemaphoreType.DMA((n, N_CHIP - 1)), pltpu.SemaphoreType.DMA((n, N_CHIP - 1)),
                pltpu.SemaphoreType.DMA((n,))]


def _inproj(x, ln_g, w_main, w_ab, late):
    s, d = x.shape
    n = w_main.shape[1]
    tm = _tile(s, (512, 256, 128))
    tn = _tile(n, (1664, 512, 128))
    nl = len(late)
    ni, nj = s // tm, n // tn

    def body(*refs):
        x_ref, g_ref, w_ref, wab_ref = refs[:4]
        proj_ref, ab_ref, ht_ref = refs[4 + nl:7 + nl]
        hs = refs[7 + 2 * nl]
        gather = _Gather(refs[4:4 + nl], refs[7 + nl:7 + 2 * nl], *refs[8 + 2 * nl:])
        step = pl.program_id(0) * nj + pl.program_id(1)

        @pl.when(step == 0)
        def _():
            gather.start()

        @pl.when(pl.program_id(1) == 0)
        def _():
            xv = x_ref[...]
            r = lax.rsqrt(jnp.mean(xv * xv, axis=-1, keepdims=True) + EPS)
            hf = xv * r * g_ref[...]
            h = hf.astype(BF16)
            hs[...] = h
            ht_ref[...] = hf.T.astype(BF16)
            ab_ref[...] = jnp.dot(h, wab_ref[...], preferred_element_type=F32)

        proj_ref[...] = jnp.dot(hs[...], w_ref[...], preferred_element_type=F32)

        @pl.when(step == ni * nj - 1)
        def _():
            gather.finish()

    any_spec = pl.BlockSpec(memory_space=pl.ANY)
    res = pl.pallas_call(
        body, name="inproj", grid=(ni, nj),
        in_specs=[pl.BlockSpec((tm, d), lambda i, j: (i, 0)), pl.BlockSpec((1, d), lambda i, j: (0, 0)),
                  pl.BlockSpec((d, tn), lambda i, j: (0, j)), pl.BlockSpec((d, LANE), lambda i, j: (0, 0))]
        + [any_spec] * nl,
        out_specs=[pl.BlockSpec((tm, tn), lambda i, j: (i, j)), pl.BlockSpec((tm, LANE), lambda i, j: (i, 0)),
                   pl.BlockSpec((d, tm), lambda i, j: (0, i))] + [any_spec] * nl,
        out_shape=[_sds((s, n)), _sds((s, LANE)), _sds((d, s), BF16)]
        + [_sds((N_DEV,) + a.shape, a.dtype) for a in late],
        scratch_shapes=[pltpu.VMEM((tm, d), BF16)] + _Gather.sems(nl),
        compiler_params=_params(("arbitrary", "arbitrary")),
    )(x, ln_g, w_main, w_ab, *late)
    return res[0], res[1], res[2], res[3:]


def _matmul_acc(a, b, name):
    m, k = a.shape
    n = b.shape[1]
    tm = _tile(m, (2048, 1024, 512, 256, 128))
    tn = _tile(n, (1024, 512, 256, 128))
    tk = _tile(k, (1024, 512, 256, 128))

    def body(a_ref, b_ref, o_ref):
        @pl.when(pl.program_id(2) == 0)
        def _():
            o_ref[...] = jnp.zeros_like(o_ref)

        o_ref[...] += jnp.dot(a_ref[...], b_ref[...], preferred_element_type=F32)

    return pl.pallas_call(
        body, name=name, grid=(m // tm, n // tn, k // tk),
        in_specs=[pl.BlockSpec((tm, tk), lambda i, j, l: (i, l)), pl.BlockSpec((tk, tn), lambda i, j, l: (l, j))],
        out_specs=pl.BlockSpec((tm, tn), lambda i, j, l: (i, j)),
        out_shape=_sds((m, n)),
        compiler_params=_params(("parallel", "parallel", "arbitrary")),
    )(a, b)


def _matmul_tn(a, b, name):
    k, m = a.shape
    n = b.shape[1]
    tm = _tile(m, (1024, 512, 256, 128))
    tn = _tile(n, (1024, 512, 256, 128))
    tk = _tile(k, (1024, 512, 256, 128))

    def body(a_ref, b_ref, o_ref):
        @pl.when(pl.program_id(2) == 0)
        def _():
            o_ref[...] = jnp.zeros_like(o_ref)

        o_ref[...] += _mm_tn(a_ref[...], b_ref[...])

    return pl.pallas_call(
        body, name=name, grid=(m // tm, n // tn, k // tk),
        in_specs=[pl.BlockSpec((tk, tm), lambda i, j, l: (l, i)), pl.BlockSpec((tk, tn), lambda i, j, l: (l, j))],
        out_specs=pl.BlockSpec((tm, tn), lambda i, j, l: (i, j)),
        out_shape=_sds((m, n)),
        compiler_params=_params(("parallel", "parallel", "arbitrary")),
    )(a, b)


def _dh_rms(pieces, w_main, w_ab, x, dx2, ln_g, chip_sums):
    s, d = x.shape
    npc = len(pieces)
    nx = len(chip_sums)
    tm = _tile(s, (256, 128))
    ni = s // tm
    widths = [p.shape[1] for p in pieces[:-1]]
    offs = [sum(widths[:p]) for p in range(npc - 1)]

    def body(*refs):
        p_refs = refs[:npc]
        w_ref, wab_ref, x_ref, dx2_ref, g_ref = refs[npc:npc + 5]
        gx_ref, dg_ref = refs[npc + 5 + nx:npc + 7 + nx]
        exch = _ChipExchange(refs[npc + 5:npc + 5 + nx], refs[npc + 7 + nx:npc + 7 + 2 * nx], *refs[npc + 7 + 2 * nx:])
        step = pl.program_id(0)

        @pl.when(step == 0)
        def _():
            dg_ref[...] = jnp.zeros_like(dg_ref)
            exch.start()

        dhv = _mm_nt(p_refs[npc - 1][...], wab_ref[...])
        for p in range(npc - 1):
            dhv += _mm_nt(p_refs[p][...], w_ref[:, offs[p]:offs[p] + widths[p]])
        xv = x_ref[...]
        r = lax.rsqrt(jnp.mean(xv * xv, axis=-1, keepdims=True) + EPS)
        xhat = xv * r
        dg_ref[...] += _colsum(dhv * xhat)
        dxh = dhv * g_ref[...]
        gx_ref[...] = dx2_ref[...] + r * (dxh - xhat * jnp.mean(dxh * xhat, axis=-1, keepdims=True))

        @pl.when(step == ni - 1)
        def _():
            exch.finish()

    any_spec = pl.BlockSpec(memory_space=pl.ANY)
    row = pl.BlockSpec((tm, d), lambda i: (i, 0))
    vec = pl.BlockSpec((1, d), lambda i: (0, 0))
    once = lambda a: pl.BlockSpec(a.shape, lambda i: (0, 0), pipeline_mode=pl.Buffered(1))
    in_specs = [pl.BlockSpec((tm, p.shape[1]), lambda i: (i, 0)) for p in pieces]
    in_specs += [once(w_main), once(w_ab), row, row, vec] + [any_spec] * nx
    res = pl.pallas_call(
        body, name="dh_rms", grid=(ni,), in_specs=in_specs,
        out_specs=[row, vec] + [any_spec] * nx,
        out_shape=[_sds((s, d)), _sds((1, d))] + [_sds(p.shape, p.dtype) for p in chip_sums],
        scratch_shapes=_ChipExchange.sems(nx),
        compiler_params=_params(("arbitrary",)),
    )(*pieces, w_main, w_ab, x, dx2, ln_g, *chip_sums)
    return res[0], res[1], res[2:]


def _final(x, tgt, out_b, out_a, out_c, w_out, final_g):
    s, d = x.shape
    tm = _tile(s, (256, 128))

    def body(x_ref, t_ref, b_ref, a_ref, c_ref, w_ref, g_ref, dx2_ref, dx2b_ref, dm_ref, loss_ref, dg_ref):
        @pl.when(pl.program_id(0) == 0)
        def _():
            loss_ref[...] = jnp.zeros_like(loss_ref)
            dg_ref[...] = jnp.zeros_like(dg_ref)

        x2 = x_ref[...]
        x2 += jnp.dot(b_ref[...], w_ref[0:DN_W, :], preferred_element_type=F32)
        x2 += jnp.dot(a_ref[...], w_ref[DN_W:DN_W + GMLP_W, :], preferred_element_type=F32)
        x2 += jnp.dot(c_ref[...], w_ref[DN_W + GMLP_W:MIX_W, :], preferred_element_type=F32)
        r = lax.rsqrt(jnp.mean(x2 * x2, axis=-1, keepdims=True) + EPS)
        xhat = x2 * r
        g = g_ref[...]
        err = xhat * g - t_ref[...]
        tok = 0.5 * jnp.mean(err * err, axis=-1, keepdims=True)
        loss_ref[...] += jnp.broadcast_to(_colsum(tok), loss_ref.shape)
        dy = err * (1.0 / d)
        dg_ref[...] += _colsum(dy * xhat)
        dxh = dy * g
        dx2 = r * (dxh - xhat * jnp.mean(dxh * xhat, axis=-1, keepdims=True))
        dx2_ref[...] = dx2
        dx2b = dx2.astype(BF16)
        dx2b_ref[...] = dx2b
        dm_ref[...] = _mm_nt(dx2b, w_ref[...])

    row = pl.BlockSpec((tm, d), lambda i: (i, 0))
    vec = pl.BlockSpec((1, d), lambda i: (0, 0))
    return pl.pallas_call(
        body, name="final", grid=(s // tm,),
        in_specs=[row, row, pl.BlockSpec((tm, DN_W), lambda i: (i, 0)), pl.BlockSpec((tm, GMLP_W), lambda i: (i, 0)),
                  pl.BlockSpec((tm, XA_W), lambda i: (i, 0)), pl.BlockSpec((MIX_W, d), lambda i: (0, 0)), vec],
        out_specs=[row, row, pl.BlockSpec((tm, MIX_W), lambda i: (i, 0)), pl.BlockSpec((1, LANE), lambda i: (0, 0)), vec],
        out_shape=[_sds((s, d)), _sds((s, d), BF16), _sds((s, MIX_W)), _sds((1, LANE)), _sds((1, d))],
        compiler_params=_params(("arbitrary",)),
    )(x, tgt, out_b, out_a, out_c, w_out, final_g)


GU_BLK = (4 * DN_W) // GMLP_W


def _gmlp_norm(gv, lng, lnb):
    va = _gelu(gv)
    mu = jnp.mean(va, axis=-1, keepdims=True)
    xc = va - mu
    rstd = lax.rsqrt(jnp.mean(xc * xc, axis=-1, keepdims=True) + EPS)
    vhat = xc * rstd
    return vhat, rstd, vhat * lng + lnb


def _gmlp_fwd(proj, lng, lnb, ws, bs_t):
    s = proj.shape[0]
    tm = _tile(s, (512, 256, 128))

    def body(u_ref, v_ref, z_ref, lng_ref, lnb_ref, ws_ref, bst_ref, o_ref):
        _, _, vn = _gmlp_norm(v_ref[...], lng_ref[...], lnb_ref[...])
        tri = _iota2((GMLP_T, GMLP_T), 0) >= _iota2((GMLP_T, GMLP_T), 1)
        for g in range(GMLP_G):
            cs = slice(g * HEAD, (g + 1) * HEAD)
            w = jnp.where(tri, ws_ref[g], 0.0).astype(BF16)
            b = bst_ref[:, g:g + 1]
            for c in range(tm // GMLP_T):
                rs = slice(c * GMLP_T, (c + 1) * GMLP_T)
                sg = _mm(w, vn[rs, cs]) + b
                o_ref[rs, cs] = (_gelu(u_ref[rs, cs]) * sg * _silu(z_ref[rs, cs])).astype(BF16)

    col = lambda k: pl.BlockSpec((tm, GMLP_W), lambda i: (i, GU_BLK + k))
    vec = pl.BlockSpec((1, GMLP_W), lambda i: (0, 0))
    return pl.pallas_call(
        body, name="gmlp_fwd", grid=(s // tm,),
        in_specs=[col(0), col(1), col(2), vec, vec, pl.BlockSpec((GMLP_G, GMLP_T, GMLP_T), lambda i: (0, 0, 0)),
                  pl.BlockSpec((GMLP_T, GMLP_G), lambda i: (0, 0))],
        out_specs=pl.BlockSpec((tm, GMLP_W), lambda i: (i, 0)), out_shape=_sds((s, GMLP_W), BF16),
        compiler_params=_params(("parallel",)),
    )(proj, proj, proj, lng, lnb, ws, bs_t)


def _gmlp_bwd(proj, dmixed, lng, lnb, ws, bs_t):
    s = proj.shape[0]
    tm = _tile(s, (512, 256, 128))

    def body(u_ref, v_ref, z_ref, d_ref, lng_ref, lnb_ref, ws_ref, bst_ref,
             dp_ref, dws_ref, dbst_ref, dlng_ref, dlnb_ref, dvn):
        @pl.when(pl.program_id(0) == 0)
        def _():
            dws_ref[...] = jnp.zeros_like(dws_ref)
            dbst_ref[...] = jnp.zeros_like(dbst_ref)
            dlng_ref[...] = jnp.zeros_like(dlng_ref)
            dlnb_ref[...] = jnp.zeros_like(dlnb_ref)

        gv = v_ref[...]
        lng_v = lng_ref[...]
        vhat, rstd, vn = _gmlp_norm(gv, lng_v, lnb_ref[...])
        tri = _iota2((GMLP_T, GMLP_T), 0) >= _iota2((GMLP_T, GMLP_T), 1)
        for g in range(GMLP_G):
            cs = slice(g * HEAD, (g + 1) * HEAD)
            w = jnp.where(tri, ws_ref[g], 0.0).astype(BF16)
            b = bst_ref[:, g:g + 1]
            dw_acc = jnp.zeros((GMLP_T, GMLP_T), F32)
            db_acc = jnp.zeros((GMLP_T, 1), F32)
            for c in range(tm // GMLP_T):
                rs = slice(c * GMLP_T, (c + 1) * GMLP_T)
                vn_b = vn[rs, cs]
                sg = _mm(w, vn_b) + b
                gu = u_ref[rs, cs]
                gz = z_ref[rs, cs]
                da = d_ref[rs, cs]
                uact = _gelu(gu)
                sz = _silu(gz)
                ds = da * uact * sz
                dp_ref[rs, cs] = (da * sg * sz * _gelu_grad(gu)).astype(BF16)
                dp_ref[rs, 2 * GMLP_W + g * HEAD:2 * GMLP_W + (g + 1) * HEAD] = (da * uact * sg * _silu_grad(gz)).astype(BF16)
                dw_acc += _mm_nt(ds, vn_b)
                db_acc += _rowsum(ds)
                dvn[rs, cs] = _mm_tn(w, ds)
            dws_ref[g] += jnp.where(tri, dw_acc, 0.0)
            dbst_ref[:, g:g + 1] += db_acc
        dvn_v = dvn[...]
        dlng_ref[...] += _colsum(dvn_v * vhat)
        dlnb_ref[...] += _colsum(dvn_v)
        dvh = dvn_v * lng_v
        dva = rstd * (dvh - jnp.mean(dvh, axis=-1, keepdims=True) - vhat * jnp.mean(dvh * vhat, axis=-1, keepdims=True))
        dp_ref[:, GMLP_W:2 * GMLP_W] = (dva * _gelu_grad(gv)).astype(BF16)

    col = lambda k: pl.BlockSpec((tm, GMLP_W), lambda i: (i, GU_BLK + k))
    vec = pl.BlockSpec((1, GMLP_W), lambda i: (0, 0))
    wsp = pl.BlockSpec((GMLP_G, GMLP_T, GMLP_T), lambda i: (0, 0, 0))
    bsp = pl.BlockSpec((GMLP_T, GMLP_G), lambda i: (0, 0))
    return pl.pallas_call(
        body, name="gmlp_bwd", grid=(s // tm,),
        in_specs=[col(0), col(1), col(2), pl.BlockSpec((tm, GMLP_W), lambda i: (i, DN_W // GMLP_W)), vec, vec, wsp, bsp],
        out_specs=[pl.BlockSpec((tm, 3 * GMLP_W), lambda i: (i, 0)), wsp, bsp, vec, vec],
        out_shape=[_sds((s, 3 * GMLP_W), BF16), _sds((GMLP_G, GMLP_T, GMLP_T)), _sds((GMLP_T, GMLP_G)),
                   _sds((1, GMLP_W)), _sds((1, GMLP_W))],
        scratch_shapes=[pltpu.VMEM((tm, GMLP_W), F32)],
        compiler_params=_params(("arbitrary",)),
    )(proj, proj, proj, dmixed, lng, lnb, ws, bs_t)


CQ_BLK = (4 * DN_W + 3 * GMLP_W) // XA_W


def _memkv_fwd(mem, g, w_kv):
    nm, d = mem.shape

    def body(m_ref, g_ref, w_ref, kv_ref):
        mv = m_ref[...]
        r = lax.rsqrt(jnp.mean(mv * mv, axis=-1, keepdims=True) + EPS)
        kv_ref[...] = _mm(mv * r * g_ref[...], w_ref[...])

    return pl.pallas_call(body, name="memkv_fwd", out_shape=_sds((nm, 2 * XA_W)), compiler_params=_params())(mem, g, w_kv)


def _memkv_bwd(mem, g, w_kv, dkv):
    nm, d = mem.shape

    def body(m_ref, g_ref, w_ref, dkv_ref, dw_ref, dg_ref):
        mv = m_ref[...]
        r = lax.rsqrt(jnp.mean(mv * mv, axis=-1, keepdims=True) + EPS)
        xhat = mv * r
        dkv_v = dkv_ref[...]
        dw_ref[...] = _mm_tn(xhat * g_ref[...], dkv_v)
        dg_ref[...] = _colsum(_mm_nt(dkv_v, w_ref[...]) * xhat)

    return pl.pallas_call(body, name="memkv_bwd", out_shape=[_sds((d, 2 * XA_W)), _sds((1, d))],
                          compiler_params=_params())(mem, g, w_kv, dkv)


def _xattn_probs(q, mk):
    sc = _mm_nt(q, mk) * (HEAD ** -0.5)
    e = jnp.exp(sc - jnp.max(sc, axis=-1, keepdims=True))
    return e / _rowsum(e)


def _xattn_fwd(proj, mkv):
    s = proj.shape[0]
    nm = mkv.shape[0]
    tm = _tile(s, (512, 256, 128))

    def body(q_ref, z_ref, kv_ref, o_ref):
        for h in range(XA_H):
            cs = slice(h * HEAD, (h + 1) * HEAD)
            p = _xattn_probs(q_ref[:, cs], kv_ref[:, cs])
            ctx = _mm(p, kv_ref[:, XA_W + h * HEAD:XA_W + (h + 1) * HEAD])
            o_ref[:, cs] = (ctx * _silu(z_ref[:, cs])).astype(BF16)

    col = lambda k: pl.BlockSpec((tm, XA_W), lambda i: (i, CQ_BLK + k))
    return pl.pallas_call(
        body, name="xattn_fwd", grid=(s // tm,),
        in_specs=[col(0), col(1), pl.BlockSpec((nm, 2 * XA_W), lambda i: (0, 0))],
        out_specs=pl.BlockSpec((tm, XA_W), lambda i: (i, 0)), out_shape=_sds((s, XA_W), BF16),
        compiler_params=_params(("parallel",)),
    )(proj, proj, mkv)


def _xattn_bwd(proj, dmixed, mkv):
    s = proj.shape[0]
    nm = mkv.shape[0]
    tm = _tile(s, (512, 256, 128))

    def body(q_ref, z_ref, d_ref, kv_ref, dp_ref, dkv_ref):
        @pl.when(pl.program_id(0) == 0)
        def _():
            dkv_ref[...] = jnp.zeros_like(dkv_ref)

        for h in range(XA_H):
            cs = slice(h * HEAD, (h + 1) * HEAD)
            vs = slice(XA_W + h * HEAD, XA_W + (h + 1) * HEAD)
            q = q_ref[:, cs]
            z = z_ref[:, cs]
            mk = kv_ref[:, cs]
            mv = kv_ref[:, vs]
            p = _xattn_probs(q, mk)
            ctx = _mm(p, mv)
            dc = d_ref[:, cs]
            dctx = dc * _silu(z)
            dp_ref[:, vs] = (dc * ctx * _silu_grad(z)).astype(BF16)
            dp = _mm_nt(dctx, mv)
            dkv_ref[:, vs] += _mm_tn(p, dctx)
            ds = p * (dp - _rowsum(dp * p)) * (HEAD ** -0.5)
            dp_ref[:, cs] = _mm(ds, mk).astype(BF16)
            dkv_ref[:, cs] += _mm_tn(ds, q)

    col = lambda k: pl.BlockSpec((tm, XA_W), lambda i: (i, CQ_BLK + k))
    kvs = pl.BlockSpec((nm, 2 * XA_W), lambda i: (0, 0))
    return pl.pallas_call(
        body, name="xattn_bwd", grid=(s // tm,),
        in_specs=[col(0), col(1), pl.BlockSpec((tm, XA_W), lambda i: (i, (DN_W + GMLP_W) // XA_W)), kvs],
        out_specs=[pl.BlockSpec((tm, 2 * XA_W), lambda i: (i, 0)), kvs],
        out_shape=[_sds((s, 2 * XA_W), BF16), _sds((nm, 2 * XA_W))],
        compiler_params=_params(("arbitrary",)),
    )(proj, proj, dmixed, mkv)


def _softplus(x):
    return jnp.maximum(x, 0.0) + jnp.log1p(jnp.exp(-jnp.abs(x)))


def _dn_pre(proj, ab, conv_w, alog_row, dt_row):
    s = proj.shape[0]
    tm = _tile(s, (256, 128))
    w3 = 3 * DN_W

    def body(x_ref, halo_ref, ab_ref, cw_ref, al_ref, dt_ref, q_ref, k_ref, v_ref, gb_ref, gbt_ref, yc_ref):
        i = pl.program_id(0)
        xv = x_ref[...]
        cat = jnp.concatenate([jnp.where(i > 0, halo_ref[...], 0.0), xv[0:HALO]], axis=0)
        yc = cw_ref[DN_K - 1:DN_K, :] * xv
        top = cw_ref[DN_K - 1:DN_K, :] * xv[0:HALO]
        for t in range(DN_K - 1):
            back = DN_K - 1 - t
            yc += cw_ref[t:t + 1, :] * pltpu.roll(xv, back, 0)
            top += cw_ref[t:t + 1, :] * pltpu.roll(cat, back, 0)[HALO:2 * HALO]
        yc = jnp.concatenate([top, yc[HALO:tm]], axis=0)
        yc_ref[...] = yc
        act = _silu(yc)
        for h in range(DN_H):
            cs = slice(h * HEAD, (h + 1) * HEAD)
            qa = act[:, cs]
            q_ref[:, cs] = qa * (lax.rsqrt(_rowsum(qa * qa) + EPS) * (HEAD ** -0.5))
            ka = act[:, DN_W + h * HEAD:DN_W + (h + 1) * HEAD]
            k_ref[:, cs] = ka * lax.rsqrt(_rowsum(ka * ka) + EPS)
        v_ref[...] = act[:, 2 * DN_W:w3]
        abv = ab_ref[...]
        lane = _iota2((tm, LANE), 1)
        g = jnp.where(lane < DN_H, -jnp.exp(al_ref[...]) * _softplus(abv + dt_ref[...]), 0.0)
        gc = _mm_hi(_chunk_tri(tm, False), g)
        gbv = jnp.where(lane < DN_H, gc, jnp.where(lane < 2 * DN_H, jax.nn.sigmoid(abv), 0.0))
        gb_ref[...] = gbv
        for c in range(tm // CH):
            gbt_ref[c] = gbv[c * CH:(c + 1) * CH, :].T[0:2 * DN_H, :]

    hb = tm // HALO
    row = lambda w: pl.BlockSpec((tm, w), lambda i: (i, 0))
    vec = pl.BlockSpec((1, LANE), lambda i: (0, 0))
    return pl.pallas_call(
        body, name="dn_pre", grid=(s // tm,),
        in_specs=[row(w3), pl.BlockSpec((HALO, w3), lambda i: (jnp.maximum(i * hb - 1, 0), 0)), row(LANE),
                  pl.BlockSpec((DN_K, w3), lambda i: (0, 0)), vec, vec],
        out_specs=[row(DN_W), row(DN_W), row(DN_W), row(LANE), pl.BlockSpec((tm // CH, 2 * DN_H, CH), lambda i: (i, 0, 0)),
                   row(w3)],
        out_shape=[_sds((s, DN_W)), _sds((s, DN_W)), _sds((s, DN_W)), _sds((s, LANE)), _sds((s // CH, 2 * DN_H, CH)),
                   _sds((s, w3))],
        compiler_params=_params(("parallel",)),
    )(proj, proj, ab, conv_w, alog_row, dt_row)


HEADS = tuple(range(DN_H))


def _hcols(h):
    return slice(h * HEAD, (h + 1) * HEAD)


def _chunk_scalings(k, v, gbv, gbt, h):
    gc = jnp.broadcast_to(gbv[:, h:h + 1], (CH, HEAD))
    beta = jnp.broadcast_to(gbv[:, DN_H + h:DN_H + h + 1], (CH, HEAD))
    gr = gbt[h:h + 1, :]
    ii = _iota2((CH, CH), 0)
    jj = _iota2((CH, CH), 1)
    dec = jnp.exp(jnp.where(ii >= jj, gc[:, 0:CH] - gr, -1e30))
    eg = jnp.exp(gc)
    gl = gr[:, CH - 1:CH]
    kb = k * beta
    return dict(beta=beta, dec=dec, eg=eg, gl=gl, ekd=jnp.exp(gl - gc), kb=kb, vb=v * beta, kbe=kb * eg)


def _chunk_scores(m, q, k):
    kq = _mm_nt(jnp.concatenate([m["kb"], q], axis=0), k)
    strict = _iota2((CH, CH), 0) > _iota2((CH, CH), 1)
    return jnp.where(strict, kq[0:CH] * m["dec"], 0.0), kq[CH:2 * CH] * m["dec"]


def _dn_local(q, k, v, gb, gbt):
    s = q.shape[0]
    cpb = 4 if (s // CH) % 4 == 0 else 1
    tb = cpb * CH
    nblk = s // tb

    def body(q_ref, k_ref, v_ref, gb_ref, gbt_ref, u_ref, w_ref, qg_ref, kd_ref, t_ref, ai_ref, egl_ref):
        def chunk(c, carry):
            r0 = pl.multiple_of(c * CH, CH)
            rows = pl.ds(r0, CH)
            gbv = gb_ref[rows, :]
            gbt_v = gbt_ref[c]
            qs = [q_ref[rows, _hcols(h)] for h in HEADS]
            ks = [k_ref[rows, _hcols(h)] for h in HEADS]
            ms = [_chunk_scalings(ks[h], v_ref[rows, _hcols(h)], gbv, gbt_v, h) for h in HEADS]
            for h in HEADS:
                qg_ref[rows, _hcols(h)] = (qs[h] * ms[h]["eg"]).astype(BF16)
                kd_ref[rows, _hcols(h)] = (ks[h] * ms[h]["ekd"]).astype(BF16)
                egl_ref[c, h:h + 1, :] = jnp.broadcast_to(jnp.exp(ms[h]["gl"]), (1, LANE))
            sc = [_chunk_scores(ms[h], qs[h], ks[h]) for h in HEADS]
            for h in HEADS:
                ai_ref[h, rows, :] = sc[h][1]
            eye = jnp.where(_iota2((CH, CH), 0) == _iota2((CH, CH), 1), 1.0, 0.0).astype(F32)
            ts = [eye - sc[h][0] for h in HEADS]
            ps = [_mm_3x(sc[h][0], sc[h][0]) for h in HEADS]
            ts = [ts[h] + _mm_3x(ts[h], ps[h]) for h in HEADS]
            for _ in range(4):
                ps = [_mm(ps[h], ps[h]) for h in HEADS]
                ts = [ts[h] + _mm(ts[h], ps[h]) for h in HEADS]
            for h in HEADS:
                t_ref[h, rows, :] = ts[h]
                uw = _mm(ts[h], jnp.concatenate([ms[h]["vb"], ms[h]["kbe"]], axis=1))
                u_ref[rows, _hcols(h)] = uw[:, 0:HEAD]
                w_ref[rows, _hcols(h)] = uw[:, HEAD:2 * HEAD].astype(BF16)
            return carry

        lax.fori_loop(0, cpb, chunk, 0, unroll=4)

    row = pl.BlockSpec((tb, DN_W), lambda i: (i, 0))
    sq = pl.BlockSpec((DN_H, tb, CH), lambda i: (0, i, 0))
    return pl.pallas_call(
        body, name="dn_local", grid=(nblk,),
        in_specs=[row, row, row, pl.BlockSpec((tb, LANE), lambda i: (i, 0)),
                  pl.BlockSpec((cpb, 2 * DN_H, CH), lambda i: (i, 0, 0))],
        out_specs=[row, row, row, row, sq, sq, pl.BlockSpec((cpb, DN_H, LANE), lambda i: (i, 0, 0))],
        out_shape=[_sds((s, DN_W)), _sds((s, DN_W), BF16), _sds((s, DN_W), BF16), _sds((s, DN_W), BF16),
                   _sds((DN_H, s, CH)), _sds((DN_H, s, CH)), _sds((s // CH, DN_H, LANE))],
        compiler_params=_params(("parallel",)),
    )(q, k, v, gb, gbt)


def _scan_cpb(s):
    return 8 if (s // CH) % 8 == 0 else 1


def _dn_scan(u, w, qg, kd, ai, egl, proj, norm_g):
    s = u.shape[0]
    cpb = _scan_cpb(s)
    tb = cpb * CH
    nblk = s // tb

    def body(u_ref, w_ref, qg_ref, kd_ref, ai_ref, egl_ref, z_ref, ng_ref, o_ref, vn_ref, st_ref, ob_ref, state):
        @pl.when(pl.program_id(0) == 0)
        def _():
            state[...] = jnp.zeros_like(state)

        ng = ng_ref[...]

        def chunk(c, carry):
            r0 = pl.multiple_of(c * CH, CH)
            rows = pl.ds(r0, CH)
            sts = [state[h] for h in HEADS]
            stb = [sts[h].astype(BF16) for h in HEADS]
            for h in HEADS:
                st_ref[c, h] = sts[h]
            vns = [u_ref[rows, _hcols(h)] - jnp.dot(w_ref[rows, _hcols(h)], stb[h], preferred_element_type=F32)
                   for h in HEADS]
            vnb = [vns[h].astype(BF16) for h in HEADS]
            for h in HEADS:
                state[h] = sts[h] * egl_ref[c, h:h + 1, :] + _mm_tn(kd_ref[rows, _hcols(h)], vnb[h])
            os_ = [jnp.dot(qg_ref[rows, _hcols(h)], stb[h], preferred_element_type=F32) + _mm(ai_ref[h, rows, :], vnb[h])
                   for h in HEADS]
            for h in HEADS:
                o = os_[h]
                vn_ref[rows, _hcols(h)] = vns[h]
                o_ref[rows, _hcols(h)] = o
                r = lax.rsqrt(jnp.mean(o * o, axis=-1, keepdims=True) + EPS)
                ob_ref[rows, _hcols(h)] = (o * r * ng * _silu(z_ref[rows, _hcols(h)])).astype(BF16)
            return carry

        lax.fori_loop(0, cpb, chunk, 0, unroll=4)

    row = pl.BlockSpec((tb, DN_W), lambda i: (i, 0))
    return pl.pallas_call(
        body, name="dn_scan", grid=(nblk,),
        in_specs=[row, row, row, row, pl.BlockSpec((DN_H, tb, CH), lambda i: (0, i, 0)),
                  pl.BlockSpec((cpb, DN_H, LANE), lambda i: (i, 0, 0)), pl.BlockSpec((tb, DN_W), lambda i: (i, 3)),
                  pl.BlockSpec((1, HEAD), lambda i: (0, 0))],
        out_specs=[row, row, pl.BlockSpec((cpb, DN_H, HEAD, HEAD), lambda i: (i, 0, 0, 0)), row],
        out_shape=[_sds((s, DN_W)), _sds((s, DN_W)), _sds((s // CH, DN_H, HEAD, HEAD)), _sds((s, DN_W), BF16)],
        scratch_shapes=[pltpu.VMEM((DN_H, HEAD, HEAD), F32)],
        compiler_params=_params(("arbitrary",)),
    )(u, w, qg, kd, ai, egl, proj, norm_g)


def _dn_scan_bwd(dmixed, o, proj, norm_g, w, qg, kd, ai, egl):
    s = o.shape[0]
    cpb = _scan_cpb(s)
    tb = cpb * CH
    nblk = s // tb

    def body(dm_ref, o_ref, z_ref, ng_ref, w_ref, qg_ref, kd_ref, ai_ref, egl_ref,
             do_ref, dvn_ref, dst_ref, dz_ref, dng_ref, dstate):
        @pl.when(pl.program_id(0) == 0)
        def _():
            dstate[...] = jnp.zeros_like(dstate)
            dng_ref[...] = jnp.zeros_like(dng_ref)

        ng = ng_ref[...]

        def chunk(cc, carry):
            c = cpb - 1 - cc
            r0 = pl.multiple_of(c * CH, CH)
            rows = pl.ds(r0, CH)
            dng = jnp.zeros((1, HEAD), F32)
            dob = []
            for h in HEADS:
                cs = _hcols(h)
                o = o_ref[rows, cs]
                z = z_ref[rows, cs]
                db = dm_ref[rows, cs]
                r = lax.rsqrt(jnp.mean(o * o, axis=-1, keepdims=True) + EPS)
                ohat = o * r
                dz_ref[rows, cs] = (db * ohat * ng * _silu_grad(z)).astype(BF16)
                dyn = db * _silu(z)
                dng += _colsum(dyn * ohat)
                doh = dyn * ng
                do = r * (doh - ohat * jnp.mean(doh * ohat, axis=-1, keepdims=True))
                do_ref[rows, cs] = do
                dob.append(do.astype(BF16))
            dng_ref[...] += dng
            dsn = [dstate[h] for h in HEADS]
            for h in HEADS:
                dst_ref[c, h] = dsn[h]
            dvn = [_mm_tn(ai_ref[h, rows, :], dob[h])
                   + jnp.dot(kd_ref[rows, _hcols(h)], dsn[h].astype(BF16), preferred_element_type=F32) for h in HEADS]
            part = [_mm_tn(qg_ref[rows, _hcols(h)], dob[h]) + egl_ref[c, h:h + 1, :] * dsn[h] for h in HEADS]
            for h in HEADS:
                dvn_ref[rows, _hcols(h)] = dvn[h]
                dstate[h] = part[h] - _mm_tn(w_ref[rows, _hcols(h)], dvn[h])
            return carry

        lax.fori_loop(0, cpb, chunk, 0, unroll=4)

    rev = lambda i: (nblk - 1 - i, 0)
    row = pl.BlockSpec((tb, DN_W), rev)
    vec = pl.BlockSpec((1, HEAD), lambda i: (0, 0))
    return pl.pallas_call(
        body, name="dn_scan_bwd", grid=(nblk,),
        in_specs=[row, row, pl.BlockSpec((tb, DN_W), lambda i: (nblk - 1 - i, 3)), vec, row, row, row,
                  pl.BlockSpec((DN_H, tb, CH), lambda i: (0, nblk - 1 - i, 0)),
                  pl.BlockSpec((cpb, DN_H, LANE), lambda i: (nblk - 1 - i, 0, 0))],
        out_specs=[row, row, pl.BlockSpec((cpb, DN_H, HEAD, HEAD), lambda i: (nblk - 1 - i, 0, 0, 0)), row, vec],
        out_shape=[_sds((s, DN_W)), _sds((s, DN_W)), _sds((s // CH, DN_H, HEAD, HEAD)), _sds((s, DN_W), BF16),
                   _sds((1, HEAD))],
        scratch_shapes=[pltpu.VMEM((DN_H, HEAD, HEAD), F32)],
        compiler_params=_params(("arbitrary",)),
    )(dmixed, o, proj, norm_g, w, qg, kd, ai, egl)


def _dn_local_bwd(q, k, v, gb, gbt, t, vn, st, dst, do, dvn):
    s = q.shape[0]
    cpb = 4 if (s // CH) % 4 == 0 else 1
    tb = cpb * CH
    nblk = s // tb

    def body(q_ref, k_ref, v_ref, gb_ref, gbt_ref, t_ref, vn_ref, st_ref, dst_ref, do_ref, dvn_ref,
             dq_ref, dk_ref, dv_ref, dgb_ref):
        lane = _iota2((CH, LANE), 1)
        last = _iota2((CH, 1), 0) == CH - 1

        def chunk(c, carry):
            r0 = pl.multiple_of(c * CH, CH)
            rows = pl.ds(r0, CH)
            gbv = gb_ref[rows, :]
            gbt_v = gbt_ref[c]
            strict = _iota2((CH, CH), 0) > _iota2((CH, CH), 1)
            qs = [q_ref[rows, _hcols(h)] for h in HEADS]
            ks = [k_ref[rows, _hcols(h)] for h in HEADS]
            vs = [v_ref[rows, _hcols(h)] for h in HEADS]
            ms = [_chunk_scalings(ks[h], vs[h], gbv, gbt_v, h) for h in HEADS]
            sts = [st_ref[c, h] for h in HEADS]
            dsn = [dst_ref[c, h] for h in HEADS]
            dob = [do_ref[rows, _hcols(h)].astype(BF16) for h in HEADS]
            dvnb = [dvn_ref[rows, _hcols(h)].astype(BF16) for h in HEADS]
            vnb = [vn_ref[rows, _hcols(h)].astype(BF16) for h in HEADS]
            tbf = [t_ref[h, rows, :].astype(BF16) for h in HEADS]
            sc = [_chunk_scores(ms[h], qs[h], ks[h]) for h in HEADS]
            xs_ = [_mm_nt(jnp.concatenate([dob[h], dvnb[h]], axis=0), sts[h]) for h in HEADS]
            dai = [_mm_nt(dob[h], vnb[h]) for h in HEADS]
            dkd = [_mm_nt(vnb[h], dsn[h]) for h in HEADS]
            dqg = [xs_[h][0:CH] for h in HEADS]
            duw = [jnp.concatenate([dvnb[h], (-xs_[h][CH:2 * CH]).astype(BF16)], axis=1) for h in HEADS]
            dt = [_mm_nt(duw[h], jnp.concatenate([ms[h]["vb"], ms[h]["kbe"]], axis=1)) for h in HEADS]
            dvk = [_mm_tn(tbf[h], duw[h]) for h in HEADS]
            tdt = [_mm_tn(tbf[h], dt[h]) for h in HEADS]
            da = [jnp.where(strict, -_mm_nt(tdt[h], tbf[h]), 0.0) for h in HEADS]
            dsc = [jnp.concatenate([da[h] * ms[h]["dec"], dai[h] * ms[h]["dec"]], axis=0) for h in HEADS]
            dkq = [_mm(dsc[h], ks[h]) for h in HEADS]
            dk1 = [_mm_tn(dsc[h], jnp.concatenate([ms[h]["kb"], qs[h]], axis=0)) for h in HEADS]
            dgb = jnp.zeros((CH, LANE), F32)
            for h in HEADS:
                m = ms[h]
                eg, ekd, beta = m["eg"], m["ekd"], m["beta"]
                dvb = dvk[h][:, 0:HEAD]
                dkbe = dvk[h][:, HEAD:2 * HEAD]
                kd = ks[h] * ekd
                dkb = dkq[h][0:CH] + dkbe * eg
                dq_ref[rows, _hcols(h)] = dkq[h][CH:2 * CH] + dqg[h] * eg
                dk_ref[rows, _hcols(h)] = dk1[h] + dkd[h] * ekd + dkb * beta
                dv_ref[rows, _hcols(h)] = dvb * beta
                dkd_kd = dkd[h] * kd
                dgl = jnp.exp(m["gl"]) * _rowsum(_colsum(sts[h] * dsn[h])) + _rowsum(_colsum(dkd_kd))
                mm_ = da[h] * sc[h][0] + dai[h] * sc[h][1]
                dgc = (_rowsum(mm_ - mm_.T) + _rowsum(dqg[h] * qs[h] * eg - dkd_kd + dkbe * m["kbe"])
                       + jnp.where(last, dgl, 0.0))
                dbeta = _rowsum(dkb * ks[h] + dvb * vs[h])
                dgb = jnp.where(lane == h, dgc, jnp.where(lane == DN_H + h, dbeta, dgb))
            dgb_ref[rows, :] = dgb
            return carry

        lax.fori_loop(0, cpb, chunk, 0, unroll=2)

    row = pl.BlockSpec((tb, DN_W), lambda i: (i, 0))
    gbs = pl.BlockSpec((tb, LANE), lambda i: (i, 0))
    sts = pl.BlockSpec((cpb, DN_H, HEAD, HEAD), lambda i: (i, 0, 0, 0))
    return pl.pallas_call(
        body, name="dn_local_bwd", grid=(nblk,),
        in_specs=[row, row, row, gbs, pl.BlockSpec((cpb, 2 * DN_H, CH), lambda i: (i, 0, 0)),
                  pl.BlockSpec((DN_H, tb, CH), lambda i: (0, i, 0)), row, sts, sts, row, row],
        out_specs=[row, row, row, gbs],
        out_shape=[_sds((s, DN_W)), _sds((s, DN_W)), _sds((s, DN_W)), _sds((s, LANE))],
        compiler_params=_params(("parallel",)),
    )(q, k, v, gb, gbt, t, vn, st, dst, do, dvn)


def _dn_pre_bwd(proj, yc_all, ab, conv_w, alog_row, dt_row, dq, dk, dv, dgb):
    s = proj.shape[0]
    tm = _tile(s, (256, 128))
    w3 = 3 * DN_W
    nblk = s // tm

    def body(x_ref, yc_ref, ab_ref, cw_ref, al_ref, dt_ref, dq_ref, dk_ref, dv_ref, dgb_ref,
             dx_ref, dab_ref, dcw_ref, dal_ref, ddt_ref, exd, carry):
        i = pl.program_id(0)

        @pl.when(i == 0)
        def _():
            carry[...] = jnp.zeros_like(carry)
            dcw_ref[...] = jnp.zeros_like(dcw_ref)
            dal_ref[...] = jnp.zeros_like(dal_ref)
            ddt_ref[...] = jnp.zeros_like(ddt_ref)

        yc = yc_ref[...]
        sg = jax.nn.sigmoid(yc)
        act = yc * sg
        dact = sg * (1.0 + yc * (1.0 - sg))
        for h in range(DN_H):
            cs = slice(h * HEAD, (h + 1) * HEAD)
            ks = slice(DN_W + h * HEAD, DN_W + (h + 1) * HEAD)
            qa = act[:, cs]
            rq = lax.rsqrt(_rowsum(qa * qa) + EPS)
            qh = qa * rq
            dqv = dq_ref[:, cs]
            exd[0:tm, cs] = (HEAD ** -0.5) * rq * (dqv - qh * _rowsum(dqv * qh)) * dact[:, cs]
            ka = act[:, ks]
            rk = lax.rsqrt(_rowsum(ka * ka) + EPS)
            kh = ka * rk
            dkv = dk_ref[:, cs]
            exd[0:tm, ks] = rk * (dkv - kh * _rowsum(dkv * kh)) * dact[:, ks]
        exd[0:tm, 2 * DN_W:w3] = dv_ref[...] * dact[:, 2 * DN_W:w3]
        xv = x_ref[...]
        dyc = exd[...]
        cat = jnp.concatenate([dyc[tm - HALO:tm], carry[...]], axis=0)
        dcw_ref[DN_K - 1:DN_K, :] += _colsum(dyc * xv)
        dx = cw_ref[DN_K - 1:DN_K, :] * dyc
        for t in range(DN_K - 1):
            ahead = DN_K - 1 - t
            view = jnp.concatenate([pltpu.roll(dyc, tm - ahead, 0)[0:tm - HALO],
                                    pltpu.roll(cat, 2 * HALO - ahead, 0)[0:HALO]], axis=0)
            dcw_ref[t:t + 1, :] += _colsum(view * xv)
            dx += cw_ref[t:t + 1, :] * view
        dx_ref[...] = dx.astype(BF16)
        carry[...] = dyc[0:HALO]

        lane = _iota2((tm, LANE), 1)
        dgbv = dgb_ref[...]
        dg = _mm_hi(_chunk_tri(tm, True), jnp.where(lane < DN_H, dgbv, 0.0))
        abv = ab_ref[...]
        xa = abv + dt_ref[...]
        nea = -jnp.exp(al_ref[...])
        d_da = jnp.where(lane < DN_H, dg * nea * jax.nn.sigmoid(xa), 0.0)
        dal_ref[...] += _colsum(jnp.where(lane < DN_H, dg * nea * _softplus(xa), 0.0))
        ddt_ref[...] += _colsum(d_da)
        beta = jax.nn.sigmoid(abv)
        d_db = jnp.where((lane >= DN_H) & (lane < 2 * DN_H), dgbv * beta * (1.0 - beta), 0.0)
        dab_ref[...] = (d_da + d_db).astype(BF16)

    rev = lambda i: (nblk - 1 - i, 0)
    row = lambda w: pl.BlockSpec((tm, w), rev)
    vec = pl.BlockSpec((1, LANE), lambda i: (0, 0))
    cws = pl.BlockSpec((DN_K, w3), lambda i: (0, 0))
    return pl.pallas_call(
        body, name="dn_pre_bwd", grid=(nblk,),
        in_specs=[row(w3), row(w3), row(LANE), cws, vec, vec, row(DN_W), row(DN_W), row(DN_W), row(LANE)],
        out_specs=[row(w3), row(LANE), cws, vec, vec],
        out_shape=[_sds((s, w3), BF16), _sds((s, LANE), BF16), _sds((DN_K, w3)), _sds((1, LANE)), _sds((1, LANE))],
        scratch_shapes=[pltpu.VMEM((tm, w3), F32), pltpu.VMEM((HALO, w3), F32)],
        compiler_params=_params(("arbitrary",)),
    )(proj, yc_all, ab, conv_w, alog_row, dt_row, dq, dk, dv, dgb)


def _adam(parts, w, m, v, name):
    r, c = w.shape
    n_parts = parts.shape[0]
    small = n_parts * r * c * 4 <= 4 * 1024 * 1024
    tr = r if small else _tile(r, (128, 64, 32, 16, 8))

    def body(p_ref, w_ref, m_ref, v_ref, g_ref, d_ref, nm_ref, nv_ref):
        g = p_ref[0].astype(F32)
        for k in range(1, n_parts):
            g = g + p_ref[k].astype(F32)
        g_ref[...] = g
        mn = ADAM_B1 * m_ref[...] + (1.0 - ADAM_B1) * g
        vn = ADAM_B2 * v_ref[...] + (1.0 - ADAM_B2) * (g * g)
        m_hat = mn / (1.0 - ADAM_B1 ** ADAM_STEP)
        v_hat = vn / (1.0 - ADAM_B2 ** ADAM_STEP)
        d_ref[...] = -ADAM_LR * (m_hat / (jnp.sqrt(v_hat) + ADAM_EPS) + ADAM_WD * w_ref[...])
        nm_ref[...] = mn
        nv_ref[...] = vn

    blk = pl.BlockSpec((tr, c), lambda i: (i, 0))
    return pl.pallas_call(
        body, name=name, grid=(r // tr,),
        in_specs=[pl.BlockSpec((n_parts, tr, c), lambda i: (0, i, 0)), blk, blk, blk],
        out_specs=[blk, blk, blk, blk], out_shape=[_sds((r, c))] * 4,
        compiler_params=_params(("parallel",)),
    )(parts, w, m, v)


_PACK_ROWS = 8


def _pack(vals):
    tiles = []
    for a in vals:
        flat = a.reshape(-1).astype(F32)
        unit = _PACK_ROWS * LANE
        n = -(-flat.shape[0] // unit) * unit
        tiles.append(jnp.pad(flat, (0, n - flat.shape[0])).reshape(n // LANE, LANE))
    return jnp.concatenate(tiles, axis=0)


def _unpack(packed, shapes):
    out = []
    r0 = 0
    for shp in shapes:
        size = 1
        for dim in shp:
            size *= dim
        unit = _PACK_ROWS * LANE
        rows = -(-size // unit) * _PACK_ROWS
        out.append(packed[r0:r0 + rows].reshape(-1)[:size].reshape(shp))
        r0 += rows
    return out


def _lane_row(vec8):
    return jnp.pad(vec8.reshape(1, -1).astype(F32), ((0, 0), (0, LANE - vec8.size)))


def kernel(x, mem, ln_g, w_in, gmlp_ln_g, gmlp_ln_b, gmlp_ws, gmlp_bs, conv_w, dn_a_log, dn_dt_bias, dn_norm_g, mem_norm_g, w_mem_kv, w_out, final_g, loss_target, m_ln_g, m_w_in, m_gmlp_ln_g, m_gmlp_ln_b, m_gmlp_ws, m_gmlp_bs, m_conv_w, m_dn_a_log, m_dn_dt_bias, m_dn_norm_g, m_mem_norm_g, m_w_mem_kv, m_w_out, m_final_g, v_ln_g, v_w_in, v_gmlp_ln_g, v_gmlp_ln_b, v_gmlp_ws, v_gmlp_bs, v_conv_w, v_dn_a_log, v_dn_dt_bias, v_dn_norm_g, v_mem_norm_g, v_w_mem_kv, v_w_out, v_final_g):
    xs = x[0]
    mems = mem[0]
    tgt = loss_target[0]
    s, d = xs.shape
    shard_w = w_in.shape[2]
    in_w = N_DEV * shard_w
    me = 4 * lax.axis_index("x") + 2 * lax.axis_index("y") + lax.axis_index("c")

    (g_in,) = _gather_two_level([w_in[0].astype(BF16)], "gather_w_in")
    o_g, o_dn, o_ab = 0, 3 * GMLP_W, 3 * GMLP_W + 4 * DN_W
    o_xa = o_ab + 2 * DN_H

    def shard_cols(lo, hi):
        out = []
        while lo < hi:
            sh = lo // shard_w
            end = min(hi, (sh + 1) * shard_w)
            out.append(g_in[sh][:, lo - sh * shard_w:end - sh * shard_w])
            lo = end
        return out

    w_main = jnp.concatenate(shard_cols(o_dn, o_ab) + shard_cols(o_g, o_dn) + shard_cols(o_xa, in_w), axis=1)
    w_ab = jnp.pad(jnp.concatenate(shard_cols(o_ab, o_xa), axis=1), ((0, 0), (0, LANE - 2 * DN_H)))

    ln_g2 = ln_g.reshape(1, d)
    lng2 = gmlp_ln_g.reshape(1, GMLP_W)
    lnb2 = gmlp_ln_b.reshape(1, GMLP_W)
    ws3 = gmlp_ws[0]
    bs_t = gmlp_bs[0].T
    alog_row = _lane_row(dn_a_log)
    dt_row = _lane_row(dn_dt_bias)
    dn_g2 = dn_norm_g.reshape(1, HEAD)
    mem_g2 = mem_norm_g.reshape(1, d)
    fin_g2 = final_g.reshape(1, d)

    proj, ab, h_t, (g_out, g_kv, g_conv) = _inproj(
        xs, ln_g2, w_main, w_ab, [w_out[0].astype(BF16), w_mem_kv[0].astype(BF16), conv_w[0]])
    wo = g_out.reshape(MIX_W, d)
    wo_perm = jnp.concatenate([wo[GMLP_W:GMLP_W + DN_W], wo[0:GMLP_W], wo[GMLP_W + DN_W:MIX_W]], axis=0)
    w_kv = g_kv.reshape(d, 2 * XA_W)
    conv_full = g_conv.transpose(1, 0, 2).reshape(DN_K, 3 * DN_W)
    out_a = _gmlp_fwd(proj, lng2, lnb2, ws3, bs_t)
    mkv = _memkv_fwd(mems, mem_g2, w_kv)
    out_c = _xattn_fwd(proj, mkv)
    q, k, v, gb, gbt, yc = _dn_pre(proj, ab, conv_full, alog_row, dt_row)
    u, wk, qg, kd, tmat, ai, egl = _dn_local(q, k, v, gb, gbt)
    o, vn, st, out_b = _dn_scan(u, wk, qg, kd, ai, egl, proj, dn_g2)

    dx2, dx2b, dmixed, loss_acc, d_fin_g = _final(xs, tgt, out_b, out_a, out_c, wo_perm, fin_g2)
    loss = lax.psum(loss_acc[0, 0], ("x", "y", "c"))

    dwo_b = _matmul_tn(out_b, dx2b, "dw_out_b")
    dwo_a = _matmul_tn(out_a, dx2b, "dw_out_a")
    dwo_c = _matmul_tn(out_c, dx2b, "dw_out_c")
    d_w_out = jnp.concatenate([dwo_a, dwo_b, dwo_c], axis=0)

    dp_g, d_ws, d_bst, d_lng, d_lnb = _gmlp_bwd(proj, dmixed, lng2, lnb2, ws3, bs_t)
    dp_x, dmkv = _xattn_bwd(proj, dmixed, mkv)
    d_w_kv, d_mem_g = _memkv_bwd(mems, mem_g2, w_kv, dmkv)
    do, dvn, dst, dp_dz, d_dn_g = _dn_scan_bwd(dmixed, o, proj, dn_g2, wk, qg, kd, ai, egl)
    dq, dk, dv, dgb = _dn_local_bwd(q, k, v, gb, gbt, tmat, vn, st, dst, do, dvn)
    dp_qkv, dp_ab, d_conv, d_alog, d_dt = _dn_pre_bwd(proj, yc, ab, conv_full, alog_row, dt_row, dq, dk, dv, dgb)

    dw_qkv = _matmul_acc(h_t, dp_qkv, "dw_in_qkv")
    dw_dz = _matmul_acc(h_t, dp_dz, "dw_in_dz")
    dw_gm = _matmul_acc(h_t, dp_g, "dw_in_gmlp")
    dw_xa = _matmul_acc(h_t, dp_x, "dw_in_xa")
    dw_ab = _matmul_acc(h_t, dp_ab, "dw_in_ab")
    segs = [(o_g, dw_gm), (o_dn, dw_qkv), (o_dn + 3 * DN_W, dw_dz), (o_ab, dw_ab[:, :2 * DN_H]), (o_xa, dw_xa)]
    shards = []
    for sh in range(N_DEV):
        lo, hi = sh * shard_w, (sh + 1) * shard_w
        parts = [arr[:, max(lo, off) - off:min(hi, off + arr.shape[1]) - off] for off, arr in segs
                 if off < hi and off + arr.shape[1] > lo]
        shards.append(jnp.concatenate(parts, axis=1).astype(BF16))
    send_in = jnp.stack(shards)

    small_shapes = [gmlp_ln_g.shape, gmlp_ln_b.shape, gmlp_ws.shape, gmlp_bs.shape, dn_a_log.shape,
                    dn_dt_bias.shape, dn_norm_g.shape, mem_norm_g.shape, final_g.shape, (DN_K, 3 * DN_W)]
    small_g = _pack([d_lng, d_lnb, d_ws, d_bst.T, d_alog[:, :DN_H], d_dt[:, :DN_H], d_dn_g, d_mem_g, d_fin_g, d_conv])
    zc = jnp.zeros((DN_K, 3 * DN_W), F32)
    small_w = _pack([gmlp_ln_g, gmlp_ln_b, gmlp_ws, gmlp_bs, dn_a_log, dn_dt_bias, dn_norm_g, mem_norm_g, final_g, zc])
    small_m = _pack([m_gmlp_ln_g, m_gmlp_ln_b, m_gmlp_ws, m_gmlp_bs, m_dn_a_log, m_dn_dt_bias, m_dn_norm_g,
                     m_mem_norm_g, m_final_g, zc])
    small_v = _pack([v_gmlp_ln_g, v_gmlp_ln_b, v_gmlp_ws, v_gmlp_bs, v_dn_a_log, v_dn_dt_bias, v_dn_norm_g,
                     v_mem_norm_g, v_final_g, zc + 1.0])

    send_out = d_w_out.reshape(N_DEV, MIX_W // N_DEV, d).astype(BF16)
    send_kv = d_w_kv.reshape(N_DEV, d // N_DEV, 2 * XA_W).astype(BF16)
    sends = [send_in, send_out, send_kv]
    all_small, got = _swap_halves(small_g, sends, "swap_halves")
    core = lax.axis_index("c").astype(jnp.int32).reshape(1)
    chip_sums = [_pair_sum(core, sends[i], got[i], "pair_sum_%d" % i) for i in range(3)]
    grad_x, d_ln_g, (r_in, r_out, r_kv) = _dh_rms(
        [dp_qkv, dp_dz, dp_g, dp_x, dp_ab], w_main, w_ab, xs, dx2, ln_g2, chip_sums)
    (all_ln_g,) = _gather_two_level([_pack([d_ln_g])], "gather_ln_g")

    g_w_in, dl_w_in, nm_w_in, nv_w_in = _adam(r_in, w_in[0], m_w_in[0], v_w_in[0], "adam_w_in")
    g_w_out, dl_w_out, nm_w_out, nv_w_out = _adam(r_out, w_out[0], m_w_out[0], v_w_out[0], "adam_w_out")
    g_w_kv, dl_w_kv, nm_w_kv, nv_w_kv = _adam(r_kv, w_mem_kv[0], m_w_mem_kv[0], v_w_mem_kv[0], "adam_w_kv")
    sm = [_unpack(t, small_shapes) for t in _adam(all_small, small_w, small_m, small_v, "adam_small")]
    ln_res = [_unpack(t, [ln_g.shape])[0]
              for t in _adam(all_ln_g, _pack([ln_g]), _pack([m_ln_g]), _pack([v_ln_g]), "adam_ln_g")]

    conv_parts = lax.dynamic_slice(all_small, (0, all_small.shape[1] - (DN_K * 3 * DN_W) // LANE, 0),
                                   (N_DEV, (DN_K * 3 * DN_W) // LANE, LANE)).reshape(N_DEV, DN_K, 3 * DN_W)
    cshard = conv_w.shape[2]
    conv_parts = lax.dynamic_slice(conv_parts, (0, 0, me * cshard), (N_DEV, DN_K, cshard))
    cpad = ((0, 0), (0, HALO - DN_K), (0, 0))
    conv_res = _adam(jnp.pad(conv_parts, cpad), jnp.pad(conv_w[0], cpad[1:]), jnp.pad(m_conv_w[0], cpad[1:]),
                     jnp.pad(v_conv_w[0], cpad[1:], constant_values=1.0), "adam_conv")
    g_conv_s, dl_conv, nm_conv, nv_conv = [t[:DN_K][None] for t in conv_res]

    def group(idx, big_in, big_conv, big_kv, big_out):
        names = sm[idx]
        return [ln_res[idx], big_in[None], names[0], names[1], names[2], names[3], big_conv, names[4], names[5], names[6],
                names[7], big_kv[None], big_out[None], names[8]]

    grads = group(0, g_w_in, g_conv_s, g_w_kv, g_w_out)
    deltas = group(1, dl_w_in, dl_conv, dl_w_kv, dl_w_out)
    new_m = group(2, nm_w_in, nm_conv, nm_w_kv, nm_w_out)
    new_v = group(3, nv_w_in, nv_conv, nv_w_kv, nv_w_out)
    return (loss, grad_x[None], *grads, *deltas, *new_m, *new_v)
```

```python
import functools

import jax
import jax.numpy as jnp
from jax import lax
from jax.experimental import pallas as pl
from jax.experimental.pallas import tpu as pltpu

F32 = jnp.float32
BF16 = jnp.bfloat16
HIGHEST = lax.Precision.HIGHEST
MESH_ID = pl.DeviceIdType.MESH

N_DEV = 8
EPS = 1e-6
GMLP_W = 512
GMLP_G = 4
GMLP_T = 128
DN_W = 1024
DN_H = 8
HEAD = 128
DN_K = 4
CH = 64
XA_W = 512
XA_H = 4
LANE = 128
HALO = 8
MAIN_W = 4 * DN_W + 3 * GMLP_W + 2 * XA_W
MIX_W = DN_W + GMLP_W + XA_W
VMEM_LIMIT = 56 * 1024 * 1024

ADAM_LR = 0.001
ADAM_B1 = 0.9
ADAM_B2 = 0.999
ADAM_EPS = 1e-08
ADAM_WD = 0.01
ADAM_STEP = 10


def _sds(shape, dtype=F32):
    return jax.ShapeDtypeStruct(tuple(shape), dtype)


def _params(sem=None):
    if sem is None:
        return pltpu.CompilerParams(vmem_limit_bytes=VMEM_LIMIT)
    return pltpu.CompilerParams(dimension_semantics=tuple(sem), vmem_limit_bytes=VMEM_LIMIT)


def _tile(n, prefs):
    for p in prefs:
        if n % p == 0:
            return p
    return n


def _mm(a, b):
    return jnp.dot(a.astype(BF16), b.astype(BF16), preferred_element_type=F32)


def _mm_nt(a, b):
    return lax.dot_general(a.astype(BF16), b.astype(BF16), (((1,), (1,)), ((), ())), preferred_element_type=F32)


def _mm_tn(a, b):
    return lax.dot_general(a.astype(BF16), b.astype(BF16), (((0,), (0,)), ((), ())), preferred_element_type=F32)


def _mm_hi(a, b):
    return jnp.dot(a, b, precision=HIGHEST, preferred_element_type=F32)


def _mm_3x(a, b):
    return jnp.dot(a, b, precision=lax.Precision.HIGH, preferred_element_type=F32)


_GELU_C = 0.7978845608028654
_GELU_A = 0.044715


def _gelu(x):
    return 0.5 * x * (1.0 + jnp.tanh(_GELU_C * (x + _GELU_A * x * x * x)))


def _gelu_grad(x):
    t = jnp.tanh(_GELU_C * (x + _GELU_A * x * x * x))
    return 0.5 * (1.0 + t) + 0.5 * x * (1.0 - t * t) * _GELU_C * (1.0 + 3.0 * _GELU_A * x * x)


def _silu(x):
    return x * jax.nn.sigmoid(x)


def _silu_grad(x):
    s = jax.nn.sigmoid(x)
    return s * (1.0 + x * (1.0 - s))


def _rowsum(x):
    return jnp.sum(x, axis=-1, keepdims=True)


def _colsum(x):
    return jnp.sum(x, axis=0, keepdims=True)


def _iota2(shape, dim):
    return lax.broadcasted_iota(jnp.int32, shape, dim)


def _chunk_tri(tm, upper):
    r = _iota2((tm, tm), 0)
    c = _iota2((tm, tm), 1)
    same = lax.shift_right_logical(r, 6) == lax.shift_right_logical(c, 6)
    tri = (r <= c) if upper else (r >= c)
    return jnp.where(same & tri, 1.0, 0.0).astype(F32)


N_CHIP = 4


def _mesh_place():
    x, y, c = lax.axis_index("x"), lax.axis_index("y"), lax.axis_index("c")
    chips = [(1 - x, y), (x, 1 - y), (1 - x, 1 - y)]
    return x, y, c, (x, y, 1 - c), chips


class _Gather:
    def __init__(self, ins, outs, send_sems, recv_sems, loc_sems):
        self.ins, self.outs, self.send_sems, self.recv_sems, self.loc_sems = ins, outs, send_sems, recv_sems, loc_sems
        self.x, self.y, self.c, self.sib, self.chips = _mesh_place()
        self.me = (self.x, self.y, self.c)

    def copy(self, a, k, block, to, src=None):
        slot = self.outs[a].at[4 * block[0] + 2 * block[1] + block[2]]
        return pltpu.make_async_remote_copy(
            src_ref=slot if src is None else src, dst_ref=slot, send_sem=self.send_sems.at[a, k],
            recv_sem=self.recv_sems.at[a, k], device_id=to, device_id_type=MESH_ID)

    def own(self, a):
        return pltpu.make_async_copy(self.ins[a], self.outs[a].at[4 * self.x + 2 * self.y + self.c], self.loc_sems.at[a])

    def first(self, a):
        return [self.copy(a, 0, self.me, self.sib, src=self.ins[a])] + [
            self.copy(a, 1 + j, self.me, (*chip, self.c), src=self.ins[a]) for j, chip in enumerate(self.chips)]

    def passed(self, a, j):
        return self.copy(a, 4 + j, (*self.chips[j], self.c), self.sib)

    def start(self):
        for a in range(len(self.ins)):
            self.own(a).start()
            for cp in self.first(a):
                cp.start()

    def finish(self):
        n = len(self.ins)
        for a in range(n):
            for j, chip in enumerate(self.chips):
                self.copy(a, 1 + j, (*chip, self.c), self.me).wait_recv()
                self.passed(a, j).start()
        for a in range(n):
            self.copy(a, 0, self.sib, self.me).wait_recv()
            for j, chip in enumerate(self.chips):
                self.copy(a, 4 + j, (*chip, 1 - self.c), self.me).wait_recv()
        for a in range(n):
            for cp in self.first(a) + [self.passed(a, j) for j in range(N_CHIP - 1)]:
                cp.wait_send()
            self.own(a).wait()

    @staticmethod
    def sems(n):
        return [pltpu.SemaphoreType.DMA((n, N_DEV - 1)), pltpu.SemaphoreType.DMA((n, N_DEV - 1)),
                pltpu.SemaphoreType.DMA((n,))]


def _gather_two_level(arrs, name):
    n = len(arrs)

    def body(*refs):
        g = _Gather(refs[:n], refs[n:2 * n], *refs[2 * n:])
        g.start()
        g.finish()

    any_spec = pl.BlockSpec(memory_space=pl.ANY)
    return pl.pallas_call(
        body, name=name, out_shape=[_sds((N_DEV,) + a.shape, a.dtype) for a in arrs],
        in_specs=[any_spec] * n, out_specs=[any_spec] * n, scratch_shapes=_Gather.sems(n),
        compiler_params=pltpu.CompilerParams(has_side_effects=True),
    )(*arrs)


def _swap_halves(small, grads, name):
    n = len(grads)

    def body(*refs):
        small_ref = refs[0]
        ins = refs[1:1 + n]
        small_out = refs[1 + n]
        got = refs[2 + n:2 + 2 * n]
        s_send, s_recv, g_send, g_recv, loc_sem = refs[2 + 2 * n:]
        x, y, c, sib, _ = _mesh_place()
        me = 4 * x + 2 * y + c
        sends, recvs = [], []
        for j in range(1, N_DEV):
            px = 1 - x if (j >> 2) & 1 else x
            py = 1 - y if (j >> 1) & 1 else y
            pc = 1 - c if j & 1 else c
            cp = pltpu.make_async_remote_copy(
                src_ref=small_ref, dst_ref=small_out.at[me], send_sem=s_send.at[j - 1], recv_sem=s_recv.at[j - 1],
                device_id=(px, py, pc), device_id_type=MESH_ID)
            cp.start()
            sends.append(cp)
            recvs.append(pltpu.make_async_remote_copy(
                src_ref=small_ref, dst_ref=small_out.at[4 * px + 2 * py + pc], send_sem=s_send.at[j - 1],
                recv_sem=s_recv.at[j - 1], device_id=(px, py, pc), device_id_type=MESH_ID))
        own = pltpu.make_async_copy(small_ref, small_out.at[me], loc_sem)
        own.start()
        for a in range(n):
            for chip in range(N_CHIP):
                cp = pltpu.make_async_remote_copy(
                    src_ref=ins[a].at[2 * chip + 1 - c], dst_ref=got[a].at[chip], send_sem=g_send.at[a, chip],
                    recv_sem=g_recv.at[a, chip], device_id=sib, device_id_type=MESH_ID)
                cp.start()
                sends.append(cp)
                recvs.append(cp)
        for cp in sends:
            cp.wait_send()
        for cp in recvs:
            cp.wait_recv()
        own.wait()

    half = [_sds((N_CHIP,) + g.shape[1:], g.dtype) for g in grads]
    any_spec = pl.BlockSpec(memory_space=pl.ANY)
    res = pl.pallas_call(
        body, name=name, out_shape=[_sds((N_DEV,) + small.shape, small.dtype)] + half,
        in_specs=[any_spec] * (1 + n), out_specs=[any_spec] * (1 + n),
        scratch_shapes=[pltpu.SemaphoreType.DMA((N_DEV - 1,)), pltpu.SemaphoreType.DMA((N_DEV - 1,)),
                        pltpu.SemaphoreType.DMA((n, N_CHIP)), pltpu.SemaphoreType.DMA((n, N_CHIP)),
                        pltpu.SemaphoreType.DMA],
        compiler_params=pltpu.CompilerParams(has_side_effects=True),
    )(small, *grads)
    return res[0], res[1:]


def _pair_sum(core, mine, got, name):
    nc, r, c = got.shape
    tr = _tile(r, (256, 128, 64, 32, 16))

    def body(core_ref, a_ref, b_ref, o_ref):
        o_ref[...] = (a_ref[...].astype(F32) + b_ref[...].astype(F32)).astype(BF16)

    return pl.pallas_call(
        body, name=name, out_shape=_sds(got.shape, BF16),
        grid_spec=pltpu.PrefetchScalarGridSpec(
            num_scalar_prefetch=1, grid=(nc, r // tr),
            in_specs=[pl.BlockSpec((1, tr, c), lambda i, j, core_ref: (2 * i + core_ref[0], j, 0)),
                      pl.BlockSpec((1, tr, c), lambda i, j, core_ref: (i, j, 0))],
            out_specs=pl.BlockSpec((1, tr, c), lambda i, j, core_ref: (i, j, 0))),
        compiler_params=_params(("parallel", "parallel")),
    )(core, mine, got)


class _ChipExchange:
    def __init__(self, ins, outs, send_sems, recv_sems, loc_sems):
        self.ins, self.outs, self.send_sems, self.recv_sems, self.loc_sems = ins, outs, send_sems, recv_sems, loc_sems
        self.x, self.y, self.c, _, self.chips = _mesh_place()
        self.mine = 2 * self.x + self.y

    def own(self, a):
        return pltpu.make_async_copy(self.ins[a].at[self.mine], self.outs[a].at[self.mine], self.loc_sems.at[a])

    def copy(self, a, j, lands_in):
        chip = self.chips[j]
        return pltpu.make_async_remote_copy(
            src_ref=self.ins[a].at[2 * chip[0] + chip[1]], dst_ref=self.outs[a].at[lands_in],
            send_sem=self.send_sems.at[a, j], recv_sem=self.recv_sems.at[a, j], device_id=(*chip, self.c),
            device_id_type=MESH_ID)

    def start(self):
        for a in range(len(self.ins)):
            self.own(a).start()
            for j in range(N_CHIP - 1):
                self.copy(a, j, self.mine).start()

    def finish(self):
        for a in range(len(self.ins)):
            for j, chip in enumerate(self.chips):
                self.copy(a, j, self.mine).wait_send()
                self.copy(a, j, 2 * chip[0] + chip[1]).wait_recv()
            self.own(a).wait()

    @staticmethod
    def sems(n):
        return [pltpu.SemaphoreType.DMA((n, N_CHIP - 1)), pltpu.SemaphoreType.DMA((n, N_CHIP - 1)),
                pltpu.SemaphoreType.DMA((n,))]


def _inproj(x, ln_g, w_main, w_ab, late):
    s, d = x.shape
    n = w_main.shape[1]
    tm = _tile(s, (512, 256, 128))
    tn = _tile(n, (1664, 512, 128))
    nl = len(late)
    ni, nj = s // tm, n // tn

    def body(*refs):
        x_ref, g_ref, w_ref, wab_ref = refs[:4]
        proj_ref, ab_ref, ht_ref = refs[4 + nl:7 + nl]
        hs = refs[7 + 2 * nl]
        gather = _Gather(refs[4:4 + nl], refs[7 + nl:7 + 2 * nl], *refs[8 + 2 * nl:])
        step = pl.program_id(0) * nj + pl.program_id(1)

        @pl.when(step == 0)
        def _():
            gather.start()

        @pl.when(pl.program_id(1) == 0)
        def _():
            xv = x_ref[...]
            r = lax.rsqrt(jnp.mean(xv * xv, axis=-1, keepdims=True) + EPS)
            hf = xv * r * g_ref[...]
            h = hf.astype(BF16)
            hs[...] = h
            ht_ref[...] = hf.T.astype(BF16)
            ab_ref[...] = jnp.dot(h, wab_ref[...], preferred_element_type=F32)

        proj_ref[...] = jnp.dot(hs[...], w_ref[...], preferred_element_type=F32)

        @pl.when(step == ni * nj - 1)
        def _():
            gather.finish()

    any_spec = pl.BlockSpec(memory_space=pl.ANY)
    res = pl.pallas_call(
        body, name="inproj", grid=(ni, nj),
        in_specs=[pl.BlockSpec((tm, d), lambda i, j: (i, 0)), pl.BlockSpec((1, d), lambda i, j: (0, 0)),
                  pl.BlockSpec((d, tn), lambda i, j: (0, j)), pl.BlockSpec((d, LANE), lambda i, j: (0, 0))]
        + [any_spec] * nl,
        out_specs=[pl.BlockSpec((tm, tn), lambda i, j: (i, j)), pl.BlockSpec((tm, LANE), lambda i, j: (i, 0)),
                   pl.BlockSpec((d, tm), lambda i, j: (0, i))] + [any_spec] * nl,
        out_shape=[_sds((s, n)), _sds((s, LANE)), _sds((d, s), BF16)]
        + [_sds((N_DEV,) + a.shape, a.dtype) for a in late],
        scratch_shapes=[pltpu.VMEM((tm, d), BF16)] + _Gather.sems(nl),
        compiler_params=_params(("arbitrary", "arbitrary")),
    )(x, ln_g, w_main, w_ab, *late)
    return res[0], res[1], res[2], res[3:]


def _matmul_acc(a, b, name):
    m, k = a.shape
    n = b.shape[1]
    tm = _tile(m, (2048, 1024, 512, 256, 128))
    tn = _tile(n, (1024, 512, 256, 128))
    tk = _tile(k, (1024, 512, 256, 128))

    def body(a_ref, b_ref, o_ref):
        @pl.when(pl.program_id(2) == 0)
        def _():
            o_ref[...] = jnp.zeros_like(o_ref)

        o_ref[...] += jnp.dot(a_ref[...], b_ref[...], preferred_element_type=F32)

    return pl.pallas_call(
        body, name=name, grid=(m // tm, n // tn, k // tk),
        in_specs=[pl.BlockSpec((tm, tk), lambda i, j, l: (i, l)), pl.BlockSpec((tk, tn), lambda i, j, l: (l, j))],
        out_specs=pl.BlockSpec((tm, tn), lambda i, j, l: (i, j)),
        out_shape=_sds((m, n)),
        compiler_params=_params(("parallel", "parallel", "arbitrary")),
    )(a, b)


def _matmul_tn(a, b, name):
    k, m = a.shape
    n = b.shape[1]
    tm = _tile(m, (1024, 512, 256, 128))
    tn = _tile(n, (1024, 512, 256, 128))
    tk = _tile(k, (1024, 512, 256, 128))

    def body(a_ref, b_ref, o_ref):
        @pl.when(pl.program_id(2) == 0)
        def _():
            o_ref[...] = jnp.zeros_like(o_ref)

        o_ref[...] += _mm_tn(a_ref[...], b_ref[...])

    return pl.pallas_call(
        body, name=name, grid=(m // tm, n // tn, k // tk),
        in_specs=[pl.BlockSpec((tk, tm), lambda i, j, l: (l, i)), pl.BlockSpec((tk, tn), lambda i, j, l: (l, j))],
        out_specs=pl.BlockSpec((tm, tn), lambda i, j, l: (i, j)),
        out_shape=_sds((m, n)),
        compiler_params=_params(("parallel", "parallel", "arbitrary")),
    )(a, b)


def _dh_rms(pieces, w_main, w_ab, x, dx2, ln_g, chip_sums):
    s, d = x.shape
    npc = len(pieces)
    nx = len(chip_sums)
    tm = _tile(s, (256, 128))
    ni = s // tm
    widths = [p.shape[1] for p in pieces[:-1]]
    offs = [sum(widths[:p]) for p in range(npc - 1)]

    def body(*refs):
        p_refs = refs[:npc]
        w_ref, wab_ref, x_ref, dx2_ref, g_ref = refs[npc:npc + 5]
        gx_ref, dg_ref = refs[npc + 5 + nx:npc + 7 + nx]
        exch = _ChipExchange(refs[npc + 5:npc + 5 + nx], refs[npc + 7 + nx:npc + 7 + 2 * nx], *refs[npc + 7 + 2 * nx:])
        step = pl.program_id(0)

        @pl.when(step == 0)
        def _():
            dg_ref[...] = jnp.zeros_like(dg_ref)
            exch.start()

        dhv = _mm_nt(p_refs[npc - 1][...], wab_ref[...])
        for p in range(npc - 1):
            dhv += _mm_nt(p_refs[p][...], w_ref[:, offs[p]:offs[p] + widths[p]])
        xv = x_ref[...]
        r = lax.rsqrt(jnp.mean(xv * xv, axis=-1, keepdims=True) + EPS)
        xhat = xv * r
        dg_ref[...] += _colsum(dhv * xhat)
        dxh = dhv * g_ref[...]
        gx_ref[...] = dx2_ref[...] + r * (dxh - xhat * jnp.mean(dxh * xhat, axis=-1, keepdims=True))

        @pl.when(step == ni - 1)
        def _():
            exch.finish()

    any_spec = pl.BlockSpec(memory_space=pl.ANY)
    row = pl.BlockSpec((tm, d), lambda i: (i, 0))
    vec = pl.BlockSpec((1, d), lambda i: (0, 0))
    once = lambda a: pl.BlockSpec(a.shape, lambda i: (0, 0), pipeline_mode=pl.Buffered(1))
    in_specs = [pl.BlockSpec((tm, p.shape[1]), lambda i: (i, 0)) for p in pieces]
    in_specs += [once(w_main), once(w_ab), row, row, vec] + [any_spec] * nx
    res = pl.pallas_call(
        body, name="dh_rms", grid=(ni,), in_specs=in_specs,
        out_specs=[row, vec] + [any_spec] * nx,
        out_shape=[_sds((s, d)), _sds((1, d))] + [_sds(p.shape, p.dtype) for p in chip_sums],
        scratch_shapes=_ChipExchange.sems(nx),
        compiler_params=_params(("arbitrary",)),
    )(*pieces, w_main, w_ab, x, dx2, ln_g, *chip_sums)
    return res[0], res[1], res[2:]


def _final(x, tgt, out_b, out_a, out_c, w_out, final_g):
    s, d = x.shape
    tm = _tile(s, (256, 128))

    def body(x_ref, t_ref, b_ref, a_ref, c_ref, w_ref, g_ref, dx2_ref, dx2b_ref, dm_ref, loss_ref, dg_ref):
        @pl.when(pl.program_id(0) == 0)
        def _():
            loss_ref[...] = jnp.zeros_like(loss_ref)
            dg_ref[...] = jnp.zeros_like(dg_ref)

        x2 = x_ref[...]
        x2 += jnp.dot(b_ref[...], w_ref[0:DN_W, :], preferred_element_type=F32)
        x2 += jnp.dot(a_ref[...], w_ref[DN_W:DN_W + GMLP_W, :], preferred_element_type=F32)
        x2 += jnp.dot(c_ref[...], w_ref[DN_W + GMLP_W:MIX_W, :], preferred_element_type=F32)
        r = lax.rsqrt(jnp.mean(x2 * x2, axis=-1, keepdims=True) + EPS)
        xhat = x2 * r
        g = g_ref[...]
        err = xhat * g - t_ref[...]
        tok = 0.5 * jnp.mean(err * err, axis=-1, keepdims=True)
        loss_ref[...] += jnp.broadcast_to(_colsum(tok), loss_ref.shape)
        dy = err * (1.0 / d)
        dg_ref[...] += _colsum(dy * xhat)
        dxh = dy * g
        dx2 = r * (dxh - xhat * jnp.mean(dxh * xhat, axis=-1, keepdims=True))
        dx2_ref[...] = dx2
        dx2b = dx2.astype(BF16)
        dx2b_ref[...] = dx2b
        dm_ref[...] = _mm_nt(dx2b, w_ref[...])

    row = pl.BlockSpec((tm, d), lambda i: (i, 0))
    vec = pl.BlockSpec((1, d), lambda i: (0, 0))
    return pl.pallas_call(
        body, name="final", grid=(s // tm,),
        in_specs=[row, row, pl.BlockSpec((tm, DN_W), lambda i: (i, 0)), pl.BlockSpec((tm, GMLP_W), lambda i: (i, 0)),
                  pl.BlockSpec((tm, XA_W), lambda i: (i, 0)), pl.BlockSpec((MIX_W, d), lambda i: (0, 0)), vec],
        out_specs=[row, row, pl.BlockSpec((tm, MIX_W), lambda i: (i, 0)), pl.BlockSpec((1, LANE), lambda i: (0, 0)), vec],
        out_shape=[_sds((s, d)), _sds((s, d), BF16), _sds((s, MIX_W)), _sds((1, LANE)), _sds((1, d))],
        compiler_params=_params(("arbitrary",)),
    )(x, tgt, out_b, out_a, out_c, w_out, final_g)


GU_BLK = (4 * DN_W) // GMLP_W


def _gmlp_norm(gv, lng, lnb):
    va = _gelu(gv)
    mu = jnp.mean(va, axis=-1, keepdims=True)
    xc = va - mu
    rstd = lax.rsqrt(jnp.mean(xc * xc, axis=-1, keepdims=True) + EPS)
    vhat = xc * rstd
    return vhat, rstd, vhat * lng + lnb


def _gmlp_fwd(proj, lng, lnb, ws, bs_t):
    s = proj.shape[0]
    tm = _tile(s, (512, 256, 128))

    def body(u_ref, v_ref, z_ref, lng_ref, lnb_ref, ws_ref, bst_ref, o_ref):
        _, _, vn = _gmlp_norm(v_ref[...], lng_ref[...], lnb_ref[...])
        tri = _iota2((GMLP_T, GMLP_T), 0) >= _iota2((GMLP_T, GMLP_T), 1)
        for g in range(GMLP_G):
            cs = slice(g * HEAD, (g + 1) * HEAD)
            w = jnp.where(tri, ws_ref[g], 0.0).astype(BF16)
            b = bst_ref[:, g:g + 1]
            for c in range(tm // GMLP_T):
                rs = slice(c * GMLP_T, (c + 1) * GMLP_T)
                sg = _mm(w, vn[rs, cs]) + b
                o_ref[rs, cs] = (_gelu(u_ref[rs, cs]) * sg * _silu(z_ref[rs, cs])).astype(BF16)

    col = lambda k: pl.BlockSpec((tm, GMLP_W), lambda i: (i, GU_BLK + k))
    vec = pl.BlockSpec((1, GMLP_W), lambda i: (0, 0))
    return pl.pallas_call(
        body, name="gmlp_fwd", grid=(s // tm,),
        in_specs=[col(0), col(1), col(2), vec, vec, pl.BlockSpec((GMLP_G, GMLP_T, GMLP_T), lambda i: (0, 0, 0)),
                  pl.BlockSpec((GMLP_T, GMLP_G), lambda i: (0, 0))],
        out_specs=pl.BlockSpec((tm, GMLP_W), lambda i: (i, 0)), out_shape=_sds((s, GMLP_W), BF16),
        compiler_params=_params(("parallel",)),
    )(proj, proj, proj, lng, lnb, ws, bs_t)


def _gmlp_bwd(proj, dmixed, lng, lnb, ws, bs_t):
    s = proj.shape[0]
    tm = _tile(s, (512, 256, 128))

    def body(u_ref, v_ref, z_ref, d_ref, lng_ref, lnb_ref, ws_ref, bst_ref,
             dp_ref, dws_ref, dbst_ref, dlng_ref, dlnb_ref, dvn):
        @pl.when(pl.program_id(0) == 0)
        def _():
            dws_ref[...] = jnp.zeros_like(dws_ref)
            dbst_ref[...] = jnp.zeros_like(dbst_ref)
            dlng_ref[...] = jnp.zeros_like(dlng_ref)
            dlnb_ref[...] = jnp.zeros_like(dlnb_ref)

        gv = v_ref[...]
        lng_v = lng_ref[...]
        vhat, rstd, vn = _gmlp_norm(gv, lng_v, lnb_ref[...])
        tri = _iota2((GMLP_T, GMLP_T), 0) >= _iota2((GMLP_T, GMLP_T), 1)
        for g in range(GMLP_G):
            cs = slice(g * HEAD, (g + 1) * HEAD)
            w = jnp.where(tri, ws_ref[g], 0.0).astype(BF16)
            b = bst_ref[:, g:g + 1]
            dw_acc = jnp.zeros((GMLP_T, GMLP_T), F32)
            db_acc = jnp.zeros((GMLP_T, 1), F32)
            for c in range(tm // GMLP_T):
                rs = slice(c * GMLP_T, (c + 1) * GMLP_T)
                vn_b = vn[rs, cs]
                sg = _mm(w, vn_b) + b
                gu = u_ref[rs, cs]
                gz = z_ref[rs, cs]
                da = d_ref[rs, cs]
                uact = _gelu(gu)
                sz = _silu(gz)
                ds = da * uact * sz
                dp_ref[rs, cs] = (da * sg * sz * _gelu_grad(gu)).astype(BF16)
                dp_ref[rs, 2 * GMLP_W + g * HEAD:2 * GMLP_W + (g + 1) * HEAD] = (da * uact * sg * _silu_grad(gz)).astype(BF16)
                dw_acc += _mm_nt(ds, vn_b)
                db_acc += _rowsum(ds)
                dvn[rs, cs] = _mm_tn(w, ds)
            dws_ref[g] += jnp.where(tri, dw_acc, 0.0)
            dbst_ref[:, g:g + 1] += db_acc
        dvn_v = dvn[...]
        dlng_ref[...] += _colsum(dvn_v * vhat)
        dlnb_ref[...] += _colsum(dvn_v)
        dvh = dvn_v * lng_v
        dva = rstd * (dvh - jnp.mean(dvh, axis=-1, keepdims=True) - vhat * jnp.mean(dvh * vhat, axis=-1, keepdims=True))
        dp_ref[:, GMLP_W:2 * GMLP_W] = (dva * _gelu_grad(gv)).astype(BF16)

    col = lambda k: pl.BlockSpec((tm, GMLP_W), lambda i: (i, GU_BLK + k))
    vec = pl.BlockSpec((1, GMLP_W), lambda i: (0, 0))
    wsp = pl.BlockSpec((GMLP_G, GMLP_T, GMLP_T), lambda i: (0, 0, 0))
    bsp = pl.BlockSpec((GMLP_T, GMLP_G), lambda i: (0, 0))
    return pl.pallas_call(
        body, name="gmlp_bwd", grid=(s // tm,),
        in_specs=[col(0), col(1), col(2), pl.BlockSpec((tm, GMLP_W), lambda i: (i, DN_W // GMLP_W)), vec, vec, wsp, bsp],
        out_specs=[pl.BlockSpec((tm, 3 * GMLP_W), lambda i: (i, 0)), wsp, bsp, vec, vec],
        out_shape=[_sds((s, 3 * GMLP_W), BF16), _sds((GMLP_G, GMLP_T, GMLP_T)), _sds((GMLP_T, GMLP_G)),
                   _sds((1, GMLP_W)), _sds((1, GMLP_W))],
        scratch_shapes=[pltpu.VMEM((tm, GMLP_W), F32)],
        compiler_params=_params(("arbitrary",)),
    )(proj, proj, proj, dmixed, lng, lnb, ws, bs_t)


CQ_BLK = (4 * DN_W + 3 * GMLP_W) // XA_W


def _memkv_fwd(mem, g, w_kv):
    nm, d = mem.shape

    def body(m_ref, g_ref, w_ref, kv_ref):
        mv = m_ref[...]
        r = lax.rsqrt(jnp.mean(mv * mv, axis=-1, keepdims=True) + EPS)
        kv_ref[...] = _mm(mv * r * g_ref[...], w_ref[...])

    return pl.pallas_call(body, name="memkv_fwd", out_shape=_sds((nm, 2 * XA_W)), compiler_params=_params())(mem, g, w_kv)


def _memkv_bwd(mem, g, w_kv, dkv):
    nm, d = mem.shape

    def body(m_ref, g_ref, w_ref, dkv_ref, dw_ref, dg_ref):
        mv = m_ref[...]
        r = lax.rsqrt(jnp.mean(mv * mv, axis=-1, keepdims=True) + EPS)
        xhat = mv * r
        dkv_v = dkv_ref[...]
        dw_ref[...] = _mm_tn(xhat * g_ref[...], dkv_v)
        dg_ref[...] = _colsum(_mm_nt(dkv_v, w_ref[...]) * xhat)

    return pl.pallas_call(body, name="memkv_bwd", out_shape=[_sds((d, 2 * XA_W)), _sds((1, d))],
                          compiler_params=_params())(mem, g, w_kv, dkv)


def _xattn_probs(q, mk):
    sc = _mm_nt(q, mk) * (HEAD ** -0.5)
    e = jnp.exp(sc - jnp.max(sc, axis=-1, keepdims=True))
    return e / _rowsum(e)


def _xattn_fwd(proj, mkv):
    s = proj.shape[0]
    nm = mkv.shape[0]
    tm = _tile(s, (512, 256, 128))

    def body(q_ref, z_ref, kv_ref, o_ref):
        for h in range(XA_H):
            cs = slice(h * HEAD, (h + 1) * HEAD)
            p = _xattn_probs(q_ref[:, cs], kv_ref[:, cs])
            ctx = _mm(p, kv_ref[:, XA_W + h * HEAD:XA_W + (h + 1) * HEAD])
            o_ref[:, cs] = (ctx * _silu(z_ref[:, cs])).astype(BF16)

    col = lambda k: pl.BlockSpec((tm, XA_W), lambda i: (i, CQ_BLK + k))
    return pl.pallas_call(
        body, name="xattn_fwd", grid=(s // tm,),
        in_specs=[col(0), col(1), pl.BlockSpec((nm, 2 * XA_W), lambda i: (0, 0))],
        out_specs=pl.BlockSpec((tm, XA_W), lambda i: (i, 0)), out_shape=_sds((s, XA_W), BF16),
        compiler_params=_params(("parallel",)),
    )(proj, proj, mkv)


def _xattn_bwd(proj, dmixed, mkv):
    s = proj.shape[0]
    nm = mkv.shape[0]
    tm = _tile(s, (512, 256, 128))

    def body(q_ref, z_ref, d_ref, kv_ref, dp_ref, dkv_ref):
        @pl.when(pl.program_id(0) == 0)
        def _():
            dkv_ref[...] = jnp.zeros_like(dkv_ref)

        for h in range(XA_H):
            cs = slice(h * HEAD, (h + 1) * HEAD)
            vs = slice(XA_W + h * HEAD, XA_W + (h + 1) * HEAD)
            q = q_ref[:, cs]
            z = z_ref[:, cs]
            mk = kv_ref[:, cs]
            mv = kv_ref[:, vs]
            p = _xattn_probs(q, mk)
            ctx = _mm(p, mv)
            dc = d_ref[:, cs]
            dctx = dc * _silu(z)
            dp_ref[:, vs] = (dc * ctx * _silu_grad(z)).astype(BF16)
            dp = _mm_nt(dctx, mv)
            dkv_ref[:, vs] += _mm_tn(p, dctx)
            ds = p * (dp - _rowsum(dp * p)) * (HEAD ** -0.5)
            dp_ref[:, cs] = _mm(ds, mk).astype(BF16)
            dkv_ref[:, cs] += _mm_tn(ds, q)

    col = lambda k: pl.BlockSpec((tm, XA_W), lambda i: (i, CQ_BLK + k))
    kvs = pl.BlockSpec((nm, 2 * XA_W), lambda i: (0, 0))
    return pl.pallas_call(
        body, name="xattn_bwd", grid=(s // tm,),
        in_specs=[col(0), col(1), pl.BlockSpec((tm, XA_W), lambda i: (i, (DN_W + GMLP_W) // XA_W)), kvs],
        out_specs=[pl.BlockSpec((tm, 2 * XA_W), lambda i: (i, 0)), kvs],
        out_shape=[_sds((s, 2 * XA_W), BF16), _sds((nm, 2 * XA_W))],
        compiler_params=_params(("arbitrary",)),
    )(proj, proj, dmixed, mkv)


def _softplus(x):
    return jnp.maximum(x, 0.0) + jnp.log1p(jnp.exp(-jnp.abs(x)))


def _dn_pre(proj, ab, conv_w, alog_row, dt_row):
    s = proj.shape[0]
    tm = _tile(s, (256, 128))
    w3 = 3 * DN_W

    def body(x_ref, halo_ref, ab_ref, cw_ref, al_ref, dt_ref, q_ref, k_ref, v_ref, gb_ref, gbt_ref, yc_ref):
        i = pl.program_id(0)
        xv = x_ref[...]
        cat = jnp.concatenate([jnp.where(i > 0, halo_ref[...], 0.0), xv[0:HALO]], axis=0)
        yc = cw_ref[DN_K - 1:DN_K, :] * xv
        top = cw_ref[DN_K - 1:DN_K, :] * xv[0:HALO]
        for t in range(DN_K - 1):
            back = DN_K - 1 - t
            yc += cw_ref[t:t + 1, :] * pltpu.roll(xv, back, 0)
            top += cw_ref[t:t + 1, :] * pltpu.roll(cat, back, 0)[HALO:2 * HALO]
        yc = jnp.concatenate([top, yc[HALO:tm]], axis=0)
        yc_ref[...] = yc
        act = _silu(yc)
        for h in range(DN_H):
            cs = slice(h * HEAD, (h + 1) * HEAD)
            qa = act[:, cs]
            q_ref[:, cs] = qa * (lax.rsqrt(_rowsum(qa * qa) + EPS) * (HEAD ** -0.5))
            ka = act[:, DN_W + h * HEAD:DN_W + (h + 1) * HEAD]
            k_ref[:, cs] = ka * lax.rsqrt(_rowsum(ka * ka) + EPS)
        v_ref[...] = act[:, 2 * DN_W:w3]
        abv = ab_ref[...]
        lane = _iota2((tm, LANE), 1)
        g = jnp.where(lane < DN_H, -jnp.exp(al_ref[...]) * _softplus(abv + dt_ref[...]), 0.0)
        gc = _mm_hi(_chunk_tri(tm, False), g)
        gbv = jnp.where(lane < DN_H, gc, jnp.where(lane < 2 * DN_H, jax.nn.sigmoid(abv), 0.0))
        gb_ref[...] = gbv
        for c in range(tm // CH):
            gbt_ref[c] = gbv[c * CH:(c + 1) * CH, :].T[0:2 * DN_H, :]

    hb = tm // HALO
    row = lambda w: pl.BlockSpec((tm, w), lambda i: (i, 0))
    vec = pl.BlockSpec((1, LANE), lambda i: (0, 0))
    return pl.pallas_call(
        body, name="dn_pre", grid=(s // tm,),
        in_specs=[row(w3), pl.BlockSpec((HALO, w3), lambda i: (jnp.maximum(i * hb - 1, 0), 0)), row(LANE),
                  pl.BlockSpec((DN_K, w3), lambda i: (0, 0)), vec, vec],
        out_specs=[row(DN_W), row(DN_W), row(DN_W), row(LANE), pl.BlockSpec((tm // CH, 2 * DN_H, CH), lambda i: (i, 0, 0)),
                   row(w3)],
        out_shape=[_sds((s, DN_W)), _sds((s, DN_W)), _sds((s, DN_W)), _sds((s, LANE)), _sds((s // CH, 2 * DN_H, CH)),
                   _sds((s, w3))],
        compiler_params=_params(("parallel",)),
    )(proj, proj, ab, conv_w, alog_row, dt_row)


HEADS = tuple(range(DN_H))


def _hcols(h):
    return slice(h * HEAD, (h + 1) * HEAD)


def _chunk_scalings(k, v, gbv, gbt, h):
    gc = jnp.broadcast_to(gbv[:, h:h + 1], (CH, HEAD))
    beta = jnp.broadcast_to(gbv[:, DN_H + h:DN_H + h + 1], (CH, HEAD))
    gr = gbt[h:h + 1, :]
    ii = _iota2((CH, CH), 0)
    jj = _iota2((CH, CH), 1)
    dec = jnp.exp(jnp.where(ii >= jj, gc[:, 0:CH] - gr, -1e30))
    eg = jnp.exp(gc)
    gl = gr[:, CH - 1:CH]
    kb = k * beta
    return dict(beta=beta, dec=dec, eg=eg, gl=gl, ekd=jnp.exp(gl - gc), kb=kb, vb=v * beta, kbe=kb * eg)


def _chunk_scores(m, q, k):
    kq = _mm_nt(jnp.concatenate([m["kb"], q], axis=0), k)
    strict = _iota2((CH, CH), 0) > _iota2((CH, CH), 1)
    return jnp.where(strict, kq[0:CH] * m["dec"], 0.0), kq[CH:2 * CH] * m["dec"]


def _dn_local(q, k, v, gb, gbt):
    s = q.shape[0]
    cpb = 4 if (s // CH) % 4 == 0 else 1
    tb = cpb * CH
    nblk = s // tb

    def body(q_ref, k_ref, v_ref, gb_ref, gbt_ref, u_ref, w_ref, qg_ref, kd_ref, t_ref, ai_ref, egl_ref):
        def chunk(c, carry):
            r0 = pl.multiple_of(c * CH, CH)
            rows = pl.ds(r0, CH)
            gbv = gb_ref[rows, :]
            gbt_v = gbt_ref[c]
            qs = [q_ref[rows, _hcols(h)] for h in HEADS]
            ks = [k_ref[rows, _hcols(h)] for h in HEADS]
            ms = [_chunk_scalings(ks[h], v_ref[rows, _hcols(h)], gbv, gbt_v, h) for h in HEADS]
            for h in HEADS:
                qg_ref[rows, _hcols(h)] = (qs[h] * ms[h]["eg"]).astype(BF16)
                kd_ref[rows, _hcols(h)] = (ks[h] * ms[h]["ekd"]).astype(BF16)
                egl_ref[c, h:h + 1, :] = jnp.broadcast_to(jnp.exp(ms[h]["gl"]), (1, LANE))
            sc = [_chunk_scores(ms[h], qs[h], ks[h]) for h in HEADS]
            for h in HEADS:
                ai_ref[h, rows, :] = sc[h][1]
            eye = jnp.where(_iota2((CH, CH), 0) == _iota2((CH, CH), 1), 1.0, 0.0).astype(F32)
            ts = [eye - sc[h][0] for h in HEADS]
            ps = [_mm_3x(sc[h][0], sc[h][0]) for h in HEADS]
            ts = [ts[h] + _mm_3x(ts[h], ps[h]) for h in HEADS]
            for _ in range(4):
                ps = [_mm(ps[h], ps[h]) for h in HEADS]
                ts = [ts[h] + _mm(ts[h], ps[h]) for h in HEADS]
            for h in HEADS:
                t_ref[h, rows, :] = ts[h]
                uw = _mm(ts[h], jnp.concatenate([ms[h]["vb"], ms[h]["kbe"]], axis=1))
                u_ref[rows, _hcols(h)] = uw[:, 0:HEAD]
                w_ref[rows, _hcols(h)] = uw[:, HEAD:2 * HEAD].astype(BF16)
            return carry

        lax.fori_loop(0, cpb, chunk, 0, unroll=4)

    row = pl.BlockSpec((tb, DN_W), lambda i: (i, 0))
    sq = pl.BlockSpec((DN_H, tb, CH), lambda i: (0, i, 0))
    return pl.pallas_call(
        body, name="dn_local", grid=(nblk,),
        in_specs=[row, row, row, pl.BlockSpec((tb, LANE), lambda i: (i, 0)),
                  pl.BlockSpec((cpb, 2 * DN_H, CH), lambda i: (i, 0, 0))],
        out_specs=[row, row, row, row, sq, sq, pl.BlockSpec((cpb, DN_H, LANE), lambda i: (i, 0, 0))],
        out_shape=[_sds((s, DN_W)), _sds((s, DN_W), BF16), _sds((s, DN_W), BF16), _sds((s, DN_W), BF16),
                   _sds((DN_H, s, CH)), _sds((DN_H, s, CH)), _sds((s // CH, DN_H, LANE))],
        compiler_params=_params(("parallel",)),
    )(q, k, v, gb, gbt)


def _scan_cpb(s):
    return 8 if (s // CH) % 8 == 0 else 1


def _dn_scan(u, w, qg, kd, ai, egl, proj, norm_g):
    s = u.shape[0]
    cpb = _scan_cpb(s)
    tb = cpb * CH
    nblk = s // tb

    def body(u_ref, w_ref, qg_ref, kd_ref, ai_ref, egl_ref, z_ref, ng_ref, o_ref, vn_ref, st_ref, ob_ref, state):
        @pl.when(pl.program_id(0) == 0)
        def _():
            state[...] = jnp.zeros_like(state)

        ng = ng_ref[...]

        def chunk(c, carry):
            r0 = pl.multiple_of(c * CH, CH)
            rows = pl.ds(r0, CH)
            sts = [state[h] for h in HEADS]
            stb = [sts[h].astype(BF16) for h in HEADS]
            for h in HEADS:
                st_ref[c, h] = stb[h]
            vns = [u_ref[rows, _hcols(h)] - jnp.dot(w_ref[rows, _hcols(h)], stb[h], preferred_element_type=F32)
                   for h in HEADS]
            vnb = [vns[h].astype(BF16) for h in HEADS]
            for h in HEADS:
                state[h] = sts[h] * egl_ref[c, h:h + 1, :] + _mm_tn(kd_ref[rows, _hcols(h)], vnb[h])
            os_ = [jnp.dot(qg_ref[rows, _hcols(h)], stb[h], preferred_element_type=F32) + _mm(ai_ref[h, rows, :], vnb[h])
                   for h in HEADS]
            for h in HEADS:
                o = os_[h]
                vn_ref[rows, _hcols(h)] = vnb[h]
                o_ref[rows, _hcols(h)] = o
                r = lax.rsqrt(jnp.mean(o * o, axis=-1, keepdims=True) + EPS)
                ob_ref[rows, _hcols(h)] = (o * r * ng * _silu(z_ref[rows, _hcols(h)])).astype(BF16)
            return carry

        lax.fori_loop(0, cpb, chunk, 0, unroll=4)

    row = pl.BlockSpec((tb, DN_W), lambda i: (i, 0))
    return pl.pallas_call(
        body, name="dn_scan", grid=(nblk,),
        in_specs=[row, row, row, row, pl.BlockSpec((DN_H, tb, CH), lambda i: (0, i, 0)),
                  pl.BlockSpec((cpb, DN_H, LANE), lambda i: (i, 0, 0)), pl.BlockSpec((tb, DN_W), lambda i: (i, 3)),
                  pl.BlockSpec((1, HEAD), lambda i: (0, 0))],
        out_specs=[row, row, pl.BlockSpec((cpb, DN_H, HEAD, HEAD), lambda i: (i, 0, 0, 0)), row],
        out_shape=[_sds((s, DN_W)), _sds((s, DN_W), BF16), _sds((s // CH, DN_H, HEAD, HEAD), BF16), _sds((s, DN_W), BF16)],
        scratch_shapes=[pltpu.VMEM((DN_H, HEAD, HEAD), F32)],
        compiler_params=_params(("arbitrary",)),
    )(u, w, qg, kd, ai, egl, proj, norm_g)


def _dn_scan_bwd(dmixed, o, proj, norm_g, w, qg, kd, ai, egl):
    s = o.shape[0]
    cpb = _scan_cpb(s)
    tb = cpb * CH
    nblk = s // tb

    def body(dm_ref, o_ref, z_ref, ng_ref, w_ref, qg_ref, kd_ref, ai_ref, egl_ref,
             do_ref, dvn_ref, dst_ref, dz_ref, dng_ref, dstate):
        @pl.when(pl.program_id(0) == 0)
        def _():
            dstate[...] = jnp.zeros_like(dstate)
            dng_ref[...] = jnp.zeros_like(dng_ref)

        ng = ng_ref[...]

        def chunk(cc, carry):
            c = cpb - 1 - cc
            r0 = pl.multiple_of(c * CH, CH)
            rows = pl.ds(r0, CH)
            dng = jnp.zeros((1, HEAD), F32)
            dob = []
            for h in HEADS:
                cs = _hcols(h)
                o = o_ref[rows, cs]
                z = z_ref[rows, cs]
                db = dm_ref[rows, cs]
                r = lax.rsqrt(jnp.mean(o * o, axis=-1, keepdims=True) + EPS)
                ohat = o * r
                dz_ref[rows, cs] = (db * ohat * ng * _silu_grad(z)).astype(BF16)
                dyn = db * _silu(z)
                dng += _colsum(dyn * ohat)
                doh = dyn * ng
                do = r * (doh - ohat * jnp.mean(doh * ohat, axis=-1, keepdims=True))
                dob.append(do.astype(BF16))
                do_ref[rows, cs] = dob[h]
            dng_ref[...] += dng
            dsn = [dstate[h] for h in HEADS]
            dsb = [dsn[h].astype(BF16) for h in HEADS]
            for h in HEADS:
                dst_ref[c, h] = dsb[h]
            dvn = [(_mm_tn(ai_ref[h, rows, :], dob[h])
                    + jnp.dot(kd_ref[rows, _hcols(h)], dsb[h], preferred_element_type=F32)).astype(BF16) for h in HEADS]
            part = [_mm_tn(qg_ref[rows, _hcols(h)], dob[h]) + egl_ref[c, h:h + 1, :] * dsn[h] for h in HEADS]
            for h in HEADS:
                dvn_ref[rows, _hcols(h)] = dvn[h]
                dstate[h] = part[h] - _mm_tn(w_ref[rows, _hcols(h)], dvn[h])
            return carry

        lax.fori_loop(0, cpb, chunk, 0, unroll=4)

    rev = lambda i: (nblk - 1 - i, 0)
    row = pl.BlockSpec((tb, DN_W), rev)
    vec = pl.BlockSpec((1, HEAD), lambda i: (0, 0))
    return pl.pallas_call(
        body, name="dn_scan_bwd", grid=(nblk,),
        in_specs=[row, row, pl.BlockSpec((tb, DN_W), lambda i: (nblk - 1 - i, 3)), vec, row, row, row,
                  pl.BlockSpec((DN_H, tb, CH), lambda i: (0, nblk - 1 - i, 0)),
                  pl.BlockSpec((cpb, DN_H, LANE), lambda i: (nblk - 1 - i, 0, 0))],
        out_specs=[row, row, pl.BlockSpec((cpb, DN_H, HEAD, HEAD), lambda i: (nblk - 1 - i, 0, 0, 0)), row, vec],
        out_shape=[_sds((s, DN_W), BF16), _sds((s, DN_W), BF16), _sds((s // CH, DN_H, HEAD, HEAD), BF16),
                   _sds((s, DN_W), BF16), _sds((1, HEAD))],
        scratch_shapes=[pltpu.VMEM((DN_H, HEAD, HEAD), F32)],
        compiler_params=_params(("arbitrary",)),
    )(dmixed, o, proj, norm_g, w, qg, kd, ai, egl)


def _dn_local_bwd(q, k, v, gb, gbt, t, vn, st, dst, do, dvn):
    s = q.shape[0]
    cpb = 4 if (s // CH) % 4 == 0 else 1
    tb = cpb * CH
    nblk = s // tb

    def body(q_ref, k_ref, v_ref, gb_ref, gbt_ref, t_ref, vn_ref, st_ref, dst_ref, do_ref, dvn_ref,
             dq_ref, dk_ref, dv_ref, dgb_ref):
        lane = _iota2((CH, LANE), 1)
        last = _iota2((CH, 1), 0) == CH - 1

        def chunk(c, carry):
            r0 = pl.multiple_of(c * CH, CH)
            rows = pl.ds(r0, CH)
            gbv = gb_ref[rows, :]
            gbt_v = gbt_ref[c]
            strict = _iota2((CH, CH), 0) > _iota2((CH, CH), 1)
            qs = [q_ref[rows, _hcols(h)] for h in HEADS]
            ks = [k_ref[rows, _hcols(h)] for h in HEADS]
            vs = [v_ref[rows, _hcols(h)] for h in HEADS]
            ms = [_chunk_scalings(ks[h], vs[h], gbv, gbt_v, h) for h in HEADS]
            sts = [st_ref[c, h] for h in HEADS]
            dsn = [dst_ref[c, h] for h in HEADS]
            dob = [do_ref[rows, _hcols(h)].astype(BF16) for h in HEADS]
            dvnb = [dvn_ref[rows, _hcols(h)].astype(BF16) for h in HEADS]
            vnb = [vn_ref[rows, _hcols(h)].astype(BF16) for h in HEADS]
            tbf = [t_ref[h, rows, :].astype(BF16) for h in HEADS]
            sc = [_chunk_scores(ms[h], qs[h], ks[h]) for h in HEADS]
            xs_ = [_mm_nt(jnp.concatenate([dob[h], dvnb[h]], axis=0), sts[h]) for h in HEADS]
            dai = [_mm_nt(dob[h], vnb[h]) for h in HEADS]
            dkd = [_mm_nt(vnb[h], dsn[h]) for h in HEADS]
            dqg = [xs_[h][0:CH] for h in HEADS]
            duw = [jnp.concatenate([dvnb[h], (-xs_[h][CH:2 * CH]).astype(BF16)], axis=1) for h in HEADS]
            dt = [_mm_nt(duw[h], jnp.concatenate([ms[h]["vb"], ms[h]["kbe"]], axis=1)) for h in HEADS]
            dvk = [_mm_tn(tbf[h], duw[h]) for h in HEADS]
            tdt = [_mm_tn(tbf[h], dt[h]) for h in HEADS]
            da = [jnp.where(strict, -_mm_nt(tdt[h], tbf[h]), 0.0) for h in HEADS]
            dsc = [jnp.concatenate([da[h] * ms[h]["dec"], dai[h] * ms[h]["dec"]], axis=0) for h in HEADS]
            dkq = [_mm(dsc[h], ks[h]) for h in HEADS]
            dk1 = [_mm_tn(dsc[h], jnp.concatenate([ms[h]["kb"], qs[h]], axis=0)) for h in HEADS]
            dgb = jnp.zeros((CH, LANE), F32)
            for h in HEADS:
                m = ms[h]
                eg, ekd, beta = m["eg"], m["ekd"], m["beta"]
                dvb = dvk[h][:, 0:HEAD]
                dkbe = dvk[h][:, HEAD:2 * HEAD]
                kd = ks[h] * ekd
                dkb = dkq[h][0:CH] + dkbe * eg
                dq_ref[rows, _hcols(h)] = dkq[h][CH:2 * CH] + dqg[h] * eg
                dk_ref[rows, _hcols(h)] = dk1[h] + dkd[h] * ekd + dkb * beta
                dv_ref[rows, _hcols(h)] = dvb * beta
                dkd_kd = dkd[h] * kd
                dgl = (jnp.exp(m["gl"]) * _rowsum(_colsum(sts[h].astype(F32) * dsn[h].astype(F32)))
                       + _rowsum(_colsum(dkd_kd)))
                mm_ = da[h] * sc[h][0] + dai[h] * sc[h][1]
                dgc = (_rowsum(mm_ - mm_.T) + _rowsum(dqg[h] * qs[h] * eg - dkd_kd + dkbe * m["kbe"])
                       + jnp.where(last, dgl, 0.0))
                dbeta = _rowsum(dkb * ks[h] + dvb * vs[h])
                dgb = jnp.where(lane == h, dgc, jnp.where(lane == DN_H + h, dbeta, dgb))
            dgb_ref[rows, :] = dgb
            return carry

        lax.fori_loop(0, cpb, chunk, 0, unroll=2)

    row = pl.BlockSpec((tb, DN_W), lambda i: (i, 0))
    gbs = pl.BlockSpec((tb, LANE), lambda i: (i, 0))
    sts = pl.BlockSpec((cpb, DN_H, HEAD, HEAD), lambda i: (i, 0, 0, 0))
    return pl.pallas_call(
        body, name="dn_local_bwd", grid=(nblk,),
        in_specs=[row, row, row, gbs, pl.BlockSpec((cpb, 2 * DN_H, CH), lambda i: (i, 0, 0)),
                  pl.BlockSpec((DN_H, tb, CH), lambda i: (0, i, 0)), row, sts, sts, row, row],
        out_specs=[row, row, row, gbs],
        out_shape=[_sds((s, DN_W)), _sds((s, DN_W)), _sds((s, DN_W)), _sds((s, LANE))],
        compiler_params=_params(("parallel",)),
    )(q, k, v, gb, gbt, t, vn, st, dst, do, dvn)


def _dn_pre_bwd(proj, yc_all, ab, conv_w, alog_row, dt_row, dq, dk, dv, dgb):
    s = proj.shape[0]
    tm = _tile(s, (256, 128))
    w3 = 3 * DN_W
    nblk = s // tm

    def body(x_ref, yc_ref, ab_ref, cw_ref, al_ref, dt_ref, dq_ref, dk_ref, dv_ref, dgb_ref,
             dx_ref, dab_ref, dcw_ref, dal_ref, ddt_ref, exd, carry):
        i = pl.program_id(0)

        @pl.when(i == 0)
        def _():
            carry[...] = jnp.zeros_like(carry)
            dcw_ref[...] = jnp.zeros_like(dcw_ref)
            dal_ref[...] = jnp.zeros_like(dal_ref)
            ddt_ref[...] = jnp.zeros_like(ddt_ref)

        yc = yc_ref[...]
        sg = jax.nn.sigmoid(yc)
        act = yc * sg
        dact = sg * (1.0 + yc * (1.0 - sg))
        for h in range(DN_H):
            cs = slice(h * HEAD, (h + 1) * HEAD)
            ks = slice(DN_W + h * HEAD, DN_W + (h + 1) * HEAD)
            qa = act[:, cs]
            rq = lax.rsqrt(_rowsum(qa * qa) + EPS)
            qh = qa * rq
            dqv = dq_ref[:, cs]
            exd[0:tm, cs] = (HEAD ** -0.5) * rq * (dqv - qh * _rowsum(dqv * qh)) * dact[:, cs]
            ka = act[:, ks]
            rk = lax.rsqrt(_rowsum(ka * ka) + EPS)
            kh = ka * rk
            dkv = dk_ref[:, cs]
            exd[0:tm, ks] = rk * (dkv - kh * _rowsum(dkv * kh)) * dact[:, ks]
        exd[0:tm, 2 * DN_W:w3] = dv_ref[...] * dact[:, 2 * DN_W:w3]
        xv = x_ref[...]
        dyc = exd[...]
        cat = jnp.concatenate([dyc[tm - HALO:tm], carry[...]], axis=0)
        dcw_ref[DN_K - 1:DN_K, :] += _colsum(dyc * xv)
        dx = cw_ref[DN_K - 1:DN_K, :] * dyc
        for t in range(DN_K - 1):
            ahead = DN_K - 1 - t
            view = jnp.concatenate([pltpu.roll(dyc, tm - ahead, 0)[0:tm - HALO],
                                    pltpu.roll(cat, 2 * HALO - ahead, 0)[0:HALO]], axis=0)
            dcw_ref[t:t + 1, :] += _colsum(view * xv)
            dx += cw_ref[t:t + 1, :] * view
        dx_ref[...] = dx.astype(BF16)
        carry[...] = dyc[0:HALO]

        lane = _iota2((tm, LANE), 1)
        dgbv = dgb_ref[...]
        dg = _mm_hi(_chunk_tri(tm, True), jnp.where(lane < DN_H, dgbv, 0.0))
        abv = ab_ref[...]
        xa = abv + dt_ref[...]
        nea = -jnp.exp(al_ref[...])
        d_da = jnp.where(lane < DN_H, dg * nea * jax.nn.sigmoid(xa), 0.0)
        dal_ref[...] += _colsum(jnp.where(lane < DN_H, dg * nea * _softplus(xa), 0.0))
        ddt_ref[...] += _colsum(d_da)
        beta = jax.nn.sigmoid(abv)
        d_db = jnp.where((lane >= DN_H) & (lane < 2 * DN_H), dgbv * beta * (1.0 - beta), 0.0)
        dab_ref[...] = (d_da + d_db).astype(BF16)

    rev = lambda i: (nblk - 1 - i, 0)
    row = lambda w: pl.BlockSpec((tm, w), rev)
    vec = pl.BlockSpec((1, LANE), lambda i: (0, 0))
    cws = pl.BlockSpec((DN_K, w3), lambda i: (0, 0))
    return pl.pallas_call(
        body, name="dn_pre_bwd", grid=(nblk,),
        in_specs=[row(w3), row(w3), row(LANE), cws, vec, vec, row(DN_W), row(DN_W), row(DN_W), row(LANE)],
        out_specs=[row(w3), row(LANE), cws, vec, vec],
        out_shape=[_sds((s, w3), BF16), _sds((s, LANE), BF16), _sds((DN_K, w3)), _sds((1, LANE)), _sds((1, LANE))],
        scratch_shapes=[pltpu.VMEM((tm, w3), F32), pltpu.VMEM((HALO, w3), F32)],
        compiler_params=_params(("arbitrary",)),
    )(proj, yc_all, ab, conv_w, alog_row, dt_row, dq, dk, dv, dgb)


def _adam(parts, w, m, v, name):
    r, c = w.shape
    n_parts = parts.shape[0]
    small = n_parts * r * c * 4 <= 4 * 1024 * 1024
    tr = r if small else _tile(r, (128, 64, 32, 16, 8))

    def body(p_ref, w_ref, m_ref, v_ref, g_ref, d_ref, nm_ref, nv_ref):
        g = p_ref[0].astype(F32)
        for k in range(1, n_parts):
            g = g + p_ref[k].astype(F32)
        g_ref[...] = g
        mn = ADAM_B1 * m_ref[...] + (1.0 - ADAM_B1) * g
        vn = ADAM_B2 * v_ref[...] + (1.0 - ADAM_B2) * (g * g)
        m_hat = mn / (1.0 - ADAM_B1 ** ADAM_STEP)
        v_hat = vn / (1.0 - ADAM_B2 ** ADAM_STEP)
        d_ref[...] = -ADAM_LR * (m_hat / (jnp.sqrt(v_hat) + ADAM_EPS) + ADAM_WD * w_ref[...])
        nm_ref[...] = mn
        nv_ref[...] = vn

    blk = pl.BlockSpec((tr, c), lambda i: (i, 0))
    return pl.pallas_call(
        body, name=name, grid=(r // tr,),
        in_specs=[pl.BlockSpec((n_parts, tr, c), lambda i: (0, i, 0)), blk, blk, blk],
        out_specs=[blk, blk, blk, blk], out_shape=[_sds((r, c))] * 4,
        compiler_params=_params(("parallel",)),
    )(parts, w, m, v)


_PACK_ROWS = 8


def _pack(vals):
    tiles = []
    for a in vals:
        flat = a.reshape(-1).astype(F32)
        unit = _PACK_ROWS * LANE
        n = -(-flat.shape[0] // unit) * unit
        tiles.append(jnp.pad(flat, (0, n - flat.shape[0])).reshape(n // LANE, LANE))
    return jnp.concatenate(tiles, axis=0)


def _unpack(packed, shapes):
    out = []
    r0 = 0
    for shp in shapes:
        size = 1
        for dim in shp:
            size *= dim
        unit = _PACK_ROWS * LANE
        rows = -(-size // unit) * _PACK_ROWS
        out.append(packed[r0:r0 + rows].reshape(-1)[:size].reshape(shp))
        r0 += rows
    return out


def _lane_row(vec8):
    return jnp.pad(vec8.reshape(1, -1).astype(F32), ((0, 0), (0, LANE - vec8.size)))


def kernel(x, mem, ln_g, w_in, gmlp_ln_g, gmlp_ln_b, gmlp_ws, gmlp_bs, conv_w, dn_a_log, dn_dt_bias, dn_norm_g, mem_norm_g, w_mem_kv, w_out, final_g, loss_target, m_ln_g, m_w_in, m_gmlp_ln_g, m_gmlp_ln_b, m_gmlp_ws, m_gmlp_bs, m_conv_w, m_dn_a_log, m_dn_dt_bias, m_dn_norm_g, m_mem_norm_g, m_w_mem_kv, m_w_out, m_final_g, v_ln_g, v_w_in, v_gmlp_ln_g, v_gmlp_ln_b, v_gmlp_ws, v_gmlp_bs, v_conv_w, v_dn_a_log, v_dn_dt_bias, v_dn_norm_g, v_mem_norm_g, v_w_mem_kv, v_w_out, v_final_g):
    xs = x[0]
    mems = mem[0]
    tgt = loss_target[0]
    s, d = xs.shape
    shard_w = w_in.shape[2]
    in_w = N_DEV * shard_w
    me = 4 * lax.axis_index("x") + 2 * lax.axis_index("y") + lax.axis_index("c")

    (g_in,) = _gather_two_level([w_in[0].astype(BF16)], "gather_w_in")
    o_g, o_dn, o_ab = 0, 3 * GMLP_W, 3 * GMLP_W + 4 * DN_W
    o_xa = o_ab + 2 * DN_H

    def shard_cols(lo, hi):
        out = []
        while lo < hi:
            sh = lo // shard_w
            end = min(hi, (sh + 1) * shard_w)
            out.append(g_in[sh][:, lo - sh * shard_w:end - sh * shard_w])
            lo = end
        return out

    w_main = jnp.concatenate(shard_cols(o_dn, o_ab) + shard_cols(o_g, o_dn) + shard_cols(o_xa, in_w), axis=1)
    w_ab = jnp.pad(jnp.concatenate(shard_cols(o_ab, o_xa), axis=1), ((0, 0), (0, LANE - 2 * DN_H)))

    ln_g2 = ln_g.reshape(1, d)
    lng2 = gmlp_ln_g.reshape(1, GMLP_W)
    lnb2 = gmlp_ln_b.reshape(1, GMLP_W)
    ws3 = gmlp_ws[0]
    bs_t = gmlp_bs[0].T
    alog_row = _lane_row(dn_a_log)
    dt_row = _lane_row(dn_dt_bias)
    dn_g2 = dn_norm_g.reshape(1, HEAD)
    mem_g2 = mem_norm_g.reshape(1, d)
    fin_g2 = final_g.reshape(1, d)

    proj, ab, h_t, (g_out, g_kv, g_conv) = _inproj(
        xs, ln_g2, w_main, w_ab, [w_out[0].astype(BF16), w_mem_kv[0].astype(BF16), conv_w[0]])
    wo = g_out.reshape(MIX_W, d)
    wo_perm = jnp.concatenate([wo[GMLP_W:GMLP_W + DN_W], wo[0:GMLP_W], wo[GMLP_W + DN_W:MIX_W]], axis=0)
    w_kv = g_kv.reshape(d, 2 * XA_W)
    conv_full = g_conv.transpose(1, 0, 2).reshape(DN_K, 3 * DN_W)
    out_a = _gmlp_fwd(proj, lng2, lnb2, ws3, bs_t)
    mkv = _memkv_fwd(mems, mem_g2, w_kv)
    out_c = _xattn_fwd(proj, mkv)
    q, k, v, gb, gbt, yc = _dn_pre(proj, ab, conv_full, alog_row, dt_row)
    u, wk, qg, kd, tmat, ai, egl = _dn_local(q, k, v, gb, gbt)
    o, vn, st, out_b = _dn_scan(u, wk, qg, kd, ai, egl, proj, dn_g2)

    dx2, dx2b, dmixed, loss_acc, d_fin_g = _final(xs, tgt, out_b, out_a, out_c, wo_perm, fin_g2)
    loss = lax.psum(loss_acc[0, 0], ("x", "y", "c"))

    dwo_b = _matmul_tn(out_b, dx2b, "dw_out_b")
    dwo_a = _matmul_tn(out_a, dx2b, "dw_out_a")
    dwo_c = _matmul_tn(out_c, dx2b, "dw_out_c")
    d_w_out = jnp.concatenate([dwo_a, dwo_b, dwo_c], axis=0)

    dp_g, d_ws, d_bst, d_lng, d_lnb = _gmlp_bwd(proj, dmixed, lng2, lnb2, ws3, bs_t)
    dp_x, dmkv = _xattn_bwd(proj, dmixed, mkv)
    d_w_kv, d_mem_g = _memkv_bwd(mems, mem_g2, w_kv, dmkv)
    do, dvn, dst, dp_dz, d_dn_g = _dn_scan_bwd(dmixed, o, proj, dn_g2, wk, qg, kd, ai, egl)
    dq, dk, dv, dgb = _dn_local_bwd(q, k, v, gb, gbt, tmat, vn, st, dst, do, dvn)
    dp_qkv, dp_ab, d_conv, d_alog, d_dt = _dn_pre_bwd(proj, yc, ab, conv_full, alog_row, dt_row, dq, dk, dv, dgb)

    dw_qkv = _matmul_acc(h_t, dp_qkv, "dw_in_qkv")
    dw_dz = _matmul_acc(h_t, dp_dz, "dw_in_dz")
    dw_gm = _matmul_acc(h_t, dp_g, "dw_in_gmlp")
    dw_xa = _matmul_acc(h_t, dp_x, "dw_in_xa")
    dw_ab = _matmul_acc(h_t, dp_ab, "dw_in_ab")
    segs = [(o_g, dw_gm), (o_dn, dw_qkv), (o_dn + 3 * DN_W, dw_dz), (o_ab, dw_ab[:, :2 * DN_H]), (o_xa, dw_xa)]
    shards = []
    for sh in range(N_DEV):
        lo, hi = sh * shard_w, (sh + 1) * shard_w
        parts = [arr[:, max(lo, off) - off:min(hi, off + arr.shape[1]) - off] for off, arr in segs
                 if off < hi and off + arr.shape[1] > lo]
        shards.append(jnp.concatenate(parts, axis=1).astype(BF16))
    send_in = jnp.stack(shards)

    small_shapes = [gmlp_ln_g.shape, gmlp_ln_b.shape, gmlp_ws.shape, gmlp_bs.shape, dn_a_log.shape,
                    dn_dt_bias.shape, dn_norm_g.shape, mem_norm_g.shape, final_g.shape, (DN_K, 3 * DN_W)]
    small_g = _pack([d_lng, d_lnb, d_ws, d_bst.T, d_alog[:, :DN_H], d_dt[:, :DN_H], d_dn_g, d_mem_g, d_fin_g, d_conv])
    zc = jnp.zeros((DN_K, 3 * DN_W), F32)
    small_w = _pack([gmlp_ln_g, gmlp_ln_b, gmlp_ws, gmlp_bs, dn_a_log, dn_dt_bias, dn_norm_g, mem_norm_g, final_g, zc])
    small_m = _pack([m_gmlp_ln_g, m_gmlp_ln_b, m_gmlp_ws, m_gmlp_bs, m_dn_a_log, m_dn_dt_bias, m_dn_norm_g,
                     m_mem_norm_g, m_final_g, zc])
    small_v = _pack([v_gmlp_ln_g, v_gmlp_ln_b, v_gmlp_ws, v_gmlp_bs, v_dn_a_log, v_dn_dt_bias, v_dn_norm_g,
                     v_mem_norm_g, v_final_g, zc + 1.0])

    send_out = d_w_out.reshape(N_DEV, MIX_W // N_DEV, d).astype(BF16)
    send_kv = d_w_kv.reshape(N_DEV, d // N_DEV, 2 * XA_W).astype(BF16)
    sends = [send_in, send_out, send_kv]
    all_small, got = _swap_halves(small_g, sends, "swap_halves")
    core = lax.axis_index("c").astype(jnp.int32).reshape(1)
    chip_sums = [_pair_sum(core, sends[i], got[i], "pair_sum_%d" % i) for i in range(3)]
    grad_x, d_ln_g, (r_in, r_out, r_kv) = _dh_rms(
        [dp_qkv, dp_dz, dp_g, dp_x, dp_ab], w_main, w_ab, xs, dx2, ln_g2, chip_sums)
    (all_ln_g,) = _gather_two_level([_pack([d_ln_g])], "gather_ln_g")

    g_w_in, dl_w_in, nm_w_in, nv_w_in = _adam(r_in, w_in[0], m_w_in[0], v_w_in[0], "adam_w_in")
    g_w_out, dl_w_out, nm_w_out, nv_w_out = _adam(r_out, w_out[0], m_w_out[0], v_w_out[0], "adam_w_out")
    g_w_kv, dl_w_kv, nm_w_kv, nv_w_kv = _adam(r_kv, w_mem_kv[0], m_w_mem_kv[0], v_w_mem_kv[0], "adam_w_kv")
    sm = [_unpack(t, small_shapes) for t in _adam(all_small, small_w, small_m, small_v, "adam_small")]
    ln_res = [_unpack(t, [ln_g.shape])[0]
              for t in _adam(all_ln_g, _pack([ln_g]), _pack([m_ln_g]), _pack([v_ln_g]), "adam_ln_g")]

    conv_parts = lax.dynamic_slice(all_small, (0, all_small.shape[1] - (DN_K * 3 * DN_W) // LANE, 0),
                                   (N_DEV, (DN_K * 3 * DN_W) // LANE, LANE)).reshape(N_DEV, DN_K, 3 * DN_W)
    cshard = conv_w.shape[2]
    conv_parts = lax.dynamic_slice(conv_parts, (0, 0, me * cshard), (N_DEV, DN_K, cshard))
    cpad = ((0, 0), (0, HALO - DN_K), (0, 0))
    conv_res = _adam(jnp.pad(conv_parts, cpad), jnp.pad(conv_w[0], cpad[1:]), jnp.pad(m_conv_w[0], cpad[1:]),
                     jnp.pad(v_conv_w[0], cpad[1:], constant_values=1.0), "adam_conv")
    g_conv_s, dl_conv, nm_conv, nv_conv = [t[:DN_K][None] for t in conv_res]

    def group(idx, big_in, big_conv, big_kv, big_out):
        names = sm[idx]
        return [ln_res[idx], big_in[None], names[0], names[1], names[2], names[3], big_conv, names[4], names[5], names[6],
                names[7], big_kv[None], big_out[None], names[8]]

    grads = group(0, g_w_in, g_conv_s, g_w_kv, g_w_out)
    deltas = group(1, dl_w_in, dl_conv, dl_w_kv, dl_w_out)
    new_m = group(2, nm_w_in, nm_conv, nm_w_kv, nm_w_out)
    new_v = group(3, nv_w_in, nv_conv, nv_w_kv, nv_w_out)
    return (loss, grad_x[None], *grads, *deltas, *new_m, *new_v)
```

```python
import functools

import jax
import jax.numpy as jnp
from jax import lax
from jax.experimental import pallas as pl
from jax.experimental.pallas import tpu as pltpu

F32 = jnp.float32
BF16 = jnp.bfloat16
HIGHEST = lax.Precision.HIGHEST
MESH_ID = pl.DeviceIdType.MESH

N_DEV = 8
EPS = 1e-6
GMLP_W = 512
GMLP_G = 4
GMLP_T = 128
DN_W = 1024
DN_H = 8
HEAD = 128
DN_K = 4
CH = 64
XA_W = 512
XA_H = 4
LANE = 128
HALO = 8
MAIN_W = 4 * DN_W + 3 * GMLP_W + 2 * XA_W
MIX_W = DN_W + GMLP_W + XA_W
VMEM_LIMIT = 56 * 1024 * 1024

ADAM_LR = 0.001
ADAM_B1 = 0.9
ADAM_B2 = 0.999
ADAM_EPS = 1e-08
ADAM_WD = 0.01
ADAM_STEP = 10


def _sds(shape, dtype=F32):
    return jax.ShapeDtypeStruct(tuple(shape), dtype)


def _params(sem=None):
    if sem is None:
        return pltpu.CompilerParams(vmem_limit_bytes=VMEM_LIMIT)
    return pltpu.CompilerParams(dimension_semantics=tuple(sem), vmem_limit_bytes=VMEM_LIMIT)


def _tile(n, prefs):
    for p in prefs:
        if n % p == 0:
            return p
    return n


def _mm(a, b):
    return jnp.dot(a.astype(BF16), b.astype(BF16), preferred_element_type=F32)


def _mm_nt(a, b):
    return lax.dot_general(a.astype(BF16), b.astype(BF16), (((1,), (1,)), ((), ())), preferred_element_type=F32)


def _mm_tn(a, b):
    return lax.dot_general(a.astype(BF16), b.astype(BF16), (((0,), (0,)), ((), ())), preferred_element_type=F32)


def _mm_hi(a, b):
    return jnp.dot(a, b, precision=HIGHEST, preferred_element_type=F32)


def _mm_3x(a, b):
    return jnp.dot(a, b, precision=lax.Precision.HIGH, preferred_element_type=F32)


_GELU_C = 0.7978845608028654
_GELU_A = 0.044715


def _gelu(x):
    return 0.5 * x * (1.0 + jnp.tanh(_GELU_C * (x + _GELU_A * x * x * x)))


def _gelu_grad(x):
    t = jnp.tanh(_GELU_C * (x + _GELU_A * x * x * x))
    return 0.5 * (1.0 + t) + 0.5 * x * (1.0 - t * t) * _GELU_C * (1.0 + 3.0 * _GELU_A * x * x)


def _silu(x):
    return x * jax.nn.sigmoid(x)


def _silu_grad(x):
    s = jax.nn.sigmoid(x)
    return s * (1.0 + x * (1.0 - s))


def _rowsum(x):
    return jnp.sum(x, axis=-1, keepdims=True)


def _colsum(x):
    return jnp.sum(x, axis=0, keepdims=True)


def _iota2(shape, dim):
    return lax.broadcasted_iota(jnp.int32, shape, dim)


def _chunk_tri(tm, upper):
    r = _iota2((tm, tm), 0)
    c = _iota2((tm, tm), 1)
    same = lax.shift_right_logical(r, 6) == lax.shift_right_logical(c, 6)
    tri = (r <= c) if upper else (r >= c)
    return jnp.where(same & tri, 1.0, 0.0).astype(F32)


N_CHIP = 4


def _mesh_place():
    x, y, c = lax.axis_index("x"), lax.axis_index("y"), lax.axis_index("c")
    chips = [(1 - x, y), (x, 1 - y), (1 - x, 1 - y)]
    return x, y, c, (x, y, 1 - c), chips


class _Gather:
    def __init__(self, ins, outs, send_sems, recv_sems, loc_sems):
        self.ins, self.outs, self.send_sems, self.recv_sems, self.loc_sems = ins, outs, send_sems, recv_sems, loc_sems
        self.x, self.y, self.c, self.sib, self.chips = _mesh_place()
        self.me = (self.x, self.y, self.c)

    def copy(self, a, k, block, to, src=None):
        slot = self.outs[a].at[4 * block[0] + 2 * block[1] + block[2]]
        return pltpu.make_async_remote_copy(
            src_ref=slot if src is None else src, dst_ref=slot, send_sem=self.send_sems.at[a, k],
            recv_sem=self.recv_sems.at[a, k], device_id=to, device_id_type=MESH_ID)

    def own(self, a):
        return pltpu.make_async_copy(self.ins[a], self.outs[a].at[4 * self.x + 2 * self.y + self.c], self.loc_sems.at[a])

    def first(self, a):
        return [self.copy(a, 0, self.me, self.sib, src=self.ins[a])] + [
            self.copy(a, 1 + j, self.me, (*chip, self.c), src=self.ins[a]) for j, chip in enumerate(self.chips)]

    def passed(self, a, j):
        return self.copy(a, 4 + j, (*self.chips[j], self.c), self.sib)

    def start(self):
        for a in range(len(self.ins)):
            self.own(a).start()
            for cp in self.first(a):
                cp.start()

    def finish(self):
        n = len(self.ins)
        for a in range(n):
            for j, chip in enumerate(self.chips):
                self.copy(a, 1 + j, (*chip, self.c), self.me).wait_recv()
                self.passed(a, j).start()
        for a in range(n):
            self.copy(a, 0, self.sib, self.me).wait_recv()
            for j, chip in enumerate(self.chips):
                self.copy(a, 4 + j, (*chip, 1 - self.c), self.me).wait_recv()
        for a in range(n):
            for cp in self.first(a) + [self.passed(a, j) for j in range(N_CHIP - 1)]:
                cp.wait_send()
            self.own(a).wait()

    @staticmethod
    def sems(n):
        return [pltpu.SemaphoreType.DMA((n, N_DEV - 1)), pltpu.SemaphoreType.DMA((n, N_DEV - 1)),
                pltpu.SemaphoreType.DMA((n,))]


def _gather_two_level(arrs, name):
    n = len(arrs)

    def body(*refs):
        g = _Gather(refs[:n], refs[n:2 * n], *refs[2 * n:])
        g.start()
        g.finish()

    any_spec = pl.BlockSpec(memory_space=pl.ANY)
    return pl.pallas_call(
        body, name=name, out_shape=[_sds((N_DEV,) + a.shape, a.dtype) for a in arrs],
        in_specs=[any_spec] * n, out_specs=[any_spec] * n, scratch_shapes=_Gather.sems(n),
        compiler_params=pltpu.CompilerParams(has_side_effects=True),
    )(*arrs)


def _swap_halves(small, grads, name):
    n = len(grads)

    def body(*refs):
        small_ref = refs[0]
        ins = refs[1:1 + n]
        small_out = refs[1 + n]
        got = refs[2 + n:2 + 2 * n]
        s_send, s_recv, g_send, g_recv, loc_sem = refs[2 + 2 * n:]
        x, y, c, sib, _ = _mesh_place()
        me = 4 * x + 2 * y + c
        sends, recvs = [], []
        for j in range(1, N_DEV):
            px = 1 - x if (j >> 2) & 1 else x
            py = 1 - y if (j >> 1) & 1 else y
            pc = 1 - c if j & 1 else c
            cp = pltpu.make_async_remote_copy(
                src_ref=small_ref, dst_ref=small_out.at[me], send_sem=s_send.at[j - 1], recv_sem=s_recv.at[j - 1],
                device_id=(px, py, pc), device_id_type=MESH_ID)
            cp.start()
            sends.append(cp)
            recvs.append(pltpu.make_async_remote_copy(
                src_ref=small_ref, dst_ref=small_out.at[4 * px + 2 * py + pc], send_sem=s_send.at[j - 1],
                recv_sem=s_recv.at[j - 1], device_id=(px, py, pc), device_id_type=MESH_ID))
        own = pltpu.make_async_copy(small_ref, small_out.at[me], loc_sem)
        own.start()
        for a in range(n):
            for chip in range(N_CHIP):
                cp = pltpu.make_async_remote_copy(
                    src_ref=ins[a].at[2 * chip + 1 - c], dst_ref=got[a].at[chip], send_sem=g_send.at[a, chip],
                    recv_sem=g_recv.at[a, chip], device_id=sib, device_id_type=MESH_ID)
                cp.start()
                sends.append(cp)
                recvs.append(cp)
        for cp in sends:
            cp.wait_send()
        for cp in recvs:
            cp.wait_recv()
        own.wait()

    half = [_sds((N_CHIP,) + g.shape[1:], g.dtype) for g in grads]
    any_spec = pl.BlockSpec(memory_space=pl.ANY)
    res = pl.pallas_call(
        body, name=name, out_shape=[_sds((N_DEV,) + small.shape, small.dtype)] + half,
        in_specs=[any_spec] * (1 + n), out_specs=[any_spec] * (1 + n),
        scratch_shapes=[pltpu.SemaphoreType.DMA((N_DEV - 1,)), pltpu.SemaphoreType.DMA((N_DEV - 1,)),
                        pltpu.SemaphoreType.DMA((n, N_CHIP)), pltpu.SemaphoreType.DMA((n, N_CHIP)),
                        pltpu.SemaphoreType.DMA],
        compiler_params=pltpu.CompilerParams(has_side_effects=True),
    )(small, *grads)
    return res[0], res[1:]


def _pair_sum(core, mine, got, name):
    nc, r, c = got.shape
    tr = _tile(r, (256, 128, 64, 32, 16))

    def body(core_ref, a_ref, b_ref, o_ref):
        o_ref[...] = (a_ref[...].astype(F32) + b_ref[...].astype(F32)).astype(BF16)

    return pl.pallas_call(
        body, name=name, out_shape=_sds(got.shape, BF16),
        grid_spec=pltpu.PrefetchScalarGridSpec(
            num_scalar_prefetch=1, grid=(nc, r // tr),
            in_specs=[pl.BlockSpec((1, tr, c), lambda i, j, core_ref: (2 * i + core_ref[0], j, 0)),
                      pl.BlockSpec((1, tr, c), lambda i, j, core_ref: (i, j, 0))],
            out_specs=pl.BlockSpec((1, tr, c), lambda i, j, core_ref: (i, j, 0))),
        compiler_params=_params(("parallel", "parallel")),
    )(core, mine, got)


class _ChipExchange:
    def __init__(self, ins, outs, send_sems, recv_sems, loc_sems):
        self.ins, self.outs, self.send_sems, self.recv_sems, self.loc_sems = ins, outs, send_sems, recv_sems, loc_sems
        self.x, self.y, self.c, _, self.chips = _mesh_place()
        self.mine = 2 * self.x + self.y

    def own(self, a):
        return pltpu.make_async_copy(self.ins[a].at[self.mine], self.outs[a].at[self.mine], self.loc_sems.at[a])

    def copy(self, a, j, lands_in):
        chip = self.chips[j]
        return pltpu.make_async_remote_copy(
            src_ref=self.ins[a].at[2 * chip[0] + chip[1]], dst_ref=self.outs[a].at[lands_in],
            send_sem=self.send_sems.at[a, j], recv_sem=self.recv_sems.at[a, j], device_id=(*chip, self.c),
            device_id_type=MESH_ID)

    def start(self):
        for a in range(len(self.ins)):
            self.own(a).start()
            for j in range(N_CHIP - 1):
                self.copy(a, j, self.mine).start()

    def finish(self):
        for a in range(len(self.ins)):
            for j, chip in enumerate(self.chips):
                self.copy(a, j, self.mine).wait_send()
                self.copy(a, j, 2 * chip[0] + chip[1]).wait_recv()
            self.own(a).wait()

    @staticmethod
    def sems(n):
        return [pltpu.SemaphoreType.DMA((n, N_CHIP - 1)), pltpu.SemaphoreType.DMA((n, N_CHIP - 1)),
                pltpu.SemaphoreType.DMA((n,))]


def _inproj_first(x, ln_g, w_top, wab_top, late):
    s, d = x.shape
    kh, n = w_top.shape
    tm = _tile(s, (512, 256, 128))
    tn = _tile(n, (1664, 512, 128))
    nl = len(late)
    ni, nj = s // tm, n // tn

    def body(*refs):
        x_ref, g_ref, w_ref, wab_ref = refs[:4]
        proj_ref, ab_ref, ht_ref, hhi_ref = refs[4 + nl:8 + nl]
        hs = refs[8 + 2 * nl]
        gather = _Gather(refs[4:4 + nl], refs[8 + nl:8 + 2 * nl], *refs[9 + 2 * nl:])
        step = pl.program_id(0) * nj + pl.program_id(1)

        @pl.when(step == 0)
        def _():
            gather.start()

        @pl.when(pl.program_id(1) == 0)
        def _():
            xv = x_ref[...]
            r = lax.rsqrt(jnp.mean(xv * xv, axis=-1, keepdims=True) + EPS)
            hf = xv * r * g_ref[...]
            h = hf.astype(BF16)
            hs[...] = h[:, 0:kh]
            hhi_ref[...] = h[:, kh:d]
            ht_ref[...] = hf.T.astype(BF16)
            ab_ref[...] = jnp.dot(h[:, 0:kh], wab_ref[...], preferred_element_type=F32)

        proj_ref[...] = jnp.dot(hs[...], w_ref[...], preferred_element_type=F32)

        @pl.when(step == ni * nj - 1)
        def _():
            gather.finish()

    any_spec = pl.BlockSpec(memory_space=pl.ANY)
    res = pl.pallas_call(
        body, name="inproj_first", grid=(ni, nj),
        in_specs=[pl.BlockSpec((tm, d), lambda i, j: (i, 0)), pl.BlockSpec((1, d), lambda i, j: (0, 0)),
                  pl.BlockSpec((kh, tn), lambda i, j: (0, j)), pl.BlockSpec((kh, LANE), lambda i, j: (0, 0))]
        + [any_spec] * nl,
        out_specs=[pl.BlockSpec((tm, tn), lambda i, j: (i, j)), pl.BlockSpec((tm, LANE), lambda i, j: (i, 0)),
                   pl.BlockSpec((d, tm), lambda i, j: (0, i)), pl.BlockSpec((tm, d - kh), lambda i, j: (i, 0))]
        + [any_spec] * nl,
        out_shape=[_sds((s, n)), _sds((s, LANE)), _sds((d, s), BF16), _sds((s, d - kh), BF16)]
        + [_sds((N_DEV,) + a.shape, a.dtype) for a in late],
        scratch_shapes=[pltpu.VMEM((tm, kh), BF16)] + _Gather.sems(nl),
        compiler_params=_params(("arbitrary", "arbitrary")),
    )(x, ln_g, w_top, wab_top, *late)
    return res[0], res[1], res[2], res[3], res[4:]


def _inproj_second(h_hi, w_bot, wab_bot, proj_a, ab_a):
    s, kh = h_hi.shape
    n = w_bot.shape[1]
    tm = _tile(s, (512, 256, 128))
    tn = _tile(n, (1664, 512, 128))

    def body(h_ref, w_ref, wab_ref, pa_ref, aba_ref, proj_ref, ab_ref):
        @pl.when(pl.program_id(1) == 0)
        def _():
            ab_ref[...] = aba_ref[...] + jnp.dot(h_ref[...], wab_ref[...], preferred_element_type=F32)

        proj_ref[...] = pa_ref[...] + jnp.dot(h_ref[...], w_ref[...], preferred_element_type=F32)

    tile = pl.BlockSpec((tm, tn), lambda i, j: (i, j))
    abs_ = pl.BlockSpec((tm, LANE), lambda i, j: (i, 0))
    return pl.pallas_call(
        body, name="inproj_second", grid=(s // tm, n // tn),
        in_specs=[pl.BlockSpec((tm, kh), lambda i, j: (i, 0)), pl.BlockSpec((kh, tn), lambda i, j: (0, j)),
                  pl.BlockSpec((kh, LANE), lambda i, j: (0, 0)), tile, abs_],
        out_specs=[tile, abs_], out_shape=[_sds((s, n)), _sds((s, LANE))],
        input_output_aliases={3: 0, 4: 1},
        compiler_params=_params(("parallel", "arbitrary")),
    )(h_hi, w_bot, wab_bot, proj_a, ab_a)


def _matmul_acc(a, b, name):
    m, k = a.shape
    n = b.shape[1]
    tm = _tile(m, (2048, 1024, 512, 256, 128))
    tn = _tile(n, (1024, 512, 256, 128))
    tk = _tile(k, (1024, 512, 256, 128))
    nk = k // tk

    def body(a_ref, b_ref, o_ref, acc):
        @pl.when(pl.program_id(2) == 0)
        def _():
            acc[...] = jnp.zeros_like(acc)

        acc[...] += jnp.dot(a_ref[...], b_ref[...], preferred_element_type=F32)

        @pl.when(pl.program_id(2) == nk - 1)
        def _():
            o_ref[...] = acc[...].astype(BF16)

    return pl.pallas_call(
        body, name=name, grid=(m // tm, n // tn, nk),
        in_specs=[pl.BlockSpec((tm, tk), lambda i, j, l: (i, l)), pl.BlockSpec((tk, tn), lambda i, j, l: (l, j))],
        out_specs=pl.BlockSpec((tm, tn), lambda i, j, l: (i, j)),
        out_shape=_sds((m, n), BF16), scratch_shapes=[pltpu.VMEM((tm, tn), F32)],
        compiler_params=_params(("parallel", "parallel", "arbitrary")),
    )(a, b)


def _matmul_tn(a, b, name):
    k, m = a.shape
    n = b.shape[1]
    tm = _tile(m, (1024, 512, 256, 128))
    tn = _tile(n, (1024, 512, 256, 128))
    tk = _tile(k, (1024, 512, 256, 128))
    nk = k // tk

    def body(a_ref, b_ref, o_ref, acc):
        @pl.when(pl.program_id(2) == 0)
        def _():
            acc[...] = jnp.zeros_like(acc)

        acc[...] += _mm_tn(a_ref[...], b_ref[...])

        @pl.when(pl.program_id(2) == nk - 1)
        def _():
            o_ref[...] = acc[...].astype(BF16)

    return pl.pallas_call(
        body, name=name, grid=(m // tm, n // tn, nk),
        in_specs=[pl.BlockSpec((tk, tm), lambda i, j, l: (l, i)), pl.BlockSpec((tk, tn), lambda i, j, l: (l, j))],
        out_specs=pl.BlockSpec((tm, tn), lambda i, j, l: (i, j)),
        out_shape=_sds((m, n), BF16), scratch_shapes=[pltpu.VMEM((tm, tn), F32)],
        compiler_params=_params(("parallel", "parallel", "arbitrary")),
    )(a, b)


def _dh_rms(pieces, w_rows, wab_rows, x, dx2, ln_g, chip_sums):
    s, d = x.shape
    npc = len(pieces)
    nx = len(chip_sums)
    tm = _tile(s, (256, 128))
    ni = s // tm
    widths = [p.shape[1] for p in pieces[:-1]]
    offs = [sum(widths[:p]) for p in range(npc - 1)]
    nw = len(w_rows)
    nin = npc + 2 * nw + 3

    def body(*refs):
        p_refs = refs[:npc]
        w_refs = refs[npc:npc + nw]
        wab_refs = refs[npc + nw:npc + 2 * nw]
        x_ref, dx2_ref, g_ref = refs[npc + 2 * nw:nin]
        gx_ref, dg_ref = refs[nin + nx:nin + nx + 2]
        exch = _ChipExchange(refs[nin:nin + nx], refs[nin + nx + 2:nin + 2 * nx + 2], *refs[nin + 2 * nx + 2:])
        step = pl.program_id(0)

        @pl.when(step == 0)
        def _():
            dg_ref[...] = jnp.zeros_like(dg_ref)
            exch.start()

        cols = []
        for w_ref, wab_ref in zip(w_refs, wab_refs):
            part = _mm_nt(p_refs[npc - 1][...], wab_ref[...])
            for p in range(npc - 1):
                part += _mm_nt(p_refs[p][...], w_ref[:, offs[p]:offs[p] + widths[p]])
            cols.append(part)
        dhv = jnp.concatenate(cols, axis=1)
        xv = x_ref[...]
        r = lax.rsqrt(jnp.mean(xv * xv, axis=-1, keepdims=True) + EPS)
        xhat = xv * r
        dg_ref[...] += _colsum(dhv * xhat)
        dxh = dhv * g_ref[...]
        gx_ref[...] = dx2_ref[...] + r * (dxh - xhat * jnp.mean(dxh * xhat, axis=-1, keepdims=True))

        @pl.when(step == ni - 1)
        def _():
            exch.finish()

    any_spec = pl.BlockSpec(memory_space=pl.ANY)
    row = pl.BlockSpec((tm, d), lambda i: (i, 0))
    vec = pl.BlockSpec((1, d), lambda i: (0, 0))
    once = lambda a: pl.BlockSpec(a.shape, lambda i: (0, 0), pipeline_mode=pl.Buffered(1))
    in_specs = [pl.BlockSpec((tm, p.shape[1]), lambda i: (i, 0)) for p in pieces]
    in_specs += [once(w) for w in w_rows] + [once(w) for w in wab_rows] + [row, row, vec] + [any_spec] * nx
    res = pl.pallas_call(
        body, name="dh_rms", grid=(ni,), in_specs=in_specs,
        out_specs=[row, vec] + [any_spec] * nx,
        out_shape=[_sds((s, d)), _sds((1, d))] + [_sds(p.shape, p.dtype) for p in chip_sums],
        scratch_shapes=_ChipExchange.sems(nx),
        compiler_params=_params(("arbitrary",)),
    )(*pieces, *w_rows, *wab_rows, x, dx2, ln_g, *chip_sums)
    return res[0], res[1], res[2:]


def _final(x, tgt, out_b, out_a, out_c, w_out, final_g):
    s, d = x.shape
    tm = _tile(s, (256, 128))

    def body(x_ref, t_ref, b_ref, a_ref, c_ref, w_ref, g_ref, dx2_ref, dx2b_ref, dm_ref, loss_ref, dg_ref):
        @pl.when(pl.program_id(0) == 0)
        def _():
            loss_ref[...] = jnp.zeros_like(loss_ref)
            dg_ref[...] = jnp.zeros_like(dg_ref)

        x2 = x_ref[...]
        x2 += jnp.dot(b_ref[...], w_ref[0:DN_W, :], preferred_element_type=F32)
        x2 += jnp.dot(a_ref[...], w_ref[DN_W:DN_W + GMLP_W, :], preferred_element_type=F32)
        x2 += jnp.dot(c_ref[...], w_ref[DN_W + GMLP_W:MIX_W, :], preferred_element_type=F32)
        r = lax.rsqrt(jnp.mean(x2 * x2, axis=-1, keepdims=True) + EPS)
        xhat = x2 * r
        g = g_ref[...]
        err = xhat * g - t_ref[...]
        tok = 0.5 * jnp.mean(err * err, axis=-1, keepdims=True)
        loss_ref[...] += jnp.broadcast_to(_colsum(tok), loss_ref.shape)
        dy = err * (1.0 / d)
        dg_ref[...] += _colsum(dy * xhat)
        dxh = dy * g
        dx2 = r * (dxh - xhat * jnp.mean(dxh * xhat, axis=-1, keepdims=True))
        dx2_ref[...] = dx2
        dx2b = dx2.astype(BF16)
        dx2b_ref[...] = dx2b
        dm_ref[...] = _mm_nt(dx2b, w_ref[...])

    row = pl.BlockSpec((tm, d), lambda i: (i, 0))
    vec = pl.BlockSpec((1, d), lambda i: (0, 0))
    return pl.pallas_call(
        body, name="final", grid=(s // tm,),
        in_specs=[row, row, pl.BlockSpec((tm, DN_W), lambda i: (i, 0)), pl.BlockSpec((tm, GMLP_W), lambda i: (i, 0)),
                  pl.BlockSpec((tm, XA_W), lambda i: (i, 0)), pl.BlockSpec((MIX_W, d), lambda i: (0, 0)), vec],
        out_specs=[row, row, pl.BlockSpec((tm, MIX_W), lambda i: (i, 0)), pl.BlockSpec((1, LANE), lambda i: (0, 0)), vec],
        out_shape=[_sds((s, d)), _sds((s, d), BF16), _sds((s, MIX_W)), _sds((1, LANE)), _sds((1, d))],
        compiler_params=_params(("arbitrary",)),
    )(x, tgt, out_b, out_a, out_c, w_out, final_g)


GU_BLK = (4 * DN_W) // GMLP_W


def _gmlp_norm(gv, lng, lnb):
    va = _gelu(gv)
    mu = jnp.mean(va, axis=-1, keepdims=True)
    xc = va - mu
    rstd = lax.rsqrt(jnp.mean(xc * xc, axis=-1, keepdims=True) + EPS)
    vhat = xc * rstd
    return vhat, rstd, vhat * lng + lnb


def _gmlp_fwd(proj, lng, lnb, ws, bs_t):
    s = proj.shape[0]
    tm = _tile(s, (512, 256, 128))

    def body(u_ref, v_ref, z_ref, lng_ref, lnb_ref, ws_ref, bst_ref, o_ref):
        _, _, vn = _gmlp_norm(v_ref[...], lng_ref[...], lnb_ref[...])
        tri = _iota2((GMLP_T, GMLP_T), 0) >= _iota2((GMLP_T, GMLP_T), 1)
        for g in range(GMLP_G):
            cs = slice(g * HEAD, (g + 1) * HEAD)
            w = jnp.where(tri, ws_ref[g], 0.0).astype(BF16)
            b = bst_ref[:, g:g + 1]
            for c in range(tm // GMLP_T):
                rs = slice(c * GMLP_T, (c + 1) * GMLP_T)
                sg = _mm(w, vn[rs, cs]) + b
                o_ref[rs, cs] = (_gelu(u_ref[rs, cs]) * sg * _silu(z_ref[rs, cs])).astype(BF16)

    col = lambda k: pl.BlockSpec((tm, GMLP_W), lambda i: (i, GU_BLK + k))
    vec = pl.BlockSpec((1, GMLP_W), lambda i: (0, 0))
    return pl.pallas_call(
        body, name="gmlp_fwd", grid=(s // tm,),
        in_specs=[col(0), col(1), col(2), vec, vec, pl.BlockSpec((GMLP_G, GMLP_T, GMLP_T), lambda i: (0, 0, 0)),
                  pl.BlockSpec((GMLP_T, GMLP_G), lambda i: (0, 0))],
        out_specs=pl.BlockSpec((tm, GMLP_W), lambda i: (i, 0)), out_shape=_sds((s, GMLP_W), BF16),
        compiler_params=_params(("parallel",)),
    )(proj, proj, proj, lng, lnb, ws, bs_t)


def _gmlp_bwd(proj, dmixed, lng, lnb, ws, bs_t):
    s = proj.shape[0]
    tm = _tile(s, (512, 256, 128))

    def body(u_ref, v_ref, z_ref, d_ref, lng_ref, lnb_ref, ws_ref, bst_ref,
             dp_ref, dws_ref, dbst_ref, dlng_ref, dlnb_ref, dvn):
        @pl.when(pl.program_id(0) == 0)
        def _():
            dws_ref[...] = jnp.zeros_like(dws_ref)
            dbst_ref[...] = jnp.zeros_like(dbst_ref)
            dlng_ref[...] = jnp.zeros_like(dlng_ref)
            dlnb_ref[...] = jnp.zeros_like(dlnb_ref)

        gv = v_ref[...]
        lng_v = lng_ref[...]
        vhat, rstd, vn = _gmlp_norm(gv, lng_v, lnb_ref[...])
        tri = _iota2((GMLP_T, GMLP_T), 0) >= _iota2((GMLP_T, GMLP_T), 1)
        for g in range(GMLP_G):
            cs = slice(g * HEAD, (g + 1) * HEAD)
            w = jnp.where(tri, ws_ref[g], 0.0).astype(BF16)
            b = bst_ref[:, g:g + 1]
            dw_acc = jnp.zeros((GMLP_T, GMLP_T), F32)
            db_acc = jnp.zeros((GMLP_T, 1), F32)
            for c in range(tm // GMLP_T):
                rs = slice(c * GMLP_T, (c + 1) * GMLP_T)
                vn_b = vn[rs, cs]
                sg = _mm(w, vn_b) + b
                gu = u_ref[rs, cs]
                gz = z_ref[rs, cs]
                da = d_ref[rs, cs]
                uact = _gelu(gu)
                sz = _silu(gz)
                ds = da * uact * sz
                dp_ref[rs, cs] = (da * sg * sz * _gelu_grad(gu)).astype(BF16)
                dp_ref[rs, 2 * GMLP_W + g * HEAD:2 * GMLP_W + (g + 1) * HEAD] = (da * uact * sg * _silu_grad(gz)).astype(BF16)
                dw_acc += _mm_nt(ds, vn_b)
                db_acc += _rowsum(ds)
                dvn[rs, cs] = _mm_tn(w, ds)
            dws_ref[g] += jnp.where(tri, dw_acc, 0.0)
            dbst_ref[:, g:g + 1] += db_acc
        dvn_v = dvn[...]
        dlng_ref[...] += _colsum(dvn_v * vhat)
        dlnb_ref[...] += _colsum(dvn_v)
        dvh = dvn_v * lng_v
        dva = rstd * (dvh - jnp.mean(dvh, axis=-1, keepdims=True) - vhat * jnp.mean(dvh * vhat, axis=-1, keepdims=True))
        dp_ref[:, GMLP_W:2 * GMLP_W] = (dva * _gelu_grad(gv)).astype(BF16)

    col = lambda k: pl.BlockSpec((tm, GMLP_W), lambda i: (i, GU_BLK + k))
    vec = pl.BlockSpec((1, GMLP_W), lambda i: (0, 0))
    wsp = pl.BlockSpec((GMLP_G, GMLP_T, GMLP_T), lambda i: (0, 0, 0))
    bsp = pl.BlockSpec((GMLP_T, GMLP_G), lambda i: (0, 0))
    return pl.pallas_call(
        body, name="gmlp_bwd", grid=(s // tm,),
        in_specs=[col(0), col(1), col(2), pl.BlockSpec((tm, GMLP_W), lambda i: (i, DN_W // GMLP_W)), vec, vec, wsp, bsp],
        out_specs=[pl.BlockSpec((tm, 3 * GMLP_W), lambda i: (i, 0)), wsp, bsp, vec, vec],
        out_shape=[_sds((s, 3 * GMLP_W), BF16), _sds((GMLP_G, GMLP_T, GMLP_T)), _sds((GMLP_T, GMLP_G)),
                   _sds((1, GMLP_W)), _sds((1, GMLP_W))],
        scratch_shapes=[pltpu.VMEM((tm, GMLP_W), F32)],
        compiler_params=_params(("arbitrary",)),
    )(proj, proj, proj, dmixed, lng, lnb, ws, bs_t)


CQ_BLK = (4 * DN_W + 3 * GMLP_W) // XA_W


def _memkv_fwd(mem, g, w_kv):
    nm, d = mem.shape

    def body(m_ref, g_ref, w_ref, kv_ref):
        mv = m_ref[...]
        r = lax.rsqrt(jnp.mean(mv * mv, axis=-1, keepdims=True) + EPS)
        kv_ref[...] = _mm(mv * r * g_ref[...], w_ref[...])

    return pl.pallas_call(body, name="memkv_fwd", out_shape=_sds((nm, 2 * XA_W)), compiler_params=_params())(mem, g, w_kv)


def _memkv_bwd(mem, g, w_kv, dkv):
    nm, d = mem.shape

    def body(m_ref, g_ref, w_ref, dkv_ref, dw_ref, dg_ref):
        mv = m_ref[...]
        r = lax.rsqrt(jnp.mean(mv * mv, axis=-1, keepdims=True) + EPS)
        xhat = mv * r
        dkv_v = dkv_ref[...]
        dw_ref[...] = _mm_tn(xhat * g_ref[...], dkv_v)
        dg_ref[...] = _colsum(_mm_nt(dkv_v, w_ref[...]) * xhat)

    return pl.pallas_call(body, name="memkv_bwd", out_shape=[_sds((d, 2 * XA_W)), _sds((1, d))],
                          compiler_params=_params())(mem, g, w_kv, dkv)


def _xattn_probs(q, mk):
    sc = _mm_nt(q, mk) * (HEAD ** -0.5)
    e = jnp.exp(sc - jnp.max(sc, axis=-1, keepdims=True))
    return e / _rowsum(e)


def _xattn_fwd(proj, mkv):
    s = proj.shape[0]
    nm = mkv.shape[0]
    tm = _tile(s, (512, 256, 128))

    def body(q_ref, z_ref, kv_ref, o_ref):
        for h in range(XA_H):
            cs = slice(h * HEAD, (h + 1) * HEAD)
            p = _xattn_probs(q_ref[:, cs], kv_ref[:, cs])
            ctx = _mm(p, kv_ref[:, XA_W + h * HEAD:XA_W + (h + 1) * HEAD])
            o_ref[:, cs] = (ctx * _silu(z_ref[:, cs])).astype(BF16)

    col = lambda k: pl.BlockSpec((tm, XA_W), lambda i: (i, CQ_BLK + k))
    return pl.pallas_call(
        body, name="xattn_fwd", grid=(s // tm,),
        in_specs=[col(0), col(1), pl.BlockSpec((nm, 2 * XA_W), lambda i: (0, 0))],
        out_specs=pl.BlockSpec((tm, XA_W), lambda i: (i, 0)), out_shape=_sds((s, XA_W), BF16),
        compiler_params=_params(("parallel",)),
    )(proj, proj, mkv)


def _xattn_bwd(proj, dmixed, mkv):
    s = proj.shape[0]
    nm = mkv.shape[0]
    tm = _tile(s, (512, 256, 128))

    def body(q_ref, z_ref, d_ref, kv_ref, dp_ref, dkv_ref):
        @pl.when(pl.program_id(0) == 0)
        def _():
            dkv_ref[...] = jnp.zeros_like(dkv_ref)

        for h in range(XA_H):
            cs = slice(h * HEAD, (h + 1) * HEAD)
            vs = slice(XA_W + h * HEAD, XA_W + (h + 1) * HEAD)
            q = q_ref[:, cs]
            z = z_ref[:, cs]
            mk = kv_ref[:, cs]
            mv = kv_ref[:, vs]
            p = _xattn_probs(q, mk)
            ctx = _mm(p, mv)
            dc = d_ref[:, cs]
            dctx = dc * _silu(z)
            dp_ref[:, vs] = (dc * ctx * _silu_grad(z)).astype(BF16)
            dp = _mm_nt(dctx, mv)
            dkv_ref[:, vs] += _mm_tn(p, dctx)
            ds = p * (dp - _rowsum(dp * p)) * (HEAD ** -0.5)
            dp_ref[:, cs] = _mm(ds, mk).astype(BF16)
            dkv_ref[:, cs] += _mm_tn(ds, q)

    col = lambda k: pl.BlockSpec((tm, XA_W), lambda i: (i, CQ_BLK + k))
    kvs = pl.BlockSpec((nm, 2 * XA_W), lambda i: (0, 0))
    return pl.pallas_call(
        body, name="xattn_bwd", grid=(s // tm,),
        in_specs=[col(0), col(1), pl.BlockSpec((tm, XA_W), lambda i: (i, (DN_W + GMLP_W) // XA_W)), kvs],
        out_specs=[pl.BlockSpec((tm, 2 * XA_W), lambda i: (i, 0)), kvs],
        out_shape=[_sds((s, 2 * XA_W), BF16), _sds((nm, 2 * XA_W))],
        compiler_params=_params(("arbitrary",)),
    )(proj, proj, dmixed, mkv)


def _softplus(x):
    return jnp.maximum(x, 0.0) + jnp.log1p(jnp.exp(-jnp.abs(x)))


def _dn_pre(proj, ab, conv_w, alog_row, dt_row):
    s = proj.shape[0]
    tm = _tile(s, (256, 128))
    w3 = 3 * DN_W

    def body(x_ref, halo_ref, ab_ref, cw_ref, al_ref, dt_ref, q_ref, k_ref, v_ref, gb_ref, gbt_ref, yc_ref):
        i = pl.program_id(0)
        xv = x_ref[...]
        cat = jnp.concatenate([jnp.where(i > 0, halo_ref[...], 0.0), xv[0:HALO]], axis=0)
        yc = cw_ref[DN_K - 1:DN_K, :] * xv
        top = cw_ref[DN_K - 1:DN_K, :] * xv[0:HALO]
        for t in range(DN_K - 1):
            back = DN_K - 1 - t
            yc += cw_ref[t:t + 1, :] * pltpu.roll(xv, back, 0)
            top += cw_ref[t:t + 1, :] * pltpu.roll(cat, back, 0)[HALO:2 * HALO]
        yc = jnp.concatenate([top, yc[HALO:tm]], axis=0)
        yc_ref[...] = yc
        act = _silu(yc)
        for h in range(DN_H):
            cs = slice(h * HEAD, (h + 1) * HEAD)
            qa = act[:, cs]
            q_ref[:, cs] = qa * (lax.rsqrt(_rowsum(qa * qa) + EPS) * (HEAD ** -0.5))
            ka = act[:, DN_W + h * HEAD:DN_W + (h + 1) * HEAD]
            k_ref[:, cs] = ka * lax.rsqrt(_rowsum(ka * ka) + EPS)
        v_ref[...] = act[:, 2 * DN_W:w3]
        abv = ab_ref[...]
        lane = _iota2((tm, LANE), 1)
        g = jnp.where(lane < DN_H, -jnp.exp(al_ref[...]) * _softplus(abv + dt_ref[...]), 0.0)
        gc = _mm_hi(_chunk_tri(tm, False), g)
        gbv = jnp.where(lane < DN_H, gc, jnp.where(lane < 2 * DN_H, jax.nn.sigmoid(abv), 0.0))
        gb_ref[...] = gbv
        for c in range(tm // CH):
            gbt_ref[c] = gbv[c * CH:(c + 1) * CH, :].T[0:2 * DN_H, :]

    hb = tm // HALO
    row = lambda w: pl.BlockSpec((tm, w), lambda i: (i, 0))
    vec = pl.BlockSpec((1, LANE), lambda i: (0, 0))
    return pl.pallas_call(
        body, name="dn_pre", grid=(s // tm,),
        in_specs=[row(w3), pl.BlockSpec((HALO, w3), lambda i: (jnp.maximum(i * hb - 1, 0), 0)), row(LANE),
                  pl.BlockSpec((DN_K, w3), lambda i: (0, 0)), vec, vec],
        out_specs=[row(DN_W), row(DN_W), row(DN_W), row(LANE), pl.BlockSpec((tm // CH, 2 * DN_H, CH), lambda i: (i, 0, 0)),
                   row(w3)],
        out_shape=[_sds((s, DN_W)), _sds((s, DN_W)), _sds((s, DN_W)), _sds((s, LANE)), _sds((s // CH, 2 * DN_H, CH)),
                   _sds((s, w3))],
        compiler_params=_params(("parallel",)),
    )(proj, proj, ab, conv_w, alog_row, dt_row)


HEADS = tuple(range(DN_H))


def _hcols(h):
    return slice(h * HEAD, (h + 1) * HEAD)


def _chunk_scalings(k, v, gbv, gbt, h):
    gc = jnp.broadcast_to(gbv[:, h:h + 1], (CH, HEAD))
    beta = jnp.broadcast_to(gbv[:, DN_H + h:DN_H + h + 1], (CH, HEAD))
    gr = gbt[h:h + 1, :]
    ii = _iota2((CH, CH), 0)
    jj = _iota2((CH, CH), 1)
    dec = jnp.exp(jnp.where(ii >= jj, gc[:, 0:CH] - gr, -1e30))
    eg = jnp.exp(gc)
    gl = gr[:, CH - 1:CH]
    kb = k * beta
    return dict(beta=beta, dec=dec, eg=eg, gl=gl, ekd=jnp.exp(gl - gc), kb=kb, vb=v * beta, kbe=kb * eg)


def _chunk_scores(m, q, k):
    kq = _mm_nt(jnp.concatenate([m["kb"], q], axis=0), k)
    strict = _iota2((CH, CH), 0) > _iota2((CH, CH), 1)
    return jnp.where(strict, kq[0:CH] * m["dec"], 0.0), kq[CH:2 * CH] * m["dec"]


def _dn_local(q, k, v, gb, gbt):
    s = q.shape[0]
    cpb = 4 if (s // CH) % 4 == 0 else 1
    tb = cpb * CH
    nblk = s // tb

    def body(q_ref, k_ref, v_ref, gb_ref, gbt_ref, u_ref, w_ref, qg_ref, kd_ref, t_ref, ai_ref, egl_ref):
        def chunk(c, carry):
            r0 = pl.multiple_of(c * CH, CH)
            rows = pl.ds(r0, CH)
            gbv = gb_ref[rows, :]
            gbt_v = gbt_ref[c]
            qs = [q_ref[rows, _hcols(h)] for h in HEADS]
            ks = [k_ref[rows, _hcols(h)] for h in HEADS]
            ms = [_chunk_scalings(ks[h], v_ref[rows, _hcols(h)], gbv, gbt_v, h) for h in HEADS]
            for h in HEADS:
                qg_ref[rows, _hcols(h)] = (qs[h] * ms[h]["eg"]).astype(BF16)
                kd_ref[rows, _hcols(h)] = (ks[h] * ms[h]["ekd"]).astype(BF16)
                egl_ref[c, h:h + 1, :] = jnp.broadcast_to(jnp.exp(ms[h]["gl"]), (1, LANE))
            sc = [_chunk_scores(ms[h], qs[h], ks[h]) for h in HEADS]
            for h in HEADS:
                ai_ref[h, rows, :] = sc[h][1]
            eye = jnp.where(_iota2((CH, CH), 0) == _iota2((CH, CH), 1), 1.0, 0.0).astype(F32)
            ts = [eye - sc[h][0] for h in HEADS]
            ps = [_mm_3x(sc[h][0], sc[h][0]) for h in HEADS]
            ts = [ts[h] + _mm_3x(ts[h], ps[h]) for h in HEADS]
            for _ in range(4):
                ps = [_mm(ps[h], ps[h]) for h in HEADS]
                ts = [ts[h] + _mm(ts[h], ps[h]) for h in HEADS]
            for h in HEADS:
                t_ref[h, rows, :] = ts[h]
                uw = _mm(ts[h], jnp.concatenate([ms[h]["vb"], ms[h]["kbe"]], axis=1))
                u_ref[rows, _hcols(h)] = uw[:, 0:HEAD]
                w_ref[rows, _hcols(h)] = uw[:, HEAD:2 * HEAD].astype(BF16)
            return carry

        lax.fori_loop(0, cpb, chunk, 0, unroll=4)

    row = pl.BlockSpec((tb, DN_W), lambda i: (i, 0))
    sq = pl.BlockSpec((DN_H, tb, CH), lambda i: (0, i, 0))
    return pl.pallas_call(
        body, name="dn_local", grid=(nblk,),
        in_specs=[row, row, row, pl.BlockSpec((tb, LANE), lambda i: (i, 0)),
                  pl.BlockSpec((cpb, 2 * DN_H, CH), lambda i: (i, 0, 0))],
        out_specs=[row, row, row, row, sq, sq, pl.BlockSpec((cpb, DN_H, LANE), lambda i: (i, 0, 0))],
        out_shape=[_sds((s, DN_W)), _sds((s, DN_W), BF16), _sds((s, DN_W), BF16), _sds((s, DN_W), BF16),
                   _sds((DN_H, s, CH)), _sds((DN_H, s, CH)), _sds((s // CH, DN_H, LANE))],
        compiler_params=_params(("parallel",)),
    )(q, k, v, gb, gbt)


def _scan_cpb(s):
    return 8 if (s // CH) % 8 == 0 else 1


def _dn_scan(u, w, qg, kd, ai, egl, proj, norm_g):
    s = u.shape[0]
    cpb = _scan_cpb(s)
    tb = cpb * CH
    nblk = s // tb

    def body(u_ref, w_ref, qg_ref, kd_ref, ai_ref, egl_ref, z_ref, ng_ref, o_ref, vn_ref, st_ref, ob_ref, state):
        @pl.when(pl.program_id(0) == 0)
        def _():
            state[...] = jnp.zeros_like(state)

        ng = ng_ref[...]

        def chunk(c, carry):
            r0 = pl.multiple_of(c * CH, CH)
            rows = pl.ds(r0, CH)
            sts = [state[h] for h in HEADS]
            stb = [sts[h].astype(BF16) for h in HEADS]
            for h in HEADS:
                st_ref[c, h] = stb[h]
            vns = [u_ref[rows, _hcols(h)] - jnp.dot(w_ref[rows, _hcols(h)], stb[h], preferred_element_type=F32)
                   for h in HEADS]
            vnb = [vns[h].astype(BF16) for h in HEADS]
            for h in HEADS:
                state[h] = sts[h] * egl_ref[c, h:h + 1, :] + _mm_tn(kd_ref[rows, _hcols(h)], vnb[h])
            os_ = [jnp.dot(qg_ref[rows, _hcols(h)], stb[h], preferred_element_type=F32) + _mm(ai_ref[h, rows, :], vnb[h])
                   for h in HEADS]
            for h in HEADS:
                o = os_[h]
                vn_ref[rows, _hcols(h)] = vnb[h]
                o_ref[rows, _hcols(h)] = o
                r = lax.rsqrt(jnp.mean(o * o, axis=-1, keepdims=True) + EPS)
                ob_ref[rows, _hcols(h)] = (o * r * ng * _silu(z_ref[rows, _hcols(h)])).astype(BF16)
            return carry

        lax.fori_loop(0, cpb, chunk, 0, unroll=4)

    row = pl.BlockSpec((tb, DN_W), lambda i: (i, 0))
    return pl.pallas_call(
        body, name="dn_scan", grid=(nblk,),
        in_specs=[row, row, row, row, pl.BlockSpec((DN_H, tb, CH), lambda i: (0, i, 0)),
                  pl.BlockSpec((cpb, DN_H, LANE), lambda i: (i, 0, 0)), pl.BlockSpec((tb, DN_W), lambda i: (i, 3)),
                  pl.BlockSpec((1, HEAD), lambda i: (0, 0))],
        out_specs=[row, row, pl.BlockSpec((cpb, DN_H, HEAD, HEAD), lambda i: (i, 0, 0, 0)), row],
        out_shape=[_sds((s, DN_W)), _sds((s, DN_W), BF16), _sds((s // CH, DN_H, HEAD, HEAD), BF16), _sds((s, DN_W), BF16)],
        scratch_shapes=[pltpu.VMEM((DN_H, HEAD, HEAD), F32)],
        compiler_params=_params(("arbitrary",)),
    )(u, w, qg, kd, ai, egl, proj, norm_g)


def _dn_scan_bwd(dmixed, o, proj, norm_g, w, qg, kd, ai, egl):
    s = o.shape[0]
    cpb = _scan_cpb(s)
    tb = cpb * CH
    nblk = s // tb

    def body(dm_ref, o_ref, z_ref, ng_ref, w_ref, qg_ref, kd_ref, ai_ref, egl_ref,
             do_ref, dvn_ref, dst_ref, dz_ref, dng_ref, dstate):
        @pl.when(pl.program_id(0) == 0)
        def _():
            dstate[...] = jnp.zeros_like(dstate)
            dng_ref[...] = jnp.zeros_like(dng_ref)

        ng = ng_ref[...]

        def chunk(cc, carry):
            c = cpb - 1 - cc
            r0 = pl.multiple_of(c * CH, CH)
            rows = pl.ds(r0, CH)
            dng = jnp.zeros((1, HEAD), F32)
            dob = []
            for h in HEADS:
                cs = _hcols(h)
                o = o_ref[rows, cs]
                z = z_ref[rows, cs]
                db = dm_ref[rows, cs]
                r = lax.rsqrt(jnp.mean(o * o, axis=-1, keepdims=True) + EPS)
                ohat = o * r
                dz_ref[rows, cs] = (db * ohat * ng * _silu_grad(z)).astype(BF16)
                dyn = db * _silu(z)
                dng += _colsum(dyn * ohat)
                doh = dyn * ng
                do = r * (doh - ohat * jnp.mean(doh * ohat, axis=-1, keepdims=True))
                dob.append(do.astype(BF16))
                do_ref[rows, cs] = dob[h]
            dng_ref[...] += dng
            dsn = [dstate[h] for h in HEADS]
            dsb = [dsn[h].astype(BF16) for h in HEADS]
            for h in HEADS:
                dst_ref[c, h] = dsb[h]
            dvn = [(_mm_tn(ai_ref[h, rows, :], dob[h])
                    + jnp.dot(kd_ref[rows, _hcols(h)], dsb[h], preferred_element_type=F32)).astype(BF16) for h in HEADS]
            part = [_mm_tn(qg_ref[rows, _hcols(h)], dob[h]) + egl_ref[c, h:h + 1, :] * dsn[h] for h in HEADS]
            for h in HEADS:
                dvn_ref[rows, _hcols(h)] = dvn[h]
                dstate[h] = part[h] - _mm_tn(w_ref[rows, _hcols(h)], dvn[h])
            return carry

        lax.fori_loop(0, cpb, chunk, 0, unroll=4)

    rev = lambda i: (nblk - 1 - i, 0)
    row = pl.BlockSpec((tb, DN_W), rev)
    vec = pl.BlockSpec((1, HEAD), lambda i: (0, 0))
    return pl.pallas_call(
        body, name="dn_scan_bwd", grid=(nblk,),
        in_specs=[row, row, pl.BlockSpec((tb, DN_W), lambda i: (nblk - 1 - i, 3)), vec, row, row, row,
                  pl.BlockSpec((DN_H, tb, CH), lambda i: (0, nblk - 1 - i, 0)),
                  pl.BlockSpec((cpb, DN_H, LANE), lambda i: (nblk - 1 - i, 0, 0))],
        out_specs=[row, row, pl.BlockSpec((cpb, DN_H, HEAD, HEAD), lambda i: (nblk - 1 - i, 0, 0, 0)), row, vec],
        out_shape=[_sds((s, DN_W), BF16), _sds((s, DN_W), BF16), _sds((s // CH, DN_H, HEAD, HEAD), BF16),
                   _sds((s, DN_W), BF16), _sds((1, HEAD))],
        scratch_shapes=[pltpu.VMEM((DN_H, HEAD, HEAD), F32)],
        compiler_params=_params(("arbitrary",)),
    )(dmixed, o, proj, norm_g, w, qg, kd, ai, egl)


def _dn_local_bwd(q, k, v, gb, gbt, t, vn, st, dst, do, dvn):
    s = q.shape[0]
    cpb = 4 if (s // CH) % 4 == 0 else 1
    tb = cpb * CH
    nblk = s // tb

    def body(q_ref, k_ref, v_ref, gb_ref, gbt_ref, t_ref, vn_ref, st_ref, dst_ref, do_ref, dvn_ref,
             dq_ref, dk_ref, dv_ref, dgb_ref):
        lane = _iota2((CH, LANE), 1)
        last = _iota2((CH, 1), 0) == CH - 1

        def chunk(c, carry):
            r0 = pl.multiple_of(c * CH, CH)
            rows = pl.ds(r0, CH)
            gbv = gb_ref[rows, :]
            gbt_v = gbt_ref[c]
            strict = _iota2((CH, CH), 0) > _iota2((CH, CH), 1)
            qs = [q_ref[rows, _hcols(h)] for h in HEADS]
            ks = [k_ref[rows, _hcols(h)] for h in HEADS]
            vs = [v_ref[rows, _hcols(h)] for h in HEADS]
            ms = [_chunk_scalings(ks[h], vs[h], gbv, gbt_v, h) for h in HEADS]
            sts = [st_ref[c, h] for h in HEADS]
            dsn = [dst_ref[c, h] for h in HEADS]
            dob = [do_ref[rows, _hcols(h)].astype(BF16) for h in HEADS]
            dvnb = [dvn_ref[rows, _hcols(h)].astype(BF16) for h in HEADS]
            vnb = [vn_ref[rows, _hcols(h)].astype(BF16) for h in HEADS]
            tbf = [t_ref[h, rows, :].astype(BF16) for h in HEADS]
            sc = [_chunk_scores(ms[h], qs[h], ks[h]) for h in HEADS]
            xs_ = [_mm_nt(jnp.concatenate([dob[h], dvnb[h]], axis=0), sts[h]) for h in HEADS]
            dai = [_mm_nt(dob[h], vnb[h]) for h in HEADS]
            dkd = [_mm_nt(vnb[h], dsn[h]) for h in HEADS]
            dqg = [xs_[h][0:CH] for h in HEADS]
            duw = [jnp.concatenate([dvnb[h], (-xs_[h][CH:2 * CH]).astype(BF16)], axis=1) for h in HEADS]
            dt = [_mm_nt(duw[h], jnp.concatenate([ms[h]["vb"], ms[h]["kbe"]], axis=1)) for h in HEADS]
            dvk = [_mm_tn(tbf[h], duw[h]) for h in HEADS]
            tdt = [_mm_tn(tbf[h], dt[h]) for h in HEADS]
            da = [jnp.where(strict, -_mm_nt(tdt[h], tbf[h]), 0.0) for h in HEADS]
            dsc = [jnp.concatenate([da[h] * ms[h]["dec"], dai[h] * ms[h]["dec"]], axis=0) for h in HEADS]
            dkq = [_mm(dsc[h], ks[h]) for h in HEADS]
            dk1 = [_mm_tn(dsc[h], jnp.concatenate([ms[h]["kb"], qs[h]], axis=0)) for h in HEADS]
            dgb = jnp.zeros((CH, LANE), F32)
            for h in HEADS:
                m = ms[h]
                eg, ekd, beta = m["eg"], m["ekd"], m["beta"]
                dvb = dvk[h][:, 0:HEAD]
                dkbe = dvk[h][:, HEAD:2 * HEAD]
                kd = ks[h] * ekd
                dkb = dkq[h][0:CH] + dkbe * eg
                dq_ref[rows, _hcols(h)] = dkq[h][CH:2 * CH] + dqg[h] * eg
                dk_ref[rows, _hcols(h)] = dk1[h] + dkd[h] * ekd + dkb * beta
                dv_ref[rows, _hcols(h)] = dvb * beta
                dkd_kd = dkd[h] * kd
                dgl = (jnp.exp(m["gl"]) * _rowsum(_colsum(sts[h].astype(F32) * dsn[h].astype(F32)))
                       + _rowsum(_colsum(dkd_kd)))
                mm_ = da[h] * sc[h][0] + dai[h] * sc[h][1]
                dgc = (_rowsum(mm_ - mm_.T) + _rowsum(dqg[h] * qs[h] * eg - dkd_kd + dkbe * m["kbe"])
                       + jnp.where(last, dgl, 0.0))
                dbeta = _rowsum(dkb * ks[h] + dvb * vs[h])
                dgb = jnp.where(lane == h, dgc, jnp.where(lane == DN_H + h, dbeta, dgb))
            dgb_ref[rows, :] = dgb
            return carry

        lax.fori_loop(0, cpb, chunk, 0, unroll=2)

    row = pl.BlockSpec((tb, DN_W), lambda i: (i, 0))
    gbs = pl.BlockSpec((tb, LANE), lambda i: (i, 0))
    sts = pl.BlockSpec((cpb, DN_H, HEAD, HEAD), lambda i: (i, 0, 0, 0))
    return pl.pallas_call(
        body, name="dn_local_bwd", grid=(nblk,),
        in_specs=[row, row, row, gbs, pl.BlockSpec((cpb, 2 * DN_H, CH), lambda i: (i, 0, 0)),
                  pl.BlockSpec((DN_H, tb, CH), lambda i: (0, i, 0)), row, sts, sts, row, row],
        out_specs=[row, row, row, gbs],
        out_shape=[_sds((s, DN_W)), _sds((s, DN_W)), _sds((s, DN_W)), _sds((s, LANE))],
        compiler_params=_params(("parallel",)),
    )(q, k, v, gb, gbt, t, vn, st, dst, do, dvn)


def _dn_pre_bwd(proj, yc_all, ab, conv_w, alog_row, dt_row, dq, dk, dv, dgb):
    s = proj.shape[0]
    tm = _tile(s, (256, 128))
    w3 = 3 * DN_W
    nblk = s // tm

    def body(x_ref, yc_ref, ab_ref, cw_ref, al_ref, dt_ref, dq_ref, dk_ref, dv_ref, dgb_ref,
             dx_ref, dab_ref, dcw_ref, dal_ref, ddt_ref, exd, carry):
        i = pl.program_id(0)

        @pl.when(i == 0)
        def _():
            carry[...] = jnp.zeros_like(carry)
            dcw_ref[...] = jnp.zeros_like(dcw_ref)
            dal_ref[...] = jnp.zeros_like(dal_ref)
            ddt_ref[...] = jnp.zeros_like(ddt_ref)

        yc = yc_ref[...]
        sg = jax.nn.sigmoid(yc)
        act = yc * sg
        dact = sg * (1.0 + yc * (1.0 - sg))
        for h in range(DN_H):
            cs = slice(h * HEAD, (h + 1) * HEAD)
            ks = slice(DN_W + h * HEAD, DN_W + (h + 1) * HEAD)
            qa = act[:, cs]
            rq = lax.rsqrt(_rowsum(qa * qa) + EPS)
            qh = qa * rq
            dqv = dq_ref[:, cs]
            exd[0:tm, cs] = (HEAD ** -0.5) * rq * (dqv - qh * _rowsum(dqv * qh)) * dact[:, cs]
            ka = act[:, ks]
            rk = lax.rsqrt(_rowsum(ka * ka) + EPS)
            kh = ka * rk
            dkv = dk_ref[:, cs]
            exd[0:tm, ks] = rk * (dkv - kh * _rowsum(dkv * kh)) * dact[:, ks]
        exd[0:tm, 2 * DN_W:w3] = dv_ref[...] * dact[:, 2 * DN_W:w3]
        xv = x_ref[...]
        dyc = exd[...]
        cat = jnp.concatenate([dyc[tm - HALO:tm], carry[...]], axis=0)
        dcw_ref[DN_K - 1:DN_K, :] += _colsum(dyc * xv)
        dx = cw_ref[DN_K - 1:DN_K, :] * dyc
        for t in range(DN_K - 1):
            ahead = DN_K - 1 - t
            view = jnp.concatenate([pltpu.roll(dyc, tm - ahead, 0)[0:tm - HALO],
                                    pltpu.roll(cat, 2 * HALO - ahead, 0)[0:HALO]], axis=0)
            dcw_ref[t:t + 1, :] += _colsum(view * xv)
            dx += cw_ref[t:t + 1, :] * view
        dx_ref[...] = dx.astype(BF16)
        carry[...] = dyc[0:HALO]

        lane = _iota2((tm, LANE), 1)
        dgbv = dgb_ref[...]
        dg = _mm_hi(_chunk_tri(tm, True), jnp.where(lane < DN_H, dgbv, 0.0))
        abv = ab_ref[...]
        xa = abv + dt_ref[...]
        nea = -jnp.exp(al_ref[...])
        d_da = jnp.where(lane < DN_H, dg * nea * jax.nn.sigmoid(xa), 0.0)
        dal_ref[...] += _colsum(jnp.where(lane < DN_H, dg * nea * _softplus(xa), 0.0))
        ddt_ref[...] += _colsum(d_da)
        beta = jax.nn.sigmoid(abv)
        d_db = jnp.where((lane >= DN_H) & (lane < 2 * DN_H), dgbv * beta * (1.0 - beta), 0.0)
        dab_ref[...] = (d_da + d_db).astype(BF16)

    rev = lambda i: (nblk - 1 - i, 0)
    row = lambda w: pl.BlockSpec((tm, w), rev)
    vec = pl.BlockSpec((1, LANE), lambda i: (0, 0))
    cws = pl.BlockSpec((DN_K, w3), lambda i: (0, 0))
    return pl.pallas_call(
        body, name="dn_pre_bwd", grid=(nblk,),
        in_specs=[row(w3), row(w3), row(LANE), cws, vec, vec, row(DN_W), row(DN_W), row(DN_W), row(LANE)],
        out_specs=[row(w3), row(LANE), cws, vec, vec],
        out_shape=[_sds((s, w3), BF16), _sds((s, LANE), BF16), _sds((DN_K, w3)), _sds((1, LANE)), _sds((1, LANE))],
        scratch_shapes=[pltpu.VMEM((tm, w3), F32), pltpu.VMEM((HALO, w3), F32)],
        compiler_params=_params(("arbitrary",)),
    )(proj, yc_all, ab, conv_w, alog_row, dt_row, dq, dk, dv, dgb)


def _adam(parts, w, m, v, name):
    r, c = w.shape
    n_parts = parts.shape[0]
    small = n_parts * r * c * 4 <= 4 * 1024 * 1024
    tr = r if small else _tile(r, (128, 64, 32, 16, 8))

    def body(p_ref, w_ref, m_ref, v_ref, g_ref, d_ref, nm_ref, nv_ref):
        g = p_ref[0].astype(F32)
        for k in range(1, n_parts):
            g = g + p_ref[k].astype(F32)
        g_ref[...] = g
        mn = ADAM_B1 * m_ref[...] + (1.0 - ADAM_B1) * g
        vn = ADAM_B2 * v_ref[...] + (1.0 - ADAM_B2) * (g * g)
        m_hat = mn / (1.0 - ADAM_B1 ** ADAM_STEP)
        v_hat = vn / (1.0 - ADAM_B2 ** ADAM_STEP)
        d_ref[...] = -ADAM_LR * (m_hat / (jnp.sqrt(v_hat) + ADAM_EPS) + ADAM_WD * w_ref[...])
        nm_ref[...] = mn
        nv_ref[...] = vn

    blk = pl.BlockSpec((tr, c), lambda i: (i, 0))
    return pl.pallas_call(
        body, name=name, grid=(r // tr,),
        in_specs=[pl.BlockSpec((n_parts, tr, c), lambda i: (0, i, 0)), blk, blk, blk],
        out_specs=[blk, blk, blk, blk], out_shape=[_sds((r, c))] * 4,
        compiler_params=_params(("parallel",)),
    )(parts, w, m, v)


_PACK_ROWS = 8


def _pack(vals):
    tiles = []
    for a in vals:
        flat = a.reshape(-1).astype(F32)
        unit = _PACK_ROWS * LANE
        n = -(-flat.shape[0] // unit) * unit
        tiles.append(jnp.pad(flat, (0, n - flat.shape[0])).reshape(n // LANE, LANE))
    return jnp.concatenate(tiles, axis=0)


def _unpack(packed, shapes):
    out = []
    r0 = 0
    for shp in shapes:
        size = 1
        for dim in shp:
            size *= dim
        unit = _PACK_ROWS * LANE
        rows = -(-size // unit) * _PACK_ROWS
        out.append(packed[r0:r0 + rows].reshape(-1)[:size].reshape(shp))
        r0 += rows
    return out


def _lane_row(vec8):
    return jnp.pad(vec8.reshape(1, -1).astype(F32), ((0, 0), (0, LANE - vec8.size)))


def kernel(x, mem, ln_g, w_in, gmlp_ln_g, gmlp_ln_b, gmlp_ws, gmlp_bs, conv_w, dn_a_log, dn_dt_bias, dn_norm_g, mem_norm_g, w_mem_kv, w_out, final_g, loss_target, m_ln_g, m_w_in, m_gmlp_ln_g, m_gmlp_ln_b, m_gmlp_ws, m_gmlp_bs, m_conv_w, m_dn_a_log, m_dn_dt_bias, m_dn_norm_g, m_mem_norm_g, m_w_mem_kv, m_w_out, m_final_g, v_ln_g, v_w_in, v_gmlp_ln_g, v_gmlp_ln_b, v_gmlp_ws, v_gmlp_bs, v_conv_w, v_dn_a_log, v_dn_dt_bias, v_dn_norm_g, v_mem_norm_g, v_w_mem_kv, v_w_out, v_final_g):
    xs = x[0]
    mems = mem[0]
    tgt = loss_target[0]
    s, d = xs.shape
    shard_w = w_in.shape[2]
    in_w = N_DEV * shard_w
    me = 4 * lax.axis_index("x") + 2 * lax.axis_index("y") + lax.axis_index("c")

    w_in_b = w_in[0].astype(BF16)
    kh = d // 2
    (g_top,) = _gather_two_level([w_in_b[:kh]], "gather_w_in_top")
    o_g, o_dn, o_ab = 0, 3 * GMLP_W, 3 * GMLP_W + 4 * DN_W
    o_xa = o_ab + 2 * DN_H

    def shard_cols(g, lo, hi):
        out = []
        while lo < hi:
            sh = lo // shard_w
            end = min(hi, (sh + 1) * shard_w)
            out.append(g[sh][:, lo - sh * shard_w:end - sh * shard_w])
            lo = end
        return out

    def own_layout(g):
        main = jnp.concatenate(shard_cols(g, o_dn, o_ab) + shard_cols(g, o_g, o_dn) + shard_cols(g, o_xa, in_w), axis=1)
        return main, jnp.pad(jnp.concatenate(shard_cols(g, o_ab, o_xa), axis=1), ((0, 0), (0, LANE - 2 * DN_H)))

    w_top, wab_top = own_layout(g_top)

    ln_g2 = ln_g.reshape(1, d)
    lng2 = gmlp_ln_g.reshape(1, GMLP_W)
    lnb2 = gmlp_ln_b.reshape(1, GMLP_W)
    ws3 = gmlp_ws[0]
    bs_t = gmlp_bs[0].T
    alog_row = _lane_row(dn_a_log)
    dt_row = _lane_row(dn_dt_bias)
    dn_g2 = dn_norm_g.reshape(1, HEAD)
    mem_g2 = mem_norm_g.reshape(1, d)
    fin_g2 = final_g.reshape(1, d)

    proj_a, ab_a, h_t, h_hi, (g_bot, g_out, g_kv, g_conv) = _inproj_first(
        xs, ln_g2, w_top, wab_top, [w_in_b[kh:], w_out[0].astype(BF16), w_mem_kv[0].astype(BF16), conv_w[0]])
    w_bot, wab_bot = own_layout(g_bot)
    proj, ab = _inproj_second(h_hi, w_bot, wab_bot, proj_a, ab_a)
    wo = g_out.reshape(MIX_W, d)
    wo_perm = jnp.concatenate([wo[GMLP_W:GMLP_W + DN_W], wo[0:GMLP_W], wo[GMLP_W + DN_W:MIX_W]], axis=0)
    w_kv = g_kv.reshape(d, 2 * XA_W)
    conv_full = g_conv.transpose(1, 0, 2).reshape(DN_K, 3 * DN_W)
    out_a = _gmlp_fwd(proj, lng2, lnb2, ws3, bs_t)
    mkv = _memkv_fwd(mems, mem_g2, w_kv)
    out_c = _xattn_fwd(proj, mkv)
    q, k, v, gb, gbt, yc = _dn_pre(proj, ab, conv_full, alog_row, dt_row)
    u, wk, qg, kd, tmat, ai, egl = _dn_local(q, k, v, gb, gbt)
    o, vn, st, out_b = _dn_scan(u, wk, qg, kd, ai, egl, proj, dn_g2)

    dx2, dx2b, dmixed, loss_acc, d_fin_g = _final(xs, tgt, out_b, out_a, out_c, wo_perm, fin_g2)
    loss = lax.psum(loss_acc[0, 0], ("x", "y", "c"))

    dwo_b = _matmul_tn(out_b, dx2b, "dw_out_b")
    dwo_a = _matmul_tn(out_a, dx2b, "dw_out_a")
    dwo_c = _matmul_tn(out_c, dx2b, "dw_out_c")
    d_w_out = jnp.concatenate([dwo_a, dwo_b, dwo_c], axis=0)

    dp_g, d_ws, d_bst, d_lng, d_lnb = _gmlp_bwd(proj, dmixed, lng2, lnb2, ws3, bs_t)
    dp_x, dmkv = _xattn_bwd(proj, dmixed, mkv)
    d_w_kv, d_mem_g = _memkv_bwd(mems, mem_g2, w_kv, dmkv)
    do, dvn, dst, dp_dz, d_dn_g = _dn_scan_bwd(dmixed, o, proj, dn_g2, wk, qg, kd, ai, egl)
    dq, dk, dv, dgb = _dn_local_bwd(q, k, v, gb, gbt, tmat, vn, st, dst, do, dvn)
    dp_qkv, dp_ab, d_conv, d_alog, d_dt = _dn_pre_bwd(proj, yc, ab, conv_full, alog_row, dt_row, dq, dk, dv, dgb)

    dw_qkv = _matmul_acc(h_t, dp_qkv, "dw_in_qkv")
    dw_dz = _matmul_acc(h_t, dp_dz, "dw_in_dz")
    dw_gm = _matmul_acc(h_t, dp_g, "dw_in_gmlp")
    dw_xa = _matmul_acc(h_t, dp_x, "dw_in_xa")
    dw_ab = _matmul_acc(h_t, dp_ab, "dw_in_ab")
    segs = [(o_g, dw_gm), (o_dn, dw_qkv), (o_dn + 3 * DN_W, dw_dz), (o_ab, dw_ab[:, :2 * DN_H]), (o_xa, dw_xa)]
    shards = []
    for sh in range(N_DEV):
        lo, hi = sh * shard_w, (sh + 1) * shard_w
        parts = [arr[:, max(lo, off) - off:min(hi, off + arr.shape[1]) - off] for off, arr in segs
                 if off < hi and off + arr.shape[1] > lo]
        shards.append(jnp.concatenate(parts, axis=1).astype(BF16))
    send_in = jnp.stack(shards)

    small_shapes = [gmlp_ln_g.shape, gmlp_ln_b.shape, gmlp_ws.shape, gmlp_bs.shape, dn_a_log.shape,
                    dn_dt_bias.shape, dn_norm_g.shape, mem_norm_g.shape, final_g.shape, (DN_K, 3 * DN_W)]
    small_g = _pack([d_lng, d_lnb, d_ws, d_bst.T, d_alog[:, :DN_H], d_dt[:, :DN_H], d_dn_g, d_mem_g, d_fin_g, d_conv])
    zc = jnp.zeros((DN_K, 3 * DN_W), F32)
    small_w = _pack([gmlp_ln_g, gmlp_ln_b, gmlp_ws, gmlp_bs, dn_a_log, dn_dt_bias, dn_norm_g, mem_norm_g, final_g, zc])
    small_m = _pack([m_gmlp_ln_g, m_gmlp_ln_b, m_gmlp_ws, m_gmlp_bs, m_dn_a_log, m_dn_dt_bias, m_dn_norm_g,
                     m_mem_norm_g, m_final_g, zc])
    small_v = _pack([v_gmlp_ln_g, v_gmlp_ln_b, v_gmlp_ws, v_gmlp_bs, v_dn_a_log, v_dn_dt_bias, v_dn_norm_g,
                     v_mem_norm_g, v_final_g, zc + 1.0])

    send_out = d_w_out.reshape(N_DEV, MIX_W // N_DEV, d).astype(BF16)
    send_kv = d_w_kv.reshape(N_DEV, d // N_DEV, 2 * XA_W).astype(BF16)
    sends = [send_in, send_out, send_kv]
    all_small, got = _swap_halves(small_g, sends, "swap_halves")
    core = lax.axis_index("c").astype(jnp.int32).reshape(1)
    chip_sums = [_pair_sum(core, sends[i], got[i], "pair_sum_%d" % i) for i in range(3)]
    grad_x, d_ln_g, (r_in, r_out, r_kv) = _dh_rms(
        [dp_qkv, dp_dz, dp_g, dp_x, dp_ab], [w_top, w_bot], [wab_top, wab_bot], xs, dx2, ln_g2, chip_sums)
    (all_ln_g,) = _gather_two_level([_pack([d_ln_g])], "gather_ln_g")

    g_w_in, dl_w_in, nm_w_in, nv_w_in = _adam(r_in, w_in[0], m_w_in[0], v_w_in[0], "adam_w_in")
    g_w_out, dl_w_out, nm_w_out, nv_w_out = _adam(r_out, w_out[0], m_w_out[0], v_w_out[0], "adam_w_out")
    g_w_kv, dl_w_kv, nm_w_kv, nv_w_kv = _adam(r_kv, w_mem_kv[0], m_w_mem_kv[0], v_w_mem_kv[0], "adam_w_kv")
    sm = [_unpack(t, small_shapes) for t in _adam(all_small, small_w, small_m, small_v, "adam_small")]
    ln_res = [_unpack(t, [ln_g.shape])[0]
              for t in _adam(all_ln_g, _pack([ln_g]), _pack([m_ln_g]), _pack([v_ln_g]), "adam_ln_g")]

    conv_parts = lax.dynamic_slice(all_small, (0, all_small.shape[1] - (DN_K * 3 * DN_W) // LANE, 0),
                                   (N_DEV, (DN_K * 3 * DN_W) // LANE, LANE)).reshape(N_DEV, DN_K, 3 * DN_W)
    cshard = conv_w.shape[2]
    conv_parts = lax.dynamic_slice(conv_parts, (0, 0, me * cshard), (N_DEV, DN_K, cshard))
    cpad = ((0, 0), (0, HALO - DN_K), (0, 0))
    conv_res = _adam(jnp.pad(conv_parts, cpad), jnp.pad(conv_w[0], cpad[1:]), jnp.pad(m_conv_w[0], cpad[1:]),
                     jnp.pad(v_conv_w[0], cpad[1:], constant_values=1.0), "adam_conv")
    g_conv_s, dl_conv, nm_conv, nv_conv = [t[:DN_K][None] for t in conv_res]

    def group(idx, big_in, big_conv, big_kv, big_out):
        names = sm[idx]
        return [ln_res[idx], big_in[None], names[0], names[1], names[2], names[3], big_conv, names[4], names[5], names[6],
                names[7], big_kv[None], big_out[None], names[8]]

    grads = group(0, g_w_in, g_conv_s, g_w_kv, g_w_out)
    deltas = group(1, dl_w_in, dl_conv, dl_w_kv, dl_w_out)
    new_m = group(2, nm_w_in, nm_conv, nm_w_kv, nm_w_out)
    new_v = group(3, nv_w_in, nv_conv, nv_w_kv, nv_w_out)
    return (loss, grad_x[None], *grads, *deltas, *new_m, *new_v)
```

```python
import functools

import jax
import jax.numpy as jnp
from jax import lax
from jax.experimental import pallas as pl
from jax.experimental.pallas import tpu as pltpu

F32 = jnp.float32
BF16 = jnp.bfloat16
HIGHEST = lax.Precision.HIGHEST
MESH_ID = pl.DeviceIdType.MESH

N_DEV = 8
EPS = 1e-6
GMLP_W = 512
GMLP_G = 4
GMLP_T = 128
DN_W = 1024
DN_H = 8
HEAD = 128
DN_K = 4
CH = 64
XA_W = 512
XA_H = 4
LANE = 128
HALO = 8
MAIN_W = 4 * DN_W + 3 * GMLP_W + 2 * XA_W
MIX_W = DN_W + GMLP_W + XA_W
VMEM_LIMIT = 56 * 1024 * 1024

ADAM_LR = 0.001
ADAM_B1 = 0.9
ADAM_B2 = 0.999
ADAM_EPS = 1e-08
ADAM_WD = 0.01
ADAM_STEP = 10


def _sds(shape, dtype=F32):
    return jax.ShapeDtypeStruct(tuple(shape), dtype)


def _params(sem=None):
    if sem is None:
        return pltpu.CompilerParams(vmem_limit_bytes=VMEM_LIMIT)
    return pltpu.CompilerParams(dimension_semantics=tuple(sem), vmem_limit_bytes=VMEM_LIMIT)


def _tile(n, prefs):
    for p in prefs:
        if n % p == 0:
            return p
    return n


def _mm(a, b):
    return jnp.dot(a.astype(BF16), b.astype(BF16), preferred_element_type=F32)


def _mm_nt(a, b):
    return lax.dot_general(a.astype(BF16), b.astype(BF16), (((1,), (1,)), ((), ())), preferred_element_type=F32)


def _mm_tn(a, b):
    return lax.dot_general(a.astype(BF16), b.astype(BF16), (((0,), (0,)), ((), ())), preferred_element_type=F32)


def _mm_hi(a, b):
    return jnp.dot(a, b, precision=HIGHEST, preferred_element_type=F32)


def _mm_3x(a, b):
    return jnp.dot(a, b, precision=lax.Precision.HIGH, preferred_element_type=F32)


_GELU_C = 0.7978845608028654
_GELU_A = 0.044715


def _gelu(x):
    return 0.5 * x * (1.0 + jnp.tanh(_GELU_C * (x + _GELU_A * x * x * x)))


def _gelu_grad(x):
    t = jnp.tanh(_GELU_C * (x + _GELU_A * x * x * x))
    return 0.5 * (1.0 + t) + 0.5 * x * (1.0 - t * t) * _GELU_C * (1.0 + 3.0 * _GELU_A * x * x)


def _silu(x):
    return x * jax.nn.sigmoid(x)


def _silu_grad(x):
    s = jax.nn.sigmoid(x)
    return s * (1.0 + x * (1.0 - s))


def _rowsum(x):
    return jnp.sum(x, axis=-1, keepdims=True)


def _colsum(x):
    return jnp.sum(x, axis=0, keepdims=True)


def _iota2(shape, dim):
    return lax.broadcasted_iota(jnp.int32, shape, dim)


def _chunk_tri(tm, upper):
    r = _iota2((tm, tm), 0)
    c = _iota2((tm, tm), 1)
    same = lax.shift_right_logical(r, 6) == lax.shift_right_logical(c, 6)
    tri = (r <= c) if upper else (r >= c)
    return jnp.where(same & tri, 1.0, 0.0).astype(F32)


N_CHIP = 4


def _mesh_place():
    x, y, c = lax.axis_index("x"), lax.axis_index("y"), lax.axis_index("c")
    chips = [(1 - x, y), (x, 1 - y), (1 - x, 1 - y)]
    return x, y, c, (x, y, 1 - c), chips


class _Gather:
    def __init__(self, ins, outs, send_sems, recv_sems, loc_sems):
        self.ins, self.outs, self.send_sems, self.recv_sems, self.loc_sems = ins, outs, send_sems, recv_sems, loc_sems
        self.x, self.y, self.c, self.sib, self.chips = _mesh_place()
        self.me = (self.x, self.y, self.c)

    def copy(self, a, k, block, to, src=None):
        slot = self.outs[a].at[4 * block[0] + 2 * block[1] + block[2]]
        return pltpu.make_async_remote_copy(
            src_ref=slot if src is None else src, dst_ref=slot, send_sem=self.send_sems.at[a, k],
            recv_sem=self.recv_sems.at[a, k], device_id=to, device_id_type=MESH_ID)

    def own(self, a):
        return pltpu.make_async_copy(self.ins[a], self.outs[a].at[4 * self.x + 2 * self.y + self.c], self.loc_sems.at[a])

    def first(self, a):
        return [self.copy(a, 0, self.me, self.sib, src=self.ins[a])] + [
            self.copy(a, 1 + j, self.me, (*chip, self.c), src=self.ins[a]) for j, chip in enumerate(self.chips)]

    def passed(self, a, j):
        return self.copy(a, 4 + j, (*self.chips[j], self.c), self.sib)

    def start(self):
        for a in range(len(self.ins)):
            self.own(a).start()
            for cp in self.first(a):
                cp.start()

    def finish(self):
        n = len(self.ins)
        for a in range(n):
            for j, chip in enumerate(self.chips):
                self.copy(a, 1 + j, (*chip, self.c), self.me).wait_recv()
                self.passed(a, j).start()
        for a in range(n):
            self.copy(a, 0, self.sib, self.me).wait_recv()
            for j, chip in enumerate(self.chips):
                self.copy(a, 4 + j, (*chip, 1 - self.c), self.me).wait_recv()
        for a in range(n):
            for cp in self.first(a) + [self.passed(a, j) for j in range(N_CHIP - 1)]:
                cp.wait_send()
            self.own(a).wait()

    @staticmethod
    def sems(n):
        return [pltpu.SemaphoreType.DMA((n, N_DEV - 1)), pltpu.SemaphoreType.DMA((n, N_DEV - 1)),
                pltpu.SemaphoreType.DMA((n,))]


def _gather_two_level(arrs, name):
    n = len(arrs)

    def body(*refs):
        g = _Gather(refs[:n], refs[n:2 * n], *refs[2 * n:])
        g.start()
        g.finish()

    any_spec = pl.BlockSpec(memory_space=pl.ANY)
    return pl.pallas_call(
        body, name=name, out_shape=[_sds((N_DEV,) + a.shape, a.dtype) for a in arrs],
        in_specs=[any_spec] * n, out_specs=[any_spec] * n, scratch_shapes=_Gather.sems(n),
        compiler_params=pltpu.CompilerParams(has_side_effects=True),
    )(*arrs)


def _swap_halves(small, grads, name):
    n = len(grads)

    def body(*refs):
        small_ref = refs[0]
        ins = refs[1:1 + n]
        small_out = refs[1 + n]
        got = refs[2 + n:2 + 2 * n]
        s_send, s_recv, g_send, g_recv, loc_sem = refs[2 + 2 * n:]
        x, y, c, sib, _ = _mesh_place()
        me = 4 * x + 2 * y + c
        sends, recvs = [], []
        for j in range(1, N_DEV):
            px = 1 - x if (j >> 2) & 1 else x
            py = 1 - y if (j >> 1) & 1 else y
            pc = 1 - c if j & 1 else c
            cp = pltpu.make_async_remote_copy(
                src_ref=small_ref, dst_ref=small_out.at[me], send_sem=s_send.at[j - 1], recv_sem=s_recv.at[j - 1],
                device_id=(px, py, pc), device_id_type=MESH_ID)
            cp.start()
            sends.append(cp)
            recvs.append(pltpu.make_async_remote_copy(
                src_ref=small_ref, dst_ref=small_out.at[4 * px + 2 * py + pc], send_sem=s_send.at[j - 1],
                recv_sem=s_recv.at[j - 1], device_id=(px, py, pc), device_id_type=MESH_ID))
        own = pltpu.make_async_copy(small_ref, small_out.at[me], loc_sem)
        own.start()
        for a in range(n):
            for chip in range(N_CHIP):
                cp = pltpu.make_async_remote_copy(
                    src_ref=ins[a].at[2 * chip + 1 - c], dst_ref=got[a].at[chip], send_sem=g_send.at[a, chip],
                    recv_sem=g_recv.at[a, chip], device_id=sib, device_id_type=MESH_ID)
                cp.start()
                sends.append(cp)
                recvs.append(cp)
        for cp in sends:
            cp.wait_send()
        for cp in recvs:
            cp.wait_recv()
        own.wait()

    half = [_sds((N_CHIP,) + g.shape[1:], g.dtype) for g in grads]
    any_spec = pl.BlockSpec(memory_space=pl.ANY)
    res = pl.pallas_call(
        body, name=name, out_shape=[_sds((N_DEV,) + small.shape, small.dtype)] + half,
        in_specs=[any_spec] * (1 + n), out_specs=[any_spec] * (1 + n),
        scratch_shapes=[pltpu.SemaphoreType.DMA((N_DEV - 1,)), pltpu.SemaphoreType.DMA((N_DEV - 1,)),
                        pltpu.SemaphoreType.DMA((n, N_CHIP)), pltpu.SemaphoreType.DMA((n, N_CHIP)),
                        pltpu.SemaphoreType.DMA],
        compiler_params=pltpu.CompilerParams(has_side_effects=True),
    )(small, *grads)
    return res[0], res[1:]


def _pair_sum(core, mine, got, name):
    nc, r, c = got.shape
    tr = _tile(r, (256, 128, 64, 32, 16))

    def body(core_ref, a_ref, b_ref, o_ref):
        o_ref[...] = (a_ref[...].astype(F32) + b_ref[...].astype(F32)).astype(BF16)

    return pl.pallas_call(
        body, name=name, out_shape=_sds(got.shape, BF16),
        grid_spec=pltpu.PrefetchScalarGridSpec(
            num_scalar_prefetch=1, grid=(nc, r // tr),
            in_specs=[pl.BlockSpec((1, tr, c), lambda i, j, core_ref: (2 * i + core_ref[0], j, 0)),
                      pl.BlockSpec((1, tr, c), lambda i, j, core_ref: (i, j, 0))],
            out_specs=pl.BlockSpec((1, tr, c), lambda i, j, core_ref: (i, j, 0))),
        compiler_params=_params(("parallel", "parallel")),
    )(core, mine, got)


class _ChipExchange:
    def __init__(self, ins, outs, send_sems, recv_sems, loc_sems):
        self.ins, self.outs, self.send_sems, self.recv_sems, self.loc_sems = ins, outs, send_sems, recv_sems, loc_sems
        self.x, self.y, self.c, _, self.chips = _mesh_place()
        self.mine = 2 * self.x + self.y

    def own(self, a):
        return pltpu.make_async_copy(self.ins[a].at[self.mine], self.outs[a].at[self.mine], self.loc_sems.at[a])

    def copy(self, a, j, lands_in):
        chip = self.chips[j]
        return pltpu.make_async_remote_copy(
            src_ref=self.ins[a].at[2 * chip[0] + chip[1]], dst_ref=self.outs[a].at[lands_in],
            send_sem=self.send_sems.at[a, j], recv_sem=self.recv_sems.at[a, j], device_id=(*chip, self.c),
            device_id_type=MESH_ID)

    def start(self):
        for a in range(len(self.ins)):
            self.own(a).start()
            for j in range(N_CHIP - 1):
                self.copy(a, j, self.mine).start()

    def finish(self):
        for a in range(len(self.ins)):
            for j, chip in enumerate(self.chips):
                self.copy(a, j, self.mine).wait_send()
                self.copy(a, j, 2 * chip[0] + chip[1]).wait_recv()
            self.own(a).wait()

    @staticmethod
    def sems(n):
        return [pltpu.SemaphoreType.DMA((n, N_CHIP - 1)), pltpu.SemaphoreType.DMA((n, N_CHIP - 1)),
                pltpu.SemaphoreType.DMA((n,))]


def _inproj(x, ln_g, w_main, w_ab, late):
    s, d = x.shape
    n = w_main.shape[1]
    tm = _tile(s, (512, 256, 128))
    tn = _tile(n, (1664, 512, 128))
    nl = len(late)
    ni, nj = s // tm, n // tn

    def body(*refs):
        x_ref, g_ref, w_ref, wab_ref = refs[:4]
        proj_ref, ab_ref, ht_ref = refs[4 + nl:7 + nl]
        hs = refs[7 + 2 * nl]
        gather = _Gather(refs[4:4 + nl], refs[7 + nl:7 + 2 * nl], *refs[8 + 2 * nl:])
        step = pl.program_id(0) * nj + pl.program_id(1)

        @pl.when(step == 0)
        def _():
            gather.start()

        @pl.when(pl.program_id(1) == 0)
        def _():
            xv = x_ref[...]
            r = lax.rsqrt(jnp.mean(xv * xv, axis=-1, keepdims=True) + EPS)
            hf = xv * r * g_ref[...]
            h = hf.astype(BF16)
            hs[...] = h
            ht_ref[...] = hf.T.astype(BF16)
            ab_ref[...] = jnp.dot(h, wab_ref[...], preferred_element_type=F32)

        proj_ref[...] = jnp.dot(hs[...], w_ref[...], preferred_element_type=F32)

        @pl.when(step == ni * nj - 1)
        def _():
            gather.finish()

    any_spec = pl.BlockSpec(memory_space=pl.ANY)
    res = pl.pallas_call(
        body, name="inproj", grid=(ni, nj),
        in_specs=[pl.BlockSpec((tm, d), lambda i, j: (i, 0)), pl.BlockSpec((1, d), lambda i, j: (0, 0)),
                  pl.BlockSpec((d, tn), lambda i, j: (0, j)), pl.BlockSpec((d, LANE), lambda i, j: (0, 0))]
        + [any_spec] * nl,
        out_specs=[pl.BlockSpec((tm, tn), lambda i, j: (i, j)), pl.BlockSpec((tm, LANE), lambda i, j: (i, 0)),
                   pl.BlockSpec((d, tm), lambda i, j: (0, i))] + [any_spec] * nl,
        out_shape=[_sds((s, n)), _sds((s, LANE)), _sds((d, s), BF16)]
        + [_sds((N_DEV,) + a.shape, a.dtype) for a in late],
        scratch_shapes=[pltpu.VMEM((tm, d), BF16)] + _Gather.sems(nl),
        compiler_params=_params(("arbitrary", "arbitrary")),
    )(x, ln_g, w_main, w_ab, *late)
    return res[0], res[1], res[2], res[3:]


def _matmul_acc(a, b, name):
    m, k = a.shape
    n = b.shape[1]
    tm = _tile(m, (2048, 1024, 512, 256, 128))
    tn = _tile(n, (1024, 512, 256, 128))
    tk = _tile(k, (1024, 512, 256, 128))
    nk = k // tk

    def body(a_ref, b_ref, o_ref, acc):
        @pl.when(pl.program_id(2) == 0)
        def _():
            acc[...] = jnp.zeros_like(acc)

        acc[...] += jnp.dot(a_ref[...], b_ref[...], preferred_element_type=F32)

        @pl.when(pl.program_id(2) == nk - 1)
        def _():
            o_ref[...] = acc[...].astype(BF16)

    return pl.pallas_call(
        body, name=name, grid=(m // tm, n // tn, nk),
        in_specs=[pl.BlockSpec((tm, tk), lambda i, j, l: (i, l)), pl.BlockSpec((tk, tn), lambda i, j, l: (l, j))],
        out_specs=pl.BlockSpec((tm, tn), lambda i, j, l: (i, j)),
        out_shape=_sds((m, n), BF16), scratch_shapes=[pltpu.VMEM((tm, tn), F32)],
        compiler_params=_params(("parallel", "parallel", "arbitrary")),
    )(a, b)


def _matmul_tn(a, b, name):
    k, m = a.shape
    n = b.shape[1]
    tm = _tile(m, (1024, 512, 256, 128))
    tn = _tile(n, (1024, 512, 256, 128))
    tk = _tile(k, (1024, 512, 256, 128))
    nk = k // tk

    def body(a_ref, b_ref, o_ref, acc):
        @pl.when(pl.program_id(2) == 0)
        def _():
            acc[...] = jnp.zeros_like(acc)

        acc[...] += _mm_tn(a_ref[...], b_ref[...])

        @pl.when(pl.program_id(2) == nk - 1)
        def _():
            o_ref[...] = acc[...].astype(BF16)

    return pl.pallas_call(
        body, name=name, grid=(m // tm, n // tn, nk),
        in_specs=[pl.BlockSpec((tk, tm), lambda i, j, l: (l, i)), pl.BlockSpec((tk, tn), lambda i, j, l: (l, j))],
        out_specs=pl.BlockSpec((tm, tn), lambda i, j, l: (i, j)),
        out_shape=_sds((m, n), BF16), scratch_shapes=[pltpu.VMEM((tm, tn), F32)],
        compiler_params=_params(("parallel", "parallel", "arbitrary")),
    )(a, b)


def _dh_rms(pieces, w_rows, wab_rows, x, dx2, ln_g, chip_sums):
    s, d = x.shape
    npc = len(pieces)
    nx = len(chip_sums)
    tm = _tile(s, (256, 128))
    ni = s // tm
    widths = [p.shape[1] for p in pieces[:-1]]
    offs = [sum(widths[:p]) for p in range(npc - 1)]
    nw = len(w_rows)
    nin = npc + 2 * nw + 3

    def body(*refs):
        p_refs = refs[:npc]
        w_refs = refs[npc:npc + nw]
        wab_refs = refs[npc + nw:npc + 2 * nw]
        x_ref, dx2_ref, g_ref = refs[npc + 2 * nw:nin]
        gx_ref, dg_ref = refs[nin + nx:nin + nx + 2]
        exch = _ChipExchange(refs[nin:nin + nx], refs[nin + nx + 2:nin + 2 * nx + 2], *refs[nin + 2 * nx + 2:])
        step = pl.program_id(0)

        @pl.when(step == 0)
        def _():
            dg_ref[...] = jnp.zeros_like(dg_ref)
            exch.start()

        cols = []
        for w_ref, wab_ref in zip(w_refs, wab_refs):
            part = _mm_nt(p_refs[npc - 1][...], wab_ref[...])
            for p in range(npc - 1):
                part += _mm_nt(p_refs[p][...], w_ref[:, offs[p]:offs[p] + widths[p]])
            cols.append(part)
        dhv = jnp.concatenate(cols, axis=1)
        xv = x_ref[...]
        r = lax.rsqrt(jnp.mean(xv * xv, axis=-1, keepdims=True) + EPS)
        xhat = xv * r
        dg_ref[...] += _colsum(dhv * xhat)
        dxh = dhv * g_ref[...]
        gx_ref[...] = dx2_ref[...] + r * (dxh - xhat * jnp.mean(dxh * xhat, axis=-1, keepdims=True))

        @pl.when(step == ni - 1)
        def _():
            exch.finish()

    any_spec = pl.BlockSpec(memory_space=pl.ANY)
    row = pl.BlockSpec((tm, d), lambda i: (i, 0))
    vec = pl.BlockSpec((1, d), lambda i: (0, 0))
    once = lambda a: pl.BlockSpec(a.shape, lambda i: (0, 0), pipeline_mode=pl.Buffered(1))
    in_specs = [pl.BlockSpec((tm, p.shape[1]), lambda i: (i, 0)) for p in pieces]
    in_specs += [once(w) for w in w_rows] + [once(w) for w in wab_rows] + [row, row, vec] + [any_spec] * nx
    res = pl.pallas_call(
        body, name="dh_rms", grid=(ni,), in_specs=in_specs,
        out_specs=[row, vec] + [any_spec] * nx,
        out_shape=[_sds((s, d)), _sds((1, d))] + [_sds(p.shape, p.dtype) for p in chip_sums],
        scratch_shapes=_ChipExchange.sems(nx),
        compiler_params=_params(("arbitrary",)),
    )(*pieces, *w_rows, *wab_rows, x, dx2, ln_g, *chip_sums)
    return res[0], res[1], res[2:]


def _final(x, tgt, out_b, out_a, out_c, w_out, final_g):
    s, d = x.shape
    tm = _tile(s, (256, 128))

    def body(x_ref, t_ref, b_ref, a_ref, c_ref, w_ref, g_ref, dx2_ref, dx2b_ref, dm_ref, loss_ref, dg_ref):
        @pl.when(pl.program_id(0) == 0)
        def _():
            loss_ref[...] = jnp.zeros_like(loss_ref)
            dg_ref[...] = jnp.zeros_like(dg_ref)

        x2 = x_ref[...]
        x2 += jnp.dot(b_ref[...], w_ref[0:DN_W, :], preferred_element_type=F32)
        x2 += jnp.dot(a_ref[...], w_ref[DN_W:DN_W + GMLP_W, :], preferred_element_type=F32)
        x2 += jnp.dot(c_ref[...], w_ref[DN_W + GMLP_W:MIX_W, :], preferred_element_type=F32)
        r = lax.rsqrt(jnp.mean(x2 * x2, axis=-1, keepdims=True) + EPS)
        xhat = x2 * r
        g = g_ref[...]
        err = xhat * g - t_ref[...]
        tok = 0.5 * jnp.mean(err * err, axis=-1, keepdims=True)
        loss_ref[...] += jnp.broadcast_to(_colsum(tok), loss_ref.shape)
        dy = err * (1.0 / d)
        dg_ref[...] += _colsum(dy * xhat)
        dxh = dy * g
        dx2 = r * (dxh - xhat * jnp.mean(dxh * xhat, axis=-1, keepdims=True))
        dx2_ref[...] = dx2
        dx2b = dx2.astype(BF16)
        dx2b_ref[...] = dx2b
        dm_ref[...] = _mm_nt(dx2b, w_ref[...])

    row = pl.BlockSpec((tm, d), lambda i: (i, 0))
    vec = pl.BlockSpec((1, d), lambda i: (0, 0))
    return pl.pallas_call(
        body, name="final", grid=(s // tm,),
        in_specs=[row, row, pl.BlockSpec((tm, DN_W), lambda i: (i, 0)), pl.BlockSpec((tm, GMLP_W), lambda i: (i, 0)),
                  pl.BlockSpec((tm, XA_W), lambda i: (i, 0)), pl.BlockSpec((MIX_W, d), lambda i: (0, 0)), vec],
        out_specs=[row, row, pl.BlockSpec((tm, MIX_W), lambda i: (i, 0)), pl.BlockSpec((1, LANE), lambda i: (0, 0)), vec],
        out_shape=[_sds((s, d)), _sds((s, d), BF16), _sds((s, MIX_W)), _sds((1, LANE)), _sds((1, d))],
        compiler_params=_params(("arbitrary",)),
    )(x, tgt, out_b, out_a, out_c, w_out, final_g)


GU_BLK = (4 * DN_W) // GMLP_W


def _gmlp_norm(gv, lng, lnb):
    va = _gelu(gv)
    mu = jnp.mean(va, axis=-1, keepdims=True)
    xc = va - mu
    rstd = lax.rsqrt(jnp.mean(xc * xc, axis=-1, keepdims=True) + EPS)
    vhat = xc * rstd
    return vhat, rstd, vhat * lng + lnb


def _gmlp_fwd(proj, lng, lnb, ws, bs_t):
    s = proj.shape[0]
    tm = _tile(s, (512, 256, 128))

    def body(u_ref, v_ref, z_ref, lng_ref, lnb_ref, ws_ref, bst_ref, o_ref):
        _, _, vn = _gmlp_norm(v_ref[...], lng_ref[...], lnb_ref[...])
        tri = _iota2((GMLP_T, GMLP_T), 0) >= _iota2((GMLP_T, GMLP_T), 1)
        for g in range(GMLP_G):
            cs = slice(g * HEAD, (g + 1) * HEAD)
            w = jnp.where(tri, ws_ref[g], 0.0).astype(BF16)
            b = bst_ref[:, g:g + 1]
            for c in range(tm // GMLP_T):
                rs = slice(c * GMLP_T, (c + 1) * GMLP_T)
                sg = _mm(w, vn[rs, cs]) + b
                o_ref[rs, cs] = (_gelu(u_ref[rs, cs]) * sg * _silu(z_ref[rs, cs])).astype(BF16)

    col = lambda k: pl.BlockSpec((tm, GMLP_W), lambda i: (i, GU_BLK + k))
    vec = pl.BlockSpec((1, GMLP_W), lambda i: (0, 0))
    return pl.pallas_call(
        body, name="gmlp_fwd", grid=(s // tm,),
        in_specs=[col(0), col(1), col(2), vec, vec, pl.BlockSpec((GMLP_G, GMLP_T, GMLP_T), lambda i: (0, 0, 0)),
                  pl.BlockSpec((GMLP_T, GMLP_G), lambda i: (0, 0))],
        out_specs=pl.BlockSpec((tm, GMLP_W), lambda i: (i, 0)), out_shape=_sds((s, GMLP_W), BF16),
        compiler_params=_params(("parallel",)),
    )(proj, proj, proj, lng, lnb, ws, bs_t)


def _gmlp_bwd(proj, dmixed, lng, lnb, ws, bs_t):
    s = proj.shape[0]
    tm = _tile(s, (512, 256, 128))

    def body(u_ref, v_ref, z_ref, d_ref, lng_ref, lnb_ref, ws_ref, bst_ref,
             dp_ref, dws_ref, dbst_ref, dlng_ref, dlnb_ref, dvn):
        @pl.when(pl.program_id(0) == 0)
        def _():
            dws_ref[...] = jnp.zeros_like(dws_ref)
            dbst_ref[...] = jnp.zeros_like(dbst_ref)
            dlng_ref[...] = jnp.zeros_like(dlng_ref)
            dlnb_ref[...] = jnp.zeros_like(dlnb_ref)

        gv = v_ref[...]
        lng_v = lng_ref[...]
        vhat, rstd, vn = _gmlp_norm(gv, lng_v, lnb_ref[...])
        tri = _iota2((GMLP_T, GMLP_T), 0) >= _iota2((GMLP_T, GMLP_T), 1)
        for g in range(GMLP_G):
            cs = slice(g * HEAD, (g + 1) * HEAD)
            w = jnp.where(tri, ws_ref[g], 0.0).astype(BF16)
            b = bst_ref[:, g:g + 1]
            dw_acc = jnp.zeros((GMLP_T, GMLP_T), F32)
            db_acc = jnp.zeros((GMLP_T, 1), F32)
            for c in range(tm // GMLP_T):
                rs = slice(c * GMLP_T, (c + 1) * GMLP_T)
                vn_b = vn[rs, cs]
                sg = _mm(w, vn_b) + b
                gu = u_ref[rs, cs]
                gz = z_ref[rs, cs]
                da = d_ref[rs, cs]
                uact = _gelu(gu)
                sz = _silu(gz)
                ds = da * uact * sz
                dp_ref[rs, cs] = (da * sg * sz * _gelu_grad(gu)).astype(BF16)
                dp_ref[rs, 2 * GMLP_W + g * HEAD:2 * GMLP_W + (g + 1) * HEAD] = (da * uact * sg * _silu_grad(gz)).astype(BF16)
                dw_acc += _mm_nt(ds, vn_b)
                db_acc += _rowsum(ds)
                dvn[rs, cs] = _mm_tn(w, ds)
            dws_ref[g] += jnp.where(tri, dw_acc, 0.0)
            dbst_ref[:, g:g + 1] += db_acc
        dvn_v = dvn[...]
        dlng_ref[...] += _colsum(dvn_v * vhat)
        dlnb_ref[...] += _colsum(dvn_v)
        dvh = dvn_v * lng_v
        dva = rstd * (dvh - jnp.mean(dvh, axis=-1, keepdims=True) - vhat * jnp.mean(dvh * vhat, axis=-1, keepdims=True))
        dp_ref[:, GMLP_W:2 * GMLP_W] = (dva * _gelu_grad(gv)).astype(BF16)

    col = lambda k: pl.BlockSpec((tm, GMLP_W), lambda i: (i, GU_BLK + k))
    vec = pl.BlockSpec((1, GMLP_W), lambda i: (0, 0))
    wsp = pl.BlockSpec((GMLP_G, GMLP_T, GMLP_T), lambda i: (0, 0, 0))
    bsp = pl.BlockSpec((GMLP_T, GMLP_G), lambda i: (0, 0))
    return pl.pallas_call(
        body, name="gmlp_bwd", grid=(s // tm,),
        in_specs=[col(0), col(1), col(2), pl.BlockSpec((tm, GMLP_W), lambda i: (i, DN_W // GMLP_W)), vec, vec, wsp, bsp],
        out_specs=[pl.BlockSpec((tm, 3 * GMLP_W), lambda i: (i, 0)), wsp, bsp, vec, vec],
        out_shape=[_sds((s, 3 * GMLP_W), BF16), _sds((GMLP_G, GMLP_T, GMLP_T)), _sds((GMLP_T, GMLP_G)),
                   _sds((1, GMLP_W)), _sds((1, GMLP_W))],
        scratch_shapes=[pltpu.VMEM((tm, GMLP_W), F32)],
        compiler_params=_params(("arbitrary",)),
    )(proj, proj, proj, dmixed, lng, lnb, ws, bs_t)


CQ_BLK = (4 * DN_W + 3 * GMLP_W) // XA_W


def _memkv_fwd(mem, g, w_kv):
    nm, d = mem.shape

    def body(m_ref, g_ref, w_ref, kv_ref):
        mv = m_ref[...]
        r = lax.rsqrt(jnp.mean(mv * mv, axis=-1, keepdims=True) + EPS)
        kv_ref[...] = _mm(mv * r * g_ref[...], w_ref[...])

    return pl.pallas_call(body, name="memkv_fwd", out_shape=_sds((nm, 2 * XA_W)), compiler_params=_params())(mem, g, w_kv)


def _memkv_bwd(mem, g, w_kv, dkv):
    nm, d = mem.shape

    def body(m_ref, g_ref, w_ref, dkv_ref, dw_ref, dg_ref):
        mv = m_ref[...]
        r = lax.rsqrt(jnp.mean(mv * mv, axis=-1, keepdims=True) + EPS)
        xhat = mv * r
        dkv_v = dkv_ref[...]
        dw_ref[...] = _mm_tn(xhat * g_ref[...], dkv_v)
        dg_ref[...] = _colsum(_mm_nt(dkv_v, w_ref[...]) * xhat)

    return pl.pallas_call(body, name="memkv_bwd", out_shape=[_sds((d, 2 * XA_W)), _sds((1, d))],
                          compiler_params=_params())(mem, g, w_kv, dkv)


def _xattn_probs(q, mk):
    sc = _mm_nt(q, mk) * (HEAD ** -0.5)
    e = jnp.exp(sc - jnp.max(sc, axis=-1, keepdims=True))
    return e / _rowsum(e)


def _xattn_fwd(proj, mkv):
    s = proj.shape[0]
    nm = mkv.shape[0]
    tm = _tile(s, (512, 256, 128))

    def body(q_ref, z_ref, kv_ref, o_ref):
        for h in range(XA_H):
            cs = slice(h * HEAD, (h + 1) * HEAD)
            p = _xattn_probs(q_ref[:, cs], kv_ref[:, cs])
            ctx = _mm(p, kv_ref[:, XA_W + h * HEAD:XA_W + (h + 1) * HEAD])
            o_ref[:, cs] = (ctx * _silu(z_ref[:, cs])).astype(BF16)

    col = lambda k: pl.BlockSpec((tm, XA_W), lambda i: (i, CQ_BLK + k))
    return pl.pallas_call(
        body, name="xattn_fwd", grid=(s // tm,),
        in_specs=[col(0), col(1), pl.BlockSpec((nm, 2 * XA_W), lambda i: (0, 0))],
        out_specs=pl.BlockSpec((tm, XA_W), lambda i: (i, 0)), out_shape=_sds((s, XA_W), BF16),
        compiler_params=_params(("parallel",)),
    )(proj, proj, mkv)


def _xattn_bwd(proj, dmixed, mkv):
    s = proj.shape[0]
    nm = mkv.shape[0]
    tm = _tile(s, (512, 256, 128))

    def body(q_ref, z_ref, d_ref, kv_ref, dp_ref, dkv_ref):
        @pl.when(pl.program_id(0) == 0)
        def _():
            dkv_ref[...] = jnp.zeros_like(dkv_ref)

        for h in range(XA_H):
            cs = slice(h * HEAD, (h + 1) * HEAD)
            vs = slice(XA_W + h * HEAD, XA_W + (h + 1) * HEAD)
            q = q_ref[:, cs]
            z = z_ref[:, cs]
            mk = kv_ref[:, cs]
            mv = kv_ref[:, vs]
            p = _xattn_probs(q, mk)
            ctx = _mm(p, mv)
            dc = d_ref[:, cs]
            dctx = dc * _silu(z)
            dp_ref[:, vs] = (dc * ctx * _silu_grad(z)).astype(BF16)
            dp = _mm_nt(dctx, mv)
            dkv_ref[:, vs] += _mm_tn(p, dctx)
            ds = p * (dp - _rowsum(dp * p)) * (HEAD ** -0.5)
            dp_ref[:, cs] = _mm(ds, mk).astype(BF16)
            dkv_ref[:, cs] += _mm_tn(ds, q)

    col = lambda k: pl.BlockSpec((tm, XA_W), lambda i: (i, CQ_BLK + k))
    kvs = pl.BlockSpec((nm, 2 * XA_W), lambda i: (0, 0))
    return pl.pallas_call(
        body, name="xattn_bwd", grid=(s // tm,),
        in_specs=[col(0), col(1), pl.BlockSpec((tm, XA_W), lambda i: (i, (DN_W + GMLP_W) // XA_W)), kvs],
        out_specs=[pl.BlockSpec((tm, 2 * XA_W), lambda i: (i, 0)), kvs],
        out_shape=[_sds((s, 2 * XA_W), BF16), _sds((nm, 2 * XA_W))],
        compiler_params=_params(("arbitrary",)),
    )(proj, proj, dmixed, mkv)


def _softplus(x):
    return jnp.maximum(x, 0.0) + jnp.log1p(jnp.exp(-jnp.abs(x)))


def _dn_pre(proj, ab, conv_w, alog_row, dt_row):
    s = proj.shape[0]
    tm = _tile(s, (256, 128))
    w3 = 3 * DN_W

    def body(x_ref, halo_ref, ab_ref, cw_ref, al_ref, dt_ref, q_ref, k_ref, v_ref, gb_ref, gbt_ref, yc_ref):
        i = pl.program_id(0)
        xv = x_ref[...]
        cat = jnp.concatenate([jnp.where(i > 0, halo_ref[...], 0.0), xv[0:HALO]], axis=0)
        yc = cw_ref[DN_K - 1:DN_K, :] * xv
        top = cw_ref[DN_K - 1:DN_K, :] * xv[0:HALO]
        for t in range(DN_K - 1):
            back = DN_K - 1 - t
            yc += cw_ref[t:t + 1, :] * pltpu.roll(xv, back, 0)
            top += cw_ref[t:t + 1, :] * pltpu.roll(cat, back, 0)[HALO:2 * HALO]
        yc = jnp.concatenate([top, yc[HALO:tm]], axis=0)
        yc_ref[...] = yc
        act = _silu(yc)
        for h in range(DN_H):
            cs = slice(h * HEAD, (h + 1) * HEAD)
            qa = act[:, cs]
            q_ref[:, cs] = qa * (lax.rsqrt(_rowsum(qa * qa) + EPS) * (HEAD ** -0.5))
            ka = act[:, DN_W + h * HEAD:DN_W + (h + 1) * HEAD]
            k_ref[:, cs] = ka * lax.rsqrt(_rowsum(ka * ka) + EPS)
        v_ref[...] = act[:, 2 * DN_W:w3]
        abv = ab_ref[...]
        lane = _iota2((tm, LANE), 1)
        g = jnp.where(lane < DN_H, -jnp.exp(al_ref[...]) * _softplus(abv + dt_ref[...]), 0.0)
        gc = _mm_hi(_chunk_tri(tm, False), g)
        gbv = jnp.where(lane < DN_H, gc, jnp.where(lane < 2 * DN_H, jax.nn.sigmoid(abv), 0.0))
        gb_ref[...] = gbv
        for c in range(tm // CH):
            gbt_ref[c] = gbv[c * CH:(c + 1) * CH, :].T[0:2 * DN_H, :]

    hb = tm // HALO
    row = lambda w: pl.BlockSpec((tm, w), lambda i: (i, 0))
    vec = pl.BlockSpec((1, LANE), lambda i: (0, 0))
    return pl.pallas_call(
        body, name="dn_pre", grid=(s // tm,),
        in_specs=[row(w3), pl.BlockSpec((HALO, w3), lambda i: (jnp.maximum(i * hb - 1, 0), 0)), row(LANE),
                  pl.BlockSpec((DN_K, w3), lambda i: (0, 0)), vec, vec],
        out_specs=[row(DN_W), row(DN_W), row(DN_W), row(LANE), pl.BlockSpec((tm // CH, 2 * DN_H, CH), lambda i: (i, 0, 0)),
                   row(w3)],
        out_shape=[_sds((s, DN_W)), _sds((s, DN_W)), _sds((s, DN_W)), _sds((s, LANE)), _sds((s // CH, 2 * DN_H, CH)),
                   _sds((s, w3))],
        compiler_params=_params(("parallel",)),
    )(proj, proj, ab, conv_w, alog_row, dt_row)


HEADS = tuple(range(DN_H))


def _hcols(h):
    return slice(h * HEAD, (h + 1) * HEAD)


def _chunk_scalings(k, v, gbv, gbt, h):
    gc = jnp.broadcast_to(gbv[:, h:h + 1], (CH, HEAD))
    beta = jnp.broadcast_to(gbv[:, DN_H + h:DN_H + h + 1], (CH, HEAD))
    gr = gbt[h:h + 1, :]
    ii = _iota2((CH, CH), 0)
    jj = _iota2((CH, CH), 1)
    dec = jnp.exp(jnp.where(ii >= jj, gc[:, 0:CH] - gr, -1e30))
    eg = jnp.exp(gc)
    gl = gr[:, CH - 1:CH]
    kb = k * beta
    return dict(beta=beta, dec=dec, eg=eg, gl=gl, ekd=jnp.exp(gl - gc), kb=kb, vb=v * beta, kbe=kb * eg)


def _chunk_scores(m, q, k):
    kq = _mm_nt(jnp.concatenate([m["kb"], q], axis=0), k)
    strict = _iota2((CH, CH), 0) > _iota2((CH, CH), 1)
    return jnp.where(strict, kq[0:CH] * m["dec"], 0.0), kq[CH:2 * CH] * m["dec"]


def _scan_cpb(s):
    return 8 if (s // CH) % 8 == 0 else 1


def _dn_fwd(q, k, v, gb, gbt, proj, norm_g):
    s = q.shape[0]
    cpb = _scan_cpb(s)
    tb = cpb * CH
    nblk = s // tb

    def body(q_ref, k_ref, v_ref, gb_ref, gbt_ref, z_ref, ng_ref,
             w_ref, qg_ref, kd_ref, t_ref, ai_ref, egl_ref, o_ref, vn_ref, st_ref, ob_ref, state):
        @pl.when(pl.program_id(0) == 0)
        def _():
            state[...] = jnp.zeros_like(state)

        ng = ng_ref[...]
        eye = jnp.where(_iota2((CH, CH), 0) == _iota2((CH, CH), 1), 1.0, 0.0).astype(F32)

        def chunk(c, carry):
            r0 = pl.multiple_of(c * CH, CH)
            rows = pl.ds(r0, CH)
            gbv = gb_ref[rows, :]
            gbt_v = gbt_ref[c]
            qs = [q_ref[rows, _hcols(h)] for h in HEADS]
            ks = [k_ref[rows, _hcols(h)] for h in HEADS]
            ms = [_chunk_scalings(ks[h], v_ref[rows, _hcols(h)], gbv, gbt_v, h) for h in HEADS]
            qgb = [(qs[h] * ms[h]["eg"]).astype(BF16) for h in HEADS]
            kdb = [(ks[h] * ms[h]["ekd"]).astype(BF16) for h in HEADS]
            egl = [jnp.broadcast_to(jnp.exp(ms[h]["gl"]), (1, LANE)) for h in HEADS]
            for h in HEADS:
                qg_ref[rows, _hcols(h)] = qgb[h]
                kd_ref[rows, _hcols(h)] = kdb[h]
                egl_ref[c, h:h + 1, :] = egl[h]
            sc = [_chunk_scores(ms[h], qs[h], ks[h]) for h in HEADS]
            for h in HEADS:
                ai_ref[h, rows, :] = sc[h][1]
            ts = [eye - sc[h][0] for h in HEADS]
            ps = [_mm_3x(sc[h][0], sc[h][0]) for h in HEADS]
            ts = [ts[h] + _mm_3x(ts[h], ps[h]) for h in HEADS]
            for _ in range(4):
                ps = [_mm(ps[h], ps[h]) for h in HEADS]
                ts = [ts[h] + _mm(ts[h], ps[h]) for h in HEADS]
            uw = [_mm(ts[h], jnp.concatenate([ms[h]["vb"], ms[h]["kbe"]], axis=1)) for h in HEADS]
            wb = [uw[h][:, HEAD:2 * HEAD].astype(BF16) for h in HEADS]
            for h in HEADS:
                t_ref[h, rows, :] = ts[h]
                w_ref[rows, _hcols(h)] = wb[h]
            sts = [state[h] for h in HEADS]
            stb = [sts[h].astype(BF16) for h in HEADS]
            for h in HEADS:
                st_ref[c, h] = stb[h]
            vnb = [(uw[h][:, 0:HEAD] - jnp.dot(wb[h], stb[h], preferred_element_type=F32)).astype(BF16) for h in HEADS]
            for h in HEADS:
                state[h] = sts[h] * egl[h] + _mm_tn(kdb[h], vnb[h])
            os_ = [jnp.dot(qgb[h], stb[h], preferred_element_type=F32) + _mm(sc[h][1], vnb[h]) for h in HEADS]
            for h in HEADS:
                o = os_[h]
                vn_ref[rows, _hcols(h)] = vnb[h]
                o_ref[rows, _hcols(h)] = o
                r = lax.rsqrt(jnp.mean(o * o, axis=-1, keepdims=True) + EPS)
                ob_ref[rows, _hcols(h)] = (o * r * ng * _silu(z_ref[rows, _hcols(h)])).astype(BF16)
            return carry

        lax.fori_loop(0, cpb, chunk, 0, unroll=4)

    row = pl.BlockSpec((tb, DN_W), lambda i: (i, 0))
    sq = pl.BlockSpec((DN_H, tb, CH), lambda i: (0, i, 0))
    return pl.pallas_call(
        body, name="dn_fwd", grid=(nblk,),
        in_specs=[row, row, row, pl.BlockSpec((tb, LANE), lambda i: (i, 0)),
                  pl.BlockSpec((cpb, 2 * DN_H, CH), lambda i: (i, 0, 0)), pl.BlockSpec((tb, DN_W), lambda i: (i, 3)),
                  pl.BlockSpec((1, HEAD), lambda i: (0, 0))],
        out_specs=[row, row, row, sq, sq, pl.BlockSpec((cpb, DN_H, LANE), lambda i: (i, 0, 0)), row, row,
                   pl.BlockSpec((cpb, DN_H, HEAD, HEAD), lambda i: (i, 0, 0, 0)), row],
        out_shape=[_sds((s, DN_W), BF16), _sds((s, DN_W), BF16), _sds((s, DN_W), BF16), _sds((DN_H, s, CH)),
                   _sds((DN_H, s, CH)), _sds((s // CH, DN_H, LANE)), _sds((s, DN_W)), _sds((s, DN_W), BF16),
                   _sds((s // CH, DN_H, HEAD, HEAD), BF16), _sds((s, DN_W), BF16)],
        scratch_shapes=[pltpu.VMEM((DN_H, HEAD, HEAD), F32)],
        compiler_params=_params(("arbitrary",)),
    )(q, k, v, gb, gbt, proj, norm_g)


def _dn_scan_bwd(dmixed, o, proj, norm_g, w, qg, kd, ai, egl):
    s = o.shape[0]
    cpb = _scan_cpb(s)
    tb = cpb * CH
    nblk = s // tb

    def body(dm_ref, o_ref, z_ref, ng_ref, w_ref, qg_ref, kd_ref, ai_ref, egl_ref,
             do_ref, dvn_ref, dst_ref, dz_ref, dng_ref, dstate):
        @pl.when(pl.program_id(0) == 0)
        def _():
            dstate[...] = jnp.zeros_like(dstate)
            dng_ref[...] = jnp.zeros_like(dng_ref)

        ng = ng_ref[...]

        def chunk(cc, carry):
            c = cpb - 1 - cc
            r0 = pl.multiple_of(c * CH, CH)
            rows = pl.ds(r0, CH)
            dng = jnp.zeros((1, HEAD), F32)
            dob = []
            for h in HEADS:
                cs = _hcols(h)
                o = o_ref[rows, cs]
                z = z_ref[rows, cs]
                db = dm_ref[rows, cs]
                r = lax.rsqrt(jnp.mean(o * o, axis=-1, keepdims=True) + EPS)
                ohat = o * r
                dz_ref[rows, cs] = (db * ohat * ng * _silu_grad(z)).astype(BF16)
                dyn = db * _silu(z)
                dng += _colsum(dyn * ohat)
                doh = dyn * ng
                do = r * (doh - ohat * jnp.mean(doh * ohat, axis=-1, keepdims=True))
                dob.append(do.astype(BF16))
                do_ref[rows, cs] = dob[h]
            dng_ref[...] += dng
            dsn = [dstate[h] for h in HEADS]
            dsb = [dsn[h].astype(BF16) for h in HEADS]
            for h in HEADS:
                dst_ref[c, h] = dsb[h]
            dvn = [(_mm_tn(ai_ref[h, rows, :], dob[h])
                    + jnp.dot(kd_ref[rows, _hcols(h)], dsb[h], preferred_element_type=F32)).astype(BF16) for h in HEADS]
            part = [_mm_tn(qg_ref[rows, _hcols(h)], dob[h]) + egl_ref[c, h:h + 1, :] * dsn[h] for h in HEADS]
            for h in HEADS:
                dvn_ref[rows, _hcols(h)] = dvn[h]
                dstate[h] = part[h] - _mm_tn(w_ref[rows, _hcols(h)], dvn[h])
            return carry

        lax.fori_loop(0, cpb, chunk, 0, unroll=4)

    rev = lambda i: (nblk - 1 - i, 0)
    row = pl.BlockSpec((tb, DN_W), rev)
    vec = pl.BlockSpec((1, HEAD), lambda i: (0, 0))
    return pl.pallas_call(
        body, name="dn_scan_bwd", grid=(nblk,),
        in_specs=[row, row, pl.BlockSpec((tb, DN_W), lambda i: (nblk - 1 - i, 3)), vec, row, row, row,
                  pl.BlockSpec((DN_H, tb, CH), lambda i: (0, nblk - 1 - i, 0)),
                  pl.BlockSpec((cpb, DN_H, LANE), lambda i: (nblk - 1 - i, 0, 0))],
        out_specs=[row, row, pl.BlockSpec((cpb, DN_H, HEAD, HEAD), lambda i: (nblk - 1 - i, 0, 0, 0)), row, vec],
        out_shape=[_sds((s, DN_W), BF16), _sds((s, DN_W), BF16), _sds((s // CH, DN_H, HEAD, HEAD), BF16),
                   _sds((s, DN_W), BF16), _sds((1, HEAD))],
        scratch_shapes=[pltpu.VMEM((DN_H, HEAD, HEAD), F32)],
        compiler_params=_params(("arbitrary",)),
    )(dmixed, o, proj, norm_g, w, qg, kd, ai, egl)


def _dn_local_bwd(q, k, v, gb, gbt, t, vn, st, dst, do, dvn):
    s = q.shape[0]
    cpb = 4 if (s // CH) % 4 == 0 else 1
    tb = cpb * CH
    nblk = s // tb

    def body(q_ref, k_ref, v_ref, gb_ref, gbt_ref, t_ref, vn_ref, st_ref, dst_ref, do_ref, dvn_ref,
             dq_ref, dk_ref, dv_ref, dgb_ref):
        lane = _iota2((CH, LANE), 1)
        last = _iota2((CH, 1), 0) == CH - 1

        def chunk(c, carry):
            r0 = pl.multiple_of(c * CH, CH)
            rows = pl.ds(r0, CH)
            gbv = gb_ref[rows, :]
            gbt_v = gbt_ref[c]
            strict = _iota2((CH, CH), 0) > _iota2((CH, CH), 1)
            qs = [q_ref[rows, _hcols(h)] for h in HEADS]
            ks = [k_ref[rows, _hcols(h)] for h in HEADS]
            vs = [v_ref[rows, _hcols(h)] for h in HEADS]
            ms = [_chunk_scalings(ks[h], vs[h], gbv, gbt_v, h) for h in HEADS]
            sts = [st_ref[c, h] for h in HEADS]
            dsn = [dst_ref[c, h] for h in HEADS]
            dob = [do_ref[rows, _hcols(h)].astype(BF16) for h in HEADS]
            dvnb = [dvn_ref[rows, _hcols(h)].astype(BF16) for h in HEADS]
            vnb = [vn_ref[rows, _hcols(h)].astype(BF16) for h in HEADS]
            tbf = [t_ref[h, rows, :].astype(BF16) for h in HEADS]
            sc = [_chunk_scores(ms[h], qs[h], ks[h]) for h in HEADS]
            xs_ = [_mm_nt(jnp.concatenate([dob[h], dvnb[h]], axis=0), sts[h]) for h in HEADS]
            dai = [_mm_nt(dob[h], vnb[h]) for h in HEADS]
            dkd = [_mm_nt(vnb[h], dsn[h]) for h in HEADS]
            dqg = [xs_[h][0:CH] for h in HEADS]
            duw = [jnp.concatenate([dvnb[h], (-xs_[h][CH:2 * CH]).astype(BF16)], axis=1) for h in HEADS]
            dt = [_mm_nt(duw[h], jnp.concatenate([ms[h]["vb"], ms[h]["kbe"]], axis=1)) for h in HEADS]
            dvk = [_mm_tn(tbf[h], duw[h]) for h in HEADS]
            tdt = [_mm_tn(tbf[h], dt[h]) for h in HEADS]
            da = [jnp.where(strict, -_mm_nt(tdt[h], tbf[h]), 0.0) for h in HEADS]
            dsc = [jnp.concatenate([da[h] * ms[h]["dec"], dai[h] * ms[h]["dec"]], axis=0) for h in HEADS]
            dkq = [_mm(dsc[h], ks[h]) for h in HEADS]
            dk1 = [_mm_tn(dsc[h], jnp.concatenate([ms[h]["kb"], qs[h]], axis=0)) for h in HEADS]
            dgb = jnp.zeros((CH, LANE), F32)
            for h in HEADS:
                m = ms[h]
                eg, ekd, beta = m["eg"], m["ekd"], m["beta"]
                dvb = dvk[h][:, 0:HEAD]
                dkbe = dvk[h][:, HEAD:2 * HEAD]
                kd = ks[h] * ekd
                dkb = dkq[h][0:CH] + dkbe * eg
                dq_ref[rows, _hcols(h)] = dkq[h][CH:2 * CH] + dqg[h] * eg
                dk_ref[rows, _hcols(h)] = dk1[h] + dkd[h] * ekd + dkb * beta
                dv_ref[rows, _hcols(h)] = dvb * beta
                dkd_kd = dkd[h] * kd
                dgl = (jnp.exp(m["gl"]) * _rowsum(_colsum(sts[h].astype(F32) * dsn[h].astype(F32)))
                       + _rowsum(_colsum(dkd_kd)))
                mm_ = da[h] * sc[h][0] + dai[h] * sc[h][1]
                dgc = (_rowsum(mm_ - mm_.T) + _rowsum(dqg[h] * qs[h] * eg - dkd_kd + dkbe * m["kbe"])
                       + jnp.where(last, dgl, 0.0))
                dbeta = _rowsum(dkb * ks[h] + dvb * vs[h])
                dgb = jnp.where(lane == h, dgc, jnp.where(lane == DN_H + h, dbeta, dgb))
            dgb_ref[rows, :] = dgb
            return carry

        lax.fori_loop(0, cpb, chunk, 0, unroll=2)

    row = pl.BlockSpec((tb, DN_W), lambda i: (i, 0))
    gbs = pl.BlockSpec((tb, LANE), lambda i: (i, 0))
    sts = pl.BlockSpec((cpb, DN_H, HEAD, HEAD), lambda i: (i, 0, 0, 0))
    return pl.pallas_call(
        body, name="dn_local_bwd", grid=(nblk,),
        in_specs=[row, row, row, gbs, pl.BlockSpec((cpb, 2 * DN_H, CH), lambda i: (i, 0, 0)),
                  pl.BlockSpec((DN_H, tb, CH), lambda i: (0, i, 0)), row, sts, sts, row, row],
        out_specs=[row, row, row, gbs],
        out_shape=[_sds((s, DN_W)), _sds((s, DN_W)), _sds((s, DN_W)), _sds((s, LANE))],
        compiler_params=_params(("parallel",)),
    )(q, k, v, gb, gbt, t, vn, st, dst, do, dvn)


def _dn_pre_bwd(proj, yc_all, ab, conv_w, alog_row, dt_row, dq, dk, dv, dgb):
    s = proj.shape[0]
    tm = _tile(s, (256, 128))
    w3 = 3 * DN_W
    nblk = s // tm

    def body(x_ref, yc_ref, ab_ref, cw_ref, al_ref, dt_ref, dq_ref, dk_ref, dv_ref, dgb_ref,
             dx_ref, dab_ref, dcw_ref, dal_ref, ddt_ref, exd, carry):
        i = pl.program_id(0)

        @pl.when(i == 0)
        def _():
            carry[...] = jnp.zeros_like(carry)
            dcw_ref[...] = jnp.zeros_like(dcw_ref)
            dal_ref[...] = jnp.zeros_like(dal_ref)
            ddt_ref[...] = jnp.zeros_like(ddt_ref)

        yc = yc_ref[...]
        sg = jax.nn.sigmoid(yc)
        act = yc * sg
        dact = sg * (1.0 + yc * (1.0 - sg))
        for h in range(DN_H):
            cs = slice(h * HEAD, (h + 1) * HEAD)
            ks = slice(DN_W + h * HEAD, DN_W + (h + 1) * HEAD)
            qa = act[:, cs]
            rq = lax.rsqrt(_rowsum(qa * qa) + EPS)
            qh = qa * rq
            dqv = dq_ref[:, cs]
            exd[0:tm, cs] = (HEAD ** -0.5) * rq * (dqv - qh * _rowsum(dqv * qh)) * dact[:, cs]
            ka = act[:, ks]
            rk = lax.rsqrt(_rowsum(ka * ka) + EPS)
            kh = ka * rk
            dkv = dk_ref[:, cs]
            exd[0:tm, ks] = rk * (dkv - kh * _rowsum(dkv * kh)) * dact[:, ks]
        exd[0:tm, 2 * DN_W:w3] = dv_ref[...] * dact[:, 2 * DN_W:w3]
        xv = x_ref[...]
        dyc = exd[...]
        cat = jnp.concatenate([dyc[tm - HALO:tm], carry[...]], axis=0)
        dcw_ref[DN_K - 1:DN_K, :] += _colsum(dyc * xv)
        dx = cw_ref[DN_K - 1:DN_K, :] * dyc
        for t in range(DN_K - 1):
            ahead = DN_K - 1 - t
            view = jnp.concatenate([pltpu.roll(dyc, tm - ahead, 0)[0:tm - HALO],
                                    pltpu.roll(cat, 2 * HALO - ahead, 0)[0:HALO]], axis=0)
            dcw_ref[t:t + 1, :] += _colsum(view * xv)
            dx += cw_ref[t:t + 1, :] * view
        dx_ref[...] = dx.astype(BF16)
        carry[...] = dyc[0:HALO]

        lane = _iota2((tm, LANE), 1)
        dgbv = dgb_ref[...]
        dg = _mm_hi(_chunk_tri(tm, True), jnp.where(lane < DN_H, dgbv, 0.0))
        abv = ab_ref[...]
        xa = abv + dt_ref[...]
        nea = -jnp.exp(al_ref[...])
        d_da = jnp.where(lane < DN_H, dg * nea * jax.nn.sigmoid(xa), 0.0)
        dal_ref[...] += _colsum(jnp.where(lane < DN_H, dg * nea * _softplus(xa), 0.0))
        ddt_ref[...] += _colsum(d_da)
        beta = jax.nn.sigmoid(abv)
        d_db = jnp.where((lane >= DN_H) & (lane < 2 * DN_H), dgbv * beta * (1.0 - beta), 0.0)
        dab_ref[...] = (d_da + d_db).astype(BF16)

    rev = lambda i: (nblk - 1 - i, 0)
    row = lambda w: pl.BlockSpec((tm, w), rev)
    vec = pl.BlockSpec((1, LANE), lambda i: (0, 0))
    cws = pl.BlockSpec((DN_K, w3), lambda i: (0, 0))
    return pl.pallas_call(
        body, name="dn_pre_bwd", grid=(nblk,),
        in_specs=[row(w3), row(w3), row(LANE), cws, vec, vec, row(DN_W), row(DN_W), row(DN_W), row(LANE)],
        out_specs=[row(w3), row(LANE), cws, vec, vec],
        out_shape=[_sds((s, w3), BF16), _sds((s, LANE), BF16), _sds((DN_K, w3)), _sds((1, LANE)), _sds((1, LANE))],
        scratch_shapes=[pltpu.VMEM((tm, w3), F32), pltpu.VMEM((HALO, w3), F32)],
        compiler_params=_params(("arbitrary",)),
    )(proj, yc_all, ab, conv_w, alog_row, dt_row, dq, dk, dv, dgb)


def _adam(parts, w, m, v, name):
    r, c = w.shape
    n_parts = parts.shape[0]
    small = n_parts * r * c * 4 <= 4 * 1024 * 1024
    tr = r if small else _tile(r, (128, 64, 32, 16, 8))

    def body(p_ref, w_ref, m_ref, v_ref, g_ref, d_ref, nm_ref, nv_ref):
        g = p_ref[0].astype(F32)
        for k in range(1, n_parts):
            g = g + p_ref[k].astype(F32)
        g_ref[...] = g
        mn = ADAM_B1 * m_ref[...] + (1.0 - ADAM_B1) * g
        vn = ADAM_B2 * v_ref[...] + (1.0 - ADAM_B2) * (g * g)
        m_hat = mn / (1.0 - ADAM_B1 ** ADAM_STEP)
        v_hat = vn / (1.0 - ADAM_B2 ** ADAM_STEP)
        d_ref[...] = -ADAM_LR * (m_hat / (jnp.sqrt(v_hat) + ADAM_EPS) + ADAM_WD * w_ref[...])
        nm_ref[...] = mn
        nv_ref[...] = vn

    blk = pl.BlockSpec((tr, c), lambda i: (i, 0))
    return pl.pallas_call(
        body, name=name, grid=(r // tr,),
        in_specs=[pl.BlockSpec((n_parts, tr, c), lambda i: (0, i, 0)), blk, blk, blk],
        out_specs=[blk, blk, blk, blk], out_shape=[_sds((r, c))] * 4,
        compiler_params=_params(("parallel",)),
    )(parts, w, m, v)


_PACK_ROWS = 8


def _pack(vals):
    tiles = []
    for a in vals:
        flat = a.reshape(-1).astype(F32)
        unit = _PACK_ROWS * LANE
        n = -(-flat.shape[0] // unit) * unit
        tiles.append(jnp.pad(flat, (0, n - flat.shape[0])).reshape(n // LANE, LANE))
    return jnp.concatenate(tiles, axis=0)


def _unpack(packed, shapes):
    out = []
    r0 = 0
    for shp in shapes:
        size = 1
        for dim in shp:
            size *= dim
        unit = _PACK_ROWS * LANE
        rows = -(-size // unit) * _PACK_ROWS
        out.append(packed[r0:r0 + rows].reshape(-1)[:size].reshape(shp))
        r0 += rows
    return out


def _lane_row(vec8):
    return jnp.pad(vec8.reshape(1, -1).astype(F32), ((0, 0), (0, LANE - vec8.size)))


def kernel(x, mem, ln_g, w_in, gmlp_ln_g, gmlp_ln_b, gmlp_ws, gmlp_bs, conv_w, dn_a_log, dn_dt_bias, dn_norm_g, mem_norm_g, w_mem_kv, w_out, final_g, loss_target, m_ln_g, m_w_in, m_gmlp_ln_g, m_gmlp_ln_b, m_gmlp_ws, m_gmlp_bs, m_conv_w, m_dn_a_log, m_dn_dt_bias, m_dn_norm_g, m_mem_norm_g, m_w_mem_kv, m_w_out, m_final_g, v_ln_g, v_w_in, v_gmlp_ln_g, v_gmlp_ln_b, v_gmlp_ws, v_gmlp_bs, v_conv_w, v_dn_a_log, v_dn_dt_bias, v_dn_norm_g, v_mem_norm_g, v_w_mem_kv, v_w_out, v_final_g):
    xs = x[0]
    mems = mem[0]
    tgt = loss_target[0]
    s, d = xs.shape
    shard_w = w_in.shape[2]
    in_w = N_DEV * shard_w
    me = 4 * lax.axis_index("x") + 2 * lax.axis_index("y") + lax.axis_index("c")

    (g_in,) = _gather_two_level([w_in[0].astype(BF16)], "gather_w_in")
    o_g, o_dn, o_ab = 0, 3 * GMLP_W, 3 * GMLP_W + 4 * DN_W
    o_xa = o_ab + 2 * DN_H

    def shard_cols(g, lo, hi):
        out = []
        while lo < hi:
            sh = lo // shard_w
            end = min(hi, (sh + 1) * shard_w)
            out.append(g[sh][:, lo - sh * shard_w:end - sh * shard_w])
            lo = end
        return out

    def own_layout(g):
        main = jnp.concatenate(shard_cols(g, o_dn, o_ab) + shard_cols(g, o_g, o_dn) + shard_cols(g, o_xa, in_w), axis=1)
        return main, jnp.pad(jnp.concatenate(shard_cols(g, o_ab, o_xa), axis=1), ((0, 0), (0, LANE - 2 * DN_H)))

    w_main, w_ab = own_layout(g_in)

    ln_g2 = ln_g.reshape(1, d)
    lng2 = gmlp_ln_g.reshape(1, GMLP_W)
    lnb2 = gmlp_ln_b.reshape(1, GMLP_W)
    ws3 = gmlp_ws[0]
    bs_t = gmlp_bs[0].T
    alog_row = _lane_row(dn_a_log)
    dt_row = _lane_row(dn_dt_bias)
    dn_g2 = dn_norm_g.reshape(1, HEAD)
    mem_g2 = mem_norm_g.reshape(1, d)
    fin_g2 = final_g.reshape(1, d)

    proj, ab, h_t, (g_out, g_kv, g_conv) = _inproj(
        xs, ln_g2, w_main, w_ab, [w_out[0].astype(BF16), w_mem_kv[0].astype(BF16), conv_w[0]])
    wo = g_out.reshape(MIX_W, d)
    wo_perm = jnp.concatenate([wo[GMLP_W:GMLP_W + DN_W], wo[0:GMLP_W], wo[GMLP_W + DN_W:MIX_W]], axis=0)
    w_kv = g_kv.reshape(d, 2 * XA_W)
    conv_full = g_conv.transpose(1, 0, 2).reshape(DN_K, 3 * DN_W)
    out_a = _gmlp_fwd(proj, lng2, lnb2, ws3, bs_t)
    mkv = _memkv_fwd(mems, mem_g2, w_kv)
    out_c = _xattn_fwd(proj, mkv)
    q, k, v, gb, gbt, yc = _dn_pre(proj, ab, conv_full, alog_row, dt_row)
    wk, qg, kd, tmat, ai, egl, o, vn, st, out_b = _dn_fwd(q, k, v, gb, gbt, proj, dn_g2)

    dx2, dx2b, dmixed, loss_acc, d_fin_g = _final(xs, tgt, out_b, out_a, out_c, wo_perm, fin_g2)
    loss = lax.psum(loss_acc[0, 0], ("x", "y", "c"))

    dwo_b = _matmul_tn(out_b, dx2b, "dw_out_b")
    dwo_a = _matmul_tn(out_a, dx2b, "dw_out_a")
    dwo_c = _matmul_tn(out_c, dx2b, "dw_out_c")
    d_w_out = jnp.concatenate([dwo_a, dwo_b, dwo_c], axis=0)

    dp_g, d_ws, d_bst, d_lng, d_lnb = _gmlp_bwd(proj, dmixed, lng2, lnb2, ws3, bs_t)
    dp_x, dmkv = _xattn_bwd(proj, dmixed, mkv)
    d_w_kv, d_mem_g = _memkv_bwd(mems, mem_g2, w_kv, dmkv)
    do, dvn, dst, dp_dz, d_dn_g = _dn_scan_bwd(dmixed, o, proj, dn_g2, wk, qg, kd, ai, egl)
    dq, dk, dv, dgb = _dn_local_bwd(q, k, v, gb, gbt, tmat, vn, st, dst, do, dvn)
    dp_qkv, dp_ab, d_conv, d_alog, d_dt = _dn_pre_bwd(proj, yc, ab, conv_full, alog_row, dt_row, dq, dk, dv, dgb)

    dw_qkv = _matmul_acc(h_t, dp_qkv, "dw_in_qkv")
    dw_dz = _matmul_acc(h_t, dp_dz, "dw_in_dz")
    dw_gm = _matmul_acc(h_t, dp_g, "dw_in_gmlp")
    dw_xa = _matmul_acc(h_t, dp_x, "dw_in_xa")
    dw_ab = _matmul_acc(h_t, dp_ab, "dw_in_ab")
    segs = [(o_g, dw_gm), (o_dn, dw_qkv), (o_dn + 3 * DN_W, dw_dz), (o_ab, dw_ab[:, :2 * DN_H]), (o_xa, dw_xa)]
    shards = []
    for sh in range(N_DEV):
        lo, hi = sh * shard_w, (sh + 1) * shard_w
        parts = [arr[:, max(lo, off) - off:min(hi, off + arr.shape[1]) - off] for off, arr in segs
                 if off < hi and off + arr.shape[1] > lo]
        shards.append(jnp.concatenate(parts, axis=1).astype(BF16))
    send_in = jnp.stack(shards)

    small_shapes = [gmlp_ln_g.shape, gmlp_ln_b.shape, gmlp_ws.shape, gmlp_bs.shape, dn_a_log.shape,
                    dn_dt_bias.shape, dn_norm_g.shape, mem_norm_g.shape, final_g.shape, (DN_K, 3 * DN_W)]
    small_g = _pack([d_lng, d_lnb, d_ws, d_bst.T, d_alog[:, :DN_H], d_dt[:, :DN_H], d_dn_g, d_mem_g, d_fin_g, d_conv])
    zc = jnp.zeros((DN_K, 3 * DN_W), F32)
    small_w = _pack([gmlp_ln_g, gmlp_ln_b, gmlp_ws, gmlp_bs, dn_a_log, dn_dt_bias, dn_norm_g, mem_norm_g, final_g, zc])
    small_m = _pack([m_gmlp_ln_g, m_gmlp_ln_b, m_gmlp_ws, m_gmlp_bs, m_dn_a_log, m_dn_dt_bias, m_dn_norm_g,
                     m_mem_norm_g, m_final_g, zc])
    small_v = _pack([v_gmlp_ln_g, v_gmlp_ln_b, v_gmlp_ws, v_gmlp_bs, v_dn_a_log, v_dn_dt_bias, v_dn_norm_g,
                     v_mem_norm_g, v_final_g, zc + 1.0])

    send_out = d_w_out.reshape(N_DEV, MIX_W // N_DEV, d).astype(BF16)
    send_kv = d_w_kv.reshape(N_DEV, d // N_DEV, 2 * XA_W).astype(BF16)
    sends = [send_in, send_out, send_kv]
    all_small, got = _swap_halves(small_g, sends, "swap_halves")
    core = lax.axis_index("c").astype(jnp.int32).reshape(1)
    chip_sums = [_pair_sum(core, sends[i], got[i], "pair_sum_%d" % i) for i in range(3)]
    grad_x, d_ln_g, (r_in, r_out, r_kv) = _dh_rms(
        [dp_qkv, dp_dz, dp_g, dp_x, dp_ab], [w_main], [w_ab], xs, dx2, ln_g2, chip_sums)
    (all_ln_g,) = _gather_two_level([_pack([d_ln_g])], "gather_ln_g")

    g_w_in, dl_w_in, nm_w_in, nv_w_in = _adam(r_in, w_in[0], m_w_in[0], v_w_in[0], "adam_w_in")
    g_w_out, dl_w_out, nm_w_out, nv_w_out = _adam(r_out, w_out[0], m_w_out[0], v_w_out[0], "adam_w_out")
    g_w_kv, dl_w_kv, nm_w_kv, nv_w_kv = _adam(r_kv, w_mem_kv[0], m_w_mem_kv[0], v_w_mem_kv[0], "adam_w_kv")
    sm = [_unpack(t, small_shapes) for t in _adam(all_small, small_w, small_m, small_v, "adam_small")]
    ln_res = [_unpack(t, [ln_g.shape])[0]
              for t in _adam(all_ln_g, _pack([ln_g]), _pack([m_ln_g]), _pack([v_ln_g]), "adam_ln_g")]

    conv_parts = lax.dynamic_slice(all_small, (0, all_small.shape[1] - (DN_K * 3 * DN_W) // LANE, 0),
                                   (N_DEV, (DN_K * 3 * DN_W) // LANE, LANE)).reshape(N_DEV, DN_K, 3 * DN_W)
    cshard = conv_w.shape[2]
    conv_parts = lax.dynamic_slice(conv_parts, (0, 0, me * cshard), (N_DEV, DN_K, cshard))
    cpad = ((0, 0), (0, HALO - DN_K), (0, 0))
    conv_res = _adam(jnp.pad(conv_parts, cpad), jnp.pad(conv_w[0], cpad[1:]), jnp.pad(m_conv_w[0], cpad[1:]),
                     jnp.pad(v_conv_w[0], cpad[1:], constant_values=1.0), "adam_conv")
    g_conv_s, dl_conv, nm_conv, nv_conv = [t[:DN_K][None] for t in conv_res]

    def group(idx, big_in, big_conv, big_kv, big_out):
        names = sm[idx]
        return [ln_res[idx], big_in[None], names[0], names[1], names[2], names[3], big_conv, names[4], names[5], names[6],
                names[7], big_kv[None], big_out[None], names[8]]

    grads = group(0, g_w_in, g_conv_s, g_w_kv, g_w_out)
    deltas = group(1, dl_w_in, dl_conv, dl_w_kv, dl_w_out)
    new_m = group(2, nm_w_in, nm_conv, nm_w_kv, nm_w_out)
    new_v = group(3, nv_w_in, nv_conv, nv_w_kv, nv_w_out)
    return (loss, grad_x[None], *grads, *deltas, *new_m, *new_v)
```

```python
import functools

import jax
import jax.numpy as jnp
from jax import lax
from jax.experimental import pallas as pl
from jax.experimental.pallas import tpu as pltpu

F32 = jnp.float32
BF16 = jnp.bfloat16
HIGHEST = lax.Precision.HIGHEST
MESH_ID = pl.DeviceIdType.MESH

N_DEV = 8
EPS = 1e-6
GMLP_W = 512
GMLP_G = 4
GMLP_T = 128
DN_W = 1024
DN_H = 8
HEAD = 128
DN_K = 4
CH = 64
XA_W = 512
XA_H = 4
LANE = 128
HALO = 8
MAIN_W = 4 * DN_W + 3 * GMLP_W + 2 * XA_W
MIX_W = DN_W + GMLP_W + XA_W
VMEM_LIMIT = 56 * 1024 * 1024

ADAM_LR = 0.001
ADAM_B1 = 0.9
ADAM_B2 = 0.999
ADAM_EPS = 1e-08
ADAM_WD = 0.01
ADAM_STEP = 10


def _sds(shape, dtype=F32):
    return jax.ShapeDtypeStruct(tuple(shape), dtype)


def _params(sem=None):
    if sem is None:
        return pltpu.CompilerParams(vmem_limit_bytes=VMEM_LIMIT)
    return pltpu.CompilerParams(dimension_semantics=tuple(sem), vmem_limit_bytes=VMEM_LIMIT)


def _tile(n, prefs):
    for p in prefs:
        if n % p == 0:
            return p
    return n


def _mm(a, b):
    return jnp.dot(a.astype(BF16), b.astype(BF16), preferred_element_type=F32)


def _mm_nt(a, b):
    return lax.dot_general(a.astype(BF16), b.astype(BF16), (((1,), (1,)), ((), ())), preferred_element_type=F32)


def _mm_tn(a, b):
    return lax.dot_general(a.astype(BF16), b.astype(BF16), (((0,), (0,)), ((), ())), preferred_element_type=F32)


def _mm_hi(a, b):
    return jnp.dot(a, b, precision=HIGHEST, preferred_element_type=F32)


def _mm_3x(a, b):
    return jnp.dot(a, b, precision=lax.Precision.HIGH, preferred_element_type=F32)


_GELU_C = 0.7978845608028654
_GELU_A = 0.044715


def _gelu(x):
    return 0.5 * x * (1.0 + jnp.tanh(_GELU_C * (x + _GELU_A * x * x * x)))


def _gelu_grad(x):
    t = jnp.tanh(_GELU_C * (x + _GELU_A * x * x * x))
    return 0.5 * (1.0 + t) + 0.5 * x * (1.0 - t * t) * _GELU_C * (1.0 + 3.0 * _GELU_A * x * x)


def _silu(x):
    return x * jax.nn.sigmoid(x)


def _silu_grad(x):
    s = jax.nn.sigmoid(x)
    return s * (1.0 + x * (1.0 - s))


def _rowsum(x):
    return jnp.sum(x, axis=-1, keepdims=True)


def _colsum(x):
    return jnp.sum(x, axis=0, keepdims=True)


def _iota2(shape, dim):
    return lax.broadcasted_iota(jnp.int32, shape, dim)


def _chunk_tri(tm, upper):
    r = _iota2((tm, tm), 0)
    c = _iota2((tm, tm), 1)
    same = lax.shift_right_logical(r, 6) == lax.shift_right_logical(c, 6)
    tri = (r <= c) if upper else (r >= c)
    return jnp.where(same & tri, 1.0, 0.0).astype(F32)


N_CHIP = 4


def _mesh_place():
    x, y, c = lax.axis_index("x"), lax.axis_index("y"), lax.axis_index("c")
    chips = [(1 - x, y), (x, 1 - y), (1 - x, 1 - y)]
    return x, y, c, (x, y, 1 - c), chips


class _Gather:
    def __init__(self, ins, outs, send_sems, recv_sems, loc_sems):
        self.ins, self.outs, self.send_sems, self.recv_sems, self.loc_sems = ins, outs, send_sems, recv_sems, loc_sems
        self.x, self.y, self.c, self.sib, self.chips = _mesh_place()
        self.me = (self.x, self.y, self.c)

    def copy(self, a, k, block, to, src=None):
        slot = self.outs[a].at[4 * block[0] + 2 * block[1] + block[2]]
        return pltpu.make_async_remote_copy(
            src_ref=slot if src is None else src, dst_ref=slot, send_sem=self.send_sems.at[a, k],
            recv_sem=self.recv_sems.at[a, k], device_id=to, device_id_type=MESH_ID)

    def own(self, a):
        return pltpu.make_async_copy(self.ins[a], self.outs[a].at[4 * self.x + 2 * self.y + self.c], self.loc_sems.at[a])

    def first(self, a):
        return [self.copy(a, 0, self.me, self.sib, src=self.ins[a])] + [
            self.copy(a, 1 + j, self.me, (*chip, self.c), src=self.ins[a]) for j, chip in enumerate(self.chips)]

    def passed(self, a, j):
        return self.copy(a, 4 + j, (*self.chips[j], self.c), self.sib)

    def start(self):
        for a in range(len(self.ins)):
            self.own(a).start()
            for cp in self.first(a):
                cp.start()

    def finish(self):
        n = len(self.ins)
        for a in range(n):
            for j, chip in enumerate(self.chips):
                self.copy(a, 1 + j, (*chip, self.c), self.me).wait_recv()
                self.passed(a, j).start()
        for a in range(n):
            self.copy(a, 0, self.sib, self.me).wait_recv()
            for j, chip in enumerate(self.chips):
                self.copy(a, 4 + j, (*chip, 1 - self.c), self.me).wait_recv()
        for a in range(n):
            for cp in self.first(a) + [self.passed(a, j) for j in range(N_CHIP - 1)]:
                cp.wait_send()
            self.own(a).wait()

    @staticmethod
    def sems(n):
        return [pltpu.SemaphoreType.DMA((n, N_DEV - 1)), pltpu.SemaphoreType.DMA((n, N_DEV - 1)),
                pltpu.SemaphoreType.DMA((n,))]


def _gather_two_level(arrs, name):
    n = len(arrs)

    def body(*refs):
        g = _Gather(refs[:n], refs[n:2 * n], *refs[2 * n:])
        g.start()
        g.finish()

    any_spec = pl.BlockSpec(memory_space=pl.ANY)
    return pl.pallas_call(
        body, name=name, out_shape=[_sds((N_DEV,) + a.shape, a.dtype) for a in arrs],
        in_specs=[any_spec] * n, out_specs=[any_spec] * n, scratch_shapes=_Gather.sems(n),
        compiler_params=pltpu.CompilerParams(has_side_effects=True),
    )(*arrs)


def _swap_halves(small, grads, name):
    n = len(grads)

    def body(*refs):
        small_ref = refs[0]
        ins = refs[1:1 + n]
        small_out = refs[1 + n]
        got = refs[2 + n:2 + 2 * n]
        s_send, s_recv, g_send, g_recv, loc_sem = refs[2 + 2 * n:]
        x, y, c, sib, _ = _mesh_place()
        me = 4 * x + 2 * y + c
        sends, recvs = [], []
        for j in range(1, N_DEV):
            px = 1 - x if (j >> 2) & 1 else x
            py = 1 - y if (j >> 1) & 1 else y
            pc = 1 - c if j & 1 else c
            cp = pltpu.make_async_remote_copy(
                src_ref=small_ref, dst_ref=small_out.at[me], send_sem=s_send.at[j - 1], recv_sem=s_recv.at[j - 1],
                device_id=(px, py, pc), device_id_type=MESH_ID)
            cp.start()
            sends.append(cp)
            recvs.append(pltpu.make_async_remote_copy(
                src_ref=small_ref, dst_ref=small_out.at[4 * px + 2 * py + pc], send_sem=s_send.at[j - 1],
                recv_sem=s_recv.at[j - 1], device_id=(px, py, pc), device_id_type=MESH_ID))
        own = pltpu.make_async_copy(small_ref, small_out.at[me], loc_sem)
        own.start()
        for a in range(n):
            for chip in range(N_CHIP):
                cp = pltpu.make_async_remote_copy(
                    src_ref=ins[a].at[2 * chip + 1 - c], dst_ref=got[a].at[chip], send_sem=g_send.at[a, chip],
                    recv_sem=g_recv.at[a, chip], device_id=sib, device_id_type=MESH_ID)
                cp.start()
                sends.append(cp)
                recvs.append(cp)
        for cp in sends:
            cp.wait_send()
        for cp in recvs:
            cp.wait_recv()
        own.wait()

    half = [_sds((N_CHIP,) + g.shape[1:], g.dtype) for g in grads]
    any_spec = pl.BlockSpec(memory_space=pl.ANY)
    res = pl.pallas_call(
        body, name=name, out_shape=[_sds((N_DEV,) + small.shape, small.dtype)] + half,
        in_specs=[any_spec] * (1 + n), out_specs=[any_spec] * (1 + n),
        scratch_shapes=[pltpu.SemaphoreType.DMA((N_DEV - 1,)), pltpu.SemaphoreType.DMA((N_DEV - 1,)),
                        pltpu.SemaphoreType.DMA((n, N_CHIP)), pltpu.SemaphoreType.DMA((n, N_CHIP)),
                        pltpu.SemaphoreType.DMA],
        compiler_params=pltpu.CompilerParams(has_side_effects=True),
    )(small, *grads)
    return res[0], res[1:]


def _pair_sum(core, mine, got, name):
    nc, r, c = got.shape
    tr = _tile(r, (256, 128, 64, 32, 16))

    def body(core_ref, a_ref, b_ref, o_ref):
        o_ref[...] = (a_ref[...].astype(F32) + b_ref[...].astype(F32)).astype(BF16)

    return pl.pallas_call(
        body, name=name, out_shape=_sds(got.shape, BF16),
        grid_spec=pltpu.PrefetchScalarGridSpec(
            num_scalar_prefetch=1, grid=(nc, r // tr),
            in_specs=[pl.BlockSpec((1, tr, c), lambda i, j, core_ref: (2 * i + core_ref[0], j, 0)),
                      pl.BlockSpec((1, tr, c), lambda i, j, core_ref: (i, j, 0))],
            out_specs=pl.BlockSpec((1, tr, c), lambda i, j, core_ref: (i, j, 0))),
        compiler_params=_params(("parallel", "parallel")),
    )(core, mine, got)


class _ChipExchange:
    def __init__(self, ins, outs, send_sems, recv_sems, loc_sems):
        self.ins, self.outs, self.send_sems, self.recv_sems, self.loc_sems = ins, outs, send_sems, recv_sems, loc_sems
        self.x, self.y, self.c, _, self.chips = _mesh_place()
        self.mine = 2 * self.x + self.y

    def own(self, a):
        return pltpu.make_async_copy(self.ins[a].at[self.mine], self.outs[a].at[self.mine], self.loc_sems.at[a])

    def copy(self, a, j, lands_in):
        chip = self.chips[j]
        return pltpu.make_async_remote_copy(
            src_ref=self.ins[a].at[2 * chip[0] + chip[1]], dst_ref=self.outs[a].at[lands_in],
            send_sem=self.send_sems.at[a, j], recv_sem=self.recv_sems.at[a, j], device_id=(*chip, self.c),
            device_id_type=MESH_ID)

    def start(self):
        for a in range(len(self.ins)):
            self.own(a).start()
            for j in range(N_CHIP - 1):
                self.copy(a, j, self.mine).start()

    def finish(self):
        for a in range(len(self.ins)):
            for j, chip in enumerate(self.chips):
                self.copy(a, j, self.mine).wait_send()
                self.copy(a, j, 2 * chip[0] + chip[1]).wait_recv()
            self.own(a).wait()

    @staticmethod
    def sems(n):
        return [pltpu.SemaphoreType.DMA((n, N_CHIP - 1)), pltpu.SemaphoreType.DMA((n, N_CHIP - 1)),
                pltpu.SemaphoreType.DMA((n,))]


def _inproj(x, ln_g, w_main, w_ab, late):
    s, d = x.shape
    n = w_main.shape[1]
    tm = _tile(s, (512, 256, 128))
    tn = _tile(n, (1664, 512, 128))
    nl = len(late)
    ni, nj = s // tm, n // tn

    def body(*refs):
        x_ref, g_ref, w_ref, wab_ref = refs[:4]
        proj_ref, ab_ref, ht_ref = refs[4 + nl:7 + nl]
        hs = refs[7 + 2 * nl]
        gather = _Gather(refs[4:4 + nl], refs[7 + nl:7 + 2 * nl], *refs[8 + 2 * nl:])
        step = pl.program_id(0) * nj + pl.program_id(1)

        @pl.when(step == 0)
        def _():
            gather.start()

        @pl.when(pl.program_id(1) == 0)
        def _():
            xv = x_ref[...]
            r = lax.rsqrt(jnp.mean(xv * xv, axis=-1, keepdims=True) + EPS)
            hf = xv * r * g_ref[...]
            h = hf.astype(BF16)
            hs[...] = h
            ht_ref[...] = hf.T.astype(BF16)
            ab_ref[...] = jnp.dot(h, wab_ref[...], preferred_element_type=F32)

        proj_ref[...] = jnp.dot(hs[...], w_ref[...], preferred_element_type=F32)

        @pl.when(step == ni * nj - 1)
        def _():
            gather.finish()

    any_spec = pl.BlockSpec(memory_space=pl.ANY)
    res = pl.pallas_call(
        body, name="inproj", grid=(ni, nj),
        in_specs=[pl.BlockSpec((tm, d), lambda i, j: (i, 0)), pl.BlockSpec((1, d), lambda i, j: (0, 0)),
                  pl.BlockSpec((d, tn), lambda i, j: (0, j)), pl.BlockSpec((d, LANE), lambda i, j: (0, 0))]
        + [any_spec] * nl,
        out_specs=[pl.BlockSpec((tm, tn), lambda i, j: (i, j)), pl.BlockSpec((tm, LANE), lambda i, j: (i, 0)),
                   pl.BlockSpec((d, tm), lambda i, j: (0, i))] + [any_spec] * nl,
        out_shape=[_sds((s, n)), _sds((s, LANE)), _sds((d, s), BF16)]
        + [_sds((N_DEV,) + a.shape, a.dtype) for a in late],
        scratch_shapes=[pltpu.VMEM((tm, d), BF16)] + _Gather.sems(nl),
        compiler_params=_params(("arbitrary", "arbitrary")),
    )(x, ln_g, w_main, w_ab, *late)
    return res[0], res[1], res[2], res[3:]


def _matmul_acc(a, b, name):
    m, k = a.shape
    n = b.shape[1]
    tm = _tile(m, (2048, 1024, 512, 256, 128))
    tn = _tile(n, (1024, 512, 256, 128))
    tk = _tile(k, (1024, 512, 256, 128))
    nk = k // tk

    def body(a_ref, b_ref, o_ref, acc):
        @pl.when(pl.program_id(2) == 0)
        def _():
            acc[...] = jnp.zeros_like(acc)

        acc[...] += jnp.dot(a_ref[...], b_ref[...], preferred_element_type=F32)

        @pl.when(pl.program_id(2) == nk - 1)
        def _():
            o_ref[...] = acc[...].astype(BF16)

    return pl.pallas_call(
        body, name=name, grid=(m // tm, n // tn, nk),
        in_specs=[pl.BlockSpec((tm, tk), lambda i, j, l: (i, l)), pl.BlockSpec((tk, tn), lambda i, j, l: (l, j))],
        out_specs=pl.BlockSpec((tm, tn), lambda i, j, l: (i, j)),
        out_shape=_sds((m, n), BF16), scratch_shapes=[pltpu.VMEM((tm, tn), F32)],
        compiler_params=_params(("parallel", "parallel", "arbitrary")),
    )(a, b)


def _matmul_tn(a, b, name):
    k, m = a.shape
    n = b.shape[1]
    tm = _tile(m, (1024, 512, 256, 128))
    tn = _tile(n, (1024, 512, 256, 128))
    tk = _tile(k, (1024, 512, 256, 128))
    nk = k // tk

    def body(a_ref, b_ref, o_ref, acc):
        @pl.when(pl.program_id(2) == 0)
        def _():
            acc[...] = jnp.zeros_like(acc)

        acc[...] += _mm_tn(a_ref[...], b_ref[...])

        @pl.when(pl.program_id(2) == nk - 1)
        def _():
            o_ref[...] = acc[...].astype(BF16)

    return pl.pallas_call(
        body, name=name, grid=(m // tm, n // tn, nk),
        in_specs=[pl.BlockSpec((tk, tm), lambda i, j, l: (l, i)), pl.BlockSpec((tk, tn), lambda i, j, l: (l, j))],
        out_specs=pl.BlockSpec((tm, tn), lambda i, j, l: (i, j)),
        out_shape=_sds((m, n), BF16), scratch_shapes=[pltpu.VMEM((tm, tn), F32)],
        compiler_params=_params(("parallel", "parallel", "arbitrary")),
    )(a, b)


def _dh_rms(pieces, w_rows, wab_rows, x, dx2, ln_g, chip_sums):
    s, d = x.shape
    npc = len(pieces)
    nx = len(chip_sums)
    tm = _tile(s, (256, 128))
    ni = s // tm
    widths = [p.shape[1] for p in pieces[:-1]]
    offs = [sum(widths[:p]) for p in range(npc - 1)]
    nw = len(w_rows)
    nin = npc + 2 * nw + 3

    def body(*refs):
        p_refs = refs[:npc]
        w_refs = refs[npc:npc + nw]
        wab_refs = refs[npc + nw:npc + 2 * nw]
        x_ref, dx2_ref, g_ref = refs[npc + 2 * nw:nin]
        gx_ref, dg_ref = refs[nin + nx:nin + nx + 2]
        exch = _ChipExchange(refs[nin:nin + nx], refs[nin + nx + 2:nin + 2 * nx + 2], *refs[nin + 2 * nx + 2:])
        step = pl.program_id(0)

        @pl.when(step == 0)
        def _():
            dg_ref[...] = jnp.zeros_like(dg_ref)
            exch.start()

        cols = []
        for w_ref, wab_ref in zip(w_refs, wab_refs):
            part = _mm_nt(p_refs[npc - 1][...], wab_ref[...])
            for p in range(npc - 1):
                part += _mm_nt(p_refs[p][...], w_ref[:, offs[p]:offs[p] + widths[p]])
            cols.append(part)
        dhv = jnp.concatenate(cols, axis=1)
        xv = x_ref[...]
        r = lax.rsqrt(jnp.mean(xv * xv, axis=-1, keepdims=True) + EPS)
        xhat = xv * r
        dg_ref[...] += _colsum(dhv * xhat)
        dxh = dhv * g_ref[...]
        gx_ref[...] = dx2_ref[...] + r * (dxh - xhat * jnp.mean(dxh * xhat, axis=-1, keepdims=True))

        @pl.when(step == ni - 1)
        def _():
            exch.finish()

    any_spec = pl.BlockSpec(memory_space=pl.ANY)
    row = pl.BlockSpec((tm, d), lambda i: (i, 0))
    vec = pl.BlockSpec((1, d), lambda i: (0, 0))
    once = lambda a: pl.BlockSpec(a.shape, lambda i: (0, 0), pipeline_mode=pl.Buffered(1))
    in_specs = [pl.BlockSpec((tm, p.shape[1]), lambda i: (i, 0)) for p in pieces]
    in_specs += [once(w) for w in w_rows] + [once(w) for w in wab_rows] + [row, row, vec] + [any_spec] * nx
    res = pl.pallas_call(
        body, name="dh_rms", grid=(ni,), in_specs=in_specs,
        out_specs=[row, vec] + [any_spec] * nx,
        out_shape=[_sds((s, d)), _sds((1, d))] + [_sds(p.shape, p.dtype) for p in chip_sums],
        scratch_shapes=_ChipExchange.sems(nx),
        compiler_params=_params(("arbitrary",)),
    )(*pieces, *w_rows, *wab_rows, x, dx2, ln_g, *chip_sums)
    return res[0], res[1], res[2:]


def _final(x, tgt, out_b, out_a, out_c, w_out, final_g):
    s, d = x.shape
    tm = _tile(s, (256, 128))

    def body(x_ref, t_ref, b_ref, a_ref, c_ref, w_ref, g_ref, dx2_ref, dx2b_ref, dm_ref, loss_ref, dg_ref):
        @pl.when(pl.program_id(0) == 0)
        def _():
            loss_ref[...] = jnp.zeros_like(loss_ref)
            dg_ref[...] = jnp.zeros_like(dg_ref)

        x2 = x_ref[...]
        x2 += jnp.dot(b_ref[...], w_ref[0:DN_W, :], preferred_element_type=F32)
        x2 += jnp.dot(a_ref[...], w_ref[DN_W:DN_W + GMLP_W, :], preferred_element_type=F32)
        x2 += jnp.dot(c_ref[...], w_ref[DN_W + GMLP_W:MIX_W, :], preferred_element_type=F32)
        r = lax.rsqrt(jnp.mean(x2 * x2, axis=-1, keepdims=True) + EPS)
        xhat = x2 * r
        g = g_ref[...]
        err = xhat * g - t_ref[...]
        tok = 0.5 * jnp.mean(err * err, axis=-1, keepdims=True)
        loss_ref[...] += jnp.broadcast_to(_colsum(tok), loss_ref.shape)
        dy = err * (1.0 / d)
        dg_ref[...] += _colsum(dy * xhat)
        dxh = dy * g
        dx2 = r * (dxh - xhat * jnp.mean(dxh * xhat, axis=-1, keepdims=True))
        dx2_ref[...] = dx2
        dx2b = dx2.astype(BF16)
        dx2b_ref[...] = dx2b
        dm_ref[...] = _mm_nt(dx2b, w_ref[...])

    row = pl.BlockSpec((tm, d), lambda i: (i, 0))
    vec = pl.BlockSpec((1, d), lambda i: (0, 0))
    return pl.pallas_call(
        body, name="final", grid=(s // tm,),
        in_specs=[row, row, pl.BlockSpec((tm, DN_W), lambda i: (i, 0)), pl.BlockSpec((tm, GMLP_W), lambda i: (i, 0)),
                  pl.BlockSpec((tm, XA_W), lambda i: (i, 0)), pl.BlockSpec((MIX_W, d), lambda i: (0, 0)), vec],
        out_specs=[row, row, pl.BlockSpec((tm, MIX_W), lambda i: (i, 0)), pl.BlockSpec((1, LANE), lambda i: (0, 0)), vec],
        out_shape=[_sds((s, d)), _sds((s, d), BF16), _sds((s, MIX_W)), _sds((1, LANE)), _sds((1, d))],
        compiler_params=_params(("arbitrary",)),
    )(x, tgt, out_b, out_a, out_c, w_out, final_g)


GU_BLK = (4 * DN_W) // GMLP_W


def _gmlp_norm(gv, lng, lnb):
    va = _gelu(gv)
    mu = jnp.mean(va, axis=-1, keepdims=True)
    xc = va - mu
    rstd = lax.rsqrt(jnp.mean(xc * xc, axis=-1, keepdims=True) + EPS)
    vhat = xc * rstd
    return vhat, rstd, vhat * lng + lnb


def _gmlp_fwd(proj, lng, lnb, ws, bs_t):
    s = proj.shape[0]
    tm = _tile(s, (512, 256, 128))

    def body(u_ref, v_ref, z_ref, lng_ref, lnb_ref, ws_ref, bst_ref, o_ref):
        _, _, vn = _gmlp_norm(v_ref[...], lng_ref[...], lnb_ref[...])
        tri = _iota2((GMLP_T, GMLP_T), 0) >= _iota2((GMLP_T, GMLP_T), 1)
        for g in range(GMLP_G):
            cs = slice(g * HEAD, (g + 1) * HEAD)
            w = jnp.where(tri, ws_ref[g], 0.0).astype(BF16)
            b = bst_ref[:, g:g + 1]
            for c in range(tm // GMLP_T):
                rs = slice(c * GMLP_T, (c + 1) * GMLP_T)
                sg = _mm(w, vn[rs, cs]) + b
                o_ref[rs, cs] = (_gelu(u_ref[rs, cs]) * sg * _silu(z_ref[rs, cs])).astype(BF16)

    col = lambda k: pl.BlockSpec((tm, GMLP_W), lambda i: (i, GU_BLK + k))
    vec = pl.BlockSpec((1, GMLP_W), lambda i: (0, 0))
    return pl.pallas_call(
        body, name="gmlp_fwd", grid=(s // tm,),
        in_specs=[col(0), col(1), col(2), vec, vec, pl.BlockSpec((GMLP_G, GMLP_T, GMLP_T), lambda i: (0, 0, 0)),
                  pl.BlockSpec((GMLP_T, GMLP_G), lambda i: (0, 0))],
        out_specs=pl.BlockSpec((tm, GMLP_W), lambda i: (i, 0)), out_shape=_sds((s, GMLP_W), BF16),
        compiler_params=_params(("parallel",)),
    )(proj, proj, proj, lng, lnb, ws, bs_t)


def _gmlp_bwd(proj, dmixed, lng, lnb, ws, bs_t):
    s = proj.shape[0]
    tm = _tile(s, (512, 256, 128))

    def body(u_ref, v_ref, z_ref, d_ref, lng_ref, lnb_ref, ws_ref, bst_ref,
             dp_ref, dws_ref, dbst_ref, dlng_ref, dlnb_ref, dvn):
        @pl.when(pl.program_id(0) == 0)
        def _():
            dws_ref[...] = jnp.zeros_like(dws_ref)
            dbst_ref[...] = jnp.zeros_like(dbst_ref)
            dlng_ref[...] = jnp.zeros_like(dlng_ref)
            dlnb_ref[...] = jnp.zeros_like(dlnb_ref)

        gv = v_ref[...]
        lng_v = lng_ref[...]
        vhat, rstd, vn = _gmlp_norm(gv, lng_v, lnb_ref[...])
        tri = _iota2((GMLP_T, GMLP_T), 0) >= _iota2((GMLP_T, GMLP_T), 1)
        for g in range(GMLP_G):
            cs = slice(g * HEAD, (g + 1) * HEAD)
            w = jnp.where(tri, ws_ref[g], 0.0).astype(BF16)
            b = bst_ref[:, g:g + 1]
            dw_acc = jnp.zeros((GMLP_T, GMLP_T), F32)
            db_acc = jnp.zeros((GMLP_T, 1), F32)
            for c in range(tm // GMLP_T):
                rs = slice(c * GMLP_T, (c + 1) * GMLP_T)
                vn_b = vn[rs, cs]
                sg = _mm(w, vn_b) + b
                gu = u_ref[rs, cs]
                gz = z_ref[rs, cs]
                da = d_ref[rs, cs]
                uact = _gelu(gu)
                sz = _silu(gz)
                ds = da * uact * sz
                dp_ref[rs, cs] = (da * sg * sz * _gelu_grad(gu)).astype(BF16)
                dp_ref[rs, 2 * GMLP_W + g * HEAD:2 * GMLP_W + (g + 1) * HEAD] = (da * uact * sg * _silu_grad(gz)).astype(BF16)
                dw_acc += _mm_nt(ds, vn_b)
                db_acc += _rowsum(ds)
                dvn[rs, cs] = _mm_tn(w, ds)
            dws_ref[g] += jnp.where(tri, dw_acc, 0.0)
            dbst_ref[:, g:g + 1] += db_acc
        dvn_v = dvn[...]
        dlng_ref[...] += _colsum(dvn_v * vhat)
        dlnb_ref[...] += _colsum(dvn_v)
        dvh = dvn_v * lng_v
        dva = rstd * (dvh - jnp.mean(dvh, axis=-1, keepdims=True) - vhat * jnp.mean(dvh * vhat, axis=-1, keepdims=True))
        dp_ref[:, GMLP_W:2 * GMLP_W] = (dva * _gelu_grad(gv)).astype(BF16)

    col = lambda k: pl.BlockSpec((tm, GMLP_W), lambda i: (i, GU_BLK + k))
    vec = pl.BlockSpec((1, GMLP_W), lambda i: (0, 0))
    wsp = pl.BlockSpec((GMLP_G, GMLP_T, GMLP_T), lambda i: (0, 0, 0))
    bsp = pl.BlockSpec((GMLP_T, GMLP_G), lambda i: (0, 0))
    return pl.pallas_call(
        body, name="gmlp_bwd", grid=(s // tm,),
        in_specs=[col(0), col(1), col(2), pl.BlockSpec((tm, GMLP_W), lambda i: (i, DN_W // GMLP_W)), vec, vec, wsp, bsp],
        out_specs=[pl.BlockSpec((tm, 3 * GMLP_W), lambda i: (i, 0)), wsp, bsp, vec, vec],
        out_shape=[_sds((s, 3 * GMLP_W), BF16), _sds((GMLP_G, GMLP_T, GMLP_T)), _sds((GMLP_T, GMLP_G)),
                   _sds((1, GMLP_W)), _sds((1, GMLP_W))],
        scratch_shapes=[pltpu.VMEM((tm, GMLP_W), F32)],
        compiler_params=_params(("arbitrary",)),
    )(proj, proj, proj, dmixed, lng, lnb, ws, bs_t)


CQ_BLK = (4 * DN_W + 3 * GMLP_W) // XA_W


def _memkv_fwd(mem, g, w_kv):
    nm, d = mem.shape

    def body(m_ref, g_ref, w_ref, kv_ref):
        mv = m_ref[...]
        r = lax.rsqrt(jnp.mean(mv * mv, axis=-1, keepdims=True) + EPS)
        kv_ref[...] = _mm(mv * r * g_ref[...], w_ref[...])

    return pl.pallas_call(body, name="memkv_fwd", out_shape=_sds((nm, 2 * XA_W)), compiler_params=_params())(mem, g, w_kv)


def _memkv_bwd(mem, g, w_kv, dkv):
    nm, d = mem.shape

    def body(m_ref, g_ref, w_ref, dkv_ref, dw_ref, dg_ref):
        mv = m_ref[...]
        r = lax.rsqrt(jnp.mean(mv * mv, axis=-1, keepdims=True) + EPS)
        xhat = mv * r
        dkv_v = dkv_ref[...]
        dw_ref[...] = _mm_tn(xhat * g_ref[...], dkv_v)
        dg_ref[...] = _colsum(_mm_nt(dkv_v, w_ref[...]) * xhat)

    return pl.pallas_call(body, name="memkv_bwd", out_shape=[_sds((d, 2 * XA_W)), _sds((1, d))],
                          compiler_params=_params())(mem, g, w_kv, dkv)


def _xattn_probs(q, mk):
    sc = _mm_nt(q, mk) * (HEAD ** -0.5)
    e = jnp.exp(sc - jnp.max(sc, axis=-1, keepdims=True))
    return e / _rowsum(e)


def _xattn_fwd(proj, mkv):
    s = proj.shape[0]
    nm = mkv.shape[0]
    tm = _tile(s, (512, 256, 128))

    def body(q_ref, z_ref, kv_ref, o_ref):
        for h in range(XA_H):
            cs = slice(h * HEAD, (h + 1) * HEAD)
            p = _xattn_probs(q_ref[:, cs], kv_ref[:, cs])
            ctx = _mm(p, kv_ref[:, XA_W + h * HEAD:XA_W + (h + 1) * HEAD])
            o_ref[:, cs] = (ctx * _silu(z_ref[:, cs])).astype(BF16)

    col = lambda k: pl.BlockSpec((tm, XA_W), lambda i: (i, CQ_BLK + k))
    return pl.pallas_call(
        body, name="xattn_fwd", grid=(s // tm,),
        in_specs=[col(0), col(1), pl.BlockSpec((nm, 2 * XA_W), lambda i: (0, 0))],
        out_specs=pl.BlockSpec((tm, XA_W), lambda i: (i, 0)), out_shape=_sds((s, XA_W), BF16),
        compiler_params=_params(("parallel",)),
    )(proj, proj, mkv)


def _xattn_bwd(proj, dmixed, mkv):
    s = proj.shape[0]
    nm = mkv.shape[0]
    tm = _tile(s, (512, 256, 128))

    def body(q_ref, z_ref, d_ref, kv_ref, dp_ref, dkv_ref):
        @pl.when(pl.program_id(0) == 0)
        def _():
            dkv_ref[...] = jnp.zeros_like(dkv_ref)

        for h in range(XA_H):
            cs = slice(h * HEAD, (h + 1) * HEAD)
            vs = slice(XA_W + h * HEAD, XA_W + (h + 1) * HEAD)
            q = q_ref[:, cs]
            z = z_ref[:, cs]
            mk = kv_ref[:, cs]
            mv = kv_ref[:, vs]
            p = _xattn_probs(q, mk)
            ctx = _mm(p, mv)
            dc = d_ref[:, cs]
            dctx = dc * _silu(z)
            dp_ref[:, vs] = (dc * ctx * _silu_grad(z)).astype(BF16)
            dp = _mm_nt(dctx, mv)
            dkv_ref[:, vs] += _mm_tn(p, dctx)
            ds = p * (dp - _rowsum(dp * p)) * (HEAD ** -0.5)
            dp_ref[:, cs] = _mm(ds, mk).astype(BF16)
            dkv_ref[:, cs] += _mm_tn(ds, q)

    col = lambda k: pl.BlockSpec((tm, XA_W), lambda i: (i, CQ_BLK + k))
    kvs = pl.BlockSpec((nm, 2 * XA_W), lambda i: (0, 0))
    return pl.pallas_call(
        body, name="xattn_bwd", grid=(s // tm,),
        in_specs=[col(0), col(1), pl.BlockSpec((tm, XA_W), lambda i: (i, (DN_W + GMLP_W) // XA_W)), kvs],
        out_specs=[pl.BlockSpec((tm, 2 * XA_W), lambda i: (i, 0)), kvs],
        out_shape=[_sds((s, 2 * XA_W), BF16), _sds((nm, 2 * XA_W))],
        compiler_params=_params(("arbitrary",)),
    )(proj, proj, dmixed, mkv)


def _softplus(x):
    return jnp.maximum(x, 0.0) + jnp.log1p(jnp.exp(-jnp.abs(x)))


def _dn_pre(proj, ab, conv_w, alog_row, dt_row):
    s = proj.shape[0]
    tm = _tile(s, (256, 128))
    w3 = 3 * DN_W

    def body(x_ref, halo_ref, ab_ref, cw_ref, al_ref, dt_ref, q_ref, k_ref, v_ref, gb_ref, gbt_ref, yc_ref):
        i = pl.program_id(0)
        xv = x_ref[...]
        cat = jnp.concatenate([jnp.where(i > 0, halo_ref[...], 0.0), xv[0:HALO]], axis=0)
        yc = cw_ref[DN_K - 1:DN_K, :] * xv
        top = cw_ref[DN_K - 1:DN_K, :] * xv[0:HALO]
        for t in range(DN_K - 1):
            back = DN_K - 1 - t
            yc += cw_ref[t:t + 1, :] * pltpu.roll(xv, back, 0)
            top += cw_ref[t:t + 1, :] * pltpu.roll(cat, back, 0)[HALO:2 * HALO]
        yc = jnp.concatenate([top, yc[HALO:tm]], axis=0)
        yc_ref[...] = yc
        act = _silu(yc)
        for h in range(DN_H):
            cs = slice(h * HEAD, (h + 1) * HEAD)
            qa = act[:, cs]
            q_ref[:, cs] = qa * (lax.rsqrt(_rowsum(qa * qa) + EPS) * (HEAD ** -0.5))
            ka = act[:, DN_W + h * HEAD:DN_W + (h + 1) * HEAD]
            k_ref[:, cs] = ka * lax.rsqrt(_rowsum(ka * ka) + EPS)
        v_ref[...] = act[:, 2 * DN_W:w3]
        abv = ab_ref[...]
        lane = _iota2((tm, LANE), 1)
        g = jnp.where(lane < DN_H, -jnp.exp(al_ref[...]) * _softplus(abv + dt_ref[...]), 0.0)
        gc = _mm_hi(_chunk_tri(tm, False), g)
        gbv = jnp.where(lane < DN_H, gc, jnp.where(lane < 2 * DN_H, jax.nn.sigmoid(abv), 0.0))
        gb_ref[...] = gbv
        for c in range(tm // CH):
            gbt_ref[c] = gbv[c * CH:(c + 1) * CH, :].T[0:2 * DN_H, :]

    hb = tm // HALO
    row = lambda w: pl.BlockSpec((tm, w), lambda i: (i, 0))
    vec = pl.BlockSpec((1, LANE), lambda i: (0, 0))
    return pl.pallas_call(
        body, name="dn_pre", grid=(s // tm,),
        in_specs=[row(w3), pl.BlockSpec((HALO, w3), lambda i: (jnp.maximum(i * hb - 1, 0), 0)), row(LANE),
                  pl.BlockSpec((DN_K, w3), lambda i: (0, 0)), vec, vec],
        out_specs=[row(DN_W), row(DN_W), row(DN_W), row(LANE), pl.BlockSpec((tm // CH, 2 * DN_H, CH), lambda i: (i, 0, 0)),
                   row(w3)],
        out_shape=[_sds((s, DN_W)), _sds((s, DN_W)), _sds((s, DN_W)), _sds((s, LANE)), _sds((s // CH, 2 * DN_H, CH)),
                   _sds((s, w3))],
        compiler_params=_params(("parallel",)),
    )(proj, proj, ab, conv_w, alog_row, dt_row)


HEADS = tuple(range(DN_H))


def _hcols(h):
    return slice(h * HEAD, (h + 1) * HEAD)


def _chunk_scalings(k, v, gbv, gbt, h):
    gc = jnp.broadcast_to(gbv[:, h:h + 1], (CH, HEAD))
    beta = jnp.broadcast_to(gbv[:, DN_H + h:DN_H + h + 1], (CH, HEAD))
    gr = gbt[h:h + 1, :]
    ii = _iota2((CH, CH), 0)
    jj = _iota2((CH, CH), 1)
    dec = jnp.exp(jnp.where(ii >= jj, gc[:, 0:CH] - gr, -1e30))
    eg = jnp.exp(gc)
    gl = gr[:, CH - 1:CH]
    kb = k * beta
    return dict(beta=beta, dec=dec, eg=eg, gl=gl, ekd=jnp.exp(gl - gc), kb=kb, vb=v * beta, kbe=kb * eg)


def _chunk_scores(m, q, k):
    kq = _mm_nt(jnp.concatenate([m["kb"], q], axis=0), k)
    strict = _iota2((CH, CH), 0) > _iota2((CH, CH), 1)
    return jnp.where(strict, kq[0:CH] * m["dec"], 0.0), kq[CH:2 * CH] * m["dec"]


def _scan_cpb(s):
    return 8 if (s // CH) % 8 == 0 else 1


def _dn_fwd(q, k, v, gb, gbt, proj, norm_g):
    s = q.shape[0]
    cpb = _scan_cpb(s)
    tb = cpb * CH
    nblk = s // tb

    def body(q_ref, k_ref, v_ref, gb_ref, gbt_ref, z_ref, ng_ref,
             w_ref, qg_ref, kd_ref, t_ref, ai_ref, egl_ref, o_ref, vn_ref, st_ref, ob_ref, state):
        @pl.when(pl.program_id(0) == 0)
        def _():
            state[...] = jnp.zeros_like(state)

        ng = ng_ref[...]
        eye = jnp.where(_iota2((CH, CH), 0) == _iota2((CH, CH), 1), 1.0, 0.0).astype(F32)

        def chunk(c, carry):
            r0 = pl.multiple_of(c * CH, CH)
            rows = pl.ds(r0, CH)
            gbv = gb_ref[rows, :]
            gbt_v = gbt_ref[c]
            qs = [q_ref[rows, _hcols(h)] for h in HEADS]
            ks = [k_ref[rows, _hcols(h)] for h in HEADS]
            ms = [_chunk_scalings(ks[h], v_ref[rows, _hcols(h)], gbv, gbt_v, h) for h in HEADS]
            qgb = [(qs[h] * ms[h]["eg"]).astype(BF16) for h in HEADS]
            kdb = [(ks[h] * ms[h]["ekd"]).astype(BF16) for h in HEADS]
            egl = [jnp.broadcast_to(jnp.exp(ms[h]["gl"]), (1, LANE)) for h in HEADS]
            for h in HEADS:
                qg_ref[rows, _hcols(h)] = qgb[h]
                kd_ref[rows, _hcols(h)] = kdb[h]
                egl_ref[c, h:h + 1, :] = egl[h]
            sc = [_chunk_scores(ms[h], qs[h], ks[h]) for h in HEADS]
            for h in HEADS:
                ai_ref[h, rows, :] = sc[h][1]
            ts = [eye - sc[h][0] for h in HEADS]
            ps = [_mm_3x(sc[h][0], sc[h][0]) for h in HEADS]
            ts = [ts[h] + _mm_3x(ts[h], ps[h]) for h in HEADS]
            for _ in range(4):
                ps = [_mm(ps[h], ps[h]) for h in HEADS]
                ts = [ts[h] + _mm(ts[h], ps[h]) for h in HEADS]
            uw = [_mm(ts[h], jnp.concatenate([ms[h]["vb"], ms[h]["kbe"]], axis=1)) for h in HEADS]
            wb = [uw[h][:, HEAD:2 * HEAD].astype(BF16) for h in HEADS]
            for h in HEADS:
                t_ref[h, rows, :] = ts[h]
                w_ref[rows, _hcols(h)] = wb[h]
            sts = [state[h] for h in HEADS]
            stb = [sts[h].astype(BF16) for h in HEADS]
            for h in HEADS:
                st_ref[c, h] = stb[h]
            vnb = [(uw[h][:, 0:HEAD] - jnp.dot(wb[h], stb[h], preferred_element_type=F32)).astype(BF16) for h in HEADS]
            for h in HEADS:
                state[h] = sts[h] * egl[h] + _mm_tn(kdb[h], vnb[h])
            os_ = [jnp.dot(qgb[h], stb[h], preferred_element_type=F32) + _mm(sc[h][1], vnb[h]) for h in HEADS]
            for h in HEADS:
                o = os_[h]
                vn_ref[rows, _hcols(h)] = vnb[h]
                o_ref[rows, _hcols(h)] = o
                r = lax.rsqrt(jnp.mean(o * o, axis=-1, keepdims=True) + EPS)
                ob_ref[rows, _hcols(h)] = (o * r * ng * _silu(z_ref[rows, _hcols(h)])).astype(BF16)
            return carry

        lax.fori_loop(0, cpb, chunk, 0, unroll=4)

    row = pl.BlockSpec((tb, DN_W), lambda i: (i, 0))
    sq = pl.BlockSpec((DN_H, tb, CH), lambda i: (0, i, 0))
    return pl.pallas_call(
        body, name="dn_fwd", grid=(nblk,),
        in_specs=[row, row, row, pl.BlockSpec((tb, LANE), lambda i: (i, 0)),
                  pl.BlockSpec((cpb, 2 * DN_H, CH), lambda i: (i, 0, 0)), pl.BlockSpec((tb, DN_W), lambda i: (i, 3)),
                  pl.BlockSpec((1, HEAD), lambda i: (0, 0))],
        out_specs=[row, row, row, sq, sq, pl.BlockSpec((cpb, DN_H, LANE), lambda i: (i, 0, 0)), row, row,
                   pl.BlockSpec((cpb, DN_H, HEAD, HEAD), lambda i: (i, 0, 0, 0)), row],
        out_shape=[_sds((s, DN_W), BF16), _sds((s, DN_W), BF16), _sds((s, DN_W), BF16), _sds((DN_H, s, CH)),
                   _sds((DN_H, s, CH)), _sds((s // CH, DN_H, LANE)), _sds((s, DN_W)), _sds((s, DN_W), BF16),
                   _sds((s // CH, DN_H, HEAD, HEAD), BF16), _sds((s, DN_W), BF16)],
        scratch_shapes=[pltpu.VMEM((DN_H, HEAD, HEAD), F32)],
        compiler_params=_params(("arbitrary",)),
    )(q, k, v, gb, gbt, proj, norm_g)


def _dn_bwd(dmixed, o, proj, norm_g, w, qg, kd, ai, egl, q, k, v, gb, gbt, t, vn, st):
    s = o.shape[0]
    cpb = 4 if (s // CH) % 4 == 0 else 1
    tb = cpb * CH
    nblk = s // tb

    def body(dm_ref, o_ref, z_ref, ng_ref, w_ref, qg_ref, kd_ref, ai_ref, egl_ref,
             q_ref, k_ref, v_ref, gb_ref, gbt_ref, t_ref, vn_ref, st_ref,
             dq_ref, dk_ref, dv_ref, dgb_ref, dz_ref, dng_ref, dstate):
        @pl.when(pl.program_id(0) == 0)
        def _():
            dstate[...] = jnp.zeros_like(dstate)
            dng_ref[...] = jnp.zeros_like(dng_ref)

        ng = ng_ref[...]
        lane = _iota2((CH, LANE), 1)
        last = _iota2((CH, 1), 0) == CH - 1
        strict = _iota2((CH, CH), 0) > _iota2((CH, CH), 1)

        def chunk(cc, carry):
            c = cpb - 1 - cc
            r0 = pl.multiple_of(c * CH, CH)
            rows = pl.ds(r0, CH)
            dng = jnp.zeros((1, HEAD), F32)
            dob = []
            for h in HEADS:
                cs = _hcols(h)
                ov = o_ref[rows, cs]
                z = z_ref[rows, cs]
                db = dm_ref[rows, cs]
                r = lax.rsqrt(jnp.mean(ov * ov, axis=-1, keepdims=True) + EPS)
                ohat = ov * r
                dz_ref[rows, cs] = (db * ohat * ng * _silu_grad(z)).astype(BF16)
                dyn = db * _silu(z)
                dng += _colsum(dyn * ohat)
                doh = dyn * ng
                dob.append((r * (doh - ohat * jnp.mean(doh * ohat, axis=-1, keepdims=True))).astype(BF16))
            dng_ref[...] += dng
            dsn = [dstate[h] for h in HEADS]
            dsb = [dsn[h].astype(BF16) for h in HEADS]
            dvnb = [(_mm_tn(ai_ref[h, rows, :], dob[h])
                     + jnp.dot(kd_ref[rows, _hcols(h)], dsb[h], preferred_element_type=F32)).astype(BF16) for h in HEADS]
            part = [_mm_tn(qg_ref[rows, _hcols(h)], dob[h]) + egl_ref[c, h:h + 1, :] * dsn[h] for h in HEADS]
            for h in HEADS:
                dstate[h] = part[h] - _mm_tn(w_ref[rows, _hcols(h)], dvnb[h])
            gbv = gb_ref[rows, :]
            gbt_v = gbt_ref[c]
            qs = [q_ref[rows, _hcols(h)] for h in HEADS]
            ks = [k_ref[rows, _hcols(h)] for h in HEADS]
            vs = [v_ref[rows, _hcols(h)] for h in HEADS]
            ms = [_chunk_scalings(ks[h], vs[h], gbv, gbt_v, h) for h in HEADS]
            sts = [st_ref[c, h] for h in HEADS]
            vnb = [vn_ref[rows, _hcols(h)] for h in HEADS]
            tbf = [t_ref[h, rows, :].astype(BF16) for h in HEADS]
            sc = [_chunk_scores(ms[h], qs[h], ks[h]) for h in HEADS]
            xs_ = [_mm_nt(jnp.concatenate([dob[h], dvnb[h]], axis=0), sts[h]) for h in HEADS]
            dai = [_mm_nt(dob[h], vnb[h]) for h in HEADS]
            dkd = [_mm_nt(vnb[h], dsb[h]) for h in HEADS]
            dqg = [xs_[h][0:CH] for h in HEADS]
            duw = [jnp.concatenate([dvnb[h], (-xs_[h][CH:2 * CH]).astype(BF16)], axis=1) for h in HEADS]
            dt = [_mm_nt(duw[h], jnp.concatenate([ms[h]["vb"], ms[h]["kbe"]], axis=1)) for h in HEADS]
            dvk = [_mm_tn(tbf[h], duw[h]) for h in HEADS]
            tdt = [_mm_tn(tbf[h], dt[h]) for h in HEADS]
            da = [jnp.where(strict, -_mm_nt(tdt[h], tbf[h]), 0.0) for h in HEADS]
            dsc = [jnp.concatenate([da[h] * ms[h]["dec"], dai[h] * ms[h]["dec"]], axis=0) for h in HEADS]
            dkq = [_mm(dsc[h], ks[h]) for h in HEADS]
            dk1 = [_mm_tn(dsc[h], jnp.concatenate([ms[h]["kb"], qs[h]], axis=0)) for h in HEADS]
            dgb = jnp.zeros((CH, LANE), F32)
            for h in HEADS:
                m = ms[h]
                eg, ekd, beta = m["eg"], m["ekd"], m["beta"]
                dvb = dvk[h][:, 0:HEAD]
                dkbe = dvk[h][:, HEAD:2 * HEAD]
                kdv = ks[h] * ekd
                dkb = dkq[h][0:CH] + dkbe * eg
                dq_ref[rows, _hcols(h)] = dkq[h][CH:2 * CH] + dqg[h] * eg
                dk_ref[rows, _hcols(h)] = dk1[h] + dkd[h] * ekd + dkb * beta
                dv_ref[rows, _hcols(h)] = dvb * beta
                dkd_kd = dkd[h] * kdv
                dgl = (jnp.exp(m["gl"]) * _rowsum(_colsum(sts[h].astype(F32) * dsb[h].astype(F32)))
                       + _rowsum(_colsum(dkd_kd)))
                mm_ = da[h] * sc[h][0] + dai[h] * sc[h][1]
                dgc = (_rowsum(mm_ - mm_.T) + _rowsum(dqg[h] * qs[h] * eg - dkd_kd + dkbe * m["kbe"])
                       + jnp.where(last, dgl, 0.0))
                dbeta = _rowsum(dkb * ks[h] + dvb * vs[h])
                dgb = jnp.where(lane == h, dgc, jnp.where(lane == DN_H + h, dbeta, dgb))
            dgb_ref[rows, :] = dgb
            return carry

        lax.fori_loop(0, cpb, chunk, 0, unroll=2)

    rev = lambda i: (nblk - 1 - i, 0)
    row = pl.BlockSpec((tb, DN_W), rev)
    vec = pl.BlockSpec((1, HEAD), lambda i: (0, 0))
    sq = pl.BlockSpec((DN_H, tb, CH), lambda i: (0, nblk - 1 - i, 0))
    gbs = pl.BlockSpec((tb, LANE), rev)
    return pl.pallas_call(
        body, name="dn_bwd", grid=(nblk,),
        in_specs=[row, row, pl.BlockSpec((tb, DN_W), lambda i: (nblk - 1 - i, 3)), vec, row, row, row, sq,
                  pl.BlockSpec((cpb, DN_H, LANE), lambda i: (nblk - 1 - i, 0, 0)),
                  row, row, row, gbs, pl.BlockSpec((cpb, 2 * DN_H, CH), lambda i: (nblk - 1 - i, 0, 0)), sq, row,
                  pl.BlockSpec((cpb, DN_H, HEAD, HEAD), lambda i: (nblk - 1 - i, 0, 0, 0))],
        out_specs=[row, row, row, gbs, row, vec],
        out_shape=[_sds((s, DN_W)), _sds((s, DN_W)), _sds((s, DN_W)), _sds((s, LANE)), _sds((s, DN_W), BF16),
                   _sds((1, HEAD))],
        scratch_shapes=[pltpu.VMEM((DN_H, HEAD, HEAD), F32)],
        compiler_params=_params(("arbitrary",)),
    )(dmixed, o, proj, norm_g, w, qg, kd, ai, egl, q, k, v, gb, gbt, t, vn, st)


def _dn_pre_bwd(proj, yc_all, ab, conv_w, alog_row, dt_row, dq, dk, dv, dgb):
    s = proj.shape[0]
    tm = _tile(s, (256, 128))
    w3 = 3 * DN_W
    nblk = s // tm

    def body(x_ref, yc_ref, ab_ref, cw_ref, al_ref, dt_ref, dq_ref, dk_ref, dv_ref, dgb_ref,
             dx_ref, dab_ref, dcw_ref, dal_ref, ddt_ref, exd, carry):
        i = pl.program_id(0)

        @pl.when(i == 0)
        def _():
            carry[...] = jnp.zeros_like(carry)
            dcw_ref[...] = jnp.zeros_like(dcw_ref)
            dal_ref[...] = jnp.zeros_like(dal_ref)
            ddt_ref[...] = jnp.zeros_like(ddt_ref)

        yc = yc_ref[...]
        sg = jax.nn.sigmoid(yc)
        act = yc * sg
        dact = sg * (1.0 + yc * (1.0 - sg))
        for h in range(DN_H):
            cs = slice(h * HEAD, (h + 1) * HEAD)
            ks = slice(DN_W + h * HEAD, DN_W + (h + 1) * HEAD)
            qa = act[:, cs]
            rq = lax.rsqrt(_rowsum(qa * qa) + EPS)
            qh = qa * rq
            dqv = dq_ref[:, cs]
            exd[0:tm, cs] = (HEAD ** -0.5) * rq * (dqv - qh * _rowsum(dqv * qh)) * dact[:, cs]
            ka = act[:, ks]
            rk = lax.rsqrt(_rowsum(ka * ka) + EPS)
            kh = ka * rk
            dkv = dk_ref[:, cs]
            exd[0:tm, ks] = rk * (dkv - kh * _rowsum(dkv * kh)) * dact[:, ks]
        exd[0:tm, 2 * DN_W:w3] = dv_ref[...] * dact[:, 2 * DN_W:w3]
        xv = x_ref[...]
        dyc = exd[...]
        cat = jnp.concatenate([dyc[tm - HALO:tm], carry[...]], axis=0)
        dcw_ref[DN_K - 1:DN_K, :] += _colsum(dyc * xv)
        dx = cw_ref[DN_K - 1:DN_K, :] * dyc
        for t in range(DN_K - 1):
            ahead = DN_K - 1 - t
            view = jnp.concatenate([pltpu.roll(dyc, tm - ahead, 0)[0:tm - HALO],
                                    pltpu.roll(cat, 2 * HALO - ahead, 0)[0:HALO]], axis=0)
            dcw_ref[t:t + 1, :] += _colsum(view * xv)
            dx += cw_ref[t:t + 1, :] * view
        dx_ref[...] = dx.astype(BF16)
        carry[...] = dyc[0:HALO]

        lane = _iota2((tm, LANE), 1)
        dgbv = dgb_ref[...]
        dg = _mm_hi(_chunk_tri(tm, True), jnp.where(lane < DN_H, dgbv, 0.0))
        abv = ab_ref[...]
        xa = abv + dt_ref[...]
        nea = -jnp.exp(al_ref[...])
        d_da = jnp.where(lane < DN_H, dg * nea * jax.nn.sigmoid(xa), 0.0)
        dal_ref[...] += _colsum(jnp.where(lane < DN_H, dg * nea * _softplus(xa), 0.0))
        ddt_ref[...] += _colsum(d_da)
        beta = jax.nn.sigmoid(abv)
        d_db = jnp.where((lane >= DN_H) & (lane < 2 * DN_H), dgbv * beta * (1.0 - beta), 0.0)
        dab_ref[...] = (d_da + d_db).astype(BF16)

    rev = lambda i: (nblk - 1 - i, 0)
    row = lambda w: pl.BlockSpec((tm, w), rev)
    vec = pl.BlockSpec((1, LANE), lambda i: (0, 0))
    cws = pl.BlockSpec((DN_K, w3), lambda i: (0, 0))
    return pl.pallas_call(
        body, name="dn_pre_bwd", grid=(nblk,),
        in_specs=[row(w3), row(w3), row(LANE), cws, vec, vec, row(DN_W), row(DN_W), row(DN_W), row(LANE)],
        out_specs=[row(w3), row(LANE), cws, vec, vec],
        out_shape=[_sds((s, w3), BF16), _sds((s, LANE), BF16), _sds((DN_K, w3)), _sds((1, LANE)), _sds((1, LANE))],
        scratch_shapes=[pltpu.VMEM((tm, w3), F32), pltpu.VMEM((HALO, w3), F32)],
        compiler_params=_params(("arbitrary",)),
    )(proj, yc_all, ab, conv_w, alog_row, dt_row, dq, dk, dv, dgb)


def _adam(parts, w, m, v, name):
    r, c = w.shape
    n_parts = parts.shape[0]
    small = n_parts * r * c * 4 <= 4 * 1024 * 1024
    tr = r if small else _tile(r, (128, 64, 32, 16, 8))

    def body(p_ref, w_ref, m_ref, v_ref, g_ref, d_ref, nm_ref, nv_ref):
        g = p_ref[0].astype(F32)
        for k in range(1, n_parts):
            g = g + p_ref[k].astype(F32)
        g_ref[...] = g
        mn = ADAM_B1 * m_ref[...] + (1.0 - ADAM_B1) * g
        vn = ADAM_B2 * v_ref[...] + (1.0 - ADAM_B2) * (g * g)
        m_hat = mn / (1.0 - ADAM_B1 ** ADAM_STEP)
        v_hat = vn / (1.0 - ADAM_B2 ** ADAM_STEP)
        d_ref[...] = -ADAM_LR * (m_hat / (jnp.sqrt(v_hat) + ADAM_EPS) + ADAM_WD * w_ref[...])
        nm_ref[...] = mn
        nv_ref[...] = vn

    blk = pl.BlockSpec((tr, c), lambda i: (i, 0))
    return pl.pallas_call(
        body, name=name, grid=(r // tr,),
        in_specs=[pl.BlockSpec((n_parts, tr, c), lambda i: (0, i, 0)), blk, blk, blk],
        out_specs=[blk, blk, blk, blk], out_shape=[_sds((r, c))] * 4,
        compiler_params=_params(("parallel",)),
    )(parts, w, m, v)


_PACK_ROWS = 8


def _pack(vals):
    tiles = []
    for a in vals:
        flat = a.reshape(-1).astype(F32)
        unit = _PACK_ROWS * LANE
        n = -(-flat.shape[0] // unit) * unit
        tiles.append(jnp.pad(flat, (0, n - flat.shape[0])).reshape(n // LANE, LANE))
    return jnp.concatenate(tiles, axis=0)


def _unpack(packed, shapes):
    out = []
    r0 = 0
    for shp in shapes:
        size = 1
        for dim in shp:
            size *= dim
        unit = _PACK_ROWS * LANE
        rows = -(-size // unit) * _PACK_ROWS
        out.append(packed[r0:r0 + rows].reshape(-1)[:size].reshape(shp))
        r0 += rows
    return out


def _lane_row(vec8):
    return jnp.pad(vec8.reshape(1, -1).astype(F32), ((0, 0), (0, LANE - vec8.size)))


def kernel(x, mem, ln_g, w_in, gmlp_ln_g, gmlp_ln_b, gmlp_ws, gmlp_bs, conv_w, dn_a_log, dn_dt_bias, dn_norm_g, mem_norm_g, w_mem_kv, w_out, final_g, loss_target, m_ln_g, m_w_in, m_gmlp_ln_g, m_gmlp_ln_b, m_gmlp_ws, m_gmlp_bs, m_conv_w, m_dn_a_log, m_dn_dt_bias, m_dn_norm_g, m_mem_norm_g, m_w_mem_kv, m_w_out, m_final_g, v_ln_g, v_w_in, v_gmlp_ln_g, v_gmlp_ln_b, v_gmlp_ws, v_gmlp_bs, v_conv_w, v_dn_a_log, v_dn_dt_bias, v_dn_norm_g, v_mem_norm_g, v_w_mem_kv, v_w_out, v_final_g):
    xs = x[0]
    mems = mem[0]
    tgt = loss_target[0]
    s, d = xs.shape
    shard_w = w_in.shape[2]
    in_w = N_DEV * shard_w
    me = 4 * lax.axis_index("x") + 2 * lax.axis_index("y") + lax.axis_index("c")

    (g_in,) = _gather_two_level([w_in[0].astype(BF16)], "gather_w_in")
    o_g, o_dn, o_ab = 0, 3 * GMLP_W, 3 * GMLP_W + 4 * DN_W
    o_xa = o_ab + 2 * DN_H

    def shard_cols(g, lo, hi):
        out = []
        while lo < hi:
            sh = lo // shard_w
            end = min(hi, (sh + 1) * shard_w)
            out.append(g[sh][:, lo - sh * shard_w:end - sh * shard_w])
            lo = end
        return out

    def own_layout(g):
        main = jnp.concatenate(shard_cols(g, o_dn, o_ab) + shard_cols(g, o_g, o_dn) + shard_cols(g, o_xa, in_w), axis=1)
        return main, jnp.pad(jnp.concatenate(shard_cols(g, o_ab, o_xa), axis=1), ((0, 0), (0, LANE - 2 * DN_H)))

    w_main, w_ab = own_layout(g_in)

    ln_g2 = ln_g.reshape(1, d)
    lng2 = gmlp_ln_g.reshape(1, GMLP_W)
    lnb2 = gmlp_ln_b.reshape(1, GMLP_W)
    ws3 = gmlp_ws[0]
    bs_t = gmlp_bs[0].T
    alog_row = _lane_row(dn_a_log)
    dt_row = _lane_row(dn_dt_bias)
    dn_g2 = dn_norm_g.reshape(1, HEAD)
    mem_g2 = mem_norm_g.reshape(1, d)
    fin_g2 = final_g.reshape(1, d)

    proj, ab, h_t, (g_out, g_kv, g_conv) = _inproj(
        xs, ln_g2, w_main, w_ab, [w_out[0].astype(BF16), w_mem_kv[0].astype(BF16), conv_w[0]])
    wo = g_out.reshape(MIX_W, d)
    wo_perm = jnp.concatenate([wo[GMLP_W:GMLP_W + DN_W], wo[0:GMLP_W], wo[GMLP_W + DN_W:MIX_W]], axis=0)
    w_kv = g_kv.reshape(d, 2 * XA_W)
    conv_full = g_conv.transpose(1, 0, 2).reshape(DN_K, 3 * DN_W)
    out_a = _gmlp_fwd(proj, lng2, lnb2, ws3, bs_t)
    mkv = _memkv_fwd(mems, mem_g2, w_kv)
    out_c = _xattn_fwd(proj, mkv)
    q, k, v, gb, gbt, yc = _dn_pre(proj, ab, conv_full, alog_row, dt_row)
    wk, qg, kd, tmat, ai, egl, o, vn, st, out_b = _dn_fwd(q, k, v, gb, gbt, proj, dn_g2)

    dx2, dx2b, dmixed, loss_acc, d_fin_g = _final(xs, tgt, out_b, out_a, out_c, wo_perm, fin_g2)
    loss = lax.psum(loss_acc[0, 0], ("x", "y", "c"))

    dwo_b = _matmul_tn(out_b, dx2b, "dw_out_b")
    dwo_a = _matmul_tn(out_a, dx2b, "dw_out_a")
    dwo_c = _matmul_tn(out_c, dx2b, "dw_out_c")
    d_w_out = jnp.concatenate([dwo_a, dwo_b, dwo_c], axis=0)

    dp_g, d_ws, d_bst, d_lng, d_lnb = _gmlp_bwd(proj, dmixed, lng2, lnb2, ws3, bs_t)
    dp_x, dmkv = _xattn_bwd(proj, dmixed, mkv)
    d_w_kv, d_mem_g = _memkv_bwd(mems, mem_g2, w_kv, dmkv)
    dq, dk, dv, dgb, dp_dz, d_dn_g = _dn_bwd(dmixed, o, proj, dn_g2, wk, qg, kd, ai, egl, q, k, v, gb, gbt, tmat, vn, st)
    dp_qkv, dp_ab, d_conv, d_alog, d_dt = _dn_pre_bwd(proj, yc, ab, conv_full, alog_row, dt_row, dq, dk, dv, dgb)

    dw_qkv = _matmul_acc(h_t, dp_qkv, "dw_in_qkv")
    dw_dz = _matmul_acc(h_t, dp_dz, "dw_in_dz")
    dw_gm = _matmul_acc(h_t, dp_g, "dw_in_gmlp")
    dw_xa = _matmul_acc(h_t, dp_x, "dw_in_xa")
    dw_ab = _matmul_acc(h_t, dp_ab, "dw_in_ab")
    segs = [(o_g, dw_gm), (o_dn, dw_qkv), (o_dn + 3 * DN_W, dw_dz), (o_ab, dw_ab[:, :2 * DN_H]), (o_xa, dw_xa)]
    shards = []
    for sh in range(N_DEV):
        lo, hi = sh * shard_w, (sh + 1) * shard_w
        parts = [arr[:, max(lo, off) - off:min(hi, off + arr.shape[1]) - off] for off, arr in segs
                 if off < hi and off + arr.shape[1] > lo]
        shards.append(jnp.concatenate(parts, axis=1).astype(BF16))
    send_in = jnp.stack(shards)

    small_shapes = [gmlp_ln_g.shape, gmlp_ln_b.shape, gmlp_ws.shape, gmlp_bs.shape, dn_a_log.shape,
                    dn_dt_bias.shape, dn_norm_g.shape, mem_norm_g.shape, final_g.shape, (DN_K, 3 * DN_W)]
    small_g = _pack([d_lng, d_lnb, d_ws, d_bst.T, d_alog[:, :DN_H], d_dt[:, :DN_H], d_dn_g, d_mem_g, d_fin_g, d_conv])
    zc = jnp.zeros((DN_K, 3 * DN_W), F32)
    small_w = _pack([gmlp_ln_g, gmlp_ln_b, gmlp_ws, gmlp_bs, dn_a_log, dn_dt_bias, dn_norm_g, mem_norm_g, final_g, zc])
    small_m = _pack([m_gmlp_ln_g, m_gmlp_ln_b, m_gmlp_ws, m_gmlp_bs, m_dn_a_log, m_dn_dt_bias, m_dn_norm_g,
                     m_mem_norm_g, m_final_g, zc])
    small_v = _pack([v_gmlp_ln_g, v_gmlp_ln_b, v_gmlp_ws, v_gmlp_bs, v_dn_a_log, v_dn_dt_bias, v_dn_norm_g,
                     v_mem_norm_g, v_final_g, zc + 1.0])

    send_out = d_w_out.reshape(N_DEV, MIX_W // N_DEV, d).astype(BF16)
    send_kv = d_w_kv.reshape(N_DEV, d // N_DEV, 2 * XA_W).astype(BF16)
    sends = [send_in, send_out, send_kv]
    all_small, got = _swap_halves(small_g, sends, "swap_halves")
    core = lax.axis_index("c").astype(jnp.int32).reshape(1)
    chip_sums = [_pair_sum(core, sends[i], got[i], "pair_sum_%d" % i) for i in range(3)]
    grad_x, d_ln_g, (r_in, r_out, r_kv) = _dh_rms(
        [dp_qkv, dp_dz, dp_g, dp_x, dp_ab], [w_main], [w_ab], xs, dx2, ln_g2, chip_sums)
    (all_ln_g,) = _gather_two_level([_pack([d_ln_g])], "gather_ln_g")

    g_w_in, dl_w_in, nm_w_in, nv_w_in = _adam(r_in, w_in[0], m_w_in[0], v_w_in[0], "adam_w_in")
    g_w_out, dl_w_out, nm_w_out, nv_w_out = _adam(r_out, w_out[0], m_w_out[0], v_w_out[0], "adam_w_out")
    g_w_kv, dl_w_kv, nm_w_kv, nv_w_kv = _adam(r_kv, w_mem_kv[0], m_w_mem_kv[0], v_w_mem_kv[0], "adam_w_kv")
    sm = [_unpack(t, small_shapes) for t in _adam(all_small, small_w, small_m, small_v, "adam_small")]
    ln_res = [_unpack(t, [ln_g.shape])[0]
              for t in _adam(all_ln_g, _pack([ln_g]), _pack([m_ln_g]), _pack([v_ln_g]), "adam_ln_g")]

    conv_parts = lax.dynamic_slice(all_small, (0, all_small.shape[1] - (DN_K * 3 * DN_W) // LANE, 0),
                                   (N_DEV, (DN_K * 3 * DN_W) // LANE, LANE)).reshape(N_DEV, DN_K, 3 * DN_W)
    cshard = conv_w.shape[2]
    conv_parts = lax.dynamic_slice(conv_parts, (0, 0, me * cshard), (N_DEV, DN_K, cshard))
    cpad = ((0, 0), (0, HALO - DN_K), (0, 0))
    conv_res = _adam(jnp.pad(conv_parts, cpad), jnp.pad(conv_w[0], cpad[1:]), jnp.pad(m_conv_w[0], cpad[1:]),
                     jnp.pad(v_conv_w[0], cpad[1:], constant_values=1.0), "adam_conv")
    g_conv_s, dl_conv, nm_conv, nv_conv = [t[:DN_K][None] for t in conv_res]

    def group(idx, big_in, big_conv, big_kv, big_out):
        names = sm[idx]
        return [ln_res[idx], big_in[None], names[0], names[1], names[2], names[3], big_conv, names[4], names[5], names[6],
                names[7], big_kv[None], big_out[None], names[8]]

    grads = group(0, g_w_in, g_conv_s, g_w_kv, g_w_out)
    deltas = group(1, dl_w_in, dl_conv, dl_w_kv, dl_w_out)
    new_m = group(2, nm_w_in, nm_conv, nm_w_kv, nm_w_out)
    new_v = group(3, nv_w_in, nv_conv, nv_w_kv, nv_w_out)
    return (loss, grad_x[None], *grads, *deltas, *new_m, *new_v)
```

```python
import functools

import jax
import jax.numpy as jnp
from jax import lax
from jax.experimental import pallas as pl
from jax.experimental.pallas import tpu as pltpu

F32 = jnp.float32
BF16 = jnp.bfloat16
HIGHEST = lax.Precision.HIGHEST
MESH_ID = pl.DeviceIdType.MESH

N_DEV = 8
EPS = 1e-6
GMLP_W = 512
GMLP_G = 4
GMLP_T = 128
DN_W = 1024
DN_H = 8
HEAD = 128
DN_K = 4
CH = 64
XA_W = 512
XA_H = 4
LANE = 128
HALO = 8
MAIN_W = 4 * DN_W + 3 * GMLP_W + 2 * XA_W
MIX_W = DN_W + GMLP_W + XA_W
VMEM_LIMIT = 56 * 1024 * 1024

ADAM_LR = 0.001
ADAM_B1 = 0.9
ADAM_B2 = 0.999
ADAM_EPS = 1e-08
ADAM_WD = 0.01
ADAM_STEP = 10


def _sds(shape, dtype=F32):
    return jax.ShapeDtypeStruct(tuple(shape), dtype)


def _params(sem=None):
    if sem is None:
        return pltpu.CompilerParams(vmem_limit_bytes=VMEM_LIMIT)
    return pltpu.CompilerParams(dimension_semantics=tuple(sem), vmem_limit_bytes=VMEM_LIMIT)


def _tile(n, prefs):
    for p in prefs:
        if n % p == 0:
            return p
    return n


def _mm(a, b):
    return jnp.dot(a.astype(BF16), b.astype(BF16), preferred_element_type=F32)


def _mm_nt(a, b):
    return lax.dot_general(a.astype(BF16), b.astype(BF16), (((1,), (1,)), ((), ())), preferred_element_type=F32)


def _mm_tn(a, b):
    return lax.dot_general(a.astype(BF16), b.astype(BF16), (((0,), (0,)), ((), ())), preferred_element_type=F32)


def _mm_hi(a, b):
    return jnp.dot(a, b, precision=HIGHEST, preferred_element_type=F32)


def _mm_3x(a, b):
    return jnp.dot(a, b, precision=lax.Precision.HIGH, preferred_element_type=F32)


_GELU_C = 0.7978845608028654
_GELU_A = 0.044715


def _gelu(x):
    return 0.5 * x * (1.0 + jnp.tanh(_GELU_C * (x + _GELU_A * x * x * x)))


def _gelu_grad(x):
    t = jnp.tanh(_GELU_C * (x + _GELU_A * x * x * x))
    return 0.5 * (1.0 + t) + 0.5 * x * (1.0 - t * t) * _GELU_C * (1.0 + 3.0 * _GELU_A * x * x)


def _silu(x):
    return x * jax.nn.sigmoid(x)


def _silu_grad(x):
    s = jax.nn.sigmoid(x)
    return s * (1.0 + x * (1.0 - s))


def _rowsum(x):
    return jnp.sum(x, axis=-1, keepdims=True)


def _colsum(x):
    return jnp.sum(x, axis=0, keepdims=True)


def _iota2(shape, dim):
    return lax.broadcasted_iota(jnp.int32, shape, dim)


def _chunk_tri(tm, upper):
    r = _iota2((tm, tm), 0)
    c = _iota2((tm, tm), 1)
    same = lax.shift_right_logical(r, 6) == lax.shift_right_logical(c, 6)
    tri = (r <= c) if upper else (r >= c)
    return jnp.where(same & tri, 1.0, 0.0).astype(F32)


N_CHIP = 4


def _mesh_place():
    x, y, c = lax.axis_index("x"), lax.axis_index("y"), lax.axis_index("c")
    chips = [(1 - x, y), (x, 1 - y), (1 - x, 1 - y)]
    return x, y, c, (x, y, 1 - c), chips


class _Gather:
    def __init__(self, ins, outs, send_sems, recv_sems, loc_sems):
        self.ins, self.outs, self.send_sems, self.recv_sems, self.loc_sems = ins, outs, send_sems, recv_sems, loc_sems
        self.x, self.y, self.c, self.sib, self.chips = _mesh_place()
        self.me = (self.x, self.y, self.c)

    def copy(self, a, k, block, to, src=None):
        slot = self.outs[a].at[4 * block[0] + 2 * block[1] + block[2]]
        return pltpu.make_async_remote_copy(
            src_ref=slot if src is None else src, dst_ref=slot, send_sem=self.send_sems.at[a, k],
            recv_sem=self.recv_sems.at[a, k], device_id=to, device_id_type=MESH_ID)

    def own(self, a):
        return pltpu.make_async_copy(self.ins[a], self.outs[a].at[4 * self.x + 2 * self.y + self.c], self.loc_sems.at[a])

    def first(self, a):
        return [self.copy(a, 0, self.me, self.sib, src=self.ins[a])] + [
            self.copy(a, 1 + j, self.me, (*chip, self.c), src=self.ins[a]) for j, chip in enumerate(self.chips)]

    def passed(self, a, j):
        return self.copy(a, 4 + j, (*self.chips[j], self.c), self.sib)

    def start(self):
        for a in range(len(self.ins)):
            self.own(a).start()
            for cp in self.first(a):
                cp.start()

    def finish(self):
        n = len(self.ins)
        for a in range(n):
            for j, chip in enumerate(self.chips):
                self.copy(a, 1 + j, (*chip, self.c), self.me).wait_recv()
                self.passed(a, j).start()
        for a in range(n):
            self.copy(a, 0, self.sib, self.me).wait_recv()
            for j, chip in enumerate(self.chips):
                self.copy(a, 4 + j, (*chip, 1 - self.c), self.me).wait_recv()
        for a in range(n):
            for cp in self.first(a) + [self.passed(a, j) for j in range(N_CHIP - 1)]:
                cp.wait_send()
            self.own(a).wait()

    @staticmethod
    def sems(n):
        return [pltpu.SemaphoreType.DMA((n, N_DEV - 1)), pltpu.SemaphoreType.DMA((n, N_DEV - 1)),
                pltpu.SemaphoreType.DMA((n,))]


def _gather_two_level(arrs, name):
    n = len(arrs)

    def body(*refs):
        g = _Gather(refs[:n], refs[n:2 * n], *refs[2 * n:])
        g.start()
        g.finish()

    any_spec = pl.BlockSpec(memory_space=pl.ANY)
    return pl.pallas_call(
        body, name=name, out_shape=[_sds((N_DEV,) + a.shape, a.dtype) for a in arrs],
        in_specs=[any_spec] * n, out_specs=[any_spec] * n, scratch_shapes=_Gather.sems(n),
        compiler_params=pltpu.CompilerParams(has_side_effects=True),
    )(*arrs)


def _swap_halves(small, grads, name):
    n = len(grads)

    def body(*refs):
        small_ref = refs[0]
        ins = refs[1:1 + n]
        small_out = refs[1 + n]
        got = refs[2 + n:2 + 2 * n]
        s_send, s_recv, g_send, g_recv, loc_sem = refs[2 + 2 * n:]
        x, y, c, sib, _ = _mesh_place()
        me = 4 * x + 2 * y + c
        sends, recvs = [], []
        for j in range(1, N_DEV):
            px = 1 - x if (j >> 2) & 1 else x
            py = 1 - y if (j >> 1) & 1 else y
            pc = 1 - c if j & 1 else c
            cp = pltpu.make_async_remote_copy(
                src_ref=small_ref, dst_ref=small_out.at[me], send_sem=s_send.at[j - 1], recv_sem=s_recv.at[j - 1],
                device_id=(px, py, pc), device_id_type=MESH_ID)
            cp.start()
            sends.append(cp)
            recvs.append(pltpu.make_async_remote_copy(
                src_ref=small_ref, dst_ref=small_out.at[4 * px + 2 * py + pc], send_sem=s_send.at[j - 1],
                recv_sem=s_recv.at[j - 1], device_id=(px, py, pc), device_id_type=MESH_ID))
        own = pltpu.make_async_copy(small_ref, small_out.at[me], loc_sem)
        own.start()
        for a in range(n):
            for chip in range(N_CHIP):
                cp = pltpu.make_async_remote_copy(
                    src_ref=ins[a].at[2 * chip + 1 - c], dst_ref=got[a].at[chip], send_sem=g_send.at[a, chip],
                    recv_sem=g_recv.at[a, chip], device_id=sib, device_id_type=MESH_ID)
                cp.start()
                sends.append(cp)
                recvs.append(cp)
        for cp in sends:
            cp.wait_send()
        for cp in recvs:
            cp.wait_recv()
        own.wait()

    half = [_sds((N_CHIP,) + g.shape[1:], g.dtype) for g in grads]
    any_spec = pl.BlockSpec(memory_space=pl.ANY)
    res = pl.pallas_call(
        body, name=name, out_shape=[_sds((N_DEV,) + small.shape, small.dtype)] + half,
        in_specs=[any_spec] * (1 + n), out_specs=[any_spec] * (1 + n),
        scratch_shapes=[pltpu.SemaphoreType.DMA((N_DEV - 1,)), pltpu.SemaphoreType.DMA((N_DEV - 1,)),
                        pltpu.SemaphoreType.DMA((n, N_CHIP)), pltpu.SemaphoreType.DMA((n, N_CHIP)),
                        pltpu.SemaphoreType.DMA],
        compiler_params=pltpu.CompilerParams(has_side_effects=True),
    )(small, *grads)
    return res[0], res[1:]


def _pair_sum(core, mine, got, name):
    nc, r, c = got.shape
    tr = _tile(r, (256, 128, 64, 32, 16))

    def body(core_ref, a_ref, b_ref, o_ref):
        o_ref[...] = (a_ref[...].astype(F32) + b_ref[...].astype(F32)).astype(BF16)

    return pl.pallas_call(
        body, name=name, out_shape=_sds(got.shape, BF16),
        grid_spec=pltpu.PrefetchScalarGridSpec(
            num_scalar_prefetch=1, grid=(nc, r // tr),
            in_specs=[pl.BlockSpec((1, tr, c), lambda i, j, core_ref: (2 * i + core_ref[0], j, 0)),
                      pl.BlockSpec((1, tr, c), lambda i, j, core_ref: (i, j, 0))],
            out_specs=pl.BlockSpec((1, tr, c), lambda i, j, core_ref: (i, j, 0))),
        compiler_params=_params(("parallel", "parallel")),
    )(core, mine, got)


class _ChipExchange:
    def __init__(self, ins, outs, send_sems, recv_sems, loc_sems):
        self.ins, self.outs, self.send_sems, self.recv_sems, self.loc_sems = ins, outs, send_sems, recv_sems, loc_sems
        self.x, self.y, self.c, _, self.chips = _mesh_place()
        self.mine = 2 * self.x + self.y

    def own(self, a):
        return pltpu.make_async_copy(self.ins[a].at[self.mine], self.outs[a].at[self.mine], self.loc_sems.at[a])

    def copy(self, a, j, lands_in):
        chip = self.chips[j]
        return pltpu.make_async_remote_copy(
            src_ref=self.ins[a].at[2 * chip[0] + chip[1]], dst_ref=self.outs[a].at[lands_in],
            send_sem=self.send_sems.at[a, j], recv_sem=self.recv_sems.at[a, j], device_id=(*chip, self.c),
            device_id_type=MESH_ID)

    def start(self):
        for a in range(len(self.ins)):
            self.own(a).start()
            for j in range(N_CHIP - 1):
                self.copy(a, j, self.mine).start()

    def finish(self):
        for a in range(len(self.ins)):
            for j, chip in enumerate(self.chips):
                self.copy(a, j, self.mine).wait_send()
                self.copy(a, j, 2 * chip[0] + chip[1]).wait_recv()
            self.own(a).wait()

    @staticmethod
    def sems(n):
        return [pltpu.SemaphoreType.DMA((n, N_CHIP - 1)), pltpu.SemaphoreType.DMA((n, N_CHIP - 1)),
                pltpu.SemaphoreType.DMA((n,))]


def _inproj(x, ln_g, w_main, w_ab, late):
    s, d = x.shape
    n = w_main.shape[1]
    tm = _tile(s, (512, 256, 128))
    tn = _tile(n, (1664, 512, 128))
    nl = len(late)
    ni, nj = s // tm, n // tn

    def body(*refs):
        x_ref, g_ref, w_ref, wab_ref = refs[:4]
        proj_ref, ab_ref, ht_ref = refs[4 + nl:7 + nl]
        hs = refs[7 + 2 * nl]
        gather = _Gather(refs[4:4 + nl], refs[7 + nl:7 + 2 * nl], *refs[8 + 2 * nl:])
        step = pl.program_id(0) * nj + pl.program_id(1)

        @pl.when(step == 0)
        def _():
            gather.start()

        @pl.when(pl.program_id(1) == 0)
        def _():
            xv = x_ref[...]
            r = lax.rsqrt(jnp.mean(xv * xv, axis=-1, keepdims=True) + EPS)
            hf = xv * r * g_ref[...]
            h = hf.astype(BF16)
            hs[...] = h
            ht_ref[...] = hf.T.astype(BF16)
            ab_ref[...] = jnp.dot(h, wab_ref[...], preferred_element_type=F32)

        proj_ref[...] = jnp.dot(hs[...], w_ref[...], preferred_element_type=F32)

        @pl.when(step == ni * nj - 1)
        def _():
            gather.finish()

    any_spec = pl.BlockSpec(memory_space=pl.ANY)
    res = pl.pallas_call(
        body, name="inproj", grid=(ni, nj),
        in_specs=[pl.BlockSpec((tm, d), lambda i, j: (i, 0)), pl.BlockSpec((1, d), lambda i, j: (0, 0)),
                  pl.BlockSpec((d, tn), lambda i, j: (0, j)), pl.BlockSpec((d, LANE), lambda i, j: (0, 0))]
        + [any_spec] * nl,
        out_specs=[pl.BlockSpec((tm, tn), lambda i, j: (i, j)), pl.BlockSpec((tm, LANE), lambda i, j: (i, 0)),
                   pl.BlockSpec((d, tm), lambda i, j: (0, i))] + [any_spec] * nl,
        out_shape=[_sds((s, n)), _sds((s, LANE)), _sds((d, s), BF16)]
        + [_sds((N_DEV,) + a.shape, a.dtype) for a in late],
        scratch_shapes=[pltpu.VMEM((tm, d), BF16)] + _Gather.sems(nl),
        compiler_params=_params(("arbitrary", "arbitrary")),
    )(x, ln_g, w_main, w_ab, *late)
    return res[0], res[1], res[2], res[3:]


def _matmul_acc(a, b, name):
    m, k = a.shape
    n = b.shape[1]
    tm = _tile(m, (2048, 1024, 512, 256, 128))
    tn = _tile(n, (1024, 512, 256, 128))
    tk = _tile(k, (1024, 512, 256, 128))
    nk = k // tk

    def body(a_ref, b_ref, o_ref, acc):
        @pl.when(pl.program_id(2) == 0)
        def _():
            acc[...] = jnp.zeros_like(acc)

        acc[...] += jnp.dot(a_ref[...], b_ref[...], preferred_element_type=F32)

        @pl.when(pl.program_id(2) == nk - 1)
        def _():
            o_ref[...] = acc[...].astype(BF16)

    return pl.pallas_call(
        body, name=name, grid=(m // tm, n // tn, nk),
        in_specs=[pl.BlockSpec((tm, tk), lambda i, j, l: (i, l)), pl.BlockSpec((tk, tn), lambda i, j, l: (l, j))],
        out_specs=pl.BlockSpec((tm, tn), lambda i, j, l: (i, j)),
        out_shape=_sds((m, n), BF16), scratch_shapes=[pltpu.VMEM((tm, tn), F32)],
        compiler_params=_params(("parallel", "parallel", "arbitrary")),
    )(a, b)


def _matmul_tn(a, b, name):
    k, m = a.shape
    n = b.shape[1]
    tm = _tile(m, (1024, 512, 256, 128))
    tn = _tile(n, (1024, 512, 256, 128))
    tk = _tile(k, (1024, 512, 256, 128))
    nk = k // tk

    def body(a_ref, b_ref, o_ref, acc):
        @pl.when(pl.program_id(2) == 0)
        def _():
            acc[...] = jnp.zeros_like(acc)

        acc[...] += _mm_tn(a_ref[...], b_ref[...])

        @pl.when(pl.program_id(2) == nk - 1)
        def _():
            o_ref[...] = acc[...].astype(BF16)

    return pl.pallas_call(
        body, name=name, grid=(m // tm, n // tn, nk),
        in_specs=[pl.BlockSpec((tk, tm), lambda i, j, l: (l, i)), pl.BlockSpec((tk, tn), lambda i, j, l: (l, j))],
        out_specs=pl.BlockSpec((tm, tn), lambda i, j, l: (i, j)),
        out_shape=_sds((m, n), BF16), scratch_shapes=[pltpu.VMEM((tm, tn), F32)],
        compiler_params=_params(("parallel", "parallel", "arbitrary")),
    )(a, b)


def _dh_rms(pieces, w_rows, wab_rows, x, dx2, ln_g, chip_sums):
    s, d = x.shape
    npc = len(pieces)
    nx = len(chip_sums)
    tm = _tile(s, (256, 128))
    ni = s // tm
    widths = [p.shape[1] for p in pieces[:-1]]
    offs = [sum(widths[:p]) for p in range(npc - 1)]
    nw = len(w_rows)
    nin = npc + 2 * nw + 3

    def body(*refs):
        p_refs = refs[:npc]
        w_refs = refs[npc:npc + nw]
        wab_refs = refs[npc + nw:npc + 2 * nw]
        x_ref, dx2_ref, g_ref = refs[npc + 2 * nw:nin]
        gx_ref, dg_ref = refs[nin + nx:nin + nx + 2]
        exch = _ChipExchange(refs[nin:nin + nx], refs[nin + nx + 2:nin + 2 * nx + 2], *refs[nin + 2 * nx + 2:])
        step = pl.program_id(0)

        @pl.when(step == 0)
        def _():
            dg_ref[...] = jnp.zeros_like(dg_ref)
            exch.start()

        cols = []
        for w_ref, wab_ref in zip(w_refs, wab_refs):
            part = _mm_nt(p_refs[npc - 1][...], wab_ref[...])
            for p in range(npc - 1):
                part += _mm_nt(p_refs[p][...], w_ref[:, offs[p]:offs[p] + widths[p]])
            cols.append(part)
        dhv = jnp.concatenate(cols, axis=1)
        xv = x_ref[...]
        r = lax.rsqrt(jnp.mean(xv * xv, axis=-1, keepdims=True) + EPS)
        xhat = xv * r
        dg_ref[...] += _colsum(dhv * xhat)
        dxh = dhv * g_ref[...]
        gx_ref[...] = dx2_ref[...] + r * (dxh - xhat * jnp.mean(dxh * xhat, axis=-1, keepdims=True))

        @pl.when(step == ni - 1)
        def _():
            exch.finish()

    any_spec = pl.BlockSpec(memory_space=pl.ANY)
    row = pl.BlockSpec((tm, d), lambda i: (i, 0))
    vec = pl.BlockSpec((1, d), lambda i: (0, 0))
    once = lambda a: pl.BlockSpec(a.shape, lambda i: (0, 0), pipeline_mode=pl.Buffered(1))
    in_specs = [pl.BlockSpec((tm, p.shape[1]), lambda i: (i, 0)) for p in pieces]
    in_specs += [once(w) for w in w_rows] + [once(w) for w in wab_rows] + [row, row, vec] + [any_spec] * nx
    res = pl.pallas_call(
        body, name="dh_rms", grid=(ni,), in_specs=in_specs,
        out_specs=[row, vec] + [any_spec] * nx,
        out_shape=[_sds((s, d)), _sds((1, d))] + [_sds(p.shape, p.dtype) for p in chip_sums],
        scratch_shapes=_ChipExchange.sems(nx),
        compiler_params=_params(("arbitrary",)),
    )(*pieces, *w_rows, *wab_rows, x, dx2, ln_g, *chip_sums)
    return res[0], res[1], res[2:]


def _final(x, tgt, out_b, out_a, out_c, w_out, final_g):
    s, d = x.shape
    tm = _tile(s, (256, 128))

    def body(x_ref, t_ref, b_ref, a_ref, c_ref, w_ref, g_ref, dx2_ref, dx2b_ref, dm_ref, loss_ref, dg_ref):
        @pl.when(pl.program_id(0) == 0)
        def _():
            loss_ref[...] = jnp.zeros_like(loss_ref)
            dg_ref[...] = jnp.zeros_like(dg_ref)

        x2 = x_ref[...]
        x2 += jnp.dot(b_ref[...], w_ref[0:DN_W, :], preferred_element_type=F32)
        x2 += jnp.dot(a_ref[...], w_ref[DN_W:DN_W + GMLP_W, :], preferred_element_type=F32)
        x2 += jnp.dot(c_ref[...], w_ref[DN_W + GMLP_W:MIX_W, :], preferred_element_type=F32)
        r = lax.rsqrt(jnp.mean(x2 * x2, axis=-1, keepdims=True) + EPS)
        xhat = x2 * r
        g = g_ref[...]
        err = xhat * g - t_ref[...]
        tok = 0.5 * jnp.mean(err * err, axis=-1, keepdims=True)
        loss_ref[...] += jnp.broadcast_to(_colsum(tok), loss_ref.shape)
        dy = err * (1.0 / d)
        dg_ref[...] += _colsum(dy * xhat)
        dxh = dy * g
        dx2 = r * (dxh - xhat * jnp.mean(dxh * xhat, axis=-1, keepdims=True))
        dx2_ref[...] = dx2
        dx2b = dx2.astype(BF16)
        dx2b_ref[...] = dx2b
        dm_ref[...] = _mm_nt(dx2b, w_ref[...])

    row = pl.BlockSpec((tm, d), lambda i: (i, 0))
    vec = pl.BlockSpec((1, d), lambda i: (0, 0))
    return pl.pallas_call(
        body, name="final", grid=(s // tm,),
        in_specs=[row, row, pl.BlockSpec((tm, DN_W), lambda i: (i, 0)), pl.BlockSpec((tm, GMLP_W), lambda i: (i, 0)),
                  pl.BlockSpec((tm, XA_W), lambda i: (i, 0)), pl.BlockSpec((MIX_W, d), lambda i: (0, 0)), vec],
        out_specs=[row, row, pl.BlockSpec((tm, MIX_W), lambda i: (i, 0)), pl.BlockSpec((1, LANE), lambda i: (0, 0)), vec],
        out_shape=[_sds((s, d)), _sds((s, d), BF16), _sds((s, MIX_W)), _sds((1, LANE)), _sds((1, d))],
        compiler_params=_params(("arbitrary",)),
    )(x, tgt, out_b, out_a, out_c, w_out, final_g)


GU_BLK = (4 * DN_W) // GMLP_W


def _gmlp_norm(gv, lng, lnb):
    va = _gelu(gv)
    mu = jnp.mean(va, axis=-1, keepdims=True)
    xc = va - mu
    rstd = lax.rsqrt(jnp.mean(xc * xc, axis=-1, keepdims=True) + EPS)
    vhat = xc * rstd
    return vhat, rstd, vhat * lng + lnb


def _gmlp_fwd(proj, lng, lnb, ws, bs_t):
    s = proj.shape[0]
    tm = _tile(s, (512, 256, 128))

    def body(u_ref, v_ref, z_ref, lng_ref, lnb_ref, ws_ref, bst_ref, o_ref):
        _, _, vn = _gmlp_norm(v_ref[...], lng_ref[...], lnb_ref[...])
        tri = _iota2((GMLP_T, GMLP_T), 0) >= _iota2((GMLP_T, GMLP_T), 1)
        for g in range(GMLP_G):
            cs = slice(g * HEAD, (g + 1) * HEAD)
            w = jnp.where(tri, ws_ref[g], 0.0).astype(BF16)
            b = bst_ref[:, g:g + 1]
            for c in range(tm // GMLP_T):
                rs = slice(c * GMLP_T, (c + 1) * GMLP_T)
                sg = _mm(w, vn[rs, cs]) + b
                o_ref[rs, cs] = (_gelu(u_ref[rs, cs]) * sg * _silu(z_ref[rs, cs])).astype(BF16)

    col = lambda k: pl.BlockSpec((tm, GMLP_W), lambda i: (i, GU_BLK + k))
    vec = pl.BlockSpec((1, GMLP_W), lambda i: (0, 0))
    return pl.pallas_call(
        body, name="gmlp_fwd", grid=(s // tm,),
        in_specs=[col(0), col(1), col(2), vec, vec, pl.BlockSpec((GMLP_G, GMLP_T, GMLP_T), lambda i: (0, 0, 0)),
                  pl.BlockSpec((GMLP_T, GMLP_G), lambda i: (0, 0))],
        out_specs=pl.BlockSpec((tm, GMLP_W), lambda i: (i, 0)), out_shape=_sds((s, GMLP_W), BF16),
        compiler_params=_params(("parallel",)),
    )(proj, proj, proj, lng, lnb, ws, bs_t)


def _gmlp_bwd(proj, dmixed, lng, lnb, ws, bs_t):
    s = proj.shape[0]
    tm = _tile(s, (512, 256, 128))

    def body(u_ref, v_ref, z_ref, d_ref, lng_ref, lnb_ref, ws_ref, bst_ref,
             dp_ref, dws_ref, dbst_ref, dlng_ref, dlnb_ref, dvn):
        @pl.when(pl.program_id(0) == 0)
        def _():
            dws_ref[...] = jnp.zeros_like(dws_ref)
            dbst_ref[...] = jnp.zeros_like(dbst_ref)
            dlng_ref[...] = jnp.zeros_like(dlng_ref)
            dlnb_ref[...] = jnp.zeros_like(dlnb_ref)

        gv = v_ref[...]
        lng_v = lng_ref[...]
        vhat, rstd, vn = _gmlp_norm(gv, lng_v, lnb_ref[...])
        tri = _iota2((GMLP_T, GMLP_T), 0) >= _iota2((GMLP_T, GMLP_T), 1)
        for g in range(GMLP_G):
            cs = slice(g * HEAD, (g + 1) * HEAD)
            w = jnp.where(tri, ws_ref[g], 0.0).astype(BF16)
            b = bst_ref[:, g:g + 1]
            dw_acc = jnp.zeros((GMLP_T, GMLP_T), F32)
            db_acc = jnp.zeros((GMLP_T, 1), F32)
            for c in range(tm // GMLP_T):
                rs = slice(c * GMLP_T, (c + 1) * GMLP_T)
                vn_b = vn[rs, cs]
                sg = _mm(w, vn_b) + b
                gu = u_ref[rs, cs]
                gz = z_ref[rs, cs]
                da = d_ref[rs, cs]
                uact = _gelu(gu)
                sz = _silu(gz)
                ds = da * uact * sz
                dp_ref[rs, cs] = (da * sg * sz * _gelu_grad(gu)).astype(BF16)
                dp_ref[rs, 2 * GMLP_W + g * HEAD:2 * GMLP_W + (g + 1) * HEAD] = (da * uact * sg * _silu_grad(gz)).astype(BF16)
                dw_acc += _mm_nt(ds, vn_b)
                db_acc += _rowsum(ds)
                dvn[rs, cs] = _mm_tn(w, ds)
            dws_ref[g] += jnp.where(tri, dw_acc, 0.0)
            dbst_ref[:, g:g + 1] += db_acc
        dvn_v = dvn[...]
        dlng_ref[...] += _colsum(dvn_v * vhat)
        dlnb_ref[...] += _colsum(dvn_v)
        dvh = dvn_v * lng_v
        dva = rstd * (dvh - jnp.mean(dvh, axis=-1, keepdims=True) - vhat * jnp.mean(dvh * vhat, axis=-1, keepdims=True))
        dp_ref[:, GMLP_W:2 * GMLP_W] = (dva * _gelu_grad(gv)).astype(BF16)

    col = lambda k: pl.BlockSpec((tm, GMLP_W), lambda i: (i, GU_BLK + k))
    vec = pl.BlockSpec((1, GMLP_W), lambda i: (0, 0))
    wsp = pl.BlockSpec((GMLP_G, GMLP_T, GMLP_T), lambda i: (0, 0, 0))
    bsp = pl.BlockSpec((GMLP_T, GMLP_G), lambda i: (0, 0))
    return pl.pallas_call(
        body, name="gmlp_bwd", grid=(s // tm,),
        in_specs=[col(0), col(1), col(2), pl.BlockSpec((tm, GMLP_W), lambda i: (i, DN_W // GMLP_W)), vec, vec, wsp, bsp],
        out_specs=[pl.BlockSpec((tm, 3 * GMLP_W), lambda i: (i, 0)), wsp, bsp, vec, vec],
        out_shape=[_sds((s, 3 * GMLP_W), BF16), _sds((GMLP_G, GMLP_T, GMLP_T)), _sds((GMLP_T, GMLP_G)),
                   _sds((1, GMLP_W)), _sds((1, GMLP_W))],
        scratch_shapes=[pltpu.VMEM((tm, GMLP_W), F32)],
        compiler_params=_params(("arbitrary",)),
    )(proj, proj, proj, dmixed, lng, lnb, ws, bs_t)


CQ_BLK = (4 * DN_W + 3 * GMLP_W) // XA_W


def _memkv_fwd(mem, g, w_kv):
    nm, d = mem.shape

    def body(m_ref, g_ref, w_ref, kv_ref):
        mv = m_ref[...]
        r = lax.rsqrt(jnp.mean(mv * mv, axis=-1, keepdims=True) + EPS)
        kv_ref[...] = _mm(mv * r * g_ref[...], w_ref[...])

    return pl.pallas_call(body, name="memkv_fwd", out_shape=_sds((nm, 2 * XA_W)), compiler_params=_params())(mem, g, w_kv)


def _memkv_bwd(mem, g, w_kv, dkv):
    nm, d = mem.shape

    def body(m_ref, g_ref, w_ref, dkv_ref, dw_ref, dg_ref):
        mv = m_ref[...]
        r = lax.rsqrt(jnp.mean(mv * mv, axis=-1, keepdims=True) + EPS)
        xhat = mv * r
        dkv_v = dkv_ref[...]
        dw_ref[...] = _mm_tn(xhat * g_ref[...], dkv_v)
        dg_ref[...] = _colsum(_mm_nt(dkv_v, w_ref[...]) * xhat)

    return pl.pallas_call(body, name="memkv_bwd", out_shape=[_sds((d, 2 * XA_W)), _sds((1, d))],
                          compiler_params=_params())(mem, g, w_kv, dkv)


def _xattn_probs(q, mk):
    sc = _mm_nt(q, mk) * (HEAD ** -0.5)
    e = jnp.exp(sc - jnp.max(sc, axis=-1, keepdims=True))
    return e / _rowsum(e)


def _xattn_fwd(proj, mkv):
    s = proj.shape[0]
    nm = mkv.shape[0]
    tm = _tile(s, (512, 256, 128))

    def body(q_ref, z_ref, kv_ref, o_ref):
        for h in range(XA_H):
            cs = slice(h * HEAD, (h + 1) * HEAD)
            p = _xattn_probs(q_ref[:, cs], kv_ref[:, cs])
            ctx = _mm(p, kv_ref[:, XA_W + h * HEAD:XA_W + (h + 1) * HEAD])
            o_ref[:, cs] = (ctx * _silu(z_ref[:, cs])).astype(BF16)

    col = lambda k: pl.BlockSpec((tm, XA_W), lambda i: (i, CQ_BLK + k))
    return pl.pallas_call(
        body, name="xattn_fwd", grid=(s // tm,),
        in_specs=[col(0), col(1), pl.BlockSpec((nm, 2 * XA_W), lambda i: (0, 0))],
        out_specs=pl.BlockSpec((tm, XA_W), lambda i: (i, 0)), out_shape=_sds((s, XA_W), BF16),
        compiler_params=_params(("parallel",)),
    )(proj, proj, mkv)


def _xattn_bwd(proj, dmixed, mkv):
    s = proj.shape[0]
    nm = mkv.shape[0]
    tm = _tile(s, (512, 256, 128))

    def body(q_ref, z_ref, d_ref, kv_ref, dp_ref, dkv_ref):
        @pl.when(pl.program_id(0) == 0)
        def _():
            dkv_ref[...] = jnp.zeros_like(dkv_ref)

        for h in range(XA_H):
            cs = slice(h * HEAD, (h + 1) * HEAD)
            vs = slice(XA_W + h * HEAD, XA_W + (h + 1) * HEAD)
            q = q_ref[:, cs]
            z = z_ref[:, cs]
            mk = kv_ref[:, cs]
            mv = kv_ref[:, vs]
            p = _xattn_probs(q, mk)
            ctx = _mm(p, mv)
            dc = d_ref[:, cs]
            dctx = dc * _silu(z)
            dp_ref[:, vs] = (dc * ctx * _silu_grad(z)).astype(BF16)
            dp = _mm_nt(dctx, mv)
            dkv_ref[:, vs] += _mm_tn(p, dctx)
            ds = p * (dp - _rowsum(dp * p)) * (HEAD ** -0.5)
            dp_ref[:, cs] = _mm(ds, mk).astype(BF16)
            dkv_ref[:, cs] += _mm_tn(ds, q)

    col = lambda k: pl.BlockSpec((tm, XA_W), lambda i: (i, CQ_BLK + k))
    kvs = pl.BlockSpec((nm, 2 * XA_W), lambda i: (0, 0))
    return pl.pallas_call(
        body, name="xattn_bwd", grid=(s // tm,),
        in_specs=[col(0), col(1), pl.BlockSpec((tm, XA_W), lambda i: (i, (DN_W + GMLP_W) // XA_W)), kvs],
        out_specs=[pl.BlockSpec((tm, 2 * XA_W), lambda i: (i, 0)), kvs],
        out_shape=[_sds((s, 2 * XA_W), BF16), _sds((nm, 2 * XA_W))],
        compiler_params=_params(("arbitrary",)),
    )(proj, proj, dmixed, mkv)


def _softplus(x):
    return jnp.maximum(x, 0.0) + jnp.log1p(jnp.exp(-jnp.abs(x)))


def _dn_pre(proj, ab, conv_w, alog_row, dt_row):
    s = proj.shape[0]
    tm = _tile(s, (256, 128))
    w3 = 3 * DN_W

    def body(x_ref, halo_ref, ab_ref, cw_ref, al_ref, dt_ref, q_ref, k_ref, v_ref, gb_ref, gbt_ref, yc_ref):
        i = pl.program_id(0)
        xv = x_ref[...]
        cat = jnp.concatenate([jnp.where(i > 0, halo_ref[...], 0.0), xv[0:HALO]], axis=0)
        yc = cw_ref[DN_K - 1:DN_K, :] * xv
        top = cw_ref[DN_K - 1:DN_K, :] * xv[0:HALO]
        for t in range(DN_K - 1):
            back = DN_K - 1 - t
            yc += cw_ref[t:t + 1, :] * pltpu.roll(xv, back, 0)
            top += cw_ref[t:t + 1, :] * pltpu.roll(cat, back, 0)[HALO:2 * HALO]
        yc = jnp.concatenate([top, yc[HALO:tm]], axis=0)
        yc_ref[...] = yc
        act = _silu(yc)
        for h in range(DN_H):
            cs = slice(h * HEAD, (h + 1) * HEAD)
            qa = act[:, cs]
            q_ref[:, cs] = (qa * (lax.rsqrt(_rowsum(qa * qa) + EPS) * (HEAD ** -0.5))).astype(BF16)
            ka = act[:, DN_W + h * HEAD:DN_W + (h + 1) * HEAD]
            k_ref[:, cs] = (ka * lax.rsqrt(_rowsum(ka * ka) + EPS)).astype(BF16)
        v_ref[...] = act[:, 2 * DN_W:w3].astype(BF16)
        abv = ab_ref[...]
        lane = _iota2((tm, LANE), 1)
        g = jnp.where(lane < DN_H, -jnp.exp(al_ref[...]) * _softplus(abv + dt_ref[...]), 0.0)
        gc = _mm_hi(_chunk_tri(tm, False), g)
        gbv = jnp.where(lane < DN_H, gc, jnp.where(lane < 2 * DN_H, jax.nn.sigmoid(abv), 0.0))
        gb_ref[...] = gbv
        for c in range(tm // CH):
            gbt_ref[c] = gbv[c * CH:(c + 1) * CH, :].T[0:2 * DN_H, :]

    hb = tm // HALO
    row = lambda w: pl.BlockSpec((tm, w), lambda i: (i, 0))
    vec = pl.BlockSpec((1, LANE), lambda i: (0, 0))
    return pl.pallas_call(
        body, name="dn_pre", grid=(s // tm,),
        in_specs=[row(w3), pl.BlockSpec((HALO, w3), lambda i: (jnp.maximum(i * hb - 1, 0), 0)), row(LANE),
                  pl.BlockSpec((DN_K, w3), lambda i: (0, 0)), vec, vec],
        out_specs=[row(DN_W), row(DN_W), row(DN_W), row(LANE), pl.BlockSpec((tm // CH, 2 * DN_H, CH), lambda i: (i, 0, 0)),
                   row(w3)],
        out_shape=[_sds((s, DN_W), BF16), _sds((s, DN_W), BF16), _sds((s, DN_W), BF16), _sds((s, LANE)),
                   _sds((s // CH, 2 * DN_H, CH)), _sds((s, w3))],
        compiler_params=_params(("parallel",)),
    )(proj, proj, ab, conv_w, alog_row, dt_row)


HEADS = tuple(range(DN_H))


def _hcols(h):
    return slice(h * HEAD, (h + 1) * HEAD)


def _chunk_scalings(k, v, gbv, gbt, h):
    gc = jnp.broadcast_to(gbv[:, h:h + 1], (CH, HEAD))
    beta = jnp.broadcast_to(gbv[:, DN_H + h:DN_H + h + 1], (CH, HEAD))
    gr = gbt[h:h + 1, :]
    ii = _iota2((CH, CH), 0)
    jj = _iota2((CH, CH), 1)
    dec = jnp.exp(jnp.where(ii >= jj, gc[:, 0:CH] - gr, -1e30))
    eg = jnp.exp(gc)
    gl = gr[:, CH - 1:CH]
    kb = k * beta
    return dict(beta=beta, dec=dec, eg=eg, gl=gl, ekd=jnp.exp(gl - gc), kb=kb, vb=v * beta, kbe=kb * eg)


def _chunk_scores(m, q, k):
    kq = _mm_nt(jnp.concatenate([m["kb"], q], axis=0), k)
    strict = _iota2((CH, CH), 0) > _iota2((CH, CH), 1)
    return jnp.where(strict, kq[0:CH] * m["dec"], 0.0), kq[CH:2 * CH] * m["dec"]


def _scan_cpb(s):
    return 8 if (s // CH) % 8 == 0 else 1


def _dn_fwd(q, k, v, gb, gbt, proj, norm_g):
    s = q.shape[0]
    cpb = _scan_cpb(s)
    tb = cpb * CH
    nblk = s // tb

    def body(q_ref, k_ref, v_ref, gb_ref, gbt_ref, z_ref, ng_ref,
             w_ref, qg_ref, kd_ref, t_ref, ai_ref, egl_ref, o_ref, vn_ref, st_ref, ob_ref, state):
        @pl.when(pl.program_id(0) == 0)
        def _():
            state[...] = jnp.zeros_like(state)

        ng = ng_ref[...]
        eye = jnp.where(_iota2((CH, CH), 0) == _iota2((CH, CH), 1), 1.0, 0.0).astype(F32)

        def chunk(c, carry):
            r0 = pl.multiple_of(c * CH, CH)
            rows = pl.ds(r0, CH)
            gbv = gb_ref[rows, :]
            gbt_v = gbt_ref[c]
            qs = [q_ref[rows, _hcols(h)].astype(F32) for h in HEADS]
            ks = [k_ref[rows, _hcols(h)].astype(F32) for h in HEADS]
            ms = [_chunk_scalings(ks[h], v_ref[rows, _hcols(h)].astype(F32), gbv, gbt_v, h) for h in HEADS]
            qgb = [(qs[h] * ms[h]["eg"]).astype(BF16) for h in HEADS]
            kdb = [(ks[h] * ms[h]["ekd"]).astype(BF16) for h in HEADS]
            egl = [jnp.broadcast_to(jnp.exp(ms[h]["gl"]), (1, LANE)) for h in HEADS]
            for h in HEADS:
                qg_ref[rows, _hcols(h)] = qgb[h]
                kd_ref[rows, _hcols(h)] = kdb[h]
                egl_ref[c, h:h + 1, :] = egl[h]
            sc = [_chunk_scores(ms[h], qs[h], ks[h]) for h in HEADS]
            for h in HEADS:
                ai_ref[h, rows, :] = sc[h][1]
            ts = [eye - sc[h][0] for h in HEADS]
            ps = [_mm_3x(sc[h][0], sc[h][0]) for h in HEADS]
            ts = [ts[h] + _mm_3x(ts[h], ps[h]) for h in HEADS]
            for _ in range(4):
                ps = [_mm(ps[h], ps[h]) for h in HEADS]
                ts = [ts[h] + _mm(ts[h], ps[h]) for h in HEADS]
            uw = [_mm(ts[h], jnp.concatenate([ms[h]["vb"], ms[h]["kbe"]], axis=1)) for h in HEADS]
            wb = [uw[h][:, HEAD:2 * HEAD].astype(BF16) for h in HEADS]
            for h in HEADS:
                t_ref[h, rows, :] = ts[h]
                w_ref[rows, _hcols(h)] = wb[h]
            sts = [state[h] for h in HEADS]
            stb = [sts[h].astype(BF16) for h in HEADS]
            for h in HEADS:
                st_ref[c, h] = stb[h]
            vnb = [(uw[h][:, 0:HEAD] - jnp.dot(wb[h], stb[h], preferred_element_type=F32)).astype(BF16) for h in HEADS]
            for h in HEADS:
                state[h] = sts[h] * egl[h] + _mm_tn(kdb[h], vnb[h])
            os_ = [jnp.dot(qgb[h], stb[h], preferred_element_type=F32) + _mm(sc[h][1], vnb[h]) for h in HEADS]
            for h in HEADS:
                o = os_[h]
                vn_ref[rows, _hcols(h)] = vnb[h]
                o_ref[rows, _hcols(h)] = o
                r = lax.rsqrt(jnp.mean(o * o, axis=-1, keepdims=True) + EPS)
                ob_ref[rows, _hcols(h)] = (o * r * ng * _silu(z_ref[rows, _hcols(h)])).astype(BF16)
            return carry

        lax.fori_loop(0, cpb, chunk, 0, unroll=4)

    row = pl.BlockSpec((tb, DN_W), lambda i: (i, 0))
    sq = pl.BlockSpec((DN_H, tb, CH), lambda i: (0, i, 0))
    return pl.pallas_call(
        body, name="dn_fwd", grid=(nblk,),
        in_specs=[row, row, row, pl.BlockSpec((tb, LANE), lambda i: (i, 0)),
                  pl.BlockSpec((cpb, 2 * DN_H, CH), lambda i: (i, 0, 0)), pl.BlockSpec((tb, DN_W), lambda i: (i, 3)),
                  pl.BlockSpec((1, HEAD), lambda i: (0, 0))],
        out_specs=[row, row, row, sq, sq, pl.BlockSpec((cpb, DN_H, LANE), lambda i: (i, 0, 0)), row, row,
                   pl.BlockSpec((cpb, DN_H, HEAD, HEAD), lambda i: (i, 0, 0, 0)), row],
        out_shape=[_sds((s, DN_W), BF16), _sds((s, DN_W), BF16), _sds((s, DN_W), BF16), _sds((DN_H, s, CH)),
                   _sds((DN_H, s, CH)), _sds((s // CH, DN_H, LANE)), _sds((s, DN_W)), _sds((s, DN_W), BF16),
                   _sds((s // CH, DN_H, HEAD, HEAD), BF16), _sds((s, DN_W), BF16)],
        scratch_shapes=[pltpu.VMEM((DN_H, HEAD, HEAD), F32)],
        compiler_params=_params(("arbitrary",)),
    )(q, k, v, gb, gbt, proj, norm_g)


def _dn_bwd(dmixed, o, proj, norm_g, w, qg, kd, ai, egl, q, k, v, gb, gbt, t, vn, st):
    s = o.shape[0]
    cpb = 4 if (s // CH) % 4 == 0 else 1
    tb = cpb * CH
    nblk = s // tb

    def body(dm_ref, o_ref, z_ref, ng_ref, w_ref, qg_ref, kd_ref, ai_ref, egl_ref,
             q_ref, k_ref, v_ref, gb_ref, gbt_ref, t_ref, vn_ref, st_ref,
             dq_ref, dk_ref, dv_ref, dgb_ref, dz_ref, dng_ref, dstate):
        @pl.when(pl.program_id(0) == 0)
        def _():
            dstate[...] = jnp.zeros_like(dstate)
            dng_ref[...] = jnp.zeros_like(dng_ref)

        ng = ng_ref[...]
        lane = _iota2((CH, LANE), 1)
        last = _iota2((CH, 1), 0) == CH - 1
        strict = _iota2((CH, CH), 0) > _iota2((CH, CH), 1)

        def chunk(cc, carry):
            c = cpb - 1 - cc
            r0 = pl.multiple_of(c * CH, CH)
            rows = pl.ds(r0, CH)
            dng = jnp.zeros((1, HEAD), F32)
            dob = []
            for h in HEADS:
                cs = _hcols(h)
                ov = o_ref[rows, cs]
                z = z_ref[rows, cs]
                db = dm_ref[rows, cs]
                r = lax.rsqrt(jnp.mean(ov * ov, axis=-1, keepdims=True) + EPS)
                ohat = ov * r
                dz_ref[rows, cs] = (db * ohat * ng * _silu_grad(z)).astype(BF16)
                dyn = db * _silu(z)
                dng += _colsum(dyn * ohat)
                doh = dyn * ng
                dob.append((r * (doh - ohat * jnp.mean(doh * ohat, axis=-1, keepdims=True))).astype(BF16))
            dng_ref[...] += dng
            dsn = [dstate[h] for h in HEADS]
            dsb = [dsn[h].astype(BF16) for h in HEADS]
            dvnb = [(_mm_tn(ai_ref[h, rows, :], dob[h])
                     + jnp.dot(kd_ref[rows, _hcols(h)], dsb[h], preferred_element_type=F32)).astype(BF16) for h in HEADS]
            part = [_mm_tn(qg_ref[rows, _hcols(h)], dob[h]) + egl_ref[c, h:h + 1, :] * dsn[h] for h in HEADS]
            for h in HEADS:
                dstate[h] = part[h] - _mm_tn(w_ref[rows, _hcols(h)], dvnb[h])
            gbv = gb_ref[rows, :]
            gbt_v = gbt_ref[c]
            qs = [q_ref[rows, _hcols(h)].astype(F32) for h in HEADS]
            ks = [k_ref[rows, _hcols(h)].astype(F32) for h in HEADS]
            vs = [v_ref[rows, _hcols(h)].astype(F32) for h in HEADS]
            ms = [_chunk_scalings(ks[h], vs[h], gbv, gbt_v, h) for h in HEADS]
            sts = [st_ref[c, h] for h in HEADS]
            vnb = [vn_ref[rows, _hcols(h)] for h in HEADS]
            tbf = [t_ref[h, rows, :].astype(BF16) for h in HEADS]
            sc = [_chunk_scores(ms[h], qs[h], ks[h]) for h in HEADS]
            xs_ = [_mm_nt(jnp.concatenate([dob[h], dvnb[h]], axis=0), sts[h]) for h in HEADS]
            dai = [_mm_nt(dob[h], vnb[h]) for h in HEADS]
            dkd = [_mm_nt(vnb[h], dsb[h]) for h in HEADS]
            dqg = [xs_[h][0:CH] for h in HEADS]
            duw = [jnp.concatenate([dvnb[h], (-xs_[h][CH:2 * CH]).astype(BF16)], axis=1) for h in HEADS]
            dt = [_mm_nt(duw[h], jnp.concatenate([ms[h]["vb"], ms[h]["kbe"]], axis=1)) for h in HEADS]
            dvk = [_mm_tn(tbf[h], duw[h]) for h in HEADS]
            tdt = [_mm_tn(tbf[h], dt[h]) for h in HEADS]
            da = [jnp.where(strict, -_mm_nt(tdt[h], tbf[h]), 0.0) for h in HEADS]
            dsc = [jnp.concatenate([da[h] * ms[h]["dec"], dai[h] * ms[h]["dec"]], axis=0) for h in HEADS]
            dkq = [_mm(dsc[h], ks[h]) for h in HEADS]
            dk1 = [_mm_tn(dsc[h], jnp.concatenate([ms[h]["kb"], qs[h]], axis=0)) for h in HEADS]
            dgb = jnp.zeros((CH, LANE), F32)
            for h in HEADS:
                m = ms[h]
                eg, ekd, beta = m["eg"], m["ekd"], m["beta"]
                dvb = dvk[h][:, 0:HEAD]
                dkbe = dvk[h][:, HEAD:2 * HEAD]
                kdv = ks[h] * ekd
                dkb = dkq[h][0:CH] + dkbe * eg
                dq_ref[rows, _hcols(h)] = dkq[h][CH:2 * CH] + dqg[h] * eg
                dk_ref[rows, _hcols(h)] = dk1[h] + dkd[h] * ekd + dkb * beta
                dv_ref[rows, _hcols(h)] = dvb * beta
                dkd_kd = dkd[h] * kdv
                dgl = (jnp.exp(m["gl"]) * _rowsum(_colsum(sts[h].astype(F32) * dsb[h].astype(F32)))
                       + _rowsum(_colsum(dkd_kd)))
                mm_ = da[h] * sc[h][0] + dai[h] * sc[h][1]
                dgc = (_rowsum(mm_ - mm_.T) + _rowsum(dqg[h] * qs[h] * eg - dkd_kd + dkbe * m["kbe"])
                       + jnp.where(last, dgl, 0.0))
                dbeta = _rowsum(dkb * ks[h] + dvb * vs[h])
                dgb = jnp.where(lane == h, dgc, jnp.where(lane == DN_H + h, dbeta, dgb))
            dgb_ref[rows, :] = dgb
            return carry

        lax.fori_loop(0, cpb, chunk, 0, unroll=2)

    rev = lambda i: (nblk - 1 - i, 0)
    row = pl.BlockSpec((tb, DN_W), rev)
    vec = pl.BlockSpec((1, HEAD), lambda i: (0, 0))
    sq = pl.BlockSpec((DN_H, tb, CH), lambda i: (0, nblk - 1 - i, 0))
    gbs = pl.BlockSpec((tb, LANE), rev)
    return pl.pallas_call(
        body, name="dn_bwd", grid=(nblk,),
        in_specs=[row, row, pl.BlockSpec((tb, DN_W), lambda i: (nblk - 1 - i, 3)), vec, row, row, row, sq,
                  pl.BlockSpec((cpb, DN_H, LANE), lambda i: (nblk - 1 - i, 0, 0)),
                  row, row, row, gbs, pl.BlockSpec((cpb, 2 * DN_H, CH), lambda i: (nblk - 1 - i, 0, 0)), sq, row,
                  pl.BlockSpec((cpb, DN_H, HEAD, HEAD), lambda i: (nblk - 1 - i, 0, 0, 0))],
        out_specs=[row, row, row, gbs, row, vec],
        out_shape=[_sds((s, DN_W)), _sds((s, DN_W)), _sds((s, DN_W)), _sds((s, LANE)), _sds((s, DN_W), BF16),
                   _sds((1, HEAD))],
        scratch_shapes=[pltpu.VMEM((DN_H, HEAD, HEAD), F32)],
        compiler_params=_params(("arbitrary",)),
    )(dmixed, o, proj, norm_g, w, qg, kd, ai, egl, q, k, v, gb, gbt, t, vn, st)


def _dn_pre_bwd(proj, yc_all, ab, conv_w, alog_row, dt_row, dq, dk, dv, dgb):
    s = proj.shape[0]
    tm = _tile(s, (256, 128))
    w3 = 3 * DN_W
    nblk = s // tm

    def body(x_ref, yc_ref, ab_ref, cw_ref, al_ref, dt_ref, dq_ref, dk_ref, dv_ref, dgb_ref,
             dx_ref, dab_ref, dcw_ref, dal_ref, ddt_ref, exd, carry):
        i = pl.program_id(0)

        @pl.when(i == 0)
        def _():
            carry[...] = jnp.zeros_like(carry)
            dcw_ref[...] = jnp.zeros_like(dcw_ref)
            dal_ref[...] = jnp.zeros_like(dal_ref)
            ddt_ref[...] = jnp.zeros_like(ddt_ref)

        yc = yc_ref[...]
        sg = jax.nn.sigmoid(yc)
        act = yc * sg
        dact = sg * (1.0 + yc * (1.0 - sg))
        for h in range(DN_H):
            cs = slice(h * HEAD, (h + 1) * HEAD)
            ks = slice(DN_W + h * HEAD, DN_W + (h + 1) * HEAD)
            qa = act[:, cs]
            rq = lax.rsqrt(_rowsum(qa * qa) + EPS)
            qh = qa * rq
            dqv = dq_ref[:, cs]
            exd[0:tm, cs] = (HEAD ** -0.5) * rq * (dqv - qh * _rowsum(dqv * qh)) * dact[:, cs]
            ka = act[:, ks]
            rk = lax.rsqrt(_rowsum(ka * ka) + EPS)
            kh = ka * rk
            dkv = dk_ref[:, cs]
            exd[0:tm, ks] = rk * (dkv - kh * _rowsum(dkv * kh)) * dact[:, ks]
        exd[0:tm, 2 * DN_W:w3] = dv_ref[...] * dact[:, 2 * DN_W:w3]
        xv = x_ref[...]
        dyc = exd[...]
        cat = jnp.concatenate([dyc[tm - HALO:tm], carry[...]], axis=0)
        dcw_ref[DN_K - 1:DN_K, :] += _colsum(dyc * xv)
        dx = cw_ref[DN_K - 1:DN_K, :] * dyc
        for t in range(DN_K - 1):
            ahead = DN_K - 1 - t
            view = jnp.concatenate([pltpu.roll(dyc, tm - ahead, 0)[0:tm - HALO],
                                    pltpu.roll(cat, 2 * HALO - ahead, 0)[0:HALO]], axis=0)
            dcw_ref[t:t + 1, :] += _colsum(view * xv)
            dx += cw_ref[t:t + 1, :] * view
        dx_ref[...] = dx.astype(BF16)
        carry[...] = dyc[0:HALO]

        lane = _iota2((tm, LANE), 1)
        dgbv = dgb_ref[...]
        dg = _mm_hi(_chunk_tri(tm, True), jnp.where(lane < DN_H, dgbv, 0.0))
        abv = ab_ref[...]
        xa = abv + dt_ref[...]
        nea = -jnp.exp(al_ref[...])
        d_da = jnp.where(lane < DN_H, dg * nea * jax.nn.sigmoid(xa), 0.0)
        dal_ref[...] += _colsum(jnp.where(lane < DN_H, dg * nea * _softplus(xa), 0.0))
        ddt_ref[...] += _colsum(d_da)
        beta = jax.nn.sigmoid(abv)
        d_db = jnp.where((lane >= DN_H) & (lane < 2 * DN_H), dgbv * beta * (1.0 - beta), 0.0)
        dab_ref[...] = (d_da + d_db).astype(BF16)

    rev = lambda i: (nblk - 1 - i, 0)
    row = lambda w: pl.BlockSpec((tm, w), rev)
    vec = pl.BlockSpec((1, LANE), lambda i: (0, 0))
    cws = pl.BlockSpec((DN_K, w3), lambda i: (0, 0))
    return pl.pallas_call(
        body, name="dn_pre_bwd", grid=(nblk,),
        in_specs=[row(w3), row(w3), row(LANE), cws, vec, vec, row(DN_W), row(DN_W), row(DN_W), row(LANE)],
        out_specs=[row(w3), row(LANE), cws, vec, vec],
        out_shape=[_sds((s, w3), BF16), _sds((s, LANE), BF16), _sds((DN_K, w3)), _sds((1, LANE)), _sds((1, LANE))],
        scratch_shapes=[pltpu.VMEM((tm, w3), F32), pltpu.VMEM((HALO, w3), F32)],
        compiler_params=_params(("arbitrary",)),
    )(proj, yc_all, ab, conv_w, alog_row, dt_row, dq, dk, dv, dgb)


def _adam(parts, w, m, v, name):
    r, c = w.shape
    n_parts = parts.shape[0]
    small = n_parts * r * c * 4 <= 4 * 1024 * 1024
    tr = r if small else _tile(r, (128, 64, 32, 16, 8))

    def body(p_ref, w_ref, m_ref, v_ref, g_ref, d_ref, nm_ref, nv_ref):
        g = p_ref[0].astype(F32)
        for k in range(1, n_parts):
            g = g + p_ref[k].astype(F32)
        g_ref[...] = g
        mn = ADAM_B1 * m_ref[...] + (1.0 - ADAM_B1) * g
        vn = ADAM_B2 * v_ref[...] + (1.0 - ADAM_B2) * (g * g)
        m_hat = mn / (1.0 - ADAM_B1 ** ADAM_STEP)
        v_hat = vn / (1.0 - ADAM_B2 ** ADAM_STEP)
        d_ref[...] = -ADAM_LR * (m_hat / (jnp.sqrt(v_hat) + ADAM_EPS) + ADAM_WD * w_ref[...])
        nm_ref[...] = mn
        nv_ref[...] = vn

    blk = pl.BlockSpec((tr, c), lambda i: (i, 0))
    return pl.pallas_call(
        body, name=name, grid=(r // tr,),
        in_specs=[pl.BlockSpec((n_parts, tr, c), lambda i: (0, i, 0)), blk, blk, blk],
        out_specs=[blk, blk, blk, blk], out_shape=[_sds((r, c))] * 4,
        compiler_params=_params(("parallel",)),
    )(parts, w, m, v)


_PACK_ROWS = 8


def _pack(vals):
    tiles = []
    for a in vals:
        flat = a.reshape(-1).astype(F32)
        unit = _PACK_ROWS * LANE
        n = -(-flat.shape[0] // unit) * unit
        tiles.append(jnp.pad(flat, (0, n - flat.shape[0])).reshape(n // LANE, LANE))
    return jnp.concatenate(tiles, axis=0)


def _unpack(packed, shapes):
    out = []
    r0 = 0
    for shp in shapes:
        size = 1
        for dim in shp:
            size *= dim
        unit = _PACK_ROWS * LANE
        rows = -(-size // unit) * _PACK_ROWS
        out.append(packed[r0:r0 + rows].reshape(-1)[:size].reshape(shp))
        r0 += rows
    return out


def _lane_row(vec8):
    return jnp.pad(vec8.reshape(1, -1).astype(F32), ((0, 0), (0, LANE - vec8.size)))


def kernel(x, mem, ln_g, w_in, gmlp_ln_g, gmlp_ln_b, gmlp_ws, gmlp_bs, conv_w, dn_a_log, dn_dt_bias, dn_norm_g, mem_norm_g, w_mem_kv, w_out, final_g, loss_target, m_ln_g, m_w_in, m_gmlp_ln_g, m_gmlp_ln_b, m_gmlp_ws, m_gmlp_bs, m_conv_w, m_dn_a_log, m_dn_dt_bias, m_dn_norm_g, m_mem_norm_g, m_w_mem_kv, m_w_out, m_final_g, v_ln_g, v_w_in, v_gmlp_ln_g, v_gmlp_ln_b, v_gmlp_ws, v_gmlp_bs, v_conv_w, v_dn_a_log, v_dn_dt_bias, v_dn_norm_g, v_mem_norm_g, v_w_mem_kv, v_w_out, v_final_g):
    xs = x[0]
    mems = mem[0]
    tgt = loss_target[0]
    s, d = xs.shape
    shard_w = w_in.shape[2]
    in_w = N_DEV * shard_w
    me = 4 * lax.axis_index("x") + 2 * lax.axis_index("y") + lax.axis_index("c")

    (g_in,) = _gather_two_level([w_in[0].astype(BF16)], "gather_w_in")
    o_g, o_dn, o_ab = 0, 3 * GMLP_W, 3 * GMLP_W + 4 * DN_W
    o_xa = o_ab + 2 * DN_H

    def shard_cols(g, lo, hi):
        out = []
        while lo < hi:
            sh = lo // shard_w
            end = min(hi, (sh + 1) * shard_w)
            out.append(g[sh][:, lo - sh * shard_w:end - sh * shard_w])
            lo = end
        return out

    def own_layout(g):
        main = jnp.concatenate(shard_cols(g, o_dn, o_ab) + shard_cols(g, o_g, o_dn) + shard_cols(g, o_xa, in_w), axis=1)
        return main, jnp.pad(jnp.concatenate(shard_cols(g, o_ab, o_xa), axis=1), ((0, 0), (0, LANE - 2 * DN_H)))

    w_main, w_ab = own_layout(g_in)

    ln_g2 = ln_g.reshape(1, d)
    lng2 = gmlp_ln_g.reshape(1, GMLP_W)
    lnb2 = gmlp_ln_b.reshape(1, GMLP_W)
    ws3 = gmlp_ws[0]
    bs_t = gmlp_bs[0].T
    alog_row = _lane_row(dn_a_log)
    dt_row = _lane_row(dn_dt_bias)
    dn_g2 = dn_norm_g.reshape(1, HEAD)
    mem_g2 = mem_norm_g.reshape(1, d)
    fin_g2 = final_g.reshape(1, d)

    proj, ab, h_t, (g_out, g_kv, g_conv) = _inproj(
        xs, ln_g2, w_main, w_ab, [w_out[0].astype(BF16), w_mem_kv[0].astype(BF16), conv_w[0]])
    wo = g_out.reshape(MIX_W, d)
    wo_perm = jnp.concatenate([wo[GMLP_W:GMLP_W + DN_W], wo[0:GMLP_W], wo[GMLP_W + DN_W:MIX_W]], axis=0)
    w_kv = g_kv.reshape(d, 2 * XA_W)
    conv_full = g_conv.transpose(1, 0, 2).reshape(DN_K, 3 * DN_W)
    out_a = _gmlp_fwd(proj, lng2, lnb2, ws3, bs_t)
    mkv = _memkv_fwd(mems, mem_g2, w_kv)
    out_c = _xattn_fwd(proj, mkv)
    q, k, v, gb, gbt, yc = _dn_pre(proj, ab, conv_full, alog_row, dt_row)
    wk, qg, kd, tmat, ai, egl, o, vn, st, out_b = _dn_fwd(q, k, v, gb, gbt, proj, dn_g2)

    dx2, dx2b, dmixed, loss_acc, d_fin_g = _final(xs, tgt, out_b, out_a, out_c, wo_perm, fin_g2)

    dwo_b = _matmul_tn(out_b, dx2b, "dw_out_b")
    dwo_a = _matmul_tn(out_a, dx2b, "dw_out_a")
    dwo_c = _matmul_tn(out_c, dx2b, "dw_out_c")
    d_w_out = jnp.concatenate([dwo_a, dwo_b, dwo_c], axis=0)

    dp_g, d_ws, d_bst, d_lng, d_lnb = _gmlp_bwd(proj, dmixed, lng2, lnb2, ws3, bs_t)
    dp_x, dmkv = _xattn_bwd(proj, dmixed, mkv)
    d_w_kv, d_mem_g = _memkv_bwd(mems, mem_g2, w_kv, dmkv)
    dq, dk, dv, dgb, dp_dz, d_dn_g = _dn_bwd(dmixed, o, proj, dn_g2, wk, qg, kd, ai, egl, q, k, v, gb, gbt, tmat, vn, st)
    dp_qkv, dp_ab, d_conv, d_alog, d_dt = _dn_pre_bwd(proj, yc, ab, conv_full, alog_row, dt_row, dq, dk, dv, dgb)

    dw_qkv = _matmul_acc(h_t, dp_qkv, "dw_in_qkv")
    dw_dz = _matmul_acc(h_t, dp_dz, "dw_in_dz")
    dw_gm = _matmul_acc(h_t, dp_g, "dw_in_gmlp")
    dw_xa = _matmul_acc(h_t, dp_x, "dw_in_xa")
    dw_ab = _matmul_acc(h_t, dp_ab, "dw_in_ab")
    segs = [(o_g, dw_gm), (o_dn, dw_qkv), (o_dn + 3 * DN_W, dw_dz), (o_ab, dw_ab[:, :2 * DN_H]), (o_xa, dw_xa)]
    shards = []
    for sh in range(N_DEV):
        lo, hi = sh * shard_w, (sh + 1) * shard_w
        parts = [arr[:, max(lo, off) - off:min(hi, off + arr.shape[1]) - off] for off, arr in segs
                 if off < hi and off + arr.shape[1] > lo]
        shards.append(jnp.concatenate(parts, axis=1).astype(BF16))
    send_in = jnp.stack(shards)

    small_shapes = [(1, 1), gmlp_ln_g.shape, gmlp_ln_b.shape, gmlp_ws.shape, gmlp_bs.shape, dn_a_log.shape,
                    dn_dt_bias.shape, dn_norm_g.shape, mem_norm_g.shape, final_g.shape, (DN_K, 3 * DN_W)]
    small_g = _pack([loss_acc[0:1, 0:1], d_lng, d_lnb, d_ws, d_bst.T, d_alog[:, :DN_H], d_dt[:, :DN_H], d_dn_g, d_mem_g,
                     d_fin_g, d_conv])
    zc = jnp.zeros((DN_K, 3 * DN_W), F32)
    z1 = jnp.zeros((1, 1), F32)
    small_w = _pack([z1, gmlp_ln_g, gmlp_ln_b, gmlp_ws, gmlp_bs, dn_a_log, dn_dt_bias, dn_norm_g, mem_norm_g, final_g, zc])
    small_m = _pack([z1, m_gmlp_ln_g, m_gmlp_ln_b, m_gmlp_ws, m_gmlp_bs, m_dn_a_log, m_dn_dt_bias, m_dn_norm_g,
                     m_mem_norm_g, m_final_g, zc])
    small_v = _pack([z1 + 1.0, v_gmlp_ln_g, v_gmlp_ln_b, v_gmlp_ws, v_gmlp_bs, v_dn_a_log, v_dn_dt_bias, v_dn_norm_g,
                     v_mem_norm_g, v_final_g, zc + 1.0])

    send_out = d_w_out.reshape(N_DEV, MIX_W // N_DEV, d).astype(BF16)
    send_kv = d_w_kv.reshape(N_DEV, d // N_DEV, 2 * XA_W).astype(BF16)
    sends = [send_in, send_out, send_kv]
    all_small, got = _swap_halves(small_g, sends, "swap_halves")
    core = lax.axis_index("c").astype(jnp.int32).reshape(1)
    chip_sums = [_pair_sum(core, sends[i], got[i], "pair_sum_%d" % i) for i in range(3)]
    grad_x, d_ln_g, (r_in, r_out, r_kv) = _dh_rms(
        [dp_qkv, dp_dz, dp_g, dp_x, dp_ab], [w_main], [w_ab], xs, dx2, ln_g2, chip_sums)
    (all_ln_g,) = _gather_two_level([_pack([d_ln_g])], "gather_ln_g")

    g_w_in, dl_w_in, nm_w_in, nv_w_in = _adam(r_in, w_in[0], m_w_in[0], v_w_in[0], "adam_w_in")
    g_w_out, dl_w_out, nm_w_out, nv_w_out = _adam(r_out, w_out[0], m_w_out[0], v_w_out[0], "adam_w_out")
    g_w_kv, dl_w_kv, nm_w_kv, nv_w_kv = _adam(r_kv, w_mem_kv[0], m_w_mem_kv[0], v_w_mem_kv[0], "adam_w_kv")
    sm = [_unpack(t, small_shapes) for t in _adam(all_small, small_w, small_m, small_v, "adam_small")]
    ln_res = [_unpack(t, [ln_g.shape])[0]
              for t in _adam(all_ln_g, _pack([ln_g]), _pack([m_ln_g]), _pack([v_ln_g]), "adam_ln_g")]

    conv_parts = lax.dynamic_slice(all_small, (0, all_small.shape[1] - (DN_K * 3 * DN_W) // LANE, 0),
                                   (N_DEV, (DN_K * 3 * DN_W) // LANE, LANE)).reshape(N_DEV, DN_K, 3 * DN_W)
    cshard = conv_w.shape[2]
    conv_parts = lax.dynamic_slice(conv_parts, (0, 0, me * cshard), (N_DEV, DN_K, cshard))
    cpad = ((0, 0), (0, HALO - DN_K), (0, 0))
    conv_res = _adam(jnp.pad(conv_parts, cpad), jnp.pad(conv_w[0], cpad[1:]), jnp.pad(m_conv_w[0], cpad[1:]),
                     jnp.pad(v_conv_w[0], cpad[1:], constant_values=1.0), "adam_conv")
    g_conv_s, dl_conv, nm_conv, nv_conv = [t[:DN_K][None] for t in conv_res]

    loss = sm[0][0].reshape(())

    def group(idx, big_in, big_conv, big_kv, big_out):
        names = sm[idx][1:]
        return [ln_res[idx], big_in[None], names[0], names[1], names[2], names[3], big_conv, names[4], names[5], names[6],
                names[7], big_kv[None], big_out[None], names[8]]

    grads = group(0, g_w_in, g_conv_s, g_w_kv, g_w_out)
    deltas = group(1, dl_w_in, dl_conv, dl_w_kv, dl_w_out)
    new_m = group(2, nm_w_in, nm_conv, nm_w_kv, nm_w_out)
    new_v = group(3, nv_w_in, nv_conv, nv_w_kv, nv_w_out)
    return (loss, grad_x[None], *grads, *deltas, *new_m, *new_v)
```

```python
import functools

import jax
import jax.numpy as jnp
from jax import lax
from jax.experimental import pallas as pl
from jax.experimental.pallas import tpu as pltpu

F32 = jnp.float32
BF16 = jnp.bfloat16
HIGHEST = lax.Precision.HIGHEST
MESH_ID = pl.DeviceIdType.MESH

N_DEV = 8
EPS = 1e-6
GMLP_W = 512
GMLP_G = 4
GMLP_T = 128
DN_W = 1024
DN_H = 8
HEAD = 128
DN_K = 4
CH = 64
XA_W = 512
XA_H = 4
LANE = 128
HALO = 8
MAIN_W = 4 * DN_W + 3 * GMLP_W + 2 * XA_W
MIX_W = DN_W + GMLP_W + XA_W
VMEM_LIMIT = 56 * 1024 * 1024

ADAM_LR = 0.001
ADAM_B1 = 0.9
ADAM_B2 = 0.999
ADAM_EPS = 1e-08
ADAM_WD = 0.01
ADAM_STEP = 10


def _sds(shape, dtype=F32):
    return jax.ShapeDtypeStruct(tuple(shape), dtype)


def _params(sem=None):
    if sem is None:
        return pltpu.CompilerParams(vmem_limit_bytes=VMEM_LIMIT)
    return pltpu.CompilerParams(dimension_semantics=tuple(sem), vmem_limit_bytes=VMEM_LIMIT)


def _tile(n, prefs):
    for p in prefs:
        if n % p == 0:
            return p
    return n


def _mm(a, b):
    return jnp.dot(a.astype(BF16), b.astype(BF16), preferred_element_type=F32)


def _mm_nt(a, b):
    return lax.dot_general(a.astype(BF16), b.astype(BF16), (((1,), (1,)), ((), ())), preferred_element_type=F32)


def _mm_tn(a, b):
    return lax.dot_general(a.astype(BF16), b.astype(BF16), (((0,), (0,)), ((), ())), preferred_element_type=F32)


def _mm_hi(a, b):
    return jnp.dot(a, b, precision=HIGHEST, preferred_element_type=F32)


def _mm_3x(a, b):
    return jnp.dot(a, b, precision=lax.Precision.HIGH, preferred_element_type=F32)


_GELU_C = 0.7978845608028654
_GELU_A = 0.044715


def _gelu(x):
    return 0.5 * x * (1.0 + jnp.tanh(_GELU_C * (x + _GELU_A * x * x * x)))


def _gelu_grad(x):
    t = jnp.tanh(_GELU_C * (x + _GELU_A * x * x * x))
    return 0.5 * (1.0 + t) + 0.5 * x * (1.0 - t * t) * _GELU_C * (1.0 + 3.0 * _GELU_A * x * x)


def _silu(x):
    return x * jax.nn.sigmoid(x)


def _silu_grad(x):
    s = jax.nn.sigmoid(x)
    return s * (1.0 + x * (1.0 - s))


def _rowsum(x):
    return jnp.sum(x, axis=-1, keepdims=True)


def _colsum(x):
    return jnp.sum(x, axis=0, keepdims=True)


def _iota2(shape, dim):
    return lax.broadcasted_iota(jnp.int32, shape, dim)


def _chunk_tri(tm, upper):
    r = _iota2((tm, tm), 0)
    c = _iota2((tm, tm), 1)
    same = lax.shift_right_logical(r, 6) == lax.shift_right_logical(c, 6)
    tri = (r <= c) if upper else (r >= c)
    return jnp.where(same & tri, 1.0, 0.0).astype(F32)


N_CHIP = 4


def _mesh_place():
    x, y, c = lax.axis_index("x"), lax.axis_index("y"), lax.axis_index("c")
    chips = [(1 - x, y), (x, 1 - y), (1 - x, 1 - y)]
    return x, y, c, (x, y, 1 - c), chips


class _Gather:
    def __init__(self, ins, outs, send_sems, recv_sems, loc_sems):
        self.ins, self.outs, self.send_sems, self.recv_sems, self.loc_sems = ins, outs, send_sems, recv_sems, loc_sems
        self.x, self.y, self.c, self.sib, self.chips = _mesh_place()
        self.me = (self.x, self.y, self.c)

    def copy(self, a, k, block, to, src=None):
        slot = self.outs[a].at[4 * block[0] + 2 * block[1] + block[2]]
        return pltpu.make_async_remote_copy(
            src_ref=slot if src is None else src, dst_ref=slot, send_sem=self.send_sems.at[a, k],
            recv_sem=self.recv_sems.at[a, k], device_id=to, device_id_type=MESH_ID)

    def own(self, a):
        return pltpu.make_async_copy(self.ins[a], self.outs[a].at[4 * self.x + 2 * self.y + self.c], self.loc_sems.at[a])

    def first(self, a):
        return [self.copy(a, 0, self.me, self.sib, src=self.ins[a])] + [
            self.copy(a, 1 + j, self.me, (*chip, self.c), src=self.ins[a]) for j, chip in enumerate(self.chips)]

    def passed(self, a, j):
        return self.copy(a, 4 + j, (*self.chips[j], self.c), self.sib)

    def start(self):
        for a in range(len(self.ins)):
            self.own(a).start()
            for cp in self.first(a):
                cp.start()

    def finish(self):
        n = len(self.ins)
        for a in range(n):
            for j, chip in enumerate(self.chips):
                self.copy(a, 1 + j, (*chip, self.c), self.me).wait_recv()
                self.passed(a, j).start()
        for a in range(n):
            self.copy(a, 0, self.sib, self.me).wait_recv()
            for j, chip in enumerate(self.chips):
                self.copy(a, 4 + j, (*chip, 1 - self.c), self.me).wait_recv()
        for a in range(n):
            for cp in self.first(a) + [self.passed(a, j) for j in range(N_CHIP - 1)]:
                cp.wait_send()
            self.own(a).wait()

    @staticmethod
    def sems(n):
        return [pltpu.SemaphoreType.DMA((n, N_DEV - 1)), pltpu.SemaphoreType.DMA((n, N_DEV - 1)),
                pltpu.SemaphoreType.DMA((n,))]


def _gather_two_level(arrs, name):
    n = len(arrs)

    def body(*refs):
        g = _Gather(refs[:n], refs[n:2 * n], *refs[2 * n:])
        g.start()
        g.finish()

    any_spec = pl.BlockSpec(memory_space=pl.ANY)
    return pl.pallas_call(
        body, name=name, out_shape=[_sds((N_DEV,) + a.shape, a.dtype) for a in arrs],
        in_specs=[any_spec] * n, out_specs=[any_spec] * n, scratch_shapes=_Gather.sems(n),
        compiler_params=pltpu.CompilerParams(has_side_effects=True),
    )(*arrs)


def _swap_halves(small, grads, name):
    n = len(grads)

    def body(*refs):
        small_ref = refs[0]
        ins = refs[1:1 + n]
        small_out = refs[1 + n]
        got = refs[2 + n:2 + 2 * n]
        s_send, s_recv, g_send, g_recv, loc_sem = refs[2 + 2 * n:]
        x, y, c, sib, _ = _mesh_place()
        me = 4 * x + 2 * y + c
        sends, recvs = [], []
        for j in range(1, N_DEV):
            px = 1 - x if (j >> 2) & 1 else x
            py = 1 - y if (j >> 1) & 1 else y
            pc = 1 - c if j & 1 else c
            cp = pltpu.make_async_remote_copy(
                src_ref=small_ref, dst_ref=small_out.at[me], send_sem=s_send.at[j - 1], recv_sem=s_recv.at[j - 1],
                device_id=(px, py, pc), device_id_type=MESH_ID)
            cp.start()
            sends.append(cp)
            recvs.append(pltpu.make_async_remote_copy(
                src_ref=small_ref, dst_ref=small_out.at[4 * px + 2 * py + pc], send_sem=s_send.at[j - 1],
                recv_sem=s_recv.at[j - 1], device_id=(px, py, pc), device_id_type=MESH_ID))
        own = pltpu.make_async_copy(small_ref, small_out.at[me], loc_sem)
        own.start()
        for a in range(n):
            for chip in range(N_CHIP):
                cp = pltpu.make_async_remote_copy(
                    src_ref=ins[a].at[2 * chip + 1 - c], dst_ref=got[a].at[chip], send_sem=g_send.at[a, chip],
                    recv_sem=g_recv.at[a, chip], device_id=sib, device_id_type=MESH_ID)
                cp.start()
                sends.append(cp)
                recvs.append(cp)
        for cp in sends:
            cp.wait_send()
        for cp in recvs:
            cp.wait_recv()
        own.wait()

    half = [_sds((N_CHIP,) + g.shape[1:], g.dtype) for g in grads]
    any_spec = pl.BlockSpec(memory_space=pl.ANY)
    res = pl.pallas_call(
        body, name=name, out_shape=[_sds((N_DEV,) + small.shape, small.dtype)] + half,
        in_specs=[any_spec] * (1 + n), out_specs=[any_spec] * (1 + n),
        scratch_shapes=[pltpu.SemaphoreType.DMA((N_DEV - 1,)), pltpu.SemaphoreType.DMA((N_DEV - 1,)),
                        pltpu.SemaphoreType.DMA((n, N_CHIP)), pltpu.SemaphoreType.DMA((n, N_CHIP)),
                        pltpu.SemaphoreType.DMA],
        compiler_params=pltpu.CompilerParams(has_side_effects=True),
    )(small, *grads)
    return res[0], res[1:]


def _pair_sums(core, mine, got):
    n = len(got)

    def body(core_ref, *refs):
        for a in range(n):
            refs[2 * n + a][...] = (refs[a][...].astype(F32) + refs[n + a][...].astype(F32)).astype(BF16)

    half = lambda g: (1, g.shape[1] // 2, g.shape[2])
    own = [pl.BlockSpec(half(g), lambda i, j, core_ref: (2 * i + core_ref[0], j, 0)) for g in got]
    slot = [pl.BlockSpec(half(g), lambda i, j, core_ref: (i, j, 0)) for g in got]
    return pl.pallas_call(
        body, name="pair_sums", out_shape=[_sds(g.shape, BF16) for g in got],
        grid_spec=pltpu.PrefetchScalarGridSpec(
            num_scalar_prefetch=1, grid=(N_CHIP, 2), in_specs=own + slot, out_specs=slot),
        compiler_params=_params(("parallel", "parallel")),
    )(core, *mine, *got)


class _ChipExchange:
    def __init__(self, ins, outs, send_sems, recv_sems, loc_sems):
        self.ins, self.outs, self.send_sems, self.recv_sems, self.loc_sems = ins, outs, send_sems, recv_sems, loc_sems
        self.x, self.y, self.c, _, self.chips = _mesh_place()
        self.mine = 2 * self.x + self.y

    def own(self, a):
        return pltpu.make_async_copy(self.ins[a].at[self.mine], self.outs[a].at[self.mine], self.loc_sems.at[a])

    def copy(self, a, j, lands_in):
        chip = self.chips[j]
        return pltpu.make_async_remote_copy(
            src_ref=self.ins[a].at[2 * chip[0] + chip[1]], dst_ref=self.outs[a].at[lands_in],
            send_sem=self.send_sems.at[a, j], recv_sem=self.recv_sems.at[a, j], device_id=(*chip, self.c),
            device_id_type=MESH_ID)

    def start(self):
        for a in range(len(self.ins)):
            self.own(a).start()
            for j in range(N_CHIP - 1):
                self.copy(a, j, self.mine).start()

    def finish(self):
        for a in range(len(self.ins)):
            for j, chip in enumerate(self.chips):
                self.copy(a, j, self.mine).wait_send()
                self.copy(a, j, 2 * chip[0] + chip[1]).wait_recv()
            self.own(a).wait()

    @staticmethod
    def sems(n):
        return [pltpu.SemaphoreType.DMA((n, N_CHIP - 1)), pltpu.SemaphoreType.DMA((n, N_CHIP - 1)),
                pltpu.SemaphoreType.DMA((n,))]


def _inproj(x, ln_g, w_main, w_ab, late):
    s, d = x.shape
    n = w_main.shape[1]
    tm = _tile(s, (512, 256, 128))
    tn = _tile(n, (1664, 512, 128))
    nl = len(late)
    ni, nj = s // tm, n // tn

    def body(*refs):
        x_ref, g_ref, w_ref, wab_ref = refs[:4]
        proj_ref, ab_ref, ht_ref = refs[4 + nl:7 + nl]
        hs = refs[7 + 2 * nl]
        gather = _Gather(refs[4:4 + nl], refs[7 + nl:7 + 2 * nl], *refs[8 + 2 * nl:])
        step = pl.program_id(0) * nj + pl.program_id(1)

        @pl.when(step == 0)
        def _():
            gather.start()

        @pl.when(pl.program_id(1) == 0)
        def _():
            xv = x_ref[...]
            r = lax.rsqrt(jnp.mean(xv * xv, axis=-1, keepdims=True) + EPS)
            hf = xv * r * g_ref[...]
            h = hf.astype(BF16)
            hs[...] = h
            ht_ref[...] = hf.T.astype(BF16)
            ab_ref[...] = jnp.dot(h, wab_ref[...], preferred_element_type=F32)

        proj_ref[...] = jnp.dot(hs[...], w_ref[...], preferred_element_type=F32)

        @pl.when(step == ni * nj - 1)
        def _():
            gather.finish()

    any_spec = pl.BlockSpec(memory_space=pl.ANY)
    res = pl.pallas_call(
        body, name="inproj", grid=(ni, nj),
        in_specs=[pl.BlockSpec((tm, d), lambda i, j: (i, 0)), pl.BlockSpec((1, d), lambda i, j: (0, 0)),
                  pl.BlockSpec((d, tn), lambda i, j: (0, j)), pl.BlockSpec((d, LANE), lambda i, j: (0, 0))]
        + [any_spec] * nl,
        out_specs=[pl.BlockSpec((tm, tn), lambda i, j: (i, j)), pl.BlockSpec((tm, LANE), lambda i, j: (i, 0)),
                   pl.BlockSpec((d, tm), lambda i, j: (0, i))] + [any_spec] * nl,
        out_shape=[_sds((s, n)), _sds((s, LANE)), _sds((d, s), BF16)]
        + [_sds((N_DEV,) + a.shape, a.dtype) for a in late],
        scratch_shapes=[pltpu.VMEM((tm, d), BF16)] + _Gather.sems(nl),
        compiler_params=_params(("arbitrary", "arbitrary")),
    )(x, ln_g, w_main, w_ab, *late)
    return res[0], res[1], res[2], res[3:]


def _matmul_acc(a, b, name):
    m, k = a.shape
    n = b.shape[1]
    tm = _tile(m, (2048, 1024, 512, 256, 128))
    tn = _tile(n, (1024, 512, 256, 128))
    tk = _tile(k, (1024, 512, 256, 128))
    nk = k // tk

    def body(a_ref, b_ref, o_ref, acc):
        @pl.when(pl.program_id(2) == 0)
        def _():
            acc[...] = jnp.zeros_like(acc)

        acc[...] += jnp.dot(a_ref[...], b_ref[...], preferred_element_type=F32)

        @pl.when(pl.program_id(2) == nk - 1)
        def _():
            o_ref[...] = acc[...].astype(BF16)

    return pl.pallas_call(
        body, name=name, grid=(m // tm, n // tn, nk),
        in_specs=[pl.BlockSpec((tm, tk), lambda i, j, l: (i, l)), pl.BlockSpec((tk, tn), lambda i, j, l: (l, j))],
        out_specs=pl.BlockSpec((tm, tn), lambda i, j, l: (i, j)),
        out_shape=_sds((m, n), BF16), scratch_shapes=[pltpu.VMEM((tm, tn), F32)],
        compiler_params=_params(("parallel", "parallel", "arbitrary")),
    )(a, b)


def _matmul_tn(a, b, name):
    k, m = a.shape
    n = b.shape[1]
    tm = _tile(m, (1024, 512, 256, 128))
    tn = _tile(n, (1024, 512, 256, 128))
    tk = _tile(k, (1024, 512, 256, 128))
    nk = k // tk

    def body(a_ref, b_ref, o_ref, acc):
        @pl.when(pl.program_id(2) == 0)
        def _():
            acc[...] = jnp.zeros_like(acc)

        acc[...] += _mm_tn(a_ref[...], b_ref[...])

        @pl.when(pl.program_id(2) == nk - 1)
        def _():
            o_ref[...] = acc[...].astype(BF16)

    return pl.pallas_call(
        body, name=name, grid=(m // tm, n // tn, nk),
        in_specs=[pl.BlockSpec((tk, tm), lambda i, j, l: (l, i)), pl.BlockSpec((tk, tn), lambda i, j, l: (l, j))],
        out_specs=pl.BlockSpec((tm, tn), lambda i, j, l: (i, j)),
        out_shape=_sds((m, n), BF16), scratch_shapes=[pltpu.VMEM((tm, tn), F32)],
        compiler_params=_params(("parallel", "parallel", "arbitrary")),
    )(a, b)


def _dh_rms(pieces, w_rows, wab_rows, x, dx2, ln_g, chip_sums):
    s, d = x.shape
    npc = len(pieces)
    nx = len(chip_sums)
    tm = _tile(s, (256, 128))
    ni = s // tm
    widths = [p.shape[1] for p in pieces[:-1]]
    offs = [sum(widths[:p]) for p in range(npc - 1)]
    nw = len(w_rows)
    nin = npc + 2 * nw + 3

    def body(*refs):
        p_refs = refs[:npc]
        w_refs = refs[npc:npc + nw]
        wab_refs = refs[npc + nw:npc + 2 * nw]
        x_ref, dx2_ref, g_ref = refs[npc + 2 * nw:nin]
        gx_ref, dg_ref = refs[nin + nx:nin + nx + 2]
        exch = _ChipExchange(refs[nin:nin + nx], refs[nin + nx + 2:nin + 2 * nx + 2], *refs[nin + 2 * nx + 2:])
        step = pl.program_id(0)

        @pl.when(step == 0)
        def _():
            dg_ref[...] = jnp.zeros_like(dg_ref)
            exch.start()

        cols = []
        for w_ref, wab_ref in zip(w_refs, wab_refs):
            part = _mm_nt(p_refs[npc - 1][...], wab_ref[...])
            for p in range(npc - 1):
                part += _mm_nt(p_refs[p][...], w_ref[:, offs[p]:offs[p] + widths[p]])
            cols.append(part)
        dhv = jnp.concatenate(cols, axis=1)
        xv = x_ref[...]
        r = lax.rsqrt(jnp.mean(xv * xv, axis=-1, keepdims=True) + EPS)
        xhat = xv * r
        dg_ref[...] += _colsum(dhv * xhat)
        dxh = dhv * g_ref[...]
        gx_ref[...] = dx2_ref[...] + r * (dxh - xhat * jnp.mean(dxh * xhat, axis=-1, keepdims=True))

        @pl.when(step == ni - 1)
        def _():
            exch.finish()

    any_spec = pl.BlockSpec(memory_space=pl.ANY)
    row = pl.BlockSpec((tm, d), lambda i: (i, 0))
    vec = pl.BlockSpec((1, d), lambda i: (0, 0))
    once = lambda a: pl.BlockSpec(a.shape, lambda i: (0, 0), pipeline_mode=pl.Buffered(1))
    in_specs = [pl.BlockSpec((tm, p.shape[1]), lambda i: (i, 0)) for p in pieces]
    in_specs += [once(w) for w in w_rows] + [once(w) for w in wab_rows] + [row, row, vec] + [any_spec] * nx
    res = pl.pallas_call(
        body, name="dh_rms", grid=(ni,), in_specs=in_specs,
        out_specs=[row, vec] + [any_spec] * nx,
        out_shape=[_sds((s, d)), _sds((1, d))] + [_sds(p.shape, p.dtype) for p in chip_sums],
        scratch_shapes=_ChipExchange.sems(nx),
        compiler_params=_params(("arbitrary",)),
    )(*pieces, *w_rows, *wab_rows, x, dx2, ln_g, *chip_sums)
    return res[0], res[1], res[2:]


def _final(x, tgt, out_b, out_a, out_c, w_out, final_g):
    s, d = x.shape
    tm = _tile(s, (256, 128))

    def body(x_ref, t_ref, b_ref, a_ref, c_ref, w_ref, g_ref, dx2_ref, dx2b_ref, dm_ref, loss_ref, dg_ref):
        @pl.when(pl.program_id(0) == 0)
        def _():
            loss_ref[...] = jnp.zeros_like(loss_ref)
            dg_ref[...] = jnp.zeros_like(dg_ref)

        x2 = x_ref[...]
        x2 += jnp.dot(b_ref[...], w_ref[0:DN_W, :], preferred_element_type=F32)
        x2 += jnp.dot(a_ref[...], w_ref[DN_W:DN_W + GMLP_W, :], preferred_element_type=F32)
        x2 += jnp.dot(c_ref[...], w_ref[DN_W + GMLP_W:MIX_W, :], preferred_element_type=F32)
        r = lax.rsqrt(jnp.mean(x2 * x2, axis=-1, keepdims=True) + EPS)
        xhat = x2 * r
        g = g_ref[...]
        err = xhat * g - t_ref[...]
        tok = 0.5 * jnp.mean(err * err, axis=-1, keepdims=True)
        loss_ref[...] += jnp.broadcast_to(_colsum(tok), loss_ref.shape)
        dy = err * (1.0 / d)
        dg_ref[...] += _colsum(dy * xhat)
        dxh = dy * g
        dx2 = r * (dxh - xhat * jnp.mean(dxh * xhat, axis=-1, keepdims=True))
        dx2_ref[...] = dx2
        dx2b = dx2.astype(BF16)
        dx2b_ref[...] = dx2b
        dm_ref[...] = _mm_nt(dx2b, w_ref[...])

    row = pl.BlockSpec((tm, d), lambda i: (i, 0))
    vec = pl.BlockSpec((1, d), lambda i: (0, 0))
    return pl.pallas_call(
        body, name="final", grid=(s // tm,),
        in_specs=[row, row, pl.BlockSpec((tm, DN_W), lambda i: (i, 0)), pl.BlockSpec((tm, GMLP_W), lambda i: (i, 0)),
                  pl.BlockSpec((tm, XA_W), lambda i: (i, 0)), pl.BlockSpec((MIX_W, d), lambda i: (0, 0)), vec],
        out_specs=[row, row, pl.BlockSpec((tm, MIX_W), lambda i: (i, 0)), pl.BlockSpec((1, LANE), lambda i: (0, 0)), vec],
        out_shape=[_sds((s, d)), _sds((s, d), BF16), _sds((s, MIX_W)), _sds((1, LANE)), _sds((1, d))],
        compiler_params=_params(("arbitrary",)),
    )(x, tgt, out_b, out_a, out_c, w_out, final_g)


GU_BLK = (4 * DN_W) // GMLP_W


def _gmlp_norm(gv, lng, lnb):
    va = _gelu(gv)
    mu = jnp.mean(va, axis=-1, keepdims=True)
    xc = va - mu
    rstd = lax.rsqrt(jnp.mean(xc * xc, axis=-1, keepdims=True) + EPS)
    vhat = xc * rstd
    return vhat, rstd, vhat * lng + lnb


def _gmlp_fwd(proj, lng, lnb, ws, bs_t):
    s = proj.shape[0]
    tm = _tile(s, (512, 256, 128))

    def body(u_ref, v_ref, z_ref, lng_ref, lnb_ref, ws_ref, bst_ref, o_ref):
        _, _, vn = _gmlp_norm(v_ref[...], lng_ref[...], lnb_ref[...])
        tri = _iota2((GMLP_T, GMLP_T), 0) >= _iota2((GMLP_T, GMLP_T), 1)
        for g in range(GMLP_G):
            cs = slice(g * HEAD, (g + 1) * HEAD)
            w = jnp.where(tri, ws_ref[g], 0.0).astype(BF16)
            b = bst_ref[:, g:g + 1]
            for c in range(tm // GMLP_T):
                rs = slice(c * GMLP_T, (c + 1) * GMLP_T)
                sg = _mm(w, vn[rs, cs]) + b
                o_ref[rs, cs] = (_gelu(u_ref[rs, cs]) * sg * _silu(z_ref[rs, cs])).astype(BF16)

    col = lambda k: pl.BlockSpec((tm, GMLP_W), lambda i: (i, GU_BLK + k))
    vec = pl.BlockSpec((1, GMLP_W), lambda i: (0, 0))
    return pl.pallas_call(
        body, name="gmlp_fwd", grid=(s // tm,),
        in_specs=[col(0), col(1), col(2), vec, vec, pl.BlockSpec((GMLP_G, GMLP_T, GMLP_T), lambda i: (0, 0, 0)),
                  pl.BlockSpec((GMLP_T, GMLP_G), lambda i: (0, 0))],
        out_specs=pl.BlockSpec((tm, GMLP_W), lambda i: (i, 0)), out_shape=_sds((s, GMLP_W), BF16),
        compiler_params=_params(("parallel",)),
    )(proj, proj, proj, lng, lnb, ws, bs_t)


def _gmlp_bwd(proj, dmixed, lng, lnb, ws, bs_t):
    s = proj.shape[0]
    tm = _tile(s, (512, 256, 128))

    def body(u_ref, v_ref, z_ref, d_ref, lng_ref, lnb_ref, ws_ref, bst_ref,
             dp_ref, dws_ref, dbst_ref, dlng_ref, dlnb_ref, dvn):
        @pl.when(pl.program_id(0) == 0)
        def _():
            dws_ref[...] = jnp.zeros_like(dws_ref)
            dbst_ref[...] = jnp.zeros_like(dbst_ref)
            dlng_ref[...] = jnp.zeros_like(dlng_ref)
            dlnb_ref[...] = jnp.zeros_like(dlnb_ref)

        gv = v_ref[...]
        lng_v = lng_ref[...]
        vhat, rstd, vn = _gmlp_norm(gv, lng_v, lnb_ref[...])
        tri = _iota2((GMLP_T, GMLP_T), 0) >= _iota2((GMLP_T, GMLP_T), 1)
        for g in range(GMLP_G):
            cs = slice(g * HEAD, (g + 1) * HEAD)
            w = jnp.where(tri, ws_ref[g], 0.0).astype(BF16)
            b = bst_ref[:, g:g + 1]
            dw_acc = jnp.zeros((GMLP_T, GMLP_T), F32)
            db_acc = jnp.zeros((GMLP_T, 1), F32)
            for c in range(tm // GMLP_T):
                rs = slice(c * GMLP_T, (c + 1) * GMLP_T)
                vn_b = vn[rs, cs]
                sg = _mm(w, vn_b) + b
                gu = u_ref[rs, cs]
                gz = z_ref[rs, cs]
                da = d_ref[rs, cs]
                uact = _gelu(gu)
                sz = _silu(gz)
                ds = da * uact * sz
                dp_ref[rs, cs] = (da * sg * sz * _gelu_grad(gu)).astype(BF16)
                dp_ref[rs, 2 * GMLP_W + g * HEAD:2 * GMLP_W + (g + 1) * HEAD] = (da * uact * sg * _silu_grad(gz)).astype(BF16)
                dw_acc += _mm_nt(ds, vn_b)
                db_acc += _rowsum(ds)
                dvn[rs, cs] = _mm_tn(w, ds)
            dws_ref[g] += jnp.where(tri, dw_acc, 0.0)
            dbst_ref[:, g:g + 1] += db_acc
        dvn_v = dvn[...]
        dlng_ref[...] += _colsum(dvn_v * vhat)
        dlnb_ref[...] += _colsum(dvn_v)
        dvh = dvn_v * lng_v
        dva = rstd * (dvh - jnp.mean(dvh, axis=-1, keepdims=True) - vhat * jnp.mean(dvh * vhat, axis=-1, keepdims=True))
        dp_ref[:, GMLP_W:2 * GMLP_W] = (dva * _gelu_grad(gv)).astype(BF16)

    col = lambda k: pl.BlockSpec((tm, GMLP_W), lambda i: (i, GU_BLK + k))
    vec = pl.BlockSpec((1, GMLP_W), lambda i: (0, 0))
    wsp = pl.BlockSpec((GMLP_G, GMLP_T, GMLP_T), lambda i: (0, 0, 0))
    bsp = pl.BlockSpec((GMLP_T, GMLP_G), lambda i: (0, 0))
    return pl.pallas_call(
        body, name="gmlp_bwd", grid=(s // tm,),
        in_specs=[col(0), col(1), col(2), pl.BlockSpec((tm, GMLP_W), lambda i: (i, DN_W // GMLP_W)), vec, vec, wsp, bsp],
        out_specs=[pl.BlockSpec((tm, 3 * GMLP_W), lambda i: (i, 0)), wsp, bsp, vec, vec],
        out_shape=[_sds((s, 3 * GMLP_W), BF16), _sds((GMLP_G, GMLP_T, GMLP_T)), _sds((GMLP_T, GMLP_G)),
                   _sds((1, GMLP_W)), _sds((1, GMLP_W))],
        scratch_shapes=[pltpu.VMEM((tm, GMLP_W), F32)],
        compiler_params=_params(("arbitrary",)),
    )(proj, proj, proj, dmixed, lng, lnb, ws, bs_t)


CQ_BLK = (4 * DN_W + 3 * GMLP_W) // XA_W


def _memkv_fwd(mem, g, w_kv):
    nm, d = mem.shape

    def body(m_ref, g_ref, w_ref, kv_ref):
        mv = m_ref[...]
        r = lax.rsqrt(jnp.mean(mv * mv, axis=-1, keepdims=True) + EPS)
        kv_ref[...] = _mm(mv * r * g_ref[...], w_ref[...])

    return pl.pallas_call(body, name="memkv_fwd", out_shape=_sds((nm, 2 * XA_W)), compiler_params=_params())(mem, g, w_kv)


def _memkv_bwd(mem, g, w_kv, dkv):
    nm, d = mem.shape

    def body(m_ref, g_ref, w_ref, dkv_ref, dw_ref, dg_ref):
        mv = m_ref[...]
        r = lax.rsqrt(jnp.mean(mv * mv, axis=-1, keepdims=True) + EPS)
        xhat = mv * r
        dkv_v = dkv_ref[...]
        dw_ref[...] = _mm_tn(xhat * g_ref[...], dkv_v)
        dg_ref[...] = _colsum(_mm_nt(dkv_v, w_ref[...]) * xhat)

    return pl.pallas_call(body, name="memkv_bwd", out_shape=[_sds((d, 2 * XA_W)), _sds((1, d))],
                          compiler_params=_params())(mem, g, w_kv, dkv)


def _xattn_probs(q, mk):
    sc = _mm_nt(q, mk) * (HEAD ** -0.5)
    e = jnp.exp(sc - jnp.max(sc, axis=-1, keepdims=True))
    return e / _rowsum(e)


def _xattn_fwd(proj, mkv):
    s = proj.shape[0]
    nm = mkv.shape[0]
    tm = _tile(s, (512, 256, 128))

    def body(q_ref, z_ref, kv_ref, o_ref):
        for h in range(XA_H):
            cs = slice(h * HEAD, (h + 1) * HEAD)
            p = _xattn_probs(q_ref[:, cs], kv_ref[:, cs])
            ctx = _mm(p, kv_ref[:, XA_W + h * HEAD:XA_W + (h + 1) * HEAD])
            o_ref[:, cs] = (ctx * _silu(z_ref[:, cs])).astype(BF16)

    col = lambda k: pl.BlockSpec((tm, XA_W), lambda i: (i, CQ_BLK + k))
    return pl.pallas_call(
        body, name="xattn_fwd", grid=(s // tm,),
        in_specs=[col(0), col(1), pl.BlockSpec((nm, 2 * XA_W), lambda i: (0, 0))],
        out_specs=pl.BlockSpec((tm, XA_W), lambda i: (i, 0)), out_shape=_sds((s, XA_W), BF16),
        compiler_params=_params(("parallel",)),
    )(proj, proj, mkv)


def _xattn_bwd(proj, dmixed, mkv):
    s = proj.shape[0]
    nm = mkv.shape[0]
    tm = _tile(s, (512, 256, 128))

    def body(q_ref, z_ref, d_ref, kv_ref, dp_ref, dkv_ref):
        @pl.when(pl.program_id(0) == 0)
        def _():
            dkv_ref[...] = jnp.zeros_like(dkv_ref)

        for h in range(XA_H):
            cs = slice(h * HEAD, (h + 1) * HEAD)
            vs = slice(XA_W + h * HEAD, XA_W + (h + 1) * HEAD)
            q = q_ref[:, cs]
            z = z_ref[:, cs]
            mk = kv_ref[:, cs]
            mv = kv_ref[:, vs]
            p = _xattn_probs(q, mk)
            ctx = _mm(p, mv)
            dc = d_ref[:, cs]
            dctx = dc * _silu(z)
            dp_ref[:, vs] = (dc * ctx * _silu_grad(z)).astype(BF16)
            dp = _mm_nt(dctx, mv)
            dkv_ref[:, vs] += _mm_tn(p, dctx)
            ds = p * (dp - _rowsum(dp * p)) * (HEAD ** -0.5)
            dp_ref[:, cs] = _mm(ds, mk).astype(BF16)
            dkv_ref[:, cs] += _mm_tn(ds, q)

    col = lambda k: pl.BlockSpec((tm, XA_W), lambda i: (i, CQ_BLK + k))
    kvs = pl.BlockSpec((nm, 2 * XA_W), lambda i: (0, 0))
    return pl.pallas_call(
        body, name="xattn_bwd", grid=(s // tm,),
        in_specs=[col(0), col(1), pl.BlockSpec((tm, XA_W), lambda i: (i, (DN_W + GMLP_W) // XA_W)), kvs],
        out_specs=[pl.BlockSpec((tm, 2 * XA_W), lambda i: (i, 0)), kvs],
        out_shape=[_sds((s, 2 * XA_W), BF16), _sds((nm, 2 * XA_W))],
        compiler_params=_params(("arbitrary",)),
    )(proj, proj, dmixed, mkv)


def _softplus(x):
    return jnp.maximum(x, 0.0) + jnp.log1p(jnp.exp(-jnp.abs(x)))


def _dn_pre(proj, ab, conv_w, alog_row, dt_row):
    s = proj.shape[0]
    tm = _tile(s, (256, 128))
    w3 = 3 * DN_W

    def body(x_ref, halo_ref, ab_ref, cw_ref, al_ref, dt_ref, q_ref, k_ref, v_ref, gb_ref, gbt_ref, yc_ref):
        i = pl.program_id(0)
        xv = x_ref[...]
        cat = jnp.concatenate([jnp.where(i > 0, halo_ref[...], 0.0), xv[0:HALO]], axis=0)
        yc = cw_ref[DN_K - 1:DN_K, :] * xv
        top = cw_ref[DN_K - 1:DN_K, :] * xv[0:HALO]
        for t in range(DN_K - 1):
            back = DN_K - 1 - t
            yc += cw_ref[t:t + 1, :] * pltpu.roll(xv, back, 0)
            top += cw_ref[t:t + 1, :] * pltpu.roll(cat, back, 0)[HALO:2 * HALO]
        yc = jnp.concatenate([top, yc[HALO:tm]], axis=0)
        yc_ref[...] = yc
        act = _silu(yc)
        for h in range(DN_H):
            cs = slice(h * HEAD, (h + 1) * HEAD)
            qa = act[:, cs]
            q_ref[:, cs] = qa * (lax.rsqrt(_rowsum(qa * qa) + EPS) * (HEAD ** -0.5))
            ka = act[:, DN_W + h * HEAD:DN_W + (h + 1) * HEAD]
            k_ref[:, cs] = ka * lax.rsqrt(_rowsum(ka * ka) + EPS)
        v_ref[...] = act[:, 2 * DN_W:w3]
        abv = ab_ref[...]
        lane = _iota2((tm, LANE), 1)
        g = jnp.where(lane < DN_H, -jnp.exp(al_ref[...]) * _softplus(abv + dt_ref[...]), 0.0)
        gc = _mm_hi(_chunk_tri(tm, False), g)
        gbv = jnp.where(lane < DN_H, gc, jnp.where(lane < 2 * DN_H, jax.nn.sigmoid(abv), 0.0))
        gb_ref[...] = gbv
        for c in range(tm // CH):
            gbt_ref[c] = gbv[c * CH:(c + 1) * CH, :].T[0:2 * DN_H, :]

    hb = tm // HALO
    row = lambda w: pl.BlockSpec((tm, w), lambda i: (i, 0))
    vec = pl.BlockSpec((1, LANE), lambda i: (0, 0))
    return pl.pallas_call(
        body, name="dn_pre", grid=(s // tm,),
        in_specs=[row(w3), pl.BlockSpec((HALO, w3), lambda i: (jnp.maximum(i * hb - 1, 0), 0)), row(LANE),
                  pl.BlockSpec((DN_K, w3), lambda i: (0, 0)), vec, vec],
        out_specs=[row(DN_W), row(DN_W), row(DN_W), row(LANE), pl.BlockSpec((tm // CH, 2 * DN_H, CH), lambda i: (i, 0, 0)),
                   row(w3)],
        out_shape=[_sds((s, DN_W)), _sds((s, DN_W)), _sds((s, DN_W)), _sds((s, LANE)),
                   _sds((s // CH, 2 * DN_H, CH)), _sds((s, w3))],
        compiler_params=_params(("parallel",)),
    )(proj, proj, ab, conv_w, alog_row, dt_row)


HEADS = tuple(range(DN_H))


def _hcols(h):
    return slice(h * HEAD, (h + 1) * HEAD)


def _chunk_scalings(k, v, gbv, gbt, h):
    gc = jnp.broadcast_to(gbv[:, h:h + 1], (CH, HEAD))
    beta = jnp.broadcast_to(gbv[:, DN_H + h:DN_H + h + 1], (CH, HEAD))
    gr = gbt[h:h + 1, :]
    ii = _iota2((CH, CH), 0)
    jj = _iota2((CH, CH), 1)
    dec = jnp.exp(jnp.where(ii >= jj, gc[:, 0:CH] - gr, -1e30))
    eg = jnp.exp(gc)
    gl = gr[:, CH - 1:CH]
    kb = k * beta
    return dict(beta=beta, dec=dec, eg=eg, gl=gl, ekd=jnp.exp(gl - gc), kb=kb, vb=v * beta, kbe=kb * eg)


def _chunk_scores(m, q, k):
    kq = _mm_nt(jnp.concatenate([m["kb"], q], axis=0), k)
    strict = _iota2((CH, CH), 0) > _iota2((CH, CH), 1)
    return jnp.where(strict, kq[0:CH] * m["dec"], 0.0), kq[CH:2 * CH] * m["dec"]


def _scan_cpb(s):
    return 8 if (s // CH) % 8 == 0 else 1


def _dn_fwd(q, k, v, gb, gbt, proj, norm_g):
    s = q.shape[0]
    cpb = _scan_cpb(s)
    tb = cpb * CH
    nblk = s // tb

    def body(q_ref, k_ref, v_ref, gb_ref, gbt_ref, z_ref, ng_ref,
             w_ref, qg_ref, kd_ref, t_ref, ai_ref, egl_ref, o_ref, vn_ref, st_ref, ob_ref, state):
        @pl.when(pl.program_id(0) == 0)
        def _():
            state[...] = jnp.zeros_like(state)

        ng = ng_ref[...]
        eye = jnp.where(_iota2((CH, CH), 0) == _iota2((CH, CH), 1), 1.0, 0.0).astype(F32)

        def chunk(c, carry):
            r0 = pl.multiple_of(c * CH, CH)
            rows = pl.ds(r0, CH)
            gbv = gb_ref[rows, :]
            gbt_v = gbt_ref[c]
            qs = [q_ref[rows, _hcols(h)] for h in HEADS]
            ks = [k_ref[rows, _hcols(h)] for h in HEADS]
            ms = [_chunk_scalings(ks[h], v_ref[rows, _hcols(h)], gbv, gbt_v, h) for h in HEADS]
            qgb = [(qs[h] * ms[h]["eg"]).astype(BF16) for h in HEADS]
            kdb = [(ks[h] * ms[h]["ekd"]).astype(BF16) for h in HEADS]
            egl = [jnp.broadcast_to(jnp.exp(ms[h]["gl"]), (1, LANE)) for h in HEADS]
            for h in HEADS:
                qg_ref[rows, _hcols(h)] = qgb[h]
                kd_ref[rows, _hcols(h)] = kdb[h]
                egl_ref[c, h:h + 1, :] = egl[h]
            sc = [_chunk_scores(ms[h], qs[h], ks[h]) for h in HEADS]
            for h in HEADS:
                ai_ref[h, rows, :] = sc[h][1]
            ts = [eye - sc[h][0] for h in HEADS]
            ps = [_mm_3x(sc[h][0], sc[h][0]) for h in HEADS]
            ts = [ts[h] + _mm_3x(ts[h], ps[h]) for h in HEADS]
            for _ in range(4):
                ps = [_mm(ps[h], ps[h]) for h in HEADS]
                ts = [ts[h] + _mm(ts[h], ps[h]) for h in HEADS]
            uw = [_mm(ts[h], jnp.concatenate([ms[h]["vb"], ms[h]["kbe"]], axis=1)) for h in HEADS]
            wb = [uw[h][:, HEAD:2 * HEAD].astype(BF16) for h in HEADS]
            for h in HEADS:
                t_ref[h, rows, :] = ts[h]
                w_ref[rows, _hcols(h)] = wb[h]
            sts = [state[h] for h in HEADS]
            stb = [sts[h].astype(BF16) for h in HEADS]
            for h in HEADS:
                st_ref[c, h] = stb[h]
            vnb = [(uw[h][:, 0:HEAD] - jnp.dot(wb[h], stb[h], preferred_element_type=F32)).astype(BF16) for h in HEADS]
            for h in HEADS:
                state[h] = sts[h] * egl[h] + _mm_tn(kdb[h], vnb[h])
            os_ = [jnp.dot(qgb[h], stb[h], preferred_element_type=F32) + _mm(sc[h][1], vnb[h]) for h in HEADS]
            for h in HEADS:
                o = os_[h]
                vn_ref[rows, _hcols(h)] = vnb[h]
                o_ref[rows, _hcols(h)] = o
                r = lax.rsqrt(jnp.mean(o * o, axis=-1, keepdims=True) + EPS)
                ob_ref[rows, _hcols(h)] = (o * r * ng * _silu(z_ref[rows, _hcols(h)])).astype(BF16)
            return carry

        lax.fori_loop(0, cpb, chunk, 0, unroll=4)

    row = pl.BlockSpec((tb, DN_W), lambda i: (i, 0))
    sq = pl.BlockSpec((DN_H, tb, CH), lambda i: (0, i, 0))
    return pl.pallas_call(
        body, name="dn_fwd", grid=(nblk,),
        in_specs=[row, row, row, pl.BlockSpec((tb, LANE), lambda i: (i, 0)),
                  pl.BlockSpec((cpb, 2 * DN_H, CH), lambda i: (i, 0, 0)), pl.BlockSpec((tb, DN_W), lambda i: (i, 3)),
                  pl.BlockSpec((1, HEAD), lambda i: (0, 0))],
        out_specs=[row, row, row, sq, sq, pl.BlockSpec((cpb, DN_H, LANE), lambda i: (i, 0, 0)), row, row,
                   pl.BlockSpec((cpb, DN_H, HEAD, HEAD), lambda i: (i, 0, 0, 0)), row],
        out_shape=[_sds((s, DN_W), BF16), _sds((s, DN_W), BF16), _sds((s, DN_W), BF16), _sds((DN_H, s, CH)),
                   _sds((DN_H, s, CH)), _sds((s // CH, DN_H, LANE)), _sds((s, DN_W)), _sds((s, DN_W), BF16),
                   _sds((s // CH, DN_H, HEAD, HEAD), BF16), _sds((s, DN_W), BF16)],
        scratch_shapes=[pltpu.VMEM((DN_H, HEAD, HEAD), F32)],
        compiler_params=_params(("arbitrary",)),
    )(q, k, v, gb, gbt, proj, norm_g)


def _dn_bwd(dmixed, o, proj, norm_g, w, qg, kd, ai, egl, q, k, v, gb, gbt, t, vn, st):
    s = o.shape[0]
    cpb = 4 if (s // CH) % 4 == 0 else 1
    tb = cpb * CH
    nblk = s // tb

    def body(dm_ref, o_ref, z_ref, ng_ref, w_ref, qg_ref, kd_ref, ai_ref, egl_ref,
             q_ref, k_ref, v_ref, gb_ref, gbt_ref, t_ref, vn_ref, st_ref,
             dq_ref, dk_ref, dv_ref, dgb_ref, dz_ref, dng_ref, dstate):
        @pl.when(pl.program_id(0) == 0)
        def _():
            dstate[...] = jnp.zeros_like(dstate)
            dng_ref[...] = jnp.zeros_like(dng_ref)

        ng = ng_ref[...]
        lane = _iota2((CH, LANE), 1)
        last = _iota2((CH, 1), 0) == CH - 1
        strict = _iota2((CH, CH), 0) > _iota2((CH, CH), 1)

        def chunk(cc, carry):
            c = cpb - 1 - cc
            r0 = pl.multiple_of(c * CH, CH)
            rows = pl.ds(r0, CH)
            dng = jnp.zeros((1, HEAD), F32)
            dob = []
            for h in HEADS:
                cs = _hcols(h)
                ov = o_ref[rows, cs]
                z = z_ref[rows, cs]
                db = dm_ref[rows, cs]
                r = lax.rsqrt(jnp.mean(ov * ov, axis=-1, keepdims=True) + EPS)
                ohat = ov * r
                dz_ref[rows, cs] = (db * ohat * ng * _silu_grad(z)).astype(BF16)
                dyn = db * _silu(z)
                dng += _colsum(dyn * ohat)
                doh = dyn * ng
                dob.append((r * (doh - ohat * jnp.mean(doh * ohat, axis=-1, keepdims=True))).astype(BF16))
            dng_ref[...] += dng
            dsn = [dstate[h] for h in HEADS]
            dsb = [dsn[h].astype(BF16) for h in HEADS]
            dvnb = [(_mm_tn(ai_ref[h, rows, :], dob[h])
                     + jnp.dot(kd_ref[rows, _hcols(h)], dsb[h], preferred_element_type=F32)).astype(BF16) for h in HEADS]
            part = [_mm_tn(qg_ref[rows, _hcols(h)], dob[h]) + egl_ref[c, h:h + 1, :] * dsn[h] for h in HEADS]
            for h in HEADS:
                dstate[h] = part[h] - _mm_tn(w_ref[rows, _hcols(h)], dvnb[h])
            gbv = gb_ref[rows, :]
            gbt_v = gbt_ref[c]
            qs = [q_ref[rows, _hcols(h)] for h in HEADS]
            ks = [k_ref[rows, _hcols(h)] for h in HEADS]
            vs = [v_ref[rows, _hcols(h)] for h in HEADS]
            ms = [_chunk_scalings(ks[h], vs[h], gbv, gbt_v, h) for h in HEADS]
            sts = [st_ref[c, h] for h in HEADS]
            vnb = [vn_ref[rows, _hcols(h)] for h in HEADS]
            tbf = [t_ref[h, rows, :].astype(BF16) for h in HEADS]
            sc = [_chunk_scores(ms[h], qs[h], ks[h]) for h in HEADS]
            xs_ = [_mm_nt(jnp.concatenate([dob[h], dvnb[h]], axis=0), sts[h]) for h in HEADS]
            dai = [_mm_nt(dob[h], vnb[h]) for h in HEADS]
            dkd = [_mm_nt(vnb[h], dsb[h]) for h in HEADS]
            dqg = [xs_[h][0:CH] for h in HEADS]
            duw = [jnp.concatenate([dvnb[h], (-xs_[h][CH:2 * CH]).astype(BF16)], axis=1) for h in HEADS]
            dt = [_mm_nt(duw[h], jnp.concatenate([ms[h]["vb"], ms[h]["kbe"]], axis=1)) for h in HEADS]
            dvk = [_mm_tn(tbf[h], duw[h]) for h in HEADS]
            tdt = [_mm_tn(tbf[h], dt[h]) for h in HEADS]
            da = [jnp.where(strict, -_mm_nt(tdt[h], tbf[h]), 0.0) for h in HEADS]
            dsc = [jnp.concatenate([da[h] * ms[h]["dec"], dai[h] * ms[h]["dec"]], axis=0) for h in HEADS]
            dkq = [_mm(dsc[h], ks[h]) for h in HEADS]
            dk1 = [_mm_tn(dsc[h], jnp.concatenate([ms[h]["kb"], qs[h]], axis=0)) for h in HEADS]
            dgb = jnp.zeros((CH, LANE), F32)
            for h in HEADS:
                m = ms[h]
                eg, ekd, beta = m["eg"], m["ekd"], m["beta"]
                dvb = dvk[h][:, 0:HEAD]
                dkbe = dvk[h][:, HEAD:2 * HEAD]
                kdv = ks[h] * ekd
                dkb = dkq[h][0:CH] + dkbe * eg
                dq_ref[rows, _hcols(h)] = dkq[h][CH:2 * CH] + dqg[h] * eg
                dk_ref[rows, _hcols(h)] = dk1[h] + dkd[h] * ekd + dkb * beta
                dv_ref[rows, _hcols(h)] = dvb * beta
                dkd_kd = dkd[h] * kdv
                dgl = (jnp.exp(m["gl"]) * _rowsum(_colsum(sts[h].astype(F32) * dsb[h].astype(F32)))
                       + _rowsum(_colsum(dkd_kd)))
                mm_ = da[h] * sc[h][0] + dai[h] * sc[h][1]
                dgc = (_rowsum(mm_ - mm_.T) + _rowsum(dqg[h] * qs[h] * eg - dkd_kd + dkbe * m["kbe"])
                       + jnp.where(last, dgl, 0.0))
                dbeta = _rowsum(dkb * ks[h] + dvb * vs[h])
                dgb = jnp.where(lane == h, dgc, jnp.where(lane == DN_H + h, dbeta, dgb))
            dgb_ref[rows, :] = dgb
            return carry

        lax.fori_loop(0, cpb, chunk, 0, unroll=2)

    rev = lambda i: (nblk - 1 - i, 0)
    row = pl.BlockSpec((tb, DN_W), rev)
    vec = pl.BlockSpec((1, HEAD), lambda i: (0, 0))
    sq = pl.BlockSpec((DN_H, tb, CH), lambda i: (0, nblk - 1 - i, 0))
    gbs = pl.BlockSpec((tb, LANE), rev)
    return pl.pallas_call(
        body, name="dn_bwd", grid=(nblk,),
        in_specs=[row, row, pl.BlockSpec((tb, DN_W), lambda i: (nblk - 1 - i, 3)), vec, row, row, row, sq,
                  pl.BlockSpec((cpb, DN_H, LANE), lambda i: (nblk - 1 - i, 0, 0)),
                  row, row, row, gbs, pl.BlockSpec((cpb, 2 * DN_H, CH), lambda i: (nblk - 1 - i, 0, 0)), sq, row,
                  pl.BlockSpec((cpb, DN_H, HEAD, HEAD), lambda i: (nblk - 1 - i, 0, 0, 0))],
        out_specs=[row, row, row, gbs, row, vec],
        out_shape=[_sds((s, DN_W)), _sds((s, DN_W)), _sds((s, DN_W)), _sds((s, LANE)), _sds((s, DN_W), BF16),
                   _sds((1, HEAD))],
        scratch_shapes=[pltpu.VMEM((DN_H, HEAD, HEAD), F32)],
        compiler_params=_params(("arbitrary",)),
    )(dmixed, o, proj, norm_g, w, qg, kd, ai, egl, q, k, v, gb, gbt, t, vn, st)


def _dn_pre_bwd(proj, yc_all, ab, conv_w, alog_row, dt_row, dq, dk, dv, dgb):
    s = proj.shape[0]
    tm = _tile(s, (256, 128))
    w3 = 3 * DN_W
    nblk = s // tm

    def body(x_ref, yc_ref, ab_ref, cw_ref, al_ref, dt_ref, dq_ref, dk_ref, dv_ref, dgb_ref,
             dx_ref, dab_ref, dcw_ref, dal_ref, ddt_ref, exd, carry):
        i = pl.program_id(0)

        @pl.when(i == 0)
        def _():
            carry[...] = jnp.zeros_like(carry)
            dcw_ref[...] = jnp.zeros_like(dcw_ref)
            dal_ref[...] = jnp.zeros_like(dal_ref)
            ddt_ref[...] = jnp.zeros_like(ddt_ref)

        yc = yc_ref[...]
        sg = jax.nn.sigmoid(yc)
        act = yc * sg
        dact = sg * (1.0 + yc * (1.0 - sg))
        for h in range(DN_H):
            cs = slice(h * HEAD, (h + 1) * HEAD)
            ks = slice(DN_W + h * HEAD, DN_W + (h + 1) * HEAD)
            qa = act[:, cs]
            rq = lax.rsqrt(_rowsum(qa * qa) + EPS)
            qh = qa * rq
            dqv = dq_ref[:, cs]
            exd[0:tm, cs] = (HEAD ** -0.5) * rq * (dqv - qh * _rowsum(dqv * qh)) * dact[:, cs]
            ka = act[:, ks]
            rk = lax.rsqrt(_rowsum(ka * ka) + EPS)
            kh = ka * rk
            dkv = dk_ref[:, cs]
            exd[0:tm, ks] = rk * (dkv - kh * _rowsum(dkv * kh)) * dact[:, ks]
        exd[0:tm, 2 * DN_W:w3] = dv_ref[...] * dact[:, 2 * DN_W:w3]
        xv = x_ref[...]
        dyc = exd[...]
        cat = jnp.concatenate([dyc[tm - HALO:tm], carry[...]], axis=0)
        dcw_ref[DN_K - 1:DN_K, :] += _colsum(dyc * xv)
        dx = cw_ref[DN_K - 1:DN_K, :] * dyc
        for t in range(DN_K - 1):
            ahead = DN_K - 1 - t
            view = jnp.concatenate([pltpu.roll(dyc, tm - ahead, 0)[0:tm - HALO],
                                    pltpu.roll(cat, 2 * HALO - ahead, 0)[0:HALO]], axis=0)
            dcw_ref[t:t + 1, :] += _colsum(view * xv)
            dx += cw_ref[t:t + 1, :] * view
        dx_ref[...] = dx.astype(BF16)
        carry[...] = dyc[0:HALO]

        lane = _iota2((tm, LANE), 1)
        dgbv = dgb_ref[...]
        dg = _mm_hi(_chunk_tri(tm, True), jnp.where(lane < DN_H, dgbv, 0.0))
        abv = ab_ref[...]
        xa = abv + dt_ref[...]
        nea = -jnp.exp(al_ref[...])
        d_da = jnp.where(lane < DN_H, dg * nea * jax.nn.sigmoid(xa), 0.0)
        dal_ref[...] += _colsum(jnp.where(lane < DN_H, dg * nea * _softplus(xa), 0.0))
        ddt_ref[...] += _colsum(d_da)
        beta = jax.nn.sigmoid(abv)
        d_db = jnp.where((lane >= DN_H) & (lane < 2 * DN_H), dgbv * beta * (1.0 - beta), 0.0)
        dab_ref[...] = (d_da + d_db).astype(BF16)

    rev = lambda i: (nblk - 1 - i, 0)
    row = lambda w: pl.BlockSpec((tm, w), rev)
    vec = pl.BlockSpec((1, LANE), lambda i: (0, 0))
    cws = pl.BlockSpec((DN_K, w3), lambda i: (0, 0))
    return pl.pallas_call(
        body, name="dn_pre_bwd", grid=(nblk,),
        in_specs=[row(w3), row(w3), row(LANE), cws, vec, vec, row(DN_W), row(DN_W), row(DN_W), row(LANE)],
        out_specs=[row(w3), row(LANE), cws, vec, vec],
        out_shape=[_sds((s, w3), BF16), _sds((s, LANE), BF16), _sds((DN_K, w3)), _sds((1, LANE)), _sds((1, LANE))],
        scratch_shapes=[pltpu.VMEM((tm, w3), F32), pltpu.VMEM((HALO, w3), F32)],
        compiler_params=_params(("arbitrary",)),
    )(proj, yc_all, ab, conv_w, alog_row, dt_row, dq, dk, dv, dgb)


def _adam(parts, w, m, v, name):
    r, c = w.shape
    n_parts = parts.shape[0]
    small = n_parts * r * c * 4 <= 4 * 1024 * 1024
    tr = r if small else _tile(r, (128, 64, 32, 16, 8))

    def body(p_ref, w_ref, m_ref, v_ref, g_ref, d_ref, nm_ref, nv_ref):
        g = p_ref[0].astype(F32)
        for k in range(1, n_parts):
            g = g + p_ref[k].astype(F32)
        g_ref[...] = g
        mn = ADAM_B1 * m_ref[...] + (1.0 - ADAM_B1) * g
        vn = ADAM_B2 * v_ref[...] + (1.0 - ADAM_B2) * (g * g)
        m_hat = mn / (1.0 - ADAM_B1 ** ADAM_STEP)
        v_hat = vn / (1.0 - ADAM_B2 ** ADAM_STEP)
        d_ref[...] = -ADAM_LR * (m_hat / (jnp.sqrt(v_hat) + ADAM_EPS) + ADAM_WD * w_ref[...])
        nm_ref[...] = mn
        nv_ref[...] = vn

    blk = pl.BlockSpec((tr, c), lambda i: (i, 0))
    return pl.pallas_call(
        body, name=name, grid=(r // tr,),
        in_specs=[pl.BlockSpec((n_parts, tr, c), lambda i: (0, i, 0)), blk, blk, blk],
        out_specs=[blk, blk, blk, blk], out_shape=[_sds((r, c))] * 4,
        compiler_params=_params(("parallel",)),
    )(parts, w, m, v)


_PACK_ROWS = 8


def _pack(vals):
    tiles = []
    for a in vals:
        flat = a.reshape(-1).astype(F32)
        unit = _PACK_ROWS * LANE
        n = -(-flat.shape[0] // unit) * unit
        tiles.append(jnp.pad(flat, (0, n - flat.shape[0])).reshape(n // LANE, LANE))
    return jnp.concatenate(tiles, axis=0)


def _unpack(packed, shapes):
    out = []
    r0 = 0
    for shp in shapes:
        size = 1
        for dim in shp:
            size *= dim
        unit = _PACK_ROWS * LANE
        rows = -(-size // unit) * _PACK_ROWS
        out.append(packed[r0:r0 + rows].reshape(-1)[:size].reshape(shp))
        r0 += rows
    return out


def _lane_row(vec8):
    return jnp.pad(vec8.reshape(1, -1).astype(F32), ((0, 0), (0, LANE - vec8.size)))


def kernel(x, mem, ln_g, w_in, gmlp_ln_g, gmlp_ln_b, gmlp_ws, gmlp_bs, conv_w, dn_a_log, dn_dt_bias, dn_norm_g, mem_norm_g, w_mem_kv, w_out, final_g, loss_target, m_ln_g, m_w_in, m_gmlp_ln_g, m_gmlp_ln_b, m_gmlp_ws, m_gmlp_bs, m_conv_w, m_dn_a_log, m_dn_dt_bias, m_dn_norm_g, m_mem_norm_g, m_w_mem_kv, m_w_out, m_final_g, v_ln_g, v_w_in, v_gmlp_ln_g, v_gmlp_ln_b, v_gmlp_ws, v_gmlp_bs, v_conv_w, v_dn_a_log, v_dn_dt_bias, v_dn_norm_g, v_mem_norm_g, v_w_mem_kv, v_w_out, v_final_g):
    xs = x[0]
    mems = mem[0]
    tgt = loss_target[0]
    s, d = xs.shape
    shard_w = w_in.shape[2]
    in_w = N_DEV * shard_w
    me = 4 * lax.axis_index("x") + 2 * lax.axis_index("y") + lax.axis_index("c")

    (g_in,) = _gather_two_level([w_in[0].astype(BF16)], "gather_w_in")
    o_g, o_dn, o_ab = 0, 3 * GMLP_W, 3 * GMLP_W + 4 * DN_W
    o_xa = o_ab + 2 * DN_H

    def shard_cols(g, lo, hi):
        out = []
        while lo < hi:
            sh = lo // shard_w
            end = min(hi, (sh + 1) * shard_w)
            out.append(g[sh][:, lo - sh * shard_w:end - sh * shard_w])
            lo = end
        return out

    def own_layout(g):
        main = jnp.concatenate(shard_cols(g, o_dn, o_ab) + shard_cols(g, o_g, o_dn) + shard_cols(g, o_xa, in_w), axis=1)
        return main, jnp.pad(jnp.concatenate(shard_cols(g, o_ab, o_xa), axis=1), ((0, 0), (0, LANE - 2 * DN_H)))

    w_main, w_ab = own_layout(g_in)

    ln_g2 = ln_g.reshape(1, d)
    lng2 = gmlp_ln_g.reshape(1, GMLP_W)
    lnb2 = gmlp_ln_b.reshape(1, GMLP_W)
    ws3 = gmlp_ws[0]
    bs_t = gmlp_bs[0].T
    alog_row = _lane_row(dn_a_log)
    dt_row = _lane_row(dn_dt_bias)
    dn_g2 = dn_norm_g.reshape(1, HEAD)
    mem_g2 = mem_norm_g.reshape(1, d)
    fin_g2 = final_g.reshape(1, d)

    proj, ab, h_t, (g_out, g_kv, g_conv) = _inproj(
        xs, ln_g2, w_main, w_ab, [w_out[0].astype(BF16), w_mem_kv[0].astype(BF16), conv_w[0]])
    wo = g_out.reshape(MIX_W, d)
    wo_perm = jnp.concatenate([wo[GMLP_W:GMLP_W + DN_W], wo[0:GMLP_W], wo[GMLP_W + DN_W:MIX_W]], axis=0)
    w_kv = g_kv.reshape(d, 2 * XA_W)
    conv_full = g_conv.transpose(1, 0, 2).reshape(DN_K, 3 * DN_W)
    out_a = _gmlp_fwd(proj, lng2, lnb2, ws3, bs_t)
    mkv = _memkv_fwd(mems, mem_g2, w_kv)
    out_c = _xattn_fwd(proj, mkv)
    q, k, v, gb, gbt, yc = _dn_pre(proj, ab, conv_full, alog_row, dt_row)
    wk, qg, kd, tmat, ai, egl, o, vn, st, out_b = _dn_fwd(q, k, v, gb, gbt, proj, dn_g2)

    dx2, dx2b, dmixed, loss_acc, d_fin_g = _final(xs, tgt, out_b, out_a, out_c, wo_perm, fin_g2)

    dwo_b = _matmul_tn(out_b, dx2b, "dw_out_b")
    dwo_a = _matmul_tn(out_a, dx2b, "dw_out_a")
    dwo_c = _matmul_tn(out_c, dx2b, "dw_out_c")
    d_w_out = jnp.concatenate([dwo_a, dwo_b, dwo_c], axis=0)

    dp_g, d_ws, d_bst, d_lng, d_lnb = _gmlp_bwd(proj, dmixed, lng2, lnb2, ws3, bs_t)
    dp_x, dmkv = _xattn_bwd(proj, dmixed, mkv)
    d_w_kv, d_mem_g = _memkv_bwd(mems, mem_g2, w_kv, dmkv)
    dq, dk, dv, dgb, dp_dz, d_dn_g = _dn_bwd(dmixed, o, proj, dn_g2, wk, qg, kd, ai, egl, q, k, v, gb, gbt, tmat, vn, st)
    dp_qkv, dp_ab, d_conv, d_alog, d_dt = _dn_pre_bwd(proj, yc, ab, conv_full, alog_row, dt_row, dq, dk, dv, dgb)

    dw_qkv = _matmul_acc(h_t, dp_qkv, "dw_in_qkv")
    dw_dz = _matmul_acc(h_t, dp_dz, "dw_in_dz")
    dw_gm = _matmul_acc(h_t, dp_g, "dw_in_gmlp")
    dw_xa = _matmul_acc(h_t, dp_x, "dw_in_xa")
    dw_ab = _matmul_acc(h_t, dp_ab, "dw_in_ab")
    segs = [(o_g, dw_gm), (o_dn, dw_qkv), (o_dn + 3 * DN_W, dw_dz), (o_ab, dw_ab[:, :2 * DN_H]), (o_xa, dw_xa)]
    shards = []
    for sh in range(N_DEV):
        lo, hi = sh * shard_w, (sh + 1) * shard_w
        parts = [arr[:, max(lo, off) - off:min(hi, off + arr.shape[1]) - off] for off, arr in segs
                 if off < hi and off + arr.shape[1] > lo]
        shards.append(jnp.concatenate(parts, axis=1).astype(BF16))
    send_in = jnp.stack(shards)

    small_shapes = [(1, 1), gmlp_ln_g.shape, gmlp_ln_b.shape, gmlp_ws.shape, gmlp_bs.shape, dn_a_log.shape,
                    dn_dt_bias.shape, dn_norm_g.shape, mem_norm_g.shape, final_g.shape, (DN_K, 3 * DN_W)]
    small_g = _pack([loss_acc[0:1, 0:1], d_lng, d_lnb, d_ws, d_bst.T, d_alog[:, :DN_H], d_dt[:, :DN_H], d_dn_g, d_mem_g,
                     d_fin_g, d_conv])
    zc = jnp.zeros((DN_K, 3 * DN_W), F32)
    z1 = jnp.zeros((1, 1), F32)
    small_w = _pack([z1, gmlp_ln_g, gmlp_ln_b, gmlp_ws, gmlp_bs, dn_a_log, dn_dt_bias, dn_norm_g, mem_norm_g, final_g, zc])
    small_m = _pack([z1, m_gmlp_ln_g, m_gmlp_ln_b, m_gmlp_ws, m_gmlp_bs, m_dn_a_log, m_dn_dt_bias, m_dn_norm_g,
                     m_mem_norm_g, m_final_g, zc])
    small_v = _pack([z1 + 1.0, v_gmlp_ln_g, v_gmlp_ln_b, v_gmlp_ws, v_gmlp_bs, v_dn_a_log, v_dn_dt_bias, v_dn_norm_g,
                     v_mem_norm_g, v_final_g, zc + 1.0])

    send_out = d_w_out.reshape(N_DEV, MIX_W // N_DEV, d).astype(BF16)
    send_kv = d_w_kv.reshape(N_DEV, d // N_DEV, 2 * XA_W).astype(BF16)
    sends = [send_in, send_out, send_kv]
    all_small, got = _swap_halves(small_g, sends, "swap_halves")
    core = lax.axis_index("c").astype(jnp.int32).reshape(1)
    chip_sums = _pair_sums(core, sends, got)
    grad_x, d_ln_g, (r_in, r_out, r_kv) = _dh_rms(
        [dp_qkv, dp_dz, dp_g, dp_x, dp_ab], [w_main], [w_ab], xs, dx2, ln_g2, chip_sums)
    (all_ln_g,) = _gather_two_level([_pack([d_ln_g])], "gather_ln_g")

    g_w_in, dl_w_in, nm_w_in, nv_w_in = _adam(r_in, w_in[0], m_w_in[0], v_w_in[0], "adam_w_in")
    g_w_out, dl_w_out, nm_w_out, nv_w_out = _adam(r_out, w_out[0], m_w_out[0], v_w_out[0], "adam_w_out")
    g_w_kv, dl_w_kv, nm_w_kv, nv_w_kv = _adam(r_kv, w_mem_kv[0], m_w_mem_kv[0], v_w_mem_kv[0], "adam_w_kv")
    sm = [_unpack(t, small_shapes) for t in _adam(all_small, small_w, small_m, small_v, "adam_small")]
    ln_res = [_unpack(t, [ln_g.shape])[0]
              for t in _adam(all_ln_g, _pack([ln_g]), _pack([m_ln_g]), _pack([v_ln_g]), "adam_ln_g")]

    conv_parts = lax.dynamic_slice(all_small, (0, all_small.shape[1] - (DN_K * 3 * DN_W) // LANE, 0),
                                   (N_DEV, (DN_K * 3 * DN_W) // LANE, LANE)).reshape(N_DEV, DN_K, 3 * DN_W)
    cshard = conv_w.shape[2]
    conv_parts = lax.dynamic_slice(conv_parts, (0, 0, me * cshard), (N_DEV, DN_K, cshard))
    cpad = ((0, 0), (0, HALO - DN_K), (0, 0))
    conv_res = _adam(jnp.pad(conv_parts, cpad), jnp.pad(conv_w[0], cpad[1:]), jnp.pad(m_conv_w[0], cpad[1:]),
                     jnp.pad(v_conv_w[0], cpad[1:], constant_values=1.0), "adam_conv")
    g_conv_s, dl_conv, nm_conv, nv_conv = [t[:DN_K][None] for t in conv_res]

    loss = sm[0][0].reshape(())

    def group(idx, big_in, big_conv, big_kv, big_out):
        names = sm[idx][1:]
        return [ln_res[idx], big_in[None], names[0], names[1], names[2], names[3], big_conv, names[4], names[5], names[6],
                names[7], big_kv[None], big_out[None], names[8]]

    grads = group(0, g_w_in, g_conv_s, g_w_kv, g_w_out)
    deltas = group(1, dl_w_in, dl_conv, dl_w_kv, dl_w_out)
    new_m = group(2, nm_w_in, nm_conv, nm_w_kv, nm_w_out)
    new_v = group(3, nv_w_in, nv_conv, nv_w_kv, nv_w_out)
    return (loss, grad_x[None], *grads, *deltas, *new_m, *new_v)
```

```python
import functools

import jax
import jax.numpy as jnp
from jax import lax
from jax.experimental import pallas as pl
from jax.experimental.pallas import tpu as pltpu

F32 = jnp.float32
BF16 = jnp.bfloat16
HIGHEST = lax.Precision.HIGHEST
MESH_ID = pl.DeviceIdType.MESH

N_DEV = 8
EPS = 1e-6
GMLP_W = 512
GMLP_G = 4
GMLP_T = 128
DN_W = 1024
DN_H = 8
HEAD = 128
DN_K = 4
CH = 64
XA_W = 512
XA_H = 4
LANE = 128
HALO = 8
MAIN_W = 4 * DN_W + 3 * GMLP_W + 2 * XA_W
MIX_W = DN_W + GMLP_W + XA_W
VMEM_LIMIT = 56 * 1024 * 1024

ADAM_LR = 0.001
ADAM_B1 = 0.9
ADAM_B2 = 0.999
ADAM_EPS = 1e-08
ADAM_WD = 0.01
ADAM_STEP = 10


def _sds(shape, dtype=F32):
    return jax.ShapeDtypeStruct(tuple(shape), dtype)


def _params(sem=None):
    if sem is None:
        return pltpu.CompilerParams(vmem_limit_bytes=VMEM_LIMIT)
    return pltpu.CompilerParams(dimension_semantics=tuple(sem), vmem_limit_bytes=VMEM_LIMIT)


def _tile(n, prefs):
    for p in prefs:
        if n % p == 0:
            return p
    return n


def _mm(a, b):
    return jnp.dot(a.astype(BF16), b.astype(BF16), preferred_element_type=F32)


def _mm_nt(a, b):
    return lax.dot_general(a.astype(BF16), b.astype(BF16), (((1,), (1,)), ((), ())), preferred_element_type=F32)


def _mm_tn(a, b):
    return lax.dot_general(a.astype(BF16), b.astype(BF16), (((0,), (0,)), ((), ())), preferred_element_type=F32)


def _mm_hi(a, b):
    return jnp.dot(a, b, precision=HIGHEST, preferred_element_type=F32)


def _mm_3x(a, b):
    return jnp.dot(a, b, precision=lax.Precision.HIGH, preferred_element_type=F32)


_GELU_C = 0.7978845608028654
_GELU_A = 0.044715


def _gelu(x):
    return 0.5 * x * (1.0 + jnp.tanh(_GELU_C * (x + _GELU_A * x * x * x)))


def _gelu_grad(x):
    t = jnp.tanh(_GELU_C * (x + _GELU_A * x * x * x))
    return 0.5 * (1.0 + t) + 0.5 * x * (1.0 - t * t) * _GELU_C * (1.0 + 3.0 * _GELU_A * x * x)


def _silu(x):
    return x * jax.nn.sigmoid(x)


def _silu_grad(x):
    s = jax.nn.sigmoid(x)
    return s * (1.0 + x * (1.0 - s))


def _rowsum(x):
    return jnp.sum(x, axis=-1, keepdims=True)


def _colsum(x):
    return jnp.sum(x, axis=0, keepdims=True)


def _iota2(shape, dim):
    return lax.broadcasted_iota(jnp.int32, shape, dim)


def _chunk_tri(tm, upper):
    r = _iota2((tm, tm), 0)
    c = _iota2((tm, tm), 1)
    same = lax.shift_right_logical(r, 6) == lax.shift_right_logical(c, 6)
    tri = (r <= c) if upper else (r >= c)
    return jnp.where(same & tri, 1.0, 0.0).astype(F32)


N_CHIP = 4


def _mesh_place():
    x, y, c = lax.axis_index("x"), lax.axis_index("y"), lax.axis_index("c")
    chips = [(1 - x, y), (x, 1 - y), (1 - x, 1 - y)]
    return x, y, c, (x, y, 1 - c), chips


class _Gather:
    def __init__(self, ins, outs, send_sems, recv_sems, loc_sems):
        self.ins, self.outs, self.send_sems, self.recv_sems, self.loc_sems = ins, outs, send_sems, recv_sems, loc_sems
        self.x, self.y, self.c, self.sib, self.chips = _mesh_place()
        self.me = (self.x, self.y, self.c)
        north = self.c == 1
        self.relay_from = (jnp.where(north, 1 - self.x, self.x), jnp.where(north, self.y, 1 - self.y))
        self.relay_to = (jnp.where(north, self.x, 1 - self.x), jnp.where(north, 1 - self.y, self.y))

    def copy(self, a, k, block, to, src=None):
        slot = self.outs[a].at[4 * block[0] + 2 * block[1] + block[2]]
        return pltpu.make_async_remote_copy(
            src_ref=slot if src is None else src, dst_ref=slot, send_sem=self.send_sems.at[a, k],
            recv_sem=self.recv_sems.at[a, k], device_id=to, device_id_type=MESH_ID)

    def own(self, a):
        return pltpu.make_async_copy(self.ins[a], self.outs[a].at[4 * self.x + 2 * self.y + self.c], self.loc_sems.at[a])

    def first(self, a):
        return [self.copy(a, 0, self.me, self.sib, src=self.ins[a])] + [
            self.copy(a, 1 + j, self.me, (*self.chips[j], self.c), src=self.ins[a]) for j in range(2)]

    def relayed(self, a):
        return self.copy(a, 3, (*self.relay_from, self.c), (*self.relay_to, self.c))

    def passed(self, a, j):
        return self.copy(a, 4 + j, (*self.chips[j], self.c), self.sib)

    def start(self):
        for a in range(len(self.ins)):
            self.own(a).start()
            for cp in self.first(a):
                cp.start()

    def relay(self):
        for a in range(len(self.ins)):
            for j in range(2):
                self.copy(a, 1 + j, (*self.chips[j], self.c), self.me).wait_recv()
            self.relayed(a).start()
            for j in range(2):
                self.passed(a, j).start()

    def finish(self):
        n = len(self.ins)
        for a in range(n):
            self.copy(a, 3, (*self.chips[2], self.c), self.me).wait_recv()
            self.passed(a, 2).start()
        for a in range(n):
            self.copy(a, 0, self.sib, self.me).wait_recv()
            for j, chip in enumerate(self.chips):
                self.copy(a, 4 + j, (*chip, 1 - self.c), self.me).wait_recv()
        for a in range(n):
            for cp in self.first(a) + [self.relayed(a)] + [self.passed(a, j) for j in range(N_CHIP - 1)]:
                cp.wait_send()
            self.own(a).wait()

    @staticmethod
    def sems(n):
        return [pltpu.SemaphoreType.DMA((n, N_DEV - 1)), pltpu.SemaphoreType.DMA((n, N_DEV - 1)),
                pltpu.SemaphoreType.DMA((n,))]


def _gather_two_level(arrs, name):
    n = len(arrs)

    def body(*refs):
        g = _Gather(refs[:n], refs[n:2 * n], *refs[2 * n:])
        g.start()
        g.relay()
        g.finish()

    any_spec = pl.BlockSpec(memory_space=pl.ANY)
    return pl.pallas_call(
        body, name=name, out_shape=[_sds((N_DEV,) + a.shape, a.dtype) for a in arrs],
        in_specs=[any_spec] * n, out_specs=[any_spec] * n, scratch_shapes=_Gather.sems(n),
        compiler_params=pltpu.CompilerParams(has_side_effects=True),
    )(*arrs)


def _swap_halves(small, grads, name):
    n = len(grads)

    def body(*refs):
        small_ref = refs[0]
        ins = refs[1:1 + n]
        small_out = refs[1 + n]
        got = refs[2 + n:2 + 2 * n]
        s_send, s_recv, g_send, g_recv, loc_sem = refs[2 + 2 * n:]
        x, y, c, sib, _ = _mesh_place()
        me = 4 * x + 2 * y + c
        sends, recvs = [], []
        for j in range(1, N_DEV):
            px = 1 - x if (j >> 2) & 1 else x
            py = 1 - y if (j >> 1) & 1 else y
            pc = 1 - c if j & 1 else c
            cp = pltpu.make_async_remote_copy(
                src_ref=small_ref, dst_ref=small_out.at[me], send_sem=s_send.at[j - 1], recv_sem=s_recv.at[j - 1],
                device_id=(px, py, pc), device_id_type=MESH_ID)
            cp.start()
            sends.append(cp)
            recvs.append(pltpu.make_async_remote_copy(
                src_ref=small_ref, dst_ref=small_out.at[4 * px + 2 * py + pc], send_sem=s_send.at[j - 1],
                recv_sem=s_recv.at[j - 1], device_id=(px, py, pc), device_id_type=MESH_ID))
        own = pltpu.make_async_copy(small_ref, small_out.at[me], loc_sem)
        own.start()
        for a in range(n):
            for chip in range(N_CHIP):
                cp = pltpu.make_async_remote_copy(
                    src_ref=ins[a].at[2 * chip + 1 - c], dst_ref=got[a].at[chip], send_sem=g_send.at[a, chip],
                    recv_sem=g_recv.at[a, chip], device_id=sib, device_id_type=MESH_ID)
                cp.start()
                sends.append(cp)
                recvs.append(cp)
        for cp in sends:
            cp.wait_send()
        for cp in recvs:
            cp.wait_recv()
        own.wait()

    half = [_sds((N_CHIP,) + g.shape[1:], g.dtype) for g in grads]
    any_spec = pl.BlockSpec(memory_space=pl.ANY)
    res = pl.pallas_call(
        body, name=name, out_shape=[_sds((N_DEV,) + small.shape, small.dtype)] + half,
        in_specs=[any_spec] * (1 + n), out_specs=[any_spec] * (1 + n),
        scratch_shapes=[pltpu.SemaphoreType.DMA((N_DEV - 1,)), pltpu.SemaphoreType.DMA((N_DEV - 1,)),
                        pltpu.SemaphoreType.DMA((n, N_CHIP)), pltpu.SemaphoreType.DMA((n, N_CHIP)),
                        pltpu.SemaphoreType.DMA],
        compiler_params=pltpu.CompilerParams(has_side_effects=True),
    )(small, *grads)
    return res[0], res[1:]


def _pair_sums(core, mine, got):
    n = len(got)

    def body(core_ref, *refs):
        for a in range(n):
            refs[2 * n + a][...] = (refs[a][...].astype(F32) + refs[n + a][...].astype(F32)).astype(BF16)

    half = lambda g: (1, g.shape[1] // 2, g.shape[2])
    own = [pl.BlockSpec(half(g), lambda i, j, core_ref: (2 * i + core_ref[0], j, 0)) for g in got]
    slot = [pl.BlockSpec(half(g), lambda i, j, core_ref: (i, j, 0)) for g in got]
    return pl.pallas_call(
        body, name="pair_sums", out_shape=[_sds(g.shape, BF16) for g in got],
        grid_spec=pltpu.PrefetchScalarGridSpec(
            num_scalar_prefetch=1, grid=(N_CHIP, 2), in_specs=own + slot, out_specs=slot),
        compiler_params=_params(("parallel", "parallel")),
    )(core, *mine, *got)


class _ChipExchange:
    def __init__(self, ins, outs, send_sems, recv_sems, loc_sems):
        self.ins, self.outs, self.send_sems, self.recv_sems, self.loc_sems = ins, outs, send_sems, recv_sems, loc_sems
        self.x, self.y, self.c, _, self.chips = _mesh_place()
        self.mine = 2 * self.x + self.y

    def own(self, a):
        return pltpu.make_async_copy(self.ins[a].at[self.mine], self.outs[a].at[self.mine], self.loc_sems.at[a])

    def copy(self, a, j, lands_in):
        chip = self.chips[j]
        return pltpu.make_async_remote_copy(
            src_ref=self.ins[a].at[2 * chip[0] + chip[1]], dst_ref=self.outs[a].at[lands_in],
            send_sem=self.send_sems.at[a, j], recv_sem=self.recv_sems.at[a, j], device_id=(*chip, self.c),
            device_id_type=MESH_ID)

    def start(self):
        for a in range(len(self.ins)):
            self.own(a).start()
            for j in range(N_CHIP - 1):
                self.copy(a, j, self.mine).start()

    def finish(self):
        for a in range(len(self.ins)):
            for j, chip in enumerate(self.chips):
                self.copy(a, j, self.mine).wait_send()
                self.copy(a, j, 2 * chip[0] + chip[1]).wait_recv()
            self.own(a).wait()

    @staticmethod
    def sems(n):
        return [pltpu.SemaphoreType.DMA((n, N_CHIP - 1)), pltpu.SemaphoreType.DMA((n, N_CHIP - 1)),
                pltpu.SemaphoreType.DMA((n,))]


def _inproj(x, ln_g, w_main, w_ab, late):
    s, d = x.shape
    n = w_main.shape[1]
    tm = _tile(s, (512, 256, 128))
    tn = _tile(n, (1664, 512, 128))
    nl = len(late)
    ni, nj = s // tm, n // tn

    def body(*refs):
        x_ref, g_ref, w_ref, wab_ref = refs[:4]
        proj_ref, ab_ref, ht_ref = refs[4 + nl:7 + nl]
        hs = refs[7 + 2 * nl]
        gather = _Gather(refs[4:4 + nl], refs[7 + nl:7 + 2 * nl], *refs[8 + 2 * nl:])
        step = pl.program_id(0) * nj + pl.program_id(1)

        @pl.when(step == 0)
        def _():
            gather.start()

        @pl.when(pl.program_id(1) == 0)
        def _():
            xv = x_ref[...]
            r = lax.rsqrt(jnp.mean(xv * xv, axis=-1, keepdims=True) + EPS)
            hf = xv * r * g_ref[...]
            h = hf.astype(BF16)
            hs[...] = h
            ht_ref[...] = hf.T.astype(BF16)
            ab_ref[...] = jnp.dot(h, wab_ref[...], preferred_element_type=F32)

        proj_ref[...] = jnp.dot(hs[...], w_ref[...], preferred_element_type=F32)

        @pl.when(step == (ni * nj) // 2)
        def _():
            gather.relay()

        @pl.when(step == ni * nj - 1)
        def _():
            gather.finish()

    any_spec = pl.BlockSpec(memory_space=pl.ANY)
    res = pl.pallas_call(
        body, name="inproj", grid=(ni, nj),
        in_specs=[pl.BlockSpec((tm, d), lambda i, j: (i, 0)), pl.BlockSpec((1, d), lambda i, j: (0, 0)),
                  pl.BlockSpec((d, tn), lambda i, j: (0, j)), pl.BlockSpec((d, LANE), lambda i, j: (0, 0))]
        + [any_spec] * nl,
        out_specs=[pl.BlockSpec((tm, tn), lambda i, j: (i, j)), pl.BlockSpec((tm, LANE), lambda i, j: (i, 0)),
                   pl.BlockSpec((d, tm), lambda i, j: (0, i))] + [any_spec] * nl,
        out_shape=[_sds((s, n)), _sds((s, LANE)), _sds((d, s), BF16)]
        + [_sds((N_DEV,) + a.shape, a.dtype) for a in late],
        scratch_shapes=[pltpu.VMEM((tm, d), BF16)] + _Gather.sems(nl),
        compiler_params=_params(("arbitrary", "arbitrary")),
    )(x, ln_g, w_main, w_ab, *late)
    return res[0], res[1], res[2], res[3:]


def _matmul_acc(a, b, name):
    m, k = a.shape
    n = b.shape[1]
    tm = _tile(m, (2048, 1024, 512, 256, 128))
    tn = _tile(n, (1024, 512, 256, 128))
    tk = _tile(k, (1024, 512, 256, 128))
    nk = k // tk

    def body(a_ref, b_ref, o_ref, acc):
        @pl.when(pl.program_id(2) == 0)
        def _():
            acc[...] = jnp.zeros_like(acc)

        acc[...] += jnp.dot(a_ref[...], b_ref[...], preferred_element_type=F32)

        @pl.when(pl.program_id(2) == nk - 1)
        def _():
            o_ref[...] = acc[...].astype(BF16)

    return pl.pallas_call(
        body, name=name, grid=(m // tm, n // tn, nk),
        in_specs=[pl.BlockSpec((tm, tk), lambda i, j, l: (i, l)), pl.BlockSpec((tk, tn), lambda i, j, l: (l, j))],
        out_specs=pl.BlockSpec((tm, tn), lambda i, j, l: (i, j)),
        out_shape=_sds((m, n), BF16), scratch_shapes=[pltpu.VMEM((tm, tn), F32)],
        compiler_params=_params(("parallel", "parallel", "arbitrary")),
    )(a, b)


def _matmul_tn(a, b, name):
    k, m = a.shape
    n = b.shape[1]
    tm = _tile(m, (1024, 512, 256, 128))
    tn = _tile(n, (1024, 512, 256, 128))
    tk = _tile(k, (1024, 512, 256, 128))
    nk = k // tk

    def body(a_ref, b_ref, o_ref, acc):
        @pl.when(pl.program_id(2) == 0)
        def _():
            acc[...] = jnp.zeros_like(acc)

        acc[...] += _mm_tn(a_ref[...], b_ref[...])

        @pl.when(pl.program_id(2) == nk - 1)
        def _():
            o_ref[...] = acc[...].astype(BF16)

    return pl.pallas_call(
        body, name=name, grid=(m // tm, n // tn, nk),
        in_specs=[pl.BlockSpec((tk, tm), lambda i, j, l: (l, i)), pl.BlockSpec((tk, tn), lambda i, j, l: (l, j))],
        out_specs=pl.BlockSpec((tm, tn), lambda i, j, l: (i, j)),
        out_shape=_sds((m, n), BF16), scratch_shapes=[pltpu.VMEM((tm, tn), F32)],
        compiler_params=_params(("parallel", "parallel", "arbitrary")),
    )(a, b)


def _dh_rms(pieces, w_rows, wab_rows, x, dx2, ln_g, chip_sums):
    s, d = x.shape
    npc = len(pieces)
    nx = len(chip_sums)
    tm = _tile(s, (256, 128))
    ni = s // tm
    widths = [p.shape[1] for p in pieces[:-1]]
    offs = [sum(widths[:p]) for p in range(npc - 1)]
    nw = len(w_rows)
    nin = npc + 2 * nw + 3

    def body(*refs):
        p_refs = refs[:npc]
        w_refs = refs[npc:npc + nw]
        wab_refs = refs[npc + nw:npc + 2 * nw]
        x_ref, dx2_ref, g_ref = refs[npc + 2 * nw:nin]
        gx_ref, dg_ref = refs[nin + nx:nin + nx + 2]
        exch = _ChipExchange(refs[nin:nin + nx], refs[nin + nx + 2:nin + 2 * nx + 2], *refs[nin + 2 * nx + 2:])
        step = pl.program_id(0)

        @pl.when(step == 0)
        def _():
            dg_ref[...] = jnp.zeros_like(dg_ref)
            exch.start()

        cols = []
        for w_ref, wab_ref in zip(w_refs, wab_refs):
            part = _mm_nt(p_refs[npc - 1][...], wab_ref[...])
            for p in range(npc - 1):
                part += _mm_nt(p_refs[p][...], w_ref[:, offs[p]:offs[p] + widths[p]])
            cols.append(part)
        dhv = jnp.concatenate(cols, axis=1)
        xv = x_ref[...]
        r = lax.rsqrt(jnp.mean(xv * xv, axis=-1, keepdims=True) + EPS)
        xhat = xv * r
        dg_ref[...] += _colsum(dhv * xhat)
        dxh = dhv * g_ref[...]
        gx_ref[...] = dx2_ref[...] + r * (dxh - xhat * jnp.mean(dxh * xhat, axis=-1, keepdims=True))

        @pl.when(step == ni - 1)
        def _():
            exch.finish()

    any_spec = pl.BlockSpec(memory_space=pl.ANY)
    row = pl.BlockSpec((tm, d), lambda i: (i, 0))
    vec = pl.BlockSpec((1, d), lambda i: (0, 0))
    once = lambda a: pl.BlockSpec(a.shape, lambda i: (0, 0), pipeline_mode=pl.Buffered(1))
    in_specs = [pl.BlockSpec((tm, p.shape[1]), lambda i: (i, 0)) for p in pieces]
    in_specs += [once(w) for w in w_rows] + [once(w) for w in wab_rows] + [row, row, vec] + [any_spec] * nx
    res = pl.pallas_call(
        body, name="dh_rms", grid=(ni,), in_specs=in_specs,
        out_specs=[row, vec] + [any_spec] * nx,
        out_shape=[_sds((s, d)), _sds((1, d))] + [_sds(p.shape, p.dtype) for p in chip_sums],
        scratch_shapes=_ChipExchange.sems(nx),
        compiler_params=_params(("arbitrary",)),
    )(*pieces, *w_rows, *wab_rows, x, dx2, ln_g, *chip_sums)
    return res[0], res[1], res[2:]


def _final(x, tgt, out_b, out_a, out_c, w_out, final_g):
    s, d = x.shape
    tm = _tile(s, (256, 128))

    def body(x_ref, t_ref, b_ref, a_ref, c_ref, w_ref, g_ref, dx2_ref, dx2b_ref, dm_ref, loss_ref, dg_ref):
        @pl.when(pl.program_id(0) == 0)
        def _():
            loss_ref[...] = jnp.zeros_like(loss_ref)
            dg_ref[...] = jnp.zeros_like(dg_ref)

        x2 = x_ref[...]
        x2 += jnp.dot(b_ref[...], w_ref[0:DN_W, :], preferred_element_type=F32)
        x2 += jnp.dot(a_ref[...], w_ref[DN_W:DN_W + GMLP_W, :], preferred_element_type=F32)
        x2 += jnp.dot(c_ref[...], w_ref[DN_W + GMLP_W:MIX_W, :], preferred_element_type=F32)
        r = lax.rsqrt(jnp.mean(x2 * x2, axis=-1, keepdims=True) + EPS)
        xhat = x2 * r
        g = g_ref[...]
        err = xhat * g - t_ref[...]
        tok = 0.5 * jnp.mean(err * err, axis=-1, keepdims=True)
        loss_ref[...] += jnp.broadcast_to(_colsum(tok), loss_ref.shape)
        dy = err * (1.0 / d)
        dg_ref[...] += _colsum(dy * xhat)
        dxh = dy * g
        dx2 = r * (dxh - xhat * jnp.mean(dxh * xhat, axis=-1, keepdims=True))
        dx2_ref[...] = dx2
        dx2b = dx2.astype(BF16)
        dx2b_ref[...] = dx2b
        dm_ref[...] = _mm_nt(dx2b, w_ref[...])

    row = pl.BlockSpec((tm, d), lambda i: (i, 0))
    vec = pl.BlockSpec((1, d), lambda i: (0, 0))
    return pl.pallas_call(
        body, name="final", grid=(s // tm,),
        in_specs=[row, row, pl.BlockSpec((tm, DN_W), lambda i: (i, 0)), pl.BlockSpec((tm, GMLP_W), lambda i: (i, 0)),
                  pl.BlockSpec((tm, XA_W), lambda i: (i, 0)), pl.BlockSpec((MIX_W, d), lambda i: (0, 0)), vec],
        out_specs=[row, row, pl.BlockSpec((tm, MIX_W), lambda i: (i, 0)), pl.BlockSpec((1, LANE), lambda i: (0, 0)), vec],
        out_shape=[_sds((s, d)), _sds((s, d), BF16), _sds((s, MIX_W)), _sds((1, LANE)), _sds((1, d))],
        compiler_params=_params(("arbitrary",)),
    )(x, tgt, out_b, out_a, out_c, w_out, final_g)


GU_BLK = (4 * DN_W) // GMLP_W


def _gmlp_norm(gv, lng, lnb):
    va = _gelu(gv)
    mu = jnp.mean(va, axis=-1, keepdims=True)
    xc = va - mu
    rstd = lax.rsqrt(jnp.mean(xc * xc, axis=-1, keepdims=True) + EPS)
    vhat = xc * rstd
    return vhat, rstd, vhat * lng + lnb


def _gmlp_fwd(proj, lng, lnb, ws, bs_t):
    s = proj.shape[0]
    tm = _tile(s, (512, 256, 128))

    def body(u_ref, v_ref, z_ref, lng_ref, lnb_ref, ws_ref, bst_ref, o_ref):
        _, _, vn = _gmlp_norm(v_ref[...], lng_ref[...], lnb_ref[...])
        tri = _iota2((GMLP_T, GMLP_T), 0) >= _iota2((GMLP_T, GMLP_T), 1)
        for g in range(GMLP_G):
            cs = slice(g * HEAD, (g + 1) * HEAD)
            w = jnp.where(tri, ws_ref[g], 0.0).astype(BF16)
            b = bst_ref[:, g:g + 1]
            for c in range(tm // GMLP_T):
                rs = slice(c * GMLP_T, (c + 1) * GMLP_T)
                sg = _mm(w, vn[rs, cs]) + b
                o_ref[rs, cs] = (_gelu(u_ref[rs, cs]) * sg * _silu(z_ref[rs, cs])).astype(BF16)

    col = lambda k: pl.BlockSpec((tm, GMLP_W), lambda i: (i, GU_BLK + k))
    vec = pl.BlockSpec((1, GMLP_W), lambda i: (0, 0))
    return pl.pallas_call(
        body, name="gmlp_fwd", grid=(s // tm,),
        in_specs=[col(0), col(1), col(2), vec, vec, pl.BlockSpec((GMLP_G, GMLP_T, GMLP_T), lambda i: (0, 0, 0)),
                  pl.BlockSpec((GMLP_T, GMLP_G), lambda i: (0, 0))],
        out_specs=pl.BlockSpec((tm, GMLP_W), lambda i: (i, 0)), out_shape=_sds((s, GMLP_W), BF16),
        compiler_params=_params(("parallel",)),
    )(proj, proj, proj, lng, lnb, ws, bs_t)


def _gmlp_bwd(proj, dmixed, lng, lnb, ws, bs_t):
    s = proj.shape[0]
    tm = _tile(s, (512, 256, 128))

    def body(u_ref, v_ref, z_ref, d_ref, lng_ref, lnb_ref, ws_ref, bst_ref,
             dp_ref, dws_ref, dbst_ref, dlng_ref, dlnb_ref, dvn):
        @pl.when(pl.program_id(0) == 0)
        def _():
            dws_ref[...] = jnp.zeros_like(dws_ref)
            dbst_ref[...] = jnp.zeros_like(dbst_ref)
            dlng_ref[...] = jnp.zeros_like(dlng_ref)
            dlnb_ref[...] = jnp.zeros_like(dlnb_ref)

        gv = v_ref[...]
        lng_v = lng_ref[...]
        vhat, rstd, vn = _gmlp_norm(gv, lng_v, lnb_ref[...])
        tri = _iota2((GMLP_T, GMLP_T), 0) >= _iota2((GMLP_T, GMLP_T), 1)
        for g in range(GMLP_G):
            cs = slice(g * HEAD, (g + 1) * HEAD)
            w = jnp.where(tri, ws_ref[g], 0.0).astype(BF16)
            b = bst_ref[:, g:g + 1]
            dw_acc = jnp.zeros((GMLP_T, GMLP_T), F32)
            db_acc = jnp.zeros((GMLP_T, 1), F32)
            for c in range(tm // GMLP_T):
                rs = slice(c * GMLP_T, (c + 1) * GMLP_T)
                vn_b = vn[rs, cs]
                sg = _mm(w, vn_b) + b
                gu = u_ref[rs, cs]
                gz = z_ref[rs, cs]
                da = d_ref[rs, cs]
                uact = _gelu(gu)
                sz = _silu(gz)
                ds = da * uact * sz
                dp_ref[rs, cs] = (da * sg * sz * _gelu_grad(gu)).astype(BF16)
                dp_ref[rs, 2 * GMLP_W + g * HEAD:2 * GMLP_W + (g + 1) * HEAD] = (da * uact * sg * _silu_grad(gz)).astype(BF16)
                dw_acc += _mm_nt(ds, vn_b)
                db_acc += _rowsum(ds)
                dvn[rs, cs] = _mm_tn(w, ds)
            dws_ref[g] += jnp.where(tri, dw_acc, 0.0)
            dbst_ref[:, g:g + 1] += db_acc
        dvn_v = dvn[...]
        dlng_ref[...] += _colsum(dvn_v * vhat)
        dlnb_ref[...] += _colsum(dvn_v)
        dvh = dvn_v * lng_v
        dva = rstd * (dvh - jnp.mean(dvh, axis=-1, keepdims=True) - vhat * jnp.mean(dvh * vhat, axis=-1, keepdims=True))
        dp_ref[:, GMLP_W:2 * GMLP_W] = (dva * _gelu_grad(gv)).astype(BF16)

    col = lambda k: pl.BlockSpec((tm, GMLP_W), lambda i: (i, GU_BLK + k))
    vec = pl.BlockSpec((1, GMLP_W), lambda i: (0, 0))
    wsp = pl.BlockSpec((GMLP_G, GMLP_T, GMLP_T), lambda i: (0, 0, 0))
    bsp = pl.BlockSpec((GMLP_T, GMLP_G), lambda i: (0, 0))
    return pl.pallas_call(
        body, name="gmlp_bwd", grid=(s // tm,),
        in_specs=[col(0), col(1), col(2), pl.BlockSpec((tm, GMLP_W), lambda i: (i, DN_W // GMLP_W)), vec, vec, wsp, bsp],
        out_specs=[pl.BlockSpec((tm, 3 * GMLP_W), lambda i: (i, 0)), wsp, bsp, vec, vec],
        out_shape=[_sds((s, 3 * GMLP_W), BF16), _sds((GMLP_G, GMLP_T, GMLP_T)), _sds((GMLP_T, GMLP_G)),
                   _sds((1, GMLP_W)), _sds((1, GMLP_W))],
        scratch_shapes=[pltpu.VMEM((tm, GMLP_W), F32)],
        compiler_params=_params(("arbitrary",)),
    )(proj, proj, proj, dmixed, lng, lnb, ws, bs_t)


CQ_BLK = (4 * DN_W + 3 * GMLP_W) // XA_W


def _memkv_fwd(mem, g, w_kv):
    nm, d = mem.shape

    def body(m_ref, g_ref, w_ref, kv_ref):
        mv = m_ref[...]
        r = lax.rsqrt(jnp.mean(mv * mv, axis=-1, keepdims=True) + EPS)
        kv_ref[...] = _mm(mv * r * g_ref[...], w_ref[...])

    return pl.pallas_call(body, name="memkv_fwd", out_shape=_sds((nm, 2 * XA_W)), compiler_params=_params())(mem, g, w_kv)


def _memkv_bwd(mem, g, w_kv, dkv):
    nm, d = mem.shape

    def body(m_ref, g_ref, w_ref, dkv_ref, dw_ref, dg_ref):
        mv = m_ref[...]
        r = lax.rsqrt(jnp.mean(mv * mv, axis=-1, keepdims=True) + EPS)
        xhat = mv * r
        dkv_v = dkv_ref[...]
        dw_ref[...] = _mm_tn(xhat * g_ref[...], dkv_v)
        dg_ref[...] = _colsum(_mm_nt(dkv_v, w_ref[...]) * xhat)

    return pl.pallas_call(body, name="memkv_bwd", out_shape=[_sds((d, 2 * XA_W)), _sds((1, d))],
                          compiler_params=_params())(mem, g, w_kv, dkv)


def _xattn_probs(q, mk):
    sc = _mm_nt(q, mk) * (HEAD ** -0.5)
    e = jnp.exp(sc - jnp.max(sc, axis=-1, keepdims=True))
    return e / _rowsum(e)


def _xattn_fwd(proj, mkv):
    s = proj.shape[0]
    nm = mkv.shape[0]
    tm = _tile(s, (512, 256, 128))

    def body(q_ref, z_ref, kv_ref, o_ref):
        for h in range(XA_H):
            cs = slice(h * HEAD, (h + 1) * HEAD)
            p = _xattn_probs(q_ref[:, cs], kv_ref[:, cs])
            ctx = _mm(p, kv_ref[:, XA_W + h * HEAD:XA_W + (h + 1) * HEAD])
            o_ref[:, cs] = (ctx * _silu(z_ref[:, cs])).astype(BF16)

    col = lambda k: pl.BlockSpec((tm, XA_W), lambda i: (i, CQ_BLK + k))
    return pl.pallas_call(
        body, name="xattn_fwd", grid=(s // tm,),
        in_specs=[col(0), col(1), pl.BlockSpec((nm, 2 * XA_W), lambda i: (0, 0))],
        out_specs=pl.BlockSpec((tm, XA_W), lambda i: (i, 0)), out_shape=_sds((s, XA_W), BF16),
        compiler_params=_params(("parallel",)),
    )(proj, proj, mkv)


def _xattn_bwd(proj, dmixed, mkv):
    s = proj.shape[0]
    nm = mkv.shape[0]
    tm = _tile(s, (512, 256, 128))

    def body(q_ref, z_ref, d_ref, kv_ref, dp_ref, dkv_ref):
        @pl.when(pl.program_id(0) == 0)
        def _():
            dkv_ref[...] = jnp.zeros_like(dkv_ref)

        for h in range(XA_H):
            cs = slice(h * HEAD, (h + 1) * HEAD)
            vs = slice(XA_W + h * HEAD, XA_W + (h + 1) * HEAD)
            q = q_ref[:, cs]
            z = z_ref[:, cs]
            mk = kv_ref[:, cs]
            mv = kv_ref[:, vs]
            p = _xattn_probs(q, mk)
            ctx = _mm(p, mv)
            dc = d_ref[:, cs]
            dctx = dc * _silu(z)
            dp_ref[:, vs] = (dc * ctx * _silu_grad(z)).astype(BF16)
            dp = _mm_nt(dctx, mv)
            dkv_ref[:, vs] += _mm_tn(p, dctx)
            ds = p * (dp - _rowsum(dp * p)) * (HEAD ** -0.5)
            dp_ref[:, cs] = _mm(ds, mk).astype(BF16)
            dkv_ref[:, cs] += _mm_tn(ds, q)

    col = lambda k: pl.BlockSpec((tm, XA_W), lambda i: (i, CQ_BLK + k))
    kvs = pl.BlockSpec((nm, 2 * XA_W), lambda i: (0, 0))
    return pl.pallas_call(
        body, name="xattn_bwd", grid=(s // tm,),
        in_specs=[col(0), col(1), pl.BlockSpec((tm, XA_W), lambda i: (i, (DN_W + GMLP_W) // XA_W)), kvs],
        out_specs=[pl.BlockSpec((tm, 2 * XA_W), lambda i: (i, 0)), kvs],
        out_shape=[_sds((s, 2 * XA_W), BF16), _sds((nm, 2 * XA_W))],
        compiler_params=_params(("arbitrary",)),
    )(proj, proj, dmixed, mkv)


def _softplus(x):
    return jnp.maximum(x, 0.0) + jnp.log1p(jnp.exp(-jnp.abs(x)))


def _dn_pre(proj, ab, conv_w, alog_row, dt_row):
    s = proj.shape[0]
    tm = _tile(s, (256, 128))
    w3 = 3 * DN_W

    def body(x_ref, halo_ref, ab_ref, cw_ref, al_ref, dt_ref, q_ref, k_ref, v_ref, gb_ref, gbt_ref, yc_ref):
        i = pl.program_id(0)
        xv = x_ref[...]
        cat = jnp.concatenate([jnp.where(i > 0, halo_ref[...], 0.0), xv[0:HALO]], axis=0)
        yc = cw_ref[DN_K - 1:DN_K, :] * xv
        top = cw_ref[DN_K - 1:DN_K, :] * xv[0:HALO]
        for t in range(DN_K - 1):
            back = DN_K - 1 - t
            yc += cw_ref[t:t + 1, :] * pltpu.roll(xv, back, 0)
            top += cw_ref[t:t + 1, :] * pltpu.roll(cat, back, 0)[HALO:2 * HALO]
        yc = jnp.concatenate([top, yc[HALO:tm]], axis=0)
        yc_ref[...] = yc
        act = _silu(yc)
        for h in range(DN_H):
            cs = slice(h * HEAD, (h + 1) * HEAD)
            qa = act[:, cs]
            q_ref[:, cs] = qa * (lax.rsqrt(_rowsum(qa * qa) + EPS) * (HEAD ** -0.5))
            ka = act[:, DN_W + h * HEAD:DN_W + (h + 1) * HEAD]
            k_ref[:, cs] = ka * lax.rsqrt(_rowsum(ka * ka) + EPS)
        v_ref[...] = act[:, 2 * DN_W:w3]
        abv = ab_ref[...]
        lane = _iota2((tm, LANE), 1)
        g = jnp.where(lane < DN_H, -jnp.exp(al_ref[...]) * _softplus(abv + dt_ref[...]), 0.0)
        gc = _mm_hi(_chunk_tri(tm, False), g)
        gbv = jnp.where(lane < DN_H, gc, jnp.where(lane < 2 * DN_H, jax.nn.sigmoid(abv), 0.0))
        gb_ref[...] = gbv
        for c in range(tm // CH):
            gbt_ref[c] = gbv[c * CH:(c + 1) * CH, :].T[0:2 * DN_H, :]

    hb = tm // HALO
    row = lambda w: pl.BlockSpec((tm, w), lambda i: (i, 0))
    vec = pl.BlockSpec((1, LANE), lambda i: (0, 0))
    return pl.pallas_call(
        body, name="dn_pre", grid=(s // tm,),
        in_specs=[row(w3), pl.BlockSpec((HALO, w3), lambda i: (jnp.maximum(i * hb - 1, 0), 0)), row(LANE),
                  pl.BlockSpec((DN_K, w3), lambda i: (0, 0)), vec, vec],
        out_specs=[row(DN_W), row(DN_W), row(DN_W), row(LANE), pl.BlockSpec((tm // CH, 2 * DN_H, CH), lambda i: (i, 0, 0)),
                   row(w3)],
        out_shape=[_sds((s, DN_W)), _sds((s, DN_W)), _sds((s, DN_W)), _sds((s, LANE)),
                   _sds((s // CH, 2 * DN_H, CH)), _sds((s, w3))],
        compiler_params=_params(("parallel",)),
    )(proj, proj, ab, conv_w, alog_row, dt_row)


HEADS = tuple(range(DN_H))


def _hcols(h):
    return slice(h * HEAD, (h + 1) * HEAD)


def _chunk_scalings(k, v, gbv, gbt, h):
    gc = jnp.broadcast_to(gbv[:, h:h + 1], (CH, HEAD))
    beta = jnp.broadcast_to(gbv[:, DN_H + h:DN_H + h + 1], (CH, HEAD))
    gr = gbt[h:h + 1, :]
    ii = _iota2((CH, CH), 0)
    jj = _iota2((CH, CH), 1)
    dec = jnp.exp(jnp.where(ii >= jj, gc[:, 0:CH] - gr, -1e30))
    eg = jnp.exp(gc)
    gl = gr[:, CH - 1:CH]
    kb = k * beta
    return dict(beta=beta, dec=dec, eg=eg, gl=gl, ekd=jnp.exp(gl - gc), kb=kb, vb=v * beta, kbe=kb * eg)


def _chunk_scores(m, q, k):
    kq = _mm_nt(jnp.concatenate([m["kb"], q], axis=0), k)
    strict = _iota2((CH, CH), 0) > _iota2((CH, CH), 1)
    return jnp.where(strict, kq[0:CH] * m["dec"], 0.0), kq[CH:2 * CH] * m["dec"]


def _scan_cpb(s):
    return 8 if (s // CH) % 8 == 0 else 1


def _dn_fwd(q, k, v, gb, gbt, proj, norm_g):
    s = q.shape[0]
    cpb = _scan_cpb(s)
    tb = cpb * CH
    nblk = s // tb

    def body(q_ref, k_ref, v_ref, gb_ref, gbt_ref, z_ref, ng_ref,
             w_ref, qg_ref, kd_ref, t_ref, ai_ref, egl_ref, o_ref, vn_ref, st_ref, ob_ref, state):
        @pl.when(pl.program_id(0) == 0)
        def _():
            state[...] = jnp.zeros_like(state)

        ng = ng_ref[...]
        eye = jnp.where(_iota2((CH, CH), 0) == _iota2((CH, CH), 1), 1.0, 0.0).astype(F32)

        def chunk(c, carry):
            r0 = pl.multiple_of(c * CH, CH)
            rows = pl.ds(r0, CH)
            gbv = gb_ref[rows, :]
            gbt_v = gbt_ref[c]
            qs = [q_ref[rows, _hcols(h)] for h in HEADS]
            ks = [k_ref[rows, _hcols(h)] for h in HEADS]
            ms = [_chunk_scalings(ks[h], v_ref[rows, _hcols(h)], gbv, gbt_v, h) for h in HEADS]
            qgb = [(qs[h] * ms[h]["eg"]).astype(BF16) for h in HEADS]
            kdb = [(ks[h] * ms[h]["ekd"]).astype(BF16) for h in HEADS]
            egl = [jnp.broadcast_to(jnp.exp(ms[h]["gl"]), (1, LANE)) for h in HEADS]
            for h in HEADS:
                qg_ref[rows, _hcols(h)] = qgb[h]
                kd_ref[rows, _hcols(h)] = kdb[h]
                egl_ref[c, h:h + 1, :] = egl[h]
            sc = [_chunk_scores(ms[h], qs[h], ks[h]) for h in HEADS]
            for h in HEADS:
                ai_ref[h, rows, :] = sc[h][1]
            ts = [eye - sc[h][0] for h in HEADS]
            ps = [_mm_3x(sc[h][0], sc[h][0]) for h in HEADS]
            ts = [ts[h] + _mm_3x(ts[h], ps[h]) for h in HEADS]
            for _ in range(4):
                ps = [_mm(ps[h], ps[h]) for h in HEADS]
                ts = [ts[h] + _mm(ts[h], ps[h]) for h in HEADS]
            uw = [_mm(ts[h], jnp.concatenate([ms[h]["vb"], ms[h]["kbe"]], axis=1)) for h in HEADS]
            wb = [uw[h][:, HEAD:2 * HEAD].astype(BF16) for h in HEADS]
            for h in HEADS:
                t_ref[h, rows, :] = ts[h]
                w_ref[rows, _hcols(h)] = wb[h]
            sts = [state[h] for h in HEADS]
            stb = [sts[h].astype(BF16) for h in HEADS]
            for h in HEADS:
                st_ref[c, h] = stb[h]
            vnb = [(uw[h][:, 0:HEAD] - jnp.dot(wb[h], stb[h], preferred_element_type=F32)).astype(BF16) for h in HEADS]
            for h in HEADS:
                state[h] = sts[h] * egl[h] + _mm_tn(kdb[h], vnb[h])
            os_ = [jnp.dot(qgb[h], stb[h], preferred_element_type=F32) + _mm(sc[h][1], vnb[h]) for h in HEADS]
            for h in HEADS:
                o = os_[h]
                vn_ref[rows, _hcols(h)] = vnb[h]
                o_ref[rows, _hcols(h)] = o
                r = lax.rsqrt(jnp.mean(o * o, axis=-1, keepdims=True) + EPS)
                ob_ref[rows, _hcols(h)] = (o * r * ng * _silu(z_ref[rows, _hcols(h)])).astype(BF16)
            return carry

        lax.fori_loop(0, cpb, chunk, 0, unroll=4)

    row = pl.BlockSpec((tb, DN_W), lambda i: (i, 0))
    sq = pl.BlockSpec((DN_H, tb, CH), lambda i: (0, i, 0))
    return pl.pallas_call(
        body, name="dn_fwd", grid=(nblk,),
        in_specs=[row, row, row, pl.BlockSpec((tb, LANE), lambda i: (i, 0)),
                  pl.BlockSpec((cpb, 2 * DN_H, CH), lambda i: (i, 0, 0)), pl.BlockSpec((tb, DN_W), lambda i: (i, 3)),
                  pl.BlockSpec((1, HEAD), lambda i: (0, 0))],
        out_specs=[row, row, row, sq, sq, pl.BlockSpec((cpb, DN_H, LANE), lambda i: (i, 0, 0)), row, row,
                   pl.BlockSpec((cpb, DN_H, HEAD, HEAD), lambda i: (i, 0, 0, 0)), row],
        out_shape=[_sds((s, DN_W), BF16), _sds((s, DN_W), BF16), _sds((s, DN_W), BF16), _sds((DN_H, s, CH)),
                   _sds((DN_H, s, CH)), _sds((s // CH, DN_H, LANE)), _sds((s, DN_W)), _sds((s, DN_W), BF16),
                   _sds((s // CH, DN_H, HEAD, HEAD), BF16), _sds((s, DN_W), BF16)],
        scratch_shapes=[pltpu.VMEM((DN_H, HEAD, HEAD), F32)],
        compiler_params=_params(("arbitrary",)),
    )(q, k, v, gb, gbt, proj, norm_g)


def _dn_bwd(dmixed, o, proj, norm_g, w, qg, kd, ai, egl, q, k, v, gb, gbt, t, vn, st):
    s = o.shape[0]
    cpb = 4 if (s // CH) % 4 == 0 else 1
    tb = cpb * CH
    nblk = s // tb

    def body(dm_ref, o_ref, z_ref, ng_ref, w_ref, qg_ref, kd_ref, ai_ref, egl_ref,
             q_ref, k_ref, v_ref, gb_ref, gbt_ref, t_ref, vn_ref, st_ref,
             dq_ref, dk_ref, dv_ref, dgb_ref, dz_ref, dng_ref, dstate):
        @pl.when(pl.program_id(0) == 0)
        def _():
            dstate[...] = jnp.zeros_like(dstate)
            dng_ref[...] = jnp.zeros_like(dng_ref)

        ng = ng_ref[...]
        lane = _iota2((CH, LANE), 1)
        last = _iota2((CH, 1), 0) == CH - 1
        strict = _iota2((CH, CH), 0) > _iota2((CH, CH), 1)

        def chunk(cc, carry):
            c = cpb - 1 - cc
            r0 = pl.multiple_of(c * CH, CH)
            rows = pl.ds(r0, CH)
            dng = jnp.zeros((1, HEAD), F32)
            dob = []
            for h in HEADS:
                cs = _hcols(h)
                ov = o_ref[rows, cs]
                z = z_ref[rows, cs]
                db = dm_ref[rows, cs]
                r = lax.rsqrt(jnp.mean(ov * ov, axis=-1, keepdims=True) + EPS)
                ohat = ov * r
                dz_ref[rows, cs] = (db * ohat * ng * _silu_grad(z)).astype(BF16)
                dyn = db * _silu(z)
                dng += _colsum(dyn * ohat)
                doh = dyn * ng
                dob.append((r * (doh - ohat * jnp.mean(doh * ohat, axis=-1, keepdims=True))).astype(BF16))
            dng_ref[...] += dng
            dsn = [dstate[h] for h in HEADS]
            dsb = [dsn[h].astype(BF16) for h in HEADS]
            dvnb = [(_mm_tn(ai_ref[h, rows, :], dob[h])
                     + jnp.dot(kd_ref[rows, _hcols(h)], dsb[h], preferred_element_type=F32)).astype(BF16) for h in HEADS]
            part = [_mm_tn(qg_ref[rows, _hcols(h)], dob[h]) + egl_ref[c, h:h + 1, :] * dsn[h] for h in HEADS]
            for h in HEADS:
                dstate[h] = part[h] - _mm_tn(w_ref[rows, _hcols(h)], dvnb[h])
            gbv = gb_ref[rows, :]
            gbt_v = gbt_ref[c]
            qs = [q_ref[rows, _hcols(h)] for h in HEADS]
            ks = [k_ref[rows, _hcols(h)] for h in HEADS]
            vs = [v_ref[rows, _hcols(h)] for h in HEADS]
            ms = [_chunk_scalings(ks[h], vs[h], gbv, gbt_v, h) for h in HEADS]
            sts = [st_ref[c, h] for h in HEADS]
            vnb = [vn_ref[rows, _hcols(h)] for h in HEADS]
            tbf = [t_ref[h, rows, :].astype(BF16) for h in HEADS]
            sc = [_chunk_scores(ms[h], qs[h], ks[h]) for h in HEADS]
            xs_ = [_mm_nt(jnp.concatenate([dob[h], dvnb[h]], axis=0), sts[h]) for h in HEADS]
            dai = [_mm_nt(dob[h], vnb[h]) for h in HEADS]
            dkd = [_mm_nt(vnb[h], dsb[h]) for h in HEADS]
            dqg = [xs_[h][0:CH] for h in HEADS]
            duw = [jnp.concatenate([dvnb[h], (-xs_[h][CH:2 * CH]).astype(BF16)], axis=1) for h in HEADS]
            dt = [_mm_nt(duw[h], jnp.concatenate([ms[h]["vb"], ms[h]["kbe"]], axis=1)) for h in HEADS]
            dvk = [_mm_tn(tbf[h], duw[h]) for h in HEADS]
            tdt = [_mm_tn(tbf[h], dt[h]) for h in HEADS]
            da = [jnp.where(strict, -_mm_nt(tdt[h], tbf[h]), 0.0) for h in HEADS]
            dsc = [jnp.concatenate([da[h] * ms[h]["dec"], dai[h] * ms[h]["dec"]], axis=0) for h in HEADS]
            dkq = [_mm(dsc[h], ks[h]) for h in HEADS]
            dk1 = [_mm_tn(dsc[h], jnp.concatenate([ms[h]["kb"], qs[h]], axis=0)) for h in HEADS]
            dgb = jnp.zeros((CH, LANE), F32)
            for h in HEADS:
                m = ms[h]
                eg, ekd, beta = m["eg"], m["ekd"], m["beta"]
                dvb = dvk[h][:, 0:HEAD]
                dkbe = dvk[h][:, HEAD:2 * HEAD]
                kdv = ks[h] * ekd
                dkb = dkq[h][0:CH] + dkbe * eg
                dq_ref[rows, _hcols(h)] = dkq[h][CH:2 * CH] + dqg[h] * eg
                dk_ref[rows, _hcols(h)] = dk1[h] + dkd[h] * ekd + dkb * beta
                dv_ref[rows, _hcols(h)] = dvb * beta
                dkd_kd = dkd[h] * kdv
                dgl = (jnp.exp(m["gl"]) * _rowsum(_colsum(sts[h].astype(F32) * dsb[h].astype(F32)))
                       + _rowsum(_colsum(dkd_kd)))
                mm_ = da[h] * sc[h][0] + dai[h] * sc[h][1]
                dgc = (_rowsum(mm_ - mm_.T) + _rowsum(dqg[h] * qs[h] * eg - dkd_kd + dkbe * m["kbe"])
                       + jnp.where(last, dgl, 0.0))
                dbeta = _rowsum(dkb * ks[h] + dvb * vs[h])
                dgb = jnp.where(lane == h, dgc, jnp.where(lane == DN_H + h, dbeta, dgb))
            dgb_ref[rows, :] = dgb
            return carry

        lax.fori_loop(0, cpb, chunk, 0, unroll=2)

    rev = lambda i: (nblk - 1 - i, 0)
    row = pl.BlockSpec((tb, DN_W), rev)
    vec = pl.BlockSpec((1, HEAD), lambda i: (0, 0))
    sq = pl.BlockSpec((DN_H, tb, CH), lambda i: (0, nblk - 1 - i, 0))
    gbs = pl.BlockSpec((tb, LANE), rev)
    return pl.pallas_call(
        body, name="dn_bwd", grid=(nblk,),
        in_specs=[row, row, pl.BlockSpec((tb, DN_W), lambda i: (nblk - 1 - i, 3)), vec, row, row, row, sq,
                  pl.BlockSpec((cpb, DN_H, LANE), lambda i: (nblk - 1 - i, 0, 0)),
                  row, row, row, gbs, pl.BlockSpec((cpb, 2 * DN_H, CH), lambda i: (nblk - 1 - i, 0, 0)), sq, row,
                  pl.BlockSpec((cpb, DN_H, HEAD, HEAD), lambda i: (nblk - 1 - i, 0, 0, 0))],
        out_specs=[row, row, row, gbs, row, vec],
        out_shape=[_sds((s, DN_W)), _sds((s, DN_W)), _sds((s, DN_W)), _sds((s, LANE)), _sds((s, DN_W), BF16),
                   _sds((1, HEAD))],
        scratch_shapes=[pltpu.VMEM((DN_H, HEAD, HEAD), F32)],
        compiler_params=_params(("arbitrary",)),
    )(dmixed, o, proj, norm_g, w, qg, kd, ai, egl, q, k, v, gb, gbt, t, vn, st)


def _dn_pre_bwd(proj, yc_all, ab, conv_w, alog_row, dt_row, dq, dk, dv, dgb):
    s = proj.shape[0]
    tm = _tile(s, (256, 128))
    w3 = 3 * DN_W
    nblk = s // tm

    def body(x_ref, yc_ref, ab_ref, cw_ref, al_ref, dt_ref, dq_ref, dk_ref, dv_ref, dgb_ref,
             dx_ref, dab_ref, dcw_ref, dal_ref, ddt_ref, exd, carry):
        i = pl.program_id(0)

        @pl.when(i == 0)
        def _():
            carry[...] = jnp.zeros_like(carry)
            dcw_ref[...] = jnp.zeros_like(dcw_ref)
            dal_ref[...] = jnp.zeros_like(dal_ref)
            ddt_ref[...] = jnp.zeros_like(ddt_ref)

        yc = yc_ref[...]
        sg = jax.nn.sigmoid(yc)
        act = yc * sg
        dact = sg * (1.0 + yc * (1.0 - sg))
        for h in range(DN_H):
            cs = slice(h * HEAD, (h + 1) * HEAD)
            ks = slice(DN_W + h * HEAD, DN_W + (h + 1) * HEAD)
            qa = act[:, cs]
            rq = lax.rsqrt(_rowsum(qa * qa) + EPS)
            qh = qa * rq
            dqv = dq_ref[:, cs]
            exd[0:tm, cs] = (HEAD ** -0.5) * rq * (dqv - qh * _rowsum(dqv * qh)) * dact[:, cs]
            ka = act[:, ks]
            rk = lax.rsqrt(_rowsum(ka * ka) + EPS)
            kh = ka * rk
            dkv = dk_ref[:, cs]
            exd[0:tm, ks] = rk * (dkv - kh * _rowsum(dkv * kh)) * dact[:, ks]
        exd[0:tm, 2 * DN_W:w3] = dv_ref[...] * dact[:, 2 * DN_W:w3]
        xv = x_ref[...]
        dyc = exd[...]
        cat = jnp.concatenate([dyc[tm - HALO:tm], carry[...]], axis=0)
        dcw_ref[DN_K - 1:DN_K, :] += _colsum(dyc * xv)
        dx = cw_ref[DN_K - 1:DN_K, :] * dyc
        for t in range(DN_K - 1):
            ahead = DN_K - 1 - t
            view = jnp.concatenate([pltpu.roll(dyc, tm - ahead, 0)[0:tm - HALO],
                                    pltpu.roll(cat, 2 * HALO - ahead, 0)[0:HALO]], axis=0)
            dcw_ref[t:t + 1, :] += _colsum(view * xv)
            dx += cw_ref[t:t + 1, :] * view
        dx_ref[...] = dx.astype(BF16)
        carry[...] = dyc[0:HALO]

        lane = _iota2((tm, LANE), 1)
        dgbv = dgb_ref[...]
        dg = _mm_hi(_chunk_tri(tm, True), jnp.where(lane < DN_H, dgbv, 0.0))
        abv = ab_ref[...]
        xa = abv + dt_ref[...]
        nea = -jnp.exp(al_ref[...])
        d_da = jnp.where(lane < DN_H, dg * nea * jax.nn.sigmoid(xa), 0.0)
        dal_ref[...] += _colsum(jnp.where(lane < DN_H, dg * nea * _softplus(xa), 0.0))
        ddt_ref[...] += _colsum(d_da)
        beta = jax.nn.sigmoid(abv)
        d_db = jnp.where((lane >= DN_H) & (lane < 2 * DN_H), dgbv * beta * (1.0 - beta), 0.0)
        dab_ref[...] = (d_da + d_db).astype(BF16)

    rev = lambda i: (nblk - 1 - i, 0)
    row = lambda w: pl.BlockSpec((tm, w), rev)
    vec = pl.BlockSpec((1, LANE), lambda i: (0, 0))
    cws = pl.BlockSpec((DN_K, w3), lambda i: (0, 0))
    return pl.pallas_call(
        body, name="dn_pre_bwd", grid=(nblk,),
        in_specs=[row(w3), row(w3), row(LANE), cws, vec, vec, row(DN_W), row(DN_W), row(DN_W), row(LANE)],
        out_specs=[row(w3), row(LANE), cws, vec, vec],
        out_shape=[_sds((s, w3), BF16), _sds((s, LANE), BF16), _sds((DN_K, w3)), _sds((1, LANE)), _sds((1, LANE))],
        scratch_shapes=[pltpu.VMEM((tm, w3), F32), pltpu.VMEM((HALO, w3), F32)],
        compiler_params=_params(("arbitrary",)),
    )(proj, yc_all, ab, conv_w, alog_row, dt_row, dq, dk, dv, dgb)


def _adam(parts, w, m, v, name):
    r, c = w.shape
    n_parts = parts.shape[0]
    small = n_parts * r * c * 4 <= 4 * 1024 * 1024
    tr = r if small else _tile(r, (128, 64, 32, 16, 8))

    def body(p_ref, w_ref, m_ref, v_ref, g_ref, d_ref, nm_ref, nv_ref):
        g = p_ref[0].astype(F32)
        for k in range(1, n_parts):
            g = g + p_ref[k].astype(F32)
        g_ref[...] = g
        mn = ADAM_B1 * m_ref[...] + (1.0 - ADAM_B1) * g
        vn = ADAM_B2 * v_ref[...] + (1.0 - ADAM_B2) * (g * g)
        m_hat = mn / (1.0 - ADAM_B1 ** ADAM_STEP)
        v_hat = vn / (1.0 - ADAM_B2 ** ADAM_STEP)
        d_ref[...] = -ADAM_LR * (m_hat / (jnp.sqrt(v_hat) + ADAM_EPS) + ADAM_WD * w_ref[...])
        nm_ref[...] = mn
        nv_ref[...] = vn

    blk = pl.BlockSpec((tr, c), lambda i: (i, 0))
    return pl.pallas_call(
        body, name=name, grid=(r // tr,),
        in_specs=[pl.BlockSpec((n_parts, tr, c), lambda i: (0, i, 0)), blk, blk, blk],
        out_specs=[blk, blk, blk, blk], out_shape=[_sds((r, c))] * 4,
        compiler_params=_params(("parallel",)),
    )(parts, w, m, v)


_PACK_ROWS = 8


def _pack(vals):
    tiles = []
    for a in vals:
        flat = a.reshape(-1).astype(F32)
        unit = _PACK_ROWS * LANE
        n = -(-flat.shape[0] // unit) * unit
        tiles.append(jnp.pad(flat, (0, n - flat.shape[0])).reshape(n // LANE, LANE))
    return jnp.concatenate(tiles, axis=0)


def _unpack(packed, shapes):
    out = []
    r0 = 0
    for shp in shapes:
        size = 1
        for dim in shp:
            size *= dim
        unit = _PACK_ROWS * LANE
        rows = -(-size // unit) * _PACK_ROWS
        out.append(packed[r0:r0 + rows].reshape(-1)[:size].reshape(shp))
        r0 += rows
    return out


def _lane_row(vec8):
    return jnp.pad(vec8.reshape(1, -1).astype(F32), ((0, 0), (0, LANE - vec8.size)))


def kernel(x, mem, ln_g, w_in, gmlp_ln_g, gmlp_ln_b, gmlp_ws, gmlp_bs, conv_w, dn_a_log, dn_dt_bias, dn_norm_g, mem_norm_g, w_mem_kv, w_out, final_g, loss_target, m_ln_g, m_w_in, m_gmlp_ln_g, m_gmlp_ln_b, m_gmlp_ws, m_gmlp_bs, m_conv_w, m_dn_a_log, m_dn_dt_bias, m_dn_norm_g, m_mem_norm_g, m_w_mem_kv, m_w_out, m_final_g, v_ln_g, v_w_in, v_gmlp_ln_g, v_gmlp_ln_b, v_gmlp_ws, v_gmlp_bs, v_conv_w, v_dn_a_log, v_dn_dt_bias, v_dn_norm_g, v_mem_norm_g, v_w_mem_kv, v_w_out, v_final_g):
    xs = x[0]
    mems = mem[0]
    tgt = loss_target[0]
    s, d = xs.shape
    shard_w = w_in.shape[2]
    in_w = N_DEV * shard_w
    me = 4 * lax.axis_index("x") + 2 * lax.axis_index("y") + lax.axis_index("c")

    (g_in,) = _gather_two_level([w_in[0].astype(BF16)], "gather_w_in")
    o_g, o_dn, o_ab = 0, 3 * GMLP_W, 3 * GMLP_W + 4 * DN_W
    o_xa = o_ab + 2 * DN_H

    def shard_cols(g, lo, hi):
        out = []
        while lo < hi:
            sh = lo // shard_w
            end = min(hi, (sh + 1) * shard_w)
            out.append(g[sh][:, lo - sh * shard_w:end - sh * shard_w])
            lo = end
        return out

    def own_layout(g):
        main = jnp.concatenate(shard_cols(g, o_dn, o_ab) + shard_cols(g, o_g, o_dn) + shard_cols(g, o_xa, in_w), axis=1)
        return main, jnp.pad(jnp.concatenate(shard_cols(g, o_ab, o_xa), axis=1), ((0, 0), (0, LANE - 2 * DN_H)))

    w_main, w_ab = own_layout(g_in)

    ln_g2 = ln_g.reshape(1, d)
    lng2 = gmlp_ln_g.reshape(1, GMLP_W)
    lnb2 = gmlp_ln_b.reshape(1, GMLP_W)
    ws3 = gmlp_ws[0]
    bs_t = gmlp_bs[0].T
    alog_row = _lane_row(dn_a_log)
    dt_row = _lane_row(dn_dt_bias)
    dn_g2 = dn_norm_g.reshape(1, HEAD)
    mem_g2 = mem_norm_g.reshape(1, d)
    fin_g2 = final_g.reshape(1, d)

    proj, ab, h_t, (g_out, g_kv, g_conv) = _inproj(
        xs, ln_g2, w_main, w_ab, [w_out[0].astype(BF16), w_mem_kv[0].astype(BF16), conv_w[0]])
    wo = g_out.reshape(MIX_W, d)
    wo_perm = jnp.concatenate([wo[GMLP_W:GMLP_W + DN_W], wo[0:GMLP_W], wo[GMLP_W + DN_W:MIX_W]], axis=0)
    w_kv = g_kv.reshape(d, 2 * XA_W)
    conv_full = g_conv.transpose(1, 0, 2).reshape(DN_K, 3 * DN_W)
    out_a = _gmlp_fwd(proj, lng2, lnb2, ws3, bs_t)
    mkv = _memkv_fwd(mems, mem_g2, w_kv)
    out_c = _xattn_fwd(proj, mkv)
    q, k, v, gb, gbt, yc = _dn_pre(proj, ab, conv_full, alog_row, dt_row)
    wk, qg, kd, tmat, ai, egl, o, vn, st, out_b = _dn_fwd(q, k, v, gb, gbt, proj, dn_g2)

    dx2, dx2b, dmixed, loss_acc, d_fin_g = _final(xs, tgt, out_b, out_a, out_c, wo_perm, fin_g2)

    dwo_b = _matmul_tn(out_b, dx2b, "dw_out_b")
    dwo_a = _matmul_tn(out_a, dx2b, "dw_out_a")
    dwo_c = _matmul_tn(out_c, dx2b, "dw_out_c")
    d_w_out = jnp.concatenate([dwo_a, dwo_b, dwo_c], axis=0)

    dp_g, d_ws, d_bst, d_lng, d_lnb = _gmlp_bwd(proj, dmixed, lng2, lnb2, ws3, bs_t)
    dp_x, dmkv = _xattn_bwd(proj, dmixed, mkv)
    d_w_kv, d_mem_g = _memkv_bwd(mems, mem_g2, w_kv, dmkv)
    dq, dk, dv, dgb, dp_dz, d_dn_g = _dn_bwd(dmixed, o, proj, dn_g2, wk, qg, kd, ai, egl, q, k, v, gb, gbt, tmat, vn, st)
    dp_qkv, dp_ab, d_conv, d_alog, d_dt = _dn_pre_bwd(proj, yc, ab, conv_full, alog_row, dt_row, dq, dk, dv, dgb)

    dw_qkv = _matmul_acc(h_t, dp_qkv, "dw_in_qkv")
    dw_dz = _matmul_acc(h_t, dp_dz, "dw_in_dz")
    dw_gm = _matmul_acc(h_t, dp_g, "dw_in_gmlp")
    dw_xa = _matmul_acc(h_t, dp_x, "dw_in_xa")
    dw_ab = _matmul_acc(h_t, dp_ab, "dw_in_ab")
    segs = [(o_g, dw_gm), (o_dn, dw_qkv), (o_dn + 3 * DN_W, dw_dz), (o_ab, dw_ab[:, :2 * DN_H]), (o_xa, dw_xa)]
    shards = []
    for sh in range(N_DEV):
        lo, hi = sh * shard_w, (sh + 1) * shard_w
        parts = [arr[:, max(lo, off) - off:min(hi, off + arr.shape[1]) - off] for off, arr in segs
                 if off < hi and off + arr.shape[1] > lo]
        shards.append(jnp.concatenate(parts, axis=1).astype(BF16))
    send_in = jnp.stack(shards)

    small_shapes = [(1, 1), gmlp_ln_g.shape, gmlp_ln_b.shape, gmlp_ws.shape, gmlp_bs.shape, dn_a_log.shape,
                    dn_dt_bias.shape, dn_norm_g.shape, mem_norm_g.shape, final_g.shape, (DN_K, 3 * DN_W)]
    small_g = _pack([loss_acc[0:1, 0:1], d_lng, d_lnb, d_ws, d_bst.T, d_alog[:, :DN_H], d_dt[:, :DN_H], d_dn_g, d_mem_g,
                     d_fin_g, d_conv])
    zc = jnp.zeros((DN_K, 3 * DN_W), F32)
    z1 = jnp.zeros((1, 1), F32)
    small_w = _pack([z1, gmlp_ln_g, gmlp_ln_b, gmlp_ws, gmlp_bs, dn_a_log, dn_dt_bias, dn_norm_g, mem_norm_g, final_g, zc])
    small_m = _pack([z1, m_gmlp_ln_g, m_gmlp_ln_b, m_gmlp_ws, m_gmlp_bs, m_dn_a_log, m_dn_dt_bias, m_dn_norm_g,
                     m_mem_norm_g, m_final_g, zc])
    small_v = _pack([z1 + 1.0, v_gmlp_ln_g, v_gmlp_ln_b, v_gmlp_ws, v_gmlp_bs, v_dn_a_log, v_dn_dt_bias, v_dn_norm_g,
                     v_mem_norm_g, v_final_g, zc + 1.0])

    send_out = d_w_out.reshape(N_DEV, MIX_W // N_DEV, d).astype(BF16)
    send_kv = d_w_kv.reshape(N_DEV, d // N_DEV, 2 * XA_W).astype(BF16)
    sends = [send_in, send_out, send_kv]
    all_small, got = _swap_halves(small_g, sends, "swap_halves")
    core = lax.axis_index("c").astype(jnp.int32).reshape(1)
    chip_sums = _pair_sums(core, sends, got)
    grad_x, d_ln_g, (r_in, r_out, r_kv) = _dh_rms(
        [dp_qkv, dp_dz, dp_g, dp_x, dp_ab], [w_main], [w_ab], xs, dx2, ln_g2, chip_sums)
    (all_ln_g,) = _gather_two_level([_pack([d_ln_g])], "gather_ln_g")

    g_w_in, dl_w_in, nm_w_in, nv_w_in = _adam(r_in, w_in[0], m_w_in[0], v_w_in[0], "adam_w_in")
    g_w_out, dl_w_out, nm_w_out, nv_w_out = _adam(r_out, w_out[0], m_w_out[0], v_w_out[0], "adam_w_out")
    g_w_kv, dl_w_kv, nm_w_kv, nv_w_kv = _adam(r_kv, w_mem_kv[0], m_w_mem_kv[0], v_w_mem_kv[0], "adam_w_kv")
    sm = [_unpack(t, small_shapes) for t in _adam(all_small, small_w, small_m, small_v, "adam_small")]
    ln_res = [_unpack(t, [ln_g.shape])[0]
              for t in _adam(all_ln_g, _pack([ln_g]), _pack([m_ln_g]), _pack([v_ln_g]), "adam_ln_g")]

    conv_parts = lax.dynamic_slice(all_small, (0, all_small.shape[1] - (DN_K * 3 * DN_W) // LANE, 0),
                                   (N_DEV, (DN_K * 3 * DN_W) // LANE, LANE)).reshape(N_DEV, DN_K, 3 * DN_W)
    cshard = conv_w.shape[2]
    conv_parts = lax.dynamic_slice(conv_parts, (0, 0, me * cshard), (N_DEV, DN_K, cshard))
    cpad = ((0, 0), (0, HALO - DN_K), (0, 0))
    conv_res = _adam(jnp.pad(conv_parts, cpad), jnp.pad(conv_w[0], cpad[1:]), jnp.pad(m_conv_w[0], cpad[1:]),
                     jnp.pad(v_conv_w[0], cpad[1:], constant_values=1.0), "adam_conv")
    g_conv_s, dl_conv, nm_conv, nv_conv = [t[:DN_K][None] for t in conv_res]

    loss = sm[0][0].reshape(())

    def group(idx, big_in, big_conv, big_kv, big_out):
        names = sm[idx][1:]
        return [ln_res[idx], big_in[None], names[0], names[1], names[2], names[3], big_conv, names[4], names[5], names[6],
                names[7], big_kv[None], big_out[None], names[8]]

    grads = group(0, g_w_in, g_conv_s, g_w_kv, g_w_out)
    deltas = group(1, dl_w_in, dl_conv, dl_w_kv, dl_w_out)
    new_m = group(2, nm_w_in, nm_conv, nm_w_kv, nm_w_out)
    new_v = group(3, nv_w_in, nv_conv, nv_w_kv, nv_w_out)
    return (loss, grad_x[None], *grads, *deltas, *new_m, *new_v)
```

```python
import functools

import jax
import jax.numpy as jnp
from jax import lax
from jax.experimental import pallas as pl
from jax.experimental.pallas import tpu as pltpu

F32 = jnp.float32
BF16 = jnp.bfloat16
HIGHEST = lax.Precision.HIGHEST
MESH_ID = pl.DeviceIdType.MESH

N_DEV = 8
EPS = 1e-6
GMLP_W = 512
GMLP_G = 4
GMLP_T = 128
DN_W = 1024
DN_H = 8
HEAD = 128
DN_K = 4
CH = 64
XA_W = 512
XA_H = 4
LANE = 128
HALO = 8
MAIN_W = 4 * DN_W + 3 * GMLP_W + 2 * XA_W
MIX_W = DN_W + GMLP_W + XA_W
VMEM_LIMIT = 56 * 1024 * 1024

ADAM_LR = 0.001
ADAM_B1 = 0.9
ADAM_B2 = 0.999
ADAM_EPS = 1e-08
ADAM_WD = 0.01
ADAM_STEP = 10


def _sds(shape, dtype=F32):
    return jax.ShapeDtypeStruct(tuple(shape), dtype)


def _params(sem=None):
    if sem is None:
        return pltpu.CompilerParams(vmem_limit_bytes=VMEM_LIMIT)
    return pltpu.CompilerParams(dimension_semantics=tuple(sem), vmem_limit_bytes=VMEM_LIMIT)


def _tile(n, prefs):
    for p in prefs:
        if n % p == 0:
            return p
    return n


def _mm(a, b):
    return jnp.dot(a.astype(BF16), b.astype(BF16), preferred_element_type=F32)


def _mm_nt(a, b):
    return lax.dot_general(a.astype(BF16), b.astype(BF16), (((1,), (1,)), ((), ())), preferred_element_type=F32)


def _mm_tn(a, b):
    return lax.dot_general(a.astype(BF16), b.astype(BF16), (((0,), (0,)), ((), ())), preferred_element_type=F32)


def _mm_hi(a, b):
    return jnp.dot(a, b, precision=HIGHEST, preferred_element_type=F32)


def _mm_3x(a, b):
    return jnp.dot(a, b, precision=lax.Precision.HIGH, preferred_element_type=F32)


_GELU_C = 0.7978845608028654
_GELU_A = 0.044715


def _gelu(x):
    return 0.5 * x * (1.0 + jnp.tanh(_GELU_C * (x + _GELU_A * x * x * x)))


def _gelu_grad(x):
    t = jnp.tanh(_GELU_C * (x + _GELU_A * x * x * x))
    return 0.5 * (1.0 + t) + 0.5 * x * (1.0 - t * t) * _GELU_C * (1.0 + 3.0 * _GELU_A * x * x)


def _silu(x):
    return x * jax.nn.sigmoid(x)


def _silu_grad(x):
    s = jax.nn.sigmoid(x)
    return s * (1.0 + x * (1.0 - s))


def _rowsum(x):
    return jnp.sum(x, axis=-1, keepdims=True)


def _colsum(x):
    return jnp.sum(x, axis=0, keepdims=True)


def _iota2(shape, dim):
    return lax.broadcasted_iota(jnp.int32, shape, dim)


def _chunk_tri(tm, upper):
    r = _iota2((tm, tm), 0)
    c = _iota2((tm, tm), 1)
    same = lax.shift_right_logical(r, 6) == lax.shift_right_logical(c, 6)
    tri = (r <= c) if upper else (r >= c)
    return jnp.where(same & tri, 1.0, 0.0).astype(F32)


N_CHIP = 4


def _mesh_place():
    x, y, c = lax.axis_index("x"), lax.axis_index("y"), lax.axis_index("c")
    chips = [(1 - x, y), (x, 1 - y), (1 - x, 1 - y)]
    return x, y, c, (x, y, 1 - c), chips


class _Gather:
    def __init__(self, ins, outs, send_sems, recv_sems, loc_sems):
        self.ins, self.outs, self.send_sems, self.recv_sems, self.loc_sems = ins, outs, send_sems, recv_sems, loc_sems
        self.x, self.y, self.c, self.sib, self.chips = _mesh_place()
        self.me = (self.x, self.y, self.c)
        north = self.c == 1
        self.relay_from = (jnp.where(north, 1 - self.x, self.x), jnp.where(north, self.y, 1 - self.y))
        self.relay_to = (jnp.where(north, self.x, 1 - self.x), jnp.where(north, 1 - self.y, self.y))

    def copy(self, a, k, block, to, src=None):
        slot = self.outs[a].at[4 * block[0] + 2 * block[1] + block[2]]
        return pltpu.make_async_remote_copy(
            src_ref=slot if src is None else src, dst_ref=slot, send_sem=self.send_sems.at[a, k],
            recv_sem=self.recv_sems.at[a, k], device_id=to, device_id_type=MESH_ID)

    def own(self, a):
        return pltpu.make_async_copy(self.ins[a], self.outs[a].at[4 * self.x + 2 * self.y + self.c], self.loc_sems.at[a])

    def first(self, a):
        return [self.copy(a, 0, self.me, self.sib, src=self.ins[a])] + [
            self.copy(a, 1 + j, self.me, (*self.chips[j], self.c), src=self.ins[a]) for j in range(2)]

    def relayed(self, a):
        return self.copy(a, 3, (*self.relay_from, self.c), (*self.relay_to, self.c))

    def passed(self, a, j):
        return self.copy(a, 4 + j, (*self.chips[j], self.c), self.sib)

    def start(self):
        for a in range(len(self.ins)):
            self.own(a).start()
            for cp in self.first(a):
                cp.start()

    def relay(self):
        for a in range(len(self.ins)):
            for j in range(2):
                self.copy(a, 1 + j, (*self.chips[j], self.c), self.me).wait_recv()
            self.relayed(a).start()
            for j in range(2):
                self.passed(a, j).start()

    def finish(self):
        n = len(self.ins)
        for a in range(n):
            self.copy(a, 3, (*self.chips[2], self.c), self.me).wait_recv()
            self.passed(a, 2).start()
        for a in range(n):
            self.copy(a, 0, self.sib, self.me).wait_recv()
            for j, chip in enumerate(self.chips):
                self.copy(a, 4 + j, (*chip, 1 - self.c), self.me).wait_recv()
        for a in range(n):
            for cp in self.first(a) + [self.relayed(a)] + [self.passed(a, j) for j in range(N_CHIP - 1)]:
                cp.wait_send()
            self.own(a).wait()

    @staticmethod
    def sems(n):
        return [pltpu.SemaphoreType.DMA((n, N_DEV - 1)), pltpu.SemaphoreType.DMA((n, N_DEV - 1)),
                pltpu.SemaphoreType.DMA((n,))]


def _gather_two_level(arrs, name):
    n = len(arrs)

    def body(*refs):
        g = _Gather(refs[:n], refs[n:2 * n], *refs[2 * n:])
        g.start()
        g.relay()
        g.finish()

    any_spec = pl.BlockSpec(memory_space=pl.ANY)
    return pl.pallas_call(
        body, name=name, out_shape=[_sds((N_DEV,) + a.shape, a.dtype) for a in arrs],
        in_specs=[any_spec] * n, out_specs=[any_spec] * n, scratch_shapes=_Gather.sems(n),
        compiler_params=pltpu.CompilerParams(has_side_effects=True),
    )(*arrs)


def _swap_halves(small, grads, name):
    n = len(grads)

    def body(*refs):
        small_ref = refs[0]
        ins = refs[1:1 + n]
        small_out = refs[1 + n]
        got = refs[2 + n:2 + 2 * n]
        s_send, s_recv, g_send, g_recv, loc_sem = refs[2 + 2 * n:]
        x, y, c, sib, _ = _mesh_place()
        me = 4 * x + 2 * y + c
        sends, recvs = [], []
        for j in range(1, N_DEV):
            px = 1 - x if (j >> 2) & 1 else x
            py = 1 - y if (j >> 1) & 1 else y
            pc = 1 - c if j & 1 else c
            cp = pltpu.make_async_remote_copy(
                src_ref=small_ref, dst_ref=small_out.at[me], send_sem=s_send.at[j - 1], recv_sem=s_recv.at[j - 1],
                device_id=(px, py, pc), device_id_type=MESH_ID)
            cp.start()
            sends.append(cp)
            recvs.append(pltpu.make_async_remote_copy(
                src_ref=small_ref, dst_ref=small_out.at[4 * px + 2 * py + pc], send_sem=s_send.at[j - 1],
                recv_sem=s_recv.at[j - 1], device_id=(px, py, pc), device_id_type=MESH_ID))
        own = pltpu.make_async_copy(small_ref, small_out.at[me], loc_sem)
        own.start()
        for a in range(n):
            for chip in range(N_CHIP):
                cp = pltpu.make_async_remote_copy(
                    src_ref=ins[a].at[2 * chip + 1 - c], dst_ref=got[a].at[chip], send_sem=g_send.at[a, chip],
                    recv_sem=g_recv.at[a, chip], device_id=sib, device_id_type=MESH_ID)
                cp.start()
                sends.append(cp)
                recvs.append(cp)
        for cp in sends:
            cp.wait_send()
        for cp in recvs:
            cp.wait_recv()
        own.wait()

    half = [_sds((N_CHIP,) + g.shape[1:], g.dtype) for g in grads]
    any_spec = pl.BlockSpec(memory_space=pl.ANY)
    res = pl.pallas_call(
        body, name=name, out_shape=[_sds((N_DEV,) + small.shape, small.dtype)] + half,
        in_specs=[any_spec] * (1 + n), out_specs=[any_spec] * (1 + n),
        scratch_shapes=[pltpu.SemaphoreType.DMA((N_DEV - 1,)), pltpu.SemaphoreType.DMA((N_DEV - 1,)),
                        pltpu.SemaphoreType.DMA((n, N_CHIP)), pltpu.SemaphoreType.DMA((n, N_CHIP)),
                        pltpu.SemaphoreType.DMA],
        compiler_params=pltpu.CompilerParams(has_side_effects=True),
    )(small, *grads)
    return res[0], res[1:]


def _pair_sums(core, mine, got):
    n = len(got)

    def body(core_ref, *refs):
        for a in range(n):
            refs[2 * n + a][...] = (refs[a][...].astype(F32) + refs[n + a][...].astype(F32)).astype(BF16)

    half = lambda g: (1, g.shape[1] // 2, g.shape[2])
    own = [pl.BlockSpec(half(g), lambda i, j, core_ref: (2 * i + core_ref[0], j, 0)) for g in got]
    slot = [pl.BlockSpec(half(g), lambda i, j, core_ref: (i, j, 0)) for g in got]
    return pl.pallas_call(
        body, name="pair_sums", out_shape=[_sds(g.shape, BF16) for g in got],
        grid_spec=pltpu.PrefetchScalarGridSpec(
            num_scalar_prefetch=1, grid=(N_CHIP, 2), in_specs=own + slot, out_specs=slot),
        compiler_params=_params(("parallel", "parallel")),
    )(core, *mine, *got)


class _ChipExchange:
    def __init__(self, ins, outs, send_sems, recv_sems, loc_sems):
        self.ins, self.outs, self.send_sems, self.recv_sems, self.loc_sems = ins, outs, send_sems, recv_sems, loc_sems
        self.x, self.y, self.c, _, self.chips = _mesh_place()
        self.mine = 2 * self.x + self.y

    def own(self, a):
        return pltpu.make_async_copy(self.ins[a].at[self.mine], self.outs[a].at[self.mine], self.loc_sems.at[a])

    def copy(self, a, j, lands_in):
        chip = self.chips[j]
        return pltpu.make_async_remote_copy(
            src_ref=self.ins[a].at[2 * chip[0] + chip[1]], dst_ref=self.outs[a].at[lands_in],
            send_sem=self.send_sems.at[a, j], recv_sem=self.recv_sems.at[a, j], device_id=(*chip, self.c),
            device_id_type=MESH_ID)

    def start(self):
        for a in range(len(self.ins)):
            self.own(a).start()
            for j in range(N_CHIP - 1):
                self.copy(a, j, self.mine).start()

    def finish(self):
        for a in range(len(self.ins)):
            for j, chip in enumerate(self.chips):
                self.copy(a, j, self.mine).wait_send()
                self.copy(a, j, 2 * chip[0] + chip[1]).wait_recv()
            self.own(a).wait()

    @staticmethod
    def sems(n):
        return [pltpu.SemaphoreType.DMA((n, N_CHIP - 1)), pltpu.SemaphoreType.DMA((n, N_CHIP - 1)),
                pltpu.SemaphoreType.DMA((n,))]


def _inproj(x, ln_g, w_main, w_ab, late):
    s, d = x.shape
    n = w_main.shape[1]
    tm = _tile(s, (512, 256, 128))
    tn = _tile(n, (1664, 512, 128))
    nl = len(late)
    ni, nj = s // tm, n // tn

    def body(*refs):
        x_ref, g_ref, w_ref, wab_ref = refs[:4]
        proj_ref, ab_ref, ht_ref = refs[4 + nl:7 + nl]
        hs = refs[7 + 2 * nl]
        gather = _Gather(refs[4:4 + nl], refs[7 + nl:7 + 2 * nl], *refs[8 + 2 * nl:])
        step = pl.program_id(0) * nj + pl.program_id(1)

        @pl.when(step == 0)
        def _():
            gather.start()

        @pl.when(pl.program_id(1) == 0)
        def _():
            xv = x_ref[...]
            r = lax.rsqrt(jnp.mean(xv * xv, axis=-1, keepdims=True) + EPS)
            hf = xv * r * g_ref[...]
            h = hf.astype(BF16)
            hs[...] = h
            ht_ref[...] = hf.T.astype(BF16)
            ab_ref[...] = jnp.dot(h, wab_ref[...], preferred_element_type=F32)

        proj_ref[...] = jnp.dot(hs[...], w_ref[...], preferred_element_type=F32)

        @pl.when(step == (ni * nj) // 2)
        def _():
            gather.relay()

        @pl.when(step == ni * nj - 1)
        def _():
            gather.finish()

    any_spec = pl.BlockSpec(memory_space=pl.ANY)
    res = pl.pallas_call(
        body, name="inproj", grid=(ni, nj),
        in_specs=[pl.BlockSpec((tm, d), lambda i, j: (i, 0)), pl.BlockSpec((1, d), lambda i, j: (0, 0)),
                  pl.BlockSpec((d, tn), lambda i, j: (0, j)), pl.BlockSpec((d, LANE), lambda i, j: (0, 0))]
        + [any_spec] * nl,
        out_specs=[pl.BlockSpec((tm, tn), lambda i, j: (i, j)), pl.BlockSpec((tm, LANE), lambda i, j: (i, 0)),
                   pl.BlockSpec((d, tm), lambda i, j: (0, i))] + [any_spec] * nl,
        out_shape=[_sds((s, n)), _sds((s, LANE)), _sds((d, s), BF16)]
        + [_sds((N_DEV,) + a.shape, a.dtype) for a in late],
        scratch_shapes=[pltpu.VMEM((tm, d), BF16)] + _Gather.sems(nl),
        compiler_params=_params(("arbitrary", "arbitrary")),
    )(x, ln_g, w_main, w_ab, *late)
    return res[0], res[1], res[2], res[3:]


def _matmul_acc(a, b, name):
    m, k = a.shape
    n = b.shape[1]
    tm = _tile(m, (2048, 1024, 512, 256, 128))
    tn = _tile(n, (1024, 512, 256, 128))
    tk = _tile(k, (1024, 512, 256, 128))
    nk = k // tk

    def body(a_ref, b_ref, o_ref, acc):
        @pl.when(pl.program_id(2) == 0)
        def _():
            acc[...] = jnp.zeros_like(acc)

        acc[...] += jnp.dot(a_ref[...], b_ref[...], preferred_element_type=F32)

        @pl.when(pl.program_id(2) == nk - 1)
        def _():
            o_ref[...] = acc[...].astype(BF16)

    return pl.pallas_call(
        body, name=name, grid=(m // tm, n // tn, nk),
        in_specs=[pl.BlockSpec((tm, tk), lambda i, j, l: (i, l)), pl.BlockSpec((tk, tn), lambda i, j, l: (l, j))],
        out_specs=pl.BlockSpec((tm, tn), lambda i, j, l: (i, j)),
        out_shape=_sds((m, n), BF16), scratch_shapes=[pltpu.VMEM((tm, tn), F32)],
        compiler_params=_params(("parallel", "parallel", "arbitrary")),
    )(a, b)


def _matmul_tn(a, b, name):
    k, m = a.shape
    n = b.shape[1]
    tm = _tile(m, (1024, 512, 256, 128))
    tn = _tile(n, (1024, 512, 256, 128))
    tk = _tile(k, (1024, 512, 256, 128))
    nk = k // tk

    def body(a_ref, b_ref, o_ref, acc):
        @pl.when(pl.program_id(2) == 0)
        def _():
            acc[...] = jnp.zeros_like(acc)

        acc[...] += _mm_tn(a_ref[...], b_ref[...])

        @pl.when(pl.program_id(2) == nk - 1)
        def _():
            o_ref[...] = acc[...].astype(BF16)

    return pl.pallas_call(
        body, name=name, grid=(m // tm, n // tn, nk),
        in_specs=[pl.BlockSpec((tk, tm), lambda i, j, l: (l, i)), pl.BlockSpec((tk, tn), lambda i, j, l: (l, j))],
        out_specs=pl.BlockSpec((tm, tn), lambda i, j, l: (i, j)),
        out_shape=_sds((m, n), BF16), scratch_shapes=[pltpu.VMEM((tm, tn), F32)],
        compiler_params=_params(("parallel", "parallel", "arbitrary")),
    )(a, b)


DH_TAIL_TILES = 2


def _dh_rms(pieces, w_rows, wab_rows, x, dx2, ln_g, chip_sums, tiles, name, prev=None):
    s, d = x.shape
    npc = len(pieces)
    nx = len(chip_sums)
    tm = _tile(s, (256, 128))
    t0 = tiles[0]
    ni = tiles[1] - tiles[0]
    widths = [p.shape[1] for p in pieces[:-1]]
    offs = [sum(widths[:p]) for p in range(npc - 1)]
    nw = len(w_rows)
    npv = 0 if prev is None else 2
    nin = npc + 2 * nw + 3 + npv

    def body(*refs):
        p_refs = refs[:npc]
        w_refs = refs[npc:npc + nw]
        wab_refs = refs[npc + nw:npc + 2 * nw]
        x_ref, dx2_ref, g_ref = refs[npc + 2 * nw:npc + 2 * nw + 3]
        gx_ref, dg_ref = refs[nin + nx:nin + nx + 2]
        exch = None
        if nx:
            exch = _ChipExchange(refs[nin:nin + nx], refs[nin + nx + 2:nin + 2 * nx + 2], *refs[nin + 2 * nx + 2:])
        step = pl.program_id(0)

        @pl.when(step == 0)
        def _():
            dg_ref[...] = jnp.zeros_like(dg_ref) if prev is None else refs[nin - 1][...]
            if nx:
                exch.start()

        cols = []
        for w_ref, wab_ref in zip(w_refs, wab_refs):
            part = _mm_nt(p_refs[npc - 1][...], wab_ref[...])
            for p in range(npc - 1):
                part += _mm_nt(p_refs[p][...], w_ref[:, offs[p]:offs[p] + widths[p]])
            cols.append(part)
        dhv = jnp.concatenate(cols, axis=1)
        xv = x_ref[...]
        r = lax.rsqrt(jnp.mean(xv * xv, axis=-1, keepdims=True) + EPS)
        xhat = xv * r
        dg_ref[...] += _colsum(dhv * xhat)
        dxh = dhv * g_ref[...]
        gx_ref[...] = dx2_ref[...] + r * (dxh - xhat * jnp.mean(dxh * xhat, axis=-1, keepdims=True))

        if nx:
            @pl.when(step == ni - 1)
            def _():
                exch.finish()

    any_spec = pl.BlockSpec(memory_space=pl.ANY)
    row = pl.BlockSpec((tm, d), lambda i: (i + t0, 0))
    vec = pl.BlockSpec((1, d), lambda i: (0, 0))
    once = lambda a: pl.BlockSpec(a.shape, lambda i: (0, 0), pipeline_mode=pl.Buffered(1))
    in_specs = [pl.BlockSpec((tm, p.shape[1]), lambda i: (i + t0, 0)) for p in pieces]
    in_specs += [once(w) for w in w_rows] + [once(w) for w in wab_rows] + [row, row, vec]
    in_specs += ([] if prev is None else [any_spec, vec]) + [any_spec] * nx
    res = pl.pallas_call(
        body, name=name, grid=(ni,), in_specs=in_specs,
        out_specs=[row, vec] + [any_spec] * nx,
        out_shape=[_sds((s, d)), _sds((1, d))] + [_sds(p.shape, p.dtype) for p in chip_sums],
        scratch_shapes=_ChipExchange.sems(nx) if nx else [],
        input_output_aliases={} if prev is None else {nin - 2: 0},
        compiler_params=_params(("arbitrary",)),
    )(*pieces, *w_rows, *wab_rows, x, dx2, ln_g, *(() if prev is None else prev), *chip_sums)
    return res[0], res[1], res[2:]


def _final(x, tgt, out_b, out_a, out_c, w_out, final_g):
    s, d = x.shape
    tm = _tile(s, (256, 128))

    def body(x_ref, t_ref, b_ref, a_ref, c_ref, w_ref, g_ref, dx2_ref, dx2b_ref, dm_ref, loss_ref, dg_ref):
        @pl.when(pl.program_id(0) == 0)
        def _():
            loss_ref[...] = jnp.zeros_like(loss_ref)
            dg_ref[...] = jnp.zeros_like(dg_ref)

        x2 = x_ref[...]
        x2 += jnp.dot(b_ref[...], w_ref[0:DN_W, :], preferred_element_type=F32)
        x2 += jnp.dot(a_ref[...], w_ref[DN_W:DN_W + GMLP_W, :], preferred_element_type=F32)
        x2 += jnp.dot(c_ref[...], w_ref[DN_W + GMLP_W:MIX_W, :], preferred_element_type=F32)
        r = lax.rsqrt(jnp.mean(x2 * x2, axis=-1, keepdims=True) + EPS)
        xhat = x2 * r
        g = g_ref[...]
        err = xhat * g - t_ref[...]
        tok = 0.5 * jnp.mean(err * err, axis=-1, keepdims=True)
        loss_ref[...] += jnp.broadcast_to(_colsum(tok), loss_ref.shape)
        dy = err * (1.0 / d)
        dg_ref[...] += _colsum(dy * xhat)
        dxh = dy * g
        dx2 = r * (dxh - xhat * jnp.mean(dxh * xhat, axis=-1, keepdims=True))
        dx2_ref[...] = dx2
        dx2b = dx2.astype(BF16)
        dx2b_ref[...] = dx2b
        dm_ref[...] = _mm_nt(dx2b, w_ref[...])

    row = pl.BlockSpec((tm, d), lambda i: (i, 0))
    vec = pl.BlockSpec((1, d), lambda i: (0, 0))
    return pl.pallas_call(
        body, name="final", grid=(s // tm,),
        in_specs=[row, row, pl.BlockSpec((tm, DN_W), lambda i: (i, 0)), pl.BlockSpec((tm, GMLP_W), lambda i: (i, 0)),
                  pl.BlockSpec((tm, XA_W), lambda i: (i, 0)), pl.BlockSpec((MIX_W, d), lambda i: (0, 0)), vec],
        out_specs=[row, row, pl.BlockSpec((tm, MIX_W), lambda i: (i, 0)), pl.BlockSpec((1, LANE), lambda i: (0, 0)), vec],
        out_shape=[_sds((s, d)), _sds((s, d), BF16), _sds((s, MIX_W)), _sds((1, LANE)), _sds((1, d))],
        compiler_params=_params(("arbitrary",)),
    )(x, tgt, out_b, out_a, out_c, w_out, final_g)


GU_BLK = (4 * DN_W) // GMLP_W


def _gmlp_norm(gv, lng, lnb):
    va = _gelu(gv)
    mu = jnp.mean(va, axis=-1, keepdims=True)
    xc = va - mu
    rstd = lax.rsqrt(jnp.mean(xc * xc, axis=-1, keepdims=True) + EPS)
    vhat = xc * rstd
    return vhat, rstd, vhat * lng + lnb


def _gmlp_fwd(proj, lng, lnb, ws, bs_t):
    s = proj.shape[0]
    tm = _tile(s, (512, 256, 128))

    def body(u_ref, v_ref, z_ref, lng_ref, lnb_ref, ws_ref, bst_ref, o_ref):
        _, _, vn = _gmlp_norm(v_ref[...], lng_ref[...], lnb_ref[...])
        tri = _iota2((GMLP_T, GMLP_T), 0) >= _iota2((GMLP_T, GMLP_T), 1)
        for g in range(GMLP_G):
            cs = slice(g * HEAD, (g + 1) * HEAD)
            w = jnp.where(tri, ws_ref[g], 0.0).astype(BF16)
            b = bst_ref[:, g:g + 1]
            for c in range(tm // GMLP_T):
                rs = slice(c * GMLP_T, (c + 1) * GMLP_T)
                sg = _mm(w, vn[rs, cs]) + b
                o_ref[rs, cs] = (_gelu(u_ref[rs, cs]) * sg * _silu(z_ref[rs, cs])).astype(BF16)

    col = lambda k: pl.BlockSpec((tm, GMLP_W), lambda i: (i, GU_BLK + k))
    vec = pl.BlockSpec((1, GMLP_W), lambda i: (0, 0))
    return pl.pallas_call(
        body, name="gmlp_fwd", grid=(s // tm,),
        in_specs=[col(0), col(1), col(2), vec, vec, pl.BlockSpec((GMLP_G, GMLP_T, GMLP_T), lambda i: (0, 0, 0)),
                  pl.BlockSpec((GMLP_T, GMLP_G), lambda i: (0, 0))],
        out_specs=pl.BlockSpec((tm, GMLP_W), lambda i: (i, 0)), out_shape=_sds((s, GMLP_W), BF16),
        compiler_params=_params(("parallel",)),
    )(proj, proj, proj, lng, lnb, ws, bs_t)


def _gmlp_bwd(proj, dmixed, lng, lnb, ws, bs_t):
    s = proj.shape[0]
    tm = _tile(s, (512, 256, 128))

    def body(u_ref, v_ref, z_ref, d_ref, lng_ref, lnb_ref, ws_ref, bst_ref,
             dp_ref, dws_ref, dbst_ref, dlng_ref, dlnb_ref, dvn):
        @pl.when(pl.program_id(0) == 0)
        def _():
            dws_ref[...] = jnp.zeros_like(dws_ref)
            dbst_ref[...] = jnp.zeros_like(dbst_ref)
            dlng_ref[...] = jnp.zeros_like(dlng_ref)
            dlnb_ref[...] = jnp.zeros_like(dlnb_ref)

        gv = v_ref[...]
        lng_v = lng_ref[...]
        vhat, rstd, vn = _gmlp_norm(gv, lng_v, lnb_ref[...])
        tri = _iota2((GMLP_T, GMLP_T), 0) >= _iota2((GMLP_T, GMLP_T), 1)
        for g in range(GMLP_G):
            cs = slice(g * HEAD, (g + 1) * HEAD)
            w = jnp.where(tri, ws_ref[g], 0.0).astype(BF16)
            b = bst_ref[:, g:g + 1]
            dw_acc = jnp.zeros((GMLP_T, GMLP_T), F32)
            db_acc = jnp.zeros((GMLP_T, 1), F32)
            for c in range(tm // GMLP_T):
                rs = slice(c * GMLP_T, (c + 1) * GMLP_T)
                vn_b = vn[rs, cs]
                sg = _mm(w, vn_b) + b
                gu = u_ref[rs, cs]
                gz = z_ref[rs, cs]
                da = d_ref[rs, cs]
                uact = _gelu(gu)
                sz = _silu(gz)
                ds = da * uact * sz
                dp_ref[rs, cs] = (da * sg * sz * _gelu_grad(gu)).astype(BF16)
                dp_ref[rs, 2 * GMLP_W + g * HEAD:2 * GMLP_W + (g + 1) * HEAD] = (da * uact * sg * _silu_grad(gz)).astype(BF16)
                dw_acc += _mm_nt(ds, vn_b)
                db_acc += _rowsum(ds)
                dvn[rs, cs] = _mm_tn(w, ds)
            dws_ref[g] += jnp.where(tri, dw_acc, 0.0)
            dbst_ref[:, g:g + 1] += db_acc
        dvn_v = dvn[...]
        dlng_ref[...] += _colsum(dvn_v * vhat)
        dlnb_ref[...] += _colsum(dvn_v)
        dvh = dvn_v * lng_v
        dva = rstd * (dvh - jnp.mean(dvh, axis=-1, keepdims=True) - vhat * jnp.mean(dvh * vhat, axis=-1, keepdims=True))
        dp_ref[:, GMLP_W:2 * GMLP_W] = (dva * _gelu_grad(gv)).astype(BF16)

    col = lambda k: pl.BlockSpec((tm, GMLP_W), lambda i: (i, GU_BLK + k))
    vec = pl.BlockSpec((1, GMLP_W), lambda i: (0, 0))
    wsp = pl.BlockSpec((GMLP_G, GMLP_T, GMLP_T), lambda i: (0, 0, 0))
    bsp = pl.BlockSpec((GMLP_T, GMLP_G), lambda i: (0, 0))
    return pl.pallas_call(
        body, name="gmlp_bwd", grid=(s // tm,),
        in_specs=[col(0), col(1), col(2), pl.BlockSpec((tm, GMLP_W), lambda i: (i, DN_W // GMLP_W)), vec, vec, wsp, bsp],
        out_specs=[pl.BlockSpec((tm, 3 * GMLP_W), lambda i: (i, 0)), wsp, bsp, vec, vec],
        out_shape=[_sds((s, 3 * GMLP_W), BF16), _sds((GMLP_G, GMLP_T, GMLP_T)), _sds((GMLP_T, GMLP_G)),
                   _sds((1, GMLP_W)), _sds((1, GMLP_W))],
        scratch_shapes=[pltpu.VMEM((tm, GMLP_W), F32)],
        compiler_params=_params(("arbitrary",)),
    )(proj, proj, proj, dmixed, lng, lnb, ws, bs_t)


CQ_BLK = (4 * DN_W + 3 * GMLP_W) // XA_W


def _memkv_fwd(mem, g, w_kv):
    nm, d = mem.shape

    def body(m_ref, g_ref, w_ref, kv_ref):
        mv = m_ref[...]
        r = lax.rsqrt(jnp.mean(mv * mv, axis=-1, keepdims=True) + EPS)
        kv_ref[...] = _mm(mv * r * g_ref[...], w_ref[...])

    return pl.pallas_call(body, name="memkv_fwd", out_shape=_sds((nm, 2 * XA_W)), compiler_params=_params())(mem, g, w_kv)


def _memkv_bwd(mem, g, w_kv, dkv):
    nm, d = mem.shape

    def body(m_ref, g_ref, w_ref, dkv_ref, dw_ref, dg_ref):
        mv = m_ref[...]
        r = lax.rsqrt(jnp.mean(mv * mv, axis=-1, keepdims=True) + EPS)
        xhat = mv * r
        dkv_v = dkv_ref[...]
        dw_ref[...] = _mm_tn(xhat * g_ref[...], dkv_v)
        dg_ref[...] = _colsum(_mm_nt(dkv_v, w_ref[...]) * xhat)

    return pl.pallas_call(body, name="memkv_bwd", out_shape=[_sds((d, 2 * XA_W)), _sds((1, d))],
                          compiler_params=_params())(mem, g, w_kv, dkv)


def _xattn_probs(q, mk):
    sc = _mm_nt(q, mk) * (HEAD ** -0.5)
    e = jnp.exp(sc - jnp.max(sc, axis=-1, keepdims=True))
    return e / _rowsum(e)


def _xattn_fwd(proj, mkv):
    s = proj.shape[0]
    nm = mkv.shape[0]
    tm = _tile(s, (512, 256, 128))

    def body(q_ref, z_ref, kv_ref, o_ref):
        for h in range(XA_H):
            cs = slice(h * HEAD, (h + 1) * HEAD)
            p = _xattn_probs(q_ref[:, cs], kv_ref[:, cs])
            ctx = _mm(p, kv_ref[:, XA_W + h * HEAD:XA_W + (h + 1) * HEAD])
            o_ref[:, cs] = (ctx * _silu(z_ref[:, cs])).astype(BF16)

    col = lambda k: pl.BlockSpec((tm, XA_W), lambda i: (i, CQ_BLK + k))
    return pl.pallas_call(
        body, name="xattn_fwd", grid=(s // tm,),
        in_specs=[col(0), col(1), pl.BlockSpec((nm, 2 * XA_W), lambda i: (0, 0))],
        out_specs=pl.BlockSpec((tm, XA_W), lambda i: (i, 0)), out_shape=_sds((s, XA_W), BF16),
        compiler_params=_params(("parallel",)),
    )(proj, proj, mkv)


def _xattn_bwd(proj, dmixed, mkv):
    s = proj.shape[0]
    nm = mkv.shape[0]
    tm = _tile(s, (512, 256, 128))

    def body(q_ref, z_ref, d_ref, kv_ref, dp_ref, dkv_ref):
        @pl.when(pl.program_id(0) == 0)
        def _():
            dkv_ref[...] = jnp.zeros_like(dkv_ref)

        for h in range(XA_H):
            cs = slice(h * HEAD, (h + 1) * HEAD)
            vs = slice(XA_W + h * HEAD, XA_W + (h + 1) * HEAD)
            q = q_ref[:, cs]
            z = z_ref[:, cs]
            mk = kv_ref[:, cs]
            mv = kv_ref[:, vs]
            p = _xattn_probs(q, mk)
            ctx = _mm(p, mv)
            dc = d_ref[:, cs]
            dctx = dc * _silu(z)
            dp_ref[:, vs] = (dc * ctx * _silu_grad(z)).astype(BF16)
            dp = _mm_nt(dctx, mv)
            dkv_ref[:, vs] += _mm_tn(p, dctx)
            ds = p * (dp - _rowsum(dp * p)) * (HEAD ** -0.5)
            dp_ref[:, cs] = _mm(ds, mk).astype(BF16)
            dkv_ref[:, cs] += _mm_tn(ds, q)

    col = lambda k: pl.BlockSpec((tm, XA_W), lambda i: (i, CQ_BLK + k))
    kvs = pl.BlockSpec((nm, 2 * XA_W), lambda i: (0, 0))
    return pl.pallas_call(
        body, name="xattn_bwd", grid=(s // tm,),
        in_specs=[col(0), col(1), pl.BlockSpec((tm, XA_W), lambda i: (i, (DN_W + GMLP_W) // XA_W)), kvs],
        out_specs=[pl.BlockSpec((tm, 2 * XA_W), lambda i: (i, 0)), kvs],
        out_shape=[_sds((s, 2 * XA_W), BF16), _sds((nm, 2 * XA_W))],
        compiler_params=_params(("arbitrary",)),
    )(proj, proj, dmixed, mkv)


def _softplus(x):
    return jnp.maximum(x, 0.0) + jnp.log1p(jnp.exp(-jnp.abs(x)))


def _dn_pre(proj, ab, conv_w, alog_row, dt_row):
    s = proj.shape[0]
    tm = _tile(s, (256, 128))
    w3 = 3 * DN_W

    def body(x_ref, halo_ref, ab_ref, cw_ref, al_ref, dt_ref, q_ref, k_ref, v_ref, gb_ref, gbt_ref, yc_ref):
        i = pl.program_id(0)
        xv = x_ref[...]
        cat = jnp.concatenate([jnp.where(i > 0, halo_ref[...], 0.0), xv[0:HALO]], axis=0)
        yc = cw_ref[DN_K - 1:DN_K, :] * xv
        top = cw_ref[DN_K - 1:DN_K, :] * xv[0:HALO]
        for t in range(DN_K - 1):
            back = DN_K - 1 - t
            yc += cw_ref[t:t + 1, :] * pltpu.roll(xv, back, 0)
            top += cw_ref[t:t + 1, :] * pltpu.roll(cat, back, 0)[HALO:2 * HALO]
        yc = jnp.concatenate([top, yc[HALO:tm]], axis=0)
        yc_ref[...] = yc
        act = _silu(yc)
        for h in range(DN_H):
            cs = slice(h * HEAD, (h + 1) * HEAD)
            qa = act[:, cs]
            q_ref[:, cs] = qa * (lax.rsqrt(_rowsum(qa * qa) + EPS) * (HEAD ** -0.5))
            ka = act[:, DN_W + h * HEAD:DN_W + (h + 1) * HEAD]
            k_ref[:, cs] = ka * lax.rsqrt(_rowsum(ka * ka) + EPS)
        v_ref[...] = act[:, 2 * DN_W:w3]
        abv = ab_ref[...]
        lane = _iota2((tm, LANE), 1)
        g = jnp.where(lane < DN_H, -jnp.exp(al_ref[...]) * _softplus(abv + dt_ref[...]), 0.0)
        gc = _mm_hi(_chunk_tri(tm, False), g)
        gbv = jnp.where(lane < DN_H, gc, jnp.where(lane < 2 * DN_H, jax.nn.sigmoid(abv), 0.0))
        gb_ref[...] = gbv
        for c in range(tm // CH):
            gbt_ref[c] = gbv[c * CH:(c + 1) * CH, :].T[0:2 * DN_H, :]

    hb = tm // HALO
    row = lambda w: pl.BlockSpec((tm, w), lambda i: (i, 0))
    vec = pl.BlockSpec((1, LANE), lambda i: (0, 0))
    return pl.pallas_call(
        body, name="dn_pre", grid=(s // tm,),
        in_specs=[row(w3), pl.BlockSpec((HALO, w3), lambda i: (jnp.maximum(i * hb - 1, 0), 0)), row(LANE),
                  pl.BlockSpec((DN_K, w3), lambda i: (0, 0)), vec, vec],
        out_specs=[row(DN_W), row(DN_W), row(DN_W), row(LANE), pl.BlockSpec((tm // CH, 2 * DN_H, CH), lambda i: (i, 0, 0)),
                   row(w3)],
        out_shape=[_sds((s, DN_W)), _sds((s, DN_W)), _sds((s, DN_W)), _sds((s, LANE)),
                   _sds((s // CH, 2 * DN_H, CH)), _sds((s, w3))],
        compiler_params=_params(("parallel",)),
    )(proj, proj, ab, conv_w, alog_row, dt_row)


HEADS = tuple(range(DN_H))


def _hcols(h):
    return slice(h * HEAD, (h + 1) * HEAD)


def _chunk_scalings(k, v, gbv, gbt, h):
    gc = jnp.broadcast_to(gbv[:, h:h + 1], (CH, HEAD))
    beta = jnp.broadcast_to(gbv[:, DN_H + h:DN_H + h + 1], (CH, HEAD))
    gr = gbt[h:h + 1, :]
    ii = _iota2((CH, CH), 0)
    jj = _iota2((CH, CH), 1)
    dec = jnp.exp(jnp.where(ii >= jj, gc[:, 0:CH] - gr, -1e30))
    eg = jnp.exp(gc)
    gl = gr[:, CH - 1:CH]
    kb = k * beta
    return dict(beta=beta, dec=dec, eg=eg, gl=gl, ekd=jnp.exp(gl - gc), kb=kb, vb=v * beta, kbe=kb * eg)


def _chunk_scores(m, q, k):
    kq = _mm_nt(jnp.concatenate([m["kb"], q], axis=0), k)
    strict = _iota2((CH, CH), 0) > _iota2((CH, CH), 1)
    return jnp.where(strict, kq[0:CH] * m["dec"], 0.0), kq[CH:2 * CH] * m["dec"]


def _scan_cpb(s):
    return 8 if (s // CH) % 8 == 0 else 1


def _dn_fwd(q, k, v, gb, gbt, proj, norm_g):
    s = q.shape[0]
    cpb = _scan_cpb(s)
    tb = cpb * CH
    nblk = s // tb

    def body(q_ref, k_ref, v_ref, gb_ref, gbt_ref, z_ref, ng_ref,
             w_ref, qg_ref, kd_ref, t_ref, ai_ref, egl_ref, o_ref, vn_ref, st_ref, ob_ref, state):
        @pl.when(pl.program_id(0) == 0)
        def _():
            state[...] = jnp.zeros_like(state)

        ng = ng_ref[...]
        eye = jnp.where(_iota2((CH, CH), 0) == _iota2((CH, CH), 1), 1.0, 0.0).astype(F32)

        def chunk(c, carry):
            r0 = pl.multiple_of(c * CH, CH)
            rows = pl.ds(r0, CH)
            gbv = gb_ref[rows, :]
            gbt_v = gbt_ref[c]
            qs = [q_ref[rows, _hcols(h)] for h in HEADS]
            ks = [k_ref[rows, _hcols(h)] for h in HEADS]
            ms = [_chunk_scalings(ks[h], v_ref[rows, _hcols(h)], gbv, gbt_v, h) for h in HEADS]
            qgb = [(qs[h] * ms[h]["eg"]).astype(BF16) for h in HEADS]
            kdb = [(ks[h] * ms[h]["ekd"]).astype(BF16) for h in HEADS]
            egl = [jnp.broadcast_to(jnp.exp(ms[h]["gl"]), (1, LANE)) for h in HEADS]
            for h in HEADS:
                qg_ref[rows, _hcols(h)] = qgb[h]
                kd_ref[rows, _hcols(h)] = kdb[h]
                egl_ref[c, h:h + 1, :] = egl[h]
            sc = [_chunk_scores(ms[h], qs[h], ks[h]) for h in HEADS]
            for h in HEADS:
                ai_ref[h, rows, :] = sc[h][1]
            ts = [eye - sc[h][0] for h in HEADS]
            ps = [_mm_3x(sc[h][0], sc[h][0]) for h in HEADS]
            ts = [ts[h] + _mm_3x(ts[h], ps[h]) for h in HEADS]
            for _ in range(4):
                ps = [_mm(ps[h], ps[h]) for h in HEADS]
                ts = [ts[h] + _mm(ts[h], ps[h]) for h in HEADS]
            uw = [_mm(ts[h], jnp.concatenate([ms[h]["vb"], ms[h]["kbe"]], axis=1)) for h in HEADS]
            wb = [uw[h][:, HEAD:2 * HEAD].astype(BF16) for h in HEADS]
            for h in HEADS:
                t_ref[h, rows, :] = ts[h]
                w_ref[rows, _hcols(h)] = wb[h]
            sts = [state[h] for h in HEADS]
            stb = [sts[h].astype(BF16) for h in HEADS]
            for h in HEADS:
                st_ref[c, h] = stb[h]
            vnb = [(uw[h][:, 0:HEAD] - jnp.dot(wb[h], stb[h], preferred_element_type=F32)).astype(BF16) for h in HEADS]
            for h in HEADS:
                state[h] = sts[h] * egl[h] + _mm_tn(kdb[h], vnb[h])
            os_ = [jnp.dot(qgb[h], stb[h], preferred_element_type=F32) + _mm(sc[h][1], vnb[h]) for h in HEADS]
            for h in HEADS:
                o = os_[h]
                vn_ref[rows, _hcols(h)] = vnb[h]
                o_ref[rows, _hcols(h)] = o
                r = lax.rsqrt(jnp.mean(o * o, axis=-1, keepdims=True) + EPS)
                ob_ref[rows, _hcols(h)] = (o * r * ng * _silu(z_ref[rows, _hcols(h)])).astype(BF16)
            return carry

        lax.fori_loop(0, cpb, chunk, 0, unroll=4)

    row = pl.BlockSpec((tb, DN_W), lambda i: (i, 0))
    sq = pl.BlockSpec((DN_H, tb, CH), lambda i: (0, i, 0))
    return pl.pallas_call(
        body, name="dn_fwd", grid=(nblk,),
        in_specs=[row, row, row, pl.BlockSpec((tb, LANE), lambda i: (i, 0)),
                  pl.BlockSpec((cpb, 2 * DN_H, CH), lambda i: (i, 0, 0)), pl.BlockSpec((tb, DN_W), lambda i: (i, 3)),
                  pl.BlockSpec((1, HEAD), lambda i: (0, 0))],
        out_specs=[row, row, row, sq, sq, pl.BlockSpec((cpb, DN_H, LANE), lambda i: (i, 0, 0)), row, row,
                   pl.BlockSpec((cpb, DN_H, HEAD, HEAD), lambda i: (i, 0, 0, 0)), row],
        out_shape=[_sds((s, DN_W), BF16), _sds((s, DN_W), BF16), _sds((s, DN_W), BF16), _sds((DN_H, s, CH)),
                   _sds((DN_H, s, CH)), _sds((s // CH, DN_H, LANE)), _sds((s, DN_W)), _sds((s, DN_W), BF16),
                   _sds((s // CH, DN_H, HEAD, HEAD), BF16), _sds((s, DN_W), BF16)],
        scratch_shapes=[pltpu.VMEM((DN_H, HEAD, HEAD), F32)],
        compiler_params=_params(("arbitrary",)),
    )(q, k, v, gb, gbt, proj, norm_g)


def _dn_bwd(dmixed, o, proj, norm_g, w, qg, kd, ai, egl, q, k, v, gb, gbt, t, vn, st):
    s = o.shape[0]
    cpb = 4 if (s // CH) % 4 == 0 else 1
    tb = cpb * CH
    nblk = s // tb

    def body(dm_ref, o_ref, z_ref, ng_ref, w_ref, qg_ref, kd_ref, ai_ref, egl_ref,
             q_ref, k_ref, v_ref, gb_ref, gbt_ref, t_ref, vn_ref, st_ref,
             dq_ref, dk_ref, dv_ref, dgb_ref, dz_ref, dng_ref, dstate):
        @pl.when(pl.program_id(0) == 0)
        def _():
            dstate[...] = jnp.zeros_like(dstate)
            dng_ref[...] = jnp.zeros_like(dng_ref)

        ng = ng_ref[...]
        lane = _iota2((CH, LANE), 1)
        last = _iota2((CH, 1), 0) == CH - 1
        strict = _iota2((CH, CH), 0) > _iota2((CH, CH), 1)

        def chunk(cc, carry):
            c = cpb - 1 - cc
            r0 = pl.multiple_of(c * CH, CH)
            rows = pl.ds(r0, CH)
            dng = jnp.zeros((1, HEAD), F32)
            dob = []
            for h in HEADS:
                cs = _hcols(h)
                ov = o_ref[rows, cs]
                z = z_ref[rows, cs]
                db = dm_ref[rows, cs]
                r = lax.rsqrt(jnp.mean(ov * ov, axis=-1, keepdims=True) + EPS)
                ohat = ov * r
                dz_ref[rows, cs] = (db * ohat * ng * _silu_grad(z)).astype(BF16)
                dyn = db * _silu(z)
                dng += _colsum(dyn * ohat)
                doh = dyn * ng
                dob.append((r * (doh - ohat * jnp.mean(doh * ohat, axis=-1, keepdims=True))).astype(BF16))
            dng_ref[...] += dng
            dsn = [dstate[h] for h in HEADS]
            dsb = [dsn[h].astype(BF16) for h in HEADS]
            dvnb = [(_mm_tn(ai_ref[h, rows, :], dob[h])
                     + jnp.dot(kd_ref[rows, _hcols(h)], dsb[h], preferred_element_type=F32)).astype(BF16) for h in HEADS]
            part = [_mm_tn(qg_ref[rows, _hcols(h)], dob[h]) + egl_ref[c, h:h + 1, :] * dsn[h] for h in HEADS]
            for h in HEADS:
                dstate[h] = part[h] - _mm_tn(w_ref[rows, _hcols(h)], dvnb[h])
            gbv = gb_ref[rows, :]
            gbt_v = gbt_ref[c]
            qs = [q_ref[rows, _hcols(h)] for h in HEADS]
            ks = [k_ref[rows, _hcols(h)] for h in HEADS]
            vs = [v_ref[rows, _hcols(h)] for h in HEADS]
            ms = [_chunk_scalings(ks[h], vs[h], gbv, gbt_v, h) for h in HEADS]
            sts = [st_ref[c, h] for h in HEADS]
            vnb = [vn_ref[rows, _hcols(h)] for h in HEADS]
            tbf = [t_ref[h, rows, :].astype(BF16) for h in HEADS]
            sc = [_chunk_scores(ms[h], qs[h], ks[h]) for h in HEADS]
            xs_ = [_mm_nt(jnp.concatenate([dob[h], dvnb[h]], axis=0), sts[h]) for h in HEADS]
            dai = [_mm_nt(dob[h], vnb[h]) for h in HEADS]
            dkd = [_mm_nt(vnb[h], dsb[h]) for h in HEADS]
            dqg = [xs_[h][0:CH] for h in HEADS]
            duw = [jnp.concatenate([dvnb[h], (-xs_[h][CH:2 * CH]).astype(BF16)], axis=1) for h in HEADS]
            dt = [_mm_nt(duw[h], jnp.concatenate([ms[h]["vb"], ms[h]["kbe"]], axis=1)) for h in HEADS]
            dvk = [_mm_tn(tbf[h], duw[h]) for h in HEADS]
            tdt = [_mm_tn(tbf[h], dt[h]) for h in HEADS]
            da = [jnp.where(strict, -_mm_nt(tdt[h], tbf[h]), 0.0) for h in HEADS]
            dsc = [jnp.concatenate([da[h] * ms[h]["dec"], dai[h] * ms[h]["dec"]], axis=0) for h in HEADS]
            dkq = [_mm(dsc[h], ks[h]) for h in HEADS]
            dk1 = [_mm_tn(dsc[h], jnp.concatenate([ms[h]["kb"], qs[h]], axis=0)) for h in HEADS]
            dgb = jnp.zeros((CH, LANE), F32)
            for h in HEADS:
                m = ms[h]
                eg, ekd, beta = m["eg"], m["ekd"], m["beta"]
                dvb = dvk[h][:, 0:HEAD]
                dkbe = dvk[h][:, HEAD:2 * HEAD]
                kdv = ks[h] * ekd
                dkb = dkq[h][0:CH] + dkbe * eg
                dq_ref[rows, _hcols(h)] = dkq[h][CH:2 * CH] + dqg[h] * eg
                dk_ref[rows, _hcols(h)] = dk1[h] + dkd[h] * ekd + dkb * beta
                dv_ref[rows, _hcols(h)] = dvb * beta
                dkd_kd = dkd[h] * kdv
                dgl = (jnp.exp(m["gl"]) * _rowsum(_colsum(sts[h].astype(F32) * dsb[h].astype(F32)))
                       + _rowsum(_colsum(dkd_kd)))
                mm_ = da[h] * sc[h][0] + dai[h] * sc[h][1]
                dgc = (_rowsum(mm_ - mm_.T) + _rowsum(dqg[h] * qs[h] * eg - dkd_kd + dkbe * m["kbe"])
                       + jnp.where(last, dgl, 0.0))
                dbeta = _rowsum(dkb * ks[h] + dvb * vs[h])
                dgb = jnp.where(lane == h, dgc, jnp.where(lane == DN_H + h, dbeta, dgb))
            dgb_ref[rows, :] = dgb
            return carry

        lax.fori_loop(0, cpb, chunk, 0, unroll=2)

    rev = lambda i: (nblk - 1 - i, 0)
    row = pl.BlockSpec((tb, DN_W), rev)
    vec = pl.BlockSpec((1, HEAD), lambda i: (0, 0))
    sq = pl.BlockSpec((DN_H, tb, CH), lambda i: (0, nblk - 1 - i, 0))
    gbs = pl.BlockSpec((tb, LANE), rev)
    return pl.pallas_call(
        body, name="dn_bwd", grid=(nblk,),
        in_specs=[row, row, pl.BlockSpec((tb, DN_W), lambda i: (nblk - 1 - i, 3)), vec, row, row, row, sq,
                  pl.BlockSpec((cpb, DN_H, LANE), lambda i: (nblk - 1 - i, 0, 0)),
                  row, row, row, gbs, pl.BlockSpec((cpb, 2 * DN_H, CH), lambda i: (nblk - 1 - i, 0, 0)), sq, row,
                  pl.BlockSpec((cpb, DN_H, HEAD, HEAD), lambda i: (nblk - 1 - i, 0, 0, 0))],
        out_specs=[row, row, row, gbs, row, vec],
        out_shape=[_sds((s, DN_W)), _sds((s, DN_W)), _sds((s, DN_W)), _sds((s, LANE)), _sds((s, DN_W), BF16),
                   _sds((1, HEAD))],
        scratch_shapes=[pltpu.VMEM((DN_H, HEAD, HEAD), F32)],
        compiler_params=_params(("arbitrary",)),
    )(dmixed, o, proj, norm_g, w, qg, kd, ai, egl, q, k, v, gb, gbt, t, vn, st)


def _dn_pre_bwd(proj, yc_all, ab, conv_w, alog_row, dt_row, dq, dk, dv, dgb):
    s = proj.shape[0]
    tm = _tile(s, (256, 128))
    w3 = 3 * DN_W
    nblk = s // tm

    def body(x_ref, yc_ref, ab_ref, cw_ref, al_ref, dt_ref, dq_ref, dk_ref, dv_ref, dgb_ref,
             dx_ref, dab_ref, dcw_ref, dal_ref, ddt_ref, exd, carry):
        i = pl.program_id(0)

        @pl.when(i == 0)
        def _():
            carry[...] = jnp.zeros_like(carry)
            dcw_ref[...] = jnp.zeros_like(dcw_ref)
            dal_ref[...] = jnp.zeros_like(dal_ref)
            ddt_ref[...] = jnp.zeros_like(ddt_ref)

        yc = yc_ref[...]
        sg = jax.nn.sigmoid(yc)
        act = yc * sg
        dact = sg * (1.0 + yc * (1.0 - sg))
        for h in range(DN_H):
            cs = slice(h * HEAD, (h + 1) * HEAD)
            ks = slice(DN_W + h * HEAD, DN_W + (h + 1) * HEAD)
            qa = act[:, cs]
            rq = lax.rsqrt(_rowsum(qa * qa) + EPS)
            qh = qa * rq
            dqv = dq_ref[:, cs]
            exd[0:tm, cs] = (HEAD ** -0.5) * rq * (dqv - qh * _rowsum(dqv * qh)) * dact[:, cs]
            ka = act[:, ks]
            rk = lax.rsqrt(_rowsum(ka * ka) + EPS)
            kh = ka * rk
            dkv = dk_ref[:, cs]
            exd[0:tm, ks] = rk * (dkv - kh * _rowsum(dkv * kh)) * dact[:, ks]
        exd[0:tm, 2 * DN_W:w3] = dv_ref[...] * dact[:, 2 * DN_W:w3]
        xv = x_ref[...]
        dyc = exd[...]
        cat = jnp.concatenate([dyc[tm - HALO:tm], carry[...]], axis=0)
        dcw_ref[DN_K - 1:DN_K, :] += _colsum(dyc * xv)
        dx = cw_ref[DN_K - 1:DN_K, :] * dyc
        for t in range(DN_K - 1):
            ahead = DN_K - 1 - t
            view = jnp.concatenate([pltpu.roll(dyc, tm - ahead, 0)[0:tm - HALO],
                                    pltpu.roll(cat, 2 * HALO - ahead, 0)[0:HALO]], axis=0)
            dcw_ref[t:t + 1, :] += _colsum(view * xv)
            dx += cw_ref[t:t + 1, :] * view
        dx_ref[...] = dx.astype(BF16)
        carry[...] = dyc[0:HALO]

        lane = _iota2((tm, LANE), 1)
        dgbv = dgb_ref[...]
        dg = _mm_hi(_chunk_tri(tm, True), jnp.where(lane < DN_H, dgbv, 0.0))
        abv = ab_ref[...]
        xa = abv + dt_ref[...]
        nea = -jnp.exp(al_ref[...])
        d_da = jnp.where(lane < DN_H, dg * nea * jax.nn.sigmoid(xa), 0.0)
        dal_ref[...] += _colsum(jnp.where(lane < DN_H, dg * nea * _softplus(xa), 0.0))
        ddt_ref[...] += _colsum(d_da)
        beta = jax.nn.sigmoid(abv)
        d_db = jnp.where((lane >= DN_H) & (lane < 2 * DN_H), dgbv * beta * (1.0 - beta), 0.0)
        dab_ref[...] = (d_da + d_db).astype(BF16)

    rev = lambda i: (nblk - 1 - i, 0)
    row = lambda w: pl.BlockSpec((tm, w), rev)
    vec = pl.BlockSpec((1, LANE), lambda i: (0, 0))
    cws = pl.BlockSpec((DN_K, w3), lambda i: (0, 0))
    return pl.pallas_call(
        body, name="dn_pre_bwd", grid=(nblk,),
        in_specs=[row(w3), row(w3), row(LANE), cws, vec, vec, row(DN_W), row(DN_W), row(DN_W), row(LANE)],
        out_specs=[row(w3), row(LANE), cws, vec, vec],
        out_shape=[_sds((s, w3), BF16), _sds((s, LANE), BF16), _sds((DN_K, w3)), _sds((1, LANE)), _sds((1, LANE))],
        scratch_shapes=[pltpu.VMEM((tm, w3), F32), pltpu.VMEM((HALO, w3), F32)],
        compiler_params=_params(("arbitrary",)),
    )(proj, yc_all, ab, conv_w, alog_row, dt_row, dq, dk, dv, dgb)


def _adam(parts, w, m, v, name):
    r, c = w.shape
    n_parts = parts.shape[0]
    small = n_parts * r * c * 4 <= 4 * 1024 * 1024
    tr = r if small else _tile(r, (128, 64, 32, 16, 8))

    def body(p_ref, w_ref, m_ref, v_ref, g_ref, d_ref, nm_ref, nv_ref):
        g = p_ref[0].astype(F32)
        for k in range(1, n_parts):
            g = g + p_ref[k].astype(F32)
        g_ref[...] = g
        mn = ADAM_B1 * m_ref[...] + (1.0 - ADAM_B1) * g
        vn = ADAM_B2 * v_ref[...] + (1.0 - ADAM_B2) * (g * g)
        m_hat = mn / (1.0 - ADAM_B1 ** ADAM_STEP)
        v_hat = vn / (1.0 - ADAM_B2 ** ADAM_STEP)
        d_ref[...] = -ADAM_LR * (m_hat / (jnp.sqrt(v_hat) + ADAM_EPS) + ADAM_WD * w_ref[...])
        nm_ref[...] = mn
        nv_ref[...] = vn

    blk = pl.BlockSpec((tr, c), lambda i: (i, 0))
    return pl.pallas_call(
        body, name=name, grid=(r // tr,),
        in_specs=[pl.BlockSpec((n_parts, tr, c), lambda i: (0, i, 0)), blk, blk, blk],
        out_specs=[blk, blk, blk, blk], out_shape=[_sds((r, c))] * 4,
        compiler_params=_params(("parallel",)),
    )(parts, w, m, v)


_PACK_ROWS = 8


def _pack(vals):
    tiles = []
    for a in vals:
        flat = a.reshape(-1).astype(F32)
        unit = _PACK_ROWS * LANE
        n = -(-flat.shape[0] // unit) * unit
        tiles.append(jnp.pad(flat, (0, n - flat.shape[0])).reshape(n // LANE, LANE))
    return jnp.concatenate(tiles, axis=0)


def _unpack(packed, shapes):
    out = []
    r0 = 0
    for shp in shapes:
        size = 1
        for dim in shp:
            size *= dim
        unit = _PACK_ROWS * LANE
        rows = -(-size // unit) * _PACK_ROWS
        out.append(packed[r0:r0 + rows].reshape(-1)[:size].reshape(shp))
        r0 += rows
    return out


def _lane_row(vec8):
    return jnp.pad(vec8.reshape(1, -1).astype(F32), ((0, 0), (0, LANE - vec8.size)))


def kernel(x, mem, ln_g, w_in, gmlp_ln_g, gmlp_ln_b, gmlp_ws, gmlp_bs, conv_w, dn_a_log, dn_dt_bias, dn_norm_g, mem_norm_g, w_mem_kv, w_out, final_g, loss_target, m_ln_g, m_w_in, m_gmlp_ln_g, m_gmlp_ln_b, m_gmlp_ws, m_gmlp_bs, m_conv_w, m_dn_a_log, m_dn_dt_bias, m_dn_norm_g, m_mem_norm_g, m_w_mem_kv, m_w_out, m_final_g, v_ln_g, v_w_in, v_gmlp_ln_g, v_gmlp_ln_b, v_gmlp_ws, v_gmlp_bs, v_conv_w, v_dn_a_log, v_dn_dt_bias, v_dn_norm_g, v_mem_norm_g, v_w_mem_kv, v_w_out, v_final_g):
    xs = x[0]
    mems = mem[0]
    tgt = loss_target[0]
    s, d = xs.shape
    shard_w = w_in.shape[2]
    in_w = N_DEV * shard_w
    me = 4 * lax.axis_index("x") + 2 * lax.axis_index("y") + lax.axis_index("c")

    (g_in,) = _gather_two_level([w_in[0].astype(BF16)], "gather_w_in")
    o_g, o_dn, o_ab = 0, 3 * GMLP_W, 3 * GMLP_W + 4 * DN_W
    o_xa = o_ab + 2 * DN_H

    def shard_cols(g, lo, hi):
        out = []
        while lo < hi:
            sh = lo // shard_w
            end = min(hi, (sh + 1) * shard_w)
            out.append(g[sh][:, lo - sh * shard_w:end - sh * shard_w])
            lo = end
        return out

    def own_layout(g):
        main = jnp.concatenate(shard_cols(g, o_dn, o_ab) + shard_cols(g, o_g, o_dn) + shard_cols(g, o_xa, in_w), axis=1)
        return main, jnp.pad(jnp.concatenate(shard_cols(g, o_ab, o_xa), axis=1), ((0, 0), (0, LANE - 2 * DN_H)))

    w_main, w_ab = own_layout(g_in)

    ln_g2 = ln_g.reshape(1, d)
    lng2 = gmlp_ln_g.reshape(1, GMLP_W)
    lnb2 = gmlp_ln_b.reshape(1, GMLP_W)
    ws3 = gmlp_ws[0]
    bs_t = gmlp_bs[0].T
    alog_row = _lane_row(dn_a_log)
    dt_row = _lane_row(dn_dt_bias)
    dn_g2 = dn_norm_g.reshape(1, HEAD)
    mem_g2 = mem_norm_g.reshape(1, d)
    fin_g2 = final_g.reshape(1, d)

    proj, ab, h_t, (g_out, g_kv, g_conv) = _inproj(
        xs, ln_g2, w_main, w_ab, [w_out[0].astype(BF16), w_mem_kv[0].astype(BF16), conv_w[0]])
    wo = g_out.reshape(MIX_W, d)
    wo_perm = jnp.concatenate([wo[GMLP_W:GMLP_W + DN_W], wo[0:GMLP_W], wo[GMLP_W + DN_W:MIX_W]], axis=0)
    w_kv = g_kv.reshape(d, 2 * XA_W)
    conv_full = g_conv.transpose(1, 0, 2).reshape(DN_K, 3 * DN_W)
    out_a = _gmlp_fwd(proj, lng2, lnb2, ws3, bs_t)
    mkv = _memkv_fwd(mems, mem_g2, w_kv)
    out_c = _xattn_fwd(proj, mkv)
    q, k, v, gb, gbt, yc = _dn_pre(proj, ab, conv_full, alog_row, dt_row)
    wk, qg, kd, tmat, ai, egl, o, vn, st, out_b = _dn_fwd(q, k, v, gb, gbt, proj, dn_g2)

    dx2, dx2b, dmixed, loss_acc, d_fin_g = _final(xs, tgt, out_b, out_a, out_c, wo_perm, fin_g2)

    dwo_b = _matmul_tn(out_b, dx2b, "dw_out_b")
    dwo_a = _matmul_tn(out_a, dx2b, "dw_out_a")
    dwo_c = _matmul_tn(out_c, dx2b, "dw_out_c")
    d_w_out = jnp.concatenate([dwo_a, dwo_b, dwo_c], axis=0)

    dp_g, d_ws, d_bst, d_lng, d_lnb = _gmlp_bwd(proj, dmixed, lng2, lnb2, ws3, bs_t)
    dp_x, dmkv = _xattn_bwd(proj, dmixed, mkv)
    d_w_kv, d_mem_g = _memkv_bwd(mems, mem_g2, w_kv, dmkv)
    dq, dk, dv, dgb, dp_dz, d_dn_g = _dn_bwd(dmixed, o, proj, dn_g2, wk, qg, kd, ai, egl, q, k, v, gb, gbt, tmat, vn, st)
    dp_qkv, dp_ab, d_conv, d_alog, d_dt = _dn_pre_bwd(proj, yc, ab, conv_full, alog_row, dt_row, dq, dk, dv, dgb)

    dw_qkv = _matmul_acc(h_t, dp_qkv, "dw_in_qkv")
    dw_dz = _matmul_acc(h_t, dp_dz, "dw_in_dz")
    dw_gm = _matmul_acc(h_t, dp_g, "dw_in_gmlp")
    dw_xa = _matmul_acc(h_t, dp_x, "dw_in_xa")
    dw_ab = _matmul_acc(h_t, dp_ab, "dw_in_ab")
    segs = [(o_g, dw_gm), (o_dn, dw_qkv), (o_dn + 3 * DN_W, dw_dz), (o_ab, dw_ab[:, :2 * DN_H]), (o_xa, dw_xa)]
    shards = []
    for sh in range(N_DEV):
        lo, hi = sh * shard_w, (sh + 1) * shard_w
        parts = [arr[:, max(lo, off) - off:min(hi, off + arr.shape[1]) - off] for off, arr in segs
                 if off < hi and off + arr.shape[1] > lo]
        shards.append(jnp.concatenate(parts, axis=1).astype(BF16))
    send_in = jnp.stack(shards)

    small_shapes = [(1, 1), gmlp_ln_g.shape, gmlp_ln_b.shape, gmlp_ws.shape, gmlp_bs.shape, dn_a_log.shape,
                    dn_dt_bias.shape, dn_norm_g.shape, mem_norm_g.shape, final_g.shape, (DN_K, 3 * DN_W)]
    small_g = _pack([loss_acc[0:1, 0:1], d_lng, d_lnb, d_ws, d_bst.T, d_alog[:, :DN_H], d_dt[:, :DN_H], d_dn_g, d_mem_g,
                     d_fin_g, d_conv])
    zc = jnp.zeros((DN_K, 3 * DN_W), F32)
    z1 = jnp.zeros((1, 1), F32)
    small_w = _pack([z1, gmlp_ln_g, gmlp_ln_b, gmlp_ws, gmlp_bs, dn_a_log, dn_dt_bias, dn_norm_g, mem_norm_g, final_g, zc])
    small_m = _pack([z1, m_gmlp_ln_g, m_gmlp_ln_b, m_gmlp_ws, m_gmlp_bs, m_dn_a_log, m_dn_dt_bias, m_dn_norm_g,
                     m_mem_norm_g, m_final_g, zc])
    small_v = _pack([z1 + 1.0, v_gmlp_ln_g, v_gmlp_ln_b, v_gmlp_ws, v_gmlp_bs, v_dn_a_log, v_dn_dt_bias, v_dn_norm_g,
                     v_mem_norm_g, v_final_g, zc + 1.0])

    send_out = d_w_out.reshape(N_DEV, MIX_W // N_DEV, d).astype(BF16)
    send_kv = d_w_kv.reshape(N_DEV, d // N_DEV, 2 * XA_W).astype(BF16)
    sends = [send_in, send_out, send_kv]
    all_small, got = _swap_halves(small_g, sends, "swap_halves")
    core = lax.axis_index("c").astype(jnp.int32).reshape(1)
    chip_sums = _pair_sums(core, sends, got)
    dps = [dp_qkv, dp_dz, dp_g, dp_x, dp_ab]
    n_tiles = s // _tile(s, (256, 128))
    head = n_tiles - min(DH_TAIL_TILES, n_tiles - 1)
    gx_head, dg_head, (r_in, r_out, r_kv) = _dh_rms(
        dps, [w_main], [w_ab], xs, dx2, ln_g2, chip_sums, (0, head), "dh_rms")
    grad_x, d_ln_g, _ = _dh_rms(
        dps, [w_main], [w_ab], xs, dx2, ln_g2, [], (head, n_tiles), "dh_rms_tail", prev=(gx_head, dg_head))
    (all_ln_g,) = _gather_two_level([_pack([d_ln_g])], "gather_ln_g")

    g_w_in, dl_w_in, nm_w_in, nv_w_in = _adam(r_in, w_in[0], m_w_in[0], v_w_in[0], "adam_w_in")
    g_w_out, dl_w_out, nm_w_out, nv_w_out = _adam(r_out, w_out[0], m_w_out[0], v_w_out[0], "adam_w_out")
    g_w_kv, dl_w_kv, nm_w_kv, nv_w_kv = _adam(r_kv, w_mem_kv[0], m_w_mem_kv[0], v_w_mem_kv[0], "adam_w_kv")
    sm = [_unpack(t, small_shapes) for t in _adam(all_small, small_w, small_m, small_v, "adam_small")]
    ln_res = [_unpack(t, [ln_g.shape])[0]
              for t in _adam(all_ln_g, _pack([ln_g]), _pack([m_ln_g]), _pack([v_ln_g]), "adam_ln_g")]

    conv_parts = lax.dynamic_slice(all_small, (0, all_small.shape[1] - (DN_K * 3 * DN_W) // LANE, 0),
                                   (N_DEV, (DN_K * 3 * DN_W) // LANE, LANE)).reshape(N_DEV, DN_K, 3 * DN_W)
    cshard = conv_w.shape[2]
    conv_parts = lax.dynamic_slice(conv_parts, (0, 0, me * cshard), (N_DEV, DN_K, cshard))
    cpad = ((0, 0), (0, HALO - DN_K), (0, 0))
    conv_res = _adam(jnp.pad(conv_parts, cpad), jnp.pad(conv_w[0], cpad[1:]), jnp.pad(m_conv_w[0], cpad[1:]),
                     jnp.pad(v_conv_w[0], cpad[1:], constant_values=1.0), "adam_conv")
    g_conv_s, dl_conv, nm_conv, nv_conv = [t[:DN_K][None] for t in conv_res]

    loss = sm[0][0].reshape(())

    def group(idx, big_in, big_conv, big_kv, big_out):
        names = sm[idx][1:]
        return [ln_res[idx], big_in[None], names[0], names[1], names[2], names[3], big_conv, names[4], names[5], names[6],
                names[7], big_kv[None], big_out[None], names[8]]

    grads = group(0, g_w_in, g_conv_s, g_w_kv, g_w_out)
    deltas = group(1, dl_w_in, dl_conv, dl_w_kv, dl_w_out)
    new_m = group(2, nm_w_in, nm_conv, nm_w_kv, nm_w_out)
    new_v = group(3, nv_w_in, nv_conv, nv_w_kv, nv_w_out)
    return (loss, grad_x[None], *grads, *deltas, *new_m, *new_v)
```

```python
import functools

import jax
import jax.numpy as jnp
from jax import lax
from jax.experimental import pallas as pl
from jax.experimental.pallas import tpu as pltpu

F32 = jnp.float32
BF16 = jnp.bfloat16
HIGHEST = lax.Precision.HIGHEST
MESH_ID = pl.DeviceIdType.MESH

N_DEV = 8
EPS = 1e-6
GMLP_W = 512
GMLP_G = 4
GMLP_T = 128
DN_W = 1024
DN_H = 8
HEAD = 128
DN_K = 4
CH = 64
XA_W = 512
XA_H = 4
LANE = 128
HALO = 8
MAIN_W = 4 * DN_W + 3 * GMLP_W + 2 * XA_W
MIX_W = DN_W + GMLP_W + XA_W
VMEM_LIMIT = 56 * 1024 * 1024

ADAM_LR = 0.001
ADAM_B1 = 0.9
ADAM_B2 = 0.999
ADAM_EPS = 1e-08
ADAM_WD = 0.01
ADAM_STEP = 10


def _sds(shape, dtype=F32):
    return jax.ShapeDtypeStruct(tuple(shape), dtype)


def _params(sem=None):
    if sem is None:
        return pltpu.CompilerParams(vmem_limit_bytes=VMEM_LIMIT)
    return pltpu.CompilerParams(dimension_semantics=tuple(sem), vmem_limit_bytes=VMEM_LIMIT)


def _tile(n, prefs):
    for p in prefs:
        if n % p == 0:
            return p
    return n


def _mm(a, b):
    return jnp.dot(a.astype(BF16), b.astype(BF16), preferred_element_type=F32)


def _mm_nt(a, b):
    return lax.dot_general(a.astype(BF16), b.astype(BF16), (((1,), (1,)), ((), ())), preferred_element_type=F32)


def _mm_tn(a, b):
    return lax.dot_general(a.astype(BF16), b.astype(BF16), (((0,), (0,)), ((), ())), preferred_element_type=F32)


def _mm_hi(a, b):
    return jnp.dot(a, b, precision=HIGHEST, preferred_element_type=F32)


def _mm_3x(a, b):
    return jnp.dot(a, b, precision=lax.Precision.HIGH, preferred_element_type=F32)


_GELU_C = 0.7978845608028654
_GELU_A = 0.044715


def _gelu(x):
    return 0.5 * x * (1.0 + jnp.tanh(_GELU_C * (x + _GELU_A * x * x * x)))


def _gelu_grad(x):
    t = jnp.tanh(_GELU_C * (x + _GELU_A * x * x * x))
    return 0.5 * (1.0 + t) + 0.5 * x * (1.0 - t * t) * _GELU_C * (1.0 + 3.0 * _GELU_A * x * x)


def _silu(x):
    return x * jax.nn.sigmoid(x)


def _silu_grad(x):
    s = jax.nn.sigmoid(x)
    return s * (1.0 + x * (1.0 - s))


def _rowsum(x):
    return jnp.sum(x, axis=-1, keepdims=True)


def _colsum(x):
    return jnp.sum(x, axis=0, keepdims=True)


def _iota2(shape, dim):
    return lax.broadcasted_iota(jnp.int32, shape, dim)


def _chunk_tri(tm, upper):
    r = _iota2((tm, tm), 0)
    c = _iota2((tm, tm), 1)
    same = lax.shift_right_logical(r, 6) == lax.shift_right_logical(c, 6)
    tri = (r <= c) if upper else (r >= c)
    return jnp.where(same & tri, 1.0, 0.0).astype(F32)


N_CHIP = 4


def _mesh_place():
    x, y, c = lax.axis_index("x"), lax.axis_index("y"), lax.axis_index("c")
    chips = [(1 - x, y), (x, 1 - y), (1 - x, 1 - y)]
    return x, y, c, (x, y, 1 - c), chips


class _Gather:
    def __init__(self, ins, outs, send_sems, recv_sems, loc_sems):
        self.ins, self.outs, self.send_sems, self.recv_sems, self.loc_sems = ins, outs, send_sems, recv_sems, loc_sems
        self.x, self.y, self.c, self.sib, self.chips = _mesh_place()
        self.me = (self.x, self.y, self.c)
        north = self.c == 1
        self.relay_from = (jnp.where(north, 1 - self.x, self.x), jnp.where(north, self.y, 1 - self.y))
        self.relay_to = (jnp.where(north, self.x, 1 - self.x), jnp.where(north, 1 - self.y, self.y))

    def copy(self, a, k, block, to, src=None):
        slot = self.outs[a].at[4 * block[0] + 2 * block[1] + block[2]]
        return pltpu.make_async_remote_copy(
            src_ref=slot if src is None else src, dst_ref=slot, send_sem=self.send_sems.at[a, k],
            recv_sem=self.recv_sems.at[a, k], device_id=to, device_id_type=MESH_ID)

    def own(self, a):
        return pltpu.make_async_copy(self.ins[a], self.outs[a].at[4 * self.x + 2 * self.y + self.c], self.loc_sems.at[a])

    def first(self, a):
        return [self.copy(a, 0, self.me, self.sib, src=self.ins[a])] + [
            self.copy(a, 1 + j, self.me, (*self.chips[j], self.c), src=self.ins[a]) for j in range(2)]

    def relayed(self, a):
        return self.copy(a, 3, (*self.relay_from, self.c), (*self.relay_to, self.c))

    def passed(self, a, j):
        return self.copy(a, 4 + j, (*self.chips[j], self.c), self.sib)

    def start(self):
        for a in range(len(self.ins)):
            self.own(a).start()
            for cp in self.first(a):
                cp.start()

    def relay(self):
        for a in range(len(self.ins)):
            for j in range(2):
                self.copy(a, 1 + j, (*self.chips[j], self.c), self.me).wait_recv()
            self.relayed(a).start()
            for j in range(2):
                self.passed(a, j).start()

    def finish(self):
        n = len(self.ins)
        for a in range(n):
            self.copy(a, 3, (*self.chips[2], self.c), self.me).wait_recv()
            self.passed(a, 2).start()
        for a in range(n):
            self.copy(a, 0, self.sib, self.me).wait_recv()
            for j, chip in enumerate(self.chips):
                self.copy(a, 4 + j, (*chip, 1 - self.c), self.me).wait_recv()
        for a in range(n):
            for cp in self.first(a) + [self.relayed(a)] + [self.passed(a, j) for j in range(N_CHIP - 1)]:
                cp.wait_send()
            self.own(a).wait()

    @staticmethod
    def sems(n):
        return [pltpu.SemaphoreType.DMA((n, N_DEV - 1)), pltpu.SemaphoreType.DMA((n, N_DEV - 1)),
                pltpu.SemaphoreType.DMA((n,))]


def _gather_two_level(arrs, name):
    n = len(arrs)

    def body(*refs):
        g = _Gather(refs[:n], refs[n:2 * n], *refs[2 * n:])
        g.start()
        g.relay()
        g.finish()

    any_spec = pl.BlockSpec(memory_space=pl.ANY)
    return pl.pallas_call(
        body, name=name, out_shape=[_sds((N_DEV,) + a.shape, a.dtype) for a in arrs],
        in_specs=[any_spec] * n, out_specs=[any_spec] * n, scratch_shapes=_Gather.sems(n),
        compiler_params=pltpu.CompilerParams(has_side_effects=True),
    )(*arrs)


def _swap_halves(small, grads, name):
    n = len(grads)

    def body(*refs):
        small_ref = refs[0]
        ins = refs[1:1 + n]
        small_out = refs[1 + n]
        got = refs[2 + n:2 + 2 * n]
        s_send, s_recv, g_send, g_recv, loc_sem = refs[2 + 2 * n:]
        x, y, c, sib, _ = _mesh_place()
        me = 4 * x + 2 * y + c
        sends, recvs = [], []
        for j in range(1, N_DEV):
            px = 1 - x if (j >> 2) & 1 else x
            py = 1 - y if (j >> 1) & 1 else y
            pc = 1 - c if j & 1 else c
            cp = pltpu.make_async_remote_copy(
                src_ref=small_ref, dst_ref=small_out.at[me], send_sem=s_send.at[j - 1], recv_sem=s_recv.at[j - 1],
                device_id=(px, py, pc), device_id_type=MESH_ID)
            cp.start()
            sends.append(cp)
            recvs.append(pltpu.make_async_remote_copy(
                src_ref=small_ref, dst_ref=small_out.at[4 * px + 2 * py + pc], send_sem=s_send.at[j - 1],
                recv_sem=s_recv.at[j - 1], device_id=(px, py, pc), device_id_type=MESH_ID))
        own = pltpu.make_async_copy(small_ref, small_out.at[me], loc_sem)
        own.start()
        for a in range(n):
            for chip in range(N_CHIP):
                cp = pltpu.make_async_remote_copy(
                    src_ref=ins[a].at[2 * chip + 1 - c], dst_ref=got[a].at[chip], send_sem=g_send.at[a, chip],
                    recv_sem=g_recv.at[a, chip], device_id=sib, device_id_type=MESH_ID)
                cp.start()
                sends.append(cp)
                recvs.append(cp)
        for cp in sends:
            cp.wait_send()
        for cp in recvs:
            cp.wait_recv()
        own.wait()

    half = [_sds((N_CHIP,) + g.shape[1:], g.dtype) for g in grads]
    any_spec = pl.BlockSpec(memory_space=pl.ANY)
    res = pl.pallas_call(
        body, name=name, out_shape=[_sds((N_DEV,) + small.shape, small.dtype)] + half,
        in_specs=[any_spec] * (1 + n), out_specs=[any_spec] * (1 + n),
        scratch_shapes=[pltpu.SemaphoreType.DMA((N_DEV - 1,)), pltpu.SemaphoreType.DMA((N_DEV - 1,)),
                        pltpu.SemaphoreType.DMA((n, N_CHIP)), pltpu.SemaphoreType.DMA((n, N_CHIP)),
                        pltpu.SemaphoreType.DMA],
        compiler_params=pltpu.CompilerParams(has_side_effects=True),
    )(small, *grads)
    return res[0], res[1:]


def _pair_sums(core, mine, got):
    n = len(got)

    def body(core_ref, *refs):
        for a in range(n):
            refs[2 * n + a][...] = (refs[a][...].astype(F32) + refs[n + a][...].astype(F32)).astype(BF16)

    half = lambda g: (1, g.shape[1] // 2, g.shape[2])
    own = [pl.BlockSpec(half(g), lambda i, j, core_ref: (2 * i + core_ref[0], j, 0)) for g in got]
    slot = [pl.BlockSpec(half(g), lambda i, j, core_ref: (i, j, 0)) for g in got]
    return pl.pallas_call(
        body, name="pair_sums", out_shape=[_sds(g.shape, BF16) for g in got],
        grid_spec=pltpu.PrefetchScalarGridSpec(
            num_scalar_prefetch=1, grid=(N_CHIP, 2), in_specs=own + slot, out_specs=slot),
        compiler_params=_params(("parallel", "parallel")),
    )(core, *mine, *got)


class _ChipExchange:
    def __init__(self, ins, outs, send_sems, recv_sems, loc_sems):
        self.ins, self.outs, self.send_sems, self.recv_sems, self.loc_sems = ins, outs, send_sems, recv_sems, loc_sems
        self.x, self.y, self.c, _, self.chips = _mesh_place()
        self.mine = 2 * self.x + self.y

    def own(self, a):
        return pltpu.make_async_copy(self.ins[a].at[self.mine], self.outs[a].at[self.mine], self.loc_sems.at[a])

    def copy(self, a, j, lands_in):
        chip = self.chips[j]
        return pltpu.make_async_remote_copy(
            src_ref=self.ins[a].at[2 * chip[0] + chip[1]], dst_ref=self.outs[a].at[lands_in],
            send_sem=self.send_sems.at[a, j], recv_sem=self.recv_sems.at[a, j], device_id=(*chip, self.c),
            device_id_type=MESH_ID)

    def start(self):
        for a in range(len(self.ins)):
            self.own(a).start()
            for j in range(N_CHIP - 1):
                self.copy(a, j, self.mine).start()

    def finish(self):
        for a in range(len(self.ins)):
            for j, chip in enumerate(self.chips):
                self.copy(a, j, self.mine).wait_send()
                self.copy(a, j, 2 * chip[0] + chip[1]).wait_recv()
            self.own(a).wait()

    @staticmethod
    def sems(n):
        return [pltpu.SemaphoreType.DMA((n, N_CHIP - 1)), pltpu.SemaphoreType.DMA((n, N_CHIP - 1)),
                pltpu.SemaphoreType.DMA((n,))]


def _inproj(x, ln_g, w_main, w_ab, late):
    s, d = x.shape
    n = w_main.shape[1]
    tm = _tile(s, (256, 128))
    tn = _tile(n, (1664, 512, 128))
    nl = len(late)
    ni = s // tm

    def body(*refs):
        x_ref, g_ref, w_ref, wab_ref = refs[:4]
        proj_ref, ab_ref, ht_ref = refs[4 + nl:7 + nl]
        gather = _Gather(refs[4:4 + nl], refs[7 + nl:7 + 2 * nl], *refs[7 + 2 * nl:])
        step = pl.program_id(0)

        @pl.when(step == 0)
        def _():
            gather.start()

        xv = x_ref[...]
        r = lax.rsqrt(jnp.mean(xv * xv, axis=-1, keepdims=True) + EPS)
        hf = xv * r * g_ref[...]
        h = hf.astype(BF16)
        ht_ref[...] = hf.T.astype(BF16)
        ab_ref[...] = jnp.dot(h, wab_ref[...], preferred_element_type=F32)
        for c0 in range(0, n, tn):
            proj_ref[:, c0:c0 + tn] = jnp.dot(h, w_ref[:, c0:c0 + tn], preferred_element_type=F32)

        @pl.when(step == ni // 2)
        def _():
            gather.relay()

        @pl.when(step == ni - 1)
        def _():
            gather.finish()

    any_spec = pl.BlockSpec(memory_space=pl.ANY)
    once = lambda a: pl.BlockSpec(a.shape, lambda i: (0, 0), pipeline_mode=pl.Buffered(1))
    res = pl.pallas_call(
        body, name="inproj", grid=(ni,),
        in_specs=[pl.BlockSpec((tm, d), lambda i: (i, 0)), pl.BlockSpec((1, d), lambda i: (0, 0)), once(w_main),
                  once(w_ab)] + [any_spec] * nl,
        out_specs=[pl.BlockSpec((tm, n), lambda i: (i, 0)), pl.BlockSpec((tm, LANE), lambda i: (i, 0)),
                   pl.BlockSpec((d, tm), lambda i: (0, i))] + [any_spec] * nl,
        out_shape=[_sds((s, n)), _sds((s, LANE)), _sds((d, s), BF16)]
        + [_sds((N_DEV,) + a.shape, a.dtype) for a in late],
        scratch_shapes=_Gather.sems(nl),
        compiler_params=_params(("arbitrary",)),
    )(x, ln_g, w_main, w_ab, *late)
    return res[0], res[1], res[2], res[3:]


def _matmul_acc(a, b, name):
    m, k = a.shape
    n = b.shape[1]
    tm = _tile(m, (2048, 1024, 512, 256, 128))
    tn = _tile(n, (1024, 512, 256, 128))
    tk = _tile(k, (1024, 512, 256, 128))
    nk = k // tk

    def body(a_ref, b_ref, o_ref, acc):
        @pl.when(pl.program_id(2) == 0)
        def _():
            acc[...] = jnp.zeros_like(acc)

        acc[...] += jnp.dot(a_ref[...], b_ref[...], preferred_element_type=F32)

        @pl.when(pl.program_id(2) == nk - 1)
        def _():
            o_ref[...] = acc[...].astype(BF16)

    return pl.pallas_call(
        body, name=name, grid=(m // tm, n // tn, nk),
        in_specs=[pl.BlockSpec((tm, tk), lambda i, j, l: (i, l)), pl.BlockSpec((tk, tn), lambda i, j, l: (l, j))],
        out_specs=pl.BlockSpec((tm, tn), lambda i, j, l: (i, j)),
        out_shape=_sds((m, n), BF16), scratch_shapes=[pltpu.VMEM((tm, tn), F32)],
        compiler_params=_params(("parallel", "parallel", "arbitrary")),
    )(a, b)


def _matmul_tn(a, b, name):
    k, m = a.shape
    n = b.shape[1]
    tm = _tile(m, (1024, 512, 256, 128))
    tn = _tile(n, (1024, 512, 256, 128))
    tk = _tile(k, (1024, 512, 256, 128))
    nk = k // tk

    def body(a_ref, b_ref, o_ref, acc):
        @pl.when(pl.program_id(2) == 0)
        def _():
            acc[...] = jnp.zeros_like(acc)

        acc[...] += _mm_tn(a_ref[...], b_ref[...])

        @pl.when(pl.program_id(2) == nk - 1)
        def _():
            o_ref[...] = acc[...].astype(BF16)

    return pl.pallas_call(
        body, name=name, grid=(m // tm, n // tn, nk),
        in_specs=[pl.BlockSpec((tk, tm), lambda i, j, l: (l, i)), pl.BlockSpec((tk, tn), lambda i, j, l: (l, j))],
        out_specs=pl.BlockSpec((tm, tn), lambda i, j, l: (i, j)),
        out_shape=_sds((m, n), BF16), scratch_shapes=[pltpu.VMEM((tm, tn), F32)],
        compiler_params=_params(("parallel", "parallel", "arbitrary")),
    )(a, b)


def _dh_rms(pieces, w_rows, wab_rows, x, dx2, ln_g, chip_sums):
    s, d = x.shape
    npc = len(pieces)
    nx = len(chip_sums)
    tm = _tile(s, (256, 128))
    ni = s // tm
    widths = [p.shape[1] for p in pieces[:-1]]
    offs = [sum(widths[:p]) for p in range(npc - 1)]
    nw = len(w_rows)
    nin = npc + 2 * nw + 3

    def body(*refs):
        p_refs = refs[:npc]
        w_refs = refs[npc:npc + nw]
        wab_refs = refs[npc + nw:npc + 2 * nw]
        x_ref, dx2_ref, g_ref = refs[npc + 2 * nw:nin]
        gx_ref, dg_ref = refs[nin + nx:nin + nx + 2]
        exch = _ChipExchange(refs[nin:nin + nx], refs[nin + nx + 2:nin + 2 * nx + 2], *refs[nin + 2 * nx + 2:])
        step = pl.program_id(0)

        @pl.when(step == 0)
        def _():
            dg_ref[...] = jnp.zeros_like(dg_ref)
            exch.start()

        cols = []
        for w_ref, wab_ref in zip(w_refs, wab_refs):
            part = _mm_nt(p_refs[npc - 1][...], wab_ref[...])
            for p in range(npc - 1):
                part += _mm_nt(p_refs[p][...], w_ref[:, offs[p]:offs[p] + widths[p]])
            cols.append(part)
        dhv = jnp.concatenate(cols, axis=1)
        xv = x_ref[...]
        r = lax.rsqrt(jnp.mean(xv * xv, axis=-1, keepdims=True) + EPS)
        xhat = xv * r
        dg_ref[...] += _colsum(dhv * xhat)
        dxh = dhv * g_ref[...]
        gx_ref[...] = dx2_ref[...] + r * (dxh - xhat * jnp.mean(dxh * xhat, axis=-1, keepdims=True))

        @pl.when(step == ni - 1)
        def _():
            exch.finish()

    any_spec = pl.BlockSpec(memory_space=pl.ANY)
    row = pl.BlockSpec((tm, d), lambda i: (i, 0))
    vec = pl.BlockSpec((1, d), lambda i: (0, 0))
    once = lambda a: pl.BlockSpec(a.shape, lambda i: (0, 0), pipeline_mode=pl.Buffered(1))
    in_specs = [pl.BlockSpec((tm, p.shape[1]), lambda i: (i, 0)) for p in pieces]
    in_specs += [once(w) for w in w_rows] + [once(w) for w in wab_rows] + [row, row, vec] + [any_spec] * nx
    res = pl.pallas_call(
        body, name="dh_rms", grid=(ni,), in_specs=in_specs,
        out_specs=[row, vec] + [any_spec] * nx,
        out_shape=[_sds((s, d)), _sds((1, d))] + [_sds(p.shape, p.dtype) for p in chip_sums],
        scratch_shapes=_ChipExchange.sems(nx),
        compiler_params=_params(("arbitrary",)),
    )(*pieces, *w_rows, *wab_rows, x, dx2, ln_g, *chip_sums)
    return res[0], res[1], res[2:]


def _final(x, tgt, out_b, out_a, out_c, w_out, final_g):
    s, d = x.shape
    tm = _tile(s, (256, 128))

    def body(x_ref, t_ref, b_ref, a_ref, c_ref, w_ref, g_ref, dx2_ref, dx2b_ref, dm_ref, loss_ref, dg_ref):
        @pl.when(pl.program_id(0) == 0)
        def _():
            loss_ref[...] = jnp.zeros_like(loss_ref)
            dg_ref[...] = jnp.zeros_like(dg_ref)

        x2 = x_ref[...]
        x2 += jnp.dot(b_ref[...], w_ref[0:DN_W, :], preferred_element_type=F32)
        x2 += jnp.dot(a_ref[...], w_ref[DN_W:DN_W + GMLP_W, :], preferred_element_type=F32)
        x2 += jnp.dot(c_ref[...], w_ref[DN_W + GMLP_W:MIX_W, :], preferred_element_type=F32)
        r = lax.rsqrt(jnp.mean(x2 * x2, axis=-1, keepdims=True) + EPS)
        xhat = x2 * r
        g = g_ref[...]
        err = xhat * g - t_ref[...]
        tok = 0.5 * jnp.mean(err * err, axis=-1, keepdims=True)
        loss_ref[...] += jnp.broadcast_to(_colsum(tok), loss_ref.shape)
        dy = err * (1.0 / d)
        dg_ref[...] += _colsum(dy * xhat)
        dxh = dy * g
        dx2 = r * (dxh - xhat * jnp.mean(dxh * xhat, axis=-1, keepdims=True))
        dx2_ref[...] = dx2
        dx2b = dx2.astype(BF16)
        dx2b_ref[...] = dx2b
        dm_ref[...] = _mm_nt(dx2b, w_ref[...])

    row = pl.BlockSpec((tm, d), lambda i: (i, 0))
    vec = pl.BlockSpec((1, d), lambda i: (0, 0))
    return pl.pallas_call(
        body, name="final", grid=(s // tm,),
        in_specs=[row, row, pl.BlockSpec((tm, DN_W), lambda i: (i, 0)), pl.BlockSpec((tm, GMLP_W), lambda i: (i, 0)),
                  pl.BlockSpec((tm, XA_W), lambda i: (i, 0)), pl.BlockSpec((MIX_W, d), lambda i: (0, 0)), vec],
        out_specs=[row, row, pl.BlockSpec((tm, MIX_W), lambda i: (i, 0)), pl.BlockSpec((1, LANE), lambda i: (0, 0)), vec],
        out_shape=[_sds((s, d)), _sds((s, d), BF16), _sds((s, MIX_W)), _sds((1, LANE)), _sds((1, d))],
        compiler_params=_params(("arbitrary",)),
    )(x, tgt, out_b, out_a, out_c, w_out, final_g)


GU_BLK = (4 * DN_W) // GMLP_W


def _gmlp_norm(gv, lng, lnb):
    va = _gelu(gv)
    mu = jnp.mean(va, axis=-1, keepdims=True)
    xc = va - mu
    rstd = lax.rsqrt(jnp.mean(xc * xc, axis=-1, keepdims=True) + EPS)
    vhat = xc * rstd
    return vhat, rstd, vhat * lng + lnb


def _gmlp_fwd(proj, lng, lnb, ws, bs_t):
    s = proj.shape[0]
    tm = _tile(s, (512, 256, 128))

    def body(u_ref, v_ref, z_ref, lng_ref, lnb_ref, ws_ref, bst_ref, o_ref):
        _, _, vn = _gmlp_norm(v_ref[...], lng_ref[...], lnb_ref[...])
        tri = _iota2((GMLP_T, GMLP_T), 0) >= _iota2((GMLP_T, GMLP_T), 1)
        for g in range(GMLP_G):
            cs = slice(g * HEAD, (g + 1) * HEAD)
            w = jnp.where(tri, ws_ref[g], 0.0).astype(BF16)
            b = bst_ref[:, g:g + 1]
            for c in range(tm // GMLP_T):
                rs = slice(c * GMLP_T, (c + 1) * GMLP_T)
                sg = _mm(w, vn[rs, cs]) + b
                o_ref[rs, cs] = (_gelu(u_ref[rs, cs]) * sg * _silu(z_ref[rs, cs])).astype(BF16)

    col = lambda k: pl.BlockSpec((tm, GMLP_W), lambda i: (i, GU_BLK + k))
    vec = pl.BlockSpec((1, GMLP_W), lambda i: (0, 0))
    return pl.pallas_call(
        body, name="gmlp_fwd", grid=(s // tm,),
        in_specs=[col(0), col(1), col(2), vec, vec, pl.BlockSpec((GMLP_G, GMLP_T, GMLP_T), lambda i: (0, 0, 0)),
                  pl.BlockSpec((GMLP_T, GMLP_G), lambda i: (0, 0))],
        out_specs=pl.BlockSpec((tm, GMLP_W), lambda i: (i, 0)), out_shape=_sds((s, GMLP_W), BF16),
        compiler_params=_params(("parallel",)),
    )(proj, proj, proj, lng, lnb, ws, bs_t)


def _gmlp_bwd(proj, dmixed, lng, lnb, ws, bs_t):
    s = proj.shape[0]
    tm = _tile(s, (512, 256, 128))

    def body(u_ref, v_ref, z_ref, d_ref, lng_ref, lnb_ref, ws_ref, bst_ref,
             dp_ref, dws_ref, dbst_ref, dlng_ref, dlnb_ref, dvn):
        @pl.when(pl.program_id(0) == 0)
        def _():
            dws_ref[...] = jnp.zeros_like(dws_ref)
            dbst_ref[...] = jnp.zeros_like(dbst_ref)
            dlng_ref[...] = jnp.zeros_like(dlng_ref)
            dlnb_ref[...] = jnp.zeros_like(dlnb_ref)

        gv = v_ref[...]
        lng_v = lng_ref[...]
        vhat, rstd, vn = _gmlp_norm(gv, lng_v, lnb_ref[...])
        tri = _iota2((GMLP_T, GMLP_T), 0) >= _iota2((GMLP_T, GMLP_T), 1)
        for g in range(GMLP_G):
            cs = slice(g * HEAD, (g + 1) * HEAD)
            w = jnp.where(tri, ws_ref[g], 0.0).astype(BF16)
            b = bst_ref[:, g:g + 1]
            dw_acc = jnp.zeros((GMLP_T, GMLP_T), F32)
            db_acc = jnp.zeros((GMLP_T, 1), F32)
            for c in range(tm // GMLP_T):
                rs = slice(c * GMLP_T, (c + 1) * GMLP_T)
                vn_b = vn[rs, cs]
                sg = _mm(w, vn_b) + b
                gu = u_ref[rs, cs]
                gz = z_ref[rs, cs]
                da = d_ref[rs, cs]
                uact = _gelu(gu)
                sz = _silu(gz)
                ds = da * uact * sz
                dp_ref[rs, cs] = (da * sg * sz * _gelu_grad(gu)).astype(BF16)
                dp_ref[rs, 2 * GMLP_W + g * HEAD:2 * GMLP_W + (g + 1) * HEAD] = (da * uact * sg * _silu_grad(gz)).astype(BF16)
                dw_acc += _mm_nt(ds, vn_b)
                db_acc += _rowsum(ds)
                dvn[rs, cs] = _mm_tn(w, ds)
            dws_ref[g] += jnp.where(tri, dw_acc, 0.0)
            dbst_ref[:, g:g + 1] += db_acc
        dvn_v = dvn[...]
        dlng_ref[...] += _colsum(dvn_v * vhat)
        dlnb_ref[...] += _colsum(dvn_v)
        dvh = dvn_v * lng_v
        dva = rstd * (dvh - jnp.mean(dvh, axis=-1, keepdims=True) - vhat * jnp.mean(dvh * vhat, axis=-1, keepdims=True))
        dp_ref[:, GMLP_W:2 * GMLP_W] = (dva * _gelu_grad(gv)).astype(BF16)

    col = lambda k: pl.BlockSpec((tm, GMLP_W), lambda i: (i, GU_BLK + k))
    vec = pl.BlockSpec((1, GMLP_W), lambda i: (0, 0))
    wsp = pl.BlockSpec((GMLP_G, GMLP_T, GMLP_T), lambda i: (0, 0, 0))
    bsp = pl.BlockSpec((GMLP_T, GMLP_G), lambda i: (0, 0))
    return pl.pallas_call(
        body, name="gmlp_bwd", grid=(s // tm,),
        in_specs=[col(0), col(1), col(2), pl.BlockSpec((tm, GMLP_W), lambda i: (i, DN_W // GMLP_W)), vec, vec, wsp, bsp],
        out_specs=[pl.BlockSpec((tm, 3 * GMLP_W), lambda i: (i, 0)), wsp, bsp, vec, vec],
        out_shape=[_sds((s, 3 * GMLP_W), BF16), _sds((GMLP_G, GMLP_T, GMLP_T)), _sds((GMLP_T, GMLP_G)),
                   _sds((1, GMLP_W)), _sds((1, GMLP_W))],
        scratch_shapes=[pltpu.VMEM((tm, GMLP_W), F32)],
        compiler_params=_params(("arbitrary",)),
    )(proj, proj, proj, dmixed, lng, lnb, ws, bs_t)


CQ_BLK = (4 * DN_W + 3 * GMLP_W) // XA_W


def _memkv_fwd(mem, g, w_kv):
    nm, d = mem.shape

    def body(m_ref, g_ref, w_ref, kv_ref):
        mv = m_ref[...]
        r = lax.rsqrt(jnp.mean(mv * mv, axis=-1, keepdims=True) + EPS)
        kv_ref[...] = _mm(mv * r * g_ref[...], w_ref[...])

    return pl.pallas_call(body, name="memkv_fwd", out_shape=_sds((nm, 2 * XA_W)), compiler_params=_params())(mem, g, w_kv)


def _memkv_bwd(mem, g, w_kv, dkv):
    nm, d = mem.shape

    def body(m_ref, g_ref, w_ref, dkv_ref, dw_ref, dg_ref):
        mv = m_ref[...]
        r = lax.rsqrt(jnp.mean(mv * mv, axis=-1, keepdims=True) + EPS)
        xhat = mv * r
        dkv_v = dkv_ref[...]
        dw_ref[...] = _mm_tn(xhat * g_ref[...], dkv_v)
        dg_ref[...] = _colsum(_mm_nt(dkv_v, w_ref[...]) * xhat)

    return pl.pallas_call(body, name="memkv_bwd", out_shape=[_sds((d, 2 * XA_W)), _sds((1, d))],
                          compiler_params=_params())(mem, g, w_kv, dkv)


def _xattn_probs(q, mk):
    sc = _mm_nt(q, mk) * (HEAD ** -0.5)
    e = jnp.exp(sc - jnp.max(sc, axis=-1, keepdims=True))
    return e / _rowsum(e)


def _xattn_fwd(proj, mkv):
    s = proj.shape[0]
    nm = mkv.shape[0]
    tm = _tile(s, (512, 256, 128))

    def body(q_ref, z_ref, kv_ref, o_ref):
        for h in range(XA_H):
            cs = slice(h * HEAD, (h + 1) * HEAD)
            p = _xattn_probs(q_ref[:, cs], kv_ref[:, cs])
            ctx = _mm(p, kv_ref[:, XA_W + h * HEAD:XA_W + (h + 1) * HEAD])
            o_ref[:, cs] = (ctx * _silu(z_ref[:, cs])).astype(BF16)

    col = lambda k: pl.BlockSpec((tm, XA_W), lambda i: (i, CQ_BLK + k))
    return pl.pallas_call(
        body, name="xattn_fwd", grid=(s // tm,),
        in_specs=[col(0), col(1), pl.BlockSpec((nm, 2 * XA_W), lambda i: (0, 0))],
        out_specs=pl.BlockSpec((tm, XA_W), lambda i: (i, 0)), out_shape=_sds((s, XA_W), BF16),
        compiler_params=_params(("parallel",)),
    )(proj, proj, mkv)


def _xattn_bwd(proj, dmixed, mkv):
    s = proj.shape[0]
    nm = mkv.shape[0]
    tm = _tile(s, (512, 256, 128))

    def body(q_ref, z_ref, d_ref, kv_ref, dp_ref, dkv_ref):
        @pl.when(pl.program_id(0) == 0)
        def _():
            dkv_ref[...] = jnp.zeros_like(dkv_ref)

        for h in range(XA_H):
            cs = slice(h * HEAD, (h + 1) * HEAD)
            vs = slice(XA_W + h * HEAD, XA_W + (h + 1) * HEAD)
            q = q_ref[:, cs]
            z = z_ref[:, cs]
            mk = kv_ref[:, cs]
            mv = kv_ref[:, vs]
            p = _xattn_probs(q, mk)
            ctx = _mm(p, mv)
            dc = d_ref[:, cs]
            dctx = dc * _silu(z)
            dp_ref[:, vs] = (dc * ctx * _silu_grad(z)).astype(BF16)
            dp = _mm_nt(dctx, mv)
            dkv_ref[:, vs] += _mm_tn(p, dctx)
            ds = p * (dp - _rowsum(dp * p)) * (HEAD ** -0.5)
            dp_ref[:, cs] = _mm(ds, mk).astype(BF16)
            dkv_ref[:, cs] += _mm_tn(ds, q)

    col = lambda k: pl.BlockSpec((tm, XA_W), lambda i: (i, CQ_BLK + k))
    kvs = pl.BlockSpec((nm, 2 * XA_W), lambda i: (0, 0))
    return pl.pallas_call(
        body, name="xattn_bwd", grid=(s // tm,),
        in_specs=[col(0), col(1), pl.BlockSpec((tm, XA_W), lambda i: (i, (DN_W + GMLP_W) // XA_W)), kvs],
        out_specs=[pl.BlockSpec((tm, 2 * XA_W), lambda i: (i, 0)), kvs],
        out_shape=[_sds((s, 2 * XA_W), BF16), _sds((nm, 2 * XA_W))],
        compiler_params=_params(("arbitrary",)),
    )(proj, proj, dmixed, mkv)


def _softplus(x):
    return jnp.maximum(x, 0.0) + jnp.log1p(jnp.exp(-jnp.abs(x)))


def _dn_pre(proj, ab, conv_w, alog_row, dt_row):
    s = proj.shape[0]
    tm = _tile(s, (256, 128))
    w3 = 3 * DN_W

    def body(x_ref, halo_ref, ab_ref, cw_ref, al_ref, dt_ref, q_ref, k_ref, v_ref, gb_ref, gbt_ref, yc_ref):
        i = pl.program_id(0)
        xv = x_ref[...]
        cat = jnp.concatenate([jnp.where(i > 0, halo_ref[...], 0.0), xv[0:HALO]], axis=0)
        yc = cw_ref[DN_K - 1:DN_K, :] * xv
        top = cw_ref[DN_K - 1:DN_K, :] * xv[0:HALO]
        for t in range(DN_K - 1):
            back = DN_K - 1 - t
            yc += cw_ref[t:t + 1, :] * pltpu.roll(xv, back, 0)
            top += cw_ref[t:t + 1, :] * pltpu.roll(cat, back, 0)[HALO:2 * HALO]
        yc = jnp.concatenate([top, yc[HALO:tm]], axis=0)
        yc_ref[...] = yc
        act = _silu(yc)
        for h in range(DN_H):
            cs = slice(h * HEAD, (h + 1) * HEAD)
            qa = act[:, cs]
            q_ref[:, cs] = qa * (lax.rsqrt(_rowsum(qa * qa) + EPS) * (HEAD ** -0.5))
            ka = act[:, DN_W + h * HEAD:DN_W + (h + 1) * HEAD]
            k_ref[:, cs] = ka * lax.rsqrt(_rowsum(ka * ka) + EPS)
        v_ref[...] = act[:, 2 * DN_W:w3]
        abv = ab_ref[...]
        lane = _iota2((tm, LANE), 1)
        g = jnp.where(lane < DN_H, -jnp.exp(al_ref[...]) * _softplus(abv + dt_ref[...]), 0.0)
        gc = _mm_hi(_chunk_tri(tm, False), g)
        gbv = jnp.where(lane < DN_H, gc, jnp.where(lane < 2 * DN_H, jax.nn.sigmoid(abv), 0.0))
        gb_ref[...] = gbv
        for c in range(tm // CH):
            gbt_ref[c] = gbv[c * CH:(c + 1) * CH, :].T[0:2 * DN_H, :]

    hb = tm // HALO
    row = lambda w: pl.BlockSpec((tm, w), lambda i: (i, 0))
    vec = pl.BlockSpec((1, LANE), lambda i: (0, 0))
    return pl.pallas_call(
        body, name="dn_pre", grid=(s // tm,),
        in_specs=[row(w3), pl.BlockSpec((HALO, w3), lambda i: (jnp.maximum(i * hb - 1, 0), 0)), row(LANE),
                  pl.BlockSpec((DN_K, w3), lambda i: (0, 0)), vec, vec],
        out_specs=[row(DN_W), row(DN_W), row(DN_W), row(LANE), pl.BlockSpec((tm // CH, 2 * DN_H, CH), lambda i: (i, 0, 0)),
                   row(w3)],
        out_shape=[_sds((s, DN_W)), _sds((s, DN_W)), _sds((s, DN_W)), _sds((s, LANE)),
                   _sds((s // CH, 2 * DN_H, CH)), _sds((s, w3))],
        compiler_params=_params(("parallel",)),
    )(proj, proj, ab, conv_w, alog_row, dt_row)


HEADS = tuple(range(DN_H))


def _hcols(h):
    return slice(h * HEAD, (h + 1) * HEAD)


def _chunk_scalings(k, v, gbv, gbt, h):
    gc = jnp.broadcast_to(gbv[:, h:h + 1], (CH, HEAD))
    beta = jnp.broadcast_to(gbv[:, DN_H + h:DN_H + h + 1], (CH, HEAD))
    gr = gbt[h:h + 1, :]
    ii = _iota2((CH, CH), 0)
    jj = _iota2((CH, CH), 1)
    dec = jnp.exp(jnp.where(ii >= jj, gc[:, 0:CH] - gr, -1e30))
    eg = jnp.exp(gc)
    gl = gr[:, CH - 1:CH]
    kb = k * beta
    return dict(beta=beta, dec=dec, eg=eg, gl=gl, ekd=jnp.exp(gl - gc), kb=kb, vb=v * beta, kbe=kb * eg)


def _chunk_scores(m, q, k):
    kq = _mm_nt(jnp.concatenate([m["kb"], q], axis=0), k)
    strict = _iota2((CH, CH), 0) > _iota2((CH, CH), 1)
    return jnp.where(strict, kq[0:CH] * m["dec"], 0.0), kq[CH:2 * CH] * m["dec"]


def _scan_cpb(s):
    return 8 if (s // CH) % 8 == 0 else 1


def _dn_fwd(q, k, v, gb, gbt, proj, norm_g):
    s = q.shape[0]
    cpb = _scan_cpb(s)
    tb = cpb * CH
    nblk = s // tb

    def body(q_ref, k_ref, v_ref, gb_ref, gbt_ref, z_ref, ng_ref,
             w_ref, qg_ref, kd_ref, t_ref, ai_ref, egl_ref, o_ref, vn_ref, st_ref, ob_ref, state):
        @pl.when(pl.program_id(0) == 0)
        def _():
            state[...] = jnp.zeros_like(state)

        ng = ng_ref[...]
        eye = jnp.where(_iota2((CH, CH), 0) == _iota2((CH, CH), 1), 1.0, 0.0).astype(F32)

        def chunk(c, carry):
            r0 = pl.multiple_of(c * CH, CH)
            rows = pl.ds(r0, CH)
            gbv = gb_ref[rows, :]
            gbt_v = gbt_ref[c]
            qs = [q_ref[rows, _hcols(h)] for h in HEADS]
            ks = [k_ref[rows, _hcols(h)] for h in HEADS]
            ms = [_chunk_scalings(ks[h], v_ref[rows, _hcols(h)], gbv, gbt_v, h) for h in HEADS]
            qgb = [(qs[h] * ms[h]["eg"]).astype(BF16) for h in HEADS]
            kdb = [(ks[h] * ms[h]["ekd"]).astype(BF16) for h in HEADS]
            egl = [jnp.broadcast_to(jnp.exp(ms[h]["gl"]), (1, LANE)) for h in HEADS]
            for h in HEADS:
                qg_ref[rows, _hcols(h)] = qgb[h]
                kd_ref[rows, _hcols(h)] = kdb[h]
                egl_ref[c, h:h + 1, :] = egl[h]
            sc = [_chunk_scores(ms[h], qs[h], ks[h]) for h in HEADS]
            for h in HEADS:
                ai_ref[h, rows, :] = sc[h][1]
            ts = [eye - sc[h][0] for h in HEADS]
            ps = [_mm_3x(sc[h][0], sc[h][0]) for h in HEADS]
            ts = [ts[h] + _mm_3x(ts[h], ps[h]) for h in HEADS]
            for _ in range(4):
                ps = [_mm(ps[h], ps[h]) for h in HEADS]
                ts = [ts[h] + _mm(ts[h], ps[h]) for h in HEADS]
            uw = [_mm(ts[h], jnp.concatenate([ms[h]["vb"], ms[h]["kbe"]], axis=1)) for h in HEADS]
            wb = [uw[h][:, HEAD:2 * HEAD].astype(BF16) for h in HEADS]
            for h in HEADS:
                t_ref[h, rows, :] = ts[h]
                w_ref[rows, _hcols(h)] = wb[h]
            sts = [state[h] for h in HEADS]
            stb = [sts[h].astype(BF16) for h in HEADS]
            for h in HEADS:
                st_ref[c, h] = stb[h]
            vnb = [(uw[h][:, 0:HEAD] - jnp.dot(wb[h], stb[h], preferred_element_type=F32)).astype(BF16) for h in HEADS]
            for h in HEADS:
                state[h] = sts[h] * egl[h] + _mm_tn(kdb[h], vnb[h])
            os_ = [jnp.dot(qgb[h], stb[h], preferred_element_type=F32) + _mm(sc[h][1], vnb[h]) for h in HEADS]
            for h in HEADS:
                o = os_[h]
                vn_ref[rows, _hcols(h)] = vnb[h]
                o_ref[rows, _hcols(h)] = o
                r = lax.rsqrt(jnp.mean(o * o, axis=-1, keepdims=True) + EPS)
                ob_ref[rows, _hcols(h)] = (o * r * ng * _silu(z_ref[rows, _hcols(h)])).astype(BF16)
            return carry

        lax.fori_loop(0, cpb, chunk, 0, unroll=4)

    row = pl.BlockSpec((tb, DN_W), lambda i: (i, 0))
    sq = pl.BlockSpec((DN_H, tb, CH), lambda i: (0, i, 0))
    return pl.pallas_call(
        body, name="dn_fwd", grid=(nblk,),
        in_specs=[row, row, row, pl.BlockSpec((tb, LANE), lambda i: (i, 0)),
                  pl.BlockSpec((cpb, 2 * DN_H, CH), lambda i: (i, 0, 0)), pl.BlockSpec((tb, DN_W), lambda i: (i, 3)),
                  pl.BlockSpec((1, HEAD), lambda i: (0, 0))],
        out_specs=[row, row, row, sq, sq, pl.BlockSpec((cpb, DN_H, LANE), lambda i: (i, 0, 0)), row, row,
                   pl.BlockSpec((cpb, DN_H, HEAD, HEAD), lambda i: (i, 0, 0, 0)), row],
        out_shape=[_sds((s, DN_W), BF16), _sds((s, DN_W), BF16), _sds((s, DN_W), BF16), _sds((DN_H, s, CH)),
                   _sds((DN_H, s, CH)), _sds((s // CH, DN_H, LANE)), _sds((s, DN_W)), _sds((s, DN_W), BF16),
                   _sds((s // CH, DN_H, HEAD, HEAD), BF16), _sds((s, DN_W), BF16)],
        scratch_shapes=[pltpu.VMEM((DN_H, HEAD, HEAD), F32)],
        compiler_params=_params(("arbitrary",)),
    )(q, k, v, gb, gbt, proj, norm_g)


def _dn_bwd(dmixed, o, proj, norm_g, w, qg, kd, ai, egl, q, k, v, gb, gbt, t, vn, st):
    s = o.shape[0]
    cpb = 4 if (s // CH) % 4 == 0 else 1
    tb = cpb * CH
    nblk = s // tb

    def body(dm_ref, o_ref, z_ref, ng_ref, w_ref, qg_ref, kd_ref, ai_ref, egl_ref,
             q_ref, k_ref, v_ref, gb_ref, gbt_ref, t_ref, vn_ref, st_ref,
             dq_ref, dk_ref, dv_ref, dgb_ref, dz_ref, dng_ref, dstate):
        @pl.when(pl.program_id(0) == 0)
        def _():
            dstate[...] = jnp.zeros_like(dstate)
            dng_ref[...] = jnp.zeros_like(dng_ref)

        ng = ng_ref[...]
        lane = _iota2((CH, LANE), 1)
        last = _iota2((CH, 1), 0) == CH - 1
        strict = _iota2((CH, CH), 0) > _iota2((CH, CH), 1)

        def chunk(cc, carry):
            c = cpb - 1 - cc
            r0 = pl.multiple_of(c * CH, CH)
            rows = pl.ds(r0, CH)
            dng = jnp.zeros((1, HEAD), F32)
            dob = []
            for h in HEADS:
                cs = _hcols(h)
                ov = o_ref[rows, cs]
                z = z_ref[rows, cs]
                db = dm_ref[rows, cs]
                r = lax.rsqrt(jnp.mean(ov * ov, axis=-1, keepdims=True) + EPS)
                ohat = ov * r
                dz_ref[rows, cs] = (db * ohat * ng * _silu_grad(z)).astype(BF16)
                dyn = db * _silu(z)
                dng += _colsum(dyn * ohat)
                doh = dyn * ng
                dob.append((r * (doh - ohat * jnp.mean(doh * ohat, axis=-1, keepdims=True))).astype(BF16))
            dng_ref[...] += dng
            dsn = [dstate[h] for h in HEADS]
            dsb = [dsn[h].astype(BF16) for h in HEADS]
            dvnb = [(_mm_tn(ai_ref[h, rows, :], dob[h])
                     + jnp.dot(kd_ref[rows, _hcols(h)], dsb[h], preferred_element_type=F32)).astype(BF16) for h in HEADS]
            part = [_mm_tn(qg_ref[rows, _hcols(h)], dob[h]) + egl_ref[c, h:h + 1, :] * dsn[h] for h in HEADS]
            for h in HEADS:
                dstate[h] = part[h] - _mm_tn(w_ref[rows, _hcols(h)], dvnb[h])
            gbv = gb_ref[rows, :]
            gbt_v = gbt_ref[c]
            qs = [q_ref[rows, _hcols(h)] for h in HEADS]
            ks = [k_ref[rows, _hcols(h)] for h in HEADS]
            vs = [v_ref[rows, _hcols(h)] for h in HEADS]
            ms = [_chunk_scalings(ks[h], vs[h], gbv, gbt_v, h) for h in HEADS]
            sts = [st_ref[c, h] for h in HEADS]
            vnb = [vn_ref[rows, _hcols(h)] for h in HEADS]
            tbf = [t_ref[h, rows, :].astype(BF16) for h in HEADS]
            sc = [_chunk_scores(ms[h], qs[h], ks[h]) for h in HEADS]
            xs_ = [_mm_nt(jnp.concatenate([dob[h], dvnb[h]], axis=0), sts[h]) for h in HEADS]
            dai = [_mm_nt(dob[h], vnb[h]) for h in HEADS]
            dkd = [_mm_nt(vnb[h], dsb[h]) for h in HEADS]
            dqg = [xs_[h][0:CH] for h in HEADS]
            duw = [jnp.concatenate([dvnb[h], (-xs_[h][CH:2 * CH]).astype(BF16)], axis=1) for h in HEADS]
            dt = [_mm_nt(duw[h], jnp.concatenate([ms[h]["vb"], ms[h]["kbe"]], axis=1)) for h in HEADS]
            dvk = [_mm_tn(tbf[h], duw[h]) for h in HEADS]
            tdt = [_mm_tn(tbf[h], dt[h]) for h in HEADS]
            da = [jnp.where(strict, -_mm_nt(tdt[h], tbf[h]), 0.0) for h in HEADS]
            dsc = [jnp.concatenate([da[h] * ms[h]["dec"], dai[h] * ms[h]["dec"]], axis=0) for h in HEADS]
            dkq = [_mm(dsc[h], ks[h]) for h in HEADS]
            dk1 = [_mm_tn(dsc[h], jnp.concatenate([ms[h]["kb"], qs[h]], axis=0)) for h in HEADS]
            dgb = jnp.zeros((CH, LANE), F32)
            for h in HEADS:
                m = ms[h]
                eg, ekd, beta = m["eg"], m["ekd"], m["beta"]
                dvb = dvk[h][:, 0:HEAD]
                dkbe = dvk[h][:, HEAD:2 * HEAD]
                kdv = ks[h] * ekd
                dkb = dkq[h][0:CH] + dkbe * eg
                dq_ref[rows, _hcols(h)] = dkq[h][CH:2 * CH] + dqg[h] * eg
                dk_ref[rows, _hcols(h)] = dk1[h] + dkd[h] * ekd + dkb * beta
                dv_ref[rows, _hcols(h)] = dvb * beta
                dkd_kd = dkd[h] * kdv
                dgl = (jnp.exp(m["gl"]) * _rowsum(_colsum(sts[h].astype(F32) * dsb[h].astype(F32)))
                       + _rowsum(_colsum(dkd_kd)))
                mm_ = da[h] * sc[h][0] + dai[h] * sc[h][1]
                dgc = (_rowsum(mm_ - mm_.T) + _rowsum(dqg[h] * qs[h] * eg - dkd_kd + dkbe * m["kbe"])
                       + jnp.where(last, dgl, 0.0))
                dbeta = _rowsum(dkb * ks[h] + dvb * vs[h])
                dgb = jnp.where(lane == h, dgc, jnp.where(lane == DN_H + h, dbeta, dgb))
            dgb_ref[rows, :] = dgb
            return carry

        lax.fori_loop(0, cpb, chunk, 0, unroll=2)

    rev = lambda i: (nblk - 1 - i, 0)
    row = pl.BlockSpec((tb, DN_W), rev)
    vec = pl.BlockSpec((1, HEAD), lambda i: (0, 0))
    sq = pl.BlockSpec((DN_H, tb, CH), lambda i: (0, nblk - 1 - i, 0))
    gbs = pl.BlockSpec((tb, LANE), rev)
    return pl.pallas_call(
        body, name="dn_bwd", grid=(nblk,),
        in_specs=[row, row, pl.BlockSpec((tb, DN_W), lambda i: (nblk - 1 - i, 3)), vec, row, row, row, sq,
                  pl.BlockSpec((cpb, DN_H, LANE), lambda i: (nblk - 1 - i, 0, 0)),
                  row, row, row, gbs, pl.BlockSpec((cpb, 2 * DN_H, CH), lambda i: (nblk - 1 - i, 0, 0)), sq, row,
                  pl.BlockSpec((cpb, DN_H, HEAD, HEAD), lambda i: (nblk - 1 - i, 0, 0, 0))],
        out_specs=[row, row, row, gbs, row, vec],
        out_shape=[_sds((s, DN_W)), _sds((s, DN_W)), _sds((s, DN_W)), _sds((s, LANE)), _sds((s, DN_W), BF16),
                   _sds((1, HEAD))],
        scratch_shapes=[pltpu.VMEM((DN_H, HEAD, HEAD), F32)],
        compiler_params=_params(("arbitrary",)),
    )(dmixed, o, proj, norm_g, w, qg, kd, ai, egl, q, k, v, gb, gbt, t, vn, st)


def _dn_pre_bwd(proj, yc_all, ab, conv_w, alog_row, dt_row, dq, dk, dv, dgb):
    s = proj.shape[0]
    tm = _tile(s, (256, 128))
    w3 = 3 * DN_W
    nblk = s // tm

    def body(x_ref, yc_ref, ab_ref, cw_ref, al_ref, dt_ref, dq_ref, dk_ref, dv_ref, dgb_ref,
             dx_ref, dab_ref, dcw_ref, dal_ref, ddt_ref, exd, carry):
        i = pl.program_id(0)

        @pl.when(i == 0)
        def _():
            carry[...] = jnp.zeros_like(carry)
            dcw_ref[...] = jnp.zeros_like(dcw_ref)
            dal_ref[...] = jnp.zeros_like(dal_ref)
            ddt_ref[...] = jnp.zeros_like(ddt_ref)

        yc = yc_ref[...]
        sg = jax.nn.sigmoid(yc)
        act = yc * sg
        dact = sg * (1.0 + yc * (1.0 - sg))
        for h in range(DN_H):
            cs = slice(h * HEAD, (h + 1) * HEAD)
            ks = slice(DN_W + h * HEAD, DN_W + (h + 1) * HEAD)
            qa = act[:, cs]
            rq = lax.rsqrt(_rowsum(qa * qa) + EPS)
            qh = qa * rq
            dqv = dq_ref[:, cs]
            exd[0:tm, cs] = (HEAD ** -0.5) * rq * (dqv - qh * _rowsum(dqv * qh)) * dact[:, cs]
            ka = act[:, ks]
            rk = lax.rsqrt(_rowsum(ka * ka) + EPS)
            kh = ka * rk
            dkv = dk_ref[:, cs]
            exd[0:tm, ks] = rk * (dkv - kh * _rowsum(dkv * kh)) * dact[:, ks]
        exd[0:tm, 2 * DN_W:w3] = dv_ref[...] * dact[:, 2 * DN_W:w3]
        xv = x_ref[...]
        dyc = exd[...]
        cat = jnp.concatenate([dyc[tm - HALO:tm], carry[...]], axis=0)
        dcw_ref[DN_K - 1:DN_K, :] += _colsum(dyc * xv)
        dx = cw_ref[DN_K - 1:DN_K, :] * dyc
        for t in range(DN_K - 1):
            ahead = DN_K - 1 - t
            view = jnp.concatenate([pltpu.roll(dyc, tm - ahead, 0)[0:tm - HALO],
                                    pltpu.roll(cat, 2 * HALO - ahead, 0)[0:HALO]], axis=0)
            dcw_ref[t:t + 1, :] += _colsum(view * xv)
            dx += cw_ref[t:t + 1, :] * view
        dx_ref[...] = dx.astype(BF16)
        carry[...] = dyc[0:HALO]

        lane = _iota2((tm, LANE), 1)
        dgbv = dgb_ref[...]
        dg = _mm_hi(_chunk_tri(tm, True), jnp.where(lane < DN_H, dgbv, 0.0))
        abv = ab_ref[...]
        xa = abv + dt_ref[...]
        nea = -jnp.exp(al_ref[...])
        d_da = jnp.where(lane < DN_H, dg * nea * jax.nn.sigmoid(xa), 0.0)
        dal_ref[...] += _colsum(jnp.where(lane < DN_H, dg * nea * _softplus(xa), 0.0))
        ddt_ref[...] += _colsum(d_da)
        beta = jax.nn.sigmoid(abv)
        d_db = jnp.where((lane >= DN_H) & (lane < 2 * DN_H), dgbv * beta * (1.0 - beta), 0.0)
        dab_ref[...] = (d_da + d_db).astype(BF16)

    rev = lambda i: (nblk - 1 - i, 0)
    row = lambda w: pl.BlockSpec((tm, w), rev)
    vec = pl.BlockSpec((1, LANE), lambda i: (0, 0))
    cws = pl.BlockSpec((DN_K, w3), lambda i: (0, 0))
    return pl.pallas_call(
        body, name="dn_pre_bwd", grid=(nblk,),
        in_specs=[row(w3), row(w3), row(LANE), cws, vec, vec, row(DN_W), row(DN_W), row(DN_W), row(LANE)],
        out_specs=[row(w3), row(LANE), cws, vec, vec],
        out_shape=[_sds((s, w3), BF16), _sds((s, LANE), BF16), _sds((DN_K, w3)), _sds((1, LANE)), _sds((1, LANE))],
        scratch_shapes=[pltpu.VMEM((tm, w3), F32), pltpu.VMEM((HALO, w3), F32)],
        compiler_params=_params(("arbitrary",)),
    )(proj, yc_all, ab, conv_w, alog_row, dt_row, dq, dk, dv, dgb)


def _adam(parts, w, m, v, name):
    r, c = w.shape
    n_parts = parts.shape[0]
    small = n_parts * r * c * 4 <= 4 * 1024 * 1024
    tr = r if small else _tile(r, (128, 64, 32, 16, 8))

    def body(p_ref, w_ref, m_ref, v_ref, g_ref, d_ref, nm_ref, nv_ref):
        g = p_ref[0].astype(F32)
        for k in range(1, n_parts):
            g = g + p_ref[k].astype(F32)
        g_ref[...] = g
        mn = ADAM_B1 * m_ref[...] + (1.0 - ADAM_B1) * g
        vn = ADAM_B2 * v_ref[...] + (1.0 - ADAM_B2) * (g * g)
        m_hat = mn / (1.0 - ADAM_B1 ** ADAM_STEP)
        v_hat = vn / (1.0 - ADAM_B2 ** ADAM_STEP)
        d_ref[...] = -ADAM_LR * (m_hat / (jnp.sqrt(v_hat) + ADAM_EPS) + ADAM_WD * w_ref[...])
        nm_ref[...] = mn
        nv_ref[...] = vn

    blk = pl.BlockSpec((tr, c), lambda i: (i, 0))
    return pl.pallas_call(
        body, name=name, grid=(r // tr,),
        in_specs=[pl.BlockSpec((n_parts, tr, c), lambda i: (0, i, 0)), blk, blk, blk],
        out_specs=[blk, blk, blk, blk], out_shape=[_sds((r, c))] * 4,
        compiler_params=_params(("parallel",)),
    )(parts, w, m, v)


_PACK_ROWS = 8


def _pack(vals):
    tiles = []
    for a in vals:
        flat = a.reshape(-1).astype(F32)
        unit = _PACK_ROWS * LANE
        n = -(-flat.shape[0] // unit) * unit
        tiles.append(jnp.pad(flat, (0, n - flat.shape[0])).reshape(n // LANE, LANE))
    return jnp.concatenate(tiles, axis=0)


def _unpack(packed, shapes):
    out = []
    r0 = 0
    for shp in shapes:
        size = 1
        for dim in shp:
            size *= dim
        unit = _PACK_ROWS * LANE
        rows = -(-size // unit) * _PACK_ROWS
        out.append(packed[r0:r0 + rows].reshape(-1)[:size].reshape(shp))
        r0 += rows
    return out


def _lane_row(vec8):
    return jnp.pad(vec8.reshape(1, -1).astype(F32), ((0, 0), (0, LANE - vec8.size)))


def kernel(x, mem, ln_g, w_in, gmlp_ln_g, gmlp_ln_b, gmlp_ws, gmlp_bs, conv_w, dn_a_log, dn_dt_bias, dn_norm_g, mem_norm_g, w_mem_kv, w_out, final_g, loss_target, m_ln_g, m_w_in, m_gmlp_ln_g, m_gmlp_ln_b, m_gmlp_ws, m_gmlp_bs, m_conv_w, m_dn_a_log, m_dn_dt_bias, m_dn_norm_g, m_mem_norm_g, m_w_mem_kv, m_w_out, m_final_g, v_ln_g, v_w_in, v_gmlp_ln_g, v_gmlp_ln_b, v_gmlp_ws, v_gmlp_bs, v_conv_w, v_dn_a_log, v_dn_dt_bias, v_dn_norm_g, v_mem_norm_g, v_w_mem_kv, v_w_out, v_final_g):
    xs = x[0]
    mems = mem[0]
    tgt = loss_target[0]
    s, d = xs.shape
    shard_w = w_in.shape[2]
    in_w = N_DEV * shard_w
    me = 4 * lax.axis_index("x") + 2 * lax.axis_index("y") + lax.axis_index("c")

    (g_in,) = _gather_two_level([w_in[0].astype(BF16)], "gather_w_in")
    o_g, o_dn, o_ab = 0, 3 * GMLP_W, 3 * GMLP_W + 4 * DN_W
    o_xa = o_ab + 2 * DN_H

    def shard_cols(g, lo, hi):
        out = []
        while lo < hi:
            sh = lo // shard_w
            end = min(hi, (sh + 1) * shard_w)
            out.append(g[sh][:, lo - sh * shard_w:end - sh * shard_w])
            lo = end
        return out

    def own_layout(g):
        main = jnp.concatenate(shard_cols(g, o_dn, o_ab) + shard_cols(g, o_g, o_dn) + shard_cols(g, o_xa, in_w), axis=1)
        return main, jnp.pad(jnp.concatenate(shard_cols(g, o_ab, o_xa), axis=1), ((0, 0), (0, LANE - 2 * DN_H)))

    w_main, w_ab = own_layout(g_in)

    ln_g2 = ln_g.reshape(1, d)
    lng2 = gmlp_ln_g.reshape(1, GMLP_W)
    lnb2 = gmlp_ln_b.reshape(1, GMLP_W)
    ws3 = gmlp_ws[0]
    bs_t = gmlp_bs[0].T
    alog_row = _lane_row(dn_a_log)
    dt_row = _lane_row(dn_dt_bias)
    dn_g2 = dn_norm_g.reshape(1, HEAD)
    mem_g2 = mem_norm_g.reshape(1, d)
    fin_g2 = final_g.reshape(1, d)

    proj, ab, h_t, (g_out, g_kv, g_conv) = _inproj(
        xs, ln_g2, w_main, w_ab, [w_out[0].astype(BF16), w_mem_kv[0].astype(BF16), conv_w[0]])
    wo = g_out.reshape(MIX_W, d)
    wo_perm = jnp.concatenate([wo[GMLP_W:GMLP_W + DN_W], wo[0:GMLP_W], wo[GMLP_W + DN_W:MIX_W]], axis=0)
    w_kv = g_kv.reshape(d, 2 * XA_W)
    conv_full = g_conv.transpose(1, 0, 2).reshape(DN_K, 3 * DN_W)
    out_a = _gmlp_fwd(proj, lng2, lnb2, ws3, bs_t)
    mkv = _memkv_fwd(mems, mem_g2, w_kv)
    out_c = _xattn_fwd(proj, mkv)
    q, k, v, gb, gbt, yc = _dn_pre(proj, ab, conv_full, alog_row, dt_row)
    wk, qg, kd, tmat, ai, egl, o, vn, st, out_b = _dn_fwd(q, k, v, gb, gbt, proj, dn_g2)

    dx2, dx2b, dmixed, loss_acc, d_fin_g = _final(xs, tgt, out_b, out_a, out_c, wo_perm, fin_g2)

    dwo_b = _matmul_tn(out_b, dx2b, "dw_out_b")
    dwo_a = _matmul_tn(out_a, dx2b, "dw_out_a")
    dwo_c = _matmul_tn(out_c, dx2b, "dw_out_c")
    d_w_out = jnp.concatenate([dwo_a, dwo_b, dwo_c], axis=0)

    dp_g, d_ws, d_bst, d_lng, d_lnb = _gmlp_bwd(proj, dmixed, lng2, lnb2, ws3, bs_t)
    dp_x, dmkv = _xattn_bwd(proj, dmixed, mkv)
    d_w_kv, d_mem_g = _memkv_bwd(mems, mem_g2, w_kv, dmkv)
    dq, dk, dv, dgb, dp_dz, d_dn_g = _dn_bwd(dmixed, o, proj, dn_g2, wk, qg, kd, ai, egl, q, k, v, gb, gbt, tmat, vn, st)
    dp_qkv, dp_ab, d_conv, d_alog, d_dt = _dn_pre_bwd(proj, yc, ab, conv_full, alog_row, dt_row, dq, dk, dv, dgb)

    dw_qkv = _matmul_acc(h_t, dp_qkv, "dw_in_qkv")
    dw_dz = _matmul_acc(h_t, dp_dz, "dw_in_dz")
    dw_gm = _matmul_acc(h_t, dp_g, "dw_in_gmlp")
    dw_xa = _matmul_acc(h_t, dp_x, "dw_in_xa")
    dw_ab = _matmul_acc(h_t, dp_ab, "dw_in_ab")
    segs = [(o_g, dw_gm), (o_dn, dw_qkv), (o_dn + 3 * DN_W, dw_dz), (o_ab, dw_ab[:, :2 * DN_H]), (o_xa, dw_xa)]
    shards = []
    for sh in range(N_DEV):
        lo, hi = sh * shard_w, (sh + 1) * shard_w
        parts = [arr[:, max(lo, off) - off:min(hi, off + arr.shape[1]) - off] for off, arr in segs
                 if off < hi and off + arr.shape[1] > lo]
        shards.append(jnp.concatenate(parts, axis=1).astype(BF16))
    send_in = jnp.stack(shards)

    small_shapes = [(1, 1), gmlp_ln_g.shape, gmlp_ln_b.shape, gmlp_ws.shape, gmlp_bs.shape, dn_a_log.shape,
                    dn_dt_bias.shape, dn_norm_g.shape, mem_norm_g.shape, final_g.shape, (DN_K, 3 * DN_W)]
    small_g = _pack([loss_acc[0:1, 0:1], d_lng, d_lnb, d_ws, d_bst.T, d_alog[:, :DN_H], d_dt[:, :DN_H], d_dn_g, d_mem_g,
                     d_fin_g, d_conv])
    zc = jnp.zeros((DN_K, 3 * DN_W), F32)
    z1 = jnp.zeros((1, 1), F32)
    small_w = _pack([z1, gmlp_ln_g, gmlp_ln_b, gmlp_ws, gmlp_bs, dn_a_log, dn_dt_bias, dn_norm_g, mem_norm_g, final_g, zc])
    small_m = _pack([z1, m_gmlp_ln_g, m_gmlp_ln_b, m_gmlp_ws, m_gmlp_bs, m_dn_a_log, m_dn_dt_bias, m_dn_norm_g,
                     m_mem_norm_g, m_final_g, zc])
    small_v = _pack([z1 + 1.0, v_gmlp_ln_g, v_gmlp_ln_b, v_gmlp_ws, v_gmlp_bs, v_dn_a_log, v_dn_dt_bias, v_dn_norm_g,
                     v_mem_norm_g, v_final_g, zc + 1.0])

    send_out = d_w_out.reshape(N_DEV, MIX_W // N_DEV, d).astype(BF16)
    send_kv = d_w_kv.reshape(N_DEV, d // N_DEV, 2 * XA_W).astype(BF16)
    sends = [send_in, send_out, send_kv]
    all_small, got = _swap_halves(small_g, sends, "swap_halves")
    core = lax.axis_index("c").astype(jnp.int32).reshape(1)
    chip_sums = _pair_sums(core, sends, got)
    grad_x, d_ln_g, (r_in, r_out, r_kv) = _dh_rms(
        [dp_qkv, dp_dz, dp_g, dp_x, dp_ab], [w_main], [w_ab], xs, dx2, ln_g2, chip_sums)
    (all_ln_g,) = _gather_two_level([_pack([d_ln_g])], "gather_ln_g")

    g_w_in, dl_w_in, nm_w_in, nv_w_in = _adam(r_in, w_in[0], m_w_in[0], v_w_in[0], "adam_w_in")
    g_w_out, dl_w_out, nm_w_out, nv_w_out = _adam(r_out, w_out[0], m_w_out[0], v_w_out[0], "adam_w_out")
    g_w_kv, dl_w_kv, nm_w_kv, nv_w_kv = _adam(r_kv, w_mem_kv[0], m_w_mem_kv[0], v_w_mem_kv[0], "adam_w_kv")
    sm = [_unpack(t, small_shapes) for t in _adam(all_small, small_w, small_m, small_v, "adam_small")]
    ln_res = [_unpack(t, [ln_g.shape])[0]
              for t in _adam(all_ln_g, _pack([ln_g]), _pack([m_ln_g]), _pack([v_ln_g]), "adam_ln_g")]

    conv_parts = lax.dynamic_slice(all_small, (0, all_small.shape[1] - (DN_K * 3 * DN_W) // LANE, 0),
                                   (N_DEV, (DN_K * 3 * DN_W) // LANE, LANE)).reshape(N_DEV, DN_K, 3 * DN_W)
    cshard = conv_w.shape[2]
    conv_parts = lax.dynamic_slice(conv_parts, (0, 0, me * cshard), (N_DEV, DN_K, cshard))
    cpad = ((0, 0), (0, HALO - DN_K), (0, 0))
    conv_res = _adam(jnp.pad(conv_parts, cpad), jnp.pad(conv_w[0], cpad[1:]), jnp.pad(m_conv_w[0], cpad[1:]),
                     jnp.pad(v_conv_w[0], cpad[1:], constant_values=1.0), "adam_conv")
    g_conv_s, dl_conv, nm_conv, nv_conv = [t[:DN_K][None] for t in conv_res]

    loss = sm[0][0].reshape(())

    def group(idx, big_in, big_conv, big_kv, big_out):
        names = sm[idx][1:]
        return [ln_res[idx], big_in[None], names[0], names[1], names[2], names[3], big_conv, names[4], names[5], names[6],
                names[7], big_kv[None], big_out[None], names[8]]

    grads = group(0, g_w_in, g_conv_s, g_w_kv, g_w_out)
    deltas = group(1, dl_w_in, dl_conv, dl_w_kv, dl_w_out)
    new_m = group(2, nm_w_in, nm_conv, nm_w_kv, nm_w_out)
    new_v = group(3, nv_w_in, nv_conv, nv_w_kv, nv_w_out)
    return (loss, grad_x[None], *grads, *deltas, *new_m, *new_v)
```

```python
import functools

import jax
import jax.numpy as jnp
from jax import lax
from jax.experimental import pallas as pl
from jax.experimental.pallas import tpu as pltpu

F32 = jnp.float32
BF16 = jnp.bfloat16
HIGHEST = lax.Precision.HIGHEST
MESH_ID = pl.DeviceIdType.MESH

N_DEV = 8
EPS = 1e-6
GMLP_W = 512
GMLP_G = 4
GMLP_T = 128
DN_W = 1024
DN_H = 8
HEAD = 128
DN_K = 4
CH = 64
XA_W = 512
XA_H = 4
LANE = 128
HALO = 8
MAIN_W = 4 * DN_W + 3 * GMLP_W + 2 * XA_W
MIX_W = DN_W + GMLP_W + XA_W
VMEM_LIMIT = 56 * 1024 * 1024

ADAM_LR = 0.001
ADAM_B1 = 0.9
ADAM_B2 = 0.999
ADAM_EPS = 1e-08
ADAM_WD = 0.01
ADAM_STEP = 10


def _sds(shape, dtype=F32):
    return jax.ShapeDtypeStruct(tuple(shape), dtype)


def _params(sem=None):
    if sem is None:
        return pltpu.CompilerParams(vmem_limit_bytes=VMEM_LIMIT)
    return pltpu.CompilerParams(dimension_semantics=tuple(sem), vmem_limit_bytes=VMEM_LIMIT)


def _tile(n, prefs):
    for p in prefs:
        if n % p == 0:
            return p
    return n


def _mm(a, b):
    return jnp.dot(a.astype(BF16), b.astype(BF16), preferred_element_type=F32)


def _mm_nt(a, b):
    return lax.dot_general(a.astype(BF16), b.astype(BF16), (((1,), (1,)), ((), ())), preferred_element_type=F32)


def _mm_tn(a, b):
    return lax.dot_general(a.astype(BF16), b.astype(BF16), (((0,), (0,)), ((), ())), preferred_element_type=F32)


def _mm_hi(a, b):
    return jnp.dot(a, b, precision=HIGHEST, preferred_element_type=F32)


def _mm_3x(a, b):
    return jnp.dot(a, b, precision=lax.Precision.HIGH, preferred_element_type=F32)


_GELU_C = 0.7978845608028654
_GELU_A = 0.044715


def _gelu(x):
    return 0.5 * x * (1.0 + jnp.tanh(_GELU_C * (x + _GELU_A * x * x * x)))


def _gelu_grad(x):
    t = jnp.tanh(_GELU_C * (x + _GELU_A * x * x * x))
    return 0.5 * (1.0 + t) + 0.5 * x * (1.0 - t * t) * _GELU_C * (1.0 + 3.0 * _GELU_A * x * x)


def _silu(x):
    return x * jax.nn.sigmoid(x)


def _silu_grad(x):
    s = jax.nn.sigmoid(x)
    return s * (1.0 + x * (1.0 - s))


def _rowsum(x):
    return jnp.sum(x, axis=-1, keepdims=True)


def _colsum(x):
    return jnp.sum(x, axis=0, keepdims=True)


def _iota2(shape, dim):
    return lax.broadcasted_iota(jnp.int32, shape, dim)


def _chunk_tri(tm, upper):
    r = _iota2((tm, tm), 0)
    c = _iota2((tm, tm), 1)
    same = lax.shift_right_logical(r, 6) == lax.shift_right_logical(c, 6)
    tri = (r <= c) if upper else (r >= c)
    return jnp.where(same & tri, 1.0, 0.0).astype(F32)


N_CHIP = 4


def _mesh_place():
    x, y, c = lax.axis_index("x"), lax.axis_index("y"), lax.axis_index("c")
    chips = [(1 - x, y), (x, 1 - y), (1 - x, 1 - y)]
    return x, y, c, (x, y, 1 - c), chips


class _Gather:
    def __init__(self, ins, outs, send_sems, recv_sems, loc_sems):
        self.ins, self.outs, self.send_sems, self.recv_sems, self.loc_sems = ins, outs, send_sems, recv_sems, loc_sems
        self.x, self.y, self.c, self.sib, self.chips = _mesh_place()
        self.me = (self.x, self.y, self.c)
        north = self.c == 1
        self.relay_from = (jnp.where(north, 1 - self.x, self.x), jnp.where(north, self.y, 1 - self.y))
        self.relay_to = (jnp.where(north, self.x, 1 - self.x), jnp.where(north, 1 - self.y, self.y))

    def copy(self, a, k, block, to, src=None):
        slot = self.outs[a].at[4 * block[0] + 2 * block[1] + block[2]]
        return pltpu.make_async_remote_copy(
            src_ref=slot if src is None else src, dst_ref=slot, send_sem=self.send_sems.at[a, k],
            recv_sem=self.recv_sems.at[a, k], device_id=to, device_id_type=MESH_ID)

    def own(self, a):
        return pltpu.make_async_copy(self.ins[a], self.outs[a].at[4 * self.x + 2 * self.y + self.c], self.loc_sems.at[a])

    def first(self, a):
        return [self.copy(a, 0, self.me, self.sib, src=self.ins[a])] + [
            self.copy(a, 1 + j, self.me, (*self.chips[j], self.c), src=self.ins[a]) for j in range(2)]

    def relayed(self, a):
        return self.copy(a, 3, (*self.relay_from, self.c), (*self.relay_to, self.c))

    def passed(self, a, j):
        return self.copy(a, 4 + j, (*self.chips[j], self.c), self.sib)

    def start(self):
        for a in range(len(self.ins)):
            self.own(a).start()
            for cp in self.first(a):
                cp.start()

    def relay(self):
        for a in range(len(self.ins)):
            for j in range(2):
                self.copy(a, 1 + j, (*self.chips[j], self.c), self.me).wait_recv()
            self.relayed(a).start()
            for j in range(2):
                self.passed(a, j).start()

    def finish(self):
        n = len(self.ins)
        for a in range(n):
            self.copy(a, 3, (*self.chips[2], self.c), self.me).wait_recv()
            self.passed(a, 2).start()
        for a in range(n):
            self.copy(a, 0, self.sib, self.me).wait_recv()
            for j, chip in enumerate(self.chips):
                self.copy(a, 4 + j, (*chip, 1 - self.c), self.me).wait_recv()
        for a in range(n):
            for cp in self.first(a) + [self.relayed(a)] + [self.passed(a, j) for j in range(N_CHIP - 1)]:
                cp.wait_send()
            self.own(a).wait()

    @staticmethod
    def sems(n):
        return [pltpu.SemaphoreType.DMA((n, N_DEV - 1)), pltpu.SemaphoreType.DMA((n, N_DEV - 1)),
                pltpu.SemaphoreType.DMA((n,))]


def _gather_two_level(arrs, name):
    n = len(arrs)

    def body(*refs):
        g = _Gather(refs[:n], refs[n:2 * n], *refs[2 * n:])
        g.start()
        g.relay()
        g.finish()

    any_spec = pl.BlockSpec(memory_space=pl.ANY)
    return pl.pallas_call(
        body, name=name, out_shape=[_sds((N_DEV,) + a.shape, a.dtype) for a in arrs],
        in_specs=[any_spec] * n, out_specs=[any_spec] * n, scratch_shapes=_Gather.sems(n),
        compiler_params=pltpu.CompilerParams(has_side_effects=True),
    )(*arrs)


def _swap_halves(small, grads, name):
    n = len(grads)

    def body(*refs):
        small_ref = refs[0]
        ins = refs[1:1 + n]
        small_out = refs[1 + n]
        got = refs[2 + n:2 + 2 * n]
        s_send, s_recv, g_send, g_recv, loc_sem = refs[2 + 2 * n:]
        x, y, c, sib, _ = _mesh_place()
        me = 4 * x + 2 * y + c
        sends, recvs = [], []
        for j in range(1, N_DEV):
            px = 1 - x if (j >> 2) & 1 else x
            py = 1 - y if (j >> 1) & 1 else y
            pc = 1 - c if j & 1 else c
            cp = pltpu.make_async_remote_copy(
                src_ref=small_ref, dst_ref=small_out.at[me], send_sem=s_send.at[j - 1], recv_sem=s_recv.at[j - 1],
                device_id=(px, py, pc), device_id_type=MESH_ID)
            cp.start()
            sends.append(cp)
            recvs.append(pltpu.make_async_remote_copy(
                src_ref=small_ref, dst_ref=small_out.at[4 * px + 2 * py + pc], send_sem=s_send.at[j - 1],
                recv_sem=s_recv.at[j - 1], device_id=(px, py, pc), device_id_type=MESH_ID))
        own = pltpu.make_async_copy(small_ref, small_out.at[me], loc_sem)
        own.start()
        for a in range(n):
            for chip in range(N_CHIP):
                cp = pltpu.make_async_remote_copy(
                    src_ref=ins[a].at[2 * chip + 1 - c], dst_ref=got[a].at[chip], send_sem=g_send.at[a, chip],
                    recv_sem=g_recv.at[a, chip], device_id=sib, device_id_type=MESH_ID)
                cp.start()
                sends.append(cp)
                recvs.append(cp)
        for cp in sends:
            cp.wait_send()
        for cp in recvs:
            cp.wait_recv()
        own.wait()

    half = [_sds((N_CHIP,) + g.shape[1:], g.dtype) for g in grads]
    any_spec = pl.BlockSpec(memory_space=pl.ANY)
    res = pl.pallas_call(
        body, name=name, out_shape=[_sds((N_DEV,) + small.shape, small.dtype)] + half,
        in_specs=[any_spec] * (1 + n), out_specs=[any_spec] * (1 + n),
        scratch_shapes=[pltpu.SemaphoreType.DMA((N_DEV - 1,)), pltpu.SemaphoreType.DMA((N_DEV - 1,)),
                        pltpu.SemaphoreType.DMA((n, N_CHIP)), pltpu.SemaphoreType.DMA((n, N_CHIP)),
                        pltpu.SemaphoreType.DMA],
        compiler_params=pltpu.CompilerParams(has_side_effects=True),
    )(small, *grads)
    return res[0], res[1:]


def _pair_sums(core, mine, got):
    n = len(got)

    def body(core_ref, *refs):
        for a in range(n):
            refs[2 * n + a][...] = (refs[a][...].astype(F32) + refs[n + a][...].astype(F32)).astype(BF16)

    half = lambda g: (1, g.shape[1] // 2, g.shape[2])
    own = [pl.BlockSpec(half(g), lambda i, j, core_ref: (2 * i + core_ref[0], j, 0)) for g in got]
    slot = [pl.BlockSpec(half(g), lambda i, j, core_ref: (i, j, 0)) for g in got]
    return pl.pallas_call(
        body, name="pair_sums", out_shape=[_sds(g.shape, BF16) for g in got],
        grid_spec=pltpu.PrefetchScalarGridSpec(
            num_scalar_prefetch=1, grid=(N_CHIP, 2), in_specs=own + slot, out_specs=slot),
        compiler_params=_params(("parallel", "parallel")),
    )(core, *mine, *got)


class _ChipExchange:
    def __init__(self, ins, outs, send_sems, recv_sems, loc_sems):
        self.ins, self.outs, self.send_sems, self.recv_sems, self.loc_sems = ins, outs, send_sems, recv_sems, loc_sems
        self.x, self.y, self.c, _, self.chips = _mesh_place()
        self.mine = 2 * self.x + self.y

    def own(self, a):
        return pltpu.make_async_copy(self.ins[a].at[self.mine], self.outs[a].at[self.mine], self.loc_sems.at[a])

    def copy(self, a, j, lands_in):
        chip = self.chips[j]
        return pltpu.make_async_remote_copy(
            src_ref=self.ins[a].at[2 * chip[0] + chip[1]], dst_ref=self.outs[a].at[lands_in],
            send_sem=self.send_sems.at[a, j], recv_sem=self.recv_sems.at[a, j], device_id=(*chip, self.c),
            device_id_type=MESH_ID)

    def start(self):
        for a in range(len(self.ins)):
            self.own(a).start()
            for j in range(N_CHIP - 1):
                self.copy(a, j, self.mine).start()

    def finish(self):
        for a in range(len(self.ins)):
            for j, chip in enumerate(self.chips):
                self.copy(a, j, self.mine).wait_send()
                self.copy(a, j, 2 * chip[0] + chip[1]).wait_recv()
            self.own(a).wait()

    @staticmethod
    def sems(n):
        return [pltpu.SemaphoreType.DMA((n, N_CHIP - 1)), pltpu.SemaphoreType.DMA((n, N_CHIP - 1)),
                pltpu.SemaphoreType.DMA((n,))]


def _inproj(x, ln_g, w_main, w_ab, late):
    s, d = x.shape
    n = w_main.shape[1]
    tm = _tile(s, (256, 128))
    tn = _tile(n, (1664, 512, 128))
    nl = len(late)
    ni = s // tm

    def body(*refs):
        x_ref, g_ref, w_ref, wab_ref = refs[:4]
        proj_ref, ab_ref, ht_ref = refs[4 + nl:7 + nl]
        gather = _Gather(refs[4:4 + nl], refs[7 + nl:7 + 2 * nl], *refs[7 + 2 * nl:])
        step = pl.program_id(0)

        @pl.when(step == 0)
        def _():
            gather.start()

        xv = x_ref[...]
        r = lax.rsqrt(jnp.mean(xv * xv, axis=-1, keepdims=True) + EPS)
        hf = xv * r * g_ref[...]
        h = hf.astype(BF16)
        ht_ref[...] = hf.T.astype(BF16)
        ab_ref[...] = jnp.dot(h, wab_ref[...], preferred_element_type=F32)
        for c0 in range(0, n, tn):
            proj_ref[:, c0:c0 + tn] = jnp.dot(h, w_ref[:, c0:c0 + tn], preferred_element_type=F32)

        @pl.when(step == ni // 2)
        def _():
            gather.relay()

        @pl.when(step == ni - 1)
        def _():
            gather.finish()

    any_spec = pl.BlockSpec(memory_space=pl.ANY)
    once = lambda a: pl.BlockSpec(a.shape, lambda i: (0, 0), pipeline_mode=pl.Buffered(1))
    res = pl.pallas_call(
        body, name="inproj", grid=(ni,),
        in_specs=[pl.BlockSpec((tm, d), lambda i: (i, 0)), pl.BlockSpec((1, d), lambda i: (0, 0)), once(w_main),
                  once(w_ab)] + [any_spec] * nl,
        out_specs=[pl.BlockSpec((tm, n), lambda i: (i, 0)), pl.BlockSpec((tm, LANE), lambda i: (i, 0)),
                   pl.BlockSpec((d, tm), lambda i: (0, i))] + [any_spec] * nl,
        out_shape=[_sds((s, n)), _sds((s, LANE)), _sds((d, s), BF16)]
        + [_sds((N_DEV,) + a.shape, a.dtype) for a in late],
        scratch_shapes=_Gather.sems(nl),
        compiler_params=_params(("arbitrary",)),
    )(x, ln_g, w_main, w_ab, *late)
    return res[0], res[1], res[2], res[3:]


def _matmul_acc(a, b, name):
    m, k = a.shape
    n = b.shape[1]
    tm = _tile(m, (1024, 512, 256, 128))
    tn = _tile(n, (1024, 512, 256, 128))
    tk = _tile(k, (2048, 1024, 512, 256, 128))
    nk = k // tk

    def body(a_ref, b_ref, o_ref, acc):
        @pl.when(pl.program_id(2) == 0)
        def _():
            acc[...] = jnp.zeros_like(acc)

        acc[...] += jnp.dot(a_ref[...], b_ref[...], preferred_element_type=F32)

        @pl.when(pl.program_id(2) == nk - 1)
        def _():
            o_ref[...] = acc[...].astype(BF16)

    return pl.pallas_call(
        body, name=name, grid=(m // tm, n // tn, nk),
        in_specs=[pl.BlockSpec((tm, tk), lambda i, j, l: (i, l)), pl.BlockSpec((tk, tn), lambda i, j, l: (l, j))],
        out_specs=pl.BlockSpec((tm, tn), lambda i, j, l: (i, j)),
        out_shape=_sds((m, n), BF16), scratch_shapes=[pltpu.VMEM((tm, tn), F32)],
        compiler_params=_params(("parallel", "parallel", "arbitrary")),
    )(a, b)


def _matmul_tn(a, b, name):
    k, m = a.shape
    n = b.shape[1]
    tm = _tile(m, (1024, 512, 256, 128))
    tn = _tile(n, (1024, 512, 256, 128))
    tk = _tile(k, (2048, 1024, 512, 256, 128))
    nk = k // tk

    def body(a_ref, b_ref, o_ref, acc):
        @pl.when(pl.program_id(2) == 0)
        def _():
            acc[...] = jnp.zeros_like(acc)

        acc[...] += _mm_tn(a_ref[...], b_ref[...])

        @pl.when(pl.program_id(2) == nk - 1)
        def _():
            o_ref[...] = acc[...].astype(BF16)

    return pl.pallas_call(
        body, name=name, grid=(m // tm, n // tn, nk),
        in_specs=[pl.BlockSpec((tk, tm), lambda i, j, l: (l, i)), pl.BlockSpec((tk, tn), lambda i, j, l: (l, j))],
        out_specs=pl.BlockSpec((tm, tn), lambda i, j, l: (i, j)),
        out_shape=_sds((m, n), BF16), scratch_shapes=[pltpu.VMEM((tm, tn), F32)],
        compiler_params=_params(("parallel", "parallel", "arbitrary")),
    )(a, b)


def _dh_rms(pieces, w_rows, wab_rows, x, dx2, ln_g, chip_sums):
    s, d = x.shape
    npc = len(pieces)
    nx = len(chip_sums)
    tm = _tile(s, (256, 128))
    ni = s // tm
    widths = [p.shape[1] for p in pieces[:-1]]
    offs = [sum(widths[:p]) for p in range(npc - 1)]
    nw = len(w_rows)
    nin = npc + 2 * nw + 3

    def body(*refs):
        p_refs = refs[:npc]
        w_refs = refs[npc:npc + nw]
        wab_refs = refs[npc + nw:npc + 2 * nw]
        x_ref, dx2_ref, g_ref = refs[npc + 2 * nw:nin]
        gx_ref, dg_ref = refs[nin + nx:nin + nx + 2]
        exch = _ChipExchange(refs[nin:nin + nx], refs[nin + nx + 2:nin + 2 * nx + 2], *refs[nin + 2 * nx + 2:])
        step = pl.program_id(0)

        @pl.when(step == 0)
        def _():
            dg_ref[...] = jnp.zeros_like(dg_ref)
            exch.start()

        cols = []
        for w_ref, wab_ref in zip(w_refs, wab_refs):
            part = _mm_nt(p_refs[npc - 1][...], wab_ref[...])
            for p in range(npc - 1):
                part += _mm_nt(p_refs[p][...], w_ref[:, offs[p]:offs[p] + widths[p]])
            cols.append(part)
        dhv = jnp.concatenate(cols, axis=1)
        xv = x_ref[...]
        r = lax.rsqrt(jnp.mean(xv * xv, axis=-1, keepdims=True) + EPS)
        xhat = xv * r
        dg_ref[...] += _colsum(dhv * xhat)
        dxh = dhv * g_ref[...]
        gx_ref[...] = dx2_ref[...] + r * (dxh - xhat * jnp.mean(dxh * xhat, axis=-1, keepdims=True))

        @pl.when(step == ni - 1)
        def _():
            exch.finish()

    any_spec = pl.BlockSpec(memory_space=pl.ANY)
    row = pl.BlockSpec((tm, d), lambda i: (i, 0))
    vec = pl.BlockSpec((1, d), lambda i: (0, 0))
    once = lambda a: pl.BlockSpec(a.shape, lambda i: (0, 0), pipeline_mode=pl.Buffered(1))
    in_specs = [pl.BlockSpec((tm, p.shape[1]), lambda i: (i, 0)) for p in pieces]
    in_specs += [once(w) for w in w_rows] + [once(w) for w in wab_rows] + [row, row, vec] + [any_spec] * nx
    res = pl.pallas_call(
        body, name="dh_rms", grid=(ni,), in_specs=in_specs,
        out_specs=[row, vec] + [any_spec] * nx,
        out_shape=[_sds((s, d)), _sds((1, d))] + [_sds(p.shape, p.dtype) for p in chip_sums],
        scratch_shapes=_ChipExchange.sems(nx),
        compiler_params=_params(("arbitrary",)),
    )(*pieces, *w_rows, *wab_rows, x, dx2, ln_g, *chip_sums)
    return res[0], res[1], res[2:]


def _final(x, tgt, out_b, out_a, out_c, w_out, final_g):
    s, d = x.shape
    tm = _tile(s, (256, 128))

    def body(x_ref, t_ref, b_ref, a_ref, c_ref, w_ref, g_ref, dx2_ref, dx2b_ref, dm_ref, loss_ref, dg_ref):
        @pl.when(pl.program_id(0) == 0)
        def _():
            loss_ref[...] = jnp.zeros_like(loss_ref)
            dg_ref[...] = jnp.zeros_like(dg_ref)

        x2 = x_ref[...]
        x2 += jnp.dot(b_ref[...], w_ref[0:DN_W, :], preferred_element_type=F32)
        x2 += jnp.dot(a_ref[...], w_ref[DN_W:DN_W + GMLP_W, :], preferred_element_type=F32)
        x2 += jnp.dot(c_ref[...], w_ref[DN_W + GMLP_W:MIX_W, :], preferred_element_type=F32)
        r = lax.rsqrt(jnp.mean(x2 * x2, axis=-1, keepdims=True) + EPS)
        xhat = x2 * r
        g = g_ref[...]
        err = xhat * g - t_ref[...]
        tok = 0.5 * jnp.mean(err * err, axis=-1, keepdims=True)
        loss_ref[...] += jnp.broadcast_to(_colsum(tok), loss_ref.shape)
        dy = err * (1.0 / d)
        dg_ref[...] += _colsum(dy * xhat)
        dxh = dy * g
        dx2 = r * (dxh - xhat * jnp.mean(dxh * xhat, axis=-1, keepdims=True))
        dx2_ref[...] = dx2
        dx2b = dx2.astype(BF16)
        dx2b_ref[...] = dx2b
        dm_ref[...] = _mm_nt(dx2b, w_ref[...])

    row = pl.BlockSpec((tm, d), lambda i: (i, 0))
    vec = pl.BlockSpec((1, d), lambda i: (0, 0))
    return pl.pallas_call(
        body, name="final", grid=(s // tm,),
        in_specs=[row, row, pl.BlockSpec((tm, DN_W), lambda i: (i, 0)), pl.BlockSpec((tm, GMLP_W), lambda i: (i, 0)),
                  pl.BlockSpec((tm, XA_W), lambda i: (i, 0)), pl.BlockSpec((MIX_W, d), lambda i: (0, 0)), vec],
        out_specs=[row, row, pl.BlockSpec((tm, MIX_W), lambda i: (i, 0)), pl.BlockSpec((1, LANE), lambda i: (0, 0)), vec],
        out_shape=[_sds((s, d)), _sds((s, d), BF16), _sds((s, MIX_W)), _sds((1, LANE)), _sds((1, d))],
        compiler_params=_params(("arbitrary",)),
    )(x, tgt, out_b, out_a, out_c, w_out, final_g)


GU_BLK = (4 * DN_W) // GMLP_W


def _gmlp_norm(gv, lng, lnb):
    va = _gelu(gv)
    mu = jnp.mean(va, axis=-1, keepdims=True)
    xc = va - mu
    rstd = lax.rsqrt(jnp.mean(xc * xc, axis=-1, keepdims=True) + EPS)
    vhat = xc * rstd
    return vhat, rstd, vhat * lng + lnb


def _gmlp_fwd(proj, lng, lnb, ws, bs_t):
    s = proj.shape[0]
    tm = _tile(s, (512, 256, 128))

    def body(u_ref, v_ref, z_ref, lng_ref, lnb_ref, ws_ref, bst_ref, o_ref):
        _, _, vn = _gmlp_norm(v_ref[...], lng_ref[...], lnb_ref[...])
        tri = _iota2((GMLP_T, GMLP_T), 0) >= _iota2((GMLP_T, GMLP_T), 1)
        for g in range(GMLP_G):
            cs = slice(g * HEAD, (g + 1) * HEAD)
            w = jnp.where(tri, ws_ref[g], 0.0).astype(BF16)
            b = bst_ref[:, g:g + 1]
            for c in range(tm // GMLP_T):
                rs = slice(c * GMLP_T, (c + 1) * GMLP_T)
                sg = _mm(w, vn[rs, cs]) + b
                o_ref[rs, cs] = (_gelu(u_ref[rs, cs]) * sg * _silu(z_ref[rs, cs])).astype(BF16)

    col = lambda k: pl.BlockSpec((tm, GMLP_W), lambda i: (i, GU_BLK + k))
    vec = pl.BlockSpec((1, GMLP_W), lambda i: (0, 0))
    return pl.pallas_call(
        body, name="gmlp_fwd", grid=(s // tm,),
        in_specs=[col(0), col(1), col(2), vec, vec, pl.BlockSpec((GMLP_G, GMLP_T, GMLP_T), lambda i: (0, 0, 0)),
                  pl.BlockSpec((GMLP_T, GMLP_G), lambda i: (0, 0))],
        out_specs=pl.BlockSpec((tm, GMLP_W), lambda i: (i, 0)), out_shape=_sds((s, GMLP_W), BF16),
        compiler_params=_params(("parallel",)),
    )(proj, proj, proj, lng, lnb, ws, bs_t)


def _gmlp_bwd(proj, dmixed, lng, lnb, ws, bs_t):
    s = proj.shape[0]
    tm = _tile(s, (512, 256, 128))

    def body(u_ref, v_ref, z_ref, d_ref, lng_ref, lnb_ref, ws_ref, bst_ref,
             dp_ref, dws_ref, dbst_ref, dlng_ref, dlnb_ref, dvn):
        @pl.when(pl.program_id(0) == 0)
        def _():
            dws_ref[...] = jnp.zeros_like(dws_ref)
            dbst_ref[...] = jnp.zeros_like(dbst_ref)
            dlng_ref[...] = jnp.zeros_like(dlng_ref)
            dlnb_ref[...] = jnp.zeros_like(dlnb_ref)

        gv = v_ref[...]
        lng_v = lng_ref[...]
        vhat, rstd, vn = _gmlp_norm(gv, lng_v, lnb_ref[...])
        tri = _iota2((GMLP_T, GMLP_T), 0) >= _iota2((GMLP_T, GMLP_T), 1)
        for g in range(GMLP_G):
            cs = slice(g * HEAD, (g + 1) * HEAD)
            w = jnp.where(tri, ws_ref[g], 0.0).astype(BF16)
            b = bst_ref[:, g:g + 1]
            dw_acc = jnp.zeros((GMLP_T, GMLP_T), F32)
            db_acc = jnp.zeros((GMLP_T, 1), F32)
            for c in range(tm // GMLP_T):
                rs = slice(c * GMLP_T, (c + 1) * GMLP_T)
                vn_b = vn[rs, cs]
                sg = _mm(w, vn_b) + b
                gu = u_ref[rs, cs]
                gz = z_ref[rs, cs]
                da = d_ref[rs, cs]
                uact = _gelu(gu)
                sz = _silu(gz)
                ds = da * uact * sz
                dp_ref[rs, cs] = (da * sg * sz * _gelu_grad(gu)).astype(BF16)
                dp_ref[rs, 2 * GMLP_W + g * HEAD:2 * GMLP_W + (g + 1) * HEAD] = (da * uact * sg * _silu_grad(gz)).astype(BF16)
                dw_acc += _mm_nt(ds, vn_b)
                db_acc += _rowsum(ds)
                dvn[rs, cs] = _mm_tn(w, ds)
            dws_ref[g] += jnp.where(tri, dw_acc, 0.0)
            dbst_ref[:, g:g + 1] += db_acc
        dvn_v = dvn[...]
        dlng_ref[...] += _colsum(dvn_v * vhat)
        dlnb_ref[...] += _colsum(dvn_v)
        dvh = dvn_v * lng_v
        dva = rstd * (dvh - jnp.mean(dvh, axis=-1, keepdims=True) - vhat * jnp.mean(dvh * vhat, axis=-1, keepdims=True))
        dp_ref[:, GMLP_W:2 * GMLP_W] = (dva * _gelu_grad(gv)).astype(BF16)

    col = lambda k: pl.BlockSpec((tm, GMLP_W), lambda i: (i, GU_BLK + k))
    vec = pl.BlockSpec((1, GMLP_W), lambda i: (0, 0))
    wsp = pl.BlockSpec((GMLP_G, GMLP_T, GMLP_T), lambda i: (0, 0, 0))
    bsp = pl.BlockSpec((GMLP_T, GMLP_G), lambda i: (0, 0))
    return pl.pallas_call(
        body, name="gmlp_bwd", grid=(s // tm,),
        in_specs=[col(0), col(1), col(2), pl.BlockSpec((tm, GMLP_W), lambda i: (i, DN_W // GMLP_W)), vec, vec, wsp, bsp],
        out_specs=[pl.BlockSpec((tm, 3 * GMLP_W), lambda i: (i, 0)), wsp, bsp, vec, vec],
        out_shape=[_sds((s, 3 * GMLP_W), BF16), _sds((GMLP_G, GMLP_T, GMLP_T)), _sds((GMLP_T, GMLP_G)),
                   _sds((1, GMLP_W)), _sds((1, GMLP_W))],
        scratch_shapes=[pltpu.VMEM((tm, GMLP_W), F32)],
        compiler_params=_params(("arbitrary",)),
    )(proj, proj, proj, dmixed, lng, lnb, ws, bs_t)


CQ_BLK = (4 * DN_W + 3 * GMLP_W) // XA_W


def _memkv_fwd(mem, g, w_kv):
    nm, d = mem.shape

    def body(m_ref, g_ref, w_ref, kv_ref):
        mv = m_ref[...]
        r = lax.rsqrt(jnp.mean(mv * mv, axis=-1, keepdims=True) + EPS)
        kv_ref[...] = _mm(mv * r * g_ref[...], w_ref[...])

    return pl.pallas_call(body, name="memkv_fwd", out_shape=_sds((nm, 2 * XA_W)), compiler_params=_params())(mem, g, w_kv)


def _memkv_bwd(mem, g, w_kv, dkv):
    nm, d = mem.shape

    def body(m_ref, g_ref, w_ref, dkv_ref, dw_ref, dg_ref):
        mv = m_ref[...]
        r = lax.rsqrt(jnp.mean(mv * mv, axis=-1, keepdims=True) + EPS)
        xhat = mv * r
        dkv_v = dkv_ref[...]
        dw_ref[...] = _mm_tn(xhat * g_ref[...], dkv_v)
        dg_ref[...] = _colsum(_mm_nt(dkv_v, w_ref[...]) * xhat)

    return pl.pallas_call(body, name="memkv_bwd", out_shape=[_sds((d, 2 * XA_W)), _sds((1, d))],
                          compiler_params=_params())(mem, g, w_kv, dkv)


def _xattn_probs(q, mk):
    sc = _mm_nt(q, mk) * (HEAD ** -0.5)
    e = jnp.exp(sc - jnp.max(sc, axis=-1, keepdims=True))
    return e / _rowsum(e)


def _xattn_fwd(proj, mkv):
    s = proj.shape[0]
    nm = mkv.shape[0]
    tm = _tile(s, (512, 256, 128))

    def body(q_ref, z_ref, kv_ref, o_ref):
        for h in range(XA_H):
            cs = slice(h * HEAD, (h + 1) * HEAD)
            p = _xattn_probs(q_ref[:, cs], kv_ref[:, cs])
            ctx = _mm(p, kv_ref[:, XA_W + h * HEAD:XA_W + (h + 1) * HEAD])
            o_ref[:, cs] = (ctx * _silu(z_ref[:, cs])).astype(BF16)

    col = lambda k: pl.BlockSpec((tm, XA_W), lambda i: (i, CQ_BLK + k))
    return pl.pallas_call(
        body, name="xattn_fwd", grid=(s // tm,),
        in_specs=[col(0), col(1), pl.BlockSpec((nm, 2 * XA_W), lambda i: (0, 0))],
        out_specs=pl.BlockSpec((tm, XA_W), lambda i: (i, 0)), out_shape=_sds((s, XA_W), BF16),
        compiler_params=_params(("parallel",)),
    )(proj, proj, mkv)


def _xattn_bwd(proj, dmixed, mkv):
    s = proj.shape[0]
    nm = mkv.shape[0]
    tm = _tile(s, (512, 256, 128))

    def body(q_ref, z_ref, d_ref, kv_ref, dp_ref, dkv_ref):
        @pl.when(pl.program_id(0) == 0)
        def _():
            dkv_ref[...] = jnp.zeros_like(dkv_ref)

        for h in range(XA_H):
            cs = slice(h * HEAD, (h + 1) * HEAD)
            vs = slice(XA_W + h * HEAD, XA_W + (h + 1) * HEAD)
            q = q_ref[:, cs]
            z = z_ref[:, cs]
            mk = kv_ref[:, cs]
            mv = kv_ref[:, vs]
            p = _xattn_probs(q, mk)
            ctx = _mm(p, mv)
            dc = d_ref[:, cs]
            dctx = dc * _silu(z)
            dp_ref[:, vs] = (dc * ctx * _silu_grad(z)).astype(BF16)
            dp = _mm_nt(dctx, mv)
            dkv_ref[:, vs] += _mm_tn(p, dctx)
            ds = p * (dp - _rowsum(dp * p)) * (HEAD ** -0.5)
            dp_ref[:, cs] = _mm(ds, mk).astype(BF16)
            dkv_ref[:, cs] += _mm_tn(ds, q)

    col = lambda k: pl.BlockSpec((tm, XA_W), lambda i: (i, CQ_BLK + k))
    kvs = pl.BlockSpec((nm, 2 * XA_W), lambda i: (0, 0))
    return pl.pallas_call(
        body, name="xattn_bwd", grid=(s // tm,),
        in_specs=[col(0), col(1), pl.BlockSpec((tm, XA_W), lambda i: (i, (DN_W + GMLP_W) // XA_W)), kvs],
        out_specs=[pl.BlockSpec((tm, 2 * XA_W), lambda i: (i, 0)), kvs],
        out_shape=[_sds((s, 2 * XA_W), BF16), _sds((nm, 2 * XA_W))],
        compiler_params=_params(("arbitrary",)),
    )(proj, proj, dmixed, mkv)


def _softplus(x):
    return jnp.maximum(x, 0.0) + jnp.log1p(jnp.exp(-jnp.abs(x)))


def _dn_pre(proj, ab, conv_w, alog_row, dt_row):
    s = proj.shape[0]
    tm = _tile(s, (256, 128))
    w3 = 3 * DN_W

    def body(x_ref, halo_ref, ab_ref, cw_ref, al_ref, dt_ref, q_ref, k_ref, v_ref, gb_ref, gbt_ref, yc_ref):
        i = pl.program_id(0)
        xv = x_ref[...]
        cat = jnp.concatenate([jnp.where(i > 0, halo_ref[...], 0.0), xv[0:HALO]], axis=0)
        yc = cw_ref[DN_K - 1:DN_K, :] * xv
        top = cw_ref[DN_K - 1:DN_K, :] * xv[0:HALO]
        for t in range(DN_K - 1):
            back = DN_K - 1 - t
            yc += cw_ref[t:t + 1, :] * pltpu.roll(xv, back, 0)
            top += cw_ref[t:t + 1, :] * pltpu.roll(cat, back, 0)[HALO:2 * HALO]
        yc = jnp.concatenate([top, yc[HALO:tm]], axis=0)
        yc_ref[...] = yc
        act = _silu(yc)
        for h in range(DN_H):
            cs = slice(h * HEAD, (h + 1) * HEAD)
            qa = act[:, cs]
            q_ref[:, cs] = qa * (lax.rsqrt(_rowsum(qa * qa) + EPS) * (HEAD ** -0.5))
            ka = act[:, DN_W + h * HEAD:DN_W + (h + 1) * HEAD]
            k_ref[:, cs] = ka * lax.rsqrt(_rowsum(ka * ka) + EPS)
        v_ref[...] = act[:, 2 * DN_W:w3]
        abv = ab_ref[...]
        lane = _iota2((tm, LANE), 1)
        g = jnp.where(lane < DN_H, -jnp.exp(al_ref[...]) * _softplus(abv + dt_ref[...]), 0.0)
        gc = _mm_hi(_chunk_tri(tm, False), g)
        gbv = jnp.where(lane < DN_H, gc, jnp.where(lane < 2 * DN_H, jax.nn.sigmoid(abv), 0.0))
        gb_ref[...] = gbv
        for c in range(tm // CH):
            gbt_ref[c] = gbv[c * CH:(c + 1) * CH, :].T[0:2 * DN_H, :]

    hb = tm // HALO
    row = lambda w: pl.BlockSpec((tm, w), lambda i: (i, 0))
    vec = pl.BlockSpec((1, LANE), lambda i: (0, 0))
    return pl.pallas_call(
        body, name="dn_pre", grid=(s // tm,),
        in_specs=[row(w3), pl.BlockSpec((HALO, w3), lambda i: (jnp.maximum(i * hb - 1, 0), 0)), row(LANE),
                  pl.BlockSpec((DN_K, w3), lambda i: (0, 0)), vec, vec],
        out_specs=[row(DN_W), row(DN_W), row(DN_W), row(LANE), pl.BlockSpec((tm // CH, 2 * DN_H, CH), lambda i: (i, 0, 0)),
                   row(w3)],
        out_shape=[_sds((s, DN_W)), _sds((s, DN_W)), _sds((s, DN_W)), _sds((s, LANE)),
                   _sds((s // CH, 2 * DN_H, CH)), _sds((s, w3))],
        compiler_params=_params(("parallel",)),
    )(proj, proj, ab, conv_w, alog_row, dt_row)


HEADS = tuple(range(DN_H))


def _hcols(h):
    return slice(h * HEAD, (h + 1) * HEAD)


def _chunk_scalings(k, v, gbv, gbt, h):
    gc = jnp.broadcast_to(gbv[:, h:h + 1], (CH, HEAD))
    beta = jnp.broadcast_to(gbv[:, DN_H + h:DN_H + h + 1], (CH, HEAD))
    gr = gbt[h:h + 1, :]
    ii = _iota2((CH, CH), 0)
    jj = _iota2((CH, CH), 1)
    dec = jnp.exp(jnp.where(ii >= jj, gc[:, 0:CH] - gr, -1e30))
    eg = jnp.exp(gc)
    gl = gr[:, CH - 1:CH]
    kb = k * beta
    return dict(beta=beta, dec=dec, eg=eg, gl=gl, ekd=jnp.exp(gl - gc), kb=kb, vb=v * beta, kbe=kb * eg)


def _chunk_scores(m, q, k):
    kq = _mm_nt(jnp.concatenate([m["kb"], q], axis=0), k)
    strict = _iota2((CH, CH), 0) > _iota2((CH, CH), 1)
    return jnp.where(strict, kq[0:CH] * m["dec"], 0.0), kq[CH:2 * CH] * m["dec"]


def _scan_cpb(s):
    return 8 if (s // CH) % 8 == 0 else 1


def _dn_fwd(q, k, v, gb, gbt, proj, norm_g):
    s = q.shape[0]
    cpb = _scan_cpb(s)
    tb = cpb * CH
    nblk = s // tb

    def body(q_ref, k_ref, v_ref, gb_ref, gbt_ref, z_ref, ng_ref,
             w_ref, qg_ref, kd_ref, t_ref, ai_ref, egl_ref, o_ref, vn_ref, st_ref, ob_ref, state):
        @pl.when(pl.program_id(0) == 0)
        def _():
            state[...] = jnp.zeros_like(state)

        ng = ng_ref[...]
        eye = jnp.where(_iota2((CH, CH), 0) == _iota2((CH, CH), 1), 1.0, 0.0).astype(F32)

        def chunk(c, carry):
            r0 = pl.multiple_of(c * CH, CH)
            rows = pl.ds(r0, CH)
            gbv = gb_ref[rows, :]
            gbt_v = gbt_ref[c]
            qs = [q_ref[rows, _hcols(h)] for h in HEADS]
            ks = [k_ref[rows, _hcols(h)] for h in HEADS]
            ms = [_chunk_scalings(ks[h], v_ref[rows, _hcols(h)], gbv, gbt_v, h) for h in HEADS]
            qgb = [(qs[h] * ms[h]["eg"]).astype(BF16) for h in HEADS]
            kdb = [(ks[h] * ms[h]["ekd"]).astype(BF16) for h in HEADS]
            egl = [jnp.broadcast_to(jnp.exp(ms[h]["gl"]), (1, LANE)) for h in HEADS]
            for h in HEADS:
                qg_ref[rows, _hcols(h)] = qgb[h]
                kd_ref[rows, _hcols(h)] = kdb[h]
                egl_ref[c, h:h + 1, :] = egl[h]
            sc = [_chunk_scores(ms[h], qs[h], ks[h]) for h in HEADS]
            for h in HEADS:
                ai_ref[h, rows, :] = sc[h][1]
            ts = [eye - sc[h][0] for h in HEADS]
            ps = [_mm_3x(sc[h][0], sc[h][0]) for h in HEADS]
            ts = [ts[h] + _mm_3x(ts[h], ps[h]) for h in HEADS]
            for _ in range(4):
                ps = [_mm(ps[h], ps[h]) for h in HEADS]
                ts = [ts[h] + _mm(ts[h], ps[h]) for h in HEADS]
            uw = [_mm(ts[h], jnp.concatenate([ms[h]["vb"], ms[h]["kbe"]], axis=1)) for h in HEADS]
            wb = [uw[h][:, HEAD:2 * HEAD].astype(BF16) for h in HEADS]
            for h in HEADS:
                t_ref[h, rows, :] = ts[h]
                w_ref[rows, _hcols(h)] = wb[h]
            sts = [state[h] for h in HEADS]
            stb = [sts[h].astype(BF16) for h in HEADS]
            for h in HEADS:
                st_ref[c, h] = stb[h]
            vnb = [(uw[h][:, 0:HEAD] - jnp.dot(wb[h], stb[h], preferred_element_type=F32)).astype(BF16) for h in HEADS]
            for h in HEADS:
                state[h] = sts[h] * egl[h] + _mm_tn(kdb[h], vnb[h])
            os_ = [jnp.dot(qgb[h], stb[h], preferred_element_type=F32) + _mm(sc[h][1], vnb[h]) for h in HEADS]
            for h in HEADS:
                o = os_[h]
                vn_ref[rows, _hcols(h)] = vnb[h]
                o_ref[rows, _hcols(h)] = o
                r = lax.rsqrt(jnp.mean(o * o, axis=-1, keepdims=True) + EPS)
                ob_ref[rows, _hcols(h)] = (o * r * ng * _silu(z_ref[rows, _hcols(h)])).astype(BF16)
            return carry

        lax.fori_loop(0, cpb, chunk, 0, unroll=4)

    row = pl.BlockSpec((tb, DN_W), lambda i: (i, 0))
    sq = pl.BlockSpec((DN_H, tb, CH), lambda i: (0, i, 0))
    return pl.pallas_call(
        body, name="dn_fwd", grid=(nblk,),
        in_specs=[row, row, row, pl.BlockSpec((tb, LANE), lambda i: (i, 0)),
                  pl.BlockSpec((cpb, 2 * DN_H, CH), lambda i: (i, 0, 0)), pl.BlockSpec((tb, DN_W), lambda i: (i, 3)),
                  pl.BlockSpec((1, HEAD), lambda i: (0, 0))],
        out_specs=[row, row, row, sq, sq, pl.BlockSpec((cpb, DN_H, LANE), lambda i: (i, 0, 0)), row, row,
                   pl.BlockSpec((cpb, DN_H, HEAD, HEAD), lambda i: (i, 0, 0, 0)), row],
        out_shape=[_sds((s, DN_W), BF16), _sds((s, DN_W), BF16), _sds((s, DN_W), BF16), _sds((DN_H, s, CH)),
                   _sds((DN_H, s, CH)), _sds((s // CH, DN_H, LANE)), _sds((s, DN_W)), _sds((s, DN_W), BF16),
                   _sds((s // CH, DN_H, HEAD, HEAD), BF16), _sds((s, DN_W), BF16)],
        scratch_shapes=[pltpu.VMEM((DN_H, HEAD, HEAD), F32)],
        compiler_params=_params(("arbitrary",)),
    )(q, k, v, gb, gbt, proj, norm_g)


def _dn_bwd(dmixed, o, proj, norm_g, w, qg, kd, ai, egl, q, k, v, gb, gbt, t, vn, st):
    s = o.shape[0]
    cpb = 4 if (s // CH) % 4 == 0 else 1
    tb = cpb * CH
    nblk = s // tb

    def body(dm_ref, o_ref, z_ref, ng_ref, w_ref, qg_ref, kd_ref, ai_ref, egl_ref,
             q_ref, k_ref, v_ref, gb_ref, gbt_ref, t_ref, vn_ref, st_ref,
             dq_ref, dk_ref, dv_ref, dgb_ref, dz_ref, dng_ref, dstate):
        @pl.when(pl.program_id(0) == 0)
        def _():
            dstate[...] = jnp.zeros_like(dstate)
            dng_ref[...] = jnp.zeros_like(dng_ref)

        ng = ng_ref[...]
        lane = _iota2((CH, LANE), 1)
        last = _iota2((CH, 1), 0) == CH - 1
        strict = _iota2((CH, CH), 0) > _iota2((CH, CH), 1)

        def chunk(cc, carry):
            c = cpb - 1 - cc
            r0 = pl.multiple_of(c * CH, CH)
            rows = pl.ds(r0, CH)
            dng = jnp.zeros((1, HEAD), F32)
            dob = []
            for h in HEADS:
                cs = _hcols(h)
                ov = o_ref[rows, cs]
                z = z_ref[rows, cs]
                db = dm_ref[rows, cs]
                r = lax.rsqrt(jnp.mean(ov * ov, axis=-1, keepdims=True) + EPS)
                ohat = ov * r
                dz_ref[rows, cs] = (db * ohat * ng * _silu_grad(z)).astype(BF16)
                dyn = db * _silu(z)
                dng += _colsum(dyn * ohat)
                doh = dyn * ng
                dob.append((r * (doh - ohat * jnp.mean(doh * ohat, axis=-1, keepdims=True))).astype(BF16))
            dng_ref[...] += dng
            dsn = [dstate[h] for h in HEADS]
            dsb = [dsn[h].astype(BF16) for h in HEADS]
            dvnb = [(_mm_tn(ai_ref[h, rows, :], dob[h])
                     + jnp.dot(kd_ref[rows, _hcols(h)], dsb[h], preferred_element_type=F32)).astype(BF16) for h in HEADS]
            part = [_mm_tn(qg_ref[rows, _hcols(h)], dob[h]) + egl_ref[c, h:h + 1, :] * dsn[h] for h in HEADS]
            for h in HEADS:
                dstate[h] = part[h] - _mm_tn(w_ref[rows, _hcols(h)], dvnb[h])
            gbv = gb_ref[rows, :]
            gbt_v = gbt_ref[c]
            qs = [q_ref[rows, _hcols(h)] for h in HEADS]
            ks = [k_ref[rows, _hcols(h)] for h in HEADS]
            vs = [v_ref[rows, _hcols(h)] for h in HEADS]
            ms = [_chunk_scalings(ks[h], vs[h], gbv, gbt_v, h) for h in HEADS]
            sts = [st_ref[c, h] for h in HEADS]
            vnb = [vn_ref[rows, _hcols(h)] for h in HEADS]
            tbf = [t_ref[h, rows, :].astype(BF16) for h in HEADS]
            sc = [_chunk_scores(ms[h], qs[h], ks[h]) for h in HEADS]
            xs_ = [_mm_nt(jnp.concatenate([dob[h], dvnb[h]], axis=0), sts[h]) for h in HEADS]
            dai = [_mm_nt(dob[h], vnb[h]) for h in HEADS]
            dkd = [_mm_nt(vnb[h], dsb[h]) for h in HEADS]
            dqg = [xs_[h][0:CH] for h in HEADS]
            duw = [jnp.concatenate([dvnb[h], (-xs_[h][CH:2 * CH]).astype(BF16)], axis=1) for h in HEADS]
            dt = [_mm_nt(duw[h], jnp.concatenate([ms[h]["vb"], ms[h]["kbe"]], axis=1)) for h in HEADS]
            dvk = [_mm_tn(tbf[h], duw[h]) for h in HEADS]
            tdt = [_mm_tn(tbf[h], dt[h]) for h in HEADS]
            da = [jnp.where(strict, -_mm_nt(tdt[h], tbf[h]), 0.0) for h in HEADS]
            dsc = [jnp.concatenate([da[h] * ms[h]["dec"], dai[h] * ms[h]["dec"]], axis=0) for h in HEADS]
            dkq = [_mm(dsc[h], ks[h]) for h in HEADS]
            dk1 = [_mm_tn(dsc[h], jnp.concatenate([ms[h]["kb"], qs[h]], axis=0)) for h in HEADS]
            dgb = jnp.zeros((CH, LANE), F32)
            for h in HEADS:
                m = ms[h]
                eg, ekd, beta = m["eg"], m["ekd"], m["beta"]
                dvb = dvk[h][:, 0:HEAD]
                dkbe = dvk[h][:, HEAD:2 * HEAD]
                kdv = ks[h] * ekd
                dkb = dkq[h][0:CH] + dkbe * eg
                dq_ref[rows, _hcols(h)] = dkq[h][CH:2 * CH] + dqg[h] * eg
                dk_ref[rows, _hcols(h)] = dk1[h] + dkd[h] * ekd + dkb * beta
                dv_ref[rows, _hcols(h)] = dvb * beta
                dkd_kd = dkd[h] * kdv
                dgl = (jnp.exp(m["gl"]) * _rowsum(_colsum(sts[h].astype(F32) * dsb[h].astype(F32)))
                       + _rowsum(_colsum(dkd_kd)))
                mm_ = da[h] * sc[h][0] + dai[h] * sc[h][1]
                dgc = (_rowsum(mm_ - mm_.T) + _rowsum(dqg[h] * qs[h] * eg - dkd_kd + dkbe * m["kbe"])
                       + jnp.where(last, dgl, 0.0))
                dbeta = _rowsum(dkb * ks[h] + dvb * vs[h])
                dgb = jnp.where(lane == h, dgc, jnp.where(lane == DN_H + h, dbeta, dgb))
            dgb_ref[rows, :] = dgb
            return carry

        lax.fori_loop(0, cpb, chunk, 0, unroll=2)

    rev = lambda i: (nblk - 1 - i, 0)
    row = pl.BlockSpec((tb, DN_W), rev)
    vec = pl.BlockSpec((1, HEAD), lambda i: (0, 0))
    sq = pl.BlockSpec((DN_H, tb, CH), lambda i: (0, nblk - 1 - i, 0))
    gbs = pl.BlockSpec((tb, LANE), rev)
    return pl.pallas_call(
        body, name="dn_bwd", grid=(nblk,),
        in_specs=[row, row, pl.BlockSpec((tb, DN_W), lambda i: (nblk - 1 - i, 3)), vec, row, row, row, sq,
                  pl.BlockSpec((cpb, DN_H, LANE), lambda i: (nblk - 1 - i, 0, 0)),
                  row, row, row, gbs, pl.BlockSpec((cpb, 2 * DN_H, CH), lambda i: (nblk - 1 - i, 0, 0)), sq, row,
                  pl.BlockSpec((cpb, DN_H, HEAD, HEAD), lambda i: (nblk - 1 - i, 0, 0, 0))],
        out_specs=[row, row, row, gbs, row, vec],
        out_shape=[_sds((s, DN_W)), _sds((s, DN_W)), _sds((s, DN_W)), _sds((s, LANE)), _sds((s, DN_W), BF16),
                   _sds((1, HEAD))],
        scratch_shapes=[pltpu.VMEM((DN_H, HEAD, HEAD), F32)],
        compiler_params=_params(("arbitrary",)),
    )(dmixed, o, proj, norm_g, w, qg, kd, ai, egl, q, k, v, gb, gbt, t, vn, st)


def _dn_pre_bwd(proj, yc_all, ab, conv_w, alog_row, dt_row, dq, dk, dv, dgb):
    s = proj.shape[0]
    tm = _tile(s, (256, 128))
    w3 = 3 * DN_W
    nblk = s // tm

    def body(x_ref, yc_ref, ab_ref, cw_ref, al_ref, dt_ref, dq_ref, dk_ref, dv_ref, dgb_ref,
             dx_ref, dab_ref, dcw_ref, dal_ref, ddt_ref, exd, carry):
        i = pl.program_id(0)

        @pl.when(i == 0)
        def _():
            carry[...] = jnp.zeros_like(carry)
            dcw_ref[...] = jnp.zeros_like(dcw_ref)
            dal_ref[...] = jnp.zeros_like(dal_ref)
            ddt_ref[...] = jnp.zeros_like(ddt_ref)

        yc = yc_ref[...]
        sg = jax.nn.sigmoid(yc)
        act = yc * sg
        dact = sg * (1.0 + yc * (1.0 - sg))
        for h in range(DN_H):
            cs = slice(h * HEAD, (h + 1) * HEAD)
            ks = slice(DN_W + h * HEAD, DN_W + (h + 1) * HEAD)
            qa = act[:, cs]
            rq = lax.rsqrt(_rowsum(qa * qa) + EPS)
            qh = qa * rq
            dqv = dq_ref[:, cs]
            exd[0:tm, cs] = (HEAD ** -0.5) * rq * (dqv - qh * _rowsum(dqv * qh)) * dact[:, cs]
            ka = act[:, ks]
            rk = lax.rsqrt(_rowsum(ka * ka) + EPS)
            kh = ka * rk
            dkv = dk_ref[:, cs]
            exd[0:tm, ks] = rk * (dkv - kh * _rowsum(dkv * kh)) * dact[:, ks]
        exd[0:tm, 2 * DN_W:w3] = dv_ref[...] * dact[:, 2 * DN_W:w3]
        xv = x_ref[...]
        dyc = exd[...]
        cat = jnp.concatenate([dyc[tm - HALO:tm], carry[...]], axis=0)
        dcw_ref[DN_K - 1:DN_K, :] += _colsum(dyc * xv)
        dx = cw_ref[DN_K - 1:DN_K, :] * dyc
        for t in range(DN_K - 1):
            ahead = DN_K - 1 - t
            view = jnp.concatenate([pltpu.roll(dyc, tm - ahead, 0)[0:tm - HALO],
                                    pltpu.roll(cat, 2 * HALO - ahead, 0)[0:HALO]], axis=0)
            dcw_ref[t:t + 1, :] += _colsum(view * xv)
            dx += cw_ref[t:t + 1, :] * view
        dx_ref[...] = dx.astype(BF16)
        carry[...] = dyc[0:HALO]

        lane = _iota2((tm, LANE), 1)
        dgbv = dgb_ref[...]
        dg = _mm_hi(_chunk_tri(tm, True), jnp.where(lane < DN_H, dgbv, 0.0))
        abv = ab_ref[...]
        xa = abv + dt_ref[...]
        nea = -jnp.exp(al_ref[...])
        d_da = jnp.where(lane < DN_H, dg * nea * jax.nn.sigmoid(xa), 0.0)
        dal_ref[...] += _colsum(jnp.where(lane < DN_H, dg * nea * _softplus(xa), 0.0))
        ddt_ref[...] += _colsum(d_da)
        beta = jax.nn.sigmoid(abv)
        d_db = jnp.where((lane >= DN_H) & (lane < 2 * DN_H), dgbv * beta * (1.0 - beta), 0.0)
        dab_ref[...] = (d_da + d_db).astype(BF16)

    rev = lambda i: (nblk - 1 - i, 0)
    row = lambda w: pl.BlockSpec((tm, w), rev)
    vec = pl.BlockSpec((1, LANE), lambda i: (0, 0))
    cws = pl.BlockSpec((DN_K, w3), lambda i: (0, 0))
    return pl.pallas_call(
        body, name="dn_pre_bwd", grid=(nblk,),
        in_specs=[row(w3), row(w3), row(LANE), cws, vec, vec, row(DN_W), row(DN_W), row(DN_W), row(LANE)],
        out_specs=[row(w3), row(LANE), cws, vec, vec],
        out_shape=[_sds((s, w3), BF16), _sds((s, LANE), BF16), _sds((DN_K, w3)), _sds((1, LANE)), _sds((1, LANE))],
        scratch_shapes=[pltpu.VMEM((tm, w3), F32), pltpu.VMEM((HALO, w3), F32)],
        compiler_params=_params(("arbitrary",)),
    )(proj, yc_all, ab, conv_w, alog_row, dt_row, dq, dk, dv, dgb)


def _adam(parts, w, m, v, name):
    r, c = w.shape
    n_parts = parts.shape[0]
    small = n_parts * r * c * 4 <= 4 * 1024 * 1024
    tr = r if small else _tile(r, (128, 64, 32, 16, 8))

    def body(p_ref, w_ref, m_ref, v_ref, g_ref, d_ref, nm_ref, nv_ref):
        g = p_ref[0].astype(F32)
        for k in range(1, n_parts):
            g = g + p_ref[k].astype(F32)
        g_ref[...] = g
        mn = ADAM_B1 * m_ref[...] + (1.0 - ADAM_B1) * g
        vn = ADAM_B2 * v_ref[...] + (1.0 - ADAM_B2) * (g * g)
        m_hat = mn / (1.0 - ADAM_B1 ** ADAM_STEP)
        v_hat = vn / (1.0 - ADAM_B2 ** ADAM_STEP)
        d_ref[...] = -ADAM_LR * (m_hat / (jnp.sqrt(v_hat) + ADAM_EPS) + ADAM_WD * w_ref[...])
        nm_ref[...] = mn
        nv_ref[...] = vn

    blk = pl.BlockSpec((tr, c), lambda i: (i, 0))
    return pl.pallas_call(
        body, name=name, grid=(r // tr,),
        in_specs=[pl.BlockSpec((n_parts, tr, c), lambda i: (0, i, 0)), blk, blk, blk],
        out_specs=[blk, blk, blk, blk], out_shape=[_sds((r, c))] * 4,
        compiler_params=_params(("parallel",)),
    )(parts, w, m, v)


_PACK_ROWS = 8


def _pack(vals):
    tiles = []
    for a in vals:
        flat = a.reshape(-1).astype(F32)
        unit = _PACK_ROWS * LANE
        n = -(-flat.shape[0] // unit) * unit
        tiles.append(jnp.pad(flat, (0, n - flat.shape[0])).reshape(n // LANE, LANE))
    return jnp.concatenate(tiles, axis=0)


def _unpack(packed, shapes):
    out = []
    r0 = 0
    for shp in shapes:
        size = 1
        for dim in shp:
            size *= dim
        unit = _PACK_ROWS * LANE
        rows = -(-size // unit) * _PACK_ROWS
        out.append(packed[r0:r0 + rows].reshape(-1)[:size].reshape(shp))
        r0 += rows
    return out


def _lane_row(vec8):
    return jnp.pad(vec8.reshape(1, -1).astype(F32), ((0, 0), (0, LANE - vec8.size)))


def kernel(x, mem, ln_g, w_in, gmlp_ln_g, gmlp_ln_b, gmlp_ws, gmlp_bs, conv_w, dn_a_log, dn_dt_bias, dn_norm_g, mem_norm_g, w_mem_kv, w_out, final_g, loss_target, m_ln_g, m_w_in, m_gmlp_ln_g, m_gmlp_ln_b, m_gmlp_ws, m_gmlp_bs, m_conv_w, m_dn_a_log, m_dn_dt_bias, m_dn_norm_g, m_mem_norm_g, m_w_mem_kv, m_w_out, m_final_g, v_ln_g, v_w_in, v_gmlp_ln_g, v_gmlp_ln_b, v_gmlp_ws, v_gmlp_bs, v_conv_w, v_dn_a_log, v_dn_dt_bias, v_dn_norm_g, v_mem_norm_g, v_w_mem_kv, v_w_out, v_final_g):
    xs = x[0]
    mems = mem[0]
    tgt = loss_target[0]
    s, d = xs.shape
    shard_w = w_in.shape[2]
    in_w = N_DEV * shard_w
    me = 4 * lax.axis_index("x") + 2 * lax.axis_index("y") + lax.axis_index("c")

    (g_in,) = _gather_two_level([w_in[0].astype(BF16)], "gather_w_in")
    o_g, o_dn, o_ab = 0, 3 * GMLP_W, 3 * GMLP_W + 4 * DN_W
    o_xa = o_ab + 2 * DN_H

    def shard_cols(g, lo, hi):
        out = []
        while lo < hi:
            sh = lo // shard_w
            end = min(hi, (sh + 1) * shard_w)
            out.append(g[sh][:, lo - sh * shard_w:end - sh * shard_w])
            lo = end
        return out

    def own_layout(g):
        main = jnp.concatenate(shard_cols(g, o_dn, o_ab) + shard_cols(g, o_g, o_dn) + shard_cols(g, o_xa, in_w), axis=1)
        return main, jnp.pad(jnp.concatenate(shard_cols(g, o_ab, o_xa), axis=1), ((0, 0), (0, LANE - 2 * DN_H)))

    w_main, w_ab = own_layout(g_in)

    ln_g2 = ln_g.reshape(1, d)
    lng2 = gmlp_ln_g.reshape(1, GMLP_W)
    lnb2 = gmlp_ln_b.reshape(1, GMLP_W)
    ws3 = gmlp_ws[0]
    bs_t = gmlp_bs[0].T
    alog_row = _lane_row(dn_a_log)
    dt_row = _lane_row(dn_dt_bias)
    dn_g2 = dn_norm_g.reshape(1, HEAD)
    mem_g2 = mem_norm_g.reshape(1, d)
    fin_g2 = final_g.reshape(1, d)

    proj, ab, h_t, (g_out, g_kv, g_conv) = _inproj(
        xs, ln_g2, w_main, w_ab, [w_out[0].astype(BF16), w_mem_kv[0].astype(BF16), conv_w[0]])
    wo = g_out.reshape(MIX_W, d)
    wo_perm = jnp.concatenate([wo[GMLP_W:GMLP_W + DN_W], wo[0:GMLP_W], wo[GMLP_W + DN_W:MIX_W]], axis=0)
    w_kv = g_kv.reshape(d, 2 * XA_W)
    conv_full = g_conv.transpose(1, 0, 2).reshape(DN_K, 3 * DN_W)
    out_a = _gmlp_fwd(proj, lng2, lnb2, ws3, bs_t)
    mkv = _memkv_fwd(mems, mem_g2, w_kv)
    out_c = _xattn_fwd(proj, mkv)
    q, k, v, gb, gbt, yc = _dn_pre(proj, ab, conv_full, alog_row, dt_row)
    wk, qg, kd, tmat, ai, egl, o, vn, st, out_b = _dn_fwd(q, k, v, gb, gbt, proj, dn_g2)

    dx2, dx2b, dmixed, loss_acc, d_fin_g = _final(xs, tgt, out_b, out_a, out_c, wo_perm, fin_g2)

    dwo_b = _matmul_tn(out_b, dx2b, "dw_out_b")
    dwo_a = _matmul_tn(out_a, dx2b, "dw_out_a")
    dwo_c = _matmul_tn(out_c, dx2b, "dw_out_c")
    d_w_out = jnp.concatenate([dwo_a, dwo_b, dwo_c], axis=0)

    dp_g, d_ws, d_bst, d_lng, d_lnb = _gmlp_bwd(proj, dmixed, lng2, lnb2, ws3, bs_t)
    dp_x, dmkv = _xattn_bwd(proj, dmixed, mkv)
    d_w_kv, d_mem_g = _memkv_bwd(mems, mem_g2, w_kv, dmkv)
    dq, dk, dv, dgb, dp_dz, d_dn_g = _dn_bwd(dmixed, o, proj, dn_g2, wk, qg, kd, ai, egl, q, k, v, gb, gbt, tmat, vn, st)
    dp_qkv, dp_ab, d_conv, d_alog, d_dt = _dn_pre_bwd(proj, yc, ab, conv_full, alog_row, dt_row, dq, dk, dv, dgb)

    dw_qkv = _matmul_acc(h_t, dp_qkv, "dw_in_qkv")
    dw_dz = _matmul_acc(h_t, dp_dz, "dw_in_dz")
    dw_gm = _matmul_acc(h_t, dp_g, "dw_in_gmlp")
    dw_xa = _matmul_acc(h_t, dp_x, "dw_in_xa")
    dw_ab = _matmul_acc(h_t, dp_ab, "dw_in_ab")
    segs = [(o_g, dw_gm), (o_dn, dw_qkv), (o_dn + 3 * DN_W, dw_dz), (o_ab, dw_ab[:, :2 * DN_H]), (o_xa, dw_xa)]
    shards = []
    for sh in range(N_DEV):
        lo, hi = sh * shard_w, (sh + 1) * shard_w
        parts = [arr[:, max(lo, off) - off:min(hi, off + arr.shape[1]) - off] for off, arr in segs
                 if off < hi and off + arr.shape[1] > lo]
        shards.append(jnp.concatenate(parts, axis=1).astype(BF16))
    send_in = jnp.stack(shards)

    small_shapes = [(1, 1), gmlp_ln_g.shape, gmlp_ln_b.shape, gmlp_ws.shape, gmlp_bs.shape, dn_a_log.shape,
                    dn_dt_bias.shape, dn_norm_g.shape, mem_norm_g.shape, final_g.shape, (DN_K, 3 * DN_W)]
    small_g = _pack([loss_acc[0:1, 0:1], d_lng, d_lnb, d_ws, d_bst.T, d_alog[:, :DN_H], d_dt[:, :DN_H], d_dn_g, d_mem_g,
                     d_fin_g, d_conv])
    zc = jnp.zeros((DN_K, 3 * DN_W), F32)
    z1 = jnp.zeros((1, 1), F32)
    small_w = _pack([z1, gmlp_ln_g, gmlp_ln_b, gmlp_ws, gmlp_bs, dn_a_log, dn_dt_bias, dn_norm_g, mem_norm_g, final_g, zc])
    small_m = _pack([z1, m_gmlp_ln_g, m_gmlp_ln_b, m_gmlp_ws, m_gmlp_bs, m_dn_a_log, m_dn_dt_bias, m_dn_norm_g,
                     m_mem_norm_g, m_final_g, zc])
    small_v = _pack([z1 + 1.0, v_gmlp_ln_g, v_gmlp_ln_b, v_gmlp_ws, v_gmlp_bs, v_dn_a_log, v_dn_dt_bias, v_dn_norm_g,
                     v_mem_norm_g, v_final_g, zc + 1.0])

    send_out = d_w_out.reshape(N_DEV, MIX_W // N_DEV, d).astype(BF16)
    send_kv = d_w_kv.reshape(N_DEV, d // N_DEV, 2 * XA_W).astype(BF16)
    sends = [send_in, send_out, send_kv]
    all_small, got = _swap_halves(small_g, sends, "swap_halves")
    core = lax.axis_index("c").astype(jnp.int32).reshape(1)
    chip_sums = _pair_sums(core, sends, got)
    grad_x, d_ln_g, (r_in, r_out, r_kv) = _dh_rms(
        [dp_qkv, dp_dz, dp_g, dp_x, dp_ab], [w_main], [w_ab], xs, dx2, ln_g2, chip_sums)
    (all_ln_g,) = _gather_two_level([_pack([d_ln_g])], "gather_ln_g")

    g_w_in, dl_w_in, nm_w_in, nv_w_in = _adam(r_in, w_in[0], m_w_in[0], v_w_in[0], "adam_w_in")
    g_w_out, dl_w_out, nm_w_out, nv_w_out = _adam(r_out, w_out[0], m_w_out[0], v_w_out[0], "adam_w_out")
    g_w_kv, dl_w_kv, nm_w_kv, nv_w_kv = _adam(r_kv, w_mem_kv[0], m_w_mem_kv[0], v_w_mem_kv[0], "adam_w_kv")
    sm = [_unpack(t, small_shapes) for t in _adam(all_small, small_w, small_m, small_v, "adam_small")]
    ln_res = [_unpack(t, [ln_g.shape])[0]
              for t in _adam(all_ln_g, _pack([ln_g]), _pack([m_ln_g]), _pack([v_ln_g]), "adam_ln_g")]

    conv_parts = lax.dynamic_slice(all_small, (0, all_small.shape[1] - (DN_K * 3 * DN_W) // LANE, 0),
                                   (N_DEV, (DN_K * 3 * DN_W) // LANE, LANE)).reshape(N_DEV, DN_K, 3 * DN_W)
    cshard = conv_w.shape[2]
    conv_parts = lax.dynamic_slice(conv_parts, (0, 0, me * cshard), (N_DEV, DN_K, cshard))
    cpad = ((0, 0), (0, HALO - DN_K), (0, 0))
    conv_res = _adam(jnp.pad(conv_parts, cpad), jnp.pad(conv_w[0], cpad[1:]), jnp.pad(m_conv_w[0], cpad[1:]),
                     jnp.pad(v_conv_w[0], cpad[1:], constant_values=1.0), "adam_conv")
    g_conv_s, dl_conv, nm_conv, nv_conv = [t[:DN_K][None] for t in conv_res]

    loss = sm[0][0].reshape(())

    def group(idx, big_in, big_conv, big_kv, big_out):
        names = sm[idx][1:]
        return [ln_res[idx], big_in[None], names[0], names[1], names[2], names[3], big_conv, names[4], names[5], names[6],
                names[7], big_kv[None], big_out[None], names[8]]

    grads = group(0, g_w_in, g_conv_s, g_w_kv, g_w_out)
    deltas = group(1, dl_w_in, dl_conv, dl_w_kv, dl_w_out)
    new_m = group(2, nm_w_in, nm_conv, nm_w_kv, nm_w_out)
    new_v = group(3, nv_w_in, nv_conv, nv_w_kv, nv_w_out)
    return (loss, grad_x[None], *grads, *deltas, *new_m, *new_v)
```

```python
import jax
import jax.numpy as jnp
from jax import lax
from jax.experimental import pallas as pl
from jax.experimental.pallas import tpu as pltpu

F32 = jnp.float32
BF16 = jnp.bfloat16
HIGHEST = lax.Precision.HIGHEST
MESH_ID = pl.DeviceIdType.MESH

N_DEV = 8
EPS = 1e-6
GMLP_W = 512
GMLP_G = 4
GMLP_T = 128
DN_W = 1024
DN_H = 8
HEAD = 128
DN_K = 4
CH = 64
XA_W = 512
XA_H = 4
LANE = 128
HALO = 8
MAIN_W = 4 * DN_W + 3 * GMLP_W + 2 * XA_W
MIX_W = DN_W + GMLP_W + XA_W
VMEM_LIMIT = 56 * 1024 * 1024

ADAM_LR = 0.001
ADAM_B1 = 0.9
ADAM_B2 = 0.999
ADAM_EPS = 1e-08
ADAM_WD = 0.01
ADAM_STEP = 10


def _sds(shape, dtype=F32):
    return jax.ShapeDtypeStruct(tuple(shape), dtype)


def _params(sem=None):
    if sem is None:
        return pltpu.CompilerParams(vmem_limit_bytes=VMEM_LIMIT)
    return pltpu.CompilerParams(dimension_semantics=tuple(sem), vmem_limit_bytes=VMEM_LIMIT)


def _tile(n, prefs):
    for p in prefs:
        if n % p == 0:
            return p
    return n


def _mm(a, b):
    return jnp.dot(a.astype(BF16), b.astype(BF16), preferred_element_type=F32)


def _mm_nt(a, b):
    return lax.dot_general(a.astype(BF16), b.astype(BF16), (((1,), (1,)), ((), ())), preferred_element_type=F32)


def _mm_tn(a, b):
    return lax.dot_general(a.astype(BF16), b.astype(BF16), (((0,), (0,)), ((), ())), preferred_element_type=F32)


def _mm_hi(a, b):
    return jnp.dot(a, b, precision=HIGHEST, preferred_element_type=F32)


def _mm_3x(a, b):
    return jnp.dot(a, b, precision=lax.Precision.HIGH, preferred_element_type=F32)


_GELU_C = 0.7978845608028654
_GELU_A = 0.044715


def _gelu(x):
    return 0.5 * x * (1.0 + jnp.tanh(_GELU_C * (x + _GELU_A * x * x * x)))


def _gelu_grad(x):
    t = jnp.tanh(_GELU_C * (x + _GELU_A * x * x * x))
    return 0.5 * (1.0 + t) + 0.5 * x * (1.0 - t * t) * _GELU_C * (1.0 + 3.0 * _GELU_A * x * x)


def _silu(x):
    return x * jax.nn.sigmoid(x)


def _silu_grad(x):
    s = jax.nn.sigmoid(x)
    return s * (1.0 + x * (1.0 - s))


def _rowsum(x):
    return jnp.sum(x, axis=-1, keepdims=True)


def _colsum(x):
    return jnp.sum(x, axis=0, keepdims=True)


def _iota2(shape, dim):
    return lax.broadcasted_iota(jnp.int32, shape, dim)


def _chunk_tri(tm, upper):
    r = _iota2((tm, tm), 0)
    c = _iota2((tm, tm), 1)
    same = lax.shift_right_logical(r, 6) == lax.shift_right_logical(c, 6)
    tri = (r <= c) if upper else (r >= c)
    return jnp.where(same & tri, 1.0, 0.0).astype(F32)


N_CHIP = 4


def _mesh_place():
    x, y, c = lax.axis_index("x"), lax.axis_index("y"), lax.axis_index("c")
    chips = [(1 - x, y), (x, 1 - y), (1 - x, 1 - y)]
    return x, y, c, (x, y, 1 - c), chips


class _Gather:
    def __init__(self, ins, outs, send_sems, recv_sems, loc_sems):
        self.ins, self.outs, self.send_sems, self.recv_sems, self.loc_sems = ins, outs, send_sems, recv_sems, loc_sems
        self.x, self.y, self.c, self.sib, self.chips = _mesh_place()
        self.me = (self.x, self.y, self.c)
        north = self.c == 1
        self.relay_from = (jnp.where(north, 1 - self.x, self.x), jnp.where(north, self.y, 1 - self.y))
        self.relay_to = (jnp.where(north, self.x, 1 - self.x), jnp.where(north, 1 - self.y, self.y))

    def copy(self, a, k, block, to, src=None):
        slot = self.outs[a].at[4 * block[0] + 2 * block[1] + block[2]]
        return pltpu.make_async_remote_copy(
            src_ref=slot if src is None else src, dst_ref=slot, send_sem=self.send_sems.at[a, k],
            recv_sem=self.recv_sems.at[a, k], device_id=to, device_id_type=MESH_ID)

    def own(self, a):
        return pltpu.make_async_copy(self.ins[a], self.outs[a].at[4 * self.x + 2 * self.y + self.c], self.loc_sems.at[a])

    def first(self, a):
        return [self.copy(a, 0, self.me, self.sib, src=self.ins[a])] + [
            self.copy(a, 1 + j, self.me, (*self.chips[j], self.c), src=self.ins[a]) for j in range(2)]

    def relayed(self, a):
        return self.copy(a, 3, (*self.relay_from, self.c), (*self.relay_to, self.c))

    def passed(self, a, j):
        return self.copy(a, 4 + j, (*self.chips[j], self.c), self.sib)

    def start(self):
        for a in range(len(self.ins)):
            self.own(a).start()
            for cp in self.first(a):
                cp.start()

    def relay(self):
        for a in range(len(self.ins)):
            for j in range(2):
                self.copy(a, 1 + j, (*self.chips[j], self.c), self.me).wait_recv()
            self.relayed(a).start()
            for j in range(2):
                self.passed(a, j).start()

    def finish(self):
        n = len(self.ins)
        for a in range(n):
            self.copy(a, 3, (*self.chips[2], self.c), self.me).wait_recv()
            self.passed(a, 2).start()
        for a in range(n):
            self.copy(a, 0, self.sib, self.me).wait_recv()
            for j, chip in enumerate(self.chips):
                self.copy(a, 4 + j, (*chip, 1 - self.c), self.me).wait_recv()
        for a in range(n):
            for cp in self.first(a) + [self.relayed(a)] + [self.passed(a, j) for j in range(N_CHIP - 1)]:
                cp.wait_send()
            self.own(a).wait()

    @staticmethod
    def sems(n):
        return [pltpu.SemaphoreType.DMA((n, N_DEV - 1)), pltpu.SemaphoreType.DMA((n, N_DEV - 1)),
                pltpu.SemaphoreType.DMA((n,))]


def _gather_two_level(arrs, name):
    n = len(arrs)

    def body(*refs):
        g = _Gather(refs[:n], refs[n:2 * n], *refs[2 * n:])
        g.start()
        g.relay()
        g.finish()

    any_spec = pl.BlockSpec(memory_space=pl.ANY)
    return pl.pallas_call(
        body, name=name, out_shape=[_sds((N_DEV,) + a.shape, a.dtype) for a in arrs],
        in_specs=[any_spec] * n, out_specs=[any_spec] * n, scratch_shapes=_Gather.sems(n),
        compiler_params=pltpu.CompilerParams(has_side_effects=True),
    )(*arrs)


def _swap_halves(small, grads, name):
    n = len(grads)

    def body(*refs):
        small_ref = refs[0]
        ins = refs[1:1 + n]
        small_out = refs[1 + n]
        got = refs[2 + n:2 + 2 * n]
        s_send, s_recv, g_send, g_recv, loc_sem = refs[2 + 2 * n:]
        x, y, c, sib, _ = _mesh_place()
        me = 4 * x + 2 * y + c
        sends, recvs = [], []
        for j in range(1, N_DEV):
            px = 1 - x if (j >> 2) & 1 else x
            py = 1 - y if (j >> 1) & 1 else y
            pc = 1 - c if j & 1 else c
            cp = pltpu.make_async_remote_copy(
                src_ref=small_ref, dst_ref=small_out.at[me], send_sem=s_send.at[j - 1], recv_sem=s_recv.at[j - 1],
                device_id=(px, py, pc), device_id_type=MESH_ID)
            cp.start()
            sends.append(cp)
            recvs.append(pltpu.make_async_remote_copy(
                src_ref=small_ref, dst_ref=small_out.at[4 * px + 2 * py + pc], send_sem=s_send.at[j - 1],
                recv_sem=s_recv.at[j - 1], device_id=(px, py, pc), device_id_type=MESH_ID))
        own = pltpu.make_async_copy(small_ref, small_out.at[me], loc_sem)
        own.start()
        for a in range(n):
            for chip in range(N_CHIP):
                cp = pltpu.make_async_remote_copy(
                    src_ref=ins[a].at[2 * chip + 1 - c], dst_ref=got[a].at[chip], send_sem=g_send.at[a, chip],
                    recv_sem=g_recv.at[a, chip], device_id=sib, device_id_type=MESH_ID)
                cp.start()
                sends.append(cp)
                recvs.append(cp)
        for cp in sends:
            cp.wait_send()
        for cp in recvs:
            cp.wait_recv()
        own.wait()

    half = [_sds((N_CHIP,) + g.shape[1:], g.dtype) for g in grads]
    any_spec = pl.BlockSpec(memory_space=pl.ANY)
    res = pl.pallas_call(
        body, name=name, out_shape=[_sds((N_DEV,) + small.shape, small.dtype)] + half,
        in_specs=[any_spec] * (1 + n), out_specs=[any_spec] * (1 + n),
        scratch_shapes=[pltpu.SemaphoreType.DMA((N_DEV - 1,)), pltpu.SemaphoreType.DMA((N_DEV - 1,)),
                        pltpu.SemaphoreType.DMA((n, N_CHIP)), pltpu.SemaphoreType.DMA((n, N_CHIP)),
                        pltpu.SemaphoreType.DMA],
        compiler_params=pltpu.CompilerParams(has_side_effects=True),
    )(small, *grads)
    return res[0], res[1:]


def _pair_sums(core, mine, got):
    n = len(got)

    def body(core_ref, *refs):
        for a in range(n):
            refs[2 * n + a][...] = (refs[a][...].astype(F32) + refs[n + a][...].astype(F32)).astype(BF16)

    half = lambda g: (1, g.shape[1] // 2, g.shape[2])
    own = [pl.BlockSpec(half(g), lambda i, j, core_ref: (2 * i + core_ref[0], j, 0)) for g in got]
    slot = [pl.BlockSpec(half(g), lambda i, j, core_ref: (i, j, 0)) for g in got]
    return pl.pallas_call(
        body, name="pair_sums", out_shape=[_sds(g.shape, BF16) for g in got],
        grid_spec=pltpu.PrefetchScalarGridSpec(
            num_scalar_prefetch=1, grid=(N_CHIP, 2), in_specs=own + slot, out_specs=slot),
        compiler_params=_params(("parallel", "parallel")),
    )(core, *mine, *got)


class _ChipExchange:
    def __init__(self, ins, outs, send_sems, recv_sems, loc_sems):
        self.ins, self.outs, self.send_sems, self.recv_sems, self.loc_sems = ins, outs, send_sems, recv_sems, loc_sems
        self.x, self.y, self.c, _, self.chips = _mesh_place()
        self.mine = 2 * self.x + self.y

    def own(self, a):
        return pltpu.make_async_copy(self.ins[a].at[self.mine], self.outs[a].at[self.mine], self.loc_sems.at[a])

    def copy(self, a, j, lands_in):
        chip = self.chips[j]
        return pltpu.make_async_remote_copy(
            src_ref=self.ins[a].at[2 * chip[0] + chip[1]], dst_ref=self.outs[a].at[lands_in],
            send_sem=self.send_sems.at[a, j], recv_sem=self.recv_sems.at[a, j], device_id=(*chip, self.c),
            device_id_type=MESH_ID)

    def start(self):
        for a in range(len(self.ins)):
            self.own(a).start()
            for j in range(N_CHIP - 1):
                self.copy(a, j, self.mine).start()

    def finish(self):
        for a in range(len(self.ins)):
            for j, chip in enumerate(self.chips):
                self.copy(a, j, self.mine).wait_send()
                self.copy(a, j, 2 * chip[0] + chip[1]).wait_recv()
            self.own(a).wait()

    @staticmethod
    def sems(n):
        return [pltpu.SemaphoreType.DMA((n, N_CHIP - 1)), pltpu.SemaphoreType.DMA((n, N_CHIP - 1)),
                pltpu.SemaphoreType.DMA((n,))]


def _inproj(x, ln_g, w_main, w_ab, late):
    s, d = x.shape
    n = w_main.shape[1]
    tm = _tile(s, (256, 128))
    tn = _tile(n, (1664, 512, 128))
    nl = len(late)
    ni = s // tm

    def body(*refs):
        x_ref, g_ref, w_ref, wab_ref = refs[:4]
        proj_ref, ab_ref, ht_ref = refs[4 + nl:7 + nl]
        gather = _Gather(refs[4:4 + nl], refs[7 + nl:7 + 2 * nl], *refs[7 + 2 * nl:])
        step = pl.program_id(0)

        @pl.when(step == 0)
        def _():
            gather.start()

        xv = x_ref[...]
        r = lax.rsqrt(jnp.mean(xv * xv, axis=-1, keepdims=True) + EPS)
        hf = xv * r * g_ref[...]
        h = hf.astype(BF16)
        ht_ref[...] = hf.T.astype(BF16)
        ab_ref[...] = jnp.dot(h, wab_ref[...], preferred_element_type=F32)
        for c0 in range(0, n, tn):
            proj_ref[:, c0:c0 + tn] = jnp.dot(h, w_ref[:, c0:c0 + tn], preferred_element_type=F32)

        @pl.when(step == ni // 2)
        def _():
            gather.relay()

        @pl.when(step == ni - 1)
        def _():
            gather.finish()

    any_spec = pl.BlockSpec(memory_space=pl.ANY)
    once = lambda a: pl.BlockSpec(a.shape, lambda i: (0, 0), pipeline_mode=pl.Buffered(1))
    res = pl.pallas_call(
        body, name="inproj", grid=(ni,),
        in_specs=[pl.BlockSpec((tm, d), lambda i: (i, 0)), pl.BlockSpec((1, d), lambda i: (0, 0)), once(w_main),
                  once(w_ab)] + [any_spec] * nl,
        out_specs=[pl.BlockSpec((tm, n), lambda i: (i, 0)), pl.BlockSpec((tm, LANE), lambda i: (i, 0)),
                   pl.BlockSpec((d, tm), lambda i: (0, i))] + [any_spec] * nl,
        out_shape=[_sds((s, n)), _sds((s, LANE)), _sds((d, s), BF16)]
        + [_sds((N_DEV,) + a.shape, a.dtype) for a in late],
        scratch_shapes=_Gather.sems(nl),
        compiler_params=_params(("arbitrary",)),
    )(x, ln_g, w_main, w_ab, *late)
    return res[0], res[1], res[2], res[3:]


def _matmul_acc(a, b, name):
    m, k = a.shape
    n = b.shape[1]
    tm = _tile(m, (1024, 512, 256, 128))
    tn = _tile(n, (1024, 512, 256, 128))
    tk = _tile(k, (2048, 1024, 512, 256, 128))
    nk = k // tk

    def body(a_ref, b_ref, o_ref, acc):
        @pl.when(pl.program_id(2) == 0)
        def _():
            acc[...] = jnp.zeros_like(acc)

        acc[...] += jnp.dot(a_ref[...], b_ref[...], preferred_element_type=F32)

        @pl.when(pl.program_id(2) == nk - 1)
        def _():
            o_ref[...] = acc[...].astype(BF16)

    return pl.pallas_call(
        body, name=name, grid=(m // tm, n // tn, nk),
        in_specs=[pl.BlockSpec((tm, tk), lambda i, j, l: (i, l)), pl.BlockSpec((tk, tn), lambda i, j, l: (l, j))],
        out_specs=pl.BlockSpec((tm, tn), lambda i, j, l: (i, j)),
        out_shape=_sds((m, n), BF16), scratch_shapes=[pltpu.VMEM((tm, tn), F32)],
        compiler_params=_params(("parallel", "parallel", "arbitrary")),
    )(a, b)


def _matmul_tn(a, b, name):
    k, m = a.shape
    n = b.shape[1]
    tm = _tile(m, (1024, 512, 256, 128))
    tn = _tile(n, (1024, 512, 256, 128))
    tk = _tile(k, (2048, 1024, 512, 256, 128))
    nk = k // tk

    def body(a_ref, b_ref, o_ref, acc):
        @pl.when(pl.program_id(2) == 0)
        def _():
            acc[...] = jnp.zeros_like(acc)

        acc[...] += _mm_tn(a_ref[...], b_ref[...])

        @pl.when(pl.program_id(2) == nk - 1)
        def _():
            o_ref[...] = acc[...].astype(BF16)

    return pl.pallas_call(
        body, name=name, grid=(m // tm, n // tn, nk),
        in_specs=[pl.BlockSpec((tk, tm), lambda i, j, l: (l, i)), pl.BlockSpec((tk, tn), lambda i, j, l: (l, j))],
        out_specs=pl.BlockSpec((tm, tn), lambda i, j, l: (i, j)),
        out_shape=_sds((m, n), BF16), scratch_shapes=[pltpu.VMEM((tm, tn), F32)],
        compiler_params=_params(("parallel", "parallel", "arbitrary")),
    )(a, b)


def _dh_rms(pieces, w_rows, wab_rows, x, dx2, ln_g, chip_sums):
    s, d = x.shape
    npc = len(pieces)
    nx = len(chip_sums)
    tm = _tile(s, (256, 128))
    ni = s // tm
    widths = [p.shape[1] for p in pieces[:-1]]
    offs = [sum(widths[:p]) for p in range(npc - 1)]
    nw = len(w_rows)
    nin = npc + 2 * nw + 3

    def body(*refs):
        p_refs = refs[:npc]
        w_refs = refs[npc:npc + nw]
        wab_refs = refs[npc + nw:npc + 2 * nw]
        x_ref, dx2_ref, g_ref = refs[npc + 2 * nw:nin]
        gx_ref, dg_ref = refs[nin + nx:nin + nx + 2]
        exch = _ChipExchange(refs[nin:nin + nx], refs[nin + nx + 2:nin + 2 * nx + 2], *refs[nin + 2 * nx + 2:])
        step = pl.program_id(0)

        @pl.when(step == 0)
        def _():
            dg_ref[...] = jnp.zeros_like(dg_ref)
            exch.start()

        cols = []
        for w_ref, wab_ref in zip(w_refs, wab_refs):
            part = _mm_nt(p_refs[npc - 1][...], wab_ref[...])
            for p in range(npc - 1):
                part += _mm_nt(p_refs[p][...], w_ref[:, offs[p]:offs[p] + widths[p]])
            cols.append(part)
        dhv = jnp.concatenate(cols, axis=1)
        xv = x_ref[...]
        r = lax.rsqrt(jnp.mean(xv * xv, axis=-1, keepdims=True) + EPS)
        xhat = xv * r
        dg_ref[...] += _colsum(dhv * xhat)
        dxh = dhv * g_ref[...]
        gx_ref[...] = dx2_ref[...] + r * (dxh - xhat * jnp.mean(dxh * xhat, axis=-1, keepdims=True))

        @pl.when(step == ni - 1)
        def _():
            exch.finish()

    any_spec = pl.BlockSpec(memory_space=pl.ANY)
    row = pl.BlockSpec((tm, d), lambda i: (i, 0))
    vec = pl.BlockSpec((1, d), lambda i: (0, 0))
    once = lambda a: pl.BlockSpec(a.shape, lambda i: (0, 0), pipeline_mode=pl.Buffered(1))
    in_specs = [pl.BlockSpec((tm, p.shape[1]), lambda i: (i, 0)) for p in pieces]
    in_specs += [once(w) for w in w_rows] + [once(w) for w in wab_rows] + [row, row, vec] + [any_spec] * nx
    res = pl.pallas_call(
        body, name="dh_rms", grid=(ni,), in_specs=in_specs,
        out_specs=[row, vec] + [any_spec] * nx,
        out_shape=[_sds((s, d)), _sds((1, d))] + [_sds(p.shape, p.dtype) for p in chip_sums],
        scratch_shapes=_ChipExchange.sems(nx),
        compiler_params=_params(("arbitrary",)),
    )(*pieces, *w_rows, *wab_rows, x, dx2, ln_g, *chip_sums)
    return res[0], res[1], res[2:]


def _final(x, tgt, out_b, out_a, out_c, w_out, final_g):
    s, d = x.shape
    tm = _tile(s, (256, 128))

    def body(x_ref, t_ref, b_ref, a_ref, c_ref, w_ref, g_ref, dx2_ref, dx2b_ref, dm_ref, loss_ref, dg_ref):
        @pl.when(pl.program_id(0) == 0)
        def _():
            loss_ref[...] = jnp.zeros_like(loss_ref)
            dg_ref[...] = jnp.zeros_like(dg_ref)

        mixed = jnp.concatenate([b_ref[...], a_ref[...], c_ref[...]], axis=1)
        x2 = x_ref[...] + jnp.dot(mixed, w_ref[...], preferred_element_type=F32)
        r = lax.rsqrt(jnp.mean(x2 * x2, axis=-1, keepdims=True) + EPS)
        xhat = x2 * r
        g = g_ref[...]
        err = xhat * g - t_ref[...]
        tok = 0.5 * jnp.mean(err * err, axis=-1, keepdims=True)
        loss_ref[...] += jnp.broadcast_to(_colsum(tok), loss_ref.shape)
        dy = err * (1.0 / d)
        dg_ref[...] += _colsum(dy * xhat)
        dxh = dy * g
        dx2 = r * (dxh - xhat * jnp.mean(dxh * xhat, axis=-1, keepdims=True))
        dx2_ref[...] = dx2
        dx2b = dx2.astype(BF16)
        dx2b_ref[...] = dx2b
        dm_ref[...] = _mm_nt(dx2b, w_ref[...])

    row = pl.BlockSpec((tm, d), lambda i: (i, 0))
    vec = pl.BlockSpec((1, d), lambda i: (0, 0))
    return pl.pallas_call(
        body, name="final", grid=(s // tm,),
        in_specs=[row, row, pl.BlockSpec((tm, DN_W), lambda i: (i, 0)), pl.BlockSpec((tm, GMLP_W), lambda i: (i, 0)),
                  pl.BlockSpec((tm, XA_W), lambda i: (i, 0)), pl.BlockSpec((MIX_W, d), lambda i: (0, 0)), vec],
        out_specs=[row, row, pl.BlockSpec((tm, MIX_W), lambda i: (i, 0)), pl.BlockSpec((1, LANE), lambda i: (0, 0)), vec],
        out_shape=[_sds((s, d)), _sds((s, d), BF16), _sds((s, MIX_W)), _sds((1, LANE)), _sds((1, d))],
        compiler_params=_params(("arbitrary",)),
    )(x, tgt, out_b, out_a, out_c, w_out, final_g)


GU_BLK = (4 * DN_W) // GMLP_W


def _gmlp_norm(gv, lng, lnb):
    va = _gelu(gv)
    mu = jnp.mean(va, axis=-1, keepdims=True)
    xc = va - mu
    rstd = lax.rsqrt(jnp.mean(xc * xc, axis=-1, keepdims=True) + EPS)
    vhat = xc * rstd
    return vhat, rstd, vhat * lng + lnb


def _gmlp_fwd(proj, lng, lnb, ws, bs_t):
    s = proj.shape[0]
    tm = _tile(s, (512, 256, 128))

    def body(u_ref, v_ref, z_ref, lng_ref, lnb_ref, ws_ref, bst_ref, o_ref):
        _, _, vn = _gmlp_norm(v_ref[...], lng_ref[...], lnb_ref[...])
        tri = _iota2((GMLP_T, GMLP_T), 0) >= _iota2((GMLP_T, GMLP_T), 1)
        for g in range(GMLP_G):
            cs = slice(g * HEAD, (g + 1) * HEAD)
            w = jnp.where(tri, ws_ref[g], 0.0).astype(BF16)
            b = bst_ref[:, g:g + 1]
            for c in range(tm // GMLP_T):
                rs = slice(c * GMLP_T, (c + 1) * GMLP_T)
                sg = _mm(w, vn[rs, cs]) + b
                o_ref[rs, cs] = (_gelu(u_ref[rs, cs]) * sg * _silu(z_ref[rs, cs])).astype(BF16)

    col = lambda k: pl.BlockSpec((tm, GMLP_W), lambda i: (i, GU_BLK + k))
    vec = pl.BlockSpec((1, GMLP_W), lambda i: (0, 0))
    return pl.pallas_call(
        body, name="gmlp_fwd", grid=(s // tm,),
        in_specs=[col(0), col(1), col(2), vec, vec, pl.BlockSpec((GMLP_G, GMLP_T, GMLP_T), lambda i: (0, 0, 0)),
                  pl.BlockSpec((GMLP_T, GMLP_G), lambda i: (0, 0))],
        out_specs=pl.BlockSpec((tm, GMLP_W), lambda i: (i, 0)), out_shape=_sds((s, GMLP_W), BF16),
        compiler_params=_params(("parallel",)),
    )(proj, proj, proj, lng, lnb, ws, bs_t)


def _gmlp_bwd(proj, dmixed, lng, lnb, ws, bs_t):
    s = proj.shape[0]
    tm = _tile(s, (512, 256, 128))

    def body(u_ref, v_ref, z_ref, d_ref, lng_ref, lnb_ref, ws_ref, bst_ref,
             dp_ref, dws_ref, dbst_ref, dlng_ref, dlnb_ref, dvn):
        @pl.when(pl.program_id(0) == 0)
        def _():
            dws_ref[...] = jnp.zeros_like(dws_ref)
            dbst_ref[...] = jnp.zeros_like(dbst_ref)
            dlng_ref[...] = jnp.zeros_like(dlng_ref)
            dlnb_ref[...] = jnp.zeros_like(dlnb_ref)

        gv = v_ref[...]
        lng_v = lng_ref[...]
        vhat, rstd, vn = _gmlp_norm(gv, lng_v, lnb_ref[...])
        tri = _iota2((GMLP_T, GMLP_T), 0) >= _iota2((GMLP_T, GMLP_T), 1)
        for g in range(GMLP_G):
            cs = slice(g * HEAD, (g + 1) * HEAD)
            w = jnp.where(tri, ws_ref[g], 0.0).astype(BF16)
            b = bst_ref[:, g:g + 1]
            dw_acc = jnp.zeros((GMLP_T, GMLP_T), F32)
            db_acc = jnp.zeros((GMLP_T, 1), F32)
            for c in range(tm // GMLP_T):
                rs = slice(c * GMLP_T, (c + 1) * GMLP_T)
                vn_b = vn[rs, cs]
                sg = _mm(w, vn_b) + b
                gu = u_ref[rs, cs]
                gz = z_ref[rs, cs]
                da = d_ref[rs, cs]
                uact = _gelu(gu)
                sz = _silu(gz)
                ds = da * uact * sz
                dp_ref[rs, cs] = (da * sg * sz * _gelu_grad(gu)).astype(BF16)
                dp_ref[rs, 2 * GMLP_W + g * HEAD:2 * GMLP_W + (g + 1) * HEAD] = (da * uact * sg * _silu_grad(gz)).astype(BF16)
                dw_acc += _mm_nt(ds, vn_b)
                db_acc += _rowsum(ds)
                dvn[rs, cs] = _mm_tn(w, ds)
            dws_ref[g] += jnp.where(tri, dw_acc, 0.0)
            dbst_ref[:, g:g + 1] += db_acc
        dvn_v = dvn[...]
        dlng_ref[...] += _colsum(dvn_v * vhat)
        dlnb_ref[...] += _colsum(dvn_v)
        dvh = dvn_v * lng_v
        dva = rstd * (dvh - jnp.mean(dvh, axis=-1, keepdims=True) - vhat * jnp.mean(dvh * vhat, axis=-1, keepdims=True))
        dp_ref[:, GMLP_W:2 * GMLP_W] = (dva * _gelu_grad(gv)).astype(BF16)

    col = lambda k: pl.BlockSpec((tm, GMLP_W), lambda i: (i, GU_BLK + k))
    vec = pl.BlockSpec((1, GMLP_W), lambda i: (0, 0))
    wsp = pl.BlockSpec((GMLP_G, GMLP_T, GMLP_T), lambda i: (0, 0, 0))
    bsp = pl.BlockSpec((GMLP_T, GMLP_G), lambda i: (0, 0))
    return pl.pallas_call(
        body, name="gmlp_bwd", grid=(s // tm,),
        in_specs=[col(0), col(1), col(2), pl.BlockSpec((tm, GMLP_W), lambda i: (i, DN_W // GMLP_W)), vec, vec, wsp, bsp],
        out_specs=[pl.BlockSpec((tm, 3 * GMLP_W), lambda i: (i, 0)), wsp, bsp, vec, vec],
        out_shape=[_sds((s, 3 * GMLP_W), BF16), _sds((GMLP_G, GMLP_T, GMLP_T)), _sds((GMLP_T, GMLP_G)),
                   _sds((1, GMLP_W)), _sds((1, GMLP_W))],
        scratch_shapes=[pltpu.VMEM((tm, GMLP_W), F32)],
        compiler_params=_params(("arbitrary",)),
    )(proj, proj, proj, dmixed, lng, lnb, ws, bs_t)


CQ_BLK = (4 * DN_W + 3 * GMLP_W) // XA_W


def _memkv_fwd(mem, g, w_kv):
    nm, d = mem.shape

    def body(m_ref, g_ref, w_ref, kv_ref):
        mv = m_ref[...]
        r = lax.rsqrt(jnp.mean(mv * mv, axis=-1, keepdims=True) + EPS)
        kv_ref[...] = _mm(mv * r * g_ref[...], w_ref[...])

    return pl.pallas_call(body, name="memkv_fwd", out_shape=_sds((nm, 2 * XA_W)), compiler_params=_params())(mem, g, w_kv)


def _memkv_bwd(mem, g, w_kv, dkv):
    nm, d = mem.shape

    def body(m_ref, g_ref, w_ref, dkv_ref, dw_ref, dg_ref):
        mv = m_ref[...]
        r = lax.rsqrt(jnp.mean(mv * mv, axis=-1, keepdims=True) + EPS)
        xhat = mv * r
        dkv_v = dkv_ref[...]
        dw_ref[...] = _mm_tn(xhat * g_ref[...], dkv_v)
        dg_ref[...] = _colsum(_mm_nt(dkv_v, w_ref[...]) * xhat)

    return pl.pallas_call(body, name="memkv_bwd", out_shape=[_sds((d, 2 * XA_W)), _sds((1, d))],
                          compiler_params=_params())(mem, g, w_kv, dkv)


def _xattn_probs(q, mk):
    sc = _mm_nt(q, mk) * (HEAD ** -0.5)
    e = jnp.exp(sc - jnp.max(sc, axis=-1, keepdims=True))
    return e / _rowsum(e)


def _xattn_fwd(proj, mkv):
    s = proj.shape[0]
    nm = mkv.shape[0]
    tm = _tile(s, (512, 256, 128))

    def body(q_ref, z_ref, kv_ref, o_ref):
        for h in range(XA_H):
            cs = slice(h * HEAD, (h + 1) * HEAD)
            p = _xattn_probs(q_ref[:, cs], kv_ref[:, cs])
            ctx = _mm(p, kv_ref[:, XA_W + h * HEAD:XA_W + (h + 1) * HEAD])
            o_ref[:, cs] = (ctx * _silu(z_ref[:, cs])).astype(BF16)

    col = lambda k: pl.BlockSpec((tm, XA_W), lambda i: (i, CQ_BLK + k))
    return pl.pallas_call(
        body, name="xattn_fwd", grid=(s // tm,),
        in_specs=[col(0), col(1), pl.BlockSpec((nm, 2 * XA_W), lambda i: (0, 0))],
        out_specs=pl.BlockSpec((tm, XA_W), lambda i: (i, 0)), out_shape=_sds((s, XA_W), BF16),
        compiler_params=_params(("parallel",)),
    )(proj, proj, mkv)


def _xattn_bwd(proj, dmixed, mkv):
    s = proj.shape[0]
    nm = mkv.shape[0]
    tm = _tile(s, (512, 256, 128))

    def body(q_ref, z_ref, d_ref, kv_ref, dp_ref, dkv_ref):
        @pl.when(pl.program_id(0) == 0)
        def _():
            dkv_ref[...] = jnp.zeros_like(dkv_ref)

        for h in range(XA_H):
            cs = slice(h * HEAD, (h + 1) * HEAD)
            vs = slice(XA_W + h * HEAD, XA_W + (h + 1) * HEAD)
            q = q_ref[:, cs]
            z = z_ref[:, cs]
            mk = kv_ref[:, cs]
            mv = kv_ref[:, vs]
            p = _xattn_probs(q, mk)
            ctx = _mm(p, mv)
            dc = d_ref[:, cs]
            dctx = dc * _silu(z)
            dp_ref[:, vs] = (dc * ctx * _silu_grad(z)).astype(BF16)
            dp = _mm_nt(dctx, mv)
            dkv_ref[:, vs] += _mm_tn(p, dctx)
            ds = p * (dp - _rowsum(dp * p)) * (HEAD ** -0.5)
            dp_ref[:, cs] = _mm(ds, mk).astype(BF16)
            dkv_ref[:, cs] += _mm_tn(ds, q)

    col = lambda k: pl.BlockSpec((tm, XA_W), lambda i: (i, CQ_BLK + k))
    kvs = pl.BlockSpec((nm, 2 * XA_W), lambda i: (0, 0))
    return pl.pallas_call(
        body, name="xattn_bwd", grid=(s // tm,),
        in_specs=[col(0), col(1), pl.BlockSpec((tm, XA_W), lambda i: (i, (DN_W + GMLP_W) // XA_W)), kvs],
        out_specs=[pl.BlockSpec((tm, 2 * XA_W), lambda i: (i, 0)), kvs],
        out_shape=[_sds((s, 2 * XA_W), BF16), _sds((nm, 2 * XA_W))],
        compiler_params=_params(("arbitrary",)),
    )(proj, proj, dmixed, mkv)


def _softplus(x):
    return jnp.maximum(x, 0.0) + jnp.log1p(jnp.exp(-jnp.abs(x)))


def _dn_pre(proj, ab, conv_w, alog_row, dt_row):
    s = proj.shape[0]
    tm = _tile(s, (256, 128))
    w3 = 3 * DN_W

    def body(x_ref, halo_ref, ab_ref, cw_ref, al_ref, dt_ref, q_ref, k_ref, v_ref, gb_ref, gbt_ref, yc_ref):
        i = pl.program_id(0)
        for blk in range(w3 // HEAD):
            cs = slice(blk * HEAD, (blk + 1) * HEAD)
            xv = x_ref[:, cs]
            cat = jnp.concatenate([jnp.where(i > 0, halo_ref[:, cs], 0.0), xv[0:HALO]], axis=0)
            yc = cw_ref[DN_K - 1:DN_K, cs] * xv
            top = cw_ref[DN_K - 1:DN_K, cs] * xv[0:HALO]
            for t in range(DN_K - 1):
                back = DN_K - 1 - t
                yc += cw_ref[t:t + 1, cs] * pltpu.roll(xv, back, 0)
                top += cw_ref[t:t + 1, cs] * pltpu.roll(cat, back, 0)[HALO:2 * HALO]
            yc = jnp.concatenate([top, yc[HALO:tm]], axis=0)
            yc_ref[:, cs] = yc
            act = _silu(yc)
            hs = slice((blk % DN_H) * HEAD, (blk % DN_H + 1) * HEAD)
            if blk < DN_H:
                q_ref[:, hs] = act * (lax.rsqrt(_rowsum(act * act) + EPS) * (HEAD ** -0.5))
            elif blk < 2 * DN_H:
                k_ref[:, hs] = act * lax.rsqrt(_rowsum(act * act) + EPS)
            else:
                v_ref[:, hs] = act
        abv = ab_ref[...]
        lane = _iota2((tm, LANE), 1)
        g = jnp.where(lane < DN_H, -jnp.exp(al_ref[...]) * _softplus(abv + dt_ref[...]), 0.0)
        gc = _mm_hi(_chunk_tri(tm, False), g)
        gbv = jnp.where(lane < DN_H, gc, jnp.where(lane < 2 * DN_H, jax.nn.sigmoid(abv), 0.0))
        gb_ref[...] = gbv
        for c in range(tm // CH):
            gbt_ref[c] = gbv[c * CH:(c + 1) * CH, :].T[0:2 * DN_H, :]

    hb = tm // HALO
    row = lambda w: pl.BlockSpec((tm, w), lambda i: (i, 0))
    vec = pl.BlockSpec((1, LANE), lambda i: (0, 0))
    return pl.pallas_call(
        body, name="dn_pre", grid=(s // tm,),
        in_specs=[row(w3), pl.BlockSpec((HALO, w3), lambda i: (jnp.maximum(i * hb - 1, 0), 0)), row(LANE),
                  pl.BlockSpec((DN_K, w3), lambda i: (0, 0)), vec, vec],
        out_specs=[row(DN_W), row(DN_W), row(DN_W), row(LANE), pl.BlockSpec((tm // CH, 2 * DN_H, CH), lambda i: (i, 0, 0)),
                   row(w3)],
        out_shape=[_sds((s, DN_W)), _sds((s, DN_W)), _sds((s, DN_W)), _sds((s, LANE)),
                   _sds((s // CH, 2 * DN_H, CH)), _sds((s, w3))],
        compiler_params=_params(("parallel",)),
    )(proj, proj, ab, conv_w, alog_row, dt_row)


HEADS = tuple(range(DN_H))


def _hcols(h):
    return slice(h * HEAD, (h + 1) * HEAD)


def _chunk_scalings(k, v, gbv, gbt, h):
    gc = jnp.broadcast_to(gbv[:, h:h + 1], (CH, HEAD))
    beta = jnp.broadcast_to(gbv[:, DN_H + h:DN_H + h + 1], (CH, HEAD))
    gr = gbt[h:h + 1, :]
    ii = _iota2((CH, CH), 0)
    jj = _iota2((CH, CH), 1)
    dec = jnp.exp(jnp.where(ii >= jj, gc[:, 0:CH] - gr, -1e30))
    eg = jnp.exp(gc)
    gl = gr[:, CH - 1:CH]
    kb = k * beta
    return dict(beta=beta, dec=dec, eg=eg, gl=gl, ekd=jnp.exp(gl - gc), kb=kb, vb=v * beta, kbe=kb * eg)


def _chunk_scores(m, q, k):
    kq = _mm_nt(jnp.concatenate([m["kb"], q], axis=0), k)
    strict = _iota2((CH, CH), 0) > _iota2((CH, CH), 1)
    return jnp.where(strict, kq[0:CH] * m["dec"], 0.0), kq[CH:2 * CH] * m["dec"]


def _scan_cpb(s):
    return 8 if (s // CH) % 8 == 0 else 1


def _dn_fwd(q, k, v, gb, gbt, proj, norm_g):
    s = q.shape[0]
    cpb = _scan_cpb(s)
    tb = cpb * CH
    nblk = s // tb

    def body(q_ref, k_ref, v_ref, gb_ref, gbt_ref, z_ref, ng_ref,
             w_ref, qg_ref, kd_ref, t_ref, ai_ref, egl_ref, o_ref, vn_ref, st_ref, ob_ref, state):
        @pl.when(pl.program_id(0) == 0)
        def _():
            state[...] = jnp.zeros_like(state)

        ng = ng_ref[...]
        eye = jnp.where(_iota2((CH, CH), 0) == _iota2((CH, CH), 1), 1.0, 0.0).astype(F32)

        def chunk(c, carry):
            r0 = pl.multiple_of(c * CH, CH)
            rows = pl.ds(r0, CH)
            gbv = gb_ref[rows, :]
            gbt_v = gbt_ref[c]
            qs = [q_ref[rows, _hcols(h)] for h in HEADS]
            ks = [k_ref[rows, _hcols(h)] for h in HEADS]
            ms = [_chunk_scalings(ks[h], v_ref[rows, _hcols(h)], gbv, gbt_v, h) for h in HEADS]
            qgb = [(qs[h] * ms[h]["eg"]).astype(BF16) for h in HEADS]
            kdb = [(ks[h] * ms[h]["ekd"]).astype(BF16) for h in HEADS]
            egl = [jnp.broadcast_to(jnp.exp(ms[h]["gl"]), (1, LANE)) for h in HEADS]
            for h in HEADS:
                qg_ref[rows, _hcols(h)] = qgb[h]
                kd_ref[rows, _hcols(h)] = kdb[h]
                egl_ref[c, h:h + 1, :] = egl[h]
            sc = [_chunk_scores(ms[h], qs[h], ks[h]) for h in HEADS]
            for h in HEADS:
                ai_ref[h, rows, :] = sc[h][1]
            ts = [eye - sc[h][0] for h in HEADS]
            ps = [_mm_3x(sc[h][0], sc[h][0]) for h in HEADS]
            ts = [ts[h] + _mm_3x(ts[h], ps[h]) for h in HEADS]
            for _ in range(4):
                ps = [_mm(ps[h], ps[h]) for h in HEADS]
                ts = [ts[h] + _mm(ts[h], ps[h]) for h in HEADS]
            uw = [_mm(ts[h], jnp.concatenate([ms[h]["vb"], ms[h]["kbe"]], axis=1)) for h in HEADS]
            wb = [uw[h][:, HEAD:2 * HEAD].astype(BF16) for h in HEADS]
            for h in HEADS:
                t_ref[h, rows, :] = ts[h]
                w_ref[rows, _hcols(h)] = wb[h]
            sts = [state[h] for h in HEADS]
            stb = [sts[h].astype(BF16) for h in HEADS]
            for h in HEADS:
                st_ref[c, h] = stb[h]
            vnb = [(uw[h][:, 0:HEAD] - jnp.dot(wb[h], stb[h], preferred_element_type=F32)).astype(BF16) for h in HEADS]
            for h in HEADS:
                state[h] = sts[h] * egl[h] + _mm_tn(kdb[h], vnb[h])
            os_ = [jnp.dot(qgb[h], stb[h], preferred_element_type=F32) + _mm(sc[h][1], vnb[h]) for h in HEADS]
            for h in HEADS:
                o = os_[h]
                vn_ref[rows, _hcols(h)] = vnb[h]
                o_ref[rows, _hcols(h)] = o
                r = lax.rsqrt(jnp.mean(o * o, axis=-1, keepdims=True) + EPS)
                ob_ref[rows, _hcols(h)] = (o * r * ng * _silu(z_ref[rows, _hcols(h)])).astype(BF16)
            return carry

        lax.fori_loop(0, cpb, chunk, 0, unroll=4)

    row = pl.BlockSpec((tb, DN_W), lambda i: (i, 0))
    sq = pl.BlockSpec((DN_H, tb, CH), lambda i: (0, i, 0))
    return pl.pallas_call(
        body, name="dn_fwd", grid=(nblk,),
        in_specs=[row, row, row, pl.BlockSpec((tb, LANE), lambda i: (i, 0)),
                  pl.BlockSpec((cpb, 2 * DN_H, CH), lambda i: (i, 0, 0)), pl.BlockSpec((tb, DN_W), lambda i: (i, 3)),
                  pl.BlockSpec((1, HEAD), lambda i: (0, 0))],
        out_specs=[row, row, row, sq, sq, pl.BlockSpec((cpb, DN_H, LANE), lambda i: (i, 0, 0)), row, row,
                   pl.BlockSpec((cpb, DN_H, HEAD, HEAD), lambda i: (i, 0, 0, 0)), row],
        out_shape=[_sds((s, DN_W), BF16), _sds((s, DN_W), BF16), _sds((s, DN_W), BF16), _sds((DN_H, s, CH)),
                   _sds((DN_H, s, CH)), _sds((s // CH, DN_H, LANE)), _sds((s, DN_W)), _sds((s, DN_W), BF16),
                   _sds((s // CH, DN_H, HEAD, HEAD), BF16), _sds((s, DN_W), BF16)],
        scratch_shapes=[pltpu.VMEM((DN_H, HEAD, HEAD), F32)],
        compiler_params=_params(("arbitrary",)),
    )(q, k, v, gb, gbt, proj, norm_g)


def _dn_bwd(dmixed, o, proj, norm_g, w, qg, kd, ai, egl, q, k, v, gb, gbt, t, vn, st):
    s = o.shape[0]
    cpb = 4 if (s // CH) % 4 == 0 else 1
    tb = cpb * CH
    nblk = s // tb

    def body(dm_ref, o_ref, z_ref, ng_ref, w_ref, qg_ref, kd_ref, ai_ref, egl_ref,
             q_ref, k_ref, v_ref, gb_ref, gbt_ref, t_ref, vn_ref, st_ref,
             dq_ref, dk_ref, dv_ref, dgb_ref, dz_ref, dng_ref, dstate):
        @pl.when(pl.program_id(0) == 0)
        def _():
            dstate[...] = jnp.zeros_like(dstate)
            dng_ref[...] = jnp.zeros_like(dng_ref)

        ng = ng_ref[...]
        lane = _iota2((CH, LANE), 1)
        last = _iota2((CH, 1), 0) == CH - 1
        strict = _iota2((CH, CH), 0) > _iota2((CH, CH), 1)

        def chunk(cc, carry):
            c = cpb - 1 - cc
            r0 = pl.multiple_of(c * CH, CH)
            rows = pl.ds(r0, CH)
            dng = jnp.zeros((1, HEAD), F32)
            dob = []
            for h in HEADS:
                cs = _hcols(h)
                ov = o_ref[rows, cs]
                z = z_ref[rows, cs]
                db = dm_ref[rows, cs]
                r = lax.rsqrt(jnp.mean(ov * ov, axis=-1, keepdims=True) + EPS)
                ohat = ov * r
                dz_ref[rows, cs] = (db * ohat * ng * _silu_grad(z)).astype(BF16)
                dyn = db * _silu(z)
                dng += _colsum(dyn * ohat)
                doh = dyn * ng
                dob.append((r * (doh - ohat * jnp.mean(doh * ohat, axis=-1, keepdims=True))).astype(BF16))
            dng_ref[...] += dng
            dsn = [dstate[h] for h in HEADS]
            dsb = [dsn[h].astype(BF16) for h in HEADS]
            dvnb = [(_mm_tn(ai_ref[h, rows, :], dob[h])
                     + jnp.dot(kd_ref[rows, _hcols(h)], dsb[h], preferred_element_type=F32)).astype(BF16) for h in HEADS]
            part = [_mm_tn(qg_ref[rows, _hcols(h)], dob[h]) + egl_ref[c, h:h + 1, :] * dsn[h] for h in HEADS]
            for h in HEADS:
                dstate[h] = part[h] - _mm_tn(w_ref[rows, _hcols(h)], dvnb[h])
            gbv = gb_ref[rows, :]
            gbt_v = gbt_ref[c]
            qs = [q_ref[rows, _hcols(h)] for h in HEADS]
            ks = [k_ref[rows, _hcols(h)] for h in HEADS]
            vs = [v_ref[rows, _hcols(h)] for h in HEADS]
            ms = [_chunk_scalings(ks[h], vs[h], gbv, gbt_v, h) for h in HEADS]
            sts = [st_ref[c, h] for h in HEADS]
            vnb = [vn_ref[rows, _hcols(h)] for h in HEADS]
            tbf = [t_ref[h, rows, :].astype(BF16) for h in HEADS]
            sc = [_chunk_scores(ms[h], qs[h], ks[h]) for h in HEADS]
            xs_ = [_mm_nt(jnp.concatenate([dob[h], dvnb[h]], axis=0), sts[h]) for h in HEADS]
            dai = [_mm_nt(dob[h], vnb[h]) for h in HEADS]
            dkd = [_mm_nt(vnb[h], dsb[h]) for h in HEADS]
            dqg = [xs_[h][0:CH] for h in HEADS]
            duw = [jnp.concatenate([dvnb[h], (-xs_[h][CH:2 * CH]).astype(BF16)], axis=1) for h in HEADS]
            dt = [_mm_nt(duw[h], jnp.concatenate([ms[h]["vb"], ms[h]["kbe"]], axis=1)) for h in HEADS]
            dvk = [_mm_tn(tbf[h], duw[h]) for h in HEADS]
            tdt = [_mm_tn(tbf[h], dt[h]) for h in HEADS]
            da = [jnp.where(strict, -_mm_nt(tdt[h], tbf[h]), 0.0) for h in HEADS]
            dsc = [jnp.concatenate([da[h] * ms[h]["dec"], dai[h] * ms[h]["dec"]], axis=0) for h in HEADS]
            dkq = [_mm(dsc[h], ks[h]) for h in HEADS]
            dk1 = [_mm_tn(dsc[h], jnp.concatenate([ms[h]["kb"], qs[h]], axis=0)) for h in HEADS]
            dgb = jnp.zeros((CH, LANE), F32)
            for h in HEADS:
                m = ms[h]
                eg, ekd, beta = m["eg"], m["ekd"], m["beta"]
                dvb = dvk[h][:, 0:HEAD]
                dkbe = dvk[h][:, HEAD:2 * HEAD]
                kdv = ks[h] * ekd
                dkb = dkq[h][0:CH] + dkbe * eg
                dq_ref[rows, _hcols(h)] = dkq[h][CH:2 * CH] + dqg[h] * eg
                dk_ref[rows, _hcols(h)] = dk1[h] + dkd[h] * ekd + dkb * beta
                dv_ref[rows, _hcols(h)] = dvb * beta
                dkd_kd = dkd[h] * kdv
                dgl = (jnp.exp(m["gl"]) * _rowsum(_colsum(sts[h].astype(F32) * dsb[h].astype(F32)))
                       + _rowsum(_colsum(dkd_kd)))
                mm_ = da[h] * sc[h][0] + dai[h] * sc[h][1]
                dgc = (_rowsum(mm_ - mm_.T) + _rowsum(dqg[h] * qs[h] * eg - dkd_kd + dkbe * m["kbe"])
                       + jnp.where(last, dgl, 0.0))
                dbeta = _rowsum(dkb * ks[h] + dvb * vs[h])
                dgb = jnp.where(lane == h, dgc, jnp.where(lane == DN_H + h, dbeta, dgb))
            dgb_ref[rows, :] = dgb
            return carry

        lax.fori_loop(0, cpb, chunk, 0, unroll=2)

    rev = lambda i: (nblk - 1 - i, 0)
    row = pl.BlockSpec((tb, DN_W), rev)
    vec = pl.BlockSpec((1, HEAD), lambda i: (0, 0))
    sq = pl.BlockSpec((DN_H, tb, CH), lambda i: (0, nblk - 1 - i, 0))
    gbs = pl.BlockSpec((tb, LANE), rev)
    return pl.pallas_call(
        body, name="dn_bwd", grid=(nblk,),
        in_specs=[row, row, pl.BlockSpec((tb, DN_W), lambda i: (nblk - 1 - i, 3)), vec, row, row, row, sq,
                  pl.BlockSpec((cpb, DN_H, LANE), lambda i: (nblk - 1 - i, 0, 0)),
                  row, row, row, gbs, pl.BlockSpec((cpb, 2 * DN_H, CH), lambda i: (nblk - 1 - i, 0, 0)), sq, row,
                  pl.BlockSpec((cpb, DN_H, HEAD, HEAD), lambda i: (nblk - 1 - i, 0, 0, 0))],
        out_specs=[row, row, row, gbs, row, vec],
        out_shape=[_sds((s, DN_W)), _sds((s, DN_W)), _sds((s, DN_W)), _sds((s, LANE)), _sds((s, DN_W), BF16),
                   _sds((1, HEAD))],
        scratch_shapes=[pltpu.VMEM((DN_H, HEAD, HEAD), F32)],
        compiler_params=_params(("arbitrary",)),
    )(dmixed, o, proj, norm_g, w, qg, kd, ai, egl, q, k, v, gb, gbt, t, vn, st)


def _dn_pre_bwd(proj, yc_all, ab, conv_w, alog_row, dt_row, dq, dk, dv, dgb):
    s = proj.shape[0]
    tm = _tile(s, (256, 128))
    w3 = 3 * DN_W
    nblk = s // tm

    def body(x_ref, yc_ref, ab_ref, cw_ref, al_ref, dt_ref, dq_ref, dk_ref, dv_ref, dgb_ref,
             dx_ref, dab_ref, dcw_ref, dal_ref, ddt_ref, carry):
        i = pl.program_id(0)

        @pl.when(i == 0)
        def _():
            carry[...] = jnp.zeros_like(carry)
            dcw_ref[...] = jnp.zeros_like(dcw_ref)
            dal_ref[...] = jnp.zeros_like(dal_ref)
            ddt_ref[...] = jnp.zeros_like(ddt_ref)

        for blk in range(w3 // HEAD):
            cs = slice(blk * HEAD, (blk + 1) * HEAD)
            yc = yc_ref[:, cs]
            sg = jax.nn.sigmoid(yc)
            act = yc * sg
            dact = sg + act * (1.0 - sg)
            if blk < 2 * DN_H:
                d_ref = dq_ref if blk < DN_H else dk_ref
                dn = d_ref[:, (blk % DN_H) * HEAD:(blk % DN_H + 1) * HEAD]
                rn = lax.rsqrt(_rowsum(act * act) + EPS)
                nh = act * rn
                scale = HEAD ** -0.5 if blk < DN_H else 1.0
                dyc = (scale * rn) * (dn - nh * _rowsum(dn * nh)) * dact
            else:
                dyc = dv_ref[:, (blk - 2 * DN_H) * HEAD:(blk - 2 * DN_H + 1) * HEAD] * dact
            xv = x_ref[:, cs]
            cat = jnp.concatenate([dyc[tm - HALO:tm], carry[:, cs]], axis=0)
            dcw_ref[DN_K - 1:DN_K, cs] += _colsum(dyc * xv)
            dx = cw_ref[DN_K - 1:DN_K, cs] * dyc
            for t in range(DN_K - 1):
                ahead = DN_K - 1 - t
                view = jnp.concatenate([pltpu.roll(dyc, tm - ahead, 0)[0:tm - HALO],
                                        pltpu.roll(cat, 2 * HALO - ahead, 0)[0:HALO]], axis=0)
                dcw_ref[t:t + 1, cs] += _colsum(view * xv)
                dx += cw_ref[t:t + 1, cs] * view
            dx_ref[:, cs] = dx.astype(BF16)
            carry[:, cs] = dyc[0:HALO]

        lane = _iota2((tm, LANE), 1)
        dgbv = dgb_ref[...]
        dg = _mm_hi(_chunk_tri(tm, True), jnp.where(lane < DN_H, dgbv, 0.0))
        abv = ab_ref[...]
        xa = abv + dt_ref[...]
        nea = -jnp.exp(al_ref[...])
        d_da = jnp.where(lane < DN_H, dg * nea * jax.nn.sigmoid(xa), 0.0)
        dal_ref[...] += _colsum(jnp.where(lane < DN_H, dg * nea * _softplus(xa), 0.0))
        ddt_ref[...] += _colsum(d_da)
        beta = jax.nn.sigmoid(abv)
        d_db = jnp.where((lane >= DN_H) & (lane < 2 * DN_H), dgbv * beta * (1.0 - beta), 0.0)
        dab_ref[...] = (d_da + d_db).astype(BF16)

    rev = lambda i: (nblk - 1 - i, 0)
    row = lambda w: pl.BlockSpec((tm, w), rev)
    vec = pl.BlockSpec((1, LANE), lambda i: (0, 0))
    cws = pl.BlockSpec((DN_K, w3), lambda i: (0, 0))
    return pl.pallas_call(
        body, name="dn_pre_bwd", grid=(nblk,),
        in_specs=[row(w3), row(w3), row(LANE), cws, vec, vec, row(DN_W), row(DN_W), row(DN_W), row(LANE)],
        out_specs=[row(w3), row(LANE), cws, vec, vec],
        out_shape=[_sds((s, w3), BF16), _sds((s, LANE), BF16), _sds((DN_K, w3)), _sds((1, LANE)), _sds((1, LANE))],
        scratch_shapes=[pltpu.VMEM((HALO, w3), F32)],
        compiler_params=_params(("arbitrary",)),
    )(proj, yc_all, ab, conv_w, alog_row, dt_row, dq, dk, dv, dgb)


def _adam(parts, w, m, v, name):
    r, c = w.shape
    n_parts = parts.shape[0]
    small = n_parts * r * c * 4 <= 4 * 1024 * 1024
    tr = r if small else _tile(r, (128, 64, 32, 16, 8))

    def body(p_ref, w_ref, m_ref, v_ref, g_ref, d_ref, nm_ref, nv_ref):
        g = p_ref[0].astype(F32)
        for k in range(1, n_parts):
            g = g + p_ref[k].astype(F32)
        g_ref[...] = g
        mn = ADAM_B1 * m_ref[...] + (1.0 - ADAM_B1) * g
        vn = ADAM_B2 * v_ref[...] + (1.0 - ADAM_B2) * (g * g)
        m_hat = mn / (1.0 - ADAM_B1 ** ADAM_STEP)
        v_hat = vn / (1.0 - ADAM_B2 ** ADAM_STEP)
        d_ref[...] = -ADAM_LR * (m_hat / (jnp.sqrt(v_hat) + ADAM_EPS) + ADAM_WD * w_ref[...])
        nm_ref[...] = mn
        nv_ref[...] = vn

    blk = pl.BlockSpec((tr, c), lambda i: (i, 0))
    return pl.pallas_call(
        body, name=name, grid=(r // tr,),
        in_specs=[pl.BlockSpec((n_parts, tr, c), lambda i: (0, i, 0)), blk, blk, blk],
        out_specs=[blk, blk, blk, blk], out_shape=[_sds((r, c))] * 4,
        compiler_params=_params(("parallel",)),
    )(parts, w, m, v)


_PACK_ROWS = 8


def _pack(vals):
    tiles = []
    for a in vals:
        flat = a.reshape(-1).astype(F32)
        unit = _PACK_ROWS * LANE
        n = -(-flat.shape[0] // unit) * unit
        tiles.append(jnp.pad(flat, (0, n - flat.shape[0])).reshape(n // LANE, LANE))
    return jnp.concatenate(tiles, axis=0)


def _unpack(packed, shapes):
    out = []
    r0 = 0
    for shp in shapes:
        size = 1
        for dim in shp:
            size *= dim
        unit = _PACK_ROWS * LANE
        rows = -(-size // unit) * _PACK_ROWS
        out.append(packed[r0:r0 + rows].reshape(-1)[:size].reshape(shp))
        r0 += rows
    return out


def _lane_row(vec8):
    return jnp.pad(vec8.reshape(1, -1).astype(F32), ((0, 0), (0, LANE - vec8.size)))


def kernel(x, mem, ln_g, w_in, gmlp_ln_g, gmlp_ln_b, gmlp_ws, gmlp_bs, conv_w, dn_a_log, dn_dt_bias, dn_norm_g, mem_norm_g, w_mem_kv, w_out, final_g, loss_target, m_ln_g, m_w_in, m_gmlp_ln_g, m_gmlp_ln_b, m_gmlp_ws, m_gmlp_bs, m_conv_w, m_dn_a_log, m_dn_dt_bias, m_dn_norm_g, m_mem_norm_g, m_w_mem_kv, m_w_out, m_final_g, v_ln_g, v_w_in, v_gmlp_ln_g, v_gmlp_ln_b, v_gmlp_ws, v_gmlp_bs, v_conv_w, v_dn_a_log, v_dn_dt_bias, v_dn_norm_g, v_mem_norm_g, v_w_mem_kv, v_w_out, v_final_g):
    xs = x[0]
    mems = mem[0]
    tgt = loss_target[0]
    s, d = xs.shape
    shard_w = w_in.shape[2]
    in_w = N_DEV * shard_w
    me = 4 * lax.axis_index("x") + 2 * lax.axis_index("y") + lax.axis_index("c")

    (g_in,) = _gather_two_level([w_in[0].astype(BF16)], "gather_w_in")
    o_g, o_dn, o_ab = 0, 3 * GMLP_W, 3 * GMLP_W + 4 * DN_W
    o_xa = o_ab + 2 * DN_H

    def shard_cols(g, lo, hi):
        out = []
        while lo < hi:
            sh = lo // shard_w
            end = min(hi, (sh + 1) * shard_w)
            out.append(g[sh][:, lo - sh * shard_w:end - sh * shard_w])
            lo = end
        return out

    def own_layout(g):
        main = jnp.concatenate(shard_cols(g, o_dn, o_ab) + shard_cols(g, o_g, o_dn) + shard_cols(g, o_xa, in_w), axis=1)
        return main, jnp.pad(jnp.concatenate(shard_cols(g, o_ab, o_xa), axis=1), ((0, 0), (0, LANE - 2 * DN_H)))

    w_main, w_ab = own_layout(g_in)

    ln_g2 = ln_g.reshape(1, d)
    lng2 = gmlp_ln_g.reshape(1, GMLP_W)
    lnb2 = gmlp_ln_b.reshape(1, GMLP_W)
    ws3 = gmlp_ws[0]
    bs_t = gmlp_bs[0].T
    alog_row = _lane_row(dn_a_log)
    dt_row = _lane_row(dn_dt_bias)
    dn_g2 = dn_norm_g.reshape(1, HEAD)
    mem_g2 = mem_norm_g.reshape(1, d)
    fin_g2 = final_g.reshape(1, d)

    proj, ab, h_t, (g_out, g_kv, g_conv) = _inproj(
        xs, ln_g2, w_main, w_ab, [w_out[0].astype(BF16), w_mem_kv[0].astype(BF16), conv_w[0]])
    wo = g_out.reshape(MIX_W, d)
    wo_perm = jnp.concatenate([wo[GMLP_W:GMLP_W + DN_W], wo[0:GMLP_W], wo[GMLP_W + DN_W:MIX_W]], axis=0)
    w_kv = g_kv.reshape(d, 2 * XA_W)
    conv_full = g_conv.transpose(1, 0, 2).reshape(DN_K, 3 * DN_W)
    out_a = _gmlp_fwd(proj, lng2, lnb2, ws3, bs_t)
    mkv = _memkv_fwd(mems, mem_g2, w_kv)
    out_c = _xattn_fwd(proj, mkv)
    q, k, v, gb, gbt, yc = _dn_pre(proj, ab, conv_full, alog_row, dt_row)
    wk, qg, kd, tmat, ai, egl, o, vn, st, out_b = _dn_fwd(q, k, v, gb, gbt, proj, dn_g2)

    dx2, dx2b, dmixed, loss_acc, d_fin_g = _final(xs, tgt, out_b, out_a, out_c, wo_perm, fin_g2)

    dwo_b = _matmul_tn(out_b, dx2b, "dw_out_b")
    dwo_a = _matmul_tn(out_a, dx2b, "dw_out_a")
    dwo_c = _matmul_tn(out_c, dx2b, "dw_out_c")
    d_w_out = jnp.concatenate([dwo_a, dwo_b, dwo_c], axis=0)

    dp_g, d_ws, d_bst, d_lng, d_lnb = _gmlp_bwd(proj, dmixed, lng2, lnb2, ws3, bs_t)
    dp_x, dmkv = _xattn_bwd(proj, dmixed, mkv)
    d_w_kv, d_mem_g = _memkv_bwd(mems, mem_g2, w_kv, dmkv)
    dq, dk, dv, dgb, dp_dz, d_dn_g = _dn_bwd(dmixed, o, proj, dn_g2, wk, qg, kd, ai, egl, q, k, v, gb, gbt, tmat, vn, st)
    dp_qkv, dp_ab, d_conv, d_alog, d_dt = _dn_pre_bwd(proj, yc, ab, conv_full, alog_row, dt_row, dq, dk, dv, dgb)

    dw_qkv = _matmul_acc(h_t, dp_qkv, "dw_in_qkv")
    dw_dz = _matmul_acc(h_t, dp_dz, "dw_in_dz")
    dw_gm = _matmul_acc(h_t, dp_g, "dw_in_gmlp")
    dw_xa = _matmul_acc(h_t, dp_x, "dw_in_xa")
    dw_ab = _matmul_acc(h_t, dp_ab, "dw_in_ab")
    segs = [(o_g, dw_gm), (o_dn, dw_qkv), (o_dn + 3 * DN_W, dw_dz), (o_ab, dw_ab[:, :2 * DN_H]), (o_xa, dw_xa)]
    shards = []
    for sh in range(N_DEV):
        lo, hi = sh * shard_w, (sh + 1) * shard_w
        parts = [arr[:, max(lo, off) - off:min(hi, off + arr.shape[1]) - off] for off, arr in segs
                 if off < hi and off + arr.shape[1] > lo]
        shards.append(jnp.concatenate(parts, axis=1).astype(BF16))
    send_in = jnp.stack(shards)

    small_shapes = [(1, 1), gmlp_ln_g.shape, gmlp_ln_b.shape, gmlp_ws.shape, gmlp_bs.shape, dn_a_log.shape,
                    dn_dt_bias.shape, dn_norm_g.shape, mem_norm_g.shape, final_g.shape, (DN_K, 3 * DN_W)]
    small_g = _pack([loss_acc[0:1, 0:1], d_lng, d_lnb, d_ws, d_bst.T, d_alog[:, :DN_H], d_dt[:, :DN_H], d_dn_g, d_mem_g,
                     d_fin_g, d_conv])
    zc = jnp.zeros((DN_K, 3 * DN_W), F32)
    z1 = jnp.zeros((1, 1), F32)
    small_w = _pack([z1, gmlp_ln_g, gmlp_ln_b, gmlp_ws, gmlp_bs, dn_a_log, dn_dt_bias, dn_norm_g, mem_norm_g, final_g, zc])
    small_m = _pack([z1, m_gmlp_ln_g, m_gmlp_ln_b, m_gmlp_ws, m_gmlp_bs, m_dn_a_log, m_dn_dt_bias, m_dn_norm_g,
                     m_mem_norm_g, m_final_g, zc])
    small_v = _pack([z1 + 1.0, v_gmlp_ln_g, v_gmlp_ln_b, v_gmlp_ws, v_gmlp_bs, v_dn_a_log, v_dn_dt_bias, v_dn_norm_g,
                     v_mem_norm_g, v_final_g, zc + 1.0])

    send_out = d_w_out.reshape(N_DEV, MIX_W // N_DEV, d).astype(BF16)
    send_kv = d_w_kv.reshape(N_DEV, d // N_DEV, 2 * XA_W).astype(BF16)
    sends = [send_in, send_out, send_kv]
    all_small, got = _swap_halves(small_g, sends, "swap_halves")
    core = lax.axis_index("c").astype(jnp.int32).reshape(1)
    chip_sums = _pair_sums(core, sends, got)
    grad_x, d_ln_g, (r_in, r_out, r_kv) = _dh_rms(
        [dp_qkv, dp_dz, dp_g, dp_x, dp_ab], [w_main], [w_ab], xs, dx2, ln_g2, chip_sums)
    (all_ln_g,) = _gather_two_level([_pack([d_ln_g])], "gather_ln_g")

    g_w_in, dl_w_in, nm_w_in, nv_w_in = _adam(r_in, w_in[0], m_w_in[0], v_w_in[0], "adam_w_in")
    g_w_out, dl_w_out, nm_w_out, nv_w_out = _adam(r_out, w_out[0], m_w_out[0], v_w_out[0], "adam_w_out")
    g_w_kv, dl_w_kv, nm_w_kv, nv_w_kv = _adam(r_kv, w_mem_kv[0], m_w_mem_kv[0], v_w_mem_kv[0], "adam_w_kv")
    sm = [_unpack(t, small_shapes) for t in _adam(all_small, small_w, small_m, small_v, "adam_small")]
    ln_res = [_unpack(t, [ln_g.shape])[0]
              for t in _adam(all_ln_g, _pack([ln_g]), _pack([m_ln_g]), _pack([v_ln_g]), "adam_ln_g")]

    conv_parts = lax.dynamic_slice(all_small, (0, all_small.shape[1] - (DN_K * 3 * DN_W) // LANE, 0),
                                   (N_DEV, (DN_K * 3 * DN_W) // LANE, LANE)).reshape(N_DEV, DN_K, 3 * DN_W)
    cshard = conv_w.shape[2]
    conv_parts = lax.dynamic_slice(conv_parts, (0, 0, me * cshard), (N_DEV, DN_K, cshard))
    cpad = ((0, 0), (0, HALO - DN_K), (0, 0))
    conv_res = _adam(jnp.pad(conv_parts, cpad), jnp.pad(conv_w[0], cpad[1:]), jnp.pad(m_conv_w[0], cpad[1:]),
                     jnp.pad(v_conv_w[0], cpad[1:], constant_values=1.0), "adam_conv")
    g_conv_s, dl_conv, nm_conv, nv_conv = [t[:DN_K][None] for t in conv_res]

    loss = sm[0][0].reshape(())

    def group(idx, big_in, big_conv, big_kv, big_out):
        names = sm[idx][1:]
        return [ln_res[idx], big_in[None], names[0], names[1], names[2], names[3], big_conv, names[4], names[5], names[6],
                names[7], big_kv[None], big_out[None], names[8]]

    grads = group(0, g_w_in, g_conv_s, g_w_kv, g_w_out)
    deltas = group(1, dl_w_in, dl_conv, dl_w_kv, dl_w_out)
    new_m = group(2, nm_w_in, nm_conv, nm_w_kv, nm_w_out)
    new_v = group(3, nv_w_in, nv_conv, nv_w_kv, nv_w_out)
    return (loss, grad_x[None], *grads, *deltas, *new_m, *new_v)
```

```python
import jax
import jax.numpy as jnp
from jax import lax
from jax.experimental import pallas as pl
from jax.experimental.pallas import tpu as pltpu

F32 = jnp.float32
BF16 = jnp.bfloat16
HIGHEST = lax.Precision.HIGHEST
MESH_ID = pl.DeviceIdType.MESH

N_DEV = 8
EPS = 1e-6
GMLP_W = 512
GMLP_G = 4
GMLP_T = 128
DN_W = 1024
DN_H = 8
HEAD = 128
DN_K = 4
CH = 64
XA_W = 512
XA_H = 4
LANE = 128
HALO = 8
MAIN_W = 4 * DN_W + 3 * GMLP_W + 2 * XA_W
MIX_W = DN_W + GMLP_W + XA_W
VMEM_LIMIT = 56 * 1024 * 1024

ADAM_LR = 0.001
ADAM_B1 = 0.9
ADAM_B2 = 0.999
ADAM_EPS = 1e-08
ADAM_WD = 0.01
ADAM_STEP = 10


def _sds(shape, dtype=F32):
    return jax.ShapeDtypeStruct(tuple(shape), dtype)


def _params(sem=None):
    if sem is None:
        return pltpu.CompilerParams(vmem_limit_bytes=VMEM_LIMIT)
    return pltpu.CompilerParams(dimension_semantics=tuple(sem), vmem_limit_bytes=VMEM_LIMIT)


def _tile(n, prefs):
    for p in prefs:
        if n % p == 0:
            return p
    return n


def _mm(a, b):
    return jnp.dot(a.astype(BF16), b.astype(BF16), preferred_element_type=F32)


def _mm_nt(a, b):
    return lax.dot_general(a.astype(BF16), b.astype(BF16), (((1,), (1,)), ((), ())), preferred_element_type=F32)


def _mm_tn(a, b):
    return lax.dot_general(a.astype(BF16), b.astype(BF16), (((0,), (0,)), ((), ())), preferred_element_type=F32)


def _mm_hi(a, b):
    return jnp.dot(a, b, precision=HIGHEST, preferred_element_type=F32)


def _mm_3x(a, b):
    return jnp.dot(a, b, precision=lax.Precision.HIGH, preferred_element_type=F32)


_GELU_C = 0.7978845608028654
_GELU_A = 0.044715


def _gelu(x):
    return 0.5 * x * (1.0 + jnp.tanh(_GELU_C * (x + _GELU_A * x * x * x)))


def _gelu_grad(x):
    t = jnp.tanh(_GELU_C * (x + _GELU_A * x * x * x))
    return 0.5 * (1.0 + t) + 0.5 * x * (1.0 - t * t) * _GELU_C * (1.0 + 3.0 * _GELU_A * x * x)


def _silu(x):
    return x * jax.nn.sigmoid(x)


def _silu_grad(x):
    s = jax.nn.sigmoid(x)
    return s * (1.0 + x * (1.0 - s))


def _rowsum(x):
    return jnp.sum(x, axis=-1, keepdims=True)


def _colsum(x):
    return jnp.sum(x, axis=0, keepdims=True)


def _iota2(shape, dim):
    return lax.broadcasted_iota(jnp.int32, shape, dim)


def _chunk_tri(tm, upper):
    r = _iota2((tm, tm), 0)
    c = _iota2((tm, tm), 1)
    same = lax.shift_right_logical(r, 6) == lax.shift_right_logical(c, 6)
    tri = (r <= c) if upper else (r >= c)
    return jnp.where(same & tri, 1.0, 0.0).astype(F32)


N_CHIP = 4


def _mesh_place():
    x, y, c = lax.axis_index("x"), lax.axis_index("y"), lax.axis_index("c")
    chips = [(1 - x, y), (x, 1 - y), (1 - x, 1 - y)]
    return x, y, c, (x, y, 1 - c), chips


class _Gather:
    def __init__(self, ins, outs, send_sems, recv_sems, loc_sems):
        self.ins, self.outs, self.send_sems, self.recv_sems, self.loc_sems = ins, outs, send_sems, recv_sems, loc_sems
        self.x, self.y, self.c, self.sib, self.chips = _mesh_place()
        self.me = (self.x, self.y, self.c)
        north = self.c == 1
        self.relay_from = (jnp.where(north, 1 - self.x, self.x), jnp.where(north, self.y, 1 - self.y))
        self.relay_to = (jnp.where(north, self.x, 1 - self.x), jnp.where(north, 1 - self.y, self.y))

    def copy(self, a, k, block, to, src=None):
        slot = self.outs[a].at[4 * block[0] + 2 * block[1] + block[2]]
        return pltpu.make_async_remote_copy(
            src_ref=slot if src is None else src, dst_ref=slot, send_sem=self.send_sems.at[a, k],
            recv_sem=self.recv_sems.at[a, k], device_id=to, device_id_type=MESH_ID)

    def own(self, a):
        return pltpu.make_async_copy(self.ins[a], self.outs[a].at[4 * self.x + 2 * self.y + self.c], self.loc_sems.at[a])

    def first(self, a):
        return [self.copy(a, 0, self.me, self.sib, src=self.ins[a])] + [
            self.copy(a, 1 + j, self.me, (*self.chips[j], self.c), src=self.ins[a]) for j in range(2)]

    def relayed(self, a):
        return self.copy(a, 3, (*self.relay_from, self.c), (*self.relay_to, self.c))

    def passed(self, a, j):
        return self.copy(a, 4 + j, (*self.chips[j], self.c), self.sib)

    def start(self):
        for a in range(len(self.ins)):
            self.own(a).start()
            for cp in self.first(a):
                cp.start()

    def relay(self):
        for a in range(len(self.ins)):
            for j in range(2):
                self.copy(a, 1 + j, (*self.chips[j], self.c), self.me).wait_recv()
            self.relayed(a).start()
            for j in range(2):
                self.passed(a, j).start()

    def finish(self):
        n = len(self.ins)
        for a in range(n):
            self.copy(a, 3, (*self.chips[2], self.c), self.me).wait_recv()
            self.passed(a, 2).start()
        for a in range(n):
            self.copy(a, 0, self.sib, self.me).wait_recv()
            for j, chip in enumerate(self.chips):
                self.copy(a, 4 + j, (*chip, 1 - self.c), self.me).wait_recv()
        for a in range(n):
            for cp in self.first(a) + [self.relayed(a)] + [self.passed(a, j) for j in range(N_CHIP - 1)]:
                cp.wait_send()
            self.own(a).wait()

    @staticmethod
    def sems(n):
        return [pltpu.SemaphoreType.DMA((n, N_DEV - 1)), pltpu.SemaphoreType.DMA((n, N_DEV - 1)),
                pltpu.SemaphoreType.DMA((n,))]


def _gather_two_level(arrs, name):
    n = len(arrs)

    def body(*refs):
        g = _Gather(refs[:n], refs[n:2 * n], *refs[2 * n:])
        g.start()
        g.relay()
        g.finish()

    any_spec = pl.BlockSpec(memory_space=pl.ANY)
    return pl.pallas_call(
        body, name=name, out_shape=[_sds((N_DEV,) + a.shape, a.dtype) for a in arrs],
        in_specs=[any_spec] * n, out_specs=[any_spec] * n, scratch_shapes=_Gather.sems(n),
        compiler_params=pltpu.CompilerParams(has_side_effects=True),
    )(*arrs)


def _swap_halves(small, grads, name):
    n = len(grads)

    def body(*refs):
        small_ref = refs[0]
        ins = refs[1:1 + n]
        small_out = refs[1 + n]
        got = refs[2 + n:2 + 2 * n]
        s_send, s_recv, g_send, g_recv, loc_sem = refs[2 + 2 * n:]
        x, y, c, sib, _ = _mesh_place()
        me = 4 * x + 2 * y + c
        sends, recvs = [], []
        for j in range(1, N_DEV):
            px = 1 - x if (j >> 2) & 1 else x
            py = 1 - y if (j >> 1) & 1 else y
            pc = 1 - c if j & 1 else c
            cp = pltpu.make_async_remote_copy(
                src_ref=small_ref, dst_ref=small_out.at[me], send_sem=s_send.at[j - 1], recv_sem=s_recv.at[j - 1],
                device_id=(px, py, pc), device_id_type=MESH_ID)
            cp.start()
            sends.append(cp)
            recvs.append(pltpu.make_async_remote_copy(
                src_ref=small_ref, dst_ref=small_out.at[4 * px + 2 * py + pc], send_sem=s_send.at[j - 1],
                recv_sem=s_recv.at[j - 1], device_id=(px, py, pc), device_id_type=MESH_ID))
        own = pltpu.make_async_copy(small_ref, small_out.at[me], loc_sem)
        own.start()
        for a in range(n):
            for chip in range(N_CHIP):
                cp = pltpu.make_async_remote_copy(
                    src_ref=ins[a].at[2 * chip + 1 - c], dst_ref=got[a].at[chip], send_sem=g_send.at[a, chip],
                    recv_sem=g_recv.at[a, chip], device_id=sib, device_id_type=MESH_ID)
                cp.start()
                sends.append(cp)
                recvs.append(cp)
        for cp in sends:
            cp.wait_send()
        for cp in recvs:
            cp.wait_recv()
        own.wait()

    half = [_sds((N_CHIP,) + g.shape[1:], g.dtype) for g in grads]
    any_spec = pl.BlockSpec(memory_space=pl.ANY)
    res = pl.pallas_call(
        body, name=name, out_shape=[_sds((N_DEV,) + small.shape, small.dtype)] + half,
        in_specs=[any_spec] * (1 + n), out_specs=[any_spec] * (1 + n),
        scratch_shapes=[pltpu.SemaphoreType.DMA((N_DEV - 1,)), pltpu.SemaphoreType.DMA((N_DEV - 1,)),
                        pltpu.SemaphoreType.DMA((n, N_CHIP)), pltpu.SemaphoreType.DMA((n, N_CHIP)),
                        pltpu.SemaphoreType.DMA],
        compiler_params=pltpu.CompilerParams(has_side_effects=True),
    )(small, *grads)
    return res[0], res[1:]


def _pair_sums(core, mine, got):
    n = len(got)

    def body(core_ref, *refs):
        for a in range(n):
            refs[2 * n + a][...] = (refs[a][...].astype(F32) + refs[n + a][...].astype(F32)).astype(BF16)

    half = lambda g: (1, g.shape[1] // 2, g.shape[2])
    own = [pl.BlockSpec(half(g), lambda i, j, core_ref: (2 * i + core_ref[0], j, 0)) for g in got]
    slot = [pl.BlockSpec(half(g), lambda i, j, core_ref: (i, j, 0)) for g in got]
    return pl.pallas_call(
        body, name="pair_sums", out_shape=[_sds(g.shape, BF16) for g in got],
        grid_spec=pltpu.PrefetchScalarGridSpec(
            num_scalar_prefetch=1, grid=(N_CHIP, 2), in_specs=own + slot, out_specs=slot),
        compiler_params=_params(("parallel", "parallel")),
    )(core, *mine, *got)


class _ChipExchange:
    def __init__(self, ins, outs, send_sems, recv_sems, loc_sems):
        self.ins, self.outs, self.send_sems, self.recv_sems, self.loc_sems = ins, outs, send_sems, recv_sems, loc_sems
        self.x, self.y, self.c, _, self.chips = _mesh_place()
        self.mine = 2 * self.x + self.y

    def own(self, a):
        return pltpu.make_async_copy(self.ins[a].at[self.mine], self.outs[a].at[self.mine], self.loc_sems.at[a])

    def copy(self, a, j, lands_in):
        chip = self.chips[j]
        return pltpu.make_async_remote_copy(
            src_ref=self.ins[a].at[2 * chip[0] + chip[1]], dst_ref=self.outs[a].at[lands_in],
            send_sem=self.send_sems.at[a, j], recv_sem=self.recv_sems.at[a, j], device_id=(*chip, self.c),
            device_id_type=MESH_ID)

    def start(self):
        for a in range(len(self.ins)):
            self.own(a).start()
            for j in range(N_CHIP - 1):
                self.copy(a, j, self.mine).start()

    def finish(self):
        for a in range(len(self.ins)):
            for j, chip in enumerate(self.chips):
                self.copy(a, j, self.mine).wait_send()
                self.copy(a, j, 2 * chip[0] + chip[1]).wait_recv()
            self.own(a).wait()

    @staticmethod
    def sems(n):
        return [pltpu.SemaphoreType.DMA((n, N_CHIP - 1)), pltpu.SemaphoreType.DMA((n, N_CHIP - 1)),
                pltpu.SemaphoreType.DMA((n,))]


def _inproj(x, ln_g, w_main, w_ab, late):
    s, d = x.shape
    n = w_main.shape[1]
    tm = _tile(s, (256, 128))
    tn = _tile(n, (1664, 512, 128))
    nl = len(late)
    ni = s // tm

    def body(*refs):
        x_ref, g_ref, w_ref, wab_ref = refs[:4]
        proj_ref, ab_ref, ht_ref = refs[4 + nl:7 + nl]
        gather = _Gather(refs[4:4 + nl], refs[7 + nl:7 + 2 * nl], *refs[7 + 2 * nl:])
        step = pl.program_id(0)

        @pl.when(step == 0)
        def _():
            gather.start()

        xv = x_ref[...]
        r = lax.rsqrt(jnp.mean(xv * xv, axis=-1, keepdims=True) + EPS)
        hf = xv * r * g_ref[...]
        h = hf.astype(BF16)
        ht_ref[...] = hf.T.astype(BF16)
        ab_ref[...] = jnp.dot(h, wab_ref[...], preferred_element_type=F32)
        for c0 in range(0, n, tn):
            proj_ref[:, c0:c0 + tn] = jnp.dot(h, w_ref[:, c0:c0 + tn], preferred_element_type=F32)

        @pl.when(step == ni // 2)
        def _():
            gather.relay()

        @pl.when(step == ni - 1)
        def _():
            gather.finish()

    any_spec = pl.BlockSpec(memory_space=pl.ANY)
    once = lambda a: pl.BlockSpec(a.shape, lambda i: (0, 0), pipeline_mode=pl.Buffered(1))
    res = pl.pallas_call(
        body, name="inproj", grid=(ni,),
        in_specs=[pl.BlockSpec((tm, d), lambda i: (i, 0)), pl.BlockSpec((1, d), lambda i: (0, 0)), once(w_main),
                  once(w_ab)] + [any_spec] * nl,
        out_specs=[pl.BlockSpec((tm, n), lambda i: (i, 0)), pl.BlockSpec((tm, LANE), lambda i: (i, 0)),
                   pl.BlockSpec((d, tm), lambda i: (0, i))] + [any_spec] * nl,
        out_shape=[_sds((s, n)), _sds((s, LANE)), _sds((d, s), BF16)]
        + [_sds((N_DEV,) + a.shape, a.dtype) for a in late],
        scratch_shapes=_Gather.sems(nl),
        compiler_params=_params(("arbitrary",)),
    )(x, ln_g, w_main, w_ab, *late)
    return res[0], res[1], res[2], res[3:]


def _matmul_acc(a, b, name):
    m, k = a.shape
    n = b.shape[1]
    tm = _tile(m, (1024, 512, 256, 128))
    tn = _tile(n, (1024, 512, 256, 128))
    tk = _tile(k, (4096, 2048, 1024, 512, 256, 128))
    nk = k // tk

    def body(a_ref, b_ref, o_ref, acc):
        @pl.when(pl.program_id(2) == 0)
        def _():
            acc[...] = jnp.zeros_like(acc)

        acc[...] += jnp.dot(a_ref[...], b_ref[...], preferred_element_type=F32)

        @pl.when(pl.program_id(2) == nk - 1)
        def _():
            o_ref[...] = acc[...].astype(BF16)

    return pl.pallas_call(
        body, name=name, grid=(m // tm, n // tn, nk),
        in_specs=[pl.BlockSpec((tm, tk), lambda i, j, l: (i, l)), pl.BlockSpec((tk, tn), lambda i, j, l: (l, j))],
        out_specs=pl.BlockSpec((tm, tn), lambda i, j, l: (i, j)),
        out_shape=_sds((m, n), BF16), scratch_shapes=[pltpu.VMEM((tm, tn), F32)],
        compiler_params=_params(("parallel", "parallel", "arbitrary")),
    )(a, b)


def _matmul_tn(a, b, name):
    k, m = a.shape
    n = b.shape[1]
    tm = _tile(m, (1024, 512, 256, 128))
    tn = _tile(n, (1024, 512, 256, 128))
    tk = _tile(k, (4096, 2048, 1024, 512, 256, 128))
    nk = k // tk

    def body(a_ref, b_ref, o_ref, acc):
        @pl.when(pl.program_id(2) == 0)
        def _():
            acc[...] = jnp.zeros_like(acc)

        acc[...] += _mm_tn(a_ref[...], b_ref[...])

        @pl.when(pl.program_id(2) == nk - 1)
        def _():
            o_ref[...] = acc[...].astype(BF16)

    return pl.pallas_call(
        body, name=name, grid=(m // tm, n // tn, nk),
        in_specs=[pl.BlockSpec((tk, tm), lambda i, j, l: (l, i)), pl.BlockSpec((tk, tn), lambda i, j, l: (l, j))],
        out_specs=pl.BlockSpec((tm, tn), lambda i, j, l: (i, j)),
        out_shape=_sds((m, n), BF16), scratch_shapes=[pltpu.VMEM((tm, tn), F32)],
        compiler_params=_params(("parallel", "parallel", "arbitrary")),
    )(a, b)


def _dh_rms(pieces, w_rows, wab_rows, x, dx2, ln_g, chip_sums):
    s, d = x.shape
    npc = len(pieces)
    nx = len(chip_sums)
    tm = _tile(s, (256, 128))
    ni = s // tm
    widths = [p.shape[1] for p in pieces[:-1]]
    offs = [sum(widths[:p]) for p in range(npc - 1)]
    nw = len(w_rows)
    nin = npc + 2 * nw + 3

    def body(*refs):
        p_refs = refs[:npc]
        w_refs = refs[npc:npc + nw]
        wab_refs = refs[npc + nw:npc + 2 * nw]
        x_ref, dx2_ref, g_ref = refs[npc + 2 * nw:nin]
        gx_ref, dg_ref = refs[nin + nx:nin + nx + 2]
        exch = _ChipExchange(refs[nin:nin + nx], refs[nin + nx + 2:nin + 2 * nx + 2], *refs[nin + 2 * nx + 2:])
        step = pl.program_id(0)

        @pl.when(step == 0)
        def _():
            dg_ref[...] = jnp.zeros_like(dg_ref)
            exch.start()

        cols = []
        for w_ref, wab_ref in zip(w_refs, wab_refs):
            part = _mm_nt(p_refs[npc - 1][...], wab_ref[...])
            for p in range(npc - 1):
                part += _mm_nt(p_refs[p][...], w_ref[:, offs[p]:offs[p] + widths[p]])
            cols.append(part)
        dhv = jnp.concatenate(cols, axis=1)
        xv = x_ref[...]
        r = lax.rsqrt(jnp.mean(xv * xv, axis=-1, keepdims=True) + EPS)
        xhat = xv * r
        dg_ref[...] += _colsum(dhv * xhat)
        dxh = dhv * g_ref[...]
        gx_ref[...] = dx2_ref[...] + r * (dxh - xhat * jnp.mean(dxh * xhat, axis=-1, keepdims=True))

        @pl.when(step == ni - 1)
        def _():
            exch.finish()

    any_spec = pl.BlockSpec(memory_space=pl.ANY)
    row = pl.BlockSpec((tm, d), lambda i: (i, 0))
    vec = pl.BlockSpec((1, d), lambda i: (0, 0))
    once = lambda a: pl.BlockSpec(a.shape, lambda i: (0, 0), pipeline_mode=pl.Buffered(1))
    in_specs = [pl.BlockSpec((tm, p.shape[1]), lambda i: (i, 0)) for p in pieces]
    in_specs += [once(w) for w in w_rows] + [once(w) for w in wab_rows] + [row, row, vec] + [any_spec] * nx
    res = pl.pallas_call(
        body, name="dh_rms", grid=(ni,), in_specs=in_specs,
        out_specs=[row, vec] + [any_spec] * nx,
        out_shape=[_sds((s, d)), _sds((1, d))] + [_sds(p.shape, p.dtype) for p in chip_sums],
        scratch_shapes=_ChipExchange.sems(nx),
        compiler_params=_params(("arbitrary",)),
    )(*pieces, *w_rows, *wab_rows, x, dx2, ln_g, *chip_sums)
    return res[0], res[1], res[2:]


def _final(x, tgt, out_b, out_a, out_c, w_out, final_g):
    s, d = x.shape
    tm = _tile(s, (256, 128))

    def body(x_ref, t_ref, b_ref, a_ref, c_ref, w_ref, g_ref, dx2_ref, dx2b_ref, dm_ref, loss_ref, dg_ref):
        @pl.when(pl.program_id(0) == 0)
        def _():
            loss_ref[...] = jnp.zeros_like(loss_ref)
            dg_ref[...] = jnp.zeros_like(dg_ref)

        mixed = jnp.concatenate([b_ref[...], a_ref[...], c_ref[...]], axis=1)
        x2 = x_ref[...] + jnp.dot(mixed, w_ref[...], preferred_element_type=F32)
        r = lax.rsqrt(jnp.mean(x2 * x2, axis=-1, keepdims=True) + EPS)
        xhat = x2 * r
        g = g_ref[...]
        err = xhat * g - t_ref[...]
        tok = 0.5 * jnp.mean(err * err, axis=-1, keepdims=True)
        loss_ref[...] += jnp.broadcast_to(_colsum(tok), loss_ref.shape)
        dy = err * (1.0 / d)
        dg_ref[...] += _colsum(dy * xhat)
        dxh = dy * g
        dx2 = r * (dxh - xhat * jnp.mean(dxh * xhat, axis=-1, keepdims=True))
        dx2_ref[...] = dx2
        dx2b = dx2.astype(BF16)
        dx2b_ref[...] = dx2b
        dm_ref[...] = _mm_nt(dx2b, w_ref[...])

    row = pl.BlockSpec((tm, d), lambda i: (i, 0))
    vec = pl.BlockSpec((1, d), lambda i: (0, 0))
    return pl.pallas_call(
        body, name="final", grid=(s // tm,),
        in_specs=[row, row, pl.BlockSpec((tm, DN_W), lambda i: (i, 0)), pl.BlockSpec((tm, GMLP_W), lambda i: (i, 0)),
                  pl.BlockSpec((tm, XA_W), lambda i: (i, 0)), pl.BlockSpec((MIX_W, d), lambda i: (0, 0)), vec],
        out_specs=[row, row, pl.BlockSpec((tm, MIX_W), lambda i: (i, 0)), pl.BlockSpec((1, LANE), lambda i: (0, 0)), vec],
        out_shape=[_sds((s, d)), _sds((s, d), BF16), _sds((s, MIX_W)), _sds((1, LANE)), _sds((1, d))],
        compiler_params=_params(("arbitrary",)),
    )(x, tgt, out_b, out_a, out_c, w_out, final_g)


GU_BLK = (4 * DN_W) // GMLP_W


def _gmlp_norm(gv, lng, lnb):
    va = _gelu(gv)
    mu = jnp.mean(va, axis=-1, keepdims=True)
    xc = va - mu
    rstd = lax.rsqrt(jnp.mean(xc * xc, axis=-1, keepdims=True) + EPS)
    vhat = xc * rstd
    return vhat, rstd, vhat * lng + lnb


def _gmlp_fwd(proj, lng, lnb, ws, bs_t):
    s = proj.shape[0]
    tm = _tile(s, (512, 256, 128))

    def body(u_ref, v_ref, z_ref, lng_ref, lnb_ref, ws_ref, bst_ref, o_ref):
        _, _, vn = _gmlp_norm(v_ref[...], lng_ref[...], lnb_ref[...])
        tri = _iota2((GMLP_T, GMLP_T), 0) >= _iota2((GMLP_T, GMLP_T), 1)
        for g in range(GMLP_G):
            cs = slice(g * HEAD, (g + 1) * HEAD)
            w = jnp.where(tri, ws_ref[g], 0.0).astype(BF16)
            b = bst_ref[:, g:g + 1]
            for c in range(tm // GMLP_T):
                rs = slice(c * GMLP_T, (c + 1) * GMLP_T)
                sg = _mm(w, vn[rs, cs]) + b
                o_ref[rs, cs] = (_gelu(u_ref[rs, cs]) * sg * _silu(z_ref[rs, cs])).astype(BF16)

    col = lambda k: pl.BlockSpec((tm, GMLP_W), lambda i: (i, GU_BLK + k))
    vec = pl.BlockSpec((1, GMLP_W), lambda i: (0, 0))
    return pl.pallas_call(
        body, name="gmlp_fwd", grid=(s // tm,),
        in_specs=[col(0), col(1), col(2), vec, vec, pl.BlockSpec((GMLP_G, GMLP_T, GMLP_T), lambda i: (0, 0, 0)),
                  pl.BlockSpec((GMLP_T, GMLP_G), lambda i: (0, 0))],
        out_specs=pl.BlockSpec((tm, GMLP_W), lambda i: (i, 0)), out_shape=_sds((s, GMLP_W), BF16),
        compiler_params=_params(("parallel",)),
    )(proj, proj, proj, lng, lnb, ws, bs_t)


def _gmlp_bwd(proj, dmixed, lng, lnb, ws, bs_t):
    s = proj.shape[0]
    tm = _tile(s, (512, 256, 128))

    def body(u_ref, v_ref, z_ref, d_ref, lng_ref, lnb_ref, ws_ref, bst_ref,
             dp_ref, dws_ref, dbst_ref, dlng_ref, dlnb_ref, dvn):
        @pl.when(pl.program_id(0) == 0)
        def _():
            dws_ref[...] = jnp.zeros_like(dws_ref)
            dbst_ref[...] = jnp.zeros_like(dbst_ref)
            dlng_ref[...] = jnp.zeros_like(dlng_ref)
            dlnb_ref[...] = jnp.zeros_like(dlnb_ref)

        gv = v_ref[...]
        lng_v = lng_ref[...]
        vhat, rstd, vn = _gmlp_norm(gv, lng_v, lnb_ref[...])
        tri = _iota2((GMLP_T, GMLP_T), 0) >= _iota2((GMLP_T, GMLP_T), 1)
        for g in range(GMLP_G):
            cs = slice(g * HEAD, (g + 1) * HEAD)
            w = jnp.where(tri, ws_ref[g], 0.0).astype(BF16)
            b = bst_ref[:, g:g + 1]
            dw_acc = jnp.zeros((GMLP_T, GMLP_T), F32)
            db_acc = jnp.zeros((GMLP_T, 1), F32)
            for c in range(tm // GMLP_T):
                rs = slice(c * GMLP_T, (c + 1) * GMLP_T)
                vn_b = vn[rs, cs]
                sg = _mm(w, vn_b) + b
                gu = u_ref[rs, cs]
                gz = z_ref[rs, cs]
                da = d_ref[rs, cs]
                uact = _gelu(gu)
                sz = _silu(gz)
                ds = da * uact * sz
                dp_ref[rs, cs] = (da * sg * sz * _gelu_grad(gu)).astype(BF16)
                dp_ref[rs, 2 * GMLP_W + g * HEAD:2 * GMLP_W + (g + 1) * HEAD] = (da * uact * sg * _silu_grad(gz)).astype(BF16)
                dw_acc += _mm_nt(ds, vn_b)
                db_acc += _rowsum(ds)
                dvn[rs, cs] = _mm_tn(w, ds)
            dws_ref[g] += jnp.where(tri, dw_acc, 0.0)
            dbst_ref[:, g:g + 1] += db_acc
        dvn_v = dvn[...]
        dlng_ref[...] += _colsum(dvn_v * vhat)
        dlnb_ref[...] += _colsum(dvn_v)
        dvh = dvn_v * lng_v
        dva = rstd * (dvh - jnp.mean(dvh, axis=-1, keepdims=True) - vhat * jnp.mean(dvh * vhat, axis=-1, keepdims=True))
        dp_ref[:, GMLP_W:2 * GMLP_W] = (dva * _gelu_grad(gv)).astype(BF16)

    col = lambda k: pl.BlockSpec((tm, GMLP_W), lambda i: (i, GU_BLK + k))
    vec = pl.BlockSpec((1, GMLP_W), lambda i: (0, 0))
    wsp = pl.BlockSpec((GMLP_G, GMLP_T, GMLP_T), lambda i: (0, 0, 0))
    bsp = pl.BlockSpec((GMLP_T, GMLP_G), lambda i: (0, 0))
    return pl.pallas_call(
        body, name="gmlp_bwd", grid=(s // tm,),
        in_specs=[col(0), col(1), col(2), pl.BlockSpec((tm, GMLP_W), lambda i: (i, DN_W // GMLP_W)), vec, vec, wsp, bsp],
        out_specs=[pl.BlockSpec((tm, 3 * GMLP_W), lambda i: (i, 0)), wsp, bsp, vec, vec],
        out_shape=[_sds((s, 3 * GMLP_W), BF16), _sds((GMLP_G, GMLP_T, GMLP_T)), _sds((GMLP_T, GMLP_G)),
                   _sds((1, GMLP_W)), _sds((1, GMLP_W))],
        scratch_shapes=[pltpu.VMEM((tm, GMLP_W), F32)],
        compiler_params=_params(("arbitrary",)),
    )(proj, proj, proj, dmixed, lng, lnb, ws, bs_t)


CQ_BLK = (4 * DN_W + 3 * GMLP_W) // XA_W


def _memkv_fwd(mem, g, w_kv):
    nm, d = mem.shape

    def body(m_ref, g_ref, w_ref, kv_ref):
        mv = m_ref[...]
        r = lax.rsqrt(jnp.mean(mv * mv, axis=-1, keepdims=True) + EPS)
        kv_ref[...] = _mm(mv * r * g_ref[...], w_ref[...])

    return pl.pallas_call(body, name="memkv_fwd", out_shape=_sds((nm, 2 * XA_W)), compiler_params=_params())(mem, g, w_kv)


def _memkv_bwd(mem, g, w_kv, dkv):
    nm, d = mem.shape

    def body(m_ref, g_ref, w_ref, dkv_ref, dw_ref, dg_ref):
        mv = m_ref[...]
        r = lax.rsqrt(jnp.mean(mv * mv, axis=-1, keepdims=True) + EPS)
        xhat = mv * r
        dkv_v = dkv_ref[...]
        dw_ref[...] = _mm_tn(xhat * g_ref[...], dkv_v)
        dg_ref[...] = _colsum(_mm_nt(dkv_v, w_ref[...]) * xhat)

    return pl.pallas_call(body, name="memkv_bwd", out_shape=[_sds((d, 2 * XA_W)), _sds((1, d))],
                          compiler_params=_params())(mem, g, w_kv, dkv)


def _xattn_probs(q, mk):
    sc = _mm_nt(q, mk) * (HEAD ** -0.5)
    e = jnp.exp(sc - jnp.max(sc, axis=-1, keepdims=True))
    return e / _rowsum(e)


def _xattn_fwd(proj, mkv):
    s = proj.shape[0]
    nm = mkv.shape[0]
    tm = _tile(s, (512, 256, 128))

    def body(q_ref, z_ref, kv_ref, o_ref):
        for h in range(XA_H):
            cs = slice(h * HEAD, (h + 1) * HEAD)
            p = _xattn_probs(q_ref[:, cs], kv_ref[:, cs])
            ctx = _mm(p, kv_ref[:, XA_W + h * HEAD:XA_W + (h + 1) * HEAD])
            o_ref[:, cs] = (ctx * _silu(z_ref[:, cs])).astype(BF16)

    col = lambda k: pl.BlockSpec((tm, XA_W), lambda i: (i, CQ_BLK + k))
    return pl.pallas_call(
        body, name="xattn_fwd", grid=(s // tm,),
        in_specs=[col(0), col(1), pl.BlockSpec((nm, 2 * XA_W), lambda i: (0, 0))],
        out_specs=pl.BlockSpec((tm, XA_W), lambda i: (i, 0)), out_shape=_sds((s, XA_W), BF16),
        compiler_params=_params(("parallel",)),
    )(proj, proj, mkv)


def _xattn_bwd(proj, dmixed, mkv):
    s = proj.shape[0]
    nm = mkv.shape[0]
    tm = _tile(s, (512, 256, 128))

    def body(q_ref, z_ref, d_ref, kv_ref, dp_ref, dkv_ref):
        @pl.when(pl.program_id(0) == 0)
        def _():
            dkv_ref[...] = jnp.zeros_like(dkv_ref)

        for h in range(XA_H):
            cs = slice(h * HEAD, (h + 1) * HEAD)
            vs = slice(XA_W + h * HEAD, XA_W + (h + 1) * HEAD)
            q = q_ref[:, cs]
            z = z_ref[:, cs]
            mk = kv_ref[:, cs]
            mv = kv_ref[:, vs]
            p = _xattn_probs(q, mk)
            ctx = _mm(p, mv)
            dc = d_ref[:, cs]
            dctx = dc * _silu(z)
            dp_ref[:, vs] = (dc * ctx * _silu_grad(z)).astype(BF16)
            dp = _mm_nt(dctx, mv)
            dkv_ref[:, vs] += _mm_tn(p, dctx)
            ds = p * (dp - _rowsum(dp * p)) * (HEAD ** -0.5)
            dp_ref[:, cs] = _mm(ds, mk).astype(BF16)
            dkv_ref[:, cs] += _mm_tn(ds, q)

    col = lambda k: pl.BlockSpec((tm, XA_W), lambda i: (i, CQ_BLK + k))
    kvs = pl.BlockSpec((nm, 2 * XA_W), lambda i: (0, 0))
    return pl.pallas_call(
        body, name="xattn_bwd", grid=(s // tm,),
        in_specs=[col(0), col(1), pl.BlockSpec((tm, XA_W), lambda i: (i, (DN_W + GMLP_W) // XA_W)), kvs],
        out_specs=[pl.BlockSpec((tm, 2 * XA_W), lambda i: (i, 0)), kvs],
        out_shape=[_sds((s, 2 * XA_W), BF16), _sds((nm, 2 * XA_W))],
        compiler_params=_params(("arbitrary",)),
    )(proj, proj, dmixed, mkv)


def _softplus(x):
    return jnp.maximum(x, 0.0) + jnp.log1p(jnp.exp(-jnp.abs(x)))


def _dn_pre(proj, ab, conv_w, alog_row, dt_row):
    s = proj.shape[0]
    tm = _tile(s, (256, 128))
    w3 = 3 * DN_W

    def body(x_ref, halo_ref, ab_ref, cw_ref, al_ref, dt_ref, q_ref, k_ref, v_ref, gb_ref, gbt_ref, yc_ref):
        i = pl.program_id(0)
        for blk in range(w3 // HEAD):
            cs = slice(blk * HEAD, (blk + 1) * HEAD)
            xv = x_ref[:, cs]
            cat = jnp.concatenate([jnp.where(i > 0, halo_ref[:, cs], 0.0), xv[0:HALO]], axis=0)
            yc = cw_ref[DN_K - 1:DN_K, cs] * xv
            top = cw_ref[DN_K - 1:DN_K, cs] * xv[0:HALO]
            for t in range(DN_K - 1):
                back = DN_K - 1 - t
                yc += cw_ref[t:t + 1, cs] * pltpu.roll(xv, back, 0)
                top += cw_ref[t:t + 1, cs] * pltpu.roll(cat, back, 0)[HALO:2 * HALO]
            yc = jnp.concatenate([top, yc[HALO:tm]], axis=0)
            yc_ref[:, cs] = yc
            act = _silu(yc)
            hs = slice((blk % DN_H) * HEAD, (blk % DN_H + 1) * HEAD)
            if blk < DN_H:
                q_ref[:, hs] = act * (lax.rsqrt(_rowsum(act * act) + EPS) * (HEAD ** -0.5))
            elif blk < 2 * DN_H:
                k_ref[:, hs] = act * lax.rsqrt(_rowsum(act * act) + EPS)
            else:
                v_ref[:, hs] = act
        abv = ab_ref[...]
        lane = _iota2((tm, LANE), 1)
        g = jnp.where(lane < DN_H, -jnp.exp(al_ref[...]) * _softplus(abv + dt_ref[...]), 0.0)
        gc = _mm_hi(_chunk_tri(tm, False), g)
        gbv = jnp.where(lane < DN_H, gc, jnp.where(lane < 2 * DN_H, jax.nn.sigmoid(abv), 0.0))
        gb_ref[...] = gbv
        for c in range(tm // CH):
            gbt_ref[c] = gbv[c * CH:(c + 1) * CH, :].T[0:2 * DN_H, :]

    hb = tm // HALO
    row = lambda w: pl.BlockSpec((tm, w), lambda i: (i, 0))
    vec = pl.BlockSpec((1, LANE), lambda i: (0, 0))
    return pl.pallas_call(
        body, name="dn_pre", grid=(s // tm,),
        in_specs=[row(w3), pl.BlockSpec((HALO, w3), lambda i: (jnp.maximum(i * hb - 1, 0), 0)), row(LANE),
                  pl.BlockSpec((DN_K, w3), lambda i: (0, 0)), vec, vec],
        out_specs=[row(DN_W), row(DN_W), row(DN_W), row(LANE), pl.BlockSpec((tm // CH, 2 * DN_H, CH), lambda i: (i, 0, 0)),
                   row(w3)],
        out_shape=[_sds((s, DN_W)), _sds((s, DN_W)), _sds((s, DN_W)), _sds((s, LANE)),
                   _sds((s // CH, 2 * DN_H, CH)), _sds((s, w3))],
        compiler_params=_params(("parallel",)),
    )(proj, proj, ab, conv_w, alog_row, dt_row)


HEADS = tuple(range(DN_H))


def _hcols(h):
    return slice(h * HEAD, (h + 1) * HEAD)


def _chunk_scalings(k, v, gbv, gbt, h):
    gc = jnp.broadcast_to(gbv[:, h:h + 1], (CH, HEAD))
    beta = jnp.broadcast_to(gbv[:, DN_H + h:DN_H + h + 1], (CH, HEAD))
    gr = gbt[h:h + 1, :]
    ii = _iota2((CH, CH), 0)
    jj = _iota2((CH, CH), 1)
    dec = jnp.exp(jnp.where(ii >= jj, gc[:, 0:CH] - gr, -1e30))
    eg = jnp.exp(gc)
    gl = gr[:, CH - 1:CH]
    kb = k * beta
    return dict(beta=beta, dec=dec, eg=eg, gl=gl, ekd=jnp.exp(gl - gc), kb=kb, vb=v * beta, kbe=kb * eg)


def _chunk_scores(m, q, k):
    kq = _mm_nt(jnp.concatenate([m["kb"], q], axis=0), k)
    strict = _iota2((CH, CH), 0) > _iota2((CH, CH), 1)
    return jnp.where(strict, kq[0:CH] * m["dec"], 0.0), kq[CH:2 * CH] * m["dec"]


def _scan_cpb(s):
    return 8 if (s // CH) % 8 == 0 else 1


def _dn_fwd(q, k, v, gb, gbt, proj, norm_g):
    s = q.shape[0]
    cpb = _scan_cpb(s)
    tb = cpb * CH
    nblk = s // tb

    def body(q_ref, k_ref, v_ref, gb_ref, gbt_ref, z_ref, ng_ref,
             w_ref, qg_ref, kd_ref, t_ref, ai_ref, egl_ref, o_ref, vn_ref, st_ref, ob_ref, state):
        @pl.when(pl.program_id(0) == 0)
        def _():
            state[...] = jnp.zeros_like(state)

        ng = ng_ref[...]
        eye = jnp.where(_iota2((CH, CH), 0) == _iota2((CH, CH), 1), 1.0, 0.0).astype(F32)

        def chunk(c, carry):
            r0 = pl.multiple_of(c * CH, CH)
            rows = pl.ds(r0, CH)
            gbv = gb_ref[rows, :]
            gbt_v = gbt_ref[c]
            qs = [q_ref[rows, _hcols(h)] for h in HEADS]
            ks = [k_ref[rows, _hcols(h)] for h in HEADS]
            ms = [_chunk_scalings(ks[h], v_ref[rows, _hcols(h)], gbv, gbt_v, h) for h in HEADS]
            qgb = [(qs[h] * ms[h]["eg"]).astype(BF16) for h in HEADS]
            kdb = [(ks[h] * ms[h]["ekd"]).astype(BF16) for h in HEADS]
            egl = [jnp.broadcast_to(jnp.exp(ms[h]["gl"]), (1, LANE)) for h in HEADS]
            for h in HEADS:
                qg_ref[rows, _hcols(h)] = qgb[h]
                kd_ref[rows, _hcols(h)] = kdb[h]
                egl_ref[c, h:h + 1, :] = egl[h]
            sc = [_chunk_scores(ms[h], qs[h], ks[h]) for h in HEADS]
            for h in HEADS:
                ai_ref[h, rows, :] = sc[h][1]
            ts = [eye - sc[h][0] for h in HEADS]
            ps = [_mm_3x(sc[h][0], sc[h][0]) for h in HEADS]
            ts = [ts[h] + _mm_3x(ts[h], ps[h]) for h in HEADS]
            for _ in range(4):
                ps = [_mm(ps[h], ps[h]) for h in HEADS]
                ts = [ts[h] + _mm(ts[h], ps[h]) for h in HEADS]
            uw = [_mm(ts[h], jnp.concatenate([ms[h]["vb"], ms[h]["kbe"]], axis=1)) for h in HEADS]
            wb = [uw[h][:, HEAD:2 * HEAD].astype(BF16) for h in HEADS]
            for h in HEADS:
                t_ref[h, rows, :] = ts[h]
                w_ref[rows, _hcols(h)] = wb[h]
            sts = [state[h] for h in HEADS]
            stb = [sts[h].astype(BF16) for h in HEADS]
            for h in HEADS:
                st_ref[c, h] = stb[h]
            vnb = [(uw[h][:, 0:HEAD] - jnp.dot(wb[h], stb[h], preferred_element_type=F32)).astype(BF16) for h in HEADS]
            for h in HEADS:
                state[h] = sts[h] * egl[h] + _mm_tn(kdb[h], vnb[h])
            os_ = [jnp.dot(qgb[h], stb[h], preferred_element_type=F32) + _mm(sc[h][1], vnb[h]) for h in HEADS]
            for h in HEADS:
                o = os_[h]
                vn_ref[rows, _hcols(h)] = vnb[h]
                o_ref[rows, _hcols(h)] = o
                r = lax.rsqrt(jnp.mean(o * o, axis=-1, keepdims=True) + EPS)
                ob_ref[rows, _hcols(h)] = (o * r * ng * _silu(z_ref[rows, _hcols(h)])).astype(BF16)
            return carry

        lax.fori_loop(0, cpb, chunk, 0, unroll=4)

    row = pl.BlockSpec((tb, DN_W), lambda i: (i, 0))
    sq = pl.BlockSpec((DN_H, tb, CH), lambda i: (0, i, 0))
    return pl.pallas_call(
        body, name="dn_fwd", grid=(nblk,),
        in_specs=[row, row, row, pl.BlockSpec((tb, LANE), lambda i: (i, 0)),
                  pl.BlockSpec((cpb, 2 * DN_H, CH), lambda i: (i, 0, 0)), pl.BlockSpec((tb, DN_W), lambda i: (i, 3)),
                  pl.BlockSpec((1, HEAD), lambda i: (0, 0))],
        out_specs=[row, row, row, sq, sq, pl.BlockSpec((cpb, DN_H, LANE), lambda i: (i, 0, 0)), row, row,
                   pl.BlockSpec((cpb, DN_H, HEAD, HEAD), lambda i: (i, 0, 0, 0)), row],
        out_shape=[_sds((s, DN_W), BF16), _sds((s, DN_W), BF16), _sds((s, DN_W), BF16), _sds((DN_H, s, CH)),
                   _sds((DN_H, s, CH)), _sds((s // CH, DN_H, LANE)), _sds((s, DN_W)), _sds((s, DN_W), BF16),
                   _sds((s // CH, DN_H, HEAD, HEAD), BF16), _sds((s, DN_W), BF16)],
        scratch_shapes=[pltpu.VMEM((DN_H, HEAD, HEAD), F32)],
        compiler_params=_params(("arbitrary",)),
    )(q, k, v, gb, gbt, proj, norm_g)


def _dn_bwd(dmixed, o, proj, norm_g, w, qg, kd, ai, egl, q, k, v, gb, gbt, t, vn, st):
    s = o.shape[0]
    cpb = 4 if (s // CH) % 4 == 0 else 1
    tb = cpb * CH
    nblk = s // tb

    def body(dm_ref, o_ref, z_ref, ng_ref, w_ref, qg_ref, kd_ref, ai_ref, egl_ref,
             q_ref, k_ref, v_ref, gb_ref, gbt_ref, t_ref, vn_ref, st_ref,
             dq_ref, dk_ref, dv_ref, dgb_ref, dz_ref, dng_ref, dstate):
        @pl.when(pl.program_id(0) == 0)
        def _():
            dstate[...] = jnp.zeros_like(dstate)
            dng_ref[...] = jnp.zeros_like(dng_ref)

        ng = ng_ref[...]
        lane = _iota2((CH, LANE), 1)
        last = _iota2((CH, 1), 0) == CH - 1
        strict = _iota2((CH, CH), 0) > _iota2((CH, CH), 1)

        def chunk(cc, carry):
            c = cpb - 1 - cc
            r0 = pl.multiple_of(c * CH, CH)
            rows = pl.ds(r0, CH)
            dng = jnp.zeros((1, HEAD), F32)
            dob = []
            for h in HEADS:
                cs = _hcols(h)
                ov = o_ref[rows, cs]
                z = z_ref[rows, cs]
                db = dm_ref[rows, cs]
                r = lax.rsqrt(jnp.mean(ov * ov, axis=-1, keepdims=True) + EPS)
                ohat = ov * r
                dz_ref[rows, cs] = (db * ohat * ng * _silu_grad(z)).astype(BF16)
                dyn = db * _silu(z)
                dng += _colsum(dyn * ohat)
                doh = dyn * ng
                dob.append((r * (doh - ohat * jnp.mean(doh * ohat, axis=-1, keepdims=True))).astype(BF16))
            dng_ref[...] += dng
            dsn = [dstate[h] for h in HEADS]
            dsb = [dsn[h].astype(BF16) for h in HEADS]
            dvnb = [(_mm_tn(ai_ref[h, rows, :], dob[h])
                     + jnp.dot(kd_ref[rows, _hcols(h)], dsb[h], preferred_element_type=F32)).astype(BF16) for h in HEADS]
            part = [_mm_tn(qg_ref[rows, _hcols(h)], dob[h]) + egl_ref[c, h:h + 1, :] * dsn[h] for h in HEADS]
            for h in HEADS:
                dstate[h] = part[h] - _mm_tn(w_ref[rows, _hcols(h)], dvnb[h])
            gbv = gb_ref[rows, :]
            gbt_v = gbt_ref[c]
            qs = [q_ref[rows, _hcols(h)] for h in HEADS]
            ks = [k_ref[rows, _hcols(h)] for h in HEADS]
            vs = [v_ref[rows, _hcols(h)] for h in HEADS]
            ms = [_chunk_scalings(ks[h], vs[h], gbv, gbt_v, h) for h in HEADS]
            sts = [st_ref[c, h] for h in HEADS]
            vnb = [vn_ref[rows, _hcols(h)] for h in HEADS]
            tbf = [t_ref[h, rows, :].astype(BF16) for h in HEADS]
            sc = [_chunk_scores(ms[h], qs[h], ks[h]) for h in HEADS]
            xs_ = [_mm_nt(jnp.concatenate([dob[h], dvnb[h]], axis=0), sts[h]) for h in HEADS]
            dai = [_mm_nt(dob[h], vnb[h]) for h in HEADS]
            dkd = [_mm_nt(vnb[h], dsb[h]) for h in HEADS]
            dqg = [xs_[h][0:CH] for h in HEADS]
            duw = [jnp.concatenate([dvnb[h], (-xs_[h][CH:2 * CH]).astype(BF16)], axis=1) for h in HEADS]
            dt = [_mm_nt(duw[h], jnp.concatenate([ms[h]["vb"], ms[h]["kbe"]], axis=1)) for h in HEADS]
            dvk = [_mm_tn(tbf[h], duw[h]) for h in HEADS]
            tdt = [_mm_tn(tbf[h], dt[h]) for h in HEADS]
            da = [jnp.where(strict, -_mm_nt(tdt[h], tbf[h]), 0.0) for h in HEADS]
            dsc = [jnp.concatenate([da[h] * ms[h]["dec"], dai[h] * ms[h]["dec"]], axis=0) for h in HEADS]
            dkq = [_mm(dsc[h], ks[h]) for h in HEADS]
            dk1 = [_mm_tn(dsc[h], jnp.concatenate([ms[h]["kb"], qs[h]], axis=0)) for h in HEADS]
            dgb = jnp.zeros((CH, LANE), F32)
            for h in HEADS:
                m = ms[h]
                eg, ekd, beta = m["eg"], m["ekd"], m["beta"]
                dvb = dvk[h][:, 0:HEAD]
                dkbe = dvk[h][:, HEAD:2 * HEAD]
                kdv = ks[h] * ekd
                dkb = dkq[h][0:CH] + dkbe * eg
                dq_ref[rows, _hcols(h)] = dkq[h][CH:2 * CH] + dqg[h] * eg
                dk_ref[rows, _hcols(h)] = dk1[h] + dkd[h] * ekd + dkb * beta
                dv_ref[rows, _hcols(h)] = dvb * beta
                dkd_kd = dkd[h] * kdv
                dgl = (jnp.exp(m["gl"]) * _rowsum(_colsum(sts[h].astype(F32) * dsb[h].astype(F32)))
                       + _rowsum(_colsum(dkd_kd)))
                mm_ = da[h] * sc[h][0] + dai[h] * sc[h][1]
                dgc = (_rowsum(mm_ - mm_.T) + _rowsum(dqg[h] * qs[h] * eg - dkd_kd + dkbe * m["kbe"])
                       + jnp.where(last, dgl, 0.0))
                dbeta = _rowsum(dkb * ks[h] + dvb * vs[h])
                dgb = jnp.where(lane == h, dgc, jnp.where(lane == DN_H + h, dbeta, dgb))
            dgb_ref[rows, :] = dgb
            return carry

        lax.fori_loop(0, cpb, chunk, 0, unroll=2)

    rev = lambda i: (nblk - 1 - i, 0)
    row = pl.BlockSpec((tb, DN_W), rev)
    vec = pl.BlockSpec((1, HEAD), lambda i: (0, 0))
    sq = pl.BlockSpec((DN_H, tb, CH), lambda i: (0, nblk - 1 - i, 0))
    gbs = pl.BlockSpec((tb, LANE), rev)
    return pl.pallas_call(
        body, name="dn_bwd", grid=(nblk,),
        in_specs=[row, row, pl.BlockSpec((tb, DN_W), lambda i: (nblk - 1 - i, 3)), vec, row, row, row, sq,
                  pl.BlockSpec((cpb, DN_H, LANE), lambda i: (nblk - 1 - i, 0, 0)),
                  row, row, row, gbs, pl.BlockSpec((cpb, 2 * DN_H, CH), lambda i: (nblk - 1 - i, 0, 0)), sq, row,
                  pl.BlockSpec((cpb, DN_H, HEAD, HEAD), lambda i: (nblk - 1 - i, 0, 0, 0))],
        out_specs=[row, row, row, gbs, row, vec],
        out_shape=[_sds((s, DN_W)), _sds((s, DN_W)), _sds((s, DN_W)), _sds((s, LANE)), _sds((s, DN_W), BF16),
                   _sds((1, HEAD))],
        scratch_shapes=[pltpu.VMEM((DN_H, HEAD, HEAD), F32)],
        compiler_params=_params(("arbitrary",)),
    )(dmixed, o, proj, norm_g, w, qg, kd, ai, egl, q, k, v, gb, gbt, t, vn, st)


def _dn_pre_bwd(proj, yc_all, ab, conv_w, alog_row, dt_row, dq, dk, dv, dgb):
    s = proj.shape[0]
    tm = _tile(s, (256, 128))
    w3 = 3 * DN_W
    nblk = s // tm

    def body(x_ref, yc_ref, ab_ref, cw_ref, al_ref, dt_ref, dq_ref, dk_ref, dv_ref, dgb_ref,
             dx_ref, dab_ref, dcw_ref, dal_ref, ddt_ref, carry):
        i = pl.program_id(0)

        @pl.when(i == 0)
        def _():
            carry[...] = jnp.zeros_like(carry)
            dcw_ref[...] = jnp.zeros_like(dcw_ref)
            dal_ref[...] = jnp.zeros_like(dal_ref)
            ddt_ref[...] = jnp.zeros_like(ddt_ref)

        for blk in range(w3 // HEAD):
            cs = slice(blk * HEAD, (blk + 1) * HEAD)
            yc = yc_ref[:, cs]
            sg = jax.nn.sigmoid(yc)
            act = yc * sg
            dact = sg + act * (1.0 - sg)
            if blk < 2 * DN_H:
                d_ref = dq_ref if blk < DN_H else dk_ref
                dn = d_ref[:, (blk % DN_H) * HEAD:(blk % DN_H + 1) * HEAD]
                rn = lax.rsqrt(_rowsum(act * act) + EPS)
                nh = act * rn
                scale = HEAD ** -0.5 if blk < DN_H else 1.0
                dyc = (scale * rn) * (dn - nh * _rowsum(dn * nh)) * dact
            else:
                dyc = dv_ref[:, (blk - 2 * DN_H) * HEAD:(blk - 2 * DN_H + 1) * HEAD] * dact
            xv = x_ref[:, cs]
            cat = jnp.concatenate([dyc[tm - HALO:tm], carry[:, cs]], axis=0)
            dcw_ref[DN_K - 1:DN_K, cs] += _colsum(dyc * xv)
            dx = cw_ref[DN_K - 1:DN_K, cs] * dyc
            for t in range(DN_K - 1):
                ahead = DN_K - 1 - t
                view = jnp.concatenate([pltpu.roll(dyc, tm - ahead, 0)[0:tm - HALO],
                                        pltpu.roll(cat, 2 * HALO - ahead, 0)[0:HALO]], axis=0)
                dcw_ref[t:t + 1, cs] += _colsum(view * xv)
                dx += cw_ref[t:t + 1, cs] * view
            dx_ref[:, cs] = dx.astype(BF16)
            carry[:, cs] = dyc[0:HALO]

        lane = _iota2((tm, LANE), 1)
        dgbv = dgb_ref[...]
        dg = _mm_hi(_chunk_tri(tm, True), jnp.where(lane < DN_H, dgbv, 0.0))
        abv = ab_ref[...]
        xa = abv + dt_ref[...]
        nea = -jnp.exp(al_ref[...])
        d_da = jnp.where(lane < DN_H, dg * nea * jax.nn.sigmoid(xa), 0.0)
        dal_ref[...] += _colsum(jnp.where(lane < DN_H, dg * nea * _softplus(xa), 0.0))
        ddt_ref[...] += _colsum(d_da)
        beta = jax.nn.sigmoid(abv)
        d_db = jnp.where((lane >= DN_H) & (lane < 2 * DN_H), dgbv * beta * (1.0 - beta), 0.0)
        dab_ref[...] = (d_da + d_db).astype(BF16)

    rev = lambda i: (nblk - 1 - i, 0)
    row = lambda w: pl.BlockSpec((tm, w), rev)
    vec = pl.BlockSpec((1, LANE), lambda i: (0, 0))
    cws = pl.BlockSpec((DN_K, w3), lambda i: (0, 0))
    return pl.pallas_call(
        body, name="dn_pre_bwd", grid=(nblk,),
        in_specs=[row(w3), row(w3), row(LANE), cws, vec, vec, row(DN_W), row(DN_W), row(DN_W), row(LANE)],
        out_specs=[row(w3), row(LANE), cws, vec, vec],
        out_shape=[_sds((s, w3), BF16), _sds((s, LANE), BF16), _sds((DN_K, w3)), _sds((1, LANE)), _sds((1, LANE))],
        scratch_shapes=[pltpu.VMEM((HALO, w3), F32)],
        compiler_params=_params(("arbitrary",)),
    )(proj, yc_all, ab, conv_w, alog_row, dt_row, dq, dk, dv, dgb)


def _adam(parts, w, m, v, name):
    r, c = w.shape
    n_parts = parts.shape[0]
    small = n_parts * r * c * 4 <= 4 * 1024 * 1024
    tr = r if small else _tile(r, (128, 64, 32, 16, 8))

    def body(p_ref, w_ref, m_ref, v_ref, g_ref, d_ref, nm_ref, nv_ref):
        g = p_ref[0].astype(F32)
        for k in range(1, n_parts):
            g = g + p_ref[k].astype(F32)
        g_ref[...] = g
        mn = ADAM_B1 * m_ref[...] + (1.0 - ADAM_B1) * g
        vn = ADAM_B2 * v_ref[...] + (1.0 - ADAM_B2) * (g * g)
        m_hat = mn / (1.0 - ADAM_B1 ** ADAM_STEP)
        v_hat = vn / (1.0 - ADAM_B2 ** ADAM_STEP)
        d_ref[...] = -ADAM_LR * (m_hat / (jnp.sqrt(v_hat) + ADAM_EPS) + ADAM_WD * w_ref[...])
        nm_ref[...] = mn
        nv_ref[...] = vn

    blk = pl.BlockSpec((tr, c), lambda i: (i, 0))
    return pl.pallas_call(
        body, name=name, grid=(r // tr,),
        in_specs=[pl.BlockSpec((n_parts, tr, c), lambda i: (0, i, 0)), blk, blk, blk],
        out_specs=[blk, blk, blk, blk], out_shape=[_sds((r, c))] * 4,
        compiler_params=_params(("parallel",)),
    )(parts, w, m, v)


_PACK_ROWS = 8


def _pack(vals):
    tiles = []
    for a in vals:
        flat = a.reshape(-1).astype(F32)
        unit = _PACK_ROWS * LANE
        n = -(-flat.shape[0] // unit) * unit
        tiles.append(jnp.pad(flat, (0, n - flat.shape[0])).reshape(n // LANE, LANE))
    return jnp.concatenate(tiles, axis=0)


def _unpack(packed, shapes):
    out = []
    r0 = 0
    for shp in shapes:
        size = 1
        for dim in shp:
            size *= dim
        unit = _PACK_ROWS * LANE
        rows = -(-size // unit) * _PACK_ROWS
        out.append(packed[r0:r0 + rows].reshape(-1)[:size].reshape(shp))
        r0 += rows
    return out


def _lane_row(vec8):
    return jnp.pad(vec8.reshape(1, -1).astype(F32), ((0, 0), (0, LANE - vec8.size)))


def kernel(x, mem, ln_g, w_in, gmlp_ln_g, gmlp_ln_b, gmlp_ws, gmlp_bs, conv_w, dn_a_log, dn_dt_bias, dn_norm_g, mem_norm_g, w_mem_kv, w_out, final_g, loss_target, m_ln_g, m_w_in, m_gmlp_ln_g, m_gmlp_ln_b, m_gmlp_ws, m_gmlp_bs, m_conv_w, m_dn_a_log, m_dn_dt_bias, m_dn_norm_g, m_mem_norm_g, m_w_mem_kv, m_w_out, m_final_g, v_ln_g, v_w_in, v_gmlp_ln_g, v_gmlp_ln_b, v_gmlp_ws, v_gmlp_bs, v_conv_w, v_dn_a_log, v_dn_dt_bias, v_dn_norm_g, v_mem_norm_g, v_w_mem_kv, v_w_out, v_final_g):
    xs = x[0]
    mems = mem[0]
    tgt = loss_target[0]
    s, d = xs.shape
    shard_w = w_in.shape[2]
    in_w = N_DEV * shard_w
    me = 4 * lax.axis_index("x") + 2 * lax.axis_index("y") + lax.axis_index("c")

    (g_in,) = _gather_two_level([w_in[0].astype(BF16)], "gather_w_in")
    o_g, o_dn, o_ab = 0, 3 * GMLP_W, 3 * GMLP_W + 4 * DN_W
    o_xa = o_ab + 2 * DN_H

    def shard_cols(g, lo, hi):
        out = []
        while lo < hi:
            sh = lo // shard_w
            end = min(hi, (sh + 1) * shard_w)
            out.append(g[sh][:, lo - sh * shard_w:end - sh * shard_w])
            lo = end
        return out

    def own_layout(g):
        main = jnp.concatenate(shard_cols(g, o_dn, o_ab) + shard_cols(g, o_g, o_dn) + shard_cols(g, o_xa, in_w), axis=1)
        return main, jnp.pad(jnp.concatenate(shard_cols(g, o_ab, o_xa), axis=1), ((0, 0), (0, LANE - 2 * DN_H)))

    w_main, w_ab = own_layout(g_in)

    ln_g2 = ln_g.reshape(1, d)
    lng2 = gmlp_ln_g.reshape(1, GMLP_W)
    lnb2 = gmlp_ln_b.reshape(1, GMLP_W)
    ws3 = gmlp_ws[0]
    bs_t = gmlp_bs[0].T
    alog_row = _lane_row(dn_a_log)
    dt_row = _lane_row(dn_dt_bias)
    dn_g2 = dn_norm_g.reshape(1, HEAD)
    mem_g2 = mem_norm_g.reshape(1, d)
    fin_g2 = final_g.reshape(1, d)

    proj, ab, h_t, (g_out, g_kv, g_conv) = _inproj(
        xs, ln_g2, w_main, w_ab, [w_out[0].astype(BF16), w_mem_kv[0].astype(BF16), conv_w[0]])
    wo = g_out.reshape(MIX_W, d)
    wo_perm = jnp.concatenate([wo[GMLP_W:GMLP_W + DN_W], wo[0:GMLP_W], wo[GMLP_W + DN_W:MIX_W]], axis=0)
    w_kv = g_kv.reshape(d, 2 * XA_W)
    conv_full = g_conv.transpose(1, 0, 2).reshape(DN_K, 3 * DN_W)
    out_a = _gmlp_fwd(proj, lng2, lnb2, ws3, bs_t)
    mkv = _memkv_fwd(mems, mem_g2, w_kv)
    out_c = _xattn_fwd(proj, mkv)
    q, k, v, gb, gbt, yc = _dn_pre(proj, ab, conv_full, alog_row, dt_row)
    wk, qg, kd, tmat, ai, egl, o, vn, st, out_b = _dn_fwd(q, k, v, gb, gbt, proj, dn_g2)

    dx2, dx2b, dmixed, loss_acc, d_fin_g = _final(xs, tgt, out_b, out_a, out_c, wo_perm, fin_g2)

    dwo_b = _matmul_tn(out_b, dx2b, "dw_out_b")
    dwo_a = _matmul_tn(out_a, dx2b, "dw_out_a")
    dwo_c = _matmul_tn(out_c, dx2b, "dw_out_c")
    d_w_out = jnp.concatenate([dwo_a, dwo_b, dwo_c], axis=0)

    dp_g, d_ws, d_bst, d_lng, d_lnb = _gmlp_bwd(proj, dmixed, lng2, lnb2, ws3, bs_t)
    dp_x, dmkv = _xattn_bwd(proj, dmixed, mkv)
    d_w_kv, d_mem_g = _memkv_bwd(mems, mem_g2, w_kv, dmkv)
    dq, dk, dv, dgb, dp_dz, d_dn_g = _dn_bwd(dmixed, o, proj, dn_g2, wk, qg, kd, ai, egl, q, k, v, gb, gbt, tmat, vn, st)
    dp_qkv, dp_ab, d_conv, d_alog, d_dt = _dn_pre_bwd(proj, yc, ab, conv_full, alog_row, dt_row, dq, dk, dv, dgb)

    dw_qkv = _matmul_acc(h_t, dp_qkv, "dw_in_qkv")
    dw_dz = _matmul_acc(h_t, dp_dz, "dw_in_dz")
    dw_gm = _matmul_acc(h_t, dp_g, "dw_in_gmlp")
    dw_xa = _matmul_acc(h_t, dp_x, "dw_in_xa")
    dw_ab = _matmul_acc(h_t, dp_ab, "dw_in_ab")
    segs = [(o_g, dw_gm), (o_dn, dw_qkv), (o_dn + 3 * DN_W, dw_dz), (o_ab, dw_ab[:, :2 * DN_H]), (o_xa, dw_xa)]
    shards = []
    for sh in range(N_DEV):
        lo, hi = sh * shard_w, (sh + 1) * shard_w
        parts = [arr[:, max(lo, off) - off:min(hi, off + arr.shape[1]) - off] for off, arr in segs
                 if off < hi and off + arr.shape[1] > lo]
        shards.append(jnp.concatenate(parts, axis=1).astype(BF16))
    send_in = jnp.stack(shards)

    small_shapes = [(1, 1), gmlp_ln_g.shape, gmlp_ln_b.shape, gmlp_ws.shape, gmlp_bs.shape, dn_a_log.shape,
                    dn_dt_bias.shape, dn_norm_g.shape, mem_norm_g.shape, final_g.shape, (DN_K, 3 * DN_W)]
    small_g = _pack([loss_acc[0:1, 0:1], d_lng, d_lnb, d_ws, d_bst.T, d_alog[:, :DN_H], d_dt[:, :DN_H], d_dn_g, d_mem_g,
                     d_fin_g, d_conv])
    zc = jnp.zeros((DN_K, 3 * DN_W), F32)
    z1 = jnp.zeros((1, 1), F32)
    small_w = _pack([z1, gmlp_ln_g, gmlp_ln_b, gmlp_ws, gmlp_bs, dn_a_log, dn_dt_bias, dn_norm_g, mem_norm_g, final_g, zc])
    small_m = _pack([z1, m_gmlp_ln_g, m_gmlp_ln_b, m_gmlp_ws, m_gmlp_bs, m_dn_a_log, m_dn_dt_bias, m_dn_norm_g,
                     m_mem_norm_g, m_final_g, zc])
    small_v = _pack([z1 + 1.0, v_gmlp_ln_g, v_gmlp_ln_b, v_gmlp_ws, v_gmlp_bs, v_dn_a_log, v_dn_dt_bias, v_dn_norm_g,
                     v_mem_norm_g, v_final_g, zc + 1.0])

    send_out = d_w_out.reshape(N_DEV, MIX_W // N_DEV, d).astype(BF16)
    send_kv = d_w_kv.reshape(N_DEV, d // N_DEV, 2 * XA_W).astype(BF16)
    sends = [send_in, send_out, send_kv]
    all_small, got = _swap_halves(small_g, sends, "swap_halves")
    core = lax.axis_index("c").astype(jnp.int32).reshape(1)
    chip_sums = _pair_sums(core, sends, got)
    grad_x, d_ln_g, (r_in, r_out, r_kv) = _dh_rms(
        [dp_qkv, dp_dz, dp_g, dp_x, dp_ab], [w_main], [w_ab], xs, dx2, ln_g2, chip_sums)
    (all_ln_g,) = _gather_two_level([_pack([d_ln_g])], "gather_ln_g")

    g_w_in, dl_w_in, nm_w_in, nv_w_in = _adam(r_in, w_in[0], m_w_in[0], v_w_in[0], "adam_w_in")
    g_w_out, dl_w_out, nm_w_out, nv_w_out = _adam(r_out, w_out[0], m_w_out[0], v_w_out[0], "adam_w_out")
    g_w_kv, dl_w_kv, nm_w_kv, nv_w_kv = _adam(r_kv, w_mem_kv[0], m_w_mem_kv[0], v_w_mem_kv[0], "adam_w_kv")
    sm = [_unpack(t, small_shapes) for t in _adam(all_small, small_w, small_m, small_v, "adam_small")]
    ln_res = [_unpack(t, [ln_g.shape])[0]
              for t in _adam(all_ln_g, _pack([ln_g]), _pack([m_ln_g]), _pack([v_ln_g]), "adam_ln_g")]

    conv_parts = lax.dynamic_slice(all_small, (0, all_small.shape[1] - (DN_K * 3 * DN_W) // LANE, 0),
                                   (N_DEV, (DN_K * 3 * DN_W) // LANE, LANE)).reshape(N_DEV, DN_K, 3 * DN_W)
    cshard = conv_w.shape[2]
    conv_parts = lax.dynamic_slice(conv_parts, (0, 0, me * cshard), (N_DEV, DN_K, cshard))
    cpad = ((0, 0), (0, HALO - DN_K), (0, 0))
    conv_res = _adam(jnp.pad(conv_parts, cpad), jnp.pad(conv_w[0], cpad[1:]), jnp.pad(m_conv_w[0], cpad[1:]),
                     jnp.pad(v_conv_w[0], cpad[1:], constant_values=1.0), "adam_conv")
    g_conv_s, dl_conv, nm_conv, nv_conv = [t[:DN_K][None] for t in conv_res]

    loss = sm[0][0].reshape(())

    def group(idx, big_in, big_conv, big_kv, big_out):
        names = sm[idx][1:]
        return [ln_res[idx], big_in[None], names[0], names[1], names[2], names[3], big_conv, names[4], names[5], names[6],
                names[7], big_kv[None], big_out[None], names[8]]

    grads = group(0, g_w_in, g_conv_s, g_w_kv, g_w_out)
    deltas = group(1, dl_w_in, dl_conv, dl_w_kv, dl_w_out)
    new_m = group(2, nm_w_in, nm_conv, nm_w_kv, nm_w_out)
    new_v = group(3, nv_w_in, nv_conv, nv_w_kv, nv_w_out)
    return (loss, grad_x[None], *grads, *deltas, *new_m, *new_v)
```

```python
import jax
import jax.numpy as jnp
from jax import lax
from jax.experimental import pallas as pl
from jax.experimental.pallas import tpu as pltpu

F32 = jnp.float32
BF16 = jnp.bfloat16
HIGHEST = lax.Precision.HIGHEST
MESH_ID = pl.DeviceIdType.MESH

N_DEV = 8
EPS = 1e-6
GMLP_W = 512
GMLP_G = 4
GMLP_T = 128
DN_W = 1024
DN_H = 8
HEAD = 128
DN_K = 4
CH = 64
XA_W = 512
XA_H = 4
LANE = 128
HALO = 8
MAIN_W = 4 * DN_W + 3 * GMLP_W + 2 * XA_W
MIX_W = DN_W + GMLP_W + XA_W
VMEM_LIMIT = 56 * 1024 * 1024

ADAM_LR = 0.001
ADAM_B1 = 0.9
ADAM_B2 = 0.999
ADAM_EPS = 1e-08
ADAM_WD = 0.01
ADAM_STEP = 10


def _sds(shape, dtype=F32):
    return jax.ShapeDtypeStruct(tuple(shape), dtype)


def _params(sem=None):
    if sem is None:
        return pltpu.CompilerParams(vmem_limit_bytes=VMEM_LIMIT)
    return pltpu.CompilerParams(dimension_semantics=tuple(sem), vmem_limit_bytes=VMEM_LIMIT)


def _tile(n, prefs):
    for p in prefs:
        if n % p == 0:
            return p
    return n


def _mm(a, b):
    return jnp.dot(a.astype(BF16), b.astype(BF16), preferred_element_type=F32)


def _mm_nt(a, b):
    return lax.dot_general(a.astype(BF16), b.astype(BF16), (((1,), (1,)), ((), ())), preferred_element_type=F32)


def _mm_tn(a, b):
    return lax.dot_general(a.astype(BF16), b.astype(BF16), (((0,), (0,)), ((), ())), preferred_element_type=F32)


def _mm_hi(a, b):
    return jnp.dot(a, b, precision=HIGHEST, preferred_element_type=F32)


def _mm_3x(a, b):
    return jnp.dot(a, b, precision=lax.Precision.HIGH, preferred_element_type=F32)


_GELU_C = 0.7978845608028654
_GELU_A = 0.044715


def _gelu(x):
    return 0.5 * x * (1.0 + jnp.tanh(_GELU_C * (x + _GELU_A * x * x * x)))


def _gelu_grad(x):
    t = jnp.tanh(_GELU_C * (x + _GELU_A * x * x * x))
    return 0.5 * (1.0 + t) + 0.5 * x * (1.0 - t * t) * _GELU_C * (1.0 + 3.0 * _GELU_A * x * x)


def _silu(x):
    return x * jax.nn.sigmoid(x)


def _silu_grad(x):
    s = jax.nn.sigmoid(x)
    return s * (1.0 + x * (1.0 - s))


def _rowsum(x):
    return jnp.sum(x, axis=-1, keepdims=True)


def _colsum(x):
    return jnp.sum(x, axis=0, keepdims=True)


def _iota2(shape, dim):
    return lax.broadcasted_iota(jnp.int32, shape, dim)


def _chunk_tri(tm, upper):
    r = _iota2((tm, tm), 0)
    c = _iota2((tm, tm), 1)
    same = lax.shift_right_logical(r, 6) == lax.shift_right_logical(c, 6)
    tri = (r <= c) if upper else (r >= c)
    return jnp.where(same & tri, 1.0, 0.0).astype(F32)


N_CHIP = 4


def _mesh_place():
    x, y, c = lax.axis_index("x"), lax.axis_index("y"), lax.axis_index("c")
    chips = [(1 - x, y), (x, 1 - y), (1 - x, 1 - y)]
    return x, y, c, (x, y, 1 - c), chips


class _Gather:
    def __init__(self, ins, outs, send_sems, recv_sems, loc_sems):
        self.ins, self.outs, self.send_sems, self.recv_sems, self.loc_sems = ins, outs, send_sems, recv_sems, loc_sems
        self.x, self.y, self.c, self.sib, self.chips = _mesh_place()
        self.me = (self.x, self.y, self.c)
        north = self.c == 1
        self.relay_from = (jnp.where(north, 1 - self.x, self.x), jnp.where(north, self.y, 1 - self.y))
        self.relay_to = (jnp.where(north, self.x, 1 - self.x), jnp.where(north, 1 - self.y, self.y))

    def copy(self, a, k, block, to, src=None):
        slot = self.outs[a].at[4 * block[0] + 2 * block[1] + block[2]]
        return pltpu.make_async_remote_copy(
            src_ref=slot if src is None else src, dst_ref=slot, send_sem=self.send_sems.at[a, k],
            recv_sem=self.recv_sems.at[a, k], device_id=to, device_id_type=MESH_ID)

    def own(self, a):
        return pltpu.make_async_copy(self.ins[a], self.outs[a].at[4 * self.x + 2 * self.y + self.c], self.loc_sems.at[a])

    def first(self, a):
        return [self.copy(a, 0, self.me, self.sib, src=self.ins[a])] + [
            self.copy(a, 1 + j, self.me, (*self.chips[j], self.c), src=self.ins[a]) for j in range(2)]

    def relayed(self, a):
        return self.copy(a, 3, (*self.relay_from, self.c), (*self.relay_to, self.c))

    def passed(self, a, j):
        return self.copy(a, 4 + j, (*self.chips[j], self.c), self.sib)

    def start(self):
        for a in range(len(self.ins)):
            self.own(a).start()
            for cp in self.first(a):
                cp.start()

    def relay(self):
        for a in range(len(self.ins)):
            for j in range(2):
                self.copy(a, 1 + j, (*self.chips[j], self.c), self.me).wait_recv()
            self.relayed(a).start()
            for j in range(2):
                self.passed(a, j).start()

    def finish(self):
        n = len(self.ins)
        for a in range(n):
            self.copy(a, 3, (*self.chips[2], self.c), self.me).wait_recv()
            self.passed(a, 2).start()
        for a in range(n):
            self.copy(a, 0, self.sib, self.me).wait_recv()
            for j, chip in enumerate(self.chips):
                self.copy(a, 4 + j, (*chip, 1 - self.c), self.me).wait_recv()
        for a in range(n):
            for cp in self.first(a) + [self.relayed(a)] + [self.passed(a, j) for j in range(N_CHIP - 1)]:
                cp.wait_send()
            self.own(a).wait()

    @staticmethod
    def sems(n):
        return [pltpu.SemaphoreType.DMA((n, N_DEV - 1)), pltpu.SemaphoreType.DMA((n, N_DEV - 1)),
                pltpu.SemaphoreType.DMA((n,))]


def _gather_two_level(arrs, name):
    n = len(arrs)

    def body(*refs):
        g = _Gather(refs[:n], refs[n:2 * n], *refs[2 * n:])
        g.start()
        g.relay()
        g.finish()

    any_spec = pl.BlockSpec(memory_space=pl.ANY)
    return pl.pallas_call(
        body, name=name, out_shape=[_sds((N_DEV,) + a.shape, a.dtype) for a in arrs],
        in_specs=[any_spec] * n, out_specs=[any_spec] * n, scratch_shapes=_Gather.sems(n),
        compiler_params=pltpu.CompilerParams(has_side_effects=True),
    )(*arrs)


class _Swap:
    def __init__(self, small_ref, ins, small_out, got, s_send, s_recv, g_send, g_recv, loc_sem):
        self.small_ref, self.ins, self.small_out, self.got = small_ref, ins, small_out, got
        self.s_send, self.s_recv, self.g_send, self.g_recv, self.loc_sem = s_send, s_recv, g_send, g_recv, loc_sem
        self.x, self.y, self.c, self.sib, _ = _mesh_place()
        self.me = 4 * self.x + 2 * self.y + self.c

    def small_copy(self, j, landing):
        px = 1 - self.x if (j >> 2) & 1 else self.x
        py = 1 - self.y if (j >> 1) & 1 else self.y
        pc = 1 - self.c if j & 1 else self.c
        slot = 4 * px + 2 * py + pc if landing else self.me
        return pltpu.make_async_remote_copy(
            src_ref=self.small_ref, dst_ref=self.small_out.at[slot], send_sem=self.s_send.at[j - 1],
            recv_sem=self.s_recv.at[j - 1], device_id=(px, py, pc), device_id_type=MESH_ID)

    def own(self):
        return pltpu.make_async_copy(self.small_ref, self.small_out.at[self.me], self.loc_sem)

    def half(self, a, chip):
        return pltpu.make_async_remote_copy(
            src_ref=self.ins[a].at[2 * chip + 1 - self.c], dst_ref=self.got[a].at[chip],
            send_sem=self.g_send.at[a, chip], recv_sem=self.g_recv.at[a, chip], device_id=self.sib,
            device_id_type=MESH_ID)

    def start(self):
        if self.small_ref is not None:
            for j in range(1, N_DEV):
                self.small_copy(j, False).start()
            self.own().start()
        for a in range(len(self.ins)):
            for chip in range(N_CHIP):
                self.half(a, chip).start()

    def finish(self):
        if self.small_ref is not None:
            for j in range(1, N_DEV):
                self.small_copy(j, False).wait_send()
                self.small_copy(j, True).wait_recv()
            self.own().wait()
        for a in range(len(self.ins)):
            for chip in range(N_CHIP):
                self.half(a, chip).wait()

    @staticmethod
    def sems(n):
        return [pltpu.SemaphoreType.DMA((N_DEV - 1,)), pltpu.SemaphoreType.DMA((N_DEV - 1,)),
                pltpu.SemaphoreType.DMA((n, N_CHIP)), pltpu.SemaphoreType.DMA((n, N_CHIP)), pltpu.SemaphoreType.DMA]


def _swap_halves(small, grads, name):
    n = len(grads)
    ns = 0 if small is None else 1

    def body(*refs):
        swap = _Swap(refs[0] if ns else None, refs[ns:ns + n], refs[ns + n] if ns else None,
                     refs[2 * ns + n:2 * ns + 2 * n], *refs[2 * ns + 2 * n:])
        swap.start()
        swap.finish()

    half = [_sds((N_CHIP,) + g.shape[1:], g.dtype) for g in grads]
    any_spec = pl.BlockSpec(memory_space=pl.ANY)
    res = pl.pallas_call(
        body, name=name, out_shape=([_sds((N_DEV,) + small.shape, small.dtype)] if ns else []) + half,
        in_specs=[any_spec] * (ns + n), out_specs=[any_spec] * (ns + n), scratch_shapes=_Swap.sems(n),
        compiler_params=pltpu.CompilerParams(has_side_effects=True),
    )(*([small] if ns else []), *grads)
    return (res[0] if ns else None), res[ns:]


def _pair_sums(core, mine, got):
    n = len(got)

    def body(core_ref, *refs):
        for a in range(n):
            refs[2 * n + a][...] = (refs[a][...].astype(F32) + refs[n + a][...].astype(F32)).astype(BF16)

    half = lambda g: (1, g.shape[1] // 2, g.shape[2])
    own = [pl.BlockSpec(half(g), lambda i, j, core_ref: (2 * i + core_ref[0], j, 0)) for g in got]
    slot = [pl.BlockSpec(half(g), lambda i, j, core_ref: (i, j, 0)) for g in got]
    return pl.pallas_call(
        body, name="pair_sums", out_shape=[_sds(g.shape, BF16) for g in got],
        grid_spec=pltpu.PrefetchScalarGridSpec(
            num_scalar_prefetch=1, grid=(N_CHIP, 2), in_specs=own + slot, out_specs=slot),
        compiler_params=_params(("parallel", "parallel")),
    )(core, *mine, *got)


class _ChipExchange:
    def __init__(self, ins, outs, send_sems, recv_sems, loc_sems):
        self.ins, self.outs, self.send_sems, self.recv_sems, self.loc_sems = ins, outs, send_sems, recv_sems, loc_sems
        self.x, self.y, self.c, _, self.chips = _mesh_place()
        self.mine = 2 * self.x + self.y

    def own(self, a):
        return pltpu.make_async_copy(self.ins[a].at[self.mine], self.outs[a].at[self.mine], self.loc_sems.at[a])

    def copy(self, a, j, lands_in):
        chip = self.chips[j]
        return pltpu.make_async_remote_copy(
            src_ref=self.ins[a].at[2 * chip[0] + chip[1]], dst_ref=self.outs[a].at[lands_in],
            send_sem=self.send_sems.at[a, j], recv_sem=self.recv_sems.at[a, j], device_id=(*chip, self.c),
            device_id_type=MESH_ID)

    def start(self):
        for a in range(len(self.ins)):
            self.own(a).start()
            for j in range(N_CHIP - 1):
                self.copy(a, j, self.mine).start()

    def finish(self):
        for a in range(len(self.ins)):
            for j, chip in enumerate(self.chips):
                self.copy(a, j, self.mine).wait_send()
                self.copy(a, j, 2 * chip[0] + chip[1]).wait_recv()
            self.own(a).wait()

    @staticmethod
    def sems(n):
        return [pltpu.SemaphoreType.DMA((n, N_CHIP - 1)), pltpu.SemaphoreType.DMA((n, N_CHIP - 1)),
                pltpu.SemaphoreType.DMA((n,))]


def _inproj(x, ln_g, w_main, w_ab, late):
    s, d = x.shape
    n = w_main.shape[1]
    tm = _tile(s, (256, 128))
    tn = _tile(n, (1664, 512, 128))
    nl = len(late)
    ni = s // tm

    def body(*refs):
        x_ref, g_ref, w_ref, wab_ref = refs[:4]
        proj_ref, ab_ref, ht_ref = refs[4 + nl:7 + nl]
        gather = _Gather(refs[4:4 + nl], refs[7 + nl:7 + 2 * nl], *refs[7 + 2 * nl:])
        step = pl.program_id(0)

        @pl.when(step == 0)
        def _():
            gather.start()

        xv = x_ref[...]
        r = lax.rsqrt(jnp.mean(xv * xv, axis=-1, keepdims=True) + EPS)
        hf = xv * r * g_ref[...]
        h = hf.astype(BF16)
        ht_ref[...] = hf.T.astype(BF16)
        ab_ref[...] = jnp.dot(h, wab_ref[...], preferred_element_type=F32)
        for c0 in range(0, n, tn):
            proj_ref[:, c0:c0 + tn] = jnp.dot(h, w_ref[:, c0:c0 + tn], preferred_element_type=F32)

        @pl.when(step == ni // 2)
        def _():
            gather.relay()

        @pl.when(step == ni - 1)
        def _():
            gather.finish()

    any_spec = pl.BlockSpec(memory_space=pl.ANY)
    once = lambda a: pl.BlockSpec(a.shape, lambda i: (0, 0), pipeline_mode=pl.Buffered(1))
    res = pl.pallas_call(
        body, name="inproj", grid=(ni,),
        in_specs=[pl.BlockSpec((tm, d), lambda i: (i, 0)), pl.BlockSpec((1, d), lambda i: (0, 0)), once(w_main),
                  once(w_ab)] + [any_spec] * nl,
        out_specs=[pl.BlockSpec((tm, n), lambda i: (i, 0)), pl.BlockSpec((tm, LANE), lambda i: (i, 0)),
                   pl.BlockSpec((d, tm), lambda i: (0, i))] + [any_spec] * nl,
        out_shape=[_sds((s, n)), _sds((s, LANE)), _sds((d, s), BF16)]
        + [_sds((N_DEV,) + a.shape, a.dtype) for a in late],
        scratch_shapes=_Gather.sems(nl),
        compiler_params=_params(("arbitrary",)),
    )(x, ln_g, w_main, w_ab, *late)
    return res[0], res[1], res[2], res[3:]


def _matmul_acc(a, b, name, swap=None):
    m, k = a.shape
    n = b.shape[1]
    tm = _tile(m, (1024, 512, 256, 128))
    tn = _tile(n, (1024, 512, 256, 128))
    tk = _tile(k, (4096, 2048, 1024, 512, 256, 128))
    ni, nj, nk = m // tm, n // tn, k // tk
    ng = 0 if swap is None else len(swap[1])

    def body(*refs):
        a_ref, b_ref = refs[:2]
        if swap is None:
            o_ref, acc = refs[2:]
        else:
            o_ref = refs[3 + ng]
            acc = refs[5 + 2 * ng]
            hosted = _Swap(refs[2], refs[3:3 + ng], refs[4 + ng], refs[5 + ng:5 + 2 * ng], *refs[6 + 2 * ng:])
            step = (pl.program_id(0) * nj + pl.program_id(1)) * nk + pl.program_id(2)

            @pl.when(step == 0)
            def _():
                hosted.start()

        @pl.when(pl.program_id(2) == 0)
        def _():
            acc[...] = jnp.zeros_like(acc)

        acc[...] += jnp.dot(a_ref[...], b_ref[...], preferred_element_type=F32)

        @pl.when(pl.program_id(2) == nk - 1)
        def _():
            o_ref[...] = acc[...].astype(BF16)

        if swap is not None:
            @pl.when(step == ni * nj * nk - 1)
            def _():
                hosted.finish()

    any_spec = pl.BlockSpec(memory_space=pl.ANY)
    in_specs = [pl.BlockSpec((tm, tk), lambda i, j, l: (i, l)), pl.BlockSpec((tk, tn), lambda i, j, l: (l, j))]
    out_specs = [pl.BlockSpec((tm, tn), lambda i, j, l: (i, j))]
    out_shape = [_sds((m, n), BF16)]
    scratch = [pltpu.VMEM((tm, tn), F32)]
    extra = []
    if swap is not None:
        small, grads = swap
        extra = [small, *grads]
        in_specs += [any_spec] * (1 + ng)
        out_specs += [any_spec] * (1 + ng)
        out_shape += [_sds((N_DEV,) + small.shape, small.dtype)] + [_sds((N_CHIP,) + g.shape[1:], g.dtype) for g in grads]
        scratch += _Swap.sems(ng)
    sem = ("parallel", "parallel", "arbitrary") if swap is None else ("arbitrary",) * 3
    res = pl.pallas_call(
        body, name=name, grid=(ni, nj, nk), in_specs=in_specs, out_specs=out_specs, out_shape=out_shape,
        scratch_shapes=scratch, compiler_params=_params(sem),
    )(a, b, *extra)
    return res[0] if swap is None else (res[0], res[1], res[2:])


def _matmul_tn(a, b, name):
    k, m = a.shape
    n = b.shape[1]
    tm = _tile(m, (1024, 512, 256, 128))
    tn = _tile(n, (1024, 512, 256, 128))
    tk = _tile(k, (4096, 2048, 1024, 512, 256, 128))
    nk = k // tk

    def body(a_ref, b_ref, o_ref, acc):
        @pl.when(pl.program_id(2) == 0)
        def _():
            acc[...] = jnp.zeros_like(acc)

        acc[...] += _mm_tn(a_ref[...], b_ref[...])

        @pl.when(pl.program_id(2) == nk - 1)
        def _():
            o_ref[...] = acc[...].astype(BF16)

    return pl.pallas_call(
        body, name=name, grid=(m // tm, n // tn, nk),
        in_specs=[pl.BlockSpec((tk, tm), lambda i, j, l: (l, i)), pl.BlockSpec((tk, tn), lambda i, j, l: (l, j))],
        out_specs=pl.BlockSpec((tm, tn), lambda i, j, l: (i, j)),
        out_shape=_sds((m, n), BF16), scratch_shapes=[pltpu.VMEM((tm, tn), F32)],
        compiler_params=_params(("parallel", "parallel", "arbitrary")),
    )(a, b)


def _dh_rms(pieces, w_rows, wab_rows, x, dx2, ln_g, chip_sums):
    s, d = x.shape
    npc = len(pieces)
    nx = len(chip_sums)
    tm = _tile(s, (256, 128))
    ni = s // tm
    widths = [p.shape[1] for p in pieces[:-1]]
    offs = [sum(widths[:p]) for p in range(npc - 1)]
    nw = len(w_rows)
    nin = npc + 2 * nw + 3

    def body(*refs):
        p_refs = refs[:npc]
        w_refs = refs[npc:npc + nw]
        wab_refs = refs[npc + nw:npc + 2 * nw]
        x_ref, dx2_ref, g_ref = refs[npc + 2 * nw:nin]
        gx_ref, dg_ref = refs[nin + nx:nin + nx + 2]
        exch = _ChipExchange(refs[nin:nin + nx], refs[nin + nx + 2:nin + 2 * nx + 2], *refs[nin + 2 * nx + 2:])
        step = pl.program_id(0)

        @pl.when(step == 0)
        def _():
            dg_ref[...] = jnp.zeros_like(dg_ref)
            exch.start()

        cols = []
        for w_ref, wab_ref in zip(w_refs, wab_refs):
            part = _mm_nt(p_refs[npc - 1][...], wab_ref[...])
            for p in range(npc - 1):
                part += _mm_nt(p_refs[p][...], w_ref[:, offs[p]:offs[p] + widths[p]])
            cols.append(part)
        dhv = jnp.concatenate(cols, axis=1)
        xv = x_ref[...]
        r = lax.rsqrt(jnp.mean(xv * xv, axis=-1, keepdims=True) + EPS)
        xhat = xv * r
        dg_ref[...] += _colsum(dhv * xhat)
        dxh = dhv * g_ref[...]
        gx_ref[...] = dx2_ref[...] + r * (dxh - xhat * jnp.mean(dxh * xhat, axis=-1, keepdims=True))

        @pl.when(step == ni - 1)
        def _():
            exch.finish()

    any_spec = pl.BlockSpec(memory_space=pl.ANY)
    row = pl.BlockSpec((tm, d), lambda i: (i, 0))
    vec = pl.BlockSpec((1, d), lambda i: (0, 0))
    once = lambda a: pl.BlockSpec(a.shape, lambda i: (0, 0), pipeline_mode=pl.Buffered(1))
    in_specs = [pl.BlockSpec((tm, p.shape[1]), lambda i: (i, 0)) for p in pieces]
    in_specs += [once(w) for w in w_rows] + [once(w) for w in wab_rows] + [row, row, vec] + [any_spec] * nx
    res = pl.pallas_call(
        body, name="dh_rms", grid=(ni,), in_specs=in_specs,
        out_specs=[row, vec] + [any_spec] * nx,
        out_shape=[_sds((s, d)), _sds((1, d))] + [_sds(p.shape, p.dtype) for p in chip_sums],
        scratch_shapes=_ChipExchange.sems(nx),
        compiler_params=_params(("arbitrary",)),
    )(*pieces, *w_rows, *wab_rows, x, dx2, ln_g, *chip_sums)
    return res[0], res[1], res[2:]


def _final(x, tgt, out_b, out_a, out_c, w_out, final_g):
    s, d = x.shape
    tm = _tile(s, (256, 128))

    def body(x_ref, t_ref, b_ref, a_ref, c_ref, w_ref, g_ref, dx2_ref, dx2b_ref, dm_ref, loss_ref, dg_ref):
        @pl.when(pl.program_id(0) == 0)
        def _():
            loss_ref[...] = jnp.zeros_like(loss_ref)
            dg_ref[...] = jnp.zeros_like(dg_ref)

        mixed = jnp.concatenate([b_ref[...], a_ref[...], c_ref[...]], axis=1)
        x2 = x_ref[...] + jnp.dot(mixed, w_ref[...], preferred_element_type=F32)
        r = lax.rsqrt(jnp.mean(x2 * x2, axis=-1, keepdims=True) + EPS)
        xhat = x2 * r
        g = g_ref[...]
        err = xhat * g - t_ref[...]
        tok = 0.5 * jnp.mean(err * err, axis=-1, keepdims=True)
        loss_ref[...] += jnp.broadcast_to(_colsum(tok), loss_ref.shape)
        dy = err * (1.0 / d)
        dg_ref[...] += _colsum(dy * xhat)
        dxh = dy * g
        dx2 = r * (dxh - xhat * jnp.mean(dxh * xhat, axis=-1, keepdims=True))
        dx2_ref[...] = dx2
        dx2b = dx2.astype(BF16)
        dx2b_ref[...] = dx2b
        dm_ref[...] = _mm_nt(dx2b, w_ref[...])

    row = pl.BlockSpec((tm, d), lambda i: (i, 0))
    vec = pl.BlockSpec((1, d), lambda i: (0, 0))
    return pl.pallas_call(
        body, name="final", grid=(s // tm,),
        in_specs=[row, row, pl.BlockSpec((tm, DN_W), lambda i: (i, 0)), pl.BlockSpec((tm, GMLP_W), lambda i: (i, 0)),
                  pl.BlockSpec((tm, XA_W), lambda i: (i, 0)), pl.BlockSpec((MIX_W, d), lambda i: (0, 0)), vec],
        out_specs=[row, row, pl.BlockSpec((tm, MIX_W), lambda i: (i, 0)), pl.BlockSpec((1, LANE), lambda i: (0, 0)), vec],
        out_shape=[_sds((s, d)), _sds((s, d), BF16), _sds((s, MIX_W)), _sds((1, LANE)), _sds((1, d))],
        compiler_params=_params(("arbitrary",)),
    )(x, tgt, out_b, out_a, out_c, w_out, final_g)


GU_BLK = (4 * DN_W) // GMLP_W


def _gmlp_norm(gv, lng, lnb):
    va = _gelu(gv)
    mu = jnp.mean(va, axis=-1, keepdims=True)
    xc = va - mu
    rstd = lax.rsqrt(jnp.mean(xc * xc, axis=-1, keepdims=True) + EPS)
    vhat = xc * rstd
    return vhat, rstd, vhat * lng + lnb


def _gmlp_fwd(proj, lng, lnb, ws, bs_t):
    s = proj.shape[0]
    tm = _tile(s, (512, 256, 128))

    def body(u_ref, v_ref, z_ref, lng_ref, lnb_ref, ws_ref, bst_ref, o_ref):
        _, _, vn = _gmlp_norm(v_ref[...], lng_ref[...], lnb_ref[...])
        tri = _iota2((GMLP_T, GMLP_T), 0) >= _iota2((GMLP_T, GMLP_T), 1)
        for g in range(GMLP_G):
            cs = slice(g * HEAD, (g + 1) * HEAD)
            w = jnp.where(tri, ws_ref[g], 0.0).astype(BF16)
            b = bst_ref[:, g:g + 1]
            for c in range(tm // GMLP_T):
                rs = slice(c * GMLP_T, (c + 1) * GMLP_T)
                sg = _mm(w, vn[rs, cs]) + b
                o_ref[rs, cs] = (_gelu(u_ref[rs, cs]) * sg * _silu(z_ref[rs, cs])).astype(BF16)

    col = lambda k: pl.BlockSpec((tm, GMLP_W), lambda i: (i, GU_BLK + k))
    vec = pl.BlockSpec((1, GMLP_W), lambda i: (0, 0))
    return pl.pallas_call(
        body, name="gmlp_fwd", grid=(s // tm,),
        in_specs=[col(0), col(1), col(2), vec, vec, pl.BlockSpec((GMLP_G, GMLP_T, GMLP_T), lambda i: (0, 0, 0)),
                  pl.BlockSpec((GMLP_T, GMLP_G), lambda i: (0, 0))],
        out_specs=pl.BlockSpec((tm, GMLP_W), lambda i: (i, 0)), out_shape=_sds((s, GMLP_W), BF16),
        compiler_params=_params(("parallel",)),
    )(proj, proj, proj, lng, lnb, ws, bs_t)


def _gmlp_bwd(proj, dmixed, lng, lnb, ws, bs_t):
    s = proj.shape[0]
    tm = _tile(s, (512, 256, 128))

    def body(u_ref, v_ref, z_ref, d_ref, lng_ref, lnb_ref, ws_ref, bst_ref,
             dp_ref, dws_ref, dbst_ref, dlng_ref, dlnb_ref, dvn):
        @pl.when(pl.program_id(0) == 0)
        def _():
            dws_ref[...] = jnp.zeros_like(dws_ref)
            dbst_ref[...] = jnp.zeros_like(dbst_ref)
            dlng_ref[...] = jnp.zeros_like(dlng_ref)
            dlnb_ref[...] = jnp.zeros_like(dlnb_ref)

        gv = v_ref[...]
        lng_v = lng_ref[...]
        vhat, rstd, vn = _gmlp_norm(gv, lng_v, lnb_ref[...])
        tri = _iota2((GMLP_T, GMLP_T), 0) >= _iota2((GMLP_T, GMLP_T), 1)
        for g in range(GMLP_G):
            cs = slice(g * HEAD, (g + 1) * HEAD)
            w = jnp.where(tri, ws_ref[g], 0.0).astype(BF16)
            b = bst_ref[:, g:g + 1]
            dw_acc = jnp.zeros((GMLP_T, GMLP_T), F32)
            db_acc = jnp.zeros((GMLP_T, 1), F32)
            for c in range(tm // GMLP_T):
                rs = slice(c * GMLP_T, (c + 1) * GMLP_T)
                vn_b = vn[rs, cs]
                sg = _mm(w, vn_b) + b
                gu = u_ref[rs, cs]
                gz = z_ref[rs, cs]
                da = d_ref[rs, cs]
                uact = _gelu(gu)
                sz = _silu(gz)
                ds = da * uact * sz
                dp_ref[rs, cs] = (da * sg * sz * _gelu_grad(gu)).astype(BF16)
                dp_ref[rs, 2 * GMLP_W + g * HEAD:2 * GMLP_W + (g + 1) * HEAD] = (da * uact * sg * _silu_grad(gz)).astype(BF16)
                dw_acc += _mm_nt(ds, vn_b)
                db_acc += _rowsum(ds)
                dvn[rs, cs] = _mm_tn(w, ds)
            dws_ref[g] += jnp.where(tri, dw_acc, 0.0)
            dbst_ref[:, g:g + 1] += db_acc
        dvn_v = dvn[...]
        dlng_ref[...] += _colsum(dvn_v * vhat)
        dlnb_ref[...] += _colsum(dvn_v)
        dvh = dvn_v * lng_v
        dva = rstd * (dvh - jnp.mean(dvh, axis=-1, keepdims=True) - vhat * jnp.mean(dvh * vhat, axis=-1, keepdims=True))
        dp_ref[:, GMLP_W:2 * GMLP_W] = (dva * _gelu_grad(gv)).astype(BF16)

    col = lambda k: pl.BlockSpec((tm, GMLP_W), lambda i: (i, GU_BLK + k))
    vec = pl.BlockSpec((1, GMLP_W), lambda i: (0, 0))
    wsp = pl.BlockSpec((GMLP_G, GMLP_T, GMLP_T), lambda i: (0, 0, 0))
    bsp = pl.BlockSpec((GMLP_T, GMLP_G), lambda i: (0, 0))
    return pl.pallas_call(
        body, name="gmlp_bwd", grid=(s // tm,),
        in_specs=[col(0), col(1), col(2), pl.BlockSpec((tm, GMLP_W), lambda i: (i, DN_W // GMLP_W)), vec, vec, wsp, bsp],
        out_specs=[pl.BlockSpec((tm, 3 * GMLP_W), lambda i: (i, 0)), wsp, bsp, vec, vec],
        out_shape=[_sds((s, 3 * GMLP_W), BF16), _sds((GMLP_G, GMLP_T, GMLP_T)), _sds((GMLP_T, GMLP_G)),
                   _sds((1, GMLP_W)), _sds((1, GMLP_W))],
        scratch_shapes=[pltpu.VMEM((tm, GMLP_W), F32)],
        compiler_params=_params(("arbitrary",)),
    )(proj, proj, proj, dmixed, lng, lnb, ws, bs_t)


CQ_BLK = (4 * DN_W + 3 * GMLP_W) // XA_W


def _memkv_fwd(mem, g, w_kv):
    nm, d = mem.shape

    def body(m_ref, g_ref, w_ref, kv_ref):
        mv = m_ref[...]
        r = lax.rsqrt(jnp.mean(mv * mv, axis=-1, keepdims=True) + EPS)
        kv_ref[...] = _mm(mv * r * g_ref[...], w_ref[...])

    return pl.pallas_call(body, name="memkv_fwd", out_shape=_sds((nm, 2 * XA_W)), compiler_params=_params())(mem, g, w_kv)


def _memkv_bwd(mem, g, w_kv, dkv):
    nm, d = mem.shape

    def body(m_ref, g_ref, w_ref, dkv_ref, dw_ref, dg_ref):
        mv = m_ref[...]
        r = lax.rsqrt(jnp.mean(mv * mv, axis=-1, keepdims=True) + EPS)
        xhat = mv * r
        dkv_v = dkv_ref[...]
        dw_ref[...] = _mm_tn(xhat * g_ref[...], dkv_v)
        dg_ref[...] = _colsum(_mm_nt(dkv_v, w_ref[...]) * xhat)

    return pl.pallas_call(body, name="memkv_bwd", out_shape=[_sds((d, 2 * XA_W)), _sds((1, d))],
                          compiler_params=_params())(mem, g, w_kv, dkv)


def _xattn_probs(q, mk):
    sc = _mm_nt(q, mk) * (HEAD ** -0.5)
    e = jnp.exp(sc - jnp.max(sc, axis=-1, keepdims=True))
    return e / _rowsum(e)


def _xattn_fwd(proj, mkv):
    s = proj.shape[0]
    nm = mkv.shape[0]
    tm = _tile(s, (512, 256, 128))

    def body(q_ref, z_ref, kv_ref, o_ref):
        for h in range(XA_H):
            cs = slice(h * HEAD, (h + 1) * HEAD)
            p = _xattn_probs(q_ref[:, cs], kv_ref[:, cs])
            ctx = _mm(p, kv_ref[:, XA_W + h * HEAD:XA_W + (h + 1) * HEAD])
            o_ref[:, cs] = (ctx * _silu(z_ref[:, cs])).astype(BF16)

    col = lambda k: pl.BlockSpec((tm, XA_W), lambda i: (i, CQ_BLK + k))
    return pl.pallas_call(
        body, name="xattn_fwd", grid=(s // tm,),
        in_specs=[col(0), col(1), pl.BlockSpec((nm, 2 * XA_W), lambda i: (0, 0))],
        out_specs=pl.BlockSpec((tm, XA_W), lambda i: (i, 0)), out_shape=_sds((s, XA_W), BF16),
        compiler_params=_params(("parallel",)),
    )(proj, proj, mkv)


def _xattn_bwd(proj, dmixed, mkv):
    s = proj.shape[0]
    nm = mkv.shape[0]
    tm = _tile(s, (512, 256, 128))

    def body(q_ref, z_ref, d_ref, kv_ref, dp_ref, dkv_ref):
        @pl.when(pl.program_id(0) == 0)
        def _():
            dkv_ref[...] = jnp.zeros_like(dkv_ref)

        for h in range(XA_H):
            cs = slice(h * HEAD, (h + 1) * HEAD)
            vs = slice(XA_W + h * HEAD, XA_W + (h + 1) * HEAD)
            q = q_ref[:, cs]
            z = z_ref[:, cs]
            mk = kv_ref[:, cs]
            mv = kv_ref[:, vs]
            p = _xattn_probs(q, mk)
            ctx = _mm(p, mv)
            dc = d_ref[:, cs]
            dctx = dc * _silu(z)
            dp_ref[:, vs] = (dc * ctx * _silu_grad(z)).astype(BF16)
            dp = _mm_nt(dctx, mv)
            dkv_ref[:, vs] += _mm_tn(p, dctx)
            ds = p * (dp - _rowsum(dp * p)) * (HEAD ** -0.5)
            dp_ref[:, cs] = _mm(ds, mk).astype(BF16)
            dkv_ref[:, cs] += _mm_tn(ds, q)

    col = lambda k: pl.BlockSpec((tm, XA_W), lambda i: (i, CQ_BLK + k))
    kvs = pl.BlockSpec((nm, 2 * XA_W), lambda i: (0, 0))
    return pl.pallas_call(
        body, name="xattn_bwd", grid=(s // tm,),
        in_specs=[col(0), col(1), pl.BlockSpec((tm, XA_W), lambda i: (i, (DN_W + GMLP_W) // XA_W)), kvs],
        out_specs=[pl.BlockSpec((tm, 2 * XA_W), lambda i: (i, 0)), kvs],
        out_shape=[_sds((s, 2 * XA_W), BF16), _sds((nm, 2 * XA_W))],
        compiler_params=_params(("arbitrary",)),
    )(proj, proj, dmixed, mkv)


def _softplus(x):
    return jnp.maximum(x, 0.0) + jnp.log1p(jnp.exp(-jnp.abs(x)))


def _dn_pre(proj, ab, conv_w, alog_row, dt_row):
    s = proj.shape[0]
    tm = _tile(s, (256, 128))
    w3 = 3 * DN_W

    def body(x_ref, halo_ref, ab_ref, cw_ref, al_ref, dt_ref, q_ref, k_ref, v_ref, gb_ref, gbt_ref, yc_ref):
        i = pl.program_id(0)
        for blk in range(w3 // HEAD):
            cs = slice(blk * HEAD, (blk + 1) * HEAD)
            xv = x_ref[:, cs]
            cat = jnp.concatenate([jnp.where(i > 0, halo_ref[:, cs], 0.0), xv[0:HALO]], axis=0)
            yc = cw_ref[DN_K - 1:DN_K, cs] * xv
            top = cw_ref[DN_K - 1:DN_K, cs] * xv[0:HALO]
            for t in range(DN_K - 1):
                back = DN_K - 1 - t
                yc += cw_ref[t:t + 1, cs] * pltpu.roll(xv, back, 0)
                top += cw_ref[t:t + 1, cs] * pltpu.roll(cat, back, 0)[HALO:2 * HALO]
            yc = jnp.concatenate([top, yc[HALO:tm]], axis=0)
            yc_ref[:, cs] = yc
            act = _silu(yc)
            hs = slice((blk % DN_H) * HEAD, (blk % DN_H + 1) * HEAD)
            if blk < DN_H:
                q_ref[:, hs] = act * (lax.rsqrt(_rowsum(act * act) + EPS) * (HEAD ** -0.5))
            elif blk < 2 * DN_H:
                k_ref[:, hs] = act * lax.rsqrt(_rowsum(act * act) + EPS)
            else:
                v_ref[:, hs] = act
        abv = ab_ref[...]
        lane = _iota2((tm, LANE), 1)
        g = jnp.where(lane < DN_H, -jnp.exp(al_ref[...]) * _softplus(abv + dt_ref[...]), 0.0)
        gc = _mm_hi(_chunk_tri(tm, False), g)
        gbv = jnp.where(lane < DN_H, gc, jnp.where(lane < 2 * DN_H, jax.nn.sigmoid(abv), 0.0))
        gb_ref[...] = gbv
        for c in range(tm // CH):
            gbt_ref[c] = gbv[c * CH:(c + 1) * CH, :].T[0:2 * DN_H, :]

    hb = tm // HALO
    row = lambda w: pl.BlockSpec((tm, w), lambda i: (i, 0))
    vec = pl.BlockSpec((1, LANE), lambda i: (0, 0))
    return pl.pallas_call(
        body, name="dn_pre", grid=(s // tm,),
        in_specs=[row(w3), pl.BlockSpec((HALO, w3), lambda i: (jnp.maximum(i * hb - 1, 0), 0)), row(LANE),
                  pl.BlockSpec((DN_K, w3), lambda i: (0, 0)), vec, vec],
        out_specs=[row(DN_W), row(DN_W), row(DN_W), row(LANE), pl.BlockSpec((tm // CH, 2 * DN_H, CH), lambda i: (i, 0, 0)),
                   row(w3)],
        out_shape=[_sds((s, DN_W)), _sds((s, DN_W)), _sds((s, DN_W)), _sds((s, LANE)),
                   _sds((s // CH, 2 * DN_H, CH)), _sds((s, w3))],
        compiler_params=_params(("parallel",)),
    )(proj, proj, ab, conv_w, alog_row, dt_row)


HEADS = tuple(range(DN_H))


def _hcols(h):
    return slice(h * HEAD, (h + 1) * HEAD)


def _chunk_scalings(k, v, gbv, gbt, h):
    gc = jnp.broadcast_to(gbv[:, h:h + 1], (CH, HEAD))
    beta = jnp.broadcast_to(gbv[:, DN_H + h:DN_H + h + 1], (CH, HEAD))
    gr = gbt[h:h + 1, :]
    ii = _iota2((CH, CH), 0)
    jj = _iota2((CH, CH), 1)
    dec = jnp.exp(jnp.where(ii >= jj, gc[:, 0:CH] - gr, -1e30))
    eg = jnp.exp(gc)
    gl = gr[:, CH - 1:CH]
    kb = k * beta
    return dict(beta=beta, dec=dec, eg=eg, gl=gl, ekd=jnp.exp(gl - gc), kb=kb, vb=v * beta, kbe=kb * eg)


def _chunk_scores(m, q, k):
    kq = _mm_nt(jnp.concatenate([m["kb"], q], axis=0), k)
    strict = _iota2((CH, CH), 0) > _iota2((CH, CH), 1)
    return jnp.where(strict, kq[0:CH] * m["dec"], 0.0), kq[CH:2 * CH] * m["dec"]


def _scan_cpb(s):
    return 8 if (s // CH) % 8 == 0 else 1


def _dn_fwd(q, k, v, gb, gbt, proj, norm_g):
    s = q.shape[0]
    cpb = _scan_cpb(s)
    tb = cpb * CH
    nblk = s // tb

    def body(q_ref, k_ref, v_ref, gb_ref, gbt_ref, z_ref, ng_ref,
             w_ref, qg_ref, kd_ref, t_ref, ai_ref, egl_ref, o_ref, vn_ref, st_ref, ob_ref, state):
        @pl.when(pl.program_id(0) == 0)
        def _():
            state[...] = jnp.zeros_like(state)

        ng = ng_ref[...]
        eye = jnp.where(_iota2((CH, CH), 0) == _iota2((CH, CH), 1), 1.0, 0.0).astype(F32)

        def chunk(c, carry):
            r0 = pl.multiple_of(c * CH, CH)
            rows = pl.ds(r0, CH)
            gbv = gb_ref[rows, :]
            gbt_v = gbt_ref[c]
            qs = [q_ref[rows, _hcols(h)] for h in HEADS]
            ks = [k_ref[rows, _hcols(h)] for h in HEADS]
            ms = [_chunk_scalings(ks[h], v_ref[rows, _hcols(h)], gbv, gbt_v, h) for h in HEADS]
            qgb = [(qs[h] * ms[h]["eg"]).astype(BF16) for h in HEADS]
            kdb = [(ks[h] * ms[h]["ekd"]).astype(BF16) for h in HEADS]
            egl = [jnp.broadcast_to(jnp.exp(ms[h]["gl"]), (1, LANE)) for h in HEADS]
            for h in HEADS:
                qg_ref[rows, _hcols(h)] = qgb[h]
                kd_ref[rows, _hcols(h)] = kdb[h]
                egl_ref[c, h:h + 1, :] = egl[h]
            sc = [_chunk_scores(ms[h], qs[h], ks[h]) for h in HEADS]
            for h in HEADS:
                ai_ref[h, rows, :] = sc[h][1]
            ts = [eye - sc[h][0] for h in HEADS]
            ps = [_mm_3x(sc[h][0], sc[h][0]) for h in HEADS]
            ts = [ts[h] + _mm_3x(ts[h], ps[h]) for h in HEADS]
            for _ in range(4):
                ps = [_mm(ps[h], ps[h]) for h in HEADS]
                ts = [ts[h] + _mm(ts[h], ps[h]) for h in HEADS]
            uw = [_mm(ts[h], jnp.concatenate([ms[h]["vb"], ms[h]["kbe"]], axis=1)) for h in HEADS]
            wb = [uw[h][:, HEAD:2 * HEAD].astype(BF16) for h in HEADS]
            for h in HEADS:
                t_ref[h, rows, :] = ts[h]
                w_ref[rows, _hcols(h)] = wb[h]
            sts = [state[h] for h in HEADS]
            stb = [sts[h].astype(BF16) for h in HEADS]
            for h in HEADS:
                st_ref[c, h] = stb[h]
            vnb = [(uw[h][:, 0:HEAD] - jnp.dot(wb[h], stb[h], preferred_element_type=F32)).astype(BF16) for h in HEADS]
            for h in HEADS:
                state[h] = sts[h] * egl[h] + _mm_tn(kdb[h], vnb[h])
            os_ = [jnp.dot(qgb[h], stb[h], preferred_element_type=F32) + _mm(sc[h][1], vnb[h]) for h in HEADS]
            for h in HEADS:
                o = os_[h]
                vn_ref[rows, _hcols(h)] = vnb[h]
                o_ref[rows, _hcols(h)] = o
                r = lax.rsqrt(jnp.mean(o * o, axis=-1, keepdims=True) + EPS)
                ob_ref[rows, _hcols(h)] = (o * r * ng * _silu(z_ref[rows, _hcols(h)])).astype(BF16)
            return carry

        lax.fori_loop(0, cpb, chunk, 0, unroll=4)

    row = pl.BlockSpec((tb, DN_W), lambda i: (i, 0))
    sq = pl.BlockSpec((DN_H, tb, CH), lambda i: (0, i, 0))
    return pl.pallas_call(
        body, name="dn_fwd", grid=(nblk,),
        in_specs=[row, row, row, pl.BlockSpec((tb, LANE), lambda i: (i, 0)),
                  pl.BlockSpec((cpb, 2 * DN_H, CH), lambda i: (i, 0, 0)), pl.BlockSpec((tb, DN_W), lambda i: (i, 3)),
                  pl.BlockSpec((1, HEAD), lambda i: (0, 0))],
        out_specs=[row, row, row, sq, sq, pl.BlockSpec((cpb, DN_H, LANE), lambda i: (i, 0, 0)), row, row,
                   pl.BlockSpec((cpb, DN_H, HEAD, HEAD), lambda i: (i, 0, 0, 0)), row],
        out_shape=[_sds((s, DN_W), BF16), _sds((s, DN_W), BF16), _sds((s, DN_W), BF16), _sds((DN_H, s, CH)),
                   _sds((DN_H, s, CH)), _sds((s // CH, DN_H, LANE)), _sds((s, DN_W)), _sds((s, DN_W), BF16),
                   _sds((s // CH, DN_H, HEAD, HEAD), BF16), _sds((s, DN_W), BF16)],
        scratch_shapes=[pltpu.VMEM((DN_H, HEAD, HEAD), F32)],
        compiler_params=_params(("arbitrary",)),
    )(q, k, v, gb, gbt, proj, norm_g)


def _dn_bwd(dmixed, o, proj, norm_g, w, qg, kd, ai, egl, q, k, v, gb, gbt, t, vn, st):
    s = o.shape[0]
    cpb = 4 if (s // CH) % 4 == 0 else 1
    tb = cpb * CH
    nblk = s // tb

    def body(dm_ref, o_ref, z_ref, ng_ref, w_ref, qg_ref, kd_ref, ai_ref, egl_ref,
             q_ref, k_ref, v_ref, gb_ref, gbt_ref, t_ref, vn_ref, st_ref,
             dq_ref, dk_ref, dv_ref, dgb_ref, dz_ref, dng_ref, dstate):
        @pl.when(pl.program_id(0) == 0)
        def _():
            dstate[...] = jnp.zeros_like(dstate)
            dng_ref[...] = jnp.zeros_like(dng_ref)

        ng = ng_ref[...]
        lane = _iota2((CH, LANE), 1)
        last = _iota2((CH, 1), 0) == CH - 1
        strict = _iota2((CH, CH), 0) > _iota2((CH, CH), 1)

        def chunk(cc, carry):
            c = cpb - 1 - cc
            r0 = pl.multiple_of(c * CH, CH)
            rows = pl.ds(r0, CH)
            dng = jnp.zeros((1, HEAD), F32)
            dob = []
            for h in HEADS:
                cs = _hcols(h)
                ov = o_ref[rows, cs]
                z = z_ref[rows, cs]
                db = dm_ref[rows, cs]
                r = lax.rsqrt(jnp.mean(ov * ov, axis=-1, keepdims=True) + EPS)
                ohat = ov * r
                dz_ref[rows, cs] = (db * ohat * ng * _silu_grad(z)).astype(BF16)
                dyn = db * _silu(z)
                dng += _colsum(dyn * ohat)
                doh = dyn * ng
                dob.append((r * (doh - ohat * jnp.mean(doh * ohat, axis=-1, keepdims=True))).astype(BF16))
            dng_ref[...] += dng
            dsn = [dstate[h] for h in HEADS]
            dsb = [dsn[h].astype(BF16) for h in HEADS]
            dvnb = [(_mm_tn(ai_ref[h, rows, :], dob[h])
                     + jnp.dot(kd_ref[rows, _hcols(h)], dsb[h], preferred_element_type=F32)).astype(BF16) for h in HEADS]
            part = [_mm_tn(qg_ref[rows, _hcols(h)], dob[h]) + egl_ref[c, h:h + 1, :] * dsn[h] for h in HEADS]
            for h in HEADS:
                dstate[h] = part[h] - _mm_tn(w_ref[rows, _hcols(h)], dvnb[h])
            gbv = gb_ref[rows, :]
            gbt_v = gbt_ref[c]
            qs = [q_ref[rows, _hcols(h)] for h in HEADS]
            ks = [k_ref[rows, _hcols(h)] for h in HEADS]
            vs = [v_ref[rows, _hcols(h)] for h in HEADS]
            ms = [_chunk_scalings(ks[h], vs[h], gbv, gbt_v, h) for h in HEADS]
            sts = [st_ref[c, h] for h in HEADS]
            vnb = [vn_ref[rows, _hcols(h)] for h in HEADS]
            tbf = [t_ref[h, rows, :].astype(BF16) for h in HEADS]
            sc = [_chunk_scores(ms[h], qs[h], ks[h]) for h in HEADS]
            xs_ = [_mm_nt(jnp.concatenate([dob[h], dvnb[h]], axis=0), sts[h]) for h in HEADS]
            dai = [_mm_nt(dob[h], vnb[h]) for h in HEADS]
            dkd = [_mm_nt(vnb[h], dsb[h]) for h in HEADS]
            dqg = [xs_[h][0:CH] for h in HEADS]
            duw = [jnp.concatenate([dvnb[h], (-xs_[h][CH:2 * CH]).astype(BF16)], axis=1) for h in HEADS]
            dt = [_mm_nt(duw[h], jnp.concatenate([ms[h]["vb"], ms[h]["kbe"]], axis=1)) for h in HEADS]
            dvk = [_mm_tn(tbf[h], duw[h]) for h in HEADS]
            tdt = [_mm_tn(tbf[h], dt[h]) for h in HEADS]
            da = [jnp.where(strict, -_mm_nt(tdt[h], tbf[h]), 0.0) for h in HEADS]
            dsc = [jnp.concatenate([da[h] * ms[h]["dec"], dai[h] * ms[h]["dec"]], axis=0) for h in HEADS]
            dkq = [_mm(dsc[h], ks[h]) for h in HEADS]
            dk1 = [_mm_tn(dsc[h], jnp.concatenate([ms[h]["kb"], qs[h]], axis=0)) for h in HEADS]
            dgb = jnp.zeros((CH, LANE), F32)
            for h in HEADS:
                m = ms[h]
                eg, ekd, beta = m["eg"], m["ekd"], m["beta"]
                dvb = dvk[h][:, 0:HEAD]
                dkbe = dvk[h][:, HEAD:2 * HEAD]
                kdv = ks[h] * ekd
                dkb = dkq[h][0:CH] + dkbe * eg
                dq_ref[rows, _hcols(h)] = dkq[h][CH:2 * CH] + dqg[h] * eg
                dk_ref[rows, _hcols(h)] = dk1[h] + dkd[h] * ekd + dkb * beta
                dv_ref[rows, _hcols(h)] = dvb * beta
                dkd_kd = dkd[h] * kdv
                dgl = (jnp.exp(m["gl"]) * _rowsum(_colsum(sts[h].astype(F32) * dsb[h].astype(F32)))
                       + _rowsum(_colsum(dkd_kd)))
                mm_ = da[h] * sc[h][0] + dai[h] * sc[h][1]
                dgc = (_rowsum(mm_ - mm_.T) + _rowsum(dqg[h] * qs[h] * eg - dkd_kd + dkbe * m["kbe"])
                       + jnp.where(last, dgl, 0.0))
                dbeta = _rowsum(dkb * ks[h] + dvb * vs[h])
                dgb = jnp.where(lane == h, dgc, jnp.where(lane == DN_H + h, dbeta, dgb))
            dgb_ref[rows, :] = dgb
            return carry

        lax.fori_loop(0, cpb, chunk, 0, unroll=2)

    rev = lambda i: (nblk - 1 - i, 0)
    row = pl.BlockSpec((tb, DN_W), rev)
    vec = pl.BlockSpec((1, HEAD), lambda i: (0, 0))
    sq = pl.BlockSpec((DN_H, tb, CH), lambda i: (0, nblk - 1 - i, 0))
    gbs = pl.BlockSpec((tb, LANE), rev)
    return pl.pallas_call(
        body, name="dn_bwd", grid=(nblk,),
        in_specs=[row, row, pl.BlockSpec((tb, DN_W), lambda i: (nblk - 1 - i, 3)), vec, row, row, row, sq,
                  pl.BlockSpec((cpb, DN_H, LANE), lambda i: (nblk - 1 - i, 0, 0)),
                  row, row, row, gbs, pl.BlockSpec((cpb, 2 * DN_H, CH), lambda i: (nblk - 1 - i, 0, 0)), sq, row,
                  pl.BlockSpec((cpb, DN_H, HEAD, HEAD), lambda i: (nblk - 1 - i, 0, 0, 0))],
        out_specs=[row, row, row, gbs, row, vec],
        out_shape=[_sds((s, DN_W)), _sds((s, DN_W)), _sds((s, DN_W)), _sds((s, LANE)), _sds((s, DN_W), BF16),
                   _sds((1, HEAD))],
        scratch_shapes=[pltpu.VMEM((DN_H, HEAD, HEAD), F32)],
        compiler_params=_params(("arbitrary",)),
    )(dmixed, o, proj, norm_g, w, qg, kd, ai, egl, q, k, v, gb, gbt, t, vn, st)


def _dn_pre_bwd(proj, yc_all, ab, conv_w, alog_row, dt_row, dq, dk, dv, dgb):
    s = proj.shape[0]
    tm = _tile(s, (256, 128))
    w3 = 3 * DN_W
    nblk = s // tm

    def body(x_ref, yc_ref, ab_ref, cw_ref, al_ref, dt_ref, dq_ref, dk_ref, dv_ref, dgb_ref,
             dx_ref, dab_ref, dcw_ref, dal_ref, ddt_ref, carry):
        i = pl.program_id(0)

        @pl.when(i == 0)
        def _():
            carry[...] = jnp.zeros_like(carry)
            dcw_ref[...] = jnp.zeros_like(dcw_ref)
            dal_ref[...] = jnp.zeros_like(dal_ref)
            ddt_ref[...] = jnp.zeros_like(ddt_ref)

        for blk in range(w3 // HEAD):
            cs = slice(blk * HEAD, (blk + 1) * HEAD)
            yc = yc_ref[:, cs]
            sg = jax.nn.sigmoid(yc)
            act = yc * sg
            dact = sg + act * (1.0 - sg)
            if blk < 2 * DN_H:
                d_ref = dq_ref if blk < DN_H else dk_ref
                dn = d_ref[:, (blk % DN_H) * HEAD:(blk % DN_H + 1) * HEAD]
                rn = lax.rsqrt(_rowsum(act * act) + EPS)
                nh = act * rn
                scale = HEAD ** -0.5 if blk < DN_H else 1.0
                dyc = (scale * rn) * (dn - nh * _rowsum(dn * nh)) * dact
            else:
                dyc = dv_ref[:, (blk - 2 * DN_H) * HEAD:(blk - 2 * DN_H + 1) * HEAD] * dact
            xv = x_ref[:, cs]
            cat = jnp.concatenate([dyc[tm - HALO:tm], carry[:, cs]], axis=0)
            dcw_ref[DN_K - 1:DN_K, cs] += _colsum(dyc * xv)
            dx = cw_ref[DN_K - 1:DN_K, cs] * dyc
            for t in range(DN_K - 1):
                ahead = DN_K - 1 - t
                view = jnp.concatenate([pltpu.roll(dyc, tm - ahead, 0)[0:tm - HALO],
                                        pltpu.roll(cat, 2 * HALO - ahead, 0)[0:HALO]], axis=0)
                dcw_ref[t:t + 1, cs] += _colsum(view * xv)
                dx += cw_ref[t:t + 1, cs] * view
            dx_ref[:, cs] = dx.astype(BF16)
            carry[:, cs] = dyc[0:HALO]

        lane = _iota2((tm, LANE), 1)
        dgbv = dgb_ref[...]
        dg = _mm_hi(_chunk_tri(tm, True), jnp.where(lane < DN_H, dgbv, 0.0))
        abv = ab_ref[...]
        xa = abv + dt_ref[...]
        nea = -jnp.exp(al_ref[...])
        d_da = jnp.where(lane < DN_H, dg * nea * jax.nn.sigmoid(xa), 0.0)
        dal_ref[...] += _colsum(jnp.where(lane < DN_H, dg * nea * _softplus(xa), 0.0))
        ddt_ref[...] += _colsum(d_da)
        beta = jax.nn.sigmoid(abv)
        d_db = jnp.where((lane >= DN_H) & (lane < 2 * DN_H), dgbv * beta * (1.0 - beta), 0.0)
        dab_ref[...] = (d_da + d_db).astype(BF16)

    rev = lambda i: (nblk - 1 - i, 0)
    row = lambda w: pl.BlockSpec((tm, w), rev)
    vec = pl.BlockSpec((1, LANE), lambda i: (0, 0))
    cws = pl.BlockSpec((DN_K, w3), lambda i: (0, 0))
    return pl.pallas_call(
        body, name="dn_pre_bwd", grid=(nblk,),
        in_specs=[row(w3), row(w3), row(LANE), cws, vec, vec, row(DN_W), row(DN_W), row(DN_W), row(LANE)],
        out_specs=[row(w3), row(LANE), cws, vec, vec],
        out_shape=[_sds((s, w3), BF16), _sds((s, LANE), BF16), _sds((DN_K, w3)), _sds((1, LANE)), _sds((1, LANE))],
        scratch_shapes=[pltpu.VMEM((HALO, w3), F32)],
        compiler_params=_params(("arbitrary",)),
    )(proj, yc_all, ab, conv_w, alog_row, dt_row, dq, dk, dv, dgb)


def _adam(parts, w, m, v, name):
    r, c = w.shape
    n_parts = parts.shape[0]
    small = n_parts * r * c * 4 <= 4 * 1024 * 1024
    tr = r if small else _tile(r, (128, 64, 32, 16, 8))

    def body(p_ref, w_ref, m_ref, v_ref, g_ref, d_ref, nm_ref, nv_ref):
        g = p_ref[0].astype(F32)
        for k in range(1, n_parts):
            g = g + p_ref[k].astype(F32)
        g_ref[...] = g
        mn = ADAM_B1 * m_ref[...] + (1.0 - ADAM_B1) * g
        vn = ADAM_B2 * v_ref[...] + (1.0 - ADAM_B2) * (g * g)
        m_hat = mn / (1.0 - ADAM_B1 ** ADAM_STEP)
        v_hat = vn / (1.0 - ADAM_B2 ** ADAM_STEP)
        d_ref[...] = -ADAM_LR * (m_hat / (jnp.sqrt(v_hat) + ADAM_EPS) + ADAM_WD * w_ref[...])
        nm_ref[...] = mn
        nv_ref[...] = vn

    blk = pl.BlockSpec((tr, c), lambda i: (i, 0))
    return pl.pallas_call(
        body, name=name, grid=(r // tr,),
        in_specs=[pl.BlockSpec((n_parts, tr, c), lambda i: (0, i, 0)), blk, blk, blk],
        out_specs=[blk, blk, blk, blk], out_shape=[_sds((r, c))] * 4,
        compiler_params=_params(("parallel",)),
    )(parts, w, m, v)


_PACK_ROWS = 8


def _pack(vals):
    tiles = []
    for a in vals:
        flat = a.reshape(-1).astype(F32)
        unit = _PACK_ROWS * LANE
        n = -(-flat.shape[0] // unit) * unit
        tiles.append(jnp.pad(flat, (0, n - flat.shape[0])).reshape(n // LANE, LANE))
    return jnp.concatenate(tiles, axis=0)


def _unpack(packed, shapes):
    out = []
    r0 = 0
    for shp in shapes:
        size = 1
        for dim in shp:
            size *= dim
        unit = _PACK_ROWS * LANE
        rows = -(-size // unit) * _PACK_ROWS
        out.append(packed[r0:r0 + rows].reshape(-1)[:size].reshape(shp))
        r0 += rows
    return out


def _lane_row(vec8):
    return jnp.pad(vec8.reshape(1, -1).astype(F32), ((0, 0), (0, LANE - vec8.size)))


def kernel(x, mem, ln_g, w_in, gmlp_ln_g, gmlp_ln_b, gmlp_ws, gmlp_bs, conv_w, dn_a_log, dn_dt_bias, dn_norm_g, mem_norm_g, w_mem_kv, w_out, final_g, loss_target, m_ln_g, m_w_in, m_gmlp_ln_g, m_gmlp_ln_b, m_gmlp_ws, m_gmlp_bs, m_conv_w, m_dn_a_log, m_dn_dt_bias, m_dn_norm_g, m_mem_norm_g, m_w_mem_kv, m_w_out, m_final_g, v_ln_g, v_w_in, v_gmlp_ln_g, v_gmlp_ln_b, v_gmlp_ws, v_gmlp_bs, v_conv_w, v_dn_a_log, v_dn_dt_bias, v_dn_norm_g, v_mem_norm_g, v_w_mem_kv, v_w_out, v_final_g):
    xs = x[0]
    mems = mem[0]
    tgt = loss_target[0]
    s, d = xs.shape
    shard_w = w_in.shape[2]
    in_w = N_DEV * shard_w
    me = 4 * lax.axis_index("x") + 2 * lax.axis_index("y") + lax.axis_index("c")

    (g_in,) = _gather_two_level([w_in[0].astype(BF16)], "gather_w_in")
    o_g, o_dn, o_ab = 0, 3 * GMLP_W, 3 * GMLP_W + 4 * DN_W
    o_xa = o_ab + 2 * DN_H

    def shard_cols(g, lo, hi):
        out = []
        while lo < hi:
            sh = lo // shard_w
            end = min(hi, (sh + 1) * shard_w)
            out.append(g[sh][:, lo - sh * shard_w:end - sh * shard_w])
            lo = end
        return out

    def own_layout(g):
        main = jnp.concatenate(shard_cols(g, o_dn, o_ab) + shard_cols(g, o_g, o_dn) + shard_cols(g, o_xa, in_w), axis=1)
        return main, jnp.pad(jnp.concatenate(shard_cols(g, o_ab, o_xa), axis=1), ((0, 0), (0, LANE - 2 * DN_H)))

    w_main, w_ab = own_layout(g_in)

    ln_g2 = ln_g.reshape(1, d)
    lng2 = gmlp_ln_g.reshape(1, GMLP_W)
    lnb2 = gmlp_ln_b.reshape(1, GMLP_W)
    ws3 = gmlp_ws[0]
    bs_t = gmlp_bs[0].T
    alog_row = _lane_row(dn_a_log)
    dt_row = _lane_row(dn_dt_bias)
    dn_g2 = dn_norm_g.reshape(1, HEAD)
    mem_g2 = mem_norm_g.reshape(1, d)
    fin_g2 = final_g.reshape(1, d)

    proj, ab, h_t, (g_out, g_kv, g_conv) = _inproj(
        xs, ln_g2, w_main, w_ab, [w_out[0].astype(BF16), w_mem_kv[0].astype(BF16), conv_w[0]])
    wo = g_out.reshape(MIX_W, d)
    wo_perm = jnp.concatenate([wo[GMLP_W:GMLP_W + DN_W], wo[0:GMLP_W], wo[GMLP_W + DN_W:MIX_W]], axis=0)
    w_kv = g_kv.reshape(d, 2 * XA_W)
    conv_full = g_conv.transpose(1, 0, 2).reshape(DN_K, 3 * DN_W)
    out_a = _gmlp_fwd(proj, lng2, lnb2, ws3, bs_t)
    mkv = _memkv_fwd(mems, mem_g2, w_kv)
    out_c = _xattn_fwd(proj, mkv)
    q, k, v, gb, gbt, yc = _dn_pre(proj, ab, conv_full, alog_row, dt_row)
    wk, qg, kd, tmat, ai, egl, o, vn, st, out_b = _dn_fwd(q, k, v, gb, gbt, proj, dn_g2)

    dx2, dx2b, dmixed, loss_acc, d_fin_g = _final(xs, tgt, out_b, out_a, out_c, wo_perm, fin_g2)

    dwo_b = _matmul_tn(out_b, dx2b, "dw_out_b")
    dwo_a = _matmul_tn(out_a, dx2b, "dw_out_a")
    dwo_c = _matmul_tn(out_c, dx2b, "dw_out_c")
    d_w_out = jnp.concatenate([dwo_a, dwo_b, dwo_c], axis=0)

    dp_g, d_ws, d_bst, d_lng, d_lnb = _gmlp_bwd(proj, dmixed, lng2, lnb2, ws3, bs_t)
    dp_x, dmkv = _xattn_bwd(proj, dmixed, mkv)
    d_w_kv, d_mem_g = _memkv_bwd(mems, mem_g2, w_kv, dmkv)
    dq, dk, dv, dgb, dp_dz, d_dn_g = _dn_bwd(dmixed, o, proj, dn_g2, wk, qg, kd, ai, egl, q, k, v, gb, gbt, tmat, vn, st)
    dp_qkv, dp_ab, d_conv, d_alog, d_dt = _dn_pre_bwd(proj, yc, ab, conv_full, alog_row, dt_row, dq, dk, dv, dgb)

    dw_dz = _matmul_acc(h_t, dp_dz, "dw_in_dz")
    dw_gm = _matmul_acc(h_t, dp_g, "dw_in_gmlp")
    dw_xa = _matmul_acc(h_t, dp_x, "dw_in_xa")
    dw_ab = _matmul_acc(h_t, dp_ab, "dw_in_ab")

    small_shapes = [(1, 1), gmlp_ln_g.shape, gmlp_ln_b.shape, gmlp_ws.shape, gmlp_bs.shape, dn_a_log.shape,
                    dn_dt_bias.shape, dn_norm_g.shape, mem_norm_g.shape, final_g.shape, (DN_K, 3 * DN_W)]
    small_g = _pack([loss_acc[0:1, 0:1], d_lng, d_lnb, d_ws, d_bst.T, d_alog[:, :DN_H], d_dt[:, :DN_H], d_dn_g, d_mem_g,
                     d_fin_g, d_conv])
    zc = jnp.zeros((DN_K, 3 * DN_W), F32)
    z1 = jnp.zeros((1, 1), F32)
    small_w = _pack([z1, gmlp_ln_g, gmlp_ln_b, gmlp_ws, gmlp_bs, dn_a_log, dn_dt_bias, dn_norm_g, mem_norm_g, final_g, zc])
    small_m = _pack([z1, m_gmlp_ln_g, m_gmlp_ln_b, m_gmlp_ws, m_gmlp_bs, m_dn_a_log, m_dn_dt_bias, m_dn_norm_g,
                     m_mem_norm_g, m_final_g, zc])
    small_v = _pack([z1 + 1.0, v_gmlp_ln_g, v_gmlp_ln_b, v_gmlp_ws, v_gmlp_bs, v_dn_a_log, v_dn_dt_bias, v_dn_norm_g,
                     v_mem_norm_g, v_final_g, zc + 1.0])

    send_out = d_w_out.reshape(N_DEV, MIX_W // N_DEV, d).astype(BF16)
    send_kv = d_w_kv.reshape(N_DEV, d // N_DEV, 2 * XA_W).astype(BF16)
    dw_qkv, all_small, (got_out, got_kv) = _matmul_acc(h_t, dp_qkv, "dw_in_qkv", swap=(small_g, [send_out, send_kv]))
    segs = [(o_g, dw_gm), (o_dn, dw_qkv), (o_dn + 3 * DN_W, dw_dz), (o_ab, dw_ab[:, :2 * DN_H]), (o_xa, dw_xa)]
    shards = []
    for sh in range(N_DEV):
        lo, hi = sh * shard_w, (sh + 1) * shard_w
        parts = [arr[:, max(lo, off) - off:min(hi, off + arr.shape[1]) - off] for off, arr in segs
                 if off < hi and off + arr.shape[1] > lo]
        shards.append(jnp.concatenate(parts, axis=1).astype(BF16))
    send_in = jnp.stack(shards)
    sends = [send_in, send_out, send_kv]
    _, (got_in,) = _swap_halves(None, [send_in], "swap_halves")
    got = [got_in, got_out, got_kv]
    core = lax.axis_index("c").astype(jnp.int32).reshape(1)
    chip_sums = _pair_sums(core, sends, got)
    grad_x, d_ln_g, (r_in, r_out, r_kv) = _dh_rms(
        [dp_qkv, dp_dz, dp_g, dp_x, dp_ab], [w_main], [w_ab], xs, dx2, ln_g2, chip_sums)
    (all_ln_g,) = _gather_two_level([_pack([d_ln_g])], "gather_ln_g")

    g_w_in, dl_w_in, nm_w_in, nv_w_in = _adam(r_in, w_in[0], m_w_in[0], v_w_in[0], "adam_w_in")
    g_w_out, dl_w_out, nm_w_out, nv_w_out = _adam(r_out, w_out[0], m_w_out[0], v_w_out[0], "adam_w_out")
    g_w_kv, dl_w_kv, nm_w_kv, nv_w_kv = _adam(r_kv, w_mem_kv[0], m_w_mem_kv[0], v_w_mem_kv[0], "adam_w_kv")
    sm = [_unpack(t, small_shapes) for t in _adam(all_small, small_w, small_m, small_v, "adam_small")]
    ln_res = [_unpack(t, [ln_g.shape])[0]
              for t in _adam(all_ln_g, _pack([ln_g]), _pack([m_ln_g]), _pack([v_ln_g]), "adam_ln_g")]

    conv_parts = lax.dynamic_slice(all_small, (0, all_small.shape[1] - (DN_K * 3 * DN_W) // LANE, 0),
                                   (N_DEV, (DN_K * 3 * DN_W) // LANE, LANE)).reshape(N_DEV, DN_K, 3 * DN_W)
    cshard = conv_w.shape[2]
    conv_parts = lax.dynamic_slice(conv_parts, (0, 0, me * cshard), (N_DEV, DN_K, cshard))
    cpad = ((0, 0), (0, HALO - DN_K), (0, 0))
    conv_res = _adam(jnp.pad(conv_parts, cpad), jnp.pad(conv_w[0], cpad[1:]), jnp.pad(m_conv_w[0], cpad[1:]),
                     jnp.pad(v_conv_w[0], cpad[1:], constant_values=1.0), "adam_conv")
    g_conv_s, dl_conv, nm_conv, nv_conv = [t[:DN_K][None] for t in conv_res]

    loss = sm[0][0].reshape(())

    def group(idx, big_in, big_conv, big_kv, big_out):
        names = sm[idx][1:]
        return [ln_res[idx], big_in[None], names[0], names[1], names[2], names[3], big_conv, names[4], names[5], names[6],
                names[7], big_kv[None], big_out[None], names[8]]

    grads = group(0, g_w_in, g_conv_s, g_w_kv, g_w_out)
    deltas = group(1, dl_w_in, dl_conv, dl_w_kv, dl_w_out)
    new_m = group(2, nm_w_in, nm_conv, nm_w_kv, nm_w_out)
    new_v = group(3, nv_w_in, nv_conv, nv_w_kv, nv_w_out)
    return (loss, grad_x[None], *grads, *deltas, *new_m, *new_v)
```

```python
import jax
import jax.numpy as jnp
from jax import lax
from jax.experimental import pallas as pl
from jax.experimental.pallas import tpu as pltpu

F32 = jnp.float32
BF16 = jnp.bfloat16
HIGHEST = lax.Precision.HIGHEST
MESH_ID = pl.DeviceIdType.MESH

N_DEV = 8
EPS = 1e-6
GMLP_W = 512
GMLP_G = 4
GMLP_T = 128
DN_W = 1024
DN_H = 8
HEAD = 128
DN_K = 4
CH = 64
XA_W = 512
XA_H = 4
LANE = 128
HALO = 8
MAIN_W = 4 * DN_W + 3 * GMLP_W + 2 * XA_W
MIX_W = DN_W + GMLP_W + XA_W
VMEM_LIMIT = 56 * 1024 * 1024

ADAM_LR = 0.001
ADAM_B1 = 0.9
ADAM_B2 = 0.999
ADAM_EPS = 1e-08
ADAM_WD = 0.01
ADAM_STEP = 10


def _sds(shape, dtype=F32):
    return jax.ShapeDtypeStruct(tuple(shape), dtype)


def _params(sem=None):
    if sem is None:
        return pltpu.CompilerParams(vmem_limit_bytes=VMEM_LIMIT)
    return pltpu.CompilerParams(dimension_semantics=tuple(sem), vmem_limit_bytes=VMEM_LIMIT)


def _tile(n, prefs):
    for p in prefs:
        if n % p == 0:
            return p
    return n


def _mm(a, b):
    return jnp.dot(a.astype(BF16), b.astype(BF16), preferred_element_type=F32)


def _mm_nt(a, b):
    return lax.dot_general(a.astype(BF16), b.astype(BF16), (((1,), (1,)), ((), ())), preferred_element_type=F32)


def _mm_tn(a, b):
    return lax.dot_general(a.astype(BF16), b.astype(BF16), (((0,), (0,)), ((), ())), preferred_element_type=F32)


def _mm_hi(a, b):
    return jnp.dot(a, b, precision=HIGHEST, preferred_element_type=F32)


def _mm_3x(a, b):
    return jnp.dot(a, b, precision=lax.Precision.HIGH, preferred_element_type=F32)


_GELU_C = 0.7978845608028654
_GELU_A = 0.044715


def _gelu(x):
    return 0.5 * x * (1.0 + jnp.tanh(_GELU_C * (x + _GELU_A * x * x * x)))


def _gelu_grad(x):
    t = jnp.tanh(_GELU_C * (x + _GELU_A * x * x * x))
    return 0.5 * (1.0 + t) + 0.5 * x * (1.0 - t * t) * _GELU_C * (1.0 + 3.0 * _GELU_A * x * x)


def _silu(x):
    return x * jax.nn.sigmoid(x)


def _silu_grad(x):
    s = jax.nn.sigmoid(x)
    return s * (1.0 + x * (1.0 - s))


def _rowsum(x):
    return jnp.sum(x, axis=-1, keepdims=True)


def _colsum(x):
    return jnp.sum(x, axis=0, keepdims=True)


def _iota2(shape, dim):
    return lax.broadcasted_iota(jnp.int32, shape, dim)


def _chunk_tri(tm, upper):
    r = _iota2((tm, tm), 0)
    c = _iota2((tm, tm), 1)
    same = lax.shift_right_logical(r, 6) == lax.shift_right_logical(c, 6)
    tri = (r <= c) if upper else (r >= c)
    return jnp.where(same & tri, 1.0, 0.0).astype(F32)


N_CHIP = 4


def _mesh_place():
    x, y, c = lax.axis_index("x"), lax.axis_index("y"), lax.axis_index("c")
    chips = [(1 - x, y), (x, 1 - y), (1 - x, 1 - y)]
    return x, y, c, (x, y, 1 - c), chips


class _Gather:
    def __init__(self, ins, outs, send_sems, recv_sems, loc_sems):
        self.ins, self.outs, self.send_sems, self.recv_sems, self.loc_sems = ins, outs, send_sems, recv_sems, loc_sems
        self.x, self.y, self.c, self.sib, self.chips = _mesh_place()
        self.me = (self.x, self.y, self.c)
        north = self.c == 1
        self.relay_from = (jnp.where(north, 1 - self.x, self.x), jnp.where(north, self.y, 1 - self.y))
        self.relay_to = (jnp.where(north, self.x, 1 - self.x), jnp.where(north, 1 - self.y, self.y))

    def copy(self, a, k, block, to, src=None):
        slot = self.outs[a].at[4 * block[0] + 2 * block[1] + block[2]]
        return pltpu.make_async_remote_copy(
            src_ref=slot if src is None else src, dst_ref=slot, send_sem=self.send_sems.at[a, k],
            recv_sem=self.recv_sems.at[a, k], device_id=to, device_id_type=MESH_ID)

    def own(self, a):
        return pltpu.make_async_copy(self.ins[a], self.outs[a].at[4 * self.x + 2 * self.y + self.c], self.loc_sems.at[a])

    def first(self, a):
        return [self.copy(a, 0, self.me, self.sib, src=self.ins[a])] + [
            self.copy(a, 1 + j, self.me, (*self.chips[j], self.c), src=self.ins[a]) for j in range(2)]

    def relayed(self, a):
        return self.copy(a, 3, (*self.relay_from, self.c), (*self.relay_to, self.c))

    def passed(self, a, j):
        return self.copy(a, 4 + j, (*self.chips[j], self.c), self.sib)

    def start(self):
        for a in range(len(self.ins)):
            self.own(a).start()
            for cp in self.first(a):
                cp.start()

    def relay(self):
        for a in range(len(self.ins)):
            for j in range(2):
                self.copy(a, 1 + j, (*self.chips[j], self.c), self.me).wait_recv()
            self.relayed(a).start()
            for j in range(2):
                self.passed(a, j).start()

    def finish(self):
        n = len(self.ins)
        for a in range(n):
            self.copy(a, 3, (*self.chips[2], self.c), self.me).wait_recv()
            self.passed(a, 2).start()
        for a in range(n):
            self.copy(a, 0, self.sib, self.me).wait_recv()
            for j, chip in enumerate(self.chips):
                self.copy(a, 4 + j, (*chip, 1 - self.c), self.me).wait_recv()
        for a in range(n):
            for cp in self.first(a) + [self.relayed(a)] + [self.passed(a, j) for j in range(N_CHIP - 1)]:
                cp.wait_send()
            self.own(a).wait()

    @staticmethod
    def sems(n):
        return [pltpu.SemaphoreType.DMA((n, N_DEV - 1)), pltpu.SemaphoreType.DMA((n, N_DEV - 1)),
                pltpu.SemaphoreType.DMA((n,))]


def _gather_two_level(arrs, name):
    n = len(arrs)

    def body(*refs):
        g = _Gather(refs[:n], refs[n:2 * n], *refs[2 * n:])
        g.start()
        g.relay()
        g.finish()

    any_spec = pl.BlockSpec(memory_space=pl.ANY)
    return pl.pallas_call(
        body, name=name, out_shape=[_sds((N_DEV,) + a.shape, a.dtype) for a in arrs],
        in_specs=[any_spec] * n, out_specs=[any_spec] * n, scratch_shapes=_Gather.sems(n),
        compiler_params=pltpu.CompilerParams(has_side_effects=True),
    )(*arrs)


class _Swap:
    def __init__(self, small_ref, ins, small_out, got, s_send, s_recv, g_send, g_recv, loc_sem):
        self.small_ref, self.ins, self.small_out, self.got = small_ref, ins, small_out, got
        self.s_send, self.s_recv, self.g_send, self.g_recv, self.loc_sem = s_send, s_recv, g_send, g_recv, loc_sem
        self.x, self.y, self.c, self.sib, _ = _mesh_place()
        self.me = 4 * self.x + 2 * self.y + self.c

    def small_copy(self, j, landing):
        px = 1 - self.x if (j >> 2) & 1 else self.x
        py = 1 - self.y if (j >> 1) & 1 else self.y
        pc = 1 - self.c if j & 1 else self.c
        slot = 4 * px + 2 * py + pc if landing else self.me
        return pltpu.make_async_remote_copy(
            src_ref=self.small_ref, dst_ref=self.small_out.at[slot], send_sem=self.s_send.at[j - 1],
            recv_sem=self.s_recv.at[j - 1], device_id=(px, py, pc), device_id_type=MESH_ID)

    def own(self):
        return pltpu.make_async_copy(self.small_ref, self.small_out.at[self.me], self.loc_sem)

    def half(self, a, chip):
        return pltpu.make_async_remote_copy(
            src_ref=self.ins[a].at[2 * chip + 1 - self.c], dst_ref=self.got[a].at[chip],
            send_sem=self.g_send.at[a, chip], recv_sem=self.g_recv.at[a, chip], device_id=self.sib,
            device_id_type=MESH_ID)

    def start(self):
        if self.small_ref is not None:
            for j in range(1, N_DEV):
                self.small_copy(j, False).start()
            self.own().start()
        for a in range(len(self.ins)):
            for chip in range(N_CHIP):
                self.half(a, chip).start()

    def finish(self):
        if self.small_ref is not None:
            for j in range(1, N_DEV):
                self.small_copy(j, False).wait_send()
                self.small_copy(j, True).wait_recv()
            self.own().wait()
        for a in range(len(self.ins)):
            for chip in range(N_CHIP):
                self.half(a, chip).wait()

    @staticmethod
    def sems(n):
        return [pltpu.SemaphoreType.DMA((N_DEV - 1,)), pltpu.SemaphoreType.DMA((N_DEV - 1,)),
                pltpu.SemaphoreType.DMA((n, N_CHIP)), pltpu.SemaphoreType.DMA((n, N_CHIP)), pltpu.SemaphoreType.DMA]


class _Bcast:
    def __init__(self, src, out, send_sems, recv_sems, loc_sem):
        self.src, self.out, self.send_sems, self.recv_sems, self.loc_sem = src, out, send_sems, recv_sems, loc_sem
        self.x, self.y, self.c, _, _ = _mesh_place()
        self.me = 4 * self.x + 2 * self.y + self.c

    def copy(self, j, landing):
        px = 1 - self.x if (j >> 2) & 1 else self.x
        py = 1 - self.y if (j >> 1) & 1 else self.y
        pc = 1 - self.c if j & 1 else self.c
        slot = 4 * px + 2 * py + pc if landing else self.me
        return pltpu.make_async_remote_copy(
            src_ref=self.src, dst_ref=self.out.at[slot], send_sem=self.send_sems.at[j - 1],
            recv_sem=self.recv_sems.at[j - 1], device_id=(px, py, pc), device_id_type=MESH_ID)

    def run(self):
        own = pltpu.make_async_copy(self.src, self.out.at[self.me], self.loc_sem)
        own.start()
        for j in range(1, N_DEV):
            self.copy(j, False).start()
        for j in range(1, N_DEV):
            self.copy(j, False).wait_send()
            self.copy(j, True).wait_recv()
        own.wait()

    @staticmethod
    def sems():
        return [pltpu.SemaphoreType.DMA((N_DEV - 1,)), pltpu.SemaphoreType.DMA((N_DEV - 1,)), pltpu.SemaphoreType.DMA]


def _swap_halves(small, grads, name):
    n = len(grads)
    ns = 0 if small is None else 1

    def body(*refs):
        swap = _Swap(refs[0] if ns else None, refs[ns:ns + n], refs[ns + n] if ns else None,
                     refs[2 * ns + n:2 * ns + 2 * n], *refs[2 * ns + 2 * n:])
        swap.start()
        swap.finish()

    half = [_sds((N_CHIP,) + g.shape[1:], g.dtype) for g in grads]
    any_spec = pl.BlockSpec(memory_space=pl.ANY)
    res = pl.pallas_call(
        body, name=name, out_shape=([_sds((N_DEV,) + small.shape, small.dtype)] if ns else []) + half,
        in_specs=[any_spec] * (ns + n), out_specs=[any_spec] * (ns + n), scratch_shapes=_Swap.sems(n),
        compiler_params=pltpu.CompilerParams(has_side_effects=True),
    )(*([small] if ns else []), *grads)
    return (res[0] if ns else None), res[ns:]


def _pair_sums(core, mine, got):
    n = len(got)

    def body(core_ref, *refs):
        for a in range(n):
            refs[2 * n + a][...] = (refs[a][...].astype(F32) + refs[n + a][...].astype(F32)).astype(BF16)

    half = lambda g: (1, g.shape[1] // 2, g.shape[2])
    own = [pl.BlockSpec(half(g), lambda i, j, core_ref: (2 * i + core_ref[0], j, 0)) for g in got]
    slot = [pl.BlockSpec(half(g), lambda i, j, core_ref: (i, j, 0)) for g in got]
    return pl.pallas_call(
        body, name="pair_sums", out_shape=[_sds(g.shape, BF16) for g in got],
        grid_spec=pltpu.PrefetchScalarGridSpec(
            num_scalar_prefetch=1, grid=(N_CHIP, 2), in_specs=own + slot, out_specs=slot),
        compiler_params=_params(("parallel", "parallel")),
    )(core, *mine, *got)


class _ChipExchange:
    def __init__(self, ins, outs, send_sems, recv_sems, loc_sems):
        self.ins, self.outs, self.send_sems, self.recv_sems, self.loc_sems = ins, outs, send_sems, recv_sems, loc_sems
        self.x, self.y, self.c, _, self.chips = _mesh_place()
        self.mine = 2 * self.x + self.y

    def own(self, a):
        return pltpu.make_async_copy(self.ins[a].at[self.mine], self.outs[a].at[self.mine], self.loc_sems.at[a])

    def copy(self, a, j, lands_in):
        chip = self.chips[j]
        return pltpu.make_async_remote_copy(
            src_ref=self.ins[a].at[2 * chip[0] + chip[1]], dst_ref=self.outs[a].at[lands_in],
            send_sem=self.send_sems.at[a, j], recv_sem=self.recv_sems.at[a, j], device_id=(*chip, self.c),
            device_id_type=MESH_ID)

    def start(self):
        for a in range(len(self.ins)):
            self.own(a).start()
            for j in range(N_CHIP - 1):
                self.copy(a, j, self.mine).start()

    def finish(self):
        for a in range(len(self.ins)):
            for j, chip in enumerate(self.chips):
                self.copy(a, j, self.mine).wait_send()
                self.copy(a, j, 2 * chip[0] + chip[1]).wait_recv()
            self.own(a).wait()

    @staticmethod
    def sems(n):
        return [pltpu.SemaphoreType.DMA((n, N_CHIP - 1)), pltpu.SemaphoreType.DMA((n, N_CHIP - 1)),
                pltpu.SemaphoreType.DMA((n,))]


def _inproj(x, ln_g, w_main, w_ab, late):
    s, d = x.shape
    n = w_main.shape[1]
    tm = _tile(s, (256, 128))
    tn = _tile(n, (1664, 512, 128))
    nl = len(late)
    ni = s // tm

    def body(*refs):
        x_ref, g_ref, w_ref, wab_ref = refs[:4]
        proj_ref, ab_ref, ht_ref = refs[4 + nl:7 + nl]
        gather = _Gather(refs[4:4 + nl], refs[7 + nl:7 + 2 * nl], *refs[7 + 2 * nl:])
        step = pl.program_id(0)

        @pl.when(step == 0)
        def _():
            gather.start()

        xv = x_ref[...]
        r = lax.rsqrt(jnp.mean(xv * xv, axis=-1, keepdims=True) + EPS)
        hf = xv * r * g_ref[...]
        h = hf.astype(BF16)
        ht_ref[...] = hf.T.astype(BF16)
        ab_ref[...] = jnp.dot(h, wab_ref[...], preferred_element_type=F32)
        for c0 in range(0, n, tn):
            proj_ref[:, c0:c0 + tn] = jnp.dot(h, w_ref[:, c0:c0 + tn], preferred_element_type=F32)

        @pl.when(step == ni // 2)
        def _():
            gather.relay()

        @pl.when(step == ni - 1)
        def _():
            gather.finish()

    any_spec = pl.BlockSpec(memory_space=pl.ANY)
    once = lambda a: pl.BlockSpec(a.shape, lambda i: (0, 0), pipeline_mode=pl.Buffered(1))
    res = pl.pallas_call(
        body, name="inproj", grid=(ni,),
        in_specs=[pl.BlockSpec((tm, d), lambda i: (i, 0)), pl.BlockSpec((1, d), lambda i: (0, 0)), once(w_main),
                  once(w_ab)] + [any_spec] * nl,
        out_specs=[pl.BlockSpec((tm, n), lambda i: (i, 0)), pl.BlockSpec((tm, LANE), lambda i: (i, 0)),
                   pl.BlockSpec((d, tm), lambda i: (0, i))] + [any_spec] * nl,
        out_shape=[_sds((s, n)), _sds((s, LANE)), _sds((d, s), BF16)]
        + [_sds((N_DEV,) + a.shape, a.dtype) for a in late],
        scratch_shapes=_Gather.sems(nl),
        compiler_params=_params(("arbitrary",)),
    )(x, ln_g, w_main, w_ab, *late)
    return res[0], res[1], res[2], res[3:]


def _matmul_acc(a, b, name, swap=None):
    m, k = a.shape
    n = b.shape[1]
    tm = _tile(m, (1024, 512, 256, 128))
    tn = _tile(n, (1024, 512, 256, 128))
    tk = _tile(k, (4096, 2048, 1024, 512, 256, 128))
    ni, nj, nk = m // tm, n // tn, k // tk
    ng = 0 if swap is None else len(swap[1])

    def body(*refs):
        a_ref, b_ref = refs[:2]
        if swap is None:
            o_ref, acc = refs[2:]
        else:
            o_ref = refs[3 + ng]
            acc = refs[5 + 2 * ng]
            hosted = _Swap(refs[2], refs[3:3 + ng], refs[4 + ng], refs[5 + ng:5 + 2 * ng], *refs[6 + 2 * ng:])
            step = (pl.program_id(0) * nj + pl.program_id(1)) * nk + pl.program_id(2)

            @pl.when(step == 0)
            def _():
                hosted.start()

        @pl.when(pl.program_id(2) == 0)
        def _():
            acc[...] = jnp.zeros_like(acc)

        acc[...] += jnp.dot(a_ref[...], b_ref[...], preferred_element_type=F32)

        @pl.when(pl.program_id(2) == nk - 1)
        def _():
            o_ref[...] = acc[...].astype(BF16)

        if swap is not None:
            @pl.when(step == ni * nj * nk - 1)
            def _():
                hosted.finish()

    any_spec = pl.BlockSpec(memory_space=pl.ANY)
    in_specs = [pl.BlockSpec((tm, tk), lambda i, j, l: (i, l)), pl.BlockSpec((tk, tn), lambda i, j, l: (l, j))]
    out_specs = [pl.BlockSpec((tm, tn), lambda i, j, l: (i, j))]
    out_shape = [_sds((m, n), BF16)]
    scratch = [pltpu.VMEM((tm, tn), F32)]
    extra = []
    if swap is not None:
        small, grads = swap
        extra = [small, *grads]
        in_specs += [any_spec] * (1 + ng)
        out_specs += [any_spec] * (1 + ng)
        out_shape += [_sds((N_DEV,) + small.shape, small.dtype)] + [_sds((N_CHIP,) + g.shape[1:], g.dtype) for g in grads]
        scratch += _Swap.sems(ng)
    sem = ("parallel", "parallel", "arbitrary") if swap is None else ("arbitrary",) * 3
    res = pl.pallas_call(
        body, name=name, grid=(ni, nj, nk), in_specs=in_specs, out_specs=out_specs, out_shape=out_shape,
        scratch_shapes=scratch, compiler_params=_params(sem),
    )(a, b, *extra)
    return res[0] if swap is None else (res[0], res[1], res[2:])


def _matmul_tn(a, b, name):
    k, m = a.shape
    n = b.shape[1]
    tm = _tile(m, (1024, 512, 256, 128))
    tn = _tile(n, (1024, 512, 256, 128))
    tk = _tile(k, (4096, 2048, 1024, 512, 256, 128))
    nk = k // tk

    def body(a_ref, b_ref, o_ref, acc):
        @pl.when(pl.program_id(2) == 0)
        def _():
            acc[...] = jnp.zeros_like(acc)

        acc[...] += _mm_tn(a_ref[...], b_ref[...])

        @pl.when(pl.program_id(2) == nk - 1)
        def _():
            o_ref[...] = acc[...].astype(BF16)

    return pl.pallas_call(
        body, name=name, grid=(m // tm, n // tn, nk),
        in_specs=[pl.BlockSpec((tk, tm), lambda i, j, l: (l, i)), pl.BlockSpec((tk, tn), lambda i, j, l: (l, j))],
        out_specs=pl.BlockSpec((tm, tn), lambda i, j, l: (i, j)),
        out_shape=_sds((m, n), BF16), scratch_shapes=[pltpu.VMEM((tm, tn), F32)],
        compiler_params=_params(("parallel", "parallel", "arbitrary")),
    )(a, b)


def _dh_rms(pieces, w_rows, wab_rows, x, dx2, ln_g, chip_sums):
    s, d = x.shape
    npc = len(pieces)
    nx = len(chip_sums)
    tm = _tile(s, (256, 128))
    ni = s // tm
    widths = [p.shape[1] for p in pieces[:-1]]
    offs = [sum(widths[:p]) for p in range(npc - 1)]
    nw = len(w_rows)
    nin = npc + 2 * nw + 3

    def body(*refs):
        p_refs = refs[:npc]
        w_refs = refs[npc:npc + nw]
        wab_refs = refs[npc + nw:npc + 2 * nw]
        x_ref, dx2_ref, g_ref = refs[npc + 2 * nw:nin]
        gx_ref, dg_ref = refs[nin + nx:nin + nx + 2]
        exch = _ChipExchange(refs[nin:nin + nx], refs[nin + nx + 2:nin + 2 * nx + 2],
                             *refs[nin + 2 * nx + 3:nin + 2 * nx + 6])
        all_dg = _Bcast(dg_ref, refs[nin + 2 * nx + 2], *refs[nin + 2 * nx + 6:])
        step = pl.program_id(0)

        @pl.when(step == 0)
        def _():
            dg_ref[...] = jnp.zeros_like(dg_ref)
            exch.start()

        cols = []
        for w_ref, wab_ref in zip(w_refs, wab_refs):
            part = _mm_nt(p_refs[npc - 1][...], wab_ref[...])
            for p in range(npc - 1):
                part += _mm_nt(p_refs[p][...], w_ref[:, offs[p]:offs[p] + widths[p]])
            cols.append(part)
        dhv = jnp.concatenate(cols, axis=1)
        xv = x_ref[...]
        r = lax.rsqrt(jnp.mean(xv * xv, axis=-1, keepdims=True) + EPS)
        xhat = xv * r
        dg_ref[...] += _colsum(dhv * xhat)
        dxh = dhv * g_ref[...]
        gx_ref[...] = dx2_ref[...] + r * (dxh - xhat * jnp.mean(dxh * xhat, axis=-1, keepdims=True))

        @pl.when(step == ni - 1)
        def _():
            exch.finish()
            all_dg.run()

    any_spec = pl.BlockSpec(memory_space=pl.ANY)
    row = pl.BlockSpec((tm, d), lambda i: (i, 0))
    vec = pl.BlockSpec((1, d), lambda i: (0, 0))
    once = lambda a: pl.BlockSpec(a.shape, lambda i: (0, 0), pipeline_mode=pl.Buffered(1))
    in_specs = [pl.BlockSpec((tm, p.shape[1]), lambda i: (i, 0)) for p in pieces]
    in_specs += [once(w) for w in w_rows] + [once(w) for w in wab_rows] + [row, row, vec] + [any_spec] * nx
    res = pl.pallas_call(
        body, name="dh_rms", grid=(ni,), in_specs=in_specs,
        out_specs=[row, vec] + [any_spec] * (nx + 1),
        out_shape=[_sds((s, d)), _sds((1, d))] + [_sds(p.shape, p.dtype) for p in chip_sums] + [_sds((N_DEV, 1, d))],
        scratch_shapes=_ChipExchange.sems(nx) + _Bcast.sems(),
        compiler_params=_params(("arbitrary",)),
    )(*pieces, *w_rows, *wab_rows, x, dx2, ln_g, *chip_sums)
    return res[0], res[2:2 + nx], res[2 + nx]


def _final(x, tgt, out_b, out_a, out_c, w_out, final_g):
    s, d = x.shape
    tm = _tile(s, (256, 128))

    def body(x_ref, t_ref, b_ref, a_ref, c_ref, w_ref, g_ref, dx2_ref, dx2b_ref, dm_ref, loss_ref, dg_ref):
        @pl.when(pl.program_id(0) == 0)
        def _():
            loss_ref[...] = jnp.zeros_like(loss_ref)
            dg_ref[...] = jnp.zeros_like(dg_ref)

        mixed = jnp.concatenate([b_ref[...], a_ref[...], c_ref[...]], axis=1)
        x2 = x_ref[...] + jnp.dot(mixed, w_ref[...], preferred_element_type=F32)
        r = lax.rsqrt(jnp.mean(x2 * x2, axis=-1, keepdims=True) + EPS)
        xhat = x2 * r
        g = g_ref[...]
        err = xhat * g - t_ref[...]
        tok = 0.5 * jnp.mean(err * err, axis=-1, keepdims=True)
        loss_ref[...] += jnp.broadcast_to(_colsum(tok), loss_ref.shape)
        dy = err * (1.0 / d)
        dg_ref[...] += _colsum(dy * xhat)
        dxh = dy * g
        dx2 = r * (dxh - xhat * jnp.mean(dxh * xhat, axis=-1, keepdims=True))
        dx2_ref[...] = dx2
        dx2b = dx2.astype(BF16)
        dx2b_ref[...] = dx2b
        dm_ref[...] = _mm_nt(dx2b, w_ref[...])

    row = pl.BlockSpec((tm, d), lambda i: (i, 0))
    vec = pl.BlockSpec((1, d), lambda i: (0, 0))
    return pl.pallas_call(
        body, name="final", grid=(s // tm,),
        in_specs=[row, row, pl.BlockSpec((tm, DN_W), lambda i: (i, 0)), pl.BlockSpec((tm, GMLP_W), lambda i: (i, 0)),
                  pl.BlockSpec((tm, XA_W), lambda i: (i, 0)), pl.BlockSpec((MIX_W, d), lambda i: (0, 0)), vec],
        out_specs=[row, row, pl.BlockSpec((tm, MIX_W), lambda i: (i, 0)), pl.BlockSpec((1, LANE), lambda i: (0, 0)), vec],
        out_shape=[_sds((s, d)), _sds((s, d), BF16), _sds((s, MIX_W)), _sds((1, LANE)), _sds((1, d))],
        compiler_params=_params(("arbitrary",)),
    )(x, tgt, out_b, out_a, out_c, w_out, final_g)


GU_BLK = (4 * DN_W) // GMLP_W


def _gmlp_norm(gv, lng, lnb):
    va = _gelu(gv)
    mu = jnp.mean(va, axis=-1, keepdims=True)
    xc = va - mu
    rstd = lax.rsqrt(jnp.mean(xc * xc, axis=-1, keepdims=True) + EPS)
    vhat = xc * rstd
    return vhat, rstd, vhat * lng + lnb


def _gmlp_fwd(proj, lng, lnb, ws, bs_t):
    s = proj.shape[0]
    tm = _tile(s, (512, 256, 128))

    def body(u_ref, v_ref, z_ref, lng_ref, lnb_ref, ws_ref, bst_ref, o_ref):
        _, _, vn = _gmlp_norm(v_ref[...], lng_ref[...], lnb_ref[...])
        tri = _iota2((GMLP_T, GMLP_T), 0) >= _iota2((GMLP_T, GMLP_T), 1)
        for g in range(GMLP_G):
            cs = slice(g * HEAD, (g + 1) * HEAD)
            w = jnp.where(tri, ws_ref[g], 0.0).astype(BF16)
            b = bst_ref[:, g:g + 1]
            for c in range(tm // GMLP_T):
                rs = slice(c * GMLP_T, (c + 1) * GMLP_T)
                sg = _mm(w, vn[rs, cs]) + b
                o_ref[rs, cs] = (_gelu(u_ref[rs, cs]) * sg * _silu(z_ref[rs, cs])).astype(BF16)

    col = lambda k: pl.BlockSpec((tm, GMLP_W), lambda i: (i, GU_BLK + k))
    vec = pl.BlockSpec((1, GMLP_W), lambda i: (0, 0))
    return pl.pallas_call(
        body, name="gmlp_fwd", grid=(s // tm,),
        in_specs=[col(0), col(1), col(2), vec, vec, pl.BlockSpec((GMLP_G, GMLP_T, GMLP_T), lambda i: (0, 0, 0)),
                  pl.BlockSpec((GMLP_T, GMLP_G), lambda i: (0, 0))],
        out_specs=pl.BlockSpec((tm, GMLP_W), lambda i: (i, 0)), out_shape=_sds((s, GMLP_W), BF16),
        compiler_params=_params(("parallel",)),
    )(proj, proj, proj, lng, lnb, ws, bs_t)


def _gmlp_bwd(proj, dmixed, lng, lnb, ws, bs_t):
    s = proj.shape[0]
    tm = _tile(s, (512, 256, 128))

    def body(u_ref, v_ref, z_ref, d_ref, lng_ref, lnb_ref, ws_ref, bst_ref,
             dp_ref, dws_ref, dbst_ref, dlng_ref, dlnb_ref, dvn):
        @pl.when(pl.program_id(0) == 0)
        def _():
            dws_ref[...] = jnp.zeros_like(dws_ref)
            dbst_ref[...] = jnp.zeros_like(dbst_ref)
            dlng_ref[...] = jnp.zeros_like(dlng_ref)
            dlnb_ref[...] = jnp.zeros_like(dlnb_ref)

        gv = v_ref[...]
        lng_v = lng_ref[...]
        vhat, rstd, vn = _gmlp_norm(gv, lng_v, lnb_ref[...])
        tri = _iota2((GMLP_T, GMLP_T), 0) >= _iota2((GMLP_T, GMLP_T), 1)
        for g in range(GMLP_G):
            cs = slice(g * HEAD, (g + 1) * HEAD)
            w = jnp.where(tri, ws_ref[g], 0.0).astype(BF16)
            b = bst_ref[:, g:g + 1]
            dw_acc = jnp.zeros((GMLP_T, GMLP_T), F32)
            db_acc = jnp.zeros((GMLP_T, 1), F32)
            for c in range(tm // GMLP_T):
                rs = slice(c * GMLP_T, (c + 1) * GMLP_T)
                vn_b = vn[rs, cs]
                sg = _mm(w, vn_b) + b
                gu = u_ref[rs, cs]
                gz = z_ref[rs, cs]
                da = d_ref[rs, cs]
                uact = _gelu(gu)
                sz = _silu(gz)
                ds = da * uact * sz
                dp_ref[rs, cs] = (da * sg * sz * _gelu_grad(gu)).astype(BF16)
                dp_ref[rs, 2 * GMLP_W + g * HEAD:2 * GMLP_W + (g + 1) * HEAD] = (da * uact * sg * _silu_grad(gz)).astype(BF16)
                dw_acc += _mm_nt(ds, vn_b)
                db_acc += _rowsum(ds)
                dvn[rs, cs] = _mm_tn(w, ds)
            dws_ref[g] += jnp.where(tri, dw_acc, 0.0)
            dbst_ref[:, g:g + 1] += db_acc
        dvn_v = dvn[...]
        dlng_ref[...] += _colsum(dvn_v * vhat)
        dlnb_ref[...] += _colsum(dvn_v)
        dvh = dvn_v * lng_v
        dva = rstd * (dvh - jnp.mean(dvh, axis=-1, keepdims=True) - vhat * jnp.mean(dvh * vhat, axis=-1, keepdims=True))
        dp_ref[:, GMLP_W:2 * GMLP_W] = (dva * _gelu_grad(gv)).astype(BF16)

    col = lambda k: pl.BlockSpec((tm, GMLP_W), lambda i: (i, GU_BLK + k))
    vec = pl.BlockSpec((1, GMLP_W), lambda i: (0, 0))
    wsp = pl.BlockSpec((GMLP_G, GMLP_T, GMLP_T), lambda i: (0, 0, 0))
    bsp = pl.BlockSpec((GMLP_T, GMLP_G), lambda i: (0, 0))
    return pl.pallas_call(
        body, name="gmlp_bwd", grid=(s // tm,),
        in_specs=[col(0), col(1), col(2), pl.BlockSpec((tm, GMLP_W), lambda i: (i, DN_W // GMLP_W)), vec, vec, wsp, bsp],
        out_specs=[pl.BlockSpec((tm, 3 * GMLP_W), lambda i: (i, 0)), wsp, bsp, vec, vec],
        out_shape=[_sds((s, 3 * GMLP_W), BF16), _sds((GMLP_G, GMLP_T, GMLP_T)), _sds((GMLP_T, GMLP_G)),
                   _sds((1, GMLP_W)), _sds((1, GMLP_W))],
        scratch_shapes=[pltpu.VMEM((tm, GMLP_W), F32)],
        compiler_params=_params(("arbitrary",)),
    )(proj, proj, proj, dmixed, lng, lnb, ws, bs_t)


CQ_BLK = (4 * DN_W + 3 * GMLP_W) // XA_W


def _memkv_fwd(mem, g, w_kv):
    nm, d = mem.shape

    def body(m_ref, g_ref, w_ref, kv_ref):
        mv = m_ref[...]
        r = lax.rsqrt(jnp.mean(mv * mv, axis=-1, keepdims=True) + EPS)
        kv_ref[...] = _mm(mv * r * g_ref[...], w_ref[...])

    return pl.pallas_call(body, name="memkv_fwd", out_shape=_sds((nm, 2 * XA_W)), compiler_params=_params())(mem, g, w_kv)


def _memkv_bwd(mem, g, w_kv, dkv):
    nm, d = mem.shape

    def body(m_ref, g_ref, w_ref, dkv_ref, dw_ref, dg_ref):
        mv = m_ref[...]
        r = lax.rsqrt(jnp.mean(mv * mv, axis=-1, keepdims=True) + EPS)
        xhat = mv * r
        dkv_v = dkv_ref[...]
        dw_ref[...] = _mm_tn(xhat * g_ref[...], dkv_v)
        dg_ref[...] = _colsum(_mm_nt(dkv_v, w_ref[...]) * xhat)

    return pl.pallas_call(body, name="memkv_bwd", out_shape=[_sds((d, 2 * XA_W)), _sds((1, d))],
                          compiler_params=_params())(mem, g, w_kv, dkv)


def _xattn_probs(q, mk):
    sc = _mm_nt(q, mk) * (HEAD ** -0.5)
    e = jnp.exp(sc - jnp.max(sc, axis=-1, keepdims=True))
    return e / _rowsum(e)


def _xattn_fwd(proj, mkv):
    s = proj.shape[0]
    nm = mkv.shape[0]
    tm = _tile(s, (512, 256, 128))

    def body(q_ref, z_ref, kv_ref, o_ref):
        for h in range(XA_H):
            cs = slice(h * HEAD, (h + 1) * HEAD)
            p = _xattn_probs(q_ref[:, cs], kv_ref[:, cs])
            ctx = _mm(p, kv_ref[:, XA_W + h * HEAD:XA_W + (h + 1) * HEAD])
            o_ref[:, cs] = (ctx * _silu(z_ref[:, cs])).astype(BF16)

    col = lambda k: pl.BlockSpec((tm, XA_W), lambda i: (i, CQ_BLK + k))
    return pl.pallas_call(
        body, name="xattn_fwd", grid=(s // tm,),
        in_specs=[col(0), col(1), pl.BlockSpec((nm, 2 * XA_W), lambda i: (0, 0))],
        out_specs=pl.BlockSpec((tm, XA_W), lambda i: (i, 0)), out_shape=_sds((s, XA_W), BF16),
        compiler_params=_params(("parallel",)),
    )(proj, proj, mkv)


def _xattn_bwd(proj, dmixed, mkv):
    s = proj.shape[0]
    nm = mkv.shape[0]
    tm = _tile(s, (512, 256, 128))

    def body(q_ref, z_ref, d_ref, kv_ref, dp_ref, dkv_ref):
        @pl.when(pl.program_id(0) == 0)
        def _():
            dkv_ref[...] = jnp.zeros_like(dkv_ref)

        for h in range(XA_H):
            cs = slice(h * HEAD, (h + 1) * HEAD)
            vs = slice(XA_W + h * HEAD, XA_W + (h + 1) * HEAD)
            q = q_ref[:, cs]
            z = z_ref[:, cs]
            mk = kv_ref[:, cs]
            mv = kv_ref[:, vs]
            p = _xattn_probs(q, mk)
            ctx = _mm(p, mv)
            dc = d_ref[:, cs]
            dctx = dc * _silu(z)
            dp_ref[:, vs] = (dc * ctx * _silu_grad(z)).astype(BF16)
            dp = _mm_nt(dctx, mv)
            dkv_ref[:, vs] += _mm_tn(p, dctx)
            ds = p * (dp - _rowsum(dp * p)) * (HEAD ** -0.5)
            dp_ref[:, cs] = _mm(ds, mk).astype(BF16)
            dkv_ref[:, cs] += _mm_tn(ds, q)

    col = lambda k: pl.BlockSpec((tm, XA_W), lambda i: (i, CQ_BLK + k))
    kvs = pl.BlockSpec((nm, 2 * XA_W), lambda i: (0, 0))
    return pl.pallas_call(
        body, name="xattn_bwd", grid=(s // tm,),
        in_specs=[col(0), col(1), pl.BlockSpec((tm, XA_W), lambda i: (i, (DN_W + GMLP_W) // XA_W)), kvs],
        out_specs=[pl.BlockSpec((tm, 2 * XA_W), lambda i: (i, 0)), kvs],
        out_shape=[_sds((s, 2 * XA_W), BF16), _sds((nm, 2 * XA_W))],
        compiler_params=_params(("arbitrary",)),
    )(proj, proj, dmixed, mkv)


def _softplus(x):
    return jnp.maximum(x, 0.0) + jnp.log1p(jnp.exp(-jnp.abs(x)))


def _dn_pre(proj, ab, conv_w, alog_row, dt_row):
    s = proj.shape[0]
    tm = _tile(s, (256, 128))
    w3 = 3 * DN_W

    def body(x_ref, halo_ref, ab_ref, cw_ref, al_ref, dt_ref, q_ref, k_ref, v_ref, gb_ref, gbt_ref, yc_ref):
        i = pl.program_id(0)
        for blk in range(w3 // HEAD):
            cs = slice(blk * HEAD, (blk + 1) * HEAD)
            xv = x_ref[:, cs]
            cat = jnp.concatenate([jnp.where(i > 0, halo_ref[:, cs], 0.0), xv[0:HALO]], axis=0)
            yc = cw_ref[DN_K - 1:DN_K, cs] * xv
            top = cw_ref[DN_K - 1:DN_K, cs] * xv[0:HALO]
            for t in range(DN_K - 1):
                back = DN_K - 1 - t
                yc += cw_ref[t:t + 1, cs] * pltpu.roll(xv, back, 0)
                top += cw_ref[t:t + 1, cs] * pltpu.roll(cat, back, 0)[HALO:2 * HALO]
            yc = jnp.concatenate([top, yc[HALO:tm]], axis=0)
            yc_ref[:, cs] = yc
            act = _silu(yc)
            hs = slice((blk % DN_H) * HEAD, (blk % DN_H + 1) * HEAD)
            if blk < DN_H:
                q_ref[:, hs] = act * (lax.rsqrt(_rowsum(act * act) + EPS) * (HEAD ** -0.5))
            elif blk < 2 * DN_H:
                k_ref[:, hs] = act * lax.rsqrt(_rowsum(act * act) + EPS)
            else:
                v_ref[:, hs] = act
        abv = ab_ref[...]
        lane = _iota2((tm, LANE), 1)
        g = jnp.where(lane < DN_H, -jnp.exp(al_ref[...]) * _softplus(abv + dt_ref[...]), 0.0)
        gc = _mm_hi(_chunk_tri(tm, False), g)
        gbv = jnp.where(lane < DN_H, gc, jnp.where(lane < 2 * DN_H, jax.nn.sigmoid(abv), 0.0))
        gb_ref[...] = gbv
        for c in range(tm // CH):
            gbt_ref[c] = gbv[c * CH:(c + 1) * CH, :].T[0:2 * DN_H, :]

    hb = tm // HALO
    row = lambda w: pl.BlockSpec((tm, w), lambda i: (i, 0))
    vec = pl.BlockSpec((1, LANE), lambda i: (0, 0))
    return pl.pallas_call(
        body, name="dn_pre", grid=(s // tm,),
        in_specs=[row(w3), pl.BlockSpec((HALO, w3), lambda i: (jnp.maximum(i * hb - 1, 0), 0)), row(LANE),
                  pl.BlockSpec((DN_K, w3), lambda i: (0, 0)), vec, vec],
        out_specs=[row(DN_W), row(DN_W), row(DN_W), row(LANE), pl.BlockSpec((tm // CH, 2 * DN_H, CH), lambda i: (i, 0, 0)),
                   row(w3)],
        out_shape=[_sds((s, DN_W)), _sds((s, DN_W)), _sds((s, DN_W)), _sds((s, LANE)),
                   _sds((s // CH, 2 * DN_H, CH)), _sds((s, w3))],
        compiler_params=_params(("parallel",)),
    )(proj, proj, ab, conv_w, alog_row, dt_row)


HEADS = tuple(range(DN_H))


def _hcols(h):
    return slice(h * HEAD, (h + 1) * HEAD)


def _chunk_scalings(k, v, gbv, gbt, h):
    gc = jnp.broadcast_to(gbv[:, h:h + 1], (CH, HEAD))
    beta = jnp.broadcast_to(gbv[:, DN_H + h:DN_H + h + 1], (CH, HEAD))
    gr = gbt[h:h + 1, :]
    ii = _iota2((CH, CH), 0)
    jj = _iota2((CH, CH), 1)
    dec = jnp.exp(jnp.where(ii >= jj, gc[:, 0:CH] - gr, -1e30))
    eg = jnp.exp(gc)
    gl = gr[:, CH - 1:CH]
    kb = k * beta
    return dict(beta=beta, dec=dec, eg=eg, gl=gl, ekd=jnp.exp(gl - gc), kb=kb, vb=v * beta, kbe=kb * eg)


def _chunk_scores(m, q, k):
    kq = _mm_nt(jnp.concatenate([m["kb"], q], axis=0), k)
    strict = _iota2((CH, CH), 0) > _iota2((CH, CH), 1)
    return jnp.where(strict, kq[0:CH] * m["dec"], 0.0), kq[CH:2 * CH] * m["dec"]


def _scan_cpb(s):
    return 8 if (s // CH) % 8 == 0 else 1


def _dn_fwd(q, k, v, gb, gbt, proj, norm_g):
    s = q.shape[0]
    cpb = _scan_cpb(s)
    tb = cpb * CH
    nblk = s // tb

    def body(q_ref, k_ref, v_ref, gb_ref, gbt_ref, z_ref, ng_ref,
             w_ref, qg_ref, kd_ref, t_ref, ai_ref, egl_ref, o_ref, vn_ref, st_ref, ob_ref, state):
        @pl.when(pl.program_id(0) == 0)
        def _():
            state[...] = jnp.zeros_like(state)

        ng = ng_ref[...]
        eye = jnp.where(_iota2((CH, CH), 0) == _iota2((CH, CH), 1), 1.0, 0.0).astype(F32)

        def chunk(c, carry):
            r0 = pl.multiple_of(c * CH, CH)
            rows = pl.ds(r0, CH)
            gbv = gb_ref[rows, :]
            gbt_v = gbt_ref[c]
            qs = [q_ref[rows, _hcols(h)] for h in HEADS]
            ks = [k_ref[rows, _hcols(h)] for h in HEADS]
            ms = [_chunk_scalings(ks[h], v_ref[rows, _hcols(h)], gbv, gbt_v, h) for h in HEADS]
            qgb = [(qs[h] * ms[h]["eg"]).astype(BF16) for h in HEADS]
            kdb = [(ks[h] * ms[h]["ekd"]).astype(BF16) for h in HEADS]
            egl = [jnp.broadcast_to(jnp.exp(ms[h]["gl"]), (1, LANE)) for h in HEADS]
            for h in HEADS:
                qg_ref[rows, _hcols(h)] = qgb[h]
                kd_ref[rows, _hcols(h)] = kdb[h]
                egl_ref[c, h:h + 1, :] = egl[h]
            sc = [_chunk_scores(ms[h], qs[h], ks[h]) for h in HEADS]
            for h in HEADS:
                ai_ref[h, rows, :] = sc[h][1]
            ts = [eye - sc[h][0] for h in HEADS]
            ps = [_mm_3x(sc[h][0], sc[h][0]) for h in HEADS]
            ts = [ts[h] + _mm_3x(ts[h], ps[h]) for h in HEADS]
            for _ in range(4):
                ps = [_mm(ps[h], ps[h]) for h in HEADS]
                ts = [ts[h] + _mm(ts[h], ps[h]) for h in HEADS]
            uw = [_mm(ts[h], jnp.concatenate([ms[h]["vb"], ms[h]["kbe"]], axis=1)) for h in HEADS]
            wb = [uw[h][:, HEAD:2 * HEAD].astype(BF16) for h in HEADS]
            for h in HEADS:
                t_ref[h, rows, :] = ts[h]
                w_ref[rows, _hcols(h)] = wb[h]
            sts = [state[h] for h in HEADS]
            stb = [sts[h].astype(BF16) for h in HEADS]
            for h in HEADS:
                st_ref[c, h] = stb[h]
            vnb = [(uw[h][:, 0:HEAD] - jnp.dot(wb[h], stb[h], preferred_element_type=F32)).astype(BF16) for h in HEADS]
            for h in HEADS:
                state[h] = sts[h] * egl[h] + _mm_tn(kdb[h], vnb[h])
            os_ = [jnp.dot(qgb[h], stb[h], preferred_element_type=F32) + _mm(sc[h][1], vnb[h]) for h in HEADS]
            for h in HEADS:
                o = os_[h]
                vn_ref[rows, _hcols(h)] = vnb[h]
                o_ref[rows, _hcols(h)] = o
                r = lax.rsqrt(jnp.mean(o * o, axis=-1, keepdims=True) + EPS)
                ob_ref[rows, _hcols(h)] = (o * r * ng * _silu(z_ref[rows, _hcols(h)])).astype(BF16)
            return carry

        lax.fori_loop(0, cpb, chunk, 0, unroll=4)

    row = pl.BlockSpec((tb, DN_W), lambda i: (i, 0))
    sq = pl.BlockSpec((DN_H, tb, CH), lambda i: (0, i, 0))
    return pl.pallas_call(
        body, name="dn_fwd", grid=(nblk,),
        in_specs=[row, row, row, pl.BlockSpec((tb, LANE), lambda i: (i, 0)),
                  pl.BlockSpec((cpb, 2 * DN_H, CH), lambda i: (i, 0, 0)), pl.BlockSpec((tb, DN_W), lambda i: (i, 3)),
                  pl.BlockSpec((1, HEAD), lambda i: (0, 0))],
        out_specs=[row, row, row, sq, sq, pl.BlockSpec((cpb, DN_H, LANE), lambda i: (i, 0, 0)), row, row,
                   pl.BlockSpec((cpb, DN_H, HEAD, HEAD), lambda i: (i, 0, 0, 0)), row],
        out_shape=[_sds((s, DN_W), BF16), _sds((s, DN_W), BF16), _sds((s, DN_W), BF16), _sds((DN_H, s, CH)),
                   _sds((DN_H, s, CH)), _sds((s // CH, DN_H, LANE)), _sds((s, DN_W)), _sds((s, DN_W), BF16),
                   _sds((s // CH, DN_H, HEAD, HEAD), BF16), _sds((s, DN_W), BF16)],
        scratch_shapes=[pltpu.VMEM((DN_H, HEAD, HEAD), F32)],
        compiler_params=_params(("arbitrary",)),
    )(q, k, v, gb, gbt, proj, norm_g)


def _dn_bwd(dmixed, o, proj, norm_g, w, qg, kd, ai, egl, q, k, v, gb, gbt, t, vn, st):
    s = o.shape[0]
    cpb = 4 if (s // CH) % 4 == 0 else 1
    tb = cpb * CH
    nblk = s // tb

    def body(dm_ref, o_ref, z_ref, ng_ref, w_ref, qg_ref, kd_ref, ai_ref, egl_ref,
             q_ref, k_ref, v_ref, gb_ref, gbt_ref, t_ref, vn_ref, st_ref,
             dq_ref, dk_ref, dv_ref, dgb_ref, dz_ref, dng_ref, dstate):
        @pl.when(pl.program_id(0) == 0)
        def _():
            dstate[...] = jnp.zeros_like(dstate)
            dng_ref[...] = jnp.zeros_like(dng_ref)

        ng = ng_ref[...]
        lane = _iota2((CH, LANE), 1)
        last = _iota2((CH, 1), 0) == CH - 1
        strict = _iota2((CH, CH), 0) > _iota2((CH, CH), 1)

        def chunk(cc, carry):
            c = cpb - 1 - cc
            r0 = pl.multiple_of(c * CH, CH)
            rows = pl.ds(r0, CH)
            dng = jnp.zeros((1, HEAD), F32)
            dob = []
            for h in HEADS:
                cs = _hcols(h)
                ov = o_ref[rows, cs]
                z = z_ref[rows, cs]
                db = dm_ref[rows, cs]
                r = lax.rsqrt(jnp.mean(ov * ov, axis=-1, keepdims=True) + EPS)
                ohat = ov * r
                dz_ref[rows, cs] = (db * ohat * ng * _silu_grad(z)).astype(BF16)
                dyn = db * _silu(z)
                dng += _colsum(dyn * ohat)
                doh = dyn * ng
                dob.append((r * (doh - ohat * jnp.mean(doh * ohat, axis=-1, keepdims=True))).astype(BF16))
            dng_ref[...] += dng
            dsn = [dstate[h] for h in HEADS]
            dsb = [dsn[h].astype(BF16) for h in HEADS]
            dvnb = [(_mm_tn(ai_ref[h, rows, :], dob[h])
                     + jnp.dot(kd_ref[rows, _hcols(h)], dsb[h], preferred_element_type=F32)).astype(BF16) for h in HEADS]
            part = [_mm_tn(qg_ref[rows, _hcols(h)], dob[h]) + egl_ref[c, h:h + 1, :] * dsn[h] for h in HEADS]
            for h in HEADS:
                dstate[h] = part[h] - _mm_tn(w_ref[rows, _hcols(h)], dvnb[h])
            gbv = gb_ref[rows, :]
            gbt_v = gbt_ref[c]
            qs = [q_ref[rows, _hcols(h)] for h in HEADS]
            ks = [k_ref[rows, _hcols(h)] for h in HEADS]
            vs = [v_ref[rows, _hcols(h)] for h in HEADS]
            ms = [_chunk_scalings(ks[h], vs[h], gbv, gbt_v, h) for h in HEADS]
            sts = [st_ref[c, h] for h in HEADS]
            vnb = [vn_ref[rows, _hcols(h)] for h in HEADS]
            tbf = [t_ref[h, rows, :].astype(BF16) for h in HEADS]
            sc = [_chunk_scores(ms[h], qs[h], ks[h]) for h in HEADS]
            xs_ = [_mm_nt(jnp.concatenate([dob[h], dvnb[h]], axis=0), sts[h]) for h in HEADS]
            dai = [_mm_nt(dob[h], vnb[h]) for h in HEADS]
            dkd = [_mm_nt(vnb[h], dsb[h]) for h in HEADS]
            dqg = [xs_[h][0:CH] for h in HEADS]
            duw = [jnp.concatenate([dvnb[h], (-xs_[h][CH:2 * CH]).astype(BF16)], axis=1) for h in HEADS]
            dt = [_mm_nt(duw[h], jnp.concatenate([ms[h]["vb"], ms[h]["kbe"]], axis=1)) for h in HEADS]
            dvk = [_mm_tn(tbf[h], duw[h]) for h in HEADS]
            tdt = [_mm_tn(tbf[h], dt[h]) for h in HEADS]
            da = [jnp.where(strict, -_mm_nt(tdt[h], tbf[h]), 0.0) for h in HEADS]
            dsc = [jnp.concatenate([da[h] * ms[h]["dec"], dai[h] * ms[h]["dec"]], axis=0) for h in HEADS]
            dkq = [_mm(dsc[h], ks[h]) for h in HEADS]
            dk1 = [_mm_tn(dsc[h], jnp.concatenate([ms[h]["kb"], qs[h]], axis=0)) for h in HEADS]
            dgb = jnp.zeros((CH, LANE), F32)
            for h in HEADS:
                m = ms[h]
                eg, ekd, beta = m["eg"], m["ekd"], m["beta"]
                dvb = dvk[h][:, 0:HEAD]
                dkbe = dvk[h][:, HEAD:2 * HEAD]
                kdv = ks[h] * ekd
                dkb = dkq[h][0:CH] + dkbe * eg
                dq_ref[rows, _hcols(h)] = dkq[h][CH:2 * CH] + dqg[h] * eg
                dk_ref[rows, _hcols(h)] = dk1[h] + dkd[h] * ekd + dkb * beta
                dv_ref[rows, _hcols(h)] = dvb * beta
                dkd_kd = dkd[h] * kdv
                dgl = (jnp.exp(m["gl"]) * _rowsum(_colsum(sts[h].astype(F32) * dsb[h].astype(F32)))
                       + _rowsum(_colsum(dkd_kd)))
                mm_ = da[h] * sc[h][0] + dai[h] * sc[h][1]
                dgc = (_rowsum(mm_ - mm_.T) + _rowsum(dqg[h] * qs[h] * eg - dkd_kd + dkbe * m["kbe"])
                       + jnp.where(last, dgl, 0.0))
                dbeta = _rowsum(dkb * ks[h] + dvb * vs[h])
                dgb = jnp.where(lane == h, dgc, jnp.where(lane == DN_H + h, dbeta, dgb))
            dgb_ref[rows, :] = dgb
            return carry

        lax.fori_loop(0, cpb, chunk, 0, unroll=2)

    rev = lambda i: (nblk - 1 - i, 0)
    row = pl.BlockSpec((tb, DN_W), rev)
    vec = pl.BlockSpec((1, HEAD), lambda i: (0, 0))
    sq = pl.BlockSpec((DN_H, tb, CH), lambda i: (0, nblk - 1 - i, 0))
    gbs = pl.BlockSpec((tb, LANE), rev)
    return pl.pallas_call(
        body, name="dn_bwd", grid=(nblk,),
        in_specs=[row, row, pl.BlockSpec((tb, DN_W), lambda i: (nblk - 1 - i, 3)), vec, row, row, row, sq,
                  pl.BlockSpec((cpb, DN_H, LANE), lambda i: (nblk - 1 - i, 0, 0)),
                  row, row, row, gbs, pl.BlockSpec((cpb, 2 * DN_H, CH), lambda i: (nblk - 1 - i, 0, 0)), sq, row,
                  pl.BlockSpec((cpb, DN_H, HEAD, HEAD), lambda i: (nblk - 1 - i, 0, 0, 0))],
        out_specs=[row, row, row, gbs, row, vec],
        out_shape=[_sds((s, DN_W)), _sds((s, DN_W)), _sds((s, DN_W)), _sds((s, LANE)), _sds((s, DN_W), BF16),
                   _sds((1, HEAD))],
        scratch_shapes=[pltpu.VMEM((DN_H, HEAD, HEAD), F32)],
        compiler_params=_params(("arbitrary",)),
    )(dmixed, o, proj, norm_g, w, qg, kd, ai, egl, q, k, v, gb, gbt, t, vn, st)


def _dn_pre_bwd(proj, yc_all, ab, conv_w, alog_row, dt_row, dq, dk, dv, dgb):
    s = proj.shape[0]
    tm = _tile(s, (256, 128))
    w3 = 3 * DN_W
    nblk = s // tm

    def body(x_ref, yc_ref, ab_ref, cw_ref, al_ref, dt_ref, dq_ref, dk_ref, dv_ref, dgb_ref,
             dx_ref, dab_ref, dcw_ref, dal_ref, ddt_ref, carry):
        i = pl.program_id(0)

        @pl.when(i == 0)
        def _():
            carry[...] = jnp.zeros_like(carry)
            dcw_ref[...] = jnp.zeros_like(dcw_ref)
            dal_ref[...] = jnp.zeros_like(dal_ref)
            ddt_ref[...] = jnp.zeros_like(ddt_ref)

        for blk in range(w3 // HEAD):
            cs = slice(blk * HEAD, (blk + 1) * HEAD)
            yc = yc_ref[:, cs]
            sg = jax.nn.sigmoid(yc)
            act = yc * sg
            dact = sg + act * (1.0 - sg)
            if blk < 2 * DN_H:
                d_ref = dq_ref if blk < DN_H else dk_ref
                dn = d_ref[:, (blk % DN_H) * HEAD:(blk % DN_H + 1) * HEAD]
                rn = lax.rsqrt(_rowsum(act * act) + EPS)
                nh = act * rn
                scale = HEAD ** -0.5 if blk < DN_H else 1.0
                dyc = (scale * rn) * (dn - nh * _rowsum(dn * nh)) * dact
            else:
                dyc = dv_ref[:, (blk - 2 * DN_H) * HEAD:(blk - 2 * DN_H + 1) * HEAD] * dact
            xv = x_ref[:, cs]
            cat = jnp.concatenate([dyc[tm - HALO:tm], carry[:, cs]], axis=0)
            dcw_ref[DN_K - 1:DN_K, cs] += _colsum(dyc * xv)
            dx = cw_ref[DN_K - 1:DN_K, cs] * dyc
            for t in range(DN_K - 1):
                ahead = DN_K - 1 - t
                view = jnp.concatenate([pltpu.roll(dyc, tm - ahead, 0)[0:tm - HALO],
                                        pltpu.roll(cat, 2 * HALO - ahead, 0)[0:HALO]], axis=0)
                dcw_ref[t:t + 1, cs] += _colsum(view * xv)
                dx += cw_ref[t:t + 1, cs] * view
            dx_ref[:, cs] = dx.astype(BF16)
            carry[:, cs] = dyc[0:HALO]

        lane = _iota2((tm, LANE), 1)
        dgbv = dgb_ref[...]
        dg = _mm_hi(_chunk_tri(tm, True), jnp.where(lane < DN_H, dgbv, 0.0))
        abv = ab_ref[...]
        xa = abv + dt_ref[...]
        nea = -jnp.exp(al_ref[...])
        d_da = jnp.where(lane < DN_H, dg * nea * jax.nn.sigmoid(xa), 0.0)
        dal_ref[...] += _colsum(jnp.where(lane < DN_H, dg * nea * _softplus(xa), 0.0))
        ddt_ref[...] += _colsum(d_da)
        beta = jax.nn.sigmoid(abv)
        d_db = jnp.where((lane >= DN_H) & (lane < 2 * DN_H), dgbv * beta * (1.0 - beta), 0.0)
        dab_ref[...] = (d_da + d_db).astype(BF16)

    rev = lambda i: (nblk - 1 - i, 0)
    row = lambda w: pl.BlockSpec((tm, w), rev)
    vec = pl.BlockSpec((1, LANE), lambda i: (0, 0))
    cws = pl.BlockSpec((DN_K, w3), lambda i: (0, 0))
    return pl.pallas_call(
        body, name="dn_pre_bwd", grid=(nblk,),
        in_specs=[row(w3), row(w3), row(LANE), cws, vec, vec, row(DN_W), row(DN_W), row(DN_W), row(LANE)],
        out_specs=[row(w3), row(LANE), cws, vec, vec],
        out_shape=[_sds((s, w3), BF16), _sds((s, LANE), BF16), _sds((DN_K, w3)), _sds((1, LANE)), _sds((1, LANE))],
        scratch_shapes=[pltpu.VMEM((HALO, w3), F32)],
        compiler_params=_params(("arbitrary",)),
    )(proj, yc_all, ab, conv_w, alog_row, dt_row, dq, dk, dv, dgb)


def _adam(parts, w, m, v, name):
    r, c = w.shape
    n_parts = parts.shape[0]
    small = n_parts * r * c * 4 <= 4 * 1024 * 1024
    tr = r if small else _tile(r, (128, 64, 32, 16, 8))

    def body(p_ref, w_ref, m_ref, v_ref, g_ref, d_ref, nm_ref, nv_ref):
        g = p_ref[0].astype(F32)
        for k in range(1, n_parts):
            g = g + p_ref[k].astype(F32)
        g_ref[...] = g
        mn = ADAM_B1 * m_ref[...] + (1.0 - ADAM_B1) * g
        vn = ADAM_B2 * v_ref[...] + (1.0 - ADAM_B2) * (g * g)
        m_hat = mn / (1.0 - ADAM_B1 ** ADAM_STEP)
        v_hat = vn / (1.0 - ADAM_B2 ** ADAM_STEP)
        d_ref[...] = -ADAM_LR * (m_hat / (jnp.sqrt(v_hat) + ADAM_EPS) + ADAM_WD * w_ref[...])
        nm_ref[...] = mn
        nv_ref[...] = vn

    blk = pl.BlockSpec((tr, c), lambda i: (i, 0))
    return pl.pallas_call(
        body, name=name, grid=(r // tr,),
        in_specs=[pl.BlockSpec((n_parts, tr, c), lambda i: (0, i, 0)), blk, blk, blk],
        out_specs=[blk, blk, blk, blk], out_shape=[_sds((r, c))] * 4,
        compiler_params=_params(("parallel",)),
    )(parts, w, m, v)


_PACK_ROWS = 8


def _pack(vals):
    tiles = []
    for a in vals:
        flat = a.reshape(-1).astype(F32)
        unit = _PACK_ROWS * LANE
        n = -(-flat.shape[0] // unit) * unit
        tiles.append(jnp.pad(flat, (0, n - flat.shape[0])).reshape(n // LANE, LANE))
    return jnp.concatenate(tiles, axis=0)


def _unpack(packed, shapes):
    out = []
    r0 = 0
    for shp in shapes:
        size = 1
        for dim in shp:
            size *= dim
        unit = _PACK_ROWS * LANE
        rows = -(-size // unit) * _PACK_ROWS
        out.append(packed[r0:r0 + rows].reshape(-1)[:size].reshape(shp))
        r0 += rows
    return out


def _lane_row(vec8):
    return jnp.pad(vec8.reshape(1, -1).astype(F32), ((0, 0), (0, LANE - vec8.size)))


def kernel(x, mem, ln_g, w_in, gmlp_ln_g, gmlp_ln_b, gmlp_ws, gmlp_bs, conv_w, dn_a_log, dn_dt_bias, dn_norm_g, mem_norm_g, w_mem_kv, w_out, final_g, loss_target, m_ln_g, m_w_in, m_gmlp_ln_g, m_gmlp_ln_b, m_gmlp_ws, m_gmlp_bs, m_conv_w, m_dn_a_log, m_dn_dt_bias, m_dn_norm_g, m_mem_norm_g, m_w_mem_kv, m_w_out, m_final_g, v_ln_g, v_w_in, v_gmlp_ln_g, v_gmlp_ln_b, v_gmlp_ws, v_gmlp_bs, v_conv_w, v_dn_a_log, v_dn_dt_bias, v_dn_norm_g, v_mem_norm_g, v_w_mem_kv, v_w_out, v_final_g):
    xs = x[0]
    mems = mem[0]
    tgt = loss_target[0]
    s, d = xs.shape
    shard_w = w_in.shape[2]
    in_w = N_DEV * shard_w
    me = 4 * lax.axis_index("x") + 2 * lax.axis_index("y") + lax.axis_index("c")

    (g_in,) = _gather_two_level([w_in[0].astype(BF16)], "gather_w_in")
    o_g, o_dn, o_ab = 0, 3 * GMLP_W, 3 * GMLP_W + 4 * DN_W
    o_xa = o_ab + 2 * DN_H

    def shard_cols(g, lo, hi):
        out = []
        while lo < hi:
            sh = lo // shard_w
            end = min(hi, (sh + 1) * shard_w)
            out.append(g[sh][:, lo - sh * shard_w:end - sh * shard_w])
            lo = end
        return out

    def own_layout(g):
        main = jnp.concatenate(shard_cols(g, o_dn, o_ab) + shard_cols(g, o_g, o_dn) + shard_cols(g, o_xa, in_w), axis=1)
        return main, jnp.pad(jnp.concatenate(shard_cols(g, o_ab, o_xa), axis=1), ((0, 0), (0, LANE - 2 * DN_H)))

    w_main, w_ab = own_layout(g_in)

    ln_g2 = ln_g.reshape(1, d)
    lng2 = gmlp_ln_g.reshape(1, GMLP_W)
    lnb2 = gmlp_ln_b.reshape(1, GMLP_W)
    ws3 = gmlp_ws[0]
    bs_t = gmlp_bs[0].T
    alog_row = _lane_row(dn_a_log)
    dt_row = _lane_row(dn_dt_bias)
    dn_g2 = dn_norm_g.reshape(1, HEAD)
    mem_g2 = mem_norm_g.reshape(1, d)
    fin_g2 = final_g.reshape(1, d)

    proj, ab, h_t, (g_out, g_kv, g_conv) = _inproj(
        xs, ln_g2, w_main, w_ab, [w_out[0].astype(BF16), w_mem_kv[0].astype(BF16), conv_w[0]])
    wo = g_out.reshape(MIX_W, d)
    wo_perm = jnp.concatenate([wo[GMLP_W:GMLP_W + DN_W], wo[0:GMLP_W], wo[GMLP_W + DN_W:MIX_W]], axis=0)
    w_kv = g_kv.reshape(d, 2 * XA_W)
    conv_full = g_conv.transpose(1, 0, 2).reshape(DN_K, 3 * DN_W)
    out_a = _gmlp_fwd(proj, lng2, lnb2, ws3, bs_t)
    mkv = _memkv_fwd(mems, mem_g2, w_kv)
    out_c = _xattn_fwd(proj, mkv)
    q, k, v, gb, gbt, yc = _dn_pre(proj, ab, conv_full, alog_row, dt_row)
    wk, qg, kd, tmat, ai, egl, o, vn, st, out_b = _dn_fwd(q, k, v, gb, gbt, proj, dn_g2)

    dx2, dx2b, dmixed, loss_acc, d_fin_g = _final(xs, tgt, out_b, out_a, out_c, wo_perm, fin_g2)

    dwo_b = _matmul_tn(out_b, dx2b, "dw_out_b")
    dwo_a = _matmul_tn(out_a, dx2b, "dw_out_a")
    dwo_c = _matmul_tn(out_c, dx2b, "dw_out_c")
    d_w_out = jnp.concatenate([dwo_a, dwo_b, dwo_c], axis=0)

    dp_g, d_ws, d_bst, d_lng, d_lnb = _gmlp_bwd(proj, dmixed, lng2, lnb2, ws3, bs_t)
    dp_x, dmkv = _xattn_bwd(proj, dmixed, mkv)
    d_w_kv, d_mem_g = _memkv_bwd(mems, mem_g2, w_kv, dmkv)
    dq, dk, dv, dgb, dp_dz, d_dn_g = _dn_bwd(dmixed, o, proj, dn_g2, wk, qg, kd, ai, egl, q, k, v, gb, gbt, tmat, vn, st)
    dp_qkv, dp_ab, d_conv, d_alog, d_dt = _dn_pre_bwd(proj, yc, ab, conv_full, alog_row, dt_row, dq, dk, dv, dgb)

    dw_dz = _matmul_acc(h_t, dp_dz, "dw_in_dz")
    dw_gm = _matmul_acc(h_t, dp_g, "dw_in_gmlp")
    dw_xa = _matmul_acc(h_t, dp_x, "dw_in_xa")
    dw_ab = _matmul_acc(h_t, dp_ab, "dw_in_ab")

    small_shapes = [(1, 1), gmlp_ln_g.shape, gmlp_ln_b.shape, gmlp_ws.shape, gmlp_bs.shape, dn_a_log.shape,
                    dn_dt_bias.shape, dn_norm_g.shape, mem_norm_g.shape, final_g.shape, (DN_K, 3 * DN_W)]
    small_g = _pack([loss_acc[0:1, 0:1], d_lng, d_lnb, d_ws, d_bst.T, d_alog[:, :DN_H], d_dt[:, :DN_H], d_dn_g, d_mem_g,
                     d_fin_g, d_conv])
    zc = jnp.zeros((DN_K, 3 * DN_W), F32)
    z1 = jnp.zeros((1, 1), F32)
    small_w = _pack([z1, gmlp_ln_g, gmlp_ln_b, gmlp_ws, gmlp_bs, dn_a_log, dn_dt_bias, dn_norm_g, mem_norm_g, final_g, zc])
    small_m = _pack([z1, m_gmlp_ln_g, m_gmlp_ln_b, m_gmlp_ws, m_gmlp_bs, m_dn_a_log, m_dn_dt_bias, m_dn_norm_g,
                     m_mem_norm_g, m_final_g, zc])
    small_v = _pack([z1 + 1.0, v_gmlp_ln_g, v_gmlp_ln_b, v_gmlp_ws, v_gmlp_bs, v_dn_a_log, v_dn_dt_bias, v_dn_norm_g,
                     v_mem_norm_g, v_final_g, zc + 1.0])

    send_out = d_w_out.reshape(N_DEV, MIX_W // N_DEV, d).astype(BF16)
    send_kv = d_w_kv.reshape(N_DEV, d // N_DEV, 2 * XA_W).astype(BF16)
    dw_qkv, all_small, (got_out, got_kv) = _matmul_acc(h_t, dp_qkv, "dw_in_qkv", swap=(small_g, [send_out, send_kv]))
    segs = [(o_g, dw_gm), (o_dn, dw_qkv), (o_dn + 3 * DN_W, dw_dz), (o_ab, dw_ab[:, :2 * DN_H]), (o_xa, dw_xa)]
    shards = []
    for sh in range(N_DEV):
        lo, hi = sh * shard_w, (sh + 1) * shard_w
        parts = [arr[:, max(lo, off) - off:min(hi, off + arr.shape[1]) - off] for off, arr in segs
                 if off < hi and off + arr.shape[1] > lo]
        shards.append(jnp.concatenate(parts, axis=1).astype(BF16))
    send_in = jnp.stack(shards)
    sends = [send_in, send_out, send_kv]
    _, (got_in,) = _swap_halves(None, [send_in], "swap_halves")
    got = [got_in, got_out, got_kv]
    core = lax.axis_index("c").astype(jnp.int32).reshape(1)
    chip_sums = _pair_sums(core, sends, got)
    grad_x, (r_in, r_out, r_kv), all_ln_g = _dh_rms(
        [dp_qkv, dp_dz, dp_g, dp_x, dp_ab], [w_main], [w_ab], xs, dx2, ln_g2, chip_sums)

    g_w_in, dl_w_in, nm_w_in, nv_w_in = _adam(r_in, w_in[0], m_w_in[0], v_w_in[0], "adam_w_in")
    g_w_out, dl_w_out, nm_w_out, nv_w_out = _adam(r_out, w_out[0], m_w_out[0], v_w_out[0], "adam_w_out")
    g_w_kv, dl_w_kv, nm_w_kv, nv_w_kv = _adam(r_kv, w_mem_kv[0], m_w_mem_kv[0], v_w_mem_kv[0], "adam_w_kv")
    sm = [_unpack(t, small_shapes) for t in _adam(all_small, small_w, small_m, small_v, "adam_small")]
    ln_res = _adam(all_ln_g, ln_g2, m_ln_g.reshape(1, d), v_ln_g.reshape(1, d), "adam_ln_g")

    conv_parts = lax.dynamic_slice(all_small, (0, all_small.shape[1] - (DN_K * 3 * DN_W) // LANE, 0),
                                   (N_DEV, (DN_K * 3 * DN_W) // LANE, LANE)).reshape(N_DEV, DN_K, 3 * DN_W)
    cshard = conv_w.shape[2]
    conv_parts = lax.dynamic_slice(conv_parts, (0, 0, me * cshard), (N_DEV, DN_K, cshard))
    cpad = ((0, 0), (0, HALO - DN_K), (0, 0))
    conv_res = _adam(jnp.pad(conv_parts, cpad), jnp.pad(conv_w[0], cpad[1:]), jnp.pad(m_conv_w[0], cpad[1:]),
                     jnp.pad(v_conv_w[0], cpad[1:], constant_values=1.0), "adam_conv")
    g_conv_s, dl_conv, nm_conv, nv_conv = [t[:DN_K][None] for t in conv_res]

    loss = sm[0][0].reshape(())

    def group(idx, big_in, big_conv, big_kv, big_out):
        names = sm[idx][1:]
        return [ln_res[idx], big_in[None], names[0], names[1], names[2], names[3], big_conv, names[4], names[5], names[6],
                names[7], big_kv[None], big_out[None], names[8]]

    grads = group(0, g_w_in, g_conv_s, g_w_kv, g_w_out)
    deltas = group(1, dl_w_in, dl_conv, dl_w_kv, dl_w_out)
    new_m = group(2, nm_w_in, nm_conv, nm_w_kv, nm_w_out)
    new_v = group(3, nv_w_in, nv_conv, nv_w_kv, nv_w_out)
    return (loss, grad_x[None], *grads, *deltas, *new_m, *new_v)
```
